```python
import jax, jax.numpy as jnp
from jax import lax
import numpy as np

D_MODEL = 1024
BATCH = 8
SEQ = 2048
DEPTH = 2

N_MIXERS = 2
N_LAYERS_A = (DEPTH + 1) // 2
N_LAYERS_B = DEPTH // 2
D_CONV = D_MODEL
CONV_WIDTH = 31
HEAD_DIM = 64
N_HEADS_B = D_MODEL // HEAD_DIM
D_ATTN = N_HEADS_B * HEAD_DIM
DILATED_GROUPS = ((128, 1), (512, 4), (2048, 16))
N_GROUPS = len(DILATED_GROUPS)
BLOCK = 128
IN_COLS_B = N_GROUPS * 3 * D_ATTN + D_ATTN
NORM_EPS = 1e-6
NEG_INF = -1e30

kernel_name = "hybrid_conv_dilated_attn_adaln"


def rms_norm(x, g):
    xf = x.astype(jnp.float32)
    y = xf * lax.rsqrt(jnp.mean(xf * xf, axis=-1, keepdims=True) + NORM_EPS)
    return (y * g.astype(jnp.float32)).astype(x.dtype)


def layer_norm(x, g, b):
    xf = x.astype(jnp.float32)
    mu = jnp.mean(xf, axis=-1, keepdims=True)
    xc = xf - mu
    y = xc * lax.rsqrt(jnp.mean(xc * xc, axis=-1, keepdims=True) + NORM_EPS)
    return (y * g.astype(jnp.float32) + b.astype(jnp.float32)).astype(x.dtype)


def alibi_slopes(n_heads):
    return jnp.exp2(-8.0 * jnp.arange(1, n_heads + 1, dtype=jnp.float32) / n_heads)


def ada_modulation(c, w, b):
    mod = jax.nn.silu(c) @ w + b
    shift, scale, gate = jnp.split(mod, 3, axis=-1)
    return shift[:, None, :], scale[:, None, :], gate[:, None, :]


def conformer_conv_mixer(h, w_in, conv_w, conv_b, ln_g, ln_b, w_out):
    proj = h @ w_in
    val, glu_gate, z = jnp.split(proj, 3, axis=-1)
    u = val * jax.nn.sigmoid(glu_gate)
    u = lax.conv_general_dilated(
        u, conv_w[:, None, :], window_strides=(1,), padding=[(CONV_WIDTH - 1, 0)],
        dimension_numbers=("NWC", "WIO", "NWC"), feature_group_count=D_CONV) + conv_b
    u = jax.nn.silu(layer_norm(u, ln_g, ln_b))
    return (u * jax.nn.silu(z)) @ w_out


def dilated_window_group(q, k, v, window, dilation, slopes):
    B, S, H, Dh = q.shape
    n_steps = window // dilation
    L = S // dilation
    nb = -(-L // BLOCK)
    Lp = nb * BLOCK
    N = B * dilation

    def to_classes(t):
        t = t.reshape(B, L, dilation, H, Dh).transpose(0, 2, 1, 3, 4).reshape(N, L, H, Dh)
        return jnp.pad(t, ((0, 0), (0, Lp - L), (0, 0), (0, 0)))

    def band(t):
        t = jnp.pad(t, ((0, 0), (BLOCK, 0), (0, 0), (0, 0))).reshape(N, nb + 1, BLOCK, H, Dh)
        return jnp.concatenate([t[:, :-1], t[:, 1:]], axis=2)

    qb = to_classes(q).reshape(N, nb, BLOCK, H, Dh)
    kb = band(to_classes(k))
    vb = band(to_classes(v))

    s = jnp.einsum("nbqhd,nbkhd->nhbqk", qb, kb) * (Dh ** -0.5)
    qi = jnp.arange(BLOCK)[:, None]
    kj = jnp.arange(2 * BLOCK)[None, :]
    steps = qi + BLOCK - kj
    key_idx = jnp.arange(nb)[:, None, None] * BLOCK + kj[None] - BLOCK
    valid = (steps >= 0) & (steps <= n_steps) & (key_idx >= 0)
    dist = (steps * dilation).astype(jnp.float32)
    s = s - slopes[:, None, None, None] * dist
    s = jnp.where(valid, s, NEG_INF)
    lse = jax.nn.logsumexp(s, axis=-1)
    p = jnp.exp(s - lse[..., None])
    o = jnp.einsum("nhbqk,nbkhd->nbqhd", p, vb)

    def from_classes(t):
        t = t.reshape((B, dilation, Lp) + t.shape[3:])[:, :, :L]
        return jnp.moveaxis(t, 1, 2).reshape((B, S) + t.shape[3:])

    return from_classes(o), from_classes(jnp.moveaxis(lse, 1, -1))


def dilated_attention_mixer(h, w_in, q_norm, k_norm, w_out):
    B, S, _ = h.shape
    proj = h @ w_in
    qkv = proj[..., :N_GROUPS * 3 * D_ATTN].reshape(B, S, N_GROUPS, 3, N_HEADS_B, HEAD_DIM)
    z = proj[..., N_GROUPS * 3 * D_ATTN:]
    slopes = alibi_slopes(N_HEADS_B)
    outs, lses = [], []
    for g, (window, dilation) in enumerate(DILATED_GROUPS):
        q = rms_norm(qkv[:, :, g, 0], q_norm[g]).astype(jnp.float32)
        k = rms_norm(qkv[:, :, g, 1], k_norm[g]).astype(jnp.float32)
        v = qkv[:, :, g, 2].astype(jnp.float32)
        o, lse = dilated_window_group(q, k, v, window, dilation, slopes)
        outs.append(o)
        lses.append(lse)
    wts = jax.nn.softmax(jnp.stack(lses), axis=0)
    o = jnp.sum(wts[..., None] * jnp.stack(outs), axis=0)
    o = o.reshape(B, S, D_ATTN).astype(h.dtype)
    return (o * jax.nn.silu(z)) @ w_out


def _fwd_setup_inputs(seed: int = 0) -> dict:
    key = jax.random.key(seed)
    ks = jax.random.split(key, 16)
    f32 = jnp.float32
    nrm = lambda k, shape, s: jax.random.normal(k, shape, f32) * s
    return {
        "x": nrm(ks[0], (BATCH, SEQ, D_MODEL), 1.0),
        "c": nrm(ks[1], (BATCH, D_MODEL), 1.0),
        "norm_g": 1.0 + nrm(ks[2], (DEPTH, D_MODEL), 0.05),
        "ada_w": nrm(ks[3], (DEPTH, D_MODEL, 3 * D_MODEL), D_MODEL ** -0.5),
        "ada_b": nrm(ks[4], (DEPTH, 3 * D_MODEL), 0.02),
        "a_w_in": nrm(ks[5], (N_LAYERS_A, D_MODEL, 3 * D_CONV), D_MODEL ** -0.5),
        "a_conv_w": nrm(ks[6], (N_LAYERS_A, CONV_WIDTH, D_CONV), CONV_WIDTH ** -0.5),
        "a_conv_b": nrm(ks[7], (N_LAYERS_A, D_CONV), 0.02),
        "a_ln_g": 1.0 + nrm(ks[8], (N_LAYERS_A, D_CONV), 0.05),
        "a_ln_b": nrm(ks[9], (N_LAYERS_A, D_CONV), 0.02),
        "a_w_out": nrm(ks[10], (N_LAYERS_A, D_CONV, D_MODEL), D_CONV ** -0.5),
        "b_w_in": nrm(ks[11], (N_LAYERS_B, D_MODEL, IN_COLS_B), D_MODEL ** -0.5),
        "b_q_norm": 1.0 + nrm(ks[12], (N_LAYERS_B, N_GROUPS, HEAD_DIM), 0.05),
        "b_k_norm": 1.0 + nrm(ks[13], (N_LAYERS_B, N_GROUPS, HEAD_DIM), 0.05),
        "b_w_out": nrm(ks[14], (N_LAYERS_B, D_ATTN, D_MODEL), D_ATTN ** -0.5),
    }


def _fwd_reference(x, c, norm_g, ada_w, ada_b, a_w_in, a_conv_w, a_conv_b, a_ln_g, a_ln_b, a_w_out,
              b_w_in, b_q_norm, b_k_norm, b_w_out):
    for layer in range(DEPTH):
        shift, scale, gate = ada_modulation(c, ada_w[layer], ada_b[layer])
        h = rms_norm(x, norm_g[layer]) * (1.0 + scale) + shift
        j = layer // N_MIXERS
        if layer % N_MIXERS == 0:
            y = conformer_conv_mixer(h, a_w_in[j], a_conv_w[j], a_conv_b[j], a_ln_g[j], a_ln_b[j], a_w_out[j])
        else:
            y = dilated_attention_mixer(h, b_w_in[j], b_q_norm[j], b_k_norm[j], b_w_out[j])
        x = x + gate * y
    return x


import jax as _jax
import jax.numpy as _jnp

TWIN_FORMAT = 'train_step'
FWD_PARAMS = ['x', 'c', 'norm_g', 'ada_w', 'ada_b', 'a_w_in', 'a_conv_w', 'a_conv_b', 'a_ln_g', 'a_ln_b', 'a_w_out', 'b_w_in', 'b_q_norm', 'b_k_norm', 'b_w_out']
TWIN_WEIGHTS = ['norm_g', 'ada_w', 'ada_b', 'a_w_in', 'a_conv_w', 'a_conv_b', 'a_ln_g', 'a_ln_b', 'a_w_out', 'b_w_in', 'b_q_norm', 'b_k_norm', 'b_w_out']
TWIN_DIFF_INPUT = 'x'
TWIN_INPUTS = ['x', 'c', 'norm_g', 'ada_w', 'ada_b', 'a_w_in', 'a_conv_w', 'a_conv_b', 'a_ln_g', 'a_ln_b', 'a_w_out', 'b_w_in', 'b_q_norm', 'b_k_norm', 'b_w_out', 'loss_target', 'm_norm_g', 'm_ada_w', 'm_ada_b', 'm_a_w_in', 'm_a_conv_w', 'm_a_conv_b', 'm_a_ln_g', 'm_a_ln_b', 'm_a_w_out', 'm_b_w_in', 'm_b_q_norm', 'm_b_k_norm', 'm_b_w_out', 'v_norm_g', 'v_ada_w', 'v_ada_b', 'v_a_w_in', 'v_a_conv_w', 'v_a_conv_b', 'v_a_ln_g', 'v_a_ln_b', 'v_a_w_out', 'v_b_w_in', 'v_b_q_norm', 'v_b_k_norm', 'v_b_w_out']
TWIN_OUTPUTS = ['loss', 'grad_x', 'grad_norm_g', 'grad_ada_w', 'grad_ada_b', 'grad_a_w_in', 'grad_a_conv_w', 'grad_a_conv_b', 'grad_a_ln_g', 'grad_a_ln_b', 'grad_a_w_out', 'grad_b_w_in', 'grad_b_q_norm', 'grad_b_k_norm', 'grad_b_w_out', 'delta_norm_g', 'delta_ada_w', 'delta_ada_b', 'delta_a_w_in', 'delta_a_conv_w', 'delta_a_conv_b', 'delta_a_ln_g', 'delta_a_ln_b', 'delta_a_w_out', 'delta_b_w_in', 'delta_b_q_norm', 'delta_b_k_norm', 'delta_b_w_out', 'new_m_norm_g', 'new_m_ada_w', 'new_m_ada_b', 'new_m_a_w_in', 'new_m_a_conv_w', 'new_m_a_conv_b', 'new_m_a_ln_g', 'new_m_a_ln_b', 'new_m_a_w_out', 'new_m_b_w_in', 'new_m_b_q_norm', 'new_m_b_k_norm', 'new_m_b_w_out', 'new_v_norm_g', 'new_v_ada_w', 'new_v_ada_b', 'new_v_a_w_in', 'new_v_a_conv_w', 'new_v_a_conv_b', 'new_v_a_ln_g', 'new_v_a_ln_b', 'new_v_a_w_out', 'new_v_b_w_in', 'new_v_b_q_norm', 'new_v_b_k_norm', 'new_v_b_w_out']
TWIN_LEAF_KINDS = {'loss': 'loss', 'grad_x': 'grad_x', 'grad_norm_g': 'grad_w', 'grad_ada_w': 'grad_w', 'grad_ada_b': 'grad_w', 'grad_a_w_in': 'grad_w', 'grad_a_conv_w': 'grad_w', 'grad_a_conv_b': 'grad_w', 'grad_a_ln_g': 'grad_w', 'grad_a_ln_b': 'grad_w', 'grad_a_w_out': 'grad_w', 'grad_b_w_in': 'grad_w', 'grad_b_q_norm': 'grad_w', 'grad_b_k_norm': 'grad_w', 'grad_b_w_out': 'grad_w', 'delta_norm_g': 'delta_w', 'delta_ada_w': 'delta_w', 'delta_ada_b': 'delta_w', 'delta_a_w_in': 'delta_w', 'delta_a_conv_w': 'delta_w', 'delta_a_conv_b': 'delta_w', 'delta_a_ln_g': 'delta_w', 'delta_a_ln_b': 'delta_w', 'delta_a_w_out': 'delta_w', 'delta_b_w_in': 'delta_w', 'delta_b_q_norm': 'delta_w', 'delta_b_k_norm': 'delta_w', 'delta_b_w_out': 'delta_w', 'new_m_norm_g': 'new_m', 'new_m_ada_w': 'new_m', 'new_m_ada_b': 'new_m', 'new_m_a_w_in': 'new_m', 'new_m_a_conv_w': 'new_m', 'new_m_a_conv_b': 'new_m', 'new_m_a_ln_g': 'new_m', 'new_m_a_ln_b': 'new_m', 'new_m_a_w_out': 'new_m', 'new_m_b_w_in': 'new_m', 'new_m_b_q_norm': 'new_m', 'new_m_b_k_norm': 'new_m', 'new_m_b_w_out': 'new_m', 'new_v_norm_g': 'new_v', 'new_v_ada_w': 'new_v', 'new_v_ada_b': 'new_v', 'new_v_a_w_in': 'new_v', 'new_v_a_conv_w': 'new_v', 'new_v_a_conv_b': 'new_v', 'new_v_a_ln_g': 'new_v', 'new_v_a_ln_b': 'new_v', 'new_v_a_w_out': 'new_v', 'new_v_b_w_in': 'new_v', 'new_v_b_q_norm': 'new_v', 'new_v_b_k_norm': 'new_v', 'new_v_b_w_out': 'new_v'}


def _forward(args):
    return _fwd_reference(*[args[k] for k in FWD_PARAMS])


def _output_shape():
    out = _jax.eval_shape(lambda: _forward(_fwd_setup_inputs(0)))
    return out.shape, out.dtype

N_MICROBATCH = 1
ADAM_LR = 0.001
ADAM_B1 = 0.9
ADAM_B2 = 0.999
ADAM_EPS = 1e-08
ADAM_WD = 0.01
ADAM_STEP = 10
PER_EXAMPLE_BATCH_AXIS = {'x': 0, 'c': 0, 'loss_target': 0}
SHARED_INPUTS = []
_WEIGHT_DTYPES = {'norm_g': _jnp.float32, 'ada_w': _jnp.float32, 'ada_b': _jnp.float32, 'a_w_in': _jnp.float32, 'a_conv_w': _jnp.float32, 'a_conv_b': _jnp.float32, 'a_ln_g': _jnp.float32, 'a_ln_b': _jnp.float32, 'a_w_out': _jnp.float32, 'b_w_in': _jnp.float32, 'b_q_norm': _jnp.float32, 'b_k_norm': _jnp.float32, 'b_w_out': _jnp.float32}
MOMENT_SCALE = {'norm_g': 1.921412e+00, 'ada_w': 7.777487e-01, 'ada_b': 1.592936e+00, 'a_w_in': 4.376411e-01, 'a_conv_w': 4.549919e-01, 'a_conv_b': 1.074793e+00, 'a_ln_g': 2.498216e+00, 'a_ln_b': 1.484565e+00, 'a_w_out': 2.160827e-01, 'b_w_in': 2.776567e-01, 'b_q_norm': 2.021599e+00, 'b_k_norm': 2.002369e+00, 'b_w_out': 2.559161e-01}


def _to_microbatches(a, axis):
    t = _jnp.moveaxis(a, axis, 0)
    t = t.reshape((N_MICROBATCH, t.shape[0] // N_MICROBATCH) + t.shape[1:])
    return _jnp.moveaxis(t, 1, axis + 1)


def setup_inputs(seed: int = 0) -> dict:
    inp = _fwd_setup_inputs(seed)
    key = _jax.random.fold_in(_jax.random.key(seed), 7919)
    shape, _ = _output_shape()
    out = dict(inp)
    out["loss_target"] = _jax.random.normal(_jax.random.fold_in(key, 0), shape, _jnp.float32)
    for i, name in enumerate(TWIN_WEIGHTS):
        w = inp[name].astype(_jnp.float32)
        if MOMENT_SCALE is None:
            s = _jnp.sqrt(_jnp.mean(_jnp.square(w)) + 1e-30)
        else:
            s = MOMENT_SCALE[name]
        km, kv = _jax.random.split(_jax.random.fold_in(key, i + 1))
        out[name] = w
        out["m_" + name] = s * _jax.random.normal(km, w.shape, _jnp.float32)
        out["v_" + name] = (s * s) * _jax.random.uniform(kv, w.shape, _jnp.float32, 0.5, 1.5)
    if N_MICROBATCH > 1:
        for name, axis in PER_EXAMPLE_BATCH_AXIS.items():
            out[name] = _to_microbatches(out[name], axis)
    return {'x': out['x'], 'c': out['c'], 'norm_g': out['norm_g'], 'ada_w': out['ada_w'], 'ada_b': out['ada_b'], 'a_w_in': out['a_w_in'], 'a_conv_w': out['a_conv_w'], 'a_conv_b': out['a_conv_b'], 'a_ln_g': out['a_ln_g'], 'a_ln_b': out['a_ln_b'], 'a_w_out': out['a_w_out'], 'b_w_in': out['b_w_in'], 'b_q_norm': out['b_q_norm'], 'b_k_norm': out['b_k_norm'], 'b_w_out': out['b_w_out'], 'loss_target': out['loss_target'], 'm_norm_g': out['m_norm_g'], 'm_ada_w': out['m_ada_w'], 'm_ada_b': out['m_ada_b'], 'm_a_w_in': out['m_a_w_in'], 'm_a_conv_w': out['m_a_conv_w'], 'm_a_conv_b': out['m_a_conv_b'], 'm_a_ln_g': out['m_a_ln_g'], 'm_a_ln_b': out['m_a_ln_b'], 'm_a_w_out': out['m_a_w_out'], 'm_b_w_in': out['m_b_w_in'], 'm_b_q_norm': out['m_b_q_norm'], 'm_b_k_norm': out['m_b_k_norm'], 'm_b_w_out': out['m_b_w_out'], 'v_norm_g': out['v_norm_g'], 'v_ada_w': out['v_ada_w'], 'v_ada_b': out['v_ada_b'], 'v_a_w_in': out['v_a_w_in'], 'v_a_conv_w': out['v_a_conv_w'], 'v_a_conv_b': out['v_a_conv_b'], 'v_a_ln_g': out['v_a_ln_g'], 'v_a_ln_b': out['v_a_ln_b'], 'v_a_w_out': out['v_a_w_out'], 'v_b_w_in': out['v_b_w_in'], 'v_b_q_norm': out['v_b_q_norm'], 'v_b_k_norm': out['v_b_k_norm'], 'v_b_w_out': out['v_b_w_out']}


def _loss(weights, diff, rest, loss_target):
    with _jax.named_scope("forward"):
        args = {**rest, TWIN_DIFF_INPUT: diff, **{k: w.astype(_WEIGHT_DTYPES[k]) for k, w in weights.items()}}
        y = _forward(args)
    with _jax.named_scope("loss_head"):
        err = _jnp.square(y.astype(_jnp.float32) - loss_target)
        return 0.5 * _jnp.sum(_jnp.mean(err, axis=-1)) if err.ndim else 0.5 * err


def _adamw(w, g, m, v):
    m = ADAM_B1 * m + (1.0 - ADAM_B1) * g
    v = ADAM_B2 * v + (1.0 - ADAM_B2) * _jnp.square(g)
    m_hat = m / (1.0 - ADAM_B1 ** ADAM_STEP)
    v_hat = v / (1.0 - ADAM_B2 ** ADAM_STEP)
    delta = -ADAM_LR * (m_hat / (_jnp.sqrt(v_hat) + ADAM_EPS) + ADAM_WD * w)
    return delta, m, v


def reference(x, c, norm_g, ada_w, ada_b, a_w_in, a_conv_w, a_conv_b, a_ln_g, a_ln_b, a_w_out, b_w_in, b_q_norm, b_k_norm, b_w_out, loss_target, m_norm_g, m_ada_w, m_ada_b, m_a_w_in, m_a_conv_w, m_a_conv_b, m_a_ln_g, m_a_ln_b, m_a_w_out, m_b_w_in, m_b_q_norm, m_b_k_norm, m_b_w_out, v_norm_g, v_ada_w, v_ada_b, v_a_w_in, v_a_conv_w, v_a_conv_b, v_a_ln_g, v_a_ln_b, v_a_w_out, v_b_w_in, v_b_q_norm, v_b_k_norm, v_b_w_out):
    given = dict(x=x, c=c, norm_g=norm_g, ada_w=ada_w, ada_b=ada_b, a_w_in=a_w_in, a_conv_w=a_conv_w, a_conv_b=a_conv_b, a_ln_g=a_ln_g, a_ln_b=a_ln_b, a_w_out=a_w_out, b_w_in=b_w_in, b_q_norm=b_q_norm, b_k_norm=b_k_norm, b_w_out=b_w_out, loss_target=loss_target, m_norm_g=m_norm_g, m_ada_w=m_ada_w, m_ada_b=m_ada_b, m_a_w_in=m_a_w_in, m_a_conv_w=m_a_conv_w, m_a_conv_b=m_a_conv_b, m_a_ln_g=m_a_ln_g, m_a_ln_b=m_a_ln_b, m_a_w_out=m_a_w_out, m_b_w_in=m_b_w_in, m_b_q_norm=m_b_q_norm, m_b_k_norm=m_b_k_norm, m_b_w_out=m_b_w_out, v_norm_g=v_norm_g, v_ada_w=v_ada_w, v_ada_b=v_ada_b, v_a_w_in=v_a_w_in, v_a_conv_w=v_a_conv_w, v_a_conv_b=v_a_conv_b, v_a_ln_g=v_a_ln_g, v_a_ln_b=v_a_ln_b, v_a_w_out=v_a_w_out, v_b_w_in=v_b_w_in, v_b_q_norm=v_b_q_norm, v_b_k_norm=v_b_k_norm, v_b_w_out=v_b_w_out)
    weights = {n: given[n] for n in TWIN_WEIGHTS}
    shared = {n: given[n] for n in SHARED_INPUTS}
    per_example = {n: given[n] for n in ['x', 'c']}
    grad_fn = _jax.value_and_grad(_loss, argnums=(0, 1))

    def one_microbatch(ex, loss_target):
        ex = dict(ex)
        diff = ex.pop(TWIN_DIFF_INPUT)
        return grad_fn(weights, diff, {**shared, **ex}, loss_target)

    if N_MICROBATCH == 1:
        loss, (grad_w, grad_x) = one_microbatch(per_example, given["loss_target"])
    else:
        def body(carry, xs):
            loss_sum, grad_sum = carry
            l_k, (gw_k, gx_k) = one_microbatch(xs[0], xs[1])
            with _jax.named_scope("update"):
                return (loss_sum + l_k, _jax.tree.map(_jnp.add, grad_sum, gw_k)), gx_k

        init = (_jnp.zeros((), _jnp.float32), _jax.tree.map(_jnp.zeros_like, weights))
        (loss, grad_w), grad_x = _jax.lax.scan(body, init, (per_example, given["loss_target"]))
    with _jax.named_scope("update"):
        delta_w, new_m, new_v = {}, {}, {}
        for n in TWIN_WEIGHTS:
            delta_w[n], new_m[n], new_v[n] = _adamw(weights[n], grad_w[n], given["m_" + n], given["v_" + n])
    return (loss, grad_x, *[grad_w[n] for n in TWIN_WEIGHTS], *[delta_w[n] for n in TWIN_WEIGHTS],
            *[new_m[n] for n in TWIN_WEIGHTS], *[new_v[n] for n in TWIN_WEIGHTS])
```

```python
import functools

import jax
import jax.numpy as jnp
from jax import lax
from jax.experimental import pallas as pl
from jax.experimental.pallas import tpu as pltpu

F32 = jnp.float32
BF16 = jnp.bfloat16

SEQ = 2048
D_MODEL = 1024
CONV_WIDTH = 31
HEAD_DIM = 64
N_HEADS = 16
DILATIONS = (1, 4, 16)
ATTN_BLOCK = 128
NORM_EPS = 1e-6
NEG_INF = -1e30
N_DEV = 8
N_CHIPS = 4

ADAM_LR = 0.001
ADAM_B1 = 0.9
ADAM_B2 = 0.999
ADAM_EPS = 1e-08
ADAM_WD = 0.01
ADAM_STEP = 10

VMEM_LIMIT_BYTES = 52 * 1024 * 1024
HALO = 32
MESH = pl.DeviceIdType.MESH


def _params(*sem):
    return pltpu.CompilerParams(dimension_semantics=sem or None, vmem_limit_bytes=VMEM_LIMIT_BYTES)


def _sigmoid(v):
    return 1.0 / (1.0 + jnp.exp(-v))


def _row_spec(tm, cols, col_block=0):
    return pl.BlockSpec((tm, cols), lambda i: (i, col_block))


def _vec_spec(rows, cols):
    return pl.BlockSpec((rows, cols), lambda i: (0, 0))


def _normmod(xv, g, scale, shift):
    r = lax.rsqrt(jnp.mean(xv * xv, axis=-1, keepdims=True) + NORM_EPS)
    return xv * r * g * (1.0 + scale) + shift


def _normmod_fwd(x, g, scale, shift, name):
    tm = 256

    def body(x_ref, g_ref, sc_ref, sh_ref, h_ref, ht_ref):
        h = _normmod(x_ref[...], g_ref[...], sc_ref[...], sh_ref[...])
        h_ref[...] = h.astype(BF16)
        ht_ref[...] = h.T.astype(BF16)

    return pl.pallas_call(
        body, name=name, grid=(SEQ // tm,),
        in_specs=[_row_spec(tm, D_MODEL)] + [_vec_spec(1, D_MODEL)] * 3,
        out_specs=[_row_spec(tm, D_MODEL), pl.BlockSpec((D_MODEL, tm), lambda i: (0, i))],
        out_shape=[jax.ShapeDtypeStruct((SEQ, D_MODEL), BF16), jax.ShapeDtypeStruct((D_MODEL, SEQ), BF16)],
        compiler_params=_params("parallel"),
    )(x, g, scale, shift)


def _normmod_bwd(x, g, scale, dh_parts, dres, name):
    tm = 256
    n_parts = len(dh_parts)

    def body(x_ref, g_ref, sc_ref, dres_ref, *rest):
        part_refs = rest[:n_parts]
        dx_ref, sums_ref = rest[n_parts:]
        xv = x_ref[...]
        r = lax.rsqrt(jnp.mean(xv * xv, axis=-1, keepdims=True) + NORM_EPS)
        xn = xv * r
        dh = part_refs[0][...]
        for p in part_refs[1:]:
            dh = dh + p[...]
        gv = g_ref[...]
        one_sc = 1.0 + sc_ref[...]
        dxn = dh * (gv * one_sc)
        dx = r * (dxn - xn * jnp.mean(dxn * xn, axis=-1, keepdims=True))
        dx_ref[...] = dres_ref[...] + dx
        dhx = dh * xn
        sums = jnp.concatenate([
            jnp.sum(dhx, axis=0, keepdims=True) * one_sc,
            jnp.sum(dhx, axis=0, keepdims=True) * gv,
            jnp.sum(dh, axis=0, keepdims=True),
            jnp.zeros((5, D_MODEL), F32)], axis=0)

        @pl.when(pl.program_id(0) == 0)
        def _():
            sums_ref[...] = jnp.zeros_like(sums_ref)

        sums_ref[...] += sums

    return pl.pallas_call(
        body, name=name, grid=(SEQ // tm,),
        in_specs=[_row_spec(tm, D_MODEL), _vec_spec(1, D_MODEL), _vec_spec(1, D_MODEL), _row_spec(tm, D_MODEL)]
        + [_row_spec(tm, D_MODEL)] * n_parts,
        out_specs=[_row_spec(tm, D_MODEL), _vec_spec(8, D_MODEL)],
        out_shape=[jax.ShapeDtypeStruct((SEQ, D_MODEL), F32), jax.ShapeDtypeStruct((8, D_MODEL), F32)],
        compiler_params=_params("arbitrary"),
    )(x, g, scale, dres, *dh_parts)


def _mm(lhs, rhs, *, tn, tile0, n_tiles, out_dtype, name, out3d=None, prev=None):
    mo, kc = lhs.shape
    cm = 512

    def body(l_ref, r_ref, *rest):
        o_ref = rest[-1]
        for m in range(mo // cm):
            rows = pl.ds(m * cm, cm)
            o_ref[rows, :] = jnp.dot(l_ref[rows, :], r_ref[...], preferred_element_type=F32).astype(out_dtype)

    if rhs.ndim == 3:
        tps_r = rhs.shape[2] // tn
        r_spec = pl.BlockSpec((None, kc, tn), lambda t: ((tile0 + t) // tps_r, 0, (tile0 + t) % tps_r))
    else:
        r_spec = pl.BlockSpec((kc, tn), lambda t: (0, t))
    in_specs = [pl.BlockSpec((mo, kc), lambda t: (0, 0)), r_spec]
    args = [lhs, rhs]
    aliases = {}
    if out3d is None:
        o_spec = pl.BlockSpec((mo, tn), lambda t: (0, t))
        o_shape = jax.ShapeDtypeStruct((mo, n_tiles * tn), out_dtype)
    else:
        j_out, ns_out = out3d
        tps_o = ns_out // tn
        o_spec = pl.BlockSpec((None, mo, tn), lambda t: ((tile0 + t) // tps_o, 0, (tile0 + t) % tps_o))
        o_shape = jax.ShapeDtypeStruct((j_out, mo, ns_out), out_dtype)
        if prev is not None:
            in_specs.append(pl.BlockSpec(memory_space=pl.ANY))
            args.append(prev)
            aliases = {2: 0}
    return pl.pallas_call(
        body, name=name, grid=(n_tiles,), in_specs=in_specs, out_specs=o_spec, out_shape=o_shape,
        input_output_aliases=aliases, compiler_params=_params("parallel"),
    )(*args)


def _mm_nt(dy, w3, *, tn, tile0, n_tiles, name):
    m_rows = dy.shape[0]
    _, kc, ns = w3.shape
    tps = ns // tn
    cm = 512

    def body(dy_ref, w_ref, o_ref):
        @pl.when(pl.program_id(0) == 0)
        def _():
            o_ref[...] = jnp.zeros_like(o_ref)

        for m in range(m_rows // cm):
            rows = pl.ds(m * cm, cm)
            o_ref[rows, :] += lax.dot_general(dy_ref[rows, :], w_ref[...], (((1,), (1,)), ((), ())),
                                              preferred_element_type=F32)

    return pl.pallas_call(
        body, name=name, grid=(n_tiles,),
        in_specs=[pl.BlockSpec((m_rows, tn), lambda t: (0, t)),
                  pl.BlockSpec((None, kc, tn), lambda t: ((tile0 + t) // tps, 0, (tile0 + t) % tps))],
        out_specs=pl.BlockSpec((m_rows, kc), lambda t: (0, 0)),
        out_shape=jax.ShapeDtypeStruct((m_rows, kc), F32),
        compiler_params=_params("arbitrary"),
    )(dy, w3)


def _conv_fwd(proj, conv_w, conv_b, ln_g, ln_b, name):
    tm = 256
    hb = tm // HALO

    def body(vg_ref, halo_ref, z_ref, w_ref, b_ref, g_ref, be_ref, u5_ref, u5t_ref, u2_ref, buf):
        i = pl.program_id(0)
        u1 = vg_ref[:, :D_MODEL] * _sigmoid(vg_ref[:, D_MODEL:])
        u1h = halo_ref[:, :D_MODEL] * _sigmoid(halo_ref[:, D_MODEL:])
        buf[pl.ds(0, HALO), :] = jnp.where(i > 0, u1h, 0.0)
        buf[pl.ds(HALO, tm), :] = u1
        acc = jnp.zeros((tm, D_MODEL), F32) + b_ref[...]
        for k in range(CONV_WIDTH):
            acc = acc + w_ref[k:k + 1, :] * buf[pl.ds(HALO - (CONV_WIDTH - 1) + k, tm), :]
        u2_ref[...] = acc
        mu = jnp.mean(acc, axis=-1, keepdims=True)
        xc = acc - mu
        rstd = lax.rsqrt(jnp.mean(xc * xc, axis=-1, keepdims=True) + NORM_EPS)
        u3 = xc * rstd * g_ref[...] + be_ref[...]
        zv = z_ref[...]
        u5 = u3 * _sigmoid(u3) * (zv * _sigmoid(zv))
        u5_ref[...] = u5.astype(BF16)
        u5t_ref[...] = u5.T.astype(BF16)

    return pl.pallas_call(
        body, name=name, grid=(SEQ // tm,),
        in_specs=[pl.BlockSpec((tm, 2 * D_MODEL), lambda i: (i, 0)),
                  pl.BlockSpec((HALO, 2 * D_MODEL), lambda i: (jnp.maximum(i * hb - 1, 0), 0)),
                  _row_spec(tm, D_MODEL, 2),
                  _vec_spec(CONV_WIDTH, D_MODEL)] + [_vec_spec(1, D_MODEL)] * 3,
        out_specs=[_row_spec(tm, D_MODEL), pl.BlockSpec((D_MODEL, tm), lambda i: (0, i)), _row_spec(tm, D_MODEL)],
        out_shape=[jax.ShapeDtypeStruct((SEQ, D_MODEL), BF16), jax.ShapeDtypeStruct((D_MODEL, SEQ), BF16),
                   jax.ShapeDtypeStruct((SEQ, D_MODEL), F32)],
        scratch_shapes=[pltpu.VMEM((HALO + tm, D_MODEL), F32)],
        compiler_params=_params("parallel"),
    )(proj, proj, proj, conv_w, conv_b, ln_g, ln_b)


def _conv_bwd_pointwise(du5, proj, u2, ln_g, ln_b, name):
    tm = 256

    def body(du5_ref, z_ref, u2_ref, g_ref, be_ref, du2_ref, dz_ref, sums_ref):
        u2v = u2_ref[...]
        mu = jnp.mean(u2v, axis=-1, keepdims=True)
        xc = u2v - mu
        rstd = lax.rsqrt(jnp.mean(xc * xc, axis=-1, keepdims=True) + NORM_EPS)
        xhat = xc * rstd
        u3 = xhat * g_ref[...] + be_ref[...]
        s3 = _sigmoid(u3)
        u4 = u3 * s3
        zv = z_ref[...]
        sz = _sigmoid(zv)
        du5v = du5_ref[...]
        dz_ref[...] = du5v * u4 * (sz * (1.0 + zv * (1.0 - sz)))
        du3 = du5v * (zv * sz) * (s3 * (1.0 + u3 * (1.0 - s3)))
        dxhat = du3 * g_ref[...]
        du2 = rstd * (dxhat - jnp.mean(dxhat, axis=-1, keepdims=True)
                      - xhat * jnp.mean(dxhat * xhat, axis=-1, keepdims=True))
        du2_ref[...] = du2
        sums = jnp.concatenate([
            jnp.sum(du3 * xhat, axis=0, keepdims=True),
            jnp.sum(du3, axis=0, keepdims=True),
            jnp.sum(du2, axis=0, keepdims=True),
            jnp.zeros((5, D_MODEL), F32)], axis=0)

        @pl.when(pl.program_id(0) == 0)
        def _():
            sums_ref[...] = jnp.zeros_like(sums_ref)

        sums_ref[...] += sums

    return pl.pallas_call(
        body, name=name, grid=(SEQ // tm,),
        in_specs=[_row_spec(tm, D_MODEL), _row_spec(tm, D_MODEL, 2), _row_spec(tm, D_MODEL),
                  _vec_spec(1, D_MODEL), _vec_spec(1, D_MODEL)],
        out_specs=[_row_spec(tm, D_MODEL), _row_spec(tm, D_MODEL), _vec_spec(8, D_MODEL)],
        out_shape=[jax.ShapeDtypeStruct((SEQ, D_MODEL), F32), jax.ShapeDtypeStruct((SEQ, D_MODEL), F32),
                   jax.ShapeDtypeStruct((8, D_MODEL), F32)],
        compiler_params=_params("arbitrary"),
    )(du5, proj, u2, ln_g, ln_b)


def _conv_bwd_taps(du2, dz, proj, conv_w, name):
    tm = 256
    hb = tm // HALO
    n_blocks = SEQ // tm

    def body(du2_ref, dnext_ref, dz_ref, vg_ref, halo_ref, w_ref, dproj_ref, dw_ref, ubuf, dbuf):
        i = pl.program_id(0)
        val = vg_ref[:, :D_MODEL]
        sg = _sigmoid(vg_ref[:, D_MODEL:])
        u1h = halo_ref[:, :D_MODEL] * _sigmoid(halo_ref[:, D_MODEL:])
        ubuf[pl.ds(0, HALO), :] = jnp.where(i > 0, u1h, 0.0)
        ubuf[pl.ds(HALO, tm), :] = val * sg
        du2v = du2_ref[...]
        dbuf[pl.ds(0, tm), :] = du2v
        dbuf[pl.ds(tm, HALO), :] = jnp.where(i < n_blocks - 1, dnext_ref[...], 0.0)

        @pl.when(i == 0)
        def _():
            dw_ref[...] = jnp.zeros_like(dw_ref)

        du1 = jnp.zeros((tm, D_MODEL), F32)
        for k in range(CONV_WIDTH):
            du1 = du1 + w_ref[k:k + 1, :] * dbuf[pl.ds(CONV_WIDTH - 1 - k, tm), :]
            dw_ref[k:k + 1, :] += jnp.sum(du2v * ubuf[pl.ds(HALO - (CONV_WIDTH - 1) + k, tm), :],
                                          axis=0, keepdims=True)
        dproj_ref[:, :D_MODEL] = (du1 * sg).astype(BF16)
        dproj_ref[:, D_MODEL:2 * D_MODEL] = (du1 * val * sg * (1.0 - sg)).astype(BF16)
        dproj_ref[:, 2 * D_MODEL:] = dz_ref[...].astype(BF16)

    return pl.pallas_call(
        body, name=name, grid=(n_blocks,),
        in_specs=[_row_spec(tm, D_MODEL),
                  pl.BlockSpec((HALO, D_MODEL), lambda i: (jnp.minimum((i + 1) * hb, SEQ // HALO - 1), 0)),
                  _row_spec(tm, D_MODEL),
                  pl.BlockSpec((tm, 2 * D_MODEL), lambda i: (i, 0)),
                  pl.BlockSpec((HALO, 2 * D_MODEL), lambda i: (jnp.maximum(i * hb - 1, 0), 0)),
                  _vec_spec(CONV_WIDTH, D_MODEL)],
        out_specs=[_row_spec(tm, 3 * D_MODEL), _vec_spec(32, D_MODEL)],
        out_shape=[jax.ShapeDtypeStruct((SEQ, 3 * D_MODEL), BF16), jax.ShapeDtypeStruct((32, D_MODEL), F32)],
        scratch_shapes=[pltpu.VMEM((HALO + tm, D_MODEL), F32), pltpu.VMEM((tm + HALO, D_MODEL), F32)],
        compiler_params=_params("arbitrary"),
    )(du2, du2, dz, proj, proj, conv_w)


def _out_a(u5, w_out, x, gate, g1, scale1, shift1, name):
    tm = 256

    def body(u_ref, w_ref, x_ref, gate_ref, g_ref, sc_ref, sh_ref, x1_ref, y_ref, h_ref, ht_ref):
        y = jnp.dot(u_ref[...], w_ref[...], preferred_element_type=F32)
        x1 = x_ref[...] + gate_ref[...] * y
        y_ref[...] = y
        x1_ref[...] = x1
        h = _normmod(x1, g_ref[...], sc_ref[...], sh_ref[...])
        h_ref[...] = h.astype(BF16)
        ht_ref[...] = h.T.astype(BF16)

    return pl.pallas_call(
        body, name=name, grid=(SEQ // tm,),
        in_specs=[_row_spec(tm, D_MODEL), _vec_spec(D_MODEL, D_MODEL), _row_spec(tm, D_MODEL)]
        + [_vec_spec(1, D_MODEL)] * 4,
        out_specs=[_row_spec(tm, D_MODEL), _row_spec(tm, D_MODEL), _row_spec(tm, D_MODEL),
                   pl.BlockSpec((D_MODEL, tm), lambda i: (0, i))],
        out_shape=[jax.ShapeDtypeStruct((SEQ, D_MODEL), F32), jax.ShapeDtypeStruct((SEQ, D_MODEL), F32),
                   jax.ShapeDtypeStruct((SEQ, D_MODEL), BF16), jax.ShapeDtypeStruct((D_MODEL, SEQ), BF16)],
        compiler_params=_params("parallel"),
    )(u5, w_out, x, gate, g1, scale1, shift1)


def _out_b_loss(u, w_out, x1, gate, target, name):
    tm = 256

    def body(u_ref, w_ref, x_ref, gate_ref, t_ref, e_ref, dy_ref, sums_ref):
        y = jnp.dot(u_ref[...], w_ref[...], preferred_element_type=F32)
        diff = x_ref[...] + gate_ref[...] * y - t_ref[...]
        e = diff * (1.0 / D_MODEL)
        e_ref[...] = e
        dy_ref[...] = (e * gate_ref[...]).astype(BF16)
        sums = jnp.concatenate([
            jnp.sum(e * y, axis=0, keepdims=True),
            jnp.sum(diff * diff, axis=0, keepdims=True),
            jnp.zeros((6, D_MODEL), F32)], axis=0)

        @pl.when(pl.program_id(0) == 0)
        def _():
            sums_ref[...] = jnp.zeros_like(sums_ref)

        sums_ref[...] += sums

    return pl.pallas_call(
        body, name=name, grid=(SEQ // tm,),
        in_specs=[_row_spec(tm, D_MODEL), _vec_spec(D_MODEL, D_MODEL), _row_spec(tm, D_MODEL),
                  _vec_spec(1, D_MODEL), _row_spec(tm, D_MODEL)],
        out_specs=[_row_spec(tm, D_MODEL), _row_spec(tm, D_MODEL), _vec_spec(8, D_MODEL)],
        out_shape=[jax.ShapeDtypeStruct((SEQ, D_MODEL), F32), jax.ShapeDtypeStruct((SEQ, D_MODEL), BF16),
                   jax.ShapeDtypeStruct((8, D_MODEL), F32)],
        compiler_params=_params("arbitrary"),
    )(u, w_out, x1, gate, target)


def _dgate_dy(dx1, y, gate, name):
    tm = 256

    def body(d_ref, y_ref, gate_ref, dy_ref, sums_ref):
        dv = d_ref[...]
        dy_ref[...] = (dv * gate_ref[...]).astype(BF16)
        sums = jnp.concatenate([jnp.sum(dv * y_ref[...], axis=0, keepdims=True), jnp.zeros((7, D_MODEL), F32)], axis=0)

        @pl.when(pl.program_id(0) == 0)
        def _():
            sums_ref[...] = jnp.zeros_like(sums_ref)

        sums_ref[...] += sums

    return pl.pallas_call(
        body, name=name, grid=(SEQ // tm,),
        in_specs=[_row_spec(tm, D_MODEL), _row_spec(tm, D_MODEL), _vec_spec(1, D_MODEL)],
        out_specs=[_row_spec(tm, D_MODEL), _vec_spec(8, D_MODEL)],
        out_shape=[jax.ShapeDtypeStruct((SEQ, D_MODEL), BF16), jax.ShapeDtypeStruct((8, D_MODEL), F32)],
        compiler_params=_params("arbitrary"),
    )(dx1, y, gate)


def _mm_nt_res(dy, w, name):
    tm = 256
    kc, n = w.shape

    def body(dy_ref, w_ref, o_ref):
        o_ref[...] = lax.dot_general(dy_ref[...], w_ref[...], (((1,), (1,)), ((), ())), preferred_element_type=F32)

    return pl.pallas_call(
        body, name=name, grid=(SEQ // tm,),
        in_specs=[_row_spec(tm, n), _vec_spec(kc, n)],
        out_specs=_row_spec(tm, kc),
        out_shape=jax.ShapeDtypeStruct((SEQ, kc), F32),
        compiler_params=_params("parallel"),
    )(dy, w)


def _seg_matrix():
    r = lax.broadcasted_iota(jnp.int32, (256, 256), 0) // HEAD_DIM
    c = lax.broadcasted_iota(jnp.int32, (256, 256), 1) // HEAD_DIM
    return (r == c).astype(BF16)


def _segsum(v, seg):
    hi = v.astype(BF16)
    lo = (v - hi.astype(F32)).astype(BF16)
    outs = []
    for c0 in range(0, D_MODEL, 256):
        outs.append(jnp.dot(hi[:, c0:c0 + 256], seg, preferred_element_type=F32)
                    + jnp.dot(lo[:, c0:c0 + 256], seg, preferred_element_type=F32))
    return jnp.concatenate(outs, axis=1)


def _qk_rstd(v, seg):
    return lax.rsqrt(_segsum(v * v, seg) * (1.0 / HEAD_DIM) + NORM_EPS)


def _qknorm_fwd(proj, qw, kw, seg, name):
    tm = 256

    def body(p_ref, qw_ref, kw_ref, seg_ref, q_ref, k_ref, v_ref):
        segv = seg_ref[...]
        q = p_ref[:, :D_MODEL]
        k = p_ref[:, D_MODEL:2 * D_MODEL]
        q_ref[...] = (q * _qk_rstd(q, segv) * qw_ref[...]).astype(BF16)
        k_ref[...] = (k * _qk_rstd(k, segv) * kw_ref[...]).astype(BF16)
        v_ref[...] = p_ref[:, 2 * D_MODEL:].astype(BF16)

    return pl.pallas_call(
        body, name=name, grid=(SEQ // tm,),
        in_specs=[_row_spec(tm, 3 * D_MODEL), _vec_spec(1, D_MODEL), _vec_spec(1, D_MODEL), _vec_spec(256, 256)],
        out_specs=[_row_spec(tm, D_MODEL)] * 3,
        out_shape=[jax.ShapeDtypeStruct((SEQ, D_MODEL), BF16)] * 3,
        compiler_params=_params("parallel"),
    )(proj, qw, kw, seg)


def _attn_masks(b, bpc, dilation, slope):
    qi = lax.broadcasted_iota(jnp.int32, (ATTN_BLOCK, ATTN_BLOCK), 0)
    kj = lax.broadcasted_iota(jnp.int32, (ATTN_BLOCK, ATTN_BLOCK), 1)
    steps_c = qi - kj
    steps_p = steps_c + ATTN_BLOCK
    valid_c = steps_c >= 0
    valid_p = jnp.logical_and(steps_p <= ATTN_BLOCK, (b % bpc) != 0)
    bias_c = (steps_c * dilation).astype(F32) * slope
    bias_p = (steps_p * dilation).astype(F32) * slope
    return bias_p, valid_p, bias_c, valid_c


def _attn_specs(n):
    cur = pl.BlockSpec((ATTN_BLOCK, 2 * HEAD_DIM), lambda hp, b: (b, hp))
    prev = pl.BlockSpec((ATTN_BLOCK, 2 * HEAD_DIM), lambda hp, b: (jnp.maximum(b - 1, 0), hp))
    return cur, prev


def _attn_fwd(q, k, v, slopes, dilation, name):
    bpc = SEQ // dilation // ATTN_BLOCK
    cur, prev = _attn_specs(2)
    scale = HEAD_DIM ** -0.5

    def body(sl_ref, q_ref, kp_ref, kc_ref, vp_ref, vc_ref, o_ref, lse_ref):
        hp = pl.program_id(0)
        b = pl.program_id(1)
        for h in range(2):
            cols = slice(h * HEAD_DIM, (h + 1) * HEAD_DIM)
            bias_p, valid_p, bias_c, valid_c = _attn_masks(b, bpc, dilation, sl_ref[2 * hp + h])
            qh = q_ref[:, cols]
            nt = (((1,), (1,)), ((), ()))
            s_p = lax.dot_general(qh, kp_ref[:, cols], nt, preferred_element_type=F32) * scale
            s_c = lax.dot_general(qh, kc_ref[:, cols], nt, preferred_element_type=F32) * scale
            s_p = jnp.where(valid_p, s_p - bias_p, NEG_INF)
            s_c = jnp.where(valid_c, s_c - bias_c, NEG_INF)
            m = jnp.maximum(jnp.max(s_p, axis=-1, keepdims=True), jnp.max(s_c, axis=-1, keepdims=True))
            p_p = jnp.exp(s_p - m)
            p_c = jnp.exp(s_c - m)
            l = jnp.sum(p_p, axis=-1, keepdims=True) + jnp.sum(p_c, axis=-1, keepdims=True)
            acc = (jnp.dot(p_p.astype(BF16), vp_ref[:, cols], preferred_element_type=F32)
                   + jnp.dot(p_c.astype(BF16), vc_ref[:, cols], preferred_element_type=F32))
            o_ref[:, cols] = acc / l
            lse_ref[:, cols] = jnp.broadcast_to(m + jnp.log(l), (ATTN_BLOCK, HEAD_DIM))

    return pl.pallas_call(
        body, name=name, grid=(N_HEADS // 2, SEQ // ATTN_BLOCK),
        in_specs=[pl.BlockSpec(memory_space=pltpu.SMEM), cur, prev, cur, prev, cur],
        out_specs=[cur, cur],
        out_shape=[jax.ShapeDtypeStruct((SEQ, D_MODEL), F32)] * 2,
        compiler_params=_params("parallel", "parallel"),
    )(slopes, q, k, k, v, v)


def _merge_fwd(o_parts, lse_parts, z, name):
    tm = 256

    def body(o0, o1, o2, l0, l1, l2, z_ref, u_ref, ut_ref, o_ref, lse_ref):
        ls = [l0[...], l1[...], l2[...]]
        m = jnp.maximum(jnp.maximum(ls[0], ls[1]), ls[2])
        tot = m + jnp.log(jnp.exp(ls[0] - m) + jnp.exp(ls[1] - m) + jnp.exp(ls[2] - m))
        o = (jnp.exp(ls[0] - tot) * o0[...] + jnp.exp(ls[1] - tot) * o1[...] + jnp.exp(ls[2] - tot) * o2[...])
        zv = z_ref[...]
        u = o * (zv * _sigmoid(zv))
        u_ref[...] = u.astype(BF16)
        ut_ref[...] = u.T.astype(BF16)
        o_ref[...] = o
        lse_ref[...] = tot

    return pl.pallas_call(
        body, name=name, grid=(SEQ // tm,),
        in_specs=[_row_spec(tm, D_MODEL)] * 7,
        out_specs=[_row_spec(tm, D_MODEL), pl.BlockSpec((D_MODEL, tm), lambda i: (0, i)),
                   _row_spec(tm, D_MODEL), _row_spec(tm, D_MODEL)],
        out_shape=[jax.ShapeDtypeStruct((SEQ, D_MODEL), BF16), jax.ShapeDtypeStruct((D_MODEL, SEQ), BF16),
                   jax.ShapeDtypeStruct((SEQ, D_MODEL), F32), jax.ShapeDtypeStruct((SEQ, D_MODEL), F32)],
        compiler_params=_params("parallel"),
    )(*o_parts, *lse_parts, z)


def _merge_bwd(du, o, z, seg, name):
    tm = 256

    def body(du_ref, o_ref, z_ref, seg_ref, do_ref, dz_ref, delta_ref):
        zv = z_ref[...]
        sz = _sigmoid(zv)
        duv = du_ref[...]
        ov = o_ref[...]
        do = duv * (zv * sz)
        do_ref[...] = do.astype(BF16)
        dz_ref[...] = (duv * ov * (sz * (1.0 + zv * (1.0 - sz)))).astype(BF16)
        delta_ref[...] = _segsum(do * ov, seg_ref[...])

    return pl.pallas_call(
        body, name=name, grid=(SEQ // tm,),
        in_specs=[_row_spec(tm, D_MODEL)] * 3 + [_vec_spec(256, 256)],
        out_specs=[_row_spec(tm, D_MODEL)] * 3,
        out_shape=[jax.ShapeDtypeStruct((SEQ, D_MODEL), BF16), jax.ShapeDtypeStruct((SEQ, D_MODEL), BF16),
                   jax.ShapeDtypeStruct((SEQ, D_MODEL), F32)],
        compiler_params=_params("parallel"),
    )(du, o, z, seg)


def _attn_bwd(q, k, v, do, lse, delta, slopes, dilation, name):
    bpc = SEQ // dilation // ATTN_BLOCK
    cur, prev = _attn_specs(2)
    scale = HEAD_DIM ** -0.5

    def body(sl_ref, q_ref, kp_ref, kc_ref, vp_ref, vc_ref, do_ref, lse_ref, dl_ref,
             dq_ref, dkc_ref, dkp_ref, dvc_ref, dvp_ref):
        hp = pl.program_id(0)
        b = pl.program_id(1)
        nt = (((1,), (1,)), ((), ()))
        tn = (((0,), (0,)), ((), ()))
        for h in range(2):
            cols = slice(h * HEAD_DIM, (h + 1) * HEAD_DIM)
            bias_p, valid_p, bias_c, valid_c = _attn_masks(b, bpc, dilation, sl_ref[2 * hp + h])
            qh = q_ref[:, cols]
            doh = do_ref[:, cols]
            lse_col = lse_ref[:, h * HEAD_DIM:h * HEAD_DIM + 1]
            dl_col = dl_ref[:, h * HEAD_DIM:h * HEAD_DIM + 1]
            dq = jnp.zeros((ATTN_BLOCK, HEAD_DIM), F32)
            for k_ref, v_ref, bias, valid, dk_ref, dv_ref in (
                    (kp_ref, vp_ref, bias_p, valid_p, dkp_ref, dvp_ref),
                    (kc_ref, vc_ref, bias_c, valid_c, dkc_ref, dvc_ref)):
                kh = k_ref[:, cols]
                s = lax.dot_general(qh, kh, nt, preferred_element_type=F32) * scale
                p = jnp.exp(jnp.where(valid, s - bias, NEG_INF) - lse_col)
                dp = lax.dot_general(doh, v_ref[:, cols], nt, preferred_element_type=F32)
                ds = (p * (dp - dl_col) * scale).astype(BF16)
                dq = dq + jnp.dot(ds, kh, preferred_element_type=F32)
                dk_ref[:, cols] = lax.dot_general(ds, qh, tn, preferred_element_type=F32)
                dv_ref[:, cols] = lax.dot_general(p.astype(BF16), doh, tn, preferred_element_type=F32)
            dq_ref[:, cols] = dq

    return pl.pallas_call(
        body, name=name, grid=(N_HEADS // 2, SEQ // ATTN_BLOCK),
        in_specs=[pl.BlockSpec(memory_space=pltpu.SMEM), cur, prev, cur, prev, cur, cur, cur, cur],
        out_specs=[cur] * 5,
        out_shape=[jax.ShapeDtypeStruct((SEQ, D_MODEL), F32)] * 5,
        compiler_params=_params("parallel", "parallel"),
    )(slopes, q, k, k, v, v, do, lse, delta)


def _qknorm_bwd(proj, qw, kw, seg, dq, dkc, dkp, dvc, dvp, name):
    tm = ATTN_BLOCK
    n_blocks = SEQ // tm
    nxt = pl.BlockSpec((tm, D_MODEL), lambda i: (jnp.minimum(i + 1, n_blocks - 1), 0))

    def body(p_ref, qw_ref, kw_ref, seg_ref, dq_ref, dkc_ref, dkp_ref, dvc_ref, dvp_ref, dproj_ref, sums_ref):
        i = pl.program_id(0)
        segv = seg_ref[...]
        has_next = i < n_blocks - 1
        dk = dkc_ref[...] + jnp.where(has_next, dkp_ref[...], 0.0)
        dv = dvc_ref[...] + jnp.where(has_next, dvp_ref[...], 0.0)
        sums = []
        for part, (raw, w, dn) in enumerate(((p_ref[:, :D_MODEL], qw_ref[...], dq_ref[...]),
                                             (p_ref[:, D_MODEL:2 * D_MODEL], kw_ref[...], dk))):
            r = _qk_rstd(raw, segv)
            gq = dn * w
            draw = r * gq - raw * (r * r * r) * (_segsum(raw * gq, segv) * (1.0 / HEAD_DIM))
            dproj_ref[:, part * D_MODEL:(part + 1) * D_MODEL] = draw.astype(BF16)
            sums.append(jnp.sum(dn * raw * r, axis=0, keepdims=True))
        dproj_ref[:, 2 * D_MODEL:] = dv.astype(BF16)

        @pl.when(i == 0)
        def _():
            sums_ref[...] = jnp.zeros_like(sums_ref)

        sums_ref[...] += jnp.concatenate(sums + [jnp.zeros((6, D_MODEL), F32)], axis=0)

    return pl.pallas_call(
        body, name=name, grid=(n_blocks,),
        in_specs=[_row_spec(tm, 3 * D_MODEL), _vec_spec(1, D_MODEL), _vec_spec(1, D_MODEL), _vec_spec(256, 256),
                  _row_spec(tm, D_MODEL), _row_spec(tm, D_MODEL), nxt, _row_spec(tm, D_MODEL), nxt],
        out_specs=[_row_spec(tm, 3 * D_MODEL), _vec_spec(8, D_MODEL)],
        out_shape=[jax.ShapeDtypeStruct((SEQ, 3 * D_MODEL), BF16), jax.ShapeDtypeStruct((8, D_MODEL), F32)],
        compiler_params=_params("arbitrary"),
    )(proj, qw, kw, seg, dq, dkc, dkp, dvc, dvp)


def _to_classes(a, dilation):
    if dilation == 1:
        return a
    s, c = a.shape
    return a.reshape(s // dilation, dilation, c).transpose(1, 0, 2).reshape(s, c)


def _from_classes(a, dilation):
    if dilation == 1:
        return a
    s, c = a.shape
    return a.reshape(dilation, s // dilation, c).transpose(1, 0, 2).reshape(s, c)


def _cols_to_classes(a, dilation):
    if dilation == 1:
        return a
    r, s = a.shape
    return a.reshape(r, s // dilation, dilation).transpose(0, 2, 1).reshape(r, s)


B_TN = 512
B_GROUP_TILES = 3 * D_MODEL // B_TN
B_Z_TILE0 = 3 * B_GROUP_TILES
B_Z_TILES = D_MODEL // B_TN


def _local_step(x, target, mods, norm_g, wa_in, conv_w, conv_b, ln_g, ln_b, wa_out, wb_in, q_norm, k_norm, wb_out):
    row = lambda a, i: a[i:i + 1]
    shift0, scale0, gate0 = row(mods[0], 0), row(mods[0], 1), row(mods[0], 2)
    shift1, scale1, gate1 = row(mods[1], 0), row(mods[1], 1), row(mods[1], 2)
    g0, g1 = row(norm_g, 0), row(norm_g, 1)
    ja, _, nsa = wa_in.shape
    jb, _, nsb = wb_in.shape
    seg = _seg_matrix()
    slopes = jnp.exp2(-8.0 * jnp.arange(1, N_HEADS + 1, dtype=F32) / N_HEADS)
    qw = [jnp.tile(q_norm[g:g + 1], (1, N_HEADS)) for g in range(3)]
    kw = [jnp.tile(k_norm[g:g + 1], (1, N_HEADS)) for g in range(3)]

    h0, h0t = _normmod_fwd(x, g0, scale0, shift0, "prenorm0")
    proj_a = _mm(h0, wa_in, tn=nsa, tile0=0, n_tiles=ja, out_dtype=F32, name="a_in")
    u5, u5t, u2 = _conv_fwd(proj_a, conv_w, conv_b, ln_g, ln_b, "a_conv")
    x1, y_a, h1, h1t = _out_a(u5, wa_out, x, gate0, g1, scale1, shift1, "a_out")

    h1c = [_to_classes(h1, d) for d in DILATIONS]
    h1tc = [_cols_to_classes(h1t, d) for d in DILATIONS]
    z_b = _mm(h1, wb_in, tn=B_TN, tile0=B_Z_TILE0, n_tiles=B_Z_TILES, out_dtype=F32, name="b_in_z")
    proj_g, qkv, o_parts, lse_parts = [], [], [], []
    for g, d in enumerate(DILATIONS):
        pg = _mm(h1c[g], wb_in, tn=B_TN, tile0=g * B_GROUP_TILES, n_tiles=B_GROUP_TILES, out_dtype=F32,
                 name=f"b_in_g{g}")
        qn, kn, vn = _qknorm_fwd(pg, qw[g], kw[g], seg, f"b_qknorm_g{g}")
        og, lg = _attn_fwd(qn, kn, vn, slopes, d, f"b_attn_g{g}")
        proj_g.append(pg)
        qkv.append((qn, kn, vn))
        o_parts.append(_from_classes(og, d))
        lse_parts.append(_from_classes(lg, d))
    u_b, u_bt, o_b, lse_b = _merge_fwd(o_parts, lse_parts, z_b, "b_merge")
    e, dy_b, sums_loss = _out_b_loss(u_b, wb_out, x1, gate1, target, "b_out_loss")

    dwb_out = _mm(u_bt, dy_b, tn=D_MODEL, tile0=0, n_tiles=1, out_dtype=BF16, name="b_dwout")
    du_b = _mm_nt_res(dy_b, wb_out, "b_dout")
    do_b, dz_b, delta_b = _merge_bwd(du_b, o_b, z_b, seg, "b_merge_bwd")
    dwb_in = _mm(h1t, dz_b, tn=B_TN, tile0=B_Z_TILE0, n_tiles=B_Z_TILES, out_dtype=BF16, name="b_dwin_z",
                 out3d=(jb, nsb))
    dh1_parts = [_mm_nt(dz_b, wb_in, tn=B_TN, tile0=B_Z_TILE0, n_tiles=B_Z_TILES, name="b_dh_z")]
    qk_sums = []
    for g, d in enumerate(DILATIONS):
        qn, kn, vn = qkv[g]
        dq, dkc, dkp, dvc, dvp = _attn_bwd(qn, kn, vn, _to_classes(do_b, d), _to_classes(lse_b, d),
                                           _to_classes(delta_b, d), slopes, d, f"b_attn_bwd_g{g}")
        dproj, sums_qk = _qknorm_bwd(proj_g[g], qw[g], kw[g], seg, dq, dkc, dkp, dvc, dvp, f"b_qknorm_bwd_g{g}")
        qk_sums.append(sums_qk)
        dwb_in = _mm(h1tc[g], dproj, tn=B_TN, tile0=g * B_GROUP_TILES, n_tiles=B_GROUP_TILES, out_dtype=BF16,
                     name=f"b_dwin_g{g}", out3d=(jb, nsb), prev=dwb_in)
        dh = _mm_nt(dproj, wb_in, tn=B_TN, tile0=g * B_GROUP_TILES, n_tiles=B_GROUP_TILES, name=f"b_dh_g{g}")
        dh1_parts.append(_from_classes(dh, d))
    dx1, sums_n1 = _normmod_bwd(x1, g1, scale1, dh1_parts, e, "prenorm1_bwd")

    dy_a, sums_ga = _dgate_dy(dx1, y_a, gate0, "a_dgate")
    dwa_out = _mm(u5t, dy_a, tn=D_MODEL, tile0=0, n_tiles=1, out_dtype=BF16, name="a_dwout")
    du5 = _mm_nt_res(dy_a, wa_out, "a_dout")
    du2, dz_a, sums_ln = _conv_bwd_pointwise(du5, proj_a, u2, ln_g, ln_b, "a_conv_bwd_pw")
    dproj_a, dconv_w = _conv_bwd_taps(du2, dz_a, proj_a, conv_w, "a_conv_bwd_taps")
    dwa_in = _mm(h0t, dproj_a, tn=nsa, tile0=0, n_tiles=ja, out_dtype=BF16, name="a_dwin", out3d=(ja, nsa))
    dh0 = _mm_nt(dproj_a, wa_in, tn=nsa, tile0=0, n_tiles=ja, name="a_dh")
    grad_x, sums_n0 = _normmod_bwd(x, g0, scale0, [dh0], dx1, "prenorm0_bwd")

    small = dict(
        dnorm_g=jnp.concatenate([sums_n0[0:1], sums_n1[0:1]], axis=0),
        dmod0=jnp.concatenate([sums_n0[2:3], sums_n0[1:2], sums_ga[0:1]], axis=0),
        dmod1=jnp.concatenate([sums_n1[2:3], sums_n1[1:2], sums_loss[0:1]], axis=0),
        dln_g=sums_ln[0:1], dln_b=sums_ln[1:2], dconv_b=sums_ln[2:3],
        dconv_w=dconv_w[:CONV_WIDTH],
        dq_norm=jnp.concatenate([s[0:1] for s in qk_sums], axis=0),
        dk_norm=jnp.concatenate([s[1:2] for s in qk_sums], axis=0),
        loss_cols=sums_loss[1:2],
    )
    big = dict(a_w_in=dwa_in, a_w_out=dwa_out, b_w_in=dwb_in, b_w_out=dwb_out)
    return grad_x, big, small


def _adamw(w, g, m, v, name):
    rows, cols = w.shape
    tr = rows if rows <= 128 else 128
    c1 = 1.0 / (1.0 - ADAM_B1 ** ADAM_STEP)
    c2 = 1.0 / (1.0 - ADAM_B2 ** ADAM_STEP)

    def body(w_ref, g_ref, m_ref, v_ref, d_ref, mo_ref, vo_ref):
        gv = g_ref[...]
        mn = ADAM_B1 * m_ref[...] + (1.0 - ADAM_B1) * gv
        vn = ADAM_B2 * v_ref[...] + (1.0 - ADAM_B2) * (gv * gv)
        mo_ref[...] = mn
        vo_ref[...] = vn
        d_ref[...] = -ADAM_LR * ((mn * c1) / (jnp.sqrt(vn * c2) + ADAM_EPS) + ADAM_WD * w_ref[...])

    spec = pl.BlockSpec((tr, cols), lambda i: (i, 0))
    return pl.pallas_call(
        body, name=name, grid=(rows // tr,), in_specs=[spec] * 4, out_specs=[spec] * 3,
        out_shape=[jax.ShapeDtypeStruct((rows, cols), F32)] * 3,
        compiler_params=_params("parallel"),
    )(w, g, m, v)


def _cast_bf16(w, name):
    rows, cols = w.shape
    tr = 256

    def body(w_ref, o_ref):
        o_ref[...] = w_ref[...].astype(BF16)

    spec = pl.BlockSpec((tr, cols), lambda i: (i, 0))
    return pl.pallas_call(
        body, name=name, grid=(rows // tr,), in_specs=[spec], out_specs=spec,
        out_shape=jax.ShapeDtypeStruct((rows, cols), BF16), compiler_params=_params("parallel"),
    )(w)


def _position():
    x, y, c = lax.axis_index("x"), lax.axis_index("y"), lax.axis_index("c")
    return x, y, c


def _xor_peer(x, y, c, k):
    return (x ^ ((k >> 2) & 1), y ^ ((k >> 1) & 1), c ^ (k & 1))


def _chip_peer(x, y, k):
    return (x ^ ((k >> 1) & 1), y ^ (k & 1))


def _ada_forward(c_row, ada_w, ada_b, conv_w):
    ns = ada_w.shape[2]
    cw = conv_w.shape[1]

    def body(c_ref, w_ref, b_ref, cv_ref, mod_ref, sc_ref, cvo_ref,
             c_all, mp, parts, cv_parts, send1, recv1, send2, recv2, send3, recv3):
        x, y, c = _position()
        me = 4 * x + 2 * y + c
        chip = 2 * x + y

        def c_copy(k):
            return pltpu.make_async_remote_copy(
                src_ref=c_all.at[me], dst_ref=c_all.at[me], send_sem=send1.at[k - 1], recv_sem=recv1.at[k - 1],
                device_id=_xor_peer(x, y, c, k), device_id_type=MESH)

        def cv_copy(k):
            px, py = _chip_peer(x, y, k)
            return pltpu.make_async_remote_copy(
                src_ref=cv_parts.at[chip], dst_ref=cv_parts.at[chip], send_sem=send3.at[k - 1],
                recv_sem=recv3.at[k - 1], device_id=(px, py, c), device_id_type=MESH)

        c_all[me] = c_ref[...]
        cv_parts[chip] = cv_ref[...]
        for k in range(1, N_DEV):
            c_copy(k).start()
        for k in range(1, N_CHIPS):
            cv_copy(k).start()
        for k in range(1, N_DEV):
            c_copy(k).wait_recv()
        cv = jnp.concatenate([c_all[i] for i in range(N_DEV)], axis=0)
        sc = cv * _sigmoid(cv)
        sc_ref[...] = sc
        for l in range(2):
            res = jnp.dot(sc, w_ref[l], preferred_element_type=F32, precision=lax.Precision.HIGHEST)
            for i in range(N_DEV):
                mp[i, l:l + 1, :] = res[i:i + 1, :]

        def mod_copy(k):
            px, py = _chip_peer(x, y, k)
            return pltpu.make_async_remote_copy(
                src_ref=mp.at[4 * px + 2 * py + c], dst_ref=parts.at[chip], send_sem=send2.at[k - 1],
                recv_sem=recv2.at[k - 1], device_id=(px, py, c), device_id_type=MESH)

        for k in range(1, N_CHIPS):
            mod_copy(k).start()
        parts[chip] = mp[me]
        for k in range(1, N_CHIPS):
            mod_copy(k).wait_recv()
            cv_copy(k).wait_recv()
        mod_ref[...] = jnp.concatenate([parts[j] for j in range(N_CHIPS)], axis=1) + b_ref[...]
        cvo_ref[...] = jnp.concatenate([cv_parts[j] for j in range(N_CHIPS)], axis=1)
        for k in range(1, N_DEV):
            c_copy(k).wait_send()
        for k in range(1, N_CHIPS):
            mod_copy(k).wait_send()
            cv_copy(k).wait_send()

    vm = pl.BlockSpec(memory_space=pltpu.VMEM)
    return pl.pallas_call(
        body, name="ada_forward",
        in_specs=[vm] * 4, out_specs=[vm] * 3,
        out_shape=[jax.ShapeDtypeStruct((2, 3 * D_MODEL), F32), jax.ShapeDtypeStruct((N_DEV, D_MODEL), F32),
                   jax.ShapeDtypeStruct((CONV_WIDTH, N_CHIPS * cw), F32)],
        scratch_shapes=[pltpu.VMEM((N_DEV, 1, D_MODEL), F32), pltpu.VMEM((N_DEV, 2, ns), F32),
                        pltpu.VMEM((N_CHIPS, 2, ns), F32), pltpu.VMEM((N_CHIPS, CONV_WIDTH, cw), F32),
                        pltpu.SemaphoreType.DMA((N_DEV - 1,)), pltpu.SemaphoreType.DMA((N_DEV - 1,)),
                        pltpu.SemaphoreType.DMA((N_CHIPS - 1,)), pltpu.SemaphoreType.DMA((N_CHIPS - 1,)),
                        pltpu.SemaphoreType.DMA((N_CHIPS - 1,)), pltpu.SemaphoreType.DMA((N_CHIPS - 1,))],
        compiler_params=pltpu.CompilerParams(vmem_limit_bytes=VMEM_LIMIT_BYTES),
    )(c_row, ada_w, ada_b, conv_w)


def _gather_weights(shards):
    n = len(shards)
    hbm = pl.BlockSpec(memory_space=pl.ANY)

    def body(*refs):
        ins, outs = refs[:n], refs[n:2 * n]
        send_ici, recv_ici, send_d2d, recv_d2d, local = refs[2 * n:]
        x, y, c = _position()
        chip = 2 * x + y
        started = []
        copies = []
        for t in range(n):
            rh = ins[t].shape[0] // 2
            half = pl.ds(c * rh, rh)
            cp = pltpu.make_async_copy(ins[t], outs[t].at[chip], local.at[t])
            cp.start()
            copies.append(cp)
            for k in range(1, N_CHIPS):
                px, py = _chip_peer(x, y, k)
                s = 3 * t + k - 1
                cp = pltpu.make_async_remote_copy(
                    src_ref=ins[t].at[half], dst_ref=outs[t].at[chip, half], send_sem=send_ici.at[s],
                    recv_sem=recv_ici.at[s], device_id=(px, py, c), device_id_type=MESH)
                cp.start()
                started.append(cp)
        for t in range(n):
            rh = ins[t].shape[0] // 2
            half = pl.ds(c * rh, rh)
            for k in range(1, N_CHIPS):
                px, py = _chip_peer(x, y, k)
                src_chip = 2 * px + py
                s = 3 * t + k - 1
                block = outs[t].at[src_chip, half]
                pltpu.make_async_remote_copy(
                    src_ref=block, dst_ref=block, send_sem=send_ici.at[s], recv_sem=recv_ici.at[s],
                    device_id=(px, py, c), device_id_type=MESH).wait_recv()
                cp = pltpu.make_async_remote_copy(
                    src_ref=block, dst_ref=block, send_sem=send_d2d.at[s], recv_sem=recv_d2d.at[s],
                    device_id=(x, y, 1 - c), device_id_type=MESH)
                cp.start()
                started.append(cp)
        for t in range(n):
            rh = ins[t].shape[0] // 2
            other = pl.ds((1 - c) * rh, rh)
            for k in range(1, N_CHIPS):
                px, py = _chip_peer(x, y, k)
                s = 3 * t + k - 1
                block = outs[t].at[2 * px + py, other]
                pltpu.make_async_remote_copy(
                    src_ref=block, dst_ref=block, send_sem=send_d2d.at[s], recv_sem=recv_d2d.at[s],
                    device_id=(x, y, 1 - c), device_id_type=MESH).wait_recv()
        for cp in started:
            cp.wait_send()
        for cp in copies:
            cp.wait()

    return pl.pallas_call(
        body, name="gather_weights", in_specs=[hbm] * n, out_specs=[hbm] * n,
        out_shape=[jax.ShapeDtypeStruct((N_CHIPS,) + s.shape, BF16) for s in shards],
        scratch_shapes=[pltpu.SemaphoreType.DMA((3 * n,)), pltpu.SemaphoreType.DMA((3 * n,)),
                        pltpu.SemaphoreType.DMA((3 * n,)), pltpu.SemaphoreType.DMA((3 * n,)),
                        pltpu.SemaphoreType.DMA((n,))],
    )(*shards)


def _exchange_halves(grads):
    n = len(grads)
    hbm = pl.BlockSpec(memory_space=pl.ANY)

    def body(*refs):
        ins, outs = refs[:n], refs[n:2 * n]
        send, recv = refs[2 * n:]
        x, y, c = _position()
        cps = []
        for t in range(n):
            rh = ins[t].shape[1] // 2
            cp = pltpu.make_async_remote_copy(
                src_ref=ins[t].at[pl.ds(0, N_CHIPS), pl.ds((1 - c) * rh, rh)], dst_ref=outs[t], send_sem=send.at[t],
                recv_sem=recv.at[t], device_id=(x, y, 1 - c), device_id_type=MESH)
            cp.start()
            cps.append(cp)
        for cp in cps:
            cp.wait()

    return pl.pallas_call(
        body, name="reduce_exchange_halves", in_specs=[hbm] * n, out_specs=[hbm] * n,
        out_shape=[jax.ShapeDtypeStruct((g.shape[0], g.shape[1] // 2, g.shape[2]), BF16) for g in grads],
        scratch_shapes=[pltpu.SemaphoreType.DMA((n,)), pltpu.SemaphoreType.DMA((n,))],
    )(*grads)


def _add_halves(grad, got, c_idx, name):
    j, r, cols = grad.shape
    rh = r // 2
    tr = 128
    nb = rh // tr

    def body(c_ref, g_ref, o_ref_in, out_ref):
        out_ref[...] = (g_ref[...].astype(F32) + o_ref_in[...].astype(F32)).astype(BF16)

    return pl.pallas_call(
        body, name=name,
        grid_spec=pltpu.PrefetchScalarGridSpec(
            num_scalar_prefetch=1, grid=(j, nb),
            in_specs=[pl.BlockSpec((None, tr, cols), lambda jj, i, c_ref: (jj, c_ref[0] * nb + i, 0)),
                      pl.BlockSpec((None, tr, cols), lambda jj, i, c_ref: (jj, i, 0))],
            out_specs=pl.BlockSpec((None, tr, cols), lambda jj, i, c_ref: (jj, i, 0))),
        out_shape=jax.ShapeDtypeStruct((j, rh, cols), BF16),
        compiler_params=_params("parallel", "parallel"),
    )(c_idx, grad, got)


def _scatter_partials(partials):
    n = len(partials)
    hbm = pl.BlockSpec(memory_space=pl.ANY)

    def body(*refs):
        ins, outs = refs[:n], refs[n:2 * n]
        send, recv, local = refs[2 * n:]
        x, y, c = _position()
        chip = 2 * x + y
        cps, lcs = [], []
        for t in range(n):
            lc = pltpu.make_async_copy(ins[t].at[chip], outs[t].at[chip], local.at[t])
            lc.start()
            lcs.append(lc)
            for k in range(1, N_CHIPS):
                px, py = _chip_peer(x, y, k)
                s = 3 * t + k - 1
                cp = pltpu.make_async_remote_copy(
                    src_ref=ins[t].at[2 * px + py], dst_ref=outs[t].at[chip], send_sem=send.at[s],
                    recv_sem=recv.at[s], device_id=(px, py, c), device_id_type=MESH)
                cp.start()
                cps.append(cp)
        for cp in cps:
            cp.wait()
        for lc in lcs:
            lc.wait()

    return pl.pallas_call(
        body, name="reduce_scatter_partials", in_specs=[hbm] * n, out_specs=[hbm] * n,
        out_shape=[jax.ShapeDtypeStruct(p.shape, BF16) for p in partials],
        scratch_shapes=[pltpu.SemaphoreType.DMA((3 * n,)), pltpu.SemaphoreType.DMA((3 * n,)),
                        pltpu.SemaphoreType.DMA((n,))],
    )(*partials)


def _sum_chips(parts, name):
    j, rh, cols = parts.shape
    tr = 128

    def body(p_ref, o_ref):
        acc = p_ref[0].astype(F32)
        for s in range(1, j):
            acc = acc + p_ref[s].astype(F32)
        o_ref[...] = acc

    return pl.pallas_call(
        body, name=name, grid=(rh // tr,),
        in_specs=[pl.BlockSpec((j, tr, cols), lambda i: (0, i, 0))],
        out_specs=pl.BlockSpec((tr, cols), lambda i: (i, 0)),
        out_shape=jax.ShapeDtypeStruct((rh, cols), F32),
        compiler_params=_params("parallel"),
    )(parts)


def _share_totals(halves):
    n = len(halves)
    hbm = pl.BlockSpec(memory_space=pl.ANY)

    def body(*refs):
        ins, outs = refs[:n], refs[n:2 * n]
        send, recv, local = refs[2 * n:]
        x, y, c = _position()
        cps, lcs = [], []
        for t in range(n):
            rh = ins[t].shape[0]
            mine = outs[t].at[pl.ds(c * rh, rh)]
            lc = pltpu.make_async_copy(ins[t], mine, local.at[t])
            lc.start()
            lcs.append(lc)
            cp = pltpu.make_async_remote_copy(
                src_ref=ins[t], dst_ref=mine, send_sem=send.at[t], recv_sem=recv.at[t],
                device_id=(x, y, 1 - c), device_id_type=MESH)
            cp.start()
            cps.append(cp)
        for cp in cps:
            cp.wait()
        for lc in lcs:
            lc.wait()

    return pl.pallas_call(
        body, name="reduce_share_totals", in_specs=[hbm] * n, out_specs=[hbm] * n,
        out_shape=[jax.ShapeDtypeStruct((2 * h.shape[0], h.shape[1]), F32) for h in halves],
        scratch_shapes=[pltpu.SemaphoreType.DMA((n,)), pltpu.SemaphoreType.DMA((n,)),
                        pltpu.SemaphoreType.DMA((n,))],
    )(*halves)


SMALL_ROWS = 56


def _reduce_small(packed, silu_c):
    ns = 3 * D_MODEL // N_CHIPS

    def body(p_ref, sc_ref, tot_ref, gw_ref, loss_ref, qk_ref, allp, send, recv):
        x, y, c = _position()
        me = 4 * x + 2 * y + c
        chip = 2 * x + y

        def copy(k):
            return pltpu.make_async_remote_copy(
                src_ref=allp.at[me], dst_ref=allp.at[me], send_sem=send.at[k - 1], recv_sem=recv.at[k - 1],
                device_id=_xor_peer(x, y, c, k), device_id_type=MESH)

        allp[me] = p_ref[...]
        for k in range(1, N_DEV):
            copy(k).start()
        for k in range(1, N_DEV):
            copy(k).wait_recv()
        tot = allp[0]
        for i in range(1, N_DEV):
            tot = tot + allp[i]
        tot_ref[...] = tot
        loss_ref[...] = jnp.sum(tot[11:12, :], axis=1, keepdims=True) * (0.5 / D_MODEL)
        fold = tot[5:11, 0:HEAD_DIM]
        for h in range(1, N_HEADS):
            fold = fold + tot[5:11, h * HEAD_DIM:(h + 1) * HEAD_DIM]
        qk_ref[...] = jnp.concatenate([fold, jnp.zeros((2, HEAD_DIM), F32)], axis=0)
        sct = sc_ref[...].T
        rc = 64
        for l in range(2):
            dms = [allp[i, pl.ds(12 + 4 * l + chip, 1), :][:, :ns] for i in range(N_DEV)]
            for r0 in range(0, D_MODEL, rc):
                acc = sct[r0:r0 + rc, 0:1] * dms[0]
                for i in range(1, N_DEV):
                    acc = acc + sct[r0:r0 + rc, i:i + 1] * dms[i]
                gw_ref[l, r0:r0 + rc, :] = acc
        for k in range(1, N_DEV):
            copy(k).wait_send()

    vm = pl.BlockSpec(memory_space=pltpu.VMEM)
    return pl.pallas_call(
        body, name="reduce_small", in_specs=[vm, vm], out_specs=[vm] * 4,
        out_shape=[jax.ShapeDtypeStruct((SMALL_ROWS, D_MODEL), F32), jax.ShapeDtypeStruct((2, D_MODEL, ns), F32),
                   jax.ShapeDtypeStruct((1, 1), F32), jax.ShapeDtypeStruct((8, HEAD_DIM), F32)],
        scratch_shapes=[pltpu.VMEM((N_DEV, SMALL_ROWS, D_MODEL), F32),
                        pltpu.SemaphoreType.DMA((N_DEV - 1,)), pltpu.SemaphoreType.DMA((N_DEV - 1,))],
        compiler_params=pltpu.CompilerParams(vmem_limit_bytes=VMEM_LIMIT_BYTES),
    )(packed, silu_c)


def _reduce_big(grads, c_idx):
    names = list(grads)
    got = _exchange_halves([grads[k] for k in names])
    partials = [_add_halves(grads[k], got[i], c_idx, f"reduce_add_{k}") for i, k in enumerate(names)]
    parts = _scatter_partials(partials)
    halves = [_sum_chips(parts[i], f"reduce_sum_{k}") for i, k in enumerate(names)]
    totals = _share_totals(halves)
    return dict(zip(names, totals))


def kernel(x, c, norm_g, ada_w, ada_b, a_w_in, a_conv_w, a_conv_b, a_ln_g, a_ln_b, a_w_out, b_w_in, b_q_norm, b_k_norm, b_w_out, loss_target, m_norm_g, m_ada_w, m_ada_b, m_a_w_in, m_a_conv_w, m_a_conv_b, m_a_ln_g, m_a_ln_b, m_a_w_out, m_b_w_in, m_b_q_norm, m_b_k_norm, m_b_w_out, v_norm_g, v_ada_w, v_ada_b, v_a_w_in, v_a_conv_w, v_a_conv_b, v_a_ln_g, v_a_ln_b, v_a_w_out, v_b_w_in, v_b_q_norm, v_b_k_norm, v_b_w_out):
    chip = 2 * lax.axis_index("x") + lax.axis_index("y")
    c_idx = lax.axis_index("c").astype(jnp.int32).reshape(1)

    mods, silu_c, conv_w_full = _ada_forward(c, ada_w, ada_b, a_conv_w[0])
    names = ["a_w_in", "a_w_out", "b_w_in", "b_w_out"]
    shards = dict(a_w_in=a_w_in[0], a_w_out=a_w_out[0], b_w_in=b_w_in[0], b_w_out=b_w_out[0])
    gathered = _gather_weights([_cast_bf16(shards[k], f"cast_{k}") for k in names])
    wa_in, wa_out, wb_in, wb_out = gathered
    wa_out = wa_out.reshape(D_MODEL, D_MODEL)
    wb_out = wb_out.reshape(D_MODEL, D_MODEL)

    grad_x, big, small = _local_step(
        x[0], loss_target[0], mods.reshape(2, 3, D_MODEL), norm_g, wa_in, conv_w_full, a_conv_b, a_ln_g[0:1],
        a_ln_b[0:1], wa_out, wb_in, b_q_norm[0], b_k_norm[0], wb_out)
    big["a_w_out"] = big["a_w_out"].reshape(N_CHIPS, D_MODEL // N_CHIPS, D_MODEL)
    big["b_w_out"] = big["b_w_out"].reshape(N_CHIPS, D_MODEL // N_CHIPS, D_MODEL)

    ns = 3 * D_MODEL // N_CHIPS
    pad_mod = lambda dm: jnp.pad(dm.reshape(N_CHIPS, ns), ((0, 0), (0, D_MODEL - ns)))
    packed = jnp.concatenate([
        small["dnorm_g"], small["dconv_b"], small["dln_g"], small["dln_b"], small["dq_norm"], small["dk_norm"],
        small["loss_cols"], pad_mod(small["dmod0"]), pad_mod(small["dmod1"]), small["dconv_w"],
        jnp.zeros((SMALL_ROWS - 20 - CONV_WIDTH, D_MODEL), F32)], axis=0)
    tot, g_ada_w, loss, qk = _reduce_small(packed, silu_c)
    cw = D_MODEL // N_CHIPS
    g_small = dict(
        norm_g=tot[0:2], a_conv_b=tot[2:3], a_ln_g=tot[3:4], a_ln_b=tot[4:5],
        b_q_norm=qk[0:3], b_k_norm=qk[3:6],
        ada_b=jnp.stack([tot[12:16, :ns].reshape(3 * D_MODEL), tot[16:20, :ns].reshape(3 * D_MODEL)]),
        a_conv_w=lax.dynamic_slice(tot[20:20 + CONV_WIDTH], (0, chip * cw), (CONV_WIDTH, cw)),
    )

    g_big = _reduce_big(big, c_idx)

    given = dict(norm_g=(norm_g, m_norm_g, v_norm_g), ada_w=(ada_w, m_ada_w, v_ada_w), ada_b=(ada_b, m_ada_b, v_ada_b),
                 a_w_in=(a_w_in, m_a_w_in, v_a_w_in), a_conv_w=(a_conv_w, m_a_conv_w, v_a_conv_w),
                 a_conv_b=(a_conv_b, m_a_conv_b, v_a_conv_b), a_ln_g=(a_ln_g, m_a_ln_g, v_a_ln_g),
                 a_ln_b=(a_ln_b, m_a_ln_b, v_a_ln_b), a_w_out=(a_w_out, m_a_w_out, v_a_w_out),
                 b_w_in=(b_w_in, m_b_w_in, v_b_w_in), b_q_norm=(b_q_norm, m_b_q_norm, v_b_q_norm),
                 b_k_norm=(b_k_norm, m_b_k_norm, v_b_k_norm), b_w_out=(b_w_out, m_b_w_out, v_b_w_out))
    grads2d = dict(g_small)
    grads2d.update(g_big)
    grads2d["ada_w"] = g_ada_w.reshape(2 * D_MODEL, ns)
    order = ["norm_g", "ada_w", "ada_b", "a_w_in", "a_conv_w", "a_conv_b", "a_ln_g", "a_ln_b", "a_w_out", "b_w_in",
             "b_q_norm", "b_k_norm", "b_w_out"]
    out_g, out_d, out_m, out_v = [], [], [], []
    for k in order:
        w, m, v = given[k]
        g2 = grads2d[k]
        shape2 = g2.shape
        d2, m2, v2 = _adamw(w.reshape(shape2), g2, m.reshape(shape2), v.reshape(shape2), f"adamw_{k}")
        out_g.append(g2.reshape(w.shape))
        out_d.append(d2.reshape(w.shape))
        out_m.append(m2.reshape(w.shape))
        out_v.append(v2.reshape(w.shape))
    return (loss.reshape(()), grad_x[None], *out_g, *out_d, *out_m, *out_v)
```

```python
import functools

import jax
import jax.numpy as jnp
from jax import lax
from jax.experimental import pallas as pl
from jax.experimental.pallas import tpu as pltpu

F32 = jnp.float32
BF16 = jnp.bfloat16

SEQ = 2048
D_MODEL = 1024
CONV_WIDTH = 31
HEAD_DIM = 64
N_HEADS = 16
DILATIONS = (1, 4, 16)
ATTN_BLOCK = 128
NORM_EPS = 1e-6
NEG_INF = -1e30
N_DEV = 8
N_CHIPS = 4

ADAM_LR = 0.001
ADAM_B1 = 0.9
ADAM_B2 = 0.999
ADAM_EPS = 1e-08
ADAM_WD = 0.01
ADAM_STEP = 10

VMEM_LIMIT_BYTES = 52 * 1024 * 1024
HALO = 32
MESH = pl.DeviceIdType.MESH


def _params(*sem):
    return pltpu.CompilerParams(dimension_semantics=sem or None, vmem_limit_bytes=VMEM_LIMIT_BYTES)


def _sigmoid(v):
    return 1.0 / (1.0 + jnp.exp(-v))


def _row_spec(tm, cols, col_block=0):
    return pl.BlockSpec((tm, cols), lambda i: (i, col_block))


def _vec_spec(rows, cols):
    return pl.BlockSpec((rows, cols), lambda i: (0, 0))


def _normmod(xv, g, scale, shift):
    r = lax.rsqrt(jnp.mean(xv * xv, axis=-1, keepdims=True) + NORM_EPS)
    return xv * r * g * (1.0 + scale) + shift


def _normmod_fwd(x, g, scale, shift, name):
    tm = 256

    def body(x_ref, g_ref, sc_ref, sh_ref, h_ref, ht_ref):
        h = _normmod(x_ref[...], g_ref[...], sc_ref[...], sh_ref[...])
        h_ref[...] = h.astype(BF16)
        ht_ref[...] = h.T.astype(BF16)

    return pl.pallas_call(
        body, name=name, grid=(SEQ // tm,),
        in_specs=[_row_spec(tm, D_MODEL)] + [_vec_spec(1, D_MODEL)] * 3,
        out_specs=[_row_spec(tm, D_MODEL), pl.BlockSpec((D_MODEL, tm), lambda i: (0, i))],
        out_shape=[jax.ShapeDtypeStruct((SEQ, D_MODEL), BF16), jax.ShapeDtypeStruct((D_MODEL, SEQ), BF16)],
        compiler_params=_params("parallel"),
    )(x, g, scale, shift)


def _normmod_bwd(x, g, scale, dh_parts, dres, name):
    tm = 256
    n_parts = len(dh_parts)

    def body(x_ref, g_ref, sc_ref, dres_ref, *rest):
        part_refs = rest[:n_parts]
        dx_ref, sums_ref = rest[n_parts:]
        xv = x_ref[...]
        r = lax.rsqrt(jnp.mean(xv * xv, axis=-1, keepdims=True) + NORM_EPS)
        xn = xv * r
        dh = part_refs[0][...]
        for p in part_refs[1:]:
            dh = dh + p[...]
        gv = g_ref[...]
        one_sc = 1.0 + sc_ref[...]
        dxn = dh * (gv * one_sc)
        dx = r * (dxn - xn * jnp.mean(dxn * xn, axis=-1, keepdims=True))
        dx_ref[...] = dres_ref[...] + dx
        dhx = dh * xn
        sums = jnp.concatenate([
            jnp.sum(dhx, axis=0, keepdims=True) * one_sc,
            jnp.sum(dhx, axis=0, keepdims=True) * gv,
            jnp.sum(dh, axis=0, keepdims=True),
            jnp.zeros((5, D_MODEL), F32)], axis=0)

        @pl.when(pl.program_id(0) == 0)
        def _():
            sums_ref[...] = jnp.zeros_like(sums_ref)

        sums_ref[...] += sums

    return pl.pallas_call(
        body, name=name, grid=(SEQ // tm,),
        in_specs=[_row_spec(tm, D_MODEL), _vec_spec(1, D_MODEL), _vec_spec(1, D_MODEL), _row_spec(tm, D_MODEL)]
        + [_row_spec(tm, D_MODEL)] * n_parts,
        out_specs=[_row_spec(tm, D_MODEL), _vec_spec(8, D_MODEL)],
        out_shape=[jax.ShapeDtypeStruct((SEQ, D_MODEL), F32), jax.ShapeDtypeStruct((8, D_MODEL), F32)],
        compiler_params=_params("arbitrary"),
    )(x, g, scale, dres, *dh_parts)


def _mm(lhs, rhs, *, tn, tile0, n_tiles, out_dtype, name, out3d=None, prev=None):
    mo, kc = lhs.shape
    cm = 512

    def body(l_ref, r_ref, *rest):
        o_ref = rest[-1]
        for m in range(mo // cm):
            rows = pl.ds(m * cm, cm)
            o_ref[rows, :] = jnp.dot(l_ref[rows, :], r_ref[...], preferred_element_type=F32).astype(out_dtype)

    if rhs.ndim == 3:
        tps_r = rhs.shape[2] // tn
        r_spec = pl.BlockSpec((None, kc, tn), lambda t: ((tile0 + t) // tps_r, 0, (tile0 + t) % tps_r))
    else:
        r_spec = pl.BlockSpec((kc, tn), lambda t: (0, t))
    in_specs = [pl.BlockSpec((mo, kc), lambda t: (0, 0)), r_spec]
    args = [lhs, rhs]
    aliases = {}
    if out3d is None:
        o_spec = pl.BlockSpec((mo, tn), lambda t: (0, t))
        o_shape = jax.ShapeDtypeStruct((mo, n_tiles * tn), out_dtype)
    else:
        j_out, ns_out = out3d
        tps_o = ns_out // tn
        o_spec = pl.BlockSpec((None, mo, tn), lambda t: ((tile0 + t) // tps_o, 0, (tile0 + t) % tps_o))
        o_shape = jax.ShapeDtypeStruct((j_out, mo, ns_out), out_dtype)
        if prev is not None:
            in_specs.append(pl.BlockSpec(memory_space=pl.ANY))
            args.append(prev)
            aliases = {2: 0}
    return pl.pallas_call(
        body, name=name, grid=(n_tiles,), in_specs=in_specs, out_specs=o_spec, out_shape=o_shape,
        input_output_aliases=aliases, compiler_params=_params("parallel"),
    )(*args)


def _mm_nt(dy, w3, *, tn, tile0, n_tiles, name, after=None):
    m_rows = dy.shape[0]
    _, kc, ns = w3.shape
    tps = ns // tn
    cm = 512
    extra = [] if after is None else [after]

    def body(dy_ref, w_ref, *rest):
        o_ref = rest[-1]

        @pl.when(pl.program_id(0) == 0)
        def _():
            o_ref[...] = jnp.zeros_like(o_ref)

        for m in range(m_rows // cm):
            rows = pl.ds(m * cm, cm)
            o_ref[rows, :] += lax.dot_general(dy_ref[rows, :], w_ref[...], (((1,), (1,)), ((), ())),
                                              preferred_element_type=F32)

    return pl.pallas_call(
        body, name=name, grid=(n_tiles,),
        in_specs=[pl.BlockSpec((m_rows, tn), lambda t: (0, t)),
                  pl.BlockSpec((None, kc, tn), lambda t: ((tile0 + t) // tps, 0, (tile0 + t) % tps))]
        + [pl.BlockSpec(memory_space=pl.ANY)] * len(extra),
        out_specs=pl.BlockSpec((m_rows, kc), lambda t: (0, 0)),
        out_shape=jax.ShapeDtypeStruct((m_rows, kc), F32),
        compiler_params=_params("arbitrary"),
    )(dy, w3, *extra)


def _conv_fwd(proj, conv_w, conv_b, ln_g, ln_b, name):
    tm = 256
    hb = tm // HALO

    def body(vg_ref, halo_ref, z_ref, w_ref, b_ref, g_ref, be_ref, u5_ref, u5t_ref, u2_ref, buf):
        i = pl.program_id(0)
        u1 = vg_ref[:, :D_MODEL] * _sigmoid(vg_ref[:, D_MODEL:])
        u1h = halo_ref[:, :D_MODEL] * _sigmoid(halo_ref[:, D_MODEL:])
        buf[pl.ds(0, HALO), :] = jnp.where(i > 0, u1h, 0.0)
        buf[pl.ds(HALO, tm), :] = u1
        acc = jnp.zeros((tm, D_MODEL), F32) + b_ref[...]
        for k in range(CONV_WIDTH):
            acc = acc + w_ref[k:k + 1, :] * buf[pl.ds(HALO - (CONV_WIDTH - 1) + k, tm), :]
        u2_ref[...] = acc
        mu = jnp.mean(acc, axis=-1, keepdims=True)
        xc = acc - mu
        rstd = lax.rsqrt(jnp.mean(xc * xc, axis=-1, keepdims=True) + NORM_EPS)
        u3 = xc * rstd * g_ref[...] + be_ref[...]
        zv = z_ref[...]
        u5 = u3 * _sigmoid(u3) * (zv * _sigmoid(zv))
        u5_ref[...] = u5.astype(BF16)
        u5t_ref[...] = u5.T.astype(BF16)

    return pl.pallas_call(
        body, name=name, grid=(SEQ // tm,),
        in_specs=[pl.BlockSpec((tm, 2 * D_MODEL), lambda i: (i, 0)),
                  pl.BlockSpec((HALO, 2 * D_MODEL), lambda i: (jnp.maximum(i * hb - 1, 0), 0)),
                  _row_spec(tm, D_MODEL, 2),
                  _vec_spec(CONV_WIDTH, D_MODEL)] + [_vec_spec(1, D_MODEL)] * 3,
        out_specs=[_row_spec(tm, D_MODEL), pl.BlockSpec((D_MODEL, tm), lambda i: (0, i)), _row_spec(tm, D_MODEL)],
        out_shape=[jax.ShapeDtypeStruct((SEQ, D_MODEL), BF16), jax.ShapeDtypeStruct((D_MODEL, SEQ), BF16),
                   jax.ShapeDtypeStruct((SEQ, D_MODEL), F32)],
        scratch_shapes=[pltpu.VMEM((HALO + tm, D_MODEL), F32)],
        compiler_params=_params("parallel"),
    )(proj, proj, proj, conv_w, conv_b, ln_g, ln_b)


def _conv_bwd_pointwise(du5, proj, u2, ln_g, ln_b, name):
    tm = 256

    def body(du5_ref, z_ref, u2_ref, g_ref, be_ref, du2_ref, dz_ref, sums_ref):
        u2v = u2_ref[...]
        mu = jnp.mean(u2v, axis=-1, keepdims=True)
        xc = u2v - mu
        rstd = lax.rsqrt(jnp.mean(xc * xc, axis=-1, keepdims=True) + NORM_EPS)
        xhat = xc * rstd
        u3 = xhat * g_ref[...] + be_ref[...]
        s3 = _sigmoid(u3)
        u4 = u3 * s3
        zv = z_ref[...]
        sz = _sigmoid(zv)
        du5v = du5_ref[...]
        dz_ref[...] = du5v * u4 * (sz * (1.0 + zv * (1.0 - sz)))
        du3 = du5v * (zv * sz) * (s3 * (1.0 + u3 * (1.0 - s3)))
        dxhat = du3 * g_ref[...]
        du2 = rstd * (dxhat - jnp.mean(dxhat, axis=-1, keepdims=True)
                      - xhat * jnp.mean(dxhat * xhat, axis=-1, keepdims=True))
        du2_ref[...] = du2
        sums = jnp.concatenate([
            jnp.sum(du3 * xhat, axis=0, keepdims=True),
            jnp.sum(du3, axis=0, keepdims=True),
            jnp.sum(du2, axis=0, keepdims=True),
            jnp.zeros((5, D_MODEL), F32)], axis=0)

        @pl.when(pl.program_id(0) == 0)
        def _():
            sums_ref[...] = jnp.zeros_like(sums_ref)

        sums_ref[...] += sums

    return pl.pallas_call(
        body, name=name, grid=(SEQ // tm,),
        in_specs=[_row_spec(tm, D_MODEL), _row_spec(tm, D_MODEL, 2), _row_spec(tm, D_MODEL),
                  _vec_spec(1, D_MODEL), _vec_spec(1, D_MODEL)],
        out_specs=[_row_spec(tm, D_MODEL), _row_spec(tm, D_MODEL), _vec_spec(8, D_MODEL)],
        out_shape=[jax.ShapeDtypeStruct((SEQ, D_MODEL), F32), jax.ShapeDtypeStruct((SEQ, D_MODEL), F32),
                   jax.ShapeDtypeStruct((8, D_MODEL), F32)],
        compiler_params=_params("arbitrary"),
    )(du5, proj, u2, ln_g, ln_b)


def _conv_bwd_taps(du2, dz, proj, conv_w, name):
    tm = 256
    hb = tm // HALO
    n_blocks = SEQ // tm

    def body(du2_ref, dnext_ref, dz_ref, vg_ref, halo_ref, w_ref, dproj_ref, dw_ref, ubuf, dbuf):
        i = pl.program_id(0)
        val = vg_ref[:, :D_MODEL]
        sg = _sigmoid(vg_ref[:, D_MODEL:])
        u1h = halo_ref[:, :D_MODEL] * _sigmoid(halo_ref[:, D_MODEL:])
        ubuf[pl.ds(0, HALO), :] = jnp.where(i > 0, u1h, 0.0)
        ubuf[pl.ds(HALO, tm), :] = val * sg
        du2v = du2_ref[...]
        dbuf[pl.ds(0, tm), :] = du2v
        dbuf[pl.ds(tm, HALO), :] = jnp.where(i < n_blocks - 1, dnext_ref[...], 0.0)

        @pl.when(i == 0)
        def _():
            dw_ref[...] = jnp.zeros_like(dw_ref)

        du1 = jnp.zeros((tm, D_MODEL), F32)
        for k in range(CONV_WIDTH):
            du1 = du1 + w_ref[k:k + 1, :] * dbuf[pl.ds(CONV_WIDTH - 1 - k, tm), :]
            dw_ref[k:k + 1, :] += jnp.sum(du2v * ubuf[pl.ds(HALO - (CONV_WIDTH - 1) + k, tm), :],
                                          axis=0, keepdims=True)
        dproj_ref[:, :D_MODEL] = (du1 * sg).astype(BF16)
        dproj_ref[:, D_MODEL:2 * D_MODEL] = (du1 * val * sg * (1.0 - sg)).astype(BF16)
        dproj_ref[:, 2 * D_MODEL:] = dz_ref[...].astype(BF16)

    return pl.pallas_call(
        body, name=name, grid=(n_blocks,),
        in_specs=[_row_spec(tm, D_MODEL),
                  pl.BlockSpec((HALO, D_MODEL), lambda i: (jnp.minimum((i + 1) * hb, SEQ // HALO - 1), 0)),
                  _row_spec(tm, D_MODEL),
                  pl.BlockSpec((tm, 2 * D_MODEL), lambda i: (i, 0)),
                  pl.BlockSpec((HALO, 2 * D_MODEL), lambda i: (jnp.maximum(i * hb - 1, 0), 0)),
                  _vec_spec(CONV_WIDTH, D_MODEL)],
        out_specs=[_row_spec(tm, 3 * D_MODEL), _vec_spec(32, D_MODEL)],
        out_shape=[jax.ShapeDtypeStruct((SEQ, 3 * D_MODEL), BF16), jax.ShapeDtypeStruct((32, D_MODEL), F32)],
        scratch_shapes=[pltpu.VMEM((HALO + tm, D_MODEL), F32), pltpu.VMEM((tm + HALO, D_MODEL), F32)],
        compiler_params=_params("arbitrary"),
    )(du2, du2, dz, proj, proj, conv_w)


def _out_a(u5, w_out, x, gate, g1, scale1, shift1, name):
    tm = 256

    def body(u_ref, w_ref, x_ref, gate_ref, g_ref, sc_ref, sh_ref, x1_ref, y_ref, h_ref, ht_ref):
        y = jnp.dot(u_ref[...], w_ref[...], preferred_element_type=F32)
        x1 = x_ref[...] + gate_ref[...] * y
        y_ref[...] = y
        x1_ref[...] = x1
        h = _normmod(x1, g_ref[...], sc_ref[...], sh_ref[...])
        h_ref[...] = h.astype(BF16)
        ht_ref[...] = h.T.astype(BF16)

    return pl.pallas_call(
        body, name=name, grid=(SEQ // tm,),
        in_specs=[_row_spec(tm, D_MODEL), _vec_spec(D_MODEL, D_MODEL), _row_spec(tm, D_MODEL)]
        + [_vec_spec(1, D_MODEL)] * 4,
        out_specs=[_row_spec(tm, D_MODEL), _row_spec(tm, D_MODEL), _row_spec(tm, D_MODEL),
                   pl.BlockSpec((D_MODEL, tm), lambda i: (0, i))],
        out_shape=[jax.ShapeDtypeStruct((SEQ, D_MODEL), F32), jax.ShapeDtypeStruct((SEQ, D_MODEL), F32),
                   jax.ShapeDtypeStruct((SEQ, D_MODEL), BF16), jax.ShapeDtypeStruct((D_MODEL, SEQ), BF16)],
        compiler_params=_params("parallel"),
    )(u5, w_out, x, gate, g1, scale1, shift1)


def _out_b_loss(u, w_out, x1, gate, target, name):
    tm = 256

    def body(u_ref, w_ref, x_ref, gate_ref, t_ref, e_ref, dy_ref, sums_ref):
        y = jnp.dot(u_ref[...], w_ref[...], preferred_element_type=F32)
        diff = x_ref[...] + gate_ref[...] * y - t_ref[...]
        e = diff * (1.0 / D_MODEL)
        e_ref[...] = e
        dy_ref[...] = (e * gate_ref[...]).astype(BF16)
        sums = jnp.concatenate([
            jnp.sum(e * y, axis=0, keepdims=True),
            jnp.sum(diff * diff, axis=0, keepdims=True),
            jnp.zeros((6, D_MODEL), F32)], axis=0)

        @pl.when(pl.program_id(0) == 0)
        def _():
            sums_ref[...] = jnp.zeros_like(sums_ref)

        sums_ref[...] += sums

    return pl.pallas_call(
        body, name=name, grid=(SEQ // tm,),
        in_specs=[_row_spec(tm, D_MODEL), _vec_spec(D_MODEL, D_MODEL), _row_spec(tm, D_MODEL),
                  _vec_spec(1, D_MODEL), _row_spec(tm, D_MODEL)],
        out_specs=[_row_spec(tm, D_MODEL), _row_spec(tm, D_MODEL), _vec_spec(8, D_MODEL)],
        out_shape=[jax.ShapeDtypeStruct((SEQ, D_MODEL), F32), jax.ShapeDtypeStruct((SEQ, D_MODEL), BF16),
                   jax.ShapeDtypeStruct((8, D_MODEL), F32)],
        compiler_params=_params("arbitrary"),
    )(u, w_out, x1, gate, target)


def _dgate_dy(dx1, y, gate, name):
    tm = 256

    def body(d_ref, y_ref, gate_ref, dy_ref, sums_ref):
        dv = d_ref[...]
        dy_ref[...] = (dv * gate_ref[...]).astype(BF16)
        sums = jnp.concatenate([jnp.sum(dv * y_ref[...], axis=0, keepdims=True), jnp.zeros((7, D_MODEL), F32)], axis=0)

        @pl.when(pl.program_id(0) == 0)
        def _():
            sums_ref[...] = jnp.zeros_like(sums_ref)

        sums_ref[...] += sums

    return pl.pallas_call(
        body, name=name, grid=(SEQ // tm,),
        in_specs=[_row_spec(tm, D_MODEL), _row_spec(tm, D_MODEL), _vec_spec(1, D_MODEL)],
        out_specs=[_row_spec(tm, D_MODEL), _vec_spec(8, D_MODEL)],
        out_shape=[jax.ShapeDtypeStruct((SEQ, D_MODEL), BF16), jax.ShapeDtypeStruct((8, D_MODEL), F32)],
        compiler_params=_params("arbitrary"),
    )(dx1, y, gate)


def _mm_nt_res(dy, w, name):
    tm = 256
    kc, n = w.shape

    def body(dy_ref, w_ref, o_ref):
        o_ref[...] = lax.dot_general(dy_ref[...], w_ref[...], (((1,), (1,)), ((), ())), preferred_element_type=F32)

    return pl.pallas_call(
        body, name=name, grid=(SEQ // tm,),
        in_specs=[_row_spec(tm, n), _vec_spec(kc, n)],
        out_specs=_row_spec(tm, kc),
        out_shape=jax.ShapeDtypeStruct((SEQ, kc), F32),
        compiler_params=_params("parallel"),
    )(dy, w)


def _seg_matrix():
    r = lax.broadcasted_iota(jnp.int32, (256, 256), 0) // HEAD_DIM
    c = lax.broadcasted_iota(jnp.int32, (256, 256), 1) // HEAD_DIM
    return (r == c).astype(BF16)


def _segsum(v, seg):
    hi = v.astype(BF16)
    lo = (v - hi.astype(F32)).astype(BF16)
    outs = []
    for c0 in range(0, D_MODEL, 256):
        outs.append(jnp.dot(hi[:, c0:c0 + 256], seg, preferred_element_type=F32)
                    + jnp.dot(lo[:, c0:c0 + 256], seg, preferred_element_type=F32))
    return jnp.concatenate(outs, axis=1)


def _qk_rstd(v, seg):
    return lax.rsqrt(_segsum(v * v, seg) * (1.0 / HEAD_DIM) + NORM_EPS)


def _qknorm_fwd(proj, qw, kw, seg, name):
    tm = 256

    def body(p_ref, qw_ref, kw_ref, seg_ref, q_ref, k_ref, v_ref):
        segv = seg_ref[...]
        q = p_ref[:, :D_MODEL]
        k = p_ref[:, D_MODEL:2 * D_MODEL]
        q_ref[...] = (q * _qk_rstd(q, segv) * qw_ref[...]).astype(BF16)
        k_ref[...] = (k * _qk_rstd(k, segv) * kw_ref[...]).astype(BF16)
        v_ref[...] = p_ref[:, 2 * D_MODEL:].astype(BF16)

    return pl.pallas_call(
        body, name=name, grid=(SEQ // tm,),
        in_specs=[_row_spec(tm, 3 * D_MODEL), _vec_spec(1, D_MODEL), _vec_spec(1, D_MODEL), _vec_spec(256, 256)],
        out_specs=[_row_spec(tm, D_MODEL)] * 3,
        out_shape=[jax.ShapeDtypeStruct((SEQ, D_MODEL), BF16)] * 3,
        compiler_params=_params("parallel"),
    )(proj, qw, kw, seg)


def _attn_masks(b, bpc, dilation, slope):
    qi = lax.broadcasted_iota(jnp.int32, (ATTN_BLOCK, ATTN_BLOCK), 0)
    kj = lax.broadcasted_iota(jnp.int32, (ATTN_BLOCK, ATTN_BLOCK), 1)
    steps_c = qi - kj
    steps_p = steps_c + ATTN_BLOCK
    valid_c = steps_c >= 0
    valid_p = jnp.logical_and(steps_p <= ATTN_BLOCK, (b % bpc) != 0)
    bias_c = (steps_c * dilation).astype(F32) * slope
    bias_p = (steps_p * dilation).astype(F32) * slope
    return bias_p, valid_p, bias_c, valid_c


def _attn_specs(n):
    cur = pl.BlockSpec((ATTN_BLOCK, 2 * HEAD_DIM), lambda hp, b: (b, hp))
    prev = pl.BlockSpec((ATTN_BLOCK, 2 * HEAD_DIM), lambda hp, b: (jnp.maximum(b - 1, 0), hp))
    return cur, prev


def _attn_fwd(q, k, v, slopes, dilation, name):
    bpc = SEQ // dilation // ATTN_BLOCK
    cur, prev = _attn_specs(2)
    scale = HEAD_DIM ** -0.5

    def body(sl_ref, q_ref, kp_ref, kc_ref, vp_ref, vc_ref, o_ref, lse_ref):
        hp = pl.program_id(0)
        b = pl.program_id(1)
        for h in range(2):
            cols = slice(h * HEAD_DIM, (h + 1) * HEAD_DIM)
            bias_p, valid_p, bias_c, valid_c = _attn_masks(b, bpc, dilation, sl_ref[2 * hp + h])
            qh = q_ref[:, cols]
            nt = (((1,), (1,)), ((), ()))
            s_p = lax.dot_general(qh, kp_ref[:, cols], nt, preferred_element_type=F32) * scale
            s_c = lax.dot_general(qh, kc_ref[:, cols], nt, preferred_element_type=F32) * scale
            s_p = jnp.where(valid_p, s_p - bias_p, NEG_INF)
            s_c = jnp.where(valid_c, s_c - bias_c, NEG_INF)
            m = jnp.maximum(jnp.max(s_p, axis=-1, keepdims=True), jnp.max(s_c, axis=-1, keepdims=True))
            p_p = jnp.exp(s_p - m)
            p_c = jnp.exp(s_c - m)
            l = jnp.sum(p_p, axis=-1, keepdims=True) + jnp.sum(p_c, axis=-1, keepdims=True)
            acc = (jnp.dot(p_p.astype(BF16), vp_ref[:, cols], preferred_element_type=F32)
                   + jnp.dot(p_c.astype(BF16), vc_ref[:, cols], preferred_element_type=F32))
            o_ref[:, cols] = acc / l
            lse_ref[:, cols] = jnp.broadcast_to(m + jnp.log(l), (ATTN_BLOCK, HEAD_DIM))

    return pl.pallas_call(
        body, name=name, grid=(N_HEADS // 2, SEQ // ATTN_BLOCK),
        in_specs=[pl.BlockSpec(memory_space=pltpu.SMEM), cur, prev, cur, prev, cur],
        out_specs=[cur, cur],
        out_shape=[jax.ShapeDtypeStruct((SEQ, D_MODEL), F32)] * 2,
        compiler_params=_params("parallel", "parallel"),
    )(slopes, q, k, k, v, v)


def _merge_fwd(o_parts, lse_parts, z, name):
    tm = 256

    def body(o0, o1, o2, l0, l1, l2, z_ref, u_ref, ut_ref, o_ref, lse_ref):
        ls = [l0[...], l1[...], l2[...]]
        m = jnp.maximum(jnp.maximum(ls[0], ls[1]), ls[2])
        tot = m + jnp.log(jnp.exp(ls[0] - m) + jnp.exp(ls[1] - m) + jnp.exp(ls[2] - m))
        o = (jnp.exp(ls[0] - tot) * o0[...] + jnp.exp(ls[1] - tot) * o1[...] + jnp.exp(ls[2] - tot) * o2[...])
        zv = z_ref[...]
        u = o * (zv * _sigmoid(zv))
        u_ref[...] = u.astype(BF16)
        ut_ref[...] = u.T.astype(BF16)
        o_ref[...] = o
        lse_ref[...] = tot

    return pl.pallas_call(
        body, name=name, grid=(SEQ // tm,),
        in_specs=[_row_spec(tm, D_MODEL)] * 7,
        out_specs=[_row_spec(tm, D_MODEL), pl.BlockSpec((D_MODEL, tm), lambda i: (0, i)),
                   _row_spec(tm, D_MODEL), _row_spec(tm, D_MODEL)],
        out_shape=[jax.ShapeDtypeStruct((SEQ, D_MODEL), BF16), jax.ShapeDtypeStruct((D_MODEL, SEQ), BF16),
                   jax.ShapeDtypeStruct((SEQ, D_MODEL), F32), jax.ShapeDtypeStruct((SEQ, D_MODEL), F32)],
        compiler_params=_params("parallel"),
    )(*o_parts, *lse_parts, z)


def _merge_bwd(du, o, z, seg, name):
    tm = 256

    def body(du_ref, o_ref, z_ref, seg_ref, do_ref, dz_ref, delta_ref):
        zv = z_ref[...]
        sz = _sigmoid(zv)
        duv = du_ref[...]
        ov = o_ref[...]
        do = duv * (zv * sz)
        do_ref[...] = do.astype(BF16)
        dz_ref[...] = (duv * ov * (sz * (1.0 + zv * (1.0 - sz)))).astype(BF16)
        delta_ref[...] = _segsum(do * ov, seg_ref[...])

    return pl.pallas_call(
        body, name=name, grid=(SEQ // tm,),
        in_specs=[_row_spec(tm, D_MODEL)] * 3 + [_vec_spec(256, 256)],
        out_specs=[_row_spec(tm, D_MODEL)] * 3,
        out_shape=[jax.ShapeDtypeStruct((SEQ, D_MODEL), BF16), jax.ShapeDtypeStruct((SEQ, D_MODEL), BF16),
                   jax.ShapeDtypeStruct((SEQ, D_MODEL), F32)],
        compiler_params=_params("parallel"),
    )(du, o, z, seg)


def _attn_bwd(q, k, v, do, lse, delta, slopes, dilation, name):
    bpc = SEQ // dilation // ATTN_BLOCK
    cur, prev = _attn_specs(2)
    scale = HEAD_DIM ** -0.5

    def body(sl_ref, q_ref, kp_ref, kc_ref, vp_ref, vc_ref, do_ref, lse_ref, dl_ref,
             dq_ref, dkc_ref, dkp_ref, dvc_ref, dvp_ref):
        hp = pl.program_id(0)
        b = pl.program_id(1)
        nt = (((1,), (1,)), ((), ()))
        tn = (((0,), (0,)), ((), ()))
        for h in range(2):
            cols = slice(h * HEAD_DIM, (h + 1) * HEAD_DIM)
            bias_p, valid_p, bias_c, valid_c = _attn_masks(b, bpc, dilation, sl_ref[2 * hp + h])
            qh = q_ref[:, cols]
            doh = do_ref[:, cols]
            lse_col = lse_ref[:, h * HEAD_DIM:h * HEAD_DIM + 1]
            dl_col = dl_ref[:, h * HEAD_DIM:h * HEAD_DIM + 1]
            dq = jnp.zeros((ATTN_BLOCK, HEAD_DIM), F32)
            for k_ref, v_ref, bias, valid, dk_ref, dv_ref in (
                    (kp_ref, vp_ref, bias_p, valid_p, dkp_ref, dvp_ref),
                    (kc_ref, vc_ref, bias_c, valid_c, dkc_ref, dvc_ref)):
                kh = k_ref[:, cols]
                s = lax.dot_general(qh, kh, nt, preferred_element_type=F32) * scale
                p = jnp.exp(jnp.where(valid, s - bias, NEG_INF) - lse_col)
                dp = lax.dot_general(doh, v_ref[:, cols], nt, preferred_element_type=F32)
                ds = (p * (dp - dl_col) * scale).astype(BF16)
                dq = dq + jnp.dot(ds, kh, preferred_element_type=F32)
                dk_ref[:, cols] = lax.dot_general(ds, qh, tn, preferred_element_type=F32)
                dv_ref[:, cols] = lax.dot_general(p.astype(BF16), doh, tn, preferred_element_type=F32)
            dq_ref[:, cols] = dq

    return pl.pallas_call(
        body, name=name, grid=(N_HEADS // 2, SEQ // ATTN_BLOCK),
        in_specs=[pl.BlockSpec(memory_space=pltpu.SMEM), cur, prev, cur, prev, cur, cur, cur, cur],
        out_specs=[cur] * 5,
        out_shape=[jax.ShapeDtypeStruct((SEQ, D_MODEL), F32)] * 5,
        compiler_params=_params("parallel", "parallel"),
    )(slopes, q, k, k, v, v, do, lse, delta)


def _qknorm_bwd(proj, qw, kw, seg, dq, dkc, dkp, dvc, dvp, name):
    tm = ATTN_BLOCK
    n_blocks = SEQ // tm
    nxt = pl.BlockSpec((tm, D_MODEL), lambda i: (jnp.minimum(i + 1, n_blocks - 1), 0))

    def body(p_ref, qw_ref, kw_ref, seg_ref, dq_ref, dkc_ref, dkp_ref, dvc_ref, dvp_ref, dproj_ref, sums_ref):
        i = pl.program_id(0)
        segv = seg_ref[...]
        has_next = i < n_blocks - 1
        dk = dkc_ref[...] + jnp.where(has_next, dkp_ref[...], 0.0)
        dv = dvc_ref[...] + jnp.where(has_next, dvp_ref[...], 0.0)
        sums = []
        for part, (raw, w, dn) in enumerate(((p_ref[:, :D_MODEL], qw_ref[...], dq_ref[...]),
                                             (p_ref[:, D_MODEL:2 * D_MODEL], kw_ref[...], dk))):
            r = _qk_rstd(raw, segv)
            gq = dn * w
            draw = r * gq - raw * (r * r * r) * (_segsum(raw * gq, segv) * (1.0 / HEAD_DIM))
            dproj_ref[:, part * D_MODEL:(part + 1) * D_MODEL] = draw.astype(BF16)
            sums.append(jnp.sum(dn * raw * r, axis=0, keepdims=True))
        dproj_ref[:, 2 * D_MODEL:] = dv.astype(BF16)

        @pl.when(i == 0)
        def _():
            sums_ref[...] = jnp.zeros_like(sums_ref)

        sums_ref[...] += jnp.concatenate(sums + [jnp.zeros((6, D_MODEL), F32)], axis=0)

    return pl.pallas_call(
        body, name=name, grid=(n_blocks,),
        in_specs=[_row_spec(tm, 3 * D_MODEL), _vec_spec(1, D_MODEL), _vec_spec(1, D_MODEL), _vec_spec(256, 256),
                  _row_spec(tm, D_MODEL), _row_spec(tm, D_MODEL), nxt, _row_spec(tm, D_MODEL), nxt],
        out_specs=[_row_spec(tm, 3 * D_MODEL), _vec_spec(8, D_MODEL)],
        out_shape=[jax.ShapeDtypeStruct((SEQ, 3 * D_MODEL), BF16), jax.ShapeDtypeStruct((8, D_MODEL), F32)],
        compiler_params=_params("arbitrary"),
    )(proj, qw, kw, seg, dq, dkc, dkp, dvc, dvp)


def _to_classes(a, dilation):
    if dilation == 1:
        return a
    s, c = a.shape
    return a.reshape(s // dilation, dilation, c).transpose(1, 0, 2).reshape(s, c)


def _from_classes(a, dilation):
    if dilation == 1:
        return a
    s, c = a.shape
    return a.reshape(dilation, s // dilation, c).transpose(1, 0, 2).reshape(s, c)


def _cols_to_classes(a, dilation):
    if dilation == 1:
        return a
    r, s = a.shape
    return a.reshape(r, s // dilation, dilation).transpose(0, 2, 1).reshape(r, s)


B_TN = 512
B_GROUP_TILES = 3 * D_MODEL // B_TN
B_Z_TILE0 = 3 * B_GROUP_TILES
B_Z_TILES = D_MODEL // B_TN


def _local_step(x, target, mods, norm_g, conv_w, conv_b, ln_g, ln_b, q_norm, k_norm,
                weights_a, weights_b, send_grads_b, send_grads_a):
    row = lambda a, i: a[i:i + 1]
    shift0, scale0, gate0 = row(mods[0], 0), row(mods[0], 1), row(mods[0], 2)
    shift1, scale1, gate1 = row(mods[1], 0), row(mods[1], 1), row(mods[1], 2)
    g0, g1 = row(norm_g, 0), row(norm_g, 1)
    seg = _seg_matrix()
    slopes = jnp.exp2(-8.0 * jnp.arange(1, N_HEADS + 1, dtype=F32) / N_HEADS)
    qw = [jnp.tile(q_norm[g:g + 1], (1, N_HEADS)) for g in range(3)]
    kw = [jnp.tile(k_norm[g:g + 1], (1, N_HEADS)) for g in range(3)]

    h0, h0t = _normmod_fwd(x, g0, scale0, shift0, "prenorm0")
    wa_in, wa_out = weights_a(h0)
    ja, _, nsa = wa_in.shape
    proj_a = _mm(h0, wa_in, tn=nsa, tile0=0, n_tiles=ja, out_dtype=F32, name="a_in")
    u5, u5t, u2 = _conv_fwd(proj_a, conv_w, conv_b, ln_g, ln_b, "a_conv")
    x1, y_a, h1, h1t = _out_a(u5, wa_out, x, gate0, g1, scale1, shift1, "a_out")

    wb_in, wb_out = weights_b(x1)
    jb, _, nsb = wb_in.shape
    h1c =[_to_classes(h1, d) for d in DILATIONS]
    h1tc = [_cols_to_classes(h1t, d) for d in DILATIONS]
    z_b = _mm(h1, wb_in, tn=B_TN, tile0=B_Z_TILE0, n_tiles=B_Z_TILES, out_dtype=F32, name="b_in_z")
    proj_g, qkv, o_parts, lse_parts = [], [], [], []
    for g, d in enumerate(DILATIONS):
        pg = _mm(h1c[g], wb_in, tn=B_TN, tile0=g * B_GROUP_TILES, n_tiles=B_GROUP_TILES, out_dtype=F32,
                 name=f"b_in_g{g}")
        qn, kn, vn = _qknorm_fwd(pg, qw[g], kw[g], seg, f"b_qknorm_g{g}")
        og, lg = _attn_fwd(qn, kn, vn, slopes, d, f"b_attn_g{g}")
        proj_g.append(pg)
        qkv.append((qn, kn, vn))
        o_parts.append(_from_classes(og, d))
        lse_parts.append(_from_classes(lg, d))
    u_b, u_bt, o_b, lse_b = _merge_fwd(o_parts, lse_parts, z_b, "b_merge")
    e, dy_b, sums_loss = _out_b_loss(u_b, wb_out, x1, gate1, target, "b_out_loss")

    dwb_out = _mm(u_bt, dy_b, tn=D_MODEL, tile0=0, n_tiles=1, out_dtype=BF16, name="b_dwout")
    du_b = _mm_nt_res(dy_b, wb_out, "b_dout")
    do_b, dz_b, delta_b = _merge_bwd(du_b, o_b, z_b, seg, "b_merge_bwd")
    dwb_in = _mm(h1t, dz_b, tn=B_TN, tile0=B_Z_TILE0, n_tiles=B_Z_TILES, out_dtype=BF16, name="b_dwin_z",
                 out3d=(jb, nsb))
    dh1_parts = [_mm_nt(dz_b, wb_in, tn=B_TN, tile0=B_Z_TILE0, n_tiles=B_Z_TILES, name="b_dh_z")]
    qk_sums = []
    for g, d in enumerate(DILATIONS):
        qn, kn, vn = qkv[g]
        dq, dkc, dkp, dvc, dvp = _attn_bwd(qn, kn, vn, _to_classes(do_b, d), _to_classes(lse_b, d),
                                           _to_classes(delta_b, d), slopes, d, f"b_attn_bwd_g{g}")
        dproj, sums_qk = _qknorm_bwd(proj_g[g], qw[g], kw[g], seg, dq, dkc, dkp, dvc, dvp, f"b_qknorm_bwd_g{g}")
        qk_sums.append(sums_qk)
        dwb_in = _mm(h1tc[g], dproj, tn=B_TN, tile0=g * B_GROUP_TILES, n_tiles=B_GROUP_TILES, out_dtype=BF16,
                     name=f"b_dwin_g{g}", out3d=(jb, nsb), prev=dwb_in)
        dh = _mm_nt(dproj, wb_in, tn=B_TN, tile0=g * B_GROUP_TILES, n_tiles=B_GROUP_TILES, name=f"b_dh_g{g}")
        dh1_parts.append(_from_classes(dh, d))
    token = send_grads_b(dwb_in, dwb_out)
    dx1, sums_n1 = _normmod_bwd(x1, g1, scale1 + token[0:1, 0:1], dh1_parts, e, "prenorm1_bwd")

    dy_a, sums_ga = _dgate_dy(dx1, y_a, gate0, "a_dgate")
    dwa_out = _mm(u5t, dy_a, tn=D_MODEL, tile0=0, n_tiles=1, out_dtype=BF16, name="a_dwout")
    du5 = _mm_nt_res(dy_a, wa_out, "a_dout")
    du2, dz_a, sums_ln = _conv_bwd_pointwise(du5, proj_a, u2, ln_g, ln_b, "a_conv_bwd_pw")
    dproj_a, dconv_w = _conv_bwd_taps(du2, dz_a, proj_a, conv_w, "a_conv_bwd_taps")
    dwa_in = _mm(h0t, dproj_a, tn=nsa, tile0=0, n_tiles=ja, out_dtype=BF16, name="a_dwin", out3d=(ja, nsa))
    token = send_grads_a(dwa_in, dwa_out)
    dh0 = _mm_nt(dproj_a, wa_in, tn=nsa, tile0=0, n_tiles=ja, name="a_dh", after=token)
    grad_x, sums_n0 = _normmod_bwd(x, g0, scale0, [dh0], dx1, "prenorm0_bwd")

    small = dict(
        dnorm_g=jnp.concatenate([sums_n0[0:1], sums_n1[0:1]], axis=0),
        dmod0=jnp.concatenate([sums_n0[2:3], sums_n0[1:2], sums_ga[0:1]], axis=0),
        dmod1=jnp.concatenate([sums_n1[2:3], sums_n1[1:2], sums_loss[0:1]], axis=0),
        dln_g=sums_ln[0:1], dln_b=sums_ln[1:2], dconv_b=sums_ln[2:3],
        dconv_w=dconv_w[:CONV_WIDTH],
        dq_norm=jnp.concatenate([s[0:1] for s in qk_sums], axis=0),
        dk_norm=jnp.concatenate([s[1:2] for s in qk_sums], axis=0),
        loss_cols=sums_loss[1:2],
    )
    return grad_x, small


def _adamw(w, g, m, v, name):
    rows, cols = w.shape
    tr = rows if rows <= 128 else 128
    c1 = 1.0 / (1.0 - ADAM_B1 ** ADAM_STEP)
    c2 = 1.0 / (1.0 - ADAM_B2 ** ADAM_STEP)

    def body(w_ref, g_ref, m_ref, v_ref, d_ref, mo_ref, vo_ref):
        gv = g_ref[...]
        mn = ADAM_B1 * m_ref[...] + (1.0 - ADAM_B1) * gv
        vn = ADAM_B2 * v_ref[...] + (1.0 - ADAM_B2) * (gv * gv)
        mo_ref[...] = mn
        vo_ref[...] = vn
        d_ref[...] = -ADAM_LR * ((mn * c1) / (jnp.sqrt(vn * c2) + ADAM_EPS) + ADAM_WD * w_ref[...])

    spec = pl.BlockSpec((tr, cols), lambda i: (i, 0))
    return pl.pallas_call(
        body, name=name, grid=(rows // tr,), in_specs=[spec] * 4, out_specs=[spec] * 3,
        out_shape=[jax.ShapeDtypeStruct((rows, cols), F32)] * 3,
        compiler_params=_params("parallel"),
    )(w, g, m, v)


def _cast_into_slot(w, chip_idx, name):
    rows, cols = w.shape
    tr = 256

    def body(ch_ref, w_ref, o_ref):
        o_ref[...] = w_ref[...].astype(BF16)

    return pl.pallas_call(
        body, name=name,
        grid_spec=pltpu.PrefetchScalarGridSpec(
            num_scalar_prefetch=1, grid=(rows // tr,),
            in_specs=[pl.BlockSpec((tr, cols), lambda i, ch: (i, 0))],
            out_specs=pl.BlockSpec((None, tr, cols), lambda i, ch: (ch[0], i, 0))),
        out_shape=jax.ShapeDtypeStruct((N_CHIPS, rows, cols), BF16), compiler_params=_params("parallel"),
    )(chip_idx, w)


def _position():
    x, y, c = lax.axis_index("x"), lax.axis_index("y"), lax.axis_index("c")
    return x, y, c


def _xor_peer(x, y, c, k):
    return (x ^ ((k >> 2) & 1), y ^ ((k >> 1) & 1), c ^ (k & 1))


def _chip_peer(x, y, k):
    return (x ^ ((k >> 1) & 1), y ^ (k & 1))


def _ada_forward(c_row, ada_w, ada_b, conv_w):
    ns = ada_w.shape[2]
    cw = conv_w.shape[1]

    def body(c_ref, w_ref, b_ref, cv_ref, mod_ref, sc_ref, cvo_ref,
             c_all, mp, parts, cv_parts, send1, recv1, send2, recv2, send3, recv3):
        x, y, c = _position()
        me = 4 * x + 2 * y + c
        chip = 2 * x + y

        def c_copy(k):
            return pltpu.make_async_remote_copy(
                src_ref=c_all.at[me], dst_ref=c_all.at[me], send_sem=send1.at[k - 1], recv_sem=recv1.at[k - 1],
                device_id=_xor_peer(x, y, c, k), device_id_type=MESH)

        def cv_copy(k):
            px, py = _chip_peer(x, y, k)
            return pltpu.make_async_remote_copy(
                src_ref=cv_parts.at[chip], dst_ref=cv_parts.at[chip], send_sem=send3.at[k - 1],
                recv_sem=recv3.at[k - 1], device_id=(px, py, c), device_id_type=MESH)

        c_all[me] = c_ref[...]
        cv_parts[chip] = cv_ref[...]
        for k in range(1, N_DEV):
            c_copy(k).start()
        for k in range(1, N_CHIPS):
            cv_copy(k).start()
        for k in range(1, N_DEV):
            c_copy(k).wait_recv()
        cv = jnp.concatenate([c_all[i] for i in range(N_DEV)], axis=0)
        sc = cv * _sigmoid(cv)
        sc_ref[...] = sc
        for l in range(2):
            res = jnp.dot(sc, w_ref[l], preferred_element_type=F32, precision=lax.Precision.HIGHEST)
            for i in range(N_DEV):
                mp[i, l:l + 1, :] = res[i:i + 1, :]

        def mod_copy(k):
            px, py = _chip_peer(x, y, k)
            return pltpu.make_async_remote_copy(
                src_ref=mp.at[4 * px + 2 * py + c], dst_ref=parts.at[chip], send_sem=send2.at[k - 1],
                recv_sem=recv2.at[k - 1], device_id=(px, py, c), device_id_type=MESH)

        for k in range(1, N_CHIPS):
            mod_copy(k).start()
        parts[chip] = mp[me]
        for k in range(1, N_CHIPS):
            mod_copy(k).wait_recv()
            cv_copy(k).wait_recv()
        mod_ref[...] = jnp.concatenate([parts[j] for j in range(N_CHIPS)], axis=1) + b_ref[...]
        cvo_ref[...] = jnp.concatenate([cv_parts[j] for j in range(N_CHIPS)], axis=1)
        for k in range(1, N_DEV):
            c_copy(k).wait_send()
        for k in range(1, N_CHIPS):
            mod_copy(k).wait_send()
            cv_copy(k).wait_send()

    vm = pl.BlockSpec(memory_space=pltpu.VMEM)
    return pl.pallas_call(
        body, name="ada_forward",
        in_specs=[vm] * 4, out_specs=[vm] * 3,
        out_shape=[jax.ShapeDtypeStruct((2, 3 * D_MODEL), F32), jax.ShapeDtypeStruct((N_DEV, D_MODEL), F32),
                   jax.ShapeDtypeStruct((CONV_WIDTH, N_CHIPS * cw), F32)],
        scratch_shapes=[pltpu.VMEM((N_DEV, 1, D_MODEL), F32), pltpu.VMEM((N_DEV, 2, ns), F32),
                        pltpu.VMEM((N_CHIPS, 2, ns), F32), pltpu.VMEM((N_CHIPS, CONV_WIDTH, cw), F32),
                        pltpu.SemaphoreType.DMA((N_DEV - 1,)), pltpu.SemaphoreType.DMA((N_DEV - 1,)),
                        pltpu.SemaphoreType.DMA((N_CHIPS - 1,)), pltpu.SemaphoreType.DMA((N_CHIPS - 1,)),
                        pltpu.SemaphoreType.DMA((N_CHIPS - 1,)), pltpu.SemaphoreType.DMA((N_CHIPS - 1,))],
        compiler_params=pltpu.CompilerParams(vmem_limit_bytes=VMEM_LIMIT_BYTES),
    )(c_row, ada_w, ada_b, conv_w)


HBM_SPEC = pl.BlockSpec(memory_space=pltpu.HBM)
ANY_SPEC = pl.BlockSpec(memory_space=pl.ANY)
SEM_SPEC = pl.BlockSpec(memory_space=pltpu.SEMAPHORE)
SPLIT_PARAMS = dict(compiler_params=pltpu.CompilerParams(has_side_effects=pltpu.SideEffectType.DATAFLOW_SIDE_EFFECTING))
TOKEN = jax.ShapeDtypeStruct((8, 128), F32)


def _hbm(arrays):
    return [pltpu.with_memory_space_constraint(a, pltpu.HBM) for a in arrays]


def _hbm_like(arrays):
    return [pltpu.HBM(a.shape, a.dtype) for a in arrays]


def _gather_start(lands, after, name):
    n = len(lands)

    def body(*refs):
        ins = refs[:n]
        send, recv = refs[n + 1], refs[n + 2]
        x, y, c = _position()
        chip = 2 * x + y
        for t in range(n):
            for k in range(1, N_CHIPS):
                px, py = _chip_peer(x, y, k)
                block = ins[t].at[chip]
                pltpu.make_async_remote_copy(
                    src_ref=block, dst_ref=block, send_sem=send.at[3 * t + k - 1], recv_sem=recv.at[3 * t + k - 1],
                    device_id=(px, py, c), device_id_type=MESH).start()
        refs[-1][...] = jnp.zeros(TOKEN.shape, F32)

    res = pl.pallas_call(
        body, name=name, in_specs=[HBM_SPEC] * n + [ANY_SPEC],
        out_specs=(SEM_SPEC, SEM_SPEC, *[HBM_SPEC] * n, pl.BlockSpec(memory_space=pltpu.VMEM)),
        out_shape=(pltpu.SemaphoreType.DMA((3 * n,)), pltpu.SemaphoreType.DMA((3 * n,)), *_hbm_like(lands), TOKEN),
        input_output_aliases={t: 2 + t for t in range(n)}, **SPLIT_PARAMS,
    )(*_hbm(lands), after)
    return res[0], res[1], list(res[2:2 + n]), res[-1]


def _gather_wait(send, recv, lands, after, name):
    n = len(lands)

    def body(*refs):
        ins = refs[:n]
        send_ref, recv_ref = refs[n], refs[n + 1]
        x, y, c = _position()
        chip = 2 * x + y
        for t in range(n):
            for k in range(1, N_CHIPS):
                px, py = _chip_peer(x, y, k)
                cp = pltpu.make_async_remote_copy(
                    src_ref=ins[t].at[chip], dst_ref=ins[t].at[2 * px + py], send_sem=send_ref.at[3 * t + k - 1],
                    recv_sem=recv_ref.at[3 * t + k - 1], device_id=(px, py, c), device_id_type=MESH)
                cp.wait_send()
                cp.wait_recv()

    res = pl.pallas_call(
        body, name=name, in_specs=[HBM_SPEC] * n + [SEM_SPEC, SEM_SPEC, ANY_SPEC], out_specs=[HBM_SPEC] * n,
        out_shape=_hbm_like(lands), input_output_aliases={t: t for t in range(n)}, **SPLIT_PARAMS,
    )(*lands, send, recv, after)
    return list(res)


def _reduce_start(grads, after, name):
    n = len(grads)
    lands = [lax.empty((N_DEV, g.shape[1] // 2, g.shape[2]), BF16) for g in grads]

    def body(*refs):
        gs, ls = refs[:n], refs[n:2 * n]
        send, recv = refs[2 * n + 1], refs[2 * n + 2]
        x, y, c = _position()
        me = 4 * x + 2 * y + c
        for t in range(n):
            rh = gs[t].shape[1] // 2
            for k in range(1, N_DEV):
                px, py, pc = _xor_peer(x, y, c, k)
                pltpu.make_async_remote_copy(
                    src_ref=gs[t].at[2 * px + py, pl.ds(pc * rh, rh)], dst_ref=ls[t].at[me],
                    send_sem=send.at[7 * t + k - 1], recv_sem=recv.at[7 * t + k - 1],
                    device_id=(px, py, pc), device_id_type=MESH).start()
        refs[-1][...] = jnp.zeros(TOKEN.shape, F32)

    res = pl.pallas_call(
        body, name=name, in_specs=[HBM_SPEC] * (2 * n) + [ANY_SPEC],
        out_specs=(SEM_SPEC, SEM_SPEC, *[HBM_SPEC] * (2 * n), pl.BlockSpec(memory_space=pltpu.VMEM)),
        out_shape=(pltpu.SemaphoreType.DMA((7 * n,)), pltpu.SemaphoreType.DMA((7 * n,)),
                   *_hbm_like(grads), *_hbm_like(lands), TOKEN),
        input_output_aliases={t: 2 + t for t in range(2 * n)}, **SPLIT_PARAMS,
    )(*_hbm(grads), *_hbm(lands), after)
    return res[0], res[1], list(res[2:2 + n]), list(res[2 + n:2 + 2 * n]), res[-1]


def _reduce_wait(send, recv, grads, lands, after, name):
    n = len(grads)

    def body(*refs):
        gs, ls = refs[:n], refs[n:2 * n]
        send_ref, recv_ref = refs[2 * n], refs[2 * n + 1]
        x, y, c = _position()
        for t in range(n):
            rh = gs[t].shape[1] // 2
            for k in range(1, N_DEV):
                px, py, pc = _xor_peer(x, y, c, k)
                cp = pltpu.make_async_remote_copy(
                    src_ref=gs[t].at[2 * px + py, pl.ds(pc * rh, rh)], dst_ref=ls[t].at[4 * px + 2 * py + pc],
                    send_sem=send_ref.at[7 * t + k - 1], recv_sem=recv_ref.at[7 * t + k - 1],
                    device_id=(px, py, pc), device_id_type=MESH)
                cp.wait_send()
                cp.wait_recv()

    res = pl.pallas_call(
        body, name=name, in_specs=[HBM_SPEC] * (2 * n) + [SEM_SPEC, SEM_SPEC, ANY_SPEC], out_specs=[HBM_SPEC] * (2 * n),
        out_shape=_hbm_like(grads) + _hbm_like(lands), input_output_aliases={t: t for t in range(2 * n)}, **SPLIT_PARAMS,
    )(*grads, *lands, send, recv, after)
    return list(res[:n]), list(res[n:])


def _sum_devices(land, grad, dev_idx, name):
    _, rh, cols = land.shape
    tr = 128
    nb = rh // tr

    def body(idx_ref, l_ref, g_ref, o_ref):
        me = idx_ref[0]
        acc = jnp.where(me == 0, g_ref[...], l_ref[0]).astype(F32)
        for d in range(1, N_DEV):
            acc = acc + jnp.where(me == d, g_ref[...], l_ref[d]).astype(F32)
        o_ref[...] = acc

    return pl.pallas_call(
        body, name=name,
        grid_spec=pltpu.PrefetchScalarGridSpec(
            num_scalar_prefetch=1, grid=(nb,),
            in_specs=[pl.BlockSpec((N_DEV, tr, cols), lambda i, idx: (0, i, 0)),
                      pl.BlockSpec((None, tr, cols), lambda i, idx: (idx[1], idx[2] * nb + i, 0))],
            out_specs=pl.BlockSpec((tr, cols), lambda i, idx: (idx[2] * nb + i, 0))),
        out_shape=jax.ShapeDtypeStruct((2 * rh, cols), F32), compiler_params=_params("parallel"),
    )(dev_idx, land, grad)


def _share_halves(totals):
    n = len(totals)

    def body(*refs):
        ins, outs = refs[:n], refs[n:2 * n]
        send, recv = refs[2 * n:]
        x, y, c = _position()
        cps = []
        for t in range(n):
            rh = ins[t].shape[0] // 2
            mine = pl.ds(c * rh, rh)
            cp = pltpu.make_async_remote_copy(
                src_ref=ins[t].at[mine], dst_ref=outs[t].at[mine], send_sem=send.at[t], recv_sem=recv.at[t],
                device_id=(x, y, 1 - c), device_id_type=MESH)
            cp.start()
            cps.append(cp)
        for cp in cps:
            cp.wait()

    return pl.pallas_call(
        body, name="reduce_share_" + "_".join(str(t.shape[1]) for t in totals), in_specs=[ANY_SPEC] * n,
        out_specs=[ANY_SPEC] * n, out_shape=[jax.ShapeDtypeStruct(t.shape, F32) for t in totals],
        input_output_aliases={t: t for t in range(n)},
        scratch_shapes=[pltpu.SemaphoreType.DMA((n,)), pltpu.SemaphoreType.DMA((n,))],
    )(*totals)


def _exchange_halves(grads):
    n = len(grads)
    hbm = pl.BlockSpec(memory_space=pl.ANY)

    def body(*refs):
        ins, outs = refs[:n], refs[n:2 * n]
        send, recv = refs[2 * n:]
        x, y, c = _position()
        cps = []
        for t in range(n):
            rh = ins[t].shape[1] // 2
            cp = pltpu.make_async_remote_copy(
                src_ref=ins[t].at[pl.ds(0, N_CHIPS), pl.ds((1 - c) * rh, rh)], dst_ref=outs[t], send_sem=send.at[t],
                recv_sem=recv.at[t], device_id=(x, y, 1 - c), device_id_type=MESH)
            cp.start()
            cps.append(cp)
        for cp in cps:
            cp.wait()

    return pl.pallas_call(
        body, name="reduce_exchange_halves", in_specs=[hbm] * n, out_specs=[hbm] * n,
        out_shape=[jax.ShapeDtypeStruct((g.shape[0], g.shape[1] // 2, g.shape[2]), BF16) for g in grads],
        scratch_shapes=[pltpu.SemaphoreType.DMA((n,)), pltpu.SemaphoreType.DMA((n,))],
    )(*grads)


def _add_halves(grad, got, c_idx, name):
    j, r, cols = grad.shape
    rh = r // 2
    tr = 128
    nb = rh // tr

    def body(c_ref, g_ref, o_ref_in, out_ref):
        out_ref[...] = (g_ref[...].astype(F32) + o_ref_in[...].astype(F32)).astype(BF16)

    return pl.pallas_call(
        body, name=name,
        grid_spec=pltpu.PrefetchScalarGridSpec(
            num_scalar_prefetch=1, grid=(j, nb),
            in_specs=[pl.BlockSpec((None, tr, cols), lambda jj, i, c_ref: (jj, c_ref[0] * nb + i, 0)),
                      pl.BlockSpec((None, tr, cols), lambda jj, i, c_ref: (jj, i, 0))],
            out_specs=pl.BlockSpec((None, tr, cols), lambda jj, i, c_ref: (jj, i, 0))),
        out_shape=jax.ShapeDtypeStruct((j, rh, cols), BF16),
        compiler_params=_params("parallel", "parallel"),
    )(c_idx, grad, got)


def _scatter_partials(partials):
    n = len(partials)
    hbm = pl.BlockSpec(memory_space=pl.ANY)

    def body(*refs):
        ins, outs = refs[:n], refs[n:2 * n]
        send, recv, local = refs[2 * n:]
        x, y, c = _position()
        chip = 2 * x + y
        cps, lcs = [], []
        for t in range(n):
            lc = pltpu.make_async_copy(ins[t].at[chip], outs[t].at[chip], local.at[t])
            lc.start()
            lcs.append(lc)
            for k in range(1, N_CHIPS):
                px, py = _chip_peer(x, y, k)
                s = 3 * t + k - 1
                cp = pltpu.make_async_remote_copy(
                    src_ref=ins[t].at[2 * px + py], dst_ref=outs[t].at[chip], send_sem=send.at[s],
                    recv_sem=recv.at[s], device_id=(px, py, c), device_id_type=MESH)
                cp.start()
                cps.append(cp)
        for cp in cps:
            cp.wait()
        for lc in lcs:
            lc.wait()

    return pl.pallas_call(
        body, name="reduce_scatter_partials", in_specs=[hbm] * n, out_specs=[hbm] * n,
        out_shape=[jax.ShapeDtypeStruct(p.shape, BF16) for p in partials],
        scratch_shapes=[pltpu.SemaphoreType.DMA((3 * n,)), pltpu.SemaphoreType.DMA((3 * n,)),
                        pltpu.SemaphoreType.DMA((n,))],
    )(*partials)


def _sum_chips(parts, name):
    j, rh, cols = parts.shape
    tr = 128

    def body(p_ref, o_ref):
        acc = p_ref[0].astype(F32)
        for s in range(1, j):
            acc = acc + p_ref[s].astype(F32)
        o_ref[...] = acc

    return pl.pallas_call(
        body, name=name, grid=(rh // tr,),
        in_specs=[pl.BlockSpec((j, tr, cols), lambda i: (0, i, 0))],
        out_specs=pl.BlockSpec((tr, cols), lambda i: (i, 0)),
        out_shape=jax.ShapeDtypeStruct((rh, cols), F32),
        compiler_params=_params("parallel"),
    )(parts)


def _share_totals(halves):
    n = len(halves)
    hbm = pl.BlockSpec(memory_space=pl.ANY)

    def body(*refs):
        ins, outs = refs[:n], refs[n:2 * n]
        send, recv, local = refs[2 * n:]
        x, y, c = _position()
        cps, lcs = [], []
        for t in range(n):
            rh = ins[t].shape[0]
            mine = outs[t].at[pl.ds(c * rh, rh)]
            lc = pltpu.make_async_copy(ins[t], mine, local.at[t])
            lc.start()
            lcs.append(lc)
            cp = pltpu.make_async_remote_copy(
                src_ref=ins[t], dst_ref=mine, send_sem=send.at[t], recv_sem=recv.at[t],
                device_id=(x, y, 1 - c), device_id_type=MESH)
            cp.start()
            cps.append(cp)
        for cp in cps:
            cp.wait()
        for lc in lcs:
            lc.wait()

    return pl.pallas_call(
        body, name="reduce_share_totals", in_specs=[hbm] * n, out_specs=[hbm] * n,
        out_shape=[jax.ShapeDtypeStruct((2 * h.shape[0], h.shape[1]), F32) for h in halves],
        scratch_shapes=[pltpu.SemaphoreType.DMA((n,)), pltpu.SemaphoreType.DMA((n,)),
                        pltpu.SemaphoreType.DMA((n,))],
    )(*halves)


SMALL_ROWS = 56


def _reduce_small(packed, silu_c):
    ns = 3 * D_MODEL // N_CHIPS

    def body(p_ref, sc_ref, tot_ref, gw_ref, loss_ref, qk_ref, allp, send, recv):
        x, y, c = _position()
        me = 4 * x + 2 * y + c
        chip = 2 * x + y

        def copy(k):
            return pltpu.make_async_remote_copy(
                src_ref=allp.at[me], dst_ref=allp.at[me], send_sem=send.at[k - 1], recv_sem=recv.at[k - 1],
                device_id=_xor_peer(x, y, c, k), device_id_type=MESH)

        allp[me] = p_ref[...]
        for k in range(1, N_DEV):
            copy(k).start()
        for k in range(1, N_DEV):
            copy(k).wait_recv()
        tot = allp[0]
        for i in range(1, N_DEV):
            tot = tot + allp[i]
        tot_ref[...] = tot
        loss_ref[...] = jnp.sum(tot[11:12, :], axis=1, keepdims=True) * (0.5 / D_MODEL)
        fold = tot[5:11, 0:HEAD_DIM]
        for h in range(1, N_HEADS):
            fold = fold + tot[5:11, h * HEAD_DIM:(h + 1) * HEAD_DIM]
        qk_ref[...] = jnp.concatenate([fold, jnp.zeros((2, HEAD_DIM), F32)], axis=0)
        sct = sc_ref[...].T
        rc = 64
        for l in range(2):
            dms = [allp[i, pl.ds(12 + 4 * l + chip, 1), :][:, :ns] for i in range(N_DEV)]
            for r0 in range(0, D_MODEL, rc):
                acc = sct[r0:r0 + rc, 0:1] * dms[0]
                for i in range(1, N_DEV):
                    acc = acc + sct[r0:r0 + rc, i:i + 1] * dms[i]
                gw_ref[l, r0:r0 + rc, :] = acc
        for k in range(1, N_DEV):
            copy(k).wait_send()

    vm = pl.BlockSpec(memory_space=pltpu.VMEM)
    return pl.pallas_call(
        body, name="reduce_small", in_specs=[vm, vm], out_specs=[vm] * 4,
        out_shape=[jax.ShapeDtypeStruct((SMALL_ROWS, D_MODEL), F32), jax.ShapeDtypeStruct((2, D_MODEL, ns), F32),
                   jax.ShapeDtypeStruct((1, 1), F32), jax.ShapeDtypeStruct((8, HEAD_DIM), F32)],
        scratch_shapes=[pltpu.VMEM((N_DEV, SMALL_ROWS, D_MODEL), F32),
                        pltpu.SemaphoreType.DMA((N_DEV - 1,)), pltpu.SemaphoreType.DMA((N_DEV - 1,))],
        compiler_params=pltpu.CompilerParams(vmem_limit_bytes=VMEM_LIMIT_BYTES),
    )(packed, silu_c)


def _reduce_big(grads, c_idx):
    names = list(grads)
    got = _exchange_halves([grads[k] for k in names])
    partials = [_add_halves(grads[k], got[i], c_idx, f"reduce_add_{k}") for i, k in enumerate(names)]
    parts = _scatter_partials(partials)
    halves = [_sum_chips(parts[i], f"reduce_sum_{k}") for i, k in enumerate(names)]
    totals = _share_totals(halves)
    return dict(zip(names, totals))


def kernel(x, c, norm_g, ada_w, ada_b, a_w_in, a_conv_w, a_conv_b, a_ln_g, a_ln_b, a_w_out, b_w_in, b_q_norm, b_k_norm, b_w_out, loss_target, m_norm_g, m_ada_w, m_ada_b, m_a_w_in, m_a_conv_w, m_a_conv_b, m_a_ln_g, m_a_ln_b, m_a_w_out, m_b_w_in, m_b_q_norm, m_b_k_norm, m_b_w_out, v_norm_g, v_ada_w, v_ada_b, v_a_w_in, v_a_conv_w, v_a_conv_b, v_a_ln_g, v_a_ln_b, v_a_w_out, v_b_w_in, v_b_q_norm, v_b_k_norm, v_b_w_out):
    chip = 2 * lax.axis_index("x") + lax.axis_index("y")
    core = lax.axis_index("c")
    chip_idx = chip.astype(jnp.int32).reshape(1)
    dev_idx = jnp.stack([2 * chip + core, chip, core]).astype(jnp.int32)

    lands_a = [_cast_into_slot(a_w_in[0], chip_idx, "cast_a_w_in"), _cast_into_slot(a_w_out[0], chip_idx, "cast_a_w_out")]
    send_a, recv_a, lands_a, token_a = _gather_start(lands_a, chip_idx, "gather_start_a")
    lands_b = [_cast_into_slot(b_w_in[0], chip_idx, "cast_b_w_in"), _cast_into_slot(b_w_out[0], chip_idx, "cast_b_w_out")]
    send_b, recv_b, lands_b, token_b = _gather_start(lands_b, token_a, "gather_start_b")
    mods, silu_c, conv_w_full = _ada_forward(c + token_b[0:1, 0:1], ada_w, ada_b, a_conv_w[0])

    def weights_a(after):
        w_in, w_out = _gather_wait(send_a, recv_a, lands_a, after, "gather_wait_a")
        return w_in, w_out.reshape(D_MODEL, D_MODEL)

    def weights_b(after):
        w_in, w_out = _gather_wait(send_b, recv_b, lands_b, after, "gather_wait_b")
        return w_in, w_out.reshape(D_MODEL, D_MODEL)

    in_flight = {}

    def send_grads(tag, dw_in, dw_out):
        grads = [dw_in, dw_out.reshape(N_CHIPS, D_MODEL // N_CHIPS, D_MODEL)]
        send, recv, grads, lands, token = _reduce_start(grads, dw_out, f"reduce_start_{tag}")
        in_flight[tag] = (send, recv, grads, lands)
        return token

    def finish_grads(tag, after):
        send, recv, grads, lands = in_flight[tag]
        grads, lands = _reduce_wait(send, recv, grads, lands, after, f"reduce_wait_{tag}")
        totals = [_sum_devices(lands[i], grads[i], dev_idx, f"reduce_sum_{tag}_{i}") for i in range(2)]
        return _share_halves(totals)

    grad_x, small = _local_step(
        x[0], loss_target[0], mods.reshape(2, 3, D_MODEL), norm_g, conv_w_full, a_conv_b, a_ln_g[0:1],
        a_ln_b[0:1], b_q_norm[0], b_k_norm[0], weights_a, weights_b,
        functools.partial(send_grads, "b"), functools.partial(send_grads, "a"))

    ns = 3 * D_MODEL // N_CHIPS
    pad_mod = lambda dm: jnp.pad(dm.reshape(N_CHIPS, ns), ((0, 0), (0, D_MODEL - ns)))
    packed = jnp.concatenate([
        small["dnorm_g"], small["dconv_b"], small["dln_g"], small["dln_b"], small["dq_norm"], small["dk_norm"],
        small["loss_cols"], pad_mod(small["dmod0"]), pad_mod(small["dmod1"]), small["dconv_w"],
        jnp.zeros((SMALL_ROWS - 20 - CONV_WIDTH, D_MODEL), F32)], axis=0)
    tot, g_ada_w, loss, qk = _reduce_small(packed, silu_c)
    cw = D_MODEL // N_CHIPS
    g_small = dict(
        norm_g=tot[0:2], a_conv_b=tot[2:3], a_ln_g=tot[3:4], a_ln_b=tot[4:5],
        b_q_norm=qk[0:3], b_k_norm=qk[3:6],
        ada_b=jnp.stack([tot[12:16, :ns].reshape(3 * D_MODEL), tot[16:20, :ns].reshape(3 * D_MODEL)]),
        a_conv_w=lax.dynamic_slice(tot[20:20 + CONV_WIDTH], (0, chip * cw), (CONV_WIDTH, cw)),
    )

    g_big = {}
    g_big["b_w_in"], g_big["b_w_out"] = finish_grads("b", tot)
    g_big["a_w_in"], g_big["a_w_out"] = finish_grads("a", g_big["b_w_out"])

    given = dict(norm_g=(norm_g, m_norm_g, v_norm_g), ada_w=(ada_w, m_ada_w, v_ada_w), ada_b=(ada_b, m_ada_b, v_ada_b),
                 a_w_in=(a_w_in, m_a_w_in, v_a_w_in), a_conv_w=(a_conv_w, m_a_conv_w, v_a_conv_w),
                 a_conv_b=(a_conv_b, m_a_conv_b, v_a_conv_b), a_ln_g=(a_ln_g, m_a_ln_g, v_a_ln_g),
                 a_ln_b=(a_ln_b, m_a_ln_b, v_a_ln_b), a_w_out=(a_w_out, m_a_w_out, v_a_w_out),
                 b_w_in=(b_w_in, m_b_w_in, v_b_w_in), b_q_norm=(b_q_norm, m_b_q_norm, v_b_q_norm),
                 b_k_norm=(b_k_norm, m_b_k_norm, v_b_k_norm), b_w_out=(b_w_out, m_b_w_out, v_b_w_out))
    grads2d = dict(g_small)
    grads2d.update(g_big)
    grads2d["ada_w"] = g_ada_w.reshape(2 * D_MODEL, ns)
    order = ["norm_g", "ada_w", "ada_b", "a_w_in", "a_conv_w", "a_conv_b", "a_ln_g", "a_ln_b", "a_w_out", "b_w_in",
             "b_q_norm", "b_k_norm", "b_w_out"]
    out_g, out_d, out_m, out_v = [], [], [], []
    for k in order:
        w, m, v = given[k]
        g2 = grads2d[k]
        shape2 = g2.shape
        d2, m2, v2 = _adamw(w.reshape(shape2), g2, m.reshape(shape2), v.reshape(shape2), f"adamw_{k}")
        out_g.append(g2.reshape(w.shape))
        out_d.append(d2.reshape(w.shape))
        out_m.append(m2.reshape(w.shape))
        out_v.append(v2.reshape(w.shape))
    return (loss.reshape(()), grad_x[None], *out_g, *out_d, *out_m, *out_v)
```

```python
import functools

import jax
import jax.numpy as jnp
from jax import lax
from jax.experimental import pallas as pl
from jax.experimental.pallas import tpu as pltpu

F32 = jnp.float32
BF16 = jnp.bfloat16

SEQ = 2048
D_MODEL = 1024
CONV_WIDTH = 31
HEAD_DIM = 64
N_HEADS = 16
DILATIONS = (1, 4, 16)
ATTN_BLOCK = 128
NORM_EPS = 1e-6
NEG_INF = -1e30
N_DEV = 8
N_CHIPS = 4

ADAM_LR = 0.001
ADAM_B1 = 0.9
ADAM_B2 = 0.999
ADAM_EPS = 1e-08
ADAM_WD = 0.01
ADAM_STEP = 10

VMEM_LIMIT_BYTES = 52 * 1024 * 1024
HALO = 32
MESH = pl.DeviceIdType.MESH


def _params(*sem):
    return pltpu.CompilerParams(dimension_semantics=sem or None, vmem_limit_bytes=VMEM_LIMIT_BYTES)


def _sigmoid(v):
    return 1.0 / (1.0 + jnp.exp(-v))


def _row_spec(tm, cols, col_block=0):
    return pl.BlockSpec((tm, cols), lambda i: (i, col_block))


def _vec_spec(rows, cols):
    return pl.BlockSpec((rows, cols), lambda i: (0, 0))


def _normmod(xv, g, scale, shift):
    r = lax.rsqrt(jnp.mean(xv * xv, axis=-1, keepdims=True) + NORM_EPS)
    return xv * r * g * (1.0 + scale) + shift


def _normmod_fwd(x, g, scale, shift, name):
    tm = 256

    def body(x_ref, g_ref, sc_ref, sh_ref, h_ref, ht_ref):
        h = _normmod(x_ref[...], g_ref[...], sc_ref[...], sh_ref[...])
        h_ref[...] = h.astype(BF16)
        ht_ref[...] = h.T.astype(BF16)

    return pl.pallas_call(
        body, name=name, grid=(SEQ // tm,),
        in_specs=[_row_spec(tm, D_MODEL)] + [_vec_spec(1, D_MODEL)] * 3,
        out_specs=[_row_spec(tm, D_MODEL), pl.BlockSpec((D_MODEL, tm), lambda i: (0, i))],
        out_shape=[jax.ShapeDtypeStruct((SEQ, D_MODEL), BF16), jax.ShapeDtypeStruct((D_MODEL, SEQ), BF16)],
        compiler_params=_params("parallel"),
    )(x, g, scale, shift)


def _normmod_bwd(x, g, scale, dh_parts, dres, name):
    tm = 256
    n_parts = len(dh_parts)

    def body(x_ref, g_ref, sc_ref, dres_ref, *rest):
        part_refs = rest[:n_parts]
        dx_ref, sums_ref = rest[n_parts:]
        xv = x_ref[...]
        r = lax.rsqrt(jnp.mean(xv * xv, axis=-1, keepdims=True) + NORM_EPS)
        xn = xv * r
        dh = part_refs[0][...]
        for p in part_refs[1:]:
            dh = dh + p[...]
        gv = g_ref[...]
        one_sc = 1.0 + sc_ref[...]
        dxn = dh * (gv * one_sc)
        dx = r * (dxn - xn * jnp.mean(dxn * xn, axis=-1, keepdims=True))
        dx_ref[...] = dres_ref[...] + dx
        dhx = dh * xn
        sums = jnp.concatenate([
            jnp.sum(dhx, axis=0, keepdims=True) * one_sc,
            jnp.sum(dhx, axis=0, keepdims=True) * gv,
            jnp.sum(dh, axis=0, keepdims=True),
            jnp.zeros((5, D_MODEL), F32)], axis=0)

        @pl.when(pl.program_id(0) == 0)
        def _():
            sums_ref[...] = jnp.zeros_like(sums_ref)

        sums_ref[...] += sums

    return pl.pallas_call(
        body, name=name, grid=(SEQ // tm,),
        in_specs=[_row_spec(tm, D_MODEL), _vec_spec(1, D_MODEL), _vec_spec(1, D_MODEL), _row_spec(tm, D_MODEL)]
        + [_row_spec(tm, D_MODEL)] * n_parts,
        out_specs=[_row_spec(tm, D_MODEL), _vec_spec(8, D_MODEL)],
        out_shape=[jax.ShapeDtypeStruct((SEQ, D_MODEL), F32), jax.ShapeDtypeStruct((8, D_MODEL), F32)],
        compiler_params=_params("arbitrary"),
    )(x, g, scale, dres, *dh_parts)


def _mm(lhs, rhs, *, tn, tile0, n_tiles, out_dtype, name, out3d=None, prev=None):
    mo, kc = lhs.shape
    cm = 512

    def body(l_ref, r_ref, *rest):
        o_ref = rest[-1]
        for m in range(mo // cm):
            rows = pl.ds(m * cm, cm)
            o_ref[rows, :] = jnp.dot(l_ref[rows, :], r_ref[...], preferred_element_type=F32).astype(out_dtype)

    if rhs.ndim == 3:
        tps_r = rhs.shape[2] // tn
        r_spec = pl.BlockSpec((None, kc, tn), lambda t: ((tile0 + t) // tps_r, 0, (tile0 + t) % tps_r))
    else:
        r_spec = pl.BlockSpec((kc, tn), lambda t: (0, t))
    in_specs = [pl.BlockSpec((mo, kc), lambda t: (0, 0)), r_spec]
    args = [lhs, rhs]
    aliases = {}
    if out3d is None:
        o_spec = pl.BlockSpec((mo, tn), lambda t: (0, t))
        o_shape = jax.ShapeDtypeStruct((mo, n_tiles * tn), out_dtype)
    else:
        j_out, ns_out = out3d
        tps_o = ns_out // tn
        o_spec = pl.BlockSpec((None, mo, tn), lambda t: ((tile0 + t) // tps_o, 0, (tile0 + t) % tps_o))
        o_shape = jax.ShapeDtypeStruct((j_out, mo, ns_out), out_dtype)
        if prev is not None:
            in_specs.append(pl.BlockSpec(memory_space=pl.ANY))
            args.append(prev)
            aliases = {2: 0}
    return pl.pallas_call(
        body, name=name, grid=(n_tiles,), in_specs=in_specs, out_specs=o_spec, out_shape=o_shape,
        input_output_aliases=aliases, compiler_params=_params("parallel"),
    )(*args)


def _mm_nt(dy, w3, *, tn, tile0, n_tiles, name, after=None):
    m_rows = dy.shape[0]
    _, kc, ns = w3.shape
    tps = ns // tn
    cm = 512
    extra = [] if after is None else [after]

    def body(dy_ref, w_ref, *rest):
        o_ref = rest[-1]

        @pl.when(pl.program_id(0) == 0)
        def _():
            o_ref[...] = jnp.zeros_like(o_ref)

        for m in range(m_rows // cm):
            rows = pl.ds(m * cm, cm)
            o_ref[rows, :] += lax.dot_general(dy_ref[rows, :], w_ref[...], (((1,), (1,)), ((), ())),
                                              preferred_element_type=F32)

    return pl.pallas_call(
        body, name=name, grid=(n_tiles,),
        in_specs=[pl.BlockSpec((m_rows, tn), lambda t: (0, t)),
                  pl.BlockSpec((None, kc, tn), lambda t: ((tile0 + t) // tps, 0, (tile0 + t) % tps))]
        + [pl.BlockSpec(memory_space=pl.ANY)] * len(extra),
        out_specs=pl.BlockSpec((m_rows, kc), lambda t: (0, 0)),
        out_shape=jax.ShapeDtypeStruct((m_rows, kc), F32),
        compiler_params=_params("arbitrary"),
    )(dy, w3, *extra)


def _conv_fwd(proj, conv_w, conv_b, ln_g, ln_b, name):
    tm = 256
    hb = tm // HALO

    def body(vg_ref, halo_ref, z_ref, w_ref, b_ref, g_ref, be_ref, u5_ref, u5t_ref, u2_ref, buf):
        i = pl.program_id(0)
        u1 = vg_ref[:, :D_MODEL] * _sigmoid(vg_ref[:, D_MODEL:])
        u1h = halo_ref[:, :D_MODEL] * _sigmoid(halo_ref[:, D_MODEL:])
        buf[pl.ds(0, HALO), :] = jnp.where(i > 0, u1h, 0.0)
        buf[pl.ds(HALO, tm), :] = u1
        acc = jnp.zeros((tm, D_MODEL), F32) + b_ref[...]
        for k in range(CONV_WIDTH):
            acc = acc + w_ref[k:k + 1, :] * buf[pl.ds(HALO - (CONV_WIDTH - 1) + k, tm), :]
        u2_ref[...] = acc
        mu = jnp.mean(acc, axis=-1, keepdims=True)
        xc = acc - mu
        rstd = lax.rsqrt(jnp.mean(xc * xc, axis=-1, keepdims=True) + NORM_EPS)
        u3 = xc * rstd * g_ref[...] + be_ref[...]
        zv = z_ref[...]
        u5 = u3 * _sigmoid(u3) * (zv * _sigmoid(zv))
        u5_ref[...] = u5.astype(BF16)
        u5t_ref[...] = u5.T.astype(BF16)

    return pl.pallas_call(
        body, name=name, grid=(SEQ // tm,),
        in_specs=[pl.BlockSpec((tm, 2 * D_MODEL), lambda i: (i, 0)),
                  pl.BlockSpec((HALO, 2 * D_MODEL), lambda i: (jnp.maximum(i * hb - 1, 0), 0)),
                  _row_spec(tm, D_MODEL, 2),
                  _vec_spec(CONV_WIDTH, D_MODEL)] + [_vec_spec(1, D_MODEL)] * 3,
        out_specs=[_row_spec(tm, D_MODEL), pl.BlockSpec((D_MODEL, tm), lambda i: (0, i)), _row_spec(tm, D_MODEL)],
        out_shape=[jax.ShapeDtypeStruct((SEQ, D_MODEL), BF16), jax.ShapeDtypeStruct((D_MODEL, SEQ), BF16),
                   jax.ShapeDtypeStruct((SEQ, D_MODEL), F32)],
        scratch_shapes=[pltpu.VMEM((HALO + tm, D_MODEL), F32)],
        compiler_params=_params("parallel"),
    )(proj, proj, proj, conv_w, conv_b, ln_g, ln_b)


def _conv_bwd_pointwise(du5, proj, u2, ln_g, ln_b, name):
    tm = 256

    def body(du5_ref, z_ref, u2_ref, g_ref, be_ref, du2_ref, dz_ref, sums_ref):
        u2v = u2_ref[...]
        mu = jnp.mean(u2v, axis=-1, keepdims=True)
        xc = u2v - mu
        rstd = lax.rsqrt(jnp.mean(xc * xc, axis=-1, keepdims=True) + NORM_EPS)
        xhat = xc * rstd
        u3 = xhat * g_ref[...] + be_ref[...]
        s3 = _sigmoid(u3)
        u4 = u3 * s3
        zv = z_ref[...]
        sz = _sigmoid(zv)
        du5v = du5_ref[...]
        dz_ref[...] = du5v * u4 * (sz * (1.0 + zv * (1.0 - sz)))
        du3 = du5v * (zv * sz) * (s3 * (1.0 + u3 * (1.0 - s3)))
        dxhat = du3 * g_ref[...]
        du2 = rstd * (dxhat - jnp.mean(dxhat, axis=-1, keepdims=True)
                      - xhat * jnp.mean(dxhat * xhat, axis=-1, keepdims=True))
        du2_ref[...] = du2
        sums = jnp.concatenate([
            jnp.sum(du3 * xhat, axis=0, keepdims=True),
            jnp.sum(du3, axis=0, keepdims=True),
            jnp.sum(du2, axis=0, keepdims=True),
            jnp.zeros((5, D_MODEL), F32)], axis=0)

        @pl.when(pl.program_id(0) == 0)
        def _():
            sums_ref[...] = jnp.zeros_like(sums_ref)

        sums_ref[...] += sums

    return pl.pallas_call(
        body, name=name, grid=(SEQ // tm,),
        in_specs=[_row_spec(tm, D_MODEL), _row_spec(tm, D_MODEL, 2), _row_spec(tm, D_MODEL),
                  _vec_spec(1, D_MODEL), _vec_spec(1, D_MODEL)],
        out_specs=[_row_spec(tm, D_MODEL), _row_spec(tm, D_MODEL), _vec_spec(8, D_MODEL)],
        out_shape=[jax.ShapeDtypeStruct((SEQ, D_MODEL), F32), jax.ShapeDtypeStruct((SEQ, D_MODEL), F32),
                   jax.ShapeDtypeStruct((8, D_MODEL), F32)],
        compiler_params=_params("arbitrary"),
    )(du5, proj, u2, ln_g, ln_b)


def _conv_bwd_taps(du2, dz, proj, conv_w, name):
    tm = 256
    hb = tm // HALO
    n_blocks = SEQ // tm

    def body(du2_ref, dnext_ref, dz_ref, vg_ref, halo_ref, w_ref, dproj_ref, dw_ref, ubuf, dbuf):
        i = pl.program_id(0)
        val = vg_ref[:, :D_MODEL]
        sg = _sigmoid(vg_ref[:, D_MODEL:])
        u1h = halo_ref[:, :D_MODEL] * _sigmoid(halo_ref[:, D_MODEL:])
        ubuf[pl.ds(0, HALO), :] = jnp.where(i > 0, u1h, 0.0)
        ubuf[pl.ds(HALO, tm), :] = val * sg
        du2v = du2_ref[...]
        dbuf[pl.ds(0, tm), :] = du2v
        dbuf[pl.ds(tm, HALO), :] = jnp.where(i < n_blocks - 1, dnext_ref[...], 0.0)

        @pl.when(i == 0)
        def _():
            dw_ref[...] = jnp.zeros_like(dw_ref)

        du1 = jnp.zeros((tm, D_MODEL), F32)
        for k in range(CONV_WIDTH):
            du1 = du1 + w_ref[k:k + 1, :] * dbuf[pl.ds(CONV_WIDTH - 1 - k, tm), :]
            dw_ref[k:k + 1, :] += jnp.sum(du2v * ubuf[pl.ds(HALO - (CONV_WIDTH - 1) + k, tm), :],
                                          axis=0, keepdims=True)
        dproj_ref[:, :D_MODEL] = (du1 * sg).astype(BF16)
        dproj_ref[:, D_MODEL:2 * D_MODEL] = (du1 * val * sg * (1.0 - sg)).astype(BF16)
        dproj_ref[:, 2 * D_MODEL:] = dz_ref[...].astype(BF16)

    return pl.pallas_call(
        body, name=name, grid=(n_blocks,),
        in_specs=[_row_spec(tm, D_MODEL),
                  pl.BlockSpec((HALO, D_MODEL), lambda i: (jnp.minimum((i + 1) * hb, SEQ // HALO - 1), 0)),
                  _row_spec(tm, D_MODEL),
                  pl.BlockSpec((tm, 2 * D_MODEL), lambda i: (i, 0)),
                  pl.BlockSpec((HALO, 2 * D_MODEL), lambda i: (jnp.maximum(i * hb - 1, 0), 0)),
                  _vec_spec(CONV_WIDTH, D_MODEL)],
        out_specs=[_row_spec(tm, 3 * D_MODEL), _vec_spec(32, D_MODEL)],
        out_shape=[jax.ShapeDtypeStruct((SEQ, 3 * D_MODEL), BF16), jax.ShapeDtypeStruct((32, D_MODEL), F32)],
        scratch_shapes=[pltpu.VMEM((HALO + tm, D_MODEL), F32), pltpu.VMEM((tm + HALO, D_MODEL), F32)],
        compiler_params=_params("arbitrary"),
    )(du2, du2, dz, proj, proj, conv_w)


def _out_a(u5, w_out, x, gate, g1, scale1, shift1, name):
    tm = 256

    def body(u_ref, w_ref, x_ref, gate_ref, g_ref, sc_ref, sh_ref, x1_ref, y_ref, h_ref, ht_ref):
        y = jnp.dot(u_ref[...], w_ref[...], preferred_element_type=F32)
        x1 = x_ref[...] + gate_ref[...] * y
        y_ref[...] = y
        x1_ref[...] = x1
        h = _normmod(x1, g_ref[...], sc_ref[...], sh_ref[...])
        h_ref[...] = h.astype(BF16)
        ht_ref[...] = h.T.astype(BF16)

    return pl.pallas_call(
        body, name=name, grid=(SEQ // tm,),
        in_specs=[_row_spec(tm, D_MODEL), _vec_spec(D_MODEL, D_MODEL), _row_spec(tm, D_MODEL)]
        + [_vec_spec(1, D_MODEL)] * 4,
        out_specs=[_row_spec(tm, D_MODEL), _row_spec(tm, D_MODEL), _row_spec(tm, D_MODEL),
                   pl.BlockSpec((D_MODEL, tm), lambda i: (0, i))],
        out_shape=[jax.ShapeDtypeStruct((SEQ, D_MODEL), F32), jax.ShapeDtypeStruct((SEQ, D_MODEL), F32),
                   jax.ShapeDtypeStruct((SEQ, D_MODEL), BF16), jax.ShapeDtypeStruct((D_MODEL, SEQ), BF16)],
        compiler_params=_params("parallel"),
    )(u5, w_out, x, gate, g1, scale1, shift1)


def _out_b_loss(u, w_out, x1, gate, target, name):
    tm = 256

    def body(u_ref, w_ref, x_ref, gate_ref, t_ref, e_ref, dy_ref, sums_ref):
        y = jnp.dot(u_ref[...], w_ref[...], preferred_element_type=F32)
        diff = x_ref[...] + gate_ref[...] * y - t_ref[...]
        e = diff * (1.0 / D_MODEL)
        e_ref[...] = e
        dy_ref[...] = (e * gate_ref[...]).astype(BF16)
        sums = jnp.concatenate([
            jnp.sum(e * y, axis=0, keepdims=True),
            jnp.sum(diff * diff, axis=0, keepdims=True),
            jnp.zeros((6, D_MODEL), F32)], axis=0)

        @pl.when(pl.program_id(0) == 0)
        def _():
            sums_ref[...] = jnp.zeros_like(sums_ref)

        sums_ref[...] += sums

    return pl.pallas_call(
        body, name=name, grid=(SEQ // tm,),
        in_specs=[_row_spec(tm, D_MODEL), _vec_spec(D_MODEL, D_MODEL), _row_spec(tm, D_MODEL),
                  _vec_spec(1, D_MODEL), _row_spec(tm, D_MODEL)],
        out_specs=[_row_spec(tm, D_MODEL), _row_spec(tm, D_MODEL), _vec_spec(8, D_MODEL)],
        out_shape=[jax.ShapeDtypeStruct((SEQ, D_MODEL), F32), jax.ShapeDtypeStruct((SEQ, D_MODEL), BF16),
                   jax.ShapeDtypeStruct((8, D_MODEL), F32)],
        compiler_params=_params("arbitrary"),
    )(u, w_out, x1, gate, target)


def _dgate_dy(dx1, y, gate, name):
    tm = 256

    def body(d_ref, y_ref, gate_ref, dy_ref, sums_ref):
        dv = d_ref[...]
        dy_ref[...] = (dv * gate_ref[...]).astype(BF16)
        sums = jnp.concatenate([jnp.sum(dv * y_ref[...], axis=0, keepdims=True), jnp.zeros((7, D_MODEL), F32)], axis=0)

        @pl.when(pl.program_id(0) == 0)
        def _():
            sums_ref[...] = jnp.zeros_like(sums_ref)

        sums_ref[...] += sums

    return pl.pallas_call(
        body, name=name, grid=(SEQ // tm,),
        in_specs=[_row_spec(tm, D_MODEL), _row_spec(tm, D_MODEL), _vec_spec(1, D_MODEL)],
        out_specs=[_row_spec(tm, D_MODEL), _vec_spec(8, D_MODEL)],
        out_shape=[jax.ShapeDtypeStruct((SEQ, D_MODEL), BF16), jax.ShapeDtypeStruct((8, D_MODEL), F32)],
        compiler_params=_params("arbitrary"),
    )(dx1, y, gate)


def _mm_nt_res(dy, w, name):
    tm = 256
    kc, n = w.shape

    def body(dy_ref, w_ref, o_ref):
        o_ref[...] = lax.dot_general(dy_ref[...], w_ref[...], (((1,), (1,)), ((), ())), preferred_element_type=F32)

    return pl.pallas_call(
        body, name=name, grid=(SEQ // tm,),
        in_specs=[_row_spec(tm, n), _vec_spec(kc, n)],
        out_specs=_row_spec(tm, kc),
        out_shape=jax.ShapeDtypeStruct((SEQ, kc), F32),
        compiler_params=_params("parallel"),
    )(dy, w)


def _seg_matrix():
    r = lax.broadcasted_iota(jnp.int32, (256, 256), 0) // HEAD_DIM
    c = lax.broadcasted_iota(jnp.int32, (256, 256), 1) // HEAD_DIM
    return (r == c).astype(BF16)


def _segsum(v, seg):
    hi = v.astype(BF16)
    lo = (v - hi.astype(F32)).astype(BF16)
    outs = []
    for c0 in range(0, D_MODEL, 256):
        outs.append(jnp.dot(hi[:, c0:c0 + 256], seg, preferred_element_type=F32)
                    + jnp.dot(lo[:, c0:c0 + 256], seg, preferred_element_type=F32))
    return jnp.concatenate(outs, axis=1)


def _qk_rstd(v, seg):
    return lax.rsqrt(_segsum(v * v, seg) * (1.0 / HEAD_DIM) + NORM_EPS)


def _qknorm_fwd(proj, qw, kw, seg, name):
    tm = 256

    def body(p_ref, qw_ref, kw_ref, seg_ref, q_ref, k_ref, v_ref):
        segv = seg_ref[...]
        q = p_ref[:, :D_MODEL]
        k = p_ref[:, D_MODEL:2 * D_MODEL]
        q_ref[...] = (q * _qk_rstd(q, segv) * qw_ref[...]).astype(BF16)
        k_ref[...] = (k * _qk_rstd(k, segv) * kw_ref[...]).astype(BF16)
        v_ref[...] = p_ref[:, 2 * D_MODEL:].astype(BF16)

    return pl.pallas_call(
        body, name=name, grid=(SEQ // tm,),
        in_specs=[_row_spec(tm, 3 * D_MODEL), _vec_spec(1, D_MODEL), _vec_spec(1, D_MODEL), _vec_spec(256, 256)],
        out_specs=[_row_spec(tm, D_MODEL)] * 3,
        out_shape=[jax.ShapeDtypeStruct((SEQ, D_MODEL), BF16)] * 3,
        compiler_params=_params("parallel"),
    )(proj, qw, kw, seg)


def _attn_masks(b, bpc, dilation, slope):
    qi = lax.broadcasted_iota(jnp.int32, (ATTN_BLOCK, 2 * ATTN_BLOCK), 0)
    kj = lax.broadcasted_iota(jnp.int32, (ATTN_BLOCK, 2 * ATTN_BLOCK), 1)
    steps = qi + ATTN_BLOCK - kj
    has_prev = (b % bpc) != 0
    valid = (steps >= 0) & (steps <= ATTN_BLOCK) & (has_prev | (kj >= ATTN_BLOCK))
    return (steps * dilation).astype(F32), valid


ATTN_HEADS_FWD = 8
ATTN_HEADS_BWD = 4
NT_DIMS = (((1,), (1,)), ((), ()))
TN_DIMS = (((0,), (0,)), ((), ()))


def _attn_specs(heads):
    cur = pl.BlockSpec((ATTN_BLOCK, heads * HEAD_DIM), lambda hg, b: (b, hg))
    prev = pl.BlockSpec((ATTN_BLOCK, heads * HEAD_DIM), lambda hg, b: (jnp.maximum(b - 1, 0), hg))
    return cur, prev


def _attn_fwd(q, k, v, slopes, dilation, name):
    bpc = SEQ // dilation // ATTN_BLOCK
    heads = ATTN_HEADS_FWD
    cur, prev = _attn_specs(heads)
    scale = HEAD_DIM ** -0.5

    def body(sl_ref, q_ref, kp_ref, kc_ref, vp_ref, vc_ref, o_ref, lse_ref):
        hg = pl.program_id(0)
        dist, valid = _attn_masks(pl.program_id(1), bpc, dilation, None)
        for h in range(heads):
            cols = slice(h * HEAD_DIM, (h + 1) * HEAD_DIM)
            kcat = jnp.concatenate([kp_ref[:, cols], kc_ref[:, cols]], axis=0)
            vcat = jnp.concatenate([vp_ref[:, cols], vc_ref[:, cols]], axis=0)
            s = lax.dot_general(q_ref[:, cols], kcat, NT_DIMS, preferred_element_type=F32)
            s = jnp.where(valid, s * scale - dist * sl_ref[heads * hg + h], NEG_INF)
            m = jnp.max(s, axis=-1, keepdims=True)
            p = jnp.exp(s - m)
            l = jnp.sum(p, axis=-1, keepdims=True)
            acc = jnp.dot(p.astype(BF16), vcat, preferred_element_type=F32)
            o_ref[:, cols] = acc / l
            lse_ref[:, cols] = jnp.broadcast_to(m + jnp.log(l), (ATTN_BLOCK, HEAD_DIM))

    return pl.pallas_call(
        body, name=name, grid=(N_HEADS // heads, SEQ // ATTN_BLOCK),
        in_specs=[pl.BlockSpec(memory_space=pltpu.SMEM), cur, prev, cur, prev, cur],
        out_specs=[cur, cur],
        out_shape=[jax.ShapeDtypeStruct((SEQ, D_MODEL), F32)] * 2,
        compiler_params=_params("parallel", "parallel"),
    )(slopes, q, k, k, v, v)


def _merge_fwd(o_parts, lse_parts, z, name):
    tm = 256

    def body(o0, o1, o2, l0, l1, l2, z_ref, u_ref, ut_ref, o_ref, lse_ref):
        ls = [l0[...], l1[...], l2[...]]
        m = jnp.maximum(jnp.maximum(ls[0], ls[1]), ls[2])
        tot = m + jnp.log(jnp.exp(ls[0] - m) + jnp.exp(ls[1] - m) + jnp.exp(ls[2] - m))
        o = (jnp.exp(ls[0] - tot) * o0[...] + jnp.exp(ls[1] - tot) * o1[...] + jnp.exp(ls[2] - tot) * o2[...])
        zv = z_ref[...]
        u = o * (zv * _sigmoid(zv))
        u_ref[...] = u.astype(BF16)
        ut_ref[...] = u.T.astype(BF16)
        o_ref[...] = o
        lse_ref[...] = tot

    return pl.pallas_call(
        body, name=name, grid=(SEQ // tm,),
        in_specs=[_row_spec(tm, D_MODEL)] * 7,
        out_specs=[_row_spec(tm, D_MODEL), pl.BlockSpec((D_MODEL, tm), lambda i: (0, i)),
                   _row_spec(tm, D_MODEL), _row_spec(tm, D_MODEL)],
        out_shape=[jax.ShapeDtypeStruct((SEQ, D_MODEL), BF16), jax.ShapeDtypeStruct((D_MODEL, SEQ), BF16),
                   jax.ShapeDtypeStruct((SEQ, D_MODEL), F32), jax.ShapeDtypeStruct((SEQ, D_MODEL), F32)],
        compiler_params=_params("parallel"),
    )(*o_parts, *lse_parts, z)


def _merge_bwd(du, o, z, seg, name):
    tm = 256

    def body(du_ref, o_ref, z_ref, seg_ref, do_ref, dz_ref, delta_ref):
        zv = z_ref[...]
        sz = _sigmoid(zv)
        duv = du_ref[...]
        ov = o_ref[...]
        do = duv * (zv * sz)
        do_ref[...] = do.astype(BF16)
        dz_ref[...] = (duv * ov * (sz * (1.0 + zv * (1.0 - sz)))).astype(BF16)
        delta_ref[...] = _segsum(do * ov, seg_ref[...])

    return pl.pallas_call(
        body, name=name, grid=(SEQ // tm,),
        in_specs=[_row_spec(tm, D_MODEL)] * 3 + [_vec_spec(256, 256)],
        out_specs=[_row_spec(tm, D_MODEL)] * 3,
        out_shape=[jax.ShapeDtypeStruct((SEQ, D_MODEL), BF16), jax.ShapeDtypeStruct((SEQ, D_MODEL), BF16),
                   jax.ShapeDtypeStruct((SEQ, D_MODEL), F32)],
        compiler_params=_params("parallel"),
    )(du, o, z, seg)


def _attn_bwd(q, k, v, do, lse, delta, slopes, dilation, name):
    bpc = SEQ // dilation // ATTN_BLOCK
    heads = ATTN_HEADS_BWD
    cur, prev = _attn_specs(heads)
    scale = HEAD_DIM ** -0.5

    def body(sl_ref, q_ref, kp_ref, kc_ref, vp_ref, vc_ref, do_ref, lse_ref, dl_ref,
             dq_ref, dkc_ref, dkp_ref, dvc_ref, dvp_ref):
        hg = pl.program_id(0)
        dist, valid = _attn_masks(pl.program_id(1), bpc, dilation, None)
        for h in range(heads):
            cols = slice(h * HEAD_DIM, (h + 1) * HEAD_DIM)
            kcat = jnp.concatenate([kp_ref[:, cols], kc_ref[:, cols]], axis=0)
            vcat = jnp.concatenate([vp_ref[:, cols], vc_ref[:, cols]], axis=0)
            qh = q_ref[:, cols]
            doh = do_ref[:, cols]
            lse_col = lse_ref[:, h * HEAD_DIM:h * HEAD_DIM + 1]
            dl_col = dl_ref[:, h * HEAD_DIM:h * HEAD_DIM + 1]
            s = lax.dot_general(qh, kcat, NT_DIMS, preferred_element_type=F32)
            p = jnp.exp(jnp.where(valid, s * scale - dist * sl_ref[heads * hg + h], NEG_INF) - lse_col)
            dp = lax.dot_general(doh, vcat, NT_DIMS, preferred_element_type=F32)
            ds = (p * (dp - dl_col) * scale).astype(BF16)
            dq_ref[:, cols] = jnp.dot(ds, kcat, preferred_element_type=F32)
            dk = lax.dot_general(ds, qh, TN_DIMS, preferred_element_type=F32)
            dv = lax.dot_general(p.astype(BF16), doh, TN_DIMS, preferred_element_type=F32)
            dkp_ref[:, cols] = dk[:ATTN_BLOCK]
            dkc_ref[:, cols] = dk[ATTN_BLOCK:]
            dvp_ref[:, cols] = dv[:ATTN_BLOCK]
            dvc_ref[:, cols] = dv[ATTN_BLOCK:]

    return pl.pallas_call(
        body, name=name, grid=(N_HEADS // heads, SEQ // ATTN_BLOCK),
        in_specs=[pl.BlockSpec(memory_space=pltpu.SMEM), cur, prev, cur, prev, cur, cur, cur, cur],
        out_specs=[cur] * 5,
        out_shape=[jax.ShapeDtypeStruct((SEQ, D_MODEL), F32)] * 5,
        compiler_params=_params("parallel", "parallel"),
    )(slopes, q, k, k, v, v, do, lse, delta)


def _qknorm_bwd(proj, qw, kw, seg, dq, dkc, dkp, dvc, dvp, name):
    tm = ATTN_BLOCK
    n_blocks = SEQ // tm
    nxt = pl.BlockSpec((tm, D_MODEL), lambda i: (jnp.minimum(i + 1, n_blocks - 1), 0))

    def body(p_ref, qw_ref, kw_ref, seg_ref, dq_ref, dkc_ref, dkp_ref, dvc_ref, dvp_ref, dproj_ref, sums_ref):
        i = pl.program_id(0)
        segv = seg_ref[...]
        has_next = i < n_blocks - 1
        dk = dkc_ref[...] + jnp.where(has_next, dkp_ref[...], 0.0)
        dv = dvc_ref[...] + jnp.where(has_next, dvp_ref[...], 0.0)
        sums = []
        for part, (raw, w, dn) in enumerate(((p_ref[:, :D_MODEL], qw_ref[...], dq_ref[...]),
                                             (p_ref[:, D_MODEL:2 * D_MODEL], kw_ref[...], dk))):
            r = _qk_rstd(raw, segv)
            gq = dn * w
            draw = r * gq - raw * (r * r * r) * (_segsum(raw * gq, segv) * (1.0 / HEAD_DIM))
            dproj_ref[:, part * D_MODEL:(part + 1) * D_MODEL] = draw.astype(BF16)
            sums.append(jnp.sum(dn * raw * r, axis=0, keepdims=True))
        dproj_ref[:, 2 * D_MODEL:] = dv.astype(BF16)

        @pl.when(i == 0)
        def _():
            sums_ref[...] = jnp.zeros_like(sums_ref)

        sums_ref[...] += jnp.concatenate(sums + [jnp.zeros((6, D_MODEL), F32)], axis=0)

    return pl.pallas_call(
        body, name=name, grid=(n_blocks,),
        in_specs=[_row_spec(tm, 3 * D_MODEL), _vec_spec(1, D_MODEL), _vec_spec(1, D_MODEL), _vec_spec(256, 256),
                  _row_spec(tm, D_MODEL), _row_spec(tm, D_MODEL), nxt, _row_spec(tm, D_MODEL), nxt],
        out_specs=[_row_spec(tm, 3 * D_MODEL), _vec_spec(8, D_MODEL)],
        out_shape=[jax.ShapeDtypeStruct((SEQ, 3 * D_MODEL), BF16), jax.ShapeDtypeStruct((8, D_MODEL), F32)],
        compiler_params=_params("arbitrary"),
    )(proj, qw, kw, seg, dq, dkc, dkp, dvc, dvp)


def _to_classes(a, dilation):
    if dilation == 1:
        return a
    s, c = a.shape
    return a.reshape(s // dilation, dilation, c).transpose(1, 0, 2).reshape(s, c)


def _from_classes(a, dilation):
    if dilation == 1:
        return a
    s, c = a.shape
    return a.reshape(dilation, s // dilation, c).transpose(1, 0, 2).reshape(s, c)


def _cols_to_classes(a, dilation):
    if dilation == 1:
        return a
    r, s = a.shape
    return a.reshape(r, s // dilation, dilation).transpose(0, 2, 1).reshape(r, s)


B_TN = 512
B_GROUP_TILES = 3 * D_MODEL // B_TN
B_Z_TILE0 = 3 * B_GROUP_TILES
B_Z_TILES = D_MODEL // B_TN


def _local_step(x, target, mods, norm_g, conv_w, conv_b, ln_g, ln_b, q_norm, k_norm,
                weights_a, weights_b, send_grads_b, send_grads_a):
    row = lambda a, i: a[i:i + 1]
    shift0, scale0, gate0 = row(mods[0], 0), row(mods[0], 1), row(mods[0], 2)
    shift1, scale1, gate1 = row(mods[1], 0), row(mods[1], 1), row(mods[1], 2)
    g0, g1 = row(norm_g, 0), row(norm_g, 1)
    seg = _seg_matrix()
    slopes = jnp.exp2(-8.0 * jnp.arange(1, N_HEADS + 1, dtype=F32) / N_HEADS)
    qw = [jnp.tile(q_norm[g:g + 1], (1, N_HEADS)) for g in range(3)]
    kw = [jnp.tile(k_norm[g:g + 1], (1, N_HEADS)) for g in range(3)]

    h0, h0t = _normmod_fwd(x, g0, scale0, shift0, "prenorm0")
    wa_in, wa_out = weights_a(h0)
    ja, _, nsa = wa_in.shape
    proj_a = _mm(h0, wa_in, tn=nsa, tile0=0, n_tiles=ja, out_dtype=F32, name="a_in")
    u5, u5t, u2 = _conv_fwd(proj_a, conv_w, conv_b, ln_g, ln_b, "a_conv")
    x1, y_a, h1, h1t = _out_a(u5, wa_out, x, gate0, g1, scale1, shift1, "a_out")

    wb_in, wb_out = weights_b(x1)
    jb, _, nsb = wb_in.shape
    h1c =[_to_classes(h1, d) for d in DILATIONS]
    h1tc = [_cols_to_classes(h1t, d) for d in DILATIONS]
    z_b = _mm(h1, wb_in, tn=B_TN, tile0=B_Z_TILE0, n_tiles=B_Z_TILES, out_dtype=F32, name="b_in_z")
    proj_g, qkv, o_parts, lse_parts = [], [], [], []
    for g, d in enumerate(DILATIONS):
        pg = _mm(h1c[g], wb_in, tn=B_TN, tile0=g * B_GROUP_TILES, n_tiles=B_GROUP_TILES, out_dtype=F32,
                 name=f"b_in_g{g}")
        qn, kn, vn = _qknorm_fwd(pg, qw[g], kw[g], seg, f"b_qknorm_g{g}")
        og, lg = _attn_fwd(qn, kn, vn, slopes, d, f"b_attn_g{g}")
        proj_g.append(pg)
        qkv.append((qn, kn, vn))
        o_parts.append(_from_classes(og, d))
        lse_parts.append(_from_classes(lg, d))
    u_b, u_bt, o_b, lse_b = _merge_fwd(o_parts, lse_parts, z_b, "b_merge")
    e, dy_b, sums_loss = _out_b_loss(u_b, wb_out, x1, gate1, target, "b_out_loss")

    dwb_out = _mm(u_bt, dy_b, tn=D_MODEL, tile0=0, n_tiles=1, out_dtype=BF16, name="b_dwout")
    du_b = _mm_nt_res(dy_b, wb_out, "b_dout")
    do_b, dz_b, delta_b = _merge_bwd(du_b, o_b, z_b, seg, "b_merge_bwd")
    dwb_in = _mm(h1t, dz_b, tn=B_TN, tile0=B_Z_TILE0, n_tiles=B_Z_TILES, out_dtype=BF16, name="b_dwin_z",
                 out3d=(jb, nsb))
    dh1_parts = [_mm_nt(dz_b, wb_in, tn=B_TN, tile0=B_Z_TILE0, n_tiles=B_Z_TILES, name="b_dh_z")]
    qk_sums = []
    for g, d in enumerate(DILATIONS):
        qn, kn, vn = qkv[g]
        dq, dkc, dkp, dvc, dvp = _attn_bwd(qn, kn, vn, _to_classes(do_b, d), _to_classes(lse_b, d),
                                           _to_classes(delta_b, d), slopes, d, f"b_attn_bwd_g{g}")
        dproj, sums_qk = _qknorm_bwd(proj_g[g], qw[g], kw[g], seg, dq, dkc, dkp, dvc, dvp, f"b_qknorm_bwd_g{g}")
        qk_sums.append(sums_qk)
        dwb_in = _mm(h1tc[g], dproj, tn=B_TN, tile0=g * B_GROUP_TILES, n_tiles=B_GROUP_TILES, out_dtype=BF16,
                     name=f"b_dwin_g{g}", out3d=(jb, nsb), prev=dwb_in)
        dh = _mm_nt(dproj, wb_in, tn=B_TN, tile0=g * B_GROUP_TILES, n_tiles=B_GROUP_TILES, name=f"b_dh_g{g}")
        dh1_parts.append(_from_classes(dh, d))
    token = send_grads_b(dwb_in, dwb_out)
    dx1, sums_n1 = _normmod_bwd(x1, g1, scale1 + token[0:1, 0:1], dh1_parts, e, "prenorm1_bwd")

    dy_a, sums_ga = _dgate_dy(dx1, y_a, gate0, "a_dgate")
    dwa_out = _mm(u5t, dy_a, tn=D_MODEL, tile0=0, n_tiles=1, out_dtype=BF16, name="a_dwout")
    du5 = _mm_nt_res(dy_a, wa_out, "a_dout")
    du2, dz_a, sums_ln = _conv_bwd_pointwise(du5, proj_a, u2, ln_g, ln_b, "a_conv_bwd_pw")
    dproj_a, dconv_w = _conv_bwd_taps(du2, dz_a, proj_a, conv_w, "a_conv_bwd_taps")
    dwa_in = _mm(h0t, dproj_a, tn=nsa, tile0=0, n_tiles=ja, out_dtype=BF16, name="a_dwin", out3d=(ja, nsa))
    token = send_grads_a(dwa_in, dwa_out)
    dh0 = _mm_nt(dproj_a, wa_in, tn=nsa, tile0=0, n_tiles=ja, name="a_dh", after=token)
    grad_x, sums_n0 = _normmod_bwd(x, g0, scale0, [dh0], dx1, "prenorm0_bwd")

    small = dict(
        dnorm_g=jnp.concatenate([sums_n0[0:1], sums_n1[0:1]], axis=0),
        dmod0=jnp.concatenate([sums_n0[2:3], sums_n0[1:2], sums_ga[0:1]], axis=0),
        dmod1=jnp.concatenate([sums_n1[2:3], sums_n1[1:2], sums_loss[0:1]], axis=0),
        dln_g=sums_ln[0:1], dln_b=sums_ln[1:2], dconv_b=sums_ln[2:3],
        dconv_w=dconv_w[:CONV_WIDTH],
        dq_norm=jnp.concatenate([s[0:1] for s in qk_sums], axis=0),
        dk_norm=jnp.concatenate([s[1:2] for s in qk_sums], axis=0),
        loss_cols=sums_loss[1:2],
    )
    return grad_x, small


def _adamw(w, g, m, v, name):
    rows, cols = w.shape
    tr = rows if rows <= 128 else 128
    c1 = 1.0 / (1.0 - ADAM_B1 ** ADAM_STEP)
    c2 = 1.0 / (1.0 - ADAM_B2 ** ADAM_STEP)

    def body(w_ref, g_ref, m_ref, v_ref, d_ref, mo_ref, vo_ref):
        gv = g_ref[...]
        mn = ADAM_B1 * m_ref[...] + (1.0 - ADAM_B1) * gv
        vn = ADAM_B2 * v_ref[...] + (1.0 - ADAM_B2) * (gv * gv)
        mo_ref[...] = mn
        vo_ref[...] = vn
        d_ref[...] = -ADAM_LR * ((mn * c1) / (jnp.sqrt(vn * c2) + ADAM_EPS) + ADAM_WD * w_ref[...])

    spec = pl.BlockSpec((tr, cols), lambda i: (i, 0))
    return pl.pallas_call(
        body, name=name, grid=(rows // tr,), in_specs=[spec] * 4, out_specs=[spec] * 3,
        out_shape=[jax.ShapeDtypeStruct((rows, cols), F32)] * 3,
        compiler_params=_params("parallel"),
    )(w, g, m, v)


def _cast_into_slot(w, chip_idx, name):
    rows, cols = w.shape
    tr = 256

    def body(ch_ref, w_ref, o_ref):
        o_ref[...] = w_ref[...].astype(BF16)

    return pl.pallas_call(
        body, name=name,
        grid_spec=pltpu.PrefetchScalarGridSpec(
            num_scalar_prefetch=1, grid=(rows // tr,),
            in_specs=[pl.BlockSpec((tr, cols), lambda i, ch: (i, 0))],
            out_specs=pl.BlockSpec((None, tr, cols), lambda i, ch: (ch[0], i, 0))),
        out_shape=jax.ShapeDtypeStruct((N_CHIPS, rows, cols), BF16), compiler_params=_params("parallel"),
    )(chip_idx, w)


def _position():
    x, y, c = lax.axis_index("x"), lax.axis_index("y"), lax.axis_index("c")
    return x, y, c


def _xor_peer(x, y, c, k):
    return (x ^ ((k >> 2) & 1), y ^ ((k >> 1) & 1), c ^ (k & 1))


def _chip_peer(x, y, k):
    return (x ^ ((k >> 1) & 1), y ^ (k & 1))


def _ada_forward(c_row, ada_w, ada_b, conv_w):
    ns = ada_w.shape[2]
    cw = conv_w.shape[1]

    def body(c_ref, w_ref, b_ref, cv_ref, mod_ref, sc_ref, cvo_ref,
             c_all, mp, parts, cv_parts, send1, recv1, send2, recv2, send3, recv3):
        x, y, c = _position()
        me = 4 * x + 2 * y + c
        chip = 2 * x + y

        def c_copy(k):
            return pltpu.make_async_remote_copy(
                src_ref=c_all.at[me], dst_ref=c_all.at[me], send_sem=send1.at[k - 1], recv_sem=recv1.at[k - 1],
                device_id=_xor_peer(x, y, c, k), device_id_type=MESH)

        def cv_copy(k):
            px, py = _chip_peer(x, y, k)
            return pltpu.make_async_remote_copy(
                src_ref=cv_parts.at[chip], dst_ref=cv_parts.at[chip], send_sem=send3.at[k - 1],
                recv_sem=recv3.at[k - 1], device_id=(px, py, c), device_id_type=MESH)

        c_all[me] = c_ref[...]
        cv_parts[chip] = cv_ref[...]
        for k in range(1, N_DEV):
            c_copy(k).start()
        for k in range(1, N_CHIPS):
            cv_copy(k).start()
        for k in range(1, N_DEV):
            c_copy(k).wait_recv()
        cv = jnp.concatenate([c_all[i] for i in range(N_DEV)], axis=0)
        sc = cv * _sigmoid(cv)
        sc_ref[...] = sc
        for l in range(2):
            res = jnp.dot(sc, w_ref[l], preferred_element_type=F32, precision=lax.Precision.HIGHEST)
            for i in range(N_DEV):
                mp[i, l:l + 1, :] = res[i:i + 1, :]

        def mod_copy(k):
            px, py = _chip_peer(x, y, k)
            return pltpu.make_async_remote_copy(
                src_ref=mp.at[4 * px + 2 * py + c], dst_ref=parts.at[chip], send_sem=send2.at[k - 1],
                recv_sem=recv2.at[k - 1], device_id=(px, py, c), device_id_type=MESH)

        for k in range(1, N_CHIPS):
            mod_copy(k).start()
        parts[chip] = mp[me]
        for k in range(1, N_CHIPS):
            mod_copy(k).wait_recv()
            cv_copy(k).wait_recv()
        mod_ref[...] = jnp.concatenate([parts[j] for j in range(N_CHIPS)], axis=1) + b_ref[...]
        cvo_ref[...] = jnp.concatenate([cv_parts[j] for j in range(N_CHIPS)], axis=1)
        for k in range(1, N_DEV):
            c_copy(k).wait_send()
        for k in range(1, N_CHIPS):
            mod_copy(k).wait_send()
            cv_copy(k).wait_send()

    vm = pl.BlockSpec(memory_space=pltpu.VMEM)
    return pl.pallas_call(
        body, name="ada_forward",
        in_specs=[vm] * 4, out_specs=[vm] * 3,
        out_shape=[jax.ShapeDtypeStruct((2, 3 * D_MODEL), F32), jax.ShapeDtypeStruct((N_DEV, D_MODEL), F32),
                   jax.ShapeDtypeStruct((CONV_WIDTH, N_CHIPS * cw), F32)],
        scratch_shapes=[pltpu.VMEM((N_DEV, 1, D_MODEL), F32), pltpu.VMEM((N_DEV, 2, ns), F32),
                        pltpu.VMEM((N_CHIPS, 2, ns), F32), pltpu.VMEM((N_CHIPS, CONV_WIDTH, cw), F32),
                        pltpu.SemaphoreType.DMA((N_DEV - 1,)), pltpu.SemaphoreType.DMA((N_DEV - 1,)),
                        pltpu.SemaphoreType.DMA((N_CHIPS - 1,)), pltpu.SemaphoreType.DMA((N_CHIPS - 1,)),
                        pltpu.SemaphoreType.DMA((N_CHIPS - 1,)), pltpu.SemaphoreType.DMA((N_CHIPS - 1,))],
        compiler_params=pltpu.CompilerParams(vmem_limit_bytes=VMEM_LIMIT_BYTES),
    )(c_row, ada_w, ada_b, conv_w)


HBM_SPEC = pl.BlockSpec(memory_space=pltpu.HBM)
ANY_SPEC = pl.BlockSpec(memory_space=pl.ANY)
SEM_SPEC = pl.BlockSpec(memory_space=pltpu.SEMAPHORE)
SPLIT_PARAMS = dict(compiler_params=pltpu.CompilerParams(has_side_effects=pltpu.SideEffectType.DATAFLOW_SIDE_EFFECTING))
TOKEN = jax.ShapeDtypeStruct((8, 128), F32)


def _hbm(arrays):
    return [pltpu.with_memory_space_constraint(a, pltpu.HBM) for a in arrays]


def _hbm_like(arrays):
    return [pltpu.HBM(a.shape, a.dtype) for a in arrays]


def _gather_start(lands, after, name):
    n = len(lands)

    def body(*refs):
        ins = refs[:n]
        send, recv = refs[n + 1], refs[n + 2]
        x, y, c = _position()
        chip = 2 * x + y
        for t in range(n):
            for k in range(1, N_CHIPS):
                px, py = _chip_peer(x, y, k)
                block = ins[t].at[chip]
                pltpu.make_async_remote_copy(
                    src_ref=block, dst_ref=block, send_sem=send.at[3 * t + k - 1], recv_sem=recv.at[3 * t + k - 1],
                    device_id=(px, py, c), device_id_type=MESH).start()
        refs[-1][...] = jnp.zeros(TOKEN.shape, F32)

    res = pl.pallas_call(
        body, name=name, in_specs=[HBM_SPEC] * n + [ANY_SPEC],
        out_specs=(SEM_SPEC, SEM_SPEC, *[HBM_SPEC] * n, pl.BlockSpec(memory_space=pltpu.VMEM)),
        out_shape=(pltpu.SemaphoreType.DMA((3 * n,)), pltpu.SemaphoreType.DMA((3 * n,)), *_hbm_like(lands), TOKEN),
        input_output_aliases={t: 2 + t for t in range(n)}, **SPLIT_PARAMS,
    )(*_hbm(lands), after)
    return res[0], res[1], list(res[2:2 + n]), res[-1]


def _gather_wait(send, recv, lands, after, name):
    n = len(lands)

    def body(*refs):
        ins = refs[:n]
        send_ref, recv_ref = refs[n], refs[n + 1]
        x, y, c = _position()
        chip = 2 * x + y
        for t in range(n):
            for k in range(1, N_CHIPS):
                px, py = _chip_peer(x, y, k)
                cp = pltpu.make_async_remote_copy(
                    src_ref=ins[t].at[chip], dst_ref=ins[t].at[2 * px + py], send_sem=send_ref.at[3 * t + k - 1],
                    recv_sem=recv_ref.at[3 * t + k - 1], device_id=(px, py, c), device_id_type=MESH)
                cp.wait_send()
                cp.wait_recv()

    res = pl.pallas_call(
        body, name=name, in_specs=[HBM_SPEC] * n + [SEM_SPEC, SEM_SPEC, ANY_SPEC], out_specs=[HBM_SPEC] * n,
        out_shape=_hbm_like(lands), input_output_aliases={t: t for t in range(n)}, **SPLIT_PARAMS,
    )(*lands, send, recv, after)
    return list(res)


def _reduce_start(grads, after, name):
    n = len(grads)
    lands = [lax.empty((N_DEV, g.shape[1] // 2, g.shape[2]), BF16) for g in grads]

    def body(*refs):
        gs, ls = refs[:n], refs[n:2 * n]
        send, recv = refs[2 * n + 1], refs[2 * n + 2]
        x, y, c = _position()
        me = 4 * x + 2 * y + c
        for t in range(n):
            rh = gs[t].shape[1] // 2
            for k in range(1, N_DEV):
                px, py, pc = _xor_peer(x, y, c, k)
                pltpu.make_async_remote_copy(
                    src_ref=gs[t].at[2 * px + py, pl.ds(pc * rh, rh)], dst_ref=ls[t].at[me],
                    send_sem=send.at[7 * t + k - 1], recv_sem=recv.at[7 * t + k - 1],
                    device_id=(px, py, pc), device_id_type=MESH).start()
        refs[-1][...] = jnp.zeros(TOKEN.shape, F32)

    res = pl.pallas_call(
        body, name=name, in_specs=[HBM_SPEC] * (2 * n) + [ANY_SPEC],
        out_specs=(SEM_SPEC, SEM_SPEC, *[HBM_SPEC] * (2 * n), pl.BlockSpec(memory_space=pltpu.VMEM)),
        out_shape=(pltpu.SemaphoreType.DMA((7 * n,)), pltpu.SemaphoreType.DMA((7 * n,)),
                   *_hbm_like(grads), *_hbm_like(lands), TOKEN),
        input_output_aliases={t: 2 + t for t in range(2 * n)}, **SPLIT_PARAMS,
    )(*_hbm(grads), *_hbm(lands), after)
    return res[0], res[1], list(res[2:2 + n]), list(res[2 + n:2 + 2 * n]), res[-1]


def _reduce_wait(send, recv, grads, lands, after, name):
    n = len(grads)

    def body(*refs):
        gs, ls = refs[:n], refs[n:2 * n]
        send_ref, recv_ref = refs[2 * n], refs[2 * n + 1]
        x, y, c = _position()
        for t in range(n):
            rh = gs[t].shape[1] // 2
            for k in range(1, N_DEV):
                px, py, pc = _xor_peer(x, y, c, k)
                cp = pltpu.make_async_remote_copy(
                    src_ref=gs[t].at[2 * px + py, pl.ds(pc * rh, rh)], dst_ref=ls[t].at[4 * px + 2 * py + pc],
                    send_sem=send_ref.at[7 * t + k - 1], recv_sem=recv_ref.at[7 * t + k - 1],
                    device_id=(px, py, pc), device_id_type=MESH)
                cp.wait_send()
                cp.wait_recv()

    res = pl.pallas_call(
        body, name=name, in_specs=[HBM_SPEC] * (2 * n) + [SEM_SPEC, SEM_SPEC, ANY_SPEC], out_specs=[HBM_SPEC] * (2 * n),
        out_shape=_hbm_like(grads) + _hbm_like(lands), input_output_aliases={t: t for t in range(2 * n)}, **SPLIT_PARAMS,
    )(*grads, *lands, send, recv, after)
    return list(res[:n]), list(res[n:])


def _sum_devices(land, grad, dev_idx, name):
    _, rh, cols = land.shape
    tr = 128
    nb = rh // tr

    def body(idx_ref, l_ref, g_ref, o_ref):
        me = idx_ref[0]
        acc = jnp.where(me == 0, g_ref[...], l_ref[0]).astype(F32)
        for d in range(1, N_DEV):
            acc = acc + jnp.where(me == d, g_ref[...], l_ref[d]).astype(F32)
        o_ref[...] = acc

    return pl.pallas_call(
        body, name=name,
        grid_spec=pltpu.PrefetchScalarGridSpec(
            num_scalar_prefetch=1, grid=(nb,),
            in_specs=[pl.BlockSpec((N_DEV, tr, cols), lambda i, idx: (0, i, 0)),
                      pl.BlockSpec((None, tr, cols), lambda i, idx: (idx[1], idx[2] * nb + i, 0))],
            out_specs=pl.BlockSpec((tr, cols), lambda i, idx: (idx[2] * nb + i, 0))),
        out_shape=jax.ShapeDtypeStruct((2 * rh, cols), F32), compiler_params=_params("parallel"),
    )(dev_idx, land, grad)


def _share_halves(totals):
    n = len(totals)

    def body(*refs):
        ins, outs = refs[:n], refs[n:2 * n]
        send, recv = refs[2 * n:]
        x, y, c = _position()
        cps = []
        for t in range(n):
            rh = ins[t].shape[0] // 2
            mine = pl.ds(c * rh, rh)
            cp = pltpu.make_async_remote_copy(
                src_ref=ins[t].at[mine], dst_ref=outs[t].at[mine], send_sem=send.at[t], recv_sem=recv.at[t],
                device_id=(x, y, 1 - c), device_id_type=MESH)
            cp.start()
            cps.append(cp)
        for cp in cps:
            cp.wait()

    return pl.pallas_call(
        body, name="reduce_share_" + "_".join(str(t.shape[1]) for t in totals), in_specs=[ANY_SPEC] * n,
        out_specs=[ANY_SPEC] * n, out_shape=[jax.ShapeDtypeStruct(t.shape, F32) for t in totals],
        input_output_aliases={t: t for t in range(n)},
        scratch_shapes=[pltpu.SemaphoreType.DMA((n,)), pltpu.SemaphoreType.DMA((n,))],
    )(*totals)


def _exchange_halves(grads):
    n = len(grads)
    hbm = pl.BlockSpec(memory_space=pl.ANY)

    def body(*refs):
        ins, outs = refs[:n], refs[n:2 * n]
        send, recv = refs[2 * n:]
        x, y, c = _position()
        cps = []
        for t in range(n):
            rh = ins[t].shape[1] // 2
            cp = pltpu.make_async_remote_copy(
                src_ref=ins[t].at[pl.ds(0, N_CHIPS), pl.ds((1 - c) * rh, rh)], dst_ref=outs[t], send_sem=send.at[t],
                recv_sem=recv.at[t], device_id=(x, y, 1 - c), device_id_type=MESH)
            cp.start()
            cps.append(cp)
        for cp in cps:
            cp.wait()

    return pl.pallas_call(
        body, name="reduce_exchange_halves", in_specs=[hbm] * n, out_specs=[hbm] * n,
        out_shape=[jax.ShapeDtypeStruct((g.shape[0], g.shape[1] // 2, g.shape[2]), BF16) for g in grads],
        scratch_shapes=[pltpu.SemaphoreType.DMA((n,)), pltpu.SemaphoreType.DMA((n,))],
    )(*grads)


def _add_halves(grad, got, c_idx, name):
    j, r, cols = grad.shape
    rh = r // 2
    tr = 128
    nb = rh // tr

    def body(c_ref, g_ref, o_ref_in, out_ref):
        out_ref[...] = (g_ref[...].astype(F32) + o_ref_in[...].astype(F32)).astype(BF16)

    return pl.pallas_call(
        body, name=name,
        grid_spec=pltpu.PrefetchScalarGridSpec(
            num_scalar_prefetch=1, grid=(j, nb),
            in_specs=[pl.BlockSpec((None, tr, cols), lambda jj, i, c_ref: (jj, c_ref[0] * nb + i, 0)),
                      pl.BlockSpec((None, tr, cols), lambda jj, i, c_ref: (jj, i, 0))],
            out_specs=pl.BlockSpec((None, tr, cols), lambda jj, i, c_ref: (jj, i, 0))),
        out_shape=jax.ShapeDtypeStruct((j, rh, cols), BF16),
        compiler_params=_params("parallel", "parallel"),
    )(c_idx, grad, got)


def _scatter_partials(partials):
    n = len(partials)
    hbm = pl.BlockSpec(memory_space=pl.ANY)

    def body(*refs):
        ins, outs = refs[:n], refs[n:2 * n]
        send, recv, local = refs[2 * n:]
        x, y, c = _position()
        chip = 2 * x + y
        cps, lcs = [], []
        for t in range(n):
            lc = pltpu.make_async_copy(ins[t].at[chip], outs[t].at[chip], local.at[t])
            lc.start()
            lcs.append(lc)
            for k in range(1, N_CHIPS):
                px, py = _chip_peer(x, y, k)
                s = 3 * t + k - 1
                cp = pltpu.make_async_remote_copy(
                    src_ref=ins[t].at[2 * px + py], dst_ref=outs[t].at[chip], send_sem=send.at[s],
                    recv_sem=recv.at[s], device_id=(px, py, c), device_id_type=MESH)
                cp.start()
                cps.append(cp)
        for cp in cps:
            cp.wait()
        for lc in lcs:
            lc.wait()

    return pl.pallas_call(
        body, name="reduce_scatter_partials", in_specs=[hbm] * n, out_specs=[hbm] * n,
        out_shape=[jax.ShapeDtypeStruct(p.shape, BF16) for p in partials],
        scratch_shapes=[pltpu.SemaphoreType.DMA((3 * n,)), pltpu.SemaphoreType.DMA((3 * n,)),
                        pltpu.SemaphoreType.DMA((n,))],
    )(*partials)


def _sum_chips(parts, name):
    j, rh, cols = parts.shape
    tr = 128

    def body(p_ref, o_ref):
        acc = p_ref[0].astype(F32)
        for s in range(1, j):
            acc = acc + p_ref[s].astype(F32)
        o_ref[...] = acc

    return pl.pallas_call(
        body, name=name, grid=(rh // tr,),
        in_specs=[pl.BlockSpec((j, tr, cols), lambda i: (0, i, 0))],
        out_specs=pl.BlockSpec((tr, cols), lambda i: (i, 0)),
        out_shape=jax.ShapeDtypeStruct((rh, cols), F32),
        compiler_params=_params("parallel"),
    )(parts)


def _share_totals(halves):
    n = len(halves)
    hbm = pl.BlockSpec(memory_space=pl.ANY)

    def body(*refs):
        ins, outs = refs[:n], refs[n:2 * n]
        send, recv, local = refs[2 * n:]
        x, y, c = _position()
        cps, lcs = [], []
        for t in range(n):
            rh = ins[t].shape[0]
            mine = outs[t].at[pl.ds(c * rh, rh)]
            lc = pltpu.make_async_copy(ins[t], mine, local.at[t])
            lc.start()
            lcs.append(lc)
            cp = pltpu.make_async_remote_copy(
                src_ref=ins[t], dst_ref=mine, send_sem=send.at[t], recv_sem=recv.at[t],
                device_id=(x, y, 1 - c), device_id_type=MESH)
            cp.start()
            cps.append(cp)
        for cp in cps:
            cp.wait()
        for lc in lcs:
            lc.wait()

    return pl.pallas_call(
        body, name="reduce_share_totals", in_specs=[hbm] * n, out_specs=[hbm] * n,
        out_shape=[jax.ShapeDtypeStruct((2 * h.shape[0], h.shape[1]), F32) for h in halves],
        scratch_shapes=[pltpu.SemaphoreType.DMA((n,)), pltpu.SemaphoreType.DMA((n,)),
                        pltpu.SemaphoreType.DMA((n,))],
    )(*halves)


SMALL_ROWS = 56


def _reduce_small(packed, silu_c):
    ns = 3 * D_MODEL // N_CHIPS

    def body(p_ref, sc_ref, tot_ref, gw_ref, loss_ref, qk_ref, allp, send, recv):
        x, y, c = _position()
        me = 4 * x + 2 * y + c
        chip = 2 * x + y

        def copy(k):
            return pltpu.make_async_remote_copy(
                src_ref=allp.at[me], dst_ref=allp.at[me], send_sem=send.at[k - 1], recv_sem=recv.at[k - 1],
                device_id=_xor_peer(x, y, c, k), device_id_type=MESH)

        allp[me] = p_ref[...]
        for k in range(1, N_DEV):
            copy(k).start()
        for k in range(1, N_DEV):
            copy(k).wait_recv()
        tot = allp[0]
        for i in range(1, N_DEV):
            tot = tot + allp[i]
        tot_ref[...] = tot
        loss_ref[...] = jnp.sum(tot[11:12, :], axis=1, keepdims=True) * (0.5 / D_MODEL)
        fold = tot[5:11, 0:HEAD_DIM]
        for h in range(1, N_HEADS):
            fold = fold + tot[5:11, h * HEAD_DIM:(h + 1) * HEAD_DIM]
        qk_ref[...] = jnp.concatenate([fold, jnp.zeros((2, HEAD_DIM), F32)], axis=0)
        sct = sc_ref[...].T
        rc = 64
        for l in range(2):
            dms = [allp[i, pl.ds(12 + 4 * l + chip, 1), :][:, :ns] for i in range(N_DEV)]
            for r0 in range(0, D_MODEL, rc):
                acc = sct[r0:r0 + rc, 0:1] * dms[0]
                for i in range(1, N_DEV):
                    acc = acc + sct[r0:r0 + rc, i:i + 1] * dms[i]
                gw_ref[l, r0:r0 + rc, :] = acc
        for k in range(1, N_DEV):
            copy(k).wait_send()

    vm = pl.BlockSpec(memory_space=pltpu.VMEM)
    return pl.pallas_call(
        body, name="reduce_small", in_specs=[vm, vm], out_specs=[vm] * 4,
        out_shape=[jax.ShapeDtypeStruct((SMALL_ROWS, D_MODEL), F32), jax.ShapeDtypeStruct((2, D_MODEL, ns), F32),
                   jax.ShapeDtypeStruct((1, 1), F32), jax.ShapeDtypeStruct((8, HEAD_DIM), F32)],
        scratch_shapes=[pltpu.VMEM((N_DEV, SMALL_ROWS, D_MODEL), F32),
                        pltpu.SemaphoreType.DMA((N_DEV - 1,)), pltpu.SemaphoreType.DMA((N_DEV - 1,))],
        compiler_params=pltpu.CompilerParams(vmem_limit_bytes=VMEM_LIMIT_BYTES),
    )(packed, silu_c)


def _reduce_big(grads, c_idx):
    names = list(grads)
    got = _exchange_halves([grads[k] for k in names])
    partials = [_add_halves(grads[k], got[i], c_idx, f"reduce_add_{k}") for i, k in enumerate(names)]
    parts = _scatter_partials(partials)
    halves = [_sum_chips(parts[i], f"reduce_sum_{k}") for i, k in enumerate(names)]
    totals = _share_totals(halves)
    return dict(zip(names, totals))


def kernel(x, c, norm_g, ada_w, ada_b, a_w_in, a_conv_w, a_conv_b, a_ln_g, a_ln_b, a_w_out, b_w_in, b_q_norm, b_k_norm, b_w_out, loss_target, m_norm_g, m_ada_w, m_ada_b, m_a_w_in, m_a_conv_w, m_a_conv_b, m_a_ln_g, m_a_ln_b, m_a_w_out, m_b_w_in, m_b_q_norm, m_b_k_norm, m_b_w_out, v_norm_g, v_ada_w, v_ada_b, v_a_w_in, v_a_conv_w, v_a_conv_b, v_a_ln_g, v_a_ln_b, v_a_w_out, v_b_w_in, v_b_q_norm, v_b_k_norm, v_b_w_out):
    chip = 2 * lax.axis_index("x") + lax.axis_index("y")
    core = lax.axis_index("c")
    chip_idx = chip.astype(jnp.int32).reshape(1)
    dev_idx = jnp.stack([2 * chip + core, chip, core]).astype(jnp.int32)

    mods, silu_c, conv_w_full = _ada_forward(c, ada_w, ada_b, a_conv_w[0])
    lands_a = [_cast_into_slot(a_w_in[0], chip_idx, "cast_a_w_in"), _cast_into_slot(a_w_out[0], chip_idx, "cast_a_w_out")]
    send_a, recv_a, lands_a, token_a = _gather_start(lands_a, mods, "gather_start_a")
    lands_b = [_cast_into_slot(b_w_in[0], chip_idx, "cast_b_w_in"), _cast_into_slot(b_w_out[0], chip_idx, "cast_b_w_out")]
    send_b, recv_b, lands_b, token_b = _gather_start(lands_b, token_a, "gather_start_b")
    mods = mods + token_b[0:2, 0:1]

    def weights_a(after):
        w_in, w_out = _gather_wait(send_a, recv_a, lands_a, after, "gather_wait_a")
        return w_in, w_out.reshape(D_MODEL, D_MODEL)

    def weights_b(after):
        w_in, w_out = _gather_wait(send_b, recv_b, lands_b, after, "gather_wait_b")
        return w_in, w_out.reshape(D_MODEL, D_MODEL)

    in_flight = {}

    def send_grads(tag, dw_in, dw_out):
        grads = [dw_in, dw_out.reshape(N_CHIPS, D_MODEL // N_CHIPS, D_MODEL)]
        send, recv, grads, lands, token = _reduce_start(grads, dw_out, f"reduce_start_{tag}")
        in_flight[tag] = (send, recv, grads, lands)
        return token

    def finish_grads(tag, after):
        send, recv, grads, lands = in_flight[tag]
        grads, lands = _reduce_wait(send, recv, grads, lands, after, f"reduce_wait_{tag}")
        totals = [_sum_devices(lands[i], grads[i], dev_idx, f"reduce_sum_{tag}_{i}") for i in range(2)]
        return _share_halves(totals)

    grad_x, small = _local_step(
        x[0], loss_target[0], mods.reshape(2, 3, D_MODEL), norm_g, conv_w_full, a_conv_b, a_ln_g[0:1],
        a_ln_b[0:1], b_q_norm[0], b_k_norm[0], weights_a, weights_b,
        functools.partial(send_grads, "b"), functools.partial(send_grads, "a"))

    ns = 3 * D_MODEL // N_CHIPS
    pad_mod = lambda dm: jnp.pad(dm.reshape(N_CHIPS, ns), ((0, 0), (0, D_MODEL - ns)))
    packed = jnp.concatenate([
        small["dnorm_g"], small["dconv_b"], small["dln_g"], small["dln_b"], small["dq_norm"], small["dk_norm"],
        small["loss_cols"], pad_mod(small["dmod0"]), pad_mod(small["dmod1"]), small["dconv_w"],
        jnp.zeros((SMALL_ROWS - 20 - CONV_WIDTH, D_MODEL), F32)], axis=0)
    tot, g_ada_w, loss, qk = _reduce_small(packed, silu_c)
    cw = D_MODEL // N_CHIPS
    g_small = dict(
        norm_g=tot[0:2], a_conv_b=tot[2:3], a_ln_g=tot[3:4], a_ln_b=tot[4:5],
        b_q_norm=qk[0:3], b_k_norm=qk[3:6],
        ada_b=jnp.stack([tot[12:16, :ns].reshape(3 * D_MODEL), tot[16:20, :ns].reshape(3 * D_MODEL)]),
        a_conv_w=lax.dynamic_slice(tot[20:20 + CONV_WIDTH], (0, chip * cw), (CONV_WIDTH, cw)),
    )


    given = dict(norm_g=(norm_g, m_norm_g, v_norm_g), ada_w=(ada_w, m_ada_w, v_ada_w), ada_b=(ada_b, m_ada_b, v_ada_b),
                 a_w_in=(a_w_in, m_a_w_in, v_a_w_in), a_conv_w=(a_conv_w, m_a_conv_w, v_a_conv_w),
                 a_conv_b=(a_conv_b, m_a_conv_b, v_a_conv_b), a_ln_g=(a_ln_g, m_a_ln_g, v_a_ln_g),
                 a_ln_b=(a_ln_b, m_a_ln_b, v_a_ln_b), a_w_out=(a_w_out, m_a_w_out, v_a_w_out),
                 b_w_in=(b_w_in, m_b_w_in, v_b_w_in), b_q_norm=(b_q_norm, m_b_q_norm, v_b_q_norm),
                 b_k_norm=(b_k_norm, m_b_k_norm, v_b_k_norm), b_w_out=(b_w_out, m_b_w_out, v_b_w_out))
    order = ["norm_g", "ada_w", "ada_b", "a_w_in", "a_conv_w", "a_conv_b", "a_ln_g", "a_ln_b", "a_w_out", "b_w_in",
             "b_q_norm", "b_k_norm", "b_w_out"]
    outs = {}

    def update(k, g2):
        w, m, v = given[k]
        shape2 = g2.shape
        d2, m2, v2 = _adamw(w.reshape(shape2), g2, m.reshape(shape2), v.reshape(shape2), f"adamw_{k}")
        outs[k] = tuple(a.reshape(w.shape) for a in (g2, d2, m2, v2))

    g_b_in, g_b_out = finish_grads("b", tot)
    update("b_w_in", g_b_in)
    update("b_w_out", g_b_out)
    update("ada_w", g_ada_w.reshape(2 * D_MODEL, ns))
    for k, g2 in g_small.items():
        update(k, g2)
    g_a_in, g_a_out = finish_grads("a", outs["b_w_in"][1])
    update("a_w_in", g_a_in)
    update("a_w_out", g_a_out)
    return (loss.reshape(()), grad_x[None], *[outs[k][0] for k in order], *[outs[k][1] for k in order],
            *[outs[k][2] for k in order], *[outs[k][3] for k in order])
```

```python
import functools

import jax
import jax.numpy as jnp
from jax import lax
from jax.experimental import pallas as pl
from jax.experimental.pallas import tpu as pltpu

F32 = jnp.float32
BF16 = jnp.bfloat16

SEQ = 2048
D_MODEL = 1024
CONV_WIDTH = 31
HEAD_DIM = 64
N_HEADS = 16
DILATIONS = (1, 4, 16)
ATTN_BLOCK = 128
NORM_EPS = 1e-6
NEG_INF = -1e30
N_DEV = 8
N_CHIPS = 4

ADAM_LR = 0.001
ADAM_B1 = 0.9
ADAM_B2 = 0.999
ADAM_EPS = 1e-08
ADAM_WD = 0.01
ADAM_STEP = 10

VMEM_LIMIT_BYTES = 52 * 1024 * 1024
HALO = 32
MESH = pl.DeviceIdType.MESH


def _params(*sem):
    return pltpu.CompilerParams(dimension_semantics=sem or None, vmem_limit_bytes=VMEM_LIMIT_BYTES)


def _sigmoid(v):
    return 1.0 / (1.0 + jnp.exp(-v))


def _row_spec(tm, cols, col_block=0):
    return pl.BlockSpec((tm, cols), lambda i: (i, col_block))


def _vec_spec(rows, cols):
    return pl.BlockSpec((rows, cols), lambda i: (0, 0))


def _normmod(xv, g, scale, shift):
    r = lax.rsqrt(jnp.mean(xv * xv, axis=-1, keepdims=True) + NORM_EPS)
    return xv * r * g * (1.0 + scale) + shift


def _normmod_fwd(x, g, scale, shift, name):
    tm = 256

    def body(x_ref, g_ref, sc_ref, sh_ref, h_ref, ht_ref):
        h = _normmod(x_ref[...], g_ref[...], sc_ref[...], sh_ref[...])
        h_ref[...] = h.astype(BF16)
        ht_ref[...] = h.T.astype(BF16)

    return pl.pallas_call(
        body, name=name, grid=(SEQ // tm,),
        in_specs=[_row_spec(tm, D_MODEL)] + [_vec_spec(1, D_MODEL)] * 3,
        out_specs=[_row_spec(tm, D_MODEL), pl.BlockSpec((D_MODEL, tm), lambda i: (0, i))],
        out_shape=[jax.ShapeDtypeStruct((SEQ, D_MODEL), BF16), jax.ShapeDtypeStruct((D_MODEL, SEQ), BF16)],
        compiler_params=_params("parallel"),
    )(x, g, scale, shift)


def _normmod_bwd(x, g, scale, dh_parts, dres, name):
    tm = 256
    n_parts = len(dh_parts)

    def body(x_ref, g_ref, sc_ref, dres_ref, *rest):
        part_refs = rest[:n_parts]
        dx_ref, sums_ref = rest[n_parts:]
        xv = x_ref[...]
        r = lax.rsqrt(jnp.mean(xv * xv, axis=-1, keepdims=True) + NORM_EPS)
        xn = xv * r
        dh = part_refs[0][...]
        for p in part_refs[1:]:
            dh = dh + p[...]
        gv = g_ref[...]
        one_sc = 1.0 + sc_ref[...]
        dxn = dh * (gv * one_sc)
        dx = r * (dxn - xn * jnp.mean(dxn * xn, axis=-1, keepdims=True))
        dx_ref[...] = dres_ref[...] + dx
        dhx = dh * xn
        sums = jnp.concatenate([
            jnp.sum(dhx, axis=0, keepdims=True) * one_sc,
            jnp.sum(dhx, axis=0, keepdims=True) * gv,
            jnp.sum(dh, axis=0, keepdims=True),
            jnp.zeros((5, D_MODEL), F32)], axis=0)

        @pl.when(pl.program_id(0) == 0)
        def _():
            sums_ref[...] = jnp.zeros_like(sums_ref)

        sums_ref[...] += sums

    return pl.pallas_call(
        body, name=name, grid=(SEQ // tm,),
        in_specs=[_row_spec(tm, D_MODEL), _vec_spec(1, D_MODEL), _vec_spec(1, D_MODEL), _row_spec(tm, D_MODEL)]
        + [_row_spec(tm, D_MODEL)] * n_parts,
        out_specs=[_row_spec(tm, D_MODEL), _vec_spec(8, D_MODEL)],
        out_shape=[jax.ShapeDtypeStruct((SEQ, D_MODEL), F32), jax.ShapeDtypeStruct((8, D_MODEL), F32)],
        compiler_params=_params("arbitrary"),
    )(x, g, scale, dres, *dh_parts)


def _mm(lhs, rhs, *, tn, tile0, n_tiles, out_dtype, name, out3d=None, prev=None):
    mo, kc = lhs.shape
    cm = 512

    def body(l_ref, r_ref, *rest):
        o_ref = rest[-1]
        for m in range(mo // cm):
            rows = pl.ds(m * cm, cm)
            o_ref[rows, :] = jnp.dot(l_ref[rows, :], r_ref[...], preferred_element_type=F32).astype(out_dtype)

    if rhs.ndim == 3:
        tps_r = rhs.shape[2] // tn
        r_spec = pl.BlockSpec((None, kc, tn), lambda t: ((tile0 + t) // tps_r, 0, (tile0 + t) % tps_r))
    else:
        r_spec = pl.BlockSpec((kc, tn), lambda t: (0, t))
    in_specs = [pl.BlockSpec((mo, kc), lambda t: (0, 0)), r_spec]
    args = [lhs, rhs]
    aliases = {}
    if out3d is None:
        o_spec = pl.BlockSpec((mo, tn), lambda t: (0, t))
        o_shape = jax.ShapeDtypeStruct((mo, n_tiles * tn), out_dtype)
    else:
        j_out, ns_out = out3d
        tps_o = ns_out // tn
        o_spec = pl.BlockSpec((None, mo, tn), lambda t: ((tile0 + t) // tps_o, 0, (tile0 + t) % tps_o))
        o_shape = jax.ShapeDtypeStruct((j_out, mo, ns_out), out_dtype)
        if prev is not None:
            in_specs.append(pl.BlockSpec(memory_space=pl.ANY))
            args.append(prev)
            aliases = {2: 0}
    return pl.pallas_call(
        body, name=name, grid=(n_tiles,), in_specs=in_specs, out_specs=o_spec, out_shape=o_shape,
        input_output_aliases=aliases, compiler_params=_params("parallel"),
    )(*args)


def _mm_nt(dy, w3, *, tn, tile0, n_tiles, name, after=None):
    m_rows = dy.shape[0]
    _, kc, ns = w3.shape
    tps = ns // tn
    cm = 512
    extra = [] if after is None else [after]

    def body(dy_ref, w_ref, *rest):
        o_ref = rest[-1]

        @pl.when(pl.program_id(0) == 0)
        def _():
            o_ref[...] = jnp.zeros_like(o_ref)

        for m in range(m_rows // cm):
            rows = pl.ds(m * cm, cm)
            o_ref[rows, :] += lax.dot_general(dy_ref[rows, :], w_ref[...], (((1,), (1,)), ((), ())),
                                              preferred_element_type=F32)

    return pl.pallas_call(
        body, name=name, grid=(n_tiles,),
        in_specs=[pl.BlockSpec((m_rows, tn), lambda t: (0, t)),
                  pl.BlockSpec((None, kc, tn), lambda t: ((tile0 + t) // tps, 0, (tile0 + t) % tps))]
        + [pl.BlockSpec(memory_space=pl.ANY)] * len(extra),
        out_specs=pl.BlockSpec((m_rows, kc), lambda t: (0, 0)),
        out_shape=jax.ShapeDtypeStruct((m_rows, kc), F32),
        compiler_params=_params("arbitrary"),
    )(dy, w3, *extra)


def _conv_fwd(proj, conv_w, conv_b, ln_g, ln_b, name):
    tm = 256
    hb = tm // HALO

    def body(vg_ref, halo_ref, z_ref, w_ref, b_ref, g_ref, be_ref, u5_ref, u5t_ref, u2_ref, buf):
        i = pl.program_id(0)
        u1 = vg_ref[:, :D_MODEL] * _sigmoid(vg_ref[:, D_MODEL:])
        u1h = halo_ref[:, :D_MODEL] * _sigmoid(halo_ref[:, D_MODEL:])
        buf[pl.ds(0, HALO), :] = jnp.where(i > 0, u1h, 0.0)
        buf[pl.ds(HALO, tm), :] = u1
        acc = jnp.zeros((tm, D_MODEL), F32) + b_ref[...]
        for k in range(CONV_WIDTH):
            acc = acc + w_ref[k:k + 1, :] * buf[pl.ds(HALO - (CONV_WIDTH - 1) + k, tm), :]
        u2_ref[...] = acc
        mu = jnp.mean(acc, axis=-1, keepdims=True)
        xc = acc - mu
        rstd = lax.rsqrt(jnp.mean(xc * xc, axis=-1, keepdims=True) + NORM_EPS)
        u3 = xc * rstd * g_ref[...] + be_ref[...]
        zv = z_ref[...]
        u5 = u3 * _sigmoid(u3) * (zv * _sigmoid(zv))
        u5_ref[...] = u5.astype(BF16)
        u5t_ref[...] = u5.T.astype(BF16)

    return pl.pallas_call(
        body, name=name, grid=(SEQ // tm,),
        in_specs=[pl.BlockSpec((tm, 2 * D_MODEL), lambda i: (i, 0)),
                  pl.BlockSpec((HALO, 2 * D_MODEL), lambda i: (jnp.maximum(i * hb - 1, 0), 0)),
                  _row_spec(tm, D_MODEL, 2),
                  _vec_spec(CONV_WIDTH, D_MODEL)] + [_vec_spec(1, D_MODEL)] * 3,
        out_specs=[_row_spec(tm, D_MODEL), pl.BlockSpec((D_MODEL, tm), lambda i: (0, i)), _row_spec(tm, D_MODEL)],
        out_shape=[jax.ShapeDtypeStruct((SEQ, D_MODEL), BF16), jax.ShapeDtypeStruct((D_MODEL, SEQ), BF16),
                   jax.ShapeDtypeStruct((SEQ, D_MODEL), F32)],
        scratch_shapes=[pltpu.VMEM((HALO + tm, D_MODEL), F32)],
        compiler_params=_params("parallel"),
    )(proj, proj, proj, conv_w, conv_b, ln_g, ln_b)


def _conv_bwd_pointwise(du5, proj, u2, ln_g, ln_b, name):
    tm = 256

    def body(du5_ref, z_ref, u2_ref, g_ref, be_ref, du2_ref, dz_ref, sums_ref):
        u2v = u2_ref[...]
        mu = jnp.mean(u2v, axis=-1, keepdims=True)
        xc = u2v - mu
        rstd = lax.rsqrt(jnp.mean(xc * xc, axis=-1, keepdims=True) + NORM_EPS)
        xhat = xc * rstd
        u3 = xhat * g_ref[...] + be_ref[...]
        s3 = _sigmoid(u3)
        u4 = u3 * s3
        zv = z_ref[...]
        sz = _sigmoid(zv)
        du5v = du5_ref[...]
        dz_ref[...] = du5v * u4 * (sz * (1.0 + zv * (1.0 - sz)))
        du3 = du5v * (zv * sz) * (s3 * (1.0 + u3 * (1.0 - s3)))
        dxhat = du3 * g_ref[...]
        du2 = rstd * (dxhat - jnp.mean(dxhat, axis=-1, keepdims=True)
                      - xhat * jnp.mean(dxhat * xhat, axis=-1, keepdims=True))
        du2_ref[...] = du2
        sums = jnp.concatenate([
            jnp.sum(du3 * xhat, axis=0, keepdims=True),
            jnp.sum(du3, axis=0, keepdims=True),
            jnp.sum(du2, axis=0, keepdims=True),
            jnp.zeros((5, D_MODEL), F32)], axis=0)

        @pl.when(pl.program_id(0) == 0)
        def _():
            sums_ref[...] = jnp.zeros_like(sums_ref)

        sums_ref[...] += sums

    return pl.pallas_call(
        body, name=name, grid=(SEQ // tm,),
        in_specs=[_row_spec(tm, D_MODEL), _row_spec(tm, D_MODEL, 2), _row_spec(tm, D_MODEL),
                  _vec_spec(1, D_MODEL), _vec_spec(1, D_MODEL)],
        out_specs=[_row_spec(tm, D_MODEL), _row_spec(tm, D_MODEL), _vec_spec(8, D_MODEL)],
        out_shape=[jax.ShapeDtypeStruct((SEQ, D_MODEL), F32), jax.ShapeDtypeStruct((SEQ, D_MODEL), F32),
                   jax.ShapeDtypeStruct((8, D_MODEL), F32)],
        compiler_params=_params("arbitrary"),
    )(du5, proj, u2, ln_g, ln_b)


def _conv_bwd_taps(du2, dz, proj, conv_w, name):
    tm = 256
    hb = tm // HALO
    n_blocks = SEQ // tm

    def body(du2_ref, dnext_ref, dz_ref, vg_ref, halo_ref, w_ref, dproj_ref, dw_ref, ubuf, dbuf):
        i = pl.program_id(0)
        val = vg_ref[:, :D_MODEL]
        sg = _sigmoid(vg_ref[:, D_MODEL:])
        u1h = halo_ref[:, :D_MODEL] * _sigmoid(halo_ref[:, D_MODEL:])
        ubuf[pl.ds(0, HALO), :] = jnp.where(i > 0, u1h, 0.0)
        ubuf[pl.ds(HALO, tm), :] = val * sg
        du2v = du2_ref[...]
        dbuf[pl.ds(0, tm), :] = du2v
        dbuf[pl.ds(tm, HALO), :] = jnp.where(i < n_blocks - 1, dnext_ref[...], 0.0)

        @pl.when(i == 0)
        def _():
            dw_ref[...] = jnp.zeros_like(dw_ref)

        du1 = jnp.zeros((tm, D_MODEL), F32)
        for k in range(CONV_WIDTH):
            du1 = du1 + w_ref[k:k + 1, :] * dbuf[pl.ds(CONV_WIDTH - 1 - k, tm), :]
            dw_ref[k:k + 1, :] += jnp.sum(du2v * ubuf[pl.ds(HALO - (CONV_WIDTH - 1) + k, tm), :],
                                          axis=0, keepdims=True)
        dproj_ref[:, :D_MODEL] = (du1 * sg).astype(BF16)
        dproj_ref[:, D_MODEL:2 * D_MODEL] = (du1 * val * sg * (1.0 - sg)).astype(BF16)
        dproj_ref[:, 2 * D_MODEL:] = dz_ref[...].astype(BF16)

    return pl.pallas_call(
        body, name=name, grid=(n_blocks,),
        in_specs=[_row_spec(tm, D_MODEL),
                  pl.BlockSpec((HALO, D_MODEL), lambda i: (jnp.minimum((i + 1) * hb, SEQ // HALO - 1), 0)),
                  _row_spec(tm, D_MODEL),
                  pl.BlockSpec((tm, 2 * D_MODEL), lambda i: (i, 0)),
                  pl.BlockSpec((HALO, 2 * D_MODEL), lambda i: (jnp.maximum(i * hb - 1, 0), 0)),
                  _vec_spec(CONV_WIDTH, D_MODEL)],
        out_specs=[_row_spec(tm, 3 * D_MODEL), _vec_spec(32, D_MODEL)],
        out_shape=[jax.ShapeDtypeStruct((SEQ, 3 * D_MODEL), BF16), jax.ShapeDtypeStruct((32, D_MODEL), F32)],
        scratch_shapes=[pltpu.VMEM((HALO + tm, D_MODEL), F32), pltpu.VMEM((tm + HALO, D_MODEL), F32)],
        compiler_params=_params("arbitrary"),
    )(du2, du2, dz, proj, proj, conv_w)


def _out_a(u5, w_out, x, gate, g1, scale1, shift1, name):
    tm = 256

    def body(u_ref, w_ref, x_ref, gate_ref, g_ref, sc_ref, sh_ref, x1_ref, y_ref, h_ref, ht_ref):
        y = jnp.dot(u_ref[...], w_ref[...], preferred_element_type=F32)
        x1 = x_ref[...] + gate_ref[...] * y
        y_ref[...] = y
        x1_ref[...] = x1
        h = _normmod(x1, g_ref[...], sc_ref[...], sh_ref[...])
        h_ref[...] = h.astype(BF16)
        ht_ref[...] = h.T.astype(BF16)

    return pl.pallas_call(
        body, name=name, grid=(SEQ // tm,),
        in_specs=[_row_spec(tm, D_MODEL), _vec_spec(D_MODEL, D_MODEL), _row_spec(tm, D_MODEL)]
        + [_vec_spec(1, D_MODEL)] * 4,
        out_specs=[_row_spec(tm, D_MODEL), _row_spec(tm, D_MODEL), _row_spec(tm, D_MODEL),
                   pl.BlockSpec((D_MODEL, tm), lambda i: (0, i))],
        out_shape=[jax.ShapeDtypeStruct((SEQ, D_MODEL), F32), jax.ShapeDtypeStruct((SEQ, D_MODEL), F32),
                   jax.ShapeDtypeStruct((SEQ, D_MODEL), BF16), jax.ShapeDtypeStruct((D_MODEL, SEQ), BF16)],
        compiler_params=_params("parallel"),
    )(u5, w_out, x, gate, g1, scale1, shift1)


def _out_b_loss(u, w_out, x1, gate, target, name):
    tm = 256

    def body(u_ref, w_ref, x_ref, gate_ref, t_ref, e_ref, dy_ref, sums_ref):
        y = jnp.dot(u_ref[...], w_ref[...], preferred_element_type=F32)
        diff = x_ref[...] + gate_ref[...] * y - t_ref[...]
        e = diff * (1.0 / D_MODEL)
        e_ref[...] = e
        dy_ref[...] = (e * gate_ref[...]).astype(BF16)
        sums = jnp.concatenate([
            jnp.sum(e * y, axis=0, keepdims=True),
            jnp.sum(diff * diff, axis=0, keepdims=True),
            jnp.zeros((6, D_MODEL), F32)], axis=0)

        @pl.when(pl.program_id(0) == 0)
        def _():
            sums_ref[...] = jnp.zeros_like(sums_ref)

        sums_ref[...] += sums

    return pl.pallas_call(
        body, name=name, grid=(SEQ // tm,),
        in_specs=[_row_spec(tm, D_MODEL), _vec_spec(D_MODEL, D_MODEL), _row_spec(tm, D_MODEL),
                  _vec_spec(1, D_MODEL), _row_spec(tm, D_MODEL)],
        out_specs=[_row_spec(tm, D_MODEL), _row_spec(tm, D_MODEL), _vec_spec(8, D_MODEL)],
        out_shape=[jax.ShapeDtypeStruct((SEQ, D_MODEL), F32), jax.ShapeDtypeStruct((SEQ, D_MODEL), BF16),
                   jax.ShapeDtypeStruct((8, D_MODEL), F32)],
        compiler_params=_params("arbitrary"),
    )(u, w_out, x1, gate, target)


def _dgate_dy(dx1, y, gate, name):
    tm = 256

    def body(d_ref, y_ref, gate_ref, dy_ref, sums_ref):
        dv = d_ref[...]
        dy_ref[...] = (dv * gate_ref[...]).astype(BF16)
        sums = jnp.concatenate([jnp.sum(dv * y_ref[...], axis=0, keepdims=True), jnp.zeros((7, D_MODEL), F32)], axis=0)

        @pl.when(pl.program_id(0) == 0)
        def _():
            sums_ref[...] = jnp.zeros_like(sums_ref)

        sums_ref[...] += sums

    return pl.pallas_call(
        body, name=name, grid=(SEQ // tm,),
        in_specs=[_row_spec(tm, D_MODEL), _row_spec(tm, D_MODEL), _vec_spec(1, D_MODEL)],
        out_specs=[_row_spec(tm, D_MODEL), _vec_spec(8, D_MODEL)],
        out_shape=[jax.ShapeDtypeStruct((SEQ, D_MODEL), BF16), jax.ShapeDtypeStruct((8, D_MODEL), F32)],
        compiler_params=_params("arbitrary"),
    )(dx1, y, gate)


def _mm_nt_res(dy, w, name):
    tm = 256
    kc, n = w.shape

    def body(dy_ref, w_ref, o_ref):
        o_ref[...] = lax.dot_general(dy_ref[...], w_ref[...], (((1,), (1,)), ((), ())), preferred_element_type=F32)

    return pl.pallas_call(
        body, name=name, grid=(SEQ // tm,),
        in_specs=[_row_spec(tm, n), _vec_spec(kc, n)],
        out_specs=_row_spec(tm, kc),
        out_shape=jax.ShapeDtypeStruct((SEQ, kc), F32),
        compiler_params=_params("parallel"),
    )(dy, w)


def _seg_matrix():
    r = lax.broadcasted_iota(jnp.int32, (256, 256), 0) // HEAD_DIM
    c = lax.broadcasted_iota(jnp.int32, (256, 256), 1) // HEAD_DIM
    return (r == c).astype(BF16)


def _segsum(v, seg):
    hi = v.astype(BF16)
    lo = (v - hi.astype(F32)).astype(BF16)
    outs = []
    for c0 in range(0, D_MODEL, 256):
        outs.append(jnp.dot(hi[:, c0:c0 + 256], seg, preferred_element_type=F32)
                    + jnp.dot(lo[:, c0:c0 + 256], seg, preferred_element_type=F32))
    return jnp.concatenate(outs, axis=1)


def _qk_rstd(v, seg):
    return lax.rsqrt(_segsum(v * v, seg) * (1.0 / HEAD_DIM) + NORM_EPS)


def _qknorm_fwd(proj, qw, kw, seg, name):
    tm = 256

    def body(p_ref, qw_ref, kw_ref, seg_ref, q_ref, k_ref, v_ref):
        segv = seg_ref[...]
        q = p_ref[:, :D_MODEL]
        k = p_ref[:, D_MODEL:2 * D_MODEL]
        q_ref[...] = (q * _qk_rstd(q, segv) * qw_ref[...]).astype(BF16)
        k_ref[...] = (k * _qk_rstd(k, segv) * kw_ref[...]).astype(BF16)
        v_ref[...] = p_ref[:, 2 * D_MODEL:].astype(BF16)

    return pl.pallas_call(
        body, name=name, grid=(SEQ // tm,),
        in_specs=[_row_spec(tm, 3 * D_MODEL), _vec_spec(1, D_MODEL), _vec_spec(1, D_MODEL), _vec_spec(256, 256)],
        out_specs=[_row_spec(tm, D_MODEL)] * 3,
        out_shape=[jax.ShapeDtypeStruct((SEQ, D_MODEL), BF16)] * 3,
        compiler_params=_params("parallel"),
    )(proj, qw, kw, seg)


def _attn_masks(b, bpc, dilation, slope):
    qi = lax.broadcasted_iota(jnp.int32, (ATTN_BLOCK, 2 * ATTN_BLOCK), 0)
    kj = lax.broadcasted_iota(jnp.int32, (ATTN_BLOCK, 2 * ATTN_BLOCK), 1)
    steps = qi + ATTN_BLOCK - kj
    has_prev = (b % bpc) != 0
    valid = (steps >= 0) & (steps <= ATTN_BLOCK) & (has_prev | (kj >= ATTN_BLOCK))
    return (steps * dilation).astype(F32), valid


ATTN_HEADS_FWD = 8
ATTN_HEADS_BWD = 4
NT_DIMS = (((1,), (1,)), ((), ()))
TN_DIMS = (((0,), (0,)), ((), ()))


def _attn_specs(heads):
    cur = pl.BlockSpec((ATTN_BLOCK, heads * HEAD_DIM), lambda hg, b: (b, hg))
    prev = pl.BlockSpec((ATTN_BLOCK, heads * HEAD_DIM), lambda hg, b: (jnp.maximum(b - 1, 0), hg))
    return cur, prev


def _attn_fwd(q, k, v, slopes, dilation, name):
    bpc = SEQ // dilation // ATTN_BLOCK
    heads = ATTN_HEADS_FWD
    cur, prev = _attn_specs(heads)
    scale = HEAD_DIM ** -0.5

    def body(sl_ref, q_ref, kp_ref, kc_ref, vp_ref, vc_ref, o_ref, lse_ref):
        hg = pl.program_id(0)
        dist, valid = _attn_masks(pl.program_id(1), bpc, dilation, None)
        for h in range(heads):
            cols = slice(h * HEAD_DIM, (h + 1) * HEAD_DIM)
            kcat = jnp.concatenate([kp_ref[:, cols], kc_ref[:, cols]], axis=0)
            vcat = jnp.concatenate([vp_ref[:, cols], vc_ref[:, cols]], axis=0)
            s = lax.dot_general(q_ref[:, cols], kcat, NT_DIMS, preferred_element_type=F32)
            s = jnp.where(valid, s * scale - dist * sl_ref[heads * hg + h], NEG_INF)
            m = jnp.max(s, axis=-1, keepdims=True)
            p = jnp.exp(s - m)
            l = jnp.sum(p, axis=-1, keepdims=True)
            acc = jnp.dot(p.astype(BF16), vcat, preferred_element_type=F32)
            o_ref[:, cols] = acc / l
            lse_ref[:, cols] = jnp.broadcast_to(m + jnp.log(l), (ATTN_BLOCK, HEAD_DIM))

    return pl.pallas_call(
        body, name=name, grid=(N_HEADS // heads, SEQ // ATTN_BLOCK),
        in_specs=[pl.BlockSpec(memory_space=pltpu.SMEM), cur, prev, cur, prev, cur],
        out_specs=[cur, cur],
        out_shape=[jax.ShapeDtypeStruct((SEQ, D_MODEL), F32)] * 2,
        compiler_params=_params("parallel", "parallel"),
    )(slopes, q, k, k, v, v)


def _merge_fwd(o_parts, lse_parts, z, name):
    tm = 256

    def body(o0, o1, o2, l0, l1, l2, z_ref, u_ref, ut_ref, o_ref, lse_ref):
        ls = [l0[...], l1[...], l2[...]]
        m = jnp.maximum(jnp.maximum(ls[0], ls[1]), ls[2])
        tot = m + jnp.log(jnp.exp(ls[0] - m) + jnp.exp(ls[1] - m) + jnp.exp(ls[2] - m))
        o = (jnp.exp(ls[0] - tot) * o0[...] + jnp.exp(ls[1] - tot) * o1[...] + jnp.exp(ls[2] - tot) * o2[...])
        zv = z_ref[...]
        u = o * (zv * _sigmoid(zv))
        u_ref[...] = u.astype(BF16)
        ut_ref[...] = u.T.astype(BF16)
        o_ref[...] = o
        lse_ref[...] = tot

    return pl.pallas_call(
        body, name=name, grid=(SEQ // tm,),
        in_specs=[_row_spec(tm, D_MODEL)] * 7,
        out_specs=[_row_spec(tm, D_MODEL), pl.BlockSpec((D_MODEL, tm), lambda i: (0, i)),
                   _row_spec(tm, D_MODEL), _row_spec(tm, D_MODEL)],
        out_shape=[jax.ShapeDtypeStruct((SEQ, D_MODEL), BF16), jax.ShapeDtypeStruct((D_MODEL, SEQ), BF16),
                   jax.ShapeDtypeStruct((SEQ, D_MODEL), F32), jax.ShapeDtypeStruct((SEQ, D_MODEL), F32)],
        compiler_params=_params("parallel"),
    )(*o_parts, *lse_parts, z)


def _merge_bwd(du, o, z, seg, name):
    tm = 256

    def body(du_ref, o_ref, z_ref, seg_ref, do_ref, dz_ref, delta_ref):
        zv = z_ref[...]
        sz = _sigmoid(zv)
        duv = du_ref[...]
        ov = o_ref[...]
        do = duv * (zv * sz)
        do_ref[...] = do.astype(BF16)
        dz_ref[...] = (duv * ov * (sz * (1.0 + zv * (1.0 - sz)))).astype(BF16)
        delta_ref[...] = _segsum(do * ov, seg_ref[...])

    return pl.pallas_call(
        body, name=name, grid=(SEQ // tm,),
        in_specs=[_row_spec(tm, D_MODEL)] * 3 + [_vec_spec(256, 256)],
        out_specs=[_row_spec(tm, D_MODEL)] * 3,
        out_shape=[jax.ShapeDtypeStruct((SEQ, D_MODEL), BF16), jax.ShapeDtypeStruct((SEQ, D_MODEL), BF16),
                   jax.ShapeDtypeStruct((SEQ, D_MODEL), F32)],
        compiler_params=_params("parallel"),
    )(du, o, z, seg)


def _attn_bwd(q, k, v, do, lse, delta, slopes, dilation, name):
    bpc = SEQ // dilation // ATTN_BLOCK
    heads = ATTN_HEADS_BWD
    cur, prev = _attn_specs(heads)
    scale = HEAD_DIM ** -0.5

    def body(sl_ref, q_ref, kp_ref, kc_ref, vp_ref, vc_ref, do_ref, lse_ref, dl_ref,
             dq_ref, dkc_ref, dkp_ref, dvc_ref, dvp_ref):
        hg = pl.program_id(0)
        dist, valid = _attn_masks(pl.program_id(1), bpc, dilation, None)
        for h in range(heads):
            cols = slice(h * HEAD_DIM, (h + 1) * HEAD_DIM)
            kcat = jnp.concatenate([kp_ref[:, cols], kc_ref[:, cols]], axis=0)
            vcat = jnp.concatenate([vp_ref[:, cols], vc_ref[:, cols]], axis=0)
            qh = q_ref[:, cols]
            doh = do_ref[:, cols]
            lse_col = lse_ref[:, h * HEAD_DIM:h * HEAD_DIM + 1]
            dl_col = dl_ref[:, h * HEAD_DIM:h * HEAD_DIM + 1]
            s = lax.dot_general(qh, kcat, NT_DIMS, preferred_element_type=F32)
            p = jnp.exp(jnp.where(valid, s * scale - dist * sl_ref[heads * hg + h], NEG_INF) - lse_col)
            dp = lax.dot_general(doh, vcat, NT_DIMS, preferred_element_type=F32)
            ds = (p * (dp - dl_col) * scale).astype(BF16)
            dq_ref[:, cols] = jnp.dot(ds, kcat, preferred_element_type=F32)
            dk = lax.dot_general(ds, qh, TN_DIMS, preferred_element_type=F32)
            dv = lax.dot_general(p.astype(BF16), doh, TN_DIMS, preferred_element_type=F32)
            dkp_ref[:, cols] = dk[:ATTN_BLOCK]
            dkc_ref[:, cols] = dk[ATTN_BLOCK:]
            dvp_ref[:, cols] = dv[:ATTN_BLOCK]
            dvc_ref[:, cols] = dv[ATTN_BLOCK:]

    return pl.pallas_call(
        body, name=name, grid=(N_HEADS // heads, SEQ // ATTN_BLOCK),
        in_specs=[pl.BlockSpec(memory_space=pltpu.SMEM), cur, prev, cur, prev, cur, cur, cur, cur],
        out_specs=[cur] * 5,
        out_shape=[jax.ShapeDtypeStruct((SEQ, D_MODEL), F32)] * 5,
        compiler_params=_params("parallel", "parallel"),
    )(slopes, q, k, k, v, v, do, lse, delta)


def _qknorm_bwd(proj, qw, kw, seg, dq, dkc, dkp, dvc, dvp, name):
    tm = ATTN_BLOCK
    n_blocks = SEQ // tm
    nxt = pl.BlockSpec((tm, D_MODEL), lambda i: (jnp.minimum(i + 1, n_blocks - 1), 0))

    def body(p_ref, qw_ref, kw_ref, seg_ref, dq_ref, dkc_ref, dkp_ref, dvc_ref, dvp_ref, dproj_ref, sums_ref):
        i = pl.program_id(0)
        segv = seg_ref[...]
        has_next = i < n_blocks - 1
        dk = dkc_ref[...] + jnp.where(has_next, dkp_ref[...], 0.0)
        dv = dvc_ref[...] + jnp.where(has_next, dvp_ref[...], 0.0)
        sums = []
        for part, (raw, w, dn) in enumerate(((p_ref[:, :D_MODEL], qw_ref[...], dq_ref[...]),
                                             (p_ref[:, D_MODEL:2 * D_MODEL], kw_ref[...], dk))):
            r = _qk_rstd(raw, segv)
            gq = dn * w
            draw = r * gq - raw * (r * r * r) * (_segsum(raw * gq, segv) * (1.0 / HEAD_DIM))
            dproj_ref[:, part * D_MODEL:(part + 1) * D_MODEL] = draw.astype(BF16)
            sums.append(jnp.sum(dn * raw * r, axis=0, keepdims=True))
        dproj_ref[:, 2 * D_MODEL:] = dv.astype(BF16)

        @pl.when(i == 0)
        def _():
            sums_ref[...] = jnp.zeros_like(sums_ref)

        sums_ref[...] += jnp.concatenate(sums + [jnp.zeros((6, D_MODEL), F32)], axis=0)

    return pl.pallas_call(
        body, name=name, grid=(n_blocks,),
        in_specs=[_row_spec(tm, 3 * D_MODEL), _vec_spec(1, D_MODEL), _vec_spec(1, D_MODEL), _vec_spec(256, 256),
                  _row_spec(tm, D_MODEL), _row_spec(tm, D_MODEL), nxt, _row_spec(tm, D_MODEL), nxt],
        out_specs=[_row_spec(tm, 3 * D_MODEL), _vec_spec(8, D_MODEL)],
        out_shape=[jax.ShapeDtypeStruct((SEQ, 3 * D_MODEL), BF16), jax.ShapeDtypeStruct((8, D_MODEL), F32)],
        compiler_params=_params("arbitrary"),
    )(proj, qw, kw, seg, dq, dkc, dkp, dvc, dvp)


def _to_classes(a, dilation):
    if dilation == 1:
        return a
    s, c = a.shape
    return a.reshape(s // dilation, dilation, c).transpose(1, 0, 2).reshape(s, c)


def _from_classes(a, dilation):
    if dilation == 1:
        return a
    s, c = a.shape
    return a.reshape(dilation, s // dilation, c).transpose(1, 0, 2).reshape(s, c)


def _cols_to_classes(a, dilation):
    if dilation == 1:
        return a
    r, s = a.shape
    return a.reshape(r, s // dilation, dilation).transpose(0, 2, 1).reshape(r, s)


B_TN = 512
B_GROUP_TILES = 3 * D_MODEL // B_TN
B_Z_TILE0 = 3 * B_GROUP_TILES
B_Z_TILES = D_MODEL // B_TN


def _local_step(x, target, mods, norm_g, conv_w, conv_b, ln_g, ln_b, q_norm, k_norm,
                weights_a, weights_b, forward_weights_b, send_grads_b, forward_grads_b, send_grads_a):
    row = lambda a, i: a[i:i + 1]
    shift0, scale0, gate0 = row(mods[0], 0), row(mods[0], 1), row(mods[0], 2)
    shift1, scale1, gate1 = row(mods[1], 0), row(mods[1], 1), row(mods[1], 2)
    g0, g1 = row(norm_g, 0), row(norm_g, 1)
    seg = _seg_matrix()
    slopes = jnp.exp2(-8.0 * jnp.arange(1, N_HEADS + 1, dtype=F32) / N_HEADS)
    qw = [jnp.tile(q_norm[g:g + 1], (1, N_HEADS)) for g in range(3)]
    kw = [jnp.tile(k_norm[g:g + 1], (1, N_HEADS)) for g in range(3)]

    h0, h0t = _normmod_fwd(x, g0, scale0, shift0, "prenorm0")
    wa_in, wa_out = weights_a(h0)
    ja, _, nsa = wa_in.shape
    proj_a = _mm(h0, wa_in, tn=nsa, tile0=0, n_tiles=ja, out_dtype=F32, name="a_in")
    u5, u5t, u2 = _conv_fwd(proj_a, conv_w, conv_b, ln_g, ln_b, "a_conv")
    token = forward_weights_b(u5)
    x1, y_a, h1, h1t = _out_a(u5, wa_out, x, gate0 + token[0:1, 0:1], g1, scale1, shift1, "a_out")

    wb_in, wb_out = weights_b(x1)
    jb, _, nsb = wb_in.shape
    h1c =[_to_classes(h1, d) for d in DILATIONS]
    h1tc = [_cols_to_classes(h1t, d) for d in DILATIONS]
    z_b = _mm(h1, wb_in, tn=B_TN, tile0=B_Z_TILE0, n_tiles=B_Z_TILES, out_dtype=F32, name="b_in_z")
    proj_g, qkv, o_parts, lse_parts = [], [], [], []
    for g, d in enumerate(DILATIONS):
        pg = _mm(h1c[g], wb_in, tn=B_TN, tile0=g * B_GROUP_TILES, n_tiles=B_GROUP_TILES, out_dtype=F32,
                 name=f"b_in_g{g}")
        qn, kn, vn = _qknorm_fwd(pg, qw[g], kw[g], seg, f"b_qknorm_g{g}")
        og, lg = _attn_fwd(qn, kn, vn, slopes, d, f"b_attn_g{g}")
        proj_g.append(pg)
        qkv.append((qn, kn, vn))
        o_parts.append(_from_classes(og, d))
        lse_parts.append(_from_classes(lg, d))
    u_b, u_bt, o_b, lse_b = _merge_fwd(o_parts, lse_parts, z_b, "b_merge")
    e, dy_b, sums_loss = _out_b_loss(u_b, wb_out, x1, gate1, target, "b_out_loss")

    dwb_out = _mm(u_bt, dy_b, tn=D_MODEL, tile0=0, n_tiles=1, out_dtype=BF16, name="b_dwout")
    du_b = _mm_nt_res(dy_b, wb_out, "b_dout")
    do_b, dz_b, delta_b = _merge_bwd(du_b, o_b, z_b, seg, "b_merge_bwd")
    dwb_in = _mm(h1t, dz_b, tn=B_TN, tile0=B_Z_TILE0, n_tiles=B_Z_TILES, out_dtype=BF16, name="b_dwin_z",
                 out3d=(jb, nsb))
    dh1_parts = [_mm_nt(dz_b, wb_in, tn=B_TN, tile0=B_Z_TILE0, n_tiles=B_Z_TILES, name="b_dh_z")]
    qk_sums = []
    for g, d in enumerate(DILATIONS):
        qn, kn, vn = qkv[g]
        dq, dkc, dkp, dvc, dvp = _attn_bwd(qn, kn, vn, _to_classes(do_b, d), _to_classes(lse_b, d),
                                           _to_classes(delta_b, d), slopes, d, f"b_attn_bwd_g{g}")
        dproj, sums_qk = _qknorm_bwd(proj_g[g], qw[g], kw[g], seg, dq, dkc, dkp, dvc, dvp, f"b_qknorm_bwd_g{g}")
        qk_sums.append(sums_qk)
        dwb_in = _mm(h1tc[g], dproj, tn=B_TN, tile0=g * B_GROUP_TILES, n_tiles=B_GROUP_TILES, out_dtype=BF16,
                     name=f"b_dwin_g{g}", out3d=(jb, nsb), prev=dwb_in)
        dh = _mm_nt(dproj, wb_in, tn=B_TN, tile0=g * B_GROUP_TILES, n_tiles=B_GROUP_TILES, name=f"b_dh_g{g}")
        dh1_parts.append(_from_classes(dh, d))
    token = send_grads_b(dwb_in, dwb_out)
    dx1, sums_n1 = _normmod_bwd(x1, g1, scale1 + token[0:1, 0:1], dh1_parts, e, "prenorm1_bwd")
    token = forward_grads_b(dx1)

    dy_a, sums_ga = _dgate_dy(dx1, y_a, gate0 + token[0:1, 0:1], "a_dgate")
    dwa_out = _mm(u5t, dy_a, tn=D_MODEL, tile0=0, n_tiles=1, out_dtype=BF16, name="a_dwout")
    du5 = _mm_nt_res(dy_a, wa_out, "a_dout")
    du2, dz_a, sums_ln = _conv_bwd_pointwise(du5, proj_a, u2, ln_g, ln_b, "a_conv_bwd_pw")
    dproj_a, dconv_w = _conv_bwd_taps(du2, dz_a, proj_a, conv_w, "a_conv_bwd_taps")
    dwa_in = _mm(h0t, dproj_a, tn=nsa, tile0=0, n_tiles=ja, out_dtype=BF16, name="a_dwin", out3d=(ja, nsa))
    token = send_grads_a(dwa_in, dwa_out)
    dh0 = _mm_nt(dproj_a, wa_in, tn=nsa, tile0=0, n_tiles=ja, name="a_dh", after=token)
    grad_x, sums_n0 = _normmod_bwd(x, g0, scale0, [dh0], dx1, "prenorm0_bwd")

    small = dict(
        dnorm_g=jnp.concatenate([sums_n0[0:1], sums_n1[0:1]], axis=0),
        dmod0=jnp.concatenate([sums_n0[2:3], sums_n0[1:2], sums_ga[0:1]], axis=0),
        dmod1=jnp.concatenate([sums_n1[2:3], sums_n1[1:2], sums_loss[0:1]], axis=0),
        dln_g=sums_ln[0:1], dln_b=sums_ln[1:2], dconv_b=sums_ln[2:3],
        dconv_w=dconv_w[:CONV_WIDTH],
        dq_norm=jnp.concatenate([s[0:1] for s in qk_sums], axis=0),
        dk_norm=jnp.concatenate([s[1:2] for s in qk_sums], axis=0),
        loss_cols=sums_loss[1:2],
    )
    return grad_x, small


def _adamw(w, g, m, v, name):
    rows, cols = w.shape
    tr = rows if rows <= 128 else 128
    c1 = 1.0 / (1.0 - ADAM_B1 ** ADAM_STEP)
    c2 = 1.0 / (1.0 - ADAM_B2 ** ADAM_STEP)

    def body(w_ref, g_ref, m_ref, v_ref, d_ref, mo_ref, vo_ref):
        gv = g_ref[...]
        mn = ADAM_B1 * m_ref[...] + (1.0 - ADAM_B1) * gv
        vn = ADAM_B2 * v_ref[...] + (1.0 - ADAM_B2) * (gv * gv)
        mo_ref[...] = mn
        vo_ref[...] = vn
        d_ref[...] = -ADAM_LR * ((mn * c1) / (jnp.sqrt(vn * c2) + ADAM_EPS) + ADAM_WD * w_ref[...])

    spec = pl.BlockSpec((tr, cols), lambda i: (i, 0))
    return pl.pallas_call(
        body, name=name, grid=(rows // tr,), in_specs=[spec] * 4, out_specs=[spec] * 3,
        out_shape=[jax.ShapeDtypeStruct((rows, cols), F32)] * 3,
        compiler_params=_params("parallel"),
    )(w, g, m, v)


def _cast_into_slot(w, chip_idx, name):
    rows, cols = w.shape
    tr = 256

    def body(ch_ref, w_ref, o_ref):
        o_ref[...] = w_ref[...].astype(BF16)

    return pl.pallas_call(
        body, name=name,
        grid_spec=pltpu.PrefetchScalarGridSpec(
            num_scalar_prefetch=1, grid=(rows // tr,),
            in_specs=[pl.BlockSpec((tr, cols), lambda i, ch: (i, 0))],
            out_specs=pl.BlockSpec((None, tr, cols), lambda i, ch: (ch[0], i, 0))),
        out_shape=jax.ShapeDtypeStruct((N_CHIPS, rows, cols), BF16), compiler_params=_params("parallel"),
    )(chip_idx, w)


def _position():
    x, y, c = lax.axis_index("x"), lax.axis_index("y"), lax.axis_index("c")
    return x, y, c


def _xor_peer(x, y, c, k):
    return (x ^ ((k >> 2) & 1), y ^ ((k >> 1) & 1), c ^ (k & 1))


def _chip_peer(x, y, k):
    return (x ^ ((k >> 1) & 1), y ^ (k & 1))


def _ada_forward(c_row, ada_w, ada_b, conv_w):
    ns = ada_w.shape[2]
    cw = conv_w.shape[1]

    def body(c_ref, w_ref, b_ref, cv_ref, mod_ref, sc_ref, cvo_ref,
             c_all, mp, parts, cv_parts, send1, recv1, send2, recv2, send3, recv3):
        x, y, c = _position()
        me = 4 * x + 2 * y + c
        chip = 2 * x + y

        def c_copy(k):
            return pltpu.make_async_remote_copy(
                src_ref=c_all.at[me], dst_ref=c_all.at[me], send_sem=send1.at[k - 1], recv_sem=recv1.at[k - 1],
                device_id=_xor_peer(x, y, c, k), device_id_type=MESH)

        def cv_copy(k):
            px, py = _chip_peer(x, y, k)
            return pltpu.make_async_remote_copy(
                src_ref=cv_parts.at[chip], dst_ref=cv_parts.at[chip], send_sem=send3.at[k - 1],
                recv_sem=recv3.at[k - 1], device_id=(px, py, c), device_id_type=MESH)

        c_all[me] = c_ref[...]
        cv_parts[chip] = cv_ref[...]
        for k in range(1, N_DEV):
            c_copy(k).start()
        for k in range(1, N_CHIPS):
            cv_copy(k).start()
        for k in range(1, N_DEV):
            c_copy(k).wait_recv()
        cv = jnp.concatenate([c_all[i] for i in range(N_DEV)], axis=0)
        sc = cv * _sigmoid(cv)
        sc_ref[...] = sc
        for l in range(2):
            res = jnp.dot(sc, w_ref[l], preferred_element_type=F32, precision=lax.Precision.HIGHEST)
            for i in range(N_DEV):
                mp[i, l:l + 1, :] = res[i:i + 1, :]

        def mod_copy(k):
            px, py = _chip_peer(x, y, k)
            return pltpu.make_async_remote_copy(
                src_ref=mp.at[4 * px + 2 * py + c], dst_ref=parts.at[chip], send_sem=send2.at[k - 1],
                recv_sem=recv2.at[k - 1], device_id=(px, py, c), device_id_type=MESH)

        for k in range(1, N_CHIPS):
            mod_copy(k).start()
        parts[chip] = mp[me]
        for k in range(1, N_CHIPS):
            mod_copy(k).wait_recv()
            cv_copy(k).wait_recv()
        mod_ref[...] = jnp.concatenate([parts[j] for j in range(N_CHIPS)], axis=1) + b_ref[...]
        cvo_ref[...] = jnp.concatenate([cv_parts[j] for j in range(N_CHIPS)], axis=1)
        for k in range(1, N_DEV):
            c_copy(k).wait_send()
        for k in range(1, N_CHIPS):
            mod_copy(k).wait_send()
            cv_copy(k).wait_send()

    vm = pl.BlockSpec(memory_space=pltpu.VMEM)
    return pl.pallas_call(
        body, name="ada_forward",
        in_specs=[vm] * 4, out_specs=[vm] * 3,
        out_shape=[jax.ShapeDtypeStruct((2, 3 * D_MODEL), F32), jax.ShapeDtypeStruct((N_DEV, D_MODEL), F32),
                   jax.ShapeDtypeStruct((CONV_WIDTH, N_CHIPS * cw), F32)],
        scratch_shapes=[pltpu.VMEM((N_DEV, 1, D_MODEL), F32), pltpu.VMEM((N_DEV, 2, ns), F32),
                        pltpu.VMEM((N_CHIPS, 2, ns), F32), pltpu.VMEM((N_CHIPS, CONV_WIDTH, cw), F32),
                        pltpu.SemaphoreType.DMA((N_DEV - 1,)), pltpu.SemaphoreType.DMA((N_DEV - 1,)),
                        pltpu.SemaphoreType.DMA((N_CHIPS - 1,)), pltpu.SemaphoreType.DMA((N_CHIPS - 1,)),
                        pltpu.SemaphoreType.DMA((N_CHIPS - 1,)), pltpu.SemaphoreType.DMA((N_CHIPS - 1,))],
        compiler_params=pltpu.CompilerParams(vmem_limit_bytes=VMEM_LIMIT_BYTES),
    )(c_row, ada_w, ada_b, conv_w)


HBM_SPEC = pl.BlockSpec(memory_space=pltpu.HBM)
ANY_SPEC = pl.BlockSpec(memory_space=pl.ANY)
SEM_SPEC = pl.BlockSpec(memory_space=pltpu.SEMAPHORE)
SPLIT_PARAMS = dict(compiler_params=pltpu.CompilerParams(has_side_effects=pltpu.SideEffectType.DATAFLOW_SIDE_EFFECTING))
TOKEN = jax.ShapeDtypeStruct((8, 128), F32)


def _hbm(arrays):
    return [pltpu.with_memory_space_constraint(a, pltpu.HBM) for a in arrays]


def _hbm_like(arrays):
    return [pltpu.HBM(a.shape, a.dtype) for a in arrays]


def _gather_start(lands, after, name):
    n = len(lands)

    def body(*refs):
        ins = refs[:n]
        send, recv = refs[n + 1], refs[n + 2]
        x, y, c = _position()
        chip = 2 * x + y
        for t in range(n):
            rh = ins[t].shape[1] // 2
            for k in range(1, N_CHIPS):
                px, py = _chip_peer(x, y, k)
                block = ins[t].at[chip, pl.ds(c * rh, rh)]
                pltpu.make_async_remote_copy(
                    src_ref=block, dst_ref=block, send_sem=send.at[3 * t + k - 1], recv_sem=recv.at[3 * t + k - 1],
                    device_id=(px, py, c), device_id_type=MESH).start()
        refs[-1][...] = jnp.zeros(TOKEN.shape, F32)

    res = pl.pallas_call(
        body, name=name, in_specs=[HBM_SPEC] * n + [ANY_SPEC],
        out_specs=(SEM_SPEC, SEM_SPEC, *[HBM_SPEC] * n, pl.BlockSpec(memory_space=pltpu.VMEM)),
        out_shape=(pltpu.SemaphoreType.DMA((3 * n,)), pltpu.SemaphoreType.DMA((3 * n,)), *_hbm_like(lands), TOKEN),
        input_output_aliases={t: 2 + t for t in range(n)}, **SPLIT_PARAMS,
    )(*_hbm(lands), after)
    return res[0], res[1], list(res[2:2 + n]), res[-1]


def _gather_forward(send, recv, lands, after, name):
    n = len(lands)

    def body(*refs):
        ins = refs[:n]
        send1, recv1 = refs[n], refs[n + 1]
        send2, recv2 = refs[n + 3], refs[n + 4]
        x, y, c = _position()
        chip = 2 * x + y
        for t in range(n):
            rh = ins[t].shape[1] // 2
            half = pl.ds(c * rh, rh)
            for k in range(1, N_CHIPS):
                px, py = _chip_peer(x, y, k)
                s = 3 * t + k - 1
                got = ins[t].at[2 * px + py, half]
                cp = pltpu.make_async_remote_copy(
                    src_ref=ins[t].at[chip, half], dst_ref=got, send_sem=send1.at[s], recv_sem=recv1.at[s],
                    device_id=(px, py, c), device_id_type=MESH)
                cp.wait_send()
                cp.wait_recv()
                pltpu.make_async_remote_copy(
                    src_ref=got, dst_ref=got, send_sem=send2.at[s], recv_sem=recv2.at[s],
                    device_id=(x, y, 1 - c), device_id_type=MESH).start()
        refs[-1][...] = jnp.zeros(TOKEN.shape, F32)

    res = pl.pallas_call(
        body, name=name, in_specs=[HBM_SPEC] * n + [SEM_SPEC, SEM_SPEC, ANY_SPEC],
        out_specs=(SEM_SPEC, SEM_SPEC, *[HBM_SPEC] * n, pl.BlockSpec(memory_space=pltpu.VMEM)),
        out_shape=(pltpu.SemaphoreType.DMA((3 * n,)), pltpu.SemaphoreType.DMA((3 * n,)), *_hbm_like(lands), TOKEN),
        input_output_aliases={t: 2 + t for t in range(n)}, **SPLIT_PARAMS,
    )(*lands, send, recv, after)
    return res[0], res[1], list(res[2:2 + n]), res[-1]


def _gather_wait(send, recv, lands, after, name):
    n = len(lands)

    def body(*refs):
        ins = refs[:n]
        send_ref, recv_ref = refs[n], refs[n + 1]
        x, y, c = _position()
        for t in range(n):
            rh = ins[t].shape[1] // 2
            for k in range(1, N_CHIPS):
                px, py = _chip_peer(x, y, k)
                cp = pltpu.make_async_remote_copy(
                    src_ref=ins[t].at[2 * px + py, pl.ds(c * rh, rh)],
                    dst_ref=ins[t].at[2 * px + py, pl.ds((1 - c) * rh, rh)], send_sem=send_ref.at[3 * t + k - 1],
                    recv_sem=recv_ref.at[3 * t + k - 1], device_id=(x, y, 1 - c), device_id_type=MESH)
                cp.wait_send()
                cp.wait_recv()

    res = pl.pallas_call(
        body, name=name, in_specs=[HBM_SPEC] * n + [SEM_SPEC, SEM_SPEC, ANY_SPEC], out_specs=[HBM_SPEC] * n,
        out_shape=_hbm_like(lands), input_output_aliases={t: t for t in range(n)}, **SPLIT_PARAMS,
    )(*lands, send, recv, after)
    return list(res)


def _reduce_start(grads, after, name):
    n = len(grads)
    lands = [lax.empty((N_DEV, g.shape[1] // 2, g.shape[2]), BF16) for g in grads]

    def body(*refs):
        gs, ls = refs[:n], refs[n:2 * n]
        send, recv = refs[2 * n + 1], refs[2 * n + 2]
        x, y, c = _position()
        me = 4 * x + 2 * y + c
        for t in range(n):
            rh = gs[t].shape[1] // 2
            for k in range(1, N_DEV):
                px, py, pc = _xor_peer(x, y, c, k)
                pltpu.make_async_remote_copy(
                    src_ref=gs[t].at[2 * px + py, pl.ds(pc * rh, rh)], dst_ref=ls[t].at[me],
                    send_sem=send.at[7 * t + k - 1], recv_sem=recv.at[7 * t + k - 1],
                    device_id=(px, py, pc), device_id_type=MESH).start()
        refs[-1][...] = jnp.zeros(TOKEN.shape, F32)

    res = pl.pallas_call(
        body, name=name, in_specs=[HBM_SPEC] * (2 * n) + [ANY_SPEC],
        out_specs=(SEM_SPEC, SEM_SPEC, *[HBM_SPEC] * (2 * n), pl.BlockSpec(memory_space=pltpu.VMEM)),
        out_shape=(pltpu.SemaphoreType.DMA((7 * n,)), pltpu.SemaphoreType.DMA((7 * n,)),
                   *_hbm_like(grads), *_hbm_like(lands), TOKEN),
        input_output_aliases={t: 2 + t for t in range(2 * n)}, **SPLIT_PARAMS,
    )(*_hbm(grads), *_hbm(lands), after)
    return res[0], res[1], list(res[2:2 + n]), list(res[2 + n:2 + 2 * n]), res[-1]


def _reduce_wait(send, recv, grads, lands, after, name):
    n = len(grads)

    def body(*refs):
        gs, ls = refs[:n], refs[n:2 * n]
        send_ref, recv_ref = refs[2 * n], refs[2 * n + 1]
        x, y, c = _position()
        for t in range(n):
            rh = gs[t].shape[1] // 2
            for k in range(1, N_DEV):
                px, py, pc = _xor_peer(x, y, c, k)
                cp = pltpu.make_async_remote_copy(
                    src_ref=gs[t].at[2 * px + py, pl.ds(pc * rh, rh)], dst_ref=ls[t].at[4 * px + 2 * py + pc],
                    send_sem=send_ref.at[7 * t + k - 1], recv_sem=recv_ref.at[7 * t + k - 1],
                    device_id=(px, py, pc), device_id_type=MESH)
                cp.wait_send()
                cp.wait_recv()

    res = pl.pallas_call(
        body, name=name, in_specs=[HBM_SPEC] * (2 * n) + [SEM_SPEC, SEM_SPEC, ANY_SPEC], out_specs=[HBM_SPEC] * (2 * n),
        out_shape=_hbm_like(grads) + _hbm_like(lands), input_output_aliases={t: t for t in range(2 * n)}, **SPLIT_PARAMS,
    )(*grads, *lands, send, recv, after)
    return list(res[:n]), list(res[n:])


def _sum_devices(land, grad, dev_idx, name):
    _, rh, cols = land.shape
    tr = 128
    nb = rh // tr

    def body(idx_ref, l_ref, g_ref, o_ref):
        me = idx_ref[0]
        acc = jnp.where(me == 0, g_ref[...], l_ref[0]).astype(F32)
        for d in range(1, N_DEV):
            acc = acc + jnp.where(me == d, g_ref[...], l_ref[d]).astype(F32)
        o_ref[...] = acc

    return pl.pallas_call(
        body, name=name,
        grid_spec=pltpu.PrefetchScalarGridSpec(
            num_scalar_prefetch=1, grid=(nb,),
            in_specs=[pl.BlockSpec((N_DEV, tr, cols), lambda i, idx: (0, i, 0)),
                      pl.BlockSpec((None, tr, cols), lambda i, idx: (idx[1], idx[2] * nb + i, 0))],
            out_specs=pl.BlockSpec((tr, cols), lambda i, idx: (idx[2] * nb + i, 0))),
        out_shape=jax.ShapeDtypeStruct((2 * rh, cols), F32), compiler_params=_params("parallel"),
    )(dev_idx, land, grad)


def _split_start(name, arrays, n_sems, after, issue):
    m = len(arrays)

    def body(*refs):
        issue(refs[:m], refs[m + 1], refs[m + 2])
        refs[-1][...] = jnp.zeros(TOKEN.shape, F32)

    res = pl.pallas_call(
        body, name=name, in_specs=[HBM_SPEC] * m + [ANY_SPEC],
        out_specs=(SEM_SPEC, SEM_SPEC, *[HBM_SPEC] * m, pl.BlockSpec(memory_space=pltpu.VMEM)),
        out_shape=(pltpu.SemaphoreType.DMA((n_sems,)), pltpu.SemaphoreType.DMA((n_sems,)), *_hbm_like(arrays), TOKEN),
        input_output_aliases={t: 2 + t for t in range(m)}, **SPLIT_PARAMS,
    )(*_hbm(arrays), after)
    return res[0], res[1], list(res[2:2 + m]), res[-1]


def _split_wait(name, arrays, send, recv, after, await_all):
    m = len(arrays)

    def body(*refs):
        await_all(refs[:m], refs[m], refs[m + 1])

    res = pl.pallas_call(
        body, name=name, in_specs=[HBM_SPEC] * m + [SEM_SPEC, SEM_SPEC, ANY_SPEC], out_specs=[HBM_SPEC] * m,
        out_shape=_hbm_like(arrays), input_output_aliases={t: t for t in range(m)}, **SPLIT_PARAMS,
    )(*arrays, send, recv, after)
    return list(res)


def _sibling_copies(refs, send, recv, n):
    x, y, c = _position()
    cps = []
    for t in range(n):
        rh = refs[t].shape[1] // 2
        cps.append(pltpu.make_async_remote_copy(
            src_ref=refs[t].at[pl.ds(0, N_CHIPS), pl.ds((1 - c) * rh, rh)], dst_ref=refs[n + t],
            send_sem=send.at[t], recv_sem=recv.at[t], device_id=(x, y, 1 - c), device_id_type=MESH))
    return cps


def _reduce_sibling_start(grads, after, name):
    n = len(grads)
    lands = [lax.empty((N_CHIPS, g.shape[1] // 2, g.shape[2]), BF16) for g in grads]

    def issue(refs, send, recv):
        for cp in _sibling_copies(refs, send, recv, n):
            cp.start()

    return _split_start(name, list(grads) + lands, n, after, issue)


def _reduce_sibling_wait(send, recv, arrays, after, name):
    n = len(arrays) // 2

    def await_all(refs, send_ref, recv_ref):
        for cp in _sibling_copies(refs, send_ref, recv_ref, n):
            cp.wait_send()
            cp.wait_recv()

    res = _split_wait(name, arrays, send, recv, after, await_all)
    return res[:n], res[n:]


def _add_sibling_half(grad, got, dev_idx, name):
    j, r, cols = grad.shape
    rh = r // 2
    tr = 128
    nb = rh // tr

    def body(idx_ref, g_ref, got_ref, out_ref):
        out_ref[...] = (g_ref[...].astype(F32) + got_ref[...].astype(F32)).astype(BF16)

    return pl.pallas_call(
        body, name=name,
        grid_spec=pltpu.PrefetchScalarGridSpec(
            num_scalar_prefetch=1, grid=(j, nb),
            in_specs=[pl.BlockSpec((None, tr, cols), lambda jj, i, idx: (jj, idx[2] * nb + i, 0)),
                      pl.BlockSpec((None, tr, cols), lambda jj, i, idx: (jj, i, 0))],
            out_specs=pl.BlockSpec((None, tr, cols), lambda jj, i, idx: (jj, i, 0))),
        out_shape=jax.ShapeDtypeStruct((j, rh, cols), BF16),
        compiler_params=_params("parallel", "parallel"),
    )(dev_idx, grad, got)


def _chip_copies(refs, send, recv, n, receiving):
    x, y, c = _position()
    chip = 2 * x + y
    cps = []
    for t in range(n):
        for k in range(1, N_CHIPS):
            px, py = _chip_peer(x, y, k)
            cps.append(pltpu.make_async_remote_copy(
                src_ref=refs[t].at[2 * px + py], dst_ref=refs[n + t].at[2 * px + py if receiving else chip],
                send_sem=send.at[3 * t + k - 1], recv_sem=recv.at[3 * t + k - 1],
                device_id=(px, py, c), device_id_type=MESH))
    return cps


def _reduce_chips_start(partials, after, name):
    n = len(partials)
    lands = [lax.empty(p.shape, BF16) for p in partials]

    def issue(refs, send, recv):
        for cp in _chip_copies(refs, send, recv, n, False):
            cp.start()

    return _split_start(name, list(partials) + lands, 3 * n, after, issue)


def _reduce_chips_wait(send, recv, arrays, after, name):
    n = len(arrays) // 2

    def await_all(refs, send_ref, recv_ref):
        for cp in _chip_copies(refs, send_ref, recv_ref, n, True):
            cp.wait_send()
            cp.wait_recv()

    res = _split_wait(name, arrays, send, recv, after, await_all)
    return res[:n], res[n:]


def _sum_partials(land, partial, dev_idx, name):
    _, rh, cols = land.shape
    tr = 128
    nb = rh // tr

    def body(idx_ref, l_ref, p_ref, o_ref):
        chip = idx_ref[1]
        acc = jnp.where(chip == 0, p_ref[...], l_ref[0]).astype(F32)
        for s in range(1, N_CHIPS):
            acc = acc + jnp.where(chip == s, p_ref[...], l_ref[s]).astype(F32)
        o_ref[...] = acc

    return pl.pallas_call(
        body, name=name,
        grid_spec=pltpu.PrefetchScalarGridSpec(
            num_scalar_prefetch=1, grid=(nb,),
            in_specs=[pl.BlockSpec((N_CHIPS, tr, cols), lambda i, idx: (0, i, 0)),
                      pl.BlockSpec((None, tr, cols), lambda i, idx: (idx[1], i, 0))],
            out_specs=pl.BlockSpec((tr, cols), lambda i, idx: (idx[2] * nb + i, 0))),
        out_shape=jax.ShapeDtypeStruct((2 * rh, cols), F32), compiler_params=_params("parallel"),
    )(dev_idx, land, partial)


def _share_halves(totals):
    n = len(totals)

    def body(*refs):
        ins, outs = refs[:n], refs[n:2 * n]
        send, recv = refs[2 * n:]
        x, y, c = _position()
        cps = []
        for t in range(n):
            rh = ins[t].shape[0] // 2
            mine = pl.ds(c * rh, rh)
            cp = pltpu.make_async_remote_copy(
                src_ref=ins[t].at[mine], dst_ref=outs[t].at[mine], send_sem=send.at[t], recv_sem=recv.at[t],
                device_id=(x, y, 1 - c), device_id_type=MESH)
            cp.start()
            cps.append(cp)
        for cp in cps:
            cp.wait()

    return pl.pallas_call(
        body, name="reduce_share_" + "_".join(str(t.shape[1]) for t in totals), in_specs=[ANY_SPEC] * n,
        out_specs=[ANY_SPEC] * n, out_shape=[jax.ShapeDtypeStruct(t.shape, F32) for t in totals],
        input_output_aliases={t: t for t in range(n)},
        scratch_shapes=[pltpu.SemaphoreType.DMA((n,)), pltpu.SemaphoreType.DMA((n,))],
    )(*totals)


def _exchange_halves(grads):
    n = len(grads)
    hbm = pl.BlockSpec(memory_space=pl.ANY)

    def body(*refs):
        ins, outs = refs[:n], refs[n:2 * n]
        send, recv = refs[2 * n:]
        x, y, c = _position()
        cps = []
        for t in range(n):
            rh = ins[t].shape[1] // 2
            cp = pltpu.make_async_remote_copy(
                src_ref=ins[t].at[pl.ds(0, N_CHIPS), pl.ds((1 - c) * rh, rh)], dst_ref=outs[t], send_sem=send.at[t],
                recv_sem=recv.at[t], device_id=(x, y, 1 - c), device_id_type=MESH)
            cp.start()
            cps.append(cp)
        for cp in cps:
            cp.wait()

    return pl.pallas_call(
        body, name="reduce_exchange_halves", in_specs=[hbm] * n, out_specs=[hbm] * n,
        out_shape=[jax.ShapeDtypeStruct((g.shape[0], g.shape[1] // 2, g.shape[2]), BF16) for g in grads],
        scratch_shapes=[pltpu.SemaphoreType.DMA((n,)), pltpu.SemaphoreType.DMA((n,))],
    )(*grads)


def _add_halves(grad, got, c_idx, name):
    j, r, cols = grad.shape
    rh = r // 2
    tr = 128
    nb = rh // tr

    def body(c_ref, g_ref, o_ref_in, out_ref):
        out_ref[...] = (g_ref[...].astype(F32) + o_ref_in[...].astype(F32)).astype(BF16)

    return pl.pallas_call(
        body, name=name,
        grid_spec=pltpu.PrefetchScalarGridSpec(
            num_scalar_prefetch=1, grid=(j, nb),
            in_specs=[pl.BlockSpec((None, tr, cols), lambda jj, i, c_ref: (jj, c_ref[0] * nb + i, 0)),
                      pl.BlockSpec((None, tr, cols), lambda jj, i, c_ref: (jj, i, 0))],
            out_specs=pl.BlockSpec((None, tr, cols), lambda jj, i, c_ref: (jj, i, 0))),
        out_shape=jax.ShapeDtypeStruct((j, rh, cols), BF16),
        compiler_params=_params("parallel", "parallel"),
    )(c_idx, grad, got)


def _scatter_partials(partials):
    n = len(partials)
    hbm = pl.BlockSpec(memory_space=pl.ANY)

    def body(*refs):
        ins, outs = refs[:n], refs[n:2 * n]
        send, recv, local = refs[2 * n:]
        x, y, c = _position()
        chip = 2 * x + y
        cps, lcs = [], []
        for t in range(n):
            lc = pltpu.make_async_copy(ins[t].at[chip], outs[t].at[chip], local.at[t])
            lc.start()
            lcs.append(lc)
            for k in range(1, N_CHIPS):
                px, py = _chip_peer(x, y, k)
                s = 3 * t + k - 1
                cp = pltpu.make_async_remote_copy(
                    src_ref=ins[t].at[2 * px + py], dst_ref=outs[t].at[chip], send_sem=send.at[s],
                    recv_sem=recv.at[s], device_id=(px, py, c), device_id_type=MESH)
                cp.start()
                cps.append(cp)
        for cp in cps:
            cp.wait()
        for lc in lcs:
            lc.wait()

    return pl.pallas_call(
        body, name="reduce_scatter_partials", in_specs=[hbm] * n, out_specs=[hbm] * n,
        out_shape=[jax.ShapeDtypeStruct(p.shape, BF16) for p in partials],
        scratch_shapes=[pltpu.SemaphoreType.DMA((3 * n,)), pltpu.SemaphoreType.DMA((3 * n,)),
                        pltpu.SemaphoreType.DMA((n,))],
    )(*partials)


def _sum_chips(parts, name):
    j, rh, cols = parts.shape
    tr = 128

    def body(p_ref, o_ref):
        acc = p_ref[0].astype(F32)
        for s in range(1, j):
            acc = acc + p_ref[s].astype(F32)
        o_ref[...] = acc

    return pl.pallas_call(
        body, name=name, grid=(rh // tr,),
        in_specs=[pl.BlockSpec((j, tr, cols), lambda i: (0, i, 0))],
        out_specs=pl.BlockSpec((tr, cols), lambda i: (i, 0)),
        out_shape=jax.ShapeDtypeStruct((rh, cols), F32),
        compiler_params=_params("parallel"),
    )(parts)


def _share_totals(halves):
    n = len(halves)
    hbm = pl.BlockSpec(memory_space=pl.ANY)

    def body(*refs):
        ins, outs = refs[:n], refs[n:2 * n]
        send, recv, local = refs[2 * n:]
        x, y, c = _position()
        cps, lcs = [], []
        for t in range(n):
            rh = ins[t].shape[0]
            mine = outs[t].at[pl.ds(c * rh, rh)]
            lc = pltpu.make_async_copy(ins[t], mine, local.at[t])
            lc.start()
            lcs.append(lc)
            cp = pltpu.make_async_remote_copy(
                src_ref=ins[t], dst_ref=mine, send_sem=send.at[t], recv_sem=recv.at[t],
                device_id=(x, y, 1 - c), device_id_type=MESH)
            cp.start()
            cps.append(cp)
        for cp in cps:
            cp.wait()
        for lc in lcs:
            lc.wait()

    return pl.pallas_call(
        body, name="reduce_share_totals", in_specs=[hbm] * n, out_specs=[hbm] * n,
        out_shape=[jax.ShapeDtypeStruct((2 * h.shape[0], h.shape[1]), F32) for h in halves],
        scratch_shapes=[pltpu.SemaphoreType.DMA((n,)), pltpu.SemaphoreType.DMA((n,)),
                        pltpu.SemaphoreType.DMA((n,))],
    )(*halves)


SMALL_ROWS = 56


def _reduce_small(packed, silu_c):
    ns = 3 * D_MODEL // N_CHIPS

    def body(p_ref, sc_ref, tot_ref, gw_ref, loss_ref, qk_ref, allp, send, recv):
        x, y, c = _position()
        me = 4 * x + 2 * y + c
        chip = 2 * x + y

        def copy(k):
            return pltpu.make_async_remote_copy(
                src_ref=allp.at[me], dst_ref=allp.at[me], send_sem=send.at[k - 1], recv_sem=recv.at[k - 1],
                device_id=_xor_peer(x, y, c, k), device_id_type=MESH)

        allp[me] = p_ref[...]
        for k in range(1, N_DEV):
            copy(k).start()
        for k in range(1, N_DEV):
            copy(k).wait_recv()
        tot = allp[0]
        for i in range(1, N_DEV):
            tot = tot + allp[i]
        tot_ref[...] = tot
        loss_ref[...] = jnp.sum(tot[11:12, :], axis=1, keepdims=True) * (0.5 / D_MODEL)
        fold = tot[5:11, 0:HEAD_DIM]
        for h in range(1, N_HEADS):
            fold = fold + tot[5:11, h * HEAD_DIM:(h + 1) * HEAD_DIM]
        qk_ref[...] = jnp.concatenate([fold, jnp.zeros((2, HEAD_DIM), F32)], axis=0)
        sct = sc_ref[...].T
        rc = 64
        for l in range(2):
            dms = [allp[i, pl.ds(12 + 4 * l + chip, 1), :][:, :ns] for i in range(N_DEV)]
            for r0 in range(0, D_MODEL, rc):
                acc = sct[r0:r0 + rc, 0:1] * dms[0]
                for i in range(1, N_DEV):
                    acc = acc + sct[r0:r0 + rc, i:i + 1] * dms[i]
                gw_ref[l, r0:r0 + rc, :] = acc
        for k in range(1, N_DEV):
            copy(k).wait_send()

    vm = pl.BlockSpec(memory_space=pltpu.VMEM)
    return pl.pallas_call(
        body, name="reduce_small", in_specs=[vm, vm], out_specs=[vm] * 4,
        out_shape=[jax.ShapeDtypeStruct((SMALL_ROWS, D_MODEL), F32), jax.ShapeDtypeStruct((2, D_MODEL, ns), F32),
                   jax.ShapeDtypeStruct((1, 1), F32), jax.ShapeDtypeStruct((8, HEAD_DIM), F32)],
        scratch_shapes=[pltpu.VMEM((N_DEV, SMALL_ROWS, D_MODEL), F32),
                        pltpu.SemaphoreType.DMA((N_DEV - 1,)), pltpu.SemaphoreType.DMA((N_DEV - 1,))],
        compiler_params=pltpu.CompilerParams(vmem_limit_bytes=VMEM_LIMIT_BYTES),
    )(packed, silu_c)


def _reduce_big(grads, c_idx):
    names = list(grads)
    got = _exchange_halves([grads[k] for k in names])
    partials = [_add_halves(grads[k], got[i], c_idx, f"reduce_add_{k}") for i, k in enumerate(names)]
    parts = _scatter_partials(partials)
    halves = [_sum_chips(parts[i], f"reduce_sum_{k}") for i, k in enumerate(names)]
    totals = _share_totals(halves)
    return dict(zip(names, totals))


def kernel(x, c, norm_g, ada_w, ada_b, a_w_in, a_conv_w, a_conv_b, a_ln_g, a_ln_b, a_w_out, b_w_in, b_q_norm, b_k_norm, b_w_out, loss_target, m_norm_g, m_ada_w, m_ada_b, m_a_w_in, m_a_conv_w, m_a_conv_b, m_a_ln_g, m_a_ln_b, m_a_w_out, m_b_w_in, m_b_q_norm, m_b_k_norm, m_b_w_out, v_norm_g, v_ada_w, v_ada_b, v_a_w_in, v_a_conv_w, v_a_conv_b, v_a_ln_g, v_a_ln_b, v_a_w_out, v_b_w_in, v_b_q_norm, v_b_k_norm, v_b_w_out):
    chip = 2 * lax.axis_index("x") + lax.axis_index("y")
    core = lax.axis_index("c")
    chip_idx = chip.astype(jnp.int32).reshape(1)
    dev_idx = jnp.stack([2 * chip + core, chip, core]).astype(jnp.int32)

    mods, silu_c, conv_w_full = _ada_forward(c, ada_w, ada_b, a_conv_w[0])
    lands_a = [_cast_into_slot(a_w_in[0], chip_idx, "cast_a_w_in"), _cast_into_slot(a_w_out[0], chip_idx, "cast_a_w_out")]
    send_a, recv_a, lands_a, token_a = _gather_start(lands_a, mods, "gather_start_a")
    lands_b = [_cast_into_slot(b_w_in[0], chip_idx, "cast_b_w_in"), _cast_into_slot(b_w_out[0], chip_idx, "cast_b_w_out")]
    send_b, recv_b, lands_b, token_b = _gather_start(lands_b, token_a, "gather_start_b")
    mods = mods + token_b[0:2, 0:1]

    def weights_a(after):
        send, recv, lands, _ = _gather_forward(send_a, recv_a, lands_a, after, "gather_forward_a")
        w_in, w_out = _gather_wait(send, recv, lands, after, "gather_wait_a")
        return w_in, w_out.reshape(D_MODEL, D_MODEL)

    forwarded_b = []

    def weights_b(after):
        send, recv, lands, _ = forwarded_b
        w_in, w_out = _gather_wait(send, recv, lands, after, "gather_wait_b")
        return w_in, w_out.reshape(D_MODEL, D_MODEL)

    def forward_weights_b(after):
        forwarded_b.extend(_gather_forward(send_b, recv_b, lands_b, after, "gather_forward_b"))
        return forwarded_b[3]

    stage1, stage2 = {}, {}

    def send_grads(tag, dw_in, dw_out):
        grads = [dw_in, dw_out.reshape(N_CHIPS, D_MODEL // N_CHIPS, D_MODEL)]
        send, recv, arrays, token = _reduce_sibling_start(grads, dw_out, f"reduce_d2d_start_{tag}")
        stage1[tag] = (send, recv, arrays)
        return token

    def forward_grads(tag, after):
        send, recv, arrays = stage1[tag]
        grads, got = _reduce_sibling_wait(send, recv, arrays, after, f"reduce_d2d_wait_{tag}")
        partials = [_add_sibling_half(grads[i], got[i], dev_idx, f"reduce_add_{tag}_{i}") for i in range(2)]
        send, recv, arrays, token = _reduce_chips_start(partials, partials[1], f"reduce_ici_start_{tag}")
        stage2[tag] = (send, recv, arrays)
        return token

    def finish_grads(tag, after):
        send, recv, arrays = stage2[tag]
        partials, lands = _reduce_chips_wait(send, recv, arrays, after, f"reduce_ici_wait_{tag}")
        totals = [_sum_partials(lands[i], partials[i], dev_idx, f"reduce_sum_{tag}_{i}") for i in range(2)]
        return _share_halves(totals)

    grad_x, small = _local_step(
        x[0], loss_target[0], mods.reshape(2, 3, D_MODEL), norm_g, conv_w_full, a_conv_b, a_ln_g[0:1],
        a_ln_b[0:1], b_q_norm[0], b_k_norm[0], weights_a, weights_b, forward_weights_b,
        functools.partial(send_grads, "b"), functools.partial(forward_grads, "b"), functools.partial(send_grads, "a"))

    ns = 3 * D_MODEL // N_CHIPS
    pad_mod = lambda dm: jnp.pad(dm.reshape(N_CHIPS, ns), ((0, 0), (0, D_MODEL - ns)))
    packed = jnp.concatenate([
        small["dnorm_g"], small["dconv_b"], small["dln_g"], small["dln_b"], small["dq_norm"], small["dk_norm"],
        small["loss_cols"], pad_mod(small["dmod0"]), pad_mod(small["dmod1"]), small["dconv_w"],
        jnp.zeros((SMALL_ROWS - 20 - CONV_WIDTH, D_MODEL), F32)], axis=0)
    tot, g_ada_w, loss, qk = _reduce_small(packed, silu_c)
    cw = D_MODEL // N_CHIPS
    g_small = dict(
        norm_g=tot[0:2], a_conv_b=tot[2:3], a_ln_g=tot[3:4], a_ln_b=tot[4:5],
        b_q_norm=qk[0:3], b_k_norm=qk[3:6],
        ada_b=jnp.stack([tot[12:16, :ns].reshape(3 * D_MODEL), tot[16:20, :ns].reshape(3 * D_MODEL)]),
        a_conv_w=lax.dynamic_slice(tot[20:20 + CONV_WIDTH], (0, chip * cw), (CONV_WIDTH, cw)),
    )


    given = dict(norm_g=(norm_g, m_norm_g, v_norm_g), ada_w=(ada_w, m_ada_w, v_ada_w), ada_b=(ada_b, m_ada_b, v_ada_b),
                 a_w_in=(a_w_in, m_a_w_in, v_a_w_in), a_conv_w=(a_conv_w, m_a_conv_w, v_a_conv_w),
                 a_conv_b=(a_conv_b, m_a_conv_b, v_a_conv_b), a_ln_g=(a_ln_g, m_a_ln_g, v_a_ln_g),
                 a_ln_b=(a_ln_b, m_a_ln_b, v_a_ln_b), a_w_out=(a_w_out, m_a_w_out, v_a_w_out),
                 b_w_in=(b_w_in, m_b_w_in, v_b_w_in), b_q_norm=(b_q_norm, m_b_q_norm, v_b_q_norm),
                 b_k_norm=(b_k_norm, m_b_k_norm, v_b_k_norm), b_w_out=(b_w_out, m_b_w_out, v_b_w_out))
    order = ["norm_g", "ada_w", "ada_b", "a_w_in", "a_conv_w", "a_conv_b", "a_ln_g", "a_ln_b", "a_w_out", "b_w_in",
             "b_q_norm", "b_k_norm", "b_w_out"]
    outs = {}

    def update(k, g2):
        w, m, v = given[k]
        shape2 = g2.shape
        d2, m2, v2 = _adamw(w.reshape(shape2), g2, m.reshape(shape2), v.reshape(shape2), f"adamw_{k}")
        outs[k] = tuple(a.reshape(w.shape) for a in (g2, d2, m2, v2))

    token = forward_grads("a", tot)
    g_b_in, g_b_out = finish_grads("b", token)
    update("b_w_in", g_b_in)
    update("b_w_out", g_b_out)
    update("ada_w", g_ada_w.reshape(2 * D_MODEL, ns))
    for k, g2 in g_small.items():
        update(k, g2)
    g_a_in, g_a_out = finish_grads("a", outs["b_w_in"][1])
    update("a_w_in", g_a_in)
    update("a_w_out", g_a_out)
    return (loss.reshape(()), grad_x[None], *[outs[k][0] for k in order], *[outs[k][1] for k in order],
            *[outs[k][2] for k in order], *[outs[k][3] for k in order])
```

```python
import functools

import jax
import jax.numpy as jnp
from jax import lax
from jax.experimental import pallas as pl
from jax.experimental.pallas import tpu as pltpu

F32 = jnp.float32
BF16 = jnp.bfloat16

SEQ = 2048
D_MODEL = 1024
CONV_WIDTH = 31
HEAD_DIM = 64
N_HEADS = 16
DILATIONS = (1, 4, 16)
ATTN_BLOCK = 128
NORM_EPS = 1e-6
NEG_INF = -1e30
N_DEV = 8
N_CHIPS = 4

ADAM_LR = 0.001
ADAM_B1 = 0.9
ADAM_B2 = 0.999
ADAM_EPS = 1e-08
ADAM_WD = 0.01
ADAM_STEP = 10

VMEM_LIMIT_BYTES = 52 * 1024 * 1024
HALO = 32
MESH = pl.DeviceIdType.MESH


def _params(*sem):
    return pltpu.CompilerParams(dimension_semantics=sem or None, vmem_limit_bytes=VMEM_LIMIT_BYTES)


def _sigmoid(v):
    return 1.0 / (1.0 + jnp.exp(-v))


def _row_spec(tm, cols, col_block=0):
    return pl.BlockSpec((tm, cols), lambda i: (i, col_block))


def _vec_spec(rows, cols):
    return pl.BlockSpec((rows, cols), lambda i: (0, 0))


def _normmod(xv, g, scale, shift):
    r = lax.rsqrt(jnp.mean(xv * xv, axis=-1, keepdims=True) + NORM_EPS)
    return xv * r * g * (1.0 + scale) + shift


def _normmod_fwd(x, g, scale, shift, name):
    tm = 256

    def body(x_ref, g_ref, sc_ref, sh_ref, h_ref, ht_ref):
        h = _normmod(x_ref[...], g_ref[...], sc_ref[...], sh_ref[...])
        h_ref[...] = h.astype(BF16)
        ht_ref[...] = h.T.astype(BF16)

    return pl.pallas_call(
        body, name=name, grid=(SEQ // tm,),
        in_specs=[_row_spec(tm, D_MODEL)] + [_vec_spec(1, D_MODEL)] * 3,
        out_specs=[_row_spec(tm, D_MODEL), pl.BlockSpec((D_MODEL, tm), lambda i: (0, i))],
        out_shape=[jax.ShapeDtypeStruct((SEQ, D_MODEL), BF16), jax.ShapeDtypeStruct((D_MODEL, SEQ), BF16)],
        compiler_params=_params("parallel"),
    )(x, g, scale, shift)


def _normmod_bwd(x, g, scale, dh_parts, dres, name):
    tm = 256
    n_parts = len(dh_parts)

    def body(x_ref, g_ref, sc_ref, dres_ref, *rest):
        part_refs = rest[:n_parts]
        dx_ref, sums_ref = rest[n_parts:]
        xv = x_ref[...]
        r = lax.rsqrt(jnp.mean(xv * xv, axis=-1, keepdims=True) + NORM_EPS)
        xn = xv * r
        dh = part_refs[0][...]
        for p in part_refs[1:]:
            dh = dh + p[...]
        gv = g_ref[...]
        one_sc = 1.0 + sc_ref[...]
        dxn = dh * (gv * one_sc)
        dx = r * (dxn - xn * jnp.mean(dxn * xn, axis=-1, keepdims=True))
        dx_ref[...] = dres_ref[...] + dx
        dhx = dh * xn
        sums = jnp.concatenate([
            jnp.sum(dhx, axis=0, keepdims=True) * one_sc,
            jnp.sum(dhx, axis=0, keepdims=True) * gv,
            jnp.sum(dh, axis=0, keepdims=True),
            jnp.zeros((5, D_MODEL), F32)], axis=0)

        @pl.when(pl.program_id(0) == 0)
        def _():
            sums_ref[...] = jnp.zeros_like(sums_ref)

        sums_ref[...] += sums

    return pl.pallas_call(
        body, name=name, grid=(SEQ // tm,),
        in_specs=[_row_spec(tm, D_MODEL), _vec_spec(1, D_MODEL), _vec_spec(1, D_MODEL), _row_spec(tm, D_MODEL)]
        + [_row_spec(tm, D_MODEL)] * n_parts,
        out_specs=[_row_spec(tm, D_MODEL), _vec_spec(8, D_MODEL)],
        out_shape=[jax.ShapeDtypeStruct((SEQ, D_MODEL), F32), jax.ShapeDtypeStruct((8, D_MODEL), F32)],
        compiler_params=_params("arbitrary"),
    )(x, g, scale, dres, *dh_parts)


def _mm(lhs, rhs, *, tn, tile0, n_tiles, out_dtype, name, out3d=None, prev=None):
    mo, kc = lhs.shape
    cm = 512

    def body(l_ref, r_ref, *rest):
        o_ref = rest[-1]
        for m in range(mo // cm):
            rows = pl.ds(m * cm, cm)
            o_ref[rows, :] = jnp.dot(l_ref[rows, :], r_ref[...], preferred_element_type=F32).astype(out_dtype)

    if rhs.ndim == 3:
        tps_r = rhs.shape[2] // tn
        r_spec = pl.BlockSpec((None, kc, tn), lambda t: ((tile0 + t) // tps_r, 0, (tile0 + t) % tps_r))
    else:
        r_spec = pl.BlockSpec((kc, tn), lambda t: (0, t))
    in_specs = [pl.BlockSpec((mo, kc), lambda t: (0, 0)), r_spec]
    args = [lhs, rhs]
    aliases = {}
    if out3d is None:
        o_spec = pl.BlockSpec((mo, tn), lambda t: (0, t))
        o_shape = jax.ShapeDtypeStruct((mo, n_tiles * tn), out_dtype)
    else:
        j_out, ns_out = out3d
        tps_o = ns_out // tn
        o_spec = pl.BlockSpec((None, mo, tn), lambda t: ((tile0 + t) // tps_o, 0, (tile0 + t) % tps_o))
        o_shape = jax.ShapeDtypeStruct((j_out, mo, ns_out), out_dtype)
        if prev is not None:
            in_specs.append(pl.BlockSpec(memory_space=pl.ANY))
            args.append(prev)
            aliases = {2: 0}
    return pl.pallas_call(
        body, name=name, grid=(n_tiles,), in_specs=in_specs, out_specs=o_spec, out_shape=o_shape,
        input_output_aliases=aliases, compiler_params=_params("parallel"),
    )(*args)


def _mm_nt(dy, w3, *, tn, tile0, n_tiles, name, after=None):
    m_rows = dy.shape[0]
    _, kc, ns = w3.shape
    tps = ns // tn
    cm = 512
    extra = [] if after is None else [after]

    def body(dy_ref, w_ref, *rest):
        o_ref = rest[-1]

        @pl.when(pl.program_id(0) == 0)
        def _():
            o_ref[...] = jnp.zeros_like(o_ref)

        for m in range(m_rows // cm):
            rows = pl.ds(m * cm, cm)
            o_ref[rows, :] += lax.dot_general(dy_ref[rows, :], w_ref[...], (((1,), (1,)), ((), ())),
                                              preferred_element_type=F32)

    return pl.pallas_call(
        body, name=name, grid=(n_tiles,),
        in_specs=[pl.BlockSpec((m_rows, tn), lambda t: (0, t)),
                  pl.BlockSpec((None, kc, tn), lambda t: ((tile0 + t) // tps, 0, (tile0 + t) % tps))]
        + [pl.BlockSpec(memory_space=pl.ANY)] * len(extra),
        out_specs=pl.BlockSpec((m_rows, kc), lambda t: (0, 0)),
        out_shape=jax.ShapeDtypeStruct((m_rows, kc), F32),
        compiler_params=_params("arbitrary"),
    )(dy, w3, *extra)


CONV_CHUNK = 16


def _shift_copies(buf, shifted):
    rows = shifted.shape[1]
    for s in range(1, 8):
        shifted[s - 1] = buf[pl.ds(s, rows), :]


def _shifted_rows(buf, shifted, offset, r0):
    s = offset % 8
    if s == 0:
        return buf[pl.ds(r0 + offset, CONV_CHUNK), :]
    return shifted[s - 1, pl.ds(r0 + (offset - s), CONV_CHUNK), :]


def _conv_fwd(proj, conv_w, conv_b, ln_g, ln_b, name):
    tm = 256
    hb = tm // HALO

    def body(vg_ref, halo_ref, z_ref, w_ref, b_ref, g_ref, be_ref, u5_ref, u5t_ref, u2_ref, buf, shifted):
        i = pl.program_id(0)
        u1 = vg_ref[:, :D_MODEL] * _sigmoid(vg_ref[:, D_MODEL:])
        u1h = halo_ref[:, :D_MODEL] * _sigmoid(halo_ref[:, D_MODEL:])
        buf[pl.ds(0, HALO), :] = jnp.where(i > 0, u1h, 0.0)
        buf[pl.ds(HALO, tm), :] = u1
        _shift_copies(buf, shifted)

        def chunk(ci, carry):
            r0 = pl.multiple_of(ci * CONV_CHUNK, CONV_CHUNK)
            acc = jnp.broadcast_to(b_ref[...], (CONV_CHUNK, D_MODEL))
            for k in range(CONV_WIDTH):
                acc = acc + w_ref[k:k + 1, :] * _shifted_rows(buf, shifted, HALO - (CONV_WIDTH - 1) + k, r0)
            u2_ref[pl.ds(r0, CONV_CHUNK), :] = acc
            return carry

        lax.fori_loop(0, tm // CONV_CHUNK, chunk, 0)
        acc = u2_ref[...]
        mu = jnp.mean(acc, axis=-1, keepdims=True)
        xc = acc - mu
        rstd = lax.rsqrt(jnp.mean(xc * xc, axis=-1, keepdims=True) + NORM_EPS)
        u3 = xc * rstd * g_ref[...] + be_ref[...]
        zv = z_ref[...]
        u5 = u3 * _sigmoid(u3) * (zv * _sigmoid(zv))
        u5_ref[...] = u5.astype(BF16)
        u5t_ref[...] = u5.T.astype(BF16)

    return pl.pallas_call(
        body, name=name, grid=(SEQ // tm,),
        in_specs=[pl.BlockSpec((tm, 2 * D_MODEL), lambda i: (i, 0)),
                  pl.BlockSpec((HALO, 2 * D_MODEL), lambda i: (jnp.maximum(i * hb - 1, 0), 0)),
                  _row_spec(tm, D_MODEL, 2),
                  _vec_spec(CONV_WIDTH, D_MODEL)] + [_vec_spec(1, D_MODEL)] * 3,
        out_specs=[_row_spec(tm, D_MODEL), pl.BlockSpec((D_MODEL, tm), lambda i: (0, i)), _row_spec(tm, D_MODEL)],
        out_shape=[jax.ShapeDtypeStruct((SEQ, D_MODEL), BF16), jax.ShapeDtypeStruct((D_MODEL, SEQ), BF16),
                   jax.ShapeDtypeStruct((SEQ, D_MODEL), F32)],
        scratch_shapes=[pltpu.VMEM((HALO + tm, D_MODEL), F32), pltpu.VMEM((7, HALO + tm - 8, D_MODEL), F32)],
        compiler_params=_params("parallel"),
    )(proj, proj, proj, conv_w, conv_b, ln_g, ln_b)


def _conv_bwd_pointwise(du5, proj, u2, ln_g, ln_b, name):
    tm = 256

    def body(du5_ref, z_ref, u2_ref, g_ref, be_ref, du2_ref, dz_ref, sums_ref):
        u2v = u2_ref[...]
        mu = jnp.mean(u2v, axis=-1, keepdims=True)
        xc = u2v - mu
        rstd = lax.rsqrt(jnp.mean(xc * xc, axis=-1, keepdims=True) + NORM_EPS)
        xhat = xc * rstd
        u3 = xhat * g_ref[...] + be_ref[...]
        s3 = _sigmoid(u3)
        u4 = u3 * s3
        zv = z_ref[...]
        sz = _sigmoid(zv)
        du5v = du5_ref[...]
        dz_ref[...] = du5v * u4 * (sz * (1.0 + zv * (1.0 - sz)))
        du3 = du5v * (zv * sz) * (s3 * (1.0 + u3 * (1.0 - s3)))
        dxhat = du3 * g_ref[...]
        du2 = rstd * (dxhat - jnp.mean(dxhat, axis=-1, keepdims=True)
                      - xhat * jnp.mean(dxhat * xhat, axis=-1, keepdims=True))
        du2_ref[...] = du2
        sums = jnp.concatenate([
            jnp.sum(du3 * xhat, axis=0, keepdims=True),
            jnp.sum(du3, axis=0, keepdims=True),
            jnp.sum(du2, axis=0, keepdims=True),
            jnp.zeros((5, D_MODEL), F32)], axis=0)

        @pl.when(pl.program_id(0) == 0)
        def _():
            sums_ref[...] = jnp.zeros_like(sums_ref)

        sums_ref[...] += sums

    return pl.pallas_call(
        body, name=name, grid=(SEQ // tm,),
        in_specs=[_row_spec(tm, D_MODEL), _row_spec(tm, D_MODEL, 2), _row_spec(tm, D_MODEL),
                  _vec_spec(1, D_MODEL), _vec_spec(1, D_MODEL)],
        out_specs=[_row_spec(tm, D_MODEL), _row_spec(tm, D_MODEL), _vec_spec(8, D_MODEL)],
        out_shape=[jax.ShapeDtypeStruct((SEQ, D_MODEL), F32), jax.ShapeDtypeStruct((SEQ, D_MODEL), F32),
                   jax.ShapeDtypeStruct((8, D_MODEL), F32)],
        compiler_params=_params("arbitrary"),
    )(du5, proj, u2, ln_g, ln_b)


def _conv_bwd_taps(du2, dz, proj, conv_w, name):
    tm = 256
    hb = tm // HALO
    n_blocks = SEQ // tm

    def body(du2_ref, dnext_ref, dz_ref, vg_ref, halo_ref, w_ref, dproj_ref, dw_ref,
             ubuf, dbuf, ushift, dshift, sgbuf, dwacc):
        i = pl.program_id(0)
        sg = _sigmoid(vg_ref[:, D_MODEL:])
        sgbuf[...] = sg
        u1h = halo_ref[:, :D_MODEL] * _sigmoid(halo_ref[:, D_MODEL:])
        ubuf[pl.ds(0, HALO), :] = jnp.where(i > 0, u1h, 0.0)
        ubuf[pl.ds(HALO, tm), :] = vg_ref[:, :D_MODEL] * sg
        dbuf[pl.ds(0, tm), :] = du2_ref[...]
        dbuf[pl.ds(tm, HALO), :] = jnp.where(i < n_blocks - 1, dnext_ref[...], 0.0)
        _shift_copies(ubuf, ushift)
        _shift_copies(dbuf, dshift)

        @pl.when(i == 0)
        def _():
            dwacc[...] = jnp.zeros_like(dwacc)

        def chunk(ci, carry):
            r0 = pl.multiple_of(ci * CONV_CHUNK, CONV_CHUNK)
            rows = pl.ds(r0, CONV_CHUNK)
            du2c = du2_ref[rows, :]
            du1 = jnp.zeros((CONV_CHUNK, D_MODEL), F32)
            for k in range(CONV_WIDTH):
                du1 = du1 + w_ref[k:k + 1, :] * _shifted_rows(dbuf, dshift, CONV_WIDTH - 1 - k, r0)
                prod = du2c * _shifted_rows(ubuf, ushift, HALO - (CONV_WIDTH - 1) + k, r0)
                dwacc[k] += prod[0:8] + prod[8:16]
            sgc = sgbuf[rows, :]
            dval = du1 * sgc
            dproj_ref[rows, 0:D_MODEL] = dval.astype(BF16)
            dproj_ref[rows, D_MODEL:2 * D_MODEL] = (dval * vg_ref[rows, 0:D_MODEL] * (1.0 - sgc)).astype(BF16)
            return carry

        lax.fori_loop(0, tm // CONV_CHUNK, chunk, 0)
        dproj_ref[:, 2 * D_MODEL:] = dz_ref[...].astype(BF16)

        @pl.when(i == n_blocks - 1)
        def _():
            for k in range(CONV_WIDTH):
                dw_ref[k:k + 1, :] = jnp.sum(dwacc[k], axis=0, keepdims=True)
            dw_ref[CONV_WIDTH:, :] = jnp.zeros((32 - CONV_WIDTH, D_MODEL), F32)

    return pl.pallas_call(
        body, name=name, grid=(n_blocks,),
        in_specs=[_row_spec(tm, D_MODEL),
                  pl.BlockSpec((HALO, D_MODEL), lambda i: (jnp.minimum((i + 1) * hb, SEQ // HALO - 1), 0)),
                  _row_spec(tm, D_MODEL),
                  pl.BlockSpec((tm, 2 * D_MODEL), lambda i: (i, 0)),
                  pl.BlockSpec((HALO, 2 * D_MODEL), lambda i: (jnp.maximum(i * hb - 1, 0), 0)),
                  _vec_spec(CONV_WIDTH, D_MODEL)],
        out_specs=[_row_spec(tm, 3 * D_MODEL), _vec_spec(32, D_MODEL)],
        out_shape=[jax.ShapeDtypeStruct((SEQ, 3 * D_MODEL), BF16), jax.ShapeDtypeStruct((32, D_MODEL), F32)],
        scratch_shapes=[pltpu.VMEM((HALO + tm, D_MODEL), F32), pltpu.VMEM((tm + HALO, D_MODEL), F32),
                        pltpu.VMEM((7, HALO + tm - 8, D_MODEL), F32), pltpu.VMEM((7, HALO + tm - 8, D_MODEL), F32),
                        pltpu.VMEM((tm, D_MODEL), F32), pltpu.VMEM((CONV_WIDTH, 8, D_MODEL), F32)],
        compiler_params=_params("arbitrary"),
    )(du2, du2, dz, proj, proj, conv_w)


def _out_a(u5, w_out, x, gate, g1, scale1, shift1, name):
    tm = 256

    def body(u_ref, w_ref, x_ref, gate_ref, g_ref, sc_ref, sh_ref, x1_ref, y_ref, h_ref, ht_ref):
        y = jnp.dot(u_ref[...], w_ref[...], preferred_element_type=F32)
        x1 = x_ref[...] + gate_ref[...] * y
        y_ref[...] = y
        x1_ref[...] = x1
        h = _normmod(x1, g_ref[...], sc_ref[...], sh_ref[...])
        h_ref[...] = h.astype(BF16)
        ht_ref[...] = h.T.astype(BF16)

    return pl.pallas_call(
        body, name=name, grid=(SEQ // tm,),
        in_specs=[_row_spec(tm, D_MODEL), _vec_spec(D_MODEL, D_MODEL), _row_spec(tm, D_MODEL)]
        + [_vec_spec(1, D_MODEL)] * 4,
        out_specs=[_row_spec(tm, D_MODEL), _row_spec(tm, D_MODEL), _row_spec(tm, D_MODEL),
                   pl.BlockSpec((D_MODEL, tm), lambda i: (0, i))],
        out_shape=[jax.ShapeDtypeStruct((SEQ, D_MODEL), F32), jax.ShapeDtypeStruct((SEQ, D_MODEL), F32),
                   jax.ShapeDtypeStruct((SEQ, D_MODEL), BF16), jax.ShapeDtypeStruct((D_MODEL, SEQ), BF16)],
        compiler_params=_params("parallel"),
    )(u5, w_out, x, gate, g1, scale1, shift1)


def _out_b_loss(u, w_out, x1, gate, target, name):
    tm = 256

    def body(u_ref, w_ref, x_ref, gate_ref, t_ref, e_ref, dy_ref, sums_ref):
        y = jnp.dot(u_ref[...], w_ref[...], preferred_element_type=F32)
        diff = x_ref[...] + gate_ref[...] * y - t_ref[...]
        e = diff * (1.0 / D_MODEL)
        e_ref[...] = e
        dy_ref[...] = (e * gate_ref[...]).astype(BF16)
        sums = jnp.concatenate([
            jnp.sum(e * y, axis=0, keepdims=True),
            jnp.sum(diff * diff, axis=0, keepdims=True),
            jnp.zeros((6, D_MODEL), F32)], axis=0)

        @pl.when(pl.program_id(0) == 0)
        def _():
            sums_ref[...] = jnp.zeros_like(sums_ref)

        sums_ref[...] += sums

    return pl.pallas_call(
        body, name=name, grid=(SEQ // tm,),
        in_specs=[_row_spec(tm, D_MODEL), _vec_spec(D_MODEL, D_MODEL), _row_spec(tm, D_MODEL),
                  _vec_spec(1, D_MODEL), _row_spec(tm, D_MODEL)],
        out_specs=[_row_spec(tm, D_MODEL), _row_spec(tm, D_MODEL), _vec_spec(8, D_MODEL)],
        out_shape=[jax.ShapeDtypeStruct((SEQ, D_MODEL), F32), jax.ShapeDtypeStruct((SEQ, D_MODEL), BF16),
                   jax.ShapeDtypeStruct((8, D_MODEL), F32)],
        compiler_params=_params("arbitrary"),
    )(u, w_out, x1, gate, target)


def _dgate_dy(dx1, y, gate, name):
    tm = 256

    def body(d_ref, y_ref, gate_ref, dy_ref, sums_ref):
        dv = d_ref[...]
        dy_ref[...] = (dv * gate_ref[...]).astype(BF16)
        sums = jnp.concatenate([jnp.sum(dv * y_ref[...], axis=0, keepdims=True), jnp.zeros((7, D_MODEL), F32)], axis=0)

        @pl.when(pl.program_id(0) == 0)
        def _():
            sums_ref[...] = jnp.zeros_like(sums_ref)

        sums_ref[...] += sums

    return pl.pallas_call(
        body, name=name, grid=(SEQ // tm,),
        in_specs=[_row_spec(tm, D_MODEL), _row_spec(tm, D_MODEL), _vec_spec(1, D_MODEL)],
        out_specs=[_row_spec(tm, D_MODEL), _vec_spec(8, D_MODEL)],
        out_shape=[jax.ShapeDtypeStruct((SEQ, D_MODEL), BF16), jax.ShapeDtypeStruct((8, D_MODEL), F32)],
        compiler_params=_params("arbitrary"),
    )(dx1, y, gate)


def _mm_nt_res(dy, w, name):
    tm = 256
    kc, n = w.shape

    def body(dy_ref, w_ref, o_ref):
        o_ref[...] = lax.dot_general(dy_ref[...], w_ref[...], (((1,), (1,)), ((), ())), preferred_element_type=F32)

    return pl.pallas_call(
        body, name=name, grid=(SEQ // tm,),
        in_specs=[_row_spec(tm, n), _vec_spec(kc, n)],
        out_specs=_row_spec(tm, kc),
        out_shape=jax.ShapeDtypeStruct((SEQ, kc), F32),
        compiler_params=_params("parallel"),
    )(dy, w)


def _seg_matrix():
    r = lax.broadcasted_iota(jnp.int32, (256, 256), 0) // HEAD_DIM
    c = lax.broadcasted_iota(jnp.int32, (256, 256), 1) // HEAD_DIM
    return (r == c).astype(BF16)


def _segsum(v, seg):
    hi = v.astype(BF16)
    lo = (v - hi.astype(F32)).astype(BF16)
    outs = []
    for c0 in range(0, D_MODEL, 256):
        outs.append(jnp.dot(hi[:, c0:c0 + 256], seg, preferred_element_type=F32)
                    + jnp.dot(lo[:, c0:c0 + 256], seg, preferred_element_type=F32))
    return jnp.concatenate(outs, axis=1)


def _qk_rstd(v, seg):
    return lax.rsqrt(_segsum(v * v, seg) * (1.0 / HEAD_DIM) + NORM_EPS)


def _qknorm_fwd(proj, qw, kw, seg, name):
    tm = 256

    def body(p_ref, qw_ref, kw_ref, seg_ref, q_ref, k_ref, v_ref):
        segv = seg_ref[...]
        q = p_ref[:, :D_MODEL]
        k = p_ref[:, D_MODEL:2 * D_MODEL]
        q_ref[...] = (q * _qk_rstd(q, segv) * qw_ref[...]).astype(BF16)
        k_ref[...] = (k * _qk_rstd(k, segv) * kw_ref[...]).astype(BF16)
        v_ref[...] = p_ref[:, 2 * D_MODEL:].astype(BF16)

    return pl.pallas_call(
        body, name=name, grid=(SEQ // tm,),
        in_specs=[_row_spec(tm, 3 * D_MODEL), _vec_spec(1, D_MODEL), _vec_spec(1, D_MODEL), _vec_spec(256, 256)],
        out_specs=[_row_spec(tm, D_MODEL)] * 3,
        out_shape=[jax.ShapeDtypeStruct((SEQ, D_MODEL), BF16)] * 3,
        compiler_params=_params("parallel"),
    )(proj, qw, kw, seg)


def _attn_masks(b, bpc, dilation, slope):
    if bpc == 1:
        qi = lax.broadcasted_iota(jnp.int32, (ATTN_BLOCK, ATTN_BLOCK), 0)
        kj = lax.broadcasted_iota(jnp.int32, (ATTN_BLOCK, ATTN_BLOCK), 1)
        steps = qi - kj
        return (steps * dilation).astype(F32), steps >= 0
    qi = lax.broadcasted_iota(jnp.int32, (ATTN_BLOCK, 2 * ATTN_BLOCK), 0)
    kj = lax.broadcasted_iota(jnp.int32, (ATTN_BLOCK, 2 * ATTN_BLOCK), 1)
    steps = qi + ATTN_BLOCK - kj
    has_prev = (b % bpc) != 0
    valid = (steps >= 0) & (steps <= ATTN_BLOCK) & (has_prev | (kj >= ATTN_BLOCK))
    return (steps * dilation).astype(F32), valid


def _key_tile(prev_ref, cur_ref, cols, bpc):
    if bpc == 1:
        return cur_ref[:, cols]
    return jnp.concatenate([prev_ref[:, cols], cur_ref[:, cols]], axis=0)


ATTN_HEADS_FWD = 8
ATTN_HEADS_BWD = 4
NT_DIMS = (((1,), (1,)), ((), ()))
TN_DIMS = (((0,), (0,)), ((), ()))


def _attn_specs(heads):
    cur = pl.BlockSpec((ATTN_BLOCK, heads * HEAD_DIM), lambda hg, b: (b, hg))
    prev = pl.BlockSpec((ATTN_BLOCK, heads * HEAD_DIM), lambda hg, b: (jnp.maximum(b - 1, 0), hg))
    return cur, prev


def _attn_fwd(q, k, v, slopes, dilation, name):
    bpc = SEQ // dilation // ATTN_BLOCK
    heads = ATTN_HEADS_FWD
    cur, prev = _attn_specs(heads)
    scale = HEAD_DIM ** -0.5

    def body(sl_ref, q_ref, kp_ref, kc_ref, vp_ref, vc_ref, o_ref, lse_ref):
        hg = pl.program_id(0)
        dist, valid = _attn_masks(pl.program_id(1), bpc, dilation, None)
        for h in range(heads):
            cols = slice(h * HEAD_DIM, (h + 1) * HEAD_DIM)
            kcat = _key_tile(kp_ref, kc_ref, cols, bpc)
            vcat = _key_tile(vp_ref, vc_ref, cols, bpc)
            s = lax.dot_general(q_ref[:, cols], kcat, NT_DIMS, preferred_element_type=F32)
            s = jnp.where(valid, s * scale - dist * sl_ref[heads * hg + h], NEG_INF)
            m = jnp.max(s, axis=-1, keepdims=True)
            p = jnp.exp(s - m)
            l = jnp.sum(p, axis=-1, keepdims=True)
            acc = jnp.dot(p.astype(BF16), vcat, preferred_element_type=F32)
            o_ref[:, cols] = acc / l
            lse_ref[:, cols] = jnp.broadcast_to(m + jnp.log(l), (ATTN_BLOCK, HEAD_DIM))

    return pl.pallas_call(
        body, name=name, grid=(N_HEADS // heads, SEQ // ATTN_BLOCK),
        in_specs=[pl.BlockSpec(memory_space=pltpu.SMEM), cur, prev, cur, prev, cur],
        out_specs=[cur, cur],
        out_shape=[jax.ShapeDtypeStruct((SEQ, D_MODEL), F32)] * 2,
        compiler_params=_params("parallel", "parallel"),
    )(slopes, q, k, k, v, v)


def _merge_fwd(o_parts, lse_parts, z, name):
    tm = 256

    def body(o0, o1, o2, l0, l1, l2, z_ref, u_ref, ut_ref, o_ref, lse_ref):
        ls = [l0[...], l1[...], l2[...]]
        m = jnp.maximum(jnp.maximum(ls[0], ls[1]), ls[2])
        tot = m + jnp.log(jnp.exp(ls[0] - m) + jnp.exp(ls[1] - m) + jnp.exp(ls[2] - m))
        o = (jnp.exp(ls[0] - tot) * o0[...] + jnp.exp(ls[1] - tot) * o1[...] + jnp.exp(ls[2] - tot) * o2[...])
        zv = z_ref[...]
        u = o * (zv * _sigmoid(zv))
        u_ref[...] = u.astype(BF16)
        ut_ref[...] = u.T.astype(BF16)
        o_ref[...] = o
        lse_ref[...] = tot

    return pl.pallas_call(
        body, name=name, grid=(SEQ // tm,),
        in_specs=[_row_spec(tm, D_MODEL)] * 7,
        out_specs=[_row_spec(tm, D_MODEL), pl.BlockSpec((D_MODEL, tm), lambda i: (0, i)),
                   _row_spec(tm, D_MODEL), _row_spec(tm, D_MODEL)],
        out_shape=[jax.ShapeDtypeStruct((SEQ, D_MODEL), BF16), jax.ShapeDtypeStruct((D_MODEL, SEQ), BF16),
                   jax.ShapeDtypeStruct((SEQ, D_MODEL), F32), jax.ShapeDtypeStruct((SEQ, D_MODEL), F32)],
        compiler_params=_params("parallel"),
    )(*o_parts, *lse_parts, z)


def _merge_bwd(du, o, z, seg, name):
    tm = 256

    def body(du_ref, o_ref, z_ref, seg_ref, do_ref, dz_ref, delta_ref):
        zv = z_ref[...]
        sz = _sigmoid(zv)
        duv = du_ref[...]
        ov = o_ref[...]
        do = duv * (zv * sz)
        do_ref[...] = do.astype(BF16)
        dz_ref[...] = (duv * ov * (sz * (1.0 + zv * (1.0 - sz)))).astype(BF16)
        delta_ref[...] = _segsum(do * ov, seg_ref[...])

    return pl.pallas_call(
        body, name=name, grid=(SEQ // tm,),
        in_specs=[_row_spec(tm, D_MODEL)] * 3 + [_vec_spec(256, 256)],
        out_specs=[_row_spec(tm, D_MODEL)] * 3,
        out_shape=[jax.ShapeDtypeStruct((SEQ, D_MODEL), BF16), jax.ShapeDtypeStruct((SEQ, D_MODEL), BF16),
                   jax.ShapeDtypeStruct((SEQ, D_MODEL), F32)],
        compiler_params=_params("parallel"),
    )(du, o, z, seg)


def _attn_bwd(q, k, v, do, lse, delta, slopes, dilation, name):
    bpc = SEQ // dilation // ATTN_BLOCK
    heads = ATTN_HEADS_BWD
    cur, prev = _attn_specs(heads)
    scale = HEAD_DIM ** -0.5

    def body(sl_ref, q_ref, kp_ref, kc_ref, vp_ref, vc_ref, do_ref, lse_ref, dl_ref,
             dq_ref, dkc_ref, dkp_ref, dvc_ref, dvp_ref):
        hg = pl.program_id(0)
        dist, valid = _attn_masks(pl.program_id(1), bpc, dilation, None)
        for h in range(heads):
            cols = slice(h * HEAD_DIM, (h + 1) * HEAD_DIM)
            kcat = _key_tile(kp_ref, kc_ref, cols, bpc)
            vcat = _key_tile(vp_ref, vc_ref, cols, bpc)
            qh = q_ref[:, cols]
            doh = do_ref[:, cols]
            lse_col = lse_ref[:, h * HEAD_DIM:h * HEAD_DIM + 1]
            dl_col = dl_ref[:, h * HEAD_DIM:h * HEAD_DIM + 1]
            s = lax.dot_general(qh, kcat, NT_DIMS, preferred_element_type=F32)
            p = jnp.exp(jnp.where(valid, s * scale - dist * sl_ref[heads * hg + h], NEG_INF) - lse_col)
            dp = lax.dot_general(doh, vcat, NT_DIMS, preferred_element_type=F32)
            ds = (p * (dp - dl_col) * scale).astype(BF16)
            dq_ref[:, cols] = jnp.dot(ds, kcat, preferred_element_type=F32)
            dk = lax.dot_general(ds, qh, TN_DIMS, preferred_element_type=F32)
            dv = lax.dot_general(p.astype(BF16), doh, TN_DIMS, preferred_element_type=F32)
            if bpc == 1:
                zeros = jnp.zeros((ATTN_BLOCK, HEAD_DIM), F32)
                dk, dv = jnp.concatenate([zeros, dk], axis=0), jnp.concatenate([zeros, dv], axis=0)
            dkp_ref[:, cols] = dk[:ATTN_BLOCK]
            dkc_ref[:, cols] = dk[ATTN_BLOCK:]
            dvp_ref[:, cols] = dv[:ATTN_BLOCK]
            dvc_ref[:, cols] = dv[ATTN_BLOCK:]

    return pl.pallas_call(
        body, name=name, grid=(N_HEADS // heads, SEQ // ATTN_BLOCK),
        in_specs=[pl.BlockSpec(memory_space=pltpu.SMEM), cur, prev, cur, prev, cur, cur, cur, cur],
        out_specs=[cur] * 5,
        out_shape=[jax.ShapeDtypeStruct((SEQ, D_MODEL), F32)] * 5,
        compiler_params=_params("parallel", "parallel"),
    )(slopes, q, k, k, v, v, do, lse, delta)


def _qknorm_bwd(proj, qw, kw, seg, dq, dkc, dkp, dvc, dvp, name):
    tm = ATTN_BLOCK
    n_blocks = SEQ // tm
    nxt = pl.BlockSpec((tm, D_MODEL), lambda i: (jnp.minimum(i + 1, n_blocks - 1), 0))

    def body(p_ref, qw_ref, kw_ref, seg_ref, dq_ref, dkc_ref, dkp_ref, dvc_ref, dvp_ref, dproj_ref, sums_ref):
        i = pl.program_id(0)
        segv = seg_ref[...]
        has_next = i < n_blocks - 1
        dk = dkc_ref[...] + jnp.where(has_next, dkp_ref[...], 0.0)
        dv = dvc_ref[...] + jnp.where(has_next, dvp_ref[...], 0.0)
        sums = []
        for part, (raw, w, dn) in enumerate(((p_ref[:, :D_MODEL], qw_ref[...], dq_ref[...]),
                                             (p_ref[:, D_MODEL:2 * D_MODEL], kw_ref[...], dk))):
            r = _qk_rstd(raw, segv)
            gq = dn * w
            draw = r * gq - raw * (r * r * r) * (_segsum(raw * gq, segv) * (1.0 / HEAD_DIM))
            dproj_ref[:, part * D_MODEL:(part + 1) * D_MODEL] = draw.astype(BF16)
            sums.append(jnp.sum(dn * raw * r, axis=0, keepdims=True))
        dproj_ref[:, 2 * D_MODEL:] = dv.astype(BF16)

        @pl.when(i == 0)
        def _():
            sums_ref[...] = jnp.zeros_like(sums_ref)

        sums_ref[...] += jnp.concatenate(sums + [jnp.zeros((6, D_MODEL), F32)], axis=0)

    return pl.pallas_call(
        body, name=name, grid=(n_blocks,),
        in_specs=[_row_spec(tm, 3 * D_MODEL), _vec_spec(1, D_MODEL), _vec_spec(1, D_MODEL), _vec_spec(256, 256),
                  _row_spec(tm, D_MODEL), _row_spec(tm, D_MODEL), nxt, _row_spec(tm, D_MODEL), nxt],
        out_specs=[_row_spec(tm, 3 * D_MODEL), _vec_spec(8, D_MODEL)],
        out_shape=[jax.ShapeDtypeStruct((SEQ, 3 * D_MODEL), BF16), jax.ShapeDtypeStruct((8, D_MODEL), F32)],
        compiler_params=_params("arbitrary"),
    )(proj, qw, kw, seg, dq, dkc, dkp, dvc, dvp)


def _to_classes(a, dilation):
    if dilation == 1:
        return a
    s, c = a.shape
    return a.reshape(s // dilation, dilation, c).transpose(1, 0, 2).reshape(s, c)


def _from_classes(a, dilation):
    if dilation == 1:
        return a
    s, c = a.shape
    return a.reshape(dilation, s // dilation, c).transpose(1, 0, 2).reshape(s, c)


def _cols_to_classes(a, dilation):
    if dilation == 1:
        return a
    r, s = a.shape
    return a.reshape(r, s // dilation, dilation).transpose(0, 2, 1).reshape(r, s)


B_TN = 512
B_GROUP_TILES = 3 * D_MODEL // B_TN
B_Z_TILE0 = 3 * B_GROUP_TILES
B_Z_TILES = D_MODEL // B_TN


def _local_step(x, target, mods, norm_g, conv_w, conv_b, ln_g, ln_b, q_norm, k_norm,
                weights_a, weights_b, forward_weights_b, send_grads_b, forward_grads_b, send_grads_a):
    row = lambda a, i: a[i:i + 1]
    shift0, scale0, gate0 = row(mods[0], 0), row(mods[0], 1), row(mods[0], 2)
    shift1, scale1, gate1 = row(mods[1], 0), row(mods[1], 1), row(mods[1], 2)
    g0, g1 = row(norm_g, 0), row(norm_g, 1)
    seg = _seg_matrix()
    slopes = jnp.exp2(-8.0 * jnp.arange(1, N_HEADS + 1, dtype=F32) / N_HEADS)
    qw = [jnp.tile(q_norm[g:g + 1], (1, N_HEADS)) for g in range(3)]
    kw = [jnp.tile(k_norm[g:g + 1], (1, N_HEADS)) for g in range(3)]

    h0, h0t = _normmod_fwd(x, g0, scale0, shift0, "prenorm0")
    wa_in, wa_out = weights_a(h0)
    ja, _, nsa = wa_in.shape
    proj_a = _mm(h0, wa_in, tn=nsa, tile0=0, n_tiles=ja, out_dtype=F32, name="a_in")
    u5, u5t, u2 = _conv_fwd(proj_a, conv_w, conv_b, ln_g, ln_b, "a_conv")
    token = forward_weights_b(u5)
    x1, y_a, h1, h1t = _out_a(u5, wa_out, x, gate0 + token[0:1, 0:1], g1, scale1, shift1, "a_out")

    wb_in, wb_out = weights_b(x1)
    jb, _, nsb = wb_in.shape
    h1c =[_to_classes(h1, d) for d in DILATIONS]
    h1tc = [_cols_to_classes(h1t, d) for d in DILATIONS]
    z_b = _mm(h1, wb_in, tn=B_TN, tile0=B_Z_TILE0, n_tiles=B_Z_TILES, out_dtype=F32, name="b_in_z")
    proj_g, qkv, o_parts, lse_parts = [], [], [], []
    for g, d in enumerate(DILATIONS):
        pg = _mm(h1c[g], wb_in, tn=B_TN, tile0=g * B_GROUP_TILES, n_tiles=B_GROUP_TILES, out_dtype=F32,
                 name=f"b_in_g{g}")
        qn, kn, vn = _qknorm_fwd(pg, qw[g], kw[g], seg, f"b_qknorm_g{g}")
        og, lg = _attn_fwd(qn, kn, vn, slopes, d, f"b_attn_g{g}")
        proj_g.append(pg)
        qkv.append((qn, kn, vn))
        o_parts.append(_from_classes(og, d))
        lse_parts.append(_from_classes(lg, d))
    u_b, u_bt, o_b, lse_b = _merge_fwd(o_parts, lse_parts, z_b, "b_merge")
    e, dy_b, sums_loss = _out_b_loss(u_b, wb_out, x1, gate1, target, "b_out_loss")

    dwb_out = _mm(u_bt, dy_b, tn=D_MODEL, tile0=0, n_tiles=1, out_dtype=BF16, name="b_dwout")
    du_b = _mm_nt_res(dy_b, wb_out, "b_dout")
    do_b, dz_b, delta_b = _merge_bwd(du_b, o_b, z_b, seg, "b_merge_bwd")
    dwb_in = _mm(h1t, dz_b, tn=B_TN, tile0=B_Z_TILE0, n_tiles=B_Z_TILES, out_dtype=BF16, name="b_dwin_z",
                 out3d=(jb, nsb))
    dh1_parts = [_mm_nt(dz_b, wb_in, tn=B_TN, tile0=B_Z_TILE0, n_tiles=B_Z_TILES, name="b_dh_z")]
    qk_sums = []
    for g, d in enumerate(DILATIONS):
        qn, kn, vn = qkv[g]
        dq, dkc, dkp, dvc, dvp = _attn_bwd(qn, kn, vn, _to_classes(do_b, d), _to_classes(lse_b, d),
                                           _to_classes(delta_b, d), slopes, d, f"b_attn_bwd_g{g}")
        dproj, sums_qk = _qknorm_bwd(proj_g[g], qw[g], kw[g], seg, dq, dkc, dkp, dvc, dvp, f"b_qknorm_bwd_g{g}")
        qk_sums.append(sums_qk)
        dwb_in = _mm(h1tc[g], dproj, tn=B_TN, tile0=g * B_GROUP_TILES, n_tiles=B_GROUP_TILES, out_dtype=BF16,
                     name=f"b_dwin_g{g}", out3d=(jb, nsb), prev=dwb_in)
        dh = _mm_nt(dproj, wb_in, tn=B_TN, tile0=g * B_GROUP_TILES, n_tiles=B_GROUP_TILES, name=f"b_dh_g{g}")
        dh1_parts.append(_from_classes(dh, d))
    token = send_grads_b(dwb_in, dwb_out)
    dx1, sums_n1 = _normmod_bwd(x1, g1, scale1 + token[0:1, 0:1], dh1_parts, e, "prenorm1_bwd")
    token = forward_grads_b(dx1)

    dy_a, sums_ga = _dgate_dy(dx1, y_a, gate0 + token[0:1, 0:1], "a_dgate")
    dwa_out = _mm(u5t, dy_a, tn=D_MODEL, tile0=0, n_tiles=1, out_dtype=BF16, name="a_dwout")
    du5 = _mm_nt_res(dy_a, wa_out, "a_dout")
    du2, dz_a, sums_ln = _conv_bwd_pointwise(du5, proj_a, u2, ln_g, ln_b, "a_conv_bwd_pw")
    dproj_a, dconv_w = _conv_bwd_taps(du2, dz_a, proj_a, conv_w, "a_conv_bwd_taps")
    dwa_in = _mm(h0t, dproj_a, tn=nsa, tile0=0, n_tiles=ja, out_dtype=BF16, name="a_dwin", out3d=(ja, nsa))
    token = send_grads_a(dwa_in, dwa_out)
    dh0 = _mm_nt(dproj_a, wa_in, tn=nsa, tile0=0, n_tiles=ja, name="a_dh", after=token)
    grad_x, sums_n0 = _normmod_bwd(x, g0, scale0, [dh0], dx1, "prenorm0_bwd")

    small = dict(
        dnorm_g=jnp.concatenate([sums_n0[0:1], sums_n1[0:1]], axis=0),
        dmod0=jnp.concatenate([sums_n0[2:3], sums_n0[1:2], sums_ga[0:1]], axis=0),
        dmod1=jnp.concatenate([sums_n1[2:3], sums_n1[1:2], sums_loss[0:1]], axis=0),
        dln_g=sums_ln[0:1], dln_b=sums_ln[1:2], dconv_b=sums_ln[2:3],
        dconv_w=dconv_w[:CONV_WIDTH],
        dq_norm=jnp.concatenate([s[0:1] for s in qk_sums], axis=0),
        dk_norm=jnp.concatenate([s[1:2] for s in qk_sums], axis=0),
        loss_cols=sums_loss[1:2],
    )
    return grad_x, small


def _adamw(w, g, m, v, name):
    rows, cols = w.shape
    tr = rows if rows <= 128 else 128
    c1 = 1.0 / (1.0 - ADAM_B1 ** ADAM_STEP)
    c2 = 1.0 / (1.0 - ADAM_B2 ** ADAM_STEP)

    def body(w_ref, g_ref, m_ref, v_ref, d_ref, mo_ref, vo_ref):
        gv = g_ref[...]
        mn = ADAM_B1 * m_ref[...] + (1.0 - ADAM_B1) * gv
        vn = ADAM_B2 * v_ref[...] + (1.0 - ADAM_B2) * (gv * gv)
        mo_ref[...] = mn
        vo_ref[...] = vn
        d_ref[...] = -ADAM_LR * ((mn * c1) / (jnp.sqrt(vn * c2) + ADAM_EPS) + ADAM_WD * w_ref[...])

    spec = pl.BlockSpec((tr, cols), lambda i: (i, 0))
    return pl.pallas_call(
        body, name=name, grid=(rows // tr,), in_specs=[spec] * 4, out_specs=[spec] * 3,
        out_shape=[jax.ShapeDtypeStruct((rows, cols), F32)] * 3,
        compiler_params=_params("parallel"),
    )(w, g, m, v)


def _cast_into_slot(w, chip_idx, name):
    rows, cols = w.shape
    tr = 256

    def body(ch_ref, w_ref, o_ref):
        o_ref[...] = w_ref[...].astype(BF16)

    return pl.pallas_call(
        body, name=name,
        grid_spec=pltpu.PrefetchScalarGridSpec(
            num_scalar_prefetch=1, grid=(rows // tr,),
            in_specs=[pl.BlockSpec((tr, cols), lambda i, ch: (i, 0))],
            out_specs=pl.BlockSpec((None, tr, cols), lambda i, ch: (ch[0], i, 0))),
        out_shape=jax.ShapeDtypeStruct((N_CHIPS, rows, cols), BF16), compiler_params=_params("parallel"),
    )(chip_idx, w)


def _position():
    x, y, c = lax.axis_index("x"), lax.axis_index("y"), lax.axis_index("c")
    return x, y, c


def _xor_peer(x, y, c, k):
    return (x ^ ((k >> 2) & 1), y ^ ((k >> 1) & 1), c ^ (k & 1))


def _chip_peer(x, y, k):
    return (x ^ ((k >> 1) & 1), y ^ (k & 1))


def _ada_forward(c_row, ada_w, ada_b, conv_w):
    ns = ada_w.shape[2]
    cw = conv_w.shape[1]

    def body(c_ref, w_ref, b_ref, cv_ref, mod_ref, sc_ref, cvo_ref,
             c_all, mp, parts, cv_parts, send1, recv1, send2, recv2, send3, recv3):
        x, y, c = _position()
        me = 4 * x + 2 * y + c
        chip = 2 * x + y

        def c_copy(k):
            return pltpu.make_async_remote_copy(
                src_ref=c_all.at[me], dst_ref=c_all.at[me], send_sem=send1.at[k - 1], recv_sem=recv1.at[k - 1],
                device_id=_xor_peer(x, y, c, k), device_id_type=MESH)

        def cv_copy(k):
            px, py = _chip_peer(x, y, k)
            return pltpu.make_async_remote_copy(
                src_ref=cv_parts.at[chip], dst_ref=cv_parts.at[chip], send_sem=send3.at[k - 1],
                recv_sem=recv3.at[k - 1], device_id=(px, py, c), device_id_type=MESH)

        c_all[me] = c_ref[...]
        cv_parts[chip] = cv_ref[...]
        for k in range(1, N_DEV):
            c_copy(k).start()
        for k in range(1, N_CHIPS):
            cv_copy(k).start()
        for k in range(1, N_DEV):
            c_copy(k).wait_recv()
        cv = jnp.concatenate([c_all[i] for i in range(N_DEV)], axis=0)
        sc = cv * _sigmoid(cv)
        sc_ref[...] = sc
        for l in range(2):
            res = jnp.dot(sc, w_ref[l], preferred_element_type=F32, precision=lax.Precision.HIGHEST)
            for i in range(N_DEV):
                mp[i, l:l + 1, :] = res[i:i + 1, :]

        def mod_copy(k):
            px, py = _chip_peer(x, y, k)
            return pltpu.make_async_remote_copy(
                src_ref=mp.at[4 * px + 2 * py + c], dst_ref=parts.at[chip], send_sem=send2.at[k - 1],
                recv_sem=recv2.at[k - 1], device_id=(px, py, c), device_id_type=MESH)

        for k in range(1, N_CHIPS):
            mod_copy(k).start()
        parts[chip] = mp[me]
        for k in range(1, N_CHIPS):
            mod_copy(k).wait_recv()
            cv_copy(k).wait_recv()
        mod_ref[...] = jnp.concatenate([parts[j] for j in range(N_CHIPS)], axis=1) + b_ref[...]
        cvo_ref[...] = jnp.concatenate([cv_parts[j] for j in range(N_CHIPS)], axis=1)
        for k in range(1, N_DEV):
            c_copy(k).wait_send()
        for k in range(1, N_CHIPS):
            mod_copy(k).wait_send()
            cv_copy(k).wait_send()

    vm = pl.BlockSpec(memory_space=pltpu.VMEM)
    return pl.pallas_call(
        body, name="ada_forward",
        in_specs=[vm] * 4, out_specs=[vm] * 3,
        out_shape=[jax.ShapeDtypeStruct((2, 3 * D_MODEL), F32), jax.ShapeDtypeStruct((N_DEV, D_MODEL), F32),
                   jax.ShapeDtypeStruct((CONV_WIDTH, N_CHIPS * cw), F32)],
        scratch_shapes=[pltpu.VMEM((N_DEV, 1, D_MODEL), F32), pltpu.VMEM((N_DEV, 2, ns), F32),
                        pltpu.VMEM((N_CHIPS, 2, ns), F32), pltpu.VMEM((N_CHIPS, CONV_WIDTH, cw), F32),
                        pltpu.SemaphoreType.DMA((N_DEV - 1,)), pltpu.SemaphoreType.DMA((N_DEV - 1,)),
                        pltpu.SemaphoreType.DMA((N_CHIPS - 1,)), pltpu.SemaphoreType.DMA((N_CHIPS - 1,)),
                        pltpu.SemaphoreType.DMA((N_CHIPS - 1,)), pltpu.SemaphoreType.DMA((N_CHIPS - 1,))],
        compiler_params=pltpu.CompilerParams(vmem_limit_bytes=VMEM_LIMIT_BYTES),
    )(c_row, ada_w, ada_b, conv_w)


HBM_SPEC = pl.BlockSpec(memory_space=pltpu.HBM)
ANY_SPEC = pl.BlockSpec(memory_space=pl.ANY)
SEM_SPEC = pl.BlockSpec(memory_space=pltpu.SEMAPHORE)
SPLIT_PARAMS = dict(compiler_params=pltpu.CompilerParams(has_side_effects=pltpu.SideEffectType.DATAFLOW_SIDE_EFFECTING))
TOKEN = jax.ShapeDtypeStruct((8, 128), F32)


def _hbm(arrays):
    return [pltpu.with_memory_space_constraint(a, pltpu.HBM) for a in arrays]


def _hbm_like(arrays):
    return [pltpu.HBM(a.shape, a.dtype) for a in arrays]


def _gather_start(lands, after, name):
    n = len(lands)

    def body(*refs):
        ins = refs[:n]
        send, recv = refs[n + 1], refs[n + 2]
        x, y, c = _position()
        chip = 2 * x + y
        for t in range(n):
            rh = ins[t].shape[1] // 2
            for k in range(1, N_CHIPS):
                px, py = _chip_peer(x, y, k)
                block = ins[t].at[chip, pl.ds(c * rh, rh)]
                pltpu.make_async_remote_copy(
                    src_ref=block, dst_ref=block, send_sem=send.at[3 * t + k - 1], recv_sem=recv.at[3 * t + k - 1],
                    device_id=(px, py, c), device_id_type=MESH).start()
        refs[-1][...] = jnp.zeros(TOKEN.shape, F32)

    res = pl.pallas_call(
        body, name=name, in_specs=[HBM_SPEC] * n + [ANY_SPEC],
        out_specs=(SEM_SPEC, SEM_SPEC, *[HBM_SPEC] * n, pl.BlockSpec(memory_space=pltpu.VMEM)),
        out_shape=(pltpu.SemaphoreType.DMA((3 * n,)), pltpu.SemaphoreType.DMA((3 * n,)), *_hbm_like(lands), TOKEN),
        input_output_aliases={t: 2 + t for t in range(n)}, **SPLIT_PARAMS,
    )(*_hbm(lands), after)
    return res[0], res[1], list(res[2:2 + n]), res[-1]


def _gather_forward(send, recv, lands, after, name):
    n = len(lands)

    def body(*refs):
        ins = refs[:n]
        send1, recv1 = refs[n], refs[n + 1]
        send2, recv2 = refs[n + 3], refs[n + 4]
        x, y, c = _position()
        chip = 2 * x + y
        for t in range(n):
            rh = ins[t].shape[1] // 2
            half = pl.ds(c * rh, rh)
            for k in range(1, N_CHIPS):
                px, py = _chip_peer(x, y, k)
                s = 3 * t + k - 1
                got = ins[t].at[2 * px + py, half]
                cp = pltpu.make_async_remote_copy(
                    src_ref=ins[t].at[chip, half], dst_ref=got, send_sem=send1.at[s], recv_sem=recv1.at[s],
                    device_id=(px, py, c), device_id_type=MESH)
                cp.wait_send()
                cp.wait_recv()
                pltpu.make_async_remote_copy(
                    src_ref=got, dst_ref=got, send_sem=send2.at[s], recv_sem=recv2.at[s],
                    device_id=(x, y, 1 - c), device_id_type=MESH).start()
        refs[-1][...] = jnp.zeros(TOKEN.shape, F32)

    res = pl.pallas_call(
        body, name=name, in_specs=[HBM_SPEC] * n + [SEM_SPEC, SEM_SPEC, ANY_SPEC],
        out_specs=(SEM_SPEC, SEM_SPEC, *[HBM_SPEC] * n, pl.BlockSpec(memory_space=pltpu.VMEM)),
        out_shape=(pltpu.SemaphoreType.DMA((3 * n,)), pltpu.SemaphoreType.DMA((3 * n,)), *_hbm_like(lands), TOKEN),
        input_output_aliases={t: 2 + t for t in range(n)}, **SPLIT_PARAMS,
    )(*lands, send, recv, after)
    return res[0], res[1], list(res[2:2 + n]), res[-1]


def _gather_wait(send, recv, lands, after, name):
    n = len(lands)

    def body(*refs):
        ins = refs[:n]
        send_ref, recv_ref = refs[n], refs[n + 1]
        x, y, c = _position()
        for t in range(n):
            rh = ins[t].shape[1] // 2
            for k in range(1, N_CHIPS):
                px, py = _chip_peer(x, y, k)
                cp = pltpu.make_async_remote_copy(
                    src_ref=ins[t].at[2 * px + py, pl.ds(c * rh, rh)],
                    dst_ref=ins[t].at[2 * px + py, pl.ds((1 - c) * rh, rh)], send_sem=send_ref.at[3 * t + k - 1],
                    recv_sem=recv_ref.at[3 * t + k - 1], device_id=(x, y, 1 - c), device_id_type=MESH)
                cp.wait_send()
                cp.wait_recv()

    res = pl.pallas_call(
        body, name=name, in_specs=[HBM_SPEC] * n + [SEM_SPEC, SEM_SPEC, ANY_SPEC], out_specs=[HBM_SPEC] * n,
        out_shape=_hbm_like(lands), input_output_aliases={t: t for t in range(n)}, **SPLIT_PARAMS,
    )(*lands, send, recv, after)
    return list(res)


def _reduce_start(grads, after, name):
    n = len(grads)
    lands = [lax.empty((N_DEV, g.shape[1] // 2, g.shape[2]), BF16) for g in grads]

    def body(*refs):
        gs, ls = refs[:n], refs[n:2 * n]
        send, recv = refs[2 * n + 1], refs[2 * n + 2]
        x, y, c = _position()
        me = 4 * x + 2 * y + c
        for t in range(n):
            rh = gs[t].shape[1] // 2
            for k in range(1, N_DEV):
                px, py, pc = _xor_peer(x, y, c, k)
                pltpu.make_async_remote_copy(
                    src_ref=gs[t].at[2 * px + py, pl.ds(pc * rh, rh)], dst_ref=ls[t].at[me],
                    send_sem=send.at[7 * t + k - 1], recv_sem=recv.at[7 * t + k - 1],
                    device_id=(px, py, pc), device_id_type=MESH).start()
        refs[-1][...] = jnp.zeros(TOKEN.shape, F32)

    res = pl.pallas_call(
        body, name=name, in_specs=[HBM_SPEC] * (2 * n) + [ANY_SPEC],
        out_specs=(SEM_SPEC, SEM_SPEC, *[HBM_SPEC] * (2 * n), pl.BlockSpec(memory_space=pltpu.VMEM)),
        out_shape=(pltpu.SemaphoreType.DMA((7 * n,)), pltpu.SemaphoreType.DMA((7 * n,)),
                   *_hbm_like(grads), *_hbm_like(lands), TOKEN),
        input_output_aliases={t: 2 + t for t in range(2 * n)}, **SPLIT_PARAMS,
    )(*_hbm(grads), *_hbm(lands), after)
    return res[0], res[1], list(res[2:2 + n]), list(res[2 + n:2 + 2 * n]), res[-1]


def _reduce_wait(send, recv, grads, lands, after, name):
    n = len(grads)

    def body(*refs):
        gs, ls = refs[:n], refs[n:2 * n]
        send_ref, recv_ref = refs[2 * n], refs[2 * n + 1]
        x, y, c = _position()
        for t in range(n):
            rh = gs[t].shape[1] // 2
            for k in range(1, N_DEV):
                px, py, pc = _xor_peer(x, y, c, k)
                cp = pltpu.make_async_remote_copy(
                    src_ref=gs[t].at[2 * px + py, pl.ds(pc * rh, rh)], dst_ref=ls[t].at[4 * px + 2 * py + pc],
                    send_sem=send_ref.at[7 * t + k - 1], recv_sem=recv_ref.at[7 * t + k - 1],
                    device_id=(px, py, pc), device_id_type=MESH)
                cp.wait_send()
                cp.wait_recv()

    res = pl.pallas_call(
        body, name=name, in_specs=[HBM_SPEC] * (2 * n) + [SEM_SPEC, SEM_SPEC, ANY_SPEC], out_specs=[HBM_SPEC] * (2 * n),
        out_shape=_hbm_like(grads) + _hbm_like(lands), input_output_aliases={t: t for t in range(2 * n)}, **SPLIT_PARAMS,
    )(*grads, *lands, send, recv, after)
    return list(res[:n]), list(res[n:])


def _sum_devices(land, grad, dev_idx, name):
    _, rh, cols = land.shape
    tr = 128
    nb = rh // tr

    def body(idx_ref, l_ref, g_ref, o_ref):
        me = idx_ref[0]
        acc = jnp.where(me == 0, g_ref[...], l_ref[0]).astype(F32)
        for d in range(1, N_DEV):
            acc = acc + jnp.where(me == d, g_ref[...], l_ref[d]).astype(F32)
        o_ref[...] = acc

    return pl.pallas_call(
        body, name=name,
        grid_spec=pltpu.PrefetchScalarGridSpec(
            num_scalar_prefetch=1, grid=(nb,),
            in_specs=[pl.BlockSpec((N_DEV, tr, cols), lambda i, idx: (0, i, 0)),
                      pl.BlockSpec((None, tr, cols), lambda i, idx: (idx[1], idx[2] * nb + i, 0))],
            out_specs=pl.BlockSpec((tr, cols), lambda i, idx: (idx[2] * nb + i, 0))),
        out_shape=jax.ShapeDtypeStruct((2 * rh, cols), F32), compiler_params=_params("parallel"),
    )(dev_idx, land, grad)


def _split_start(name, arrays, n_sems, after, issue):
    m = len(arrays)

    def body(*refs):
        issue(refs[:m], refs[m + 1], refs[m + 2])
        refs[-1][...] = jnp.zeros(TOKEN.shape, F32)

    res = pl.pallas_call(
        body, name=name, in_specs=[HBM_SPEC] * m + [ANY_SPEC],
        out_specs=(SEM_SPEC, SEM_SPEC, *[HBM_SPEC] * m, pl.BlockSpec(memory_space=pltpu.VMEM)),
        out_shape=(pltpu.SemaphoreType.DMA((n_sems,)), pltpu.SemaphoreType.DMA((n_sems,)), *_hbm_like(arrays), TOKEN),
        input_output_aliases={t: 2 + t for t in range(m)}, **SPLIT_PARAMS,
    )(*_hbm(arrays), after)
    return res[0], res[1], list(res[2:2 + m]), res[-1]


def _split_wait(name, arrays, send, recv, after, await_all):
    m = len(arrays)

    def body(*refs):
        await_all(refs[:m], refs[m], refs[m + 1])

    res = pl.pallas_call(
        body, name=name, in_specs=[HBM_SPEC] * m + [SEM_SPEC, SEM_SPEC, ANY_SPEC], out_specs=[HBM_SPEC] * m,
        out_shape=_hbm_like(arrays), input_output_aliases={t: t for t in range(m)}, **SPLIT_PARAMS,
    )(*arrays, send, recv, after)
    return list(res)


def _sibling_copies(refs, send, recv, n):
    x, y, c = _position()
    cps = []
    for t in range(n):
        rh = refs[t].shape[1] // 2
        cps.append(pltpu.make_async_remote_copy(
            src_ref=refs[t].at[pl.ds(0, N_CHIPS), pl.ds((1 - c) * rh, rh)], dst_ref=refs[n + t],
            send_sem=send.at[t], recv_sem=recv.at[t], device_id=(x, y, 1 - c), device_id_type=MESH))
    return cps


def _reduce_sibling_start(grads, after, name):
    n = len(grads)
    lands = [lax.empty((N_CHIPS, g.shape[1] // 2, g.shape[2]), BF16) for g in grads]

    def issue(refs, send, recv):
        for cp in _sibling_copies(refs, send, recv, n):
            cp.start()

    return _split_start(name, list(grads) + lands, n, after, issue)


def _reduce_sibling_wait(send, recv, arrays, after, name):
    n = len(arrays) // 2

    def await_all(refs, send_ref, recv_ref):
        for cp in _sibling_copies(refs, send_ref, recv_ref, n):
            cp.wait_send()
            cp.wait_recv()

    res = _split_wait(name, arrays, send, recv, after, await_all)
    return res[:n], res[n:]


def _add_sibling_half(grad, got, dev_idx, name):
    j, r, cols = grad.shape
    rh = r // 2
    tr = 128
    nb = rh // tr

    def body(idx_ref, g_ref, got_ref, out_ref):
        out_ref[...] = (g_ref[...].astype(F32) + got_ref[...].astype(F32)).astype(BF16)

    return pl.pallas_call(
        body, name=name,
        grid_spec=pltpu.PrefetchScalarGridSpec(
            num_scalar_prefetch=1, grid=(j, nb),
            in_specs=[pl.BlockSpec((None, tr, cols), lambda jj, i, idx: (jj, idx[2] * nb + i, 0)),
                      pl.BlockSpec((None, tr, cols), lambda jj, i, idx: (jj, i, 0))],
            out_specs=pl.BlockSpec((None, tr, cols), lambda jj, i, idx: (jj, i, 0))),
        out_shape=jax.ShapeDtypeStruct((j, rh, cols), BF16),
        compiler_params=_params("parallel", "parallel"),
    )(dev_idx, grad, got)


def _chip_copies(refs, send, recv, n, receiving):
    x, y, c = _position()
    chip = 2 * x + y
    cps = []
    for t in range(n):
        for k in range(1, N_CHIPS):
            px, py = _chip_peer(x, y, k)
            cps.append(pltpu.make_async_remote_copy(
                src_ref=refs[t].at[2 * px + py], dst_ref=refs[n + t].at[2 * px + py if receiving else chip],
                send_sem=send.at[3 * t + k - 1], recv_sem=recv.at[3 * t + k - 1],
                device_id=(px, py, c), device_id_type=MESH))
    return cps


def _reduce_chips_start(partials, after, name):
    n = len(partials)
    lands = [lax.empty(p.shape, BF16) for p in partials]

    def issue(refs, send, recv):
        for cp in _chip_copies(refs, send, recv, n, False):
            cp.start()

    return _split_start(name, list(partials) + lands, 3 * n, after, issue)


def _reduce_chips_wait(send, recv, arrays, after, name):
    n = len(arrays) // 2

    def await_all(refs, send_ref, recv_ref):
        for cp in _chip_copies(refs, send_ref, recv_ref, n, True):
            cp.wait_send()
            cp.wait_recv()

    res = _split_wait(name, arrays, send, recv, after, await_all)
    return res[:n], res[n:]


def _sum_partials(land, partial, dev_idx, name):
    _, rh, cols = land.shape
    tr = 128
    nb = rh // tr

    def body(idx_ref, l_ref, p_ref, o_ref):
        chip = idx_ref[1]
        acc = jnp.where(chip == 0, p_ref[...], l_ref[0]).astype(F32)
        for s in range(1, N_CHIPS):
            acc = acc + jnp.where(chip == s, p_ref[...], l_ref[s]).astype(F32)
        o_ref[...] = acc

    return pl.pallas_call(
        body, name=name,
        grid_spec=pltpu.PrefetchScalarGridSpec(
            num_scalar_prefetch=1, grid=(nb,),
            in_specs=[pl.BlockSpec((N_CHIPS, tr, cols), lambda i, idx: (0, i, 0)),
                      pl.BlockSpec((None, tr, cols), lambda i, idx: (idx[1], i, 0))],
            out_specs=pl.BlockSpec((tr, cols), lambda i, idx: (idx[2] * nb + i, 0))),
        out_shape=jax.ShapeDtypeStruct((2 * rh, cols), F32), compiler_params=_params("parallel"),
    )(dev_idx, land, partial)


def _share_halves(totals):
    n = len(totals)

    def body(*refs):
        ins, outs = refs[:n], refs[n:2 * n]
        send, recv = refs[2 * n:]
        x, y, c = _position()
        cps = []
        for t in range(n):
            rh = ins[t].shape[0] // 2
            mine = pl.ds(c * rh, rh)
            cp = pltpu.make_async_remote_copy(
                src_ref=ins[t].at[mine], dst_ref=outs[t].at[mine], send_sem=send.at[t], recv_sem=recv.at[t],
                device_id=(x, y, 1 - c), device_id_type=MESH)
            cp.start()
            cps.append(cp)
        for cp in cps:
            cp.wait()

    return pl.pallas_call(
        body, name="reduce_share_" + "_".join(str(t.shape[1]) for t in totals), in_specs=[ANY_SPEC] * n,
        out_specs=[ANY_SPEC] * n, out_shape=[jax.ShapeDtypeStruct(t.shape, F32) for t in totals],
        input_output_aliases={t: t for t in range(n)},
        scratch_shapes=[pltpu.SemaphoreType.DMA((n,)), pltpu.SemaphoreType.DMA((n,))],
    )(*totals)


def _exchange_halves(grads):
    n = len(grads)
    hbm = pl.BlockSpec(memory_space=pl.ANY)

    def body(*refs):
        ins, outs = refs[:n], refs[n:2 * n]
        send, recv = refs[2 * n:]
        x, y, c = _position()
        cps = []
        for t in range(n):
            rh = ins[t].shape[1] // 2
            cp = pltpu.make_async_remote_copy(
                src_ref=ins[t].at[pl.ds(0, N_CHIPS), pl.ds((1 - c) * rh, rh)], dst_ref=outs[t], send_sem=send.at[t],
                recv_sem=recv.at[t], device_id=(x, y, 1 - c), device_id_type=MESH)
            cp.start()
            cps.append(cp)
        for cp in cps:
            cp.wait()

    return pl.pallas_call(
        body, name="reduce_exchange_halves", in_specs=[hbm] * n, out_specs=[hbm] * n,
        out_shape=[jax.ShapeDtypeStruct((g.shape[0], g.shape[1] // 2, g.shape[2]), BF16) for g in grads],
        scratch_shapes=[pltpu.SemaphoreType.DMA((n,)), pltpu.SemaphoreType.DMA((n,))],
    )(*grads)


def _add_halves(grad, got, c_idx, name):
    j, r, cols = grad.shape
    rh = r // 2
    tr = 128
    nb = rh // tr

    def body(c_ref, g_ref, o_ref_in, out_ref):
        out_ref[...] = (g_ref[...].astype(F32) + o_ref_in[...].astype(F32)).astype(BF16)

    return pl.pallas_call(
        body, name=name,
        grid_spec=pltpu.PrefetchScalarGridSpec(
            num_scalar_prefetch=1, grid=(j, nb),
            in_specs=[pl.BlockSpec((None, tr, cols), lambda jj, i, c_ref: (jj, c_ref[0] * nb + i, 0)),
                      pl.BlockSpec((None, tr, cols), lambda jj, i, c_ref: (jj, i, 0))],
            out_specs=pl.BlockSpec((None, tr, cols), lambda jj, i, c_ref: (jj, i, 0))),
        out_shape=jax.ShapeDtypeStruct((j, rh, cols), BF16),
        compiler_params=_params("parallel", "parallel"),
    )(c_idx, grad, got)


def _scatter_partials(partials):
    n = len(partials)
    hbm = pl.BlockSpec(memory_space=pl.ANY)

    def body(*refs):
        ins, outs = refs[:n], refs[n:2 * n]
        send, recv, local = refs[2 * n:]
        x, y, c = _position()
        chip = 2 * x + y
        cps, lcs = [], []
        for t in range(n):
            lc = pltpu.make_async_copy(ins[t].at[chip], outs[t].at[chip], local.at[t])
            lc.start()
            lcs.append(lc)
            for k in range(1, N_CHIPS):
                px, py = _chip_peer(x, y, k)
                s = 3 * t + k - 1
                cp = pltpu.make_async_remote_copy(
                    src_ref=ins[t].at[2 * px + py], dst_ref=outs[t].at[chip], send_sem=send.at[s],
                    recv_sem=recv.at[s], device_id=(px, py, c), device_id_type=MESH)
                cp.start()
                cps.append(cp)
        for cp in cps:
            cp.wait()
        for lc in lcs:
            lc.wait()

    return pl.pallas_call(
        body, name="reduce_scatter_partials", in_specs=[hbm] * n, out_specs=[hbm] * n,
        out_shape=[jax.ShapeDtypeStruct(p.shape, BF16) for p in partials],
        scratch_shapes=[pltpu.SemaphoreType.DMA((3 * n,)), pltpu.SemaphoreType.DMA((3 * n,)),
                        pltpu.SemaphoreType.DMA((n,))],
    )(*partials)


def _sum_chips(parts, name):
    j, rh, cols = parts.shape
    tr = 128

    def body(p_ref, o_ref):
        acc = p_ref[0].astype(F32)
        for s in range(1, j):
            acc = acc + p_ref[s].astype(F32)
        o_ref[...] = acc

    return pl.pallas_call(
        body, name=name, grid=(rh // tr,),
        in_specs=[pl.BlockSpec((j, tr, cols), lambda i: (0, i, 0))],
        out_specs=pl.BlockSpec((tr, cols), lambda i: (i, 0)),
        out_shape=jax.ShapeDtypeStruct((rh, cols), F32),
        compiler_params=_params("parallel"),
    )(parts)


def _share_totals(halves):
    n = len(halves)
    hbm = pl.BlockSpec(memory_space=pl.ANY)

    def body(*refs):
        ins, outs = refs[:n], refs[n:2 * n]
        send, recv, local = refs[2 * n:]
        x, y, c = _position()
        cps, lcs = [], []
        for t in range(n):
            rh = ins[t].shape[0]
            mine = outs[t].at[pl.ds(c * rh, rh)]
            lc = pltpu.make_async_copy(ins[t], mine, local.at[t])
            lc.start()
            lcs.append(lc)
            cp = pltpu.make_async_remote_copy(
                src_ref=ins[t], dst_ref=mine, send_sem=send.at[t], recv_sem=recv.at[t],
                device_id=(x, y, 1 - c), device_id_type=MESH)
            cp.start()
            cps.append(cp)
        for cp in cps:
            cp.wait()
        for lc in lcs:
            lc.wait()

    return pl.pallas_call(
        body, name="reduce_share_totals", in_specs=[hbm] * n, out_specs=[hbm] * n,
        out_shape=[jax.ShapeDtypeStruct((2 * h.shape[0], h.shape[1]), F32) for h in halves],
        scratch_shapes=[pltpu.SemaphoreType.DMA((n,)), pltpu.SemaphoreType.DMA((n,)),
                        pltpu.SemaphoreType.DMA((n,))],
    )(*halves)


SMALL_ROWS = 56


def _reduce_small(packed, silu_c):
    ns = 3 * D_MODEL // N_CHIPS

    def body(p_ref, sc_ref, tot_ref, gw_ref, loss_ref, qk_ref, allp, send, recv):
        x, y, c = _position()
        me = 4 * x + 2 * y + c
        chip = 2 * x + y

        def copy(k):
            return pltpu.make_async_remote_copy(
                src_ref=allp.at[me], dst_ref=allp.at[me], send_sem=send.at[k - 1], recv_sem=recv.at[k - 1],
                device_id=_xor_peer(x, y, c, k), device_id_type=MESH)

        allp[me] = p_ref[...]
        for k in range(1, N_DEV):
            copy(k).start()
        for k in range(1, N_DEV):
            copy(k).wait_recv()
        tot = allp[0]
        for i in range(1, N_DEV):
            tot = tot + allp[i]
        tot_ref[...] = tot
        loss_ref[...] = jnp.sum(tot[11:12, :], axis=1, keepdims=True) * (0.5 / D_MODEL)
        fold = tot[5:11, 0:HEAD_DIM]
        for h in range(1, N_HEADS):
            fold = fold + tot[5:11, h * HEAD_DIM:(h + 1) * HEAD_DIM]
        qk_ref[...] = jnp.concatenate([fold, jnp.zeros((2, HEAD_DIM), F32)], axis=0)
        sct = sc_ref[...].T
        rc = 64
        for l in range(2):
            dms = [allp[i, pl.ds(12 + 4 * l + chip, 1), :][:, :ns] for i in range(N_DEV)]
            for r0 in range(0, D_MODEL, rc):
                acc = sct[r0:r0 + rc, 0:1] * dms[0]
                for i in range(1, N_DEV):
                    acc = acc + sct[r0:r0 + rc, i:i + 1] * dms[i]
                gw_ref[l, r0:r0 + rc, :] = acc
        for k in range(1, N_DEV):
            copy(k).wait_send()

    vm = pl.BlockSpec(memory_space=pltpu.VMEM)
    return pl.pallas_call(
        body, name="reduce_small", in_specs=[vm, vm], out_specs=[vm] * 4,
        out_shape=[jax.ShapeDtypeStruct((SMALL_ROWS, D_MODEL), F32), jax.ShapeDtypeStruct((2, D_MODEL, ns), F32),
                   jax.ShapeDtypeStruct((1, 1), F32), jax.ShapeDtypeStruct((8, HEAD_DIM), F32)],
        scratch_shapes=[pltpu.VMEM((N_DEV, SMALL_ROWS, D_MODEL), F32),
                        pltpu.SemaphoreType.DMA((N_DEV - 1,)), pltpu.SemaphoreType.DMA((N_DEV - 1,))],
        compiler_params=pltpu.CompilerParams(vmem_limit_bytes=VMEM_LIMIT_BYTES),
    )(packed, silu_c)


def _reduce_big(grads, c_idx):
    names = list(grads)
    got = _exchange_halves([grads[k] for k in names])
    partials = [_add_halves(grads[k], got[i], c_idx, f"reduce_add_{k}") for i, k in enumerate(names)]
    parts = _scatter_partials(partials)
    halves = [_sum_chips(parts[i], f"reduce_sum_{k}") for i, k in enumerate(names)]
    totals = _share_totals(halves)
    return dict(zip(names, totals))


def kernel(x, c, norm_g, ada_w, ada_b, a_w_in, a_conv_w, a_conv_b, a_ln_g, a_ln_b, a_w_out, b_w_in, b_q_norm, b_k_norm, b_w_out, loss_target, m_norm_g, m_ada_w, m_ada_b, m_a_w_in, m_a_conv_w, m_a_conv_b, m_a_ln_g, m_a_ln_b, m_a_w_out, m_b_w_in, m_b_q_norm, m_b_k_norm, m_b_w_out, v_norm_g, v_ada_w, v_ada_b, v_a_w_in, v_a_conv_w, v_a_conv_b, v_a_ln_g, v_a_ln_b, v_a_w_out, v_b_w_in, v_b_q_norm, v_b_k_norm, v_b_w_out):
    chip = 2 * lax.axis_index("x") + lax.axis_index("y")
    core = lax.axis_index("c")
    chip_idx = chip.astype(jnp.int32).reshape(1)
    dev_idx = jnp.stack([2 * chip + core, chip, core]).astype(jnp.int32)

    mods, silu_c, conv_w_full = _ada_forward(c, ada_w, ada_b, a_conv_w[0])
    lands_a = [_cast_into_slot(a_w_in[0], chip_idx, "cast_a_w_in"), _cast_into_slot(a_w_out[0], chip_idx, "cast_a_w_out")]
    send_a, recv_a, lands_a, token_a = _gather_start(lands_a, mods, "gather_start_a")
    lands_b = [_cast_into_slot(b_w_in[0], chip_idx, "cast_b_w_in"), _cast_into_slot(b_w_out[0], chip_idx, "cast_b_w_out")]
    send_b, recv_b, lands_b, token_b = _gather_start(lands_b, token_a, "gather_start_b")
    mods = mods + token_b[0:2, 0:1]

    def weights_a(after):
        send, recv, lands, _ = _gather_forward(send_a, recv_a, lands_a, after, "gather_forward_a")
        w_in, w_out = _gather_wait(send, recv, lands, after, "gather_wait_a")
        return w_in, w_out.reshape(D_MODEL, D_MODEL)

    forwarded_b = []

    def weights_b(after):
        send, recv, lands, _ = forwarded_b
        w_in, w_out = _gather_wait(send, recv, lands, after, "gather_wait_b")
        return w_in, w_out.reshape(D_MODEL, D_MODEL)

    def forward_weights_b(after):
        forwarded_b.extend(_gather_forward(send_b, recv_b, lands_b, after, "gather_forward_b"))
        return forwarded_b[3]

    stage1, stage2 = {}, {}

    def send_grads(tag, dw_in, dw_out):
        grads = [dw_in, dw_out.reshape(N_CHIPS, D_MODEL // N_CHIPS, D_MODEL)]
        send, recv, arrays, token = _reduce_sibling_start(grads, dw_out, f"reduce_d2d_start_{tag}")
        stage1[tag] = (send, recv, arrays)
        return token

    def forward_grads(tag, after):
        send, recv, arrays = stage1[tag]
        grads, got = _reduce_sibling_wait(send, recv, arrays, after, f"reduce_d2d_wait_{tag}")
        partials = [_add_sibling_half(grads[i], got[i], dev_idx, f"reduce_add_{tag}_{i}") for i in range(2)]
        send, recv, arrays, token = _reduce_chips_start(partials, partials[1], f"reduce_ici_start_{tag}")
        stage2[tag] = (send, recv, arrays)
        return token

    def finish_grads(tag, after):
        send, recv, arrays = stage2[tag]
        partials, lands = _reduce_chips_wait(send, recv, arrays, after, f"reduce_ici_wait_{tag}")
        totals = [_sum_partials(lands[i], partials[i], dev_idx, f"reduce_sum_{tag}_{i}") for i in range(2)]
        return _share_halves(totals)

    grad_x, small = _local_step(
        x[0], loss_target[0], mods.reshape(2, 3, D_MODEL), norm_g, conv_w_full, a_conv_b, a_ln_g[0:1],
        a_ln_b[0:1], b_q_norm[0], b_k_norm[0], weights_a, weights_b, forward_weights_b,
        functools.partial(send_grads, "b"), functools.partial(forward_grads, "b"), functools.partial(send_grads, "a"))

    ns = 3 * D_MODEL // N_CHIPS
    pad_mod = lambda dm: jnp.pad(dm.reshape(N_CHIPS, ns), ((0, 0), (0, D_MODEL - ns)))
    packed = jnp.concatenate([
        small["dnorm_g"], small["dconv_b"], small["dln_g"], small["dln_b"], small["dq_norm"], small["dk_norm"],
        small["loss_cols"], pad_mod(small["dmod0"]), pad_mod(small["dmod1"]), small["dconv_w"],
        jnp.zeros((SMALL_ROWS - 20 - CONV_WIDTH, D_MODEL), F32)], axis=0)
    tot, g_ada_w, loss, qk = _reduce_small(packed, silu_c)
    cw = D_MODEL // N_CHIPS
    g_small = dict(
        norm_g=tot[0:2], a_conv_b=tot[2:3], a_ln_g=tot[3:4], a_ln_b=tot[4:5],
        b_q_norm=qk[0:3], b_k_norm=qk[3:6],
        ada_b=jnp.stack([tot[12:16, :ns].reshape(3 * D_MODEL), tot[16:20, :ns].reshape(3 * D_MODEL)]),
        a_conv_w=lax.dynamic_slice(tot[20:20 + CONV_WIDTH], (0, chip * cw), (CONV_WIDTH, cw)),
    )


    given = dict(norm_g=(norm_g, m_norm_g, v_norm_g), ada_w=(ada_w, m_ada_w, v_ada_w), ada_b=(ada_b, m_ada_b, v_ada_b),
                 a_w_in=(a_w_in, m_a_w_in, v_a_w_in), a_conv_w=(a_conv_w, m_a_conv_w, v_a_conv_w),
                 a_conv_b=(a_conv_b, m_a_conv_b, v_a_conv_b), a_ln_g=(a_ln_g, m_a_ln_g, v_a_ln_g),
                 a_ln_b=(a_ln_b, m_a_ln_b, v_a_ln_b), a_w_out=(a_w_out, m_a_w_out, v_a_w_out),
                 b_w_in=(b_w_in, m_b_w_in, v_b_w_in), b_q_norm=(b_q_norm, m_b_q_norm, v_b_q_norm),
                 b_k_norm=(b_k_norm, m_b_k_norm, v_b_k_norm), b_w_out=(b_w_out, m_b_w_out, v_b_w_out))
    order = ["norm_g", "ada_w", "ada_b", "a_w_in", "a_conv_w", "a_conv_b", "a_ln_g", "a_ln_b", "a_w_out", "b_w_in",
             "b_q_norm", "b_k_norm", "b_w_out"]
    outs = {}

    def update(k, g2):
        w, m, v = given[k]
        shape2 = g2.shape
        d2, m2, v2 = _adamw(w.reshape(shape2), g2, m.reshape(shape2), v.reshape(shape2), f"adamw_{k}")
        outs[k] = tuple(a.reshape(w.shape) for a in (g2, d2, m2, v2))

    token = forward_grads("a", tot)
    g_b_in, g_b_out = finish_grads("b", token)
    update("b_w_in", g_b_in)
    update("b_w_out", g_b_out)
    update("ada_w", g_ada_w.reshape(2 * D_MODEL, ns))
    for k, g2 in g_small.items():
        update(k, g2)
    g_a_in, g_a_out = finish_grads("a", outs["b_w_in"][1])
    update("a_w_in", g_a_in)
    update("a_w_out", g_a_out)
    return (loss.reshape(()), grad_x[None], *[outs[k][0] for k in order], *[outs[k][1] for k in order],
            *[outs[k][2] for k in order], *[outs[k][3] for k in order])
```

```python
import functools

import jax
import jax.numpy as jnp
from jax import lax
from jax.experimental import pallas as pl
from jax.experimental.pallas import tpu as pltpu

F32 = jnp.float32
BF16 = jnp.bfloat16

SEQ = 2048
D_MODEL = 1024
CONV_WIDTH = 31
HEAD_DIM = 64
N_HEADS = 16
DILATIONS = (1, 4, 16)
ATTN_BLOCK = 128
NORM_EPS = 1e-6
NEG_INF = -1e30
N_DEV = 8
N_CHIPS = 4

ADAM_LR = 0.001
ADAM_B1 = 0.9
ADAM_B2 = 0.999
ADAM_EPS = 1e-08
ADAM_WD = 0.01
ADAM_STEP = 10

VMEM_LIMIT_BYTES = 52 * 1024 * 1024
HALO = 32
MESH = pl.DeviceIdType.MESH


def _params(*sem):
    return pltpu.CompilerParams(dimension_semantics=sem or None, vmem_limit_bytes=VMEM_LIMIT_BYTES)


def _sigmoid(v):
    return 1.0 / (1.0 + jnp.exp(-v))


def _row_spec(tm, cols, col_block=0):
    return pl.BlockSpec((tm, cols), lambda i: (i, col_block))


def _vec_spec(rows, cols):
    return pl.BlockSpec((rows, cols), lambda i: (0, 0))


def _normmod(xv, g, scale, shift):
    r = lax.rsqrt(jnp.mean(xv * xv, axis=-1, keepdims=True) + NORM_EPS)
    return xv * r * g * (1.0 + scale) + shift


def _normmod_fwd(x, g, scale, shift, name):
    tm = 256

    def body(x_ref, g_ref, sc_ref, sh_ref, h_ref, ht_ref):
        h = _normmod(x_ref[...], g_ref[...], sc_ref[...], sh_ref[...])
        h_ref[...] = h.astype(BF16)
        ht_ref[...] = h.T.astype(BF16)

    return pl.pallas_call(
        body, name=name, grid=(SEQ // tm,),
        in_specs=[_row_spec(tm, D_MODEL)] + [_vec_spec(1, D_MODEL)] * 3,
        out_specs=[_row_spec(tm, D_MODEL), pl.BlockSpec((D_MODEL, tm), lambda i: (0, i))],
        out_shape=[jax.ShapeDtypeStruct((SEQ, D_MODEL), BF16), jax.ShapeDtypeStruct((D_MODEL, SEQ), BF16)],
        compiler_params=_params("parallel"),
    )(x, g, scale, shift)


def _normmod_bwd(x, g, scale, dh_parts, dres, name):
    tm = 256
    n_parts = len(dh_parts)

    def body(x_ref, g_ref, sc_ref, dres_ref, *rest):
        part_refs = rest[:n_parts]
        dx_ref, sums_ref = rest[n_parts:]
        xv = x_ref[...]
        r = lax.rsqrt(jnp.mean(xv * xv, axis=-1, keepdims=True) + NORM_EPS)
        xn = xv * r
        dh = part_refs[0][...]
        for p in part_refs[1:]:
            dh = dh + p[...]
        gv = g_ref[...]
        one_sc = 1.0 + sc_ref[...]
        dxn = dh * (gv * one_sc)
        dx = r * (dxn - xn * jnp.mean(dxn * xn, axis=-1, keepdims=True))
        dx_ref[...] = dres_ref[...] + dx
        dhx = dh * xn
        sums = jnp.concatenate([
            jnp.sum(dhx, axis=0, keepdims=True) * one_sc,
            jnp.sum(dhx, axis=0, keepdims=True) * gv,
            jnp.sum(dh, axis=0, keepdims=True),
            jnp.zeros((5, D_MODEL), F32)], axis=0)

        @pl.when(pl.program_id(0) == 0)
        def _():
            sums_ref[...] = jnp.zeros_like(sums_ref)

        sums_ref[...] += sums

    return pl.pallas_call(
        body, name=name, grid=(SEQ // tm,),
        in_specs=[_row_spec(tm, D_MODEL), _vec_spec(1, D_MODEL), _vec_spec(1, D_MODEL), _row_spec(tm, D_MODEL)]
        + [_row_spec(tm, D_MODEL)] * n_parts,
        out_specs=[_row_spec(tm, D_MODEL), _vec_spec(8, D_MODEL)],
        out_shape=[jax.ShapeDtypeStruct((SEQ, D_MODEL), F32), jax.ShapeDtypeStruct((8, D_MODEL), F32)],
        compiler_params=_params("arbitrary"),
    )(x, g, scale, dres, *dh_parts)


def _mm(lhs, rhs, *, tn, tile0, n_tiles, out_dtype, name, out3d=None, prev=None):
    mo, kc = lhs.shape
    cm = 512

    def body(l_ref, r_ref, *rest):
        o_ref = rest[-1]
        for m in range(mo // cm):
            rows = pl.ds(m * cm, cm)
            o_ref[rows, :] = jnp.dot(l_ref[rows, :], r_ref[...], preferred_element_type=F32).astype(out_dtype)

    if rhs.ndim == 3:
        tps_r = rhs.shape[2] // tn
        r_spec = pl.BlockSpec((None, kc, tn), lambda t: ((tile0 + t) // tps_r, 0, (tile0 + t) % tps_r))
    else:
        r_spec = pl.BlockSpec((kc, tn), lambda t: (0, t))
    in_specs = [pl.BlockSpec((mo, kc), lambda t: (0, 0)), r_spec]
    args = [lhs, rhs]
    aliases = {}
    if out3d is None:
        o_spec = pl.BlockSpec((mo, tn), lambda t: (0, t))
        o_shape = jax.ShapeDtypeStruct((mo, n_tiles * tn), out_dtype)
    else:
        j_out, ns_out = out3d
        tps_o = ns_out // tn
        o_spec = pl.BlockSpec((None, mo, tn), lambda t: ((tile0 + t) // tps_o, 0, (tile0 + t) % tps_o))
        o_shape = jax.ShapeDtypeStruct((j_out, mo, ns_out), out_dtype)
        if prev is not None:
            in_specs.append(pl.BlockSpec(memory_space=pl.ANY))
            args.append(prev)
            aliases = {2: 0}
    return pl.pallas_call(
        body, name=name, grid=(n_tiles,), in_specs=in_specs, out_specs=o_spec, out_shape=o_shape,
        input_output_aliases=aliases, compiler_params=_params("parallel"),
    )(*args)


def _mm_nt(dy, w3, *, tn, tile0, n_tiles, name, after=None):
    m_rows = dy.shape[0]
    _, kc, ns = w3.shape
    tps = ns // tn
    cm = 512
    extra = [] if after is None else [after]

    def body(dy_ref, w_ref, *rest):
        o_ref = rest[-1]

        @pl.when(pl.program_id(0) == 0)
        def _():
            o_ref[...] = jnp.zeros_like(o_ref)

        for m in range(m_rows // cm):
            rows = pl.ds(m * cm, cm)
            o_ref[rows, :] += lax.dot_general(dy_ref[rows, :], w_ref[...], (((1,), (1,)), ((), ())),
                                              preferred_element_type=F32)

    return pl.pallas_call(
        body, name=name, grid=(n_tiles,),
        in_specs=[pl.BlockSpec((m_rows, tn), lambda t: (0, t)),
                  pl.BlockSpec((None, kc, tn), lambda t: ((tile0 + t) // tps, 0, (tile0 + t) % tps))]
        + [pl.BlockSpec(memory_space=pl.ANY)] * len(extra),
        out_specs=pl.BlockSpec((m_rows, kc), lambda t: (0, 0)),
        out_shape=jax.ShapeDtypeStruct((m_rows, kc), F32),
        compiler_params=_params("arbitrary"),
    )(dy, w3, *extra)


CONV_CHUNK = 16


def _shift_copies(buf, shifted):
    rows = shifted.shape[1]
    for s in range(1, 8):
        shifted[s - 1] = buf[pl.ds(s, rows), :]


def _shifted_rows(buf, shifted, offset, r0):
    s = offset % 8
    if s == 0:
        return buf[pl.ds(r0 + offset, CONV_CHUNK), :]
    return shifted[s - 1, pl.ds(r0 + (offset - s), CONV_CHUNK), :]


def _conv_fwd(proj, conv_w, conv_b, ln_g, ln_b, name):
    tm = 256
    hb = tm // HALO

    def body(vg_ref, halo_ref, z_ref, w_ref, b_ref, g_ref, be_ref, u5_ref, u5t_ref, u2_ref, buf, shifted):
        i = pl.program_id(0)
        u1 = vg_ref[:, :D_MODEL] * _sigmoid(vg_ref[:, D_MODEL:])
        u1h = halo_ref[:, :D_MODEL] * _sigmoid(halo_ref[:, D_MODEL:])
        buf[pl.ds(0, HALO), :] = jnp.where(i > 0, u1h, 0.0)
        buf[pl.ds(HALO, tm), :] = u1
        _shift_copies(buf, shifted)

        def chunk(ci, carry):
            r0 = pl.multiple_of(ci * CONV_CHUNK, CONV_CHUNK)
            acc = jnp.broadcast_to(b_ref[...], (CONV_CHUNK, D_MODEL))
            for k in range(CONV_WIDTH):
                acc = acc + w_ref[k:k + 1, :] * _shifted_rows(buf, shifted, HALO - (CONV_WIDTH - 1) + k, r0)
            u2_ref[pl.ds(r0, CONV_CHUNK), :] = acc
            return carry

        lax.fori_loop(0, tm // CONV_CHUNK, chunk, 0)
        acc = u2_ref[...]
        mu = jnp.mean(acc, axis=-1, keepdims=True)
        xc = acc - mu
        rstd = lax.rsqrt(jnp.mean(xc * xc, axis=-1, keepdims=True) + NORM_EPS)
        u3 = xc * rstd * g_ref[...] + be_ref[...]
        zv = z_ref[...]
        u5 = u3 * _sigmoid(u3) * (zv * _sigmoid(zv))
        u5_ref[...] = u5.astype(BF16)
        u5t_ref[...] = u5.T.astype(BF16)

    return pl.pallas_call(
        body, name=name, grid=(SEQ // tm,),
        in_specs=[pl.BlockSpec((tm, 2 * D_MODEL), lambda i: (i, 0)),
                  pl.BlockSpec((HALO, 2 * D_MODEL), lambda i: (jnp.maximum(i * hb - 1, 0), 0)),
                  _row_spec(tm, D_MODEL, 2),
                  _vec_spec(CONV_WIDTH, D_MODEL)] + [_vec_spec(1, D_MODEL)] * 3,
        out_specs=[_row_spec(tm, D_MODEL), pl.BlockSpec((D_MODEL, tm), lambda i: (0, i)), _row_spec(tm, D_MODEL)],
        out_shape=[jax.ShapeDtypeStruct((SEQ, D_MODEL), BF16), jax.ShapeDtypeStruct((D_MODEL, SEQ), BF16),
                   jax.ShapeDtypeStruct((SEQ, D_MODEL), F32)],
        scratch_shapes=[pltpu.VMEM((HALO + tm, D_MODEL), F32), pltpu.VMEM((7, HALO + tm - 8, D_MODEL), F32)],
        compiler_params=_params("parallel"),
    )(proj, proj, proj, conv_w, conv_b, ln_g, ln_b)


def _conv_bwd_pointwise(du5, proj, u2, ln_g, ln_b, name):
    tm = 256

    def body(du5_ref, z_ref, u2_ref, g_ref, be_ref, du2_ref, dz_ref, sums_ref):
        u2v = u2_ref[...]
        mu = jnp.mean(u2v, axis=-1, keepdims=True)
        xc = u2v - mu
        rstd = lax.rsqrt(jnp.mean(xc * xc, axis=-1, keepdims=True) + NORM_EPS)
        xhat = xc * rstd
        u3 = xhat * g_ref[...] + be_ref[...]
        s3 = _sigmoid(u3)
        u4 = u3 * s3
        zv = z_ref[...]
        sz = _sigmoid(zv)
        du5v = du5_ref[...]
        dz_ref[...] = du5v * u4 * (sz * (1.0 + zv * (1.0 - sz)))
        du3 = du5v * (zv * sz) * (s3 * (1.0 + u3 * (1.0 - s3)))
        dxhat = du3 * g_ref[...]
        du2 = rstd * (dxhat - jnp.mean(dxhat, axis=-1, keepdims=True)
                      - xhat * jnp.mean(dxhat * xhat, axis=-1, keepdims=True))
        du2_ref[...] = du2
        sums = jnp.concatenate([
            jnp.sum(du3 * xhat, axis=0, keepdims=True),
            jnp.sum(du3, axis=0, keepdims=True),
            jnp.sum(du2, axis=0, keepdims=True),
            jnp.zeros((5, D_MODEL), F32)], axis=0)

        @pl.when(pl.program_id(0) == 0)
        def _():
            sums_ref[...] = jnp.zeros_like(sums_ref)

        sums_ref[...] += sums

    return pl.pallas_call(
        body, name=name, grid=(SEQ // tm,),
        in_specs=[_row_spec(tm, D_MODEL), _row_spec(tm, D_MODEL, 2), _row_spec(tm, D_MODEL),
                  _vec_spec(1, D_MODEL), _vec_spec(1, D_MODEL)],
        out_specs=[_row_spec(tm, D_MODEL), _row_spec(tm, D_MODEL), _vec_spec(8, D_MODEL)],
        out_shape=[jax.ShapeDtypeStruct((SEQ, D_MODEL), F32), jax.ShapeDtypeStruct((SEQ, D_MODEL), F32),
                   jax.ShapeDtypeStruct((8, D_MODEL), F32)],
        compiler_params=_params("arbitrary"),
    )(du5, proj, u2, ln_g, ln_b)


def _conv_bwd_taps(du2, dz, proj, conv_w, name):
    tm = 256
    hb = tm // HALO
    n_blocks = SEQ // tm

    def body(du2_ref, dnext_ref, dz_ref, vg_ref, halo_ref, w_ref, dproj_ref, dw_ref,
             ubuf, dbuf, ushift, dshift, sgbuf, dwacc):
        i = pl.program_id(0)
        sg = _sigmoid(vg_ref[:, D_MODEL:])
        sgbuf[...] = sg
        u1h = halo_ref[:, :D_MODEL] * _sigmoid(halo_ref[:, D_MODEL:])
        ubuf[pl.ds(0, HALO), :] = jnp.where(i > 0, u1h, 0.0)
        ubuf[pl.ds(HALO, tm), :] = vg_ref[:, :D_MODEL] * sg
        dbuf[pl.ds(0, tm), :] = du2_ref[...]
        dbuf[pl.ds(tm, HALO), :] = jnp.where(i < n_blocks - 1, dnext_ref[...], 0.0)
        _shift_copies(ubuf, ushift)
        _shift_copies(dbuf, dshift)

        @pl.when(i == 0)
        def _():
            dwacc[...] = jnp.zeros_like(dwacc)

        def chunk(ci, carry):
            r0 = pl.multiple_of(ci * CONV_CHUNK, CONV_CHUNK)
            rows = pl.ds(r0, CONV_CHUNK)
            du2c = du2_ref[rows, :]
            du1 = jnp.zeros((CONV_CHUNK, D_MODEL), F32)
            for k in range(CONV_WIDTH):
                du1 = du1 + w_ref[k:k + 1, :] * _shifted_rows(dbuf, dshift, CONV_WIDTH - 1 - k, r0)
                prod = du2c * _shifted_rows(ubuf, ushift, HALO - (CONV_WIDTH - 1) + k, r0)
                dwacc[k] += prod[0:8] + prod[8:16]
            sgc = sgbuf[rows, :]
            dval = du1 * sgc
            dproj_ref[rows, 0:D_MODEL] = dval.astype(BF16)
            dproj_ref[rows, D_MODEL:2 * D_MODEL] = (dval * vg_ref[rows, 0:D_MODEL] * (1.0 - sgc)).astype(BF16)
            return carry

        lax.fori_loop(0, tm // CONV_CHUNK, chunk, 0)
        dproj_ref[:, 2 * D_MODEL:] = dz_ref[...].astype(BF16)

        @pl.when(i == n_blocks - 1)
        def _():
            for k in range(CONV_WIDTH):
                dw_ref[k:k + 1, :] = jnp.sum(dwacc[k], axis=0, keepdims=True)
            dw_ref[CONV_WIDTH:, :] = jnp.zeros((32 - CONV_WIDTH, D_MODEL), F32)

    return pl.pallas_call(
        body, name=name, grid=(n_blocks,),
        in_specs=[_row_spec(tm, D_MODEL),
                  pl.BlockSpec((HALO, D_MODEL), lambda i: (jnp.minimum((i + 1) * hb, SEQ // HALO - 1), 0)),
                  _row_spec(tm, D_MODEL),
                  pl.BlockSpec((tm, 2 * D_MODEL), lambda i: (i, 0)),
                  pl.BlockSpec((HALO, 2 * D_MODEL), lambda i: (jnp.maximum(i * hb - 1, 0), 0)),
                  _vec_spec(CONV_WIDTH, D_MODEL)],
        out_specs=[_row_spec(tm, 3 * D_MODEL), _vec_spec(32, D_MODEL)],
        out_shape=[jax.ShapeDtypeStruct((SEQ, 3 * D_MODEL), BF16), jax.ShapeDtypeStruct((32, D_MODEL), F32)],
        scratch_shapes=[pltpu.VMEM((HALO + tm, D_MODEL), F32), pltpu.VMEM((tm + HALO, D_MODEL), F32),
                        pltpu.VMEM((7, HALO + tm - 8, D_MODEL), F32), pltpu.VMEM((7, HALO + tm - 8, D_MODEL), F32),
                        pltpu.VMEM((tm, D_MODEL), F32), pltpu.VMEM((CONV_WIDTH, 8, D_MODEL), F32)],
        compiler_params=_params("arbitrary"),
    )(du2, du2, dz, proj, proj, conv_w)


def _out_a(u5, w_out, x, gate, g1, scale1, shift1, name):
    tm = 256

    def body(u_ref, w_ref, x_ref, gate_ref, g_ref, sc_ref, sh_ref, x1_ref, y_ref, h_ref, ht_ref):
        y = jnp.dot(u_ref[...], w_ref[...], preferred_element_type=F32)
        x1 = x_ref[...] + gate_ref[...] * y
        y_ref[...] = y
        x1_ref[...] = x1
        h = _normmod(x1, g_ref[...], sc_ref[...], sh_ref[...])
        h_ref[...] = h.astype(BF16)
        ht_ref[...] = h.T.astype(BF16)

    return pl.pallas_call(
        body, name=name, grid=(SEQ // tm,),
        in_specs=[_row_spec(tm, D_MODEL), _vec_spec(D_MODEL, D_MODEL), _row_spec(tm, D_MODEL)]
        + [_vec_spec(1, D_MODEL)] * 4,
        out_specs=[_row_spec(tm, D_MODEL), _row_spec(tm, D_MODEL), _row_spec(tm, D_MODEL),
                   pl.BlockSpec((D_MODEL, tm), lambda i: (0, i))],
        out_shape=[jax.ShapeDtypeStruct((SEQ, D_MODEL), F32), jax.ShapeDtypeStruct((SEQ, D_MODEL), F32),
                   jax.ShapeDtypeStruct((SEQ, D_MODEL), BF16), jax.ShapeDtypeStruct((D_MODEL, SEQ), BF16)],
        compiler_params=_params("parallel"),
    )(u5, w_out, x, gate, g1, scale1, shift1)


def _out_b_loss(u, w_out, x1, gate, target, name):
    tm = 256

    def body(u_ref, w_ref, x_ref, gate_ref, t_ref, e_ref, dy_ref, sums_ref):
        y = jnp.dot(u_ref[...], w_ref[...], preferred_element_type=F32)
        diff = x_ref[...] + gate_ref[...] * y - t_ref[...]
        e = diff * (1.0 / D_MODEL)
        e_ref[...] = e
        dy_ref[...] = (e * gate_ref[...]).astype(BF16)
        sums = jnp.concatenate([
            jnp.sum(e * y, axis=0, keepdims=True),
            jnp.sum(diff * diff, axis=0, keepdims=True),
            jnp.zeros((6, D_MODEL), F32)], axis=0)

        @pl.when(pl.program_id(0) == 0)
        def _():
            sums_ref[...] = jnp.zeros_like(sums_ref)

        sums_ref[...] += sums

    return pl.pallas_call(
        body, name=name, grid=(SEQ // tm,),
        in_specs=[_row_spec(tm, D_MODEL), _vec_spec(D_MODEL, D_MODEL), _row_spec(tm, D_MODEL),
                  _vec_spec(1, D_MODEL), _row_spec(tm, D_MODEL)],
        out_specs=[_row_spec(tm, D_MODEL), _row_spec(tm, D_MODEL), _vec_spec(8, D_MODEL)],
        out_shape=[jax.ShapeDtypeStruct((SEQ, D_MODEL), F32), jax.ShapeDtypeStruct((SEQ, D_MODEL), BF16),
                   jax.ShapeDtypeStruct((8, D_MODEL), F32)],
        compiler_params=_params("arbitrary"),
    )(u, w_out, x1, gate, target)


def _dgate_dy(dx1, y, gate, name):
    tm = 256

    def body(d_ref, y_ref, gate_ref, dy_ref, sums_ref):
        dv = d_ref[...]
        dy_ref[...] = (dv * gate_ref[...]).astype(BF16)
        sums = jnp.concatenate([jnp.sum(dv * y_ref[...], axis=0, keepdims=True), jnp.zeros((7, D_MODEL), F32)], axis=0)

        @pl.when(pl.program_id(0) == 0)
        def _():
            sums_ref[...] = jnp.zeros_like(sums_ref)

        sums_ref[...] += sums

    return pl.pallas_call(
        body, name=name, grid=(SEQ // tm,),
        in_specs=[_row_spec(tm, D_MODEL), _row_spec(tm, D_MODEL), _vec_spec(1, D_MODEL)],
        out_specs=[_row_spec(tm, D_MODEL), _vec_spec(8, D_MODEL)],
        out_shape=[jax.ShapeDtypeStruct((SEQ, D_MODEL), BF16), jax.ShapeDtypeStruct((8, D_MODEL), F32)],
        compiler_params=_params("arbitrary"),
    )(dx1, y, gate)


def _mm_nt_res(dy, w, name):
    tm = 256
    kc, n = w.shape

    def body(dy_ref, w_ref, o_ref):
        o_ref[...] = lax.dot_general(dy_ref[...], w_ref[...], (((1,), (1,)), ((), ())), preferred_element_type=F32)

    return pl.pallas_call(
        body, name=name, grid=(SEQ // tm,),
        in_specs=[_row_spec(tm, n), _vec_spec(kc, n)],
        out_specs=_row_spec(tm, kc),
        out_shape=jax.ShapeDtypeStruct((SEQ, kc), F32),
        compiler_params=_params("parallel"),
    )(dy, w)


def _seg_matrix():
    r = lax.broadcasted_iota(jnp.int32, (256, 256), 0) // HEAD_DIM
    c = lax.broadcasted_iota(jnp.int32, (256, 256), 1) // HEAD_DIM
    return (r == c).astype(BF16)


def _segsum(v, seg):
    hi = v.astype(BF16)
    lo = (v - hi.astype(F32)).astype(BF16)
    outs = []
    for c0 in range(0, D_MODEL, 256):
        outs.append(jnp.dot(hi[:, c0:c0 + 256], seg, preferred_element_type=F32)
                    + jnp.dot(lo[:, c0:c0 + 256], seg, preferred_element_type=F32))
    return jnp.concatenate(outs, axis=1)


def _qk_rstd(v, seg):
    return lax.rsqrt(_segsum(v * v, seg) * (1.0 / HEAD_DIM) + NORM_EPS)


def _qknorm_fwd(proj, qw, kw, seg, name):
    tm = 256

    def body(p_ref, qw_ref, kw_ref, seg_ref, q_ref, k_ref, v_ref):
        segv = seg_ref[...]
        q = p_ref[:, :D_MODEL]
        k = p_ref[:, D_MODEL:2 * D_MODEL]
        q_ref[...] = (q * _qk_rstd(q, segv) * qw_ref[...]).astype(BF16)
        k_ref[...] = (k * _qk_rstd(k, segv) * kw_ref[...]).astype(BF16)
        v_ref[...] = p_ref[:, 2 * D_MODEL:].astype(BF16)

    return pl.pallas_call(
        body, name=name, grid=(SEQ // tm,),
        in_specs=[_row_spec(tm, 3 * D_MODEL), _vec_spec(1, D_MODEL), _vec_spec(1, D_MODEL), _vec_spec(256, 256)],
        out_specs=[_row_spec(tm, D_MODEL)] * 3,
        out_shape=[jax.ShapeDtypeStruct((SEQ, D_MODEL), BF16)] * 3,
        compiler_params=_params("parallel"),
    )(proj, qw, kw, seg)


def _attn_masks(b, bpc, dilation, slope):
    if bpc == 1:
        qi = lax.broadcasted_iota(jnp.int32, (ATTN_BLOCK, ATTN_BLOCK), 0)
        kj = lax.broadcasted_iota(jnp.int32, (ATTN_BLOCK, ATTN_BLOCK), 1)
        steps = qi - kj
        return (steps * dilation).astype(F32), steps >= 0
    qi = lax.broadcasted_iota(jnp.int32, (ATTN_BLOCK, 2 * ATTN_BLOCK), 0)
    kj = lax.broadcasted_iota(jnp.int32, (ATTN_BLOCK, 2 * ATTN_BLOCK), 1)
    steps = qi + ATTN_BLOCK - kj
    has_prev = (b % bpc) != 0
    valid = (steps >= 0) & (steps <= ATTN_BLOCK) & (has_prev | (kj >= ATTN_BLOCK))
    return (steps * dilation).astype(F32), valid


def _key_tile(prev_ref, cur_ref, cols, bpc):
    if bpc == 1:
        return cur_ref[:, cols]
    return jnp.concatenate([prev_ref[:, cols], cur_ref[:, cols]], axis=0)


ATTN_HEADS_FWD = 8
ATTN_HEADS_BWD = 4
NT_DIMS = (((1,), (1,)), ((), ()))
TN_DIMS = (((0,), (0,)), ((), ()))
BATCH_NT_DIMS = (((2,), (2,)), ((0,), (0,)))
BATCH_NN_DIMS = (((2,), (1,)), ((0,), (0,)))
BATCH_TN_DIMS = (((1,), (1,)), ((0,), (0,)))


def _head_stack(tile_of, heads):
    return jnp.stack([tile_of(slice(h * HEAD_DIM, (h + 1) * HEAD_DIM)) for h in range(heads)], axis=0)


def _attn_specs(heads):
    cur = pl.BlockSpec((ATTN_BLOCK, heads * HEAD_DIM), lambda hg, b: (b, hg))
    prev = pl.BlockSpec((ATTN_BLOCK, heads * HEAD_DIM), lambda hg, b: (jnp.maximum(b - 1, 0), hg))
    return cur, prev


def _attn_fwd(q, k, v, slopes, dilation, name):
    bpc = SEQ // dilation // ATTN_BLOCK
    heads = ATTN_HEADS_FWD
    cur, prev = _attn_specs(heads)
    scale = HEAD_DIM ** -0.5

    def body(sl_ref, q_ref, kp_ref, kc_ref, vp_ref, vc_ref, o_ref, lse_ref):
        dist, valid = _attn_masks(pl.program_id(1), bpc, dilation, None)
        q3 = _head_stack(lambda cols: q_ref[:, cols], heads)
        k3 = _head_stack(lambda cols: _key_tile(kp_ref, kc_ref, cols, bpc), heads)
        v3 = _head_stack(lambda cols: _key_tile(vp_ref, vc_ref, cols, bpc), heads)
        s = lax.dot_general(q3, k3, BATCH_NT_DIMS, preferred_element_type=F32)
        s = jnp.where(valid[None], s * scale - dist[None] * sl_ref[...], NEG_INF)
        m = jnp.max(s, axis=-1, keepdims=True)
        p = jnp.exp(s - m)
        l = jnp.sum(p, axis=-1, keepdims=True)
        o3 = lax.dot_general(p.astype(BF16), v3, BATCH_NN_DIMS, preferred_element_type=F32) / l
        lse3 = m + jnp.log(l)
        for h in range(heads):
            cols = slice(h * HEAD_DIM, (h + 1) * HEAD_DIM)
            o_ref[:, cols] = o3[h]
            lse_ref[:, cols] = jnp.broadcast_to(lse3[h], (ATTN_BLOCK, HEAD_DIM))

    return pl.pallas_call(
        body, name=name, grid=(N_HEADS // heads, SEQ // ATTN_BLOCK),
        in_specs=[pl.BlockSpec((heads, 1, 1), lambda hg, b: (hg, 0, 0)), cur, prev, cur, prev, cur],
        out_specs=[cur, cur],
        out_shape=[jax.ShapeDtypeStruct((SEQ, D_MODEL), F32)] * 2,
        compiler_params=_params("parallel", "parallel"),
    )(slopes.reshape(N_HEADS, 1, 1), q, k, k, v, v)


def _merge_fwd(o_parts, lse_parts, z, name):
    tm = 256

    def body(o0, o1, o2, l0, l1, l2, z_ref, u_ref, ut_ref, o_ref, lse_ref):
        ls = [l0[...], l1[...], l2[...]]
        m = jnp.maximum(jnp.maximum(ls[0], ls[1]), ls[2])
        tot = m + jnp.log(jnp.exp(ls[0] - m) + jnp.exp(ls[1] - m) + jnp.exp(ls[2] - m))
        o = (jnp.exp(ls[0] - tot) * o0[...] + jnp.exp(ls[1] - tot) * o1[...] + jnp.exp(ls[2] - tot) * o2[...])
        zv = z_ref[...]
        u = o * (zv * _sigmoid(zv))
        u_ref[...] = u.astype(BF16)
        ut_ref[...] = u.T.astype(BF16)
        o_ref[...] = o
        lse_ref[...] = tot

    return pl.pallas_call(
        body, name=name, grid=(SEQ // tm,),
        in_specs=[_row_spec(tm, D_MODEL)] * 7,
        out_specs=[_row_spec(tm, D_MODEL), pl.BlockSpec((D_MODEL, tm), lambda i: (0, i)),
                   _row_spec(tm, D_MODEL), _row_spec(tm, D_MODEL)],
        out_shape=[jax.ShapeDtypeStruct((SEQ, D_MODEL), BF16), jax.ShapeDtypeStruct((D_MODEL, SEQ), BF16),
                   jax.ShapeDtypeStruct((SEQ, D_MODEL), F32), jax.ShapeDtypeStruct((SEQ, D_MODEL), F32)],
        compiler_params=_params("parallel"),
    )(*o_parts, *lse_parts, z)


def _merge_bwd(du, o, z, seg, name):
    tm = 256

    def body(du_ref, o_ref, z_ref, seg_ref, do_ref, dz_ref, delta_ref):
        zv = z_ref[...]
        sz = _sigmoid(zv)
        duv = du_ref[...]
        ov = o_ref[...]
        do = duv * (zv * sz)
        do_ref[...] = do.astype(BF16)
        dz_ref[...] = (duv * ov * (sz * (1.0 + zv * (1.0 - sz)))).astype(BF16)
        delta_ref[...] = _segsum(do * ov, seg_ref[...])

    return pl.pallas_call(
        body, name=name, grid=(SEQ // tm,),
        in_specs=[_row_spec(tm, D_MODEL)] * 3 + [_vec_spec(256, 256)],
        out_specs=[_row_spec(tm, D_MODEL)] * 3,
        out_shape=[jax.ShapeDtypeStruct((SEQ, D_MODEL), BF16), jax.ShapeDtypeStruct((SEQ, D_MODEL), BF16),
                   jax.ShapeDtypeStruct((SEQ, D_MODEL), F32)],
        compiler_params=_params("parallel"),
    )(du, o, z, seg)


def _attn_bwd(q, k, v, do, lse, delta, slopes, dilation, name):
    bpc = SEQ // dilation // ATTN_BLOCK
    heads = ATTN_HEADS_BWD
    cur, prev = _attn_specs(heads)
    scale = HEAD_DIM ** -0.5

    def body(sl_ref, q_ref, kp_ref, kc_ref, vp_ref, vc_ref, do_ref, lse_ref, dl_ref,
             dq_ref, dkc_ref, dkp_ref, dvc_ref, dvp_ref):
        dist, valid = _attn_masks(pl.program_id(1), bpc, dilation, None)
        q3 = _head_stack(lambda cols: q_ref[:, cols], heads)
        k3 = _head_stack(lambda cols: _key_tile(kp_ref, kc_ref, cols, bpc), heads)
        v3 = _head_stack(lambda cols: _key_tile(vp_ref, vc_ref, cols, bpc), heads)
        do3 = _head_stack(lambda cols: do_ref[:, cols], heads)
        lse3 = _head_stack(lambda cols: lse_ref[:, cols.start:cols.start + 1], heads)
        dl3 = _head_stack(lambda cols: dl_ref[:, cols.start:cols.start + 1], heads)
        s = lax.dot_general(q3, k3, BATCH_NT_DIMS, preferred_element_type=F32)
        p = jnp.exp(jnp.where(valid[None], s * scale - dist[None] * sl_ref[...], NEG_INF) - lse3)
        dp = lax.dot_general(do3, v3, BATCH_NT_DIMS, preferred_element_type=F32)
        ds = (p * (dp - dl3) * scale).astype(BF16)
        dq3 = lax.dot_general(ds, k3, BATCH_NN_DIMS, preferred_element_type=F32)
        dk3 = lax.dot_general(ds, q3, BATCH_TN_DIMS, preferred_element_type=F32)
        dv3 = lax.dot_general(p.astype(BF16), do3, BATCH_TN_DIMS, preferred_element_type=F32)
        for h in range(heads):
            cols = slice(h * HEAD_DIM, (h + 1) * HEAD_DIM)
            dq_ref[:, cols] = dq3[h]
            if bpc == 1:
                zeros = jnp.zeros((ATTN_BLOCK, HEAD_DIM), F32)
                dkp_ref[:, cols], dkc_ref[:, cols] = zeros, dk3[h]
                dvp_ref[:, cols], dvc_ref[:, cols] = zeros, dv3[h]
            else:
                dkp_ref[:, cols], dkc_ref[:, cols] = dk3[h, :ATTN_BLOCK], dk3[h, ATTN_BLOCK:]
                dvp_ref[:, cols], dvc_ref[:, cols] = dv3[h, :ATTN_BLOCK], dv3[h, ATTN_BLOCK:]

    return pl.pallas_call(
        body, name=name, grid=(N_HEADS // heads, SEQ // ATTN_BLOCK),
        in_specs=[pl.BlockSpec((heads, 1, 1), lambda hg, b: (hg, 0, 0)), cur, prev, cur, prev, cur, cur, cur, cur],
        out_specs=[cur] * 5,
        out_shape=[jax.ShapeDtypeStruct((SEQ, D_MODEL), F32)] * 5,
        compiler_params=_params("parallel", "parallel"),
    )(slopes.reshape(N_HEADS, 1, 1), q, k, k, v, v, do, lse, delta)


def _qknorm_bwd(proj, qw, kw, seg, dq, dkc, dkp, dvc, dvp, name):
    tm = ATTN_BLOCK
    n_blocks = SEQ // tm
    nxt = pl.BlockSpec((tm, D_MODEL), lambda i: (jnp.minimum(i + 1, n_blocks - 1), 0))

    def body(p_ref, qw_ref, kw_ref, seg_ref, dq_ref, dkc_ref, dkp_ref, dvc_ref, dvp_ref, dproj_ref, sums_ref):
        i = pl.program_id(0)
        segv = seg_ref[...]
        has_next = i < n_blocks - 1
        dk = dkc_ref[...] + jnp.where(has_next, dkp_ref[...], 0.0)
        dv = dvc_ref[...] + jnp.where(has_next, dvp_ref[...], 0.0)
        sums = []
        for part, (raw, w, dn) in enumerate(((p_ref[:, :D_MODEL], qw_ref[...], dq_ref[...]),
                                             (p_ref[:, D_MODEL:2 * D_MODEL], kw_ref[...], dk))):
            r = _qk_rstd(raw, segv)
            gq = dn * w
            draw = r * gq - raw * (r * r * r) * (_segsum(raw * gq, segv) * (1.0 / HEAD_DIM))
            dproj_ref[:, part * D_MODEL:(part + 1) * D_MODEL] = draw.astype(BF16)
            sums.append(jnp.sum(dn * raw * r, axis=0, keepdims=True))
        dproj_ref[:, 2 * D_MODEL:] = dv.astype(BF16)

        @pl.when(i == 0)
        def _():
            sums_ref[...] = jnp.zeros_like(sums_ref)

        sums_ref[...] += jnp.concatenate(sums + [jnp.zeros((6, D_MODEL), F32)], axis=0)

    return pl.pallas_call(
        body, name=name, grid=(n_blocks,),
        in_specs=[_row_spec(tm, 3 * D_MODEL), _vec_spec(1, D_MODEL), _vec_spec(1, D_MODEL), _vec_spec(256, 256),
                  _row_spec(tm, D_MODEL), _row_spec(tm, D_MODEL), nxt, _row_spec(tm, D_MODEL), nxt],
        out_specs=[_row_spec(tm, 3 * D_MODEL), _vec_spec(8, D_MODEL)],
        out_shape=[jax.ShapeDtypeStruct((SEQ, 3 * D_MODEL), BF16), jax.ShapeDtypeStruct((8, D_MODEL), F32)],
        compiler_params=_params("arbitrary"),
    )(proj, qw, kw, seg, dq, dkc, dkp, dvc, dvp)


def _to_classes(a, dilation):
    if dilation == 1:
        return a
    s, c = a.shape
    return a.reshape(s // dilation, dilation, c).transpose(1, 0, 2).reshape(s, c)


def _from_classes(a, dilation):
    if dilation == 1:
        return a
    s, c = a.shape
    return a.reshape(dilation, s // dilation, c).transpose(1, 0, 2).reshape(s, c)


def _cols_to_classes(a, dilation):
    if dilation == 1:
        return a
    r, s = a.shape
    return a.reshape(r, s // dilation, dilation).transpose(0, 2, 1).reshape(r, s)


B_TN = 512
B_GROUP_TILES = 3 * D_MODEL // B_TN
B_Z_TILE0 = 3 * B_GROUP_TILES
B_Z_TILES = D_MODEL // B_TN


def _local_step(x, target, mods, norm_g, conv_w, conv_b, ln_g, ln_b, q_norm, k_norm,
                weights_a, weights_b, forward_weights_b, send_grads_b, forward_grads_b, send_grads_a):
    row = lambda a, i: a[i:i + 1]
    shift0, scale0, gate0 = row(mods[0], 0), row(mods[0], 1), row(mods[0], 2)
    shift1, scale1, gate1 = row(mods[1], 0), row(mods[1], 1), row(mods[1], 2)
    g0, g1 = row(norm_g, 0), row(norm_g, 1)
    seg = _seg_matrix()
    slopes = jnp.exp2(-8.0 * jnp.arange(1, N_HEADS + 1, dtype=F32) / N_HEADS)
    qw = [jnp.tile(q_norm[g:g + 1], (1, N_HEADS)) for g in range(3)]
    kw = [jnp.tile(k_norm[g:g + 1], (1, N_HEADS)) for g in range(3)]

    h0, h0t = _normmod_fwd(x, g0, scale0, shift0, "prenorm0")
    wa_in, wa_out = weights_a(h0)
    ja, _, nsa = wa_in.shape
    proj_a = _mm(h0, wa_in, tn=nsa, tile0=0, n_tiles=ja, out_dtype=F32, name="a_in")
    u5, u5t, u2 = _conv_fwd(proj_a, conv_w, conv_b, ln_g, ln_b, "a_conv")
    token = forward_weights_b(u5)
    x1, y_a, h1, h1t = _out_a(u5, wa_out, x, gate0 + token[0:1, 0:1], g1, scale1, shift1, "a_out")

    wb_in, wb_out = weights_b(x1)
    jb, _, nsb = wb_in.shape
    h1c =[_to_classes(h1, d) for d in DILATIONS]
    h1tc = [_cols_to_classes(h1t, d) for d in DILATIONS]
    z_b = _mm(h1, wb_in, tn=B_TN, tile0=B_Z_TILE0, n_tiles=B_Z_TILES, out_dtype=F32, name="b_in_z")
    proj_g, qkv, o_parts, lse_parts = [], [], [], []
    for g, d in enumerate(DILATIONS):
        pg = _mm(h1c[g], wb_in, tn=B_TN, tile0=g * B_GROUP_TILES, n_tiles=B_GROUP_TILES, out_dtype=F32,
                 name=f"b_in_g{g}")
        qn, kn, vn = _qknorm_fwd(pg, qw[g], kw[g], seg, f"b_qknorm_g{g}")
        og, lg = _attn_fwd(qn, kn, vn, slopes, d, f"b_attn_g{g}")
        proj_g.append(pg)
        qkv.append((qn, kn, vn))
        o_parts.append(_from_classes(og, d))
        lse_parts.append(_from_classes(lg, d))
    u_b, u_bt, o_b, lse_b = _merge_fwd(o_parts, lse_parts, z_b, "b_merge")
    e, dy_b, sums_loss = _out_b_loss(u_b, wb_out, x1, gate1, target, "b_out_loss")

    dwb_out = _mm(u_bt, dy_b, tn=D_MODEL, tile0=0, n_tiles=1, out_dtype=BF16, name="b_dwout")
    du_b = _mm_nt_res(dy_b, wb_out, "b_dout")
    do_b, dz_b, delta_b = _merge_bwd(du_b, o_b, z_b, seg, "b_merge_bwd")
    dwb_in = _mm(h1t, dz_b, tn=B_TN, tile0=B_Z_TILE0, n_tiles=B_Z_TILES, out_dtype=BF16, name="b_dwin_z",
                 out3d=(jb, nsb))
    dh1_parts = [_mm_nt(dz_b, wb_in, tn=B_TN, tile0=B_Z_TILE0, n_tiles=B_Z_TILES, name="b_dh_z")]
    qk_sums = []
    for g, d in enumerate(DILATIONS):
        qn, kn, vn = qkv[g]
        dq, dkc, dkp, dvc, dvp = _attn_bwd(qn, kn, vn, _to_classes(do_b, d), _to_classes(lse_b, d),
                                           _to_classes(delta_b, d), slopes, d, f"b_attn_bwd_g{g}")
        dproj, sums_qk = _qknorm_bwd(proj_g[g], qw[g], kw[g], seg, dq, dkc, dkp, dvc, dvp, f"b_qknorm_bwd_g{g}")
        qk_sums.append(sums_qk)
        dwb_in = _mm(h1tc[g], dproj, tn=B_TN, tile0=g * B_GROUP_TILES, n_tiles=B_GROUP_TILES, out_dtype=BF16,
                     name=f"b_dwin_g{g}", out3d=(jb, nsb), prev=dwb_in)
        dh = _mm_nt(dproj, wb_in, tn=B_TN, tile0=g * B_GROUP_TILES, n_tiles=B_GROUP_TILES, name=f"b_dh_g{g}")
        dh1_parts.append(_from_classes(dh, d))
    token = send_grads_b(dwb_in, dwb_out)
    dx1, sums_n1 = _normmod_bwd(x1, g1, scale1 + token[0:1, 0:1], dh1_parts, e, "prenorm1_bwd")
    token = forward_grads_b(dx1)

    dy_a, sums_ga = _dgate_dy(dx1, y_a, gate0 + token[0:1, 0:1], "a_dgate")
    dwa_out = _mm(u5t, dy_a, tn=D_MODEL, tile0=0, n_tiles=1, out_dtype=BF16, name="a_dwout")
    du5 = _mm_nt_res(dy_a, wa_out, "a_dout")
    du2, dz_a, sums_ln = _conv_bwd_pointwise(du5, proj_a, u2, ln_g, ln_b, "a_conv_bwd_pw")
    dproj_a, dconv_w = _conv_bwd_taps(du2, dz_a, proj_a, conv_w, "a_conv_bwd_taps")
    dwa_in = _mm(h0t, dproj_a, tn=nsa, tile0=0, n_tiles=ja, out_dtype=BF16, name="a_dwin", out3d=(ja, nsa))
    token = send_grads_a(dwa_in, dwa_out)
    dh0 = _mm_nt(dproj_a, wa_in, tn=nsa, tile0=0, n_tiles=ja, name="a_dh", after=token)
    grad_x, sums_n0 = _normmod_bwd(x, g0, scale0, [dh0], dx1, "prenorm0_bwd")

    small = dict(
        dnorm_g=jnp.concatenate([sums_n0[0:1], sums_n1[0:1]], axis=0),
        dmod0=jnp.concatenate([sums_n0[2:3], sums_n0[1:2], sums_ga[0:1]], axis=0),
        dmod1=jnp.concatenate([sums_n1[2:3], sums_n1[1:2], sums_loss[0:1]], axis=0),
        dln_g=sums_ln[0:1], dln_b=sums_ln[1:2], dconv_b=sums_ln[2:3],
        dconv_w=dconv_w[:CONV_WIDTH],
        dq_norm=jnp.concatenate([s[0:1] for s in qk_sums], axis=0),
        dk_norm=jnp.concatenate([s[1:2] for s in qk_sums], axis=0),
        loss_cols=sums_loss[1:2],
    )
    return grad_x, small


def _adamw(w, g, m, v, name):
    rows, cols = w.shape
    tr = rows if rows <= 128 else 128
    c1 = 1.0 / (1.0 - ADAM_B1 ** ADAM_STEP)
    c2 = 1.0 / (1.0 - ADAM_B2 ** ADAM_STEP)

    def body(w_ref, g_ref, m_ref, v_ref, d_ref, mo_ref, vo_ref):
        gv = g_ref[...]
        mn = ADAM_B1 * m_ref[...] + (1.0 - ADAM_B1) * gv
        vn = ADAM_B2 * v_ref[...] + (1.0 - ADAM_B2) * (gv * gv)
        mo_ref[...] = mn
        vo_ref[...] = vn
        d_ref[...] = -ADAM_LR * ((mn * c1) / (jnp.sqrt(vn * c2) + ADAM_EPS) + ADAM_WD * w_ref[...])

    spec = pl.BlockSpec((tr, cols), lambda i: (i, 0))
    return pl.pallas_call(
        body, name=name, grid=(rows // tr,), in_specs=[spec] * 4, out_specs=[spec] * 3,
        out_shape=[jax.ShapeDtypeStruct((rows, cols), F32)] * 3,
        compiler_params=_params("parallel"),
    )(w, g, m, v)


def _cast_into_slot(w, chip_idx, name):
    rows, cols = w.shape
    tr = 256

    def body(ch_ref, w_ref, o_ref):
        o_ref[...] = w_ref[...].astype(BF16)

    return pl.pallas_call(
        body, name=name,
        grid_spec=pltpu.PrefetchScalarGridSpec(
            num_scalar_prefetch=1, grid=(rows // tr,),
            in_specs=[pl.BlockSpec((tr, cols), lambda i, ch: (i, 0))],
            out_specs=pl.BlockSpec((None, tr, cols), lambda i, ch: (ch[0], i, 0))),
        out_shape=jax.ShapeDtypeStruct((N_CHIPS, rows, cols), BF16), compiler_params=_params("parallel"),
    )(chip_idx, w)


def _position():
    x, y, c = lax.axis_index("x"), lax.axis_index("y"), lax.axis_index("c")
    return x, y, c


def _xor_peer(x, y, c, k):
    return (x ^ ((k >> 2) & 1), y ^ ((k >> 1) & 1), c ^ (k & 1))


def _chip_peer(x, y, k):
    return (x ^ ((k >> 1) & 1), y ^ (k & 1))


def _ada_forward(c_row, ada_w, ada_b, conv_w):
    ns = ada_w.shape[2]
    cw = conv_w.shape[1]

    def body(c_ref, w_ref, b_ref, cv_ref, mod_ref, sc_ref, cvo_ref,
             c_all, mp, parts, cv_parts, send1, recv1, send2, recv2, send3, recv3):
        x, y, c = _position()
        me = 4 * x + 2 * y + c
        chip = 2 * x + y

        def c_copy(k):
            return pltpu.make_async_remote_copy(
                src_ref=c_all.at[me], dst_ref=c_all.at[me], send_sem=send1.at[k - 1], recv_sem=recv1.at[k - 1],
                device_id=_xor_peer(x, y, c, k), device_id_type=MESH)

        def cv_copy(k):
            px, py = _chip_peer(x, y, k)
            return pltpu.make_async_remote_copy(
                src_ref=cv_parts.at[chip], dst_ref=cv_parts.at[chip], send_sem=send3.at[k - 1],
                recv_sem=recv3.at[k - 1], device_id=(px, py, c), device_id_type=MESH)

        c_all[me] = c_ref[...]
        cv_parts[chip] = cv_ref[...]
        for k in range(1, N_DEV):
            c_copy(k).start()
        for k in range(1, N_CHIPS):
            cv_copy(k).start()
        for k in range(1, N_DEV):
            c_copy(k).wait_recv()
        cv = jnp.concatenate([c_all[i] for i in range(N_DEV)], axis=0)
        sc = cv * _sigmoid(cv)
        sc_ref[...] = sc
        for l in range(2):
            res = jnp.dot(sc, w_ref[l], preferred_element_type=F32, precision=lax.Precision.HIGHEST)
            for i in range(N_DEV):
                mp[i, l:l + 1, :] = res[i:i + 1, :]

        def mod_copy(k):
            px, py = _chip_peer(x, y, k)
            return pltpu.make_async_remote_copy(
                src_ref=mp.at[4 * px + 2 * py + c], dst_ref=parts.at[chip], send_sem=send2.at[k - 1],
                recv_sem=recv2.at[k - 1], device_id=(px, py, c), device_id_type=MESH)

        for k in range(1, N_CHIPS):
            mod_copy(k).start()
        parts[chip] = mp[me]
        for k in range(1, N_CHIPS):
            mod_copy(k).wait_recv()
            cv_copy(k).wait_recv()
        mod_ref[...] = jnp.concatenate([parts[j] for j in range(N_CHIPS)], axis=1) + b_ref[...]
        cvo_ref[...] = jnp.concatenate([cv_parts[j] for j in range(N_CHIPS)], axis=1)
        for k in range(1, N_DEV):
            c_copy(k).wait_send()
        for k in range(1, N_CHIPS):
            mod_copy(k).wait_send()
            cv_copy(k).wait_send()

    vm = pl.BlockSpec(memory_space=pltpu.VMEM)
    return pl.pallas_call(
        body, name="ada_forward",
        in_specs=[vm] * 4, out_specs=[vm] * 3,
        out_shape=[jax.ShapeDtypeStruct((2, 3 * D_MODEL), F32), jax.ShapeDtypeStruct((N_DEV, D_MODEL), F32),
                   jax.ShapeDtypeStruct((CONV_WIDTH, N_CHIPS * cw), F32)],
        scratch_shapes=[pltpu.VMEM((N_DEV, 1, D_MODEL), F32), pltpu.VMEM((N_DEV, 2, ns), F32),
                        pltpu.VMEM((N_CHIPS, 2, ns), F32), pltpu.VMEM((N_CHIPS, CONV_WIDTH, cw), F32),
                        pltpu.SemaphoreType.DMA((N_DEV - 1,)), pltpu.SemaphoreType.DMA((N_DEV - 1,)),
                        pltpu.SemaphoreType.DMA((N_CHIPS - 1,)), pltpu.SemaphoreType.DMA((N_CHIPS - 1,)),
                        pltpu.SemaphoreType.DMA((N_CHIPS - 1,)), pltpu.SemaphoreType.DMA((N_CHIPS - 1,))],
        compiler_params=pltpu.CompilerParams(vmem_limit_bytes=VMEM_LIMIT_BYTES),
    )(c_row, ada_w, ada_b, conv_w)


HBM_SPEC = pl.BlockSpec(memory_space=pltpu.HBM)
ANY_SPEC = pl.BlockSpec(memory_space=pl.ANY)
SEM_SPEC = pl.BlockSpec(memory_space=pltpu.SEMAPHORE)
SPLIT_PARAMS = dict(compiler_params=pltpu.CompilerParams(has_side_effects=pltpu.SideEffectType.DATAFLOW_SIDE_EFFECTING))
TOKEN = jax.ShapeDtypeStruct((8, 128), F32)


def _hbm(arrays):
    return [pltpu.with_memory_space_constraint(a, pltpu.HBM) for a in arrays]


def _hbm_like(arrays):
    return [pltpu.HBM(a.shape, a.dtype) for a in arrays]


def _gather_start(lands, after, name):
    n = len(lands)

    def body(*refs):
        ins = refs[:n]
        send, recv = refs[n + 1], refs[n + 2]
        x, y, c = _position()
        chip = 2 * x + y
        for t in range(n):
            rh = ins[t].shape[1] // 2
            for k in range(1, N_CHIPS):
                px, py = _chip_peer(x, y, k)
                block = ins[t].at[chip, pl.ds(c * rh, rh)]
                pltpu.make_async_remote_copy(
                    src_ref=block, dst_ref=block, send_sem=send.at[3 * t + k - 1], recv_sem=recv.at[3 * t + k - 1],
                    device_id=(px, py, c), device_id_type=MESH).start()
        refs[-1][...] = jnp.zeros(TOKEN.shape, F32)

    res = pl.pallas_call(
        body, name=name, in_specs=[HBM_SPEC] * n + [ANY_SPEC],
        out_specs=(SEM_SPEC, SEM_SPEC, *[HBM_SPEC] * n, pl.BlockSpec(memory_space=pltpu.VMEM)),
        out_shape=(pltpu.SemaphoreType.DMA((3 * n,)), pltpu.SemaphoreType.DMA((3 * n,)), *_hbm_like(lands), TOKEN),
        input_output_aliases={t: 2 + t for t in range(n)}, **SPLIT_PARAMS,
    )(*_hbm(lands), after)
    return res[0], res[1], list(res[2:2 + n]), res[-1]


def _gather_forward(send, recv, lands, after, name):
    n = len(lands)

    def body(*refs):
        ins = refs[:n]
        send1, recv1 = refs[n], refs[n + 1]
        send2, recv2 = refs[n + 3], refs[n + 4]
        x, y, c = _position()
        chip = 2 * x + y
        for t in range(n):
            rh = ins[t].shape[1] // 2
            half = pl.ds(c * rh, rh)
            for k in range(1, N_CHIPS):
                px, py = _chip_peer(x, y, k)
                s = 3 * t + k - 1
                got = ins[t].at[2 * px + py, half]
                cp = pltpu.make_async_remote_copy(
                    src_ref=ins[t].at[chip, half], dst_ref=got, send_sem=send1.at[s], recv_sem=recv1.at[s],
                    device_id=(px, py, c), device_id_type=MESH)
                cp.wait_send()
                cp.wait_recv()
                pltpu.make_async_remote_copy(
                    src_ref=got, dst_ref=got, send_sem=send2.at[s], recv_sem=recv2.at[s],
                    device_id=(x, y, 1 - c), device_id_type=MESH).start()
        refs[-1][...] = jnp.zeros(TOKEN.shape, F32)

    res = pl.pallas_call(
        body, name=name, in_specs=[HBM_SPEC] * n + [SEM_SPEC, SEM_SPEC, ANY_SPEC],
        out_specs=(SEM_SPEC, SEM_SPEC, *[HBM_SPEC] * n, pl.BlockSpec(memory_space=pltpu.VMEM)),
        out_shape=(pltpu.SemaphoreType.DMA((3 * n,)), pltpu.SemaphoreType.DMA((3 * n,)), *_hbm_like(lands), TOKEN),
        input_output_aliases={t: 2 + t for t in range(n)}, **SPLIT_PARAMS,
    )(*lands, send, recv, after)
    return res[0], res[1], list(res[2:2 + n]), res[-1]


def _gather_wait(send, recv, lands, after, name):
    n = len(lands)

    def body(*refs):
        ins = refs[:n]
        send_ref, recv_ref = refs[n], refs[n + 1]
        x, y, c = _position()
        for t in range(n):
            rh = ins[t].shape[1] // 2
            for k in range(1, N_CHIPS):
                px, py = _chip_peer(x, y, k)
                cp = pltpu.make_async_remote_copy(
                    src_ref=ins[t].at[2 * px + py, pl.ds(c * rh, rh)],
                    dst_ref=ins[t].at[2 * px + py, pl.ds((1 - c) * rh, rh)], send_sem=send_ref.at[3 * t + k - 1],
                    recv_sem=recv_ref.at[3 * t + k - 1], device_id=(x, y, 1 - c), device_id_type=MESH)
                cp.wait_send()
                cp.wait_recv()

    res = pl.pallas_call(
        body, name=name, in_specs=[HBM_SPEC] * n + [SEM_SPEC, SEM_SPEC, ANY_SPEC], out_specs=[HBM_SPEC] * n,
        out_shape=_hbm_like(lands), input_output_aliases={t: t for t in range(n)}, **SPLIT_PARAMS,
    )(*lands, send, recv, after)
    return list(res)


def _reduce_start(grads, after, name):
    n = len(grads)
    lands = [lax.empty((N_DEV, g.shape[1] // 2, g.shape[2]), BF16) for g in grads]

    def body(*refs):
        gs, ls = refs[:n], refs[n:2 * n]
        send, recv = refs[2 * n + 1], refs[2 * n + 2]
        x, y, c = _position()
        me = 4 * x + 2 * y + c
        for t in range(n):
            rh = gs[t].shape[1] // 2
            for k in range(1, N_DEV):
                px, py, pc = _xor_peer(x, y, c, k)
                pltpu.make_async_remote_copy(
                    src_ref=gs[t].at[2 * px + py, pl.ds(pc * rh, rh)], dst_ref=ls[t].at[me],
                    send_sem=send.at[7 * t + k - 1], recv_sem=recv.at[7 * t + k - 1],
                    device_id=(px, py, pc), device_id_type=MESH).start()
        refs[-1][...] = jnp.zeros(TOKEN.shape, F32)

    res = pl.pallas_call(
        body, name=name, in_specs=[HBM_SPEC] * (2 * n) + [ANY_SPEC],
        out_specs=(SEM_SPEC, SEM_SPEC, *[HBM_SPEC] * (2 * n), pl.BlockSpec(memory_space=pltpu.VMEM)),
        out_shape=(pltpu.SemaphoreType.DMA((7 * n,)), pltpu.SemaphoreType.DMA((7 * n,)),
                   *_hbm_like(grads), *_hbm_like(lands), TOKEN),
        input_output_aliases={t: 2 + t for t in range(2 * n)}, **SPLIT_PARAMS,
    )(*_hbm(grads), *_hbm(lands), after)
    return res[0], res[1], list(res[2:2 + n]), list(res[2 + n:2 + 2 * n]), res[-1]


def _reduce_wait(send, recv, grads, lands, after, name):
    n = len(grads)

    def body(*refs):
        gs, ls = refs[:n], refs[n:2 * n]
        send_ref, recv_ref = refs[2 * n], refs[2 * n + 1]
        x, y, c = _position()
        for t in range(n):
            rh = gs[t].shape[1] // 2
            for k in range(1, N_DEV):
                px, py, pc = _xor_peer(x, y, c, k)
                cp = pltpu.make_async_remote_copy(
                    src_ref=gs[t].at[2 * px + py, pl.ds(pc * rh, rh)], dst_ref=ls[t].at[4 * px + 2 * py + pc],
                    send_sem=send_ref.at[7 * t + k - 1], recv_sem=recv_ref.at[7 * t + k - 1],
                    device_id=(px, py, pc), device_id_type=MESH)
                cp.wait_send()
                cp.wait_recv()

    res = pl.pallas_call(
        body, name=name, in_specs=[HBM_SPEC] * (2 * n) + [SEM_SPEC, SEM_SPEC, ANY_SPEC], out_specs=[HBM_SPEC] * (2 * n),
        out_shape=_hbm_like(grads) + _hbm_like(lands), input_output_aliases={t: t for t in range(2 * n)}, **SPLIT_PARAMS,
    )(*grads, *lands, send, recv, after)
    return list(res[:n]), list(res[n:])


def _sum_devices(land, grad, dev_idx, name):
    _, rh, cols = land.shape
    tr = 128
    nb = rh // tr

    def body(idx_ref, l_ref, g_ref, o_ref):
        me = idx_ref[0]
        acc = jnp.where(me == 0, g_ref[...], l_ref[0]).astype(F32)
        for d in range(1, N_DEV):
            acc = acc + jnp.where(me == d, g_ref[...], l_ref[d]).astype(F32)
        o_ref[...] = acc

    return pl.pallas_call(
        body, name=name,
        grid_spec=pltpu.PrefetchScalarGridSpec(
            num_scalar_prefetch=1, grid=(nb,),
            in_specs=[pl.BlockSpec((N_DEV, tr, cols), lambda i, idx: (0, i, 0)),
                      pl.BlockSpec((None, tr, cols), lambda i, idx: (idx[1], idx[2] * nb + i, 0))],
            out_specs=pl.BlockSpec((tr, cols), lambda i, idx: (idx[2] * nb + i, 0))),
        out_shape=jax.ShapeDtypeStruct((2 * rh, cols), F32), compiler_params=_params("parallel"),
    )(dev_idx, land, grad)


def _split_start(name, arrays, n_sems, after, issue):
    m = len(arrays)

    def body(*refs):
        issue(refs[:m], refs[m + 1], refs[m + 2])
        refs[-1][...] = jnp.zeros(TOKEN.shape, F32)

    res = pl.pallas_call(
        body, name=name, in_specs=[HBM_SPEC] * m + [ANY_SPEC],
        out_specs=(SEM_SPEC, SEM_SPEC, *[HBM_SPEC] * m, pl.BlockSpec(memory_space=pltpu.VMEM)),
        out_shape=(pltpu.SemaphoreType.DMA((n_sems,)), pltpu.SemaphoreType.DMA((n_sems,)), *_hbm_like(arrays), TOKEN),
        input_output_aliases={t: 2 + t for t in range(m)}, **SPLIT_PARAMS,
    )(*_hbm(arrays), after)
    return res[0], res[1], list(res[2:2 + m]), res[-1]


def _split_wait(name, arrays, send, recv, after, await_all):
    m = len(arrays)

    def body(*refs):
        await_all(refs[:m], refs[m], refs[m + 1])

    res = pl.pallas_call(
        body, name=name, in_specs=[HBM_SPEC] * m + [SEM_SPEC, SEM_SPEC, ANY_SPEC], out_specs=[HBM_SPEC] * m,
        out_shape=_hbm_like(arrays), input_output_aliases={t: t for t in range(m)}, **SPLIT_PARAMS,
    )(*arrays, send, recv, after)
    return list(res)


def _sibling_copies(refs, send, recv, n):
    x, y, c = _position()
    cps = []
    for t in range(n):
        rh = refs[t].shape[1] // 2
        cps.append(pltpu.make_async_remote_copy(
            src_ref=refs[t].at[pl.ds(0, N_CHIPS), pl.ds((1 - c) * rh, rh)], dst_ref=refs[n + t],
            send_sem=send.at[t], recv_sem=recv.at[t], device_id=(x, y, 1 - c), device_id_type=MESH))
    return cps


def _reduce_sibling_start(grads, after, name):
    n = len(grads)
    lands = [lax.empty((N_CHIPS, g.shape[1] // 2, g.shape[2]), BF16) for g in grads]

    def issue(refs, send, recv):
        for cp in _sibling_copies(refs, send, recv, n):
            cp.start()

    return _split_start(name, list(grads) + lands, n, after, issue)


def _reduce_sibling_wait(send, recv, arrays, after, name):
    n = len(arrays) // 2

    def await_all(refs, send_ref, recv_ref):
        for cp in _sibling_copies(refs, send_ref, recv_ref, n):
            cp.wait_send()
            cp.wait_recv()

    res = _split_wait(name, arrays, send, recv, after, await_all)
    return res[:n], res[n:]


def _add_sibling_half(grad, got, dev_idx, name):
    j, r, cols = grad.shape
    rh = r // 2
    tr = 128
    nb = rh // tr

    def body(idx_ref, g_ref, got_ref, out_ref):
        out_ref[...] = (g_ref[...].astype(F32) + got_ref[...].astype(F32)).astype(BF16)

    return pl.pallas_call(
        body, name=name,
        grid_spec=pltpu.PrefetchScalarGridSpec(
            num_scalar_prefetch=1, grid=(j, nb),
            in_specs=[pl.BlockSpec((None, tr, cols), lambda jj, i, idx: (jj, idx[2] * nb + i, 0)),
                      pl.BlockSpec((None, tr, cols), lambda jj, i, idx: (jj, i, 0))],
            out_specs=pl.BlockSpec((None, tr, cols), lambda jj, i, idx: (jj, i, 0))),
        out_shape=jax.ShapeDtypeStruct((j, rh, cols), BF16),
        compiler_params=_params("parallel", "parallel"),
    )(dev_idx, grad, got)


def _chip_copies(refs, send, recv, n, receiving):
    x, y, c = _position()
    chip = 2 * x + y
    cps = []
    for t in range(n):
        for k in range(1, N_CHIPS):
            px, py = _chip_peer(x, y, k)
            cps.append(pltpu.make_async_remote_copy(
                src_ref=refs[t].at[2 * px + py], dst_ref=refs[n + t].at[2 * px + py if receiving else chip],
                send_sem=send.at[3 * t + k - 1], recv_sem=recv.at[3 * t + k - 1],
                device_id=(px, py, c), device_id_type=MESH))
    return cps


def _reduce_chips_start(partials, after, name):
    n = len(partials)
    lands = [lax.empty(p.shape, BF16) for p in partials]

    def issue(refs, send, recv):
        for cp in _chip_copies(refs, send, recv, n, False):
            cp.start()

    return _split_start(name, list(partials) + lands, 3 * n, after, issue)


def _reduce_chips_wait(send, recv, arrays, after, name):
    n = len(arrays) // 2

    def await_all(refs, send_ref, recv_ref):
        for cp in _chip_copies(refs, send_ref, recv_ref, n, True):
            cp.wait_send()
            cp.wait_recv()

    res = _split_wait(name, arrays, send, recv, after, await_all)
    return res[:n], res[n:]


def _sum_partials(land, partial, dev_idx, name):
    _, rh, cols = land.shape
    tr = 128
    nb = rh // tr

    def body(idx_ref, l_ref, p_ref, o_ref):
        chip = idx_ref[1]
        acc = jnp.where(chip == 0, p_ref[...], l_ref[0]).astype(F32)
        for s in range(1, N_CHIPS):
            acc = acc + jnp.where(chip == s, p_ref[...], l_ref[s]).astype(F32)
        o_ref[...] = acc

    return pl.pallas_call(
        body, name=name,
        grid_spec=pltpu.PrefetchScalarGridSpec(
            num_scalar_prefetch=1, grid=(nb,),
            in_specs=[pl.BlockSpec((N_CHIPS, tr, cols), lambda i, idx: (0, i, 0)),
                      pl.BlockSpec((None, tr, cols), lambda i, idx: (idx[1], i, 0))],
            out_specs=pl.BlockSpec((tr, cols), lambda i, idx: (idx[2] * nb + i, 0))),
        out_shape=jax.ShapeDtypeStruct((2 * rh, cols), F32), compiler_params=_params("parallel"),
    )(dev_idx, land, partial)


def _share_halves(totals):
    n = len(totals)

    def body(*refs):
        ins, outs = refs[:n], refs[n:2 * n]
        send, recv = refs[2 * n:]
        x, y, c = _position()
        cps = []
        for t in range(n):
            rh = ins[t].shape[0] // 2
            mine = pl.ds(c * rh, rh)
            cp = pltpu.make_async_remote_copy(
                src_ref=ins[t].at[mine], dst_ref=outs[t].at[mine], send_sem=send.at[t], recv_sem=recv.at[t],
                device_id=(x, y, 1 - c), device_id_type=MESH)
            cp.start()
            cps.append(cp)
        for cp in cps:
            cp.wait()

    return pl.pallas_call(
        body, name="reduce_share_" + "_".join(str(t.shape[1]) for t in totals), in_specs=[ANY_SPEC] * n,
        out_specs=[ANY_SPEC] * n, out_shape=[jax.ShapeDtypeStruct(t.shape, F32) for t in totals],
        input_output_aliases={t: t for t in range(n)},
        scratch_shapes=[pltpu.SemaphoreType.DMA((n,)), pltpu.SemaphoreType.DMA((n,))],
    )(*totals)


def _exchange_halves(grads):
    n = len(grads)
    hbm = pl.BlockSpec(memory_space=pl.ANY)

    def body(*refs):
        ins, outs = refs[:n], refs[n:2 * n]
        send, recv = refs[2 * n:]
        x, y, c = _position()
        cps = []
        for t in range(n):
            rh = ins[t].shape[1] // 2
            cp = pltpu.make_async_remote_copy(
                src_ref=ins[t].at[pl.ds(0, N_CHIPS), pl.ds((1 - c) * rh, rh)], dst_ref=outs[t], send_sem=send.at[t],
                recv_sem=recv.at[t], device_id=(x, y, 1 - c), device_id_type=MESH)
            cp.start()
            cps.append(cp)
        for cp in cps:
            cp.wait()

    return pl.pallas_call(
        body, name="reduce_exchange_halves", in_specs=[hbm] * n, out_specs=[hbm] * n,
        out_shape=[jax.ShapeDtypeStruct((g.shape[0], g.shape[1] // 2, g.shape[2]), BF16) for g in grads],
        scratch_shapes=[pltpu.SemaphoreType.DMA((n,)), pltpu.SemaphoreType.DMA((n,))],
    )(*grads)


def _add_halves(grad, got, c_idx, name):
    j, r, cols = grad.shape
    rh = r // 2
    tr = 128
    nb = rh // tr

    def body(c_ref, g_ref, o_ref_in, out_ref):
        out_ref[...] = (g_ref[...].astype(F32) + o_ref_in[...].astype(F32)).astype(BF16)

    return pl.pallas_call(
        body, name=name,
        grid_spec=pltpu.PrefetchScalarGridSpec(
            num_scalar_prefetch=1, grid=(j, nb),
            in_specs=[pl.BlockSpec((None, tr, cols), lambda jj, i, c_ref: (jj, c_ref[0] * nb + i, 0)),
                      pl.BlockSpec((None, tr, cols), lambda jj, i, c_ref: (jj, i, 0))],
            out_specs=pl.BlockSpec((None, tr, cols), lambda jj, i, c_ref: (jj, i, 0))),
        out_shape=jax.ShapeDtypeStruct((j, rh, cols), BF16),
        compiler_params=_params("parallel", "parallel"),
    )(c_idx, grad, got)


def _scatter_partials(partials):
    n = len(partials)
    hbm = pl.BlockSpec(memory_space=pl.ANY)

    def body(*refs):
        ins, outs = refs[:n], refs[n:2 * n]
        send, recv, local = refs[2 * n:]
        x, y, c = _position()
        chip = 2 * x + y
        cps, lcs = [], []
        for t in range(n):
            lc = pltpu.make_async_copy(ins[t].at[chip], outs[t].at[chip], local.at[t])
            lc.start()
            lcs.append(lc)
            for k in range(1, N_CHIPS):
                px, py = _chip_peer(x, y, k)
                s = 3 * t + k - 1
                cp = pltpu.make_async_remote_copy(
                    src_ref=ins[t].at[2 * px + py], dst_ref=outs[t].at[chip], send_sem=send.at[s],
                    recv_sem=recv.at[s], device_id=(px, py, c), device_id_type=MESH)
                cp.start()
                cps.append(cp)
        for cp in cps:
            cp.wait()
        for lc in lcs:
            lc.wait()

    return pl.pallas_call(
        body, name="reduce_scatter_partials", in_specs=[hbm] * n, out_specs=[hbm] * n,
        out_shape=[jax.ShapeDtypeStruct(p.shape, BF16) for p in partials],
        scratch_shapes=[pltpu.SemaphoreType.DMA((3 * n,)), pltpu.SemaphoreType.DMA((3 * n,)),
                        pltpu.SemaphoreType.DMA((n,))],
    )(*partials)


def _sum_chips(parts, name):
    j, rh, cols = parts.shape
    tr = 128

    def body(p_ref, o_ref):
        acc = p_ref[0].astype(F32)
        for s in range(1, j):
            acc = acc + p_ref[s].astype(F32)
        o_ref[...] = acc

    return pl.pallas_call(
        body, name=name, grid=(rh // tr,),
        in_specs=[pl.BlockSpec((j, tr, cols), lambda i: (0, i, 0))],
        out_specs=pl.BlockSpec((tr, cols), lambda i: (i, 0)),
        out_shape=jax.ShapeDtypeStruct((rh, cols), F32),
        compiler_params=_params("parallel"),
    )(parts)


def _share_totals(halves):
    n = len(halves)
    hbm = pl.BlockSpec(memory_space=pl.ANY)

    def body(*refs):
        ins, outs = refs[:n], refs[n:2 * n]
        send, recv, local = refs[2 * n:]
        x, y, c = _position()
        cps, lcs = [], []
        for t in range(n):
            rh = ins[t].shape[0]
            mine = outs[t].at[pl.ds(c * rh, rh)]
            lc = pltpu.make_async_copy(ins[t], mine, local.at[t])
            lc.start()
            lcs.append(lc)
            cp = pltpu.make_async_remote_copy(
                src_ref=ins[t], dst_ref=mine, send_sem=send.at[t], recv_sem=recv.at[t],
                device_id=(x, y, 1 - c), device_id_type=MESH)
            cp.start()
            cps.append(cp)
        for cp in cps:
            cp.wait()
        for lc in lcs:
            lc.wait()

    return pl.pallas_call(
        body, name="reduce_share_totals", in_specs=[hbm] * n, out_specs=[hbm] * n,
        out_shape=[jax.ShapeDtypeStruct((2 * h.shape[0], h.shape[1]), F32) for h in halves],
        scratch_shapes=[pltpu.SemaphoreType.DMA((n,)), pltpu.SemaphoreType.DMA((n,)),
                        pltpu.SemaphoreType.DMA((n,))],
    )(*halves)


SMALL_ROWS = 56


def _reduce_small(packed, silu_c):
    ns = 3 * D_MODEL // N_CHIPS

    def body(p_ref, sc_ref, tot_ref, gw_ref, loss_ref, qk_ref, allp, send, recv):
        x, y, c = _position()
        me = 4 * x + 2 * y + c
        chip = 2 * x + y

        def copy(k):
            return pltpu.make_async_remote_copy(
                src_ref=allp.at[me], dst_ref=allp.at[me], send_sem=send.at[k - 1], recv_sem=recv.at[k - 1],
                device_id=_xor_peer(x, y, c, k), device_id_type=MESH)

        allp[me] = p_ref[...]
        for k in range(1, N_DEV):
            copy(k).start()
        for k in range(1, N_DEV):
            copy(k).wait_recv()
        tot = allp[0]
        for i in range(1, N_DEV):
            tot = tot + allp[i]
        tot_ref[...] = tot
        loss_ref[...] = jnp.sum(tot[11:12, :], axis=1, keepdims=True) * (0.5 / D_MODEL)
        fold = tot[5:11, 0:HEAD_DIM]
        for h in range(1, N_HEADS):
            fold = fold + tot[5:11, h * HEAD_DIM:(h + 1) * HEAD_DIM]
        qk_ref[...] = jnp.concatenate([fold, jnp.zeros((2, HEAD_DIM), F32)], axis=0)
        sct = sc_ref[...].T
        rc = 64
        for l in range(2):
            dms = [allp[i, pl.ds(12 + 4 * l + chip, 1), :][:, :ns] for i in range(N_DEV)]
            for r0 in range(0, D_MODEL, rc):
                acc = sct[r0:r0 + rc, 0:1] * dms[0]
                for i in range(1, N_DEV):
                    acc = acc + sct[r0:r0 + rc, i:i + 1] * dms[i]
                gw_ref[l, r0:r0 + rc, :] = acc
        for k in range(1, N_DEV):
            copy(k).wait_send()

    vm = pl.BlockSpec(memory_space=pltpu.VMEM)
    return pl.pallas_call(
        body, name="reduce_small", in_specs=[vm, vm], out_specs=[vm] * 4,
        out_shape=[jax.ShapeDtypeStruct((SMALL_ROWS, D_MODEL), F32), jax.ShapeDtypeStruct((2, D_MODEL, ns), F32),
                   jax.ShapeDtypeStruct((1, 1), F32), jax.ShapeDtypeStruct((8, HEAD_DIM), F32)],
        scratch_shapes=[pltpu.VMEM((N_DEV, SMALL_ROWS, D_MODEL), F32),
                        pltpu.SemaphoreType.DMA((N_DEV - 1,)), pltpu.SemaphoreType.DMA((N_DEV - 1,))],
        compiler_params=pltpu.CompilerParams(vmem_limit_bytes=VMEM_LIMIT_BYTES),
    )(packed, silu_c)


def _reduce_big(grads, c_idx):
    names = list(grads)
    got = _exchange_halves([grads[k] for k in names])
    partials = [_add_halves(grads[k], got[i], c_idx, f"reduce_add_{k}") for i, k in enumerate(names)]
    parts = _scatter_partials(partials)
    halves = [_sum_chips(parts[i], f"reduce_sum_{k}") for i, k in enumerate(names)]
    totals = _share_totals(halves)
    return dict(zip(names, totals))


def kernel(x, c, norm_g, ada_w, ada_b, a_w_in, a_conv_w, a_conv_b, a_ln_g, a_ln_b, a_w_out, b_w_in, b_q_norm, b_k_norm, b_w_out, loss_target, m_norm_g, m_ada_w, m_ada_b, m_a_w_in, m_a_conv_w, m_a_conv_b, m_a_ln_g, m_a_ln_b, m_a_w_out, m_b_w_in, m_b_q_norm, m_b_k_norm, m_b_w_out, v_norm_g, v_ada_w, v_ada_b, v_a_w_in, v_a_conv_w, v_a_conv_b, v_a_ln_g, v_a_ln_b, v_a_w_out, v_b_w_in, v_b_q_norm, v_b_k_norm, v_b_w_out):
    chip = 2 * lax.axis_index("x") + lax.axis_index("y")
    core = lax.axis_index("c")
    chip_idx = chip.astype(jnp.int32).reshape(1)
    dev_idx = jnp.stack([2 * chip + core, chip, core]).astype(jnp.int32)

    mods, silu_c, conv_w_full = _ada_forward(c, ada_w, ada_b, a_conv_w[0])
    lands_a = [_cast_into_slot(a_w_in[0], chip_idx, "cast_a_w_in"), _cast_into_slot(a_w_out[0], chip_idx, "cast_a_w_out")]
    send_a, recv_a, lands_a, token_a = _gather_start(lands_a, mods, "gather_start_a")
    lands_b = [_cast_into_slot(b_w_in[0], chip_idx, "cast_b_w_in"), _cast_into_slot(b_w_out[0], chip_idx, "cast_b_w_out")]
    send_b, recv_b, lands_b, token_b = _gather_start(lands_b, token_a, "gather_start_b")
    mods = mods + token_b[0:2, 0:1]

    def weights_a(after):
        send, recv, lands, _ = _gather_forward(send_a, recv_a, lands_a, after, "gather_forward_a")
        w_in, w_out = _gather_wait(send, recv, lands, after, "gather_wait_a")
        return w_in, w_out.reshape(D_MODEL, D_MODEL)

    forwarded_b = []

    def weights_b(after):
        send, recv, lands, _ = forwarded_b
        w_in, w_out = _gather_wait(send, recv, lands, after, "gather_wait_b")
        return w_in, w_out.reshape(D_MODEL, D_MODEL)

    def forward_weights_b(after):
        forwarded_b.extend(_gather_forward(send_b, recv_b, lands_b, after, "gather_forward_b"))
        return forwarded_b[3]

    stage1, stage2 = {}, {}

    def send_grads(tag, dw_in, dw_out):
        grads = [dw_in, dw_out.reshape(N_CHIPS, D_MODEL // N_CHIPS, D_MODEL)]
        send, recv, arrays, token = _reduce_sibling_start(grads, dw_out, f"reduce_d2d_start_{tag}")
        stage1[tag] = (send, recv, arrays)
        return token

    def forward_grads(tag, after):
        send, recv, arrays = stage1[tag]
        grads, got = _reduce_sibling_wait(send, recv, arrays, after, f"reduce_d2d_wait_{tag}")
        partials = [_add_sibling_half(grads[i], got[i], dev_idx, f"reduce_add_{tag}_{i}") for i in range(2)]
        send, recv, arrays, token = _reduce_chips_start(partials, partials[1], f"reduce_ici_start_{tag}")
        stage2[tag] = (send, recv, arrays)
        return token

    def finish_grads(tag, after):
        send, recv, arrays = stage2[tag]
        partials, lands = _reduce_chips_wait(send, recv, arrays, after, f"reduce_ici_wait_{tag}")
        totals = [_sum_partials(lands[i], partials[i], dev_idx, f"reduce_sum_{tag}_{i}") for i in range(2)]
        return _share_halves(totals)

    grad_x, small = _local_step(
        x[0], loss_target[0], mods.reshape(2, 3, D_MODEL), norm_g, conv_w_full, a_conv_b, a_ln_g[0:1],
        a_ln_b[0:1], b_q_norm[0], b_k_norm[0], weights_a, weights_b, forward_weights_b,
        functools.partial(send_grads, "b"), functools.partial(forward_grads, "b"), functools.partial(send_grads, "a"))

    ns = 3 * D_MODEL // N_CHIPS
    pad_mod = lambda dm: jnp.pad(dm.reshape(N_CHIPS, ns), ((0, 0), (0, D_MODEL - ns)))
    packed = jnp.concatenate([
        small["dnorm_g"], small["dconv_b"], small["dln_g"], small["dln_b"], small["dq_norm"], small["dk_norm"],
        small["loss_cols"], pad_mod(small["dmod0"]), pad_mod(small["dmod1"]), small["dconv_w"],
        jnp.zeros((SMALL_ROWS - 20 - CONV_WIDTH, D_MODEL), F32)], axis=0)
    tot, g_ada_w, loss, qk = _reduce_small(packed, silu_c)
    cw = D_MODEL // N_CHIPS
    g_small = dict(
        norm_g=tot[0:2], a_conv_b=tot[2:3], a_ln_g=tot[3:4], a_ln_b=tot[4:5],
        b_q_norm=qk[0:3], b_k_norm=qk[3:6],
        ada_b=jnp.stack([tot[12:16, :ns].reshape(3 * D_MODEL), tot[16:20, :ns].reshape(3 * D_MODEL)]),
        a_conv_w=lax.dynamic_slice(tot[20:20 + CONV_WIDTH], (0, chip * cw), (CONV_WIDTH, cw)),
    )


    given = dict(norm_g=(norm_g, m_norm_g, v_norm_g), ada_w=(ada_w, m_ada_w, v_ada_w), ada_b=(ada_b, m_ada_b, v_ada_b),
                 a_w_in=(a_w_in, m_a_w_in, v_a_w_in), a_conv_w=(a_conv_w, m_a_conv_w, v_a_conv_w),
                 a_conv_b=(a_conv_b, m_a_conv_b, v_a_conv_b), a_ln_g=(a_ln_g, m_a_ln_g, v_a_ln_g),
                 a_ln_b=(a_ln_b, m_a_ln_b, v_a_ln_b), a_w_out=(a_w_out, m_a_w_out, v_a_w_out),
                 b_w_in=(b_w_in, m_b_w_in, v_b_w_in), b_q_norm=(b_q_norm, m_b_q_norm, v_b_q_norm),
                 b_k_norm=(b_k_norm, m_b_k_norm, v_b_k_norm), b_w_out=(b_w_out, m_b_w_out, v_b_w_out))
    order = ["norm_g", "ada_w", "ada_b", "a_w_in", "a_conv_w", "a_conv_b", "a_ln_g", "a_ln_b", "a_w_out", "b_w_in",
             "b_q_norm", "b_k_norm", "b_w_out"]
    outs = {}

    def update(k, g2):
        w, m, v = given[k]
        shape2 = g2.shape
        d2, m2, v2 = _adamw(w.reshape(shape2), g2, m.reshape(shape2), v.reshape(shape2), f"adamw_{k}")
        outs[k] = tuple(a.reshape(w.shape) for a in (g2, d2, m2, v2))

    token = forward_grads("a", tot)
    g_b_in, g_b_out = finish_grads("b", token)
    update("b_w_in", g_b_in)
    update("b_w_out", g_b_out)
    update("ada_w", g_ada_w.reshape(2 * D_MODEL, ns))
    for k, g2 in g_small.items():
        update(k, g2)
    g_a_in, g_a_out = finish_grads("a", outs["b_w_in"][1])
    update("a_w_in", g_a_in)
    update("a_w_out", g_a_out)
    return (loss.reshape(()), grad_x[None], *[outs[k][0] for k in order], *[outs[k][1] for k in order],
            *[outs[k][2] for k in order], *[outs[k][3] for k in order])
```

```python
import functools

import jax
import jax.numpy as jnp
from jax import lax
from jax.experimental import pallas as pl
from jax.experimental.pallas import tpu as pltpu

F32 = jnp.float32
BF16 = jnp.bfloat16

SEQ = 2048
D_MODEL = 1024
CONV_WIDTH = 31
HEAD_DIM = 64
N_HEADS = 16
DILATIONS = (1, 4, 16)
ATTN_BLOCK = 128
NORM_EPS = 1e-6
NEG_INF = -1e30
N_DEV = 8
N_CHIPS = 4

ADAM_LR = 0.001
ADAM_B1 = 0.9
ADAM_B2 = 0.999
ADAM_EPS = 1e-08
ADAM_WD = 0.01
ADAM_STEP = 10

VMEM_LIMIT_BYTES = 52 * 1024 * 1024
HALO = 32
LANES = 128
MESH = pl.DeviceIdType.MESH


def _params(*sem):
    return pltpu.CompilerParams(dimension_semantics=sem or None, vmem_limit_bytes=VMEM_LIMIT_BYTES)


def _sigmoid(v):
    return 1.0 / (1.0 + jnp.exp(-v))


def _row_spec(tm, cols, col_block=0):
    return pl.BlockSpec((tm, cols), lambda i: (i, col_block))


def _vec_spec(rows, cols):
    return pl.BlockSpec((rows, cols), lambda i: (0, 0))


def _normmod(xv, g, scale, shift):
    r = lax.rsqrt(jnp.mean(xv * xv, axis=-1, keepdims=True) + NORM_EPS)
    return xv * r * g * (1.0 + scale) + shift


def _normmod_fwd(x, g, scale, shift, name):
    tm = 256

    def body(x_ref, g_ref, sc_ref, sh_ref, h_ref, ht_ref):
        h = _normmod(x_ref[...], g_ref[...], sc_ref[...], sh_ref[...])
        h_ref[...] = h.astype(BF16)
        ht_ref[...] = h.T.astype(BF16)

    return pl.pallas_call(
        body, name=name, grid=(SEQ // tm,),
        in_specs=[_row_spec(tm, D_MODEL)] + [_vec_spec(1, D_MODEL)] * 3,
        out_specs=[_row_spec(tm, D_MODEL), pl.BlockSpec((D_MODEL, tm), lambda i: (0, i))],
        out_shape=[jax.ShapeDtypeStruct((SEQ, D_MODEL), BF16), jax.ShapeDtypeStruct((D_MODEL, SEQ), BF16)],
        compiler_params=_params("parallel"),
    )(x, g, scale, shift)


def _normmod_bwd(x, g, scale, dh_parts, dres, name, part_dilations=None):
    tm = 256
    n_parts = len(dh_parts)
    dils = part_dilations or (1,) * n_parts
    dh_parts = [p if d == 1 else p.reshape(d, SEQ // d, D_MODEL) for p, d in zip(dh_parts, dils)]

    def body(x_ref, g_ref, sc_ref, dres_ref, *rest):
        part_refs = rest[:n_parts]
        dx_ref, sums_ref, nat = rest[n_parts:]
        xv = x_ref[...]
        r = lax.rsqrt(jnp.mean(xv * xv, axis=-1, keepdims=True) + NORM_EPS)
        xn = xv * r
        dh = _load_natural(part_refs[0], nat, dils[0])
        for p, d in zip(part_refs[1:], dils[1:]):
            dh = dh + _load_natural(p, nat, d)
        gv = g_ref[...]
        one_sc = 1.0 + sc_ref[...]
        dxn = dh * (gv * one_sc)
        dx = r * (dxn - xn * jnp.mean(dxn * xn, axis=-1, keepdims=True))
        dx_ref[...] = dres_ref[...] + dx
        dhx = dh * xn
        sums = jnp.concatenate([
            jnp.sum(dhx, axis=0, keepdims=True) * one_sc,
            jnp.sum(dhx, axis=0, keepdims=True) * gv,
            jnp.sum(dh, axis=0, keepdims=True),
            jnp.zeros((5, D_MODEL), F32)], axis=0)

        @pl.when(pl.program_id(0) == 0)
        def _():
            sums_ref[...] = jnp.zeros_like(sums_ref)

        sums_ref[...] += sums

    return pl.pallas_call(
        body, name=name, grid=(SEQ // tm,),
        in_specs=[_row_spec(tm, D_MODEL), _vec_spec(1, D_MODEL), _vec_spec(1, D_MODEL), _row_spec(tm, D_MODEL)]
        + [_class_spec(tm, d) for d in dils],
        out_specs=[_row_spec(tm, D_MODEL), _vec_spec(8, D_MODEL)],
        out_shape=[jax.ShapeDtypeStruct((SEQ, D_MODEL), F32), jax.ShapeDtypeStruct((8, D_MODEL), F32)],
        scratch_shapes=[_natural_scratch(tm)],
        compiler_params=_params("arbitrary"),
    )(x, g, scale, dres, *dh_parts)


def _mm(lhs, rhs, *, tn, tile0, n_tiles, out_dtype, name, out3d=None, prev=None):
    mo, kc = lhs.shape
    cm = 512

    def body(l_ref, r_ref, *rest):
        o_ref = rest[-1]
        for m in range(mo // cm):
            rows = pl.ds(m * cm, cm)
            o_ref[rows, :] = jnp.dot(l_ref[rows, :], r_ref[...], preferred_element_type=F32).astype(out_dtype)

    if rhs.ndim == 3:
        tps_r = rhs.shape[2] // tn
        r_spec = pl.BlockSpec((None, kc, tn), lambda t: ((tile0 + t) // tps_r, 0, (tile0 + t) % tps_r))
    else:
        r_spec = pl.BlockSpec((kc, tn), lambda t: (0, t))
    in_specs = [pl.BlockSpec((mo, kc), lambda t: (0, 0)), r_spec]
    args = [lhs, rhs]
    aliases = {}
    if out3d is None:
        o_spec = pl.BlockSpec((mo, tn), lambda t: (0, t))
        o_shape = jax.ShapeDtypeStruct((mo, n_tiles * tn), out_dtype)
    else:
        j_out, ns_out = out3d
        tps_o = ns_out // tn
        o_spec = pl.BlockSpec((None, mo, tn), lambda t: ((tile0 + t) // tps_o, 0, (tile0 + t) % tps_o))
        o_shape = jax.ShapeDtypeStruct((j_out, mo, ns_out), out_dtype)
        if prev is not None:
            in_specs.append(pl.BlockSpec(memory_space=pl.ANY))
            args.append(prev)
            aliases = {2: 0}
    return pl.pallas_call(
        body, name=name, grid=(n_tiles,), in_specs=in_specs, out_specs=o_spec, out_shape=o_shape,
        input_output_aliases=aliases, compiler_params=_params("parallel"),
    )(*args)


def _mm_nt(dy, w3, *, tn, tile0, n_tiles, name, after=None):
    m_rows = dy.shape[0]
    _, kc, ns = w3.shape
    tps = ns // tn
    cm = 512
    extra = [] if after is None else [after]

    def body(dy_ref, w_ref, *rest):
        o_ref = rest[-1]

        @pl.when(pl.program_id(0) == 0)
        def _():
            o_ref[...] = jnp.zeros_like(o_ref)

        for m in range(m_rows // cm):
            rows = pl.ds(m * cm, cm)
            o_ref[rows, :] += lax.dot_general(dy_ref[rows, :], w_ref[...], (((1,), (1,)), ((), ())),
                                              preferred_element_type=F32)

    return pl.pallas_call(
        body, name=name, grid=(n_tiles,),
        in_specs=[pl.BlockSpec((m_rows, tn), lambda t: (0, t)),
                  pl.BlockSpec((None, kc, tn), lambda t: ((tile0 + t) // tps, 0, (tile0 + t) % tps))]
        + [pl.BlockSpec(memory_space=pl.ANY)] * len(extra),
        out_specs=pl.BlockSpec((m_rows, kc), lambda t: (0, 0)),
        out_shape=jax.ShapeDtypeStruct((m_rows, kc), F32),
        compiler_params=_params("arbitrary"),
    )(dy, w3, *extra)


CONV_CHUNK = 16


def _shift_copies(buf, shifted):
    rows = shifted.shape[1]
    for s in range(1, 8):
        shifted[s - 1] = buf[pl.ds(s, rows), :]


def _shifted_rows(buf, shifted, offset, r0):
    s = offset % 8
    if s == 0:
        return buf[pl.ds(r0 + offset, CONV_CHUNK), :]
    return shifted[s - 1, pl.ds(r0 + (offset - s), CONV_CHUNK), :]


def _conv_fwd(proj, conv_w, conv_b, ln_g, ln_b, name):
    tm = 256
    hb = tm // HALO

    def body(vg_ref, halo_ref, z_ref, w_ref, b_ref, g_ref, be_ref, u5_ref, u5t_ref, u2_ref, buf, shifted):
        i = pl.program_id(0)
        u1 = vg_ref[:, :D_MODEL] * _sigmoid(vg_ref[:, D_MODEL:])
        u1h = halo_ref[:, :D_MODEL] * _sigmoid(halo_ref[:, D_MODEL:])
        buf[pl.ds(0, HALO), :] = jnp.where(i > 0, u1h, 0.0)
        buf[pl.ds(HALO, tm), :] = u1
        _shift_copies(buf, shifted)

        def chunk(ci, carry):
            r0 = pl.multiple_of(ci * CONV_CHUNK, CONV_CHUNK)
            acc = jnp.broadcast_to(b_ref[...], (CONV_CHUNK, D_MODEL))
            for k in range(CONV_WIDTH):
                acc = acc + w_ref[k:k + 1, :] * _shifted_rows(buf, shifted, HALO - (CONV_WIDTH - 1) + k, r0)
            u2_ref[pl.ds(r0, CONV_CHUNK), :] = acc
            return carry

        lax.fori_loop(0, tm // CONV_CHUNK, chunk, 0)
        acc = u2_ref[...]
        mu = jnp.mean(acc, axis=-1, keepdims=True)
        xc = acc - mu
        rstd = lax.rsqrt(jnp.mean(xc * xc, axis=-1, keepdims=True) + NORM_EPS)
        u3 = xc * rstd * g_ref[...] + be_ref[...]
        zv = z_ref[...]
        u5 = u3 * _sigmoid(u3) * (zv * _sigmoid(zv))
        u5_ref[...] = u5.astype(BF16)
        u5t_ref[...] = u5.T.astype(BF16)

    return pl.pallas_call(
        body, name=name, grid=(SEQ // tm,),
        in_specs=[pl.BlockSpec((tm, 2 * D_MODEL), lambda i: (i, 0)),
                  pl.BlockSpec((HALO, 2 * D_MODEL), lambda i: (jnp.maximum(i * hb - 1, 0), 0)),
                  _row_spec(tm, D_MODEL, 2),
                  _vec_spec(CONV_WIDTH, D_MODEL)] + [_vec_spec(1, D_MODEL)] * 3,
        out_specs=[_row_spec(tm, D_MODEL), pl.BlockSpec((D_MODEL, tm), lambda i: (0, i)), _row_spec(tm, D_MODEL)],
        out_shape=[jax.ShapeDtypeStruct((SEQ, D_MODEL), BF16), jax.ShapeDtypeStruct((D_MODEL, SEQ), BF16),
                   jax.ShapeDtypeStruct((SEQ, D_MODEL), F32)],
        scratch_shapes=[pltpu.VMEM((HALO + tm, D_MODEL), F32), pltpu.VMEM((7, HALO + tm - 8, D_MODEL), F32)],
        compiler_params=_params("parallel"),
    )(proj, proj, proj, conv_w, conv_b, ln_g, ln_b)


def _conv_bwd_pointwise(du5, proj, u2, ln_g, ln_b, name):
    tm = 256

    def body(du5_ref, z_ref, u2_ref, g_ref, be_ref, du2_ref, dz_ref, sums_ref):
        u2v = u2_ref[...]
        mu = jnp.mean(u2v, axis=-1, keepdims=True)
        xc = u2v - mu
        rstd = lax.rsqrt(jnp.mean(xc * xc, axis=-1, keepdims=True) + NORM_EPS)
        xhat = xc * rstd
        u3 = xhat * g_ref[...] + be_ref[...]
        s3 = _sigmoid(u3)
        u4 = u3 * s3
        zv = z_ref[...]
        sz = _sigmoid(zv)
        du5v = du5_ref[...]
        dz_ref[...] = du5v * u4 * (sz * (1.0 + zv * (1.0 - sz)))
        du3 = du5v * (zv * sz) * (s3 * (1.0 + u3 * (1.0 - s3)))
        dxhat = du3 * g_ref[...]
        du2 = rstd * (dxhat - jnp.mean(dxhat, axis=-1, keepdims=True)
                      - xhat * jnp.mean(dxhat * xhat, axis=-1, keepdims=True))
        du2_ref[...] = du2
        sums = jnp.concatenate([
            jnp.sum(du3 * xhat, axis=0, keepdims=True),
            jnp.sum(du3, axis=0, keepdims=True),
            jnp.sum(du2, axis=0, keepdims=True),
            jnp.zeros((5, D_MODEL), F32)], axis=0)

        @pl.when(pl.program_id(0) == 0)
        def _():
            sums_ref[...] = jnp.zeros_like(sums_ref)

        sums_ref[...] += sums

    return pl.pallas_call(
        body, name=name, grid=(SEQ // tm,),
        in_specs=[_row_spec(tm, D_MODEL), _row_spec(tm, D_MODEL, 2), _row_spec(tm, D_MODEL),
                  _vec_spec(1, D_MODEL), _vec_spec(1, D_MODEL)],
        out_specs=[_row_spec(tm, D_MODEL), _row_spec(tm, D_MODEL), _vec_spec(8, D_MODEL)],
        out_shape=[jax.ShapeDtypeStruct((SEQ, D_MODEL), F32), jax.ShapeDtypeStruct((SEQ, D_MODEL), F32),
                   jax.ShapeDtypeStruct((8, D_MODEL), F32)],
        compiler_params=_params("arbitrary"),
    )(du5, proj, u2, ln_g, ln_b)


def _conv_bwd_taps(du2, dz, proj, conv_w, name):
    tm = 256
    hb = tm // HALO
    n_blocks = SEQ // tm

    def body(du2_ref, dnext_ref, dz_ref, vg_ref, halo_ref, w_ref, dproj_ref, dw_ref,
             ubuf, dbuf, ushift, dshift, sgbuf, dwacc):
        i = pl.program_id(0)
        sg = _sigmoid(vg_ref[:, D_MODEL:])
        sgbuf[...] = sg
        u1h = halo_ref[:, :D_MODEL] * _sigmoid(halo_ref[:, D_MODEL:])
        ubuf[pl.ds(0, HALO), :] = jnp.where(i > 0, u1h, 0.0)
        ubuf[pl.ds(HALO, tm), :] = vg_ref[:, :D_MODEL] * sg
        dbuf[pl.ds(0, tm), :] = du2_ref[...]
        dbuf[pl.ds(tm, HALO), :] = jnp.where(i < n_blocks - 1, dnext_ref[...], 0.0)
        _shift_copies(ubuf, ushift)
        _shift_copies(dbuf, dshift)

        @pl.when(i == 0)
        def _():
            dwacc[...] = jnp.zeros_like(dwacc)

        def chunk(ci, carry):
            r0 = pl.multiple_of(ci * CONV_CHUNK, CONV_CHUNK)
            rows = pl.ds(r0, CONV_CHUNK)
            du2c = du2_ref[rows, :]
            du1 = jnp.zeros((CONV_CHUNK, D_MODEL), F32)
            for k in range(CONV_WIDTH):
                du1 = du1 + w_ref[k:k + 1, :] * _shifted_rows(dbuf, dshift, CONV_WIDTH - 1 - k, r0)
                prod = du2c * _shifted_rows(ubuf, ushift, HALO - (CONV_WIDTH - 1) + k, r0)
                dwacc[k] += prod[0:8] + prod[8:16]
            sgc = sgbuf[rows, :]
            dval = du1 * sgc
            dproj_ref[rows, 0:D_MODEL] = dval.astype(BF16)
            dproj_ref[rows, D_MODEL:2 * D_MODEL] = (dval * vg_ref[rows, 0:D_MODEL] * (1.0 - sgc)).astype(BF16)
            return carry

        lax.fori_loop(0, tm // CONV_CHUNK, chunk, 0)
        dproj_ref[:, 2 * D_MODEL:] = dz_ref[...].astype(BF16)

        @pl.when(i == n_blocks - 1)
        def _():
            for k in range(CONV_WIDTH):
                dw_ref[k:k + 1, :] = jnp.sum(dwacc[k], axis=0, keepdims=True)
            dw_ref[CONV_WIDTH:, :] = jnp.zeros((32 - CONV_WIDTH, D_MODEL), F32)

    return pl.pallas_call(
        body, name=name, grid=(n_blocks,),
        in_specs=[_row_spec(tm, D_MODEL),
                  pl.BlockSpec((HALO, D_MODEL), lambda i: (jnp.minimum((i + 1) * hb, SEQ // HALO - 1), 0)),
                  _row_spec(tm, D_MODEL),
                  pl.BlockSpec((tm, 2 * D_MODEL), lambda i: (i, 0)),
                  pl.BlockSpec((HALO, 2 * D_MODEL), lambda i: (jnp.maximum(i * hb - 1, 0), 0)),
                  _vec_spec(CONV_WIDTH, D_MODEL)],
        out_specs=[_row_spec(tm, 3 * D_MODEL), _vec_spec(32, D_MODEL)],
        out_shape=[jax.ShapeDtypeStruct((SEQ, 3 * D_MODEL), BF16), jax.ShapeDtypeStruct((32, D_MODEL), F32)],
        scratch_shapes=[pltpu.VMEM((HALO + tm, D_MODEL), F32), pltpu.VMEM((tm + HALO, D_MODEL), F32),
                        pltpu.VMEM((7, HALO + tm - 8, D_MODEL), F32), pltpu.VMEM((7, HALO + tm - 8, D_MODEL), F32),
                        pltpu.VMEM((tm, D_MODEL), F32), pltpu.VMEM((CONV_WIDTH, 8, D_MODEL), F32)],
        compiler_params=_params("arbitrary"),
    )(du2, du2, dz, proj, proj, conv_w)


def _out_a(u5, w_out, x, gate, g1, scale1, shift1, name):
    tm = 256
    n_d = len(DILATIONS)

    def body(u_ref, w_ref, x_ref, gate_ref, g_ref, sc_ref, sh_ref, x1_ref, y_ref, ht_ref, *rest):
        h_refs, nat = rest[:n_d], rest[-1]
        y = jnp.dot(u_ref[...], w_ref[...], preferred_element_type=F32)
        x1 = x_ref[...] + gate_ref[...] * y
        y_ref[...] = y
        x1_ref[...] = x1
        h = _normmod(x1, g_ref[...], sc_ref[...], sh_ref[...])
        ht_ref[...] = h.T.astype(BF16)
        for h_ref, d in zip(h_refs, DILATIONS):
            _store_classes(h_ref, h, nat, d)

    res = pl.pallas_call(
        body, name=name, grid=(SEQ // tm,),
        in_specs=[_row_spec(tm, D_MODEL), _vec_spec(D_MODEL, D_MODEL), _row_spec(tm, D_MODEL)]
        + [_vec_spec(1, D_MODEL)] * 4,
        out_specs=[_row_spec(tm, D_MODEL), _row_spec(tm, D_MODEL), pl.BlockSpec((D_MODEL, tm), lambda i: (0, i))]
        + [_class_spec(tm, d) for d in DILATIONS],
        out_shape=[jax.ShapeDtypeStruct((SEQ, D_MODEL), F32), jax.ShapeDtypeStruct((SEQ, D_MODEL), F32),
                   jax.ShapeDtypeStruct((D_MODEL, SEQ), BF16)] + [_class_shape(d, BF16) for d in DILATIONS],
        scratch_shapes=[_natural_scratch(tm)],
        compiler_params=_params("parallel"),
    )(u5, w_out, x, gate, g1, scale1, shift1)
    return res[0], res[1], res[2], [a.reshape(SEQ, D_MODEL) for a in res[3:]]


def _out_b_loss(u, w_out, x1, gate, target, name):
    tm = 256

    def body(u_ref, w_ref, x_ref, gate_ref, t_ref, e_ref, dy_ref, sums_ref):
        y = jnp.dot(u_ref[...], w_ref[...], preferred_element_type=F32)
        diff = x_ref[...] + gate_ref[...] * y - t_ref[...]
        e = diff * (1.0 / D_MODEL)
        e_ref[...] = e
        dy_ref[...] = (e * gate_ref[...]).astype(BF16)
        sums = jnp.concatenate([
            jnp.sum(e * y, axis=0, keepdims=True),
            jnp.sum(diff * diff, axis=0, keepdims=True),
            jnp.zeros((6, D_MODEL), F32)], axis=0)

        @pl.when(pl.program_id(0) == 0)
        def _():
            sums_ref[...] = jnp.zeros_like(sums_ref)

        sums_ref[...] += sums

    return pl.pallas_call(
        body, name=name, grid=(SEQ // tm,),
        in_specs=[_row_spec(tm, D_MODEL), _vec_spec(D_MODEL, D_MODEL), _row_spec(tm, D_MODEL),
                  _vec_spec(1, D_MODEL), _row_spec(tm, D_MODEL)],
        out_specs=[_row_spec(tm, D_MODEL), _row_spec(tm, D_MODEL), _vec_spec(8, D_MODEL)],
        out_shape=[jax.ShapeDtypeStruct((SEQ, D_MODEL), F32), jax.ShapeDtypeStruct((SEQ, D_MODEL), BF16),
                   jax.ShapeDtypeStruct((8, D_MODEL), F32)],
        compiler_params=_params("arbitrary"),
    )(u, w_out, x1, gate, target)


def _dgate_dy(dx1, y, gate, name):
    tm = 256

    def body(d_ref, y_ref, gate_ref, dy_ref, sums_ref):
        dv = d_ref[...]
        dy_ref[...] = (dv * gate_ref[...]).astype(BF16)
        sums = jnp.concatenate([jnp.sum(dv * y_ref[...], axis=0, keepdims=True), jnp.zeros((7, D_MODEL), F32)], axis=0)

        @pl.when(pl.program_id(0) == 0)
        def _():
            sums_ref[...] = jnp.zeros_like(sums_ref)

        sums_ref[...] += sums

    return pl.pallas_call(
        body, name=name, grid=(SEQ // tm,),
        in_specs=[_row_spec(tm, D_MODEL), _row_spec(tm, D_MODEL), _vec_spec(1, D_MODEL)],
        out_specs=[_row_spec(tm, D_MODEL), _vec_spec(8, D_MODEL)],
        out_shape=[jax.ShapeDtypeStruct((SEQ, D_MODEL), BF16), jax.ShapeDtypeStruct((8, D_MODEL), F32)],
        compiler_params=_params("arbitrary"),
    )(dx1, y, gate)


def _mm_nt_res(dy, w, name):
    tm = 256
    kc, n = w.shape

    def body(dy_ref, w_ref, o_ref):
        o_ref[...] = lax.dot_general(dy_ref[...], w_ref[...], (((1,), (1,)), ((), ())), preferred_element_type=F32)

    return pl.pallas_call(
        body, name=name, grid=(SEQ // tm,),
        in_specs=[_row_spec(tm, n), _vec_spec(kc, n)],
        out_specs=_row_spec(tm, kc),
        out_shape=jax.ShapeDtypeStruct((SEQ, kc), F32),
        compiler_params=_params("parallel"),
    )(dy, w)


def _seg_matrix():
    r = lax.broadcasted_iota(jnp.int32, (256, 256), 0) // HEAD_DIM
    c = lax.broadcasted_iota(jnp.int32, (256, 256), 1) // HEAD_DIM
    return (r == c).astype(BF16)


def _segsum(v, seg):
    hi = v.astype(BF16)
    lo = (v - hi.astype(F32)).astype(BF16)
    outs = []
    for c0 in range(0, D_MODEL, 256):
        outs.append(jnp.dot(hi[:, c0:c0 + 256], seg, preferred_element_type=F32)
                    + jnp.dot(lo[:, c0:c0 + 256], seg, preferred_element_type=F32))
    return jnp.concatenate(outs, axis=1)


def _qk_rstd(v, seg):
    return lax.rsqrt(_segsum(v * v, seg) * (1.0 / HEAD_DIM) + NORM_EPS)


def _qknorm_fwd(proj, qw, kw, seg, name):
    tm = 256

    def body(p_ref, qw_ref, kw_ref, seg_ref, q_ref, k_ref, v_ref):
        segv = seg_ref[...]
        q = p_ref[:, :D_MODEL]
        k = p_ref[:, D_MODEL:2 * D_MODEL]
        q_ref[...] = (q * _qk_rstd(q, segv) * qw_ref[...]).astype(BF16)
        k_ref[...] = (k * _qk_rstd(k, segv) * kw_ref[...]).astype(BF16)
        v_ref[...] = p_ref[:, 2 * D_MODEL:].astype(BF16)

    return pl.pallas_call(
        body, name=name, grid=(SEQ // tm,),
        in_specs=[_row_spec(tm, 3 * D_MODEL), _vec_spec(1, D_MODEL), _vec_spec(1, D_MODEL), _vec_spec(256, 256)],
        out_specs=[_row_spec(tm, D_MODEL)] * 3,
        out_shape=[jax.ShapeDtypeStruct((SEQ, D_MODEL), BF16)] * 3,
        compiler_params=_params("parallel"),
    )(proj, qw, kw, seg)


def _attn_masks(b, bpc, dilation, slope):
    if bpc == 1:
        qi = lax.broadcasted_iota(jnp.int32, (ATTN_BLOCK, ATTN_BLOCK), 0)
        kj = lax.broadcasted_iota(jnp.int32, (ATTN_BLOCK, ATTN_BLOCK), 1)
        steps = qi - kj
        return (steps * dilation).astype(F32), steps >= 0
    qi = lax.broadcasted_iota(jnp.int32, (ATTN_BLOCK, 2 * ATTN_BLOCK), 0)
    kj = lax.broadcasted_iota(jnp.int32, (ATTN_BLOCK, 2 * ATTN_BLOCK), 1)
    steps = qi + ATTN_BLOCK - kj
    has_prev = (b % bpc) != 0
    valid = (steps >= 0) & (steps <= ATTN_BLOCK) & (has_prev | (kj >= ATTN_BLOCK))
    return (steps * dilation).astype(F32), valid


def _key_tile(prev_ref, cur_ref, cols, bpc):
    if bpc == 1:
        return cur_ref[:, cols]
    return jnp.concatenate([prev_ref[:, cols], cur_ref[:, cols]], axis=0)


ATTN_HEADS_FWD = 8
ATTN_HEADS_BWD = 4
NT_DIMS = (((1,), (1,)), ((), ()))
TN_DIMS = (((0,), (0,)), ((), ()))
BATCH_NT_DIMS = (((2,), (2,)), ((0,), (0,)))
BATCH_NN_DIMS = (((2,), (1,)), ((0,), (0,)))
BATCH_TN_DIMS = (((1,), (1,)), ((0,), (0,)))


def _head_stack(tile_of, heads):
    return jnp.stack([tile_of(slice(h * HEAD_DIM, (h + 1) * HEAD_DIM)) for h in range(heads)], axis=0)


def _attn_specs(heads):
    cur = pl.BlockSpec((ATTN_BLOCK, heads * HEAD_DIM), lambda hg, b: (b, hg))
    prev = pl.BlockSpec((ATTN_BLOCK, heads * HEAD_DIM), lambda hg, b: (jnp.maximum(b - 1, 0), hg))
    return cur, prev


def _attn_fwd(q, k, v, slopes, dilation, name):
    bpc = SEQ // dilation // ATTN_BLOCK
    heads = ATTN_HEADS_FWD
    cur, prev = _attn_specs(heads)
    scale = HEAD_DIM ** -0.5

    def body(sl_ref, q_ref, kp_ref, kc_ref, vp_ref, vc_ref, o_ref, lse_ref):
        dist, valid = _attn_masks(pl.program_id(1), bpc, dilation, None)
        q3 = _head_stack(lambda cols: q_ref[:, cols], heads)
        k3 = _head_stack(lambda cols: _key_tile(kp_ref, kc_ref, cols, bpc), heads)
        v3 = _head_stack(lambda cols: _key_tile(vp_ref, vc_ref, cols, bpc), heads)
        s = lax.dot_general(q3, k3, BATCH_NT_DIMS, preferred_element_type=F32)
        s = jnp.where(valid[None], s * scale - dist[None] * sl_ref[...], NEG_INF)
        m = jnp.max(s, axis=-1, keepdims=True)
        p = jnp.exp(s - m)
        l = jnp.sum(p, axis=-1, keepdims=True)
        o3 = lax.dot_general(p.astype(BF16), v3, BATCH_NN_DIMS, preferred_element_type=F32) / l
        lse3 = m + jnp.log(l)
        for h in range(heads):
            cols = slice(h * HEAD_DIM, (h + 1) * HEAD_DIM)
            o_ref[:, cols] = o3[h]
            lse_ref[:, cols] = jnp.broadcast_to(lse3[h], (ATTN_BLOCK, HEAD_DIM))

    return pl.pallas_call(
        body, name=name, grid=(N_HEADS // heads, SEQ // ATTN_BLOCK),
        in_specs=[pl.BlockSpec((heads, 1, 1), lambda hg, b: (hg, 0, 0)), cur, prev, cur, prev, cur],
        out_specs=[cur, cur],
        out_shape=[jax.ShapeDtypeStruct((SEQ, D_MODEL), F32)] * 2,
        compiler_params=_params("parallel", "parallel"),
    )(slopes.reshape(N_HEADS, 1, 1), q, k, k, v, v)


def _class_spec(tm, dilation):
    if dilation == 1:
        return _row_spec(tm, D_MODEL)
    return pl.BlockSpec((dilation, tm // dilation, D_MODEL), lambda i: (0, i, 0))


def _class_shape(dilation, dtype):
    if dilation == 1:
        return jax.ShapeDtypeStruct((SEQ, D_MODEL), dtype)
    return jax.ShapeDtypeStruct((dilation, SEQ // dilation, D_MODEL), dtype)


def _load_natural(in_ref, nat_ref, dilation):
    if dilation == 1:
        return in_ref[...].astype(F32)
    n = nat_ref.shape[1] // dilation
    for r in range(dilation):
        for j in range(D_MODEL // LANES):
            nat_ref[j, pl.ds(r, n, stride=dilation), :] = in_ref[r, :, j * LANES:(j + 1) * LANES].astype(F32)
    return jnp.concatenate([nat_ref[j] for j in range(D_MODEL // LANES)], axis=1)


def _store_classes(out_ref, value, nat_ref, dilation):
    if dilation == 1:
        out_ref[...] = value.astype(out_ref.dtype)
        return
    n = nat_ref.shape[1] // dilation
    for j in range(D_MODEL // LANES):
        nat_ref[j] = value[:, j * LANES:(j + 1) * LANES]
    for r in range(dilation):
        for j in range(D_MODEL // LANES):
            out_ref[r, :, j * LANES:(j + 1) * LANES] = (
                nat_ref[j, pl.ds(r, n, stride=dilation), :].astype(out_ref.dtype))


def _natural_scratch(tm):
    return pltpu.VMEM((D_MODEL // LANES, tm, LANES), F32)


def _merge_fwd(o_parts, lse_parts, z, name):
    tm = 256

    def body(o0, o1, o2, l0, l1, l2, z_ref, u_ref, ut_ref, o_ref, lse_ref, nat):
        ls = [_load_natural(l, nat, d) for l, d in zip((l0, l1, l2), DILATIONS)]
        m = jnp.maximum(jnp.maximum(ls[0], ls[1]), ls[2])
        tot = m + jnp.log(jnp.exp(ls[0] - m) + jnp.exp(ls[1] - m) + jnp.exp(ls[2] - m))
        o = jnp.zeros((tm, D_MODEL), F32)
        for o_in, l, d in zip((o0, o1, o2), ls, DILATIONS):
            o = o + jnp.exp(l - tot) * _load_natural(o_in, nat, d)
        zv = z_ref[...]
        u = o * (zv * _sigmoid(zv))
        u_ref[...] = u.astype(BF16)
        ut_ref[...] = u.T.astype(BF16)
        o_ref[...] = o
        lse_ref[...] = tot

    return pl.pallas_call(
        body, name=name, grid=(SEQ // tm,),
        in_specs=[_class_spec(tm, d) for d in DILATIONS] * 2 + [_row_spec(tm, D_MODEL)],
        out_specs=[_row_spec(tm, D_MODEL), pl.BlockSpec((D_MODEL, tm), lambda i: (0, i)),
                   _row_spec(tm, D_MODEL), _row_spec(tm, D_MODEL)],
        out_shape=[jax.ShapeDtypeStruct((SEQ, D_MODEL), BF16), jax.ShapeDtypeStruct((D_MODEL, SEQ), BF16),
                   jax.ShapeDtypeStruct((SEQ, D_MODEL), F32), jax.ShapeDtypeStruct((SEQ, D_MODEL), F32)],
        scratch_shapes=[_natural_scratch(tm)],
        compiler_params=_params("parallel"),
    )(*o_parts, *lse_parts, z)


def _merge_bwd(du, o, lse, z, seg, name):
    tm = 256
    n_d = len(DILATIONS)
    assert DILATIONS[0] == 1

    def body(du_ref, o_ref, lse_ref, z_ref, seg_ref, dz_ref, *rest):
        do_refs, delta_refs, lse_refs = rest[:n_d], rest[n_d:2 * n_d], rest[2 * n_d:3 * n_d - 1]
        nat = rest[-1]
        zv = z_ref[...]
        sz = _sigmoid(zv)
        duv = du_ref[...]
        ov = o_ref[...]
        do = duv * (zv * sz)
        dz_ref[...] = (duv * ov * (sz * (1.0 + zv * (1.0 - sz)))).astype(BF16)
        delta = _segsum(do * ov, seg_ref[...])
        lv = lse_ref[...]
        for i, d in enumerate(DILATIONS):
            _store_classes(do_refs[i], do, nat, d)
            _store_classes(delta_refs[i], delta, nat, d)
            if i > 0:
                _store_classes(lse_refs[i - 1], lv, nat, d)

    res = pl.pallas_call(
        body, name=name, grid=(SEQ // tm,),
        in_specs=[_row_spec(tm, D_MODEL)] * 4 + [_vec_spec(256, 256)],
        out_specs=[_row_spec(tm, D_MODEL)] + [_class_spec(tm, d) for d in DILATIONS] * 2
        + [_class_spec(tm, d) for d in DILATIONS[1:]],
        out_shape=[jax.ShapeDtypeStruct((SEQ, D_MODEL), BF16)] + [_class_shape(d, BF16) for d in DILATIONS]
        + [_class_shape(d, F32) for d in DILATIONS] + [_class_shape(d, F32) for d in DILATIONS[1:]],
        scratch_shapes=[_natural_scratch(tm)],
        compiler_params=_params("parallel"),
    )(du, o, lse, z, seg)
    flat = lambda a: a.reshape(SEQ, D_MODEL)
    dz, dos, deltas, lses = res[0], res[1:1 + n_d], res[1 + n_d:1 + 2 * n_d], [lse] + list(res[1 + 2 * n_d:])
    return dz, [flat(a) for a in dos], [flat(a) for a in deltas], [flat(a) for a in lses]


def _attn_bwd(q, k, v, do, lse, delta, slopes, dilation, name):
    bpc = SEQ // dilation // ATTN_BLOCK
    heads = ATTN_HEADS_BWD
    cur, prev = _attn_specs(heads)
    scale = HEAD_DIM ** -0.5

    def body(sl_ref, q_ref, kp_ref, kc_ref, vp_ref, vc_ref, do_ref, lse_ref, dl_ref,
             dq_ref, dkc_ref, dkp_ref, dvc_ref, dvp_ref):
        dist, valid = _attn_masks(pl.program_id(1), bpc, dilation, None)
        q3 = _head_stack(lambda cols: q_ref[:, cols], heads)
        k3 = _head_stack(lambda cols: _key_tile(kp_ref, kc_ref, cols, bpc), heads)
        v3 = _head_stack(lambda cols: _key_tile(vp_ref, vc_ref, cols, bpc), heads)
        do3 = _head_stack(lambda cols: do_ref[:, cols], heads)
        lse3 = _head_stack(lambda cols: lse_ref[:, cols.start:cols.start + 1], heads)
        dl3 = _head_stack(lambda cols: dl_ref[:, cols.start:cols.start + 1], heads)
        s = lax.dot_general(q3, k3, BATCH_NT_DIMS, preferred_element_type=F32)
        p = jnp.exp(jnp.where(valid[None], s * scale - dist[None] * sl_ref[...], NEG_INF) - lse3)
        dp = lax.dot_general(do3, v3, BATCH_NT_DIMS, preferred_element_type=F32)
        ds = (p * (dp - dl3) * scale).astype(BF16)
        dq3 = lax.dot_general(ds, k3, BATCH_NN_DIMS, preferred_element_type=F32)
        dk3 = lax.dot_general(ds, q3, BATCH_TN_DIMS, preferred_element_type=F32)
        dv3 = lax.dot_general(p.astype(BF16), do3, BATCH_TN_DIMS, preferred_element_type=F32)
        for h in range(heads):
            cols = slice(h * HEAD_DIM, (h + 1) * HEAD_DIM)
            dq_ref[:, cols] = dq3[h]
            if bpc == 1:
                zeros = jnp.zeros((ATTN_BLOCK, HEAD_DIM), F32)
                dkp_ref[:, cols], dkc_ref[:, cols] = zeros, dk3[h]
                dvp_ref[:, cols], dvc_ref[:, cols] = zeros, dv3[h]
            else:
                dkp_ref[:, cols], dkc_ref[:, cols] = dk3[h, :ATTN_BLOCK], dk3[h, ATTN_BLOCK:]
                dvp_ref[:, cols], dvc_ref[:, cols] = dv3[h, :ATTN_BLOCK], dv3[h, ATTN_BLOCK:]

    return pl.pallas_call(
        body, name=name, grid=(N_HEADS // heads, SEQ // ATTN_BLOCK),
        in_specs=[pl.BlockSpec((heads, 1, 1), lambda hg, b: (hg, 0, 0)), cur, prev, cur, prev, cur, cur, cur, cur],
        out_specs=[cur] * 5,
        out_shape=[jax.ShapeDtypeStruct((SEQ, D_MODEL), F32)] * 5,
        compiler_params=_params("parallel", "parallel"),
    )(slopes.reshape(N_HEADS, 1, 1), q, k, k, v, v, do, lse, delta)


def _qknorm_bwd(proj, qw, kw, seg, dq, dkc, dkp, dvc, dvp, name):
    tm = ATTN_BLOCK
    n_blocks = SEQ // tm
    nxt = pl.BlockSpec((tm, D_MODEL), lambda i: (jnp.minimum(i + 1, n_blocks - 1), 0))

    def body(p_ref, qw_ref, kw_ref, seg_ref, dq_ref, dkc_ref, dkp_ref, dvc_ref, dvp_ref, dproj_ref, sums_ref):
        i = pl.program_id(0)
        segv = seg_ref[...]
        has_next = i < n_blocks - 1
        dk = dkc_ref[...] + jnp.where(has_next, dkp_ref[...], 0.0)
        dv = dvc_ref[...] + jnp.where(has_next, dvp_ref[...], 0.0)
        sums = []
        for part, (raw, w, dn) in enumerate(((p_ref[:, :D_MODEL], qw_ref[...], dq_ref[...]),
                                             (p_ref[:, D_MODEL:2 * D_MODEL], kw_ref[...], dk))):
            r = _qk_rstd(raw, segv)
            gq = dn * w
            draw = r * gq - raw * (r * r * r) * (_segsum(raw * gq, segv) * (1.0 / HEAD_DIM))
            dproj_ref[:, part * D_MODEL:(part + 1) * D_MODEL] = draw.astype(BF16)
            sums.append(jnp.sum(dn * raw * r, axis=0, keepdims=True))
        dproj_ref[:, 2 * D_MODEL:] = dv.astype(BF16)

        @pl.when(i == 0)
        def _():
            sums_ref[...] = jnp.zeros_like(sums_ref)

        sums_ref[...] += jnp.concatenate(sums + [jnp.zeros((6, D_MODEL), F32)], axis=0)

    return pl.pallas_call(
        body, name=name, grid=(n_blocks,),
        in_specs=[_row_spec(tm, 3 * D_MODEL), _vec_spec(1, D_MODEL), _vec_spec(1, D_MODEL), _vec_spec(256, 256),
                  _row_spec(tm, D_MODEL), _row_spec(tm, D_MODEL), nxt, _row_spec(tm, D_MODEL), nxt],
        out_specs=[_row_spec(tm, 3 * D_MODEL), _vec_spec(8, D_MODEL)],
        out_shape=[jax.ShapeDtypeStruct((SEQ, 3 * D_MODEL), BF16), jax.ShapeDtypeStruct((8, D_MODEL), F32)],
        compiler_params=_params("arbitrary"),
    )(proj, qw, kw, seg, dq, dkc, dkp, dvc, dvp)


def _to_classes(a, dilation):
    if dilation == 1:
        return a
    s, c = a.shape
    return a.reshape(s // dilation, dilation, c).transpose(1, 0, 2).reshape(s, c)


def _from_classes(a, dilation):
    if dilation == 1:
        return a
    s, c = a.shape
    return a.reshape(dilation, s // dilation, c).transpose(1, 0, 2).reshape(s, c)


def _cols_to_classes(a, dilation):
    if dilation == 1:
        return a
    r, s = a.shape
    return a.reshape(r, s // dilation, dilation).transpose(0, 2, 1).reshape(r, s)


B_TN = 512
B_GROUP_TILES = 3 * D_MODEL // B_TN
B_Z_TILE0 = 3 * B_GROUP_TILES
B_Z_TILES = D_MODEL // B_TN


def _local_step(x, target, mods, norm_g, conv_w, conv_b, ln_g, ln_b, q_norm, k_norm,
                weights_a, weights_b, forward_weights_b, send_grads_b, forward_grads_b, send_grads_a):
    row = lambda a, i: a[i:i + 1]
    shift0, scale0, gate0 = row(mods[0], 0), row(mods[0], 1), row(mods[0], 2)
    shift1, scale1, gate1 = row(mods[1], 0), row(mods[1], 1), row(mods[1], 2)
    g0, g1 = row(norm_g, 0), row(norm_g, 1)
    seg = _seg_matrix()
    slopes = jnp.exp2(-8.0 * jnp.arange(1, N_HEADS + 1, dtype=F32) / N_HEADS)
    qw = [jnp.tile(q_norm[g:g + 1], (1, N_HEADS)) for g in range(3)]
    kw = [jnp.tile(k_norm[g:g + 1], (1, N_HEADS)) for g in range(3)]

    h0, h0t = _normmod_fwd(x, g0, scale0, shift0, "prenorm0")
    wa_in, wa_out = weights_a(h0)
    ja, _, nsa = wa_in.shape
    proj_a = _mm(h0, wa_in, tn=nsa, tile0=0, n_tiles=ja, out_dtype=F32, name="a_in")
    u5, u5t, u2 = _conv_fwd(proj_a, conv_w, conv_b, ln_g, ln_b, "a_conv")
    token = forward_weights_b(u5)
    x1, y_a, h1t, h1c = _out_a(u5, wa_out, x, gate0 + token[0:1, 0:1], g1, scale1, shift1, "a_out")

    wb_in, wb_out = weights_b(x1)
    jb, _, nsb = wb_in.shape
    h1 = h1c[0]
    h1tc = [_cols_to_classes(h1t, d) for d in DILATIONS]
    z_b = _mm(h1, wb_in, tn=B_TN, tile0=B_Z_TILE0, n_tiles=B_Z_TILES, out_dtype=F32, name="b_in_z")
    proj_g, qkv, o_parts, lse_parts = [], [], [], []
    for g, d in enumerate(DILATIONS):
        pg = _mm(h1c[g], wb_in, tn=B_TN, tile0=g * B_GROUP_TILES, n_tiles=B_GROUP_TILES, out_dtype=F32,
                 name=f"b_in_g{g}")
        qn, kn, vn = _qknorm_fwd(pg, qw[g], kw[g], seg, f"b_qknorm_g{g}")
        og, lg = _attn_fwd(qn, kn, vn, slopes, d, f"b_attn_g{g}")
        proj_g.append(pg)
        qkv.append((qn, kn, vn))
        classes = (lambda a: a) if d == 1 else (lambda a: a.reshape(d, SEQ // d, D_MODEL))
        o_parts.append(classes(og))
        lse_parts.append(classes(lg))
    u_b, u_bt, o_b, lse_b = _merge_fwd(o_parts, lse_parts, z_b, "b_merge")
    e, dy_b, sums_loss = _out_b_loss(u_b, wb_out, x1, gate1, target, "b_out_loss")

    dwb_out = _mm(u_bt, dy_b, tn=D_MODEL, tile0=0, n_tiles=1, out_dtype=BF16, name="b_dwout")
    du_b = _mm_nt_res(dy_b, wb_out, "b_dout")
    dz_b, do_c, delta_c, lse_c = _merge_bwd(du_b, o_b, lse_b, z_b, seg, "b_merge_bwd")
    dwb_in = _mm(h1t, dz_b, tn=B_TN, tile0=B_Z_TILE0, n_tiles=B_Z_TILES, out_dtype=BF16, name="b_dwin_z",
                 out3d=(jb, nsb))
    dh1_parts = [_mm_nt(dz_b, wb_in, tn=B_TN, tile0=B_Z_TILE0, n_tiles=B_Z_TILES, name="b_dh_z")]
    qk_sums = []
    for g, d in enumerate(DILATIONS):
        qn, kn, vn = qkv[g]
        dq, dkc, dkp, dvc, dvp = _attn_bwd(qn, kn, vn, do_c[g], lse_c[g], delta_c[g], slopes, d,
                                           f"b_attn_bwd_g{g}")
        dproj, sums_qk = _qknorm_bwd(proj_g[g], qw[g], kw[g], seg, dq, dkc, dkp, dvc, dvp, f"b_qknorm_bwd_g{g}")
        qk_sums.append(sums_qk)
        dwb_in = _mm(h1tc[g], dproj, tn=B_TN, tile0=g * B_GROUP_TILES, n_tiles=B_GROUP_TILES, out_dtype=BF16,
                     name=f"b_dwin_g{g}", out3d=(jb, nsb), prev=dwb_in)
        dh = _mm_nt(dproj, wb_in, tn=B_TN, tile0=g * B_GROUP_TILES, n_tiles=B_GROUP_TILES, name=f"b_dh_g{g}")
        dh1_parts.append(dh)
    token = send_grads_b(dwb_in, dwb_out)
    dx1, sums_n1 = _normmod_bwd(x1, g1, scale1 + token[0:1, 0:1], dh1_parts, e, "prenorm1_bwd",
                                part_dilations=(1,) + DILATIONS)
    token = forward_grads_b(dx1)

    dy_a, sums_ga = _dgate_dy(dx1, y_a, gate0 + token[0:1, 0:1], "a_dgate")
    dwa_out = _mm(u5t, dy_a, tn=D_MODEL, tile0=0, n_tiles=1, out_dtype=BF16, name="a_dwout")
    du5 = _mm_nt_res(dy_a, wa_out, "a_dout")
    du2, dz_a, sums_ln = _conv_bwd_pointwise(du5, proj_a, u2, ln_g, ln_b, "a_conv_bwd_pw")
    dproj_a, dconv_w = _conv_bwd_taps(du2, dz_a, proj_a, conv_w, "a_conv_bwd_taps")
    dwa_in = _mm(h0t, dproj_a, tn=nsa, tile0=0, n_tiles=ja, out_dtype=BF16, name="a_dwin", out3d=(ja, nsa))
    token = send_grads_a(dwa_in, dwa_out)
    dh0 = _mm_nt(dproj_a, wa_in, tn=nsa, tile0=0, n_tiles=ja, name="a_dh", after=token)
    grad_x, sums_n0 = _normmod_bwd(x, g0, scale0, [dh0], dx1, "prenorm0_bwd")

    small = dict(
        dnorm_g=jnp.concatenate([sums_n0[0:1], sums_n1[0:1]], axis=0),
        dmod0=jnp.concatenate([sums_n0[2:3], sums_n0[1:2], sums_ga[0:1]], axis=0),
        dmod1=jnp.concatenate([sums_n1[2:3], sums_n1[1:2], sums_loss[0:1]], axis=0),
        dln_g=sums_ln[0:1], dln_b=sums_ln[1:2], dconv_b=sums_ln[2:3],
        dconv_w=dconv_w[:CONV_WIDTH],
        dq_norm=jnp.concatenate([s[0:1] for s in qk_sums], axis=0),
        dk_norm=jnp.concatenate([s[1:2] for s in qk_sums], axis=0),
        loss_cols=sums_loss[1:2],
    )
    return grad_x, small


def _adamw(w, g, m, v, name):
    rows, cols = w.shape
    tr = rows if rows <= 128 else 128
    c1 = 1.0 / (1.0 - ADAM_B1 ** ADAM_STEP)
    c2 = 1.0 / (1.0 - ADAM_B2 ** ADAM_STEP)

    def body(w_ref, g_ref, m_ref, v_ref, d_ref, mo_ref, vo_ref):
        gv = g_ref[...]
        mn = ADAM_B1 * m_ref[...] + (1.0 - ADAM_B1) * gv
        vn = ADAM_B2 * v_ref[...] + (1.0 - ADAM_B2) * (gv * gv)
        mo_ref[...] = mn
        vo_ref[...] = vn
        d_ref[...] = -ADAM_LR * ((mn * c1) / (jnp.sqrt(vn * c2) + ADAM_EPS) + ADAM_WD * w_ref[...])

    spec = pl.BlockSpec((tr, cols), lambda i: (i, 0))
    return pl.pallas_call(
        body, name=name, grid=(rows // tr,), in_specs=[spec] * 4, out_specs=[spec] * 3,
        out_shape=[jax.ShapeDtypeStruct((rows, cols), F32)] * 3,
        compiler_params=_params("parallel"),
    )(w, g, m, v)


def _cast_into_slot(w, chip_idx, name):
    rows, cols = w.shape
    tr = 256

    def body(ch_ref, w_ref, o_ref):
        o_ref[...] = w_ref[...].astype(BF16)

    return pl.pallas_call(
        body, name=name,
        grid_spec=pltpu.PrefetchScalarGridSpec(
            num_scalar_prefetch=1, grid=(rows // tr,),
            in_specs=[pl.BlockSpec((tr, cols), lambda i, ch: (i, 0))],
            out_specs=pl.BlockSpec((None, tr, cols), lambda i, ch: (ch[0], i, 0))),
        out_shape=jax.ShapeDtypeStruct((N_CHIPS, rows, cols), BF16), compiler_params=_params("parallel"),
    )(chip_idx, w)


def _position():
    x, y, c = lax.axis_index("x"), lax.axis_index("y"), lax.axis_index("c")
    return x, y, c


def _xor_peer(x, y, c, k):
    return (x ^ ((k >> 2) & 1), y ^ ((k >> 1) & 1), c ^ (k & 1))


def _chip_peer(x, y, k):
    return (x ^ ((k >> 1) & 1), y ^ (k & 1))


def _ada_forward(c_row, ada_w, ada_b, conv_w):
    ns = ada_w.shape[2]
    cw = conv_w.shape[1]

    def body(c_ref, w_ref, b_ref, cv_ref, mod_ref, sc_ref, cvo_ref,
             c_all, mp, parts, cv_parts, send1, recv1, send2, recv2, send3, recv3):
        x, y, c = _position()
        me = 4 * x + 2 * y + c
        chip = 2 * x + y

        def c_copy(k):
            return pltpu.make_async_remote_copy(
                src_ref=c_all.at[me], dst_ref=c_all.at[me], send_sem=send1.at[k - 1], recv_sem=recv1.at[k - 1],
                device_id=_xor_peer(x, y, c, k), device_id_type=MESH)

        def cv_copy(k):
            px, py = _chip_peer(x, y, k)
            return pltpu.make_async_remote_copy(
                src_ref=cv_parts.at[chip], dst_ref=cv_parts.at[chip], send_sem=send3.at[k - 1],
                recv_sem=recv3.at[k - 1], device_id=(px, py, c), device_id_type=MESH)

        c_all[me] = c_ref[...]
        cv_parts[chip] = cv_ref[...]
        for k in range(1, N_DEV):
            c_copy(k).start()
        for k in range(1, N_CHIPS):
            cv_copy(k).start()
        for k in range(1, N_DEV):
            c_copy(k).wait_recv()
        cv = jnp.concatenate([c_all[i] for i in range(N_DEV)], axis=0)
        sc = cv * _sigmoid(cv)
        sc_ref[...] = sc
        for l in range(2):
            res = jnp.dot(sc, w_ref[l], preferred_element_type=F32, precision=lax.Precision.HIGHEST)
            for i in range(N_DEV):
                mp[i, l:l + 1, :] = res[i:i + 1, :]

        def mod_copy(k):
            px, py = _chip_peer(x, y, k)
            return pltpu.make_async_remote_copy(
                src_ref=mp.at[4 * px + 2 * py + c], dst_ref=parts.at[chip], send_sem=send2.at[k - 1],
                recv_sem=recv2.at[k - 1], device_id=(px, py, c), device_id_type=MESH)

        for k in range(1, N_CHIPS):
            mod_copy(k).start()
        parts[chip] = mp[me]
        for k in range(1, N_CHIPS):
            mod_copy(k).wait_recv()
            cv_copy(k).wait_recv()
        mod_ref[...] = jnp.concatenate([parts[j] for j in range(N_CHIPS)], axis=1) + b_ref[...]
        cvo_ref[...] = jnp.concatenate([cv_parts[j] for j in range(N_CHIPS)], axis=1)
        for k in range(1, N_DEV):
            c_copy(k).wait_send()
        for k in range(1, N_CHIPS):
            mod_copy(k).wait_send()
            cv_copy(k).wait_send()

    vm = pl.BlockSpec(memory_space=pltpu.VMEM)
    return pl.pallas_call(
        body, name="ada_forward",
        in_specs=[vm] * 4, out_specs=[vm] * 3,
        out_shape=[jax.ShapeDtypeStruct((2, 3 * D_MODEL), F32), jax.ShapeDtypeStruct((N_DEV, D_MODEL), F32),
                   jax.ShapeDtypeStruct((CONV_WIDTH, N_CHIPS * cw), F32)],
        scratch_shapes=[pltpu.VMEM((N_DEV, 1, D_MODEL), F32), pltpu.VMEM((N_DEV, 2, ns), F32),
                        pltpu.VMEM((N_CHIPS, 2, ns), F32), pltpu.VMEM((N_CHIPS, CONV_WIDTH, cw), F32),
                        pltpu.SemaphoreType.DMA((N_DEV - 1,)), pltpu.SemaphoreType.DMA((N_DEV - 1,)),
                        pltpu.SemaphoreType.DMA((N_CHIPS - 1,)), pltpu.SemaphoreType.DMA((N_CHIPS - 1,)),
                        pltpu.SemaphoreType.DMA((N_CHIPS - 1,)), pltpu.SemaphoreType.DMA((N_CHIPS - 1,))],
        compiler_params=pltpu.CompilerParams(vmem_limit_bytes=VMEM_LIMIT_BYTES),
    )(c_row, ada_w, ada_b, conv_w)


HBM_SPEC = pl.BlockSpec(memory_space=pltpu.HBM)
ANY_SPEC = pl.BlockSpec(memory_space=pl.ANY)
SEM_SPEC = pl.BlockSpec(memory_space=pltpu.SEMAPHORE)
SPLIT_PARAMS = dict(compiler_params=pltpu.CompilerParams(has_side_effects=pltpu.SideEffectType.DATAFLOW_SIDE_EFFECTING))
TOKEN = jax.ShapeDtypeStruct((8, 128), F32)


def _hbm(arrays):
    return [pltpu.with_memory_space_constraint(a, pltpu.HBM) for a in arrays]


def _hbm_like(arrays):
    return [pltpu.HBM(a.shape, a.dtype) for a in arrays]


def _gather_start(lands, after, name):
    n = len(lands)

    def body(*refs):
        ins = refs[:n]
        send, recv = refs[n + 1], refs[n + 2]
        x, y, c = _position()
        chip = 2 * x + y
        for t in range(n):
            rh = ins[t].shape[1] // 2
            for k in range(1, N_CHIPS):
                px, py = _chip_peer(x, y, k)
                block = ins[t].at[chip, pl.ds(c * rh, rh)]
                pltpu.make_async_remote_copy(
                    src_ref=block, dst_ref=block, send_sem=send.at[3 * t + k - 1], recv_sem=recv.at[3 * t + k - 1],
                    device_id=(px, py, c), device_id_type=MESH).start()
        refs[-1][...] = jnp.zeros(TOKEN.shape, F32)

    res = pl.pallas_call(
        body, name=name, in_specs=[HBM_SPEC] * n + [ANY_SPEC],
        out_specs=(SEM_SPEC, SEM_SPEC, *[HBM_SPEC] * n, pl.BlockSpec(memory_space=pltpu.VMEM)),
        out_shape=(pltpu.SemaphoreType.DMA((3 * n,)), pltpu.SemaphoreType.DMA((3 * n,)), *_hbm_like(lands), TOKEN),
        input_output_aliases={t: 2 + t for t in range(n)}, **SPLIT_PARAMS,
    )(*_hbm(lands), after)
    return res[0], res[1], list(res[2:2 + n]), res[-1]


def _gather_forward(send, recv, lands, after, name):
    n = len(lands)

    def body(*refs):
        ins = refs[:n]
        send1, recv1 = refs[n], refs[n + 1]
        send2, recv2 = refs[n + 3], refs[n + 4]
        x, y, c = _position()
        chip = 2 * x + y
        for t in range(n):
            rh = ins[t].shape[1] // 2
            half = pl.ds(c * rh, rh)
            for k in range(1, N_CHIPS):
                px, py = _chip_peer(x, y, k)
                s = 3 * t + k - 1
                got = ins[t].at[2 * px + py, half]
                cp = pltpu.make_async_remote_copy(
                    src_ref=ins[t].at[chip, half], dst_ref=got, send_sem=send1.at[s], recv_sem=recv1.at[s],
                    device_id=(px, py, c), device_id_type=MESH)
                cp.wait_send()
                cp.wait_recv()
                pltpu.make_async_remote_copy(
                    src_ref=got, dst_ref=got, send_sem=send2.at[s], recv_sem=recv2.at[s],
                    device_id=(x, y, 1 - c), device_id_type=MESH).start()
        refs[-1][...] = jnp.zeros(TOKEN.shape, F32)

    res = pl.pallas_call(
        body, name=name, in_specs=[HBM_SPEC] * n + [SEM_SPEC, SEM_SPEC, ANY_SPEC],
        out_specs=(SEM_SPEC, SEM_SPEC, *[HBM_SPEC] * n, pl.BlockSpec(memory_space=pltpu.VMEM)),
        out_shape=(pltpu.SemaphoreType.DMA((3 * n,)), pltpu.SemaphoreType.DMA((3 * n,)), *_hbm_like(lands), TOKEN),
        input_output_aliases={t: 2 + t for t in range(n)}, **SPLIT_PARAMS,
    )(*lands, send, recv, after)
    return res[0], res[1], list(res[2:2 + n]), res[-1]


def _gather_wait(send, recv, lands, after, name):
    n = len(lands)

    def body(*refs):
        ins = refs[:n]
        send_ref, recv_ref = refs[n], refs[n + 1]
        x, y, c = _position()
        for t in range(n):
            rh = ins[t].shape[1] // 2
            for k in range(1, N_CHIPS):
                px, py = _chip_peer(x, y, k)
                cp = pltpu.make_async_remote_copy(
                    src_ref=ins[t].at[2 * px + py, pl.ds(c * rh, rh)],
                    dst_ref=ins[t].at[2 * px + py, pl.ds((1 - c) * rh, rh)], send_sem=send_ref.at[3 * t + k - 1],
                    recv_sem=recv_ref.at[3 * t + k - 1], device_id=(x, y, 1 - c), device_id_type=MESH)
                cp.wait_send()
                cp.wait_recv()

    res = pl.pallas_call(
        body, name=name, in_specs=[HBM_SPEC] * n + [SEM_SPEC, SEM_SPEC, ANY_SPEC], out_specs=[HBM_SPEC] * n,
        out_shape=_hbm_like(lands), input_output_aliases={t: t for t in range(n)}, **SPLIT_PARAMS,
    )(*lands, send, recv, after)
    return list(res)


def _reduce_start(grads, after, name):
    n = len(grads)
    lands = [lax.empty((N_DEV, g.shape[1] // 2, g.shape[2]), BF16) for g in grads]

    def body(*refs):
        gs, ls = refs[:n], refs[n:2 * n]
        send, recv = refs[2 * n + 1], refs[2 * n + 2]
        x, y, c = _position()
        me = 4 * x + 2 * y + c
        for t in range(n):
            rh = gs[t].shape[1] // 2
            for k in range(1, N_DEV):
                px, py, pc = _xor_peer(x, y, c, k)
                pltpu.make_async_remote_copy(
                    src_ref=gs[t].at[2 * px + py, pl.ds(pc * rh, rh)], dst_ref=ls[t].at[me],
                    send_sem=send.at[7 * t + k - 1], recv_sem=recv.at[7 * t + k - 1],
                    device_id=(px, py, pc), device_id_type=MESH).start()
        refs[-1][...] = jnp.zeros(TOKEN.shape, F32)

    res = pl.pallas_call(
        body, name=name, in_specs=[HBM_SPEC] * (2 * n) + [ANY_SPEC],
        out_specs=(SEM_SPEC, SEM_SPEC, *[HBM_SPEC] * (2 * n), pl.BlockSpec(memory_space=pltpu.VMEM)),
        out_shape=(pltpu.SemaphoreType.DMA((7 * n,)), pltpu.SemaphoreType.DMA((7 * n,)),
                   *_hbm_like(grads), *_hbm_like(lands), TOKEN),
        input_output_aliases={t: 2 + t for t in range(2 * n)}, **SPLIT_PARAMS,
    )(*_hbm(grads), *_hbm(lands), after)
    return res[0], res[1], list(res[2:2 + n]), list(res[2 + n:2 + 2 * n]), res[-1]


def _reduce_wait(send, recv, grads, lands, after, name):
    n = len(grads)

    def body(*refs):
        gs, ls = refs[:n], refs[n:2 * n]
        send_ref, recv_ref = refs[2 * n], refs[2 * n + 1]
        x, y, c = _position()
        for t in range(n):
            rh = gs[t].shape[1] // 2
            for k in range(1, N_DEV):
                px, py, pc = _xor_peer(x, y, c, k)
                cp = pltpu.make_async_remote_copy(
                    src_ref=gs[t].at[2 * px + py, pl.ds(pc * rh, rh)], dst_ref=ls[t].at[4 * px + 2 * py + pc],
                    send_sem=send_ref.at[7 * t + k - 1], recv_sem=recv_ref.at[7 * t + k - 1],
                    device_id=(px, py, pc), device_id_type=MESH)
                cp.wait_send()
                cp.wait_recv()

    res = pl.pallas_call(
        body, name=name, in_specs=[HBM_SPEC] * (2 * n) + [SEM_SPEC, SEM_SPEC, ANY_SPEC], out_specs=[HBM_SPEC] * (2 * n),
        out_shape=_hbm_like(grads) + _hbm_like(lands), input_output_aliases={t: t for t in range(2 * n)}, **SPLIT_PARAMS,
    )(*grads, *lands, send, recv, after)
    return list(res[:n]), list(res[n:])


def _sum_devices(land, grad, dev_idx, name):
    _, rh, cols = land.shape
    tr = 128
    nb = rh // tr

    def body(idx_ref, l_ref, g_ref, o_ref):
        me = idx_ref[0]
        acc = jnp.where(me == 0, g_ref[...], l_ref[0]).astype(F32)
        for d in range(1, N_DEV):
            acc = acc + jnp.where(me == d, g_ref[...], l_ref[d]).astype(F32)
        o_ref[...] = acc

    return pl.pallas_call(
        body, name=name,
        grid_spec=pltpu.PrefetchScalarGridSpec(
            num_scalar_prefetch=1, grid=(nb,),
            in_specs=[pl.BlockSpec((N_DEV, tr, cols), lambda i, idx: (0, i, 0)),
                      pl.BlockSpec((None, tr, cols), lambda i, idx: (idx[1], idx[2] * nb + i, 0))],
            out_specs=pl.BlockSpec((tr, cols), lambda i, idx: (idx[2] * nb + i, 0))),
        out_shape=jax.ShapeDtypeStruct((2 * rh, cols), F32), compiler_params=_params("parallel"),
    )(dev_idx, land, grad)


def _split_start(name, arrays, n_sems, after, issue):
    m = len(arrays)

    def body(*refs):
        issue(refs[:m], refs[m + 1], refs[m + 2])
        refs[-1][...] = jnp.zeros(TOKEN.shape, F32)

    res = pl.pallas_call(
        body, name=name, in_specs=[HBM_SPEC] * m + [ANY_SPEC],
        out_specs=(SEM_SPEC, SEM_SPEC, *[HBM_SPEC] * m, pl.BlockSpec(memory_space=pltpu.VMEM)),
        out_shape=(pltpu.SemaphoreType.DMA((n_sems,)), pltpu.SemaphoreType.DMA((n_sems,)), *_hbm_like(arrays), TOKEN),
        input_output_aliases={t: 2 + t for t in range(m)}, **SPLIT_PARAMS,
    )(*_hbm(arrays), after)
    return res[0], res[1], list(res[2:2 + m]), res[-1]


def _split_wait(name, arrays, send, recv, after, await_all):
    m = len(arrays)

    def body(*refs):
        await_all(refs[:m], refs[m], refs[m + 1])

    res = pl.pallas_call(
        body, name=name, in_specs=[HBM_SPEC] * m + [SEM_SPEC, SEM_SPEC, ANY_SPEC], out_specs=[HBM_SPEC] * m,
        out_shape=_hbm_like(arrays), input_output_aliases={t: t for t in range(m)}, **SPLIT_PARAMS,
    )(*arrays, send, recv, after)
    return list(res)


def _sibling_copies(refs, send, recv, n):
    x, y, c = _position()
    cps = []
    for t in range(n):
        rh = refs[t].shape[1] // 2
        cps.append(pltpu.make_async_remote_copy(
            src_ref=refs[t].at[pl.ds(0, N_CHIPS), pl.ds((1 - c) * rh, rh)], dst_ref=refs[n + t],
            send_sem=send.at[t], recv_sem=recv.at[t], device_id=(x, y, 1 - c), device_id_type=MESH))
    return cps


def _reduce_sibling_start(grads, after, name):
    n = len(grads)
    lands = [lax.empty((N_CHIPS, g.shape[1] // 2, g.shape[2]), BF16) for g in grads]

    def issue(refs, send, recv):
        for cp in _sibling_copies(refs, send, recv, n):
            cp.start()

    return _split_start(name, list(grads) + lands, n, after, issue)


def _reduce_sibling_wait(send, recv, arrays, after, name):
    n = len(arrays) // 2

    def await_all(refs, send_ref, recv_ref):
        for cp in _sibling_copies(refs, send_ref, recv_ref, n):
            cp.wait_send()
            cp.wait_recv()

    res = _split_wait(name, arrays, send, recv, after, await_all)
    return res[:n], res[n:]


def _add_sibling_half(grad, got, dev_idx, name):
    j, r, cols = grad.shape
    rh = r // 2
    tr = 128
    nb = rh // tr

    def body(idx_ref, g_ref, got_ref, out_ref):
        out_ref[...] = (g_ref[...].astype(F32) + got_ref[...].astype(F32)).astype(BF16)

    return pl.pallas_call(
        body, name=name,
        grid_spec=pltpu.PrefetchScalarGridSpec(
            num_scalar_prefetch=1, grid=(j, nb),
            in_specs=[pl.BlockSpec((None, tr, cols), lambda jj, i, idx: (jj, idx[2] * nb + i, 0)),
                      pl.BlockSpec((None, tr, cols), lambda jj, i, idx: (jj, i, 0))],
            out_specs=pl.BlockSpec((None, tr, cols), lambda jj, i, idx: (jj, i, 0))),
        out_shape=jax.ShapeDtypeStruct((j, rh, cols), BF16),
        compiler_params=_params("parallel", "parallel"),
    )(dev_idx, grad, got)


def _chip_copies(refs, send, recv, n, receiving):
    x, y, c = _position()
    chip = 2 * x + y
    cps = []
    for t in range(n):
        for k in range(1, N_CHIPS):
            px, py = _chip_peer(x, y, k)
            cps.append(pltpu.make_async_remote_copy(
                src_ref=refs[t].at[2 * px + py], dst_ref=refs[n + t].at[2 * px + py if receiving else chip],
                send_sem=send.at[3 * t + k - 1], recv_sem=recv.at[3 * t + k - 1],
                device_id=(px, py, c), device_id_type=MESH))
    return cps


def _reduce_chips_start(partials, after, name):
    n = len(partials)
    lands = [lax.empty(p.shape, BF16) for p in partials]

    def issue(refs, send, recv):
        for cp in _chip_copies(refs, send, recv, n, False):
            cp.start()

    return _split_start(name, list(partials) + lands, 3 * n, after, issue)


def _reduce_chips_wait(send, recv, arrays, after, name):
    n = len(arrays) // 2

    def await_all(refs, send_ref, recv_ref):
        for cp in _chip_copies(refs, send_ref, recv_ref, n, True):
            cp.wait_send()
            cp.wait_recv()

    res = _split_wait(name, arrays, send, recv, after, await_all)
    return res[:n], res[n:]


def _sum_partials(land, partial, dev_idx, name):
    _, rh, cols = land.shape
    tr = 128
    nb = rh // tr

    def body(idx_ref, l_ref, p_ref, o_ref):
        chip = idx_ref[1]
        acc = jnp.where(chip == 0, p_ref[...], l_ref[0]).astype(F32)
        for s in range(1, N_CHIPS):
            acc = acc + jnp.where(chip == s, p_ref[...], l_ref[s]).astype(F32)
        o_ref[...] = acc

    return pl.pallas_call(
        body, name=name,
        grid_spec=pltpu.PrefetchScalarGridSpec(
            num_scalar_prefetch=1, grid=(nb,),
            in_specs=[pl.BlockSpec((N_CHIPS, tr, cols), lambda i, idx: (0, i, 0)),
                      pl.BlockSpec((None, tr, cols), lambda i, idx: (idx[1], i, 0))],
            out_specs=pl.BlockSpec((tr, cols), lambda i, idx: (idx[2] * nb + i, 0))),
        out_shape=jax.ShapeDtypeStruct((2 * rh, cols), F32), compiler_params=_params("parallel"),
    )(dev_idx, land, partial)


def _share_halves(totals):
    n = len(totals)

    def body(*refs):
        ins, outs = refs[:n], refs[n:2 * n]
        send, recv = refs[2 * n:]
        x, y, c = _position()
        cps = []
        for t in range(n):
            rh = ins[t].shape[0] // 2
            mine = pl.ds(c * rh, rh)
            cp = pltpu.make_async_remote_copy(
                src_ref=ins[t].at[mine], dst_ref=outs[t].at[mine], send_sem=send.at[t], recv_sem=recv.at[t],
                device_id=(x, y, 1 - c), device_id_type=MESH)
            cp.start()
            cps.append(cp)
        for cp in cps:
            cp.wait()

    return pl.pallas_call(
        body, name="reduce_share_" + "_".join(str(t.shape[1]) for t in totals), in_specs=[ANY_SPEC] * n,
        out_specs=[ANY_SPEC] * n, out_shape=[jax.ShapeDtypeStruct(t.shape, F32) for t in totals],
        input_output_aliases={t: t for t in range(n)},
        scratch_shapes=[pltpu.SemaphoreType.DMA((n,)), pltpu.SemaphoreType.DMA((n,))],
    )(*totals)


def _exchange_halves(grads):
    n = len(grads)
    hbm = pl.BlockSpec(memory_space=pl.ANY)

    def body(*refs):
        ins, outs = refs[:n], refs[n:2 * n]
        send, recv = refs[2 * n:]
        x, y, c = _position()
        cps = []
        for t in range(n):
            rh = ins[t].shape[1] // 2
            cp = pltpu.make_async_remote_copy(
                src_ref=ins[t].at[pl.ds(0, N_CHIPS), pl.ds((1 - c) * rh, rh)], dst_ref=outs[t], send_sem=send.at[t],
                recv_sem=recv.at[t], device_id=(x, y, 1 - c), device_id_type=MESH)
            cp.start()
            cps.append(cp)
        for cp in cps:
            cp.wait()

    return pl.pallas_call(
        body, name="reduce_exchange_halves", in_specs=[hbm] * n, out_specs=[hbm] * n,
        out_shape=[jax.ShapeDtypeStruct((g.shape[0], g.shape[1] // 2, g.shape[2]), BF16) for g in grads],
        scratch_shapes=[pltpu.SemaphoreType.DMA((n,)), pltpu.SemaphoreType.DMA((n,))],
    )(*grads)


def _add_halves(grad, got, c_idx, name):
    j, r, cols = grad.shape
    rh = r // 2
    tr = 128
    nb = rh // tr

    def body(c_ref, g_ref, o_ref_in, out_ref):
        out_ref[...] = (g_ref[...].astype(F32) + o_ref_in[...].astype(F32)).astype(BF16)

    return pl.pallas_call(
        body, name=name,
        grid_spec=pltpu.PrefetchScalarGridSpec(
            num_scalar_prefetch=1, grid=(j, nb),
            in_specs=[pl.BlockSpec((None, tr, cols), lambda jj, i, c_ref: (jj, c_ref[0] * nb + i, 0)),
                      pl.BlockSpec((None, tr, cols), lambda jj, i, c_ref: (jj, i, 0))],
            out_specs=pl.BlockSpec((None, tr, cols), lambda jj, i, c_ref: (jj, i, 0))),
        out_shape=jax.ShapeDtypeStruct((j, rh, cols), BF16),
        compiler_params=_params("parallel", "parallel"),
    )(c_idx, grad, got)


def _scatter_partials(partials):
    n = len(partials)
    hbm = pl.BlockSpec(memory_space=pl.ANY)

    def body(*refs):
        ins, outs = refs[:n], refs[n:2 * n]
        send, recv, local = refs[2 * n:]
        x, y, c = _position()
        chip = 2 * x + y
        cps, lcs = [], []
        for t in range(n):
            lc = pltpu.make_async_copy(ins[t].at[chip], outs[t].at[chip], local.at[t])
            lc.start()
            lcs.append(lc)
            for k in range(1, N_CHIPS):
                px, py = _chip_peer(x, y, k)
                s = 3 * t + k - 1
                cp = pltpu.make_async_remote_copy(
                    src_ref=ins[t].at[2 * px + py], dst_ref=outs[t].at[chip], send_sem=send.at[s],
                    recv_sem=recv.at[s], device_id=(px, py, c), device_id_type=MESH)
                cp.start()
                cps.append(cp)
        for cp in cps:
            cp.wait()
        for lc in lcs:
            lc.wait()

    return pl.pallas_call(
        body, name="reduce_scatter_partials", in_specs=[hbm] * n, out_specs=[hbm] * n,
        out_shape=[jax.ShapeDtypeStruct(p.shape, BF16) for p in partials],
        scratch_shapes=[pltpu.SemaphoreType.DMA((3 * n,)), pltpu.SemaphoreType.DMA((3 * n,)),
                        pltpu.SemaphoreType.DMA((n,))],
    )(*partials)


def _sum_chips(parts, name):
    j, rh, cols = parts.shape
    tr = 128

    def body(p_ref, o_ref):
        acc = p_ref[0].astype(F32)
        for s in range(1, j):
            acc = acc + p_ref[s].astype(F32)
        o_ref[...] = acc

    return pl.pallas_call(
        body, name=name, grid=(rh // tr,),
        in_specs=[pl.BlockSpec((j, tr, cols), lambda i: (0, i, 0))],
        out_specs=pl.BlockSpec((tr, cols), lambda i: (i, 0)),
        out_shape=jax.ShapeDtypeStruct((rh, cols), F32),
        compiler_params=_params("parallel"),
    )(parts)


def _share_totals(halves):
    n = len(halves)
    hbm = pl.BlockSpec(memory_space=pl.ANY)

    def body(*refs):
        ins, outs = refs[:n], refs[n:2 * n]
        send, recv, local = refs[2 * n:]
        x, y, c = _position()
        cps, lcs = [], []
        for t in range(n):
            rh = ins[t].shape[0]
            mine = outs[t].at[pl.ds(c * rh, rh)]
            lc = pltpu.make_async_copy(ins[t], mine, local.at[t])
            lc.start()
            lcs.append(lc)
            cp = pltpu.make_async_remote_copy(
                src_ref=ins[t], dst_ref=mine, send_sem=send.at[t], recv_sem=recv.at[t],
                device_id=(x, y, 1 - c), device_id_type=MESH)
            cp.start()
            cps.append(cp)
        for cp in cps:
            cp.wait()
        for lc in lcs:
            lc.wait()

    return pl.pallas_call(
        body, name="reduce_share_totals", in_specs=[hbm] * n, out_specs=[hbm] * n,
        out_shape=[jax.ShapeDtypeStruct((2 * h.shape[0], h.shape[1]), F32) for h in halves],
        scratch_shapes=[pltpu.SemaphoreType.DMA((n,)), pltpu.SemaphoreType.DMA((n,)),
                        pltpu.SemaphoreType.DMA((n,))],
    )(*halves)


SMALL_ROWS = 56


def _reduce_small(packed, silu_c):
    ns = 3 * D_MODEL // N_CHIPS

    def body(p_ref, sc_ref, tot_ref, gw_ref, loss_ref, qk_ref, allp, send, recv):
        x, y, c = _position()
        me = 4 * x + 2 * y + c
        chip = 2 * x + y

        def copy(k):
            return pltpu.make_async_remote_copy(
                src_ref=allp.at[me], dst_ref=allp.at[me], send_sem=send.at[k - 1], recv_sem=recv.at[k - 1],
                device_id=_xor_peer(x, y, c, k), device_id_type=MESH)

        allp[me] = p_ref[...]
        for k in range(1, N_DEV):
            copy(k).start()
        for k in range(1, N_DEV):
            copy(k).wait_recv()
        tot = allp[0]
        for i in range(1, N_DEV):
            tot = tot + allp[i]
        tot_ref[...] = tot
        loss_ref[...] = jnp.sum(tot[11:12, :], axis=1, keepdims=True) * (0.5 / D_MODEL)
        fold = tot[5:11, 0:HEAD_DIM]
        for h in range(1, N_HEADS):
            fold = fold + tot[5:11, h * HEAD_DIM:(h + 1) * HEAD_DIM]
        qk_ref[...] = jnp.concatenate([fold, jnp.zeros((2, HEAD_DIM), F32)], axis=0)
        sct = sc_ref[...].T
        rc = 64
        for l in range(2):
            dms = [allp[i, pl.ds(12 + 4 * l + chip, 1), :][:, :ns] for i in range(N_DEV)]
            for r0 in range(0, D_MODEL, rc):
                acc = sct[r0:r0 + rc, 0:1] * dms[0]
                for i in range(1, N_DEV):
                    acc = acc + sct[r0:r0 + rc, i:i + 1] * dms[i]
                gw_ref[l, r0:r0 + rc, :] = acc
        for k in range(1, N_DEV):
            copy(k).wait_send()

    vm = pl.BlockSpec(memory_space=pltpu.VMEM)
    return pl.pallas_call(
        body, name="reduce_small", in_specs=[vm, vm], out_specs=[vm] * 4,
        out_shape=[jax.ShapeDtypeStruct((SMALL_ROWS, D_MODEL), F32), jax.ShapeDtypeStruct((2, D_MODEL, ns), F32),
                   jax.ShapeDtypeStruct((1, 1), F32), jax.ShapeDtypeStruct((8, HEAD_DIM), F32)],
        scratch_shapes=[pltpu.VMEM((N_DEV, SMALL_ROWS, D_MODEL), F32),
                        pltpu.SemaphoreType.DMA((N_DEV - 1,)), pltpu.SemaphoreType.DMA((N_DEV - 1,))],
        compiler_params=pltpu.CompilerParams(vmem_limit_bytes=VMEM_LIMIT_BYTES),
    )(packed, silu_c)


def _reduce_big(grads, c_idx):
    names = list(grads)
    got = _exchange_halves([grads[k] for k in names])
    partials = [_add_halves(grads[k], got[i], c_idx, f"reduce_add_{k}") for i, k in enumerate(names)]
    parts = _scatter_partials(partials)
    halves = [_sum_chips(parts[i], f"reduce_sum_{k}") for i, k in enumerate(names)]
    totals = _share_totals(halves)
    return dict(zip(names, totals))


def kernel(x, c, norm_g, ada_w, ada_b, a_w_in, a_conv_w, a_conv_b, a_ln_g, a_ln_b, a_w_out, b_w_in, b_q_norm, b_k_norm, b_w_out, loss_target, m_norm_g, m_ada_w, m_ada_b, m_a_w_in, m_a_conv_w, m_a_conv_b, m_a_ln_g, m_a_ln_b, m_a_w_out, m_b_w_in, m_b_q_norm, m_b_k_norm, m_b_w_out, v_norm_g, v_ada_w, v_ada_b, v_a_w_in, v_a_conv_w, v_a_conv_b, v_a_ln_g, v_a_ln_b, v_a_w_out, v_b_w_in, v_b_q_norm, v_b_k_norm, v_b_w_out):
    chip = 2 * lax.axis_index("x") + lax.axis_index("y")
    core = lax.axis_index("c")
    chip_idx = chip.astype(jnp.int32).reshape(1)
    dev_idx = jnp.stack([2 * chip + core, chip, core]).astype(jnp.int32)

    mods, silu_c, conv_w_full = _ada_forward(c, ada_w, ada_b, a_conv_w[0])
    lands_a = [_cast_into_slot(a_w_in[0], chip_idx, "cast_a_w_in"), _cast_into_slot(a_w_out[0], chip_idx, "cast_a_w_out")]
    send_a, recv_a, lands_a, token_a = _gather_start(lands_a, mods, "gather_start_a")
    lands_b = [_cast_into_slot(b_w_in[0], chip_idx, "cast_b_w_in"), _cast_into_slot(b_w_out[0], chip_idx, "cast_b_w_out")]
    send_b, recv_b, lands_b, token_b = _gather_start(lands_b, token_a, "gather_start_b")
    mods = mods + token_b[0:2, 0:1]

    def weights_a(after):
        send, recv, lands, _ = _gather_forward(send_a, recv_a, lands_a, after, "gather_forward_a")
        w_in, w_out = _gather_wait(send, recv, lands, after, "gather_wait_a")
        return w_in, w_out.reshape(D_MODEL, D_MODEL)

    forwarded_b = []

    def weights_b(after):
        send, recv, lands, _ = forwarded_b
        w_in, w_out = _gather_wait(send, recv, lands, after, "gather_wait_b")
        return w_in, w_out.reshape(D_MODEL, D_MODEL)

    def forward_weights_b(after):
        forwarded_b.extend(_gather_forward(send_b, recv_b, lands_b, after, "gather_forward_b"))
        return forwarded_b[3]

    stage1, stage2 = {}, {}

    def send_grads(tag, dw_in, dw_out):
        grads = [dw_in, dw_out.reshape(N_CHIPS, D_MODEL // N_CHIPS, D_MODEL)]
        send, recv, arrays, token = _reduce_sibling_start(grads, dw_out, f"reduce_d2d_start_{tag}")
        stage1[tag] = (send, recv, arrays)
        return token

    def forward_grads(tag, after):
        send, recv, arrays = stage1[tag]
        grads, got = _reduce_sibling_wait(send, recv, arrays, after, f"reduce_d2d_wait_{tag}")
        partials = [_add_sibling_half(grads[i], got[i], dev_idx, f"reduce_add_{tag}_{i}") for i in range(2)]
        send, recv, arrays, token = _reduce_chips_start(partials, partials[1], f"reduce_ici_start_{tag}")
        stage2[tag] = (send, recv, arrays)
        return token

    def finish_grads(tag, after):
        send, recv, arrays = stage2[tag]
        partials, lands = _reduce_chips_wait(send, recv, arrays, after, f"reduce_ici_wait_{tag}")
        totals = [_sum_partials(lands[i], partials[i], dev_idx, f"reduce_sum_{tag}_{i}") for i in range(2)]
        return _share_halves(totals)

    grad_x, small = _local_step(
        x[0], loss_target[0], mods.reshape(2, 3, D_MODEL), norm_g, conv_w_full, a_conv_b, a_ln_g[0:1],
        a_ln_b[0:1], b_q_norm[0], b_k_norm[0], weights_a, weights_b, forward_weights_b,
        functools.partial(send_grads, "b"), functools.partial(forward_grads, "b"), functools.partial(send_grads, "a"))

    ns = 3 * D_MODEL // N_CHIPS
    pad_mod = lambda dm: jnp.pad(dm.reshape(N_CHIPS, ns), ((0, 0), (0, D_MODEL - ns)))
    packed = jnp.concatenate([
        small["dnorm_g"], small["dconv_b"], small["dln_g"], small["dln_b"], small["dq_norm"], small["dk_norm"],
        small["loss_cols"], pad_mod(small["dmod0"]), pad_mod(small["dmod1"]), small["dconv_w"],
        jnp.zeros((SMALL_ROWS - 20 - CONV_WIDTH, D_MODEL), F32)], axis=0)
    tot, g_ada_w, loss, qk = _reduce_small(packed, silu_c)
    cw = D_MODEL // N_CHIPS
    g_small = dict(
        norm_g=tot[0:2], a_conv_b=tot[2:3], a_ln_g=tot[3:4], a_ln_b=tot[4:5],
        b_q_norm=qk[0:3], b_k_norm=qk[3:6],
        ada_b=jnp.stack([tot[12:16, :ns].reshape(3 * D_MODEL), tot[16:20, :ns].reshape(3 * D_MODEL)]),
        a_conv_w=lax.dynamic_slice(tot[20:20 + CONV_WIDTH], (0, chip * cw), (CONV_WIDTH, cw)),
    )


    given = dict(norm_g=(norm_g, m_norm_g, v_norm_g), ada_w=(ada_w, m_ada_w, v_ada_w), ada_b=(ada_b, m_ada_b, v_ada_b),
                 a_w_in=(a_w_in, m_a_w_in, v_a_w_in), a_conv_w=(a_conv_w, m_a_conv_w, v_a_conv_w),
                 a_conv_b=(a_conv_b, m_a_conv_b, v_a_conv_b), a_ln_g=(a_ln_g, m_a_ln_g, v_a_ln_g),
                 a_ln_b=(a_ln_b, m_a_ln_b, v_a_ln_b), a_w_out=(a_w_out, m_a_w_out, v_a_w_out),
                 b_w_in=(b_w_in, m_b_w_in, v_b_w_in), b_q_norm=(b_q_norm, m_b_q_norm, v_b_q_norm),
                 b_k_norm=(b_k_norm, m_b_k_norm, v_b_k_norm), b_w_out=(b_w_out, m_b_w_out, v_b_w_out))
    order = ["norm_g", "ada_w", "ada_b", "a_w_in", "a_conv_w", "a_conv_b", "a_ln_g", "a_ln_b", "a_w_out", "b_w_in",
             "b_q_norm", "b_k_norm", "b_w_out"]
    outs = {}

    def update(k, g2):
        w, m, v = given[k]
        shape2 = g2.shape
        d2, m2, v2 = _adamw(w.reshape(shape2), g2, m.reshape(shape2), v.reshape(shape2), f"adamw_{k}")
        outs[k] = tuple(a.reshape(w.shape) for a in (g2, d2, m2, v2))

    token = forward_grads("a", tot)
    g_b_in, g_b_out = finish_grads("b", token)
    update("b_w_in", g_b_in)
    update("b_w_out", g_b_out)
    update("ada_w", g_ada_w.reshape(2 * D_MODEL, ns))
    for k, g2 in g_small.items():
        update(k, g2)
    g_a_in, g_a_out = finish_grads("a", outs["b_w_in"][1])
    update("a_w_in", g_a_in)
    update("a_w_out", g_a_out)
    return (loss.reshape(()), grad_x[None], *[outs[k][0] for k in order], *[outs[k][1] for k in order],
            *[outs[k][2] for k in order], *[outs[k][3] for k in order])
```

```python
import functools

import jax
import jax.numpy as jnp
from jax import lax
from jax.experimental import pallas as pl
from jax.experimental.pallas import tpu as pltpu

F32 = jnp.float32
BF16 = jnp.bfloat16

SEQ = 2048
D_MODEL = 1024
CONV_WIDTH = 31
HEAD_DIM = 64
N_HEADS = 16
DILATIONS = (1, 4, 16)
ATTN_BLOCK = 128
NORM_EPS = 1e-6
NEG_INF = -1e30
N_DEV = 8
N_CHIPS = 4

ADAM_LR = 0.001
ADAM_B1 = 0.9
ADAM_B2 = 0.999
ADAM_EPS = 1e-08
ADAM_WD = 0.01
ADAM_STEP = 10

VMEM_LIMIT_BYTES = 52 * 1024 * 1024
HALO = 32
LANES = 128
MESH = pl.DeviceIdType.MESH


def _params(*sem):
    return pltpu.CompilerParams(dimension_semantics=sem or None, vmem_limit_bytes=VMEM_LIMIT_BYTES)


def _sigmoid(v):
    return 1.0 / (1.0 + jnp.exp(-v))


def _row_spec(tm, cols, col_block=0):
    return pl.BlockSpec((tm, cols), lambda i: (i, col_block))


def _vec_spec(rows, cols):
    return pl.BlockSpec((rows, cols), lambda i: (0, 0))


def _normmod(xv, g, scale, shift):
    r = lax.rsqrt(jnp.mean(xv * xv, axis=-1, keepdims=True) + NORM_EPS)
    return xv * r * g * (1.0 + scale) + shift


def _normmod_fwd(x, g, scale, shift, name):
    tm = 256

    def body(x_ref, g_ref, sc_ref, sh_ref, h_ref, ht_ref):
        h = _normmod(x_ref[...], g_ref[...], sc_ref[...], sh_ref[...])
        h_ref[...] = h.astype(BF16)
        ht_ref[...] = h.T.astype(BF16)

    return pl.pallas_call(
        body, name=name, grid=(SEQ // tm,),
        in_specs=[_row_spec(tm, D_MODEL)] + [_vec_spec(1, D_MODEL)] * 3,
        out_specs=[_row_spec(tm, D_MODEL), pl.BlockSpec((D_MODEL, tm), lambda i: (0, i))],
        out_shape=[jax.ShapeDtypeStruct((SEQ, D_MODEL), BF16), jax.ShapeDtypeStruct((D_MODEL, SEQ), BF16)],
        compiler_params=_params("parallel"),
    )(x, g, scale, shift)


def _normmod_bwd(x, g, scale, dh_parts, dres, name, part_dilations=None):
    tm = 256
    n_parts = len(dh_parts)
    dils = part_dilations or (1,) * n_parts
    dh_parts = [p if d == 1 else p.reshape(d, SEQ // d, D_MODEL) for p, d in zip(dh_parts, dils)]

    def body(x_ref, g_ref, sc_ref, dres_ref, *rest):
        part_refs = rest[:n_parts]
        dx_ref, sums_ref, nat = rest[n_parts:]
        xv = x_ref[...]
        r = lax.rsqrt(jnp.mean(xv * xv, axis=-1, keepdims=True) + NORM_EPS)
        xn = xv * r
        dh = _load_natural(part_refs[0], nat, dils[0])
        for p, d in zip(part_refs[1:], dils[1:]):
            dh = dh + _load_natural(p, nat, d)
        gv = g_ref[...]
        one_sc = 1.0 + sc_ref[...]
        dxn = dh * (gv * one_sc)
        dx = r * (dxn - xn * jnp.mean(dxn * xn, axis=-1, keepdims=True))
        dx_ref[...] = dres_ref[...] + dx
        dhx = dh * xn
        sums = jnp.concatenate([
            jnp.sum(dhx, axis=0, keepdims=True) * one_sc,
            jnp.sum(dhx, axis=0, keepdims=True) * gv,
            jnp.sum(dh, axis=0, keepdims=True),
            jnp.zeros((5, D_MODEL), F32)], axis=0)

        @pl.when(pl.program_id(0) == 0)
        def _():
            sums_ref[...] = jnp.zeros_like(sums_ref)

        sums_ref[...] += sums

    return pl.pallas_call(
        body, name=name, grid=(SEQ // tm,),
        in_specs=[_row_spec(tm, D_MODEL), _vec_spec(1, D_MODEL), _vec_spec(1, D_MODEL), _row_spec(tm, D_MODEL)]
        + [_class_spec(tm, d) for d in dils],
        out_specs=[_row_spec(tm, D_MODEL), _vec_spec(8, D_MODEL)],
        out_shape=[jax.ShapeDtypeStruct((SEQ, D_MODEL), F32), jax.ShapeDtypeStruct((8, D_MODEL), F32)],
        scratch_shapes=[_natural_scratch(tm)],
        compiler_params=_params("arbitrary"),
    )(x, g, scale, dres, *dh_parts)


def _mm(lhs, rhs, *, tn, tile0, n_tiles, out_dtype, name, out3d=None, prev=None):
    mo, kc = lhs.shape
    cm = 512

    def body(l_ref, r_ref, *rest):
        o_ref = rest[-1]
        for m in range(mo // cm):
            rows = pl.ds(m * cm, cm)
            o_ref[rows, :] = jnp.dot(l_ref[rows, :], r_ref[...], preferred_element_type=F32).astype(out_dtype)

    if rhs.ndim == 3:
        tps_r = rhs.shape[2] // tn
        r_spec = pl.BlockSpec((None, kc, tn), lambda t: ((tile0 + t) // tps_r, 0, (tile0 + t) % tps_r))
    else:
        r_spec = pl.BlockSpec((kc, tn), lambda t: (0, t))
    in_specs = [pl.BlockSpec((mo, kc), lambda t: (0, 0)), r_spec]
    args = [lhs, rhs]
    aliases = {}
    if out3d is None:
        o_spec = pl.BlockSpec((mo, tn), lambda t: (0, t))
        o_shape = jax.ShapeDtypeStruct((mo, n_tiles * tn), out_dtype)
    else:
        j_out, ns_out = out3d
        tps_o = ns_out // tn
        o_spec = pl.BlockSpec((None, mo, tn), lambda t: ((tile0 + t) // tps_o, 0, (tile0 + t) % tps_o))
        o_shape = jax.ShapeDtypeStruct((j_out, mo, ns_out), out_dtype)
        if prev is not None:
            in_specs.append(pl.BlockSpec(memory_space=pl.ANY))
            args.append(prev)
            aliases = {2: 0}
    return pl.pallas_call(
        body, name=name, grid=(n_tiles,), in_specs=in_specs, out_specs=o_spec, out_shape=o_shape,
        input_output_aliases=aliases, compiler_params=_params("parallel"),
    )(*args)


def _mm_nt(dy, w3, *, tn, tile0, n_tiles, name, after=None):
    m_rows = dy.shape[0]
    _, kc, ns = w3.shape
    tps = ns // tn
    cm = 512
    extra = [] if after is None else [after]

    def body(dy_ref, w_ref, *rest):
        o_ref = rest[-1]

        @pl.when(pl.program_id(0) == 0)
        def _():
            o_ref[...] = jnp.zeros_like(o_ref)

        for m in range(m_rows // cm):
            rows = pl.ds(m * cm, cm)
            o_ref[rows, :] += lax.dot_general(dy_ref[rows, :], w_ref[...], (((1,), (1,)), ((), ())),
                                              preferred_element_type=F32)

    return pl.pallas_call(
        body, name=name, grid=(n_tiles,),
        in_specs=[pl.BlockSpec((m_rows, tn), lambda t: (0, t)),
                  pl.BlockSpec((None, kc, tn), lambda t: ((tile0 + t) // tps, 0, (tile0 + t) % tps))]
        + [pl.BlockSpec(memory_space=pl.ANY)] * len(extra),
        out_specs=pl.BlockSpec((m_rows, kc), lambda t: (0, 0)),
        out_shape=jax.ShapeDtypeStruct((m_rows, kc), F32),
        compiler_params=_params("arbitrary"),
    )(dy, w3, *extra)


CONV_CHUNK = 16


def _shift_copies(buf, shifted):
    rows = shifted.shape[1]
    for s in range(1, 8):
        shifted[s - 1] = buf[pl.ds(s, rows), :]


def _shifted_rows(buf, shifted, offset, r0):
    s = offset % 8
    if s == 0:
        return buf[pl.ds(r0 + offset, CONV_CHUNK), :]
    return shifted[s - 1, pl.ds(r0 + (offset - s), CONV_CHUNK), :]


def _conv_fwd(proj, conv_w, conv_b, ln_g, ln_b, name):
    tm = 256
    hb = tm // HALO

    def body(vg_ref, halo_ref, z_ref, w_ref, b_ref, g_ref, be_ref, u5_ref, u5t_ref, u2_ref, buf, shifted):
        i = pl.program_id(0)
        u1 = vg_ref[:, :D_MODEL] * _sigmoid(vg_ref[:, D_MODEL:])
        u1h = halo_ref[:, :D_MODEL] * _sigmoid(halo_ref[:, D_MODEL:])
        buf[pl.ds(0, HALO), :] = jnp.where(i > 0, u1h, 0.0)
        buf[pl.ds(HALO, tm), :] = u1
        _shift_copies(buf, shifted)

        def chunk(ci, carry):
            r0 = pl.multiple_of(ci * CONV_CHUNK, CONV_CHUNK)
            acc = jnp.broadcast_to(b_ref[...], (CONV_CHUNK, D_MODEL))
            for k in range(CONV_WIDTH):
                acc = acc + w_ref[k:k + 1, :] * _shifted_rows(buf, shifted, HALO - (CONV_WIDTH - 1) + k, r0)
            u2_ref[pl.ds(r0, CONV_CHUNK), :] = acc
            return carry

        lax.fori_loop(0, tm // CONV_CHUNK, chunk, 0)
        acc = u2_ref[...]
        mu = jnp.mean(acc, axis=-1, keepdims=True)
        xc = acc - mu
        rstd = lax.rsqrt(jnp.mean(xc * xc, axis=-1, keepdims=True) + NORM_EPS)
        u3 = xc * rstd * g_ref[...] + be_ref[...]
        zv = z_ref[...]
        u5 = u3 * _sigmoid(u3) * (zv * _sigmoid(zv))
        u5_ref[...] = u5.astype(BF16)
        u5t_ref[...] = u5.T.astype(BF16)

    return pl.pallas_call(
        body, name=name, grid=(SEQ // tm,),
        in_specs=[pl.BlockSpec((tm, 2 * D_MODEL), lambda i: (i, 0)),
                  pl.BlockSpec((HALO, 2 * D_MODEL), lambda i: (jnp.maximum(i * hb - 1, 0), 0)),
                  _row_spec(tm, D_MODEL, 2),
                  _vec_spec(CONV_WIDTH, D_MODEL)] + [_vec_spec(1, D_MODEL)] * 3,
        out_specs=[_row_spec(tm, D_MODEL), pl.BlockSpec((D_MODEL, tm), lambda i: (0, i)), _row_spec(tm, D_MODEL)],
        out_shape=[jax.ShapeDtypeStruct((SEQ, D_MODEL), BF16), jax.ShapeDtypeStruct((D_MODEL, SEQ), BF16),
                   jax.ShapeDtypeStruct((SEQ, D_MODEL), F32)],
        scratch_shapes=[pltpu.VMEM((HALO + tm, D_MODEL), F32), pltpu.VMEM((7, HALO + tm - 8, D_MODEL), F32)],
        compiler_params=_params("parallel"),
    )(proj, proj, proj, conv_w, conv_b, ln_g, ln_b)


def _conv_bwd_pointwise(du5, proj, u2, ln_g, ln_b, name):
    tm = 256

    def body(du5_ref, z_ref, u2_ref, g_ref, be_ref, du2_ref, dz_ref, sums_ref):
        u2v = u2_ref[...]
        mu = jnp.mean(u2v, axis=-1, keepdims=True)
        xc = u2v - mu
        rstd = lax.rsqrt(jnp.mean(xc * xc, axis=-1, keepdims=True) + NORM_EPS)
        xhat = xc * rstd
        u3 = xhat * g_ref[...] + be_ref[...]
        s3 = _sigmoid(u3)
        u4 = u3 * s3
        zv = z_ref[...]
        sz = _sigmoid(zv)
        du5v = du5_ref[...]
        dz_ref[...] = du5v * u4 * (sz * (1.0 + zv * (1.0 - sz)))
        du3 = du5v * (zv * sz) * (s3 * (1.0 + u3 * (1.0 - s3)))
        dxhat = du3 * g_ref[...]
        du2 = rstd * (dxhat - jnp.mean(dxhat, axis=-1, keepdims=True)
                      - xhat * jnp.mean(dxhat * xhat, axis=-1, keepdims=True))
        du2_ref[...] = du2
        sums = jnp.concatenate([
            jnp.sum(du3 * xhat, axis=0, keepdims=True),
            jnp.sum(du3, axis=0, keepdims=True),
            jnp.sum(du2, axis=0, keepdims=True),
            jnp.zeros((5, D_MODEL), F32)], axis=0)

        @pl.when(pl.program_id(0) == 0)
        def _():
            sums_ref[...] = jnp.zeros_like(sums_ref)

        sums_ref[...] += sums

    return pl.pallas_call(
        body, name=name, grid=(SEQ // tm,),
        in_specs=[_row_spec(tm, D_MODEL), _row_spec(tm, D_MODEL, 2), _row_spec(tm, D_MODEL),
                  _vec_spec(1, D_MODEL), _vec_spec(1, D_MODEL)],
        out_specs=[_row_spec(tm, D_MODEL), _row_spec(tm, D_MODEL), _vec_spec(8, D_MODEL)],
        out_shape=[jax.ShapeDtypeStruct((SEQ, D_MODEL), F32), jax.ShapeDtypeStruct((SEQ, D_MODEL), F32),
                   jax.ShapeDtypeStruct((8, D_MODEL), F32)],
        compiler_params=_params("arbitrary"),
    )(du5, proj, u2, ln_g, ln_b)


def _conv_bwd_taps(du2, dz, proj, conv_w, name):
    tm = 256
    hb = tm // HALO
    n_blocks = SEQ // tm

    def body(du2_ref, dnext_ref, dz_ref, vg_ref, halo_ref, w_ref, dproj_ref, dw_ref,
             ubuf, dbuf, ushift, dshift, sgbuf, dwacc):
        i = pl.program_id(0)
        sg = _sigmoid(vg_ref[:, D_MODEL:])
        sgbuf[...] = sg
        u1h = halo_ref[:, :D_MODEL] * _sigmoid(halo_ref[:, D_MODEL:])
        ubuf[pl.ds(0, HALO), :] = jnp.where(i > 0, u1h, 0.0)
        ubuf[pl.ds(HALO, tm), :] = vg_ref[:, :D_MODEL] * sg
        dbuf[pl.ds(0, tm), :] = du2_ref[...]
        dbuf[pl.ds(tm, HALO), :] = jnp.where(i < n_blocks - 1, dnext_ref[...], 0.0)
        _shift_copies(ubuf, ushift)
        _shift_copies(dbuf, dshift)

        @pl.when(i == 0)
        def _():
            dwacc[...] = jnp.zeros_like(dwacc)

        def chunk(ci, carry):
            r0 = pl.multiple_of(ci * CONV_CHUNK, CONV_CHUNK)
            rows = pl.ds(r0, CONV_CHUNK)
            du2c = du2_ref[rows, :]
            du1 = jnp.zeros((CONV_CHUNK, D_MODEL), F32)
            for k in range(CONV_WIDTH):
                du1 = du1 + w_ref[k:k + 1, :] * _shifted_rows(dbuf, dshift, CONV_WIDTH - 1 - k, r0)
                prod = du2c * _shifted_rows(ubuf, ushift, HALO - (CONV_WIDTH - 1) + k, r0)
                dwacc[k] += prod[0:8] + prod[8:16]
            sgc = sgbuf[rows, :]
            dval = du1 * sgc
            dproj_ref[rows, 0:D_MODEL] = dval.astype(BF16)
            dproj_ref[rows, D_MODEL:2 * D_MODEL] = (dval * vg_ref[rows, 0:D_MODEL] * (1.0 - sgc)).astype(BF16)
            return carry

        lax.fori_loop(0, tm // CONV_CHUNK, chunk, 0)
        dproj_ref[:, 2 * D_MODEL:] = dz_ref[...].astype(BF16)

        @pl.when(i == n_blocks - 1)
        def _():
            for k in range(CONV_WIDTH):
                dw_ref[k:k + 1, :] = jnp.sum(dwacc[k], axis=0, keepdims=True)
            dw_ref[CONV_WIDTH:, :] = jnp.zeros((32 - CONV_WIDTH, D_MODEL), F32)

    return pl.pallas_call(
        body, name=name, grid=(n_blocks,),
        in_specs=[_row_spec(tm, D_MODEL),
                  pl.BlockSpec((HALO, D_MODEL), lambda i: (jnp.minimum((i + 1) * hb, SEQ // HALO - 1), 0)),
                  _row_spec(tm, D_MODEL),
                  pl.BlockSpec((tm, 2 * D_MODEL), lambda i: (i, 0)),
                  pl.BlockSpec((HALO, 2 * D_MODEL), lambda i: (jnp.maximum(i * hb - 1, 0), 0)),
                  _vec_spec(CONV_WIDTH, D_MODEL)],
        out_specs=[_row_spec(tm, 3 * D_MODEL), _vec_spec(32, D_MODEL)],
        out_shape=[jax.ShapeDtypeStruct((SEQ, 3 * D_MODEL), BF16), jax.ShapeDtypeStruct((32, D_MODEL), F32)],
        scratch_shapes=[pltpu.VMEM((HALO + tm, D_MODEL), F32), pltpu.VMEM((tm + HALO, D_MODEL), F32),
                        pltpu.VMEM((7, HALO + tm - 8, D_MODEL), F32), pltpu.VMEM((7, HALO + tm - 8, D_MODEL), F32),
                        pltpu.VMEM((tm, D_MODEL), F32), pltpu.VMEM((CONV_WIDTH, 8, D_MODEL), F32)],
        compiler_params=_params("arbitrary"),
    )(du2, du2, dz, proj, proj, conv_w)


def _out_a(u5, w_out, x, gate, g1, scale1, shift1, name):
    tm = 256
    n_d = len(DILATIONS)

    def body(u_ref, w_ref, x_ref, gate_ref, g_ref, sc_ref, sh_ref, x1_ref, y_ref, ht_ref, *rest):
        h_refs, nat = rest[:n_d], rest[-1]
        y = jnp.dot(u_ref[...], w_ref[...], preferred_element_type=F32)
        x1 = x_ref[...] + gate_ref[...] * y
        y_ref[...] = y
        x1_ref[...] = x1
        h = _normmod(x1, g_ref[...], sc_ref[...], sh_ref[...])
        ht_ref[...] = h.T.astype(BF16)
        for h_ref, d in zip(h_refs, DILATIONS):
            _store_classes(h_ref, h, nat, d)

    res = pl.pallas_call(
        body, name=name, grid=(SEQ // tm,),
        in_specs=[_row_spec(tm, D_MODEL), _vec_spec(D_MODEL, D_MODEL), _row_spec(tm, D_MODEL)]
        + [_vec_spec(1, D_MODEL)] * 4,
        out_specs=[_row_spec(tm, D_MODEL), _row_spec(tm, D_MODEL), pl.BlockSpec((D_MODEL, tm), lambda i: (0, i))]
        + [_class_spec(tm, d) for d in DILATIONS],
        out_shape=[jax.ShapeDtypeStruct((SEQ, D_MODEL), F32), jax.ShapeDtypeStruct((SEQ, D_MODEL), F32),
                   jax.ShapeDtypeStruct((D_MODEL, SEQ), BF16)] + [_class_shape(d, BF16) for d in DILATIONS],
        scratch_shapes=[_natural_scratch(tm)],
        compiler_params=_params("parallel"),
    )(u5, w_out, x, gate, g1, scale1, shift1)
    return res[0], res[1], res[2], [a.reshape(SEQ, D_MODEL) for a in res[3:]]


def _out_b_loss(u, w_out, x1, gate, target, name):
    tm = 256

    def body(u_ref, w_ref, x_ref, gate_ref, t_ref, e_ref, dy_ref, sums_ref):
        y = jnp.dot(u_ref[...], w_ref[...], preferred_element_type=F32)
        diff = x_ref[...] + gate_ref[...] * y - t_ref[...]
        e = diff * (1.0 / D_MODEL)
        e_ref[...] = e
        dy_ref[...] = (e * gate_ref[...]).astype(BF16)
        sums = jnp.concatenate([
            jnp.sum(e * y, axis=0, keepdims=True),
            jnp.sum(diff * diff, axis=0, keepdims=True),
            jnp.zeros((6, D_MODEL), F32)], axis=0)

        @pl.when(pl.program_id(0) == 0)
        def _():
            sums_ref[...] = jnp.zeros_like(sums_ref)

        sums_ref[...] += sums

    return pl.pallas_call(
        body, name=name, grid=(SEQ // tm,),
        in_specs=[_row_spec(tm, D_MODEL), _vec_spec(D_MODEL, D_MODEL), _row_spec(tm, D_MODEL),
                  _vec_spec(1, D_MODEL), _row_spec(tm, D_MODEL)],
        out_specs=[_row_spec(tm, D_MODEL), _row_spec(tm, D_MODEL), _vec_spec(8, D_MODEL)],
        out_shape=[jax.ShapeDtypeStruct((SEQ, D_MODEL), F32), jax.ShapeDtypeStruct((SEQ, D_MODEL), BF16),
                   jax.ShapeDtypeStruct((8, D_MODEL), F32)],
        compiler_params=_params("arbitrary"),
    )(u, w_out, x1, gate, target)


def _dgate_dy(dx1, y, gate, name):
    tm = 256

    def body(d_ref, y_ref, gate_ref, dy_ref, sums_ref):
        dv = d_ref[...]
        dy_ref[...] = (dv * gate_ref[...]).astype(BF16)
        sums = jnp.concatenate([jnp.sum(dv * y_ref[...], axis=0, keepdims=True), jnp.zeros((7, D_MODEL), F32)], axis=0)

        @pl.when(pl.program_id(0) == 0)
        def _():
            sums_ref[...] = jnp.zeros_like(sums_ref)

        sums_ref[...] += sums

    return pl.pallas_call(
        body, name=name, grid=(SEQ // tm,),
        in_specs=[_row_spec(tm, D_MODEL), _row_spec(tm, D_MODEL), _vec_spec(1, D_MODEL)],
        out_specs=[_row_spec(tm, D_MODEL), _vec_spec(8, D_MODEL)],
        out_shape=[jax.ShapeDtypeStruct((SEQ, D_MODEL), BF16), jax.ShapeDtypeStruct((8, D_MODEL), F32)],
        compiler_params=_params("arbitrary"),
    )(dx1, y, gate)


def _mm_nt_res(dy, w, name):
    tm = 256
    kc, n = w.shape

    def body(dy_ref, w_ref, o_ref):
        o_ref[...] = lax.dot_general(dy_ref[...], w_ref[...], (((1,), (1,)), ((), ())), preferred_element_type=F32)

    return pl.pallas_call(
        body, name=name, grid=(SEQ // tm,),
        in_specs=[_row_spec(tm, n), _vec_spec(kc, n)],
        out_specs=_row_spec(tm, kc),
        out_shape=jax.ShapeDtypeStruct((SEQ, kc), F32),
        compiler_params=_params("parallel"),
    )(dy, w)


def _seg_matrix():
    r = lax.broadcasted_iota(jnp.int32, (256, 256), 0) // HEAD_DIM
    c = lax.broadcasted_iota(jnp.int32, (256, 256), 1) // HEAD_DIM
    return (r == c).astype(BF16)


def _segsum(v, seg):
    hi = v.astype(BF16)
    lo = (v - hi.astype(F32)).astype(BF16)
    outs = []
    for c0 in range(0, D_MODEL, 256):
        outs.append(jnp.dot(hi[:, c0:c0 + 256], seg, preferred_element_type=F32)
                    + jnp.dot(lo[:, c0:c0 + 256], seg, preferred_element_type=F32))
    return jnp.concatenate(outs, axis=1)


def _qk_rstd(v, seg):
    return lax.rsqrt(_segsum(v * v, seg) * (1.0 / HEAD_DIM) + NORM_EPS)


def _qknorm_fwd(proj, qw, kw, seg, name):
    tm = 256

    def body(p_ref, qw_ref, kw_ref, seg_ref, q_ref, k_ref):
        segv = seg_ref[...]
        q = p_ref[:, :D_MODEL].astype(F32)
        k = p_ref[:, D_MODEL:].astype(F32)
        q_ref[...] = (q * _qk_rstd(q, segv) * qw_ref[...]).astype(BF16)
        k_ref[...] = (k * _qk_rstd(k, segv) * kw_ref[...]).astype(BF16)

    return pl.pallas_call(
        body, name=name, grid=(SEQ // tm,),
        in_specs=[_row_spec(tm, 2 * D_MODEL), _vec_spec(1, D_MODEL), _vec_spec(1, D_MODEL), _vec_spec(256, 256)],
        out_specs=[_row_spec(tm, D_MODEL)] * 2,
        out_shape=[jax.ShapeDtypeStruct((SEQ, D_MODEL), BF16)] * 2,
        compiler_params=_params("parallel"),
    )(proj, qw, kw, seg)


def _attn_masks(b, bpc, dilation, slope):
    if bpc == 1:
        qi = lax.broadcasted_iota(jnp.int32, (ATTN_BLOCK, ATTN_BLOCK), 0)
        kj = lax.broadcasted_iota(jnp.int32, (ATTN_BLOCK, ATTN_BLOCK), 1)
        steps = qi - kj
        return (steps * dilation).astype(F32), steps >= 0
    qi = lax.broadcasted_iota(jnp.int32, (ATTN_BLOCK, 2 * ATTN_BLOCK), 0)
    kj = lax.broadcasted_iota(jnp.int32, (ATTN_BLOCK, 2 * ATTN_BLOCK), 1)
    steps = qi + ATTN_BLOCK - kj
    has_prev = (b % bpc) != 0
    valid = (steps >= 0) & (steps <= ATTN_BLOCK) & (has_prev | (kj >= ATTN_BLOCK))
    return (steps * dilation).astype(F32), valid


def _key_tile(prev_ref, cur_ref, cols, bpc):
    if bpc == 1:
        return cur_ref[:, cols]
    return jnp.concatenate([prev_ref[:, cols], cur_ref[:, cols]], axis=0)


ATTN_HEADS_FWD = 8
ATTN_HEADS_BWD = 4
NT_DIMS = (((1,), (1,)), ((), ()))
TN_DIMS = (((0,), (0,)), ((), ()))
BATCH_NT_DIMS = (((2,), (2,)), ((0,), (0,)))
BATCH_NN_DIMS = (((2,), (1,)), ((0,), (0,)))
BATCH_TN_DIMS = (((1,), (1,)), ((0,), (0,)))


def _head_stack(tile_of, heads):
    return jnp.stack([tile_of(slice(h * HEAD_DIM, (h + 1) * HEAD_DIM)) for h in range(heads)], axis=0)


def _attn_specs(heads, segment=0):
    width = heads * HEAD_DIM
    off = segment * (D_MODEL // width)
    last = SEQ // ATTN_BLOCK - 1
    cur = pl.BlockSpec((ATTN_BLOCK, width), lambda hg, b: (jnp.minimum(b, last), hg + off))
    prev = pl.BlockSpec((ATTN_BLOCK, width), lambda hg, b: (jnp.clip(b - 1, 0, last), hg + off))
    return cur, prev


def _attn_fwd(q, k, proj, slopes, dilation, name):
    bpc = SEQ // dilation // ATTN_BLOCK
    heads = ATTN_HEADS_FWD
    cur, prev = _attn_specs(heads)
    v_cur, v_prev = _attn_specs(heads, segment=2)
    scale = HEAD_DIM ** -0.5

    def body(sl_ref, q_ref, kp_ref, kc_ref, vp_ref, vc_ref, o_ref, lse_ref):
        dist, valid = _attn_masks(pl.program_id(1), bpc, dilation, None)
        q3 = _head_stack(lambda cols: q_ref[:, cols], heads)
        k3 = _head_stack(lambda cols: _key_tile(kp_ref, kc_ref, cols, bpc), heads)
        v3 = _head_stack(lambda cols: _key_tile(vp_ref, vc_ref, cols, bpc), heads)
        s = lax.dot_general(q3, k3, BATCH_NT_DIMS, preferred_element_type=F32)
        s = jnp.where(valid[None], s * scale - dist[None] * sl_ref[...], NEG_INF)
        m = jnp.max(s, axis=-1, keepdims=True)
        p = jnp.exp(s - m)
        l = jnp.sum(p, axis=-1, keepdims=True)
        o3 = lax.dot_general(p.astype(BF16), v3, BATCH_NN_DIMS, preferred_element_type=F32) / l
        lse3 = m + jnp.log(l)
        for h in range(heads):
            cols = slice(h * HEAD_DIM, (h + 1) * HEAD_DIM)
            o_ref[:, cols] = o3[h]
            lse_ref[:, cols] = jnp.broadcast_to(lse3[h], (ATTN_BLOCK, HEAD_DIM))

    return pl.pallas_call(
        body, name=name, grid=(N_HEADS // heads, SEQ // ATTN_BLOCK),
        in_specs=[pl.BlockSpec((heads, 1, 1), lambda hg, b: (hg, 0, 0)), cur, prev, cur, v_prev, v_cur],
        out_specs=[cur, cur],
        out_shape=[jax.ShapeDtypeStruct((SEQ, D_MODEL), F32)] * 2,
        compiler_params=_params("parallel", "parallel"),
    )(slopes.reshape(N_HEADS, 1, 1), q, k, k, proj, proj)


def _class_spec(tm, dilation):
    if dilation == 1:
        return _row_spec(tm, D_MODEL)
    return pl.BlockSpec((dilation, tm // dilation, D_MODEL), lambda i: (0, i, 0))


def _class_shape(dilation, dtype):
    if dilation == 1:
        return jax.ShapeDtypeStruct((SEQ, D_MODEL), dtype)
    return jax.ShapeDtypeStruct((dilation, SEQ // dilation, D_MODEL), dtype)


def _load_natural(in_ref, nat_ref, dilation):
    if dilation == 1:
        return in_ref[...].astype(F32)
    n = nat_ref.shape[1] // dilation
    for r in range(dilation):
        for j in range(D_MODEL // LANES):
            nat_ref.at[j][pl.ds(r, n, stride=dilation), :] = in_ref[r, :, j * LANES:(j + 1) * LANES].astype(F32)
    return jnp.concatenate([nat_ref[j] for j in range(D_MODEL // LANES)], axis=1)


def _store_classes(out_ref, value, nat_ref, dilation):
    if dilation == 1:
        out_ref[...] = value.astype(out_ref.dtype)
        return
    n = nat_ref.shape[1] // dilation
    for j in range(D_MODEL // LANES):
        nat_ref[j] = value[:, j * LANES:(j + 1) * LANES]
    for r in range(dilation):
        for j in range(D_MODEL // LANES):
            out_ref[r, :, j * LANES:(j + 1) * LANES] = (
                nat_ref.at[j][pl.ds(r, n, stride=dilation), :].astype(out_ref.dtype))


def _natural_scratch(tm):
    return pltpu.VMEM((D_MODEL // LANES, tm, LANES), F32)


def _merge_fwd(o_parts, lse_parts, z, name):
    tm = 256

    def body(o0, o1, o2, l0, l1, l2, z_ref, u_ref, ut_ref, o_ref, lse_ref, nat):
        ls = [_load_natural(l, nat, d) for l, d in zip((l0, l1, l2), DILATIONS)]
        m = jnp.maximum(jnp.maximum(ls[0], ls[1]), ls[2])
        tot = m + jnp.log(jnp.exp(ls[0] - m) + jnp.exp(ls[1] - m) + jnp.exp(ls[2] - m))
        o = jnp.zeros((tm, D_MODEL), F32)
        for o_in, l, d in zip((o0, o1, o2), ls, DILATIONS):
            o = o + jnp.exp(l - tot) * _load_natural(o_in, nat, d)
        zv = z_ref[...]
        u = o * (zv * _sigmoid(zv))
        u_ref[...] = u.astype(BF16)
        ut_ref[...] = u.T.astype(BF16)
        o_ref[...] = o
        lse_ref[...] = tot

    return pl.pallas_call(
        body, name=name, grid=(SEQ // tm,),
        in_specs=[_class_spec(tm, d) for d in DILATIONS] * 2 + [_row_spec(tm, D_MODEL)],
        out_specs=[_row_spec(tm, D_MODEL), pl.BlockSpec((D_MODEL, tm), lambda i: (0, i)),
                   _row_spec(tm, D_MODEL), _row_spec(tm, D_MODEL)],
        out_shape=[jax.ShapeDtypeStruct((SEQ, D_MODEL), BF16), jax.ShapeDtypeStruct((D_MODEL, SEQ), BF16),
                   jax.ShapeDtypeStruct((SEQ, D_MODEL), F32), jax.ShapeDtypeStruct((SEQ, D_MODEL), F32)],
        scratch_shapes=[_natural_scratch(tm)],
        compiler_params=_params("parallel"),
    )(*o_parts, *lse_parts, z)


def _merge_bwd(du, o, lse, z, seg, name):
    tm = 256
    n_d = len(DILATIONS)
    assert DILATIONS[0] == 1

    def body(du_ref, o_ref, lse_ref, z_ref, seg_ref, dz_ref, *rest):
        do_refs, delta_refs, lse_refs = rest[:n_d], rest[n_d:2 * n_d], rest[2 * n_d:3 * n_d - 1]
        nat = rest[-1]
        zv = z_ref[...]
        sz = _sigmoid(zv)
        duv = du_ref[...]
        ov = o_ref[...]
        do = duv * (zv * sz)
        dz_ref[...] = (duv * ov * (sz * (1.0 + zv * (1.0 - sz)))).astype(BF16)
        delta = _segsum(do * ov, seg_ref[...])
        lv = lse_ref[...]
        for i, d in enumerate(DILATIONS):
            _store_classes(do_refs[i], do, nat, d)
            _store_classes(delta_refs[i], delta, nat, d)
            if i > 0:
                _store_classes(lse_refs[i - 1], lv, nat, d)

    res = pl.pallas_call(
        body, name=name, grid=(SEQ // tm,),
        in_specs=[_row_spec(tm, D_MODEL)] * 4 + [_vec_spec(256, 256)],
        out_specs=[_row_spec(tm, D_MODEL)] + [_class_spec(tm, d) for d in DILATIONS] * 2
        + [_class_spec(tm, d) for d in DILATIONS[1:]],
        out_shape=[jax.ShapeDtypeStruct((SEQ, D_MODEL), BF16)] + [_class_shape(d, BF16) for d in DILATIONS]
        + [_class_shape(d, F32) for d in DILATIONS] + [_class_shape(d, F32) for d in DILATIONS[1:]],
        scratch_shapes=[_natural_scratch(tm)],
        compiler_params=_params("parallel"),
    )(du, o, lse, z, seg)
    flat = lambda a: a.reshape(SEQ, D_MODEL)
    dz, dos, deltas, lses = res[0], res[1:1 + n_d], res[1 + n_d:1 + 2 * n_d], [lse] + list(res[1 + 2 * n_d:])
    return dz, [flat(a) for a in dos], [flat(a) for a in deltas], [flat(a) for a in lses]


def _attn_bwd(q, k, proj, do, lse, delta, slopes, dilation, name):
    bpc = SEQ // dilation // ATTN_BLOCK
    heads = ATTN_HEADS_BWD
    n_blocks = SEQ // ATTN_BLOCK
    carry = bpc > 1
    width = heads * HEAD_DIM
    cur, prev = _attn_specs(heads)
    v_cur, v_prev = _attn_specs(heads, segment=2)
    scale = HEAD_DIM ** -0.5

    def body(sl_ref, q_ref, kp_ref, kc_ref, vp_ref, vc_ref, do_ref, lse_ref, dl_ref,
             dq_ref, dk_ref, dv_ref, *scratch):
        b = pl.program_id(1)
        if carry:
            dk_carry, dv_carry = scratch

            @pl.when(b == n_blocks)
            def _():
                dk_ref[...] = dk_carry[...].astype(BF16)
                dv_ref[...] = dv_carry[...].astype(BF16)

            @pl.when(b < n_blocks)
            def _():
                step(sl_ref, q_ref, kp_ref, kc_ref, vp_ref, vc_ref, do_ref, lse_ref, dl_ref,
                     dq_ref, dk_ref, dv_ref, dk_carry, dv_carry, b)
        else:
            step(sl_ref, q_ref, kp_ref, kc_ref, vp_ref, vc_ref, do_ref, lse_ref, dl_ref,
                 dq_ref, dk_ref, dv_ref, None, None, b)

    def step(sl_ref, q_ref, kp_ref, kc_ref, vp_ref, vc_ref, do_ref, lse_ref, dl_ref,
             dq_ref, dk_ref, dv_ref, dk_carry, dv_carry, b):
        if carry:
            @pl.when(b == 0)
            def _():
                dk_carry[...] = jnp.zeros_like(dk_carry)
                dv_carry[...] = jnp.zeros_like(dv_carry)

        dist, valid = _attn_masks(b, bpc, dilation, None)
        q3 = _head_stack(lambda cols: q_ref[:, cols], heads)
        k3 = _head_stack(lambda cols: _key_tile(kp_ref, kc_ref, cols, bpc), heads)
        v3 = _head_stack(lambda cols: _key_tile(vp_ref, vc_ref, cols, bpc), heads)
        do3 = _head_stack(lambda cols: do_ref[:, cols], heads)
        lse3 = _head_stack(lambda cols: lse_ref[:, cols.start:cols.start + 1], heads)
        dl3 = _head_stack(lambda cols: dl_ref[:, cols.start:cols.start + 1], heads)
        s = lax.dot_general(q3, k3, BATCH_NT_DIMS, preferred_element_type=F32)
        p = jnp.exp(jnp.where(valid[None], s * scale - dist[None] * sl_ref[...], NEG_INF) - lse3)
        dp = lax.dot_general(do3, v3, BATCH_NT_DIMS, preferred_element_type=F32)
        ds = (p * (dp - dl3) * scale).astype(BF16)
        dq3 = lax.dot_general(ds, k3, BATCH_NN_DIMS, preferred_element_type=F32)
        dk3 = lax.dot_general(ds, q3, BATCH_TN_DIMS, preferred_element_type=F32)
        dv3 = lax.dot_general(p.astype(BF16), do3, BATCH_TN_DIMS, preferred_element_type=F32)
        for h in range(heads):
            cols = slice(h * HEAD_DIM, (h + 1) * HEAD_DIM)
            dq_ref[:, cols] = dq3[h].astype(BF16)
            if carry:
                dk_ref[:, cols] = (dk_carry[:, cols] + dk3[h, :ATTN_BLOCK]).astype(BF16)
                dv_ref[:, cols] = (dv_carry[:, cols] + dv3[h, :ATTN_BLOCK]).astype(BF16)
                dk_carry[:, cols] = dk3[h, ATTN_BLOCK:]
                dv_carry[:, cols] = dv3[h, ATTN_BLOCK:]
            else:
                dk_ref[:, cols] = dk3[h].astype(BF16)
                dv_ref[:, cols] = dv3[h].astype(BF16)

    kv_out = prev if carry else cur
    return pl.pallas_call(
        body, name=name, grid=(N_HEADS // heads, n_blocks + (1 if carry else 0)),
        in_specs=[pl.BlockSpec((heads, 1, 1), lambda hg, b: (hg, 0, 0)), cur, prev, cur, v_prev, v_cur,
                  cur, cur, cur],
        out_specs=[cur, kv_out, kv_out],
        out_shape=[jax.ShapeDtypeStruct((SEQ, D_MODEL), BF16)] * 3,
        scratch_shapes=[pltpu.VMEM((ATTN_BLOCK, width), F32)] * 2 if carry else [],
        compiler_params=_params("parallel", "arbitrary"),
    )(slopes.reshape(N_HEADS, 1, 1), q, k, k, proj, proj, do, lse, delta)


def _qknorm_bwd(proj, qw, kw, seg, dq, dk, dv, name):
    tm = 256

    def body(p_ref, qw_ref, kw_ref, seg_ref, dq_ref, dk_ref, dv_ref, dproj_ref, sums_ref):
        segv = seg_ref[...]
        sums = []
        for part, (w_ref, dn_ref) in enumerate(((qw_ref, dq_ref), (kw_ref, dk_ref))):
            raw = p_ref[:, part * D_MODEL:(part + 1) * D_MODEL].astype(F32)
            dn = dn_ref[...].astype(F32)
            r = _qk_rstd(raw, segv)
            gq = dn * w_ref[...]
            draw = r * gq - raw * (r * r * r) * (_segsum(raw * gq, segv) * (1.0 / HEAD_DIM))
            dproj_ref[:, part * D_MODEL:(part + 1) * D_MODEL] = draw.astype(BF16)
            sums.append(jnp.sum(dn * raw * r, axis=0, keepdims=True))
        dproj_ref[:, 2 * D_MODEL:] = dv_ref[...]

        @pl.when(pl.program_id(0) == 0)
        def _():
            sums_ref[...] = jnp.zeros_like(sums_ref)

        sums_ref[...] += jnp.concatenate(sums + [jnp.zeros((6, D_MODEL), F32)], axis=0)

    return pl.pallas_call(
        body, name=name, grid=(SEQ // tm,),
        in_specs=[_row_spec(tm, 3 * D_MODEL), _vec_spec(1, D_MODEL), _vec_spec(1, D_MODEL), _vec_spec(256, 256)]
        + [_row_spec(tm, D_MODEL)] * 3,
        out_specs=[_row_spec(tm, 3 * D_MODEL), _vec_spec(8, D_MODEL)],
        out_shape=[jax.ShapeDtypeStruct((SEQ, 3 * D_MODEL), BF16), jax.ShapeDtypeStruct((8, D_MODEL), F32)],
        compiler_params=_params("arbitrary"),
    )(proj, qw, kw, seg, dq, dk, dv)


def _to_classes(a, dilation):
    if dilation == 1:
        return a
    s, c = a.shape
    return a.reshape(s // dilation, dilation, c).transpose(1, 0, 2).reshape(s, c)


def _from_classes(a, dilation):
    if dilation == 1:
        return a
    s, c = a.shape
    return a.reshape(dilation, s // dilation, c).transpose(1, 0, 2).reshape(s, c)


def _cols_to_classes(a, dilation):
    if dilation == 1:
        return a
    r, s = a.shape
    return a.reshape(r, s // dilation, dilation).transpose(0, 2, 1).reshape(r, s)


B_TN = 512
B_GROUP_TILES = 3 * D_MODEL // B_TN
B_Z_TILE0 = 3 * B_GROUP_TILES
B_Z_TILES = D_MODEL // B_TN


def _local_step(x, target, mods, norm_g, conv_w, conv_b, ln_g, ln_b, q_norm, k_norm,
                weights_a, weights_b, forward_weights_b, send_grads_b, forward_grads_b, send_grads_a):
    row = lambda a, i: a[i:i + 1]
    shift0, scale0, gate0 = row(mods[0], 0), row(mods[0], 1), row(mods[0], 2)
    shift1, scale1, gate1 = row(mods[1], 0), row(mods[1], 1), row(mods[1], 2)
    g0, g1 = row(norm_g, 0), row(norm_g, 1)
    seg = _seg_matrix()
    slopes = jnp.exp2(-8.0 * jnp.arange(1, N_HEADS + 1, dtype=F32) / N_HEADS)
    qw = [jnp.tile(q_norm[g:g + 1], (1, N_HEADS)) for g in range(3)]
    kw = [jnp.tile(k_norm[g:g + 1], (1, N_HEADS)) for g in range(3)]

    h0, h0t = _normmod_fwd(x, g0, scale0, shift0, "prenorm0")
    wa_in, wa_out = weights_a(h0)
    ja, _, nsa = wa_in.shape
    proj_a = _mm(h0, wa_in, tn=nsa, tile0=0, n_tiles=ja, out_dtype=F32, name="a_in")
    u5, u5t, u2 = _conv_fwd(proj_a, conv_w, conv_b, ln_g, ln_b, "a_conv")
    token = forward_weights_b(u5)
    x1, y_a, h1t, h1c = _out_a(u5, wa_out, x, gate0 + token[0:1, 0:1], g1, scale1, shift1, "a_out")

    wb_in, wb_out = weights_b(x1)
    jb, _, nsb = wb_in.shape
    h1 = h1c[0]
    h1tc = [_cols_to_classes(h1t, d) for d in DILATIONS]
    z_b = _mm(h1, wb_in, tn=B_TN, tile0=B_Z_TILE0, n_tiles=B_Z_TILES, out_dtype=F32, name="b_in_z")
    proj_g, qkv, o_parts, lse_parts = [], [], [], []
    for g, d in enumerate(DILATIONS):
        pg = _mm(h1c[g], wb_in, tn=B_TN, tile0=g * B_GROUP_TILES, n_tiles=B_GROUP_TILES, out_dtype=BF16,
                 name=f"b_in_g{g}")
        qn, kn = _qknorm_fwd(pg, qw[g], kw[g], seg, f"b_qknorm_g{g}")
        og, lg = _attn_fwd(qn, kn, pg, slopes, d, f"b_attn_g{g}")
        proj_g.append(pg)
        qkv.append((qn, kn))
        classes = (lambda a: a) if d == 1 else (lambda a: a.reshape(d, SEQ // d, D_MODEL))
        o_parts.append(classes(og))
        lse_parts.append(classes(lg))
    u_b, u_bt, o_b, lse_b = _merge_fwd(o_parts, lse_parts, z_b, "b_merge")
    e, dy_b, sums_loss = _out_b_loss(u_b, wb_out, x1, gate1, target, "b_out_loss")

    dwb_out = _mm(u_bt, dy_b, tn=D_MODEL, tile0=0, n_tiles=1, out_dtype=BF16, name="b_dwout")
    du_b = _mm_nt_res(dy_b, wb_out, "b_dout")
    dz_b, do_c, delta_c, lse_c = _merge_bwd(du_b, o_b, lse_b, z_b, seg, "b_merge_bwd")
    dwb_in = _mm(h1t, dz_b, tn=B_TN, tile0=B_Z_TILE0, n_tiles=B_Z_TILES, out_dtype=BF16, name="b_dwin_z",
                 out3d=(jb, nsb))
    dh1_parts = [_mm_nt(dz_b, wb_in, tn=B_TN, tile0=B_Z_TILE0, n_tiles=B_Z_TILES, name="b_dh_z")]
    qk_sums = []
    for g, d in enumerate(DILATIONS):
        qn, kn = qkv[g]
        dq, dk, dv = _attn_bwd(qn, kn, proj_g[g], do_c[g], lse_c[g], delta_c[g], slopes, d, f"b_attn_bwd_g{g}")
        dproj, sums_qk = _qknorm_bwd(proj_g[g], qw[g], kw[g], seg, dq, dk, dv, f"b_qknorm_bwd_g{g}")
        qk_sums.append(sums_qk)
        dwb_in = _mm(h1tc[g], dproj, tn=B_TN, tile0=g * B_GROUP_TILES, n_tiles=B_GROUP_TILES, out_dtype=BF16,
                     name=f"b_dwin_g{g}", out3d=(jb, nsb), prev=dwb_in)
        dh = _mm_nt(dproj, wb_in, tn=B_TN, tile0=g * B_GROUP_TILES, n_tiles=B_GROUP_TILES, name=f"b_dh_g{g}")
        dh1_parts.append(dh)
    token = send_grads_b(dwb_in, dwb_out)
    dx1, sums_n1 = _normmod_bwd(x1, g1, scale1 + token[0:1, 0:1], dh1_parts, e, "prenorm1_bwd",
                                part_dilations=(1,) + DILATIONS)
    token = forward_grads_b(dx1)

    dy_a, sums_ga = _dgate_dy(dx1, y_a, gate0 + token[0:1, 0:1], "a_dgate")
    dwa_out = _mm(u5t, dy_a, tn=D_MODEL, tile0=0, n_tiles=1, out_dtype=BF16, name="a_dwout")
    du5 = _mm_nt_res(dy_a, wa_out, "a_dout")
    du2, dz_a, sums_ln = _conv_bwd_pointwise(du5, proj_a, u2, ln_g, ln_b, "a_conv_bwd_pw")
    dproj_a, dconv_w = _conv_bwd_taps(du2, dz_a, proj_a, conv_w, "a_conv_bwd_taps")
    dwa_in = _mm(h0t, dproj_a, tn=nsa, tile0=0, n_tiles=ja, out_dtype=BF16, name="a_dwin", out3d=(ja, nsa))
    token = send_grads_a(dwa_in, dwa_out)
    dh0 = _mm_nt(dproj_a, wa_in, tn=nsa, tile0=0, n_tiles=ja, name="a_dh", after=token)
    grad_x, sums_n0 = _normmod_bwd(x, g0, scale0, [dh0], dx1, "prenorm0_bwd")

    small = dict(
        dnorm_g=jnp.concatenate([sums_n0[0:1], sums_n1[0:1]], axis=0),
        dmod0=jnp.concatenate([sums_n0[2:3], sums_n0[1:2], sums_ga[0:1]], axis=0),
        dmod1=jnp.concatenate([sums_n1[2:3], sums_n1[1:2], sums_loss[0:1]], axis=0),
        dln_g=sums_ln[0:1], dln_b=sums_ln[1:2], dconv_b=sums_ln[2:3],
        dconv_w=dconv_w[:CONV_WIDTH],
        dq_norm=jnp.concatenate([s[0:1] for s in qk_sums], axis=0),
        dk_norm=jnp.concatenate([s[1:2] for s in qk_sums], axis=0),
        loss_cols=sums_loss[1:2],
    )
    return grad_x, small


def _adamw(w, g, m, v, name):
    rows, cols = w.shape
    tr = rows if rows <= 128 else 128
    c1 = 1.0 / (1.0 - ADAM_B1 ** ADAM_STEP)
    c2 = 1.0 / (1.0 - ADAM_B2 ** ADAM_STEP)

    def body(w_ref, g_ref, m_ref, v_ref, d_ref, mo_ref, vo_ref):
        gv = g_ref[...]
        mn = ADAM_B1 * m_ref[...] + (1.0 - ADAM_B1) * gv
        vn = ADAM_B2 * v_ref[...] + (1.0 - ADAM_B2) * (gv * gv)
        mo_ref[...] = mn
        vo_ref[...] = vn
        d_ref[...] = -ADAM_LR * ((mn * c1) / (jnp.sqrt(vn * c2) + ADAM_EPS) + ADAM_WD * w_ref[...])

    spec = pl.BlockSpec((tr, cols), lambda i: (i, 0))
    return pl.pallas_call(
        body, name=name, grid=(rows // tr,), in_specs=[spec] * 4, out_specs=[spec] * 3,
        out_shape=[jax.ShapeDtypeStruct((rows, cols), F32)] * 3,
        compiler_params=_params("parallel"),
    )(w, g, m, v)


def _cast_into_slot(w, chip_idx, name):
    rows, cols = w.shape
    tr = 256

    def body(ch_ref, w_ref, o_ref):
        o_ref[...] = w_ref[...].astype(BF16)

    return pl.pallas_call(
        body, name=name,
        grid_spec=pltpu.PrefetchScalarGridSpec(
            num_scalar_prefetch=1, grid=(rows // tr,),
            in_specs=[pl.BlockSpec((tr, cols), lambda i, ch: (i, 0))],
            out_specs=pl.BlockSpec((None, tr, cols), lambda i, ch: (ch[0], i, 0))),
        out_shape=jax.ShapeDtypeStruct((N_CHIPS, rows, cols), BF16), compiler_params=_params("parallel"),
    )(chip_idx, w)


def _position():
    x, y, c = lax.axis_index("x"), lax.axis_index("y"), lax.axis_index("c")
    return x, y, c


def _xor_peer(x, y, c, k):
    return (x ^ ((k >> 2) & 1), y ^ ((k >> 1) & 1), c ^ (k & 1))


def _chip_peer(x, y, k):
    return (x ^ ((k >> 1) & 1), y ^ (k & 1))


def _ada_forward(c_row, ada_w, ada_b, conv_w):
    ns = ada_w.shape[2]
    cw = conv_w.shape[1]

    def body(c_ref, w_ref, b_ref, cv_ref, mod_ref, sc_ref, cvo_ref,
             c_all, mp, parts, cv_parts, send1, recv1, send2, recv2, send3, recv3):
        x, y, c = _position()
        me = 4 * x + 2 * y + c
        chip = 2 * x + y

        def c_copy(k):
            return pltpu.make_async_remote_copy(
                src_ref=c_all.at[me], dst_ref=c_all.at[me], send_sem=send1.at[k - 1], recv_sem=recv1.at[k - 1],
                device_id=_xor_peer(x, y, c, k), device_id_type=MESH)

        def cv_copy(k):
            px, py = _chip_peer(x, y, k)
            return pltpu.make_async_remote_copy(
                src_ref=cv_parts.at[chip], dst_ref=cv_parts.at[chip], send_sem=send3.at[k - 1],
                recv_sem=recv3.at[k - 1], device_id=(px, py, c), device_id_type=MESH)

        c_all[me] = c_ref[...]
        cv_parts[chip] = cv_ref[...]
        for k in range(1, N_DEV):
            c_copy(k).start()
        for k in range(1, N_CHIPS):
            cv_copy(k).start()
        for k in range(1, N_DEV):
            c_copy(k).wait_recv()
        cv = jnp.concatenate([c_all[i] for i in range(N_DEV)], axis=0)
        sc = cv * _sigmoid(cv)
        sc_ref[...] = sc
        for l in range(2):
            res = jnp.dot(sc, w_ref[l], preferred_element_type=F32, precision=lax.Precision.HIGHEST)
            for i in range(N_DEV):
                mp[i, l:l + 1, :] = res[i:i + 1, :]

        def mod_copy(k):
            px, py = _chip_peer(x, y, k)
            return pltpu.make_async_remote_copy(
                src_ref=mp.at[4 * px + 2 * py + c], dst_ref=parts.at[chip], send_sem=send2.at[k - 1],
                recv_sem=recv2.at[k - 1], device_id=(px, py, c), device_id_type=MESH)

        for k in range(1, N_CHIPS):
            mod_copy(k).start()
        parts[chip] = mp[me]
        for k in range(1, N_CHIPS):
            mod_copy(k).wait_recv()
            cv_copy(k).wait_recv()
        mod_ref[...] = jnp.concatenate([parts[j] for j in range(N_CHIPS)], axis=1) + b_ref[...]
        cvo_ref[...] = jnp.concatenate([cv_parts[j] for j in range(N_CHIPS)], axis=1)
        for k in range(1, N_DEV):
            c_copy(k).wait_send()
        for k in range(1, N_CHIPS):
            mod_copy(k).wait_send()
            cv_copy(k).wait_send()

    vm = pl.BlockSpec(memory_space=pltpu.VMEM)
    return pl.pallas_call(
        body, name="ada_forward",
        in_specs=[vm] * 4, out_specs=[vm] * 3,
        out_shape=[jax.ShapeDtypeStruct((2, 3 * D_MODEL), F32), jax.ShapeDtypeStruct((N_DEV, D_MODEL), F32),
                   jax.ShapeDtypeStruct((CONV_WIDTH, N_CHIPS * cw), F32)],
        scratch_shapes=[pltpu.VMEM((N_DEV, 1, D_MODEL), F32), pltpu.VMEM((N_DEV, 2, ns), F32),
                        pltpu.VMEM((N_CHIPS, 2, ns), F32), pltpu.VMEM((N_CHIPS, CONV_WIDTH, cw), F32),
                        pltpu.SemaphoreType.DMA((N_DEV - 1,)), pltpu.SemaphoreType.DMA((N_DEV - 1,)),
                        pltpu.SemaphoreType.DMA((N_CHIPS - 1,)), pltpu.SemaphoreType.DMA((N_CHIPS - 1,)),
                        pltpu.SemaphoreType.DMA((N_CHIPS - 1,)), pltpu.SemaphoreType.DMA((N_CHIPS - 1,))],
        compiler_params=pltpu.CompilerParams(vmem_limit_bytes=VMEM_LIMIT_BYTES),
    )(c_row, ada_w, ada_b, conv_w)


HBM_SPEC = pl.BlockSpec(memory_space=pltpu.HBM)
ANY_SPEC = pl.BlockSpec(memory_space=pl.ANY)
SEM_SPEC = pl.BlockSpec(memory_space=pltpu.SEMAPHORE)
SPLIT_PARAMS = dict(compiler_params=pltpu.CompilerParams(has_side_effects=pltpu.SideEffectType.DATAFLOW_SIDE_EFFECTING))
TOKEN = jax.ShapeDtypeStruct((8, 128), F32)


def _hbm(arrays):
    return [pltpu.with_memory_space_constraint(a, pltpu.HBM) for a in arrays]


def _hbm_like(arrays):
    return [pltpu.HBM(a.shape, a.dtype) for a in arrays]


def _gather_start(lands, after, name):
    n = len(lands)

    def body(*refs):
        ins = refs[:n]
        send, recv = refs[n + 1], refs[n + 2]
        x, y, c = _position()
        chip = 2 * x + y
        for t in range(n):
            rh = ins[t].shape[1] // 2
            for k in range(1, N_CHIPS):
                px, py = _chip_peer(x, y, k)
                block = ins[t].at[chip, pl.ds(c * rh, rh)]
                pltpu.make_async_remote_copy(
                    src_ref=block, dst_ref=block, send_sem=send.at[3 * t + k - 1], recv_sem=recv.at[3 * t + k - 1],
                    device_id=(px, py, c), device_id_type=MESH).start()
        refs[-1][...] = jnp.zeros(TOKEN.shape, F32)

    res = pl.pallas_call(
        body, name=name, in_specs=[HBM_SPEC] * n + [ANY_SPEC],
        out_specs=(SEM_SPEC, SEM_SPEC, *[HBM_SPEC] * n, pl.BlockSpec(memory_space=pltpu.VMEM)),
        out_shape=(pltpu.SemaphoreType.DMA((3 * n,)), pltpu.SemaphoreType.DMA((3 * n,)), *_hbm_like(lands), TOKEN),
        input_output_aliases={t: 2 + t for t in range(n)}, **SPLIT_PARAMS,
    )(*_hbm(lands), after)
    return res[0], res[1], list(res[2:2 + n]), res[-1]


def _gather_forward(send, recv, lands, after, name):
    n = len(lands)

    def body(*refs):
        ins = refs[:n]
        send1, recv1 = refs[n], refs[n + 1]
        send2, recv2 = refs[n + 3], refs[n + 4]
        x, y, c = _position()
        chip = 2 * x + y
        for t in range(n):
            rh = ins[t].shape[1] // 2
            half = pl.ds(c * rh, rh)
            for k in range(1, N_CHIPS):
                px, py = _chip_peer(x, y, k)
                s = 3 * t + k - 1
                got = ins[t].at[2 * px + py, half]
                cp = pltpu.make_async_remote_copy(
                    src_ref=ins[t].at[chip, half], dst_ref=got, send_sem=send1.at[s], recv_sem=recv1.at[s],
                    device_id=(px, py, c), device_id_type=MESH)
                cp.wait_send()
                cp.wait_recv()
                pltpu.make_async_remote_copy(
                    src_ref=got, dst_ref=got, send_sem=send2.at[s], recv_sem=recv2.at[s],
                    device_id=(x, y, 1 - c), device_id_type=MESH).start()
        refs[-1][...] = jnp.zeros(TOKEN.shape, F32)

    res = pl.pallas_call(
        body, name=name, in_specs=[HBM_SPEC] * n + [SEM_SPEC, SEM_SPEC, ANY_SPEC],
        out_specs=(SEM_SPEC, SEM_SPEC, *[HBM_SPEC] * n, pl.BlockSpec(memory_space=pltpu.VMEM)),
        out_shape=(pltpu.SemaphoreType.DMA((3 * n,)), pltpu.SemaphoreType.DMA((3 * n,)), *_hbm_like(lands), TOKEN),
        input_output_aliases={t: 2 + t for t in range(n)}, **SPLIT_PARAMS,
    )(*lands, send, recv, after)
    return res[0], res[1], list(res[2:2 + n]), res[-1]


def _gather_wait(send, recv, lands, after, name):
    n = len(lands)

    def body(*refs):
        ins = refs[:n]
        send_ref, recv_ref = refs[n], refs[n + 1]
        x, y, c = _position()
        for t in range(n):
            rh = ins[t].shape[1] // 2
            for k in range(1, N_CHIPS):
                px, py = _chip_peer(x, y, k)
                cp = pltpu.make_async_remote_copy(
                    src_ref=ins[t].at[2 * px + py, pl.ds(c * rh, rh)],
                    dst_ref=ins[t].at[2 * px + py, pl.ds((1 - c) * rh, rh)], send_sem=send_ref.at[3 * t + k - 1],
                    recv_sem=recv_ref.at[3 * t + k - 1], device_id=(x, y, 1 - c), device_id_type=MESH)
                cp.wait_send()
                cp.wait_recv()

    res = pl.pallas_call(
        body, name=name, in_specs=[HBM_SPEC] * n + [SEM_SPEC, SEM_SPEC, ANY_SPEC], out_specs=[HBM_SPEC] * n,
        out_shape=_hbm_like(lands), input_output_aliases={t: t for t in range(n)}, **SPLIT_PARAMS,
    )(*lands, send, recv, after)
    return list(res)


def _reduce_start(grads, after, name):
    n = len(grads)
    lands = [lax.empty((N_DEV, g.shape[1] // 2, g.shape[2]), BF16) for g in grads]

    def body(*refs):
        gs, ls = refs[:n], refs[n:2 * n]
        send, recv = refs[2 * n + 1], refs[2 * n + 2]
        x, y, c = _position()
        me = 4 * x + 2 * y + c
        for t in range(n):
            rh = gs[t].shape[1] // 2
            for k in range(1, N_DEV):
                px, py, pc = _xor_peer(x, y, c, k)
                pltpu.make_async_remote_copy(
                    src_ref=gs[t].at[2 * px + py, pl.ds(pc * rh, rh)], dst_ref=ls[t].at[me],
                    send_sem=send.at[7 * t + k - 1], recv_sem=recv.at[7 * t + k - 1],
                    device_id=(px, py, pc), device_id_type=MESH).start()
        refs[-1][...] = jnp.zeros(TOKEN.shape, F32)

    res = pl.pallas_call(
        body, name=name, in_specs=[HBM_SPEC] * (2 * n) + [ANY_SPEC],
        out_specs=(SEM_SPEC, SEM_SPEC, *[HBM_SPEC] * (2 * n), pl.BlockSpec(memory_space=pltpu.VMEM)),
        out_shape=(pltpu.SemaphoreType.DMA((7 * n,)), pltpu.SemaphoreType.DMA((7 * n,)),
                   *_hbm_like(grads), *_hbm_like(lands), TOKEN),
        input_output_aliases={t: 2 + t for t in range(2 * n)}, **SPLIT_PARAMS,
    )(*_hbm(grads), *_hbm(lands), after)
    return res[0], res[1], list(res[2:2 + n]), list(res[2 + n:2 + 2 * n]), res[-1]


def _reduce_wait(send, recv, grads, lands, after, name):
    n = len(grads)

    def body(*refs):
        gs, ls = refs[:n], refs[n:2 * n]
        send_ref, recv_ref = refs[2 * n], refs[2 * n + 1]
        x, y, c = _position()
        for t in range(n):
            rh = gs[t].shape[1] // 2
            for k in range(1, N_DEV):
                px, py, pc = _xor_peer(x, y, c, k)
                cp = pltpu.make_async_remote_copy(
                    src_ref=gs[t].at[2 * px + py, pl.ds(pc * rh, rh)], dst_ref=ls[t].at[4 * px + 2 * py + pc],
                    send_sem=send_ref.at[7 * t + k - 1], recv_sem=recv_ref.at[7 * t + k - 1],
                    device_id=(px, py, pc), device_id_type=MESH)
                cp.wait_send()
                cp.wait_recv()

    res = pl.pallas_call(
        body, name=name, in_specs=[HBM_SPEC] * (2 * n) + [SEM_SPEC, SEM_SPEC, ANY_SPEC], out_specs=[HBM_SPEC] * (2 * n),
        out_shape=_hbm_like(grads) + _hbm_like(lands), input_output_aliases={t: t for t in range(2 * n)}, **SPLIT_PARAMS,
    )(*grads, *lands, send, recv, after)
    return list(res[:n]), list(res[n:])


def _sum_devices(land, grad, dev_idx, name):
    _, rh, cols = land.shape
    tr = 128
    nb = rh // tr

    def body(idx_ref, l_ref, g_ref, o_ref):
        me = idx_ref[0]
        acc = jnp.where(me == 0, g_ref[...], l_ref[0]).astype(F32)
        for d in range(1, N_DEV):
            acc = acc + jnp.where(me == d, g_ref[...], l_ref[d]).astype(F32)
        o_ref[...] = acc

    return pl.pallas_call(
        body, name=name,
        grid_spec=pltpu.PrefetchScalarGridSpec(
            num_scalar_prefetch=1, grid=(nb,),
            in_specs=[pl.BlockSpec((N_DEV, tr, cols), lambda i, idx: (0, i, 0)),
                      pl.BlockSpec((None, tr, cols), lambda i, idx: (idx[1], idx[2] * nb + i, 0))],
            out_specs=pl.BlockSpec((tr, cols), lambda i, idx: (idx[2] * nb + i, 0))),
        out_shape=jax.ShapeDtypeStruct((2 * rh, cols), F32), compiler_params=_params("parallel"),
    )(dev_idx, land, grad)


def _split_start(name, arrays, n_sems, after, issue):
    m = len(arrays)

    def body(*refs):
        issue(refs[:m], refs[m + 1], refs[m + 2])
        refs[-1][...] = jnp.zeros(TOKEN.shape, F32)

    res = pl.pallas_call(
        body, name=name, in_specs=[HBM_SPEC] * m + [ANY_SPEC],
        out_specs=(SEM_SPEC, SEM_SPEC, *[HBM_SPEC] * m, pl.BlockSpec(memory_space=pltpu.VMEM)),
        out_shape=(pltpu.SemaphoreType.DMA((n_sems,)), pltpu.SemaphoreType.DMA((n_sems,)), *_hbm_like(arrays), TOKEN),
        input_output_aliases={t: 2 + t for t in range(m)}, **SPLIT_PARAMS,
    )(*_hbm(arrays), after)
    return res[0], res[1], list(res[2:2 + m]), res[-1]


def _split_wait(name, arrays, send, recv, after, await_all):
    m = len(arrays)

    def body(*refs):
        await_all(refs[:m], refs[m], refs[m + 1])

    res = pl.pallas_call(
        body, name=name, in_specs=[HBM_SPEC] * m + [SEM_SPEC, SEM_SPEC, ANY_SPEC], out_specs=[HBM_SPEC] * m,
        out_shape=_hbm_like(arrays), input_output_aliases={t: t for t in range(m)}, **SPLIT_PARAMS,
    )(*arrays, send, recv, after)
    return list(res)


def _sibling_copies(refs, send, recv, n):
    x, y, c = _position()
    cps = []
    for t in range(n):
        rh = refs[t].shape[1] // 2
        cps.append(pltpu.make_async_remote_copy(
            src_ref=refs[t].at[pl.ds(0, N_CHIPS), pl.ds((1 - c) * rh, rh)], dst_ref=refs[n + t],
            send_sem=send.at[t], recv_sem=recv.at[t], device_id=(x, y, 1 - c), device_id_type=MESH))
    return cps


def _reduce_sibling_start(grads, after, name):
    n = len(grads)
    lands = [lax.empty((N_CHIPS, g.shape[1] // 2, g.shape[2]), BF16) for g in grads]

    def issue(refs, send, recv):
        for cp in _sibling_copies(refs, send, recv, n):
            cp.start()

    return _split_start(name, list(grads) + lands, n, after, issue)


def _reduce_sibling_wait(send, recv, arrays, after, name):
    n = len(arrays) // 2

    def await_all(refs, send_ref, recv_ref):
        for cp in _sibling_copies(refs, send_ref, recv_ref, n):
            cp.wait_send()
            cp.wait_recv()

    res = _split_wait(name, arrays, send, recv, after, await_all)
    return res[:n], res[n:]


def _add_sibling_half(grad, got, dev_idx, name):
    j, r, cols = grad.shape
    rh = r // 2
    tr = 128
    nb = rh // tr

    def body(idx_ref, g_ref, got_ref, out_ref):
        out_ref[...] = (g_ref[...].astype(F32) + got_ref[...].astype(F32)).astype(BF16)

    return pl.pallas_call(
        body, name=name,
        grid_spec=pltpu.PrefetchScalarGridSpec(
            num_scalar_prefetch=1, grid=(j, nb),
            in_specs=[pl.BlockSpec((None, tr, cols), lambda jj, i, idx: (jj, idx[2] * nb + i, 0)),
                      pl.BlockSpec((None, tr, cols), lambda jj, i, idx: (jj, i, 0))],
            out_specs=pl.BlockSpec((None, tr, cols), lambda jj, i, idx: (jj, i, 0))),
        out_shape=jax.ShapeDtypeStruct((j, rh, cols), BF16),
        compiler_params=_params("parallel", "parallel"),
    )(dev_idx, grad, got)


def _chip_copies(refs, send, recv, n, receiving):
    x, y, c = _position()
    chip = 2 * x + y
    cps = []
    for t in range(n):
        for k in range(1, N_CHIPS):
            px, py = _chip_peer(x, y, k)
            cps.append(pltpu.make_async_remote_copy(
                src_ref=refs[t].at[2 * px + py], dst_ref=refs[n + t].at[2 * px + py if receiving else chip],
                send_sem=send.at[3 * t + k - 1], recv_sem=recv.at[3 * t + k - 1],
                device_id=(px, py, c), device_id_type=MESH))
    return cps


def _reduce_chips_start(partials, after, name):
    n = len(partials)
    lands = [lax.empty(p.shape, BF16) for p in partials]

    def issue(refs, send, recv):
        for cp in _chip_copies(refs, send, recv, n, False):
            cp.start()

    return _split_start(name, list(partials) + lands, 3 * n, after, issue)


def _reduce_chips_wait(send, recv, arrays, after, name):
    n = len(arrays) // 2

    def await_all(refs, send_ref, recv_ref):
        for cp in _chip_copies(refs, send_ref, recv_ref, n, True):
            cp.wait_send()
            cp.wait_recv()

    res = _split_wait(name, arrays, send, recv, after, await_all)
    return res[:n], res[n:]


def _sum_partials(land, partial, dev_idx, name):
    _, rh, cols = land.shape
    tr = 128
    nb = rh // tr

    def body(idx_ref, l_ref, p_ref, o_ref):
        chip = idx_ref[1]
        acc = jnp.where(chip == 0, p_ref[...], l_ref[0]).astype(F32)
        for s in range(1, N_CHIPS):
            acc = acc + jnp.where(chip == s, p_ref[...], l_ref[s]).astype(F32)
        o_ref[...] = acc

    return pl.pallas_call(
        body, name=name,
        grid_spec=pltpu.PrefetchScalarGridSpec(
            num_scalar_prefetch=1, grid=(nb,),
            in_specs=[pl.BlockSpec((N_CHIPS, tr, cols), lambda i, idx: (0, i, 0)),
                      pl.BlockSpec((None, tr, cols), lambda i, idx: (idx[1], i, 0))],
            out_specs=pl.BlockSpec((tr, cols), lambda i, idx: (idx[2] * nb + i, 0))),
        out_shape=jax.ShapeDtypeStruct((2 * rh, cols), F32), compiler_params=_params("parallel"),
    )(dev_idx, land, partial)


def _share_halves(totals):
    n = len(totals)

    def body(*refs):
        ins, outs = refs[:n], refs[n:2 * n]
        send, recv = refs[2 * n:]
        x, y, c = _position()
        cps = []
        for t in range(n):
            rh = ins[t].shape[0] // 2
            mine = pl.ds(c * rh, rh)
            cp = pltpu.make_async_remote_copy(
                src_ref=ins[t].at[mine], dst_ref=outs[t].at[mine], send_sem=send.at[t], recv_sem=recv.at[t],
                device_id=(x, y, 1 - c), device_id_type=MESH)
            cp.start()
            cps.append(cp)
        for cp in cps:
            cp.wait()

    return pl.pallas_call(
        body, name="reduce_share_" + "_".join(str(t.shape[1]) for t in totals), in_specs=[ANY_SPEC] * n,
        out_specs=[ANY_SPEC] * n, out_shape=[jax.ShapeDtypeStruct(t.shape, F32) for t in totals],
        input_output_aliases={t: t for t in range(n)},
        scratch_shapes=[pltpu.SemaphoreType.DMA((n,)), pltpu.SemaphoreType.DMA((n,))],
    )(*totals)


def _exchange_halves(grads):
    n = len(grads)
    hbm = pl.BlockSpec(memory_space=pl.ANY)

    def body(*refs):
        ins, outs = refs[:n], refs[n:2 * n]
        send, recv = refs[2 * n:]
        x, y, c = _position()
        cps = []
        for t in range(n):
            rh = ins[t].shape[1] // 2
            cp = pltpu.make_async_remote_copy(
                src_ref=ins[t].at[pl.ds(0, N_CHIPS), pl.ds((1 - c) * rh, rh)], dst_ref=outs[t], send_sem=send.at[t],
                recv_sem=recv.at[t], device_id=(x, y, 1 - c), device_id_type=MESH)
            cp.start()
            cps.append(cp)
        for cp in cps:
            cp.wait()

    return pl.pallas_call(
        body, name="reduce_exchange_halves", in_specs=[hbm] * n, out_specs=[hbm] * n,
        out_shape=[jax.ShapeDtypeStruct((g.shape[0], g.shape[1] // 2, g.shape[2]), BF16) for g in grads],
        scratch_shapes=[pltpu.SemaphoreType.DMA((n,)), pltpu.SemaphoreType.DMA((n,))],
    )(*grads)


def _add_halves(grad, got, c_idx, name):
    j, r, cols = grad.shape
    rh = r // 2
    tr = 128
    nb = rh // tr

    def body(c_ref, g_ref, o_ref_in, out_ref):
        out_ref[...] = (g_ref[...].astype(F32) + o_ref_in[...].astype(F32)).astype(BF16)

    return pl.pallas_call(
        body, name=name,
        grid_spec=pltpu.PrefetchScalarGridSpec(
            num_scalar_prefetch=1, grid=(j, nb),
            in_specs=[pl.BlockSpec((None, tr, cols), lambda jj, i, c_ref: (jj, c_ref[0] * nb + i, 0)),
                      pl.BlockSpec((None, tr, cols), lambda jj, i, c_ref: (jj, i, 0))],
            out_specs=pl.BlockSpec((None, tr, cols), lambda jj, i, c_ref: (jj, i, 0))),
        out_shape=jax.ShapeDtypeStruct((j, rh, cols), BF16),
        compiler_params=_params("parallel", "parallel"),
    )(c_idx, grad, got)


def _scatter_partials(partials):
    n = len(partials)
    hbm = pl.BlockSpec(memory_space=pl.ANY)

    def body(*refs):
        ins, outs = refs[:n], refs[n:2 * n]
        send, recv, local = refs[2 * n:]
        x, y, c = _position()
        chip = 2 * x + y
        cps, lcs = [], []
        for t in range(n):
            lc = pltpu.make_async_copy(ins[t].at[chip], outs[t].at[chip], local.at[t])
            lc.start()
            lcs.append(lc)
            for k in range(1, N_CHIPS):
                px, py = _chip_peer(x, y, k)
                s = 3 * t + k - 1
                cp = pltpu.make_async_remote_copy(
                    src_ref=ins[t].at[2 * px + py], dst_ref=outs[t].at[chip], send_sem=send.at[s],
                    recv_sem=recv.at[s], device_id=(px, py, c), device_id_type=MESH)
                cp.start()
                cps.append(cp)
        for cp in cps:
            cp.wait()
        for lc in lcs:
            lc.wait()

    return pl.pallas_call(
        body, name="reduce_scatter_partials", in_specs=[hbm] * n, out_specs=[hbm] * n,
        out_shape=[jax.ShapeDtypeStruct(p.shape, BF16) for p in partials],
        scratch_shapes=[pltpu.SemaphoreType.DMA((3 * n,)), pltpu.SemaphoreType.DMA((3 * n,)),
                        pltpu.SemaphoreType.DMA((n,))],
    )(*partials)


def _sum_chips(parts, name):
    j, rh, cols = parts.shape
    tr = 128

    def body(p_ref, o_ref):
        acc = p_ref[0].astype(F32)
        for s in range(1, j):
            acc = acc + p_ref[s].astype(F32)
        o_ref[...] = acc

    return pl.pallas_call(
        body, name=name, grid=(rh // tr,),
        in_specs=[pl.BlockSpec((j, tr, cols), lambda i: (0, i, 0))],
        out_specs=pl.BlockSpec((tr, cols), lambda i: (i, 0)),
        out_shape=jax.ShapeDtypeStruct((rh, cols), F32),
        compiler_params=_params("parallel"),
    )(parts)


def _share_totals(halves):
    n = len(halves)
    hbm = pl.BlockSpec(memory_space=pl.ANY)

    def body(*refs):
        ins, outs = refs[:n], refs[n:2 * n]
        send, recv, local = refs[2 * n:]
        x, y, c = _position()
        cps, lcs = [], []
        for t in range(n):
            rh = ins[t].shape[0]
            mine = outs[t].at[pl.ds(c * rh, rh)]
            lc = pltpu.make_async_copy(ins[t], mine, local.at[t])
            lc.start()
            lcs.append(lc)
            cp = pltpu.make_async_remote_copy(
                src_ref=ins[t], dst_ref=mine, send_sem=send.at[t], recv_sem=recv.at[t],
                device_id=(x, y, 1 - c), device_id_type=MESH)
            cp.start()
            cps.append(cp)
        for cp in cps:
            cp.wait()
        for lc in lcs:
            lc.wait()

    return pl.pallas_call(
        body, name="reduce_share_totals", in_specs=[hbm] * n, out_specs=[hbm] * n,
        out_shape=[jax.ShapeDtypeStruct((2 * h.shape[0], h.shape[1]), F32) for h in halves],
        scratch_shapes=[pltpu.SemaphoreType.DMA((n,)), pltpu.SemaphoreType.DMA((n,)),
                        pltpu.SemaphoreType.DMA((n,))],
    )(*halves)


SMALL_ROWS = 56


def _reduce_small(packed, silu_c):
    ns = 3 * D_MODEL // N_CHIPS

    def body(p_ref, sc_ref, tot_ref, gw_ref, loss_ref, qk_ref, allp, send, recv):
        x, y, c = _position()
        me = 4 * x + 2 * y + c
        chip = 2 * x + y

        def copy(k):
            return pltpu.make_async_remote_copy(
                src_ref=allp.at[me], dst_ref=allp.at[me], send_sem=send.at[k - 1], recv_sem=recv.at[k - 1],
                device_id=_xor_peer(x, y, c, k), device_id_type=MESH)

        allp[me] = p_ref[...]
        for k in range(1, N_DEV):
            copy(k).start()
        for k in range(1, N_DEV):
            copy(k).wait_recv()
        tot = allp[0]
        for i in range(1, N_DEV):
            tot = tot + allp[i]
        tot_ref[...] = tot
        loss_ref[...] = jnp.sum(tot[11:12, :], axis=1, keepdims=True) * (0.5 / D_MODEL)
        fold = tot[5:11, 0:HEAD_DIM]
        for h in range(1, N_HEADS):
            fold = fold + tot[5:11, h * HEAD_DIM:(h + 1) * HEAD_DIM]
        qk_ref[...] = jnp.concatenate([fold, jnp.zeros((2, HEAD_DIM), F32)], axis=0)
        sct = sc_ref[...].T
        rc = 64
        for l in range(2):
            dms = [allp[i, pl.ds(12 + 4 * l + chip, 1), :][:, :ns] for i in range(N_DEV)]
            for r0 in range(0, D_MODEL, rc):
                acc = sct[r0:r0 + rc, 0:1] * dms[0]
                for i in range(1, N_DEV):
                    acc = acc + sct[r0:r0 + rc, i:i + 1] * dms[i]
                gw_ref[l, r0:r0 + rc, :] = acc
        for k in range(1, N_DEV):
            copy(k).wait_send()

    vm = pl.BlockSpec(memory_space=pltpu.VMEM)
    return pl.pallas_call(
        body, name="reduce_small", in_specs=[vm, vm], out_specs=[vm] * 4,
        out_shape=[jax.ShapeDtypeStruct((SMALL_ROWS, D_MODEL), F32), jax.ShapeDtypeStruct((2, D_MODEL, ns), F32),
                   jax.ShapeDtypeStruct((1, 1), F32), jax.ShapeDtypeStruct((8, HEAD_DIM), F32)],
        scratch_shapes=[pltpu.VMEM((N_DEV, SMALL_ROWS, D_MODEL), F32),
                        pltpu.SemaphoreType.DMA((N_DEV - 1,)), pltpu.SemaphoreType.DMA((N_DEV - 1,))],
        compiler_params=pltpu.CompilerParams(vmem_limit_bytes=VMEM_LIMIT_BYTES),
    )(packed, silu_c)


def _reduce_big(grads, c_idx):
    names = list(grads)
    got = _exchange_halves([grads[k] for k in names])
    partials = [_add_halves(grads[k], got[i], c_idx, f"reduce_add_{k}") for i, k in enumerate(names)]
    parts = _scatter_partials(partials)
    halves = [_sum_chips(parts[i], f"reduce_sum_{k}") for i, k in enumerate(names)]
    totals = _share_totals(halves)
    return dict(zip(names, totals))


def kernel(x, c, norm_g, ada_w, ada_b, a_w_in, a_conv_w, a_conv_b, a_ln_g, a_ln_b, a_w_out, b_w_in, b_q_norm, b_k_norm, b_w_out, loss_target, m_norm_g, m_ada_w, m_ada_b, m_a_w_in, m_a_conv_w, m_a_conv_b, m_a_ln_g, m_a_ln_b, m_a_w_out, m_b_w_in, m_b_q_norm, m_b_k_norm, m_b_w_out, v_norm_g, v_ada_w, v_ada_b, v_a_w_in, v_a_conv_w, v_a_conv_b, v_a_ln_g, v_a_ln_b, v_a_w_out, v_b_w_in, v_b_q_norm, v_b_k_norm, v_b_w_out):
    chip = 2 * lax.axis_index("x") + lax.axis_index("y")
    core = lax.axis_index("c")
    chip_idx = chip.astype(jnp.int32).reshape(1)
    dev_idx = jnp.stack([2 * chip + core, chip, core]).astype(jnp.int32)

    mods, silu_c, conv_w_full = _ada_forward(c, ada_w, ada_b, a_conv_w[0])
    lands_a = [_cast_into_slot(a_w_in[0], chip_idx, "cast_a_w_in"), _cast_into_slot(a_w_out[0], chip_idx, "cast_a_w_out")]
    send_a, recv_a, lands_a, token_a = _gather_start(lands_a, mods, "gather_start_a")
    lands_b = [_cast_into_slot(b_w_in[0], chip_idx, "cast_b_w_in"), _cast_into_slot(b_w_out[0], chip_idx, "cast_b_w_out")]
    send_b, recv_b, lands_b, token_b = _gather_start(lands_b, token_a, "gather_start_b")
    mods = mods + token_b[0:2, 0:1]

    def weights_a(after):
        send, recv, lands, _ = _gather_forward(send_a, recv_a, lands_a, after, "gather_forward_a")
        w_in, w_out = _gather_wait(send, recv, lands, after, "gather_wait_a")
        return w_in, w_out.reshape(D_MODEL, D_MODEL)

    forwarded_b = []

    def weights_b(after):
        send, recv, lands, _ = forwarded_b
        w_in, w_out = _gather_wait(send, recv, lands, after, "gather_wait_b")
        return w_in, w_out.reshape(D_MODEL, D_MODEL)

    def forward_weights_b(after):
        forwarded_b.extend(_gather_forward(send_b, recv_b, lands_b, after, "gather_forward_b"))
        return forwarded_b[3]

    stage1, stage2 = {}, {}

    def send_grads(tag, dw_in, dw_out):
        grads = [dw_in, dw_out.reshape(N_CHIPS, D_MODEL // N_CHIPS, D_MODEL)]
        send, recv, arrays, token = _reduce_sibling_start(grads, dw_out, f"reduce_d2d_start_{tag}")
        stage1[tag] = (send, recv, arrays)
        return token

    def forward_grads(tag, after):
        send, recv, arrays = stage1[tag]
        grads, got = _reduce_sibling_wait(send, recv, arrays, after, f"reduce_d2d_wait_{tag}")
        partials = [_add_sibling_half(grads[i], got[i], dev_idx, f"reduce_add_{tag}_{i}") for i in range(2)]
        send, recv, arrays, token = _reduce_chips_start(partials, partials[1], f"reduce_ici_start_{tag}")
        stage2[tag] = (send, recv, arrays)
        return token

    def finish_grads(tag, after):
        send, recv, arrays = stage2[tag]
        partials, lands = _reduce_chips_wait(send, recv, arrays, after, f"reduce_ici_wait_{tag}")
        totals = [_sum_partials(lands[i], partials[i], dev_idx, f"reduce_sum_{tag}_{i}") for i in range(2)]
        return _share_halves(totals)

    grad_x, small = _local_step(
        x[0], loss_target[0], mods.reshape(2, 3, D_MODEL), norm_g, conv_w_full, a_conv_b, a_ln_g[0:1],
        a_ln_b[0:1], b_q_norm[0], b_k_norm[0], weights_a, weights_b, forward_weights_b,
        functools.partial(send_grads, "b"), functools.partial(forward_grads, "b"), functools.partial(send_grads, "a"))

    ns = 3 * D_MODEL // N_CHIPS
    pad_mod = lambda dm: jnp.pad(dm.reshape(N_CHIPS, ns), ((0, 0), (0, D_MODEL - ns)))
    packed = jnp.concatenate([
        small["dnorm_g"], small["dconv_b"], small["dln_g"], small["dln_b"], small["dq_norm"], small["dk_norm"],
        small["loss_cols"], pad_mod(small["dmod0"]), pad_mod(small["dmod1"]), small["dconv_w"],
        jnp.zeros((SMALL_ROWS - 20 - CONV_WIDTH, D_MODEL), F32)], axis=0)
    tot, g_ada_w, loss, qk = _reduce_small(packed, silu_c)
    cw = D_MODEL // N_CHIPS
    g_small = dict(
        norm_g=tot[0:2], a_conv_b=tot[2:3], a_ln_g=tot[3:4], a_ln_b=tot[4:5],
        b_q_norm=qk[0:3], b_k_norm=qk[3:6],
        ada_b=jnp.stack([tot[12:16, :ns].reshape(3 * D_MODEL), tot[16:20, :ns].reshape(3 * D_MODEL)]),
        a_conv_w=lax.dynamic_slice(tot[20:20 + CONV_WIDTH], (0, chip * cw), (CONV_WIDTH, cw)),
    )


    given = dict(norm_g=(norm_g, m_norm_g, v_norm_g), ada_w=(ada_w, m_ada_w, v_ada_w), ada_b=(ada_b, m_ada_b, v_ada_b),
                 a_w_in=(a_w_in, m_a_w_in, v_a_w_in), a_conv_w=(a_conv_w, m_a_conv_w, v_a_conv_w),
                 a_conv_b=(a_conv_b, m_a_conv_b, v_a_conv_b), a_ln_g=(a_ln_g, m_a_ln_g, v_a_ln_g),
                 a_ln_b=(a_ln_b, m_a_ln_b, v_a_ln_b), a_w_out=(a_w_out, m_a_w_out, v_a_w_out),
                 b_w_in=(b_w_in, m_b_w_in, v_b_w_in), b_q_norm=(b_q_norm, m_b_q_norm, v_b_q_norm),
                 b_k_norm=(b_k_norm, m_b_k_norm, v_b_k_norm), b_w_out=(b_w_out, m_b_w_out, v_b_w_out))
    order = ["norm_g", "ada_w", "ada_b", "a_w_in", "a_conv_w", "a_conv_b", "a_ln_g", "a_ln_b", "a_w_out", "b_w_in",
             "b_q_norm", "b_k_norm", "b_w_out"]
    outs = {}

    def update(k, g2):
        w, m, v = given[k]
        shape2 = g2.shape
        d2, m2, v2 = _adamw(w.reshape(shape2), g2, m.reshape(shape2), v.reshape(shape2), f"adamw_{k}")
        outs[k] = tuple(a.reshape(w.shape) for a in (g2, d2, m2, v2))

    token = forward_grads("a", tot)
    g_b_in, g_b_out = finish_grads("b", token)
    update("b_w_in", g_b_in)
    update("b_w_out", g_b_out)
    update("ada_w", g_ada_w.reshape(2 * D_MODEL, ns))
    for k, g2 in g_small.items():
        update(k, g2)
    g_a_in, g_a_out = finish_grads("a", outs["b_w_in"][1])
    update("a_w_in", g_a_in)
    update("a_w_out", g_a_out)
    return (loss.reshape(()), grad_x[None], *[outs[k][0] for k in order], *[outs[k][1] for k in order],
            *[outs[k][2] for k in order], *[outs[k][3] for k in order])
```

```python
import functools

import jax
import jax.numpy as jnp
from jax import lax
from jax.experimental import pallas as pl
from jax.experimental.pallas import tpu as pltpu

F32 = jnp.float32
BF16 = jnp.bfloat16

SEQ = 2048
D_MODEL = 1024
CONV_WIDTH = 31
HEAD_DIM = 64
N_HEADS = 16
DILATIONS = (1, 4, 16)
ATTN_BLOCK = 128
NORM_EPS = 1e-6
NEG_INF = -1e30
N_DEV = 8
N_CHIPS = 4

ADAM_LR = 0.001
ADAM_B1 = 0.9
ADAM_B2 = 0.999
ADAM_EPS = 1e-08
ADAM_WD = 0.01
ADAM_STEP = 10

VMEM_LIMIT_BYTES = 52 * 1024 * 1024
HALO = 32
LANES = 128
MESH = pl.DeviceIdType.MESH


def _params(*sem):
    return pltpu.CompilerParams(dimension_semantics=sem or None, vmem_limit_bytes=VMEM_LIMIT_BYTES)


def _sigmoid(v):
    return 1.0 / (1.0 + jnp.exp(-v))


def _row_spec(tm, cols, col_block=0):
    return pl.BlockSpec((tm, cols), lambda i: (i, col_block))


def _vec_spec(rows, cols):
    return pl.BlockSpec((rows, cols), lambda i: (0, 0))


def _normmod(xv, g, scale, shift):
    r = lax.rsqrt(jnp.mean(xv * xv, axis=-1, keepdims=True) + NORM_EPS)
    return xv * r * g * (1.0 + scale) + shift


def _normmod_fwd(x, g, scale, shift, name):
    tm = 256

    def body(x_ref, g_ref, sc_ref, sh_ref, h_ref, ht_ref):
        h = _normmod(x_ref[...], g_ref[...], sc_ref[...], sh_ref[...])
        h_ref[...] = h.astype(BF16)
        ht_ref[...] = h.T.astype(BF16)

    return pl.pallas_call(
        body, name=name, grid=(SEQ // tm,),
        in_specs=[_row_spec(tm, D_MODEL)] + [_vec_spec(1, D_MODEL)] * 3,
        out_specs=[_row_spec(tm, D_MODEL), pl.BlockSpec((D_MODEL, tm), lambda i: (0, i))],
        out_shape=[jax.ShapeDtypeStruct((SEQ, D_MODEL), BF16), jax.ShapeDtypeStruct((D_MODEL, SEQ), BF16)],
        compiler_params=_params("parallel"),
    )(x, g, scale, shift)


def _normmod_bwd(x, g, scale, dh_parts, dres, name, part_dilations=None):
    tm = 256
    n_parts = len(dh_parts)
    dils = part_dilations or (1,) * n_parts
    dh_parts = [p if d == 1 else p.reshape(d, SEQ // d, D_MODEL) for p, d in zip(dh_parts, dils)]

    def body(x_ref, g_ref, sc_ref, dres_ref, *rest):
        part_refs = rest[:n_parts]
        dx_ref, sums_ref, nat = rest[n_parts:]
        xv = x_ref[...]
        r = lax.rsqrt(jnp.mean(xv * xv, axis=-1, keepdims=True) + NORM_EPS)
        xn = xv * r
        dh = _load_natural(part_refs[0], nat, dils[0])
        for p, d in zip(part_refs[1:], dils[1:]):
            dh = dh + _load_natural(p, nat, d)
        gv = g_ref[...]
        one_sc = 1.0 + sc_ref[...]
        dxn = dh * (gv * one_sc)
        dx = r * (dxn - xn * jnp.mean(dxn * xn, axis=-1, keepdims=True))
        dx_ref[...] = dres_ref[...] + dx
        dhx = dh * xn
        sums = jnp.concatenate([
            jnp.sum(dhx, axis=0, keepdims=True) * one_sc,
            jnp.sum(dhx, axis=0, keepdims=True) * gv,
            jnp.sum(dh, axis=0, keepdims=True),
            jnp.zeros((5, D_MODEL), F32)], axis=0)

        @pl.when(pl.program_id(0) == 0)
        def _():
            sums_ref[...] = jnp.zeros_like(sums_ref)

        sums_ref[...] += sums

    return pl.pallas_call(
        body, name=name, grid=(SEQ // tm,),
        in_specs=[_row_spec(tm, D_MODEL), _vec_spec(1, D_MODEL), _vec_spec(1, D_MODEL), _row_spec(tm, D_MODEL)]
        + [_class_spec(tm, d) for d in dils],
        out_specs=[_row_spec(tm, D_MODEL), _vec_spec(8, D_MODEL)],
        out_shape=[jax.ShapeDtypeStruct((SEQ, D_MODEL), F32), jax.ShapeDtypeStruct((8, D_MODEL), F32)],
        scratch_shapes=[_natural_scratch(tm)],
        compiler_params=_params("arbitrary"),
    )(x, g, scale, dres, *dh_parts)


def _mm(lhs, rhs, *, tn, tile0, n_tiles, out_dtype, name, out3d=None, prev=None):
    mo, kc = lhs.shape
    cm = 512

    def body(l_ref, r_ref, *rest):
        o_ref = rest[-1]
        for m in range(mo // cm):
            rows = pl.ds(m * cm, cm)
            o_ref[rows, :] = jnp.dot(l_ref[rows, :], r_ref[...], preferred_element_type=F32).astype(out_dtype)

    if rhs.ndim == 3:
        tps_r = rhs.shape[2] // tn
        r_spec = pl.BlockSpec((None, kc, tn), lambda t: ((tile0 + t) // tps_r, 0, (tile0 + t) % tps_r))
    else:
        r_spec = pl.BlockSpec((kc, tn), lambda t: (0, t))
    in_specs = [pl.BlockSpec((mo, kc), lambda t: (0, 0)), r_spec]
    args = [lhs, rhs]
    aliases = {}
    if out3d is None:
        o_spec = pl.BlockSpec((mo, tn), lambda t: (0, t))
        o_shape = jax.ShapeDtypeStruct((mo, n_tiles * tn), out_dtype)
    else:
        j_out, ns_out = out3d
        tps_o = ns_out // tn
        o_spec = pl.BlockSpec((None, mo, tn), lambda t: ((tile0 + t) // tps_o, 0, (tile0 + t) % tps_o))
        o_shape = jax.ShapeDtypeStruct((j_out, mo, ns_out), out_dtype)
        if prev is not None:
            in_specs.append(pl.BlockSpec(memory_space=pl.ANY))
            args.append(prev)
            aliases = {2: 0}
    return pl.pallas_call(
        body, name=name, grid=(n_tiles,), in_specs=in_specs, out_specs=o_spec, out_shape=o_shape,
        input_output_aliases=aliases, compiler_params=_params("parallel"),
    )(*args)


def _mm_nt(dy, w3, *, tn, tile0, n_tiles, name, after=None):
    m_rows = dy.shape[0]
    _, kc, ns = w3.shape
    tps = ns // tn
    cm = 512
    extra = [] if after is None else [after]

    def body(dy_ref, w_ref, *rest):
        o_ref = rest[-1]

        @pl.when(pl.program_id(0) == 0)
        def _():
            o_ref[...] = jnp.zeros_like(o_ref)

        for m in range(m_rows // cm):
            rows = pl.ds(m * cm, cm)
            o_ref[rows, :] += lax.dot_general(dy_ref[rows, :], w_ref[...], (((1,), (1,)), ((), ())),
                                              preferred_element_type=F32)

    return pl.pallas_call(
        body, name=name, grid=(n_tiles,),
        in_specs=[pl.BlockSpec((m_rows, tn), lambda t: (0, t)),
                  pl.BlockSpec((None, kc, tn), lambda t: ((tile0 + t) // tps, 0, (tile0 + t) % tps))]
        + [pl.BlockSpec(memory_space=pl.ANY)] * len(extra),
        out_specs=pl.BlockSpec((m_rows, kc), lambda t: (0, 0)),
        out_shape=jax.ShapeDtypeStruct((m_rows, kc), F32),
        compiler_params=_params("arbitrary"),
    )(dy, w3, *extra)


CONV_CHUNK = 16


def _shift_copies(buf, shifted):
    rows = shifted.shape[1]
    for s in range(1, 8):
        shifted[s - 1] = buf[pl.ds(s, rows), :]


def _shifted_rows(buf, shifted, offset, r0):
    s = offset % 8
    if s == 0:
        return buf[pl.ds(r0 + offset, CONV_CHUNK), :]
    return shifted[s - 1, pl.ds(r0 + (offset - s), CONV_CHUNK), :]


def _conv_fwd(proj, conv_w, conv_b, ln_g, ln_b, name):
    tm = 256
    hb = tm // HALO

    def body(vg_ref, halo_ref, z_ref, w_ref, b_ref, g_ref, be_ref, u5_ref, u5t_ref, u2_ref, buf, shifted):
        i = pl.program_id(0)
        u1 = vg_ref[:, :D_MODEL] * _sigmoid(vg_ref[:, D_MODEL:])
        u1h = halo_ref[:, :D_MODEL] * _sigmoid(halo_ref[:, D_MODEL:])
        buf[pl.ds(0, HALO), :] = jnp.where(i > 0, u1h, 0.0)
        buf[pl.ds(HALO, tm), :] = u1
        _shift_copies(buf, shifted)

        def chunk(ci, carry):
            r0 = pl.multiple_of(ci * CONV_CHUNK, CONV_CHUNK)
            acc = jnp.broadcast_to(b_ref[...], (CONV_CHUNK, D_MODEL))
            for k in range(CONV_WIDTH):
                acc = acc + w_ref[k:k + 1, :] * _shifted_rows(buf, shifted, HALO - (CONV_WIDTH - 1) + k, r0)
            u2_ref[pl.ds(r0, CONV_CHUNK), :] = acc
            return carry

        lax.fori_loop(0, tm // CONV_CHUNK, chunk, 0)
        acc = u2_ref[...]
        mu = jnp.mean(acc, axis=-1, keepdims=True)
        xc = acc - mu
        rstd = lax.rsqrt(jnp.mean(xc * xc, axis=-1, keepdims=True) + NORM_EPS)
        u3 = xc * rstd * g_ref[...] + be_ref[...]
        zv = z_ref[...]
        u5 = u3 * _sigmoid(u3) * (zv * _sigmoid(zv))
        u5_ref[...] = u5.astype(BF16)
        u5t_ref[...] = u5.T.astype(BF16)

    return pl.pallas_call(
        body, name=name, grid=(SEQ // tm,),
        in_specs=[pl.BlockSpec((tm, 2 * D_MODEL), lambda i: (i, 0)),
                  pl.BlockSpec((HALO, 2 * D_MODEL), lambda i: (jnp.maximum(i * hb - 1, 0), 0)),
                  _row_spec(tm, D_MODEL, 2),
                  _vec_spec(CONV_WIDTH, D_MODEL)] + [_vec_spec(1, D_MODEL)] * 3,
        out_specs=[_row_spec(tm, D_MODEL), pl.BlockSpec((D_MODEL, tm), lambda i: (0, i)), _row_spec(tm, D_MODEL)],
        out_shape=[jax.ShapeDtypeStruct((SEQ, D_MODEL), BF16), jax.ShapeDtypeStruct((D_MODEL, SEQ), BF16),
                   jax.ShapeDtypeStruct((SEQ, D_MODEL), F32)],
        scratch_shapes=[pltpu.VMEM((HALO + tm, D_MODEL), F32), pltpu.VMEM((7, HALO + tm - 8, D_MODEL), F32)],
        compiler_params=_params("parallel"),
    )(proj, proj, proj, conv_w, conv_b, ln_g, ln_b)


def _conv_bwd_pointwise(du5, proj, u2, ln_g, ln_b, name):
    tm = 256

    def body(du5_ref, z_ref, u2_ref, g_ref, be_ref, du2_ref, dz_ref, sums_ref):
        u2v = u2_ref[...]
        mu = jnp.mean(u2v, axis=-1, keepdims=True)
        xc = u2v - mu
        rstd = lax.rsqrt(jnp.mean(xc * xc, axis=-1, keepdims=True) + NORM_EPS)
        xhat = xc * rstd
        u3 = xhat * g_ref[...] + be_ref[...]
        s3 = _sigmoid(u3)
        u4 = u3 * s3
        zv = z_ref[...]
        sz = _sigmoid(zv)
        du5v = du5_ref[...]
        dz_ref[...] = du5v * u4 * (sz * (1.0 + zv * (1.0 - sz)))
        du3 = du5v * (zv * sz) * (s3 * (1.0 + u3 * (1.0 - s3)))
        dxhat = du3 * g_ref[...]
        du2 = rstd * (dxhat - jnp.mean(dxhat, axis=-1, keepdims=True)
                      - xhat * jnp.mean(dxhat * xhat, axis=-1, keepdims=True))
        du2_ref[...] = du2
        sums = jnp.concatenate([
            jnp.sum(du3 * xhat, axis=0, keepdims=True),
            jnp.sum(du3, axis=0, keepdims=True),
            jnp.sum(du2, axis=0, keepdims=True),
            jnp.zeros((5, D_MODEL), F32)], axis=0)

        @pl.when(pl.program_id(0) == 0)
        def _():
            sums_ref[...] = jnp.zeros_like(sums_ref)

        sums_ref[...] += sums

    return pl.pallas_call(
        body, name=name, grid=(SEQ // tm,),
        in_specs=[_row_spec(tm, D_MODEL), _row_spec(tm, D_MODEL, 2), _row_spec(tm, D_MODEL),
                  _vec_spec(1, D_MODEL), _vec_spec(1, D_MODEL)],
        out_specs=[_row_spec(tm, D_MODEL), _row_spec(tm, D_MODEL), _vec_spec(8, D_MODEL)],
        out_shape=[jax.ShapeDtypeStruct((SEQ, D_MODEL), F32), jax.ShapeDtypeStruct((SEQ, D_MODEL), F32),
                   jax.ShapeDtypeStruct((8, D_MODEL), F32)],
        compiler_params=_params("arbitrary"),
    )(du5, proj, u2, ln_g, ln_b)


def _conv_bwd_taps(du2, dz, proj, conv_w, name):
    tm = 256
    hb = tm // HALO
    n_blocks = SEQ // tm

    def body(du2_ref, dnext_ref, dz_ref, vg_ref, halo_ref, w_ref, dproj_ref, dw_ref,
             ubuf, dbuf, ushift, dshift, sgbuf, dwacc):
        i = pl.program_id(0)
        sg = _sigmoid(vg_ref[:, D_MODEL:])
        sgbuf[...] = sg
        u1h = halo_ref[:, :D_MODEL] * _sigmoid(halo_ref[:, D_MODEL:])
        ubuf[pl.ds(0, HALO), :] = jnp.where(i > 0, u1h, 0.0)
        ubuf[pl.ds(HALO, tm), :] = vg_ref[:, :D_MODEL] * sg
        dbuf[pl.ds(0, tm), :] = du2_ref[...]
        dbuf[pl.ds(tm, HALO), :] = jnp.where(i < n_blocks - 1, dnext_ref[...], 0.0)
        _shift_copies(ubuf, ushift)
        _shift_copies(dbuf, dshift)

        @pl.when(i == 0)
        def _():
            dwacc[...] = jnp.zeros_like(dwacc)

        def chunk(ci, carry):
            r0 = pl.multiple_of(ci * CONV_CHUNK, CONV_CHUNK)
            rows = pl.ds(r0, CONV_CHUNK)
            du2c = du2_ref[rows, :]
            du1 = jnp.zeros((CONV_CHUNK, D_MODEL), F32)
            for k in range(CONV_WIDTH):
                du1 = du1 + w_ref[k:k + 1, :] * _shifted_rows(dbuf, dshift, CONV_WIDTH - 1 - k, r0)
                prod = du2c * _shifted_rows(ubuf, ushift, HALO - (CONV_WIDTH - 1) + k, r0)
                dwacc[k] += prod[0:8] + prod[8:16]
            sgc = sgbuf[rows, :]
            dval = du1 * sgc
            dproj_ref[rows, 0:D_MODEL] = dval.astype(BF16)
            dproj_ref[rows, D_MODEL:2 * D_MODEL] = (dval * vg_ref[rows, 0:D_MODEL] * (1.0 - sgc)).astype(BF16)
            return carry

        lax.fori_loop(0, tm // CONV_CHUNK, chunk, 0)
        dproj_ref[:, 2 * D_MODEL:] = dz_ref[...].astype(BF16)

        @pl.when(i == n_blocks - 1)
        def _():
            for k in range(CONV_WIDTH):
                dw_ref[k:k + 1, :] = jnp.sum(dwacc[k], axis=0, keepdims=True)
            dw_ref[CONV_WIDTH:, :] = jnp.zeros((32 - CONV_WIDTH, D_MODEL), F32)

    return pl.pallas_call(
        body, name=name, grid=(n_blocks,),
        in_specs=[_row_spec(tm, D_MODEL),
                  pl.BlockSpec((HALO, D_MODEL), lambda i: (jnp.minimum((i + 1) * hb, SEQ // HALO - 1), 0)),
                  _row_spec(tm, D_MODEL),
                  pl.BlockSpec((tm, 2 * D_MODEL), lambda i: (i, 0)),
                  pl.BlockSpec((HALO, 2 * D_MODEL), lambda i: (jnp.maximum(i * hb - 1, 0), 0)),
                  _vec_spec(CONV_WIDTH, D_MODEL)],
        out_specs=[_row_spec(tm, 3 * D_MODEL), _vec_spec(32, D_MODEL)],
        out_shape=[jax.ShapeDtypeStruct((SEQ, 3 * D_MODEL), BF16), jax.ShapeDtypeStruct((32, D_MODEL), F32)],
        scratch_shapes=[pltpu.VMEM((HALO + tm, D_MODEL), F32), pltpu.VMEM((tm + HALO, D_MODEL), F32),
                        pltpu.VMEM((7, HALO + tm - 8, D_MODEL), F32), pltpu.VMEM((7, HALO + tm - 8, D_MODEL), F32),
                        pltpu.VMEM((tm, D_MODEL), F32), pltpu.VMEM((CONV_WIDTH, 8, D_MODEL), F32)],
        compiler_params=_params("arbitrary"),
    )(du2, du2, dz, proj, proj, conv_w)


def _out_a(u5, w_out, x, gate, g1, scale1, shift1, name):
    tm = 256
    n_d = len(DILATIONS)

    def body(u_ref, w_ref, x_ref, gate_ref, g_ref, sc_ref, sh_ref, x1_ref, y_ref, ht_ref, *rest):
        h_refs, nat = rest[:n_d], rest[-1]
        y = jnp.dot(u_ref[...], w_ref[...], preferred_element_type=F32)
        x1 = x_ref[...] + gate_ref[...] * y
        y_ref[...] = y
        x1_ref[...] = x1
        h = _normmod(x1, g_ref[...], sc_ref[...], sh_ref[...])
        ht_ref[...] = h.T.astype(BF16)
        for h_ref, d in zip(h_refs, DILATIONS):
            _store_classes(h_ref, h, nat, d)

    res = pl.pallas_call(
        body, name=name, grid=(SEQ // tm,),
        in_specs=[_row_spec(tm, D_MODEL), _vec_spec(D_MODEL, D_MODEL), _row_spec(tm, D_MODEL)]
        + [_vec_spec(1, D_MODEL)] * 4,
        out_specs=[_row_spec(tm, D_MODEL), _row_spec(tm, D_MODEL), pl.BlockSpec((D_MODEL, tm), lambda i: (0, i))]
        + [_class_spec(tm, d) for d in DILATIONS],
        out_shape=[jax.ShapeDtypeStruct((SEQ, D_MODEL), F32), jax.ShapeDtypeStruct((SEQ, D_MODEL), F32),
                   jax.ShapeDtypeStruct((D_MODEL, SEQ), BF16)] + [_class_shape(d, BF16) for d in DILATIONS],
        scratch_shapes=[_natural_scratch(tm)],
        compiler_params=_params("parallel"),
    )(u5, w_out, x, gate, g1, scale1, shift1)
    return res[0], res[1], res[2], [a.reshape(SEQ, D_MODEL) for a in res[3:]]


def _out_b_loss(u, w_out, x1, gate, target, name):
    tm = 256

    def body(u_ref, w_ref, x_ref, gate_ref, t_ref, e_ref, dy_ref, sums_ref):
        y = jnp.dot(u_ref[...], w_ref[...], preferred_element_type=F32)
        diff = x_ref[...] + gate_ref[...] * y - t_ref[...]
        e = diff * (1.0 / D_MODEL)
        e_ref[...] = e
        dy_ref[...] = (e * gate_ref[...]).astype(BF16)
        sums = jnp.concatenate([
            jnp.sum(e * y, axis=0, keepdims=True),
            jnp.sum(diff * diff, axis=0, keepdims=True),
            jnp.zeros((6, D_MODEL), F32)], axis=0)

        @pl.when(pl.program_id(0) == 0)
        def _():
            sums_ref[...] = jnp.zeros_like(sums_ref)

        sums_ref[...] += sums

    return pl.pallas_call(
        body, name=name, grid=(SEQ // tm,),
        in_specs=[_row_spec(tm, D_MODEL), _vec_spec(D_MODEL, D_MODEL), _row_spec(tm, D_MODEL),
                  _vec_spec(1, D_MODEL), _row_spec(tm, D_MODEL)],
        out_specs=[_row_spec(tm, D_MODEL), _row_spec(tm, D_MODEL), _vec_spec(8, D_MODEL)],
        out_shape=[jax.ShapeDtypeStruct((SEQ, D_MODEL), F32), jax.ShapeDtypeStruct((SEQ, D_MODEL), BF16),
                   jax.ShapeDtypeStruct((8, D_MODEL), F32)],
        compiler_params=_params("arbitrary"),
    )(u, w_out, x1, gate, target)


def _dgate_dy(dx1, y, gate, name):
    tm = 256

    def body(d_ref, y_ref, gate_ref, dy_ref, sums_ref):
        dv = d_ref[...]
        dy_ref[...] = (dv * gate_ref[...]).astype(BF16)
        sums = jnp.concatenate([jnp.sum(dv * y_ref[...], axis=0, keepdims=True), jnp.zeros((7, D_MODEL), F32)], axis=0)

        @pl.when(pl.program_id(0) == 0)
        def _():
            sums_ref[...] = jnp.zeros_like(sums_ref)

        sums_ref[...] += sums

    return pl.pallas_call(
        body, name=name, grid=(SEQ // tm,),
        in_specs=[_row_spec(tm, D_MODEL), _row_spec(tm, D_MODEL), _vec_spec(1, D_MODEL)],
        out_specs=[_row_spec(tm, D_MODEL), _vec_spec(8, D_MODEL)],
        out_shape=[jax.ShapeDtypeStruct((SEQ, D_MODEL), BF16), jax.ShapeDtypeStruct((8, D_MODEL), F32)],
        compiler_params=_params("arbitrary"),
    )(dx1, y, gate)


def _mm_nt_res(dy, w, name):
    tm = 256
    kc, n = w.shape

    def body(dy_ref, w_ref, o_ref):
        o_ref[...] = lax.dot_general(dy_ref[...], w_ref[...], (((1,), (1,)), ((), ())), preferred_element_type=F32)

    return pl.pallas_call(
        body, name=name, grid=(SEQ // tm,),
        in_specs=[_row_spec(tm, n), _vec_spec(kc, n)],
        out_specs=_row_spec(tm, kc),
        out_shape=jax.ShapeDtypeStruct((SEQ, kc), F32),
        compiler_params=_params("parallel"),
    )(dy, w)


def _seg_matrix():
    r = lax.broadcasted_iota(jnp.int32, (256, 256), 0) // HEAD_DIM
    c = lax.broadcasted_iota(jnp.int32, (256, 256), 1) // HEAD_DIM
    return (r == c).astype(BF16)


def _segsum(v, seg):
    hi = v.astype(BF16)
    lo = (v - hi.astype(F32)).astype(BF16)
    outs = []
    for c0 in range(0, D_MODEL, 256):
        outs.append(jnp.dot(hi[:, c0:c0 + 256], seg, preferred_element_type=F32)
                    + jnp.dot(lo[:, c0:c0 + 256], seg, preferred_element_type=F32))
    return jnp.concatenate(outs, axis=1)


def _qk_rstd(v, seg):
    return lax.rsqrt(_segsum(v * v, seg) * (1.0 / HEAD_DIM) + NORM_EPS)


def _qknorm_fwd(proj, qw, kw, seg, name):
    tm = 256

    def body(p_ref, qw_ref, kw_ref, seg_ref, q_ref, k_ref):
        segv = seg_ref[...]
        q = p_ref[:, :D_MODEL].astype(F32)
        k = p_ref[:, D_MODEL:].astype(F32)
        q_ref[...] = (q * _qk_rstd(q, segv) * qw_ref[...]).astype(BF16)
        k_ref[...] = (k * _qk_rstd(k, segv) * kw_ref[...]).astype(BF16)

    return pl.pallas_call(
        body, name=name, grid=(SEQ // tm,),
        in_specs=[_row_spec(tm, 2 * D_MODEL), _vec_spec(1, D_MODEL), _vec_spec(1, D_MODEL), _vec_spec(256, 256)],
        out_specs=[_row_spec(tm, D_MODEL)] * 2,
        out_shape=[jax.ShapeDtypeStruct((SEQ, D_MODEL), BF16)] * 2,
        compiler_params=_params("parallel"),
    )(proj, qw, kw, seg)


def _attn_masks(b, bpc, dilation, slope):
    if bpc == 1:
        qi = lax.broadcasted_iota(jnp.int32, (ATTN_BLOCK, ATTN_BLOCK), 0)
        kj = lax.broadcasted_iota(jnp.int32, (ATTN_BLOCK, ATTN_BLOCK), 1)
        steps = qi - kj
        return (steps * dilation).astype(F32), steps >= 0
    qi = lax.broadcasted_iota(jnp.int32, (ATTN_BLOCK, 2 * ATTN_BLOCK), 0)
    kj = lax.broadcasted_iota(jnp.int32, (ATTN_BLOCK, 2 * ATTN_BLOCK), 1)
    steps = qi + ATTN_BLOCK - kj
    has_prev = (b % bpc) != 0
    valid = (steps >= 0) & (steps <= ATTN_BLOCK) & (has_prev | (kj >= ATTN_BLOCK))
    return (steps * dilation).astype(F32), valid


def _key_tile(prev_ref, cur_ref, cols, bpc):
    if bpc == 1:
        return cur_ref[:, cols]
    return jnp.concatenate([prev_ref[:, cols], cur_ref[:, cols]], axis=0)


ATTN_HEADS_FWD = 16
ATTN_HEADS_BWD = 16
NT_DIMS = (((1,), (1,)), ((), ()))
TN_DIMS = (((0,), (0,)), ((), ()))
BATCH_NT_DIMS = (((2,), (2,)), ((0,), (0,)))
BATCH_NN_DIMS = (((2,), (1,)), ((0,), (0,)))
BATCH_TN_DIMS = (((1,), (1,)), ((0,), (0,)))


def _head_stack(tile_of, heads):
    return jnp.stack([tile_of(slice(h * HEAD_DIM, (h + 1) * HEAD_DIM)) for h in range(heads)], axis=0)


def _attn_specs(heads, segment=0):
    width = heads * HEAD_DIM
    off = segment * (D_MODEL // width)
    last = SEQ // ATTN_BLOCK - 1
    cur = pl.BlockSpec((ATTN_BLOCK, width), lambda hg, b: (jnp.minimum(b, last), hg + off))
    prev = pl.BlockSpec((ATTN_BLOCK, width), lambda hg, b: (jnp.clip(b - 1, 0, last), hg + off))
    return cur, prev


def _attn_fwd(q, k, proj, slopes, dilation, name):
    bpc = SEQ // dilation // ATTN_BLOCK
    heads = ATTN_HEADS_FWD
    cur, prev = _attn_specs(heads)
    v_cur, v_prev = _attn_specs(heads, segment=2)
    scale = HEAD_DIM ** -0.5

    def body(sl_ref, q_ref, kp_ref, kc_ref, vp_ref, vc_ref, o_ref, lse_ref):
        dist, valid = _attn_masks(pl.program_id(1), bpc, dilation, None)
        q3 = _head_stack(lambda cols: q_ref[:, cols], heads)
        k3 = _head_stack(lambda cols: _key_tile(kp_ref, kc_ref, cols, bpc), heads)
        v3 = _head_stack(lambda cols: _key_tile(vp_ref, vc_ref, cols, bpc), heads)
        s = lax.dot_general(q3, k3, BATCH_NT_DIMS, preferred_element_type=F32)
        s = jnp.where(valid[None], s * scale - dist[None] * sl_ref[...], NEG_INF)
        m = jnp.max(s, axis=-1, keepdims=True)
        p = jnp.exp(s - m)
        l = jnp.sum(p, axis=-1, keepdims=True)
        o3 = lax.dot_general(p.astype(BF16), v3, BATCH_NN_DIMS, preferred_element_type=F32) / l
        lse3 = m + jnp.log(l)
        for h in range(heads):
            cols = slice(h * HEAD_DIM, (h + 1) * HEAD_DIM)
            o_ref[:, cols] = o3[h]
            lse_ref[:, cols] = jnp.broadcast_to(lse3[h], (ATTN_BLOCK, HEAD_DIM))

    return pl.pallas_call(
        body, name=name, grid=(N_HEADS // heads, SEQ // ATTN_BLOCK),
        in_specs=[pl.BlockSpec((heads, 1, 1), lambda hg, b: (hg, 0, 0)), cur, prev, cur, v_prev, v_cur],
        out_specs=[cur, cur],
        out_shape=[jax.ShapeDtypeStruct((SEQ, D_MODEL), F32)] * 2,
        compiler_params=_params("parallel", "parallel"),
    )(slopes.reshape(N_HEADS, 1, 1), q, k, k, proj, proj)


def _class_spec(tm, dilation):
    if dilation == 1:
        return _row_spec(tm, D_MODEL)
    return pl.BlockSpec((dilation, tm // dilation, D_MODEL), lambda i: (0, i, 0))


def _class_shape(dilation, dtype):
    if dilation == 1:
        return jax.ShapeDtypeStruct((SEQ, D_MODEL), dtype)
    return jax.ShapeDtypeStruct((dilation, SEQ // dilation, D_MODEL), dtype)


def _load_natural(in_ref, nat_ref, dilation):
    if dilation == 1:
        return in_ref[...].astype(F32)
    n = nat_ref.shape[1] // dilation
    for r in range(dilation):
        for j in range(D_MODEL // LANES):
            nat_ref.at[j][pl.ds(r, n, stride=dilation), :] = in_ref[r, :, j * LANES:(j + 1) * LANES].astype(F32)
    return jnp.concatenate([nat_ref[j] for j in range(D_MODEL // LANES)], axis=1)


def _store_classes(out_ref, value, nat_ref, dilation):
    if dilation == 1:
        out_ref[...] = value.astype(out_ref.dtype)
        return
    n = nat_ref.shape[1] // dilation
    for j in range(D_MODEL // LANES):
        nat_ref[j] = value[:, j * LANES:(j + 1) * LANES]
    for r in range(dilation):
        for j in range(D_MODEL // LANES):
            out_ref[r, :, j * LANES:(j + 1) * LANES] = (
                nat_ref.at[j][pl.ds(r, n, stride=dilation), :].astype(out_ref.dtype))


def _natural_scratch(tm):
    return pltpu.VMEM((D_MODEL // LANES, tm, LANES), F32)


def _merge_fwd(o_parts, lse_parts, z, name):
    tm = 256

    def body(o0, o1, o2, l0, l1, l2, z_ref, u_ref, ut_ref, o_ref, lse_ref, nat):
        ls = [_load_natural(l, nat, d) for l, d in zip((l0, l1, l2), DILATIONS)]
        m = jnp.maximum(jnp.maximum(ls[0], ls[1]), ls[2])
        tot = m + jnp.log(jnp.exp(ls[0] - m) + jnp.exp(ls[1] - m) + jnp.exp(ls[2] - m))
        o = jnp.zeros((tm, D_MODEL), F32)
        for o_in, l, d in zip((o0, o1, o2), ls, DILATIONS):
            o = o + jnp.exp(l - tot) * _load_natural(o_in, nat, d)
        zv = z_ref[...]
        u = o * (zv * _sigmoid(zv))
        u_ref[...] = u.astype(BF16)
        ut_ref[...] = u.T.astype(BF16)
        o_ref[...] = o
        lse_ref[...] = tot

    return pl.pallas_call(
        body, name=name, grid=(SEQ // tm,),
        in_specs=[_class_spec(tm, d) for d in DILATIONS] * 2 + [_row_spec(tm, D_MODEL)],
        out_specs=[_row_spec(tm, D_MODEL), pl.BlockSpec((D_MODEL, tm), lambda i: (0, i)),
                   _row_spec(tm, D_MODEL), _row_spec(tm, D_MODEL)],
        out_shape=[jax.ShapeDtypeStruct((SEQ, D_MODEL), BF16), jax.ShapeDtypeStruct((D_MODEL, SEQ), BF16),
                   jax.ShapeDtypeStruct((SEQ, D_MODEL), F32), jax.ShapeDtypeStruct((SEQ, D_MODEL), F32)],
        scratch_shapes=[_natural_scratch(tm)],
        compiler_params=_params("parallel"),
    )(*o_parts, *lse_parts, z)


def _merge_bwd(du, o, lse, z, seg, name):
    tm = 256
    n_d = len(DILATIONS)
    assert DILATIONS[0] == 1

    def body(du_ref, o_ref, lse_ref, z_ref, seg_ref, dz_ref, *rest):
        do_refs, delta_refs, lse_refs = rest[:n_d], rest[n_d:2 * n_d], rest[2 * n_d:3 * n_d - 1]
        nat = rest[-1]
        zv = z_ref[...]
        sz = _sigmoid(zv)
        duv = du_ref[...]
        ov = o_ref[...]
        do = duv * (zv * sz)
        dz_ref[...] = (duv * ov * (sz * (1.0 + zv * (1.0 - sz)))).astype(BF16)
        delta = _segsum(do * ov, seg_ref[...])
        lv = lse_ref[...]
        for i, d in enumerate(DILATIONS):
            _store_classes(do_refs[i], do, nat, d)
            _store_classes(delta_refs[i], delta, nat, d)
            if i > 0:
                _store_classes(lse_refs[i - 1], lv, nat, d)

    res = pl.pallas_call(
        body, name=name, grid=(SEQ // tm,),
        in_specs=[_row_spec(tm, D_MODEL)] * 4 + [_vec_spec(256, 256)],
        out_specs=[_row_spec(tm, D_MODEL)] + [_class_spec(tm, d) for d in DILATIONS] * 2
        + [_class_spec(tm, d) for d in DILATIONS[1:]],
        out_shape=[jax.ShapeDtypeStruct((SEQ, D_MODEL), BF16)] + [_class_shape(d, BF16) for d in DILATIONS]
        + [_class_shape(d, F32) for d in DILATIONS] + [_class_shape(d, F32) for d in DILATIONS[1:]],
        scratch_shapes=[_natural_scratch(tm)],
        compiler_params=_params("parallel"),
    )(du, o, lse, z, seg)
    flat = lambda a: a.reshape(SEQ, D_MODEL)
    dz, dos, deltas, lses = res[0], res[1:1 + n_d], res[1 + n_d:1 + 2 * n_d], [lse] + list(res[1 + 2 * n_d:])
    return dz, [flat(a) for a in dos], [flat(a) for a in deltas], [flat(a) for a in lses]


def _attn_bwd(q, k, proj, do, lse, delta, slopes, dilation, name):
    bpc = SEQ // dilation // ATTN_BLOCK
    heads = ATTN_HEADS_BWD
    n_blocks = SEQ // ATTN_BLOCK
    carry = bpc > 1
    width = heads * HEAD_DIM
    cur, prev = _attn_specs(heads)
    v_cur, v_prev = _attn_specs(heads, segment=2)
    scale = HEAD_DIM ** -0.5

    def body(sl_ref, q_ref, kp_ref, kc_ref, vp_ref, vc_ref, do_ref, lse_ref, dl_ref,
             dq_ref, dk_ref, dv_ref, *scratch):
        b = pl.program_id(1)
        if carry:
            dk_carry, dv_carry = scratch

            @pl.when(b == n_blocks)
            def _():
                dk_ref[...] = dk_carry[...].astype(BF16)
                dv_ref[...] = dv_carry[...].astype(BF16)

            @pl.when(b < n_blocks)
            def _():
                step(sl_ref, q_ref, kp_ref, kc_ref, vp_ref, vc_ref, do_ref, lse_ref, dl_ref,
                     dq_ref, dk_ref, dv_ref, dk_carry, dv_carry, b)
        else:
            step(sl_ref, q_ref, kp_ref, kc_ref, vp_ref, vc_ref, do_ref, lse_ref, dl_ref,
                 dq_ref, dk_ref, dv_ref, None, None, b)

    def step(sl_ref, q_ref, kp_ref, kc_ref, vp_ref, vc_ref, do_ref, lse_ref, dl_ref,
             dq_ref, dk_ref, dv_ref, dk_carry, dv_carry, b):
        if carry:
            @pl.when(b == 0)
            def _():
                dk_carry[...] = jnp.zeros_like(dk_carry)
                dv_carry[...] = jnp.zeros_like(dv_carry)

        dist, valid = _attn_masks(b, bpc, dilation, None)
        q3 = _head_stack(lambda cols: q_ref[:, cols], heads)
        k3 = _head_stack(lambda cols: _key_tile(kp_ref, kc_ref, cols, bpc), heads)
        v3 = _head_stack(lambda cols: _key_tile(vp_ref, vc_ref, cols, bpc), heads)
        do3 = _head_stack(lambda cols: do_ref[:, cols], heads)
        lse3 = _head_stack(lambda cols: lse_ref[:, cols.start:cols.start + 1], heads)
        dl3 = _head_stack(lambda cols: dl_ref[:, cols.start:cols.start + 1], heads)
        s = lax.dot_general(q3, k3, BATCH_NT_DIMS, preferred_element_type=F32)
        p = jnp.exp(jnp.where(valid[None], s * scale - dist[None] * sl_ref[...], NEG_INF) - lse3)
        dp = lax.dot_general(do3, v3, BATCH_NT_DIMS, preferred_element_type=F32)
        ds = (p * (dp - dl3) * scale).astype(BF16)
        dq3 = lax.dot_general(ds, k3, BATCH_NN_DIMS, preferred_element_type=F32)
        dk3 = lax.dot_general(ds, q3, BATCH_TN_DIMS, preferred_element_type=F32)
        dv3 = lax.dot_general(p.astype(BF16), do3, BATCH_TN_DIMS, preferred_element_type=F32)
        for h in range(heads):
            cols = slice(h * HEAD_DIM, (h + 1) * HEAD_DIM)
            dq_ref[:, cols] = dq3[h].astype(BF16)
            if carry:
                dk_ref[:, cols] = (dk_carry[:, cols] + dk3[h, :ATTN_BLOCK]).astype(BF16)
                dv_ref[:, cols] = (dv_carry[:, cols] + dv3[h, :ATTN_BLOCK]).astype(BF16)
                dk_carry[:, cols] = dk3[h, ATTN_BLOCK:]
                dv_carry[:, cols] = dv3[h, ATTN_BLOCK:]
            else:
                dk_ref[:, cols] = dk3[h].astype(BF16)
                dv_ref[:, cols] = dv3[h].astype(BF16)

    kv_out = prev if carry else cur
    return pl.pallas_call(
        body, name=name, grid=(N_HEADS // heads, n_blocks + (1 if carry else 0)),
        in_specs=[pl.BlockSpec((heads, 1, 1), lambda hg, b: (hg, 0, 0)), cur, prev, cur, v_prev, v_cur,
                  cur, cur, cur],
        out_specs=[cur, kv_out, kv_out],
        out_shape=[jax.ShapeDtypeStruct((SEQ, D_MODEL), BF16)] * 3,
        scratch_shapes=[pltpu.VMEM((ATTN_BLOCK, width), F32)] * 2 if carry else [],
        compiler_params=_params("parallel", "arbitrary"),
    )(slopes.reshape(N_HEADS, 1, 1), q, k, k, proj, proj, do, lse, delta)


def _qknorm_bwd(proj, qw, kw, seg, dq, dk, dv, name):
    tm = 256

    def body(p_ref, qw_ref, kw_ref, seg_ref, dq_ref, dk_ref, dv_ref, dproj_ref, sums_ref):
        segv = seg_ref[...]
        sums = []
        for part, (w_ref, dn_ref) in enumerate(((qw_ref, dq_ref), (kw_ref, dk_ref))):
            raw = p_ref[:, part * D_MODEL:(part + 1) * D_MODEL].astype(F32)
            dn = dn_ref[...].astype(F32)
            r = _qk_rstd(raw, segv)
            gq = dn * w_ref[...]
            draw = r * gq - raw * (r * r * r) * (_segsum(raw * gq, segv) * (1.0 / HEAD_DIM))
            dproj_ref[:, part * D_MODEL:(part + 1) * D_MODEL] = draw.astype(BF16)
            sums.append(jnp.sum(dn * raw * r, axis=0, keepdims=True))
        dproj_ref[:, 2 * D_MODEL:] = dv_ref[...]

        @pl.when(pl.program_id(0) == 0)
        def _():
            sums_ref[...] = jnp.zeros_like(sums_ref)

        sums_ref[...] += jnp.concatenate(sums + [jnp.zeros((6, D_MODEL), F32)], axis=0)

    return pl.pallas_call(
        body, name=name, grid=(SEQ // tm,),
        in_specs=[_row_spec(tm, 3 * D_MODEL), _vec_spec(1, D_MODEL), _vec_spec(1, D_MODEL), _vec_spec(256, 256)]
        + [_row_spec(tm, D_MODEL)] * 3,
        out_specs=[_row_spec(tm, 3 * D_MODEL), _vec_spec(8, D_MODEL)],
        out_shape=[jax.ShapeDtypeStruct((SEQ, 3 * D_MODEL), BF16), jax.ShapeDtypeStruct((8, D_MODEL), F32)],
        compiler_params=_params("arbitrary"),
    )(proj, qw, kw, seg, dq, dk, dv)


def _to_classes(a, dilation):
    if dilation == 1:
        return a
    s, c = a.shape
    return a.reshape(s // dilation, dilation, c).transpose(1, 0, 2).reshape(s, c)


def _from_classes(a, dilation):
    if dilation == 1:
        return a
    s, c = a.shape
    return a.reshape(dilation, s // dilation, c).transpose(1, 0, 2).reshape(s, c)


def _cols_to_classes(a, dilation):
    if dilation == 1:
        return a
    r, s = a.shape
    return a.reshape(r, s // dilation, dilation).transpose(0, 2, 1).reshape(r, s)


B_TN = 512
B_GROUP_TILES = 3 * D_MODEL // B_TN
B_Z_TILE0 = 3 * B_GROUP_TILES
B_Z_TILES = D_MODEL // B_TN


def _local_step(x, target, mods, norm_g, conv_w, conv_b, ln_g, ln_b, q_norm, k_norm,
                weights_a, weights_b, forward_weights_b, send_grads_b, forward_grads_b, send_grads_a):
    row = lambda a, i: a[i:i + 1]
    shift0, scale0, gate0 = row(mods[0], 0), row(mods[0], 1), row(mods[0], 2)
    shift1, scale1, gate1 = row(mods[1], 0), row(mods[1], 1), row(mods[1], 2)
    g0, g1 = row(norm_g, 0), row(norm_g, 1)
    seg = _seg_matrix()
    slopes = jnp.exp2(-8.0 * jnp.arange(1, N_HEADS + 1, dtype=F32) / N_HEADS)
    qw = [jnp.tile(q_norm[g:g + 1], (1, N_HEADS)) for g in range(3)]
    kw = [jnp.tile(k_norm[g:g + 1], (1, N_HEADS)) for g in range(3)]

    h0, h0t = _normmod_fwd(x, g0, scale0, shift0, "prenorm0")
    wa_in, wa_out = weights_a(h0)
    ja, _, nsa = wa_in.shape
    proj_a = _mm(h0, wa_in, tn=nsa, tile0=0, n_tiles=ja, out_dtype=F32, name="a_in")
    u5, u5t, u2 = _conv_fwd(proj_a, conv_w, conv_b, ln_g, ln_b, "a_conv")
    token = forward_weights_b(u5)
    x1, y_a, h1t, h1c = _out_a(u5, wa_out, x, gate0 + token[0:1, 0:1], g1, scale1, shift1, "a_out")

    wb_in, wb_out = weights_b(x1)
    jb, _, nsb = wb_in.shape
    h1 = h1c[0]
    h1tc = [_cols_to_classes(h1t, d) for d in DILATIONS]
    z_b = _mm(h1, wb_in, tn=B_TN, tile0=B_Z_TILE0, n_tiles=B_Z_TILES, out_dtype=F32, name="b_in_z")
    proj_g, qkv, o_parts, lse_parts = [], [], [], []
    for g, d in enumerate(DILATIONS):
        pg = _mm(h1c[g], wb_in, tn=B_TN, tile0=g * B_GROUP_TILES, n_tiles=B_GROUP_TILES, out_dtype=BF16,
                 name=f"b_in_g{g}")
        qn, kn = _qknorm_fwd(pg, qw[g], kw[g], seg, f"b_qknorm_g{g}")
        og, lg = _attn_fwd(qn, kn, pg, slopes, d, f"b_attn_g{g}")
        proj_g.append(pg)
        qkv.append((qn, kn))
        classes = (lambda a: a) if d == 1 else (lambda a: a.reshape(d, SEQ // d, D_MODEL))
        o_parts.append(classes(og))
        lse_parts.append(classes(lg))
    u_b, u_bt, o_b, lse_b = _merge_fwd(o_parts, lse_parts, z_b, "b_merge")
    e, dy_b, sums_loss = _out_b_loss(u_b, wb_out, x1, gate1, target, "b_out_loss")

    dwb_out = _mm(u_bt, dy_b, tn=D_MODEL, tile0=0, n_tiles=1, out_dtype=BF16, name="b_dwout")
    du_b = _mm_nt_res(dy_b, wb_out, "b_dout")
    dz_b, do_c, delta_c, lse_c = _merge_bwd(du_b, o_b, lse_b, z_b, seg, "b_merge_bwd")
    dwb_in = _mm(h1t, dz_b, tn=B_TN, tile0=B_Z_TILE0, n_tiles=B_Z_TILES, out_dtype=BF16, name="b_dwin_z",
                 out3d=(jb, nsb))
    dh1_parts = [_mm_nt(dz_b, wb_in, tn=B_TN, tile0=B_Z_TILE0, n_tiles=B_Z_TILES, name="b_dh_z")]
    qk_sums = []
    for g, d in enumerate(DILATIONS):
        qn, kn = qkv[g]
        dq, dk, dv = _attn_bwd(qn, kn, proj_g[g], do_c[g], lse_c[g], delta_c[g], slopes, d, f"b_attn_bwd_g{g}")
        dproj, sums_qk = _qknorm_bwd(proj_g[g], qw[g], kw[g], seg, dq, dk, dv, f"b_qknorm_bwd_g{g}")
        qk_sums.append(sums_qk)
        dwb_in = _mm(h1tc[g], dproj, tn=B_TN, tile0=g * B_GROUP_TILES, n_tiles=B_GROUP_TILES, out_dtype=BF16,
                     name=f"b_dwin_g{g}", out3d=(jb, nsb), prev=dwb_in)
        dh = _mm_nt(dproj, wb_in, tn=B_TN, tile0=g * B_GROUP_TILES, n_tiles=B_GROUP_TILES, name=f"b_dh_g{g}")
        dh1_parts.append(dh)
    token = send_grads_b(dwb_in, dwb_out)
    dx1, sums_n1 = _normmod_bwd(x1, g1, scale1 + token[0:1, 0:1], dh1_parts, e, "prenorm1_bwd",
                                part_dilations=(1,) + DILATIONS)
    token = forward_grads_b(dx1)

    dy_a, sums_ga = _dgate_dy(dx1, y_a, gate0 + token[0:1, 0:1], "a_dgate")
    dwa_out = _mm(u5t, dy_a, tn=D_MODEL, tile0=0, n_tiles=1, out_dtype=BF16, name="a_dwout")
    du5 = _mm_nt_res(dy_a, wa_out, "a_dout")
    du2, dz_a, sums_ln = _conv_bwd_pointwise(du5, proj_a, u2, ln_g, ln_b, "a_conv_bwd_pw")
    dproj_a, dconv_w = _conv_bwd_taps(du2, dz_a, proj_a, conv_w, "a_conv_bwd_taps")
    dwa_in = _mm(h0t, dproj_a, tn=nsa, tile0=0, n_tiles=ja, out_dtype=BF16, name="a_dwin", out3d=(ja, nsa))
    token = send_grads_a(dwa_in, dwa_out)
    dh0 = _mm_nt(dproj_a, wa_in, tn=nsa, tile0=0, n_tiles=ja, name="a_dh", after=token)
    grad_x, sums_n0 = _normmod_bwd(x, g0, scale0, [dh0], dx1, "prenorm0_bwd")

    small = dict(
        dnorm_g=jnp.concatenate([sums_n0[0:1], sums_n1[0:1]], axis=0),
        dmod0=jnp.concatenate([sums_n0[2:3], sums_n0[1:2], sums_ga[0:1]], axis=0),
        dmod1=jnp.concatenate([sums_n1[2:3], sums_n1[1:2], sums_loss[0:1]], axis=0),
        dln_g=sums_ln[0:1], dln_b=sums_ln[1:2], dconv_b=sums_ln[2:3],
        dconv_w=dconv_w[:CONV_WIDTH],
        dq_norm=jnp.concatenate([s[0:1] for s in qk_sums], axis=0),
        dk_norm=jnp.concatenate([s[1:2] for s in qk_sums], axis=0),
        loss_cols=sums_loss[1:2],
    )
    return grad_x, small


def _adamw(w, g, m, v, name):
    rows, cols = w.shape
    tr = rows if rows <= 128 else 128
    c1 = 1.0 / (1.0 - ADAM_B1 ** ADAM_STEP)
    c2 = 1.0 / (1.0 - ADAM_B2 ** ADAM_STEP)

    def body(w_ref, g_ref, m_ref, v_ref, d_ref, mo_ref, vo_ref):
        gv = g_ref[...]
        mn = ADAM_B1 * m_ref[...] + (1.0 - ADAM_B1) * gv
        vn = ADAM_B2 * v_ref[...] + (1.0 - ADAM_B2) * (gv * gv)
        mo_ref[...] = mn
        vo_ref[...] = vn
        d_ref[...] = -ADAM_LR * ((mn * c1) / (jnp.sqrt(vn * c2) + ADAM_EPS) + ADAM_WD * w_ref[...])

    spec = pl.BlockSpec((tr, cols), lambda i: (i, 0))
    return pl.pallas_call(
        body, name=name, grid=(rows // tr,), in_specs=[spec] * 4, out_specs=[spec] * 3,
        out_shape=[jax.ShapeDtypeStruct((rows, cols), F32)] * 3,
        compiler_params=_params("parallel"),
    )(w, g, m, v)


def _cast_into_slot(w, chip_idx, name):
    rows, cols = w.shape
    tr = 256

    def body(ch_ref, w_ref, o_ref):
        o_ref[...] = w_ref[...].astype(BF16)

    return pl.pallas_call(
        body, name=name,
        grid_spec=pltpu.PrefetchScalarGridSpec(
            num_scalar_prefetch=1, grid=(rows // tr,),
            in_specs=[pl.BlockSpec((tr, cols), lambda i, ch: (i, 0))],
            out_specs=pl.BlockSpec((None, tr, cols), lambda i, ch: (ch[0], i, 0))),
        out_shape=jax.ShapeDtypeStruct((N_CHIPS, rows, cols), BF16), compiler_params=_params("parallel"),
    )(chip_idx, w)


def _position():
    x, y, c = lax.axis_index("x"), lax.axis_index("y"), lax.axis_index("c")
    return x, y, c


def _xor_peer(x, y, c, k):
    return (x ^ ((k >> 2) & 1), y ^ ((k >> 1) & 1), c ^ (k & 1))


def _chip_peer(x, y, k):
    return (x ^ ((k >> 1) & 1), y ^ (k & 1))


def _ada_forward(c_row, ada_w, ada_b, conv_w):
    ns = ada_w.shape[2]
    cw = conv_w.shape[1]

    def body(c_ref, w_ref, b_ref, cv_ref, mod_ref, sc_ref, cvo_ref,
             c_all, mp, parts, cv_parts, send1, recv1, send2, recv2, send3, recv3):
        x, y, c = _position()
        me = 4 * x + 2 * y + c
        chip = 2 * x + y

        def c_copy(k):
            return pltpu.make_async_remote_copy(
                src_ref=c_all.at[me], dst_ref=c_all.at[me], send_sem=send1.at[k - 1], recv_sem=recv1.at[k - 1],
                device_id=_xor_peer(x, y, c, k), device_id_type=MESH)

        def cv_copy(k):
            px, py = _chip_peer(x, y, k)
            return pltpu.make_async_remote_copy(
                src_ref=cv_parts.at[chip], dst_ref=cv_parts.at[chip], send_sem=send3.at[k - 1],
                recv_sem=recv3.at[k - 1], device_id=(px, py, c), device_id_type=MESH)

        c_all[me] = c_ref[...]
        cv_parts[chip] = cv_ref[...]
        for k in range(1, N_DEV):
            c_copy(k).start()
        for k in range(1, N_CHIPS):
            cv_copy(k).start()
        for k in range(1, N_DEV):
            c_copy(k).wait_recv()
        cv = jnp.concatenate([c_all[i] for i in range(N_DEV)], axis=0)
        sc = cv * _sigmoid(cv)
        sc_ref[...] = sc
        for l in range(2):
            res = jnp.dot(sc, w_ref[l], preferred_element_type=F32, precision=lax.Precision.HIGHEST)
            for i in range(N_DEV):
                mp[i, l:l + 1, :] = res[i:i + 1, :]

        def mod_copy(k):
            px, py = _chip_peer(x, y, k)
            return pltpu.make_async_remote_copy(
                src_ref=mp.at[4 * px + 2 * py + c], dst_ref=parts.at[chip], send_sem=send2.at[k - 1],
                recv_sem=recv2.at[k - 1], device_id=(px, py, c), device_id_type=MESH)

        for k in range(1, N_CHIPS):
            mod_copy(k).start()
        parts[chip] = mp[me]
        for k in range(1, N_CHIPS):
            mod_copy(k).wait_recv()
            cv_copy(k).wait_recv()
        mod_ref[...] = jnp.concatenate([parts[j] for j in range(N_CHIPS)], axis=1) + b_ref[...]
        cvo_ref[...] = jnp.concatenate([cv_parts[j] for j in range(N_CHIPS)], axis=1)
        for k in range(1, N_DEV):
            c_copy(k).wait_send()
        for k in range(1, N_CHIPS):
            mod_copy(k).wait_send()
            cv_copy(k).wait_send()

    vm = pl.BlockSpec(memory_space=pltpu.VMEM)
    return pl.pallas_call(
        body, name="ada_forward",
        in_specs=[vm] * 4, out_specs=[vm] * 3,
        out_shape=[jax.ShapeDtypeStruct((2, 3 * D_MODEL), F32), jax.ShapeDtypeStruct((N_DEV, D_MODEL), F32),
                   jax.ShapeDtypeStruct((CONV_WIDTH, N_CHIPS * cw), F32)],
        scratch_shapes=[pltpu.VMEM((N_DEV, 1, D_MODEL), F32), pltpu.VMEM((N_DEV, 2, ns), F32),
                        pltpu.VMEM((N_CHIPS, 2, ns), F32), pltpu.VMEM((N_CHIPS, CONV_WIDTH, cw), F32),
                        pltpu.SemaphoreType.DMA((N_DEV - 1,)), pltpu.SemaphoreType.DMA((N_DEV - 1,)),
                        pltpu.SemaphoreType.DMA((N_CHIPS - 1,)), pltpu.SemaphoreType.DMA((N_CHIPS - 1,)),
                        pltpu.SemaphoreType.DMA((N_CHIPS - 1,)), pltpu.SemaphoreType.DMA((N_CHIPS - 1,))],
        compiler_params=pltpu.CompilerParams(vmem_limit_bytes=VMEM_LIMIT_BYTES),
    )(c_row, ada_w, ada_b, conv_w)


HBM_SPEC = pl.BlockSpec(memory_space=pltpu.HBM)
ANY_SPEC = pl.BlockSpec(memory_space=pl.ANY)
SEM_SPEC = pl.BlockSpec(memory_space=pltpu.SEMAPHORE)
SPLIT_PARAMS = dict(compiler_params=pltpu.CompilerParams(has_side_effects=pltpu.SideEffectType.DATAFLOW_SIDE_EFFECTING))
TOKEN = jax.ShapeDtypeStruct((8, 128), F32)


def _hbm(arrays):
    return [pltpu.with_memory_space_constraint(a, pltpu.HBM) for a in arrays]


def _hbm_like(arrays):
    return [pltpu.HBM(a.shape, a.dtype) for a in arrays]


def _gather_start(lands, after, name):
    n = len(lands)

    def body(*refs):
        ins = refs[:n]
        send, recv = refs[n + 1], refs[n + 2]
        x, y, c = _position()
        chip = 2 * x + y
        for t in range(n):
            rh = ins[t].shape[1] // 2
            for k in range(1, N_CHIPS):
                px, py = _chip_peer(x, y, k)
                block = ins[t].at[chip, pl.ds(c * rh, rh)]
                pltpu.make_async_remote_copy(
                    src_ref=block, dst_ref=block, send_sem=send.at[3 * t + k - 1], recv_sem=recv.at[3 * t + k - 1],
                    device_id=(px, py, c), device_id_type=MESH).start()
        refs[-1][...] = jnp.zeros(TOKEN.shape, F32)

    res = pl.pallas_call(
        body, name=name, in_specs=[HBM_SPEC] * n + [ANY_SPEC],
        out_specs=(SEM_SPEC, SEM_SPEC, *[HBM_SPEC] * n, pl.BlockSpec(memory_space=pltpu.VMEM)),
        out_shape=(pltpu.SemaphoreType.DMA((3 * n,)), pltpu.SemaphoreType.DMA((3 * n,)), *_hbm_like(lands), TOKEN),
        input_output_aliases={t: 2 + t for t in range(n)}, **SPLIT_PARAMS,
    )(*_hbm(lands), after)
    return res[0], res[1], list(res[2:2 + n]), res[-1]


def _gather_forward(send, recv, lands, after, name):
    n = len(lands)

    def body(*refs):
        ins = refs[:n]
        send1, recv1 = refs[n], refs[n + 1]
        send2, recv2 = refs[n + 3], refs[n + 4]
        x, y, c = _position()
        chip = 2 * x + y
        for t in range(n):
            rh = ins[t].shape[1] // 2
            half = pl.ds(c * rh, rh)
            for k in range(1, N_CHIPS):
                px, py = _chip_peer(x, y, k)
                s = 3 * t + k - 1
                got = ins[t].at[2 * px + py, half]
                cp = pltpu.make_async_remote_copy(
                    src_ref=ins[t].at[chip, half], dst_ref=got, send_sem=send1.at[s], recv_sem=recv1.at[s],
                    device_id=(px, py, c), device_id_type=MESH)
                cp.wait_send()
                cp.wait_recv()
                pltpu.make_async_remote_copy(
                    src_ref=got, dst_ref=got, send_sem=send2.at[s], recv_sem=recv2.at[s],
                    device_id=(x, y, 1 - c), device_id_type=MESH).start()
        refs[-1][...] = jnp.zeros(TOKEN.shape, F32)

    res = pl.pallas_call(
        body, name=name, in_specs=[HBM_SPEC] * n + [SEM_SPEC, SEM_SPEC, ANY_SPEC],
        out_specs=(SEM_SPEC, SEM_SPEC, *[HBM_SPEC] * n, pl.BlockSpec(memory_space=pltpu.VMEM)),
        out_shape=(pltpu.SemaphoreType.DMA((3 * n,)), pltpu.SemaphoreType.DMA((3 * n,)), *_hbm_like(lands), TOKEN),
        input_output_aliases={t: 2 + t for t in range(n)}, **SPLIT_PARAMS,
    )(*lands, send, recv, after)
    return res[0], res[1], list(res[2:2 + n]), res[-1]


def _gather_wait(send, recv, lands, after, name):
    n = len(lands)

    def body(*refs):
        ins = refs[:n]
        send_ref, recv_ref = refs[n], refs[n + 1]
        x, y, c = _position()
        for t in range(n):
            rh = ins[t].shape[1] // 2
            for k in range(1, N_CHIPS):
                px, py = _chip_peer(x, y, k)
                cp = pltpu.make_async_remote_copy(
                    src_ref=ins[t].at[2 * px + py, pl.ds(c * rh, rh)],
                    dst_ref=ins[t].at[2 * px + py, pl.ds((1 - c) * rh, rh)], send_sem=send_ref.at[3 * t + k - 1],
                    recv_sem=recv_ref.at[3 * t + k - 1], device_id=(x, y, 1 - c), device_id_type=MESH)
                cp.wait_send()
                cp.wait_recv()

    res = pl.pallas_call(
        body, name=name, in_specs=[HBM_SPEC] * n + [SEM_SPEC, SEM_SPEC, ANY_SPEC], out_specs=[HBM_SPEC] * n,
        out_shape=_hbm_like(lands), input_output_aliases={t: t for t in range(n)}, **SPLIT_PARAMS,
    )(*lands, send, recv, after)
    return list(res)


def _reduce_start(grads, after, name):
    n = len(grads)
    lands = [lax.empty((N_DEV, g.shape[1] // 2, g.shape[2]), BF16) for g in grads]

    def body(*refs):
        gs, ls = refs[:n], refs[n:2 * n]
        send, recv = refs[2 * n + 1], refs[2 * n + 2]
        x, y, c = _position()
        me = 4 * x + 2 * y + c
        for t in range(n):
            rh = gs[t].shape[1] // 2
            for k in range(1, N_DEV):
                px, py, pc = _xor_peer(x, y, c, k)
                pltpu.make_async_remote_copy(
                    src_ref=gs[t].at[2 * px + py, pl.ds(pc * rh, rh)], dst_ref=ls[t].at[me],
                    send_sem=send.at[7 * t + k - 1], recv_sem=recv.at[7 * t + k - 1],
                    device_id=(px, py, pc), device_id_type=MESH).start()
        refs[-1][...] = jnp.zeros(TOKEN.shape, F32)

    res = pl.pallas_call(
        body, name=name, in_specs=[HBM_SPEC] * (2 * n) + [ANY_SPEC],
        out_specs=(SEM_SPEC, SEM_SPEC, *[HBM_SPEC] * (2 * n), pl.BlockSpec(memory_space=pltpu.VMEM)),
        out_shape=(pltpu.SemaphoreType.DMA((7 * n,)), pltpu.SemaphoreType.DMA((7 * n,)),
                   *_hbm_like(grads), *_hbm_like(lands), TOKEN),
        input_output_aliases={t: 2 + t for t in range(2 * n)}, **SPLIT_PARAMS,
    )(*_hbm(grads), *_hbm(lands), after)
    return res[0], res[1], list(res[2:2 + n]), list(res[2 + n:2 + 2 * n]), res[-1]


def _reduce_wait(send, recv, grads, lands, after, name):
    n = len(grads)

    def body(*refs):
        gs, ls = refs[:n], refs[n:2 * n]
        send_ref, recv_ref = refs[2 * n], refs[2 * n + 1]
        x, y, c = _position()
        for t in range(n):
            rh = gs[t].shape[1] // 2
            for k in range(1, N_DEV):
                px, py, pc = _xor_peer(x, y, c, k)
                cp = pltpu.make_async_remote_copy(
                    src_ref=gs[t].at[2 * px + py, pl.ds(pc * rh, rh)], dst_ref=ls[t].at[4 * px + 2 * py + pc],
                    send_sem=send_ref.at[7 * t + k - 1], recv_sem=recv_ref.at[7 * t + k - 1],
                    device_id=(px, py, pc), device_id_type=MESH)
                cp.wait_send()
                cp.wait_recv()

    res = pl.pallas_call(
        body, name=name, in_specs=[HBM_SPEC] * (2 * n) + [SEM_SPEC, SEM_SPEC, ANY_SPEC], out_specs=[HBM_SPEC] * (2 * n),
        out_shape=_hbm_like(grads) + _hbm_like(lands), input_output_aliases={t: t for t in range(2 * n)}, **SPLIT_PARAMS,
    )(*grads, *lands, send, recv, after)
    return list(res[:n]), list(res[n:])


def _sum_devices(land, grad, dev_idx, name):
    _, rh, cols = land.shape
    tr = 128
    nb = rh // tr

    def body(idx_ref, l_ref, g_ref, o_ref):
        me = idx_ref[0]
        acc = jnp.where(me == 0, g_ref[...], l_ref[0]).astype(F32)
        for d in range(1, N_DEV):
            acc = acc + jnp.where(me == d, g_ref[...], l_ref[d]).astype(F32)
        o_ref[...] = acc

    return pl.pallas_call(
        body, name=name,
        grid_spec=pltpu.PrefetchScalarGridSpec(
            num_scalar_prefetch=1, grid=(nb,),
            in_specs=[pl.BlockSpec((N_DEV, tr, cols), lambda i, idx: (0, i, 0)),
                      pl.BlockSpec((None, tr, cols), lambda i, idx: (idx[1], idx[2] * nb + i, 0))],
            out_specs=pl.BlockSpec((tr, cols), lambda i, idx: (idx[2] * nb + i, 0))),
        out_shape=jax.ShapeDtypeStruct((2 * rh, cols), F32), compiler_params=_params("parallel"),
    )(dev_idx, land, grad)


def _split_start(name, arrays, n_sems, after, issue):
    m = len(arrays)

    def body(*refs):
        issue(refs[:m], refs[m + 1], refs[m + 2])
        refs[-1][...] = jnp.zeros(TOKEN.shape, F32)

    res = pl.pallas_call(
        body, name=name, in_specs=[HBM_SPEC] * m + [ANY_SPEC],
        out_specs=(SEM_SPEC, SEM_SPEC, *[HBM_SPEC] * m, pl.BlockSpec(memory_space=pltpu.VMEM)),
        out_shape=(pltpu.SemaphoreType.DMA((n_sems,)), pltpu.SemaphoreType.DMA((n_sems,)), *_hbm_like(arrays), TOKEN),
        input_output_aliases={t: 2 + t for t in range(m)}, **SPLIT_PARAMS,
    )(*_hbm(arrays), after)
    return res[0], res[1], list(res[2:2 + m]), res[-1]


def _split_wait(name, arrays, send, recv, after, await_all):
    m = len(arrays)

    def body(*refs):
        await_all(refs[:m], refs[m], refs[m + 1])

    res = pl.pallas_call(
        body, name=name, in_specs=[HBM_SPEC] * m + [SEM_SPEC, SEM_SPEC, ANY_SPEC], out_specs=[HBM_SPEC] * m,
        out_shape=_hbm_like(arrays), input_output_aliases={t: t for t in range(m)}, **SPLIT_PARAMS,
    )(*arrays, send, recv, after)
    return list(res)


def _sibling_copies(refs, send, recv, n):
    x, y, c = _position()
    cps = []
    for t in range(n):
        rh = refs[t].shape[1] // 2
        cps.append(pltpu.make_async_remote_copy(
            src_ref=refs[t].at[pl.ds(0, N_CHIPS), pl.ds((1 - c) * rh, rh)], dst_ref=refs[n + t],
            send_sem=send.at[t], recv_sem=recv.at[t], device_id=(x, y, 1 - c), device_id_type=MESH))
    return cps


def _reduce_sibling_start(grads, after, name):
    n = len(grads)
    lands = [lax.empty((N_CHIPS, g.shape[1] // 2, g.shape[2]), BF16) for g in grads]

    def issue(refs, send, recv):
        for cp in _sibling_copies(refs, send, recv, n):
            cp.start()

    return _split_start(name, list(grads) + lands, n, after, issue)


def _reduce_sibling_wait(send, recv, arrays, after, name):
    n = len(arrays) // 2

    def await_all(refs, send_ref, recv_ref):
        for cp in _sibling_copies(refs, send_ref, recv_ref, n):
            cp.wait_send()
            cp.wait_recv()

    res = _split_wait(name, arrays, send, recv, after, await_all)
    return res[:n], res[n:]


def _add_sibling_half(grad, got, dev_idx, name):
    j, r, cols = grad.shape
    rh = r // 2
    tr = 128
    nb = rh // tr

    def body(idx_ref, g_ref, got_ref, out_ref):
        out_ref[...] = (g_ref[...].astype(F32) + got_ref[...].astype(F32)).astype(BF16)

    return pl.pallas_call(
        body, name=name,
        grid_spec=pltpu.PrefetchScalarGridSpec(
            num_scalar_prefetch=1, grid=(j, nb),
            in_specs=[pl.BlockSpec((None, tr, cols), lambda jj, i, idx: (jj, idx[2] * nb + i, 0)),
                      pl.BlockSpec((None, tr, cols), lambda jj, i, idx: (jj, i, 0))],
            out_specs=pl.BlockSpec((None, tr, cols), lambda jj, i, idx: (jj, i, 0))),
        out_shape=jax.ShapeDtypeStruct((j, rh, cols), BF16),
        compiler_params=_params("parallel", "parallel"),
    )(dev_idx, grad, got)


def _chip_copies(refs, send, recv, n, receiving):
    x, y, c = _position()
    chip = 2 * x + y
    cps = []
    for t in range(n):
        for k in range(1, N_CHIPS):
            px, py = _chip_peer(x, y, k)
            cps.append(pltpu.make_async_remote_copy(
                src_ref=refs[t].at[2 * px + py], dst_ref=refs[n + t].at[2 * px + py if receiving else chip],
                send_sem=send.at[3 * t + k - 1], recv_sem=recv.at[3 * t + k - 1],
                device_id=(px, py, c), device_id_type=MESH))
    return cps


def _reduce_chips_start(partials, after, name):
    n = len(partials)
    lands = [lax.empty(p.shape, BF16) for p in partials]

    def issue(refs, send, recv):
        for cp in _chip_copies(refs, send, recv, n, False):
            cp.start()

    return _split_start(name, list(partials) + lands, 3 * n, after, issue)


def _reduce_chips_wait(send, recv, arrays, after, name):
    n = len(arrays) // 2

    def await_all(refs, send_ref, recv_ref):
        for cp in _chip_copies(refs, send_ref, recv_ref, n, True):
            cp.wait_send()
            cp.wait_recv()

    res = _split_wait(name, arrays, send, recv, after, await_all)
    return res[:n], res[n:]


def _sum_partials(land, partial, dev_idx, name):
    _, rh, cols = land.shape
    tr = 128
    nb = rh // tr

    def body(idx_ref, l_ref, p_ref, o_ref):
        chip = idx_ref[1]
        acc = jnp.where(chip == 0, p_ref[...], l_ref[0]).astype(F32)
        for s in range(1, N_CHIPS):
            acc = acc + jnp.where(chip == s, p_ref[...], l_ref[s]).astype(F32)
        o_ref[...] = acc

    return pl.pallas_call(
        body, name=name,
        grid_spec=pltpu.PrefetchScalarGridSpec(
            num_scalar_prefetch=1, grid=(nb,),
            in_specs=[pl.BlockSpec((N_CHIPS, tr, cols), lambda i, idx: (0, i, 0)),
                      pl.BlockSpec((None, tr, cols), lambda i, idx: (idx[1], i, 0))],
            out_specs=pl.BlockSpec((tr, cols), lambda i, idx: (idx[2] * nb + i, 0))),
        out_shape=jax.ShapeDtypeStruct((2 * rh, cols), F32), compiler_params=_params("parallel"),
    )(dev_idx, land, partial)


def _share_halves(totals):
    n = len(totals)

    def body(*refs):
        ins, outs = refs[:n], refs[n:2 * n]
        send, recv = refs[2 * n:]
        x, y, c = _position()
        cps = []
        for t in range(n):
            rh = ins[t].shape[0] // 2
            mine = pl.ds(c * rh, rh)
            cp = pltpu.make_async_remote_copy(
                src_ref=ins[t].at[mine], dst_ref=outs[t].at[mine], send_sem=send.at[t], recv_sem=recv.at[t],
                device_id=(x, y, 1 - c), device_id_type=MESH)
            cp.start()
            cps.append(cp)
        for cp in cps:
            cp.wait()

    return pl.pallas_call(
        body, name="reduce_share_" + "_".join(str(t.shape[1]) for t in totals), in_specs=[ANY_SPEC] * n,
        out_specs=[ANY_SPEC] * n, out_shape=[jax.ShapeDtypeStruct(t.shape, F32) for t in totals],
        input_output_aliases={t: t for t in range(n)},
        scratch_shapes=[pltpu.SemaphoreType.DMA((n,)), pltpu.SemaphoreType.DMA((n,))],
    )(*totals)


def _exchange_halves(grads):
    n = len(grads)
    hbm = pl.BlockSpec(memory_space=pl.ANY)

    def body(*refs):
        ins, outs = refs[:n], refs[n:2 * n]
        send, recv = refs[2 * n:]
        x, y, c = _position()
        cps = []
        for t in range(n):
            rh = ins[t].shape[1] // 2
            cp = pltpu.make_async_remote_copy(
                src_ref=ins[t].at[pl.ds(0, N_CHIPS), pl.ds((1 - c) * rh, rh)], dst_ref=outs[t], send_sem=send.at[t],
                recv_sem=recv.at[t], device_id=(x, y, 1 - c), device_id_type=MESH)
            cp.start()
            cps.append(cp)
        for cp in cps:
            cp.wait()

    return pl.pallas_call(
        body, name="reduce_exchange_halves", in_specs=[hbm] * n, out_specs=[hbm] * n,
        out_shape=[jax.ShapeDtypeStruct((g.shape[0], g.shape[1] // 2, g.shape[2]), BF16) for g in grads],
        scratch_shapes=[pltpu.SemaphoreType.DMA((n,)), pltpu.SemaphoreType.DMA((n,))],
    )(*grads)


def _add_halves(grad, got, c_idx, name):
    j, r, cols = grad.shape
    rh = r // 2
    tr = 128
    nb = rh // tr

    def body(c_ref, g_ref, o_ref_in, out_ref):
        out_ref[...] = (g_ref[...].astype(F32) + o_ref_in[...].astype(F32)).astype(BF16)

    return pl.pallas_call(
        body, name=name,
        grid_spec=pltpu.PrefetchScalarGridSpec(
            num_scalar_prefetch=1, grid=(j, nb),
            in_specs=[pl.BlockSpec((None, tr, cols), lambda jj, i, c_ref: (jj, c_ref[0] * nb + i, 0)),
                      pl.BlockSpec((None, tr, cols), lambda jj, i, c_ref: (jj, i, 0))],
            out_specs=pl.BlockSpec((None, tr, cols), lambda jj, i, c_ref: (jj, i, 0))),
        out_shape=jax.ShapeDtypeStruct((j, rh, cols), BF16),
        compiler_params=_params("parallel", "parallel"),
    )(c_idx, grad, got)


def _scatter_partials(partials):
    n = len(partials)
    hbm = pl.BlockSpec(memory_space=pl.ANY)

    def body(*refs):
        ins, outs = refs[:n], refs[n:2 * n]
        send, recv, local = refs[2 * n:]
        x, y, c = _position()
        chip = 2 * x + y
        cps, lcs = [], []
        for t in range(n):
            lc = pltpu.make_async_copy(ins[t].at[chip], outs[t].at[chip], local.at[t])
            lc.start()
            lcs.append(lc)
            for k in range(1, N_CHIPS):
                px, py = _chip_peer(x, y, k)
                s = 3 * t + k - 1
                cp = pltpu.make_async_remote_copy(
                    src_ref=ins[t].at[2 * px + py], dst_ref=outs[t].at[chip], send_sem=send.at[s],
                    recv_sem=recv.at[s], device_id=(px, py, c), device_id_type=MESH)
                cp.start()
                cps.append(cp)
        for cp in cps:
            cp.wait()
        for lc in lcs:
            lc.wait()

    return pl.pallas_call(
        body, name="reduce_scatter_partials", in_specs=[hbm] * n, out_specs=[hbm] * n,
        out_shape=[jax.ShapeDtypeStruct(p.shape, BF16) for p in partials],
        scratch_shapes=[pltpu.SemaphoreType.DMA((3 * n,)), pltpu.SemaphoreType.DMA((3 * n,)),
                        pltpu.SemaphoreType.DMA((n,))],
    )(*partials)


def _sum_chips(parts, name):
    j, rh, cols = parts.shape
    tr = 128

    def body(p_ref, o_ref):
        acc = p_ref[0].astype(F32)
        for s in range(1, j):
            acc = acc + p_ref[s].astype(F32)
        o_ref[...] = acc

    return pl.pallas_call(
        body, name=name, grid=(rh // tr,),
        in_specs=[pl.BlockSpec((j, tr, cols), lambda i: (0, i, 0))],
        out_specs=pl.BlockSpec((tr, cols), lambda i: (i, 0)),
        out_shape=jax.ShapeDtypeStruct((rh, cols), F32),
        compiler_params=_params("parallel"),
    )(parts)


def _share_totals(halves):
    n = len(halves)
    hbm = pl.BlockSpec(memory_space=pl.ANY)

    def body(*refs):
        ins, outs = refs[:n], refs[n:2 * n]
        send, recv, local = refs[2 * n:]
        x, y, c = _position()
        cps, lcs = [], []
        for t in range(n):
            rh = ins[t].shape[0]
            mine = outs[t].at[pl.ds(c * rh, rh)]
            lc = pltpu.make_async_copy(ins[t], mine, local.at[t])
            lc.start()
            lcs.append(lc)
            cp = pltpu.make_async_remote_copy(
                src_ref=ins[t], dst_ref=mine, send_sem=send.at[t], recv_sem=recv.at[t],
                device_id=(x, y, 1 - c), device_id_type=MESH)
            cp.start()
            cps.append(cp)
        for cp in cps:
            cp.wait()
        for lc in lcs:
            lc.wait()

    return pl.pallas_call(
        body, name="reduce_share_totals", in_specs=[hbm] * n, out_specs=[hbm] * n,
        out_shape=[jax.ShapeDtypeStruct((2 * h.shape[0], h.shape[1]), F32) for h in halves],
        scratch_shapes=[pltpu.SemaphoreType.DMA((n,)), pltpu.SemaphoreType.DMA((n,)),
                        pltpu.SemaphoreType.DMA((n,))],
    )(*halves)


SMALL_ROWS = 56


def _reduce_small(packed, silu_c):
    ns = 3 * D_MODEL // N_CHIPS

    def body(p_ref, sc_ref, tot_ref, gw_ref, loss_ref, qk_ref, allp, send, recv):
        x, y, c = _position()
        me = 4 * x + 2 * y + c
        chip = 2 * x + y

        def copy(k):
            return pltpu.make_async_remote_copy(
                src_ref=allp.at[me], dst_ref=allp.at[me], send_sem=send.at[k - 1], recv_sem=recv.at[k - 1],
                device_id=_xor_peer(x, y, c, k), device_id_type=MESH)

        allp[me] = p_ref[...]
        for k in range(1, N_DEV):
            copy(k).start()
        for k in range(1, N_DEV):
            copy(k).wait_recv()
        tot = allp[0]
        for i in range(1, N_DEV):
            tot = tot + allp[i]
        tot_ref[...] = tot
        loss_ref[...] = jnp.sum(tot[11:12, :], axis=1, keepdims=True) * (0.5 / D_MODEL)
        fold = tot[5:11, 0:HEAD_DIM]
        for h in range(1, N_HEADS):
            fold = fold + tot[5:11, h * HEAD_DIM:(h + 1) * HEAD_DIM]
        qk_ref[...] = jnp.concatenate([fold, jnp.zeros((2, HEAD_DIM), F32)], axis=0)
        sct = sc_ref[...].T
        rc = 64
        for l in range(2):
            dms = [allp[i, pl.ds(12 + 4 * l + chip, 1), :][:, :ns] for i in range(N_DEV)]
            for r0 in range(0, D_MODEL, rc):
                acc = sct[r0:r0 + rc, 0:1] * dms[0]
                for i in range(1, N_DEV):
                    acc = acc + sct[r0:r0 + rc, i:i + 1] * dms[i]
                gw_ref[l, r0:r0 + rc, :] = acc
        for k in range(1, N_DEV):
            copy(k).wait_send()

    vm = pl.BlockSpec(memory_space=pltpu.VMEM)
    return pl.pallas_call(
        body, name="reduce_small", in_specs=[vm, vm], out_specs=[vm] * 4,
        out_shape=[jax.ShapeDtypeStruct((SMALL_ROWS, D_MODEL), F32), jax.ShapeDtypeStruct((2, D_MODEL, ns), F32),
                   jax.ShapeDtypeStruct((1, 1), F32), jax.ShapeDtypeStruct((8, HEAD_DIM), F32)],
        scratch_shapes=[pltpu.VMEM((N_DEV, SMALL_ROWS, D_MODEL), F32),
                        pltpu.SemaphoreType.DMA((N_DEV - 1,)), pltpu.SemaphoreType.DMA((N_DEV - 1,))],
        compiler_params=pltpu.CompilerParams(vmem_limit_bytes=VMEM_LIMIT_BYTES),
    )(packed, silu_c)


def _reduce_big(grads, c_idx):
    names = list(grads)
    got = _exchange_halves([grads[k] for k in names])
    partials = [_add_halves(grads[k], got[i], c_idx, f"reduce_add_{k}") for i, k in enumerate(names)]
    parts = _scatter_partials(partials)
    halves = [_sum_chips(parts[i], f"reduce_sum_{k}") for i, k in enumerate(names)]
    totals = _share_totals(halves)
    return dict(zip(names, totals))


def kernel(x, c, norm_g, ada_w, ada_b, a_w_in, a_conv_w, a_conv_b, a_ln_g, a_ln_b, a_w_out, b_w_in, b_q_norm, b_k_norm, b_w_out, loss_target, m_norm_g, m_ada_w, m_ada_b, m_a_w_in, m_a_conv_w, m_a_conv_b, m_a_ln_g, m_a_ln_b, m_a_w_out, m_b_w_in, m_b_q_norm, m_b_k_norm, m_b_w_out, v_norm_g, v_ada_w, v_ada_b, v_a_w_in, v_a_conv_w, v_a_conv_b, v_a_ln_g, v_a_ln_b, v_a_w_out, v_b_w_in, v_b_q_norm, v_b_k_norm, v_b_w_out):
    chip = 2 * lax.axis_index("x") + lax.axis_index("y")
    core = lax.axis_index("c")
    chip_idx = chip.astype(jnp.int32).reshape(1)
    dev_idx = jnp.stack([2 * chip + core, chip, core]).astype(jnp.int32)

    mods, silu_c, conv_w_full = _ada_forward(c, ada_w, ada_b, a_conv_w[0])
    lands_a = [_cast_into_slot(a_w_in[0], chip_idx, "cast_a_w_in"), _cast_into_slot(a_w_out[0], chip_idx, "cast_a_w_out")]
    send_a, recv_a, lands_a, token_a = _gather_start(lands_a, mods, "gather_start_a")
    lands_b = [_cast_into_slot(b_w_in[0], chip_idx, "cast_b_w_in"), _cast_into_slot(b_w_out[0], chip_idx, "cast_b_w_out")]
    send_b, recv_b, lands_b, token_b = _gather_start(lands_b, token_a, "gather_start_b")
    mods = mods + token_b[0:2, 0:1]

    def weights_a(after):
        send, recv, lands, _ = _gather_forward(send_a, recv_a, lands_a, after, "gather_forward_a")
        w_in, w_out = _gather_wait(send, recv, lands, after, "gather_wait_a")
        return w_in, w_out.reshape(D_MODEL, D_MODEL)

    forwarded_b = []

    def weights_b(after):
        send, recv, lands, _ = forwarded_b
        w_in, w_out = _gather_wait(send, recv, lands, after, "gather_wait_b")
        return w_in, w_out.reshape(D_MODEL, D_MODEL)

    def forward_weights_b(after):
        forwarded_b.extend(_gather_forward(send_b, recv_b, lands_b, after, "gather_forward_b"))
        return forwarded_b[3]

    stage1, stage2 = {}, {}

    def send_grads(tag, dw_in, dw_out):
        grads = [dw_in, dw_out.reshape(N_CHIPS, D_MODEL // N_CHIPS, D_MODEL)]
        send, recv, arrays, token = _reduce_sibling_start(grads, dw_out, f"reduce_d2d_start_{tag}")
        stage1[tag] = (send, recv, arrays)
        return token

    def forward_grads(tag, after):
        send, recv, arrays = stage1[tag]
        grads, got = _reduce_sibling_wait(send, recv, arrays, after, f"reduce_d2d_wait_{tag}")
        partials = [_add_sibling_half(grads[i], got[i], dev_idx, f"reduce_add_{tag}_{i}") for i in range(2)]
        send, recv, arrays, token = _reduce_chips_start(partials, partials[1], f"reduce_ici_start_{tag}")
        stage2[tag] = (send, recv, arrays)
        return token

    def finish_grads(tag, after):
        send, recv, arrays = stage2[tag]
        partials, lands = _reduce_chips_wait(send, recv, arrays, after, f"reduce_ici_wait_{tag}")
        totals = [_sum_partials(lands[i], partials[i], dev_idx, f"reduce_sum_{tag}_{i}") for i in range(2)]
        return _share_halves(totals)

    grad_x, small = _local_step(
        x[0], loss_target[0], mods.reshape(2, 3, D_MODEL), norm_g, conv_w_full, a_conv_b, a_ln_g[0:1],
        a_ln_b[0:1], b_q_norm[0], b_k_norm[0], weights_a, weights_b, forward_weights_b,
        functools.partial(send_grads, "b"), functools.partial(forward_grads, "b"), functools.partial(send_grads, "a"))

    ns = 3 * D_MODEL // N_CHIPS
    pad_mod = lambda dm: jnp.pad(dm.reshape(N_CHIPS, ns), ((0, 0), (0, D_MODEL - ns)))
    packed = jnp.concatenate([
        small["dnorm_g"], small["dconv_b"], small["dln_g"], small["dln_b"], small["dq_norm"], small["dk_norm"],
        small["loss_cols"], pad_mod(small["dmod0"]), pad_mod(small["dmod1"]), small["dconv_w"],
        jnp.zeros((SMALL_ROWS - 20 - CONV_WIDTH, D_MODEL), F32)], axis=0)
    tot, g_ada_w, loss, qk = _reduce_small(packed, silu_c)
    cw = D_MODEL // N_CHIPS
    g_small = dict(
        norm_g=tot[0:2], a_conv_b=tot[2:3], a_ln_g=tot[3:4], a_ln_b=tot[4:5],
        b_q_norm=qk[0:3], b_k_norm=qk[3:6],
        ada_b=jnp.stack([tot[12:16, :ns].reshape(3 * D_MODEL), tot[16:20, :ns].reshape(3 * D_MODEL)]),
        a_conv_w=lax.dynamic_slice(tot[20:20 + CONV_WIDTH], (0, chip * cw), (CONV_WIDTH, cw)),
    )


    given = dict(norm_g=(norm_g, m_norm_g, v_norm_g), ada_w=(ada_w, m_ada_w, v_ada_w), ada_b=(ada_b, m_ada_b, v_ada_b),
                 a_w_in=(a_w_in, m_a_w_in, v_a_w_in), a_conv_w=(a_conv_w, m_a_conv_w, v_a_conv_w),
                 a_conv_b=(a_conv_b, m_a_conv_b, v_a_conv_b), a_ln_g=(a_ln_g, m_a_ln_g, v_a_ln_g),
                 a_ln_b=(a_ln_b, m_a_ln_b, v_a_ln_b), a_w_out=(a_w_out, m_a_w_out, v_a_w_out),
                 b_w_in=(b_w_in, m_b_w_in, v_b_w_in), b_q_norm=(b_q_norm, m_b_q_norm, v_b_q_norm),
                 b_k_norm=(b_k_norm, m_b_k_norm, v_b_k_norm), b_w_out=(b_w_out, m_b_w_out, v_b_w_out))
    order = ["norm_g", "ada_w", "ada_b", "a_w_in", "a_conv_w", "a_conv_b", "a_ln_g", "a_ln_b", "a_w_out", "b_w_in",
             "b_q_norm", "b_k_norm", "b_w_out"]
    outs = {}

    def update(k, g2):
        w, m, v = given[k]
        shape2 = g2.shape
        d2, m2, v2 = _adamw(w.reshape(shape2), g2, m.reshape(shape2), v.reshape(shape2), f"adamw_{k}")
        outs[k] = tuple(a.reshape(w.shape) for a in (g2, d2, m2, v2))

    token = forward_grads("a", tot)
    g_b_in, g_b_out = finish_grads("b", token)
    update("b_w_in", g_b_in)
    update("b_w_out", g_b_out)
    update("ada_w", g_ada_w.reshape(2 * D_MODEL, ns))
    for k, g2 in g_small.items():
        update(k, g2)
    g_a_in, g_a_out = finish_grads("a", outs["b_w_in"][1])
    update("a_w_in", g_a_in)
    update("a_w_out", g_a_out)
    return (loss.reshape(()), grad_x[None], *[outs[k][0] for k in order], *[outs[k][1] for k in order],
            *[outs[k][2] for k in order], *[outs[k][3] for k in order])
```

```python
import functools

import jax
import jax.numpy as jnp
from jax import lax
from jax.experimental import pallas as pl
from jax.experimental.pallas import tpu as pltpu

F32 = jnp.float32
BF16 = jnp.bfloat16

SEQ = 2048
D_MODEL = 1024
CONV_WIDTH = 31
HEAD_DIM = 64
N_HEADS = 16
DILATIONS = (1, 4, 16)
ATTN_BLOCK = 128
NORM_EPS = 1e-6
NEG_INF = -1e30
N_DEV = 8
N_CHIPS = 4

ADAM_LR = 0.001
ADAM_B1 = 0.9
ADAM_B2 = 0.999
ADAM_EPS = 1e-08
ADAM_WD = 0.01
ADAM_STEP = 10

VMEM_LIMIT_BYTES = 52 * 1024 * 1024
HALO = 32
LANES = 128
MESH = pl.DeviceIdType.MESH


def _params(*sem):
    return pltpu.CompilerParams(dimension_semantics=sem or None, vmem_limit_bytes=VMEM_LIMIT_BYTES)


def _sigmoid(v):
    return 1.0 / (1.0 + jnp.exp(-v))


def _row_spec(tm, cols, col_block=0):
    return pl.BlockSpec((tm, cols), lambda i: (i, col_block))


def _vec_spec(rows, cols):
    return pl.BlockSpec((rows, cols), lambda i: (0, 0))


def _normmod(xv, g, scale, shift):
    r = lax.rsqrt(jnp.mean(xv * xv, axis=-1, keepdims=True) + NORM_EPS)
    return xv * r * g * (1.0 + scale) + shift


def _normmod_fwd(x, g, scale, shift, name):
    tm = 256

    def body(x_ref, g_ref, sc_ref, sh_ref, h_ref, ht_ref):
        h = _normmod(x_ref[...], g_ref[...], sc_ref[...], sh_ref[...])
        h_ref[...] = h.astype(BF16)
        ht_ref[...] = h.T.astype(BF16)

    return pl.pallas_call(
        body, name=name, grid=(SEQ // tm,),
        in_specs=[_row_spec(tm, D_MODEL)] + [_vec_spec(1, D_MODEL)] * 3,
        out_specs=[_row_spec(tm, D_MODEL), pl.BlockSpec((D_MODEL, tm), lambda i: (0, i))],
        out_shape=[jax.ShapeDtypeStruct((SEQ, D_MODEL), BF16), jax.ShapeDtypeStruct((D_MODEL, SEQ), BF16)],
        compiler_params=_params("parallel"),
    )(x, g, scale, shift)


def _normmod_bwd(x, g, scale, dh_parts, dres, name, part_dilations=None):
    tm = 256
    n_parts = len(dh_parts)
    dils = part_dilations or (1,) * n_parts
    dh_parts = [p if d == 1 else p.reshape(d, SEQ // d, D_MODEL) for p, d in zip(dh_parts, dils)]

    def body(x_ref, g_ref, sc_ref, dres_ref, *rest):
        part_refs = rest[:n_parts]
        dx_ref, sums_ref, nat = rest[n_parts:]
        xv = x_ref[...]
        r = lax.rsqrt(jnp.mean(xv * xv, axis=-1, keepdims=True) + NORM_EPS)
        xn = xv * r
        dh = _load_natural(part_refs[0], nat, dils[0])
        for p, d in zip(part_refs[1:], dils[1:]):
            dh = dh + _load_natural(p, nat, d)
        gv = g_ref[...]
        one_sc = 1.0 + sc_ref[...]
        dxn = dh * (gv * one_sc)
        dx = r * (dxn - xn * jnp.mean(dxn * xn, axis=-1, keepdims=True))
        dx_ref[...] = dres_ref[...] + dx
        dhx = dh * xn
        sums = jnp.concatenate([
            jnp.sum(dhx, axis=0, keepdims=True) * one_sc,
            jnp.sum(dhx, axis=0, keepdims=True) * gv,
            jnp.sum(dh, axis=0, keepdims=True),
            jnp.zeros((5, D_MODEL), F32)], axis=0)

        @pl.when(pl.program_id(0) == 0)
        def _():
            sums_ref[...] = jnp.zeros_like(sums_ref)

        sums_ref[...] += sums

    return pl.pallas_call(
        body, name=name, grid=(SEQ // tm,),
        in_specs=[_row_spec(tm, D_MODEL), _vec_spec(1, D_MODEL), _vec_spec(1, D_MODEL), _row_spec(tm, D_MODEL)]
        + [_class_spec(tm, d) for d in dils],
        out_specs=[_row_spec(tm, D_MODEL), _vec_spec(8, D_MODEL)],
        out_shape=[jax.ShapeDtypeStruct((SEQ, D_MODEL), F32), jax.ShapeDtypeStruct((8, D_MODEL), F32)],
        scratch_shapes=[_natural_scratch(tm)],
        compiler_params=_params("arbitrary"),
    )(x, g, scale, dres, *dh_parts)


def _mm(lhs, rhs, *, tn, tile0, n_tiles, out_dtype, name, out3d=None, prev=None):
    mo, kc = lhs.shape
    cm = 512

    def body(l_ref, r_ref, *rest):
        o_ref = rest[-1]
        for m in range(mo // cm):
            rows = pl.ds(m * cm, cm)
            o_ref[rows, :] = jnp.dot(l_ref[rows, :], r_ref[...], preferred_element_type=F32).astype(out_dtype)

    if rhs.ndim == 3:
        tps_r = rhs.shape[2] // tn
        r_spec = pl.BlockSpec((None, kc, tn), lambda t: ((tile0 + t) // tps_r, 0, (tile0 + t) % tps_r))
    else:
        r_spec = pl.BlockSpec((kc, tn), lambda t: (0, t))
    in_specs = [pl.BlockSpec((mo, kc), lambda t: (0, 0)), r_spec]
    args = [lhs, rhs]
    aliases = {}
    if out3d is None:
        o_spec = pl.BlockSpec((mo, tn), lambda t: (0, t))
        o_shape = jax.ShapeDtypeStruct((mo, n_tiles * tn), out_dtype)
    else:
        j_out, ns_out = out3d
        tps_o = ns_out // tn
        o_spec = pl.BlockSpec((None, mo, tn), lambda t: ((tile0 + t) // tps_o, 0, (tile0 + t) % tps_o))
        o_shape = jax.ShapeDtypeStruct((j_out, mo, ns_out), out_dtype)
        if prev is not None:
            in_specs.append(pl.BlockSpec(memory_space=pl.ANY))
            args.append(prev)
            aliases = {2: 0}
    return pl.pallas_call(
        body, name=name, grid=(n_tiles,), in_specs=in_specs, out_specs=o_spec, out_shape=o_shape,
        input_output_aliases=aliases, compiler_params=_params("parallel"),
    )(*args)


def _mm_nt(dy, w3, *, tn, tile0, n_tiles, name, after=None):
    m_rows = dy.shape[0]
    _, kc, ns = w3.shape
    tps = ns // tn
    cm = 512
    extra = [] if after is None else [after]

    def body(dy_ref, w_ref, *rest):
        o_ref = rest[-1]

        @pl.when(pl.program_id(0) == 0)
        def _():
            o_ref[...] = jnp.zeros_like(o_ref)

        for m in range(m_rows // cm):
            rows = pl.ds(m * cm, cm)
            o_ref[rows, :] += lax.dot_general(dy_ref[rows, :], w_ref[...], (((1,), (1,)), ((), ())),
                                              preferred_element_type=F32)

    return pl.pallas_call(
        body, name=name, grid=(n_tiles,),
        in_specs=[pl.BlockSpec((m_rows, tn), lambda t: (0, t)),
                  pl.BlockSpec((None, kc, tn), lambda t: ((tile0 + t) // tps, 0, (tile0 + t) % tps))]
        + [pl.BlockSpec(memory_space=pl.ANY)] * len(extra),
        out_specs=pl.BlockSpec((m_rows, kc), lambda t: (0, 0)),
        out_shape=jax.ShapeDtypeStruct((m_rows, kc), F32),
        compiler_params=_params("arbitrary"),
    )(dy, w3, *extra)


CONV_CHUNK = 16


def _shift_copies(buf, shifted):
    rows = shifted.shape[1]
    for s in range(1, 8):
        shifted[s - 1] = buf[pl.ds(s, rows), :]


def _shifted_rows(buf, shifted, offset, r0):
    s = offset % 8
    if s == 0:
        return buf[pl.ds(r0 + offset, CONV_CHUNK), :]
    return shifted[s - 1, pl.ds(r0 + (offset - s), CONV_CHUNK), :]


def _conv_fwd(proj, conv_w, conv_b, ln_g, ln_b, name):
    tm = 256
    hb = tm // HALO

    def body(vg_ref, halo_ref, z_ref, w_ref, b_ref, g_ref, be_ref, u5_ref, u5t_ref, u2_ref, buf, shifted):
        i = pl.program_id(0)
        u1 = vg_ref[:, :D_MODEL] * _sigmoid(vg_ref[:, D_MODEL:])
        u1h = halo_ref[:, :D_MODEL] * _sigmoid(halo_ref[:, D_MODEL:])
        buf[pl.ds(0, HALO), :] = jnp.where(i > 0, u1h, 0.0)
        buf[pl.ds(HALO, tm), :] = u1
        _shift_copies(buf, shifted)

        def chunk(ci, carry):
            r0 = pl.multiple_of(ci * CONV_CHUNK, CONV_CHUNK)
            acc = jnp.broadcast_to(b_ref[...], (CONV_CHUNK, D_MODEL))
            for k in range(CONV_WIDTH):
                acc = acc + w_ref[k:k + 1, :] * _shifted_rows(buf, shifted, HALO - (CONV_WIDTH - 1) + k, r0)
            u2_ref[pl.ds(r0, CONV_CHUNK), :] = acc
            return carry

        lax.fori_loop(0, tm // CONV_CHUNK, chunk, 0)
        acc = u2_ref[...]
        mu = jnp.mean(acc, axis=-1, keepdims=True)
        xc = acc - mu
        rstd = lax.rsqrt(jnp.mean(xc * xc, axis=-1, keepdims=True) + NORM_EPS)
        u3 = xc * rstd * g_ref[...] + be_ref[...]
        zv = z_ref[...]
        u5 = u3 * _sigmoid(u3) * (zv * _sigmoid(zv))
        u5_ref[...] = u5.astype(BF16)
        u5t_ref[...] = u5.T.astype(BF16)

    return pl.pallas_call(
        body, name=name, grid=(SEQ // tm,),
        in_specs=[pl.BlockSpec((tm, 2 * D_MODEL), lambda i: (i, 0)),
                  pl.BlockSpec((HALO, 2 * D_MODEL), lambda i: (jnp.maximum(i * hb - 1, 0), 0)),
                  _row_spec(tm, D_MODEL, 2),
                  _vec_spec(CONV_WIDTH, D_MODEL)] + [_vec_spec(1, D_MODEL)] * 3,
        out_specs=[_row_spec(tm, D_MODEL), pl.BlockSpec((D_MODEL, tm), lambda i: (0, i)), _row_spec(tm, D_MODEL)],
        out_shape=[jax.ShapeDtypeStruct((SEQ, D_MODEL), BF16), jax.ShapeDtypeStruct((D_MODEL, SEQ), BF16),
                   jax.ShapeDtypeStruct((SEQ, D_MODEL), F32)],
        scratch_shapes=[pltpu.VMEM((HALO + tm, D_MODEL), F32), pltpu.VMEM((7, HALO + tm - 8, D_MODEL), F32)],
        compiler_params=_params("parallel"),
    )(proj, proj, proj, conv_w, conv_b, ln_g, ln_b)


def _conv_bwd_pointwise(du5, proj, u2, ln_g, ln_b, name):
    tm = 256

    def body(du5_ref, z_ref, u2_ref, g_ref, be_ref, du2_ref, dz_ref, sums_ref):
        u2v = u2_ref[...]
        mu = jnp.mean(u2v, axis=-1, keepdims=True)
        xc = u2v - mu
        rstd = lax.rsqrt(jnp.mean(xc * xc, axis=-1, keepdims=True) + NORM_EPS)
        xhat = xc * rstd
        u3 = xhat * g_ref[...] + be_ref[...]
        s3 = _sigmoid(u3)
        u4 = u3 * s3
        zv = z_ref[...]
        sz = _sigmoid(zv)
        du5v = du5_ref[...]
        dz_ref[...] = du5v * u4 * (sz * (1.0 + zv * (1.0 - sz)))
        du3 = du5v * (zv * sz) * (s3 * (1.0 + u3 * (1.0 - s3)))
        dxhat = du3 * g_ref[...]
        du2 = rstd * (dxhat - jnp.mean(dxhat, axis=-1, keepdims=True)
                      - xhat * jnp.mean(dxhat * xhat, axis=-1, keepdims=True))
        du2_ref[...] = du2
        sums = jnp.concatenate([
            jnp.sum(du3 * xhat, axis=0, keepdims=True),
            jnp.sum(du3, axis=0, keepdims=True),
            jnp.sum(du2, axis=0, keepdims=True),
            jnp.zeros((5, D_MODEL), F32)], axis=0)

        @pl.when(pl.program_id(0) == 0)
        def _():
            sums_ref[...] = jnp.zeros_like(sums_ref)

        sums_ref[...] += sums

    return pl.pallas_call(
        body, name=name, grid=(SEQ // tm,),
        in_specs=[_row_spec(tm, D_MODEL), _row_spec(tm, D_MODEL, 2), _row_spec(tm, D_MODEL),
                  _vec_spec(1, D_MODEL), _vec_spec(1, D_MODEL)],
        out_specs=[_row_spec(tm, D_MODEL), _row_spec(tm, D_MODEL), _vec_spec(8, D_MODEL)],
        out_shape=[jax.ShapeDtypeStruct((SEQ, D_MODEL), F32), jax.ShapeDtypeStruct((SEQ, D_MODEL), F32),
                   jax.ShapeDtypeStruct((8, D_MODEL), F32)],
        compiler_params=_params("arbitrary"),
    )(du5, proj, u2, ln_g, ln_b)


def _conv_bwd_taps(du2, dz, proj, conv_w, name):
    tm = 256
    hb = tm // HALO
    n_blocks = SEQ // tm

    def body(du2_ref, dnext_ref, dz_ref, vg_ref, halo_ref, w_ref, dproj_ref, dw_ref,
             ubuf, dbuf, ushift, dshift, sgbuf, dwacc):
        i = pl.program_id(0)
        sg = _sigmoid(vg_ref[:, D_MODEL:])
        sgbuf[...] = sg
        u1h = halo_ref[:, :D_MODEL] * _sigmoid(halo_ref[:, D_MODEL:])
        ubuf[pl.ds(0, HALO), :] = jnp.where(i > 0, u1h, 0.0)
        ubuf[pl.ds(HALO, tm), :] = vg_ref[:, :D_MODEL] * sg
        dbuf[pl.ds(0, tm), :] = du2_ref[...]
        dbuf[pl.ds(tm, HALO), :] = jnp.where(i < n_blocks - 1, dnext_ref[...], 0.0)
        _shift_copies(ubuf, ushift)
        _shift_copies(dbuf, dshift)

        @pl.when(i == 0)
        def _():
            dwacc[...] = jnp.zeros_like(dwacc)

        def chunk(ci, carry):
            r0 = pl.multiple_of(ci * CONV_CHUNK, CONV_CHUNK)
            rows = pl.ds(r0, CONV_CHUNK)
            du2c = du2_ref[rows, :]
            du1 = jnp.zeros((CONV_CHUNK, D_MODEL), F32)
            for k in range(CONV_WIDTH):
                du1 = du1 + w_ref[k:k + 1, :] * _shifted_rows(dbuf, dshift, CONV_WIDTH - 1 - k, r0)
                prod = du2c * _shifted_rows(ubuf, ushift, HALO - (CONV_WIDTH - 1) + k, r0)
                dwacc[k] += prod[0:8] + prod[8:16]
            sgc = sgbuf[rows, :]
            dval = du1 * sgc
            dproj_ref[rows, 0:D_MODEL] = dval.astype(BF16)
            dproj_ref[rows, D_MODEL:2 * D_MODEL] = (dval * vg_ref[rows, 0:D_MODEL] * (1.0 - sgc)).astype(BF16)
            return carry

        lax.fori_loop(0, tm // CONV_CHUNK, chunk, 0)
        dproj_ref[:, 2 * D_MODEL:] = dz_ref[...].astype(BF16)

        @pl.when(i == n_blocks - 1)
        def _():
            for k in range(CONV_WIDTH):
                dw_ref[k:k + 1, :] = jnp.sum(dwacc[k], axis=0, keepdims=True)
            dw_ref[CONV_WIDTH:, :] = jnp.zeros((32 - CONV_WIDTH, D_MODEL), F32)

    return pl.pallas_call(
        body, name=name, grid=(n_blocks,),
        in_specs=[_row_spec(tm, D_MODEL),
                  pl.BlockSpec((HALO, D_MODEL), lambda i: (jnp.minimum((i + 1) * hb, SEQ // HALO - 1), 0)),
                  _row_spec(tm, D_MODEL),
                  pl.BlockSpec((tm, 2 * D_MODEL), lambda i: (i, 0)),
                  pl.BlockSpec((HALO, 2 * D_MODEL), lambda i: (jnp.maximum(i * hb - 1, 0), 0)),
                  _vec_spec(CONV_WIDTH, D_MODEL)],
        out_specs=[_row_spec(tm, 3 * D_MODEL), _vec_spec(32, D_MODEL)],
        out_shape=[jax.ShapeDtypeStruct((SEQ, 3 * D_MODEL), BF16), jax.ShapeDtypeStruct((32, D_MODEL), F32)],
        scratch_shapes=[pltpu.VMEM((HALO + tm, D_MODEL), F32), pltpu.VMEM((tm + HALO, D_MODEL), F32),
                        pltpu.VMEM((7, HALO + tm - 8, D_MODEL), F32), pltpu.VMEM((7, HALO + tm - 8, D_MODEL), F32),
                        pltpu.VMEM((tm, D_MODEL), F32), pltpu.VMEM((CONV_WIDTH, 8, D_MODEL), F32)],
        compiler_params=_params("arbitrary"),
    )(du2, du2, dz, proj, proj, conv_w)


def _out_a(u5, w_out, x, gate, g1, scale1, shift1, name):
    tm = 256
    n_d = len(DILATIONS)

    def body(u_ref, w_ref, x_ref, gate_ref, g_ref, sc_ref, sh_ref, x1_ref, y_ref, ht_ref, *rest):
        h_refs, nat = rest[:n_d], rest[-1]
        y = jnp.dot(u_ref[...], w_ref[...], preferred_element_type=F32)
        x1 = x_ref[...] + gate_ref[...] * y
        y_ref[...] = y
        x1_ref[...] = x1
        h = _normmod(x1, g_ref[...], sc_ref[...], sh_ref[...])
        ht_ref[...] = h.T.astype(BF16)
        for h_ref, d in zip(h_refs, DILATIONS):
            _store_classes(h_ref, h, nat, d)

    res = pl.pallas_call(
        body, name=name, grid=(SEQ // tm,),
        in_specs=[_row_spec(tm, D_MODEL), _vec_spec(D_MODEL, D_MODEL), _row_spec(tm, D_MODEL)]
        + [_vec_spec(1, D_MODEL)] * 4,
        out_specs=[_row_spec(tm, D_MODEL), _row_spec(tm, D_MODEL), pl.BlockSpec((D_MODEL, tm), lambda i: (0, i))]
        + [_class_spec(tm, d) for d in DILATIONS],
        out_shape=[jax.ShapeDtypeStruct((SEQ, D_MODEL), F32), jax.ShapeDtypeStruct((SEQ, D_MODEL), F32),
                   jax.ShapeDtypeStruct((D_MODEL, SEQ), BF16)] + [_class_shape(d, BF16) for d in DILATIONS],
        scratch_shapes=[_natural_scratch(tm)],
        compiler_params=_params("parallel"),
    )(u5, w_out, x, gate, g1, scale1, shift1)
    return res[0], res[1], res[2], [a.reshape(SEQ, D_MODEL) for a in res[3:]]


def _out_b_loss(u, w_out, x1, gate, target, name):
    tm = 256

    def body(u_ref, w_ref, x_ref, gate_ref, t_ref, e_ref, dy_ref, sums_ref):
        y = jnp.dot(u_ref[...], w_ref[...], preferred_element_type=F32)
        diff = x_ref[...] + gate_ref[...] * y - t_ref[...]
        e = diff * (1.0 / D_MODEL)
        e_ref[...] = e
        dy_ref[...] = (e * gate_ref[...]).astype(BF16)
        sums = jnp.concatenate([
            jnp.sum(e * y, axis=0, keepdims=True),
            jnp.sum(diff * diff, axis=0, keepdims=True),
            jnp.zeros((6, D_MODEL), F32)], axis=0)

        @pl.when(pl.program_id(0) == 0)
        def _():
            sums_ref[...] = jnp.zeros_like(sums_ref)

        sums_ref[...] += sums

    return pl.pallas_call(
        body, name=name, grid=(SEQ // tm,),
        in_specs=[_row_spec(tm, D_MODEL), _vec_spec(D_MODEL, D_MODEL), _row_spec(tm, D_MODEL),
                  _vec_spec(1, D_MODEL), _row_spec(tm, D_MODEL)],
        out_specs=[_row_spec(tm, D_MODEL), _row_spec(tm, D_MODEL), _vec_spec(8, D_MODEL)],
        out_shape=[jax.ShapeDtypeStruct((SEQ, D_MODEL), F32), jax.ShapeDtypeStruct((SEQ, D_MODEL), BF16),
                   jax.ShapeDtypeStruct((8, D_MODEL), F32)],
        compiler_params=_params("arbitrary"),
    )(u, w_out, x1, gate, target)


def _dgate_dy(dx1, y, gate, name):
    tm = 256

    def body(d_ref, y_ref, gate_ref, dy_ref, sums_ref):
        dv = d_ref[...]
        dy_ref[...] = (dv * gate_ref[...]).astype(BF16)
        sums = jnp.concatenate([jnp.sum(dv * y_ref[...], axis=0, keepdims=True), jnp.zeros((7, D_MODEL), F32)], axis=0)

        @pl.when(pl.program_id(0) == 0)
        def _():
            sums_ref[...] = jnp.zeros_like(sums_ref)

        sums_ref[...] += sums

    return pl.pallas_call(
        body, name=name, grid=(SEQ // tm,),
        in_specs=[_row_spec(tm, D_MODEL), _row_spec(tm, D_MODEL), _vec_spec(1, D_MODEL)],
        out_specs=[_row_spec(tm, D_MODEL), _vec_spec(8, D_MODEL)],
        out_shape=[jax.ShapeDtypeStruct((SEQ, D_MODEL), BF16), jax.ShapeDtypeStruct((8, D_MODEL), F32)],
        compiler_params=_params("arbitrary"),
    )(dx1, y, gate)


def _mm_nt_res(dy, w, name):
    tm = 256
    kc, n = w.shape

    def body(dy_ref, w_ref, o_ref):
        o_ref[...] = lax.dot_general(dy_ref[...], w_ref[...], (((1,), (1,)), ((), ())), preferred_element_type=F32)

    return pl.pallas_call(
        body, name=name, grid=(SEQ // tm,),
        in_specs=[_row_spec(tm, n), _vec_spec(kc, n)],
        out_specs=_row_spec(tm, kc),
        out_shape=jax.ShapeDtypeStruct((SEQ, kc), F32),
        compiler_params=_params("parallel"),
    )(dy, w)


def _seg_matrix():
    r = lax.broadcasted_iota(jnp.int32, (256, 256), 0) // HEAD_DIM
    c = lax.broadcasted_iota(jnp.int32, (256, 256), 1) // HEAD_DIM
    return (r == c).astype(BF16)


def _segsum(v, seg):
    hi = v.astype(BF16)
    lo = (v - hi.astype(F32)).astype(BF16)
    outs = []
    for c0 in range(0, D_MODEL, 256):
        outs.append(jnp.dot(hi[:, c0:c0 + 256], seg, preferred_element_type=F32)
                    + jnp.dot(lo[:, c0:c0 + 256], seg, preferred_element_type=F32))
    return jnp.concatenate(outs, axis=1)


def _qk_rstd(v, seg):
    return lax.rsqrt(_segsum(v * v, seg) * (1.0 / HEAD_DIM) + NORM_EPS)


def _qknorm_fwd(proj, qw, kw, seg, name):
    tm = 256

    def body(p_ref, qw_ref, kw_ref, seg_ref, q_ref, k_ref):
        segv = seg_ref[...]
        q = p_ref[:, :D_MODEL].astype(F32)
        k = p_ref[:, D_MODEL:].astype(F32)
        q_ref[...] = (q * _qk_rstd(q, segv) * qw_ref[...]).astype(BF16)
        k_ref[...] = (k * _qk_rstd(k, segv) * kw_ref[...]).astype(BF16)

    return pl.pallas_call(
        body, name=name, grid=(SEQ // tm,),
        in_specs=[_row_spec(tm, 2 * D_MODEL), _vec_spec(1, D_MODEL), _vec_spec(1, D_MODEL), _vec_spec(256, 256)],
        out_specs=[_row_spec(tm, D_MODEL)] * 2,
        out_shape=[jax.ShapeDtypeStruct((SEQ, D_MODEL), BF16)] * 2,
        compiler_params=_params("parallel"),
    )(proj, qw, kw, seg)


def _attn_masks(b, bpc, dilation, slope):
    if bpc == 1:
        qi = lax.broadcasted_iota(jnp.int32, (ATTN_BLOCK, ATTN_BLOCK), 0)
        kj = lax.broadcasted_iota(jnp.int32, (ATTN_BLOCK, ATTN_BLOCK), 1)
        steps = qi - kj
        return (steps * dilation).astype(F32), steps >= 0
    qi = lax.broadcasted_iota(jnp.int32, (ATTN_BLOCK, 2 * ATTN_BLOCK), 0)
    kj = lax.broadcasted_iota(jnp.int32, (ATTN_BLOCK, 2 * ATTN_BLOCK), 1)
    steps = qi + ATTN_BLOCK - kj
    has_prev = (b % bpc) != 0
    valid = (steps >= 0) & (steps <= ATTN_BLOCK) & (has_prev | (kj >= ATTN_BLOCK))
    return (steps * dilation).astype(F32), valid


def _key_tile(prev_ref, cur_ref, cols, bpc):
    if bpc == 1:
        return cur_ref[:, cols]
    return jnp.concatenate([prev_ref[:, cols], cur_ref[:, cols]], axis=0)


ATTN_HEADS_FWD = 16
ATTN_HEADS_BWD = 16
NT_DIMS = (((1,), (1,)), ((), ()))
TN_DIMS = (((0,), (0,)), ((), ()))
BATCH_NT_DIMS = (((2,), (2,)), ((0,), (0,)))
BATCH_NN_DIMS = (((2,), (1,)), ((0,), (0,)))
BATCH_TN_DIMS = (((1,), (1,)), ((0,), (0,)))


def _head_stack(tile_of, heads):
    return jnp.stack([tile_of(slice(h * HEAD_DIM, (h + 1) * HEAD_DIM)) for h in range(heads)], axis=0)


def _attn_specs(heads, segment=0):
    width = heads * HEAD_DIM
    off = segment * (D_MODEL // width)
    last = SEQ // ATTN_BLOCK - 1
    cur = pl.BlockSpec((ATTN_BLOCK, width), lambda hg, b: (jnp.minimum(b, last), hg + off))
    prev = pl.BlockSpec((ATTN_BLOCK, width), lambda hg, b: (jnp.clip(b - 1, 0, last), hg + off))
    return cur, prev


def _attn_fwd(q, k, proj, slopes, dilation, name):
    bpc = SEQ // dilation // ATTN_BLOCK
    heads = ATTN_HEADS_FWD
    assert heads == N_HEADS
    cur, prev = _attn_specs(heads)
    v_cur, v_prev = _attn_specs(heads, segment=2)
    scale = HEAD_DIM ** -0.5

    def body(sl_ref, q_ref, kp_ref, kc_ref, vp_ref, vc_ref, o_ref, lse_ref):
        dist, valid = _attn_masks(pl.program_id(1), bpc, dilation, None)
        q3 = _head_stack(lambda cols: q_ref[:, cols], heads)
        k3 = _head_stack(lambda cols: _key_tile(kp_ref, kc_ref, cols, bpc), heads)
        v3 = _head_stack(lambda cols: _key_tile(vp_ref, vc_ref, cols, bpc), heads)
        s = lax.dot_general(q3, k3, BATCH_NT_DIMS, preferred_element_type=F32)
        s = jnp.where(valid[None], s * scale - dist[None] * sl_ref[...], NEG_INF)
        m = jnp.max(s, axis=-1, keepdims=True)
        p = jnp.exp(s - m)
        l = jnp.sum(p, axis=-1, keepdims=True)
        o3 = lax.dot_general(p.astype(BF16), v3, BATCH_NN_DIMS, preferred_element_type=F32) / l
        lse3 = m + jnp.log(l)
        for h in range(heads):
            o_ref[:, h * HEAD_DIM:(h + 1) * HEAD_DIM] = o3[h]
        lse_ref[...] = jnp.concatenate([lse3[h] for h in range(heads)], axis=1)

    return pl.pallas_call(
        body, name=name, grid=(N_HEADS // heads, SEQ // ATTN_BLOCK),
        in_specs=[pl.BlockSpec((heads, 1, 1), lambda hg, b: (hg, 0, 0)), cur, prev, cur, v_prev, v_cur],
        out_specs=[cur, pl.BlockSpec((ATTN_BLOCK, N_HEADS), lambda hg, b: (b, 0))],
        out_shape=[jax.ShapeDtypeStruct((SEQ, D_MODEL), F32), jax.ShapeDtypeStruct((SEQ, N_HEADS), F32)],
        compiler_params=_params("parallel", "parallel"),
    )(slopes.reshape(N_HEADS, 1, 1), q, k, k, proj, proj)


def _class_spec(tm, dilation):
    if dilation == 1:
        return _row_spec(tm, D_MODEL)
    return pl.BlockSpec((dilation, tm // dilation, D_MODEL), lambda i: (0, i, 0))


def _class_shape(dilation, dtype):
    if dilation == 1:
        return jax.ShapeDtypeStruct((SEQ, D_MODEL), dtype)
    return jax.ShapeDtypeStruct((dilation, SEQ // dilation, D_MODEL), dtype)


def _load_natural(in_ref, nat_ref, dilation):
    if dilation == 1:
        return in_ref[...].astype(F32)
    n = nat_ref.shape[1] // dilation
    for r in range(dilation):
        for j in range(D_MODEL // LANES):
            nat_ref.at[j][pl.ds(r, n, stride=dilation), :] = in_ref[r, :, j * LANES:(j + 1) * LANES].astype(F32)
    return jnp.concatenate([nat_ref[j] for j in range(D_MODEL // LANES)], axis=1)


def _store_classes(out_ref, value, nat_ref, dilation):
    if dilation == 1:
        out_ref[...] = value.astype(out_ref.dtype)
        return
    n = nat_ref.shape[1] // dilation
    for j in range(D_MODEL // LANES):
        nat_ref[j] = value[:, j * LANES:(j + 1) * LANES]
    for r in range(dilation):
        for j in range(D_MODEL // LANES):
            out_ref[r, :, j * LANES:(j + 1) * LANES] = (
                nat_ref.at[j][pl.ds(r, n, stride=dilation), :].astype(out_ref.dtype))


def _natural_scratch(tm):
    return pltpu.VMEM((D_MODEL // LANES, tm, LANES), F32)


def _head_selector():
    lane_head = lax.broadcasted_iota(jnp.int32, (D_MODEL, N_HEADS), 0) // HEAD_DIM
    head = lax.broadcasted_iota(jnp.int32, (D_MODEL, N_HEADS), 1)
    return (lane_head == head).astype(BF16)


def _dot_split(v, m01, dims):
    hi = v.astype(BF16)
    lo = (v - hi.astype(F32)).astype(BF16)
    return (lax.dot_general(hi, m01, dims, preferred_element_type=F32)
            + lax.dot_general(lo, m01, dims, preferred_element_type=F32))


def _merge_fwd(o_parts, lse_parts, z, sel, name):
    tm = 256
    h_spec = pl.BlockSpec((tm, N_HEADS), lambda i: (i, 0))

    def body(o0, o1, o2, l0, l1, l2, z_ref, sel_ref, u_ref, ut_ref, o_ref, lse_ref, nat):
        ls = [l0[...], l1[...], l2[...]]
        m = jnp.maximum(jnp.maximum(ls[0], ls[1]), ls[2])
        tot = m + jnp.log(jnp.exp(ls[0] - m) + jnp.exp(ls[1] - m) + jnp.exp(ls[2] - m))
        o = jnp.zeros((tm, D_MODEL), F32)
        for o_in, l, d in zip((o0, o1, o2), ls, DILATIONS):
            weight = _dot_split(jnp.exp(l - tot), sel_ref[...], NT_DIMS)
            o = o + weight * _load_natural(o_in, nat, d)
        zv = z_ref[...]
        u = o * (zv * _sigmoid(zv))
        u_ref[...] = u.astype(BF16)
        ut_ref[...] = u.T.astype(BF16)
        o_ref[...] = o
        lse_ref[...] = tot

    return pl.pallas_call(
        body, name=name, grid=(SEQ // tm,),
        in_specs=[_class_spec(tm, d) for d in DILATIONS] + [h_spec] * 3
        + [_row_spec(tm, D_MODEL), _vec_spec(D_MODEL, N_HEADS)],
        out_specs=[_row_spec(tm, D_MODEL), pl.BlockSpec((D_MODEL, tm), lambda i: (0, i)),
                   _row_spec(tm, D_MODEL), h_spec],
        out_shape=[jax.ShapeDtypeStruct((SEQ, D_MODEL), BF16), jax.ShapeDtypeStruct((D_MODEL, SEQ), BF16),
                   jax.ShapeDtypeStruct((SEQ, D_MODEL), F32), jax.ShapeDtypeStruct((SEQ, N_HEADS), F32)],
        scratch_shapes=[_natural_scratch(tm)],
        compiler_params=_params("parallel"),
    )(*o_parts, *lse_parts, z, sel)


def _merge_bwd(du, o, z, sel, name):
    tm = 256
    n_d = len(DILATIONS)

    def body(du_ref, o_ref, z_ref, sel_ref, dz_ref, delta_ref, *rest):
        do_refs, nat = rest[:n_d], rest[-1]
        zv = z_ref[...]
        sz = _sigmoid(zv)
        duv = du_ref[...]
        ov = o_ref[...]
        do = duv * (zv * sz)
        dz_ref[...] = (duv * ov * (sz * (1.0 + zv * (1.0 - sz)))).astype(BF16)
        delta_ref[...] = _dot_split(do * ov, sel_ref[...], (((1,), (0,)), ((), ())))
        for do_ref, d in zip(do_refs, DILATIONS):
            _store_classes(do_ref, do, nat, d)

    res = pl.pallas_call(
        body, name=name, grid=(SEQ // tm,),
        in_specs=[_row_spec(tm, D_MODEL)] * 3 + [_vec_spec(D_MODEL, N_HEADS)],
        out_specs=[_row_spec(tm, D_MODEL), pl.BlockSpec((tm, N_HEADS), lambda i: (i, 0))]
        + [_class_spec(tm, d) for d in DILATIONS],
        out_shape=[jax.ShapeDtypeStruct((SEQ, D_MODEL), BF16), jax.ShapeDtypeStruct((SEQ, N_HEADS), F32)]
        + [_class_shape(d, BF16) for d in DILATIONS],
        scratch_shapes=[_natural_scratch(tm)],
        compiler_params=_params("parallel"),
    )(du, o, z, sel)
    return res[0], res[1], [a.reshape(SEQ, D_MODEL) for a in res[2:]]


def _attn_bwd(q, k, proj, do, lse, delta, slopes, dilation, name):
    bpc = SEQ // dilation // ATTN_BLOCK
    heads = ATTN_HEADS_BWD
    n_blocks = SEQ // ATTN_BLOCK
    carry = bpc > 1
    width = heads * HEAD_DIM
    cur, prev = _attn_specs(heads)
    v_cur, v_prev = _attn_specs(heads, segment=2)
    assert heads == N_HEADS
    per_head = pl.BlockSpec((ATTN_BLOCK, N_HEADS), lambda hg, b: (jnp.minimum(b, n_blocks - 1), 0))
    scale = HEAD_DIM ** -0.5

    def body(sl_ref, q_ref, kp_ref, kc_ref, vp_ref, vc_ref, do_ref, lse_ref, dl_ref,
             dq_ref, dk_ref, dv_ref, *scratch):
        b = pl.program_id(1)
        if carry:
            dk_carry, dv_carry = scratch

            @pl.when(b == n_blocks)
            def _():
                dk_ref[...] = dk_carry[...].astype(BF16)
                dv_ref[...] = dv_carry[...].astype(BF16)

            @pl.when(b < n_blocks)
            def _():
                step(sl_ref, q_ref, kp_ref, kc_ref, vp_ref, vc_ref, do_ref, lse_ref, dl_ref,
                     dq_ref, dk_ref, dv_ref, dk_carry, dv_carry, b)
        else:
            step(sl_ref, q_ref, kp_ref, kc_ref, vp_ref, vc_ref, do_ref, lse_ref, dl_ref,
                 dq_ref, dk_ref, dv_ref, None, None, b)

    def step(sl_ref, q_ref, kp_ref, kc_ref, vp_ref, vc_ref, do_ref, lse_ref, dl_ref,
             dq_ref, dk_ref, dv_ref, dk_carry, dv_carry, b):
        if carry:
            @pl.when(b == 0)
            def _():
                dk_carry[...] = jnp.zeros_like(dk_carry)
                dv_carry[...] = jnp.zeros_like(dv_carry)

        dist, valid = _attn_masks(b, bpc, dilation, None)
        q3 = _head_stack(lambda cols: q_ref[:, cols], heads)
        k3 = _head_stack(lambda cols: _key_tile(kp_ref, kc_ref, cols, bpc), heads)
        v3 = _head_stack(lambda cols: _key_tile(vp_ref, vc_ref, cols, bpc), heads)
        do3 = _head_stack(lambda cols: do_ref[:, cols], heads)
        lse3 = jnp.stack([lse_ref[:, h:h + 1] for h in range(heads)], axis=0)
        dl3 = jnp.stack([dl_ref[:, h:h + 1] for h in range(heads)], axis=0)
        s = lax.dot_general(q3, k3, BATCH_NT_DIMS, preferred_element_type=F32)
        p = jnp.exp(jnp.where(valid[None], s * scale - dist[None] * sl_ref[...], NEG_INF) - lse3)
        dp = lax.dot_general(do3, v3, BATCH_NT_DIMS, preferred_element_type=F32)
        ds = (p * (dp - dl3) * scale).astype(BF16)
        dq3 = lax.dot_general(ds, k3, BATCH_NN_DIMS, preferred_element_type=F32)
        dk3 = lax.dot_general(ds, q3, BATCH_TN_DIMS, preferred_element_type=F32)
        dv3 = lax.dot_general(p.astype(BF16), do3, BATCH_TN_DIMS, preferred_element_type=F32)
        for h in range(heads):
            cols = slice(h * HEAD_DIM, (h + 1) * HEAD_DIM)
            dq_ref[:, cols] = dq3[h].astype(BF16)
            if carry:
                dk_ref[:, cols] = (dk_carry[:, cols] + dk3[h, :ATTN_BLOCK]).astype(BF16)
                dv_ref[:, cols] = (dv_carry[:, cols] + dv3[h, :ATTN_BLOCK]).astype(BF16)
                dk_carry[:, cols] = dk3[h, ATTN_BLOCK:]
                dv_carry[:, cols] = dv3[h, ATTN_BLOCK:]
            else:
                dk_ref[:, cols] = dk3[h].astype(BF16)
                dv_ref[:, cols] = dv3[h].astype(BF16)

    kv_out = prev if carry else cur
    return pl.pallas_call(
        body, name=name, grid=(N_HEADS // heads, n_blocks + (1 if carry else 0)),
        in_specs=[pl.BlockSpec((heads, 1, 1), lambda hg, b: (hg, 0, 0)), cur, prev, cur, v_prev, v_cur,
                  cur, per_head, per_head],
        out_specs=[cur, kv_out, kv_out],
        out_shape=[jax.ShapeDtypeStruct((SEQ, D_MODEL), BF16)] * 3,
        scratch_shapes=[pltpu.VMEM((ATTN_BLOCK, width), F32)] * 2 if carry else [],
        compiler_params=_params("parallel", "arbitrary"),
    )(slopes.reshape(N_HEADS, 1, 1), q, k, k, proj, proj, do, lse, delta)


def _qknorm_bwd(proj, qw, kw, seg, dq, dk, dv, name):
    tm = 256

    def body(p_ref, qw_ref, kw_ref, seg_ref, dq_ref, dk_ref, dv_ref, dproj_ref, sums_ref):
        segv = seg_ref[...]
        sums = []
        for part, (w_ref, dn_ref) in enumerate(((qw_ref, dq_ref), (kw_ref, dk_ref))):
            raw = p_ref[:, part * D_MODEL:(part + 1) * D_MODEL].astype(F32)
            dn = dn_ref[...].astype(F32)
            r = _qk_rstd(raw, segv)
            gq = dn * w_ref[...]
            draw = r * gq - raw * (r * r * r) * (_segsum(raw * gq, segv) * (1.0 / HEAD_DIM))
            dproj_ref[:, part * D_MODEL:(part + 1) * D_MODEL] = draw.astype(BF16)
            sums.append(jnp.sum(dn * raw * r, axis=0, keepdims=True))
        dproj_ref[:, 2 * D_MODEL:] = dv_ref[...]

        @pl.when(pl.program_id(0) == 0)
        def _():
            sums_ref[...] = jnp.zeros_like(sums_ref)

        sums_ref[...] += jnp.concatenate(sums + [jnp.zeros((6, D_MODEL), F32)], axis=0)

    return pl.pallas_call(
        body, name=name, grid=(SEQ // tm,),
        in_specs=[_row_spec(tm, 3 * D_MODEL), _vec_spec(1, D_MODEL), _vec_spec(1, D_MODEL), _vec_spec(256, 256)]
        + [_row_spec(tm, D_MODEL)] * 3,
        out_specs=[_row_spec(tm, 3 * D_MODEL), _vec_spec(8, D_MODEL)],
        out_shape=[jax.ShapeDtypeStruct((SEQ, 3 * D_MODEL), BF16), jax.ShapeDtypeStruct((8, D_MODEL), F32)],
        compiler_params=_params("arbitrary"),
    )(proj, qw, kw, seg, dq, dk, dv)


def _to_classes(a, dilation):
    if dilation == 1:
        return a
    s, c = a.shape
    return a.reshape(s // dilation, dilation, c).transpose(1, 0, 2).reshape(s, c)


def _from_classes(a, dilation):
    if dilation == 1:
        return a
    s, c = a.shape
    return a.reshape(dilation, s // dilation, c).transpose(1, 0, 2).reshape(s, c)


def _cols_to_classes(a, dilation):
    if dilation == 1:
        return a
    r, s = a.shape
    return a.reshape(r, s // dilation, dilation).transpose(0, 2, 1).reshape(r, s)


B_TN = 512
B_GROUP_TILES = 3 * D_MODEL // B_TN
B_Z_TILE0 = 3 * B_GROUP_TILES
B_Z_TILES = D_MODEL // B_TN


def _local_step(x, target, mods, norm_g, conv_w, conv_b, ln_g, ln_b, q_norm, k_norm,
                weights_a, weights_b, forward_weights_b, send_grads_b, forward_grads_b, send_grads_a):
    row = lambda a, i: a[i:i + 1]
    shift0, scale0, gate0 = row(mods[0], 0), row(mods[0], 1), row(mods[0], 2)
    shift1, scale1, gate1 = row(mods[1], 0), row(mods[1], 1), row(mods[1], 2)
    g0, g1 = row(norm_g, 0), row(norm_g, 1)
    seg = _seg_matrix()
    slopes = jnp.exp2(-8.0 * jnp.arange(1, N_HEADS + 1, dtype=F32) / N_HEADS)
    qw = [jnp.tile(q_norm[g:g + 1], (1, N_HEADS)) for g in range(3)]
    kw = [jnp.tile(k_norm[g:g + 1], (1, N_HEADS)) for g in range(3)]

    h0, h0t = _normmod_fwd(x, g0, scale0, shift0, "prenorm0")
    wa_in, wa_out = weights_a(h0)
    ja, _, nsa = wa_in.shape
    proj_a = _mm(h0, wa_in, tn=nsa, tile0=0, n_tiles=ja, out_dtype=F32, name="a_in")
    u5, u5t, u2 = _conv_fwd(proj_a, conv_w, conv_b, ln_g, ln_b, "a_conv")
    token = forward_weights_b(u5)
    x1, y_a, h1t, h1c = _out_a(u5, wa_out, x, gate0 + token[0:1, 0:1], g1, scale1, shift1, "a_out")

    wb_in, wb_out = weights_b(x1)
    jb, _, nsb = wb_in.shape
    h1 = h1c[0]
    h1tc = [_cols_to_classes(h1t, d) for d in DILATIONS]
    z_b = _mm(h1, wb_in, tn=B_TN, tile0=B_Z_TILE0, n_tiles=B_Z_TILES, out_dtype=F32, name="b_in_z")
    proj_g, qkv, o_parts, lse_parts = [], [], [], []
    for g, d in enumerate(DILATIONS):
        pg = _mm(h1c[g], wb_in, tn=B_TN, tile0=g * B_GROUP_TILES, n_tiles=B_GROUP_TILES, out_dtype=BF16,
                 name=f"b_in_g{g}")
        qn, kn = _qknorm_fwd(pg, qw[g], kw[g], seg, f"b_qknorm_g{g}")
        og, lg = _attn_fwd(qn, kn, pg, slopes, d, f"b_attn_g{g}")
        proj_g.append(pg)
        qkv.append((qn, kn))
        o_parts.append(og if d == 1 else og.reshape(d, SEQ // d, D_MODEL))
        lse_parts.append(_from_classes(lg, d))
    sel = _head_selector()
    u_b, u_bt, o_b, lse_b = _merge_fwd(o_parts, lse_parts, z_b, sel, "b_merge")
    e, dy_b, sums_loss = _out_b_loss(u_b, wb_out, x1, gate1, target, "b_out_loss")

    dwb_out = _mm(u_bt, dy_b, tn=D_MODEL, tile0=0, n_tiles=1, out_dtype=BF16, name="b_dwout")
    du_b = _mm_nt_res(dy_b, wb_out, "b_dout")
    dz_b, delta_b, do_c = _merge_bwd(du_b, o_b, z_b, sel, "b_merge_bwd")
    dwb_in = _mm(h1t, dz_b, tn=B_TN, tile0=B_Z_TILE0, n_tiles=B_Z_TILES, out_dtype=BF16, name="b_dwin_z",
                 out3d=(jb, nsb))
    dh1_parts = [_mm_nt(dz_b, wb_in, tn=B_TN, tile0=B_Z_TILE0, n_tiles=B_Z_TILES, name="b_dh_z")]
    qk_sums = []
    for g, d in enumerate(DILATIONS):
        qn, kn = qkv[g]
        dq, dk, dv = _attn_bwd(qn, kn, proj_g[g], do_c[g], _to_classes(lse_b, d), _to_classes(delta_b, d),
                               slopes, d, f"b_attn_bwd_g{g}")
        dproj, sums_qk = _qknorm_bwd(proj_g[g], qw[g], kw[g], seg, dq, dk, dv, f"b_qknorm_bwd_g{g}")
        qk_sums.append(sums_qk)
        dwb_in = _mm(h1tc[g], dproj, tn=B_TN, tile0=g * B_GROUP_TILES, n_tiles=B_GROUP_TILES, out_dtype=BF16,
                     name=f"b_dwin_g{g}", out3d=(jb, nsb), prev=dwb_in)
        dh = _mm_nt(dproj, wb_in, tn=B_TN, tile0=g * B_GROUP_TILES, n_tiles=B_GROUP_TILES, name=f"b_dh_g{g}")
        dh1_parts.append(dh)
    token = send_grads_b(dwb_in, dwb_out)
    dx1, sums_n1 = _normmod_bwd(x1, g1, scale1 + token[0:1, 0:1], dh1_parts, e, "prenorm1_bwd",
                                part_dilations=(1,) + DILATIONS)
    token = forward_grads_b(dx1)

    dy_a, sums_ga = _dgate_dy(dx1, y_a, gate0 + token[0:1, 0:1], "a_dgate")
    dwa_out = _mm(u5t, dy_a, tn=D_MODEL, tile0=0, n_tiles=1, out_dtype=BF16, name="a_dwout")
    du5 = _mm_nt_res(dy_a, wa_out, "a_dout")
    du2, dz_a, sums_ln = _conv_bwd_pointwise(du5, proj_a, u2, ln_g, ln_b, "a_conv_bwd_pw")
    dproj_a, dconv_w = _conv_bwd_taps(du2, dz_a, proj_a, conv_w, "a_conv_bwd_taps")
    dwa_in = _mm(h0t, dproj_a, tn=nsa, tile0=0, n_tiles=ja, out_dtype=BF16, name="a_dwin", out3d=(ja, nsa))
    token = send_grads_a(dwa_in, dwa_out)
    dh0 = _mm_nt(dproj_a, wa_in, tn=nsa, tile0=0, n_tiles=ja, name="a_dh", after=token)
    grad_x, sums_n0 = _normmod_bwd(x, g0, scale0, [dh0], dx1, "prenorm0_bwd")

    small = dict(
        dnorm_g=jnp.concatenate([sums_n0[0:1], sums_n1[0:1]], axis=0),
        dmod0=jnp.concatenate([sums_n0[2:3], sums_n0[1:2], sums_ga[0:1]], axis=0),
        dmod1=jnp.concatenate([sums_n1[2:3], sums_n1[1:2], sums_loss[0:1]], axis=0),
        dln_g=sums_ln[0:1], dln_b=sums_ln[1:2], dconv_b=sums_ln[2:3],
        dconv_w=dconv_w[:CONV_WIDTH],
        dq_norm=jnp.concatenate([s[0:1] for s in qk_sums], axis=0),
        dk_norm=jnp.concatenate([s[1:2] for s in qk_sums], axis=0),
        loss_cols=sums_loss[1:2],
    )
    return grad_x, small


def _adamw(w, g, m, v, name):
    rows, cols = w.shape
    tr = rows if rows <= 128 else 128
    c1 = 1.0 / (1.0 - ADAM_B1 ** ADAM_STEP)
    c2 = 1.0 / (1.0 - ADAM_B2 ** ADAM_STEP)

    def body(w_ref, g_ref, m_ref, v_ref, d_ref, mo_ref, vo_ref):
        gv = g_ref[...]
        mn = ADAM_B1 * m_ref[...] + (1.0 - ADAM_B1) * gv
        vn = ADAM_B2 * v_ref[...] + (1.0 - ADAM_B2) * (gv * gv)
        mo_ref[...] = mn
        vo_ref[...] = vn
        d_ref[...] = -ADAM_LR * ((mn * c1) / (jnp.sqrt(vn * c2) + ADAM_EPS) + ADAM_WD * w_ref[...])

    spec = pl.BlockSpec((tr, cols), lambda i: (i, 0))
    return pl.pallas_call(
        body, name=name, grid=(rows // tr,), in_specs=[spec] * 4, out_specs=[spec] * 3,
        out_shape=[jax.ShapeDtypeStruct((rows, cols), F32)] * 3,
        compiler_params=_params("parallel"),
    )(w, g, m, v)


def _cast_into_slot(w, chip_idx, name):
    rows, cols = w.shape
    tr = 256

    def body(ch_ref, w_ref, o_ref):
        o_ref[...] = w_ref[...].astype(BF16)

    return pl.pallas_call(
        body, name=name,
        grid_spec=pltpu.PrefetchScalarGridSpec(
            num_scalar_prefetch=1, grid=(rows // tr,),
            in_specs=[pl.BlockSpec((tr, cols), lambda i, ch: (i, 0))],
            out_specs=pl.BlockSpec((None, tr, cols), lambda i, ch: (ch[0], i, 0))),
        out_shape=jax.ShapeDtypeStruct((N_CHIPS, rows, cols), BF16), compiler_params=_params("parallel"),
    )(chip_idx, w)


def _position():
    x, y, c = lax.axis_index("x"), lax.axis_index("y"), lax.axis_index("c")
    return x, y, c


def _xor_peer(x, y, c, k):
    return (x ^ ((k >> 2) & 1), y ^ ((k >> 1) & 1), c ^ (k & 1))


def _chip_peer(x, y, k):
    return (x ^ ((k >> 1) & 1), y ^ (k & 1))


def _ada_forward(c_row, ada_w, ada_b, conv_w):
    ns = ada_w.shape[2]
    cw = conv_w.shape[1]

    def body(c_ref, w_ref, b_ref, cv_ref, mod_ref, sc_ref, cvo_ref,
             c_all, mp, parts, cv_parts, send1, recv1, send2, recv2, send3, recv3):
        x, y, c = _position()
        me = 4 * x + 2 * y + c
        chip = 2 * x + y

        def c_copy(k):
            return pltpu.make_async_remote_copy(
                src_ref=c_all.at[me], dst_ref=c_all.at[me], send_sem=send1.at[k - 1], recv_sem=recv1.at[k - 1],
                device_id=_xor_peer(x, y, c, k), device_id_type=MESH)

        def cv_copy(k):
            px, py = _chip_peer(x, y, k)
            return pltpu.make_async_remote_copy(
                src_ref=cv_parts.at[chip], dst_ref=cv_parts.at[chip], send_sem=send3.at[k - 1],
                recv_sem=recv3.at[k - 1], device_id=(px, py, c), device_id_type=MESH)

        c_all[me] = c_ref[...]
        cv_parts[chip] = cv_ref[...]
        for k in range(1, N_DEV):
            c_copy(k).start()
        for k in range(1, N_CHIPS):
            cv_copy(k).start()
        for k in range(1, N_DEV):
            c_copy(k).wait_recv()
        cv = jnp.concatenate([c_all[i] for i in range(N_DEV)], axis=0)
        sc = cv * _sigmoid(cv)
        sc_ref[...] = sc
        for l in range(2):
            res = jnp.dot(sc, w_ref[l], preferred_element_type=F32, precision=lax.Precision.HIGHEST)
            for i in range(N_DEV):
                mp[i, l:l + 1, :] = res[i:i + 1, :]

        def mod_copy(k):
            px, py = _chip_peer(x, y, k)
            return pltpu.make_async_remote_copy(
                src_ref=mp.at[4 * px + 2 * py + c], dst_ref=parts.at[chip], send_sem=send2.at[k - 1],
                recv_sem=recv2.at[k - 1], device_id=(px, py, c), device_id_type=MESH)

        for k in range(1, N_CHIPS):
            mod_copy(k).start()
        parts[chip] = mp[me]
        for k in range(1, N_CHIPS):
            mod_copy(k).wait_recv()
            cv_copy(k).wait_recv()
        mod_ref[...] = jnp.concatenate([parts[j] for j in range(N_CHIPS)], axis=1) + b_ref[...]
        cvo_ref[...] = jnp.concatenate([cv_parts[j] for j in range(N_CHIPS)], axis=1)
        for k in range(1, N_DEV):
            c_copy(k).wait_send()
        for k in range(1, N_CHIPS):
            mod_copy(k).wait_send()
            cv_copy(k).wait_send()

    vm = pl.BlockSpec(memory_space=pltpu.VMEM)
    return pl.pallas_call(
        body, name="ada_forward",
        in_specs=[vm] * 4, out_specs=[vm] * 3,
        out_shape=[jax.ShapeDtypeStruct((2, 3 * D_MODEL), F32), jax.ShapeDtypeStruct((N_DEV, D_MODEL), F32),
                   jax.ShapeDtypeStruct((CONV_WIDTH, N_CHIPS * cw), F32)],
        scratch_shapes=[pltpu.VMEM((N_DEV, 1, D_MODEL), F32), pltpu.VMEM((N_DEV, 2, ns), F32),
                        pltpu.VMEM((N_CHIPS, 2, ns), F32), pltpu.VMEM((N_CHIPS, CONV_WIDTH, cw), F32),
                        pltpu.SemaphoreType.DMA((N_DEV - 1,)), pltpu.SemaphoreType.DMA((N_DEV - 1,)),
                        pltpu.SemaphoreType.DMA((N_CHIPS - 1,)), pltpu.SemaphoreType.DMA((N_CHIPS - 1,)),
                        pltpu.SemaphoreType.DMA((N_CHIPS - 1,)), pltpu.SemaphoreType.DMA((N_CHIPS - 1,))],
        compiler_params=pltpu.CompilerParams(vmem_limit_bytes=VMEM_LIMIT_BYTES),
    )(c_row, ada_w, ada_b, conv_w)


HBM_SPEC = pl.BlockSpec(memory_space=pltpu.HBM)
ANY_SPEC = pl.BlockSpec(memory_space=pl.ANY)
SEM_SPEC = pl.BlockSpec(memory_space=pltpu.SEMAPHORE)
SPLIT_PARAMS = dict(compiler_params=pltpu.CompilerParams(has_side_effects=pltpu.SideEffectType.DATAFLOW_SIDE_EFFECTING))
TOKEN = jax.ShapeDtypeStruct((8, 128), F32)


def _hbm(arrays):
    return [pltpu.with_memory_space_constraint(a, pltpu.HBM) for a in arrays]


def _hbm_like(arrays):
    return [pltpu.HBM(a.shape, a.dtype) for a in arrays]


def _gather_start(lands, after, name):
    n = len(lands)

    def body(*refs):
        ins = refs[:n]
        send, recv = refs[n + 1], refs[n + 2]
        x, y, c = _position()
        chip = 2 * x + y
        for t in range(n):
            rh = ins[t].shape[1] // 2
            for k in range(1, N_CHIPS):
                px, py = _chip_peer(x, y, k)
                block = ins[t].at[chip, pl.ds(c * rh, rh)]
                pltpu.make_async_remote_copy(
                    src_ref=block, dst_ref=block, send_sem=send.at[3 * t + k - 1], recv_sem=recv.at[3 * t + k - 1],
                    device_id=(px, py, c), device_id_type=MESH).start()
        refs[-1][...] = jnp.zeros(TOKEN.shape, F32)

    res = pl.pallas_call(
        body, name=name, in_specs=[HBM_SPEC] * n + [ANY_SPEC],
        out_specs=(SEM_SPEC, SEM_SPEC, *[HBM_SPEC] * n, pl.BlockSpec(memory_space=pltpu.VMEM)),
        out_shape=(pltpu.SemaphoreType.DMA((3 * n,)), pltpu.SemaphoreType.DMA((3 * n,)), *_hbm_like(lands), TOKEN),
        input_output_aliases={t: 2 + t for t in range(n)}, **SPLIT_PARAMS,
    )(*_hbm(lands), after)
    return res[0], res[1], list(res[2:2 + n]), res[-1]


def _gather_forward(send, recv, lands, after, name):
    n = len(lands)

    def body(*refs):
        ins = refs[:n]
        send1, recv1 = refs[n], refs[n + 1]
        send2, recv2 = refs[n + 3], refs[n + 4]
        x, y, c = _position()
        chip = 2 * x + y
        for t in range(n):
            rh = ins[t].shape[1] // 2
            half = pl.ds(c * rh, rh)
            for k in range(1, N_CHIPS):
                px, py = _chip_peer(x, y, k)
                s = 3 * t + k - 1
                got = ins[t].at[2 * px + py, half]
                cp = pltpu.make_async_remote_copy(
                    src_ref=ins[t].at[chip, half], dst_ref=got, send_sem=send1.at[s], recv_sem=recv1.at[s],
                    device_id=(px, py, c), device_id_type=MESH)
                cp.wait_send()
                cp.wait_recv()
                pltpu.make_async_remote_copy(
                    src_ref=got, dst_ref=got, send_sem=send2.at[s], recv_sem=recv2.at[s],
                    device_id=(x, y, 1 - c), device_id_type=MESH).start()
        refs[-1][...] = jnp.zeros(TOKEN.shape, F32)

    res = pl.pallas_call(
        body, name=name, in_specs=[HBM_SPEC] * n + [SEM_SPEC, SEM_SPEC, ANY_SPEC],
        out_specs=(SEM_SPEC, SEM_SPEC, *[HBM_SPEC] * n, pl.BlockSpec(memory_space=pltpu.VMEM)),
        out_shape=(pltpu.SemaphoreType.DMA((3 * n,)), pltpu.SemaphoreType.DMA((3 * n,)), *_hbm_like(lands), TOKEN),
        input_output_aliases={t: 2 + t for t in range(n)}, **SPLIT_PARAMS,
    )(*lands, send, recv, after)
    return res[0], res[1], list(res[2:2 + n]), res[-1]


def _gather_wait(send, recv, lands, after, name):
    n = len(lands)

    def body(*refs):
        ins = refs[:n]
        send_ref, recv_ref = refs[n], refs[n + 1]
        x, y, c = _position()
        for t in range(n):
            rh = ins[t].shape[1] // 2
            for k in range(1, N_CHIPS):
                px, py = _chip_peer(x, y, k)
                cp = pltpu.make_async_remote_copy(
                    src_ref=ins[t].at[2 * px + py, pl.ds(c * rh, rh)],
                    dst_ref=ins[t].at[2 * px + py, pl.ds((1 - c) * rh, rh)], send_sem=send_ref.at[3 * t + k - 1],
                    recv_sem=recv_ref.at[3 * t + k - 1], device_id=(x, y, 1 - c), device_id_type=MESH)
                cp.wait_send()
                cp.wait_recv()

    res = pl.pallas_call(
        body, name=name, in_specs=[HBM_SPEC] * n + [SEM_SPEC, SEM_SPEC, ANY_SPEC], out_specs=[HBM_SPEC] * n,
        out_shape=_hbm_like(lands), input_output_aliases={t: t for t in range(n)}, **SPLIT_PARAMS,
    )(*lands, send, recv, after)
    return list(res)


def _reduce_start(grads, after, name):
    n = len(grads)
    lands = [lax.empty((N_DEV, g.shape[1] // 2, g.shape[2]), BF16) for g in grads]

    def body(*refs):
        gs, ls = refs[:n], refs[n:2 * n]
        send, recv = refs[2 * n + 1], refs[2 * n + 2]
        x, y, c = _position()
        me = 4 * x + 2 * y + c
        for t in range(n):
            rh = gs[t].shape[1] // 2
            for k in range(1, N_DEV):
                px, py, pc = _xor_peer(x, y, c, k)
                pltpu.make_async_remote_copy(
                    src_ref=gs[t].at[2 * px + py, pl.ds(pc * rh, rh)], dst_ref=ls[t].at[me],
                    send_sem=send.at[7 * t + k - 1], recv_sem=recv.at[7 * t + k - 1],
                    device_id=(px, py, pc), device_id_type=MESH).start()
        refs[-1][...] = jnp.zeros(TOKEN.shape, F32)

    res = pl.pallas_call(
        body, name=name, in_specs=[HBM_SPEC] * (2 * n) + [ANY_SPEC],
        out_specs=(SEM_SPEC, SEM_SPEC, *[HBM_SPEC] * (2 * n), pl.BlockSpec(memory_space=pltpu.VMEM)),
        out_shape=(pltpu.SemaphoreType.DMA((7 * n,)), pltpu.SemaphoreType.DMA((7 * n,)),
                   *_hbm_like(grads), *_hbm_like(lands), TOKEN),
        input_output_aliases={t: 2 + t for t in range(2 * n)}, **SPLIT_PARAMS,
    )(*_hbm(grads), *_hbm(lands), after)
    return res[0], res[1], list(res[2:2 + n]), list(res[2 + n:2 + 2 * n]), res[-1]


def _reduce_wait(send, recv, grads, lands, after, name):
    n = len(grads)

    def body(*refs):
        gs, ls = refs[:n], refs[n:2 * n]
        send_ref, recv_ref = refs[2 * n], refs[2 * n + 1]
        x, y, c = _position()
        for t in range(n):
            rh = gs[t].shape[1] // 2
            for k in range(1, N_DEV):
                px, py, pc = _xor_peer(x, y, c, k)
                cp = pltpu.make_async_remote_copy(
                    src_ref=gs[t].at[2 * px + py, pl.ds(pc * rh, rh)], dst_ref=ls[t].at[4 * px + 2 * py + pc],
                    send_sem=send_ref.at[7 * t + k - 1], recv_sem=recv_ref.at[7 * t + k - 1],
                    device_id=(px, py, pc), device_id_type=MESH)
                cp.wait_send()
                cp.wait_recv()

    res = pl.pallas_call(
        body, name=name, in_specs=[HBM_SPEC] * (2 * n) + [SEM_SPEC, SEM_SPEC, ANY_SPEC], out_specs=[HBM_SPEC] * (2 * n),
        out_shape=_hbm_like(grads) + _hbm_like(lands), input_output_aliases={t: t for t in range(2 * n)}, **SPLIT_PARAMS,
    )(*grads, *lands, send, recv, after)
    return list(res[:n]), list(res[n:])


def _sum_devices(land, grad, dev_idx, name):
    _, rh, cols = land.shape
    tr = 128
    nb = rh // tr

    def body(idx_ref, l_ref, g_ref, o_ref):
        me = idx_ref[0]
        acc = jnp.where(me == 0, g_ref[...], l_ref[0]).astype(F32)
        for d in range(1, N_DEV):
            acc = acc + jnp.where(me == d, g_ref[...], l_ref[d]).astype(F32)
        o_ref[...] = acc

    return pl.pallas_call(
        body, name=name,
        grid_spec=pltpu.PrefetchScalarGridSpec(
            num_scalar_prefetch=1, grid=(nb,),
            in_specs=[pl.BlockSpec((N_DEV, tr, cols), lambda i, idx: (0, i, 0)),
                      pl.BlockSpec((None, tr, cols), lambda i, idx: (idx[1], idx[2] * nb + i, 0))],
            out_specs=pl.BlockSpec((tr, cols), lambda i, idx: (idx[2] * nb + i, 0))),
        out_shape=jax.ShapeDtypeStruct((2 * rh, cols), F32), compiler_params=_params("parallel"),
    )(dev_idx, land, grad)


def _split_start(name, arrays, n_sems, after, issue):
    m = len(arrays)

    def body(*refs):
        issue(refs[:m], refs[m + 1], refs[m + 2])
        refs[-1][...] = jnp.zeros(TOKEN.shape, F32)

    res = pl.pallas_call(
        body, name=name, in_specs=[HBM_SPEC] * m + [ANY_SPEC],
        out_specs=(SEM_SPEC, SEM_SPEC, *[HBM_SPEC] * m, pl.BlockSpec(memory_space=pltpu.VMEM)),
        out_shape=(pltpu.SemaphoreType.DMA((n_sems,)), pltpu.SemaphoreType.DMA((n_sems,)), *_hbm_like(arrays), TOKEN),
        input_output_aliases={t: 2 + t for t in range(m)}, **SPLIT_PARAMS,
    )(*_hbm(arrays), after)
    return res[0], res[1], list(res[2:2 + m]), res[-1]


def _split_wait(name, arrays, send, recv, after, await_all):
    m = len(arrays)

    def body(*refs):
        await_all(refs[:m], refs[m], refs[m + 1])

    res = pl.pallas_call(
        body, name=name, in_specs=[HBM_SPEC] * m + [SEM_SPEC, SEM_SPEC, ANY_SPEC], out_specs=[HBM_SPEC] * m,
        out_shape=_hbm_like(arrays), input_output_aliases={t: t for t in range(m)}, **SPLIT_PARAMS,
    )(*arrays, send, recv, after)
    return list(res)


def _sibling_copies(refs, send, recv, n):
    x, y, c = _position()
    cps = []
    for t in range(n):
        rh = refs[t].shape[1] // 2
        cps.append(pltpu.make_async_remote_copy(
            src_ref=refs[t].at[pl.ds(0, N_CHIPS), pl.ds((1 - c) * rh, rh)], dst_ref=refs[n + t],
            send_sem=send.at[t], recv_sem=recv.at[t], device_id=(x, y, 1 - c), device_id_type=MESH))
    return cps


def _reduce_sibling_start(grads, after, name):
    n = len(grads)
    lands = [lax.empty((N_CHIPS, g.shape[1] // 2, g.shape[2]), BF16) for g in grads]

    def issue(refs, send, recv):
        for cp in _sibling_copies(refs, send, recv, n):
            cp.start()

    return _split_start(name, list(grads) + lands, n, after, issue)


def _reduce_sibling_wait(send, recv, arrays, after, name):
    n = len(arrays) // 2

    def await_all(refs, send_ref, recv_ref):
        for cp in _sibling_copies(refs, send_ref, recv_ref, n):
            cp.wait_send()
            cp.wait_recv()

    res = _split_wait(name, arrays, send, recv, after, await_all)
    return res[:n], res[n:]


def _add_sibling_half(grad, got, dev_idx, name):
    j, r, cols = grad.shape
    rh = r // 2
    tr = 128
    nb = rh // tr

    def body(idx_ref, g_ref, got_ref, out_ref):
        out_ref[...] = (g_ref[...].astype(F32) + got_ref[...].astype(F32)).astype(BF16)

    return pl.pallas_call(
        body, name=name,
        grid_spec=pltpu.PrefetchScalarGridSpec(
            num_scalar_prefetch=1, grid=(j, nb),
            in_specs=[pl.BlockSpec((None, tr, cols), lambda jj, i, idx: (jj, idx[2] * nb + i, 0)),
                      pl.BlockSpec((None, tr, cols), lambda jj, i, idx: (jj, i, 0))],
            out_specs=pl.BlockSpec((None, tr, cols), lambda jj, i, idx: (jj, i, 0))),
        out_shape=jax.ShapeDtypeStruct((j, rh, cols), BF16),
        compiler_params=_params("parallel", "parallel"),
    )(dev_idx, grad, got)


def _chip_copies(refs, send, recv, n, receiving):
    x, y, c = _position()
    chip = 2 * x + y
    cps = []
    for t in range(n):
        for k in range(1, N_CHIPS):
            px, py = _chip_peer(x, y, k)
            cps.append(pltpu.make_async_remote_copy(
                src_ref=refs[t].at[2 * px + py], dst_ref=refs[n + t].at[2 * px + py if receiving else chip],
                send_sem=send.at[3 * t + k - 1], recv_sem=recv.at[3 * t + k - 1],
                device_id=(px, py, c), device_id_type=MESH))
    return cps


def _reduce_chips_start(partials, after, name):
    n = len(partials)
    lands = [lax.empty(p.shape, BF16) for p in partials]

    def issue(refs, send, recv):
        for cp in _chip_copies(refs, send, recv, n, False):
            cp.start()

    return _split_start(name, list(partials) + lands, 3 * n, after, issue)


def _reduce_chips_wait(send, recv, arrays, after, name):
    n = len(arrays) // 2

    def await_all(refs, send_ref, recv_ref):
        for cp in _chip_copies(refs, send_ref, recv_ref, n, True):
            cp.wait_send()
            cp.wait_recv()

    res = _split_wait(name, arrays, send, recv, after, await_all)
    return res[:n], res[n:]


def _sum_partials(land, partial, dev_idx, name):
    _, rh, cols = land.shape
    tr = 128
    nb = rh // tr

    def body(idx_ref, l_ref, p_ref, o_ref):
        chip = idx_ref[1]
        acc = jnp.where(chip == 0, p_ref[...], l_ref[0]).astype(F32)
        for s in range(1, N_CHIPS):
            acc = acc + jnp.where(chip == s, p_ref[...], l_ref[s]).astype(F32)
        o_ref[...] = acc

    return pl.pallas_call(
        body, name=name,
        grid_spec=pltpu.PrefetchScalarGridSpec(
            num_scalar_prefetch=1, grid=(nb,),
            in_specs=[pl.BlockSpec((N_CHIPS, tr, cols), lambda i, idx: (0, i, 0)),
                      pl.BlockSpec((None, tr, cols), lambda i, idx: (idx[1], i, 0))],
            out_specs=pl.BlockSpec((tr, cols), lambda i, idx: (idx[2] * nb + i, 0))),
        out_shape=jax.ShapeDtypeStruct((2 * rh, cols), F32), compiler_params=_params("parallel"),
    )(dev_idx, land, partial)


def _share_halves(totals):
    n = len(totals)

    def body(*refs):
        ins, outs = refs[:n], refs[n:2 * n]
        send, recv = refs[2 * n:]
        x, y, c = _position()
        cps = []
        for t in range(n):
            rh = ins[t].shape[0] // 2
            mine = pl.ds(c * rh, rh)
            cp = pltpu.make_async_remote_copy(
                src_ref=ins[t].at[mine], dst_ref=outs[t].at[mine], send_sem=send.at[t], recv_sem=recv.at[t],
                device_id=(x, y, 1 - c), device_id_type=MESH)
            cp.start()
            cps.append(cp)
        for cp in cps:
            cp.wait()

    return pl.pallas_call(
        body, name="reduce_share_" + "_".join(str(t.shape[1]) for t in totals), in_specs=[ANY_SPEC] * n,
        out_specs=[ANY_SPEC] * n, out_shape=[jax.ShapeDtypeStruct(t.shape, F32) for t in totals],
        input_output_aliases={t: t for t in range(n)},
        scratch_shapes=[pltpu.SemaphoreType.DMA((n,)), pltpu.SemaphoreType.DMA((n,))],
    )(*totals)


def _exchange_halves(grads):
    n = len(grads)
    hbm = pl.BlockSpec(memory_space=pl.ANY)

    def body(*refs):
        ins, outs = refs[:n], refs[n:2 * n]
        send, recv = refs[2 * n:]
        x, y, c = _position()
        cps = []
        for t in range(n):
            rh = ins[t].shape[1] // 2
            cp = pltpu.make_async_remote_copy(
                src_ref=ins[t].at[pl.ds(0, N_CHIPS), pl.ds((1 - c) * rh, rh)], dst_ref=outs[t], send_sem=send.at[t],
                recv_sem=recv.at[t], device_id=(x, y, 1 - c), device_id_type=MESH)
            cp.start()
            cps.append(cp)
        for cp in cps:
            cp.wait()

    return pl.pallas_call(
        body, name="reduce_exchange_halves", in_specs=[hbm] * n, out_specs=[hbm] * n,
        out_shape=[jax.ShapeDtypeStruct((g.shape[0], g.shape[1] // 2, g.shape[2]), BF16) for g in grads],
        scratch_shapes=[pltpu.SemaphoreType.DMA((n,)), pltpu.SemaphoreType.DMA((n,))],
    )(*grads)


def _add_halves(grad, got, c_idx, name):
    j, r, cols = grad.shape
    rh = r // 2
    tr = 128
    nb = rh // tr

    def body(c_ref, g_ref, o_ref_in, out_ref):
        out_ref[...] = (g_ref[...].astype(F32) + o_ref_in[...].astype(F32)).astype(BF16)

    return pl.pallas_call(
        body, name=name,
        grid_spec=pltpu.PrefetchScalarGridSpec(
            num_scalar_prefetch=1, grid=(j, nb),
            in_specs=[pl.BlockSpec((None, tr, cols), lambda jj, i, c_ref: (jj, c_ref[0] * nb + i, 0)),
                      pl.BlockSpec((None, tr, cols), lambda jj, i, c_ref: (jj, i, 0))],
            out_specs=pl.BlockSpec((None, tr, cols), lambda jj, i, c_ref: (jj, i, 0))),
        out_shape=jax.ShapeDtypeStruct((j, rh, cols), BF16),
        compiler_params=_params("parallel", "parallel"),
    )(c_idx, grad, got)


def _scatter_partials(partials):
    n = len(partials)
    hbm = pl.BlockSpec(memory_space=pl.ANY)

    def body(*refs):
        ins, outs = refs[:n], refs[n:2 * n]
        send, recv, local = refs[2 * n:]
        x, y, c = _position()
        chip = 2 * x + y
        cps, lcs = [], []
        for t in range(n):
            lc = pltpu.make_async_copy(ins[t].at[chip], outs[t].at[chip], local.at[t])
            lc.start()
            lcs.append(lc)
            for k in range(1, N_CHIPS):
                px, py = _chip_peer(x, y, k)
                s = 3 * t + k - 1
                cp = pltpu.make_async_remote_copy(
                    src_ref=ins[t].at[2 * px + py], dst_ref=outs[t].at[chip], send_sem=send.at[s],
                    recv_sem=recv.at[s], device_id=(px, py, c), device_id_type=MESH)
                cp.start()
                cps.append(cp)
        for cp in cps:
            cp.wait()
        for lc in lcs:
            lc.wait()

    return pl.pallas_call(
        body, name="reduce_scatter_partials", in_specs=[hbm] * n, out_specs=[hbm] * n,
        out_shape=[jax.ShapeDtypeStruct(p.shape, BF16) for p in partials],
        scratch_shapes=[pltpu.SemaphoreType.DMA((3 * n,)), pltpu.SemaphoreType.DMA((3 * n,)),
                        pltpu.SemaphoreType.DMA((n,))],
    )(*partials)


def _sum_chips(parts, name):
    j, rh, cols = parts.shape
    tr = 128

    def body(p_ref, o_ref):
        acc = p_ref[0].astype(F32)
        for s in range(1, j):
            acc = acc + p_ref[s].astype(F32)
        o_ref[...] = acc

    return pl.pallas_call(
        body, name=name, grid=(rh // tr,),
        in_specs=[pl.BlockSpec((j, tr, cols), lambda i: (0, i, 0))],
        out_specs=pl.BlockSpec((tr, cols), lambda i: (i, 0)),
        out_shape=jax.ShapeDtypeStruct((rh, cols), F32),
        compiler_params=_params("parallel"),
    )(parts)


def _share_totals(halves):
    n = len(halves)
    hbm = pl.BlockSpec(memory_space=pl.ANY)

    def body(*refs):
        ins, outs = refs[:n], refs[n:2 * n]
        send, recv, local = refs[2 * n:]
        x, y, c = _position()
        cps, lcs = [], []
        for t in range(n):
            rh = ins[t].shape[0]
            mine = outs[t].at[pl.ds(c * rh, rh)]
            lc = pltpu.make_async_copy(ins[t], mine, local.at[t])
            lc.start()
            lcs.append(lc)
            cp = pltpu.make_async_remote_copy(
                src_ref=ins[t], dst_ref=mine, send_sem=send.at[t], recv_sem=recv.at[t],
                device_id=(x, y, 1 - c), device_id_type=MESH)
            cp.start()
            cps.append(cp)
        for cp in cps:
            cp.wait()
        for lc in lcs:
            lc.wait()

    return pl.pallas_call(
        body, name="reduce_share_totals", in_specs=[hbm] * n, out_specs=[hbm] * n,
        out_shape=[jax.ShapeDtypeStruct((2 * h.shape[0], h.shape[1]), F32) for h in halves],
        scratch_shapes=[pltpu.SemaphoreType.DMA((n,)), pltpu.SemaphoreType.DMA((n,)),
                        pltpu.SemaphoreType.DMA((n,))],
    )(*halves)


SMALL_ROWS = 56


def _reduce_small(packed, silu_c):
    ns = 3 * D_MODEL // N_CHIPS

    def body(p_ref, sc_ref, tot_ref, gw_ref, loss_ref, qk_ref, allp, send, recv):
        x, y, c = _position()
        me = 4 * x + 2 * y + c
        chip = 2 * x + y

        def copy(k):
            return pltpu.make_async_remote_copy(
                src_ref=allp.at[me], dst_ref=allp.at[me], send_sem=send.at[k - 1], recv_sem=recv.at[k - 1],
                device_id=_xor_peer(x, y, c, k), device_id_type=MESH)

        allp[me] = p_ref[...]
        for k in range(1, N_DEV):
            copy(k).start()
        for k in range(1, N_DEV):
            copy(k).wait_recv()
        tot = allp[0]
        for i in range(1, N_DEV):
            tot = tot + allp[i]
        tot_ref[...] = tot
        loss_ref[...] = jnp.sum(tot[11:12, :], axis=1, keepdims=True) * (0.5 / D_MODEL)
        fold = tot[5:11, 0:HEAD_DIM]
        for h in range(1, N_HEADS):
            fold = fold + tot[5:11, h * HEAD_DIM:(h + 1) * HEAD_DIM]
        qk_ref[...] = jnp.concatenate([fold, jnp.zeros((2, HEAD_DIM), F32)], axis=0)
        sct = sc_ref[...].T
        rc = 64
        for l in range(2):
            dms = [allp[i, pl.ds(12 + 4 * l + chip, 1), :][:, :ns] for i in range(N_DEV)]
            for r0 in range(0, D_MODEL, rc):
                acc = sct[r0:r0 + rc, 0:1] * dms[0]
                for i in range(1, N_DEV):
                    acc = acc + sct[r0:r0 + rc, i:i + 1] * dms[i]
                gw_ref[l, r0:r0 + rc, :] = acc
        for k in range(1, N_DEV):
            copy(k).wait_send()

    vm = pl.BlockSpec(memory_space=pltpu.VMEM)
    return pl.pallas_call(
        body, name="reduce_small", in_specs=[vm, vm], out_specs=[vm] * 4,
        out_shape=[jax.ShapeDtypeStruct((SMALL_ROWS, D_MODEL), F32), jax.ShapeDtypeStruct((2, D_MODEL, ns), F32),
                   jax.ShapeDtypeStruct((1, 1), F32), jax.ShapeDtypeStruct((8, HEAD_DIM), F32)],
        scratch_shapes=[pltpu.VMEM((N_DEV, SMALL_ROWS, D_MODEL), F32),
                        pltpu.SemaphoreType.DMA((N_DEV - 1,)), pltpu.SemaphoreType.DMA((N_DEV - 1,))],
        compiler_params=pltpu.CompilerParams(vmem_limit_bytes=VMEM_LIMIT_BYTES),
    )(packed, silu_c)


def _reduce_big(grads, c_idx):
    names = list(grads)
    got = _exchange_halves([grads[k] for k in names])
    partials = [_add_halves(grads[k], got[i], c_idx, f"reduce_add_{k}") for i, k in enumerate(names)]
    parts = _scatter_partials(partials)
    halves = [_sum_chips(parts[i], f"reduce_sum_{k}") for i, k in enumerate(names)]
    totals = _share_totals(halves)
    return dict(zip(names, totals))


def kernel(x, c, norm_g, ada_w, ada_b, a_w_in, a_conv_w, a_conv_b, a_ln_g, a_ln_b, a_w_out, b_w_in, b_q_norm, b_k_norm, b_w_out, loss_target, m_norm_g, m_ada_w, m_ada_b, m_a_w_in, m_a_conv_w, m_a_conv_b, m_a_ln_g, m_a_ln_b, m_a_w_out, m_b_w_in, m_b_q_norm, m_b_k_norm, m_b_w_out, v_norm_g, v_ada_w, v_ada_b, v_a_w_in, v_a_conv_w, v_a_conv_b, v_a_ln_g, v_a_ln_b, v_a_w_out, v_b_w_in, v_b_q_norm, v_b_k_norm, v_b_w_out):
    chip = 2 * lax.axis_index("x") + lax.axis_index("y")
    core = lax.axis_index("c")
    chip_idx = chip.astype(jnp.int32).reshape(1)
    dev_idx = jnp.stack([2 * chip + core, chip, core]).astype(jnp.int32)

    mods, silu_c, conv_w_full = _ada_forward(c, ada_w, ada_b, a_conv_w[0])
    lands_a = [_cast_into_slot(a_w_in[0], chip_idx, "cast_a_w_in"), _cast_into_slot(a_w_out[0], chip_idx, "cast_a_w_out")]
    send_a, recv_a, lands_a, token_a = _gather_start(lands_a, mods, "gather_start_a")
    lands_b = [_cast_into_slot(b_w_in[0], chip_idx, "cast_b_w_in"), _cast_into_slot(b_w_out[0], chip_idx, "cast_b_w_out")]
    send_b, recv_b, lands_b, token_b = _gather_start(lands_b, token_a, "gather_start_b")
    mods = mods + token_b[0:2, 0:1]

    def weights_a(after):
        send, recv, lands, _ = _gather_forward(send_a, recv_a, lands_a, after, "gather_forward_a")
        w_in, w_out = _gather_wait(send, recv, lands, after, "gather_wait_a")
        return w_in, w_out.reshape(D_MODEL, D_MODEL)

    forwarded_b = []

    def weights_b(after):
        send, recv, lands, _ = forwarded_b
        w_in, w_out = _gather_wait(send, recv, lands, after, "gather_wait_b")
        return w_in, w_out.reshape(D_MODEL, D_MODEL)

    def forward_weights_b(after):
        forwarded_b.extend(_gather_forward(send_b, recv_b, lands_b, after, "gather_forward_b"))
        return forwarded_b[3]

    stage1, stage2 = {}, {}

    def send_grads(tag, dw_in, dw_out):
        grads = [dw_in, dw_out.reshape(N_CHIPS, D_MODEL // N_CHIPS, D_MODEL)]
        send, recv, arrays, token = _reduce_sibling_start(grads, dw_out, f"reduce_d2d_start_{tag}")
        stage1[tag] = (send, recv, arrays)
        return token

    def forward_grads(tag, after):
        send, recv, arrays = stage1[tag]
        grads, got = _reduce_sibling_wait(send, recv, arrays, after, f"reduce_d2d_wait_{tag}")
        partials = [_add_sibling_half(grads[i], got[i], dev_idx, f"reduce_add_{tag}_{i}") for i in range(2)]
        send, recv, arrays, token = _reduce_chips_start(partials, partials[1], f"reduce_ici_start_{tag}")
        stage2[tag] = (send, recv, arrays)
        return token

    def finish_grads(tag, after):
        send, recv, arrays = stage2[tag]
        partials, lands = _reduce_chips_wait(send, recv, arrays, after, f"reduce_ici_wait_{tag}")
        totals = [_sum_partials(lands[i], partials[i], dev_idx, f"reduce_sum_{tag}_{i}") for i in range(2)]
        return _share_halves(totals)

    grad_x, small = _local_step(
        x[0], loss_target[0], mods.reshape(2, 3, D_MODEL), norm_g, conv_w_full, a_conv_b, a_ln_g[0:1],
        a_ln_b[0:1], b_q_norm[0], b_k_norm[0], weights_a, weights_b, forward_weights_b,
        functools.partial(send_grads, "b"), functools.partial(forward_grads, "b"), functools.partial(send_grads, "a"))

    ns = 3 * D_MODEL // N_CHIPS
    pad_mod = lambda dm: jnp.pad(dm.reshape(N_CHIPS, ns), ((0, 0), (0, D_MODEL - ns)))
    packed = jnp.concatenate([
        small["dnorm_g"], small["dconv_b"], small["dln_g"], small["dln_b"], small["dq_norm"], small["dk_norm"],
        small["loss_cols"], pad_mod(small["dmod0"]), pad_mod(small["dmod1"]), small["dconv_w"],
        jnp.zeros((SMALL_ROWS - 20 - CONV_WIDTH, D_MODEL), F32)], axis=0)
    tot, g_ada_w, loss, qk = _reduce_small(packed, silu_c)
    cw = D_MODEL // N_CHIPS
    g_small = dict(
        norm_g=tot[0:2], a_conv_b=tot[2:3], a_ln_g=tot[3:4], a_ln_b=tot[4:5],
        b_q_norm=qk[0:3], b_k_norm=qk[3:6],
        ada_b=jnp.stack([tot[12:16, :ns].reshape(3 * D_MODEL), tot[16:20, :ns].reshape(3 * D_MODEL)]),
        a_conv_w=lax.dynamic_slice(tot[20:20 + CONV_WIDTH], (0, chip * cw), (CONV_WIDTH, cw)),
    )


    given = dict(norm_g=(norm_g, m_norm_g, v_norm_g), ada_w=(ada_w, m_ada_w, v_ada_w), ada_b=(ada_b, m_ada_b, v_ada_b),
                 a_w_in=(a_w_in, m_a_w_in, v_a_w_in), a_conv_w=(a_conv_w, m_a_conv_w, v_a_conv_w),
                 a_conv_b=(a_conv_b, m_a_conv_b, v_a_conv_b), a_ln_g=(a_ln_g, m_a_ln_g, v_a_ln_g),
                 a_ln_b=(a_ln_b, m_a_ln_b, v_a_ln_b), a_w_out=(a_w_out, m_a_w_out, v_a_w_out),
                 b_w_in=(b_w_in, m_b_w_in, v_b_w_in), b_q_norm=(b_q_norm, m_b_q_norm, v_b_q_norm),
                 b_k_norm=(b_k_norm, m_b_k_norm, v_b_k_norm), b_w_out=(b_w_out, m_b_w_out, v_b_w_out))
    order = ["norm_g", "ada_w", "ada_b", "a_w_in", "a_conv_w", "a_conv_b", "a_ln_g", "a_ln_b", "a_w_out", "b_w_in",
             "b_q_norm", "b_k_norm", "b_w_out"]
    outs = {}

    def update(k, g2):
        w, m, v = given[k]
        shape2 = g2.shape
        d2, m2, v2 = _adamw(w.reshape(shape2), g2, m.reshape(shape2), v.reshape(shape2), f"adamw_{k}")
        outs[k] = tuple(a.reshape(w.shape) for a in (g2, d2, m2, v2))

    token = forward_grads("a", tot)
    g_b_in, g_b_out = finish_grads("b", token)
    update("b_w_in", g_b_in)
    update("b_w_out", g_b_out)
    update("ada_w", g_ada_w.reshape(2 * D_MODEL, ns))
    for k, g2 in g_small.items():
        update(k, g2)
    g_a_in, g_a_out = finish_grads("a", outs["b_w_in"][1])
    update("a_w_in", g_a_in)
    update("a_w_out", g_a_out)
    return (loss.reshape(()), grad_x[None], *[outs[k][0] for k in order], *[outs[k][1] for k in order],
            *[outs[k][2] for k in order], *[outs[k][3] for k in order])
```

```python
import functools

import jax
import jax.numpy as jnp
from jax import lax
from jax.experimental import pallas as pl
from jax.experimental.pallas import tpu as pltpu

F32 = jnp.float32
BF16 = jnp.bfloat16

SEQ = 2048
D_MODEL = 1024
CONV_WIDTH = 31
HEAD_DIM = 64
N_HEADS = 16
DILATIONS = (1, 4, 16)
ATTN_BLOCK = 128
NORM_EPS = 1e-6
NEG_INF = -1e30
N_DEV = 8
N_CHIPS = 4

ADAM_LR = 0.001
ADAM_B1 = 0.9
ADAM_B2 = 0.999
ADAM_EPS = 1e-08
ADAM_WD = 0.01
ADAM_STEP = 10

VMEM_LIMIT_BYTES = 52 * 1024 * 1024
HALO = 32
LANES = 128
MESH = pl.DeviceIdType.MESH


def _params(*sem):
    return pltpu.CompilerParams(dimension_semantics=sem or None, vmem_limit_bytes=VMEM_LIMIT_BYTES)


def _sigmoid(v):
    return 1.0 / (1.0 + jnp.exp(-v))


def _row_spec(tm, cols, col_block=0):
    return pl.BlockSpec((tm, cols), lambda i: (i, col_block))


def _vec_spec(rows, cols):
    return pl.BlockSpec((rows, cols), lambda i: (0, 0))


def _normmod(xv, g, scale, shift):
    r = lax.rsqrt(jnp.mean(xv * xv, axis=-1, keepdims=True) + NORM_EPS)
    return xv * r * g * (1.0 + scale) + shift


def _normmod_fwd(x, g, scale, shift, name):
    tm = 256

    def body(x_ref, g_ref, sc_ref, sh_ref, h_ref, ht_ref):
        h = _normmod(x_ref[...], g_ref[...], sc_ref[...], sh_ref[...])
        h_ref[...] = h.astype(BF16)
        ht_ref[...] = h.T.astype(BF16)

    return pl.pallas_call(
        body, name=name, grid=(SEQ // tm,),
        in_specs=[_row_spec(tm, D_MODEL)] + [_vec_spec(1, D_MODEL)] * 3,
        out_specs=[_row_spec(tm, D_MODEL), pl.BlockSpec((D_MODEL, tm), lambda i: (0, i))],
        out_shape=[jax.ShapeDtypeStruct((SEQ, D_MODEL), BF16), jax.ShapeDtypeStruct((D_MODEL, SEQ), BF16)],
        compiler_params=_params("parallel"),
    )(x, g, scale, shift)


def _normmod_bwd(x, g, scale, dh_parts, dres, name, part_dilations=None):
    tm = 256
    n_parts = len(dh_parts)
    dils = part_dilations or (1,) * n_parts
    dh_parts = [p if d == 1 else p.reshape(d, SEQ // d, D_MODEL) for p, d in zip(dh_parts, dils)]

    def body(x_ref, g_ref, sc_ref, dres_ref, *rest):
        part_refs = rest[:n_parts]
        dx_ref, sums_ref, nat = rest[n_parts:]
        xv = x_ref[...]
        r = lax.rsqrt(jnp.mean(xv * xv, axis=-1, keepdims=True) + NORM_EPS)
        xn = xv * r
        dh = _load_natural(part_refs[0], nat, dils[0])
        for p, d in zip(part_refs[1:], dils[1:]):
            dh = dh + _load_natural(p, nat, d)
        gv = g_ref[...]
        one_sc = 1.0 + sc_ref[...]
        dxn = dh * (gv * one_sc)
        dx = r * (dxn - xn * jnp.mean(dxn * xn, axis=-1, keepdims=True))
        dx_ref[...] = dres_ref[...] + dx
        dhx = dh * xn
        sums = jnp.concatenate([
            jnp.sum(dhx, axis=0, keepdims=True) * one_sc,
            jnp.sum(dhx, axis=0, keepdims=True) * gv,
            jnp.sum(dh, axis=0, keepdims=True),
            jnp.zeros((5, D_MODEL), F32)], axis=0)

        @pl.when(pl.program_id(0) == 0)
        def _():
            sums_ref[...] = jnp.zeros_like(sums_ref)

        sums_ref[...] += sums

    return pl.pallas_call(
        body, name=name, grid=(SEQ // tm,),
        in_specs=[_row_spec(tm, D_MODEL), _vec_spec(1, D_MODEL), _vec_spec(1, D_MODEL), _row_spec(tm, D_MODEL)]
        + [_class_spec(tm, d) for d in dils],
        out_specs=[_row_spec(tm, D_MODEL), _vec_spec(8, D_MODEL)],
        out_shape=[jax.ShapeDtypeStruct((SEQ, D_MODEL), F32), jax.ShapeDtypeStruct((8, D_MODEL), F32)],
        scratch_shapes=[_natural_scratch(tm)],
        compiler_params=_params("arbitrary"),
    )(x, g, scale, dres, *dh_parts)


def _mm(lhs, rhs, *, tn, tile0, n_tiles, out_dtype, name, out3d=None, prev=None):
    mo, kc = lhs.shape
    cm = min(mo, 1024)

    def body(l_ref, r_ref, *rest):
        o_ref = rest[-1]
        for m in range(mo // cm):
            rows = pl.ds(m * cm, cm)
            o_ref[rows, :] = jnp.dot(l_ref[rows, :], r_ref[...], preferred_element_type=F32).astype(out_dtype)

    if rhs.ndim == 3:
        tps_r = rhs.shape[2] // tn
        r_spec = pl.BlockSpec((None, kc, tn), lambda t: ((tile0 + t) // tps_r, 0, (tile0 + t) % tps_r))
    else:
        r_spec = pl.BlockSpec((kc, tn), lambda t: (0, t))
    in_specs = [pl.BlockSpec((mo, kc), lambda t: (0, 0)), r_spec]
    args = [lhs, rhs]
    aliases = {}
    if out3d is None:
        o_spec = pl.BlockSpec((mo, tn), lambda t: (0, t))
        o_shape = jax.ShapeDtypeStruct((mo, n_tiles * tn), out_dtype)
    else:
        j_out, ns_out = out3d
        tps_o = ns_out // tn
        o_spec = pl.BlockSpec((None, mo, tn), lambda t: ((tile0 + t) // tps_o, 0, (tile0 + t) % tps_o))
        o_shape = jax.ShapeDtypeStruct((j_out, mo, ns_out), out_dtype)
        if prev is not None:
            in_specs.append(pl.BlockSpec(memory_space=pl.ANY))
            args.append(prev)
            aliases = {2: 0}
    return pl.pallas_call(
        body, name=name, grid=(n_tiles,), in_specs=in_specs, out_specs=o_spec, out_shape=o_shape,
        input_output_aliases=aliases, compiler_params=_params("parallel"),
    )(*args)


def _mm_nt(dy, w3, *, tn, tile0, n_tiles, name, after=None):
    m_rows = dy.shape[0]
    _, kc, ns = w3.shape
    tps = ns // tn
    cm = 512
    extra = [] if after is None else [after]

    def body(dy_ref, w_ref, *rest):
        o_ref = rest[-1]

        @pl.when(pl.program_id(0) == 0)
        def _():
            o_ref[...] = jnp.zeros_like(o_ref)

        for m in range(m_rows // cm):
            rows = pl.ds(m * cm, cm)
            o_ref[rows, :] += lax.dot_general(dy_ref[rows, :], w_ref[...], (((1,), (1,)), ((), ())),
                                              preferred_element_type=F32)

    return pl.pallas_call(
        body, name=name, grid=(n_tiles,),
        in_specs=[pl.BlockSpec((m_rows, tn), lambda t: (0, t)),
                  pl.BlockSpec((None, kc, tn), lambda t: ((tile0 + t) // tps, 0, (tile0 + t) % tps))]
        + [pl.BlockSpec(memory_space=pl.ANY)] * len(extra),
        out_specs=pl.BlockSpec((m_rows, kc), lambda t: (0, 0)),
        out_shape=jax.ShapeDtypeStruct((m_rows, kc), F32),
        compiler_params=_params("arbitrary"),
    )(dy, w3, *extra)


CONV_CHUNK = 16


def _shift_copies(buf, shifted):
    rows = shifted.shape[1]
    for s in range(1, 8):
        shifted[s - 1] = buf[pl.ds(s, rows), :]


def _shifted_rows(buf, shifted, offset, r0):
    s = offset % 8
    if s == 0:
        return buf[pl.ds(r0 + offset, CONV_CHUNK), :]
    return shifted[s - 1, pl.ds(r0 + (offset - s), CONV_CHUNK), :]


def _conv_fwd(proj, conv_w, conv_b, ln_g, ln_b, name):
    tm = 256
    hb = tm // HALO

    def body(vg_ref, halo_ref, z_ref, w_ref, b_ref, g_ref, be_ref, u5_ref, u5t_ref, u2_ref, buf, shifted):
        i = pl.program_id(0)
        u1 = vg_ref[:, :D_MODEL] * _sigmoid(vg_ref[:, D_MODEL:])
        u1h = halo_ref[:, :D_MODEL] * _sigmoid(halo_ref[:, D_MODEL:])
        buf[pl.ds(0, HALO), :] = jnp.where(i > 0, u1h, 0.0)
        buf[pl.ds(HALO, tm), :] = u1
        _shift_copies(buf, shifted)

        def chunk(ci, carry):
            r0 = pl.multiple_of(ci * CONV_CHUNK, CONV_CHUNK)
            acc = jnp.broadcast_to(b_ref[...], (CONV_CHUNK, D_MODEL))
            for k in range(CONV_WIDTH):
                acc = acc + w_ref[k:k + 1, :] * _shifted_rows(buf, shifted, HALO - (CONV_WIDTH - 1) + k, r0)
            u2_ref[pl.ds(r0, CONV_CHUNK), :] = acc
            return carry

        lax.fori_loop(0, tm // CONV_CHUNK, chunk, 0)
        acc = u2_ref[...]
        mu = jnp.mean(acc, axis=-1, keepdims=True)
        xc = acc - mu
        rstd = lax.rsqrt(jnp.mean(xc * xc, axis=-1, keepdims=True) + NORM_EPS)
        u3 = xc * rstd * g_ref[...] + be_ref[...]
        zv = z_ref[...]
        u5 = u3 * _sigmoid(u3) * (zv * _sigmoid(zv))
        u5_ref[...] = u5.astype(BF16)
        u5t_ref[...] = u5.T.astype(BF16)

    return pl.pallas_call(
        body, name=name, grid=(SEQ // tm,),
        in_specs=[pl.BlockSpec((tm, 2 * D_MODEL), lambda i: (i, 0)),
                  pl.BlockSpec((HALO, 2 * D_MODEL), lambda i: (jnp.maximum(i * hb - 1, 0), 0)),
                  _row_spec(tm, D_MODEL, 2),
                  _vec_spec(CONV_WIDTH, D_MODEL)] + [_vec_spec(1, D_MODEL)] * 3,
        out_specs=[_row_spec(tm, D_MODEL), pl.BlockSpec((D_MODEL, tm), lambda i: (0, i)), _row_spec(tm, D_MODEL)],
        out_shape=[jax.ShapeDtypeStruct((SEQ, D_MODEL), BF16), jax.ShapeDtypeStruct((D_MODEL, SEQ), BF16),
                   jax.ShapeDtypeStruct((SEQ, D_MODEL), F32)],
        scratch_shapes=[pltpu.VMEM((HALO + tm, D_MODEL), F32), pltpu.VMEM((7, HALO + tm - 8, D_MODEL), F32)],
        compiler_params=_params("parallel"),
    )(proj, proj, proj, conv_w, conv_b, ln_g, ln_b)


def _conv_bwd_pointwise(du5, proj, u2, ln_g, ln_b, name):
    tm = 256

    def body(du5_ref, z_ref, u2_ref, g_ref, be_ref, du2_ref, dz_ref, sums_ref):
        u2v = u2_ref[...]
        mu = jnp.mean(u2v, axis=-1, keepdims=True)
        xc = u2v - mu
        rstd = lax.rsqrt(jnp.mean(xc * xc, axis=-1, keepdims=True) + NORM_EPS)
        xhat = xc * rstd
        u3 = xhat * g_ref[...] + be_ref[...]
        s3 = _sigmoid(u3)
        u4 = u3 * s3
        zv = z_ref[...]
        sz = _sigmoid(zv)
        du5v = du5_ref[...]
        dz_ref[...] = du5v * u4 * (sz * (1.0 + zv * (1.0 - sz)))
        du3 = du5v * (zv * sz) * (s3 * (1.0 + u3 * (1.0 - s3)))
        dxhat = du3 * g_ref[...]
        du2 = rstd * (dxhat - jnp.mean(dxhat, axis=-1, keepdims=True)
                      - xhat * jnp.mean(dxhat * xhat, axis=-1, keepdims=True))
        du2_ref[...] = du2
        sums = jnp.concatenate([
            jnp.sum(du3 * xhat, axis=0, keepdims=True),
            jnp.sum(du3, axis=0, keepdims=True),
            jnp.sum(du2, axis=0, keepdims=True),
            jnp.zeros((5, D_MODEL), F32)], axis=0)

        @pl.when(pl.program_id(0) == 0)
        def _():
            sums_ref[...] = jnp.zeros_like(sums_ref)

        sums_ref[...] += sums

    return pl.pallas_call(
        body, name=name, grid=(SEQ // tm,),
        in_specs=[_row_spec(tm, D_MODEL), _row_spec(tm, D_MODEL, 2), _row_spec(tm, D_MODEL),
                  _vec_spec(1, D_MODEL), _vec_spec(1, D_MODEL)],
        out_specs=[_row_spec(tm, D_MODEL), _row_spec(tm, D_MODEL), _vec_spec(8, D_MODEL)],
        out_shape=[jax.ShapeDtypeStruct((SEQ, D_MODEL), F32), jax.ShapeDtypeStruct((SEQ, D_MODEL), F32),
                   jax.ShapeDtypeStruct((8, D_MODEL), F32)],
        compiler_params=_params("arbitrary"),
    )(du5, proj, u2, ln_g, ln_b)


def _conv_bwd_taps(du2, dz, proj, conv_w, name):
    tm = 256
    hb = tm // HALO
    n_blocks = SEQ // tm

    def body(du2_ref, dnext_ref, dz_ref, vg_ref, halo_ref, w_ref, dproj_ref, dw_ref,
             ubuf, dbuf, ushift, dshift, sgbuf, dwacc):
        i = pl.program_id(0)
        sg = _sigmoid(vg_ref[:, D_MODEL:])
        sgbuf[...] = sg
        u1h = halo_ref[:, :D_MODEL] * _sigmoid(halo_ref[:, D_MODEL:])
        ubuf[pl.ds(0, HALO), :] = jnp.where(i > 0, u1h, 0.0)
        ubuf[pl.ds(HALO, tm), :] = vg_ref[:, :D_MODEL] * sg
        dbuf[pl.ds(0, tm), :] = du2_ref[...]
        dbuf[pl.ds(tm, HALO), :] = jnp.where(i < n_blocks - 1, dnext_ref[...], 0.0)
        _shift_copies(ubuf, ushift)
        _shift_copies(dbuf, dshift)

        @pl.when(i == 0)
        def _():
            dwacc[...] = jnp.zeros_like(dwacc)

        def chunk(ci, carry):
            r0 = pl.multiple_of(ci * CONV_CHUNK, CONV_CHUNK)
            rows = pl.ds(r0, CONV_CHUNK)
            du2c = du2_ref[rows, :]
            du1 = jnp.zeros((CONV_CHUNK, D_MODEL), F32)
            for k in range(CONV_WIDTH):
                du1 = du1 + w_ref[k:k + 1, :] * _shifted_rows(dbuf, dshift, CONV_WIDTH - 1 - k, r0)
                prod = du2c * _shifted_rows(ubuf, ushift, HALO - (CONV_WIDTH - 1) + k, r0)
                dwacc[k] += prod[0:8] + prod[8:16]
            sgc = sgbuf[rows, :]
            dval = du1 * sgc
            dproj_ref[rows, 0:D_MODEL] = dval.astype(BF16)
            dproj_ref[rows, D_MODEL:2 * D_MODEL] = (dval * vg_ref[rows, 0:D_MODEL] * (1.0 - sgc)).astype(BF16)
            return carry

        lax.fori_loop(0, tm // CONV_CHUNK, chunk, 0)
        dproj_ref[:, 2 * D_MODEL:] = dz_ref[...].astype(BF16)

        @pl.when(i == n_blocks - 1)
        def _():
            for k in range(CONV_WIDTH):
                dw_ref[k:k + 1, :] = jnp.sum(dwacc[k], axis=0, keepdims=True)
            dw_ref[CONV_WIDTH:, :] = jnp.zeros((32 - CONV_WIDTH, D_MODEL), F32)

    return pl.pallas_call(
        body, name=name, grid=(n_blocks,),
        in_specs=[_row_spec(tm, D_MODEL),
                  pl.BlockSpec((HALO, D_MODEL), lambda i: (jnp.minimum((i + 1) * hb, SEQ // HALO - 1), 0)),
                  _row_spec(tm, D_MODEL),
                  pl.BlockSpec((tm, 2 * D_MODEL), lambda i: (i, 0)),
                  pl.BlockSpec((HALO, 2 * D_MODEL), lambda i: (jnp.maximum(i * hb - 1, 0), 0)),
                  _vec_spec(CONV_WIDTH, D_MODEL)],
        out_specs=[_row_spec(tm, 3 * D_MODEL), _vec_spec(32, D_MODEL)],
        out_shape=[jax.ShapeDtypeStruct((SEQ, 3 * D_MODEL), BF16), jax.ShapeDtypeStruct((32, D_MODEL), F32)],
        scratch_shapes=[pltpu.VMEM((HALO + tm, D_MODEL), F32), pltpu.VMEM((tm + HALO, D_MODEL), F32),
                        pltpu.VMEM((7, HALO + tm - 8, D_MODEL), F32), pltpu.VMEM((7, HALO + tm - 8, D_MODEL), F32),
                        pltpu.VMEM((tm, D_MODEL), F32), pltpu.VMEM((CONV_WIDTH, 8, D_MODEL), F32)],
        compiler_params=_params("arbitrary"),
    )(du2, du2, dz, proj, proj, conv_w)


def _out_a(u5, w_out, x, gate, g1, scale1, shift1, name):
    tm = 256
    n_d = len(DILATIONS)

    def body(u_ref, w_ref, x_ref, gate_ref, g_ref, sc_ref, sh_ref, x1_ref, y_ref, ht_ref, *rest):
        h_refs, nat = rest[:n_d], rest[-1]
        y = jnp.dot(u_ref[...], w_ref[...], preferred_element_type=F32)
        x1 = x_ref[...] + gate_ref[...] * y
        y_ref[...] = y
        x1_ref[...] = x1
        h = _normmod(x1, g_ref[...], sc_ref[...], sh_ref[...])
        ht_ref[...] = h.T.astype(BF16)
        for h_ref, d in zip(h_refs, DILATIONS):
            _store_classes(h_ref, h, nat, d)

    res = pl.pallas_call(
        body, name=name, grid=(SEQ // tm,),
        in_specs=[_row_spec(tm, D_MODEL), _vec_spec(D_MODEL, D_MODEL), _row_spec(tm, D_MODEL)]
        + [_vec_spec(1, D_MODEL)] * 4,
        out_specs=[_row_spec(tm, D_MODEL), _row_spec(tm, D_MODEL), pl.BlockSpec((D_MODEL, tm), lambda i: (0, i))]
        + [_class_spec(tm, d) for d in DILATIONS],
        out_shape=[jax.ShapeDtypeStruct((SEQ, D_MODEL), F32), jax.ShapeDtypeStruct((SEQ, D_MODEL), F32),
                   jax.ShapeDtypeStruct((D_MODEL, SEQ), BF16)] + [_class_shape(d, BF16) for d in DILATIONS],
        scratch_shapes=[_natural_scratch(tm)],
        compiler_params=_params("parallel"),
    )(u5, w_out, x, gate, g1, scale1, shift1)
    return res[0], res[1], res[2], [a.reshape(SEQ, D_MODEL) for a in res[3:]]


def _out_b_loss(u, w_out, x1, gate, target, name):
    tm = 256

    def body(u_ref, w_ref, x_ref, gate_ref, t_ref, e_ref, dy_ref, sums_ref):
        y = jnp.dot(u_ref[...], w_ref[...], preferred_element_type=F32)
        diff = x_ref[...] + gate_ref[...] * y - t_ref[...]
        e = diff * (1.0 / D_MODEL)
        e_ref[...] = e
        dy_ref[...] = (e * gate_ref[...]).astype(BF16)
        sums = jnp.concatenate([
            jnp.sum(e * y, axis=0, keepdims=True),
            jnp.sum(diff * diff, axis=0, keepdims=True),
            jnp.zeros((6, D_MODEL), F32)], axis=0)

        @pl.when(pl.program_id(0) == 0)
        def _():
            sums_ref[...] = jnp.zeros_like(sums_ref)

        sums_ref[...] += sums

    return pl.pallas_call(
        body, name=name, grid=(SEQ // tm,),
        in_specs=[_row_spec(tm, D_MODEL), _vec_spec(D_MODEL, D_MODEL), _row_spec(tm, D_MODEL),
                  _vec_spec(1, D_MODEL), _row_spec(tm, D_MODEL)],
        out_specs=[_row_spec(tm, D_MODEL), _row_spec(tm, D_MODEL), _vec_spec(8, D_MODEL)],
        out_shape=[jax.ShapeDtypeStruct((SEQ, D_MODEL), F32), jax.ShapeDtypeStruct((SEQ, D_MODEL), BF16),
                   jax.ShapeDtypeStruct((8, D_MODEL), F32)],
        compiler_params=_params("arbitrary"),
    )(u, w_out, x1, gate, target)


def _dgate_dy(dx1, y, gate, name):
    tm = 256

    def body(d_ref, y_ref, gate_ref, dy_ref, sums_ref):
        dv = d_ref[...]
        dy_ref[...] = (dv * gate_ref[...]).astype(BF16)
        sums = jnp.concatenate([jnp.sum(dv * y_ref[...], axis=0, keepdims=True), jnp.zeros((7, D_MODEL), F32)], axis=0)

        @pl.when(pl.program_id(0) == 0)
        def _():
            sums_ref[...] = jnp.zeros_like(sums_ref)

        sums_ref[...] += sums

    return pl.pallas_call(
        body, name=name, grid=(SEQ // tm,),
        in_specs=[_row_spec(tm, D_MODEL), _row_spec(tm, D_MODEL), _vec_spec(1, D_MODEL)],
        out_specs=[_row_spec(tm, D_MODEL), _vec_spec(8, D_MODEL)],
        out_shape=[jax.ShapeDtypeStruct((SEQ, D_MODEL), BF16), jax.ShapeDtypeStruct((8, D_MODEL), F32)],
        compiler_params=_params("arbitrary"),
    )(dx1, y, gate)


def _mm_nt_res(dy, w, name):
    tm = 256
    kc, n = w.shape

    def body(dy_ref, w_ref, o_ref):
        o_ref[...] = lax.dot_general(dy_ref[...], w_ref[...], (((1,), (1,)), ((), ())), preferred_element_type=F32)

    return pl.pallas_call(
        body, name=name, grid=(SEQ // tm,),
        in_specs=[_row_spec(tm, n), _vec_spec(kc, n)],
        out_specs=_row_spec(tm, kc),
        out_shape=jax.ShapeDtypeStruct((SEQ, kc), F32),
        compiler_params=_params("parallel"),
    )(dy, w)


def _seg_matrix():
    r = lax.broadcasted_iota(jnp.int32, (256, 256), 0) // HEAD_DIM
    c = lax.broadcasted_iota(jnp.int32, (256, 256), 1) // HEAD_DIM
    return (r == c).astype(BF16)


def _segsum(v, seg):
    hi = v.astype(BF16)
    lo = (v - hi.astype(F32)).astype(BF16)
    outs = []
    for c0 in range(0, D_MODEL, 256):
        outs.append(jnp.dot(hi[:, c0:c0 + 256], seg, preferred_element_type=F32)
                    + jnp.dot(lo[:, c0:c0 + 256], seg, preferred_element_type=F32))
    return jnp.concatenate(outs, axis=1)


def _qk_rstd(v, seg):
    return lax.rsqrt(_segsum(v * v, seg) * (1.0 / HEAD_DIM) + NORM_EPS)


def _qknorm_fwd(proj, qw, kw, seg, name):
    tm = 256

    def body(p_ref, qw_ref, kw_ref, seg_ref, q_ref, k_ref):
        segv = seg_ref[...]
        q = p_ref[:, :D_MODEL].astype(F32)
        k = p_ref[:, D_MODEL:].astype(F32)
        q_ref[...] = (q * _qk_rstd(q, segv) * qw_ref[...]).astype(BF16)
        k_ref[...] = (k * _qk_rstd(k, segv) * kw_ref[...]).astype(BF16)

    return pl.pallas_call(
        body, name=name, grid=(SEQ // tm,),
        in_specs=[_row_spec(tm, 2 * D_MODEL), _vec_spec(1, D_MODEL), _vec_spec(1, D_MODEL), _vec_spec(256, 256)],
        out_specs=[_row_spec(tm, D_MODEL)] * 2,
        out_shape=[jax.ShapeDtypeStruct((SEQ, D_MODEL), BF16)] * 2,
        compiler_params=_params("parallel"),
    )(proj, qw, kw, seg)


def _attn_masks(b, bpc, dilation, slope):
    if bpc == 1:
        qi = lax.broadcasted_iota(jnp.int32, (ATTN_BLOCK, ATTN_BLOCK), 0)
        kj = lax.broadcasted_iota(jnp.int32, (ATTN_BLOCK, ATTN_BLOCK), 1)
        steps = qi - kj
        return (steps * dilation).astype(F32), steps >= 0
    qi = lax.broadcasted_iota(jnp.int32, (ATTN_BLOCK, 2 * ATTN_BLOCK), 0)
    kj = lax.broadcasted_iota(jnp.int32, (ATTN_BLOCK, 2 * ATTN_BLOCK), 1)
    steps = qi + ATTN_BLOCK - kj
    has_prev = (b % bpc) != 0
    valid = (steps >= 0) & (steps <= ATTN_BLOCK) & (has_prev | (kj >= ATTN_BLOCK))
    return (steps * dilation).astype(F32), valid


def _key_tile(prev_ref, cur_ref, cols, bpc):
    if bpc == 1:
        return cur_ref[:, cols]
    return jnp.concatenate([prev_ref[:, cols], cur_ref[:, cols]], axis=0)


ATTN_HEADS_FWD = 16
ATTN_HEADS_BWD = 16
NT_DIMS = (((1,), (1,)), ((), ()))
TN_DIMS = (((0,), (0,)), ((), ()))
BATCH_NT_DIMS = (((2,), (2,)), ((0,), (0,)))
BATCH_NN_DIMS = (((2,), (1,)), ((0,), (0,)))
BATCH_TN_DIMS = (((1,), (1,)), ((0,), (0,)))


def _head_stack(tile_of, heads):
    return jnp.stack([tile_of(slice(h * HEAD_DIM, (h + 1) * HEAD_DIM)) for h in range(heads)], axis=0)


def _attn_specs(heads, segment=0):
    width = heads * HEAD_DIM
    off = segment * (D_MODEL // width)
    last = SEQ // ATTN_BLOCK - 1
    cur = pl.BlockSpec((ATTN_BLOCK, width), lambda hg, b: (jnp.minimum(b, last), hg + off))
    prev = pl.BlockSpec((ATTN_BLOCK, width), lambda hg, b: (jnp.clip(b - 1, 0, last), hg + off))
    return cur, prev


def _attn_fwd(q, k, proj, slopes, dilation, name):
    bpc = SEQ // dilation // ATTN_BLOCK
    heads = ATTN_HEADS_FWD
    assert heads == N_HEADS
    cur, prev = _attn_specs(heads)
    v_cur, v_prev = _attn_specs(heads, segment=2)
    scale = HEAD_DIM ** -0.5

    def body(sl_ref, q_ref, kp_ref, kc_ref, vp_ref, vc_ref, o_ref, lse_ref):
        dist, valid = _attn_masks(pl.program_id(1), bpc, dilation, None)
        q3 = _head_stack(lambda cols: q_ref[:, cols], heads)
        k3 = _head_stack(lambda cols: _key_tile(kp_ref, kc_ref, cols, bpc), heads)
        v3 = _head_stack(lambda cols: _key_tile(vp_ref, vc_ref, cols, bpc), heads)
        s = lax.dot_general(q3, k3, BATCH_NT_DIMS, preferred_element_type=F32)
        s = jnp.where(valid[None], s * scale - dist[None] * sl_ref[...], NEG_INF)
        m = jnp.max(s, axis=-1, keepdims=True)
        p = jnp.exp(s - m)
        l = jnp.sum(p, axis=-1, keepdims=True)
        o3 = lax.dot_general(p.astype(BF16), v3, BATCH_NN_DIMS, preferred_element_type=F32) / l
        lse3 = m + jnp.log(l)
        for h in range(heads):
            o_ref[:, h * HEAD_DIM:(h + 1) * HEAD_DIM] = o3[h]
        lse_ref[...] = jnp.concatenate([lse3[h] for h in range(heads)], axis=1)

    return pl.pallas_call(
        body, name=name, grid=(N_HEADS // heads, SEQ // ATTN_BLOCK),
        in_specs=[pl.BlockSpec((heads, 1, 1), lambda hg, b: (hg, 0, 0)), cur, prev, cur, v_prev, v_cur],
        out_specs=[cur, pl.BlockSpec((ATTN_BLOCK, N_HEADS), lambda hg, b: (b, 0))],
        out_shape=[jax.ShapeDtypeStruct((SEQ, D_MODEL), F32), jax.ShapeDtypeStruct((SEQ, N_HEADS), F32)],
        compiler_params=_params("parallel", "parallel"),
    )(slopes.reshape(N_HEADS, 1, 1), q, k, k, proj, proj)


def _class_spec(tm, dilation):
    if dilation == 1:
        return _row_spec(tm, D_MODEL)
    return pl.BlockSpec((dilation, tm // dilation, D_MODEL), lambda i: (0, i, 0))


def _class_shape(dilation, dtype):
    if dilation == 1:
        return jax.ShapeDtypeStruct((SEQ, D_MODEL), dtype)
    return jax.ShapeDtypeStruct((dilation, SEQ // dilation, D_MODEL), dtype)


def _load_natural(in_ref, nat_ref, dilation):
    if dilation == 1:
        return in_ref[...].astype(F32)
    n = nat_ref.shape[1] // dilation
    for r in range(dilation):
        for j in range(D_MODEL // LANES):
            nat_ref.at[j][pl.ds(r, n, stride=dilation), :] = in_ref[r, :, j * LANES:(j + 1) * LANES].astype(F32)
    return jnp.concatenate([nat_ref[j] for j in range(D_MODEL // LANES)], axis=1)


def _store_classes(out_ref, value, nat_ref, dilation):
    if dilation == 1:
        out_ref[...] = value.astype(out_ref.dtype)
        return
    n = nat_ref.shape[1] // dilation
    for j in range(D_MODEL // LANES):
        nat_ref[j] = value[:, j * LANES:(j + 1) * LANES]
    for r in range(dilation):
        for j in range(D_MODEL // LANES):
            out_ref[r, :, j * LANES:(j + 1) * LANES] = (
                nat_ref.at[j][pl.ds(r, n, stride=dilation), :].astype(out_ref.dtype))


def _natural_scratch(tm):
    return pltpu.VMEM((D_MODEL // LANES, tm, LANES), F32)


def _head_selector():
    lane_head = lax.broadcasted_iota(jnp.int32, (D_MODEL, N_HEADS), 0) // HEAD_DIM
    head = lax.broadcasted_iota(jnp.int32, (D_MODEL, N_HEADS), 1)
    return (lane_head == head).astype(BF16)


def _dot_split(v, m01, dims):
    hi = v.astype(BF16)
    lo = (v - hi.astype(F32)).astype(BF16)
    return (lax.dot_general(hi, m01, dims, preferred_element_type=F32)
            + lax.dot_general(lo, m01, dims, preferred_element_type=F32))


def _merge_fwd(o_parts, lse_parts, z, sel, name):
    tm = 256
    h_spec = pl.BlockSpec((tm, N_HEADS), lambda i: (i, 0))

    def body(o0, o1, o2, l0, l1, l2, z_ref, sel_ref, u_ref, ut_ref, o_ref, lse_ref, nat):
        ls = [l0[...], l1[...], l2[...]]
        m = jnp.maximum(jnp.maximum(ls[0], ls[1]), ls[2])
        tot = m + jnp.log(jnp.exp(ls[0] - m) + jnp.exp(ls[1] - m) + jnp.exp(ls[2] - m))
        o = jnp.zeros((tm, D_MODEL), F32)
        for o_in, l, d in zip((o0, o1, o2), ls, DILATIONS):
            weight = _dot_split(jnp.exp(l - tot), sel_ref[...], NT_DIMS)
            o = o + weight * _load_natural(o_in, nat, d)
        zv = z_ref[...]
        u = o * (zv * _sigmoid(zv))
        u_ref[...] = u.astype(BF16)
        ut_ref[...] = u.T.astype(BF16)
        o_ref[...] = o
        lse_ref[...] = tot

    return pl.pallas_call(
        body, name=name, grid=(SEQ // tm,),
        in_specs=[_class_spec(tm, d) for d in DILATIONS] + [h_spec] * 3
        + [_row_spec(tm, D_MODEL), _vec_spec(D_MODEL, N_HEADS)],
        out_specs=[_row_spec(tm, D_MODEL), pl.BlockSpec((D_MODEL, tm), lambda i: (0, i)),
                   _row_spec(tm, D_MODEL), h_spec],
        out_shape=[jax.ShapeDtypeStruct((SEQ, D_MODEL), BF16), jax.ShapeDtypeStruct((D_MODEL, SEQ), BF16),
                   jax.ShapeDtypeStruct((SEQ, D_MODEL), F32), jax.ShapeDtypeStruct((SEQ, N_HEADS), F32)],
        scratch_shapes=[_natural_scratch(tm)],
        compiler_params=_params("parallel"),
    )(*o_parts, *lse_parts, z, sel)


def _merge_bwd(du, o, z, sel, name):
    tm = 256
    n_d = len(DILATIONS)

    def body(du_ref, o_ref, z_ref, sel_ref, dz_ref, delta_ref, *rest):
        do_refs, nat = rest[:n_d], rest[-1]
        zv = z_ref[...]
        sz = _sigmoid(zv)
        duv = du_ref[...]
        ov = o_ref[...]
        do = duv * (zv * sz)
        dz_ref[...] = (duv * ov * (sz * (1.0 + zv * (1.0 - sz)))).astype(BF16)
        delta_ref[...] = _dot_split(do * ov, sel_ref[...], (((1,), (0,)), ((), ())))
        for do_ref, d in zip(do_refs, DILATIONS):
            _store_classes(do_ref, do, nat, d)

    res = pl.pallas_call(
        body, name=name, grid=(SEQ // tm,),
        in_specs=[_row_spec(tm, D_MODEL)] * 3 + [_vec_spec(D_MODEL, N_HEADS)],
        out_specs=[_row_spec(tm, D_MODEL), pl.BlockSpec((tm, N_HEADS), lambda i: (i, 0))]
        + [_class_spec(tm, d) for d in DILATIONS],
        out_shape=[jax.ShapeDtypeStruct((SEQ, D_MODEL), BF16), jax.ShapeDtypeStruct((SEQ, N_HEADS), F32)]
        + [_class_shape(d, BF16) for d in DILATIONS],
        scratch_shapes=[_natural_scratch(tm)],
        compiler_params=_params("parallel"),
    )(du, o, z, sel)
    return res[0], res[1], [a.reshape(SEQ, D_MODEL) for a in res[2:]]


def _attn_bwd(q, k, proj, do, lse, delta, slopes, dilation, name):
    bpc = SEQ // dilation // ATTN_BLOCK
    heads = ATTN_HEADS_BWD
    n_blocks = SEQ // ATTN_BLOCK
    carry = bpc > 1
    width = heads * HEAD_DIM
    cur, prev = _attn_specs(heads)
    v_cur, v_prev = _attn_specs(heads, segment=2)
    assert heads == N_HEADS
    per_head = pl.BlockSpec((ATTN_BLOCK, N_HEADS), lambda hg, b: (jnp.minimum(b, n_blocks - 1), 0))
    scale = HEAD_DIM ** -0.5

    def body(sl_ref, q_ref, kp_ref, kc_ref, vp_ref, vc_ref, do_ref, lse_ref, dl_ref,
             dq_ref, dk_ref, dv_ref, *scratch):
        b = pl.program_id(1)
        if carry:
            dk_carry, dv_carry = scratch

            @pl.when(b == n_blocks)
            def _():
                dk_ref[...] = dk_carry[...].astype(BF16)
                dv_ref[...] = dv_carry[...].astype(BF16)

            @pl.when(b < n_blocks)
            def _():
                step(sl_ref, q_ref, kp_ref, kc_ref, vp_ref, vc_ref, do_ref, lse_ref, dl_ref,
                     dq_ref, dk_ref, dv_ref, dk_carry, dv_carry, b)
        else:
            step(sl_ref, q_ref, kp_ref, kc_ref, vp_ref, vc_ref, do_ref, lse_ref, dl_ref,
                 dq_ref, dk_ref, dv_ref, None, None, b)

    def step(sl_ref, q_ref, kp_ref, kc_ref, vp_ref, vc_ref, do_ref, lse_ref, dl_ref,
             dq_ref, dk_ref, dv_ref, dk_carry, dv_carry, b):
        if carry:
            @pl.when(b == 0)
            def _():
                dk_carry[...] = jnp.zeros_like(dk_carry)
                dv_carry[...] = jnp.zeros_like(dv_carry)

        dist, valid = _attn_masks(b, bpc, dilation, None)
        q3 = _head_stack(lambda cols: q_ref[:, cols], heads)
        k3 = _head_stack(lambda cols: _key_tile(kp_ref, kc_ref, cols, bpc), heads)
        v3 = _head_stack(lambda cols: _key_tile(vp_ref, vc_ref, cols, bpc), heads)
        do3 = _head_stack(lambda cols: do_ref[:, cols], heads)
        lse3 = jnp.stack([lse_ref[:, h:h + 1] for h in range(heads)], axis=0)
        dl3 = jnp.stack([dl_ref[:, h:h + 1] for h in range(heads)], axis=0)
        s = lax.dot_general(q3, k3, BATCH_NT_DIMS, preferred_element_type=F32)
        p = jnp.exp(jnp.where(valid[None], s * scale - dist[None] * sl_ref[...], NEG_INF) - lse3)
        dp = lax.dot_general(do3, v3, BATCH_NT_DIMS, preferred_element_type=F32)
        ds = (p * (dp - dl3) * scale).astype(BF16)
        dq3 = lax.dot_general(ds, k3, BATCH_NN_DIMS, preferred_element_type=F32)
        dk3 = lax.dot_general(ds, q3, BATCH_TN_DIMS, preferred_element_type=F32)
        dv3 = lax.dot_general(p.astype(BF16), do3, BATCH_TN_DIMS, preferred_element_type=F32)
        for h in range(heads):
            cols = slice(h * HEAD_DIM, (h + 1) * HEAD_DIM)
            dq_ref[:, cols] = dq3[h].astype(BF16)
            if carry:
                dk_ref[:, cols] = (dk_carry[:, cols] + dk3[h, :ATTN_BLOCK]).astype(BF16)
                dv_ref[:, cols] = (dv_carry[:, cols] + dv3[h, :ATTN_BLOCK]).astype(BF16)
                dk_carry[:, cols] = dk3[h, ATTN_BLOCK:]
                dv_carry[:, cols] = dv3[h, ATTN_BLOCK:]
            else:
                dk_ref[:, cols] = dk3[h].astype(BF16)
                dv_ref[:, cols] = dv3[h].astype(BF16)

    kv_out = prev if carry else cur
    return pl.pallas_call(
        body, name=name, grid=(N_HEADS // heads, n_blocks + (1 if carry else 0)),
        in_specs=[pl.BlockSpec((heads, 1, 1), lambda hg, b: (hg, 0, 0)), cur, prev, cur, v_prev, v_cur,
                  cur, per_head, per_head],
        out_specs=[cur, kv_out, kv_out],
        out_shape=[jax.ShapeDtypeStruct((SEQ, D_MODEL), BF16)] * 3,
        scratch_shapes=[pltpu.VMEM((ATTN_BLOCK, width), F32)] * 2 if carry else [],
        compiler_params=_params("parallel", "arbitrary"),
    )(slopes.reshape(N_HEADS, 1, 1), q, k, k, proj, proj, do, lse, delta)


def _qknorm_bwd(proj, qw, kw, seg, dq, dk, dv, name):
    tm = 256

    def body(p_ref, qw_ref, kw_ref, seg_ref, dq_ref, dk_ref, dv_ref, dproj_ref, sums_ref):
        segv = seg_ref[...]
        sums = []
        for part, (w_ref, dn_ref) in enumerate(((qw_ref, dq_ref), (kw_ref, dk_ref))):
            raw = p_ref[:, part * D_MODEL:(part + 1) * D_MODEL].astype(F32)
            dn = dn_ref[...].astype(F32)
            r = _qk_rstd(raw, segv)
            gq = dn * w_ref[...]
            draw = r * gq - raw * (r * r * r) * (_segsum(raw * gq, segv) * (1.0 / HEAD_DIM))
            dproj_ref[:, part * D_MODEL:(part + 1) * D_MODEL] = draw.astype(BF16)
            sums.append(jnp.sum(dn * raw * r, axis=0, keepdims=True))
        dproj_ref[:, 2 * D_MODEL:] = dv_ref[...]

        @pl.when(pl.program_id(0) == 0)
        def _():
            sums_ref[...] = jnp.zeros_like(sums_ref)

        sums_ref[...] += jnp.concatenate(sums + [jnp.zeros((6, D_MODEL), F32)], axis=0)

    return pl.pallas_call(
        body, name=name, grid=(SEQ // tm,),
        in_specs=[_row_spec(tm, 3 * D_MODEL), _vec_spec(1, D_MODEL), _vec_spec(1, D_MODEL), _vec_spec(256, 256)]
        + [_row_spec(tm, D_MODEL)] * 3,
        out_specs=[_row_spec(tm, 3 * D_MODEL), _vec_spec(8, D_MODEL)],
        out_shape=[jax.ShapeDtypeStruct((SEQ, 3 * D_MODEL), BF16), jax.ShapeDtypeStruct((8, D_MODEL), F32)],
        compiler_params=_params("arbitrary"),
    )(proj, qw, kw, seg, dq, dk, dv)


def _to_classes(a, dilation):
    if dilation == 1:
        return a
    s, c = a.shape
    return a.reshape(s // dilation, dilation, c).transpose(1, 0, 2).reshape(s, c)


def _from_classes(a, dilation):
    if dilation == 1:
        return a
    s, c = a.shape
    return a.reshape(dilation, s // dilation, c).transpose(1, 0, 2).reshape(s, c)


def _cols_to_classes(a, dilation):
    if dilation == 1:
        return a
    r, s = a.shape
    return a.reshape(r, s // dilation, dilation).transpose(0, 2, 1).reshape(r, s)


B_TN = 512
B_GROUP_TILES = 3 * D_MODEL // B_TN
B_Z_TILE0 = 3 * B_GROUP_TILES
B_Z_TILES = D_MODEL // B_TN


def _local_step(x, target, mods, norm_g, conv_w, conv_b, ln_g, ln_b, q_norm, k_norm,
                weights_a, weights_b, forward_weights_b, send_grads_b, forward_grads_b, send_grads_a):
    row = lambda a, i: a[i:i + 1]
    shift0, scale0, gate0 = row(mods[0], 0), row(mods[0], 1), row(mods[0], 2)
    shift1, scale1, gate1 = row(mods[1], 0), row(mods[1], 1), row(mods[1], 2)
    g0, g1 = row(norm_g, 0), row(norm_g, 1)
    seg = _seg_matrix()
    slopes = jnp.exp2(-8.0 * jnp.arange(1, N_HEADS + 1, dtype=F32) / N_HEADS)
    qw = [jnp.tile(q_norm[g:g + 1], (1, N_HEADS)) for g in range(3)]
    kw = [jnp.tile(k_norm[g:g + 1], (1, N_HEADS)) for g in range(3)]

    h0, h0t = _normmod_fwd(x, g0, scale0, shift0, "prenorm0")
    wa_in, wa_out = weights_a(h0)
    ja, _, nsa = wa_in.shape
    proj_a = _mm(h0, wa_in, tn=nsa, tile0=0, n_tiles=ja, out_dtype=F32, name="a_in")
    u5, u5t, u2 = _conv_fwd(proj_a, conv_w, conv_b, ln_g, ln_b, "a_conv")
    token = forward_weights_b(u5)
    x1, y_a, h1t, h1c = _out_a(u5, wa_out, x, gate0 + token[0:1, 0:1], g1, scale1, shift1, "a_out")

    wb_in, wb_out = weights_b(x1)
    jb, _, nsb = wb_in.shape
    h1 = h1c[0]
    h1tc = [_cols_to_classes(h1t, d) for d in DILATIONS]
    z_b = _mm(h1, wb_in, tn=B_TN, tile0=B_Z_TILE0, n_tiles=B_Z_TILES, out_dtype=F32, name="b_in_z")
    proj_g, qkv, o_parts, lse_parts = [], [], [], []
    for g, d in enumerate(DILATIONS):
        pg = _mm(h1c[g], wb_in, tn=B_TN, tile0=g * B_GROUP_TILES, n_tiles=B_GROUP_TILES, out_dtype=BF16,
                 name=f"b_in_g{g}")
        qn, kn = _qknorm_fwd(pg, qw[g], kw[g], seg, f"b_qknorm_g{g}")
        og, lg = _attn_fwd(qn, kn, pg, slopes, d, f"b_attn_g{g}")
        proj_g.append(pg)
        qkv.append((qn, kn))
        o_parts.append(og if d == 1 else og.reshape(d, SEQ // d, D_MODEL))
        lse_parts.append(_from_classes(lg, d))
    sel = _head_selector()
    u_b, u_bt, o_b, lse_b = _merge_fwd(o_parts, lse_parts, z_b, sel, "b_merge")
    e, dy_b, sums_loss = _out_b_loss(u_b, wb_out, x1, gate1, target, "b_out_loss")

    dwb_out = _mm(u_bt, dy_b, tn=D_MODEL, tile0=0, n_tiles=1, out_dtype=BF16, name="b_dwout")
    du_b = _mm_nt_res(dy_b, wb_out, "b_dout")
    dz_b, delta_b, do_c = _merge_bwd(du_b, o_b, z_b, sel, "b_merge_bwd")
    dwb_in = _mm(h1t, dz_b, tn=B_TN, tile0=B_Z_TILE0, n_tiles=B_Z_TILES, out_dtype=BF16, name="b_dwin_z",
                 out3d=(jb, nsb))
    dh1_parts = [_mm_nt(dz_b, wb_in, tn=B_TN, tile0=B_Z_TILE0, n_tiles=B_Z_TILES, name="b_dh_z")]
    qk_sums = []
    for g, d in enumerate(DILATIONS):
        qn, kn = qkv[g]
        dq, dk, dv = _attn_bwd(qn, kn, proj_g[g], do_c[g], _to_classes(lse_b, d), _to_classes(delta_b, d),
                               slopes, d, f"b_attn_bwd_g{g}")
        dproj, sums_qk = _qknorm_bwd(proj_g[g], qw[g], kw[g], seg, dq, dk, dv, f"b_qknorm_bwd_g{g}")
        qk_sums.append(sums_qk)
        dwb_in = _mm(h1tc[g], dproj, tn=B_TN, tile0=g * B_GROUP_TILES, n_tiles=B_GROUP_TILES, out_dtype=BF16,
                     name=f"b_dwin_g{g}", out3d=(jb, nsb), prev=dwb_in)
        dh = _mm_nt(dproj, wb_in, tn=B_TN, tile0=g * B_GROUP_TILES, n_tiles=B_GROUP_TILES, name=f"b_dh_g{g}")
        dh1_parts.append(dh)
    token = send_grads_b(dwb_in, dwb_out)
    dx1, sums_n1 = _normmod_bwd(x1, g1, scale1 + token[0:1, 0:1], dh1_parts, e, "prenorm1_bwd",
                                part_dilations=(1,) + DILATIONS)
    token = forward_grads_b(dx1)

    dy_a, sums_ga = _dgate_dy(dx1, y_a, gate0 + token[0:1, 0:1], "a_dgate")
    dwa_out = _mm(u5t, dy_a, tn=D_MODEL, tile0=0, n_tiles=1, out_dtype=BF16, name="a_dwout")
    du5 = _mm_nt_res(dy_a, wa_out, "a_dout")
    du2, dz_a, sums_ln = _conv_bwd_pointwise(du5, proj_a, u2, ln_g, ln_b, "a_conv_bwd_pw")
    dproj_a, dconv_w = _conv_bwd_taps(du2, dz_a, proj_a, conv_w, "a_conv_bwd_taps")
    dwa_in = _mm(h0t, dproj_a, tn=nsa, tile0=0, n_tiles=ja, out_dtype=BF16, name="a_dwin", out3d=(ja, nsa))
    token = send_grads_a(dwa_in, dwa_out)
    dh0 = _mm_nt(dproj_a, wa_in, tn=nsa, tile0=0, n_tiles=ja, name="a_dh", after=token)
    grad_x, sums_n0 = _normmod_bwd(x, g0, scale0, [dh0], dx1, "prenorm0_bwd")

    small = dict(
        dnorm_g=jnp.concatenate([sums_n0[0:1], sums_n1[0:1]], axis=0),
        dmod0=jnp.concatenate([sums_n0[2:3], sums_n0[1:2], sums_ga[0:1]], axis=0),
        dmod1=jnp.concatenate([sums_n1[2:3], sums_n1[1:2], sums_loss[0:1]], axis=0),
        dln_g=sums_ln[0:1], dln_b=sums_ln[1:2], dconv_b=sums_ln[2:3],
        dconv_w=dconv_w[:CONV_WIDTH],
        dq_norm=jnp.concatenate([s[0:1] for s in qk_sums], axis=0),
        dk_norm=jnp.concatenate([s[1:2] for s in qk_sums], axis=0),
        loss_cols=sums_loss[1:2],
    )
    return grad_x, small


def _adamw(w, g, m, v, name):
    rows, cols = w.shape
    tr = rows if rows <= 128 else 128
    c1 = 1.0 / (1.0 - ADAM_B1 ** ADAM_STEP)
    c2 = 1.0 / (1.0 - ADAM_B2 ** ADAM_STEP)

    def body(w_ref, g_ref, m_ref, v_ref, d_ref, mo_ref, vo_ref):
        gv = g_ref[...]
        mn = ADAM_B1 * m_ref[...] + (1.0 - ADAM_B1) * gv
        vn = ADAM_B2 * v_ref[...] + (1.0 - ADAM_B2) * (gv * gv)
        mo_ref[...] = mn
        vo_ref[...] = vn
        d_ref[...] = -ADAM_LR * ((mn * c1) / (jnp.sqrt(vn * c2) + ADAM_EPS) + ADAM_WD * w_ref[...])

    spec = pl.BlockSpec((tr, cols), lambda i: (i, 0))
    return pl.pallas_call(
        body, name=name, grid=(rows // tr,), in_specs=[spec] * 4, out_specs=[spec] * 3,
        out_shape=[jax.ShapeDtypeStruct((rows, cols), F32)] * 3,
        compiler_params=_params("parallel"),
    )(w, g, m, v)


def _cast_into_slot(w, chip_idx, name):
    rows, cols = w.shape
    tr = 256

    def body(ch_ref, w_ref, o_ref):
        o_ref[...] = w_ref[...].astype(BF16)

    return pl.pallas_call(
        body, name=name,
        grid_spec=pltpu.PrefetchScalarGridSpec(
            num_scalar_prefetch=1, grid=(rows // tr,),
            in_specs=[pl.BlockSpec((tr, cols), lambda i, ch: (i, 0))],
            out_specs=pl.BlockSpec((None, tr, cols), lambda i, ch: (ch[0], i, 0))),
        out_shape=jax.ShapeDtypeStruct((N_CHIPS, rows, cols), BF16), compiler_params=_params("parallel"),
    )(chip_idx, w)


def _position():
    x, y, c = lax.axis_index("x"), lax.axis_index("y"), lax.axis_index("c")
    return x, y, c


def _xor_peer(x, y, c, k):
    return (x ^ ((k >> 2) & 1), y ^ ((k >> 1) & 1), c ^ (k & 1))


def _chip_peer(x, y, k):
    return (x ^ ((k >> 1) & 1), y ^ (k & 1))


def _ada_forward(c_row, ada_w, ada_b, conv_w):
    ns = ada_w.shape[2]
    cw = conv_w.shape[1]

    def body(c_ref, w_ref, b_ref, cv_ref, mod_ref, sc_ref, cvo_ref,
             c_all, mp, parts, cv_parts, send1, recv1, send2, recv2, send3, recv3):
        x, y, c = _position()
        me = 4 * x + 2 * y + c
        chip = 2 * x + y

        def c_copy(k):
            return pltpu.make_async_remote_copy(
                src_ref=c_all.at[me], dst_ref=c_all.at[me], send_sem=send1.at[k - 1], recv_sem=recv1.at[k - 1],
                device_id=_xor_peer(x, y, c, k), device_id_type=MESH)

        def cv_copy(k):
            px, py = _chip_peer(x, y, k)
            return pltpu.make_async_remote_copy(
                src_ref=cv_parts.at[chip], dst_ref=cv_parts.at[chip], send_sem=send3.at[k - 1],
                recv_sem=recv3.at[k - 1], device_id=(px, py, c), device_id_type=MESH)

        c_all[me] = c_ref[...]
        cv_parts[chip] = cv_ref[...]
        for k in range(1, N_DEV):
            c_copy(k).start()
        for k in range(1, N_CHIPS):
            cv_copy(k).start()
        for k in range(1, N_DEV):
            c_copy(k).wait_recv()
        cv = jnp.concatenate([c_all[i] for i in range(N_DEV)], axis=0)
        sc = cv * _sigmoid(cv)
        sc_ref[...] = sc
        for l in range(2):
            res = jnp.dot(sc, w_ref[l], preferred_element_type=F32, precision=lax.Precision.HIGHEST)
            for i in range(N_DEV):
                mp[i, l:l + 1, :] = res[i:i + 1, :]

        def mod_copy(k):
            px, py = _chip_peer(x, y, k)
            return pltpu.make_async_remote_copy(
                src_ref=mp.at[4 * px + 2 * py + c], dst_ref=parts.at[chip], send_sem=send2.at[k - 1],
                recv_sem=recv2.at[k - 1], device_id=(px, py, c), device_id_type=MESH)

        for k in range(1, N_CHIPS):
            mod_copy(k).start()
        parts[chip] = mp[me]
        for k in range(1, N_CHIPS):
            mod_copy(k).wait_recv()
            cv_copy(k).wait_recv()
        mod_ref[...] = jnp.concatenate([parts[j] for j in range(N_CHIPS)], axis=1) + b_ref[...]
        cvo_ref[...] = jnp.concatenate([cv_parts[j] for j in range(N_CHIPS)], axis=1)
        for k in range(1, N_DEV):
            c_copy(k).wait_send()
        for k in range(1, N_CHIPS):
            mod_copy(k).wait_send()
            cv_copy(k).wait_send()

    vm = pl.BlockSpec(memory_space=pltpu.VMEM)
    return pl.pallas_call(
        body, name="ada_forward",
        in_specs=[vm] * 4, out_specs=[vm] * 3,
        out_shape=[jax.ShapeDtypeStruct((2, 3 * D_MODEL), F32), jax.ShapeDtypeStruct((N_DEV, D_MODEL), F32),
                   jax.ShapeDtypeStruct((CONV_WIDTH, N_CHIPS * cw), F32)],
        scratch_shapes=[pltpu.VMEM((N_DEV, 1, D_MODEL), F32), pltpu.VMEM((N_DEV, 2, ns), F32),
                        pltpu.VMEM((N_CHIPS, 2, ns), F32), pltpu.VMEM((N_CHIPS, CONV_WIDTH, cw), F32),
                        pltpu.SemaphoreType.DMA((N_DEV - 1,)), pltpu.SemaphoreType.DMA((N_DEV - 1,)),
                        pltpu.SemaphoreType.DMA((N_CHIPS - 1,)), pltpu.SemaphoreType.DMA((N_CHIPS - 1,)),
                        pltpu.SemaphoreType.DMA((N_CHIPS - 1,)), pltpu.SemaphoreType.DMA((N_CHIPS - 1,))],
        compiler_params=pltpu.CompilerParams(vmem_limit_bytes=VMEM_LIMIT_BYTES),
    )(c_row, ada_w, ada_b, conv_w)


HBM_SPEC = pl.BlockSpec(memory_space=pltpu.HBM)
ANY_SPEC = pl.BlockSpec(memory_space=pl.ANY)
SEM_SPEC = pl.BlockSpec(memory_space=pltpu.SEMAPHORE)
SPLIT_PARAMS = dict(compiler_params=pltpu.CompilerParams(has_side_effects=pltpu.SideEffectType.DATAFLOW_SIDE_EFFECTING))
TOKEN = jax.ShapeDtypeStruct((8, 128), F32)


def _hbm(arrays):
    return [pltpu.with_memory_space_constraint(a, pltpu.HBM) for a in arrays]


def _hbm_like(arrays):
    return [pltpu.HBM(a.shape, a.dtype) for a in arrays]


def _gather_start(lands, after, name):
    n = len(lands)

    def body(*refs):
        ins = refs[:n]
        send, recv = refs[n + 1], refs[n + 2]
        x, y, c = _position()
        chip = 2 * x + y
        for t in range(n):
            rh = ins[t].shape[1] // 2
            for k in range(1, N_CHIPS):
                px, py = _chip_peer(x, y, k)
                block = ins[t].at[chip, pl.ds(c * rh, rh)]
                pltpu.make_async_remote_copy(
                    src_ref=block, dst_ref=block, send_sem=send.at[3 * t + k - 1], recv_sem=recv.at[3 * t + k - 1],
                    device_id=(px, py, c), device_id_type=MESH).start()
        refs[-1][...] = jnp.zeros(TOKEN.shape, F32)

    res = pl.pallas_call(
        body, name=name, in_specs=[HBM_SPEC] * n + [ANY_SPEC],
        out_specs=(SEM_SPEC, SEM_SPEC, *[HBM_SPEC] * n, pl.BlockSpec(memory_space=pltpu.VMEM)),
        out_shape=(pltpu.SemaphoreType.DMA((3 * n,)), pltpu.SemaphoreType.DMA((3 * n,)), *_hbm_like(lands), TOKEN),
        input_output_aliases={t: 2 + t for t in range(n)}, **SPLIT_PARAMS,
    )(*_hbm(lands), after)
    return res[0], res[1], list(res[2:2 + n]), res[-1]


def _gather_forward(send, recv, lands, after, name):
    n = len(lands)

    def body(*refs):
        ins = refs[:n]
        send1, recv1 = refs[n], refs[n + 1]
        send2, recv2 = refs[n + 3], refs[n + 4]
        x, y, c = _position()
        chip = 2 * x + y
        for t in range(n):
            rh = ins[t].shape[1] // 2
            half = pl.ds(c * rh, rh)
            for k in range(1, N_CHIPS):
                px, py = _chip_peer(x, y, k)
                s = 3 * t + k - 1
                got = ins[t].at[2 * px + py, half]
                cp = pltpu.make_async_remote_copy(
                    src_ref=ins[t].at[chip, half], dst_ref=got, send_sem=send1.at[s], recv_sem=recv1.at[s],
                    device_id=(px, py, c), device_id_type=MESH)
                cp.wait_send()
                cp.wait_recv()
                pltpu.make_async_remote_copy(
                    src_ref=got, dst_ref=got, send_sem=send2.at[s], recv_sem=recv2.at[s],
                    device_id=(x, y, 1 - c), device_id_type=MESH).start()
        refs[-1][...] = jnp.zeros(TOKEN.shape, F32)

    res = pl.pallas_call(
        body, name=name, in_specs=[HBM_SPEC] * n + [SEM_SPEC, SEM_SPEC, ANY_SPEC],
        out_specs=(SEM_SPEC, SEM_SPEC, *[HBM_SPEC] * n, pl.BlockSpec(memory_space=pltpu.VMEM)),
        out_shape=(pltpu.SemaphoreType.DMA((3 * n,)), pltpu.SemaphoreType.DMA((3 * n,)), *_hbm_like(lands), TOKEN),
        input_output_aliases={t: 2 + t for t in range(n)}, **SPLIT_PARAMS,
    )(*lands, send, recv, after)
    return res[0], res[1], list(res[2:2 + n]), res[-1]


def _gather_wait(send, recv, lands, after, name):
    n = len(lands)

    def body(*refs):
        ins = refs[:n]
        send_ref, recv_ref = refs[n], refs[n + 1]
        x, y, c = _position()
        for t in range(n):
            rh = ins[t].shape[1] // 2
            for k in range(1, N_CHIPS):
                px, py = _chip_peer(x, y, k)
                cp = pltpu.make_async_remote_copy(
                    src_ref=ins[t].at[2 * px + py, pl.ds(c * rh, rh)],
                    dst_ref=ins[t].at[2 * px + py, pl.ds((1 - c) * rh, rh)], send_sem=send_ref.at[3 * t + k - 1],
                    recv_sem=recv_ref.at[3 * t + k - 1], device_id=(x, y, 1 - c), device_id_type=MESH)
                cp.wait_send()
                cp.wait_recv()

    res = pl.pallas_call(
        body, name=name, in_specs=[HBM_SPEC] * n + [SEM_SPEC, SEM_SPEC, ANY_SPEC], out_specs=[HBM_SPEC] * n,
        out_shape=_hbm_like(lands), input_output_aliases={t: t for t in range(n)}, **SPLIT_PARAMS,
    )(*lands, send, recv, after)
    return list(res)


def _reduce_start(grads, after, name):
    n = len(grads)
    lands = [lax.empty((N_DEV, g.shape[1] // 2, g.shape[2]), BF16) for g in grads]

    def body(*refs):
        gs, ls = refs[:n], refs[n:2 * n]
        send, recv = refs[2 * n + 1], refs[2 * n + 2]
        x, y, c = _position()
        me = 4 * x + 2 * y + c
        for t in range(n):
            rh = gs[t].shape[1] // 2
            for k in range(1, N_DEV):
                px, py, pc = _xor_peer(x, y, c, k)
                pltpu.make_async_remote_copy(
                    src_ref=gs[t].at[2 * px + py, pl.ds(pc * rh, rh)], dst_ref=ls[t].at[me],
                    send_sem=send.at[7 * t + k - 1], recv_sem=recv.at[7 * t + k - 1],
                    device_id=(px, py, pc), device_id_type=MESH).start()
        refs[-1][...] = jnp.zeros(TOKEN.shape, F32)

    res = pl.pallas_call(
        body, name=name, in_specs=[HBM_SPEC] * (2 * n) + [ANY_SPEC],
        out_specs=(SEM_SPEC, SEM_SPEC, *[HBM_SPEC] * (2 * n), pl.BlockSpec(memory_space=pltpu.VMEM)),
        out_shape=(pltpu.SemaphoreType.DMA((7 * n,)), pltpu.SemaphoreType.DMA((7 * n,)),
                   *_hbm_like(grads), *_hbm_like(lands), TOKEN),
        input_output_aliases={t: 2 + t for t in range(2 * n)}, **SPLIT_PARAMS,
    )(*_hbm(grads), *_hbm(lands), after)
    return res[0], res[1], list(res[2:2 + n]), list(res[2 + n:2 + 2 * n]), res[-1]


def _reduce_wait(send, recv, grads, lands, after, name):
    n = len(grads)

    def body(*refs):
        gs, ls = refs[:n], refs[n:2 * n]
        send_ref, recv_ref = refs[2 * n], refs[2 * n + 1]
        x, y, c = _position()
        for t in range(n):
            rh = gs[t].shape[1] // 2
            for k in range(1, N_DEV):
                px, py, pc = _xor_peer(x, y, c, k)
                cp = pltpu.make_async_remote_copy(
                    src_ref=gs[t].at[2 * px + py, pl.ds(pc * rh, rh)], dst_ref=ls[t].at[4 * px + 2 * py + pc],
                    send_sem=send_ref.at[7 * t + k - 1], recv_sem=recv_ref.at[7 * t + k - 1],
                    device_id=(px, py, pc), device_id_type=MESH)
                cp.wait_send()
                cp.wait_recv()

    res = pl.pallas_call(
        body, name=name, in_specs=[HBM_SPEC] * (2 * n) + [SEM_SPEC, SEM_SPEC, ANY_SPEC], out_specs=[HBM_SPEC] * (2 * n),
        out_shape=_hbm_like(grads) + _hbm_like(lands), input_output_aliases={t: t for t in range(2 * n)}, **SPLIT_PARAMS,
    )(*grads, *lands, send, recv, after)
    return list(res[:n]), list(res[n:])


def _sum_devices(land, grad, dev_idx, name):
    _, rh, cols = land.shape
    tr = 128
    nb = rh // tr

    def body(idx_ref, l_ref, g_ref, o_ref):
        me = idx_ref[0]
        acc = jnp.where(me == 0, g_ref[...], l_ref[0]).astype(F32)
        for d in range(1, N_DEV):
            acc = acc + jnp.where(me == d, g_ref[...], l_ref[d]).astype(F32)
        o_ref[...] = acc

    return pl.pallas_call(
        body, name=name,
        grid_spec=pltpu.PrefetchScalarGridSpec(
            num_scalar_prefetch=1, grid=(nb,),
            in_specs=[pl.BlockSpec((N_DEV, tr, cols), lambda i, idx: (0, i, 0)),
                      pl.BlockSpec((None, tr, cols), lambda i, idx: (idx[1], idx[2] * nb + i, 0))],
            out_specs=pl.BlockSpec((tr, cols), lambda i, idx: (idx[2] * nb + i, 0))),
        out_shape=jax.ShapeDtypeStruct((2 * rh, cols), F32), compiler_params=_params("parallel"),
    )(dev_idx, land, grad)


def _split_start(name, arrays, n_sems, after, issue):
    m = len(arrays)

    def body(*refs):
        issue(refs[:m], refs[m + 1], refs[m + 2])
        refs[-1][...] = jnp.zeros(TOKEN.shape, F32)

    res = pl.pallas_call(
        body, name=name, in_specs=[HBM_SPEC] * m + [ANY_SPEC],
        out_specs=(SEM_SPEC, SEM_SPEC, *[HBM_SPEC] * m, pl.BlockSpec(memory_space=pltpu.VMEM)),
        out_shape=(pltpu.SemaphoreType.DMA((n_sems,)), pltpu.SemaphoreType.DMA((n_sems,)), *_hbm_like(arrays), TOKEN),
        input_output_aliases={t: 2 + t for t in range(m)}, **SPLIT_PARAMS,
    )(*_hbm(arrays), after)
    return res[0], res[1], list(res[2:2 + m]), res[-1]


def _split_wait(name, arrays, send, recv, after, await_all):
    m = len(arrays)

    def body(*refs):
        await_all(refs[:m], refs[m], refs[m + 1])

    res = pl.pallas_call(
        body, name=name, in_specs=[HBM_SPEC] * m + [SEM_SPEC, SEM_SPEC, ANY_SPEC], out_specs=[HBM_SPEC] * m,
        out_shape=_hbm_like(arrays), input_output_aliases={t: t for t in range(m)}, **SPLIT_PARAMS,
    )(*arrays, send, recv, after)
    return list(res)


def _sibling_copies(refs, send, recv, n):
    x, y, c = _position()
    cps = []
    for t in range(n):
        rh = refs[t].shape[1] // 2
        cps.append(pltpu.make_async_remote_copy(
            src_ref=refs[t].at[pl.ds(0, N_CHIPS), pl.ds((1 - c) * rh, rh)], dst_ref=refs[n + t],
            send_sem=send.at[t], recv_sem=recv.at[t], device_id=(x, y, 1 - c), device_id_type=MESH))
    return cps


def _reduce_sibling_start(grads, after, name):
    n = len(grads)
    lands = [lax.empty((N_CHIPS, g.shape[1] // 2, g.shape[2]), BF16) for g in grads]

    def issue(refs, send, recv):
        for cp in _sibling_copies(refs, send, recv, n):
            cp.start()

    return _split_start(name, list(grads) + lands, n, after, issue)


def _reduce_sibling_wait(send, recv, arrays, after, name):
    n = len(arrays) // 2

    def await_all(refs, send_ref, recv_ref):
        for cp in _sibling_copies(refs, send_ref, recv_ref, n):
            cp.wait_send()
            cp.wait_recv()

    res = _split_wait(name, arrays, send, recv, after, await_all)
    return res[:n], res[n:]


def _add_sibling_half(grad, got, dev_idx, name):
    j, r, cols = grad.shape
    rh = r // 2
    tr = 128
    nb = rh // tr

    def body(idx_ref, g_ref, got_ref, out_ref):
        out_ref[...] = (g_ref[...].astype(F32) + got_ref[...].astype(F32)).astype(BF16)

    return pl.pallas_call(
        body, name=name,
        grid_spec=pltpu.PrefetchScalarGridSpec(
            num_scalar_prefetch=1, grid=(j, nb),
            in_specs=[pl.BlockSpec((None, tr, cols), lambda jj, i, idx: (jj, idx[2] * nb + i, 0)),
                      pl.BlockSpec((None, tr, cols), lambda jj, i, idx: (jj, i, 0))],
            out_specs=pl.BlockSpec((None, tr, cols), lambda jj, i, idx: (jj, i, 0))),
        out_shape=jax.ShapeDtypeStruct((j, rh, cols), BF16),
        compiler_params=_params("parallel", "parallel"),
    )(dev_idx, grad, got)


def _chip_copies(refs, send, recv, n, receiving):
    x, y, c = _position()
    chip = 2 * x + y
    cps = []
    for t in range(n):
        for k in range(1, N_CHIPS):
            px, py = _chip_peer(x, y, k)
            cps.append(pltpu.make_async_remote_copy(
                src_ref=refs[t].at[2 * px + py], dst_ref=refs[n + t].at[2 * px + py if receiving else chip],
                send_sem=send.at[3 * t + k - 1], recv_sem=recv.at[3 * t + k - 1],
                device_id=(px, py, c), device_id_type=MESH))
    return cps


def _reduce_chips_start(partials, after, name):
    n = len(partials)
    lands = [lax.empty(p.shape, BF16) for p in partials]

    def issue(refs, send, recv):
        for cp in _chip_copies(refs, send, recv, n, False):
            cp.start()

    return _split_start(name, list(partials) + lands, 3 * n, after, issue)


def _reduce_chips_wait(send, recv, arrays, after, name):
    n = len(arrays) // 2

    def await_all(refs, send_ref, recv_ref):
        for cp in _chip_copies(refs, send_ref, recv_ref, n, True):
            cp.wait_send()
            cp.wait_recv()

    res = _split_wait(name, arrays, send, recv, after, await_all)
    return res[:n], res[n:]


def _sum_partials(land, partial, dev_idx, name):
    _, rh, cols = land.shape
    tr = 128
    nb = rh // tr

    def body(idx_ref, l_ref, p_ref, o_ref):
        chip = idx_ref[1]
        acc = jnp.where(chip == 0, p_ref[...], l_ref[0]).astype(F32)
        for s in range(1, N_CHIPS):
            acc = acc + jnp.where(chip == s, p_ref[...], l_ref[s]).astype(F32)
        o_ref[...] = acc

    return pl.pallas_call(
        body, name=name,
        grid_spec=pltpu.PrefetchScalarGridSpec(
            num_scalar_prefetch=1, grid=(nb,),
            in_specs=[pl.BlockSpec((N_CHIPS, tr, cols), lambda i, idx: (0, i, 0)),
                      pl.BlockSpec((None, tr, cols), lambda i, idx: (idx[1], i, 0))],
            out_specs=pl.BlockSpec((tr, cols), lambda i, idx: (idx[2] * nb + i, 0))),
        out_shape=jax.ShapeDtypeStruct((2 * rh, cols), F32), compiler_params=_params("parallel"),
    )(dev_idx, land, partial)


def _share_halves(totals):
    n = len(totals)

    def body(*refs):
        ins, outs = refs[:n], refs[n:2 * n]
        send, recv = refs[2 * n:]
        x, y, c = _position()
        cps = []
        for t in range(n):
            rh = ins[t].shape[0] // 2
            mine = pl.ds(c * rh, rh)
            cp = pltpu.make_async_remote_copy(
                src_ref=ins[t].at[mine], dst_ref=outs[t].at[mine], send_sem=send.at[t], recv_sem=recv.at[t],
                device_id=(x, y, 1 - c), device_id_type=MESH)
            cp.start()
            cps.append(cp)
        for cp in cps:
            cp.wait()

    return pl.pallas_call(
        body, name="reduce_share_" + "_".join(str(t.shape[1]) for t in totals), in_specs=[ANY_SPEC] * n,
        out_specs=[ANY_SPEC] * n, out_shape=[jax.ShapeDtypeStruct(t.shape, F32) for t in totals],
        input_output_aliases={t: t for t in range(n)},
        scratch_shapes=[pltpu.SemaphoreType.DMA((n,)), pltpu.SemaphoreType.DMA((n,))],
    )(*totals)


def _exchange_halves(grads):
    n = len(grads)
    hbm = pl.BlockSpec(memory_space=pl.ANY)

    def body(*refs):
        ins, outs = refs[:n], refs[n:2 * n]
        send, recv = refs[2 * n:]
        x, y, c = _position()
        cps = []
        for t in range(n):
            rh = ins[t].shape[1] // 2
            cp = pltpu.make_async_remote_copy(
                src_ref=ins[t].at[pl.ds(0, N_CHIPS), pl.ds((1 - c) * rh, rh)], dst_ref=outs[t], send_sem=send.at[t],
                recv_sem=recv.at[t], device_id=(x, y, 1 - c), device_id_type=MESH)
            cp.start()
            cps.append(cp)
        for cp in cps:
            cp.wait()

    return pl.pallas_call(
        body, name="reduce_exchange_halves", in_specs=[hbm] * n, out_specs=[hbm] * n,
        out_shape=[jax.ShapeDtypeStruct((g.shape[0], g.shape[1] // 2, g.shape[2]), BF16) for g in grads],
        scratch_shapes=[pltpu.SemaphoreType.DMA((n,)), pltpu.SemaphoreType.DMA((n,))],
    )(*grads)


def _add_halves(grad, got, c_idx, name):
    j, r, cols = grad.shape
    rh = r // 2
    tr = 128
    nb = rh // tr

    def body(c_ref, g_ref, o_ref_in, out_ref):
        out_ref[...] = (g_ref[...].astype(F32) + o_ref_in[...].astype(F32)).astype(BF16)

    return pl.pallas_call(
        body, name=name,
        grid_spec=pltpu.PrefetchScalarGridSpec(
            num_scalar_prefetch=1, grid=(j, nb),
            in_specs=[pl.BlockSpec((None, tr, cols), lambda jj, i, c_ref: (jj, c_ref[0] * nb + i, 0)),
                      pl.BlockSpec((None, tr, cols), lambda jj, i, c_ref: (jj, i, 0))],
            out_specs=pl.BlockSpec((None, tr, cols), lambda jj, i, c_ref: (jj, i, 0))),
        out_shape=jax.ShapeDtypeStruct((j, rh, cols), BF16),
        compiler_params=_params("parallel", "parallel"),
    )(c_idx, grad, got)


def _scatter_partials(partials):
    n = len(partials)
    hbm = pl.BlockSpec(memory_space=pl.ANY)

    def body(*refs):
        ins, outs = refs[:n], refs[n:2 * n]
        send, recv, local = refs[2 * n:]
        x, y, c = _position()
        chip = 2 * x + y
        cps, lcs = [], []
        for t in range(n):
            lc = pltpu.make_async_copy(ins[t].at[chip], outs[t].at[chip], local.at[t])
            lc.start()
            lcs.append(lc)
            for k in range(1, N_CHIPS):
                px, py = _chip_peer(x, y, k)
                s = 3 * t + k - 1
                cp = pltpu.make_async_remote_copy(
                    src_ref=ins[t].at[2 * px + py], dst_ref=outs[t].at[chip], send_sem=send.at[s],
                    recv_sem=recv.at[s], device_id=(px, py, c), device_id_type=MESH)
                cp.start()
                cps.append(cp)
        for cp in cps:
            cp.wait()
        for lc in lcs:
            lc.wait()

    return pl.pallas_call(
        body, name="reduce_scatter_partials", in_specs=[hbm] * n, out_specs=[hbm] * n,
        out_shape=[jax.ShapeDtypeStruct(p.shape, BF16) for p in partials],
        scratch_shapes=[pltpu.SemaphoreType.DMA((3 * n,)), pltpu.SemaphoreType.DMA((3 * n,)),
                        pltpu.SemaphoreType.DMA((n,))],
    )(*partials)


def _sum_chips(parts, name):
    j, rh, cols = parts.shape
    tr = 128

    def body(p_ref, o_ref):
        acc = p_ref[0].astype(F32)
        for s in range(1, j):
            acc = acc + p_ref[s].astype(F32)
        o_ref[...] = acc

    return pl.pallas_call(
        body, name=name, grid=(rh // tr,),
        in_specs=[pl.BlockSpec((j, tr, cols), lambda i: (0, i, 0))],
        out_specs=pl.BlockSpec((tr, cols), lambda i: (i, 0)),
        out_shape=jax.ShapeDtypeStruct((rh, cols), F32),
        compiler_params=_params("parallel"),
    )(parts)


def _share_totals(halves):
    n = len(halves)
    hbm = pl.BlockSpec(memory_space=pl.ANY)

    def body(*refs):
        ins, outs = refs[:n], refs[n:2 * n]
        send, recv, local = refs[2 * n:]
        x, y, c = _position()
        cps, lcs = [], []
        for t in range(n):
            rh = ins[t].shape[0]
            mine = outs[t].at[pl.ds(c * rh, rh)]
            lc = pltpu.make_async_copy(ins[t], mine, local.at[t])
            lc.start()
            lcs.append(lc)
            cp = pltpu.make_async_remote_copy(
                src_ref=ins[t], dst_ref=mine, send_sem=send.at[t], recv_sem=recv.at[t],
                device_id=(x, y, 1 - c), device_id_type=MESH)
            cp.start()
            cps.append(cp)
        for cp in cps:
            cp.wait()
        for lc in lcs:
            lc.wait()

    return pl.pallas_call(
        body, name="reduce_share_totals", in_specs=[hbm] * n, out_specs=[hbm] * n,
        out_shape=[jax.ShapeDtypeStruct((2 * h.shape[0], h.shape[1]), F32) for h in halves],
        scratch_shapes=[pltpu.SemaphoreType.DMA((n,)), pltpu.SemaphoreType.DMA((n,)),
                        pltpu.SemaphoreType.DMA((n,))],
    )(*halves)


SMALL_ROWS = 56


def _small_copies(refs, send, recv, receiving):
    x, y, c = _position()
    me = 4 * x + 2 * y + c
    cps = []
    for k in range(1, N_DEV):
        px, py, pc = _xor_peer(x, y, c, k)
        cps.append(pltpu.make_async_remote_copy(
            src_ref=refs[0], dst_ref=refs[1].at[4 * px + 2 * py + pc if receiving else me],
            send_sem=send.at[k - 1], recv_sem=recv.at[k - 1], device_id=(px, py, pc), device_id_type=MESH))
    return cps


def _small_gather_start(packed, after):
    land = lax.empty((N_DEV,) + packed.shape, F32)

    def issue(refs, send, recv):
        for cp in _small_copies(refs, send, recv, False):
            cp.start()

    return _split_start("small_gather_start", [packed, land], N_DEV - 1, after, issue)


def _small_gather_wait(send, recv, arrays, after):
    def await_all(refs, send_ref, recv_ref):
        for cp in _small_copies(refs, send_ref, recv_ref, True):
            cp.wait_send()
            cp.wait_recv()

    return _split_wait("small_gather_wait", arrays, send, recv, after, await_all)


def _reduce_small(packed, land, silu_c):
    ns = 3 * D_MODEL // N_CHIPS

    def body(p_ref, land_ref, sc_ref, tot_ref, gw_ref, loss_ref, qk_ref, allp):
        x, y, c = _position()
        me = 4 * x + 2 * y + c
        chip = 2 * x + y
        for i in range(N_DEV):
            allp[i] = jnp.where(me == i, p_ref[...], land_ref[i])
        tot = allp[0]
        for i in range(1, N_DEV):
            tot = tot + allp[i]
        tot_ref[...] = tot
        loss_ref[...] = jnp.sum(tot[11:12, :], axis=1, keepdims=True) * (0.5 / D_MODEL)
        fold = tot[5:11, 0:HEAD_DIM]
        for h in range(1, N_HEADS):
            fold = fold + tot[5:11, h * HEAD_DIM:(h + 1) * HEAD_DIM]
        qk_ref[...] = jnp.concatenate([fold, jnp.zeros((2, HEAD_DIM), F32)], axis=0)
        sct = sc_ref[...].T
        rc = 64
        for l in range(2):
            dms = [allp[i, pl.ds(12 + 4 * l + chip, 1), :][:, :ns] for i in range(N_DEV)]
            for r0 in range(0, D_MODEL, rc):
                acc = sct[r0:r0 + rc, 0:1] * dms[0]
                for i in range(1, N_DEV):
                    acc = acc + sct[r0:r0 + rc, i:i + 1] * dms[i]
                gw_ref[l, r0:r0 + rc, :] = acc

    vm = pl.BlockSpec(memory_space=pltpu.VMEM)
    return pl.pallas_call(
        body, name="reduce_small", in_specs=[vm, vm, vm], out_specs=[vm] * 4,
        out_shape=[jax.ShapeDtypeStruct((SMALL_ROWS, D_MODEL), F32), jax.ShapeDtypeStruct((2, D_MODEL, ns), F32),
                   jax.ShapeDtypeStruct((1, 1), F32), jax.ShapeDtypeStruct((8, HEAD_DIM), F32)],
        scratch_shapes=[pltpu.VMEM((N_DEV, SMALL_ROWS, D_MODEL), F32)],
        compiler_params=pltpu.CompilerParams(vmem_limit_bytes=VMEM_LIMIT_BYTES),
    )(packed, land, silu_c)


def _reduce_big(grads, c_idx):
    names = list(grads)
    got = _exchange_halves([grads[k] for k in names])
    partials = [_add_halves(grads[k], got[i], c_idx, f"reduce_add_{k}") for i, k in enumerate(names)]
    parts = _scatter_partials(partials)
    halves = [_sum_chips(parts[i], f"reduce_sum_{k}") for i, k in enumerate(names)]
    totals = _share_totals(halves)
    return dict(zip(names, totals))


def kernel(x, c, norm_g, ada_w, ada_b, a_w_in, a_conv_w, a_conv_b, a_ln_g, a_ln_b, a_w_out, b_w_in, b_q_norm, b_k_norm, b_w_out, loss_target, m_norm_g, m_ada_w, m_ada_b, m_a_w_in, m_a_conv_w, m_a_conv_b, m_a_ln_g, m_a_ln_b, m_a_w_out, m_b_w_in, m_b_q_norm, m_b_k_norm, m_b_w_out, v_norm_g, v_ada_w, v_ada_b, v_a_w_in, v_a_conv_w, v_a_conv_b, v_a_ln_g, v_a_ln_b, v_a_w_out, v_b_w_in, v_b_q_norm, v_b_k_norm, v_b_w_out):
    chip = 2 * lax.axis_index("x") + lax.axis_index("y")
    core = lax.axis_index("c")
    chip_idx = chip.astype(jnp.int32).reshape(1)
    dev_idx = jnp.stack([2 * chip + core, chip, core]).astype(jnp.int32)

    mods, silu_c, conv_w_full = _ada_forward(c, ada_w, ada_b, a_conv_w[0])
    lands_a = [_cast_into_slot(a_w_in[0], chip_idx, "cast_a_w_in"), _cast_into_slot(a_w_out[0], chip_idx, "cast_a_w_out")]
    send_a, recv_a, lands_a, token_a = _gather_start(lands_a, mods, "gather_start_a")
    lands_b = [_cast_into_slot(b_w_in[0], chip_idx, "cast_b_w_in"), _cast_into_slot(b_w_out[0], chip_idx, "cast_b_w_out")]
    send_b, recv_b, lands_b, token_b = _gather_start(lands_b, token_a, "gather_start_b")
    mods = mods + token_b[0:2, 0:1]

    def weights_a(after):
        send, recv, lands, _ = _gather_forward(send_a, recv_a, lands_a, after, "gather_forward_a")
        w_in, w_out = _gather_wait(send, recv, lands, after, "gather_wait_a")
        return w_in, w_out.reshape(D_MODEL, D_MODEL)

    forwarded_b = []

    def weights_b(after):
        send, recv, lands, _ = forwarded_b
        w_in, w_out = _gather_wait(send, recv, lands, after, "gather_wait_b")
        return w_in, w_out.reshape(D_MODEL, D_MODEL)

    def forward_weights_b(after):
        forwarded_b.extend(_gather_forward(send_b, recv_b, lands_b, after, "gather_forward_b"))
        return forwarded_b[3]

    stage1, stage2 = {}, {}

    def send_grads(tag, dw_in, dw_out):
        grads = [dw_in, dw_out.reshape(N_CHIPS, D_MODEL // N_CHIPS, D_MODEL)]
        send, recv, arrays, token = _reduce_sibling_start(grads, dw_out, f"reduce_d2d_start_{tag}")
        stage1[tag] = (send, recv, arrays)
        return token

    def forward_grads(tag, after):
        send, recv, arrays = stage1[tag]
        grads, got = _reduce_sibling_wait(send, recv, arrays, after, f"reduce_d2d_wait_{tag}")
        partials = [_add_sibling_half(grads[i], got[i], dev_idx, f"reduce_add_{tag}_{i}") for i in range(2)]
        send, recv, arrays, token = _reduce_chips_start(partials, partials[1], f"reduce_ici_start_{tag}")
        stage2[tag] = (send, recv, arrays)
        return token

    def finish_grads(tag, after):
        send, recv, arrays = stage2[tag]
        partials, lands = _reduce_chips_wait(send, recv, arrays, after, f"reduce_ici_wait_{tag}")
        totals = [_sum_partials(lands[i], partials[i], dev_idx, f"reduce_sum_{tag}_{i}") for i in range(2)]
        return _share_halves(totals)

    grad_x, small = _local_step(
        x[0], loss_target[0], mods.reshape(2, 3, D_MODEL), norm_g, conv_w_full, a_conv_b, a_ln_g[0:1],
        a_ln_b[0:1], b_q_norm[0], b_k_norm[0], weights_a, weights_b, forward_weights_b,
        functools.partial(send_grads, "b"), functools.partial(forward_grads, "b"), functools.partial(send_grads, "a"))

    ns = 3 * D_MODEL // N_CHIPS
    pad_mod = lambda dm: jnp.pad(dm.reshape(N_CHIPS, ns), ((0, 0), (0, D_MODEL - ns)))
    packed = jnp.concatenate([
        small["dnorm_g"], small["dconv_b"], small["dln_g"], small["dln_b"], small["dq_norm"], small["dk_norm"],
        small["loss_cols"], pad_mod(small["dmod0"]), pad_mod(small["dmod1"]), small["dconv_w"],
        jnp.zeros((SMALL_ROWS - 20 - CONV_WIDTH, D_MODEL), F32)], axis=0)
    send_s, recv_s, small_arrays, token_s = _small_gather_start(packed, packed)

    given = dict(norm_g=(norm_g, m_norm_g, v_norm_g), ada_w=(ada_w, m_ada_w, v_ada_w), ada_b=(ada_b, m_ada_b, v_ada_b),
                 a_w_in=(a_w_in, m_a_w_in, v_a_w_in), a_conv_w=(a_conv_w, m_a_conv_w, v_a_conv_w),
                 a_conv_b=(a_conv_b, m_a_conv_b, v_a_conv_b), a_ln_g=(a_ln_g, m_a_ln_g, v_a_ln_g),
                 a_ln_b=(a_ln_b, m_a_ln_b, v_a_ln_b), a_w_out=(a_w_out, m_a_w_out, v_a_w_out),
                 b_w_in=(b_w_in, m_b_w_in, v_b_w_in), b_q_norm=(b_q_norm, m_b_q_norm, v_b_q_norm),
                 b_k_norm=(b_k_norm, m_b_k_norm, v_b_k_norm), b_w_out=(b_w_out, m_b_w_out, v_b_w_out))
    order = ["norm_g", "ada_w", "ada_b", "a_w_in", "a_conv_w", "a_conv_b", "a_ln_g", "a_ln_b", "a_w_out", "b_w_in",
             "b_q_norm", "b_k_norm", "b_w_out"]
    outs = {}

    def update(k, g2):
        w, m, v = given[k]
        shape2 = g2.shape
        d2, m2, v2 = _adamw(w.reshape(shape2), g2, m.reshape(shape2), v.reshape(shape2), f"adamw_{k}")
        outs[k] = tuple(a.reshape(w.shape) for a in (g2, d2, m2, v2))

    token = forward_grads("a", token_s)
    g_b_in, g_b_out = finish_grads("b", token)
    update("b_w_in", g_b_in)
    update("b_w_out", g_b_out)
    packed, land = _small_gather_wait(send_s, recv_s, small_arrays, outs["b_w_in"][1])
    tot, g_ada_w, loss, qk = _reduce_small(packed, land, silu_c)
    cw = D_MODEL // N_CHIPS
    g_small = dict(
        norm_g=tot[0:2], a_conv_b=tot[2:3], a_ln_g=tot[3:4], a_ln_b=tot[4:5],
        b_q_norm=qk[0:3], b_k_norm=qk[3:6],
        ada_b=jnp.stack([tot[12:16, :ns].reshape(3 * D_MODEL), tot[16:20, :ns].reshape(3 * D_MODEL)]),
        a_conv_w=lax.dynamic_slice(tot[20:20 + CONV_WIDTH], (0, chip * cw), (CONV_WIDTH, cw)),
    )
    update("ada_w", g_ada_w.reshape(2 * D_MODEL, ns))
    for k, g2 in g_small.items():
        update(k, g2)
    g_a_in, g_a_out = finish_grads("a", outs["ada_w"][1])
    update("a_w_in", g_a_in)
    update("a_w_out", g_a_out)
    return (loss.reshape(()), grad_x[None], *[outs[k][0] for k in order], *[outs[k][1] for k in order],
            *[outs[k][2] for k in order], *[outs[k][3] for k in order])
```

```python
import functools

import jax
import jax.numpy as jnp
from jax import lax
from jax.experimental import pallas as pl
from jax.experimental.pallas import tpu as pltpu

F32 = jnp.float32
BF16 = jnp.bfloat16

SEQ = 2048
D_MODEL = 1024
CONV_WIDTH = 31
HEAD_DIM = 64
N_HEADS = 16
DILATIONS = (1, 4, 16)
ATTN_BLOCK = 128
NORM_EPS = 1e-6
NEG_INF = -1e30
N_DEV = 8
N_CHIPS = 4

ADAM_LR = 0.001
ADAM_B1 = 0.9
ADAM_B2 = 0.999
ADAM_EPS = 1e-08
ADAM_WD = 0.01
ADAM_STEP = 10

VMEM_LIMIT_BYTES = 52 * 1024 * 1024
HALO = 32
LANES = 128
MESH = pl.DeviceIdType.MESH


def _params(*sem):
    return pltpu.CompilerParams(dimension_semantics=sem or None, vmem_limit_bytes=VMEM_LIMIT_BYTES)


def _sigmoid(v):
    return 1.0 / (1.0 + jnp.exp(-v))


def _row_spec(tm, cols, col_block=0):
    return pl.BlockSpec((tm, cols), lambda i: (i, col_block))


def _vec_spec(rows, cols):
    return pl.BlockSpec((rows, cols), lambda i: (0, 0))


def _normmod(xv, g, scale, shift):
    r = lax.rsqrt(jnp.mean(xv * xv, axis=-1, keepdims=True) + NORM_EPS)
    return xv * r * g * (1.0 + scale) + shift


def _normmod_fwd(x, g, scale, shift, name):
    tm = 256

    def body(x_ref, g_ref, sc_ref, sh_ref, h_ref, ht_ref):
        h = _normmod(x_ref[...], g_ref[...], sc_ref[...], sh_ref[...])
        h_ref[...] = h.astype(BF16)
        ht_ref[...] = h.T.astype(BF16)

    return pl.pallas_call(
        body, name=name, grid=(SEQ // tm,),
        in_specs=[_row_spec(tm, D_MODEL)] + [_vec_spec(1, D_MODEL)] * 3,
        out_specs=[_row_spec(tm, D_MODEL), pl.BlockSpec((D_MODEL, tm), lambda i: (0, i))],
        out_shape=[jax.ShapeDtypeStruct((SEQ, D_MODEL), BF16), jax.ShapeDtypeStruct((D_MODEL, SEQ), BF16)],
        compiler_params=_params("parallel"),
    )(x, g, scale, shift)


def _normmod_bwd(x, g, scale, dh_parts, dres, name, part_dilations=None, gated=None):
    tm = 256
    n_parts = len(dh_parts)
    dils = part_dilations or (1,) * n_parts
    dh_parts = [p if d == 1 else p.reshape(d, SEQ // d, D_MODEL) for p, d in zip(dh_parts, dils)]
    n_gated = 0 if gated is None else 2

    def body(x_ref, g_ref, sc_ref, dres_ref, *rest):
        part_refs = rest[:n_parts]
        gated_refs = rest[n_parts:n_parts + n_gated]
        out_refs = rest[n_parts + n_gated:]
        dx_ref, sums_ref, nat = out_refs[0], out_refs[1], out_refs[-1]
        xv = x_ref[...]
        r = lax.rsqrt(jnp.mean(xv * xv, axis=-1, keepdims=True) + NORM_EPS)
        xn = xv * r
        dh = _load_natural(part_refs[0], nat, dils[0])
        for p, d in zip(part_refs[1:], dils[1:]):
            dh = dh + _load_natural(p, nat, d)
        gv = g_ref[...]
        one_sc = 1.0 + sc_ref[...]
        dxn = dh * (gv * one_sc)
        dx = dres_ref[...] + r * (dxn - xn * jnp.mean(dxn * xn, axis=-1, keepdims=True))
        dx_ref[...] = dx
        dhx = dh * xn
        rows = [jnp.sum(dhx, axis=0, keepdims=True) * one_sc,
                jnp.sum(dhx, axis=0, keepdims=True) * gv,
                jnp.sum(dh, axis=0, keepdims=True)]
        if gated is not None:
            gate_ref, y_ref = gated_refs
            out_refs[2][...] = (dx * gate_ref[...]).astype(BF16)
            rows.append(jnp.sum(dx * y_ref[...], axis=0, keepdims=True))
        sums = jnp.concatenate(rows + [jnp.zeros((8 - len(rows), D_MODEL), F32)], axis=0)

        @pl.when(pl.program_id(0) == 0)
        def _():
            sums_ref[...] = jnp.zeros_like(sums_ref)

        sums_ref[...] += sums

    gated_specs = [] if gated is None else [_vec_spec(1, D_MODEL), _row_spec(tm, D_MODEL)]
    dy_spec = [] if gated is None else [_row_spec(tm, D_MODEL)]
    dy_shape = [] if gated is None else [jax.ShapeDtypeStruct((SEQ, D_MODEL), BF16)]
    return pl.pallas_call(
        body, name=name, grid=(SEQ // tm,),
        in_specs=[_row_spec(tm, D_MODEL), _vec_spec(1, D_MODEL), _vec_spec(1, D_MODEL), _row_spec(tm, D_MODEL)]
        + [_class_spec(tm, d) for d in dils] + gated_specs,
        out_specs=[_row_spec(tm, D_MODEL), _vec_spec(8, D_MODEL)] + dy_spec,
        out_shape=[jax.ShapeDtypeStruct((SEQ, D_MODEL), F32), jax.ShapeDtypeStruct((8, D_MODEL), F32)] + dy_shape,
        scratch_shapes=[_natural_scratch(tm)],
        compiler_params=_params("arbitrary"),
    )(x, g, scale, dres, *dh_parts, *(gated or ()))


def _mm(lhs, rhs, *, tn, tile0, n_tiles, out_dtype, name, out3d=None, prev=None):
    mo, kc = lhs.shape
    cm = min(mo, 1024)

    def body(l_ref, r_ref, *rest):
        o_ref = rest[-1]
        for m in range(mo // cm):
            rows = pl.ds(m * cm, cm)
            o_ref[rows, :] = jnp.dot(l_ref[rows, :], r_ref[...], preferred_element_type=F32).astype(out_dtype)

    if rhs.ndim == 3:
        tps_r = rhs.shape[2] // tn
        r_spec = pl.BlockSpec((None, kc, tn), lambda t: ((tile0 + t) // tps_r, 0, (tile0 + t) % tps_r))
    else:
        r_spec = pl.BlockSpec((kc, tn), lambda t: (0, t))
    in_specs = [pl.BlockSpec((mo, kc), lambda t: (0, 0)), r_spec]
    args = [lhs, rhs]
    aliases = {}
    if out3d is None:
        o_spec = pl.BlockSpec((mo, tn), lambda t: (0, t))
        o_shape = jax.ShapeDtypeStruct((mo, n_tiles * tn), out_dtype)
    else:
        j_out, ns_out = out3d
        tps_o = ns_out // tn
        o_spec = pl.BlockSpec((None, mo, tn), lambda t: ((tile0 + t) // tps_o, 0, (tile0 + t) % tps_o))
        o_shape = jax.ShapeDtypeStruct((j_out, mo, ns_out), out_dtype)
        if prev is not None:
            in_specs.append(pl.BlockSpec(memory_space=pl.ANY))
            args.append(prev)
            aliases = {2: 0}
    return pl.pallas_call(
        body, name=name, grid=(n_tiles,), in_specs=in_specs, out_specs=o_spec, out_shape=o_shape,
        input_output_aliases=aliases, compiler_params=_params("parallel"),
    )(*args)


def _mm_nt(dy, w3, *, tn, tile0, n_tiles, name, after=None):
    m_rows = dy.shape[0]
    _, kc, ns = w3.shape
    tps = ns // tn
    cm = 512
    extra = [] if after is None else [after]

    def body(dy_ref, w_ref, *rest):
        o_ref = rest[-1]

        @pl.when(pl.program_id(0) == 0)
        def _():
            o_ref[...] = jnp.zeros_like(o_ref)

        for m in range(m_rows // cm):
            rows = pl.ds(m * cm, cm)
            o_ref[rows, :] += lax.dot_general(dy_ref[rows, :], w_ref[...], (((1,), (1,)), ((), ())),
                                              preferred_element_type=F32)

    return pl.pallas_call(
        body, name=name, grid=(n_tiles,),
        in_specs=[pl.BlockSpec((m_rows, tn), lambda t: (0, t)),
                  pl.BlockSpec((None, kc, tn), lambda t: ((tile0 + t) // tps, 0, (tile0 + t) % tps))]
        + [pl.BlockSpec(memory_space=pl.ANY)] * len(extra),
        out_specs=pl.BlockSpec((m_rows, kc), lambda t: (0, 0)),
        out_shape=jax.ShapeDtypeStruct((m_rows, kc), F32),
        compiler_params=_params("arbitrary"),
    )(dy, w3, *extra)


CONV_CHUNK = 16


def _shift_copies(buf, shifted):
    rows = shifted.shape[1]
    for s in range(1, 8):
        shifted[s - 1] = buf[pl.ds(s, rows), :]


def _shifted_rows(buf, shifted, offset, r0):
    s = offset % 8
    if s == 0:
        return buf[pl.ds(r0 + offset, CONV_CHUNK), :]
    return shifted[s - 1, pl.ds(r0 + (offset - s), CONV_CHUNK), :]


def _conv_fwd(proj, conv_w, conv_b, ln_g, ln_b, name):
    tm = 256
    hb = tm // HALO

    def body(vg_ref, halo_ref, z_ref, w_ref, b_ref, g_ref, be_ref, u5_ref, u5t_ref, u2_ref, buf, shifted):
        i = pl.program_id(0)
        u1 = vg_ref[:, :D_MODEL] * _sigmoid(vg_ref[:, D_MODEL:])
        u1h = halo_ref[:, :D_MODEL] * _sigmoid(halo_ref[:, D_MODEL:])
        buf[pl.ds(0, HALO), :] = jnp.where(i > 0, u1h, 0.0)
        buf[pl.ds(HALO, tm), :] = u1
        _shift_copies(buf, shifted)

        def chunk(ci, carry):
            r0 = pl.multiple_of(ci * CONV_CHUNK, CONV_CHUNK)
            acc = jnp.broadcast_to(b_ref[...], (CONV_CHUNK, D_MODEL))
            for k in range(CONV_WIDTH):
                acc = acc + w_ref[k:k + 1, :] * _shifted_rows(buf, shifted, HALO - (CONV_WIDTH - 1) + k, r0)
            u2_ref[pl.ds(r0, CONV_CHUNK), :] = acc
            return carry

        lax.fori_loop(0, tm // CONV_CHUNK, chunk, 0)
        acc = u2_ref[...]
        mu = jnp.mean(acc, axis=-1, keepdims=True)
        xc = acc - mu
        rstd = lax.rsqrt(jnp.mean(xc * xc, axis=-1, keepdims=True) + NORM_EPS)
        u3 = xc * rstd * g_ref[...] + be_ref[...]
        zv = z_ref[...]
        u5 = u3 * _sigmoid(u3) * (zv * _sigmoid(zv))
        u5_ref[...] = u5.astype(BF16)
        u5t_ref[...] = u5.T.astype(BF16)

    return pl.pallas_call(
        body, name=name, grid=(SEQ // tm,),
        in_specs=[pl.BlockSpec((tm, 2 * D_MODEL), lambda i: (i, 0)),
                  pl.BlockSpec((HALO, 2 * D_MODEL), lambda i: (jnp.maximum(i * hb - 1, 0), 0)),
                  _row_spec(tm, D_MODEL, 2),
                  _vec_spec(CONV_WIDTH, D_MODEL)] + [_vec_spec(1, D_MODEL)] * 3,
        out_specs=[_row_spec(tm, D_MODEL), pl.BlockSpec((D_MODEL, tm), lambda i: (0, i)), _row_spec(tm, D_MODEL)],
        out_shape=[jax.ShapeDtypeStruct((SEQ, D_MODEL), BF16), jax.ShapeDtypeStruct((D_MODEL, SEQ), BF16),
                   jax.ShapeDtypeStruct((SEQ, D_MODEL), F32)],
        scratch_shapes=[pltpu.VMEM((HALO + tm, D_MODEL), F32), pltpu.VMEM((7, HALO + tm - 8, D_MODEL), F32)],
        compiler_params=_params("parallel"),
    )(proj, proj, proj, conv_w, conv_b, ln_g, ln_b)


def _conv_bwd_pointwise(dy, w_out, proj, u2, ln_g, ln_b, name):
    tm = 256

    def body(dy_ref, w_ref, z_ref, u2_ref, g_ref, be_ref, du2_ref, dz_ref, sums_ref):
        u2v = u2_ref[...]
        mu = jnp.mean(u2v, axis=-1, keepdims=True)
        xc = u2v - mu
        rstd = lax.rsqrt(jnp.mean(xc * xc, axis=-1, keepdims=True) + NORM_EPS)
        xhat = xc * rstd
        u3 = xhat * g_ref[...] + be_ref[...]
        s3 = _sigmoid(u3)
        u4 = u3 * s3
        zv = z_ref[...]
        sz = _sigmoid(zv)
        du5v = lax.dot_general(dy_ref[...], w_ref[...], NT_DIMS, preferred_element_type=F32)
        dz_ref[...] = du5v * u4 * (sz * (1.0 + zv * (1.0 - sz)))
        du3 = du5v * (zv * sz) * (s3 * (1.0 + u3 * (1.0 - s3)))
        dxhat = du3 * g_ref[...]
        du2 = rstd * (dxhat - jnp.mean(dxhat, axis=-1, keepdims=True)
                      - xhat * jnp.mean(dxhat * xhat, axis=-1, keepdims=True))
        du2_ref[...] = du2
        sums = jnp.concatenate([
            jnp.sum(du3 * xhat, axis=0, keepdims=True),
            jnp.sum(du3, axis=0, keepdims=True),
            jnp.sum(du2, axis=0, keepdims=True),
            jnp.zeros((5, D_MODEL), F32)], axis=0)

        @pl.when(pl.program_id(0) == 0)
        def _():
            sums_ref[...] = jnp.zeros_like(sums_ref)

        sums_ref[...] += sums

    return pl.pallas_call(
        body, name=name, grid=(SEQ // tm,),
        in_specs=[_row_spec(tm, D_MODEL), _vec_spec(D_MODEL, D_MODEL), _row_spec(tm, D_MODEL, 2),
                  _row_spec(tm, D_MODEL), _vec_spec(1, D_MODEL), _vec_spec(1, D_MODEL)],
        out_specs=[_row_spec(tm, D_MODEL), _row_spec(tm, D_MODEL), _vec_spec(8, D_MODEL)],
        out_shape=[jax.ShapeDtypeStruct((SEQ, D_MODEL), F32), jax.ShapeDtypeStruct((SEQ, D_MODEL), F32),
                   jax.ShapeDtypeStruct((8, D_MODEL), F32)],
        compiler_params=_params("arbitrary"),
    )(dy, w_out, proj, u2, ln_g, ln_b)


def _conv_bwd_taps(du2, dz, proj, conv_w, name):
    tm = 256
    hb = tm // HALO
    n_blocks = SEQ // tm

    def body(du2_ref, dnext_ref, dz_ref, vg_ref, w_ref, dproj_ref, dw_ref, dbuf, dshift, sgbuf, ubuf, dwacc):
        i = pl.program_id(0)
        sg = _sigmoid(vg_ref[:, D_MODEL:])
        sgbuf[...] = sg
        ubuf[...] = vg_ref[:, :D_MODEL] * sg
        dbuf[pl.ds(0, tm), :] = du2_ref[...]
        dbuf[pl.ds(tm, HALO), :] = jnp.where(i < n_blocks - 1, dnext_ref[...], 0.0)
        _shift_copies(dbuf, dshift)

        @pl.when(i == 0)
        def _():
            dwacc[...] = jnp.zeros_like(dwacc)

        def chunk(ci, carry):
            r0 = pl.multiple_of(ci * CONV_CHUNK, CONV_CHUNK)
            rows = pl.ds(r0, CONV_CHUNK)
            u1c = ubuf[rows, :]
            du1 = jnp.zeros((CONV_CHUNK, D_MODEL), F32)
            for k in range(CONV_WIDTH):
                ahead = _shifted_rows(dbuf, dshift, CONV_WIDTH - 1 - k, r0)
                du1 = du1 + w_ref[k:k + 1, :] * ahead
                prod = u1c * ahead
                dwacc[k] += prod[0:8] + prod[8:16]
            sgc = sgbuf[rows, :]
            dval = du1 * sgc
            dproj_ref[rows, 0:D_MODEL] = dval.astype(BF16)
            dproj_ref[rows, D_MODEL:2 * D_MODEL] = (dval * vg_ref[rows, 0:D_MODEL] * (1.0 - sgc)).astype(BF16)
            return carry

        lax.fori_loop(0, tm // CONV_CHUNK, chunk, 0)
        dproj_ref[:, 2 * D_MODEL:] = dz_ref[...].astype(BF16)

        @pl.when(i == n_blocks - 1)
        def _():
            for k in range(CONV_WIDTH):
                dw_ref[k:k + 1, :] = jnp.sum(dwacc[k], axis=0, keepdims=True)
            dw_ref[CONV_WIDTH:, :] = jnp.zeros((32 - CONV_WIDTH, D_MODEL), F32)

    return pl.pallas_call(
        body, name=name, grid=(n_blocks,),
        in_specs=[_row_spec(tm, D_MODEL),
                  pl.BlockSpec((HALO, D_MODEL), lambda i: (jnp.minimum((i + 1) * hb, SEQ // HALO - 1), 0)),
                  _row_spec(tm, D_MODEL),
                  pl.BlockSpec((tm, 2 * D_MODEL), lambda i: (i, 0)),
                  _vec_spec(CONV_WIDTH, D_MODEL)],
        out_specs=[_row_spec(tm, 3 * D_MODEL), _vec_spec(32, D_MODEL)],
        out_shape=[jax.ShapeDtypeStruct((SEQ, 3 * D_MODEL), BF16), jax.ShapeDtypeStruct((32, D_MODEL), F32)],
        scratch_shapes=[pltpu.VMEM((tm + HALO, D_MODEL), F32), pltpu.VMEM((7, HALO + tm - 8, D_MODEL), F32),
                        pltpu.VMEM((tm, D_MODEL), F32), pltpu.VMEM((tm, D_MODEL), F32),
                        pltpu.VMEM((CONV_WIDTH, 8, D_MODEL), F32)],
        compiler_params=_params("arbitrary"),
    )(du2, du2, dz, proj, conv_w)


def _out_a(u5, w_out, x, gate, g1, scale1, shift1, name):
    tm = 256
    n_d = len(DILATIONS)

    def body(u_ref, w_ref, x_ref, gate_ref, g_ref, sc_ref, sh_ref, x1_ref, y_ref, ht_ref, *rest):
        h_refs, nat = rest[:n_d], rest[-1]
        y = jnp.dot(u_ref[...], w_ref[...], preferred_element_type=F32)
        x1 = x_ref[...] + gate_ref[...] * y
        y_ref[...] = y
        x1_ref[...] = x1
        h = _normmod(x1, g_ref[...], sc_ref[...], sh_ref[...])
        ht_ref[...] = h.T.astype(BF16)
        for h_ref, d in zip(h_refs, DILATIONS):
            _store_classes(h_ref, h, nat, d)

    res = pl.pallas_call(
        body, name=name, grid=(SEQ // tm,),
        in_specs=[_row_spec(tm, D_MODEL), _vec_spec(D_MODEL, D_MODEL), _row_spec(tm, D_MODEL)]
        + [_vec_spec(1, D_MODEL)] * 4,
        out_specs=[_row_spec(tm, D_MODEL), _row_spec(tm, D_MODEL), pl.BlockSpec((D_MODEL, tm), lambda i: (0, i))]
        + [_class_spec(tm, d) for d in DILATIONS],
        out_shape=[jax.ShapeDtypeStruct((SEQ, D_MODEL), F32), jax.ShapeDtypeStruct((SEQ, D_MODEL), F32),
                   jax.ShapeDtypeStruct((D_MODEL, SEQ), BF16)] + [_class_shape(d, BF16) for d in DILATIONS],
        scratch_shapes=[_natural_scratch(tm)],
        compiler_params=_params("parallel"),
    )(u5, w_out, x, gate, g1, scale1, shift1)
    return res[0], res[1], res[2], [a.reshape(SEQ, D_MODEL) for a in res[3:]]


def _out_b_loss(u, w_out, x1, gate, target, name):
    tm = 256

    def body(u_ref, w_ref, x_ref, gate_ref, t_ref, e_ref, dy_ref, sums_ref):
        y = jnp.dot(u_ref[...], w_ref[...], preferred_element_type=F32)
        diff = x_ref[...] + gate_ref[...] * y - t_ref[...]
        e = diff * (1.0 / D_MODEL)
        e_ref[...] = e
        dy_ref[...] = (e * gate_ref[...]).astype(BF16)
        sums = jnp.concatenate([
            jnp.sum(e * y, axis=0, keepdims=True),
            jnp.sum(diff * diff, axis=0, keepdims=True),
            jnp.zeros((6, D_MODEL), F32)], axis=0)

        @pl.when(pl.program_id(0) == 0)
        def _():
            sums_ref[...] = jnp.zeros_like(sums_ref)

        sums_ref[...] += sums

    return pl.pallas_call(
        body, name=name, grid=(SEQ // tm,),
        in_specs=[_row_spec(tm, D_MODEL), _vec_spec(D_MODEL, D_MODEL), _row_spec(tm, D_MODEL),
                  _vec_spec(1, D_MODEL), _row_spec(tm, D_MODEL)],
        out_specs=[_row_spec(tm, D_MODEL), _row_spec(tm, D_MODEL), _vec_spec(8, D_MODEL)],
        out_shape=[jax.ShapeDtypeStruct((SEQ, D_MODEL), F32), jax.ShapeDtypeStruct((SEQ, D_MODEL), BF16),
                   jax.ShapeDtypeStruct((8, D_MODEL), F32)],
        compiler_params=_params("arbitrary"),
    )(u, w_out, x1, gate, target)


def _dgate_dy(dx1, y, gate, name):
    tm = 256

    def body(d_ref, y_ref, gate_ref, dy_ref, sums_ref):
        dv = d_ref[...]
        dy_ref[...] = (dv * gate_ref[...]).astype(BF16)
        sums = jnp.concatenate([jnp.sum(dv * y_ref[...], axis=0, keepdims=True), jnp.zeros((7, D_MODEL), F32)], axis=0)

        @pl.when(pl.program_id(0) == 0)
        def _():
            sums_ref[...] = jnp.zeros_like(sums_ref)

        sums_ref[...] += sums

    return pl.pallas_call(
        body, name=name, grid=(SEQ // tm,),
        in_specs=[_row_spec(tm, D_MODEL), _row_spec(tm, D_MODEL), _vec_spec(1, D_MODEL)],
        out_specs=[_row_spec(tm, D_MODEL), _vec_spec(8, D_MODEL)],
        out_shape=[jax.ShapeDtypeStruct((SEQ, D_MODEL), BF16), jax.ShapeDtypeStruct((8, D_MODEL), F32)],
        compiler_params=_params("arbitrary"),
    )(dx1, y, gate)


def _mm_nt_res(dy, w, name):
    tm = 256
    kc, n = w.shape

    def body(dy_ref, w_ref, o_ref):
        o_ref[...] = lax.dot_general(dy_ref[...], w_ref[...], (((1,), (1,)), ((), ())), preferred_element_type=F32)

    return pl.pallas_call(
        body, name=name, grid=(SEQ // tm,),
        in_specs=[_row_spec(tm, n), _vec_spec(kc, n)],
        out_specs=_row_spec(tm, kc),
        out_shape=jax.ShapeDtypeStruct((SEQ, kc), F32),
        compiler_params=_params("parallel"),
    )(dy, w)


def _seg_matrix():
    r = lax.broadcasted_iota(jnp.int32, (256, 256), 0) // HEAD_DIM
    c = lax.broadcasted_iota(jnp.int32, (256, 256), 1) // HEAD_DIM
    return (r == c).astype(BF16)


def _segsum(v, seg):
    hi = v.astype(BF16)
    lo = (v - hi.astype(F32)).astype(BF16)
    outs = []
    for c0 in range(0, D_MODEL, 256):
        outs.append(jnp.dot(hi[:, c0:c0 + 256], seg, preferred_element_type=F32)
                    + jnp.dot(lo[:, c0:c0 + 256], seg, preferred_element_type=F32))
    return jnp.concatenate(outs, axis=1)


def _qk_rstd(v, seg):
    return lax.rsqrt(_segsum(v * v, seg) * (1.0 / HEAD_DIM) + NORM_EPS)


def _qknorm_fwd(proj, qw, kw, seg, name):
    tm = 256

    def body(p_ref, qw_ref, kw_ref, seg_ref, q_ref, k_ref):
        segv = seg_ref[...]
        q = p_ref[:, :D_MODEL].astype(F32)
        k = p_ref[:, D_MODEL:].astype(F32)
        q_ref[...] = (q * _qk_rstd(q, segv) * qw_ref[...]).astype(BF16)
        k_ref[...] = (k * _qk_rstd(k, segv) * kw_ref[...]).astype(BF16)

    return pl.pallas_call(
        body, name=name, grid=(SEQ // tm,),
        in_specs=[_row_spec(tm, 2 * D_MODEL), _vec_spec(1, D_MODEL), _vec_spec(1, D_MODEL), _vec_spec(256, 256)],
        out_specs=[_row_spec(tm, D_MODEL)] * 2,
        out_shape=[jax.ShapeDtypeStruct((SEQ, D_MODEL), BF16)] * 2,
        compiler_params=_params("parallel"),
    )(proj, qw, kw, seg)


def _attn_masks(b, bpc, dilation, slope):
    if bpc == 1:
        qi = lax.broadcasted_iota(jnp.int32, (ATTN_BLOCK, ATTN_BLOCK), 0)
        kj = lax.broadcasted_iota(jnp.int32, (ATTN_BLOCK, ATTN_BLOCK), 1)
        steps = qi - kj
        return (steps * dilation).astype(F32), steps >= 0
    qi = lax.broadcasted_iota(jnp.int32, (ATTN_BLOCK, 2 * ATTN_BLOCK), 0)
    kj = lax.broadcasted_iota(jnp.int32, (ATTN_BLOCK, 2 * ATTN_BLOCK), 1)
    steps = qi + ATTN_BLOCK - kj
    has_prev = (b % bpc) != 0
    valid = (steps >= 0) & (steps <= ATTN_BLOCK) & (has_prev | (kj >= ATTN_BLOCK))
    return (steps * dilation).astype(F32), valid


def _key_tile(prev_ref, cur_ref, cols, bpc):
    if bpc == 1:
        return cur_ref[:, cols]
    return jnp.concatenate([prev_ref[:, cols], cur_ref[:, cols]], axis=0)


ATTN_HEADS_FWD = 16
ATTN_HEADS_BWD = 16
NT_DIMS = (((1,), (1,)), ((), ()))
TN_DIMS = (((0,), (0,)), ((), ()))
BATCH_NT_DIMS = (((2,), (2,)), ((0,), (0,)))
BATCH_NN_DIMS = (((2,), (1,)), ((0,), (0,)))
BATCH_TN_DIMS = (((1,), (1,)), ((0,), (0,)))


def _head_stack(tile_of, heads):
    return jnp.stack([tile_of(slice(h * HEAD_DIM, (h + 1) * HEAD_DIM)) for h in range(heads)], axis=0)


def _attn_specs(heads, segment=0):
    width = heads * HEAD_DIM
    off = segment * (D_MODEL // width)
    last = SEQ // ATTN_BLOCK - 1
    cur = pl.BlockSpec((ATTN_BLOCK, width), lambda hg, b: (jnp.minimum(b, last), hg + off))
    prev = pl.BlockSpec((ATTN_BLOCK, width), lambda hg, b: (jnp.clip(b - 1, 0, last), hg + off))
    return cur, prev


def _attn_fwd(q, k, proj, slopes, dilation, name):
    bpc = SEQ // dilation // ATTN_BLOCK
    heads = ATTN_HEADS_FWD
    assert heads == N_HEADS
    cur, prev = _attn_specs(heads)
    v_cur, v_prev = _attn_specs(heads, segment=2)
    scale = HEAD_DIM ** -0.5

    def body(sl_ref, q_ref, kp_ref, kc_ref, vp_ref, vc_ref, o_ref, lse_ref):
        dist, valid = _attn_masks(pl.program_id(1), bpc, dilation, None)
        q3 = _head_stack(lambda cols: q_ref[:, cols], heads)
        k3 = _head_stack(lambda cols: _key_tile(kp_ref, kc_ref, cols, bpc), heads)
        v3 = _head_stack(lambda cols: _key_tile(vp_ref, vc_ref, cols, bpc), heads)
        s = lax.dot_general(q3, k3, BATCH_NT_DIMS, preferred_element_type=F32)
        s = jnp.where(valid[None], s * scale - dist[None] * sl_ref[...], NEG_INF)
        m = jnp.max(s, axis=-1, keepdims=True)
        p = jnp.exp(s - m)
        l = jnp.sum(p, axis=-1, keepdims=True)
        o3 = lax.dot_general(p.astype(BF16), v3, BATCH_NN_DIMS, preferred_element_type=F32) / l
        lse3 = m + jnp.log(l)
        for h in range(heads):
            o_ref[:, h * HEAD_DIM:(h + 1) * HEAD_DIM] = o3[h]
        lse_ref[...] = jnp.concatenate([lse3[h] for h in range(heads)], axis=1)

    return pl.pallas_call(
        body, name=name, grid=(N_HEADS // heads, SEQ // ATTN_BLOCK),
        in_specs=[pl.BlockSpec((heads, 1, 1), lambda hg, b: (hg, 0, 0)), cur, prev, cur, v_prev, v_cur],
        out_specs=[cur, pl.BlockSpec((ATTN_BLOCK, N_HEADS), lambda hg, b: (b, 0))],
        out_shape=[jax.ShapeDtypeStruct((SEQ, D_MODEL), F32), jax.ShapeDtypeStruct((SEQ, N_HEADS), F32)],
        compiler_params=_params("parallel", "parallel"),
    )(slopes.reshape(N_HEADS, 1, 1), q, k, k, proj, proj)


def _class_spec(tm, dilation):
    if dilation == 1:
        return _row_spec(tm, D_MODEL)
    return pl.BlockSpec((dilation, tm // dilation, D_MODEL), lambda i: (0, i, 0))


def _class_shape(dilation, dtype):
    if dilation == 1:
        return jax.ShapeDtypeStruct((SEQ, D_MODEL), dtype)
    return jax.ShapeDtypeStruct((dilation, SEQ // dilation, D_MODEL), dtype)


def _load_natural(in_ref, nat_ref, dilation):
    if dilation == 1:
        return in_ref[...].astype(F32)
    n = nat_ref.shape[1] // dilation
    for r in range(dilation):
        for j in range(D_MODEL // LANES):
            nat_ref.at[j][pl.ds(r, n, stride=dilation), :] = in_ref[r, :, j * LANES:(j + 1) * LANES].astype(F32)
    return jnp.concatenate([nat_ref[j] for j in range(D_MODEL // LANES)], axis=1)


def _store_classes(out_ref, value, nat_ref, dilation):
    if dilation == 1:
        out_ref[...] = value.astype(out_ref.dtype)
        return
    n = nat_ref.shape[1] // dilation
    for j in range(D_MODEL // LANES):
        nat_ref[j] = value[:, j * LANES:(j + 1) * LANES]
    for r in range(dilation):
        for j in range(D_MODEL // LANES):
            out_ref[r, :, j * LANES:(j + 1) * LANES] = (
                nat_ref.at[j][pl.ds(r, n, stride=dilation), :].astype(out_ref.dtype))


def _natural_scratch(tm):
    return pltpu.VMEM((D_MODEL // LANES, tm, LANES), F32)


def _head_selector():
    lane_head = lax.broadcasted_iota(jnp.int32, (D_MODEL, N_HEADS), 0) // HEAD_DIM
    head = lax.broadcasted_iota(jnp.int32, (D_MODEL, N_HEADS), 1)
    return (lane_head == head).astype(BF16)


def _dot_split(v, m01, dims):
    hi = v.astype(BF16)
    lo = (v - hi.astype(F32)).astype(BF16)
    return (lax.dot_general(hi, m01, dims, preferred_element_type=F32)
            + lax.dot_general(lo, m01, dims, preferred_element_type=F32))


def _merge_fwd(o_parts, lse_parts, z, sel, name):
    tm = 256
    h_spec = pl.BlockSpec((tm, N_HEADS), lambda i: (i, 0))

    def body(o0, o1, o2, l0, l1, l2, z_ref, sel_ref, u_ref, ut_ref, o_ref, lse_ref, nat):
        ls = [l0[...], l1[...], l2[...]]
        m = jnp.maximum(jnp.maximum(ls[0], ls[1]), ls[2])
        tot = m + jnp.log(jnp.exp(ls[0] - m) + jnp.exp(ls[1] - m) + jnp.exp(ls[2] - m))
        o = jnp.zeros((tm, D_MODEL), F32)
        for o_in, l, d in zip((o0, o1, o2), ls, DILATIONS):
            weight = _dot_split(jnp.exp(l - tot), sel_ref[...], NT_DIMS)
            o = o + weight * _load_natural(o_in, nat, d)
        zv = z_ref[...]
        u = o * (zv * _sigmoid(zv))
        u_ref[...] = u.astype(BF16)
        ut_ref[...] = u.T.astype(BF16)
        o_ref[...] = o
        lse_ref[...] = tot

    return pl.pallas_call(
        body, name=name, grid=(SEQ // tm,),
        in_specs=[_class_spec(tm, d) for d in DILATIONS] + [h_spec] * 3
        + [_row_spec(tm, D_MODEL), _vec_spec(D_MODEL, N_HEADS)],
        out_specs=[_row_spec(tm, D_MODEL), pl.BlockSpec((D_MODEL, tm), lambda i: (0, i)),
                   _row_spec(tm, D_MODEL), h_spec],
        out_shape=[jax.ShapeDtypeStruct((SEQ, D_MODEL), BF16), jax.ShapeDtypeStruct((D_MODEL, SEQ), BF16),
                   jax.ShapeDtypeStruct((SEQ, D_MODEL), F32), jax.ShapeDtypeStruct((SEQ, N_HEADS), F32)],
        scratch_shapes=[_natural_scratch(tm)],
        compiler_params=_params("parallel"),
    )(*o_parts, *lse_parts, z, sel)


def _merge_bwd(dy, w_out, o, z, sel, name):
    tm = 256
    n_d = len(DILATIONS)

    def body(dy_ref, w_ref, o_ref, z_ref, sel_ref, dz_ref, delta_ref, *rest):
        do_refs, nat = rest[:n_d], rest[-1]
        zv = z_ref[...]
        sz = _sigmoid(zv)
        duv = lax.dot_general(dy_ref[...], w_ref[...], NT_DIMS, preferred_element_type=F32)
        ov = o_ref[...]
        do = duv * (zv * sz)
        dz_ref[...] = (duv * ov * (sz * (1.0 + zv * (1.0 - sz)))).astype(BF16)
        delta_ref[...] = _dot_split(do * ov, sel_ref[...], (((1,), (0,)), ((), ())))
        for do_ref, d in zip(do_refs, DILATIONS):
            _store_classes(do_ref, do, nat, d)

    res = pl.pallas_call(
        body, name=name, grid=(SEQ // tm,),
        in_specs=[_row_spec(tm, D_MODEL), _vec_spec(D_MODEL, D_MODEL), _row_spec(tm, D_MODEL), _row_spec(tm, D_MODEL),
                  _vec_spec(D_MODEL, N_HEADS)],
        out_specs=[_row_spec(tm, D_MODEL), pl.BlockSpec((tm, N_HEADS), lambda i: (i, 0))]
        + [_class_spec(tm, d) for d in DILATIONS],
        out_shape=[jax.ShapeDtypeStruct((SEQ, D_MODEL), BF16), jax.ShapeDtypeStruct((SEQ, N_HEADS), F32)]
        + [_class_shape(d, BF16) for d in DILATIONS],
        scratch_shapes=[_natural_scratch(tm)],
        compiler_params=_params("parallel"),
    )(dy, w_out, o, z, sel)
    return res[0], res[1], [a.reshape(SEQ, D_MODEL) for a in res[2:]]


def _attn_bwd(q, k, proj, do, lse, delta, slopes, dilation, name):
    bpc = SEQ // dilation // ATTN_BLOCK
    heads = ATTN_HEADS_BWD
    n_blocks = SEQ // ATTN_BLOCK
    carry = bpc > 1
    width = heads * HEAD_DIM
    cur, prev = _attn_specs(heads)
    v_cur, v_prev = _attn_specs(heads, segment=2)
    assert heads == N_HEADS
    per_head = pl.BlockSpec((ATTN_BLOCK, N_HEADS), lambda hg, b: (jnp.minimum(b, n_blocks - 1), 0))
    scale = HEAD_DIM ** -0.5

    def body(sl_ref, q_ref, kp_ref, kc_ref, vp_ref, vc_ref, do_ref, lse_ref, dl_ref,
             dq_ref, dk_ref, dv_ref, *scratch):
        b = pl.program_id(1)
        if carry:
            dk_carry, dv_carry = scratch

            @pl.when(b == n_blocks)
            def _():
                dk_ref[...] = dk_carry[...].astype(BF16)
                dv_ref[...] = dv_carry[...].astype(BF16)

            @pl.when(b < n_blocks)
            def _():
                step(sl_ref, q_ref, kp_ref, kc_ref, vp_ref, vc_ref, do_ref, lse_ref, dl_ref,
                     dq_ref, dk_ref, dv_ref, dk_carry, dv_carry, b)
        else:
            step(sl_ref, q_ref, kp_ref, kc_ref, vp_ref, vc_ref, do_ref, lse_ref, dl_ref,
                 dq_ref, dk_ref, dv_ref, None, None, b)

    def step(sl_ref, q_ref, kp_ref, kc_ref, vp_ref, vc_ref, do_ref, lse_ref, dl_ref,
             dq_ref, dk_ref, dv_ref, dk_carry, dv_carry, b):
        if carry:
            @pl.when(b == 0)
            def _():
                dk_carry[...] = jnp.zeros_like(dk_carry)
                dv_carry[...] = jnp.zeros_like(dv_carry)

        dist, valid = _attn_masks(b, bpc, dilation, None)
        q3 = _head_stack(lambda cols: q_ref[:, cols], heads)
        k3 = _head_stack(lambda cols: _key_tile(kp_ref, kc_ref, cols, bpc), heads)
        v3 = _head_stack(lambda cols: _key_tile(vp_ref, vc_ref, cols, bpc), heads)
        do3 = _head_stack(lambda cols: do_ref[:, cols], heads)
        lse3 = jnp.stack([lse_ref[:, h:h + 1] for h in range(heads)], axis=0)
        dl3 = jnp.stack([dl_ref[:, h:h + 1] for h in range(heads)], axis=0)
        s = lax.dot_general(q3, k3, BATCH_NT_DIMS, preferred_element_type=F32)
        p = jnp.exp(jnp.where(valid[None], s * scale - dist[None] * sl_ref[...], NEG_INF) - lse3)
        dp = lax.dot_general(do3, v3, BATCH_NT_DIMS, preferred_element_type=F32)
        ds = (p * (dp - dl3) * scale).astype(BF16)
        dq3 = lax.dot_general(ds, k3, BATCH_NN_DIMS, preferred_element_type=F32)
        dk3 = lax.dot_general(ds, q3, BATCH_TN_DIMS, preferred_element_type=F32)
        dv3 = lax.dot_general(p.astype(BF16), do3, BATCH_TN_DIMS, preferred_element_type=F32)
        for h in range(heads):
            cols = slice(h * HEAD_DIM, (h + 1) * HEAD_DIM)
            dq_ref[:, cols] = dq3[h].astype(BF16)
            if carry:
                dk_ref[:, cols] = (dk_carry[:, cols] + dk3[h, :ATTN_BLOCK]).astype(BF16)
                dv_ref[:, cols] = (dv_carry[:, cols] + dv3[h, :ATTN_BLOCK]).astype(BF16)
                dk_carry[:, cols] = dk3[h, ATTN_BLOCK:]
                dv_carry[:, cols] = dv3[h, ATTN_BLOCK:]
            else:
                dk_ref[:, cols] = dk3[h].astype(BF16)
                dv_ref[:, cols] = dv3[h].astype(BF16)

    kv_out = prev if carry else cur
    return pl.pallas_call(
        body, name=name, grid=(N_HEADS // heads, n_blocks + (1 if carry else 0)),
        in_specs=[pl.BlockSpec((heads, 1, 1), lambda hg, b: (hg, 0, 0)), cur, prev, cur, v_prev, v_cur,
                  cur, per_head, per_head],
        out_specs=[cur, kv_out, kv_out],
        out_shape=[jax.ShapeDtypeStruct((SEQ, D_MODEL), BF16)] * 3,
        scratch_shapes=[pltpu.VMEM((ATTN_BLOCK, width), F32)] * 2 if carry else [],
        compiler_params=_params("parallel", "arbitrary"),
    )(slopes.reshape(N_HEADS, 1, 1), q, k, k, proj, proj, do, lse, delta)


def _qknorm_bwd(proj, qw, kw, seg, dq, dk, dv, name):
    tm = 256

    def body(p_ref, qw_ref, kw_ref, seg_ref, dq_ref, dk_ref, dv_ref, dproj_ref, sums_ref):
        segv = seg_ref[...]
        sums = []
        for part, (w_ref, dn_ref) in enumerate(((qw_ref, dq_ref), (kw_ref, dk_ref))):
            raw = p_ref[:, part * D_MODEL:(part + 1) * D_MODEL].astype(F32)
            dn = dn_ref[...].astype(F32)
            r = _qk_rstd(raw, segv)
            gq = dn * w_ref[...]
            draw = r * gq - raw * (r * r * r) * (_segsum(raw * gq, segv) * (1.0 / HEAD_DIM))
            dproj_ref[:, part * D_MODEL:(part + 1) * D_MODEL] = draw.astype(BF16)
            sums.append(jnp.sum(dn * raw * r, axis=0, keepdims=True))
        dproj_ref[:, 2 * D_MODEL:] = dv_ref[...]

        @pl.when(pl.program_id(0) == 0)
        def _():
            sums_ref[...] = jnp.zeros_like(sums_ref)

        sums_ref[...] += jnp.concatenate(sums + [jnp.zeros((6, D_MODEL), F32)], axis=0)

    return pl.pallas_call(
        body, name=name, grid=(SEQ // tm,),
        in_specs=[_row_spec(tm, 3 * D_MODEL), _vec_spec(1, D_MODEL), _vec_spec(1, D_MODEL), _vec_spec(256, 256)]
        + [_row_spec(tm, D_MODEL)] * 3,
        out_specs=[_row_spec(tm, 3 * D_MODEL), _vec_spec(8, D_MODEL)],
        out_shape=[jax.ShapeDtypeStruct((SEQ, 3 * D_MODEL), BF16), jax.ShapeDtypeStruct((8, D_MODEL), F32)],
        compiler_params=_params("arbitrary"),
    )(proj, qw, kw, seg, dq, dk, dv)


def _to_classes(a, dilation):
    if dilation == 1:
        return a
    s, c = a.shape
    return a.reshape(s // dilation, dilation, c).transpose(1, 0, 2).reshape(s, c)


def _from_classes(a, dilation):
    if dilation == 1:
        return a
    s, c = a.shape
    return a.reshape(dilation, s // dilation, c).transpose(1, 0, 2).reshape(s, c)


def _cols_to_classes(a, dilation):
    if dilation == 1:
        return a
    r, s = a.shape
    return a.reshape(r, s // dilation, dilation).transpose(0, 2, 1).reshape(r, s)


B_TN = 512
B_GROUP_TILES = 3 * D_MODEL // B_TN
B_Z_TILE0 = 3 * B_GROUP_TILES
B_Z_TILES = D_MODEL // B_TN


def _local_step(x, target, mods, norm_g, conv_w, conv_b, ln_g, ln_b, q_norm, k_norm,
                weights_a, weights_b, forward_weights_b, send_grads_b, forward_grads_b, send_grads_a):
    row = lambda a, i: a[i:i + 1]
    shift0, scale0, gate0 = row(mods[0], 0), row(mods[0], 1), row(mods[0], 2)
    shift1, scale1, gate1 = row(mods[1], 0), row(mods[1], 1), row(mods[1], 2)
    g0, g1 = row(norm_g, 0), row(norm_g, 1)
    seg = _seg_matrix()
    slopes = jnp.exp2(-8.0 * jnp.arange(1, N_HEADS + 1, dtype=F32) / N_HEADS)
    qw = [jnp.tile(q_norm[g:g + 1], (1, N_HEADS)) for g in range(3)]
    kw = [jnp.tile(k_norm[g:g + 1], (1, N_HEADS)) for g in range(3)]

    h0, h0t = _normmod_fwd(x, g0, scale0, shift0, "prenorm0")
    wa_in, wa_out = weights_a(h0)
    ja, _, nsa = wa_in.shape
    proj_a = _mm(h0, wa_in, tn=nsa, tile0=0, n_tiles=ja, out_dtype=F32, name="a_in")
    u5, u5t, u2 = _conv_fwd(proj_a, conv_w, conv_b, ln_g, ln_b, "a_conv")
    token = forward_weights_b(u5)
    x1, y_a, h1t, h1c = _out_a(u5, wa_out, x, gate0 + token[0:1, 0:1], g1, scale1, shift1, "a_out")

    wb_in, wb_out = weights_b(x1)
    jb, _, nsb = wb_in.shape
    h1 = h1c[0]
    h1tc = [_cols_to_classes(h1t, d) for d in DILATIONS]
    z_b = _mm(h1, wb_in, tn=B_TN, tile0=B_Z_TILE0, n_tiles=B_Z_TILES, out_dtype=F32, name="b_in_z")
    proj_g, qkv, o_parts, lse_parts = [], [], [], []
    for g, d in enumerate(DILATIONS):
        pg = _mm(h1c[g], wb_in, tn=B_TN, tile0=g * B_GROUP_TILES, n_tiles=B_GROUP_TILES, out_dtype=BF16,
                 name=f"b_in_g{g}")
        qn, kn = _qknorm_fwd(pg, qw[g], kw[g], seg, f"b_qknorm_g{g}")
        og, lg = _attn_fwd(qn, kn, pg, slopes, d, f"b_attn_g{g}")
        proj_g.append(pg)
        qkv.append((qn, kn))
        o_parts.append(og if d == 1 else og.reshape(d, SEQ // d, D_MODEL))
        lse_parts.append(_from_classes(lg, d))
    sel = _head_selector()
    u_b, u_bt, o_b, lse_b = _merge_fwd(o_parts, lse_parts, z_b, sel, "b_merge")
    e, dy_b, sums_loss = _out_b_loss(u_b, wb_out, x1, gate1, target, "b_out_loss")

    dwb_out = _mm(u_bt, dy_b, tn=D_MODEL, tile0=0, n_tiles=1, out_dtype=BF16, name="b_dwout")
    dz_b, delta_b, do_c = _merge_bwd(dy_b, wb_out, o_b, z_b, sel, "b_merge_bwd")
    dwb_in = _mm(h1t, dz_b, tn=B_TN, tile0=B_Z_TILE0, n_tiles=B_Z_TILES, out_dtype=BF16, name="b_dwin_z",
                 out3d=(jb, nsb))
    dh1_parts = [_mm_nt(dz_b, wb_in, tn=B_TN, tile0=B_Z_TILE0, n_tiles=B_Z_TILES, name="b_dh_z")]
    qk_sums = []
    for g, d in enumerate(DILATIONS):
        qn, kn = qkv[g]
        dq, dk, dv = _attn_bwd(qn, kn, proj_g[g], do_c[g], _to_classes(lse_b, d), _to_classes(delta_b, d),
                               slopes, d, f"b_attn_bwd_g{g}")
        dproj, sums_qk = _qknorm_bwd(proj_g[g], qw[g], kw[g], seg, dq, dk, dv, f"b_qknorm_bwd_g{g}")
        qk_sums.append(sums_qk)
        dwb_in = _mm(h1tc[g], dproj, tn=B_TN, tile0=g * B_GROUP_TILES, n_tiles=B_GROUP_TILES, out_dtype=BF16,
                     name=f"b_dwin_g{g}", out3d=(jb, nsb), prev=dwb_in)
        dh = _mm_nt(dproj, wb_in, tn=B_TN, tile0=g * B_GROUP_TILES, n_tiles=B_GROUP_TILES, name=f"b_dh_g{g}")
        dh1_parts.append(dh)
    token = send_grads_b(dwb_in, dwb_out)
    dx1, sums_n1, dy_a = _normmod_bwd(x1, g1, scale1 + token[0:1, 0:1], dh1_parts, e, "prenorm1_bwd",
                                      part_dilations=(1,) + DILATIONS, gated=(gate0, y_a))
    token = forward_grads_b(dx1)

    dwa_out = _mm(u5t, dy_a, tn=D_MODEL, tile0=0, n_tiles=1, out_dtype=BF16, name="a_dwout")
    du2, dz_a, sums_ln = _conv_bwd_pointwise(dy_a, wa_out, proj_a, u2, ln_g + token[0:1, 0:1], ln_b,
                                             "a_conv_bwd_pw")
    dproj_a, dconv_w = _conv_bwd_taps(du2, dz_a, proj_a, conv_w, "a_conv_bwd_taps")
    dwa_in = _mm(h0t, dproj_a, tn=nsa, tile0=0, n_tiles=ja, out_dtype=BF16, name="a_dwin", out3d=(ja, nsa))
    token = send_grads_a(dwa_in, dwa_out)
    dh0 = _mm_nt(dproj_a, wa_in, tn=nsa, tile0=0, n_tiles=ja, name="a_dh", after=token)
    grad_x, sums_n0 = _normmod_bwd(x, g0, scale0, [dh0], dx1, "prenorm0_bwd")

    small = dict(
        dnorm_g=jnp.concatenate([sums_n0[0:1], sums_n1[0:1]], axis=0),
        dmod0=jnp.concatenate([sums_n0[2:3], sums_n0[1:2], sums_n1[3:4]], axis=0),
        dmod1=jnp.concatenate([sums_n1[2:3], sums_n1[1:2], sums_loss[0:1]], axis=0),
        dln_g=sums_ln[0:1], dln_b=sums_ln[1:2], dconv_b=sums_ln[2:3],
        dconv_w=dconv_w[:CONV_WIDTH],
        dq_norm=jnp.concatenate([s[0:1] for s in qk_sums], axis=0),
        dk_norm=jnp.concatenate([s[1:2] for s in qk_sums], axis=0),
        loss_cols=sums_loss[1:2],
    )
    return grad_x, small


def _adamw(w, g, m, v, name):
    rows, cols = w.shape
    tr = rows if rows <= 128 else 128
    c1 = 1.0 / (1.0 - ADAM_B1 ** ADAM_STEP)
    c2 = 1.0 / (1.0 - ADAM_B2 ** ADAM_STEP)

    def body(w_ref, g_ref, m_ref, v_ref, d_ref, mo_ref, vo_ref):
        gv = g_ref[...]
        mn = ADAM_B1 * m_ref[...] + (1.0 - ADAM_B1) * gv
        vn = ADAM_B2 * v_ref[...] + (1.0 - ADAM_B2) * (gv * gv)
        mo_ref[...] = mn
        vo_ref[...] = vn
        d_ref[...] = -ADAM_LR * ((mn * c1) / (jnp.sqrt(vn * c2) + ADAM_EPS) + ADAM_WD * w_ref[...])

    spec = pl.BlockSpec((tr, cols), lambda i: (i, 0))
    return pl.pallas_call(
        body, name=name, grid=(rows // tr,), in_specs=[spec] * 4, out_specs=[spec] * 3,
        out_shape=[jax.ShapeDtypeStruct((rows, cols), F32)] * 3,
        compiler_params=_params("parallel"),
    )(w, g, m, v)


def _cast_into_slot(w, chip_idx, name):
    rows, cols = w.shape
    tr = 256

    def body(ch_ref, w_ref, o_ref):
        o_ref[...] = w_ref[...].astype(BF16)

    return pl.pallas_call(
        body, name=name,
        grid_spec=pltpu.PrefetchScalarGridSpec(
            num_scalar_prefetch=1, grid=(rows // tr,),
            in_specs=[pl.BlockSpec((tr, cols), lambda i, ch: (i, 0))],
            out_specs=pl.BlockSpec((None, tr, cols), lambda i, ch: (ch[0], i, 0))),
        out_shape=jax.ShapeDtypeStruct((N_CHIPS, rows, cols), BF16), compiler_params=_params("parallel"),
    )(chip_idx, w)


def _position():
    x, y, c = lax.axis_index("x"), lax.axis_index("y"), lax.axis_index("c")
    return x, y, c


def _xor_peer(x, y, c, k):
    return (x ^ ((k >> 2) & 1), y ^ ((k >> 1) & 1), c ^ (k & 1))


def _chip_peer(x, y, k):
    return (x ^ ((k >> 1) & 1), y ^ (k & 1))


def _ada_forward(c_row, ada_w, ada_b, conv_w):
    ns = ada_w.shape[2]
    cw = conv_w.shape[1]

    def body(c_ref, w_ref, b_ref, cv_ref, mod_ref, sc_ref, cvo_ref,
             c_all, mp, parts, cv_parts, send1, recv1, send2, recv2, send3, recv3):
        x, y, c = _position()
        me = 4 * x + 2 * y + c
        chip = 2 * x + y

        def c_copy(k):
            return pltpu.make_async_remote_copy(
                src_ref=c_all.at[me], dst_ref=c_all.at[me], send_sem=send1.at[k - 1], recv_sem=recv1.at[k - 1],
                device_id=_xor_peer(x, y, c, k), device_id_type=MESH)

        def cv_copy(k):
            px, py = _chip_peer(x, y, k)
            return pltpu.make_async_remote_copy(
                src_ref=cv_parts.at[chip], dst_ref=cv_parts.at[chip], send_sem=send3.at[k - 1],
                recv_sem=recv3.at[k - 1], device_id=(px, py, c), device_id_type=MESH)

        c_all[me] = c_ref[...]
        cv_parts[chip] = cv_ref[...]
        for k in range(1, N_DEV):
            c_copy(k).start()
        for k in range(1, N_CHIPS):
            cv_copy(k).start()
        for k in range(1, N_DEV):
            c_copy(k).wait_recv()
        cv = jnp.concatenate([c_all[i] for i in range(N_DEV)], axis=0)
        sc = cv * _sigmoid(cv)
        sc_ref[...] = sc
        for l in range(2):
            res = jnp.dot(sc, w_ref[l], preferred_element_type=F32, precision=lax.Precision.HIGHEST)
            for i in range(N_DEV):
                mp[i, l:l + 1, :] = res[i:i + 1, :]

        def mod_copy(k):
            px, py = _chip_peer(x, y, k)
            return pltpu.make_async_remote_copy(
                src_ref=mp.at[4 * px + 2 * py + c], dst_ref=parts.at[chip], send_sem=send2.at[k - 1],
                recv_sem=recv2.at[k - 1], device_id=(px, py, c), device_id_type=MESH)

        for k in range(1, N_CHIPS):
            mod_copy(k).start()
        parts[chip] = mp[me]
        for k in range(1, N_CHIPS):
            mod_copy(k).wait_recv()
            cv_copy(k).wait_recv()
        mod_ref[...] = jnp.concatenate([parts[j] for j in range(N_CHIPS)], axis=1) + b_ref[...]
        cvo_ref[...] = jnp.concatenate([cv_parts[j] for j in range(N_CHIPS)], axis=1)
        for k in range(1, N_DEV):
            c_copy(k).wait_send()
        for k in range(1, N_CHIPS):
            mod_copy(k).wait_send()
            cv_copy(k).wait_send()

    vm = pl.BlockSpec(memory_space=pltpu.VMEM)
    return pl.pallas_call(
        body, name="ada_forward",
        in_specs=[vm] * 4, out_specs=[vm] * 3,
        out_shape=[jax.ShapeDtypeStruct((2, 3 * D_MODEL), F32), jax.ShapeDtypeStruct((N_DEV, D_MODEL), F32),
                   jax.ShapeDtypeStruct((CONV_WIDTH, N_CHIPS * cw), F32)],
        scratch_shapes=[pltpu.VMEM((N_DEV, 1, D_MODEL), F32), pltpu.VMEM((N_DEV, 2, ns), F32),
                        pltpu.VMEM((N_CHIPS, 2, ns), F32), pltpu.VMEM((N_CHIPS, CONV_WIDTH, cw), F32),
                        pltpu.SemaphoreType.DMA((N_DEV - 1,)), pltpu.SemaphoreType.DMA((N_DEV - 1,)),
                        pltpu.SemaphoreType.DMA((N_CHIPS - 1,)), pltpu.SemaphoreType.DMA((N_CHIPS - 1,)),
                        pltpu.SemaphoreType.DMA((N_CHIPS - 1,)), pltpu.SemaphoreType.DMA((N_CHIPS - 1,))],
        compiler_params=pltpu.CompilerParams(vmem_limit_bytes=VMEM_LIMIT_BYTES),
    )(c_row, ada_w, ada_b, conv_w)


HBM_SPEC = pl.BlockSpec(memory_space=pltpu.HBM)
ANY_SPEC = pl.BlockSpec(memory_space=pl.ANY)
SEM_SPEC = pl.BlockSpec(memory_space=pltpu.SEMAPHORE)
SPLIT_PARAMS = dict(compiler_params=pltpu.CompilerParams(has_side_effects=pltpu.SideEffectType.DATAFLOW_SIDE_EFFECTING))
TOKEN = jax.ShapeDtypeStruct((8, 128), F32)


def _hbm(arrays):
    return [pltpu.with_memory_space_constraint(a, pltpu.HBM) for a in arrays]


def _hbm_like(arrays):
    return [pltpu.HBM(a.shape, a.dtype) for a in arrays]


def _gather_start(lands, after, name):
    n = len(lands)

    def body(*refs):
        ins = refs[:n]
        send, recv = refs[n + 1], refs[n + 2]
        x, y, c = _position()
        chip = 2 * x + y
        for t in range(n):
            rh = ins[t].shape[1] // 2
            for k in range(1, N_CHIPS):
                px, py = _chip_peer(x, y, k)
                block = ins[t].at[chip, pl.ds(c * rh, rh)]
                pltpu.make_async_remote_copy(
                    src_ref=block, dst_ref=block, send_sem=send.at[3 * t + k - 1], recv_sem=recv.at[3 * t + k - 1],
                    device_id=(px, py, c), device_id_type=MESH).start()
        refs[-1][...] = jnp.zeros(TOKEN.shape, F32)

    res = pl.pallas_call(
        body, name=name, in_specs=[HBM_SPEC] * n + [ANY_SPEC],
        out_specs=(SEM_SPEC, SEM_SPEC, *[HBM_SPEC] * n, pl.BlockSpec(memory_space=pltpu.VMEM)),
        out_shape=(pltpu.SemaphoreType.DMA((3 * n,)), pltpu.SemaphoreType.DMA((3 * n,)), *_hbm_like(lands), TOKEN),
        input_output_aliases={t: 2 + t for t in range(n)}, **SPLIT_PARAMS,
    )(*_hbm(lands), after)
    return res[0], res[1], list(res[2:2 + n]), res[-1]


def _gather_forward(send, recv, lands, after, name):
    n = len(lands)

    def body(*refs):
        ins = refs[:n]
        send1, recv1 = refs[n], refs[n + 1]
        send2, recv2 = refs[n + 3], refs[n + 4]
        x, y, c = _position()
        chip = 2 * x + y
        for t in range(n):
            rh = ins[t].shape[1] // 2
            half = pl.ds(c * rh, rh)
            for k in range(1, N_CHIPS):
                px, py = _chip_peer(x, y, k)
                s = 3 * t + k - 1
                got = ins[t].at[2 * px + py, half]
                cp = pltpu.make_async_remote_copy(
                    src_ref=ins[t].at[chip, half], dst_ref=got, send_sem=send1.at[s], recv_sem=recv1.at[s],
                    device_id=(px, py, c), device_id_type=MESH)
                cp.wait_send()
                cp.wait_recv()
                pltpu.make_async_remote_copy(
                    src_ref=got, dst_ref=got, send_sem=send2.at[s], recv_sem=recv2.at[s],
                    device_id=(x, y, 1 - c), device_id_type=MESH).start()
        refs[-1][...] = jnp.zeros(TOKEN.shape, F32)

    res = pl.pallas_call(
        body, name=name, in_specs=[HBM_SPEC] * n + [SEM_SPEC, SEM_SPEC, ANY_SPEC],
        out_specs=(SEM_SPEC, SEM_SPEC, *[HBM_SPEC] * n, pl.BlockSpec(memory_space=pltpu.VMEM)),
        out_shape=(pltpu.SemaphoreType.DMA((3 * n,)), pltpu.SemaphoreType.DMA((3 * n,)), *_hbm_like(lands), TOKEN),
        input_output_aliases={t: 2 + t for t in range(n)}, **SPLIT_PARAMS,
    )(*lands, send, recv, after)
    return res[0], res[1], list(res[2:2 + n]), res[-1]


def _gather_wait(send, recv, lands, after, name):
    n = len(lands)

    def body(*refs):
        ins = refs[:n]
        send_ref, recv_ref = refs[n], refs[n + 1]
        x, y, c = _position()
        for t in range(n):
            rh = ins[t].shape[1] // 2
            for k in range(1, N_CHIPS):
                px, py = _chip_peer(x, y, k)
                cp = pltpu.make_async_remote_copy(
                    src_ref=ins[t].at[2 * px + py, pl.ds(c * rh, rh)],
                    dst_ref=ins[t].at[2 * px + py, pl.ds((1 - c) * rh, rh)], send_sem=send_ref.at[3 * t + k - 1],
                    recv_sem=recv_ref.at[3 * t + k - 1], device_id=(x, y, 1 - c), device_id_type=MESH)
                cp.wait_send()
                cp.wait_recv()

    res = pl.pallas_call(
        body, name=name, in_specs=[HBM_SPEC] * n + [SEM_SPEC, SEM_SPEC, ANY_SPEC], out_specs=[HBM_SPEC] * n,
        out_shape=_hbm_like(lands), input_output_aliases={t: t for t in range(n)}, **SPLIT_PARAMS,
    )(*lands, send, recv, after)
    return list(res)


def _reduce_start(grads, after, name):
    n = len(grads)
    lands = [lax.empty((N_DEV, g.shape[1] // 2, g.shape[2]), BF16) for g in grads]

    def body(*refs):
        gs, ls = refs[:n], refs[n:2 * n]
        send, recv = refs[2 * n + 1], refs[2 * n + 2]
        x, y, c = _position()
        me = 4 * x + 2 * y + c
        for t in range(n):
            rh = gs[t].shape[1] // 2
            for k in range(1, N_DEV):
                px, py, pc = _xor_peer(x, y, c, k)
                pltpu.make_async_remote_copy(
                    src_ref=gs[t].at[2 * px + py, pl.ds(pc * rh, rh)], dst_ref=ls[t].at[me],
                    send_sem=send.at[7 * t + k - 1], recv_sem=recv.at[7 * t + k - 1],
                    device_id=(px, py, pc), device_id_type=MESH).start()
        refs[-1][...] = jnp.zeros(TOKEN.shape, F32)

    res = pl.pallas_call(
        body, name=name, in_specs=[HBM_SPEC] * (2 * n) + [ANY_SPEC],
        out_specs=(SEM_SPEC, SEM_SPEC, *[HBM_SPEC] * (2 * n), pl.BlockSpec(memory_space=pltpu.VMEM)),
        out_shape=(pltpu.SemaphoreType.DMA((7 * n,)), pltpu.SemaphoreType.DMA((7 * n,)),
                   *_hbm_like(grads), *_hbm_like(lands), TOKEN),
        input_output_aliases={t: 2 + t for t in range(2 * n)}, **SPLIT_PARAMS,
    )(*_hbm(grads), *_hbm(lands), after)
    return res[0], res[1], list(res[2:2 + n]), list(res[2 + n:2 + 2 * n]), res[-1]


def _reduce_wait(send, recv, grads, lands, after, name):
    n = len(grads)

    def body(*refs):
        gs, ls = refs[:n], refs[n:2 * n]
        send_ref, recv_ref = refs[2 * n], refs[2 * n + 1]
        x, y, c = _position()
        for t in range(n):
            rh = gs[t].shape[1] // 2
            for k in range(1, N_DEV):
                px, py, pc = _xor_peer(x, y, c, k)
                cp = pltpu.make_async_remote_copy(
                    src_ref=gs[t].at[2 * px + py, pl.ds(pc * rh, rh)], dst_ref=ls[t].at[4 * px + 2 * py + pc],
                    send_sem=send_ref.at[7 * t + k - 1], recv_sem=recv_ref.at[7 * t + k - 1],
                    device_id=(px, py, pc), device_id_type=MESH)
                cp.wait_send()
                cp.wait_recv()

    res = pl.pallas_call(
        body, name=name, in_specs=[HBM_SPEC] * (2 * n) + [SEM_SPEC, SEM_SPEC, ANY_SPEC], out_specs=[HBM_SPEC] * (2 * n),
        out_shape=_hbm_like(grads) + _hbm_like(lands), input_output_aliases={t: t for t in range(2 * n)}, **SPLIT_PARAMS,
    )(*grads, *lands, send, recv, after)
    return list(res[:n]), list(res[n:])


def _sum_devices(land, grad, dev_idx, name):
    _, rh, cols = land.shape
    tr = 128
    nb = rh // tr

    def body(idx_ref, l_ref, g_ref, o_ref):
        me = idx_ref[0]
        acc = jnp.where(me == 0, g_ref[...], l_ref[0]).astype(F32)
        for d in range(1, N_DEV):
            acc = acc + jnp.where(me == d, g_ref[...], l_ref[d]).astype(F32)
        o_ref[...] = acc

    return pl.pallas_call(
        body, name=name,
        grid_spec=pltpu.PrefetchScalarGridSpec(
            num_scalar_prefetch=1, grid=(nb,),
            in_specs=[pl.BlockSpec((N_DEV, tr, cols), lambda i, idx: (0, i, 0)),
                      pl.BlockSpec((None, tr, cols), lambda i, idx: (idx[1], idx[2] * nb + i, 0))],
            out_specs=pl.BlockSpec((tr, cols), lambda i, idx: (idx[2] * nb + i, 0))),
        out_shape=jax.ShapeDtypeStruct((2 * rh, cols), F32), compiler_params=_params("parallel"),
    )(dev_idx, land, grad)


def _split_start(name, arrays, n_sems, after, issue):
    m = len(arrays)

    def body(*refs):
        issue(refs[:m], refs[m + 1], refs[m + 2])
        refs[-1][...] = jnp.zeros(TOKEN.shape, F32)

    res = pl.pallas_call(
        body, name=name, in_specs=[HBM_SPEC] * m + [ANY_SPEC],
        out_specs=(SEM_SPEC, SEM_SPEC, *[HBM_SPEC] * m, pl.BlockSpec(memory_space=pltpu.VMEM)),
        out_shape=(pltpu.SemaphoreType.DMA((n_sems,)), pltpu.SemaphoreType.DMA((n_sems,)), *_hbm_like(arrays), TOKEN),
        input_output_aliases={t: 2 + t for t in range(m)}, **SPLIT_PARAMS,
    )(*_hbm(arrays), after)
    return res[0], res[1], list(res[2:2 + m]), res[-1]


def _split_wait(name, arrays, send, recv, after, await_all):
    m = len(arrays)

    def body(*refs):
        await_all(refs[:m], refs[m], refs[m + 1])

    res = pl.pallas_call(
        body, name=name, in_specs=[HBM_SPEC] * m + [SEM_SPEC, SEM_SPEC, ANY_SPEC], out_specs=[HBM_SPEC] * m,
        out_shape=_hbm_like(arrays), input_output_aliases={t: t for t in range(m)}, **SPLIT_PARAMS,
    )(*arrays, send, recv, after)
    return list(res)


def _sibling_copies(refs, send, recv, n):
    x, y, c = _position()
    cps = []
    for t in range(n):
        rh = refs[t].shape[1] // 2
        cps.append(pltpu.make_async_remote_copy(
            src_ref=refs[t].at[pl.ds(0, N_CHIPS), pl.ds((1 - c) * rh, rh)], dst_ref=refs[n + t],
            send_sem=send.at[t], recv_sem=recv.at[t], device_id=(x, y, 1 - c), device_id_type=MESH))
    return cps


def _reduce_sibling_start(grads, after, name):
    n = len(grads)
    lands = [lax.empty((N_CHIPS, g.shape[1] // 2, g.shape[2]), BF16) for g in grads]

    def issue(refs, send, recv):
        for cp in _sibling_copies(refs, send, recv, n):
            cp.start()

    return _split_start(name, list(grads) + lands, n, after, issue)


def _reduce_sibling_wait(send, recv, arrays, after, name):
    n = len(arrays) // 2

    def await_all(refs, send_ref, recv_ref):
        for cp in _sibling_copies(refs, send_ref, recv_ref, n):
            cp.wait_send()
            cp.wait_recv()

    res = _split_wait(name, arrays, send, recv, after, await_all)
    return res[:n], res[n:]


def _add_sibling_half(grad, got, dev_idx, name):
    j, r, cols = grad.shape
    rh = r // 2
    tr = 128
    nb = rh // tr

    def body(idx_ref, g_ref, got_ref, out_ref):
        out_ref[...] = (g_ref[...].astype(F32) + got_ref[...].astype(F32)).astype(BF16)

    return pl.pallas_call(
        body, name=name,
        grid_spec=pltpu.PrefetchScalarGridSpec(
            num_scalar_prefetch=1, grid=(j, nb),
            in_specs=[pl.BlockSpec((None, tr, cols), lambda jj, i, idx: (jj, idx[2] * nb + i, 0)),
                      pl.BlockSpec((None, tr, cols), lambda jj, i, idx: (jj, i, 0))],
            out_specs=pl.BlockSpec((None, tr, cols), lambda jj, i, idx: (jj, i, 0))),
        out_shape=jax.ShapeDtypeStruct((j, rh, cols), BF16),
        compiler_params=_params("parallel", "parallel"),
    )(dev_idx, grad, got)


def _chip_copies(refs, send, recv, n, receiving):
    x, y, c = _position()
    chip = 2 * x + y
    cps = []
    for t in range(n):
        for k in range(1, N_CHIPS):
            px, py = _chip_peer(x, y, k)
            cps.append(pltpu.make_async_remote_copy(
                src_ref=refs[t].at[2 * px + py], dst_ref=refs[n + t].at[2 * px + py if receiving else chip],
                send_sem=send.at[3 * t + k - 1], recv_sem=recv.at[3 * t + k - 1],
                device_id=(px, py, c), device_id_type=MESH))
    return cps


def _reduce_chips_start(partials, after, name):
    n = len(partials)
    lands = [lax.empty(p.shape, BF16) for p in partials]

    def issue(refs, send, recv):
        for cp in _chip_copies(refs, send, recv, n, False):
            cp.start()

    return _split_start(name, list(partials) + lands, 3 * n, after, issue)


def _reduce_chips_wait(send, recv, arrays, after, name):
    n = len(arrays) // 2

    def await_all(refs, send_ref, recv_ref):
        for cp in _chip_copies(refs, send_ref, recv_ref, n, True):
            cp.wait_send()
            cp.wait_recv()

    res = _split_wait(name, arrays, send, recv, after, await_all)
    return res[:n], res[n:]


def _sum_partials(land, partial, dev_idx, name):
    _, rh, cols = land.shape
    tr = 128
    nb = rh // tr

    def body(idx_ref, l_ref, p_ref, o_ref):
        chip = idx_ref[1]
        acc = jnp.where(chip == 0, p_ref[...], l_ref[0]).astype(F32)
        for s in range(1, N_CHIPS):
            acc = acc + jnp.where(chip == s, p_ref[...], l_ref[s]).astype(F32)
        o_ref[...] = acc

    return pl.pallas_call(
        body, name=name,
        grid_spec=pltpu.PrefetchScalarGridSpec(
            num_scalar_prefetch=1, grid=(nb,),
            in_specs=[pl.BlockSpec((N_CHIPS, tr, cols), lambda i, idx: (0, i, 0)),
                      pl.BlockSpec((None, tr, cols), lambda i, idx: (idx[1], i, 0))],
            out_specs=pl.BlockSpec((tr, cols), lambda i, idx: (idx[2] * nb + i, 0))),
        out_shape=jax.ShapeDtypeStruct((2 * rh, cols), F32), compiler_params=_params("parallel"),
    )(dev_idx, land, partial)


def _share_halves(totals):
    n = len(totals)

    def body(*refs):
        ins, outs = refs[:n], refs[n:2 * n]
        send, recv = refs[2 * n:]
        x, y, c = _position()
        cps = []
        for t in range(n):
            rh = ins[t].shape[0] // 2
            mine = pl.ds(c * rh, rh)
            cp = pltpu.make_async_remote_copy(
                src_ref=ins[t].at[mine], dst_ref=outs[t].at[mine], send_sem=send.at[t], recv_sem=recv.at[t],
                device_id=(x, y, 1 - c), device_id_type=MESH)
            cp.start()
            cps.append(cp)
        for cp in cps:
            cp.wait()

    return pl.pallas_call(
        body, name="reduce_share_" + "_".join(str(t.shape[1]) for t in totals), in_specs=[ANY_SPEC] * n,
        out_specs=[ANY_SPEC] * n, out_shape=[jax.ShapeDtypeStruct(t.shape, F32) for t in totals],
        input_output_aliases={t: t for t in range(n)},
        scratch_shapes=[pltpu.SemaphoreType.DMA((n,)), pltpu.SemaphoreType.DMA((n,))],
    )(*totals)


def _exchange_halves(grads):
    n = len(grads)
    hbm = pl.BlockSpec(memory_space=pl.ANY)

    def body(*refs):
        ins, outs = refs[:n], refs[n:2 * n]
        send, recv = refs[2 * n:]
        x, y, c = _position()
        cps = []
        for t in range(n):
            rh = ins[t].shape[1] // 2
            cp = pltpu.make_async_remote_copy(
                src_ref=ins[t].at[pl.ds(0, N_CHIPS), pl.ds((1 - c) * rh, rh)], dst_ref=outs[t], send_sem=send.at[t],
                recv_sem=recv.at[t], device_id=(x, y, 1 - c), device_id_type=MESH)
            cp.start()
            cps.append(cp)
        for cp in cps:
            cp.wait()

    return pl.pallas_call(
        body, name="reduce_exchange_halves", in_specs=[hbm] * n, out_specs=[hbm] * n,
        out_shape=[jax.ShapeDtypeStruct((g.shape[0], g.shape[1] // 2, g.shape[2]), BF16) for g in grads],
        scratch_shapes=[pltpu.SemaphoreType.DMA((n,)), pltpu.SemaphoreType.DMA((n,))],
    )(*grads)


def _add_halves(grad, got, c_idx, name):
    j, r, cols = grad.shape
    rh = r // 2
    tr = 128
    nb = rh // tr

    def body(c_ref, g_ref, o_ref_in, out_ref):
        out_ref[...] = (g_ref[...].astype(F32) + o_ref_in[...].astype(F32)).astype(BF16)

    return pl.pallas_call(
        body, name=name,
        grid_spec=pltpu.PrefetchScalarGridSpec(
            num_scalar_prefetch=1, grid=(j, nb),
            in_specs=[pl.BlockSpec((None, tr, cols), lambda jj, i, c_ref: (jj, c_ref[0] * nb + i, 0)),
                      pl.BlockSpec((None, tr, cols), lambda jj, i, c_ref: (jj, i, 0))],
            out_specs=pl.BlockSpec((None, tr, cols), lambda jj, i, c_ref: (jj, i, 0))),
        out_shape=jax.ShapeDtypeStruct((j, rh, cols), BF16),
        compiler_params=_params("parallel", "parallel"),
    )(c_idx, grad, got)


def _scatter_partials(partials):
    n = len(partials)
    hbm = pl.BlockSpec(memory_space=pl.ANY)

    def body(*refs):
        ins, outs = refs[:n], refs[n:2 * n]
        send, recv, local = refs[2 * n:]
        x, y, c = _position()
        chip = 2 * x + y
        cps, lcs = [], []
        for t in range(n):
            lc = pltpu.make_async_copy(ins[t].at[chip], outs[t].at[chip], local.at[t])
            lc.start()
            lcs.append(lc)
            for k in range(1, N_CHIPS):
                px, py = _chip_peer(x, y, k)
                s = 3 * t + k - 1
                cp = pltpu.make_async_remote_copy(
                    src_ref=ins[t].at[2 * px + py], dst_ref=outs[t].at[chip], send_sem=send.at[s],
                    recv_sem=recv.at[s], device_id=(px, py, c), device_id_type=MESH)
                cp.start()
                cps.append(cp)
        for cp in cps:
            cp.wait()
        for lc in lcs:
            lc.wait()

    return pl.pallas_call(
        body, name="reduce_scatter_partials", in_specs=[hbm] * n, out_specs=[hbm] * n,
        out_shape=[jax.ShapeDtypeStruct(p.shape, BF16) for p in partials],
        scratch_shapes=[pltpu.SemaphoreType.DMA((3 * n,)), pltpu.SemaphoreType.DMA((3 * n,)),
                        pltpu.SemaphoreType.DMA((n,))],
    )(*partials)


def _sum_chips(parts, name):
    j, rh, cols = parts.shape
    tr = 128

    def body(p_ref, o_ref):
        acc = p_ref[0].astype(F32)
        for s in range(1, j):
            acc = acc + p_ref[s].astype(F32)
        o_ref[...] = acc

    return pl.pallas_call(
        body, name=name, grid=(rh // tr,),
        in_specs=[pl.BlockSpec((j, tr, cols), lambda i: (0, i, 0))],
        out_specs=pl.BlockSpec((tr, cols), lambda i: (i, 0)),
        out_shape=jax.ShapeDtypeStruct((rh, cols), F32),
        compiler_params=_params("parallel"),
    )(parts)


def _share_totals(halves):
    n = len(halves)
    hbm = pl.BlockSpec(memory_space=pl.ANY)

    def body(*refs):
        ins, outs = refs[:n], refs[n:2 * n]
        send, recv, local = refs[2 * n:]
        x, y, c = _position()
        cps, lcs = [], []
        for t in range(n):
            rh = ins[t].shape[0]
            mine = outs[t].at[pl.ds(c * rh, rh)]
            lc = pltpu.make_async_copy(ins[t], mine, local.at[t])
            lc.start()
            lcs.append(lc)
            cp = pltpu.make_async_remote_copy(
                src_ref=ins[t], dst_ref=mine, send_sem=send.at[t], recv_sem=recv.at[t],
                device_id=(x, y, 1 - c), device_id_type=MESH)
            cp.start()
            cps.append(cp)
        for cp in cps:
            cp.wait()
        for lc in lcs:
            lc.wait()

    return pl.pallas_call(
        body, name="reduce_share_totals", in_specs=[hbm] * n, out_specs=[hbm] * n,
        out_shape=[jax.ShapeDtypeStruct((2 * h.shape[0], h.shape[1]), F32) for h in halves],
        scratch_shapes=[pltpu.SemaphoreType.DMA((n,)), pltpu.SemaphoreType.DMA((n,)),
                        pltpu.SemaphoreType.DMA((n,))],
    )(*halves)


SMALL_ROWS = 56


def _small_copies(refs, send, recv, receiving):
    x, y, c = _position()
    me = 4 * x + 2 * y + c
    cps = []
    for k in range(1, N_DEV):
        px, py, pc = _xor_peer(x, y, c, k)
        cps.append(pltpu.make_async_remote_copy(
            src_ref=refs[0], dst_ref=refs[1].at[4 * px + 2 * py + pc if receiving else me],
            send_sem=send.at[k - 1], recv_sem=recv.at[k - 1], device_id=(px, py, pc), device_id_type=MESH))
    return cps


def _small_gather_start(packed, after):
    land = lax.empty((N_DEV,) + packed.shape, F32)

    def issue(refs, send, recv):
        for cp in _small_copies(refs, send, recv, False):
            cp.start()

    return _split_start("small_gather_start", [packed, land], N_DEV - 1, after, issue)


def _small_gather_wait(send, recv, arrays, after):
    def await_all(refs, send_ref, recv_ref):
        for cp in _small_copies(refs, send_ref, recv_ref, True):
            cp.wait_send()
            cp.wait_recv()

    return _split_wait("small_gather_wait", arrays, send, recv, after, await_all)


def _reduce_small(packed, land, silu_c):
    ns = 3 * D_MODEL // N_CHIPS

    def body(p_ref, land_ref, sc_ref, tot_ref, gw_ref, loss_ref, qk_ref, allp):
        x, y, c = _position()
        me = 4 * x + 2 * y + c
        chip = 2 * x + y
        for i in range(N_DEV):
            allp[i] = jnp.where(me == i, p_ref[...], land_ref[i])
        tot = allp[0]
        for i in range(1, N_DEV):
            tot = tot + allp[i]
        tot_ref[...] = tot
        loss_ref[...] = jnp.sum(tot[11:12, :], axis=1, keepdims=True) * (0.5 / D_MODEL)
        fold = tot[5:11, 0:HEAD_DIM]
        for h in range(1, N_HEADS):
            fold = fold + tot[5:11, h * HEAD_DIM:(h + 1) * HEAD_DIM]
        qk_ref[...] = jnp.concatenate([fold, jnp.zeros((2, HEAD_DIM), F32)], axis=0)
        sct = sc_ref[...].T
        rc = 64
        for l in range(2):
            dms = [allp[i, pl.ds(12 + 4 * l + chip, 1), :][:, :ns] for i in range(N_DEV)]
            for r0 in range(0, D_MODEL, rc):
                acc = sct[r0:r0 + rc, 0:1] * dms[0]
                for i in range(1, N_DEV):
                    acc = acc + sct[r0:r0 + rc, i:i + 1] * dms[i]
                gw_ref[l, r0:r0 + rc, :] = acc

    vm = pl.BlockSpec(memory_space=pltpu.VMEM)
    return pl.pallas_call(
        body, name="reduce_small", in_specs=[vm, vm, vm], out_specs=[vm] * 4,
        out_shape=[jax.ShapeDtypeStruct((SMALL_ROWS, D_MODEL), F32), jax.ShapeDtypeStruct((2, D_MODEL, ns), F32),
                   jax.ShapeDtypeStruct((1, 1), F32), jax.ShapeDtypeStruct((8, HEAD_DIM), F32)],
        scratch_shapes=[pltpu.VMEM((N_DEV, SMALL_ROWS, D_MODEL), F32)],
        compiler_params=pltpu.CompilerParams(vmem_limit_bytes=VMEM_LIMIT_BYTES),
    )(packed, land, silu_c)


def _reduce_big(grads, c_idx):
    names = list(grads)
    got = _exchange_halves([grads[k] for k in names])
    partials = [_add_halves(grads[k], got[i], c_idx, f"reduce_add_{k}") for i, k in enumerate(names)]
    parts = _scatter_partials(partials)
    halves = [_sum_chips(parts[i], f"reduce_sum_{k}") for i, k in enumerate(names)]
    totals = _share_totals(halves)
    return dict(zip(names, totals))


def kernel(x, c, norm_g, ada_w, ada_b, a_w_in, a_conv_w, a_conv_b, a_ln_g, a_ln_b, a_w_out, b_w_in, b_q_norm, b_k_norm, b_w_out, loss_target, m_norm_g, m_ada_w, m_ada_b, m_a_w_in, m_a_conv_w, m_a_conv_b, m_a_ln_g, m_a_ln_b, m_a_w_out, m_b_w_in, m_b_q_norm, m_b_k_norm, m_b_w_out, v_norm_g, v_ada_w, v_ada_b, v_a_w_in, v_a_conv_w, v_a_conv_b, v_a_ln_g, v_a_ln_b, v_a_w_out, v_b_w_in, v_b_q_norm, v_b_k_norm, v_b_w_out):
    chip = 2 * lax.axis_index("x") + lax.axis_index("y")
    core = lax.axis_index("c")
    chip_idx = chip.astype(jnp.int32).reshape(1)
    dev_idx = jnp.stack([2 * chip + core, chip, core]).astype(jnp.int32)

    mods, silu_c, conv_w_full = _ada_forward(c, ada_w, ada_b, a_conv_w[0])
    lands_a = [_cast_into_slot(a_w_in[0], chip_idx, "cast_a_w_in"), _cast_into_slot(a_w_out[0], chip_idx, "cast_a_w_out")]
    send_a, recv_a, lands_a, token_a = _gather_start(lands_a, mods, "gather_start_a")
    lands_b = [_cast_into_slot(b_w_in[0], chip_idx, "cast_b_w_in"), _cast_into_slot(b_w_out[0], chip_idx, "cast_b_w_out")]
    send_b, recv_b, lands_b, token_b = _gather_start(lands_b, token_a, "gather_start_b")
    mods = mods + token_b[0:2, 0:1]

    def weights_a(after):
        send, recv, lands, _ = _gather_forward(send_a, recv_a, lands_a, after, "gather_forward_a")
        w_in, w_out = _gather_wait(send, recv, lands, after, "gather_wait_a")
        return w_in, w_out.reshape(D_MODEL, D_MODEL)

    forwarded_b = []

    def weights_b(after):
        send, recv, lands, _ = forwarded_b
        w_in, w_out = _gather_wait(send, recv, lands, after, "gather_wait_b")
        return w_in, w_out.reshape(D_MODEL, D_MODEL)

    def forward_weights_b(after):
        forwarded_b.extend(_gather_forward(send_b, recv_b, lands_b, after, "gather_forward_b"))
        return forwarded_b[3]

    stage1, stage2 = {}, {}

    def send_grads(tag, dw_in, dw_out):
        grads = [dw_in, dw_out.reshape(N_CHIPS, D_MODEL // N_CHIPS, D_MODEL)]
        send, recv, arrays, token = _reduce_sibling_start(grads, dw_out, f"reduce_d2d_start_{tag}")
        stage1[tag] = (send, recv, arrays)
        return token

    def forward_grads(tag, after):
        send, recv, arrays = stage1[tag]
        grads, got = _reduce_sibling_wait(send, recv, arrays, after, f"reduce_d2d_wait_{tag}")
        partials = [_add_sibling_half(grads[i], got[i], dev_idx, f"reduce_add_{tag}_{i}") for i in range(2)]
        send, recv, arrays, token = _reduce_chips_start(partials, partials[1], f"reduce_ici_start_{tag}")
        stage2[tag] = (send, recv, arrays)
        return token

    def finish_grads(tag, after):
        send, recv, arrays = stage2[tag]
        partials, lands = _reduce_chips_wait(send, recv, arrays, after, f"reduce_ici_wait_{tag}")
        totals = [_sum_partials(lands[i], partials[i], dev_idx, f"reduce_sum_{tag}_{i}") for i in range(2)]
        return _share_halves(totals)

    grad_x, small = _local_step(
        x[0], loss_target[0], mods.reshape(2, 3, D_MODEL), norm_g, conv_w_full, a_conv_b, a_ln_g[0:1],
        a_ln_b[0:1], b_q_norm[0], b_k_norm[0], weights_a, weights_b, forward_weights_b,
        functools.partial(send_grads, "b"), functools.partial(forward_grads, "b"), functools.partial(send_grads, "a"))

    ns = 3 * D_MODEL // N_CHIPS
    pad_mod = lambda dm: jnp.pad(dm.reshape(N_CHIPS, ns), ((0, 0), (0, D_MODEL - ns)))
    packed = jnp.concatenate([
        small["dnorm_g"], small["dconv_b"], small["dln_g"], small["dln_b"], small["dq_norm"], small["dk_norm"],
        small["loss_cols"], pad_mod(small["dmod0"]), pad_mod(small["dmod1"]), small["dconv_w"],
        jnp.zeros((SMALL_ROWS - 20 - CONV_WIDTH, D_MODEL), F32)], axis=0)
    send_s, recv_s, small_arrays, token_s = _small_gather_start(packed, packed)

    given = dict(norm_g=(norm_g, m_norm_g, v_norm_g), ada_w=(ada_w, m_ada_w, v_ada_w), ada_b=(ada_b, m_ada_b, v_ada_b),
                 a_w_in=(a_w_in, m_a_w_in, v_a_w_in), a_conv_w=(a_conv_w, m_a_conv_w, v_a_conv_w),
                 a_conv_b=(a_conv_b, m_a_conv_b, v_a_conv_b), a_ln_g=(a_ln_g, m_a_ln_g, v_a_ln_g),
                 a_ln_b=(a_ln_b, m_a_ln_b, v_a_ln_b), a_w_out=(a_w_out, m_a_w_out, v_a_w_out),
                 b_w_in=(b_w_in, m_b_w_in, v_b_w_in), b_q_norm=(b_q_norm, m_b_q_norm, v_b_q_norm),
                 b_k_norm=(b_k_norm, m_b_k_norm, v_b_k_norm), b_w_out=(b_w_out, m_b_w_out, v_b_w_out))
    order = ["norm_g", "ada_w", "ada_b", "a_w_in", "a_conv_w", "a_conv_b", "a_ln_g", "a_ln_b", "a_w_out", "b_w_in",
             "b_q_norm", "b_k_norm", "b_w_out"]
    outs = {}

    def update(k, g2):
        w, m, v = given[k]
        shape2 = g2.shape
        d2, m2, v2 = _adamw(w.reshape(shape2), g2, m.reshape(shape2), v.reshape(shape2), f"adamw_{k}")
        outs[k] = tuple(a.reshape(w.shape) for a in (g2, d2, m2, v2))

    token = forward_grads("a", token_s)
    g_b_in, g_b_out = finish_grads("b", token)
    update("b_w_in", g_b_in)
    update("b_w_out", g_b_out)
    packed, land = _small_gather_wait(send_s, recv_s, small_arrays, outs["b_w_in"][1])
    tot, g_ada_w, loss, qk = _reduce_small(packed, land, silu_c)
    cw = D_MODEL // N_CHIPS
    g_small = dict(
        norm_g=tot[0:2], a_conv_b=tot[2:3], a_ln_g=tot[3:4], a_ln_b=tot[4:5],
        b_q_norm=qk[0:3], b_k_norm=qk[3:6],
        ada_b=jnp.stack([tot[12:16, :ns].reshape(3 * D_MODEL), tot[16:20, :ns].reshape(3 * D_MODEL)]),
        a_conv_w=lax.dynamic_slice(tot[20:20 + CONV_WIDTH], (0, chip * cw), (CONV_WIDTH, cw)),
    )
    update("ada_w", g_ada_w.reshape(2 * D_MODEL, ns))
    for k, g2 in g_small.items():
        update(k, g2)
    g_a_in, g_a_out = finish_grads("a", outs["ada_w"][1])
    update("a_w_in", g_a_in)
    update("a_w_out", g_a_out)
    return (loss.reshape(()), grad_x[None], *[outs[k][0] for k in order], *[outs[k][1] for k in order],
            *[outs[k][2] for k in order], *[outs[k][3] for k in order])
```

```python
import functools

import jax
import jax.numpy as jnp
from jax import lax
from jax.experimental import pallas as pl
from jax.experimental.pallas import tpu as pltpu

F32 = jnp.float32
BF16 = jnp.bfloat16

SEQ = 2048
D_MODEL = 1024
CONV_WIDTH = 31
HEAD_DIM = 64
N_HEADS = 16
DILATIONS = (1, 4, 16)
ATTN_BLOCK = 128
NORM_EPS = 1e-6
NEG_INF = -1e30
N_DEV = 8
N_CHIPS = 4

ADAM_LR = 0.001
ADAM_B1 = 0.9
ADAM_B2 = 0.999
ADAM_EPS = 1e-08
ADAM_WD = 0.01
ADAM_STEP = 10

VMEM_LIMIT_BYTES = 52 * 1024 * 1024
HALO = 32
LANES = 128
MESH = pl.DeviceIdType.MESH


def _params(*sem):
    return pltpu.CompilerParams(dimension_semantics=sem or None, vmem_limit_bytes=VMEM_LIMIT_BYTES)


def _sigmoid(v):
    return 1.0 / (1.0 + jnp.exp(-v))


def _row_spec(tm, cols, col_block=0):
    return pl.BlockSpec((tm, cols), lambda i: (i, col_block))


def _vec_spec(rows, cols):
    return pl.BlockSpec((rows, cols), lambda i: (0, 0))


def _normmod(xv, g, scale, shift):
    r = lax.rsqrt(jnp.mean(xv * xv, axis=-1, keepdims=True) + NORM_EPS)
    return xv * r * g * (1.0 + scale) + shift


def _normmod_fwd(x, g, scale, shift, name):
    tm = 256

    def body(x_ref, g_ref, sc_ref, sh_ref, h_ref, ht_ref):
        h = _normmod(x_ref[...], g_ref[...], sc_ref[...], sh_ref[...])
        h_ref[...] = h.astype(BF16)
        ht_ref[...] = h.T.astype(BF16)

    return pl.pallas_call(
        body, name=name, grid=(SEQ // tm,),
        in_specs=[_row_spec(tm, D_MODEL)] + [_vec_spec(1, D_MODEL)] * 3,
        out_specs=[_row_spec(tm, D_MODEL), pl.BlockSpec((D_MODEL, tm), lambda i: (0, i))],
        out_shape=[jax.ShapeDtypeStruct((SEQ, D_MODEL), BF16), jax.ShapeDtypeStruct((D_MODEL, SEQ), BF16)],
        compiler_params=_params("parallel"),
    )(x, g, scale, shift)


def _normmod_bwd(x, g, scale, dh_parts, dres, name, part_dilations=None, gated=None):
    tm = 256
    n_parts = len(dh_parts)
    dils = part_dilations or (1,) * n_parts
    dh_parts = [p if d == 1 else p.reshape(d, SEQ // d, D_MODEL) for p, d in zip(dh_parts, dils)]
    n_gated = 0 if gated is None else 2

    def body(x_ref, g_ref, sc_ref, dres_ref, *rest):
        part_refs = rest[:n_parts]
        gated_refs = rest[n_parts:n_parts + n_gated]
        out_refs = rest[n_parts + n_gated:]
        dx_ref, sums_ref, nat = out_refs[0], out_refs[1], out_refs[-1]
        xv = x_ref[...]
        r = lax.rsqrt(jnp.mean(xv * xv, axis=-1, keepdims=True) + NORM_EPS)
        xn = xv * r
        dh = _load_natural(part_refs[0], nat, dils[0])
        for p, d in zip(part_refs[1:], dils[1:]):
            dh = dh + _load_natural(p, nat, d)
        gv = g_ref[...]
        one_sc = 1.0 + sc_ref[...]
        dxn = dh * (gv * one_sc)
        dx = dres_ref[...] + r * (dxn - xn * jnp.mean(dxn * xn, axis=-1, keepdims=True))
        dx_ref[...] = dx
        dhx = dh * xn
        rows = [jnp.sum(dhx, axis=0, keepdims=True) * one_sc,
                jnp.sum(dhx, axis=0, keepdims=True) * gv,
                jnp.sum(dh, axis=0, keepdims=True)]
        if gated is not None:
            gate_ref, y_ref = gated_refs
            out_refs[2][...] = (dx * gate_ref[...]).astype(BF16)
            rows.append(jnp.sum(dx * y_ref[...], axis=0, keepdims=True))
        sums = jnp.concatenate(rows + [jnp.zeros((8 - len(rows), D_MODEL), F32)], axis=0)

        @pl.when(pl.program_id(0) == 0)
        def _():
            sums_ref[...] = jnp.zeros_like(sums_ref)

        sums_ref[...] += sums

    gated_specs = [] if gated is None else [_vec_spec(1, D_MODEL), _row_spec(tm, D_MODEL)]
    dy_spec = [] if gated is None else [_row_spec(tm, D_MODEL)]
    dy_shape = [] if gated is None else [jax.ShapeDtypeStruct((SEQ, D_MODEL), BF16)]
    return pl.pallas_call(
        body, name=name, grid=(SEQ // tm,),
        in_specs=[_row_spec(tm, D_MODEL), _vec_spec(1, D_MODEL), _vec_spec(1, D_MODEL), _row_spec(tm, D_MODEL)]
        + [_class_spec(tm, d) for d in dils] + gated_specs,
        out_specs=[_row_spec(tm, D_MODEL), _vec_spec(8, D_MODEL)] + dy_spec,
        out_shape=[jax.ShapeDtypeStruct((SEQ, D_MODEL), F32), jax.ShapeDtypeStruct((8, D_MODEL), F32)] + dy_shape,
        scratch_shapes=[_natural_scratch(tm)],
        compiler_params=_params("arbitrary"),
    )(x, g, scale, dres, *dh_parts, *(gated or ()))


def _mm(lhs, rhs, *, tn, tile0, n_tiles, out_dtype, name, out3d=None, prev=None):
    mo, kc = lhs.shape
    cm = min(mo, 1024)

    def body(l_ref, r_ref, *rest):
        o_ref = rest[-1]
        for m in range(mo // cm):
            rows = pl.ds(m * cm, cm)
            o_ref[rows, :] = jnp.dot(l_ref[rows, :], r_ref[...], preferred_element_type=F32).astype(out_dtype)

    if rhs.ndim == 3:
        tps_r = rhs.shape[2] // tn
        r_spec = pl.BlockSpec((None, kc, tn), lambda t: ((tile0 + t) // tps_r, 0, (tile0 + t) % tps_r))
    else:
        r_spec = pl.BlockSpec((kc, tn), lambda t: (0, t))
    in_specs = [pl.BlockSpec((mo, kc), lambda t: (0, 0)), r_spec]
    args = [lhs, rhs]
    aliases = {}
    if out3d is None:
        o_spec = pl.BlockSpec((mo, tn), lambda t: (0, t))
        o_shape = jax.ShapeDtypeStruct((mo, n_tiles * tn), out_dtype)
    else:
        j_out, ns_out = out3d
        tps_o = ns_out // tn
        o_spec = pl.BlockSpec((None, mo, tn), lambda t: ((tile0 + t) // tps_o, 0, (tile0 + t) % tps_o))
        o_shape = jax.ShapeDtypeStruct((j_out, mo, ns_out), out_dtype)
        if prev is not None:
            in_specs.append(pl.BlockSpec(memory_space=pl.ANY))
            args.append(prev)
            aliases = {2: 0}
    return pl.pallas_call(
        body, name=name, grid=(n_tiles,), in_specs=in_specs, out_specs=o_spec, out_shape=o_shape,
        input_output_aliases=aliases, compiler_params=_params("parallel"),
    )(*args)


def _mm_nt(dy, w3, *, tn, tile0, n_tiles, name, after=None):
    m_rows = dy.shape[0]
    _, kc, ns = w3.shape
    tps = ns // tn
    cm = 512
    extra = [] if after is None else [after]

    def body(dy_ref, w_ref, *rest):
        o_ref = rest[-1]

        @pl.when(pl.program_id(0) == 0)
        def _():
            o_ref[...] = jnp.zeros_like(o_ref)

        for m in range(m_rows // cm):
            rows = pl.ds(m * cm, cm)
            o_ref[rows, :] += lax.dot_general(dy_ref[rows, :], w_ref[...], (((1,), (1,)), ((), ())),
                                              preferred_element_type=F32)

    return pl.pallas_call(
        body, name=name, grid=(n_tiles,),
        in_specs=[pl.BlockSpec((m_rows, tn), lambda t: (0, t)),
                  pl.BlockSpec((None, kc, tn), lambda t: ((tile0 + t) // tps, 0, (tile0 + t) % tps))]
        + [pl.BlockSpec(memory_space=pl.ANY)] * len(extra),
        out_specs=pl.BlockSpec((m_rows, kc), lambda t: (0, 0)),
        out_shape=jax.ShapeDtypeStruct((m_rows, kc), F32),
        compiler_params=_params("arbitrary"),
    )(dy, w3, *extra)


CONV_CHUNK = 16


def _shift_copies(buf, shifted):
    rows = shifted.shape[1]
    for s in range(1, 8):
        shifted[s - 1] = buf[pl.ds(s, rows), :]


def _shifted_rows(buf, shifted, offset, r0):
    s = offset % 8
    if s == 0:
        return buf[pl.ds(r0 + offset, CONV_CHUNK), :]
    return shifted[s - 1, pl.ds(r0 + (offset - s), CONV_CHUNK), :]


def _spread_taps(w_ref, taps):
    for k in range(CONV_WIDTH):
        taps[k] = jnp.broadcast_to(w_ref[k:k + 1, :], (8, D_MODEL))


def _times_tap(taps, k, rows):
    return (rows.reshape(CONV_CHUNK // 8, 8, D_MODEL) * taps[k][None]).reshape(CONV_CHUNK, D_MODEL)


def _conv_fwd(proj, conv_w, conv_b, ln_g, ln_b, name):
    tm = 256
    hb = tm // HALO

    def body(vg_ref, halo_ref, z_ref, w_ref, b_ref, g_ref, be_ref, u5_ref, u5t_ref, u2_ref, buf, shifted, taps):
        i = pl.program_id(0)
        u1 = vg_ref[:, :D_MODEL] * _sigmoid(vg_ref[:, D_MODEL:])
        u1h = halo_ref[:, :D_MODEL] * _sigmoid(halo_ref[:, D_MODEL:])
        buf[pl.ds(0, HALO), :] = jnp.where(i > 0, u1h, 0.0)
        buf[pl.ds(HALO, tm), :] = u1
        _shift_copies(buf, shifted)
        _spread_taps(w_ref, taps)

        def chunk(ci, carry):
            r0 = pl.multiple_of(ci * CONV_CHUNK, CONV_CHUNK)
            acc = jnp.broadcast_to(b_ref[...], (CONV_CHUNK, D_MODEL))
            for k in range(CONV_WIDTH):
                acc = acc + _times_tap(taps, k, _shifted_rows(buf, shifted, HALO - (CONV_WIDTH - 1) + k, r0))
            u2_ref[pl.ds(r0, CONV_CHUNK), :] = acc
            return carry

        lax.fori_loop(0, tm // CONV_CHUNK, chunk, 0)
        acc = u2_ref[...]
        mu = jnp.mean(acc, axis=-1, keepdims=True)
        xc = acc - mu
        rstd = lax.rsqrt(jnp.mean(xc * xc, axis=-1, keepdims=True) + NORM_EPS)
        u3 = xc * rstd * g_ref[...] + be_ref[...]
        zv = z_ref[...]
        u5 = u3 * _sigmoid(u3) * (zv * _sigmoid(zv))
        u5_ref[...] = u5.astype(BF16)
        u5t_ref[...] = u5.T.astype(BF16)

    return pl.pallas_call(
        body, name=name, grid=(SEQ // tm,),
        in_specs=[pl.BlockSpec((tm, 2 * D_MODEL), lambda i: (i, 0)),
                  pl.BlockSpec((HALO, 2 * D_MODEL), lambda i: (jnp.maximum(i * hb - 1, 0), 0)),
                  _row_spec(tm, D_MODEL, 2),
                  _vec_spec(CONV_WIDTH, D_MODEL)] + [_vec_spec(1, D_MODEL)] * 3,
        out_specs=[_row_spec(tm, D_MODEL), pl.BlockSpec((D_MODEL, tm), lambda i: (0, i)), _row_spec(tm, D_MODEL)],
        out_shape=[jax.ShapeDtypeStruct((SEQ, D_MODEL), BF16), jax.ShapeDtypeStruct((D_MODEL, SEQ), BF16),
                   jax.ShapeDtypeStruct((SEQ, D_MODEL), F32)],
        scratch_shapes=[pltpu.VMEM((HALO + tm, D_MODEL), F32), pltpu.VMEM((7, HALO + tm - 8, D_MODEL), F32),
                        pltpu.VMEM((CONV_WIDTH, 8, D_MODEL), F32)],
        compiler_params=_params("parallel"),
    )(proj, proj, proj, conv_w, conv_b, ln_g, ln_b)


def _conv_bwd_pointwise(dy, w_out, proj, u2, ln_g, ln_b, name):
    tm = 256

    def body(dy_ref, w_ref, z_ref, u2_ref, g_ref, be_ref, du2_ref, dz_ref, sums_ref):
        u2v = u2_ref[...]
        mu = jnp.mean(u2v, axis=-1, keepdims=True)
        xc = u2v - mu
        rstd = lax.rsqrt(jnp.mean(xc * xc, axis=-1, keepdims=True) + NORM_EPS)
        xhat = xc * rstd
        u3 = xhat * g_ref[...] + be_ref[...]
        s3 = _sigmoid(u3)
        u4 = u3 * s3
        zv = z_ref[...]
        sz = _sigmoid(zv)
        du5v = lax.dot_general(dy_ref[...], w_ref[...], NT_DIMS, preferred_element_type=F32)
        dz_ref[...] = du5v * u4 * (sz * (1.0 + zv * (1.0 - sz)))
        du3 = du5v * (zv * sz) * (s3 * (1.0 + u3 * (1.0 - s3)))
        dxhat = du3 * g_ref[...]
        du2 = rstd * (dxhat - jnp.mean(dxhat, axis=-1, keepdims=True)
                      - xhat * jnp.mean(dxhat * xhat, axis=-1, keepdims=True))
        du2_ref[...] = du2
        sums = jnp.concatenate([
            jnp.sum(du3 * xhat, axis=0, keepdims=True),
            jnp.sum(du3, axis=0, keepdims=True),
            jnp.sum(du2, axis=0, keepdims=True),
            jnp.zeros((5, D_MODEL), F32)], axis=0)

        @pl.when(pl.program_id(0) == 0)
        def _():
            sums_ref[...] = jnp.zeros_like(sums_ref)

        sums_ref[...] += sums

    return pl.pallas_call(
        body, name=name, grid=(SEQ // tm,),
        in_specs=[_row_spec(tm, D_MODEL), _vec_spec(D_MODEL, D_MODEL), _row_spec(tm, D_MODEL, 2),
                  _row_spec(tm, D_MODEL), _vec_spec(1, D_MODEL), _vec_spec(1, D_MODEL)],
        out_specs=[_row_spec(tm, D_MODEL), _row_spec(tm, D_MODEL), _vec_spec(8, D_MODEL)],
        out_shape=[jax.ShapeDtypeStruct((SEQ, D_MODEL), F32), jax.ShapeDtypeStruct((SEQ, D_MODEL), F32),
                   jax.ShapeDtypeStruct((8, D_MODEL), F32)],
        compiler_params=_params("arbitrary"),
    )(dy, w_out, proj, u2, ln_g, ln_b)


def _conv_bwd_taps(du2, dz, proj, conv_w, name):
    tm = 256
    hb = tm // HALO
    n_blocks = SEQ // tm

    def body(du2_ref, dnext_ref, dz_ref, vg_ref, w_ref, dproj_ref, dw_ref, dbuf, dshift, sgbuf, ubuf, dwacc, taps):
        i = pl.program_id(0)
        _spread_taps(w_ref, taps)
        sg = _sigmoid(vg_ref[:, D_MODEL:])
        sgbuf[...] = sg
        ubuf[...] = vg_ref[:, :D_MODEL] * sg
        dbuf[pl.ds(0, tm), :] = du2_ref[...]
        dbuf[pl.ds(tm, HALO), :] = jnp.where(i < n_blocks - 1, dnext_ref[...], 0.0)
        _shift_copies(dbuf, dshift)

        @pl.when(i == 0)
        def _():
            dwacc[...] = jnp.zeros_like(dwacc)

        def chunk(ci, carry):
            r0 = pl.multiple_of(ci * CONV_CHUNK, CONV_CHUNK)
            rows = pl.ds(r0, CONV_CHUNK)
            u1c = ubuf[rows, :]
            du1 = jnp.zeros((CONV_CHUNK, D_MODEL), F32)
            for k in range(CONV_WIDTH):
                ahead = _shifted_rows(dbuf, dshift, CONV_WIDTH - 1 - k, r0)
                du1 = du1 + _times_tap(taps, k, ahead)
                prod = u1c * ahead
                dwacc[k] += prod[0:8] + prod[8:16]
            sgc = sgbuf[rows, :]
            dval = du1 * sgc
            dproj_ref[rows, 0:D_MODEL] = dval.astype(BF16)
            dproj_ref[rows, D_MODEL:2 * D_MODEL] = (dval * vg_ref[rows, 0:D_MODEL] * (1.0 - sgc)).astype(BF16)
            return carry

        lax.fori_loop(0, tm // CONV_CHUNK, chunk, 0)
        dproj_ref[:, 2 * D_MODEL:] = dz_ref[...].astype(BF16)

        @pl.when(i == n_blocks - 1)
        def _():
            for k in range(CONV_WIDTH):
                dw_ref[k:k + 1, :] = jnp.sum(dwacc[k], axis=0, keepdims=True)
            dw_ref[CONV_WIDTH:, :] = jnp.zeros((32 - CONV_WIDTH, D_MODEL), F32)

    return pl.pallas_call(
        body, name=name, grid=(n_blocks,),
        in_specs=[_row_spec(tm, D_MODEL),
                  pl.BlockSpec((HALO, D_MODEL), lambda i: (jnp.minimum((i + 1) * hb, SEQ // HALO - 1), 0)),
                  _row_spec(tm, D_MODEL),
                  pl.BlockSpec((tm, 2 * D_MODEL), lambda i: (i, 0)),
                  _vec_spec(CONV_WIDTH, D_MODEL)],
        out_specs=[_row_spec(tm, 3 * D_MODEL), _vec_spec(32, D_MODEL)],
        out_shape=[jax.ShapeDtypeStruct((SEQ, 3 * D_MODEL), BF16), jax.ShapeDtypeStruct((32, D_MODEL), F32)],
        scratch_shapes=[pltpu.VMEM((tm + HALO, D_MODEL), F32), pltpu.VMEM((7, HALO + tm - 8, D_MODEL), F32),
                        pltpu.VMEM((tm, D_MODEL), F32), pltpu.VMEM((tm, D_MODEL), F32),
                        pltpu.VMEM((CONV_WIDTH, 8, D_MODEL), F32), pltpu.VMEM((CONV_WIDTH, 8, D_MODEL), F32)],
        compiler_params=_params("arbitrary"),
    )(du2, du2, dz, proj, conv_w)


def _out_a(u5, w_out, x, gate, g1, scale1, shift1, name):
    tm = 256
    n_d = len(DILATIONS)

    def body(u_ref, w_ref, x_ref, gate_ref, g_ref, sc_ref, sh_ref, x1_ref, y_ref, ht_ref, *rest):
        h_refs, nat = rest[:n_d], rest[-1]
        y = jnp.dot(u_ref[...], w_ref[...], preferred_element_type=F32)
        x1 = x_ref[...] + gate_ref[...] * y
        y_ref[...] = y
        x1_ref[...] = x1
        h = _normmod(x1, g_ref[...], sc_ref[...], sh_ref[...])
        ht_ref[...] = h.T.astype(BF16)
        for h_ref, d in zip(h_refs, DILATIONS):
            _store_classes(h_ref, h, nat, d)

    res = pl.pallas_call(
        body, name=name, grid=(SEQ // tm,),
        in_specs=[_row_spec(tm, D_MODEL), _vec_spec(D_MODEL, D_MODEL), _row_spec(tm, D_MODEL)]
        + [_vec_spec(1, D_MODEL)] * 4,
        out_specs=[_row_spec(tm, D_MODEL), _row_spec(tm, D_MODEL), pl.BlockSpec((D_MODEL, tm), lambda i: (0, i))]
        + [_class_spec(tm, d) for d in DILATIONS],
        out_shape=[jax.ShapeDtypeStruct((SEQ, D_MODEL), F32), jax.ShapeDtypeStruct((SEQ, D_MODEL), F32),
                   jax.ShapeDtypeStruct((D_MODEL, SEQ), BF16)] + [_class_shape(d, BF16) for d in DILATIONS],
        scratch_shapes=[_natural_scratch(tm)],
        compiler_params=_params("parallel"),
    )(u5, w_out, x, gate, g1, scale1, shift1)
    return res[0], res[1], res[2], [a.reshape(SEQ, D_MODEL) for a in res[3:]]


def _out_b_loss(u, w_out, x1, gate, target, name):
    tm = 256

    def body(u_ref, w_ref, x_ref, gate_ref, t_ref, e_ref, dy_ref, sums_ref):
        y = jnp.dot(u_ref[...], w_ref[...], preferred_element_type=F32)
        diff = x_ref[...] + gate_ref[...] * y - t_ref[...]
        e = diff * (1.0 / D_MODEL)
        e_ref[...] = e
        dy_ref[...] = (e * gate_ref[...]).astype(BF16)
        sums = jnp.concatenate([
            jnp.sum(e * y, axis=0, keepdims=True),
            jnp.sum(diff * diff, axis=0, keepdims=True),
            jnp.zeros((6, D_MODEL), F32)], axis=0)

        @pl.when(pl.program_id(0) == 0)
        def _():
            sums_ref[...] = jnp.zeros_like(sums_ref)

        sums_ref[...] += sums

    return pl.pallas_call(
        body, name=name, grid=(SEQ // tm,),
        in_specs=[_row_spec(tm, D_MODEL), _vec_spec(D_MODEL, D_MODEL), _row_spec(tm, D_MODEL),
                  _vec_spec(1, D_MODEL), _row_spec(tm, D_MODEL)],
        out_specs=[_row_spec(tm, D_MODEL), _row_spec(tm, D_MODEL), _vec_spec(8, D_MODEL)],
        out_shape=[jax.ShapeDtypeStruct((SEQ, D_MODEL), F32), jax.ShapeDtypeStruct((SEQ, D_MODEL), BF16),
                   jax.ShapeDtypeStruct((8, D_MODEL), F32)],
        compiler_params=_params("arbitrary"),
    )(u, w_out, x1, gate, target)


def _dgate_dy(dx1, y, gate, name):
    tm = 256

    def body(d_ref, y_ref, gate_ref, dy_ref, sums_ref):
        dv = d_ref[...]
        dy_ref[...] = (dv * gate_ref[...]).astype(BF16)
        sums = jnp.concatenate([jnp.sum(dv * y_ref[...], axis=0, keepdims=True), jnp.zeros((7, D_MODEL), F32)], axis=0)

        @pl.when(pl.program_id(0) == 0)
        def _():
            sums_ref[...] = jnp.zeros_like(sums_ref)

        sums_ref[...] += sums

    return pl.pallas_call(
        body, name=name, grid=(SEQ // tm,),
        in_specs=[_row_spec(tm, D_MODEL), _row_spec(tm, D_MODEL), _vec_spec(1, D_MODEL)],
        out_specs=[_row_spec(tm, D_MODEL), _vec_spec(8, D_MODEL)],
        out_shape=[jax.ShapeDtypeStruct((SEQ, D_MODEL), BF16), jax.ShapeDtypeStruct((8, D_MODEL), F32)],
        compiler_params=_params("arbitrary"),
    )(dx1, y, gate)


def _mm_nt_res(dy, w, name):
    tm = 256
    kc, n = w.shape

    def body(dy_ref, w_ref, o_ref):
        o_ref[...] = lax.dot_general(dy_ref[...], w_ref[...], (((1,), (1,)), ((), ())), preferred_element_type=F32)

    return pl.pallas_call(
        body, name=name, grid=(SEQ // tm,),
        in_specs=[_row_spec(tm, n), _vec_spec(kc, n)],
        out_specs=_row_spec(tm, kc),
        out_shape=jax.ShapeDtypeStruct((SEQ, kc), F32),
        compiler_params=_params("parallel"),
    )(dy, w)


def _seg_matrix():
    r = lax.broadcasted_iota(jnp.int32, (256, 256), 0) // HEAD_DIM
    c = lax.broadcasted_iota(jnp.int32, (256, 256), 1) // HEAD_DIM
    return (r == c).astype(BF16)


def _segsum(v, seg):
    hi = v.astype(BF16)
    lo = (v - hi.astype(F32)).astype(BF16)
    outs = []
    for c0 in range(0, D_MODEL, 256):
        outs.append(jnp.dot(hi[:, c0:c0 + 256], seg, preferred_element_type=F32)
                    + jnp.dot(lo[:, c0:c0 + 256], seg, preferred_element_type=F32))
    return jnp.concatenate(outs, axis=1)


def _qk_rstd(v, seg):
    return lax.rsqrt(_segsum(v * v, seg) * (1.0 / HEAD_DIM) + NORM_EPS)


def _qknorm_fwd(proj, qw, kw, seg, name):
    tm = 256

    def body(p_ref, qw_ref, kw_ref, seg_ref, q_ref, k_ref):
        segv = seg_ref[...]
        q = p_ref[:, :D_MODEL].astype(F32)
        k = p_ref[:, D_MODEL:].astype(F32)
        q_ref[...] = (q * _qk_rstd(q, segv) * qw_ref[...]).astype(BF16)
        k_ref[...] = (k * _qk_rstd(k, segv) * kw_ref[...]).astype(BF16)

    return pl.pallas_call(
        body, name=name, grid=(SEQ // tm,),
        in_specs=[_row_spec(tm, 2 * D_MODEL), _vec_spec(1, D_MODEL), _vec_spec(1, D_MODEL), _vec_spec(256, 256)],
        out_specs=[_row_spec(tm, D_MODEL)] * 2,
        out_shape=[jax.ShapeDtypeStruct((SEQ, D_MODEL), BF16)] * 2,
        compiler_params=_params("parallel"),
    )(proj, qw, kw, seg)


def _attn_masks(b, bpc, dilation, slope):
    if bpc == 1:
        qi = lax.broadcasted_iota(jnp.int32, (ATTN_BLOCK, ATTN_BLOCK), 0)
        kj = lax.broadcasted_iota(jnp.int32, (ATTN_BLOCK, ATTN_BLOCK), 1)
        steps = qi - kj
        return (steps * dilation).astype(F32), steps >= 0
    qi = lax.broadcasted_iota(jnp.int32, (ATTN_BLOCK, 2 * ATTN_BLOCK), 0)
    kj = lax.broadcasted_iota(jnp.int32, (ATTN_BLOCK, 2 * ATTN_BLOCK), 1)
    steps = qi + ATTN_BLOCK - kj
    has_prev = (b % bpc) != 0
    valid = (steps >= 0) & (steps <= ATTN_BLOCK) & (has_prev | (kj >= ATTN_BLOCK))
    return (steps * dilation).astype(F32), valid


def _key_tile(prev_ref, cur_ref, cols, bpc):
    if bpc == 1:
        return cur_ref[:, cols]
    return jnp.concatenate([prev_ref[:, cols], cur_ref[:, cols]], axis=0)


ATTN_HEADS_FWD = 16
ATTN_HEADS_BWD = 16
NT_DIMS = (((1,), (1,)), ((), ()))
TN_DIMS = (((0,), (0,)), ((), ()))
BATCH_NT_DIMS = (((2,), (2,)), ((0,), (0,)))
BATCH_NN_DIMS = (((2,), (1,)), ((0,), (0,)))
BATCH_TN_DIMS = (((1,), (1,)), ((0,), (0,)))


def _head_stack(tile_of, heads):
    return jnp.stack([tile_of(slice(h * HEAD_DIM, (h + 1) * HEAD_DIM)) for h in range(heads)], axis=0)


def _attn_specs(heads, segment=0):
    width = heads * HEAD_DIM
    off = segment * (D_MODEL // width)
    last = SEQ // ATTN_BLOCK - 1
    cur = pl.BlockSpec((ATTN_BLOCK, width), lambda hg, b: (jnp.minimum(b, last), hg + off))
    prev = pl.BlockSpec((ATTN_BLOCK, width), lambda hg, b: (jnp.clip(b - 1, 0, last), hg + off))
    return cur, prev


def _attn_fwd(q, k, proj, slopes, dilation, name):
    bpc = SEQ // dilation // ATTN_BLOCK
    heads = ATTN_HEADS_FWD
    assert heads == N_HEADS
    cur, prev = _attn_specs(heads)
    v_cur, v_prev = _attn_specs(heads, segment=2)
    scale = HEAD_DIM ** -0.5

    def body(sl_ref, q_ref, kp_ref, kc_ref, vp_ref, vc_ref, o_ref, lse_ref):
        dist, valid = _attn_masks(pl.program_id(1), bpc, dilation, None)
        q3 = _head_stack(lambda cols: q_ref[:, cols], heads)
        k3 = _head_stack(lambda cols: _key_tile(kp_ref, kc_ref, cols, bpc), heads)
        v3 = _head_stack(lambda cols: _key_tile(vp_ref, vc_ref, cols, bpc), heads)
        s = lax.dot_general(q3, k3, BATCH_NT_DIMS, preferred_element_type=F32)
        s = jnp.where(valid[None], s * scale - dist[None] * sl_ref[...], NEG_INF)
        m = jnp.max(s, axis=-1, keepdims=True)
        p = jnp.exp(s - m)
        l = jnp.sum(p, axis=-1, keepdims=True)
        o3 = lax.dot_general(p.astype(BF16), v3, BATCH_NN_DIMS, preferred_element_type=F32) / l
        lse3 = m + jnp.log(l)
        for h in range(heads):
            o_ref[:, h * HEAD_DIM:(h + 1) * HEAD_DIM] = o3[h]
        lse_ref[...] = jnp.concatenate([lse3[h] for h in range(heads)], axis=1)

    return pl.pallas_call(
        body, name=name, grid=(N_HEADS // heads, SEQ // ATTN_BLOCK),
        in_specs=[pl.BlockSpec((heads, 1, 1), lambda hg, b: (hg, 0, 0)), cur, prev, cur, v_prev, v_cur],
        out_specs=[cur, pl.BlockSpec((ATTN_BLOCK, N_HEADS), lambda hg, b: (b, 0))],
        out_shape=[jax.ShapeDtypeStruct((SEQ, D_MODEL), F32), jax.ShapeDtypeStruct((SEQ, N_HEADS), F32)],
        compiler_params=_params("parallel", "parallel"),
    )(slopes.reshape(N_HEADS, 1, 1), q, k, k, proj, proj)


def _class_spec(tm, dilation):
    if dilation == 1:
        return _row_spec(tm, D_MODEL)
    return pl.BlockSpec((dilation, tm // dilation, D_MODEL), lambda i: (0, i, 0))


def _class_shape(dilation, dtype):
    if dilation == 1:
        return jax.ShapeDtypeStruct((SEQ, D_MODEL), dtype)
    return jax.ShapeDtypeStruct((dilation, SEQ // dilation, D_MODEL), dtype)


def _load_natural(in_ref, nat_ref, dilation):
    if dilation == 1:
        return in_ref[...].astype(F32)
    n = nat_ref.shape[1] // dilation
    for r in range(dilation):
        for j in range(D_MODEL // LANES):
            nat_ref.at[j][pl.ds(r, n, stride=dilation), :] = in_ref[r, :, j * LANES:(j + 1) * LANES].astype(F32)
    return jnp.concatenate([nat_ref[j] for j in range(D_MODEL // LANES)], axis=1)


def _store_classes(out_ref, value, nat_ref, dilation):
    if dilation == 1:
        out_ref[...] = value.astype(out_ref.dtype)
        return
    n = nat_ref.shape[1] // dilation
    for j in range(D_MODEL // LANES):
        nat_ref[j] = value[:, j * LANES:(j + 1) * LANES]
    for r in range(dilation):
        for j in range(D_MODEL // LANES):
            out_ref[r, :, j * LANES:(j + 1) * LANES] = (
                nat_ref.at[j][pl.ds(r, n, stride=dilation), :].astype(out_ref.dtype))


def _natural_scratch(tm):
    return pltpu.VMEM((D_MODEL // LANES, tm, LANES), F32)


def _head_selector():
    lane_head = lax.broadcasted_iota(jnp.int32, (D_MODEL, N_HEADS), 0) // HEAD_DIM
    head = lax.broadcasted_iota(jnp.int32, (D_MODEL, N_HEADS), 1)
    return (lane_head == head).astype(BF16)


def _dot_split(v, m01, dims):
    hi = v.astype(BF16)
    lo = (v - hi.astype(F32)).astype(BF16)
    return (lax.dot_general(hi, m01, dims, preferred_element_type=F32)
            + lax.dot_general(lo, m01, dims, preferred_element_type=F32))


def _merge_fwd(o_parts, lse_parts, z, sel, name):
    tm = 256
    h_spec = pl.BlockSpec((tm, N_HEADS), lambda i: (i, 0))

    def body(o0, o1, o2, l0, l1, l2, z_ref, sel_ref, u_ref, ut_ref, o_ref, lse_ref, nat):
        ls = [l0[...], l1[...], l2[...]]
        m = jnp.maximum(jnp.maximum(ls[0], ls[1]), ls[2])
        tot = m + jnp.log(jnp.exp(ls[0] - m) + jnp.exp(ls[1] - m) + jnp.exp(ls[2] - m))
        o = jnp.zeros((tm, D_MODEL), F32)
        for o_in, l, d in zip((o0, o1, o2), ls, DILATIONS):
            weight = _dot_split(jnp.exp(l - tot), sel_ref[...], NT_DIMS)
            o = o + weight * _load_natural(o_in, nat, d)
        zv = z_ref[...]
        u = o * (zv * _sigmoid(zv))
        u_ref[...] = u.astype(BF16)
        ut_ref[...] = u.T.astype(BF16)
        o_ref[...] = o
        lse_ref[...] = tot

    return pl.pallas_call(
        body, name=name, grid=(SEQ // tm,),
        in_specs=[_class_spec(tm, d) for d in DILATIONS] + [h_spec] * 3
        + [_row_spec(tm, D_MODEL), _vec_spec(D_MODEL, N_HEADS)],
        out_specs=[_row_spec(tm, D_MODEL), pl.BlockSpec((D_MODEL, tm), lambda i: (0, i)),
                   _row_spec(tm, D_MODEL), h_spec],
        out_shape=[jax.ShapeDtypeStruct((SEQ, D_MODEL), BF16), jax.ShapeDtypeStruct((D_MODEL, SEQ), BF16),
                   jax.ShapeDtypeStruct((SEQ, D_MODEL), F32), jax.ShapeDtypeStruct((SEQ, N_HEADS), F32)],
        scratch_shapes=[_natural_scratch(tm)],
        compiler_params=_params("parallel"),
    )(*o_parts, *lse_parts, z, sel)


def _merge_bwd(dy, w_out, o, z, sel, name):
    tm = 256
    n_d = len(DILATIONS)

    def body(dy_ref, w_ref, o_ref, z_ref, sel_ref, dz_ref, delta_ref, *rest):
        do_refs, nat = rest[:n_d], rest[-1]
        zv = z_ref[...]
        sz = _sigmoid(zv)
        duv = lax.dot_general(dy_ref[...], w_ref[...], NT_DIMS, preferred_element_type=F32)
        ov = o_ref[...]
        do = duv * (zv * sz)
        dz_ref[...] = (duv * ov * (sz * (1.0 + zv * (1.0 - sz)))).astype(BF16)
        delta_ref[...] = _dot_split(do * ov, sel_ref[...], (((1,), (0,)), ((), ())))
        for do_ref, d in zip(do_refs, DILATIONS):
            _store_classes(do_ref, do, nat, d)

    res = pl.pallas_call(
        body, name=name, grid=(SEQ // tm,),
        in_specs=[_row_spec(tm, D_MODEL), _vec_spec(D_MODEL, D_MODEL), _row_spec(tm, D_MODEL), _row_spec(tm, D_MODEL),
                  _vec_spec(D_MODEL, N_HEADS)],
        out_specs=[_row_spec(tm, D_MODEL), pl.BlockSpec((tm, N_HEADS), lambda i: (i, 0))]
        + [_class_spec(tm, d) for d in DILATIONS],
        out_shape=[jax.ShapeDtypeStruct((SEQ, D_MODEL), BF16), jax.ShapeDtypeStruct((SEQ, N_HEADS), F32)]
        + [_class_shape(d, BF16) for d in DILATIONS],
        scratch_shapes=[_natural_scratch(tm)],
        compiler_params=_params("parallel"),
    )(dy, w_out, o, z, sel)
    return res[0], res[1], [a.reshape(SEQ, D_MODEL) for a in res[2:]]


def _attn_bwd(q, k, proj, do, lse, delta, slopes, dilation, name):
    bpc = SEQ // dilation // ATTN_BLOCK
    heads = ATTN_HEADS_BWD
    n_blocks = SEQ // ATTN_BLOCK
    carry = bpc > 1
    width = heads * HEAD_DIM
    cur, prev = _attn_specs(heads)
    v_cur, v_prev = _attn_specs(heads, segment=2)
    assert heads == N_HEADS
    per_head = pl.BlockSpec((ATTN_BLOCK, N_HEADS), lambda hg, b: (jnp.minimum(b, n_blocks - 1), 0))
    scale = HEAD_DIM ** -0.5

    def body(sl_ref, q_ref, kp_ref, kc_ref, vp_ref, vc_ref, do_ref, lse_ref, dl_ref,
             dq_ref, dk_ref, dv_ref, *scratch):
        b = pl.program_id(1)
        if carry:
            dk_carry, dv_carry = scratch

            @pl.when(b == n_blocks)
            def _():
                dk_ref[...] = dk_carry[...].astype(BF16)
                dv_ref[...] = dv_carry[...].astype(BF16)

            @pl.when(b < n_blocks)
            def _():
                step(sl_ref, q_ref, kp_ref, kc_ref, vp_ref, vc_ref, do_ref, lse_ref, dl_ref,
                     dq_ref, dk_ref, dv_ref, dk_carry, dv_carry, b)
        else:
            step(sl_ref, q_ref, kp_ref, kc_ref, vp_ref, vc_ref, do_ref, lse_ref, dl_ref,
                 dq_ref, dk_ref, dv_ref, None, None, b)

    def step(sl_ref, q_ref, kp_ref, kc_ref, vp_ref, vc_ref, do_ref, lse_ref, dl_ref,
             dq_ref, dk_ref, dv_ref, dk_carry, dv_carry, b):
        if carry:
            @pl.when(b == 0)
            def _():
                dk_carry[...] = jnp.zeros_like(dk_carry)
                dv_carry[...] = jnp.zeros_like(dv_carry)

        dist, valid = _attn_masks(b, bpc, dilation, None)
        q3 = _head_stack(lambda cols: q_ref[:, cols], heads)
        k3 = _head_stack(lambda cols: _key_tile(kp_ref, kc_ref, cols, bpc), heads)
        v3 = _head_stack(lambda cols: _key_tile(vp_ref, vc_ref, cols, bpc), heads)
        do3 = _head_stack(lambda cols: do_ref[:, cols], heads)
        lse3 = jnp.stack([lse_ref[:, h:h + 1] for h in range(heads)], axis=0)
        dl3 = jnp.stack([dl_ref[:, h:h + 1] for h in range(heads)], axis=0)
        s = lax.dot_general(q3, k3, BATCH_NT_DIMS, preferred_element_type=F32)
        p = jnp.exp(jnp.where(valid[None], s * scale - dist[None] * sl_ref[...], NEG_INF) - lse3)
        dp = lax.dot_general(do3, v3, BATCH_NT_DIMS, preferred_element_type=F32)
        ds = (p * (dp - dl3) * scale).astype(BF16)
        dq3 = lax.dot_general(ds, k3, BATCH_NN_DIMS, preferred_element_type=F32)
        dk3 = lax.dot_general(ds, q3, BATCH_TN_DIMS, preferred_element_type=F32)
        dv3 = lax.dot_general(p.astype(BF16), do3, BATCH_TN_DIMS, preferred_element_type=F32)
        for h in range(heads):
            cols = slice(h * HEAD_DIM, (h + 1) * HEAD_DIM)
            dq_ref[:, cols] = dq3[h].astype(BF16)
            if carry:
                dk_ref[:, cols] = (dk_carry[:, cols] + dk3[h, :ATTN_BLOCK]).astype(BF16)
                dv_ref[:, cols] = (dv_carry[:, cols] + dv3[h, :ATTN_BLOCK]).astype(BF16)
                dk_carry[:, cols] = dk3[h, ATTN_BLOCK:]
                dv_carry[:, cols] = dv3[h, ATTN_BLOCK:]
            else:
                dk_ref[:, cols] = dk3[h].astype(BF16)
                dv_ref[:, cols] = dv3[h].astype(BF16)

    kv_out = prev if carry else cur
    return pl.pallas_call(
        body, name=name, grid=(N_HEADS // heads, n_blocks + (1 if carry else 0)),
        in_specs=[pl.BlockSpec((heads, 1, 1), lambda hg, b: (hg, 0, 0)), cur, prev, cur, v_prev, v_cur,
                  cur, per_head, per_head],
        out_specs=[cur, kv_out, kv_out],
        out_shape=[jax.ShapeDtypeStruct((SEQ, D_MODEL), BF16)] * 3,
        scratch_shapes=[pltpu.VMEM((ATTN_BLOCK, width), F32)] * 2 if carry else [],
        compiler_params=_params("parallel", "arbitrary"),
    )(slopes.reshape(N_HEADS, 1, 1), q, k, k, proj, proj, do, lse, delta)


def _qknorm_bwd(proj, qw, kw, seg, dq, dk, dv, name):
    tm = 256

    def body(p_ref, qw_ref, kw_ref, seg_ref, dq_ref, dk_ref, dv_ref, dproj_ref, sums_ref):
        segv = seg_ref[...]
        sums = []
        for part, (w_ref, dn_ref) in enumerate(((qw_ref, dq_ref), (kw_ref, dk_ref))):
            raw = p_ref[:, part * D_MODEL:(part + 1) * D_MODEL].astype(F32)
            dn = dn_ref[...].astype(F32)
            r = _qk_rstd(raw, segv)
            gq = dn * w_ref[...]
            draw = r * gq - raw * (r * r * r) * (_segsum(raw * gq, segv) * (1.0 / HEAD_DIM))
            dproj_ref[:, part * D_MODEL:(part + 1) * D_MODEL] = draw.astype(BF16)
            sums.append(jnp.sum(dn * raw * r, axis=0, keepdims=True))
        dproj_ref[:, 2 * D_MODEL:] = dv_ref[...]

        @pl.when(pl.program_id(0) == 0)
        def _():
            sums_ref[...] = jnp.zeros_like(sums_ref)

        sums_ref[...] += jnp.concatenate(sums + [jnp.zeros((6, D_MODEL), F32)], axis=0)

    return pl.pallas_call(
        body, name=name, grid=(SEQ // tm,),
        in_specs=[_row_spec(tm, 3 * D_MODEL), _vec_spec(1, D_MODEL), _vec_spec(1, D_MODEL), _vec_spec(256, 256)]
        + [_row_spec(tm, D_MODEL)] * 3,
        out_specs=[_row_spec(tm, 3 * D_MODEL), _vec_spec(8, D_MODEL)],
        out_shape=[jax.ShapeDtypeStruct((SEQ, 3 * D_MODEL), BF16), jax.ShapeDtypeStruct((8, D_MODEL), F32)],
        compiler_params=_params("arbitrary"),
    )(proj, qw, kw, seg, dq, dk, dv)


def _to_classes(a, dilation):
    if dilation == 1:
        return a
    s, c = a.shape
    return a.reshape(s // dilation, dilation, c).transpose(1, 0, 2).reshape(s, c)


def _from_classes(a, dilation):
    if dilation == 1:
        return a
    s, c = a.shape
    return a.reshape(dilation, s // dilation, c).transpose(1, 0, 2).reshape(s, c)


def _cols_to_classes(a, dilation):
    if dilation == 1:
        return a
    r, s = a.shape
    return a.reshape(r, s // dilation, dilation).transpose(0, 2, 1).reshape(r, s)


B_TN = 512
B_GROUP_TILES = 3 * D_MODEL // B_TN
B_Z_TILE0 = 3 * B_GROUP_TILES
B_Z_TILES = D_MODEL // B_TN


def _local_step(x, target, mods, norm_g, conv_w, conv_b, ln_g, ln_b, q_norm, k_norm,
                weights_a, weights_b, forward_weights_b, send_grads_b, forward_grads_b, send_grads_a):
    row = lambda a, i: a[i:i + 1]
    shift0, scale0, gate0 = row(mods[0], 0), row(mods[0], 1), row(mods[0], 2)
    shift1, scale1, gate1 = row(mods[1], 0), row(mods[1], 1), row(mods[1], 2)
    g0, g1 = row(norm_g, 0), row(norm_g, 1)
    seg = _seg_matrix()
    slopes = jnp.exp2(-8.0 * jnp.arange(1, N_HEADS + 1, dtype=F32) / N_HEADS)
    qw = [jnp.tile(q_norm[g:g + 1], (1, N_HEADS)) for g in range(3)]
    kw = [jnp.tile(k_norm[g:g + 1], (1, N_HEADS)) for g in range(3)]

    h0, h0t = _normmod_fwd(x, g0, scale0, shift0, "prenorm0")
    wa_in, wa_out = weights_a(h0)
    ja, _, nsa = wa_in.shape
    proj_a = _mm(h0, wa_in, tn=nsa, tile0=0, n_tiles=ja, out_dtype=F32, name="a_in")
    u5, u5t, u2 = _conv_fwd(proj_a, conv_w, conv_b, ln_g, ln_b, "a_conv")
    token = forward_weights_b(u5)
    x1, y_a, h1t, h1c = _out_a(u5, wa_out, x, gate0 + token[0:1, 0:1], g1, scale1, shift1, "a_out")

    wb_in, wb_out = weights_b(x1)
    jb, _, nsb = wb_in.shape
    h1 = h1c[0]
    h1tc = [_cols_to_classes(h1t, d) for d in DILATIONS]
    z_b = _mm(h1, wb_in, tn=B_TN, tile0=B_Z_TILE0, n_tiles=B_Z_TILES, out_dtype=F32, name="b_in_z")
    proj_g, qkv, o_parts, lse_parts = [], [], [], []
    for g, d in enumerate(DILATIONS):
        pg = _mm(h1c[g], wb_in, tn=B_TN, tile0=g * B_GROUP_TILES, n_tiles=B_GROUP_TILES, out_dtype=BF16,
                 name=f"b_in_g{g}")
        qn, kn = _qknorm_fwd(pg, qw[g], kw[g], seg, f"b_qknorm_g{g}")
        og, lg = _attn_fwd(qn, kn, pg, slopes, d, f"b_attn_g{g}")
        proj_g.append(pg)
        qkv.append((qn, kn))
        o_parts.append(og if d == 1 else og.reshape(d, SEQ // d, D_MODEL))
        lse_parts.append(_from_classes(lg, d))
    sel = _head_selector()
    u_b, u_bt, o_b, lse_b = _merge_fwd(o_parts, lse_parts, z_b, sel, "b_merge")
    e, dy_b, sums_loss = _out_b_loss(u_b, wb_out, x1, gate1, target, "b_out_loss")

    dwb_out = _mm(u_bt, dy_b, tn=D_MODEL, tile0=0, n_tiles=1, out_dtype=BF16, name="b_dwout")
    dz_b, delta_b, do_c = _merge_bwd(dy_b, wb_out, o_b, z_b, sel, "b_merge_bwd")
    dwb_in = _mm(h1t, dz_b, tn=B_TN, tile0=B_Z_TILE0, n_tiles=B_Z_TILES, out_dtype=BF16, name="b_dwin_z",
                 out3d=(jb, nsb))
    dh1_parts = [_mm_nt(dz_b, wb_in, tn=B_TN, tile0=B_Z_TILE0, n_tiles=B_Z_TILES, name="b_dh_z")]
    qk_sums = []
    for g, d in enumerate(DILATIONS):
        qn, kn = qkv[g]
        dq, dk, dv = _attn_bwd(qn, kn, proj_g[g], do_c[g], _to_classes(lse_b, d), _to_classes(delta_b, d),
                               slopes, d, f"b_attn_bwd_g{g}")
        dproj, sums_qk = _qknorm_bwd(proj_g[g], qw[g], kw[g], seg, dq, dk, dv, f"b_qknorm_bwd_g{g}")
        qk_sums.append(sums_qk)
        dwb_in = _mm(h1tc[g], dproj, tn=B_TN, tile0=g * B_GROUP_TILES, n_tiles=B_GROUP_TILES, out_dtype=BF16,
                     name=f"b_dwin_g{g}", out3d=(jb, nsb), prev=dwb_in)
        dh = _mm_nt(dproj, wb_in, tn=B_TN, tile0=g * B_GROUP_TILES, n_tiles=B_GROUP_TILES, name=f"b_dh_g{g}")
        dh1_parts.append(dh)
    token = send_grads_b(dwb_in, dwb_out)
    dx1, sums_n1, dy_a = _normmod_bwd(x1, g1, scale1 + token[0:1, 0:1], dh1_parts, e, "prenorm1_bwd",
                                      part_dilations=(1,) + DILATIONS, gated=(gate0, y_a))
    token = forward_grads_b(dx1)

    dwa_out = _mm(u5t, dy_a, tn=D_MODEL, tile0=0, n_tiles=1, out_dtype=BF16, name="a_dwout")
    du2, dz_a, sums_ln = _conv_bwd_pointwise(dy_a, wa_out, proj_a, u2, ln_g + token[0:1, 0:1], ln_b,
                                             "a_conv_bwd_pw")
    dproj_a, dconv_w = _conv_bwd_taps(du2, dz_a, proj_a, conv_w, "a_conv_bwd_taps")
    dwa_in = _mm(h0t, dproj_a, tn=nsa, tile0=0, n_tiles=ja, out_dtype=BF16, name="a_dwin", out3d=(ja, nsa))
    token = send_grads_a(dwa_in, dwa_out)
    dh0 = _mm_nt(dproj_a, wa_in, tn=nsa, tile0=0, n_tiles=ja, name="a_dh", after=token)
    grad_x, sums_n0 = _normmod_bwd(x, g0, scale0, [dh0], dx1, "prenorm0_bwd")

    small = dict(
        dnorm_g=jnp.concatenate([sums_n0[0:1], sums_n1[0:1]], axis=0),
        dmod0=jnp.concatenate([sums_n0[2:3], sums_n0[1:2], sums_n1[3:4]], axis=0),
        dmod1=jnp.concatenate([sums_n1[2:3], sums_n1[1:2], sums_loss[0:1]], axis=0),
        dln_g=sums_ln[0:1], dln_b=sums_ln[1:2], dconv_b=sums_ln[2:3],
        dconv_w=dconv_w[:CONV_WIDTH],
        dq_norm=jnp.concatenate([s[0:1] for s in qk_sums], axis=0),
        dk_norm=jnp.concatenate([s[1:2] for s in qk_sums], axis=0),
        loss_cols=sums_loss[1:2],
    )
    return grad_x, small


def _adamw(w, g, m, v, name, after=None):
    rows, cols = w.shape
    tr = rows if rows <= 128 else 128
    c1 = 1.0 / (1.0 - ADAM_B1 ** ADAM_STEP)
    c2 = 1.0 / (1.0 - ADAM_B2 ** ADAM_STEP)
    extra = [] if after is None else [after]

    def body(w_ref, g_ref, m_ref, v_ref, *rest):
        d_ref, mo_ref, vo_ref = rest[len(extra):]
        gv = g_ref[...]
        mn = ADAM_B1 * m_ref[...] + (1.0 - ADAM_B1) * gv
        vn = ADAM_B2 * v_ref[...] + (1.0 - ADAM_B2) * (gv * gv)
        mo_ref[...] = mn
        vo_ref[...] = vn
        d_ref[...] = -ADAM_LR * ((mn * c1) / (jnp.sqrt(vn * c2) + ADAM_EPS) + ADAM_WD * w_ref[...])

    spec = pl.BlockSpec((tr, cols), lambda i: (i, 0))
    return pl.pallas_call(
        body, name=name, grid=(rows // tr,),
        in_specs=[spec] * 4 + [pl.BlockSpec(memory_space=pl.ANY)] * len(extra), out_specs=[spec] * 3,
        out_shape=[jax.ShapeDtypeStruct((rows, cols), F32)] * 3,
        compiler_params=_params("parallel"),
    )(w, g, m, v, *extra)


def _cast_into_slot(w, chip_idx, name):
    rows, cols = w.shape
    tr = 256

    def body(ch_ref, w_ref, o_ref):
        o_ref[...] = w_ref[...].astype(BF16)

    return pl.pallas_call(
        body, name=name,
        grid_spec=pltpu.PrefetchScalarGridSpec(
            num_scalar_prefetch=1, grid=(rows // tr,),
            in_specs=[pl.BlockSpec((tr, cols), lambda i, ch: (i, 0))],
            out_specs=pl.BlockSpec((None, tr, cols), lambda i, ch: (ch[0], i, 0))),
        out_shape=jax.ShapeDtypeStruct((N_CHIPS, rows, cols), BF16), compiler_params=_params("parallel"),
    )(chip_idx, w)


def _position():
    x, y, c = lax.axis_index("x"), lax.axis_index("y"), lax.axis_index("c")
    return x, y, c


def _xor_peer(x, y, c, k):
    return (x ^ ((k >> 2) & 1), y ^ ((k >> 1) & 1), c ^ (k & 1))


def _chip_peer(x, y, k):
    return (x ^ ((k >> 1) & 1), y ^ (k & 1))


def _ada_forward(c_row, ada_w, ada_b, conv_w):
    ns = ada_w.shape[2]
    cw = conv_w.shape[1]

    def body(c_ref, w_ref, b_ref, cv_ref, mod_ref, sc_ref, cvo_ref,
             c_all, mp, parts, cv_parts, send1, recv1, send2, recv2, send3, recv3):
        x, y, c = _position()
        me = 4 * x + 2 * y + c
        chip = 2 * x + y

        def c_copy(k):
            return pltpu.make_async_remote_copy(
                src_ref=c_all.at[me], dst_ref=c_all.at[me], send_sem=send1.at[k - 1], recv_sem=recv1.at[k - 1],
                device_id=_xor_peer(x, y, c, k), device_id_type=MESH)

        def cv_copy(k):
            px, py = _chip_peer(x, y, k)
            return pltpu.make_async_remote_copy(
                src_ref=cv_parts.at[chip], dst_ref=cv_parts.at[chip], send_sem=send3.at[k - 1],
                recv_sem=recv3.at[k - 1], device_id=(px, py, c), device_id_type=MESH)

        c_all[me] = c_ref[...]
        cv_parts[chip] = cv_ref[...]
        for k in range(1, N_DEV):
            c_copy(k).start()
        for k in range(1, N_CHIPS):
            cv_copy(k).start()
        for k in range(1, N_DEV):
            c_copy(k).wait_recv()
        cv = jnp.concatenate([c_all[i] for i in range(N_DEV)], axis=0)
        sc = cv * _sigmoid(cv)
        sc_ref[...] = sc
        for l in range(2):
            res = jnp.dot(sc, w_ref[l], preferred_element_type=F32, precision=lax.Precision.HIGHEST)
            for i in range(N_DEV):
                mp[i, l:l + 1, :] = res[i:i + 1, :]

        def mod_copy(k):
            px, py = _chip_peer(x, y, k)
            return pltpu.make_async_remote_copy(
                src_ref=mp.at[4 * px + 2 * py + c], dst_ref=parts.at[chip], send_sem=send2.at[k - 1],
                recv_sem=recv2.at[k - 1], device_id=(px, py, c), device_id_type=MESH)

        for k in range(1, N_CHIPS):
            mod_copy(k).start()
        parts[chip] = mp[me]
        for k in range(1, N_CHIPS):
            mod_copy(k).wait_recv()
            cv_copy(k).wait_recv()
        mod_ref[...] = jnp.concatenate([parts[j] for j in range(N_CHIPS)], axis=1) + b_ref[...]
        cvo_ref[...] = jnp.concatenate([cv_parts[j] for j in range(N_CHIPS)], axis=1)
        for k in range(1, N_DEV):
            c_copy(k).wait_send()
        for k in range(1, N_CHIPS):
            mod_copy(k).wait_send()
            cv_copy(k).wait_send()

    vm = pl.BlockSpec(memory_space=pltpu.VMEM)
    return pl.pallas_call(
        body, name="ada_forward",
        in_specs=[vm] * 4, out_specs=[vm] * 3,
        out_shape=[jax.ShapeDtypeStruct((2, 3 * D_MODEL), F32), jax.ShapeDtypeStruct((N_DEV, D_MODEL), F32),
                   jax.ShapeDtypeStruct((CONV_WIDTH, N_CHIPS * cw), F32)],
        scratch_shapes=[pltpu.VMEM((N_DEV, 1, D_MODEL), F32), pltpu.VMEM((N_DEV, 2, ns), F32),
                        pltpu.VMEM((N_CHIPS, 2, ns), F32), pltpu.VMEM((N_CHIPS, CONV_WIDTH, cw), F32),
                        pltpu.SemaphoreType.DMA((N_DEV - 1,)), pltpu.SemaphoreType.DMA((N_DEV - 1,)),
                        pltpu.SemaphoreType.DMA((N_CHIPS - 1,)), pltpu.SemaphoreType.DMA((N_CHIPS - 1,)),
                        pltpu.SemaphoreType.DMA((N_CHIPS - 1,)), pltpu.SemaphoreType.DMA((N_CHIPS - 1,))],
        compiler_params=pltpu.CompilerParams(vmem_limit_bytes=VMEM_LIMIT_BYTES),
    )(c_row, ada_w, ada_b, conv_w)


HBM_SPEC = pl.BlockSpec(memory_space=pltpu.HBM)
ANY_SPEC = pl.BlockSpec(memory_space=pl.ANY)
SEM_SPEC = pl.BlockSpec(memory_space=pltpu.SEMAPHORE)
SPLIT_PARAMS = dict(compiler_params=pltpu.CompilerParams(has_side_effects=pltpu.SideEffectType.DATAFLOW_SIDE_EFFECTING))
TOKEN = jax.ShapeDtypeStruct((8, 128), F32)


def _hbm(arrays):
    return [pltpu.with_memory_space_constraint(a, pltpu.HBM) for a in arrays]


def _hbm_like(arrays):
    return [pltpu.HBM(a.shape, a.dtype) for a in arrays]


def _gather_start(lands, after, name):
    n = len(lands)

    def body(*refs):
        ins = refs[:n]
        send, recv = refs[n + 1], refs[n + 2]
        x, y, c = _position()
        chip = 2 * x + y
        for t in range(n):
            rh = ins[t].shape[1] // 2
            for k in range(1, N_CHIPS):
                px, py = _chip_peer(x, y, k)
                block = ins[t].at[chip, pl.ds(c * rh, rh)]
                pltpu.make_async_remote_copy(
                    src_ref=block, dst_ref=block, send_sem=send.at[3 * t + k - 1], recv_sem=recv.at[3 * t + k - 1],
                    device_id=(px, py, c), device_id_type=MESH).start()
        refs[-1][...] = jnp.zeros(TOKEN.shape, F32)

    res = pl.pallas_call(
        body, name=name, in_specs=[HBM_SPEC] * n + [ANY_SPEC],
        out_specs=(SEM_SPEC, SEM_SPEC, *[HBM_SPEC] * n, pl.BlockSpec(memory_space=pltpu.VMEM)),
        out_shape=(pltpu.SemaphoreType.DMA((3 * n,)), pltpu.SemaphoreType.DMA((3 * n,)), *_hbm_like(lands), TOKEN),
        input_output_aliases={t: 2 + t for t in range(n)}, **SPLIT_PARAMS,
    )(*_hbm(lands), after)
    return res[0], res[1], list(res[2:2 + n]), res[-1]


def _gather_forward(send, recv, lands, after, name):
    n = len(lands)

    def body(*refs):
        ins = refs[:n]
        send1, recv1 = refs[n], refs[n + 1]
        send2, recv2 = refs[n + 3], refs[n + 4]
        x, y, c = _position()
        chip = 2 * x + y
        for t in range(n):
            rh = ins[t].shape[1] // 2
            half = pl.ds(c * rh, rh)
            for k in range(1, N_CHIPS):
                px, py = _chip_peer(x, y, k)
                s = 3 * t + k - 1
                got = ins[t].at[2 * px + py, half]
                cp = pltpu.make_async_remote_copy(
                    src_ref=ins[t].at[chip, half], dst_ref=got, send_sem=send1.at[s], recv_sem=recv1.at[s],
                    device_id=(px, py, c), device_id_type=MESH)
                cp.wait_send()
                cp.wait_recv()
                pltpu.make_async_remote_copy(
                    src_ref=got, dst_ref=got, send_sem=send2.at[s], recv_sem=recv2.at[s],
                    device_id=(x, y, 1 - c), device_id_type=MESH).start()
        refs[-1][...] = jnp.zeros(TOKEN.shape, F32)

    res = pl.pallas_call(
        body, name=name, in_specs=[HBM_SPEC] * n + [SEM_SPEC, SEM_SPEC, ANY_SPEC],
        out_specs=(SEM_SPEC, SEM_SPEC, *[HBM_SPEC] * n, pl.BlockSpec(memory_space=pltpu.VMEM)),
        out_shape=(pltpu.SemaphoreType.DMA((3 * n,)), pltpu.SemaphoreType.DMA((3 * n,)), *_hbm_like(lands), TOKEN),
        input_output_aliases={t: 2 + t for t in range(n)}, **SPLIT_PARAMS,
    )(*lands, send, recv, after)
    return res[0], res[1], list(res[2:2 + n]), res[-1]


def _gather_wait(send, recv, lands, after, name):
    n = len(lands)

    def body(*refs):
        ins = refs[:n]
        send_ref, recv_ref = refs[n], refs[n + 1]
        x, y, c = _position()
        for t in range(n):
            rh = ins[t].shape[1] // 2
            for k in range(1, N_CHIPS):
                px, py = _chip_peer(x, y, k)
                cp = pltpu.make_async_remote_copy(
                    src_ref=ins[t].at[2 * px + py, pl.ds(c * rh, rh)],
                    dst_ref=ins[t].at[2 * px + py, pl.ds((1 - c) * rh, rh)], send_sem=send_ref.at[3 * t + k - 1],
                    recv_sem=recv_ref.at[3 * t + k - 1], device_id=(x, y, 1 - c), device_id_type=MESH)
                cp.wait_send()
                cp.wait_recv()

    res = pl.pallas_call(
        body, name=name, in_specs=[HBM_SPEC] * n + [SEM_SPEC, SEM_SPEC, ANY_SPEC], out_specs=[HBM_SPEC] * n,
        out_shape=_hbm_like(lands), input_output_aliases={t: t for t in range(n)}, **SPLIT_PARAMS,
    )(*lands, send, recv, after)
    return list(res)


def _reduce_start(grads, after, name):
    n = len(grads)
    lands = [lax.empty((N_DEV, g.shape[1] // 2, g.shape[2]), BF16) for g in grads]

    def body(*refs):
        gs, ls = refs[:n], refs[n:2 * n]
        send, recv = refs[2 * n + 1], refs[2 * n + 2]
        x, y, c = _position()
        me = 4 * x + 2 * y + c
        for t in range(n):
            rh = gs[t].shape[1] // 2
            for k in range(1, N_DEV):
                px, py, pc = _xor_peer(x, y, c, k)
                pltpu.make_async_remote_copy(
                    src_ref=gs[t].at[2 * px + py, pl.ds(pc * rh, rh)], dst_ref=ls[t].at[me],
                    send_sem=send.at[7 * t + k - 1], recv_sem=recv.at[7 * t + k - 1],
                    device_id=(px, py, pc), device_id_type=MESH).start()
        refs[-1][...] = jnp.zeros(TOKEN.shape, F32)

    res = pl.pallas_call(
        body, name=name, in_specs=[HBM_SPEC] * (2 * n) + [ANY_SPEC],
        out_specs=(SEM_SPEC, SEM_SPEC, *[HBM_SPEC] * (2 * n), pl.BlockSpec(memory_space=pltpu.VMEM)),
        out_shape=(pltpu.SemaphoreType.DMA((7 * n,)), pltpu.SemaphoreType.DMA((7 * n,)),
                   *_hbm_like(grads), *_hbm_like(lands), TOKEN),
        input_output_aliases={t: 2 + t for t in range(2 * n)}, **SPLIT_PARAMS,
    )(*_hbm(grads), *_hbm(lands), after)
    return res[0], res[1], list(res[2:2 + n]), list(res[2 + n:2 + 2 * n]), res[-1]


def _reduce_wait(send, recv, grads, lands, after, name):
    n = len(grads)

    def body(*refs):
        gs, ls = refs[:n], refs[n:2 * n]
        send_ref, recv_ref = refs[2 * n], refs[2 * n + 1]
        x, y, c = _position()
        for t in range(n):
            rh = gs[t].shape[1] // 2
            for k in range(1, N_DEV):
                px, py, pc = _xor_peer(x, y, c, k)
                cp = pltpu.make_async_remote_copy(
                    src_ref=gs[t].at[2 * px + py, pl.ds(pc * rh, rh)], dst_ref=ls[t].at[4 * px + 2 * py + pc],
                    send_sem=send_ref.at[7 * t + k - 1], recv_sem=recv_ref.at[7 * t + k - 1],
                    device_id=(px, py, pc), device_id_type=MESH)
                cp.wait_send()
                cp.wait_recv()

    res = pl.pallas_call(
        body, name=name, in_specs=[HBM_SPEC] * (2 * n) + [SEM_SPEC, SEM_SPEC, ANY_SPEC], out_specs=[HBM_SPEC] * (2 * n),
        out_shape=_hbm_like(grads) + _hbm_like(lands), input_output_aliases={t: t for t in range(2 * n)}, **SPLIT_PARAMS,
    )(*grads, *lands, send, recv, after)
    return list(res[:n]), list(res[n:])


def _sum_devices(land, grad, dev_idx, name):
    _, rh, cols = land.shape
    tr = 128
    nb = rh // tr

    def body(idx_ref, l_ref, g_ref, o_ref):
        me = idx_ref[0]
        acc = jnp.where(me == 0, g_ref[...], l_ref[0]).astype(F32)
        for d in range(1, N_DEV):
            acc = acc + jnp.where(me == d, g_ref[...], l_ref[d]).astype(F32)
        o_ref[...] = acc

    return pl.pallas_call(
        body, name=name,
        grid_spec=pltpu.PrefetchScalarGridSpec(
            num_scalar_prefetch=1, grid=(nb,),
            in_specs=[pl.BlockSpec((N_DEV, tr, cols), lambda i, idx: (0, i, 0)),
                      pl.BlockSpec((None, tr, cols), lambda i, idx: (idx[1], idx[2] * nb + i, 0))],
            out_specs=pl.BlockSpec((tr, cols), lambda i, idx: (idx[2] * nb + i, 0))),
        out_shape=jax.ShapeDtypeStruct((2 * rh, cols), F32), compiler_params=_params("parallel"),
    )(dev_idx, land, grad)


def _split_start(name, arrays, n_sems, after, issue):
    m = len(arrays)

    def body(*refs):
        issue(refs[:m], refs[m + 1], refs[m + 2])
        refs[-1][...] = jnp.zeros(TOKEN.shape, F32)

    res = pl.pallas_call(
        body, name=name, in_specs=[HBM_SPEC] * m + [ANY_SPEC],
        out_specs=(SEM_SPEC, SEM_SPEC, *[HBM_SPEC] * m, pl.BlockSpec(memory_space=pltpu.VMEM)),
        out_shape=(pltpu.SemaphoreType.DMA((n_sems,)), pltpu.SemaphoreType.DMA((n_sems,)), *_hbm_like(arrays), TOKEN),
        input_output_aliases={t: 2 + t for t in range(m)}, **SPLIT_PARAMS,
    )(*_hbm(arrays), after)
    return res[0], res[1], list(res[2:2 + m]), res[-1]


def _split_wait(name, arrays, send, recv, after, await_all):
    m = len(arrays)

    def body(*refs):
        await_all(refs[:m], refs[m], refs[m + 1])

    res = pl.pallas_call(
        body, name=name, in_specs=[HBM_SPEC] * m + [SEM_SPEC, SEM_SPEC, ANY_SPEC], out_specs=[HBM_SPEC] * m,
        out_shape=_hbm_like(arrays), input_output_aliases={t: t for t in range(m)}, **SPLIT_PARAMS,
    )(*arrays, send, recv, after)
    return list(res)


def _sibling_copies(refs, send, recv, n):
    x, y, c = _position()
    cps = []
    for t in range(n):
        rh = refs[t].shape[1] // 2
        cps.append(pltpu.make_async_remote_copy(
            src_ref=refs[t].at[pl.ds(0, N_CHIPS), pl.ds((1 - c) * rh, rh)], dst_ref=refs[n + t],
            send_sem=send.at[t], recv_sem=recv.at[t], device_id=(x, y, 1 - c), device_id_type=MESH))
    return cps


def _reduce_sibling_start(grads, after, name):
    n = len(grads)
    lands = [lax.empty((N_CHIPS, g.shape[1] // 2, g.shape[2]), BF16) for g in grads]

    def issue(refs, send, recv):
        for cp in _sibling_copies(refs, send, recv, n):
            cp.start()

    return _split_start(name, list(grads) + lands, n, after, issue)


def _reduce_sibling_wait(send, recv, arrays, after, name):
    n = len(arrays) // 2

    def await_all(refs, send_ref, recv_ref):
        for cp in _sibling_copies(refs, send_ref, recv_ref, n):
            cp.wait_send()
            cp.wait_recv()

    res = _split_wait(name, arrays, send, recv, after, await_all)
    return res[:n], res[n:]


def _add_sibling_half(grad, got, dev_idx, name):
    j, r, cols = grad.shape
    rh = r // 2
    tr = 128
    nb = rh // tr

    def body(idx_ref, g_ref, got_ref, out_ref):
        out_ref[...] = (g_ref[...].astype(F32) + got_ref[...].astype(F32)).astype(BF16)

    return pl.pallas_call(
        body, name=name,
        grid_spec=pltpu.PrefetchScalarGridSpec(
            num_scalar_prefetch=1, grid=(j, nb),
            in_specs=[pl.BlockSpec((None, tr, cols), lambda jj, i, idx: (jj, idx[2] * nb + i, 0)),
                      pl.BlockSpec((None, tr, cols), lambda jj, i, idx: (jj, i, 0))],
            out_specs=pl.BlockSpec((None, tr, cols), lambda jj, i, idx: (jj, i, 0))),
        out_shape=jax.ShapeDtypeStruct((j, rh, cols), BF16),
        compiler_params=_params("parallel", "parallel"),
    )(dev_idx, grad, got)


def _chip_copies(refs, send, recv, n, receiving):
    x, y, c = _position()
    chip = 2 * x + y
    cps = []
    for t in range(n):
        for k in range(1, N_CHIPS):
            px, py = _chip_peer(x, y, k)
            cps.append(pltpu.make_async_remote_copy(
                src_ref=refs[t].at[2 * px + py], dst_ref=refs[n + t].at[2 * px + py if receiving else chip],
                send_sem=send.at[3 * t + k - 1], recv_sem=recv.at[3 * t + k - 1],
                device_id=(px, py, c), device_id_type=MESH))
    return cps


def _reduce_chips_start(partials, after, name):
    n = len(partials)
    lands = [lax.empty(p.shape, BF16) for p in partials]

    def issue(refs, send, recv):
        for cp in _chip_copies(refs, send, recv, n, False):
            cp.start()

    return _split_start(name, list(partials) + lands, 3 * n, after, issue)


def _reduce_chips_wait(send, recv, arrays, after, name):
    n = len(arrays) // 2

    def await_all(refs, send_ref, recv_ref):
        for cp in _chip_copies(refs, send_ref, recv_ref, n, True):
            cp.wait_send()
            cp.wait_recv()

    res = _split_wait(name, arrays, send, recv, after, await_all)
    return res[:n], res[n:]


def _sum_partials(land, partial, dev_idx, name):
    _, rh, cols = land.shape
    tr = 128
    nb = rh // tr

    def body(idx_ref, l_ref, p_ref, o_ref):
        chip = idx_ref[1]
        acc = jnp.where(chip == 0, p_ref[...], l_ref[0]).astype(F32)
        for s in range(1, N_CHIPS):
            acc = acc + jnp.where(chip == s, p_ref[...], l_ref[s]).astype(F32)
        o_ref[...] = acc

    return pl.pallas_call(
        body, name=name,
        grid_spec=pltpu.PrefetchScalarGridSpec(
            num_scalar_prefetch=1, grid=(nb,),
            in_specs=[pl.BlockSpec((N_CHIPS, tr, cols), lambda i, idx: (0, i, 0)),
                      pl.BlockSpec((None, tr, cols), lambda i, idx: (idx[1], i, 0))],
            out_specs=pl.BlockSpec((tr, cols), lambda i, idx: (idx[2] * nb + i, 0))),
        out_shape=jax.ShapeDtypeStruct((2 * rh, cols), F32), compiler_params=_params("parallel"),
    )(dev_idx, land, partial)


def _half_copies(refs, send, recv, receiving):
    x, y, c = _position()
    cps = []
    for t, ref in enumerate(refs):
        rh = ref.shape[0] // 2
        cps.append(pltpu.make_async_remote_copy(
            src_ref=ref.at[pl.ds(c * rh, rh)], dst_ref=ref.at[pl.ds(((1 - c) if receiving else c) * rh, rh)],
            send_sem=send.at[t], recv_sem=recv.at[t], device_id=(x, y, 1 - c), device_id_type=MESH))
    return cps


def _share_halves_start(totals, after, name):
    def issue(refs, send, recv):
        for cp in _half_copies(refs, send, recv, False):
            cp.start()

    return _split_start(name, list(totals), len(totals), after, issue)


def _share_halves_wait(send, recv, totals, after, name):
    def await_all(refs, send_ref, recv_ref):
        for cp in _half_copies(refs, send_ref, recv_ref, True):
            cp.wait_send()
            cp.wait_recv()

    return _split_wait(name, totals, send, recv, after, await_all)


def _exchange_halves(grads):
    n = len(grads)
    hbm = pl.BlockSpec(memory_space=pl.ANY)

    def body(*refs):
        ins, outs = refs[:n], refs[n:2 * n]
        send, recv = refs[2 * n:]
        x, y, c = _position()
        cps = []
        for t in range(n):
            rh = ins[t].shape[1] // 2
            cp = pltpu.make_async_remote_copy(
                src_ref=ins[t].at[pl.ds(0, N_CHIPS), pl.ds((1 - c) * rh, rh)], dst_ref=outs[t], send_sem=send.at[t],
                recv_sem=recv.at[t], device_id=(x, y, 1 - c), device_id_type=MESH)
            cp.start()
            cps.append(cp)
        for cp in cps:
            cp.wait()

    return pl.pallas_call(
        body, name="reduce_exchange_halves", in_specs=[hbm] * n, out_specs=[hbm] * n,
        out_shape=[jax.ShapeDtypeStruct((g.shape[0], g.shape[1] // 2, g.shape[2]), BF16) for g in grads],
        scratch_shapes=[pltpu.SemaphoreType.DMA((n,)), pltpu.SemaphoreType.DMA((n,))],
    )(*grads)


def _add_halves(grad, got, c_idx, name):
    j, r, cols = grad.shape
    rh = r // 2
    tr = 128
    nb = rh // tr

    def body(c_ref, g_ref, o_ref_in, out_ref):
        out_ref[...] = (g_ref[...].astype(F32) + o_ref_in[...].astype(F32)).astype(BF16)

    return pl.pallas_call(
        body, name=name,
        grid_spec=pltpu.PrefetchScalarGridSpec(
            num_scalar_prefetch=1, grid=(j, nb),
            in_specs=[pl.BlockSpec((None, tr, cols), lambda jj, i, c_ref: (jj, c_ref[0] * nb + i, 0)),
                      pl.BlockSpec((None, tr, cols), lambda jj, i, c_ref: (jj, i, 0))],
            out_specs=pl.BlockSpec((None, tr, cols), lambda jj, i, c_ref: (jj, i, 0))),
        out_shape=jax.ShapeDtypeStruct((j, rh, cols), BF16),
        compiler_params=_params("parallel", "parallel"),
    )(c_idx, grad, got)


def _scatter_partials(partials):
    n = len(partials)
    hbm = pl.BlockSpec(memory_space=pl.ANY)

    def body(*refs):
        ins, outs = refs[:n], refs[n:2 * n]
        send, recv, local = refs[2 * n:]
        x, y, c = _position()
        chip = 2 * x + y
        cps, lcs = [], []
        for t in range(n):
            lc = pltpu.make_async_copy(ins[t].at[chip], outs[t].at[chip], local.at[t])
            lc.start()
            lcs.append(lc)
            for k in range(1, N_CHIPS):
                px, py = _chip_peer(x, y, k)
                s = 3 * t + k - 1
                cp = pltpu.make_async_remote_copy(
                    src_ref=ins[t].at[2 * px + py], dst_ref=outs[t].at[chip], send_sem=send.at[s],
                    recv_sem=recv.at[s], device_id=(px, py, c), device_id_type=MESH)
                cp.start()
                cps.append(cp)
        for cp in cps:
            cp.wait()
        for lc in lcs:
            lc.wait()

    return pl.pallas_call(
        body, name="reduce_scatter_partials", in_specs=[hbm] * n, out_specs=[hbm] * n,
        out_shape=[jax.ShapeDtypeStruct(p.shape, BF16) for p in partials],
        scratch_shapes=[pltpu.SemaphoreType.DMA((3 * n,)), pltpu.SemaphoreType.DMA((3 * n,)),
                        pltpu.SemaphoreType.DMA((n,))],
    )(*partials)


def _sum_chips(parts, name):
    j, rh, cols = parts.shape
    tr = 128

    def body(p_ref, o_ref):
        acc = p_ref[0].astype(F32)
        for s in range(1, j):
            acc = acc + p_ref[s].astype(F32)
        o_ref[...] = acc

    return pl.pallas_call(
        body, name=name, grid=(rh // tr,),
        in_specs=[pl.BlockSpec((j, tr, cols), lambda i: (0, i, 0))],
        out_specs=pl.BlockSpec((tr, cols), lambda i: (i, 0)),
        out_shape=jax.ShapeDtypeStruct((rh, cols), F32),
        compiler_params=_params("parallel"),
    )(parts)


def _share_totals(halves):
    n = len(halves)
    hbm = pl.BlockSpec(memory_space=pl.ANY)

    def body(*refs):
        ins, outs = refs[:n], refs[n:2 * n]
        send, recv, local = refs[2 * n:]
        x, y, c = _position()
        cps, lcs = [], []
        for t in range(n):
            rh = ins[t].shape[0]
            mine = outs[t].at[pl.ds(c * rh, rh)]
            lc = pltpu.make_async_copy(ins[t], mine, local.at[t])
            lc.start()
            lcs.append(lc)
            cp = pltpu.make_async_remote_copy(
                src_ref=ins[t], dst_ref=mine, send_sem=send.at[t], recv_sem=recv.at[t],
                device_id=(x, y, 1 - c), device_id_type=MESH)
            cp.start()
            cps.append(cp)
        for cp in cps:
            cp.wait()
        for lc in lcs:
            lc.wait()

    return pl.pallas_call(
        body, name="reduce_share_totals", in_specs=[hbm] * n, out_specs=[hbm] * n,
        out_shape=[jax.ShapeDtypeStruct((2 * h.shape[0], h.shape[1]), F32) for h in halves],
        scratch_shapes=[pltpu.SemaphoreType.DMA((n,)), pltpu.SemaphoreType.DMA((n,)),
                        pltpu.SemaphoreType.DMA((n,))],
    )(*halves)


SMALL_ROWS = 56


def _small_copies(refs, send, recv, receiving):
    x, y, c = _position()
    me = 4 * x + 2 * y + c
    cps = []
    for k in range(1, N_DEV):
        px, py, pc = _xor_peer(x, y, c, k)
        cps.append(pltpu.make_async_remote_copy(
            src_ref=refs[0], dst_ref=refs[1].at[4 * px + 2 * py + pc if receiving else me],
            send_sem=send.at[k - 1], recv_sem=recv.at[k - 1], device_id=(px, py, pc), device_id_type=MESH))
    return cps


def _small_gather_start(packed, after):
    land = lax.empty((N_DEV,) + packed.shape, F32)

    def issue(refs, send, recv):
        for cp in _small_copies(refs, send, recv, False):
            cp.start()

    return _split_start("small_gather_start", [packed, land], N_DEV - 1, after, issue)


def _small_gather_wait(send, recv, arrays, after):
    def await_all(refs, send_ref, recv_ref):
        for cp in _small_copies(refs, send_ref, recv_ref, True):
            cp.wait_send()
            cp.wait_recv()

    return _split_wait("small_gather_wait", arrays, send, recv, after, await_all)


def _reduce_small(packed, land, silu_c):
    ns = 3 * D_MODEL // N_CHIPS

    def body(p_ref, land_ref, sc_ref, tot_ref, gw_ref, loss_ref, qk_ref, allp):
        x, y, c = _position()
        me = 4 * x + 2 * y + c
        chip = 2 * x + y
        for i in range(N_DEV):
            allp[i] = jnp.where(me == i, p_ref[...], land_ref[i])
        tot = allp[0]
        for i in range(1, N_DEV):
            tot = tot + allp[i]
        tot_ref[...] = tot
        loss_ref[...] = jnp.sum(tot[11:12, :], axis=1, keepdims=True) * (0.5 / D_MODEL)
        fold = tot[5:11, 0:HEAD_DIM]
        for h in range(1, N_HEADS):
            fold = fold + tot[5:11, h * HEAD_DIM:(h + 1) * HEAD_DIM]
        qk_ref[...] = jnp.concatenate([fold, jnp.zeros((2, HEAD_DIM), F32)], axis=0)
        sct = sc_ref[...].T
        rc = 64
        for l in range(2):
            dms = [allp[i, pl.ds(12 + 4 * l + chip, 1), :][:, :ns] for i in range(N_DEV)]
            for r0 in range(0, D_MODEL, rc):
                acc = sct[r0:r0 + rc, 0:1] * dms[0]
                for i in range(1, N_DEV):
                    acc = acc + sct[r0:r0 + rc, i:i + 1] * dms[i]
                gw_ref[l, r0:r0 + rc, :] = acc

    vm = pl.BlockSpec(memory_space=pltpu.VMEM)
    return pl.pallas_call(
        body, name="reduce_small", in_specs=[vm, vm, vm], out_specs=[vm] * 4,
        out_shape=[jax.ShapeDtypeStruct((SMALL_ROWS, D_MODEL), F32), jax.ShapeDtypeStruct((2, D_MODEL, ns), F32),
                   jax.ShapeDtypeStruct((1, 1), F32), jax.ShapeDtypeStruct((8, HEAD_DIM), F32)],
        scratch_shapes=[pltpu.VMEM((N_DEV, SMALL_ROWS, D_MODEL), F32)],
        compiler_params=pltpu.CompilerParams(vmem_limit_bytes=VMEM_LIMIT_BYTES),
    )(packed, land, silu_c)


def _reduce_big(grads, c_idx):
    names = list(grads)
    got = _exchange_halves([grads[k] for k in names])
    partials = [_add_halves(grads[k], got[i], c_idx, f"reduce_add_{k}") for i, k in enumerate(names)]
    parts = _scatter_partials(partials)
    halves = [_sum_chips(parts[i], f"reduce_sum_{k}") for i, k in enumerate(names)]
    totals = _share_totals(halves)
    return dict(zip(names, totals))


def kernel(x, c, norm_g, ada_w, ada_b, a_w_in, a_conv_w, a_conv_b, a_ln_g, a_ln_b, a_w_out, b_w_in, b_q_norm, b_k_norm, b_w_out, loss_target, m_norm_g, m_ada_w, m_ada_b, m_a_w_in, m_a_conv_w, m_a_conv_b, m_a_ln_g, m_a_ln_b, m_a_w_out, m_b_w_in, m_b_q_norm, m_b_k_norm, m_b_w_out, v_norm_g, v_ada_w, v_ada_b, v_a_w_in, v_a_conv_w, v_a_conv_b, v_a_ln_g, v_a_ln_b, v_a_w_out, v_b_w_in, v_b_q_norm, v_b_k_norm, v_b_w_out):
    chip = 2 * lax.axis_index("x") + lax.axis_index("y")
    core = lax.axis_index("c")
    chip_idx = chip.astype(jnp.int32).reshape(1)
    dev_idx = jnp.stack([2 * chip + core, chip, core]).astype(jnp.int32)

    mods, silu_c, conv_w_full = _ada_forward(c, ada_w, ada_b, a_conv_w[0])
    lands_a = [_cast_into_slot(a_w_in[0], chip_idx, "cast_a_w_in"), _cast_into_slot(a_w_out[0], chip_idx, "cast_a_w_out")]
    send_a, recv_a, lands_a, token_a = _gather_start(lands_a, mods, "gather_start_a")
    lands_b = [_cast_into_slot(b_w_in[0], chip_idx, "cast_b_w_in"), _cast_into_slot(b_w_out[0], chip_idx, "cast_b_w_out")]
    send_b, recv_b, lands_b, token_b = _gather_start(lands_b, token_a, "gather_start_b")
    mods = mods + token_b[0:2, 0:1]

    def weights_a(after):
        send, recv, lands, _ = _gather_forward(send_a, recv_a, lands_a, after, "gather_forward_a")
        w_in, w_out = _gather_wait(send, recv, lands, after, "gather_wait_a")
        return w_in, w_out.reshape(D_MODEL, D_MODEL)

    forwarded_b = []

    def weights_b(after):
        send, recv, lands, _ = forwarded_b
        w_in, w_out = _gather_wait(send, recv, lands, after, "gather_wait_b")
        return w_in, w_out.reshape(D_MODEL, D_MODEL)

    def forward_weights_b(after):
        forwarded_b.extend(_gather_forward(send_b, recv_b, lands_b, after, "gather_forward_b"))
        return forwarded_b[3]

    stage1, stage2 = {}, {}

    def send_grads(tag, dw_in, dw_out):
        grads = [dw_in, dw_out.reshape(N_CHIPS, D_MODEL // N_CHIPS, D_MODEL)]
        send, recv, arrays, token = _reduce_sibling_start(grads, dw_out, f"reduce_d2d_start_{tag}")
        stage1[tag] = (send, recv, arrays)
        return token

    def forward_grads(tag, after):
        send, recv, arrays = stage1[tag]
        grads, got = _reduce_sibling_wait(send, recv, arrays, after, f"reduce_d2d_wait_{tag}")
        partials = [_add_sibling_half(grads[i], got[i], dev_idx, f"reduce_add_{tag}_{i}") for i in range(2)]
        send, recv, arrays, token = _reduce_chips_start(partials, partials[1], f"reduce_ici_start_{tag}")
        stage2[tag] = (send, recv, arrays)
        return token

    stage3 = {}

    def sum_grads(tag, after):
        send, recv, arrays = stage2[tag]
        partials, lands = _reduce_chips_wait(send, recv, arrays, after, f"reduce_ici_wait_{tag}")
        totals = [_sum_partials(lands[i], partials[i], dev_idx, f"reduce_sum_{tag}_{i}") for i in range(2)]
        send, recv, totals, token = _share_halves_start(totals, totals[1], f"reduce_share_start_{tag}")
        stage3[tag] = (send, recv, totals)
        return token

    def finish_grads(tag, after):
        send, recv, totals = stage3[tag]
        return _share_halves_wait(send, recv, totals, after, f"reduce_share_wait_{tag}")

    grad_x, small = _local_step(
        x[0], loss_target[0], mods.reshape(2, 3, D_MODEL), norm_g, conv_w_full, a_conv_b, a_ln_g[0:1],
        a_ln_b[0:1], b_q_norm[0], b_k_norm[0], weights_a, weights_b, forward_weights_b,
        functools.partial(send_grads, "b"), functools.partial(forward_grads, "b"), functools.partial(send_grads, "a"))

    ns = 3 * D_MODEL // N_CHIPS
    pad_mod = lambda dm: jnp.pad(dm.reshape(N_CHIPS, ns), ((0, 0), (0, D_MODEL - ns)))
    packed = jnp.concatenate([
        small["dnorm_g"], small["dconv_b"], small["dln_g"], small["dln_b"], small["dq_norm"], small["dk_norm"],
        small["loss_cols"], pad_mod(small["dmod0"]), pad_mod(small["dmod1"]), small["dconv_w"],
        jnp.zeros((SMALL_ROWS - 20 - CONV_WIDTH, D_MODEL), F32)], axis=0)
    send_s, recv_s, small_arrays, token_s = _small_gather_start(packed, packed)

    given = dict(norm_g=(norm_g, m_norm_g, v_norm_g), ada_w=(ada_w, m_ada_w, v_ada_w), ada_b=(ada_b, m_ada_b, v_ada_b),
                 a_w_in=(a_w_in, m_a_w_in, v_a_w_in), a_conv_w=(a_conv_w, m_a_conv_w, v_a_conv_w),
                 a_conv_b=(a_conv_b, m_a_conv_b, v_a_conv_b), a_ln_g=(a_ln_g, m_a_ln_g, v_a_ln_g),
                 a_ln_b=(a_ln_b, m_a_ln_b, v_a_ln_b), a_w_out=(a_w_out, m_a_w_out, v_a_w_out),
                 b_w_in=(b_w_in, m_b_w_in, v_b_w_in), b_q_norm=(b_q_norm, m_b_q_norm, v_b_q_norm),
                 b_k_norm=(b_k_norm, m_b_k_norm, v_b_k_norm), b_w_out=(b_w_out, m_b_w_out, v_b_w_out))
    order = ["norm_g", "ada_w", "ada_b", "a_w_in", "a_conv_w", "a_conv_b", "a_ln_g", "a_ln_b", "a_w_out", "b_w_in",
             "b_q_norm", "b_k_norm", "b_w_out"]
    outs = {}

    def update(k, g2, after=None):
        w, m, v = given[k]
        shape2 = g2.shape
        d2, m2, v2 = _adamw(w.reshape(shape2), g2, m.reshape(shape2), v.reshape(shape2), f"adamw_{k}", after)
        outs[k] = tuple(a.reshape(w.shape) for a in (g2, d2, m2, v2))

    token = forward_grads("a", token_s)
    token = sum_grads("b", token)
    packed, land = _small_gather_wait(send_s, recv_s, small_arrays, token)
    tot, g_ada_w, loss, qk = _reduce_small(packed, land, silu_c)
    g_b_in, g_b_out = finish_grads("b", tot)
    update("b_w_in", g_b_in)
    update("b_w_out", g_b_out)
    token = sum_grads("a", outs["b_w_in"][1])
    cw = D_MODEL // N_CHIPS
    g_small = dict(
        norm_g=tot[0:2], a_conv_b=tot[2:3], a_ln_g=tot[3:4], a_ln_b=tot[4:5],
        b_q_norm=qk[0:3], b_k_norm=qk[3:6],
        ada_b=jnp.stack([tot[12:16, :ns].reshape(3 * D_MODEL), tot[16:20, :ns].reshape(3 * D_MODEL)]),
        a_conv_w=lax.dynamic_slice(tot[20:20 + CONV_WIDTH], (0, chip * cw), (CONV_WIDTH, cw)),
    )
    update("ada_w", g_ada_w.reshape(2 * D_MODEL, ns), after=token)
    for k, g2 in g_small.items():
        update(k, g2, after=token)
    g_a_in, g_a_out = finish_grads("a", outs["ada_w"][1])
    update("a_w_in", g_a_in)
    update("a_w_out", g_a_out)
    return (loss.reshape(()), grad_x[None], *[outs[k][0] for k in order], *[outs[k][1] for k in order],
            *[outs[k][2] for k in order], *[outs[k][3] for k in order])
```

```python
import functools

import jax
import jax.numpy as jnp
from jax import lax
from jax.experimental import pallas as pl
from jax.experimental.pallas import tpu as pltpu

F32 = jnp.float32
BF16 = jnp.bfloat16

SEQ = 2048
D_MODEL = 1024
CONV_WIDTH = 31
HEAD_DIM = 64
N_HEADS = 16
DILATIONS = (1, 4, 16)
ATTN_BLOCK = 128
NORM_EPS = 1e-6
NEG_INF = -1e30
N_DEV = 8
N_CHIPS = 4

ADAM_LR = 0.001
ADAM_B1 = 0.9
ADAM_B2 = 0.999
ADAM_EPS = 1e-08
ADAM_WD = 0.01
ADAM_STEP = 10

VMEM_LIMIT_BYTES = 52 * 1024 * 1024
HALO = 32
LANES = 128
MESH = pl.DeviceIdType.MESH


def _params(*sem):
    return pltpu.CompilerParams(dimension_semantics=sem or None, vmem_limit_bytes=VMEM_LIMIT_BYTES)


def _sigmoid(v):
    return 1.0 / (1.0 + jnp.exp(-v))


def _row_spec(tm, cols, col_block=0):
    return pl.BlockSpec((tm, cols), lambda i: (i, col_block))


def _vec_spec(rows, cols):
    return pl.BlockSpec((rows, cols), lambda i: (0, 0))


def _normmod(xv, g, scale, shift):
    r = lax.rsqrt(jnp.mean(xv * xv, axis=-1, keepdims=True) + NORM_EPS)
    return xv * r * g * (1.0 + scale) + shift


def _normmod_fwd(x, g, scale, shift, name):
    tm = 256

    def body(x_ref, g_ref, sc_ref, sh_ref, h_ref, ht_ref):
        h = _normmod(x_ref[...], g_ref[...], sc_ref[...], sh_ref[...])
        h_ref[...] = h.astype(BF16)
        ht_ref[...] = h.T.astype(BF16)

    return pl.pallas_call(
        body, name=name, grid=(SEQ // tm,),
        in_specs=[_row_spec(tm, D_MODEL)] + [_vec_spec(1, D_MODEL)] * 3,
        out_specs=[_row_spec(tm, D_MODEL), pl.BlockSpec((D_MODEL, tm), lambda i: (0, i))],
        out_shape=[jax.ShapeDtypeStruct((SEQ, D_MODEL), BF16), jax.ShapeDtypeStruct((D_MODEL, SEQ), BF16)],
        compiler_params=_params("parallel"),
    )(x, g, scale, shift)


def _normmod_bwd(x, g, scale, dh_parts, dres, name, part_dilations=None, gated=None):
    tm = 256
    n_parts = len(dh_parts)
    dils = part_dilations or (1,) * n_parts
    dh_parts = [p if d == 1 else p.reshape(d, SEQ // d, D_MODEL) for p, d in zip(dh_parts, dils)]
    n_gated = 0 if gated is None else 2

    def body(x_ref, g_ref, sc_ref, dres_ref, *rest):
        part_refs = rest[:n_parts]
        gated_refs = rest[n_parts:n_parts + n_gated]
        out_refs = rest[n_parts + n_gated:]
        dx_ref, sums_ref, nat = out_refs[0], out_refs[1], out_refs[-1]
        xv = x_ref[...]
        r = lax.rsqrt(jnp.mean(xv * xv, axis=-1, keepdims=True) + NORM_EPS)
        xn = xv * r
        dh = _load_natural(part_refs[0], nat, dils[0])
        for p, d in zip(part_refs[1:], dils[1:]):
            dh = dh + _load_natural(p, nat, d)
        gv = g_ref[...]
        one_sc = 1.0 + sc_ref[...]
        dxn = dh * (gv * one_sc)
        dx = dres_ref[...] + r * (dxn - xn * jnp.mean(dxn * xn, axis=-1, keepdims=True))
        dx_ref[...] = dx
        dhx = dh * xn
        rows = [jnp.sum(dhx, axis=0, keepdims=True) * one_sc,
                jnp.sum(dhx, axis=0, keepdims=True) * gv,
                jnp.sum(dh, axis=0, keepdims=True)]
        if gated is not None:
            gate_ref, y_ref = gated_refs
            out_refs[2][...] = (dx * gate_ref[...]).astype(BF16)
            rows.append(jnp.sum(dx * y_ref[...], axis=0, keepdims=True))
        sums = jnp.concatenate(rows + [jnp.zeros((8 - len(rows), D_MODEL), F32)], axis=0)

        @pl.when(pl.program_id(0) == 0)
        def _():
            sums_ref[...] = jnp.zeros_like(sums_ref)

        sums_ref[...] += sums

    gated_specs = [] if gated is None else [_vec_spec(1, D_MODEL), _row_spec(tm, D_MODEL)]
    dy_spec = [] if gated is None else [_row_spec(tm, D_MODEL)]
    dy_shape = [] if gated is None else [jax.ShapeDtypeStruct((SEQ, D_MODEL), BF16)]
    return pl.pallas_call(
        body, name=name, grid=(SEQ // tm,),
        in_specs=[_row_spec(tm, D_MODEL), _vec_spec(1, D_MODEL), _vec_spec(1, D_MODEL), _row_spec(tm, D_MODEL)]
        + [_class_spec(tm, d) for d in dils] + gated_specs,
        out_specs=[_row_spec(tm, D_MODEL), _vec_spec(8, D_MODEL)] + dy_spec,
        out_shape=[jax.ShapeDtypeStruct((SEQ, D_MODEL), F32), jax.ShapeDtypeStruct((8, D_MODEL), F32)] + dy_shape,
        scratch_shapes=[_natural_scratch(tm)],
        compiler_params=_params("arbitrary"),
    )(x, g, scale, dres, *dh_parts, *(gated or ()))


def _mm(lhs, rhs, *, tn, tile0, n_tiles, out_dtype, name, out3d=None, prev=None):
    mo, kc = lhs.shape
    cm = min(mo, 1024)

    def body(l_ref, r_ref, *rest):
        o_ref = rest[-1]
        for m in range(mo // cm):
            rows = pl.ds(m * cm, cm)
            o_ref[rows, :] = jnp.dot(l_ref[rows, :], r_ref[...], preferred_element_type=F32).astype(out_dtype)

    if rhs.ndim == 3:
        tps_r = rhs.shape[2] // tn
        r_spec = pl.BlockSpec((None, kc, tn), lambda t: ((tile0 + t) // tps_r, 0, (tile0 + t) % tps_r))
    else:
        r_spec = pl.BlockSpec((kc, tn), lambda t: (0, t))
    in_specs = [pl.BlockSpec((mo, kc), lambda t: (0, 0)), r_spec]
    args = [lhs, rhs]
    aliases = {}
    if out3d is None:
        o_spec = pl.BlockSpec((mo, tn), lambda t: (0, t))
        o_shape = jax.ShapeDtypeStruct((mo, n_tiles * tn), out_dtype)
    else:
        j_out, ns_out = out3d
        tps_o = ns_out // tn
        o_spec = pl.BlockSpec((None, mo, tn), lambda t: ((tile0 + t) // tps_o, 0, (tile0 + t) % tps_o))
        o_shape = jax.ShapeDtypeStruct((j_out, mo, ns_out), out_dtype)
        if prev is not None:
            in_specs.append(pl.BlockSpec(memory_space=pl.ANY))
            args.append(prev)
            aliases = {2: 0}
    return pl.pallas_call(
        body, name=name, grid=(n_tiles,), in_specs=in_specs, out_specs=o_spec, out_shape=o_shape,
        input_output_aliases=aliases, compiler_params=_params("parallel"),
    )(*args)


def _mm_nt(dy, w3, *, tn, tile0, n_tiles, name, after=None):
    m_rows = dy.shape[0]
    _, kc, ns = w3.shape
    tps = ns // tn
    cm = 512
    extra = [] if after is None else [after]

    def body(dy_ref, w_ref, *rest):
        o_ref = rest[-1]

        @pl.when(pl.program_id(0) == 0)
        def _():
            o_ref[...] = jnp.zeros_like(o_ref)

        for m in range(m_rows // cm):
            rows = pl.ds(m * cm, cm)
            o_ref[rows, :] += lax.dot_general(dy_ref[rows, :], w_ref[...], (((1,), (1,)), ((), ())),
                                              preferred_element_type=F32)

    return pl.pallas_call(
        body, name=name, grid=(n_tiles,),
        in_specs=[pl.BlockSpec((m_rows, tn), lambda t: (0, t)),
                  pl.BlockSpec((None, kc, tn), lambda t: ((tile0 + t) // tps, 0, (tile0 + t) % tps))]
        + [pl.BlockSpec(memory_space=pl.ANY)] * len(extra),
        out_specs=pl.BlockSpec((m_rows, kc), lambda t: (0, 0)),
        out_shape=jax.ShapeDtypeStruct((m_rows, kc), F32),
        compiler_params=_params("arbitrary"),
    )(dy, w3, *extra)


CONV_CHUNK = 16


def _shift_copies(buf, shifted):
    rows = shifted.shape[1]
    for s in range(1, 8):
        shifted[s - 1] = buf[pl.ds(s, rows), :]


def _shifted_rows(buf, shifted, offset, r0):
    s = offset % 8
    if s == 0:
        return buf[pl.ds(r0 + offset, CONV_CHUNK), :]
    return shifted[s - 1, pl.ds(r0 + (offset - s), CONV_CHUNK), :]


def _spread_taps(w_ref, taps):
    for k in range(CONV_WIDTH):
        taps[k] = jnp.broadcast_to(w_ref[k:k + 1, :], (8, D_MODEL))


def _times_tap(taps, k, rows):
    return (rows.reshape(CONV_CHUNK // 8, 8, D_MODEL) * taps[k][None]).reshape(CONV_CHUNK, D_MODEL)


def _conv_fwd(proj, conv_w, conv_b, ln_g, ln_b, name):
    tm = 256
    hb = tm // HALO

    def body(vg_ref, halo_ref, z_ref, w_ref, b_ref, g_ref, be_ref, u5_ref, u5t_ref, u2_ref, buf, shifted, taps):
        i = pl.program_id(0)
        u1 = vg_ref[:, :D_MODEL] * _sigmoid(vg_ref[:, D_MODEL:])
        u1h = halo_ref[:, :D_MODEL] * _sigmoid(halo_ref[:, D_MODEL:])
        buf[pl.ds(0, HALO), :] = jnp.where(i > 0, u1h, 0.0)
        buf[pl.ds(HALO, tm), :] = u1
        _shift_copies(buf, shifted)
        _spread_taps(w_ref, taps)

        def chunk(ci, carry):
            r0 = pl.multiple_of(ci * CONV_CHUNK, CONV_CHUNK)
            acc = jnp.broadcast_to(b_ref[...], (CONV_CHUNK, D_MODEL))
            for k in range(CONV_WIDTH):
                acc = acc + _times_tap(taps, k, _shifted_rows(buf, shifted, HALO - (CONV_WIDTH - 1) + k, r0))
            u2_ref[pl.ds(r0, CONV_CHUNK), :] = acc
            return carry

        lax.fori_loop(0, tm // CONV_CHUNK, chunk, 0)
        acc = u2_ref[...]
        mu = jnp.mean(acc, axis=-1, keepdims=True)
        xc = acc - mu
        rstd = lax.rsqrt(jnp.mean(xc * xc, axis=-1, keepdims=True) + NORM_EPS)
        u3 = xc * rstd * g_ref[...] + be_ref[...]
        zv = z_ref[...]
        u5 = u3 * _sigmoid(u3) * (zv * _sigmoid(zv))
        u5_ref[...] = u5.astype(BF16)
        u5t_ref[...] = u5.T.astype(BF16)

    return pl.pallas_call(
        body, name=name, grid=(SEQ // tm,),
        in_specs=[pl.BlockSpec((tm, 2 * D_MODEL), lambda i: (i, 0)),
                  pl.BlockSpec((HALO, 2 * D_MODEL), lambda i: (jnp.maximum(i * hb - 1, 0), 0)),
                  _row_spec(tm, D_MODEL, 2),
                  _vec_spec(CONV_WIDTH, D_MODEL)] + [_vec_spec(1, D_MODEL)] * 3,
        out_specs=[_row_spec(tm, D_MODEL), pl.BlockSpec((D_MODEL, tm), lambda i: (0, i)), _row_spec(tm, D_MODEL)],
        out_shape=[jax.ShapeDtypeStruct((SEQ, D_MODEL), BF16), jax.ShapeDtypeStruct((D_MODEL, SEQ), BF16),
                   jax.ShapeDtypeStruct((SEQ, D_MODEL), F32)],
        scratch_shapes=[pltpu.VMEM((HALO + tm, D_MODEL), F32), pltpu.VMEM((7, HALO + tm - 8, D_MODEL), F32),
                        pltpu.VMEM((CONV_WIDTH, 8, D_MODEL), F32)],
        compiler_params=_params("parallel"),
    )(proj, proj, proj, conv_w, conv_b, ln_g, ln_b)


def _conv_bwd_pointwise(dy, w_out, proj, u2, ln_g, ln_b, name):
    tm = 256

    def body(dy_ref, w_ref, z_ref, u2_ref, g_ref, be_ref, du2_ref, dz_ref, sums_ref):
        u2v = u2_ref[...]
        mu = jnp.mean(u2v, axis=-1, keepdims=True)
        xc = u2v - mu
        rstd = lax.rsqrt(jnp.mean(xc * xc, axis=-1, keepdims=True) + NORM_EPS)
        xhat = xc * rstd
        u3 = xhat * g_ref[...] + be_ref[...]
        s3 = _sigmoid(u3)
        u4 = u3 * s3
        zv = z_ref[...]
        sz = _sigmoid(zv)
        du5v = lax.dot_general(dy_ref[...], w_ref[...], NT_DIMS, preferred_element_type=F32)
        dz_ref[...] = du5v * u4 * (sz * (1.0 + zv * (1.0 - sz)))
        du3 = du5v * (zv * sz) * (s3 * (1.0 + u3 * (1.0 - s3)))
        dxhat = du3 * g_ref[...]
        du2 = rstd * (dxhat - jnp.mean(dxhat, axis=-1, keepdims=True)
                      - xhat * jnp.mean(dxhat * xhat, axis=-1, keepdims=True))
        du2_ref[...] = du2
        sums = jnp.concatenate([
            jnp.sum(du3 * xhat, axis=0, keepdims=True),
            jnp.sum(du3, axis=0, keepdims=True),
            jnp.sum(du2, axis=0, keepdims=True),
            jnp.zeros((5, D_MODEL), F32)], axis=0)

        @pl.when(pl.program_id(0) == 0)
        def _():
            sums_ref[...] = jnp.zeros_like(sums_ref)

        sums_ref[...] += sums

    return pl.pallas_call(
        body, name=name, grid=(SEQ // tm,),
        in_specs=[_row_spec(tm, D_MODEL), _vec_spec(D_MODEL, D_MODEL), _row_spec(tm, D_MODEL, 2),
                  _row_spec(tm, D_MODEL), _vec_spec(1, D_MODEL), _vec_spec(1, D_MODEL)],
        out_specs=[_row_spec(tm, D_MODEL), _row_spec(tm, D_MODEL), _vec_spec(8, D_MODEL)],
        out_shape=[jax.ShapeDtypeStruct((SEQ, D_MODEL), F32), jax.ShapeDtypeStruct((SEQ, D_MODEL), F32),
                   jax.ShapeDtypeStruct((8, D_MODEL), F32)],
        compiler_params=_params("arbitrary"),
    )(dy, w_out, proj, u2, ln_g, ln_b)


def _conv_bwd_taps(du2, dz, proj, conv_w, name):
    tm = 256
    hb = tm // HALO
    n_blocks = SEQ // tm

    def body(du2_ref, dnext_ref, dz_ref, vg_ref, w_ref, dproj_ref, dw_ref, dbuf, dshift, sgbuf, ubuf, dwacc, taps):
        i = pl.program_id(0)
        _spread_taps(w_ref, taps)
        sg = _sigmoid(vg_ref[:, D_MODEL:])
        sgbuf[...] = sg
        ubuf[...] = vg_ref[:, :D_MODEL] * sg
        dbuf[pl.ds(0, tm), :] = du2_ref[...]
        dbuf[pl.ds(tm, HALO), :] = jnp.where(i < n_blocks - 1, dnext_ref[...], 0.0)
        _shift_copies(dbuf, dshift)

        @pl.when(i == 0)
        def _():
            dwacc[...] = jnp.zeros_like(dwacc)

        def chunk(ci, carry):
            r0 = pl.multiple_of(ci * CONV_CHUNK, CONV_CHUNK)
            rows = pl.ds(r0, CONV_CHUNK)
            u1c = ubuf[rows, :]
            du1 = jnp.zeros((CONV_CHUNK, D_MODEL), F32)
            for k in range(CONV_WIDTH):
                ahead = _shifted_rows(dbuf, dshift, CONV_WIDTH - 1 - k, r0)
                du1 = du1 + _times_tap(taps, k, ahead)
                prod = u1c * ahead
                dwacc[k] += prod[0:8] + prod[8:16]
            sgc = sgbuf[rows, :]
            dval = du1 * sgc
            dproj_ref[rows, 0:D_MODEL] = dval.astype(BF16)
            dproj_ref[rows, D_MODEL:2 * D_MODEL] = (dval * vg_ref[rows, 0:D_MODEL] * (1.0 - sgc)).astype(BF16)
            return carry

        lax.fori_loop(0, tm // CONV_CHUNK, chunk, 0)
        dproj_ref[:, 2 * D_MODEL:] = dz_ref[...].astype(BF16)

        @pl.when(i == n_blocks - 1)
        def _():
            for k in range(CONV_WIDTH):
                dw_ref[k:k + 1, :] = jnp.sum(dwacc[k], axis=0, keepdims=True)
            dw_ref[CONV_WIDTH:, :] = jnp.zeros((32 - CONV_WIDTH, D_MODEL), F32)

    return pl.pallas_call(
        body, name=name, grid=(n_blocks,),
        in_specs=[_row_spec(tm, D_MODEL),
                  pl.BlockSpec((HALO, D_MODEL), lambda i: (jnp.minimum((i + 1) * hb, SEQ // HALO - 1), 0)),
                  _row_spec(tm, D_MODEL),
                  pl.BlockSpec((tm, 2 * D_MODEL), lambda i: (i, 0)),
                  _vec_spec(CONV_WIDTH, D_MODEL)],
        out_specs=[_row_spec(tm, 3 * D_MODEL), _vec_spec(32, D_MODEL)],
        out_shape=[jax.ShapeDtypeStruct((SEQ, 3 * D_MODEL), BF16), jax.ShapeDtypeStruct((32, D_MODEL), F32)],
        scratch_shapes=[pltpu.VMEM((tm + HALO, D_MODEL), F32), pltpu.VMEM((7, HALO + tm - 8, D_MODEL), F32),
                        pltpu.VMEM((tm, D_MODEL), F32), pltpu.VMEM((tm, D_MODEL), F32),
                        pltpu.VMEM((CONV_WIDTH, 8, D_MODEL), F32), pltpu.VMEM((CONV_WIDTH, 8, D_MODEL), F32)],
        compiler_params=_params("arbitrary"),
    )(du2, du2, dz, proj, conv_w)


def _out_a(u5, w_out, x, gate, g1, scale1, shift1, name):
    tm = 256
    n_d = len(DILATIONS)

    def body(u_ref, w_ref, x_ref, gate_ref, g_ref, sc_ref, sh_ref, x1_ref, y_ref, ht_ref, *rest):
        h_refs, nat = rest[:n_d], rest[-1]
        y = jnp.dot(u_ref[...], w_ref[...], preferred_element_type=F32)
        x1 = x_ref[...] + gate_ref[...] * y
        y_ref[...] = y
        x1_ref[...] = x1
        h = _normmod(x1, g_ref[...], sc_ref[...], sh_ref[...])
        ht_ref[...] = h.T.astype(BF16)
        for h_ref, d in zip(h_refs, DILATIONS):
            _store_classes(h_ref, h, nat, d)

    res = pl.pallas_call(
        body, name=name, grid=(SEQ // tm,),
        in_specs=[_row_spec(tm, D_MODEL), _vec_spec(D_MODEL, D_MODEL), _row_spec(tm, D_MODEL)]
        + [_vec_spec(1, D_MODEL)] * 4,
        out_specs=[_row_spec(tm, D_MODEL), _row_spec(tm, D_MODEL), pl.BlockSpec((D_MODEL, tm), lambda i: (0, i))]
        + [_class_spec(tm, d) for d in DILATIONS],
        out_shape=[jax.ShapeDtypeStruct((SEQ, D_MODEL), F32), jax.ShapeDtypeStruct((SEQ, D_MODEL), F32),
                   jax.ShapeDtypeStruct((D_MODEL, SEQ), BF16)] + [_class_shape(d, BF16) for d in DILATIONS],
        scratch_shapes=[_natural_scratch(tm)],
        compiler_params=_params("parallel"),
    )(u5, w_out, x, gate, g1, scale1, shift1)
    return res[0], res[1], res[2], [a.reshape(SEQ, D_MODEL) for a in res[3:]]


def _out_b_loss(u, w_out, x1, gate, target, name):
    tm = 256

    def body(u_ref, w_ref, x_ref, gate_ref, t_ref, e_ref, dy_ref, sums_ref):
        y = jnp.dot(u_ref[...], w_ref[...], preferred_element_type=F32)
        diff = x_ref[...] + gate_ref[...] * y - t_ref[...]
        e = diff * (1.0 / D_MODEL)
        e_ref[...] = e
        dy_ref[...] = (e * gate_ref[...]).astype(BF16)
        sums = jnp.concatenate([
            jnp.sum(e * y, axis=0, keepdims=True),
            jnp.sum(diff * diff, axis=0, keepdims=True),
            jnp.zeros((6, D_MODEL), F32)], axis=0)

        @pl.when(pl.program_id(0) == 0)
        def _():
            sums_ref[...] = jnp.zeros_like(sums_ref)

        sums_ref[...] += sums

    return pl.pallas_call(
        body, name=name, grid=(SEQ // tm,),
        in_specs=[_row_spec(tm, D_MODEL), _vec_spec(D_MODEL, D_MODEL), _row_spec(tm, D_MODEL),
                  _vec_spec(1, D_MODEL), _row_spec(tm, D_MODEL)],
        out_specs=[_row_spec(tm, D_MODEL), _row_spec(tm, D_MODEL), _vec_spec(8, D_MODEL)],
        out_shape=[jax.ShapeDtypeStruct((SEQ, D_MODEL), F32), jax.ShapeDtypeStruct((SEQ, D_MODEL), BF16),
                   jax.ShapeDtypeStruct((8, D_MODEL), F32)],
        compiler_params=_params("arbitrary"),
    )(u, w_out, x1, gate, target)


def _dgate_dy(dx1, y, gate, name):
    tm = 256

    def body(d_ref, y_ref, gate_ref, dy_ref, sums_ref):
        dv = d_ref[...]
        dy_ref[...] = (dv * gate_ref[...]).astype(BF16)
        sums = jnp.concatenate([jnp.sum(dv * y_ref[...], axis=0, keepdims=True), jnp.zeros((7, D_MODEL), F32)], axis=0)

        @pl.when(pl.program_id(0) == 0)
        def _():
            sums_ref[...] = jnp.zeros_like(sums_ref)

        sums_ref[...] += sums

    return pl.pallas_call(
        body, name=name, grid=(SEQ // tm,),
        in_specs=[_row_spec(tm, D_MODEL), _row_spec(tm, D_MODEL), _vec_spec(1, D_MODEL)],
        out_specs=[_row_spec(tm, D_MODEL), _vec_spec(8, D_MODEL)],
        out_shape=[jax.ShapeDtypeStruct((SEQ, D_MODEL), BF16), jax.ShapeDtypeStruct((8, D_MODEL), F32)],
        compiler_params=_params("arbitrary"),
    )(dx1, y, gate)


def _mm_nt_res(dy, w, name):
    tm = 256
    kc, n = w.shape

    def body(dy_ref, w_ref, o_ref):
        o_ref[...] = lax.dot_general(dy_ref[...], w_ref[...], (((1,), (1,)), ((), ())), preferred_element_type=F32)

    return pl.pallas_call(
        body, name=name, grid=(SEQ // tm,),
        in_specs=[_row_spec(tm, n), _vec_spec(kc, n)],
        out_specs=_row_spec(tm, kc),
        out_shape=jax.ShapeDtypeStruct((SEQ, kc), F32),
        compiler_params=_params("parallel"),
    )(dy, w)


def _seg_matrix():
    r = lax.broadcasted_iota(jnp.int32, (256, 256), 0) // HEAD_DIM
    c = lax.broadcasted_iota(jnp.int32, (256, 256), 1) // HEAD_DIM
    return (r == c).astype(BF16)


def _segsum(v, seg):
    hi = v.astype(BF16)
    lo = (v - hi.astype(F32)).astype(BF16)
    outs = []
    for c0 in range(0, D_MODEL, 256):
        outs.append(jnp.dot(hi[:, c0:c0 + 256], seg, preferred_element_type=F32)
                    + jnp.dot(lo[:, c0:c0 + 256], seg, preferred_element_type=F32))
    return jnp.concatenate(outs, axis=1)


def _qk_rstd(v, seg):
    return lax.rsqrt(_segsum(v * v, seg) * (1.0 / HEAD_DIM) + NORM_EPS)


def _qknorm_fwd(proj, qw, kw, seg, name):
    tm = 256

    def body(p_ref, qw_ref, kw_ref, seg_ref, q_ref, k_ref):
        segv = seg_ref[...]
        q = p_ref[:, :D_MODEL].astype(F32)
        k = p_ref[:, D_MODEL:].astype(F32)
        q_ref[...] = (q * _qk_rstd(q, segv) * qw_ref[...]).astype(BF16)
        k_ref[...] = (k * _qk_rstd(k, segv) * kw_ref[...]).astype(BF16)

    return pl.pallas_call(
        body, name=name, grid=(SEQ // tm,),
        in_specs=[_row_spec(tm, 2 * D_MODEL), _vec_spec(1, D_MODEL), _vec_spec(1, D_MODEL), _vec_spec(256, 256)],
        out_specs=[_row_spec(tm, D_MODEL)] * 2,
        out_shape=[jax.ShapeDtypeStruct((SEQ, D_MODEL), BF16)] * 2,
        compiler_params=_params("parallel"),
    )(proj, qw, kw, seg)


def _attn_masks(b, bpc, dilation, slope):
    if bpc == 1:
        qi = lax.broadcasted_iota(jnp.int32, (ATTN_BLOCK, ATTN_BLOCK), 0)
        kj = lax.broadcasted_iota(jnp.int32, (ATTN_BLOCK, ATTN_BLOCK), 1)
        steps = qi - kj
        return (steps * dilation).astype(F32), steps >= 0
    qi = lax.broadcasted_iota(jnp.int32, (ATTN_BLOCK, 2 * ATTN_BLOCK), 0)
    kj = lax.broadcasted_iota(jnp.int32, (ATTN_BLOCK, 2 * ATTN_BLOCK), 1)
    steps = qi + ATTN_BLOCK - kj
    has_prev = (b % bpc) != 0
    valid = (steps >= 0) & (steps <= ATTN_BLOCK) & (has_prev | (kj >= ATTN_BLOCK))
    return (steps * dilation).astype(F32), valid


def _key_tile(prev_ref, cur_ref, cols, bpc):
    if bpc == 1:
        return cur_ref[:, cols]
    return jnp.concatenate([prev_ref[:, cols], cur_ref[:, cols]], axis=0)


ATTN_HEADS_FWD = 16
ATTN_HEADS_BWD = 16
NT_DIMS = (((1,), (1,)), ((), ()))
TN_DIMS = (((0,), (0,)), ((), ()))
BATCH_NT_DIMS = (((2,), (2,)), ((0,), (0,)))
BATCH_NN_DIMS = (((2,), (1,)), ((0,), (0,)))
BATCH_TN_DIMS = (((1,), (1,)), ((0,), (0,)))


def _head_stack(tile_of, heads):
    return jnp.stack([tile_of(slice(h * HEAD_DIM, (h + 1) * HEAD_DIM)) for h in range(heads)], axis=0)


def _attn_specs(heads, segment=0):
    width = heads * HEAD_DIM
    off = segment * (D_MODEL // width)
    last = SEQ // ATTN_BLOCK - 1
    cur = pl.BlockSpec((ATTN_BLOCK, width), lambda hg, b: (jnp.minimum(b, last), hg + off))
    prev = pl.BlockSpec((ATTN_BLOCK, width), lambda hg, b: (jnp.clip(b - 1, 0, last), hg + off))
    return cur, prev


def _attn_fwd(q, k, proj, slopes, dilation, name):
    bpc = SEQ // dilation // ATTN_BLOCK
    heads = ATTN_HEADS_FWD
    assert heads == N_HEADS
    cur, prev = _attn_specs(heads)
    v_cur, v_prev = _attn_specs(heads, segment=2)
    scale = HEAD_DIM ** -0.5

    def body(sl_ref, q_ref, kp_ref, kc_ref, vp_ref, vc_ref, o_ref, lse_ref):
        dist, valid = _attn_masks(pl.program_id(1), bpc, dilation, None)
        q3 = _head_stack(lambda cols: q_ref[:, cols], heads)
        k3 = _head_stack(lambda cols: _key_tile(kp_ref, kc_ref, cols, bpc), heads)
        v3 = _head_stack(lambda cols: _key_tile(vp_ref, vc_ref, cols, bpc), heads)
        s = lax.dot_general(q3, k3, BATCH_NT_DIMS, preferred_element_type=F32)
        s = jnp.where(valid[None], s * scale - dist[None] * sl_ref[...], NEG_INF)
        m = jnp.max(s, axis=-1, keepdims=True)
        p = jnp.exp(s - m)
        l = jnp.sum(p, axis=-1, keepdims=True)
        o3 = lax.dot_general(p.astype(BF16), v3, BATCH_NN_DIMS, preferred_element_type=F32) / l
        lse3 = m + jnp.log(l)
        for h in range(heads):
            o_ref[:, h * HEAD_DIM:(h + 1) * HEAD_DIM] = o3[h]
        lse_ref[...] = jnp.concatenate([lse3[h] for h in range(heads)]
                                       + [jnp.zeros((ATTN_BLOCK, LANES - heads), F32)], axis=1)

    return pl.pallas_call(
        body, name=name, grid=(N_HEADS // heads, SEQ // ATTN_BLOCK),
        in_specs=[pl.BlockSpec((heads, 1, 1), lambda hg, b: (hg, 0, 0)), cur, prev, cur, v_prev, v_cur],
        out_specs=[cur, pl.BlockSpec((ATTN_BLOCK, LANES), lambda hg, b: (b, 0))],
        out_shape=[jax.ShapeDtypeStruct((SEQ, D_MODEL), F32), jax.ShapeDtypeStruct((SEQ, LANES), F32)],
        compiler_params=_params("parallel", "parallel"),
    )(slopes.reshape(N_HEADS, 1, 1), q, k, k, proj, proj)


def _class_spec(tm, dilation, width=D_MODEL):
    if dilation == 1:
        return _row_spec(tm, width)
    return pl.BlockSpec((dilation, tm // dilation, width), lambda i: (0, i, 0))


def _class_shape(dilation, dtype, width=D_MODEL):
    if dilation == 1:
        return jax.ShapeDtypeStruct((SEQ, width), dtype)
    return jax.ShapeDtypeStruct((dilation, SEQ // dilation, width), dtype)


def _load_natural(in_ref, nat_ref, dilation):
    if dilation == 1:
        return in_ref[...].astype(F32)
    n = nat_ref.shape[1] // dilation
    tiles = in_ref.shape[-1] // LANES
    for r in range(dilation):
        for j in range(tiles):
            nat_ref.at[j][pl.ds(r, n, stride=dilation), :] = in_ref[r, :, j * LANES:(j + 1) * LANES].astype(F32)
    if tiles == 1:
        return nat_ref[0]
    return jnp.concatenate([nat_ref[j] for j in range(tiles)], axis=1)


def _store_classes(out_ref, value, nat_ref, dilation):
    if dilation == 1:
        out_ref[...] = value.astype(out_ref.dtype)
        return
    n = nat_ref.shape[1] // dilation
    tiles = value.shape[-1] // LANES
    for j in range(tiles):
        nat_ref[j] = value[:, j * LANES:(j + 1) * LANES]
    for r in range(dilation):
        for j in range(tiles):
            out_ref[r, :, j * LANES:(j + 1) * LANES] = (
                nat_ref.at[j][pl.ds(r, n, stride=dilation), :].astype(out_ref.dtype))


def _natural_scratch(tm):
    return pltpu.VMEM((D_MODEL // LANES, tm, LANES), F32)


def _head_selector():
    lane_head = lax.broadcasted_iota(jnp.int32, (D_MODEL, LANES), 0) // HEAD_DIM
    head = lax.broadcasted_iota(jnp.int32, (D_MODEL, LANES), 1)
    return (lane_head == head).astype(BF16)


def _dot_split(v, m01, dims):
    hi = v.astype(BF16)
    lo = (v - hi.astype(F32)).astype(BF16)
    return (lax.dot_general(hi, m01, dims, preferred_element_type=F32)
            + lax.dot_general(lo, m01, dims, preferred_element_type=F32))


def _merge_fwd(o_parts, lse_parts, z, sel, name):
    tm = 256
    h_spec = pl.BlockSpec((tm, LANES), lambda i: (i, 0))

    def body(o0, o1, o2, l0, l1, l2, z_ref, sel_ref, u_ref, ut_ref, o_ref, lse_ref, nat):
        ls = [_load_natural(l, nat, d) for l, d in zip((l0, l1, l2), DILATIONS)]
        m = jnp.maximum(jnp.maximum(ls[0], ls[1]), ls[2])
        tot = m + jnp.log(jnp.exp(ls[0] - m) + jnp.exp(ls[1] - m) + jnp.exp(ls[2] - m))
        o = jnp.zeros((tm, D_MODEL), F32)
        for o_in, l, d in zip((o0, o1, o2), ls, DILATIONS):
            weight = _dot_split(jnp.exp(l - tot), sel_ref[...], NT_DIMS)
            o = o + weight * _load_natural(o_in, nat, d)
        zv = z_ref[...]
        u = o * (zv * _sigmoid(zv))
        u_ref[...] = u.astype(BF16)
        ut_ref[...] = u.T.astype(BF16)
        o_ref[...] = o
        lse_ref[...] = tot

    return pl.pallas_call(
        body, name=name, grid=(SEQ // tm,),
        in_specs=[_class_spec(tm, d) for d in DILATIONS] + [_class_spec(tm, d, LANES) for d in DILATIONS]
        + [_row_spec(tm, D_MODEL), _vec_spec(D_MODEL, LANES)],
        out_specs=[_row_spec(tm, D_MODEL), pl.BlockSpec((D_MODEL, tm), lambda i: (0, i)),
                   _row_spec(tm, D_MODEL), h_spec],
        out_shape=[jax.ShapeDtypeStruct((SEQ, D_MODEL), BF16), jax.ShapeDtypeStruct((D_MODEL, SEQ), BF16),
                   jax.ShapeDtypeStruct((SEQ, D_MODEL), F32), jax.ShapeDtypeStruct((SEQ, LANES), F32)],
        scratch_shapes=[_natural_scratch(tm)],
        compiler_params=_params("parallel"),
    )(*o_parts, *lse_parts, z, sel)


def _merge_bwd(dy, w_out, o, lse, z, sel, name):
    tm = 256
    n_d = len(DILATIONS)

    def body(dy_ref, w_ref, o_ref, lse_ref, z_ref, sel_ref, dz_ref, *rest):
        do_refs, delta_refs, lse_refs, nat = rest[:n_d], rest[n_d:2 * n_d], rest[2 * n_d:3 * n_d], rest[-1]
        zv = z_ref[...]
        sz = _sigmoid(zv)
        duv = lax.dot_general(dy_ref[...], w_ref[...], NT_DIMS, preferred_element_type=F32)
        ov = o_ref[...]
        do = duv * (zv * sz)
        dz_ref[...] = (duv * ov * (sz * (1.0 + zv * (1.0 - sz)))).astype(BF16)
        delta = _dot_split(do * ov, sel_ref[...], (((1,), (0,)), ((), ())))
        lv = lse_ref[...]
        for i, d in enumerate(DILATIONS):
            _store_classes(do_refs[i], do, nat, d)
            _store_classes(delta_refs[i], delta, nat, d)
            _store_classes(lse_refs[i], lv, nat, d)

    res = pl.pallas_call(
        body, name=name, grid=(SEQ // tm,),
        in_specs=[_row_spec(tm, D_MODEL), _vec_spec(D_MODEL, D_MODEL), _row_spec(tm, D_MODEL), _row_spec(tm, LANES),
                  _row_spec(tm, D_MODEL), _vec_spec(D_MODEL, LANES)],
        out_specs=[_row_spec(tm, D_MODEL)] + [_class_spec(tm, d) for d in DILATIONS]
        + [_class_spec(tm, d, LANES) for d in DILATIONS] * 2,
        out_shape=[jax.ShapeDtypeStruct((SEQ, D_MODEL), BF16)] + [_class_shape(d, BF16) for d in DILATIONS]
        + [_class_shape(d, F32, LANES) for d in DILATIONS] * 2,
        scratch_shapes=[_natural_scratch(tm)],
        compiler_params=_params("parallel"),
    )(dy, w_out, o, lse, z, sel)
    flat = lambda a: a.reshape(SEQ, a.shape[-1])
    return (res[0], [flat(a) for a in res[1:1 + n_d]], [flat(a) for a in res[1 + n_d:1 + 2 * n_d]],
            [flat(a) for a in res[1 + 2 * n_d:]])


def _attn_bwd(q, k, proj, do, lse, delta, slopes, dilation, name):
    bpc = SEQ // dilation // ATTN_BLOCK
    heads = ATTN_HEADS_BWD
    n_blocks = SEQ // ATTN_BLOCK
    carry = bpc > 1
    width = heads * HEAD_DIM
    cur, prev = _attn_specs(heads)
    v_cur, v_prev = _attn_specs(heads, segment=2)
    assert heads == N_HEADS
    per_head = pl.BlockSpec((ATTN_BLOCK, LANES), lambda hg, b: (jnp.minimum(b, n_blocks - 1), 0))
    scale = HEAD_DIM ** -0.5

    def body(sl_ref, q_ref, kp_ref, kc_ref, vp_ref, vc_ref, do_ref, lse_ref, dl_ref,
             dq_ref, dk_ref, dv_ref, *scratch):
        b = pl.program_id(1)
        if carry:
            dk_carry, dv_carry = scratch

            @pl.when(b == n_blocks)
            def _():
                dk_ref[...] = dk_carry[...].astype(BF16)
                dv_ref[...] = dv_carry[...].astype(BF16)

            @pl.when(b < n_blocks)
            def _():
                step(sl_ref, q_ref, kp_ref, kc_ref, vp_ref, vc_ref, do_ref, lse_ref, dl_ref,
                     dq_ref, dk_ref, dv_ref, dk_carry, dv_carry, b)
        else:
            step(sl_ref, q_ref, kp_ref, kc_ref, vp_ref, vc_ref, do_ref, lse_ref, dl_ref,
                 dq_ref, dk_ref, dv_ref, None, None, b)

    def step(sl_ref, q_ref, kp_ref, kc_ref, vp_ref, vc_ref, do_ref, lse_ref, dl_ref,
             dq_ref, dk_ref, dv_ref, dk_carry, dv_carry, b):
        if carry:
            @pl.when(b == 0)
            def _():
                dk_carry[...] = jnp.zeros_like(dk_carry)
                dv_carry[...] = jnp.zeros_like(dv_carry)

        dist, valid = _attn_masks(b, bpc, dilation, None)
        q3 = _head_stack(lambda cols: q_ref[:, cols], heads)
        k3 = _head_stack(lambda cols: _key_tile(kp_ref, kc_ref, cols, bpc), heads)
        v3 = _head_stack(lambda cols: _key_tile(vp_ref, vc_ref, cols, bpc), heads)
        do3 = _head_stack(lambda cols: do_ref[:, cols], heads)
        lse3 = jnp.stack([lse_ref[:, h:h + 1] for h in range(heads)], axis=0)
        dl3 = jnp.stack([dl_ref[:, h:h + 1] for h in range(heads)], axis=0)
        s = lax.dot_general(q3, k3, BATCH_NT_DIMS, preferred_element_type=F32)
        p = jnp.exp(jnp.where(valid[None], s * scale - dist[None] * sl_ref[...], NEG_INF) - lse3)
        dp = lax.dot_general(do3, v3, BATCH_NT_DIMS, preferred_element_type=F32)
        ds = (p * (dp - dl3) * scale).astype(BF16)
        dq3 = lax.dot_general(ds, k3, BATCH_NN_DIMS, preferred_element_type=F32)
        dk3 = lax.dot_general(ds, q3, BATCH_TN_DIMS, preferred_element_type=F32)
        dv3 = lax.dot_general(p.astype(BF16), do3, BATCH_TN_DIMS, preferred_element_type=F32)
        for h in range(heads):
            cols = slice(h * HEAD_DIM, (h + 1) * HEAD_DIM)
            dq_ref[:, cols] = dq3[h].astype(BF16)
            if carry:
                dk_ref[:, cols] = (dk_carry[:, cols] + dk3[h, :ATTN_BLOCK]).astype(BF16)
                dv_ref[:, cols] = (dv_carry[:, cols] + dv3[h, :ATTN_BLOCK]).astype(BF16)
                dk_carry[:, cols] = dk3[h, ATTN_BLOCK:]
                dv_carry[:, cols] = dv3[h, ATTN_BLOCK:]
            else:
                dk_ref[:, cols] = dk3[h].astype(BF16)
                dv_ref[:, cols] = dv3[h].astype(BF16)

    kv_out = prev if carry else cur
    return pl.pallas_call(
        body, name=name, grid=(N_HEADS // heads, n_blocks + (1 if carry else 0)),
        in_specs=[pl.BlockSpec((heads, 1, 1), lambda hg, b: (hg, 0, 0)), cur, prev, cur, v_prev, v_cur,
                  cur, per_head, per_head],
        out_specs=[cur, kv_out, kv_out],
        out_shape=[jax.ShapeDtypeStruct((SEQ, D_MODEL), BF16)] * 3,
        scratch_shapes=[pltpu.VMEM((ATTN_BLOCK, width), F32)] * 2 if carry else [],
        compiler_params=_params("parallel", "arbitrary"),
    )(slopes.reshape(N_HEADS, 1, 1), q, k, k, proj, proj, do, lse, delta)


def _qknorm_bwd(proj, qw, kw, seg, dq, dk, dv, name):
    tm = 256

    def body(p_ref, qw_ref, kw_ref, seg_ref, dq_ref, dk_ref, dv_ref, dproj_ref, sums_ref):
        segv = seg_ref[...]
        sums = []
        for part, (w_ref, dn_ref) in enumerate(((qw_ref, dq_ref), (kw_ref, dk_ref))):
            raw = p_ref[:, part * D_MODEL:(part + 1) * D_MODEL].astype(F32)
            dn = dn_ref[...].astype(F32)
            r = _qk_rstd(raw, segv)
            gq = dn * w_ref[...]
            draw = r * gq - raw * (r * r * r) * (_segsum(raw * gq, segv) * (1.0 / HEAD_DIM))
            dproj_ref[:, part * D_MODEL:(part + 1) * D_MODEL] = draw.astype(BF16)
            sums.append(jnp.sum(dn * raw * r, axis=0, keepdims=True))
        dproj_ref[:, 2 * D_MODEL:] = dv_ref[...]

        @pl.when(pl.program_id(0) == 0)
        def _():
            sums_ref[...] = jnp.zeros_like(sums_ref)

        sums_ref[...] += jnp.concatenate(sums + [jnp.zeros((6, D_MODEL), F32)], axis=0)

    return pl.pallas_call(
        body, name=name, grid=(SEQ // tm,),
        in_specs=[_row_spec(tm, 3 * D_MODEL), _vec_spec(1, D_MODEL), _vec_spec(1, D_MODEL), _vec_spec(256, 256)]
        + [_row_spec(tm, D_MODEL)] * 3,
        out_specs=[_row_spec(tm, 3 * D_MODEL), _vec_spec(8, D_MODEL)],
        out_shape=[jax.ShapeDtypeStruct((SEQ, 3 * D_MODEL), BF16), jax.ShapeDtypeStruct((8, D_MODEL), F32)],
        compiler_params=_params("arbitrary"),
    )(proj, qw, kw, seg, dq, dk, dv)


def _to_classes(a, dilation):
    if dilation == 1:
        return a
    s, c = a.shape
    return a.reshape(s // dilation, dilation, c).transpose(1, 0, 2).reshape(s, c)


def _from_classes(a, dilation):
    if dilation == 1:
        return a
    s, c = a.shape
    return a.reshape(dilation, s // dilation, c).transpose(1, 0, 2).reshape(s, c)


def _cols_to_classes(a, dilation):
    if dilation == 1:
        return a
    r, s = a.shape
    return a.reshape(r, s // dilation, dilation).transpose(0, 2, 1).reshape(r, s)


B_TN = 512
B_GROUP_TILES = 3 * D_MODEL // B_TN
B_Z_TILE0 = 3 * B_GROUP_TILES
B_Z_TILES = D_MODEL // B_TN


def _local_step(x, target, mods, norm_g, conv_w, conv_b, ln_g, ln_b, q_norm, k_norm,
                weights_a, weights_b, forward_weights_b, send_grads_b, forward_grads_b, send_grads_a):
    row = lambda a, i: a[i:i + 1]
    shift0, scale0, gate0 = row(mods[0], 0), row(mods[0], 1), row(mods[0], 2)
    shift1, scale1, gate1 = row(mods[1], 0), row(mods[1], 1), row(mods[1], 2)
    g0, g1 = row(norm_g, 0), row(norm_g, 1)
    seg = _seg_matrix()
    slopes = jnp.exp2(-8.0 * jnp.arange(1, N_HEADS + 1, dtype=F32) / N_HEADS)
    qw = [jnp.tile(q_norm[g:g + 1], (1, N_HEADS)) for g in range(3)]
    kw = [jnp.tile(k_norm[g:g + 1], (1, N_HEADS)) for g in range(3)]

    h0, h0t = _normmod_fwd(x, g0, scale0, shift0, "prenorm0")
    wa_in, wa_out = weights_a(h0)
    ja, _, nsa = wa_in.shape
    proj_a = _mm(h0, wa_in, tn=nsa, tile0=0, n_tiles=ja, out_dtype=F32, name="a_in")
    u5, u5t, u2 = _conv_fwd(proj_a, conv_w, conv_b, ln_g, ln_b, "a_conv")
    token = forward_weights_b(u5)
    x1, y_a, h1t, h1c = _out_a(u5, wa_out, x, gate0 + token[0:1, 0:1], g1, scale1, shift1, "a_out")

    wb_in, wb_out = weights_b(x1)
    jb, _, nsb = wb_in.shape
    h1 = h1c[0]
    h1tc = [_cols_to_classes(h1t, d) for d in DILATIONS]
    z_b = _mm(h1, wb_in, tn=B_TN, tile0=B_Z_TILE0, n_tiles=B_Z_TILES, out_dtype=F32, name="b_in_z")
    proj_g, qkv, o_parts, lse_parts = [], [], [], []
    for g, d in enumerate(DILATIONS):
        pg = _mm(h1c[g], wb_in, tn=B_TN, tile0=g * B_GROUP_TILES, n_tiles=B_GROUP_TILES, out_dtype=BF16,
                 name=f"b_in_g{g}")
        qn, kn = _qknorm_fwd(pg, qw[g], kw[g], seg, f"b_qknorm_g{g}")
        og, lg = _attn_fwd(qn, kn, pg, slopes, d, f"b_attn_g{g}")
        proj_g.append(pg)
        qkv.append((qn, kn))
        o_parts.append(og if d == 1 else og.reshape(d, SEQ // d, D_MODEL))
        lse_parts.append(lg if d == 1 else lg.reshape(d, SEQ // d, LANES))
    sel = _head_selector()
    u_b, u_bt, o_b, lse_b = _merge_fwd(o_parts, lse_parts, z_b, sel, "b_merge")
    e, dy_b, sums_loss = _out_b_loss(u_b, wb_out, x1, gate1, target, "b_out_loss")

    dwb_out = _mm(u_bt, dy_b, tn=D_MODEL, tile0=0, n_tiles=1, out_dtype=BF16, name="b_dwout")
    dz_b, do_c, delta_c, lse_c = _merge_bwd(dy_b, wb_out, o_b, lse_b, z_b, sel, "b_merge_bwd")
    dwb_in = _mm(h1t, dz_b, tn=B_TN, tile0=B_Z_TILE0, n_tiles=B_Z_TILES, out_dtype=BF16, name="b_dwin_z",
                 out3d=(jb, nsb))
    dh1_parts = [_mm_nt(dz_b, wb_in, tn=B_TN, tile0=B_Z_TILE0, n_tiles=B_Z_TILES, name="b_dh_z")]
    qk_sums = []
    for g, d in enumerate(DILATIONS):
        qn, kn = qkv[g]
        dq, dk, dv = _attn_bwd(qn, kn, proj_g[g], do_c[g], lse_c[g], delta_c[g], slopes, d, f"b_attn_bwd_g{g}")
        dproj, sums_qk = _qknorm_bwd(proj_g[g], qw[g], kw[g], seg, dq, dk, dv, f"b_qknorm_bwd_g{g}")
        qk_sums.append(sums_qk)
        dwb_in = _mm(h1tc[g], dproj, tn=B_TN, tile0=g * B_GROUP_TILES, n_tiles=B_GROUP_TILES, out_dtype=BF16,
                     name=f"b_dwin_g{g}", out3d=(jb, nsb), prev=dwb_in)
        dh = _mm_nt(dproj, wb_in, tn=B_TN, tile0=g * B_GROUP_TILES, n_tiles=B_GROUP_TILES, name=f"b_dh_g{g}")
        dh1_parts.append(dh)
    token = send_grads_b(dwb_in, dwb_out)
    dx1, sums_n1, dy_a = _normmod_bwd(x1, g1, scale1 + token[0:1, 0:1], dh1_parts, e, "prenorm1_bwd",
                                      part_dilations=(1,) + DILATIONS, gated=(gate0, y_a))
    token = forward_grads_b(dx1)

    dwa_out = _mm(u5t, dy_a, tn=D_MODEL, tile0=0, n_tiles=1, out_dtype=BF16, name="a_dwout")
    du2, dz_a, sums_ln = _conv_bwd_pointwise(dy_a, wa_out, proj_a, u2, ln_g + token[0:1, 0:1], ln_b,
                                             "a_conv_bwd_pw")
    dproj_a, dconv_w = _conv_bwd_taps(du2, dz_a, proj_a, conv_w, "a_conv_bwd_taps")
    dwa_in = _mm(h0t, dproj_a, tn=nsa, tile0=0, n_tiles=ja, out_dtype=BF16, name="a_dwin", out3d=(ja, nsa))
    token = send_grads_a(dwa_in, dwa_out)
    dh0 = _mm_nt(dproj_a, wa_in, tn=nsa, tile0=0, n_tiles=ja, name="a_dh", after=token)
    grad_x, sums_n0 = _normmod_bwd(x, g0, scale0, [dh0], dx1, "prenorm0_bwd")

    small = dict(
        dnorm_g=jnp.concatenate([sums_n0[0:1], sums_n1[0:1]], axis=0),
        dmod0=jnp.concatenate([sums_n0[2:3], sums_n0[1:2], sums_n1[3:4]], axis=0),
        dmod1=jnp.concatenate([sums_n1[2:3], sums_n1[1:2], sums_loss[0:1]], axis=0),
        dln_g=sums_ln[0:1], dln_b=sums_ln[1:2], dconv_b=sums_ln[2:3],
        dconv_w=dconv_w[:CONV_WIDTH],
        dq_norm=jnp.concatenate([s[0:1] for s in qk_sums], axis=0),
        dk_norm=jnp.concatenate([s[1:2] for s in qk_sums], axis=0),
        loss_cols=sums_loss[1:2],
    )
    return grad_x, small


def _adamw(w, g, m, v, name, after=None):
    rows, cols = w.shape
    tr = rows if rows <= 128 else 128
    c1 = 1.0 / (1.0 - ADAM_B1 ** ADAM_STEP)
    c2 = 1.0 / (1.0 - ADAM_B2 ** ADAM_STEP)
    extra = [] if after is None else [after]

    def body(w_ref, g_ref, m_ref, v_ref, *rest):
        d_ref, mo_ref, vo_ref = rest[len(extra):]
        gv = g_ref[...]
        mn = ADAM_B1 * m_ref[...] + (1.0 - ADAM_B1) * gv
        vn = ADAM_B2 * v_ref[...] + (1.0 - ADAM_B2) * (gv * gv)
        mo_ref[...] = mn
        vo_ref[...] = vn
        d_ref[...] = -ADAM_LR * ((mn * c1) / (jnp.sqrt(vn * c2) + ADAM_EPS) + ADAM_WD * w_ref[...])

    spec = pl.BlockSpec((tr, cols), lambda i: (i, 0))
    return pl.pallas_call(
        body, name=name, grid=(rows // tr,),
        in_specs=[spec] * 4 + [pl.BlockSpec(memory_space=pl.ANY)] * len(extra), out_specs=[spec] * 3,
        out_shape=[jax.ShapeDtypeStruct((rows, cols), F32)] * 3,
        compiler_params=_params("parallel"),
    )(w, g, m, v, *extra)


def _cast_into_slot(w, chip_idx, name):
    rows, cols = w.shape
    tr = 256

    def body(ch_ref, w_ref, o_ref):
        o_ref[...] = w_ref[...].astype(BF16)

    return pl.pallas_call(
        body, name=name,
        grid_spec=pltpu.PrefetchScalarGridSpec(
            num_scalar_prefetch=1, grid=(rows // tr,),
            in_specs=[pl.BlockSpec((tr, cols), lambda i, ch: (i, 0))],
            out_specs=pl.BlockSpec((None, tr, cols), lambda i, ch: (ch[0], i, 0))),
        out_shape=jax.ShapeDtypeStruct((N_CHIPS, rows, cols), BF16), compiler_params=_params("parallel"),
    )(chip_idx, w)


def _position():
    x, y, c = lax.axis_index("x"), lax.axis_index("y"), lax.axis_index("c")
    return x, y, c


def _xor_peer(x, y, c, k):
    return (x ^ ((k >> 2) & 1), y ^ ((k >> 1) & 1), c ^ (k & 1))


def _chip_peer(x, y, k):
    return (x ^ ((k >> 1) & 1), y ^ (k & 1))


def _ada_forward(c_row, ada_w, ada_b, conv_w):
    ns = ada_w.shape[2]
    cw = conv_w.shape[1]

    def body(c_ref, w_ref, b_ref, cv_ref, mod_ref, sc_ref, cvo_ref,
             c_all, mp, parts, cv_parts, send1, recv1, send2, recv2, send3, recv3):
        x, y, c = _position()
        me = 4 * x + 2 * y + c
        chip = 2 * x + y

        def c_copy(k):
            return pltpu.make_async_remote_copy(
                src_ref=c_all.at[me], dst_ref=c_all.at[me], send_sem=send1.at[k - 1], recv_sem=recv1.at[k - 1],
                device_id=_xor_peer(x, y, c, k), device_id_type=MESH)

        def cv_copy(k):
            px, py = _chip_peer(x, y, k)
            return pltpu.make_async_remote_copy(
                src_ref=cv_parts.at[chip], dst_ref=cv_parts.at[chip], send_sem=send3.at[k - 1],
                recv_sem=recv3.at[k - 1], device_id=(px, py, c), device_id_type=MESH)

        c_all[me] = c_ref[...]
        cv_parts[chip] = cv_ref[...]
        for k in range(1, N_DEV):
            c_copy(k).start()
        for k in range(1, N_CHIPS):
            cv_copy(k).start()
        for k in range(1, N_DEV):
            c_copy(k).wait_recv()
        cv = jnp.concatenate([c_all[i] for i in range(N_DEV)], axis=0)
        sc = cv * _sigmoid(cv)
        sc_ref[...] = sc
        for l in range(2):
            res = jnp.dot(sc, w_ref[l], preferred_element_type=F32, precision=lax.Precision.HIGHEST)
            for i in range(N_DEV):
                mp[i, l:l + 1, :] = res[i:i + 1, :]

        def mod_copy(k):
            px, py = _chip_peer(x, y, k)
            return pltpu.make_async_remote_copy(
                src_ref=mp.at[4 * px + 2 * py + c], dst_ref=parts.at[chip], send_sem=send2.at[k - 1],
                recv_sem=recv2.at[k - 1], device_id=(px, py, c), device_id_type=MESH)

        for k in range(1, N_CHIPS):
            mod_copy(k).start()
        parts[chip] = mp[me]
        for k in range(1, N_CHIPS):
            mod_copy(k).wait_recv()
            cv_copy(k).wait_recv()
        mod_ref[...] = jnp.concatenate([parts[j] for j in range(N_CHIPS)], axis=1) + b_ref[...]
        cvo_ref[...] = jnp.concatenate([cv_parts[j] for j in range(N_CHIPS)], axis=1)
        for k in range(1, N_DEV):
            c_copy(k).wait_send()
        for k in range(1, N_CHIPS):
            mod_copy(k).wait_send()
            cv_copy(k).wait_send()

    vm = pl.BlockSpec(memory_space=pltpu.VMEM)
    return pl.pallas_call(
        body, name="ada_forward",
        in_specs=[vm] * 4, out_specs=[vm] * 3,
        out_shape=[jax.ShapeDtypeStruct((2, 3 * D_MODEL), F32), jax.ShapeDtypeStruct((N_DEV, D_MODEL), F32),
                   jax.ShapeDtypeStruct((CONV_WIDTH, N_CHIPS * cw), F32)],
        scratch_shapes=[pltpu.VMEM((N_DEV, 1, D_MODEL), F32), pltpu.VMEM((N_DEV, 2, ns), F32),
                        pltpu.VMEM((N_CHIPS, 2, ns), F32), pltpu.VMEM((N_CHIPS, CONV_WIDTH, cw), F32),
                        pltpu.SemaphoreType.DMA((N_DEV - 1,)), pltpu.SemaphoreType.DMA((N_DEV - 1,)),
                        pltpu.SemaphoreType.DMA((N_CHIPS - 1,)), pltpu.SemaphoreType.DMA((N_CHIPS - 1,)),
                        pltpu.SemaphoreType.DMA((N_CHIPS - 1,)), pltpu.SemaphoreType.DMA((N_CHIPS - 1,))],
        compiler_params=pltpu.CompilerParams(vmem_limit_bytes=VMEM_LIMIT_BYTES),
    )(c_row, ada_w, ada_b, conv_w)


HBM_SPEC = pl.BlockSpec(memory_space=pltpu.HBM)
ANY_SPEC = pl.BlockSpec(memory_space=pl.ANY)
SEM_SPEC = pl.BlockSpec(memory_space=pltpu.SEMAPHORE)
SPLIT_PARAMS = dict(compiler_params=pltpu.CompilerParams(has_side_effects=pltpu.SideEffectType.DATAFLOW_SIDE_EFFECTING))
TOKEN = jax.ShapeDtypeStruct((8, 128), F32)


def _hbm(arrays):
    return [pltpu.with_memory_space_constraint(a, pltpu.HBM) for a in arrays]


def _hbm_like(arrays):
    return [pltpu.HBM(a.shape, a.dtype) for a in arrays]


def _gather_start(lands, after, name):
    n = len(lands)

    def body(*refs):
        ins = refs[:n]
        send, recv = refs[n + 1], refs[n + 2]
        x, y, c = _position()
        chip = 2 * x + y
        for t in range(n):
            rh = ins[t].shape[1] // 2
            for k in range(1, N_CHIPS):
                px, py = _chip_peer(x, y, k)
                block = ins[t].at[chip, pl.ds(c * rh, rh)]
                pltpu.make_async_remote_copy(
                    src_ref=block, dst_ref=block, send_sem=send.at[3 * t + k - 1], recv_sem=recv.at[3 * t + k - 1],
                    device_id=(px, py, c), device_id_type=MESH).start()
        refs[-1][...] = jnp.zeros(TOKEN.shape, F32)

    res = pl.pallas_call(
        body, name=name, in_specs=[HBM_SPEC] * n + [ANY_SPEC],
        out_specs=(SEM_SPEC, SEM_SPEC, *[HBM_SPEC] * n, pl.BlockSpec(memory_space=pltpu.VMEM)),
        out_shape=(pltpu.SemaphoreType.DMA((3 * n,)), pltpu.SemaphoreType.DMA((3 * n,)), *_hbm_like(lands), TOKEN),
        input_output_aliases={t: 2 + t for t in range(n)}, **SPLIT_PARAMS,
    )(*_hbm(lands), after)
    return res[0], res[1], list(res[2:2 + n]), res[-1]


def _gather_forward(send, recv, lands, after, name):
    n = len(lands)

    def body(*refs):
        ins = refs[:n]
        send1, recv1 = refs[n], refs[n + 1]
        send2, recv2 = refs[n + 3], refs[n + 4]
        x, y, c = _position()
        chip = 2 * x + y
        for t in range(n):
            rh = ins[t].shape[1] // 2
            half = pl.ds(c * rh, rh)
            for k in range(1, N_CHIPS):
                px, py = _chip_peer(x, y, k)
                s = 3 * t + k - 1
                got = ins[t].at[2 * px + py, half]
                cp = pltpu.make_async_remote_copy(
                    src_ref=ins[t].at[chip, half], dst_ref=got, send_sem=send1.at[s], recv_sem=recv1.at[s],
                    device_id=(px, py, c), device_id_type=MESH)
                cp.wait_send()
                cp.wait_recv()
                pltpu.make_async_remote_copy(
                    src_ref=got, dst_ref=got, send_sem=send2.at[s], recv_sem=recv2.at[s],
                    device_id=(x, y, 1 - c), device_id_type=MESH).start()
        refs[-1][...] = jnp.zeros(TOKEN.shape, F32)

    res = pl.pallas_call(
        body, name=name, in_specs=[HBM_SPEC] * n + [SEM_SPEC, SEM_SPEC, ANY_SPEC],
        out_specs=(SEM_SPEC, SEM_SPEC, *[HBM_SPEC] * n, pl.BlockSpec(memory_space=pltpu.VMEM)),
        out_shape=(pltpu.SemaphoreType.DMA((3 * n,)), pltpu.SemaphoreType.DMA((3 * n,)), *_hbm_like(lands), TOKEN),
        input_output_aliases={t: 2 + t for t in range(n)}, **SPLIT_PARAMS,
    )(*lands, send, recv, after)
    return res[0], res[1], list(res[2:2 + n]), res[-1]


def _gather_wait(send, recv, lands, after, name):
    n = len(lands)

    def body(*refs):
        ins = refs[:n]
        send_ref, recv_ref = refs[n], refs[n + 1]
        x, y, c = _position()
        for t in range(n):
            rh = ins[t].shape[1] // 2
            for k in range(1, N_CHIPS):
                px, py = _chip_peer(x, y, k)
                cp = pltpu.make_async_remote_copy(
                    src_ref=ins[t].at[2 * px + py, pl.ds(c * rh, rh)],
                    dst_ref=ins[t].at[2 * px + py, pl.ds((1 - c) * rh, rh)], send_sem=send_ref.at[3 * t + k - 1],
                    recv_sem=recv_ref.at[3 * t + k - 1], device_id=(x, y, 1 - c), device_id_type=MESH)
                cp.wait_send()
                cp.wait_recv()

    res = pl.pallas_call(
        body, name=name, in_specs=[HBM_SPEC] * n + [SEM_SPEC, SEM_SPEC, ANY_SPEC], out_specs=[HBM_SPEC] * n,
        out_shape=_hbm_like(lands), input_output_aliases={t: t for t in range(n)}, **SPLIT_PARAMS,
    )(*lands, send, recv, after)
    return list(res)


def _reduce_start(grads, after, name):
    n = len(grads)
    lands = [lax.empty((N_DEV, g.shape[1] // 2, g.shape[2]), BF16) for g in grads]

    def body(*refs):
        gs, ls = refs[:n], refs[n:2 * n]
        send, recv = refs[2 * n + 1], refs[2 * n + 2]
        x, y, c = _position()
        me = 4 * x + 2 * y + c
        for t in range(n):
            rh = gs[t].shape[1] // 2
            for k in range(1, N_DEV):
                px, py, pc = _xor_peer(x, y, c, k)
                pltpu.make_async_remote_copy(
                    src_ref=gs[t].at[2 * px + py, pl.ds(pc * rh, rh)], dst_ref=ls[t].at[me],
                    send_sem=send.at[7 * t + k - 1], recv_sem=recv.at[7 * t + k - 1],
                    device_id=(px, py, pc), device_id_type=MESH).start()
        refs[-1][...] = jnp.zeros(TOKEN.shape, F32)

    res = pl.pallas_call(
        body, name=name, in_specs=[HBM_SPEC] * (2 * n) + [ANY_SPEC],
        out_specs=(SEM_SPEC, SEM_SPEC, *[HBM_SPEC] * (2 * n), pl.BlockSpec(memory_space=pltpu.VMEM)),
        out_shape=(pltpu.SemaphoreType.DMA((7 * n,)), pltpu.SemaphoreType.DMA((7 * n,)),
                   *_hbm_like(grads), *_hbm_like(lands), TOKEN),
        input_output_aliases={t: 2 + t for t in range(2 * n)}, **SPLIT_PARAMS,
    )(*_hbm(grads), *_hbm(lands), after)
    return res[0], res[1], list(res[2:2 + n]), list(res[2 + n:2 + 2 * n]), res[-1]


def _reduce_wait(send, recv, grads, lands, after, name):
    n = len(grads)

    def body(*refs):
        gs, ls = refs[:n], refs[n:2 * n]
        send_ref, recv_ref = refs[2 * n], refs[2 * n + 1]
        x, y, c = _position()
        for t in range(n):
            rh = gs[t].shape[1] // 2
            for k in range(1, N_DEV):
                px, py, pc = _xor_peer(x, y, c, k)
                cp = pltpu.make_async_remote_copy(
                    src_ref=gs[t].at[2 * px + py, pl.ds(pc * rh, rh)], dst_ref=ls[t].at[4 * px + 2 * py + pc],
                    send_sem=send_ref.at[7 * t + k - 1], recv_sem=recv_ref.at[7 * t + k - 1],
                    device_id=(px, py, pc), device_id_type=MESH)
                cp.wait_send()
                cp.wait_recv()

    res = pl.pallas_call(
        body, name=name, in_specs=[HBM_SPEC] * (2 * n) + [SEM_SPEC, SEM_SPEC, ANY_SPEC], out_specs=[HBM_SPEC] * (2 * n),
        out_shape=_hbm_like(grads) + _hbm_like(lands), input_output_aliases={t: t for t in range(2 * n)}, **SPLIT_PARAMS,
    )(*grads, *lands, send, recv, after)
    return list(res[:n]), list(res[n:])


def _sum_devices(land, grad, dev_idx, name):
    _, rh, cols = land.shape
    tr = 128
    nb = rh // tr

    def body(idx_ref, l_ref, g_ref, o_ref):
        me = idx_ref[0]
        acc = jnp.where(me == 0, g_ref[...], l_ref[0]).astype(F32)
        for d in range(1, N_DEV):
            acc = acc + jnp.where(me == d, g_ref[...], l_ref[d]).astype(F32)
        o_ref[...] = acc

    return pl.pallas_call(
        body, name=name,
        grid_spec=pltpu.PrefetchScalarGridSpec(
            num_scalar_prefetch=1, grid=(nb,),
            in_specs=[pl.BlockSpec((N_DEV, tr, cols), lambda i, idx: (0, i, 0)),
                      pl.BlockSpec((None, tr, cols), lambda i, idx: (idx[1], idx[2] * nb + i, 0))],
            out_specs=pl.BlockSpec((tr, cols), lambda i, idx: (idx[2] * nb + i, 0))),
        out_shape=jax.ShapeDtypeStruct((2 * rh, cols), F32), compiler_params=_params("parallel"),
    )(dev_idx, land, grad)


def _split_start(name, arrays, n_sems, after, issue):
    m = len(arrays)

    def body(*refs):
        issue(refs[:m], refs[m + 1], refs[m + 2])
        refs[-1][...] = jnp.zeros(TOKEN.shape, F32)

    res = pl.pallas_call(
        body, name=name, in_specs=[HBM_SPEC] * m + [ANY_SPEC],
        out_specs=(SEM_SPEC, SEM_SPEC, *[HBM_SPEC] * m, pl.BlockSpec(memory_space=pltpu.VMEM)),
        out_shape=(pltpu.SemaphoreType.DMA((n_sems,)), pltpu.SemaphoreType.DMA((n_sems,)), *_hbm_like(arrays), TOKEN),
        input_output_aliases={t: 2 + t for t in range(m)}, **SPLIT_PARAMS,
    )(*_hbm(arrays), after)
    return res[0], res[1], list(res[2:2 + m]), res[-1]


def _split_wait(name, arrays, send, recv, after, await_all):
    m = len(arrays)

    def body(*refs):
        await_all(refs[:m], refs[m], refs[m + 1])

    res = pl.pallas_call(
        body, name=name, in_specs=[HBM_SPEC] * m + [SEM_SPEC, SEM_SPEC, ANY_SPEC], out_specs=[HBM_SPEC] * m,
        out_shape=_hbm_like(arrays), input_output_aliases={t: t for t in range(m)}, **SPLIT_PARAMS,
    )(*arrays, send, recv, after)
    return list(res)


def _sibling_copies(refs, send, recv, n):
    x, y, c = _position()
    cps = []
    for t in range(n):
        rh = refs[t].shape[1] // 2
        cps.append(pltpu.make_async_remote_copy(
            src_ref=refs[t].at[pl.ds(0, N_CHIPS), pl.ds((1 - c) * rh, rh)], dst_ref=refs[n + t],
            send_sem=send.at[t], recv_sem=recv.at[t], device_id=(x, y, 1 - c), device_id_type=MESH))
    return cps


def _reduce_sibling_start(grads, after, name):
    n = len(grads)
    lands = [lax.empty((N_CHIPS, g.shape[1] // 2, g.shape[2]), BF16) for g in grads]

    def issue(refs, send, recv):
        for cp in _sibling_copies(refs, send, recv, n):
            cp.start()

    return _split_start(name, list(grads) + lands, n, after, issue)


def _reduce_sibling_wait(send, recv, arrays, after, name):
    n = len(arrays) // 2

    def await_all(refs, send_ref, recv_ref):
        for cp in _sibling_copies(refs, send_ref, recv_ref, n):
            cp.wait_send()
            cp.wait_recv()

    res = _split_wait(name, arrays, send, recv, after, await_all)
    return res[:n], res[n:]


def _add_sibling_half(grad, got, dev_idx, name):
    j, r, cols = grad.shape
    rh = r // 2
    tr = 128
    nb = rh // tr

    def body(idx_ref, g_ref, got_ref, out_ref):
        out_ref[...] = (g_ref[...].astype(F32) + got_ref[...].astype(F32)).astype(BF16)

    return pl.pallas_call(
        body, name=name,
        grid_spec=pltpu.PrefetchScalarGridSpec(
            num_scalar_prefetch=1, grid=(j, nb),
            in_specs=[pl.BlockSpec((None, tr, cols), lambda jj, i, idx: (jj, idx[2] * nb + i, 0)),
                      pl.BlockSpec((None, tr, cols), lambda jj, i, idx: (jj, i, 0))],
            out_specs=pl.BlockSpec((None, tr, cols), lambda jj, i, idx: (jj, i, 0))),
        out_shape=jax.ShapeDtypeStruct((j, rh, cols), BF16),
        compiler_params=_params("parallel", "parallel"),
    )(dev_idx, grad, got)


def _chip_copies(refs, send, recv, n, receiving):
    x, y, c = _position()
    chip = 2 * x + y
    cps = []
    for t in range(n):
        for k in range(1, N_CHIPS):
            px, py = _chip_peer(x, y, k)
            cps.append(pltpu.make_async_remote_copy(
                src_ref=refs[t].at[2 * px + py], dst_ref=refs[n + t].at[2 * px + py if receiving else chip],
                send_sem=send.at[3 * t + k - 1], recv_sem=recv.at[3 * t + k - 1],
                device_id=(px, py, c), device_id_type=MESH))
    return cps


def _reduce_chips_start(partials, after, name):
    n = len(partials)
    lands = [lax.empty(p.shape, BF16) for p in partials]

    def issue(refs, send, recv):
        for cp in _chip_copies(refs, send, recv, n, False):
            cp.start()

    return _split_start(name, list(partials) + lands, 3 * n, after, issue)


def _reduce_chips_wait(send, recv, arrays, after, name):
    n = len(arrays) // 2

    def await_all(refs, send_ref, recv_ref):
        for cp in _chip_copies(refs, send_ref, recv_ref, n, True):
            cp.wait_send()
            cp.wait_recv()

    res = _split_wait(name, arrays, send, recv, after, await_all)
    return res[:n], res[n:]


def _sum_partials(land, partial, dev_idx, name):
    _, rh, cols = land.shape
    tr = 128
    nb = rh // tr

    def body(idx_ref, l_ref, p_ref, o_ref):
        chip = idx_ref[1]
        acc = jnp.where(chip == 0, p_ref[...], l_ref[0]).astype(F32)
        for s in range(1, N_CHIPS):
            acc = acc + jnp.where(chip == s, p_ref[...], l_ref[s]).astype(F32)
        o_ref[...] = acc

    return pl.pallas_call(
        body, name=name,
        grid_spec=pltpu.PrefetchScalarGridSpec(
            num_scalar_prefetch=1, grid=(nb,),
            in_specs=[pl.BlockSpec((N_CHIPS, tr, cols), lambda i, idx: (0, i, 0)),
                      pl.BlockSpec((None, tr, cols), lambda i, idx: (idx[1], i, 0))],
            out_specs=pl.BlockSpec((tr, cols), lambda i, idx: (idx[2] * nb + i, 0))),
        out_shape=jax.ShapeDtypeStruct((2 * rh, cols), F32), compiler_params=_params("parallel"),
    )(dev_idx, land, partial)


def _half_copies(refs, send, recv, receiving):
    x, y, c = _position()
    cps = []
    for t, ref in enumerate(refs):
        rh = ref.shape[0] // 2
        cps.append(pltpu.make_async_remote_copy(
            src_ref=ref.at[pl.ds(c * rh, rh)], dst_ref=ref.at[pl.ds(((1 - c) if receiving else c) * rh, rh)],
            send_sem=send.at[t], recv_sem=recv.at[t], device_id=(x, y, 1 - c), device_id_type=MESH))
    return cps


def _share_halves_start(totals, after, name):
    def issue(refs, send, recv):
        for cp in _half_copies(refs, send, recv, False):
            cp.start()

    return _split_start(name, list(totals), len(totals), after, issue)


def _share_halves_wait(send, recv, totals, after, name):
    def await_all(refs, send_ref, recv_ref):
        for cp in _half_copies(refs, send_ref, recv_ref, True):
            cp.wait_send()
            cp.wait_recv()

    return _split_wait(name, totals, send, recv, after, await_all)


def _exchange_halves(grads):
    n = len(grads)
    hbm = pl.BlockSpec(memory_space=pl.ANY)

    def body(*refs):
        ins, outs = refs[:n], refs[n:2 * n]
        send, recv = refs[2 * n:]
        x, y, c = _position()
        cps = []
        for t in range(n):
            rh = ins[t].shape[1] // 2
            cp = pltpu.make_async_remote_copy(
                src_ref=ins[t].at[pl.ds(0, N_CHIPS), pl.ds((1 - c) * rh, rh)], dst_ref=outs[t], send_sem=send.at[t],
                recv_sem=recv.at[t], device_id=(x, y, 1 - c), device_id_type=MESH)
            cp.start()
            cps.append(cp)
        for cp in cps:
            cp.wait()

    return pl.pallas_call(
        body, name="reduce_exchange_halves", in_specs=[hbm] * n, out_specs=[hbm] * n,
        out_shape=[jax.ShapeDtypeStruct((g.shape[0], g.shape[1] // 2, g.shape[2]), BF16) for g in grads],
        scratch_shapes=[pltpu.SemaphoreType.DMA((n,)), pltpu.SemaphoreType.DMA((n,))],
    )(*grads)


def _add_halves(grad, got, c_idx, name):
    j, r, cols = grad.shape
    rh = r // 2
    tr = 128
    nb = rh // tr

    def body(c_ref, g_ref, o_ref_in, out_ref):
        out_ref[...] = (g_ref[...].astype(F32) + o_ref_in[...].astype(F32)).astype(BF16)

    return pl.pallas_call(
        body, name=name,
        grid_spec=pltpu.PrefetchScalarGridSpec(
            num_scalar_prefetch=1, grid=(j, nb),
            in_specs=[pl.BlockSpec((None, tr, cols), lambda jj, i, c_ref: (jj, c_ref[0] * nb + i, 0)),
                      pl.BlockSpec((None, tr, cols), lambda jj, i, c_ref: (jj, i, 0))],
            out_specs=pl.BlockSpec((None, tr, cols), lambda jj, i, c_ref: (jj, i, 0))),
        out_shape=jax.ShapeDtypeStruct((j, rh, cols), BF16),
        compiler_params=_params("parallel", "parallel"),
    )(c_idx, grad, got)


def _scatter_partials(partials):
    n = len(partials)
    hbm = pl.BlockSpec(memory_space=pl.ANY)

    def body(*refs):
        ins, outs = refs[:n], refs[n:2 * n]
        send, recv, local = refs[2 * n:]
        x, y, c = _position()
        chip = 2 * x + y
        cps, lcs = [], []
        for t in range(n):
            lc = pltpu.make_async_copy(ins[t].at[chip], outs[t].at[chip], local.at[t])
            lc.start()
            lcs.append(lc)
            for k in range(1, N_CHIPS):
                px, py = _chip_peer(x, y, k)
                s = 3 * t + k - 1
                cp = pltpu.make_async_remote_copy(
                    src_ref=ins[t].at[2 * px + py], dst_ref=outs[t].at[chip], send_sem=send.at[s],
                    recv_sem=recv.at[s], device_id=(px, py, c), device_id_type=MESH)
                cp.start()
                cps.append(cp)
        for cp in cps:
            cp.wait()
        for lc in lcs:
            lc.wait()

    return pl.pallas_call(
        body, name="reduce_scatter_partials", in_specs=[hbm] * n, out_specs=[hbm] * n,
        out_shape=[jax.ShapeDtypeStruct(p.shape, BF16) for p in partials],
        scratch_shapes=[pltpu.SemaphoreType.DMA((3 * n,)), pltpu.SemaphoreType.DMA((3 * n,)),
                        pltpu.SemaphoreType.DMA((n,))],
    )(*partials)


def _sum_chips(parts, name):
    j, rh, cols = parts.shape
    tr = 128

    def body(p_ref, o_ref):
        acc = p_ref[0].astype(F32)
        for s in range(1, j):
            acc = acc + p_ref[s].astype(F32)
        o_ref[...] = acc

    return pl.pallas_call(
        body, name=name, grid=(rh // tr,),
        in_specs=[pl.BlockSpec((j, tr, cols), lambda i: (0, i, 0))],
        out_specs=pl.BlockSpec((tr, cols), lambda i: (i, 0)),
        out_shape=jax.ShapeDtypeStruct((rh, cols), F32),
        compiler_params=_params("parallel"),
    )(parts)


def _share_totals(halves):
    n = len(halves)
    hbm = pl.BlockSpec(memory_space=pl.ANY)

    def body(*refs):
        ins, outs = refs[:n], refs[n:2 * n]
        send, recv, local = refs[2 * n:]
        x, y, c = _position()
        cps, lcs = [], []
        for t in range(n):
            rh = ins[t].shape[0]
            mine = outs[t].at[pl.ds(c * rh, rh)]
            lc = pltpu.make_async_copy(ins[t], mine, local.at[t])
            lc.start()
            lcs.append(lc)
            cp = pltpu.make_async_remote_copy(
                src_ref=ins[t], dst_ref=mine, send_sem=send.at[t], recv_sem=recv.at[t],
                device_id=(x, y, 1 - c), device_id_type=MESH)
            cp.start()
            cps.append(cp)
        for cp in cps:
            cp.wait()
        for lc in lcs:
            lc.wait()

    return pl.pallas_call(
        body, name="reduce_share_totals", in_specs=[hbm] * n, out_specs=[hbm] * n,
        out_shape=[jax.ShapeDtypeStruct((2 * h.shape[0], h.shape[1]), F32) for h in halves],
        scratch_shapes=[pltpu.SemaphoreType.DMA((n,)), pltpu.SemaphoreType.DMA((n,)),
                        pltpu.SemaphoreType.DMA((n,))],
    )(*halves)


SMALL_ROWS = 56


def _small_copies(refs, send, recv, receiving):
    x, y, c = _position()
    me = 4 * x + 2 * y + c
    cps = []
    for k in range(1, N_DEV):
        px, py, pc = _xor_peer(x, y, c, k)
        cps.append(pltpu.make_async_remote_copy(
            src_ref=refs[0], dst_ref=refs[1].at[4 * px + 2 * py + pc if receiving else me],
            send_sem=send.at[k - 1], recv_sem=recv.at[k - 1], device_id=(px, py, pc), device_id_type=MESH))
    return cps


def _small_gather_start(packed, after):
    land = lax.empty((N_DEV,) + packed.shape, F32)

    def issue(refs, send, recv):
        for cp in _small_copies(refs, send, recv, False):
            cp.start()

    return _split_start("small_gather_start", [packed, land], N_DEV - 1, after, issue)


def _small_gather_wait(send, recv, arrays, after):
    def await_all(refs, send_ref, recv_ref):
        for cp in _small_copies(refs, send_ref, recv_ref, True):
            cp.wait_send()
            cp.wait_recv()

    return _split_wait("small_gather_wait", arrays, send, recv, after, await_all)


def _reduce_small(packed, land, silu_c):
    ns = 3 * D_MODEL // N_CHIPS

    def body(p_ref, land_ref, sc_ref, tot_ref, gw_ref, loss_ref, qk_ref, allp):
        x, y, c = _position()
        me = 4 * x + 2 * y + c
        chip = 2 * x + y
        for i in range(N_DEV):
            allp[i] = jnp.where(me == i, p_ref[...], land_ref[i])
        tot = allp[0]
        for i in range(1, N_DEV):
            tot = tot + allp[i]
        tot_ref[...] = tot
        loss_ref[...] = jnp.sum(tot[11:12, :], axis=1, keepdims=True) * (0.5 / D_MODEL)
        fold = tot[5:11, 0:HEAD_DIM]
        for h in range(1, N_HEADS):
            fold = fold + tot[5:11, h * HEAD_DIM:(h + 1) * HEAD_DIM]
        qk_ref[...] = jnp.concatenate([fold, jnp.zeros((2, HEAD_DIM), F32)], axis=0)
        sct = sc_ref[...].T
        rc = 64
        for l in range(2):
            dms = [allp[i, pl.ds(12 + 4 * l + chip, 1), :][:, :ns] for i in range(N_DEV)]
            for r0 in range(0, D_MODEL, rc):
                acc = sct[r0:r0 + rc, 0:1] * dms[0]
                for i in range(1, N_DEV):
                    acc = acc + sct[r0:r0 + rc, i:i + 1] * dms[i]
                gw_ref[l, r0:r0 + rc, :] = acc

    vm = pl.BlockSpec(memory_space=pltpu.VMEM)
    return pl.pallas_call(
        body, name="reduce_small", in_specs=[vm, vm, vm], out_specs=[vm] * 4,
        out_shape=[jax.ShapeDtypeStruct((SMALL_ROWS, D_MODEL), F32), jax.ShapeDtypeStruct((2, D_MODEL, ns), F32),
                   jax.ShapeDtypeStruct((1, 1), F32), jax.ShapeDtypeStruct((8, HEAD_DIM), F32)],
        scratch_shapes=[pltpu.VMEM((N_DEV, SMALL_ROWS, D_MODEL), F32)],
        compiler_params=pltpu.CompilerParams(vmem_limit_bytes=VMEM_LIMIT_BYTES),
    )(packed, land, silu_c)


def _reduce_big(grads, c_idx):
    names = list(grads)
    got = _exchange_halves([grads[k] for k in names])
    partials = [_add_halves(grads[k], got[i], c_idx, f"reduce_add_{k}") for i, k in enumerate(names)]
    parts = _scatter_partials(partials)
    halves = [_sum_chips(parts[i], f"reduce_sum_{k}") for i, k in enumerate(names)]
    totals = _share_totals(halves)
    return dict(zip(names, totals))


def kernel(x, c, norm_g, ada_w, ada_b, a_w_in, a_conv_w, a_conv_b, a_ln_g, a_ln_b, a_w_out, b_w_in, b_q_norm, b_k_norm, b_w_out, loss_target, m_norm_g, m_ada_w, m_ada_b, m_a_w_in, m_a_conv_w, m_a_conv_b, m_a_ln_g, m_a_ln_b, m_a_w_out, m_b_w_in, m_b_q_norm, m_b_k_norm, m_b_w_out, v_norm_g, v_ada_w, v_ada_b, v_a_w_in, v_a_conv_w, v_a_conv_b, v_a_ln_g, v_a_ln_b, v_a_w_out, v_b_w_in, v_b_q_norm, v_b_k_norm, v_b_w_out):
    chip = 2 * lax.axis_index("x") + lax.axis_index("y")
    core = lax.axis_index("c")
    chip_idx = chip.astype(jnp.int32).reshape(1)
    dev_idx = jnp.stack([2 * chip + core, chip, core]).astype(jnp.int32)

    mods, silu_c, conv_w_full = _ada_forward(c, ada_w, ada_b, a_conv_w[0])
    lands_a = [_cast_into_slot(a_w_in[0], chip_idx, "cast_a_w_in"), _cast_into_slot(a_w_out[0], chip_idx, "cast_a_w_out")]
    send_a, recv_a, lands_a, token_a = _gather_start(lands_a, mods, "gather_start_a")
    lands_b = [_cast_into_slot(b_w_in[0], chip_idx, "cast_b_w_in"), _cast_into_slot(b_w_out[0], chip_idx, "cast_b_w_out")]
    send_b, recv_b, lands_b, token_b = _gather_start(lands_b, token_a, "gather_start_b")
    mods = mods + token_b[0:2, 0:1]

    def weights_a(after):
        send, recv, lands, _ = _gather_forward(send_a, recv_a, lands_a, after, "gather_forward_a")
        w_in, w_out = _gather_wait(send, recv, lands, after, "gather_wait_a")
        return w_in, w_out.reshape(D_MODEL, D_MODEL)

    forwarded_b = []

    def weights_b(after):
        send, recv, lands, _ = forwarded_b
        w_in, w_out = _gather_wait(send, recv, lands, after, "gather_wait_b")
        return w_in, w_out.reshape(D_MODEL, D_MODEL)

    def forward_weights_b(after):
        forwarded_b.extend(_gather_forward(send_b, recv_b, lands_b, after, "gather_forward_b"))
        return forwarded_b[3]

    stage1, stage2 = {}, {}

    def send_grads(tag, dw_in, dw_out):
        grads = [dw_in, dw_out.reshape(N_CHIPS, D_MODEL // N_CHIPS, D_MODEL)]
        send, recv, arrays, token = _reduce_sibling_start(grads, dw_out, f"reduce_d2d_start_{tag}")
        stage1[tag] = (send, recv, arrays)
        return token

    def forward_grads(tag, after):
        send, recv, arrays = stage1[tag]
        grads, got = _reduce_sibling_wait(send, recv, arrays, after, f"reduce_d2d_wait_{tag}")
        partials = [_add_sibling_half(grads[i], got[i], dev_idx, f"reduce_add_{tag}_{i}") for i in range(2)]
        send, recv, arrays, token = _reduce_chips_start(partials, partials[1], f"reduce_ici_start_{tag}")
        stage2[tag] = (send, recv, arrays)
        return token

    stage3 = {}

    def sum_grads(tag, after):
        send, recv, arrays = stage2[tag]
        partials, lands = _reduce_chips_wait(send, recv, arrays, after, f"reduce_ici_wait_{tag}")
        totals = [_sum_partials(lands[i], partials[i], dev_idx, f"reduce_sum_{tag}_{i}") for i in range(2)]
        send, recv, totals, token = _share_halves_start(totals, totals[1], f"reduce_share_start_{tag}")
        stage3[tag] = (send, recv, totals)
        return token

    def finish_grads(tag, after):
        send, recv, totals = stage3[tag]
        return _share_halves_wait(send, recv, totals, after, f"reduce_share_wait_{tag}")

    grad_x, small = _local_step(
        x[0], loss_target[0], mods.reshape(2, 3, D_MODEL), norm_g, conv_w_full, a_conv_b, a_ln_g[0:1],
        a_ln_b[0:1], b_q_norm[0], b_k_norm[0], weights_a, weights_b, forward_weights_b,
        functools.partial(send_grads, "b"), functools.partial(forward_grads, "b"), functools.partial(send_grads, "a"))

    ns = 3 * D_MODEL // N_CHIPS
    pad_mod = lambda dm: jnp.pad(dm.reshape(N_CHIPS, ns), ((0, 0), (0, D_MODEL - ns)))
    packed = jnp.concatenate([
        small["dnorm_g"], small["dconv_b"], small["dln_g"], small["dln_b"], small["dq_norm"], small["dk_norm"],
        small["loss_cols"], pad_mod(small["dmod0"]), pad_mod(small["dmod1"]), small["dconv_w"],
        jnp.zeros((SMALL_ROWS - 20 - CONV_WIDTH, D_MODEL), F32)], axis=0)
    send_s, recv_s, small_arrays, token_s = _small_gather_start(packed, packed)

    given = dict(norm_g=(norm_g, m_norm_g, v_norm_g), ada_w=(ada_w, m_ada_w, v_ada_w), ada_b=(ada_b, m_ada_b, v_ada_b),
                 a_w_in=(a_w_in, m_a_w_in, v_a_w_in), a_conv_w=(a_conv_w, m_a_conv_w, v_a_conv_w),
                 a_conv_b=(a_conv_b, m_a_conv_b, v_a_conv_b), a_ln_g=(a_ln_g, m_a_ln_g, v_a_ln_g),
                 a_ln_b=(a_ln_b, m_a_ln_b, v_a_ln_b), a_w_out=(a_w_out, m_a_w_out, v_a_w_out),
                 b_w_in=(b_w_in, m_b_w_in, v_b_w_in), b_q_norm=(b_q_norm, m_b_q_norm, v_b_q_norm),
                 b_k_norm=(b_k_norm, m_b_k_norm, v_b_k_norm), b_w_out=(b_w_out, m_b_w_out, v_b_w_out))
    order = ["norm_g", "ada_w", "ada_b", "a_w_in", "a_conv_w", "a_conv_b", "a_ln_g", "a_ln_b", "a_w_out", "b_w_in",
             "b_q_norm", "b_k_norm", "b_w_out"]
    outs = {}

    def update(k, g2, after=None):
        w, m, v = given[k]
        shape2 = g2.shape
        d2, m2, v2 = _adamw(w.reshape(shape2), g2, m.reshape(shape2), v.reshape(shape2), f"adamw_{k}", after)
        outs[k] = tuple(a.reshape(w.shape) for a in (g2, d2, m2, v2))

    token = forward_grads("a", token_s)
    token = sum_grads("b", token)
    packed, land = _small_gather_wait(send_s, recv_s, small_arrays, token)
    tot, g_ada_w, loss, qk = _reduce_small(packed, land, silu_c)
    g_b_in, g_b_out = finish_grads("b", tot)
    update("b_w_in", g_b_in)
    update("b_w_out", g_b_out)
    token = sum_grads("a", outs["b_w_in"][1])
    cw = D_MODEL // N_CHIPS
    g_small = dict(
        norm_g=tot[0:2], a_conv_b=tot[2:3], a_ln_g=tot[3:4], a_ln_b=tot[4:5],
        b_q_norm=qk[0:3], b_k_norm=qk[3:6],
        ada_b=jnp.stack([tot[12:16, :ns].reshape(3 * D_MODEL), tot[16:20, :ns].reshape(3 * D_MODEL)]),
        a_conv_w=lax.dynamic_slice(tot[20:20 + CONV_WIDTH], (0, chip * cw), (CONV_WIDTH, cw)),
    )
    update("ada_w", g_ada_w.reshape(2 * D_MODEL, ns), after=token)
    for k, g2 in g_small.items():
        update(k, g2, after=token)
    g_a_in, g_a_out = finish_grads("a", outs["ada_w"][1])
    update("a_w_in", g_a_in)
    update("a_w_out", g_a_out)
    return (loss.reshape(()), grad_x[None], *[outs[k][0] for k in order], *[outs[k][1] for k in order],
            *[outs[k][2] for k in order], *[outs[k][3] for k in order])
```

```python
import functools

import jax
import jax.numpy as jnp
from jax import lax
from jax.experimental import pallas as pl
from jax.experimental.pallas import tpu as pltpu

F32 = jnp.float32
BF16 = jnp.bfloat16

SEQ = 2048
D_MODEL = 1024
CONV_WIDTH = 31
HEAD_DIM = 64
N_HEADS = 16
DILATIONS = (1, 4, 16)
ATTN_BLOCK = 128
NORM_EPS = 1e-6
NEG_INF = -1e30
N_DEV = 8
N_CHIPS = 4

ADAM_LR = 0.001
ADAM_B1 = 0.9
ADAM_B2 = 0.999
ADAM_EPS = 1e-08
ADAM_WD = 0.01
ADAM_STEP = 10

VMEM_LIMIT_BYTES = 52 * 1024 * 1024
HALO = 32
LANES = 128
MESH = pl.DeviceIdType.MESH


def _params(*sem):
    return pltpu.CompilerParams(dimension_semantics=sem or None, vmem_limit_bytes=VMEM_LIMIT_BYTES)


def _sigmoid(v):
    return 1.0 / (1.0 + jnp.exp(-v))


def _row_spec(tm, cols, col_block=0):
    return pl.BlockSpec((tm, cols), lambda i: (i, col_block))


def _vec_spec(rows, cols):
    return pl.BlockSpec((rows, cols), lambda i: (0, 0))


def _normmod(xv, g, scale, shift):
    r = lax.rsqrt(jnp.mean(xv * xv, axis=-1, keepdims=True) + NORM_EPS)
    return xv * r * g * (1.0 + scale) + shift


def _normmod_fwd(x, g, scale, shift, name):
    tm = 256

    def body(x_ref, g_ref, sc_ref, sh_ref, h_ref, ht_ref):
        h = _normmod(x_ref[...], g_ref[...], sc_ref[...], sh_ref[...])
        h_ref[...] = h.astype(BF16)
        ht_ref[...] = h.T.astype(BF16)

    return pl.pallas_call(
        body, name=name, grid=(SEQ // tm,),
        in_specs=[_row_spec(tm, D_MODEL)] + [_vec_spec(1, D_MODEL)] * 3,
        out_specs=[_row_spec(tm, D_MODEL), pl.BlockSpec((D_MODEL, tm), lambda i: (0, i))],
        out_shape=[jax.ShapeDtypeStruct((SEQ, D_MODEL), BF16), jax.ShapeDtypeStruct((D_MODEL, SEQ), BF16)],
        compiler_params=_params("parallel"),
    )(x, g, scale, shift)


def _normmod_bwd(x, g, scale, dh_parts, dres, name, part_dilations=None, gated=None):
    tm = 256
    n_parts = len(dh_parts)
    dils = part_dilations or (1,) * n_parts
    dh_parts = [p if d == 1 else p.reshape(d, SEQ // d, D_MODEL) for p, d in zip(dh_parts, dils)]
    n_gated = 0 if gated is None else 2

    def body(x_ref, g_ref, sc_ref, dres_ref, *rest):
        part_refs = rest[:n_parts]
        gated_refs = rest[n_parts:n_parts + n_gated]
        out_refs = rest[n_parts + n_gated:]
        dx_ref, sums_ref, nat = out_refs[0], out_refs[1], out_refs[-1]
        xv = x_ref[...]
        r = lax.rsqrt(jnp.mean(xv * xv, axis=-1, keepdims=True) + NORM_EPS)
        xn = xv * r
        dh = _load_natural(part_refs[0], nat, dils[0])
        for p, d in zip(part_refs[1:], dils[1:]):
            dh = dh + _load_natural(p, nat, d)
        gv = g_ref[...]
        one_sc = 1.0 + sc_ref[...]
        dxn = dh * (gv * one_sc)
        dx = dres_ref[...] + r * (dxn - xn * jnp.mean(dxn * xn, axis=-1, keepdims=True))
        dx_ref[...] = dx
        dhx = dh * xn
        rows = [jnp.sum(dhx, axis=0, keepdims=True) * one_sc,
                jnp.sum(dhx, axis=0, keepdims=True) * gv,
                jnp.sum(dh, axis=0, keepdims=True)]
        if gated is not None:
            gate_ref, y_ref = gated_refs
            out_refs[2][...] = (dx * gate_ref[...]).astype(BF16)
            rows.append(jnp.sum(dx * y_ref[...], axis=0, keepdims=True))
        sums = jnp.concatenate(rows + [jnp.zeros((8 - len(rows), D_MODEL), F32)], axis=0)

        @pl.when(pl.program_id(0) == 0)
        def _():
            sums_ref[...] = jnp.zeros_like(sums_ref)

        sums_ref[...] += sums

    gated_specs = [] if gated is None else [_vec_spec(1, D_MODEL), _row_spec(tm, D_MODEL)]
    dy_spec = [] if gated is None else [_row_spec(tm, D_MODEL)]
    dy_shape = [] if gated is None else [jax.ShapeDtypeStruct((SEQ, D_MODEL), BF16)]
    return pl.pallas_call(
        body, name=name, grid=(SEQ // tm,),
        in_specs=[_row_spec(tm, D_MODEL), _vec_spec(1, D_MODEL), _vec_spec(1, D_MODEL), _row_spec(tm, D_MODEL)]
        + [_class_spec(tm, d) for d in dils] + gated_specs,
        out_specs=[_row_spec(tm, D_MODEL), _vec_spec(8, D_MODEL)] + dy_spec,
        out_shape=[jax.ShapeDtypeStruct((SEQ, D_MODEL), F32), jax.ShapeDtypeStruct((8, D_MODEL), F32)] + dy_shape,
        scratch_shapes=[_natural_scratch(tm)],
        compiler_params=_params("arbitrary"),
    )(x, g, scale, dres, *dh_parts, *(gated or ()))


def _mm(lhs, rhs, *, tn, tile0, n_tiles, out_dtype, name, out3d=None, prev=None, transpose_lhs=False):
    mo, kc = lhs.shape[::-1] if transpose_lhs else lhs.shape
    cm = min(mo, 1024)
    tc = 256

    def body(l_ref, r_ref, *rest):
        if transpose_lhs:
            o_ref, lt_ref = rest[-2], rest[-1]

            @pl.when(pl.program_id(0) == 0)
            def _():
                for c in range(kc // tc):
                    lt_ref[:, c * tc:(c + 1) * tc] = l_ref[c * tc:(c + 1) * tc, :].astype(F32).T.astype(l_ref.dtype)
        else:
            o_ref, lt_ref = rest[-1], l_ref
        for m in range(mo // cm):
            rows = pl.ds(m * cm, cm)
            o_ref[rows, :] = jnp.dot(lt_ref[rows, :], r_ref[...], preferred_element_type=F32).astype(out_dtype)

    if rhs.ndim == 3:
        tps_r = rhs.shape[2] // tn
        r_spec = pl.BlockSpec((None, kc, tn), lambda t: ((tile0 + t) // tps_r, 0, (tile0 + t) % tps_r))
    else:
        r_spec = pl.BlockSpec((kc, tn), lambda t: (0, t))
    in_specs = [pl.BlockSpec(lhs.shape, lambda t: (0, 0)), r_spec]
    args = [lhs, rhs]
    aliases = {}
    if out3d is None:
        o_spec = pl.BlockSpec((mo, tn), lambda t: (0, t))
        o_shape = jax.ShapeDtypeStruct((mo, n_tiles * tn), out_dtype)
    else:
        j_out, ns_out = out3d
        tps_o = ns_out // tn
        o_spec = pl.BlockSpec((None, mo, tn), lambda t: ((tile0 + t) // tps_o, 0, (tile0 + t) % tps_o))
        o_shape = jax.ShapeDtypeStruct((j_out, mo, ns_out), out_dtype)
        if prev is not None:
            in_specs.append(pl.BlockSpec(memory_space=pl.ANY))
            args.append(prev)
            aliases = {2: 0}
    return pl.pallas_call(
        body, name=name, grid=(n_tiles,), in_specs=in_specs, out_specs=o_spec, out_shape=o_shape,
        input_output_aliases=aliases,
        scratch_shapes=[pltpu.VMEM((mo, kc), lhs.dtype)] if transpose_lhs else [],
        compiler_params=_params("arbitrary" if transpose_lhs else "parallel"),
    )(*args)


def _mm_nt(dy, w3, *, tn, tile0, n_tiles, name, after=None):
    m_rows = dy.shape[0]
    _, kc, ns = w3.shape
    tps = ns // tn
    cm = 512
    extra = [] if after is None else [after]

    def body(dy_ref, w_ref, *rest):
        o_ref = rest[-1]

        @pl.when(pl.program_id(0) == 0)
        def _():
            o_ref[...] = jnp.zeros_like(o_ref)

        for m in range(m_rows // cm):
            rows = pl.ds(m * cm, cm)
            o_ref[rows, :] += lax.dot_general(dy_ref[rows, :], w_ref[...], (((1,), (1,)), ((), ())),
                                              preferred_element_type=F32)

    return pl.pallas_call(
        body, name=name, grid=(n_tiles,),
        in_specs=[pl.BlockSpec((m_rows, tn), lambda t: (0, t)),
                  pl.BlockSpec((None, kc, tn), lambda t: ((tile0 + t) // tps, 0, (tile0 + t) % tps))]
        + [pl.BlockSpec(memory_space=pl.ANY)] * len(extra),
        out_specs=pl.BlockSpec((m_rows, kc), lambda t: (0, 0)),
        out_shape=jax.ShapeDtypeStruct((m_rows, kc), F32),
        compiler_params=_params("arbitrary"),
    )(dy, w3, *extra)


CONV_CHUNK = 16


def _shift_copies(buf, shifted):
    rows = shifted.shape[1]
    for s in range(1, 8):
        shifted[s - 1] = buf[pl.ds(s, rows), :]


def _shifted_rows(buf, shifted, offset, r0):
    s = offset % 8
    if s == 0:
        return buf[pl.ds(r0 + offset, CONV_CHUNK), :]
    return shifted[s - 1, pl.ds(r0 + (offset - s), CONV_CHUNK), :]


def _spread_taps(w_ref, taps):
    for k in range(CONV_WIDTH):
        taps[k] = jnp.broadcast_to(w_ref[k:k + 1, :], (8, D_MODEL))


def _times_tap(taps, k, rows):
    return (rows.reshape(CONV_CHUNK // 8, 8, D_MODEL) * taps[k][None]).reshape(CONV_CHUNK, D_MODEL)


def _conv_fwd(proj, conv_w, conv_b, ln_g, ln_b, name):
    tm = 256
    hb = tm // HALO

    def body(vg_ref, halo_ref, z_ref, w_ref, b_ref, g_ref, be_ref, u5_ref, u5t_ref, u2_ref, buf, shifted, taps):
        i = pl.program_id(0)
        u1 = vg_ref[:, :D_MODEL] * _sigmoid(vg_ref[:, D_MODEL:])
        u1h = halo_ref[:, :D_MODEL] * _sigmoid(halo_ref[:, D_MODEL:])
        buf[pl.ds(0, HALO), :] = jnp.where(i > 0, u1h, 0.0)
        buf[pl.ds(HALO, tm), :] = u1
        _shift_copies(buf, shifted)
        _spread_taps(w_ref, taps)

        def chunk(ci, carry):
            r0 = pl.multiple_of(ci * CONV_CHUNK, CONV_CHUNK)
            acc = jnp.broadcast_to(b_ref[...], (CONV_CHUNK, D_MODEL))
            for k in range(CONV_WIDTH):
                acc = acc + _times_tap(taps, k, _shifted_rows(buf, shifted, HALO - (CONV_WIDTH - 1) + k, r0))
            u2_ref[pl.ds(r0, CONV_CHUNK), :] = acc
            return carry

        lax.fori_loop(0, tm // CONV_CHUNK, chunk, 0)
        acc = u2_ref[...]
        mu = jnp.mean(acc, axis=-1, keepdims=True)
        xc = acc - mu
        rstd = lax.rsqrt(jnp.mean(xc * xc, axis=-1, keepdims=True) + NORM_EPS)
        u3 = xc * rstd * g_ref[...] + be_ref[...]
        zv = z_ref[...]
        u5 = u3 * _sigmoid(u3) * (zv * _sigmoid(zv))
        u5_ref[...] = u5.astype(BF16)
        u5t_ref[...] = u5.T.astype(BF16)

    return pl.pallas_call(
        body, name=name, grid=(SEQ // tm,),
        in_specs=[pl.BlockSpec((tm, 2 * D_MODEL), lambda i: (i, 0)),
                  pl.BlockSpec((HALO, 2 * D_MODEL), lambda i: (jnp.maximum(i * hb - 1, 0), 0)),
                  _row_spec(tm, D_MODEL, 2),
                  _vec_spec(CONV_WIDTH, D_MODEL)] + [_vec_spec(1, D_MODEL)] * 3,
        out_specs=[_row_spec(tm, D_MODEL), pl.BlockSpec((D_MODEL, tm), lambda i: (0, i)), _row_spec(tm, D_MODEL)],
        out_shape=[jax.ShapeDtypeStruct((SEQ, D_MODEL), BF16), jax.ShapeDtypeStruct((D_MODEL, SEQ), BF16),
                   jax.ShapeDtypeStruct((SEQ, D_MODEL), F32)],
        scratch_shapes=[pltpu.VMEM((HALO + tm, D_MODEL), F32), pltpu.VMEM((7, HALO + tm - 8, D_MODEL), F32),
                        pltpu.VMEM((CONV_WIDTH, 8, D_MODEL), F32)],
        compiler_params=_params("parallel"),
    )(proj, proj, proj, conv_w, conv_b, ln_g, ln_b)


def _conv_bwd_pointwise(dy, w_out, proj, u2, ln_g, ln_b, name):
    tm = 256

    def body(dy_ref, w_ref, z_ref, u2_ref, g_ref, be_ref, du2_ref, dz_ref, sums_ref):
        u2v = u2_ref[...]
        mu = jnp.mean(u2v, axis=-1, keepdims=True)
        xc = u2v - mu
        rstd = lax.rsqrt(jnp.mean(xc * xc, axis=-1, keepdims=True) + NORM_EPS)
        xhat = xc * rstd
        u3 = xhat * g_ref[...] + be_ref[...]
        s3 = _sigmoid(u3)
        u4 = u3 * s3
        zv = z_ref[...]
        sz = _sigmoid(zv)
        du5v = lax.dot_general(dy_ref[...], w_ref[...], NT_DIMS, preferred_element_type=F32)
        dz_ref[...] = du5v * u4 * (sz * (1.0 + zv * (1.0 - sz)))
        du3 = du5v * (zv * sz) * (s3 * (1.0 + u3 * (1.0 - s3)))
        dxhat = du3 * g_ref[...]
        du2 = rstd * (dxhat - jnp.mean(dxhat, axis=-1, keepdims=True)
                      - xhat * jnp.mean(dxhat * xhat, axis=-1, keepdims=True))
        du2_ref[...] = du2
        sums = jnp.concatenate([
            jnp.sum(du3 * xhat, axis=0, keepdims=True),
            jnp.sum(du3, axis=0, keepdims=True),
            jnp.sum(du2, axis=0, keepdims=True),
            jnp.zeros((5, D_MODEL), F32)], axis=0)

        @pl.when(pl.program_id(0) == 0)
        def _():
            sums_ref[...] = jnp.zeros_like(sums_ref)

        sums_ref[...] += sums

    return pl.pallas_call(
        body, name=name, grid=(SEQ // tm,),
        in_specs=[_row_spec(tm, D_MODEL), _vec_spec(D_MODEL, D_MODEL), _row_spec(tm, D_MODEL, 2),
                  _row_spec(tm, D_MODEL), _vec_spec(1, D_MODEL), _vec_spec(1, D_MODEL)],
        out_specs=[_row_spec(tm, D_MODEL), _row_spec(tm, D_MODEL), _vec_spec(8, D_MODEL)],
        out_shape=[jax.ShapeDtypeStruct((SEQ, D_MODEL), F32), jax.ShapeDtypeStruct((SEQ, D_MODEL), F32),
                   jax.ShapeDtypeStruct((8, D_MODEL), F32)],
        compiler_params=_params("arbitrary"),
    )(dy, w_out, proj, u2, ln_g, ln_b)


def _conv_bwd_taps(du2, dz, proj, conv_w, name):
    tm = 256
    hb = tm // HALO
    n_blocks = SEQ // tm

    def body(du2_ref, dnext_ref, dz_ref, vg_ref, w_ref, dproj_ref, dw_ref, dbuf, dshift, sgbuf, ubuf, dwacc, taps):
        i = pl.program_id(0)
        _spread_taps(w_ref, taps)
        sg = _sigmoid(vg_ref[:, D_MODEL:])
        sgbuf[...] = sg
        ubuf[...] = vg_ref[:, :D_MODEL] * sg
        dbuf[pl.ds(0, tm), :] = du2_ref[...]
        dbuf[pl.ds(tm, HALO), :] = jnp.where(i < n_blocks - 1, dnext_ref[...], 0.0)
        _shift_copies(dbuf, dshift)

        @pl.when(i == 0)
        def _():
            dwacc[...] = jnp.zeros_like(dwacc)

        def chunk(ci, carry):
            r0 = pl.multiple_of(ci * CONV_CHUNK, CONV_CHUNK)
            rows = pl.ds(r0, CONV_CHUNK)
            u1c = ubuf[rows, :]
            du1 = jnp.zeros((CONV_CHUNK, D_MODEL), F32)
            for k in range(CONV_WIDTH):
                ahead = _shifted_rows(dbuf, dshift, CONV_WIDTH - 1 - k, r0)
                du1 = du1 + _times_tap(taps, k, ahead)
                prod = u1c * ahead
                dwacc[k] += prod[0:8] + prod[8:16]
            sgc = sgbuf[rows, :]
            dval = du1 * sgc
            dproj_ref[rows, 0:D_MODEL] = dval.astype(BF16)
            dproj_ref[rows, D_MODEL:2 * D_MODEL] = (dval * vg_ref[rows, 0:D_MODEL] * (1.0 - sgc)).astype(BF16)
            return carry

        lax.fori_loop(0, tm // CONV_CHUNK, chunk, 0)
        dproj_ref[:, 2 * D_MODEL:] = dz_ref[...].astype(BF16)

        @pl.when(i == n_blocks - 1)
        def _():
            for k in range(CONV_WIDTH):
                dw_ref[k:k + 1, :] = jnp.sum(dwacc[k], axis=0, keepdims=True)
            dw_ref[CONV_WIDTH:, :] = jnp.zeros((32 - CONV_WIDTH, D_MODEL), F32)

    return pl.pallas_call(
        body, name=name, grid=(n_blocks,),
        in_specs=[_row_spec(tm, D_MODEL),
                  pl.BlockSpec((HALO, D_MODEL), lambda i: (jnp.minimum((i + 1) * hb, SEQ // HALO - 1), 0)),
                  _row_spec(tm, D_MODEL),
                  pl.BlockSpec((tm, 2 * D_MODEL), lambda i: (i, 0)),
                  _vec_spec(CONV_WIDTH, D_MODEL)],
        out_specs=[_row_spec(tm, 3 * D_MODEL), _vec_spec(32, D_MODEL)],
        out_shape=[jax.ShapeDtypeStruct((SEQ, 3 * D_MODEL), BF16), jax.ShapeDtypeStruct((32, D_MODEL), F32)],
        scratch_shapes=[pltpu.VMEM((tm + HALO, D_MODEL), F32), pltpu.VMEM((7, HALO + tm - 8, D_MODEL), F32),
                        pltpu.VMEM((tm, D_MODEL), F32), pltpu.VMEM((tm, D_MODEL), F32),
                        pltpu.VMEM((CONV_WIDTH, 8, D_MODEL), F32), pltpu.VMEM((CONV_WIDTH, 8, D_MODEL), F32)],
        compiler_params=_params("arbitrary"),
    )(du2, du2, dz, proj, conv_w)


def _out_a(u5, w_out, x, gate, g1, scale1, shift1, name):
    tm = 256
    n_d = len(DILATIONS)

    def body(u_ref, w_ref, x_ref, gate_ref, g_ref, sc_ref, sh_ref, x1_ref, y_ref, ht_ref, *rest):
        h_refs, nat = rest[:n_d], rest[-1]
        y = jnp.dot(u_ref[...], w_ref[...], preferred_element_type=F32)
        x1 = x_ref[...] + gate_ref[...] * y
        y_ref[...] = y
        x1_ref[...] = x1
        h = _normmod(x1, g_ref[...], sc_ref[...], sh_ref[...])
        ht_ref[...] = h.T.astype(BF16)
        for h_ref, d in zip(h_refs, DILATIONS):
            _store_classes(h_ref, h, nat, d)

    res = pl.pallas_call(
        body, name=name, grid=(SEQ // tm,),
        in_specs=[_row_spec(tm, D_MODEL), _vec_spec(D_MODEL, D_MODEL), _row_spec(tm, D_MODEL)]
        + [_vec_spec(1, D_MODEL)] * 4,
        out_specs=[_row_spec(tm, D_MODEL), _row_spec(tm, D_MODEL), pl.BlockSpec((D_MODEL, tm), lambda i: (0, i))]
        + [_class_spec(tm, d) for d in DILATIONS],
        out_shape=[jax.ShapeDtypeStruct((SEQ, D_MODEL), F32), jax.ShapeDtypeStruct((SEQ, D_MODEL), F32),
                   jax.ShapeDtypeStruct((D_MODEL, SEQ), BF16)] + [_class_shape(d, BF16) for d in DILATIONS],
        scratch_shapes=[_natural_scratch(tm)],
        compiler_params=_params("parallel"),
    )(u5, w_out, x, gate, g1, scale1, shift1)
    return res[0], res[1], res[2], [a.reshape(SEQ, D_MODEL) for a in res[3:]]


def _out_b_loss(u, w_out, x1, gate, target, name):
    tm = 256

    def body(u_ref, w_ref, x_ref, gate_ref, t_ref, e_ref, dy_ref, sums_ref):
        y = jnp.dot(u_ref[...], w_ref[...], preferred_element_type=F32)
        diff = x_ref[...] + gate_ref[...] * y - t_ref[...]
        e = diff * (1.0 / D_MODEL)
        e_ref[...] = e
        dy_ref[...] = (e * gate_ref[...]).astype(BF16)
        sums = jnp.concatenate([
            jnp.sum(e * y, axis=0, keepdims=True),
            jnp.sum(diff * diff, axis=0, keepdims=True),
            jnp.zeros((6, D_MODEL), F32)], axis=0)

        @pl.when(pl.program_id(0) == 0)
        def _():
            sums_ref[...] = jnp.zeros_like(sums_ref)

        sums_ref[...] += sums

    return pl.pallas_call(
        body, name=name, grid=(SEQ // tm,),
        in_specs=[_row_spec(tm, D_MODEL), _vec_spec(D_MODEL, D_MODEL), _row_spec(tm, D_MODEL),
                  _vec_spec(1, D_MODEL), _row_spec(tm, D_MODEL)],
        out_specs=[_row_spec(tm, D_MODEL), _row_spec(tm, D_MODEL), _vec_spec(8, D_MODEL)],
        out_shape=[jax.ShapeDtypeStruct((SEQ, D_MODEL), F32), jax.ShapeDtypeStruct((SEQ, D_MODEL), BF16),
                   jax.ShapeDtypeStruct((8, D_MODEL), F32)],
        compiler_params=_params("arbitrary"),
    )(u, w_out, x1, gate, target)


def _dgate_dy(dx1, y, gate, name):
    tm = 256

    def body(d_ref, y_ref, gate_ref, dy_ref, sums_ref):
        dv = d_ref[...]
        dy_ref[...] = (dv * gate_ref[...]).astype(BF16)
        sums = jnp.concatenate([jnp.sum(dv * y_ref[...], axis=0, keepdims=True), jnp.zeros((7, D_MODEL), F32)], axis=0)

        @pl.when(pl.program_id(0) == 0)
        def _():
            sums_ref[...] = jnp.zeros_like(sums_ref)

        sums_ref[...] += sums

    return pl.pallas_call(
        body, name=name, grid=(SEQ // tm,),
        in_specs=[_row_spec(tm, D_MODEL), _row_spec(tm, D_MODEL), _vec_spec(1, D_MODEL)],
        out_specs=[_row_spec(tm, D_MODEL), _vec_spec(8, D_MODEL)],
        out_shape=[jax.ShapeDtypeStruct((SEQ, D_MODEL), BF16), jax.ShapeDtypeStruct((8, D_MODEL), F32)],
        compiler_params=_params("arbitrary"),
    )(dx1, y, gate)


def _mm_nt_res(dy, w, name):
    tm = 256
    kc, n = w.shape

    def body(dy_ref, w_ref, o_ref):
        o_ref[...] = lax.dot_general(dy_ref[...], w_ref[...], (((1,), (1,)), ((), ())), preferred_element_type=F32)

    return pl.pallas_call(
        body, name=name, grid=(SEQ // tm,),
        in_specs=[_row_spec(tm, n), _vec_spec(kc, n)],
        out_specs=_row_spec(tm, kc),
        out_shape=jax.ShapeDtypeStruct((SEQ, kc), F32),
        compiler_params=_params("parallel"),
    )(dy, w)


def _seg_matrix():
    r = lax.broadcasted_iota(jnp.int32, (256, 256), 0) // HEAD_DIM
    c = lax.broadcasted_iota(jnp.int32, (256, 256), 1) // HEAD_DIM
    return (r == c).astype(BF16)


def _segsum(v, seg):
    hi = v.astype(BF16)
    lo = (v - hi.astype(F32)).astype(BF16)
    outs = []
    for c0 in range(0, D_MODEL, 256):
        outs.append(jnp.dot(hi[:, c0:c0 + 256], seg, preferred_element_type=F32)
                    + jnp.dot(lo[:, c0:c0 + 256], seg, preferred_element_type=F32))
    return jnp.concatenate(outs, axis=1)


def _qk_rstd(v, seg):
    return lax.rsqrt(_segsum(v * v, seg) * (1.0 / HEAD_DIM) + NORM_EPS)


def _qknorm_fwd(proj, qw, kw, seg, name):
    tm = 256

    def body(p_ref, qw_ref, kw_ref, seg_ref, q_ref, k_ref):
        segv = seg_ref[...]
        q = p_ref[:, :D_MODEL].astype(F32)
        k = p_ref[:, D_MODEL:].astype(F32)
        q_ref[...] = (q * _qk_rstd(q, segv) * qw_ref[...]).astype(BF16)
        k_ref[...] = (k * _qk_rstd(k, segv) * kw_ref[...]).astype(BF16)

    return pl.pallas_call(
        body, name=name, grid=(SEQ // tm,),
        in_specs=[_row_spec(tm, 2 * D_MODEL), _vec_spec(1, D_MODEL), _vec_spec(1, D_MODEL), _vec_spec(256, 256)],
        out_specs=[_row_spec(tm, D_MODEL)] * 2,
        out_shape=[jax.ShapeDtypeStruct((SEQ, D_MODEL), BF16)] * 2,
        compiler_params=_params("parallel"),
    )(proj, qw, kw, seg)


def _attn_masks(b, bpc, dilation, slope):
    if bpc == 1:
        qi = lax.broadcasted_iota(jnp.int32, (ATTN_BLOCK, ATTN_BLOCK), 0)
        kj = lax.broadcasted_iota(jnp.int32, (ATTN_BLOCK, ATTN_BLOCK), 1)
        steps = qi - kj
        return (steps * dilation).astype(F32), steps >= 0
    qi = lax.broadcasted_iota(jnp.int32, (ATTN_BLOCK, 2 * ATTN_BLOCK), 0)
    kj = lax.broadcasted_iota(jnp.int32, (ATTN_BLOCK, 2 * ATTN_BLOCK), 1)
    steps = qi + ATTN_BLOCK - kj
    has_prev = (b % bpc) != 0
    valid = (steps >= 0) & (steps <= ATTN_BLOCK) & (has_prev | (kj >= ATTN_BLOCK))
    return (steps * dilation).astype(F32), valid


def _key_tile(prev_ref, cur_ref, cols, bpc):
    if bpc == 1:
        return cur_ref[:, cols]
    return jnp.concatenate([prev_ref[:, cols], cur_ref[:, cols]], axis=0)


ATTN_HEADS_FWD = 16
ATTN_HEADS_BWD = 16
NT_DIMS = (((1,), (1,)), ((), ()))
TN_DIMS = (((0,), (0,)), ((), ()))
BATCH_NT_DIMS = (((2,), (2,)), ((0,), (0,)))
BATCH_NN_DIMS = (((2,), (1,)), ((0,), (0,)))
BATCH_TN_DIMS = (((1,), (1,)), ((0,), (0,)))


def _head_stack(tile_of, heads):
    return jnp.stack([tile_of(slice(h * HEAD_DIM, (h + 1) * HEAD_DIM)) for h in range(heads)], axis=0)


def _attn_specs(heads, segment=0):
    width = heads * HEAD_DIM
    off = segment * (D_MODEL // width)
    last = SEQ // ATTN_BLOCK - 1
    cur = pl.BlockSpec((ATTN_BLOCK, width), lambda hg, b: (jnp.minimum(b, last), hg + off))
    prev = pl.BlockSpec((ATTN_BLOCK, width), lambda hg, b: (jnp.clip(b - 1, 0, last), hg + off))
    return cur, prev


def _attn_fwd(q, k, proj, slopes, dilation, name):
    bpc = SEQ // dilation // ATTN_BLOCK
    heads = ATTN_HEADS_FWD
    assert heads == N_HEADS
    cur, prev = _attn_specs(heads)
    v_cur, v_prev = _attn_specs(heads, segment=2)
    scale = HEAD_DIM ** -0.5

    def body(sl_ref, q_ref, kp_ref, kc_ref, vp_ref, vc_ref, o_ref, lse_ref):
        dist, valid = _attn_masks(pl.program_id(1), bpc, dilation, None)
        q3 = _head_stack(lambda cols: q_ref[:, cols], heads)
        k3 = _head_stack(lambda cols: _key_tile(kp_ref, kc_ref, cols, bpc), heads)
        v3 = _head_stack(lambda cols: _key_tile(vp_ref, vc_ref, cols, bpc), heads)
        s = lax.dot_general(q3, k3, BATCH_NT_DIMS, preferred_element_type=F32)
        s = jnp.where(valid[None], s * scale - dist[None] * sl_ref[...], NEG_INF)
        m = jnp.max(s, axis=-1, keepdims=True)
        p = jnp.exp(s - m)
        l = jnp.sum(p, axis=-1, keepdims=True)
        o3 = lax.dot_general(p.astype(BF16), v3, BATCH_NN_DIMS, preferred_element_type=F32) / l
        lse3 = m + jnp.log(l)
        for h in range(heads):
            o_ref[:, h * HEAD_DIM:(h + 1) * HEAD_DIM] = o3[h]
        lse_ref[...] = jnp.concatenate([lse3[h] for h in range(heads)]
                                       + [jnp.zeros((ATTN_BLOCK, LANES - heads), F32)], axis=1)

    return pl.pallas_call(
        body, name=name, grid=(N_HEADS // heads, SEQ // ATTN_BLOCK),
        in_specs=[pl.BlockSpec((heads, 1, 1), lambda hg, b: (hg, 0, 0)), cur, prev, cur, v_prev, v_cur],
        out_specs=[cur, pl.BlockSpec((ATTN_BLOCK, LANES), lambda hg, b: (b, 0))],
        out_shape=[jax.ShapeDtypeStruct((SEQ, D_MODEL), F32), jax.ShapeDtypeStruct((SEQ, LANES), F32)],
        compiler_params=_params("parallel", "parallel"),
    )(slopes.reshape(N_HEADS, 1, 1), q, k, k, proj, proj)


def _class_spec(tm, dilation, width=D_MODEL):
    if dilation == 1:
        return _row_spec(tm, width)
    return pl.BlockSpec((dilation, tm // dilation, width), lambda i: (0, i, 0))


def _class_shape(dilation, dtype, width=D_MODEL):
    if dilation == 1:
        return jax.ShapeDtypeStruct((SEQ, width), dtype)
    return jax.ShapeDtypeStruct((dilation, SEQ // dilation, width), dtype)


def _load_natural(in_ref, nat_ref, dilation):
    if dilation == 1:
        return in_ref[...].astype(F32)
    n = nat_ref.shape[1] // dilation
    tiles = in_ref.shape[-1] // LANES
    for r in range(dilation):
        for j in range(tiles):
            nat_ref.at[j][pl.ds(r, n, stride=dilation), :] = in_ref[r, :, j * LANES:(j + 1) * LANES].astype(F32)
    if tiles == 1:
        return nat_ref[0]
    return jnp.concatenate([nat_ref[j] for j in range(tiles)], axis=1)


def _store_classes(out_ref, value, nat_ref, dilation):
    if dilation == 1:
        out_ref[...] = value.astype(out_ref.dtype)
        return
    n = nat_ref.shape[1] // dilation
    tiles = value.shape[-1] // LANES
    for j in range(tiles):
        nat_ref[j] = value[:, j * LANES:(j + 1) * LANES]
    for r in range(dilation):
        for j in range(tiles):
            out_ref[r, :, j * LANES:(j + 1) * LANES] = (
                nat_ref.at[j][pl.ds(r, n, stride=dilation), :].astype(out_ref.dtype))


def _natural_scratch(tm):
    return pltpu.VMEM((D_MODEL // LANES, tm, LANES), F32)


def _head_selector():
    lane_head = lax.broadcasted_iota(jnp.int32, (D_MODEL, LANES), 0) // HEAD_DIM
    head = lax.broadcasted_iota(jnp.int32, (D_MODEL, LANES), 1)
    return (lane_head == head).astype(BF16)


def _dot_split(v, m01, dims):
    hi = v.astype(BF16)
    lo = (v - hi.astype(F32)).astype(BF16)
    return (lax.dot_general(hi, m01, dims, preferred_element_type=F32)
            + lax.dot_general(lo, m01, dims, preferred_element_type=F32))


def _merge_fwd(o_parts, lse_parts, z, sel, name):
    tm = 256
    h_spec = pl.BlockSpec((tm, LANES), lambda i: (i, 0))

    def body(o0, o1, o2, l0, l1, l2, z_ref, sel_ref, u_ref, ut_ref, o_ref, lse_ref, nat):
        ls = [_load_natural(l, nat, d) for l, d in zip((l0, l1, l2), DILATIONS)]
        m = jnp.maximum(jnp.maximum(ls[0], ls[1]), ls[2])
        tot = m + jnp.log(jnp.exp(ls[0] - m) + jnp.exp(ls[1] - m) + jnp.exp(ls[2] - m))
        o = jnp.zeros((tm, D_MODEL), F32)
        for o_in, l, d in zip((o0, o1, o2), ls, DILATIONS):
            weight = _dot_split(jnp.exp(l - tot), sel_ref[...], NT_DIMS)
            o = o + weight * _load_natural(o_in, nat, d)
        zv = z_ref[...]
        u = o * (zv * _sigmoid(zv))
        u_ref[...] = u.astype(BF16)
        ut_ref[...] = u.T.astype(BF16)
        o_ref[...] = o
        lse_ref[...] = tot

    return pl.pallas_call(
        body, name=name, grid=(SEQ // tm,),
        in_specs=[_class_spec(tm, d) for d in DILATIONS] + [_class_spec(tm, d, LANES) for d in DILATIONS]
        + [_row_spec(tm, D_MODEL), _vec_spec(D_MODEL, LANES)],
        out_specs=[_row_spec(tm, D_MODEL), pl.BlockSpec((D_MODEL, tm), lambda i: (0, i)),
                   _row_spec(tm, D_MODEL), h_spec],
        out_shape=[jax.ShapeDtypeStruct((SEQ, D_MODEL), BF16), jax.ShapeDtypeStruct((D_MODEL, SEQ), BF16),
                   jax.ShapeDtypeStruct((SEQ, D_MODEL), F32), jax.ShapeDtypeStruct((SEQ, LANES), F32)],
        scratch_shapes=[_natural_scratch(tm)],
        compiler_params=_params("parallel"),
    )(*o_parts, *lse_parts, z, sel)


def _merge_bwd(dy, w_out, o, lse, z, sel, name):
    tm = 256
    n_d = len(DILATIONS)

    def body(dy_ref, w_ref, o_ref, lse_ref, z_ref, sel_ref, dz_ref, *rest):
        do_refs, delta_refs, lse_refs, nat = rest[:n_d], rest[n_d:2 * n_d], rest[2 * n_d:3 * n_d], rest[-1]
        zv = z_ref[...]
        sz = _sigmoid(zv)
        duv = lax.dot_general(dy_ref[...], w_ref[...], NT_DIMS, preferred_element_type=F32)
        ov = o_ref[...]
        do = duv * (zv * sz)
        dz_ref[...] = (duv * ov * (sz * (1.0 + zv * (1.0 - sz)))).astype(BF16)
        delta = _dot_split(do * ov, sel_ref[...], (((1,), (0,)), ((), ())))
        lv = lse_ref[...]
        for i, d in enumerate(DILATIONS):
            _store_classes(do_refs[i], do, nat, d)
            _store_classes(delta_refs[i], delta, nat, d)
            _store_classes(lse_refs[i], lv, nat, d)

    res = pl.pallas_call(
        body, name=name, grid=(SEQ // tm,),
        in_specs=[_row_spec(tm, D_MODEL), _vec_spec(D_MODEL, D_MODEL), _row_spec(tm, D_MODEL), _row_spec(tm, LANES),
                  _row_spec(tm, D_MODEL), _vec_spec(D_MODEL, LANES)],
        out_specs=[_row_spec(tm, D_MODEL)] + [_class_spec(tm, d) for d in DILATIONS]
        + [_class_spec(tm, d, LANES) for d in DILATIONS] * 2,
        out_shape=[jax.ShapeDtypeStruct((SEQ, D_MODEL), BF16)] + [_class_shape(d, BF16) for d in DILATIONS]
        + [_class_shape(d, F32, LANES) for d in DILATIONS] * 2,
        scratch_shapes=[_natural_scratch(tm)],
        compiler_params=_params("parallel"),
    )(dy, w_out, o, lse, z, sel)
    flat = lambda a: a.reshape(SEQ, a.shape[-1])
    return (res[0], [flat(a) for a in res[1:1 + n_d]], [flat(a) for a in res[1 + n_d:1 + 2 * n_d]],
            [flat(a) for a in res[1 + 2 * n_d:]])


def _attn_bwd(q, k, proj, do, lse, delta, slopes, dilation, name):
    bpc = SEQ // dilation // ATTN_BLOCK
    heads = ATTN_HEADS_BWD
    n_blocks = SEQ // ATTN_BLOCK
    carry = bpc > 1
    width = heads * HEAD_DIM
    cur, prev = _attn_specs(heads)
    v_cur, v_prev = _attn_specs(heads, segment=2)
    assert heads == N_HEADS
    per_head = pl.BlockSpec((ATTN_BLOCK, LANES), lambda hg, b: (jnp.minimum(b, n_blocks - 1), 0))
    scale = HEAD_DIM ** -0.5

    def body(sl_ref, q_ref, kp_ref, kc_ref, vp_ref, vc_ref, do_ref, lse_ref, dl_ref,
             dq_ref, dk_ref, dv_ref, *scratch):
        b = pl.program_id(1)
        if carry:
            dk_carry, dv_carry = scratch

            @pl.when(b == n_blocks)
            def _():
                dk_ref[...] = dk_carry[...].astype(BF16)
                dv_ref[...] = dv_carry[...].astype(BF16)

            @pl.when(b < n_blocks)
            def _():
                step(sl_ref, q_ref, kp_ref, kc_ref, vp_ref, vc_ref, do_ref, lse_ref, dl_ref,
                     dq_ref, dk_ref, dv_ref, dk_carry, dv_carry, b)
        else:
            step(sl_ref, q_ref, kp_ref, kc_ref, vp_ref, vc_ref, do_ref, lse_ref, dl_ref,
                 dq_ref, dk_ref, dv_ref, None, None, b)

    def step(sl_ref, q_ref, kp_ref, kc_ref, vp_ref, vc_ref, do_ref, lse_ref, dl_ref,
             dq_ref, dk_ref, dv_ref, dk_carry, dv_carry, b):
        if carry:
            @pl.when(b == 0)
            def _():
                dk_carry[...] = jnp.zeros_like(dk_carry)
                dv_carry[...] = jnp.zeros_like(dv_carry)

        dist, valid = _attn_masks(b, bpc, dilation, None)
        q3 = _head_stack(lambda cols: q_ref[:, cols], heads)
        k3 = _head_stack(lambda cols: _key_tile(kp_ref, kc_ref, cols, bpc), heads)
        v3 = _head_stack(lambda cols: _key_tile(vp_ref, vc_ref, cols, bpc), heads)
        do3 = _head_stack(lambda cols: do_ref[:, cols], heads)
        lse3 = jnp.stack([lse_ref[:, h:h + 1] for h in range(heads)], axis=0)
        dl3 = jnp.stack([dl_ref[:, h:h + 1] for h in range(heads)], axis=0)
        s = lax.dot_general(q3, k3, BATCH_NT_DIMS, preferred_element_type=F32)
        p = jnp.exp(jnp.where(valid[None], s * scale - dist[None] * sl_ref[...], NEG_INF) - lse3)
        dp = lax.dot_general(do3, v3, BATCH_NT_DIMS, preferred_element_type=F32)
        ds = (p * (dp - dl3) * scale).astype(BF16)
        dq3 = lax.dot_general(ds, k3, BATCH_NN_DIMS, preferred_element_type=F32)
        dk3 = lax.dot_general(ds, q3, BATCH_TN_DIMS, preferred_element_type=F32)
        dv3 = lax.dot_general(p.astype(BF16), do3, BATCH_TN_DIMS, preferred_element_type=F32)
        for h in range(heads):
            cols = slice(h * HEAD_DIM, (h + 1) * HEAD_DIM)
            dq_ref[:, cols] = dq3[h].astype(BF16)
            if carry:
                dk_ref[:, cols] = (dk_carry[:, cols] + dk3[h, :ATTN_BLOCK]).astype(BF16)
                dv_ref[:, cols] = (dv_carry[:, cols] + dv3[h, :ATTN_BLOCK]).astype(BF16)
                dk_carry[:, cols] = dk3[h, ATTN_BLOCK:]
                dv_carry[:, cols] = dv3[h, ATTN_BLOCK:]
            else:
                dk_ref[:, cols] = dk3[h].astype(BF16)
                dv_ref[:, cols] = dv3[h].astype(BF16)

    kv_out = prev if carry else cur
    return pl.pallas_call(
        body, name=name, grid=(N_HEADS // heads, n_blocks + (1 if carry else 0)),
        in_specs=[pl.BlockSpec((heads, 1, 1), lambda hg, b: (hg, 0, 0)), cur, prev, cur, v_prev, v_cur,
                  cur, per_head, per_head],
        out_specs=[cur, kv_out, kv_out],
        out_shape=[jax.ShapeDtypeStruct((SEQ, D_MODEL), BF16)] * 3,
        scratch_shapes=[pltpu.VMEM((ATTN_BLOCK, width), F32)] * 2 if carry else [],
        compiler_params=_params("parallel", "arbitrary"),
    )(slopes.reshape(N_HEADS, 1, 1), q, k, k, proj, proj, do, lse, delta)


def _qknorm_bwd(proj, qw, kw, seg, dq, dk, dv, name):
    tm = 256

    def body(p_ref, qw_ref, kw_ref, seg_ref, dq_ref, dk_ref, dv_ref, dproj_ref, sums_ref):
        segv = seg_ref[...]
        sums = []
        for part, (w_ref, dn_ref) in enumerate(((qw_ref, dq_ref), (kw_ref, dk_ref))):
            raw = p_ref[:, part * D_MODEL:(part + 1) * D_MODEL].astype(F32)
            dn = dn_ref[...].astype(F32)
            r = _qk_rstd(raw, segv)
            gq = dn * w_ref[...]
            draw = r * gq - raw * (r * r * r) * (_segsum(raw * gq, segv) * (1.0 / HEAD_DIM))
            dproj_ref[:, part * D_MODEL:(part + 1) * D_MODEL] = draw.astype(BF16)
            sums.append(jnp.sum(dn * raw * r, axis=0, keepdims=True))
        dproj_ref[:, 2 * D_MODEL:] = dv_ref[...]

        @pl.when(pl.program_id(0) == 0)
        def _():
            sums_ref[...] = jnp.zeros_like(sums_ref)

        sums_ref[...] += jnp.concatenate(sums + [jnp.zeros((6, D_MODEL), F32)], axis=0)

    return pl.pallas_call(
        body, name=name, grid=(SEQ // tm,),
        in_specs=[_row_spec(tm, 3 * D_MODEL), _vec_spec(1, D_MODEL), _vec_spec(1, D_MODEL), _vec_spec(256, 256)]
        + [_row_spec(tm, D_MODEL)] * 3,
        out_specs=[_row_spec(tm, 3 * D_MODEL), _vec_spec(8, D_MODEL)],
        out_shape=[jax.ShapeDtypeStruct((SEQ, 3 * D_MODEL), BF16), jax.ShapeDtypeStruct((8, D_MODEL), F32)],
        compiler_params=_params("arbitrary"),
    )(proj, qw, kw, seg, dq, dk, dv)


def _to_classes(a, dilation):
    if dilation == 1:
        return a
    s, c = a.shape
    return a.reshape(s // dilation, dilation, c).transpose(1, 0, 2).reshape(s, c)


def _from_classes(a, dilation):
    if dilation == 1:
        return a
    s, c = a.shape
    return a.reshape(dilation, s // dilation, c).transpose(1, 0, 2).reshape(s, c)


def _cols_to_classes(a, dilation):
    if dilation == 1:
        return a
    r, s = a.shape
    return a.reshape(r, s // dilation, dilation).transpose(0, 2, 1).reshape(r, s)


B_TN = 512
B_GROUP_TILES = 3 * D_MODEL // B_TN
B_Z_TILE0 = 3 * B_GROUP_TILES
B_Z_TILES = D_MODEL // B_TN


def _local_step(x, target, mods, norm_g, conv_w, conv_b, ln_g, ln_b, q_norm, k_norm,
                weights_a, weights_b, forward_weights_b, send_grads_b, forward_grads_b, send_grads_a):
    row = lambda a, i: a[i:i + 1]
    shift0, scale0, gate0 = row(mods[0], 0), row(mods[0], 1), row(mods[0], 2)
    shift1, scale1, gate1 = row(mods[1], 0), row(mods[1], 1), row(mods[1], 2)
    g0, g1 = row(norm_g, 0), row(norm_g, 1)
    seg = _seg_matrix()
    slopes = jnp.exp2(-8.0 * jnp.arange(1, N_HEADS + 1, dtype=F32) / N_HEADS)
    qw = [jnp.tile(q_norm[g:g + 1], (1, N_HEADS)) for g in range(3)]
    kw = [jnp.tile(k_norm[g:g + 1], (1, N_HEADS)) for g in range(3)]

    h0, h0t = _normmod_fwd(x, g0, scale0, shift0, "prenorm0")
    wa_in, wa_out = weights_a(h0)
    ja, _, nsa = wa_in.shape
    proj_a = _mm(h0, wa_in, tn=nsa, tile0=0, n_tiles=ja, out_dtype=F32, name="a_in")
    u5, u5t, u2 = _conv_fwd(proj_a, conv_w, conv_b, ln_g, ln_b, "a_conv")
    token = forward_weights_b(u5)
    x1, y_a, h1t, h1c = _out_a(u5, wa_out, x, gate0 + token[0:1, 0:1], g1, scale1, shift1, "a_out")

    wb_in, wb_out = weights_b(x1)
    jb, _, nsb = wb_in.shape
    h1 = h1c[0]
    z_b = _mm(h1, wb_in, tn=B_TN, tile0=B_Z_TILE0, n_tiles=B_Z_TILES, out_dtype=F32, name="b_in_z")
    proj_g, qkv, o_parts, lse_parts = [], [], [], []
    for g, d in enumerate(DILATIONS):
        pg = _mm(h1c[g], wb_in, tn=B_TN, tile0=g * B_GROUP_TILES, n_tiles=B_GROUP_TILES, out_dtype=BF16,
                 name=f"b_in_g{g}")
        qn, kn = _qknorm_fwd(pg, qw[g], kw[g], seg, f"b_qknorm_g{g}")
        og, lg = _attn_fwd(qn, kn, pg, slopes, d, f"b_attn_g{g}")
        proj_g.append(pg)
        qkv.append((qn, kn))
        o_parts.append(og if d == 1 else og.reshape(d, SEQ // d, D_MODEL))
        lse_parts.append(lg if d == 1 else lg.reshape(d, SEQ // d, LANES))
    sel = _head_selector()
    u_b, u_bt, o_b, lse_b = _merge_fwd(o_parts, lse_parts, z_b, sel, "b_merge")
    e, dy_b, sums_loss = _out_b_loss(u_b, wb_out, x1, gate1, target, "b_out_loss")

    dwb_out = _mm(u_bt, dy_b, tn=D_MODEL, tile0=0, n_tiles=1, out_dtype=BF16, name="b_dwout")
    dz_b, do_c, delta_c, lse_c = _merge_bwd(dy_b, wb_out, o_b, lse_b, z_b, sel, "b_merge_bwd")
    dwb_in = _mm(h1t, dz_b, tn=B_TN, tile0=B_Z_TILE0, n_tiles=B_Z_TILES, out_dtype=BF16, name="b_dwin_z",
                 out3d=(jb, nsb))
    dh1_parts = [_mm_nt(dz_b, wb_in, tn=B_TN, tile0=B_Z_TILE0, n_tiles=B_Z_TILES, name="b_dh_z")]
    qk_sums = []
    for g, d in enumerate(DILATIONS):
        qn, kn = qkv[g]
        dq, dk, dv = _attn_bwd(qn, kn, proj_g[g], do_c[g], lse_c[g], delta_c[g], slopes, d, f"b_attn_bwd_g{g}")
        dproj, sums_qk = _qknorm_bwd(proj_g[g], qw[g], kw[g], seg, dq, dk, dv, f"b_qknorm_bwd_g{g}")
        qk_sums.append(sums_qk)
        dwb_in = _mm(h1t if d == 1 else h1c[g], dproj, tn=B_TN, tile0=g * B_GROUP_TILES, n_tiles=B_GROUP_TILES,
                     out_dtype=BF16, name=f"b_dwin_g{g}", out3d=(jb, nsb), prev=dwb_in, transpose_lhs=d != 1)
        dh = _mm_nt(dproj, wb_in, tn=B_TN, tile0=g * B_GROUP_TILES, n_tiles=B_GROUP_TILES, name=f"b_dh_g{g}")
        dh1_parts.append(dh)
    token = send_grads_b(dwb_in, dwb_out)
    dx1, sums_n1, dy_a = _normmod_bwd(x1, g1, scale1 + token[0:1, 0:1], dh1_parts, e, "prenorm1_bwd",
                                      part_dilations=(1,) + DILATIONS, gated=(gate0, y_a))
    token = forward_grads_b(dx1)

    dwa_out = _mm(u5t, dy_a, tn=D_MODEL, tile0=0, n_tiles=1, out_dtype=BF16, name="a_dwout")
    du2, dz_a, sums_ln = _conv_bwd_pointwise(dy_a, wa_out, proj_a, u2, ln_g + token[0:1, 0:1], ln_b,
                                             "a_conv_bwd_pw")
    dproj_a, dconv_w = _conv_bwd_taps(du2, dz_a, proj_a, conv_w, "a_conv_bwd_taps")
    dwa_in = _mm(h0t, dproj_a, tn=nsa, tile0=0, n_tiles=ja, out_dtype=BF16, name="a_dwin", out3d=(ja, nsa))
    token = send_grads_a(dwa_in, dwa_out)
    dh0 = _mm_nt(dproj_a, wa_in, tn=nsa, tile0=0, n_tiles=ja, name="a_dh", after=token)
    grad_x, sums_n0 = _normmod_bwd(x, g0, scale0, [dh0], dx1, "prenorm0_bwd")

    small = dict(
        dnorm_g=jnp.concatenate([sums_n0[0:1], sums_n1[0:1]], axis=0),
        dmod0=jnp.concatenate([sums_n0[2:3], sums_n0[1:2], sums_n1[3:4]], axis=0),
        dmod1=jnp.concatenate([sums_n1[2:3], sums_n1[1:2], sums_loss[0:1]], axis=0),
        dln_g=sums_ln[0:1], dln_b=sums_ln[1:2], dconv_b=sums_ln[2:3],
        dconv_w=dconv_w[:CONV_WIDTH],
        dq_norm=jnp.concatenate([s[0:1] for s in qk_sums], axis=0),
        dk_norm=jnp.concatenate([s[1:2] for s in qk_sums], axis=0),
        loss_cols=sums_loss[1:2],
    )
    return grad_x, small


def _adamw(w, g, m, v, name, after=None, copy_grad=False):
    rows, cols = w.shape
    tr = rows if rows <= 128 else 128
    c1 = 1.0 / (1.0 - ADAM_B1 ** ADAM_STEP)
    c2 = 1.0 / (1.0 - ADAM_B2 ** ADAM_STEP)
    extra = [] if after is None else [after]
    n_out = 4 if copy_grad else 3

    def body(w_ref, g_ref, m_ref, v_ref, *rest):
        d_ref, mo_ref, vo_ref = rest[len(extra):len(extra) + 3]
        gv = g_ref[...]
        if copy_grad:
            rest[-1][...] = gv
        mn = ADAM_B1 * m_ref[...] + (1.0 - ADAM_B1) * gv
        vn = ADAM_B2 * v_ref[...] + (1.0 - ADAM_B2) * (gv * gv)
        mo_ref[...] = mn
        vo_ref[...] = vn
        d_ref[...] = -ADAM_LR * ((mn * c1) / (jnp.sqrt(vn * c2) + ADAM_EPS) + ADAM_WD * w_ref[...])

    spec = pl.BlockSpec((tr, cols), lambda i: (i, 0))
    return pl.pallas_call(
        body, name=name, grid=(rows // tr,),
        in_specs=[spec] * 4 + [pl.BlockSpec(memory_space=pl.ANY)] * len(extra), out_specs=[spec] * n_out,
        out_shape=[jax.ShapeDtypeStruct((rows, cols), F32)] * n_out,
        compiler_params=_params("parallel"),
    )(w, g, m, v, *extra)


def _cast_into_slot(w, chip_idx, name):
    rows, cols = w.shape
    tr = 256

    def body(ch_ref, w_ref, o_ref):
        o_ref[...] = w_ref[...].astype(BF16)

    return pl.pallas_call(
        body, name=name,
        grid_spec=pltpu.PrefetchScalarGridSpec(
            num_scalar_prefetch=1, grid=(rows // tr,),
            in_specs=[pl.BlockSpec((tr, cols), lambda i, ch: (i, 0))],
            out_specs=pl.BlockSpec((None, tr, cols), lambda i, ch: (ch[0], i, 0))),
        out_shape=jax.ShapeDtypeStruct((N_CHIPS, rows, cols), BF16), compiler_params=_params("parallel"),
    )(chip_idx, w)


def _position():
    x, y, c = lax.axis_index("x"), lax.axis_index("y"), lax.axis_index("c")
    return x, y, c


def _xor_peer(x, y, c, k):
    return (x ^ ((k >> 2) & 1), y ^ ((k >> 1) & 1), c ^ (k & 1))


def _chip_peer(x, y, k):
    return (x ^ ((k >> 1) & 1), y ^ (k & 1))


def _ada_forward(c_row, ada_w, ada_b, conv_w):
    ns = ada_w.shape[2]
    cw = conv_w.shape[1]

    def body(c_ref, w_ref, b_ref, cv_ref, mod_ref, sc_ref, cvo_ref,
             c_all, mp, parts, cv_parts, send1, recv1, send2, recv2, send3, recv3):
        x, y, c = _position()
        me = 4 * x + 2 * y + c
        chip = 2 * x + y

        def c_copy(k):
            return pltpu.make_async_remote_copy(
                src_ref=c_all.at[me], dst_ref=c_all.at[me], send_sem=send1.at[k - 1], recv_sem=recv1.at[k - 1],
                device_id=_xor_peer(x, y, c, k), device_id_type=MESH)

        def cv_copy(k):
            px, py = _chip_peer(x, y, k)
            return pltpu.make_async_remote_copy(
                src_ref=cv_parts.at[chip], dst_ref=cv_parts.at[chip], send_sem=send3.at[k - 1],
                recv_sem=recv3.at[k - 1], device_id=(px, py, c), device_id_type=MESH)

        c_all[me] = c_ref[...]
        cv_parts[chip] = cv_ref[...]
        for k in range(1, N_DEV):
            c_copy(k).start()
        for k in range(1, N_CHIPS):
            cv_copy(k).start()
        for k in range(1, N_DEV):
            c_copy(k).wait_recv()
        cv = jnp.concatenate([c_all[i] for i in range(N_DEV)], axis=0)
        sc = cv * _sigmoid(cv)
        sc_ref[...] = sc
        for l in range(2):
            res = jnp.dot(sc, w_ref[l], preferred_element_type=F32, precision=lax.Precision.HIGHEST)
            for i in range(N_DEV):
                mp[i, l:l + 1, :] = res[i:i + 1, :]

        def mod_copy(k):
            px, py = _chip_peer(x, y, k)
            return pltpu.make_async_remote_copy(
                src_ref=mp.at[4 * px + 2 * py + c], dst_ref=parts.at[chip], send_sem=send2.at[k - 1],
                recv_sem=recv2.at[k - 1], device_id=(px, py, c), device_id_type=MESH)

        for k in range(1, N_CHIPS):
            mod_copy(k).start()
        parts[chip] = mp[me]
        for k in range(1, N_CHIPS):
            mod_copy(k).wait_recv()
            cv_copy(k).wait_recv()
        mod_ref[...] = jnp.concatenate([parts[j] for j in range(N_CHIPS)], axis=1) + b_ref[...]
        cvo_ref[...] = jnp.concatenate([cv_parts[j] for j in range(N_CHIPS)], axis=1)
        for k in range(1, N_DEV):
            c_copy(k).wait_send()
        for k in range(1, N_CHIPS):
            mod_copy(k).wait_send()
            cv_copy(k).wait_send()

    vm = pl.BlockSpec(memory_space=pltpu.VMEM)
    return pl.pallas_call(
        body, name="ada_forward",
        in_specs=[vm] * 4, out_specs=[vm] * 3,
        out_shape=[jax.ShapeDtypeStruct((2, 3 * D_MODEL), F32), jax.ShapeDtypeStruct((N_DEV, D_MODEL), F32),
                   jax.ShapeDtypeStruct((CONV_WIDTH, N_CHIPS * cw), F32)],
        scratch_shapes=[pltpu.VMEM((N_DEV, 1, D_MODEL), F32), pltpu.VMEM((N_DEV, 2, ns), F32),
                        pltpu.VMEM((N_CHIPS, 2, ns), F32), pltpu.VMEM((N_CHIPS, CONV_WIDTH, cw), F32),
                        pltpu.SemaphoreType.DMA((N_DEV - 1,)), pltpu.SemaphoreType.DMA((N_DEV - 1,)),
                        pltpu.SemaphoreType.DMA((N_CHIPS - 1,)), pltpu.SemaphoreType.DMA((N_CHIPS - 1,)),
                        pltpu.SemaphoreType.DMA((N_CHIPS - 1,)), pltpu.SemaphoreType.DMA((N_CHIPS - 1,))],
        compiler_params=pltpu.CompilerParams(vmem_limit_bytes=VMEM_LIMIT_BYTES),
    )(c_row, ada_w, ada_b, conv_w)


HBM_SPEC = pl.BlockSpec(memory_space=pltpu.HBM)
ANY_SPEC = pl.BlockSpec(memory_space=pl.ANY)
SEM_SPEC = pl.BlockSpec(memory_space=pltpu.SEMAPHORE)
SPLIT_PARAMS = dict(compiler_params=pltpu.CompilerParams(has_side_effects=pltpu.SideEffectType.DATAFLOW_SIDE_EFFECTING))
TOKEN = jax.ShapeDtypeStruct((8, 128), F32)


def _hbm(arrays):
    return [pltpu.with_memory_space_constraint(a, pltpu.HBM) for a in arrays]


def _hbm_like(arrays):
    return [pltpu.HBM(a.shape, a.dtype) for a in arrays]


def _gather_start(lands, after, name):
    n = len(lands)

    def body(*refs):
        ins = refs[:n]
        send, recv = refs[n + 1], refs[n + 2]
        x, y, c = _position()
        chip = 2 * x + y
        for t in range(n):
            rh = ins[t].shape[1] // 2
            for k in range(1, N_CHIPS):
                px, py = _chip_peer(x, y, k)
                block = ins[t].at[chip, pl.ds(c * rh, rh)]
                pltpu.make_async_remote_copy(
                    src_ref=block, dst_ref=block, send_sem=send.at[3 * t + k - 1], recv_sem=recv.at[3 * t + k - 1],
                    device_id=(px, py, c), device_id_type=MESH).start()
        refs[-1][...] = jnp.zeros(TOKEN.shape, F32)

    res = pl.pallas_call(
        body, name=name, in_specs=[HBM_SPEC] * n + [ANY_SPEC],
        out_specs=(SEM_SPEC, SEM_SPEC, *[HBM_SPEC] * n, pl.BlockSpec(memory_space=pltpu.VMEM)),
        out_shape=(pltpu.SemaphoreType.DMA((3 * n,)), pltpu.SemaphoreType.DMA((3 * n,)), *_hbm_like(lands), TOKEN),
        input_output_aliases={t: 2 + t for t in range(n)}, **SPLIT_PARAMS,
    )(*_hbm(lands), after)
    return res[0], res[1], list(res[2:2 + n]), res[-1]


def _gather_forward(send, recv, lands, after, name):
    n = len(lands)

    def body(*refs):
        ins = refs[:n]
        send1, recv1 = refs[n], refs[n + 1]
        send2, recv2 = refs[n + 3], refs[n + 4]
        x, y, c = _position()
        chip = 2 * x + y
        for t in range(n):
            rh = ins[t].shape[1] // 2
            half = pl.ds(c * rh, rh)
            for k in range(1, N_CHIPS):
                px, py = _chip_peer(x, y, k)
                s = 3 * t + k - 1
                got = ins[t].at[2 * px + py, half]
                cp = pltpu.make_async_remote_copy(
                    src_ref=ins[t].at[chip, half], dst_ref=got, send_sem=send1.at[s], recv_sem=recv1.at[s],
                    device_id=(px, py, c), device_id_type=MESH)
                cp.wait_send()
                cp.wait_recv()
                pltpu.make_async_remote_copy(
                    src_ref=got, dst_ref=got, send_sem=send2.at[s], recv_sem=recv2.at[s],
                    device_id=(x, y, 1 - c), device_id_type=MESH).start()
        refs[-1][...] = jnp.zeros(TOKEN.shape, F32)

    res = pl.pallas_call(
        body, name=name, in_specs=[HBM_SPEC] * n + [SEM_SPEC, SEM_SPEC, ANY_SPEC],
        out_specs=(SEM_SPEC, SEM_SPEC, *[HBM_SPEC] * n, pl.BlockSpec(memory_space=pltpu.VMEM)),
        out_shape=(pltpu.SemaphoreType.DMA((3 * n,)), pltpu.SemaphoreType.DMA((3 * n,)), *_hbm_like(lands), TOKEN),
        input_output_aliases={t: 2 + t for t in range(n)}, **SPLIT_PARAMS,
    )(*lands, send, recv, after)
    return res[0], res[1], list(res[2:2 + n]), res[-1]


def _gather_wait(send, recv, lands, after, name):
    n = len(lands)

    def body(*refs):
        ins = refs[:n]
        send_ref, recv_ref = refs[n], refs[n + 1]
        x, y, c = _position()
        for t in range(n):
            rh = ins[t].shape[1] // 2
            for k in range(1, N_CHIPS):
                px, py = _chip_peer(x, y, k)
                cp = pltpu.make_async_remote_copy(
                    src_ref=ins[t].at[2 * px + py, pl.ds(c * rh, rh)],
                    dst_ref=ins[t].at[2 * px + py, pl.ds((1 - c) * rh, rh)], send_sem=send_ref.at[3 * t + k - 1],
                    recv_sem=recv_ref.at[3 * t + k - 1], device_id=(x, y, 1 - c), device_id_type=MESH)
                cp.wait_send()
                cp.wait_recv()

    res = pl.pallas_call(
        body, name=name, in_specs=[HBM_SPEC] * n + [SEM_SPEC, SEM_SPEC, ANY_SPEC], out_specs=[HBM_SPEC] * n,
        out_shape=_hbm_like(lands), input_output_aliases={t: t for t in range(n)}, **SPLIT_PARAMS,
    )(*lands, send, recv, after)
    return list(res)


def _reduce_start(grads, after, name):
    n = len(grads)
    lands = [lax.empty((N_DEV, g.shape[1] // 2, g.shape[2]), BF16) for g in grads]

    def body(*refs):
        gs, ls = refs[:n], refs[n:2 * n]
        send, recv = refs[2 * n + 1], refs[2 * n + 2]
        x, y, c = _position()
        me = 4 * x + 2 * y + c
        for t in range(n):
            rh = gs[t].shape[1] // 2
            for k in range(1, N_DEV):
                px, py, pc = _xor_peer(x, y, c, k)
                pltpu.make_async_remote_copy(
                    src_ref=gs[t].at[2 * px + py, pl.ds(pc * rh, rh)], dst_ref=ls[t].at[me],
                    send_sem=send.at[7 * t + k - 1], recv_sem=recv.at[7 * t + k - 1],
                    device_id=(px, py, pc), device_id_type=MESH).start()
        refs[-1][...] = jnp.zeros(TOKEN.shape, F32)

    res = pl.pallas_call(
        body, name=name, in_specs=[HBM_SPEC] * (2 * n) + [ANY_SPEC],
        out_specs=(SEM_SPEC, SEM_SPEC, *[HBM_SPEC] * (2 * n), pl.BlockSpec(memory_space=pltpu.VMEM)),
        out_shape=(pltpu.SemaphoreType.DMA((7 * n,)), pltpu.SemaphoreType.DMA((7 * n,)),
                   *_hbm_like(grads), *_hbm_like(lands), TOKEN),
        input_output_aliases={t: 2 + t for t in range(2 * n)}, **SPLIT_PARAMS,
    )(*_hbm(grads), *_hbm(lands), after)
    return res[0], res[1], list(res[2:2 + n]), list(res[2 + n:2 + 2 * n]), res[-1]


def _reduce_wait(send, recv, grads, lands, after, name):
    n = len(grads)

    def body(*refs):
        gs, ls = refs[:n], refs[n:2 * n]
        send_ref, recv_ref = refs[2 * n], refs[2 * n + 1]
        x, y, c = _position()
        for t in range(n):
            rh = gs[t].shape[1] // 2
            for k in range(1, N_DEV):
                px, py, pc = _xor_peer(x, y, c, k)
                cp = pltpu.make_async_remote_copy(
                    src_ref=gs[t].at[2 * px + py, pl.ds(pc * rh, rh)], dst_ref=ls[t].at[4 * px + 2 * py + pc],
                    send_sem=send_ref.at[7 * t + k - 1], recv_sem=recv_ref.at[7 * t + k - 1],
                    device_id=(px, py, pc), device_id_type=MESH)
                cp.wait_send()
                cp.wait_recv()

    res = pl.pallas_call(
        body, name=name, in_specs=[HBM_SPEC] * (2 * n) + [SEM_SPEC, SEM_SPEC, ANY_SPEC], out_specs=[HBM_SPEC] * (2 * n),
        out_shape=_hbm_like(grads) + _hbm_like(lands), input_output_aliases={t: t for t in range(2 * n)}, **SPLIT_PARAMS,
    )(*grads, *lands, send, recv, after)
    return list(res[:n]), list(res[n:])


def _sum_devices(land, grad, dev_idx, name):
    _, rh, cols = land.shape
    tr = 128
    nb = rh // tr

    def body(idx_ref, l_ref, g_ref, o_ref):
        me = idx_ref[0]
        acc = jnp.where(me == 0, g_ref[...], l_ref[0]).astype(F32)
        for d in range(1, N_DEV):
            acc = acc + jnp.where(me == d, g_ref[...], l_ref[d]).astype(F32)
        o_ref[...] = acc

    return pl.pallas_call(
        body, name=name,
        grid_spec=pltpu.PrefetchScalarGridSpec(
            num_scalar_prefetch=1, grid=(nb,),
            in_specs=[pl.BlockSpec((N_DEV, tr, cols), lambda i, idx: (0, i, 0)),
                      pl.BlockSpec((None, tr, cols), lambda i, idx: (idx[1], idx[2] * nb + i, 0))],
            out_specs=pl.BlockSpec((tr, cols), lambda i, idx: (idx[2] * nb + i, 0))),
        out_shape=jax.ShapeDtypeStruct((2 * rh, cols), F32), compiler_params=_params("parallel"),
    )(dev_idx, land, grad)


def _split_start(name, arrays, n_sems, after, issue):
    m = len(arrays)

    def body(*refs):
        issue(refs[:m], refs[m + 1], refs[m + 2])
        refs[-1][...] = jnp.zeros(TOKEN.shape, F32)

    res = pl.pallas_call(
        body, name=name, in_specs=[HBM_SPEC] * m + [ANY_SPEC],
        out_specs=(SEM_SPEC, SEM_SPEC, *[HBM_SPEC] * m, pl.BlockSpec(memory_space=pltpu.VMEM)),
        out_shape=(pltpu.SemaphoreType.DMA((n_sems,)), pltpu.SemaphoreType.DMA((n_sems,)), *_hbm_like(arrays), TOKEN),
        input_output_aliases={t: 2 + t for t in range(m)}, **SPLIT_PARAMS,
    )(*_hbm(arrays), after)
    return res[0], res[1], list(res[2:2 + m]), res[-1]


def _split_wait(name, arrays, send, recv, after, await_all):
    m = len(arrays)

    def body(*refs):
        await_all(refs[:m], refs[m], refs[m + 1])

    res = pl.pallas_call(
        body, name=name, in_specs=[HBM_SPEC] * m + [SEM_SPEC, SEM_SPEC, ANY_SPEC], out_specs=[HBM_SPEC] * m,
        out_shape=_hbm_like(arrays), input_output_aliases={t: t for t in range(m)}, **SPLIT_PARAMS,
    )(*arrays, send, recv, after)
    return list(res)


def _sibling_copies(refs, send, recv, n):
    x, y, c = _position()
    cps = []
    for t in range(n):
        rh = refs[t].shape[1] // 2
        cps.append(pltpu.make_async_remote_copy(
            src_ref=refs[t].at[pl.ds(0, N_CHIPS), pl.ds((1 - c) * rh, rh)], dst_ref=refs[n + t],
            send_sem=send.at[t], recv_sem=recv.at[t], device_id=(x, y, 1 - c), device_id_type=MESH))
    return cps


def _reduce_sibling_start(grads, after, name):
    n = len(grads)
    lands = [lax.empty((N_CHIPS, g.shape[1] // 2, g.shape[2]), BF16) for g in grads]

    def issue(refs, send, recv):
        for cp in _sibling_copies(refs, send, recv, n):
            cp.start()

    return _split_start(name, list(grads) + lands, n, after, issue)


def _reduce_sibling_wait(send, recv, arrays, after, name):
    n = len(arrays) // 2

    def await_all(refs, send_ref, recv_ref):
        for cp in _sibling_copies(refs, send_ref, recv_ref, n):
            cp.wait_send()
            cp.wait_recv()

    res = _split_wait(name, arrays, send, recv, after, await_all)
    return res[:n], res[n:]


def _add_sibling_half(grad, got, dev_idx, name):
    j, r, cols = grad.shape
    rh = r // 2
    tr = 128
    nb = rh // tr

    def body(idx_ref, g_ref, got_ref, out_ref):
        out_ref[...] = (g_ref[...].astype(F32) + got_ref[...].astype(F32)).astype(BF16)

    return pl.pallas_call(
        body, name=name,
        grid_spec=pltpu.PrefetchScalarGridSpec(
            num_scalar_prefetch=1, grid=(j, nb),
            in_specs=[pl.BlockSpec((None, tr, cols), lambda jj, i, idx: (jj, idx[2] * nb + i, 0)),
                      pl.BlockSpec((None, tr, cols), lambda jj, i, idx: (jj, i, 0))],
            out_specs=pl.BlockSpec((None, tr, cols), lambda jj, i, idx: (jj, i, 0))),
        out_shape=jax.ShapeDtypeStruct((j, rh, cols), BF16),
        compiler_params=_params("parallel", "parallel"),
    )(dev_idx, grad, got)


def _chip_copies(refs, send, recv, n, receiving):
    x, y, c = _position()
    chip = 2 * x + y
    cps = []
    for t in range(n):
        for k in range(1, N_CHIPS):
            px, py = _chip_peer(x, y, k)
            cps.append(pltpu.make_async_remote_copy(
                src_ref=refs[t].at[2 * px + py], dst_ref=refs[n + t].at[2 * px + py if receiving else chip],
                send_sem=send.at[3 * t + k - 1], recv_sem=recv.at[3 * t + k - 1],
                device_id=(px, py, c), device_id_type=MESH))
    return cps


def _reduce_chips_start(partials, after, name):
    n = len(partials)
    lands = [lax.empty(p.shape, BF16) for p in partials]

    def issue(refs, send, recv):
        for cp in _chip_copies(refs, send, recv, n, False):
            cp.start()

    return _split_start(name, list(partials) + lands, 3 * n, after, issue)


def _reduce_chips_wait(send, recv, arrays, after, name):
    n = len(arrays) // 2

    def await_all(refs, send_ref, recv_ref):
        for cp in _chip_copies(refs, send_ref, recv_ref, n, True):
            cp.wait_send()
            cp.wait_recv()

    res = _split_wait(name, arrays, send, recv, after, await_all)
    return res[:n], res[n:]


def _sum_partials(land, partial, dev_idx, name):
    _, rh, cols = land.shape
    tr = 128
    nb = rh // tr

    def body(idx_ref, l_ref, p_ref, o_ref):
        chip = idx_ref[1]
        acc = jnp.where(chip == 0, p_ref[...], l_ref[0]).astype(F32)
        for s in range(1, N_CHIPS):
            acc = acc + jnp.where(chip == s, p_ref[...], l_ref[s]).astype(F32)
        o_ref[...] = acc

    return pl.pallas_call(
        body, name=name,
        grid_spec=pltpu.PrefetchScalarGridSpec(
            num_scalar_prefetch=1, grid=(nb,),
            in_specs=[pl.BlockSpec((N_CHIPS, tr, cols), lambda i, idx: (0, i, 0)),
                      pl.BlockSpec((None, tr, cols), lambda i, idx: (idx[1], i, 0))],
            out_specs=pl.BlockSpec((tr, cols), lambda i, idx: (idx[2] * nb + i, 0))),
        out_shape=jax.ShapeDtypeStruct((2 * rh, cols), F32), compiler_params=_params("parallel"),
    )(dev_idx, land, partial)


def _half_copies(refs, send, recv, receiving):
    x, y, c = _position()
    cps = []
    for t, ref in enumerate(refs):
        rh = ref.shape[0] // 2
        cps.append(pltpu.make_async_remote_copy(
            src_ref=ref.at[pl.ds(c * rh, rh)], dst_ref=ref.at[pl.ds(((1 - c) if receiving else c) * rh, rh)],
            send_sem=send.at[t], recv_sem=recv.at[t], device_id=(x, y, 1 - c), device_id_type=MESH))
    return cps


def _share_halves_start(totals, after, name):
    def issue(refs, send, recv):
        for cp in _half_copies(refs, send, recv, False):
            cp.start()

    return _split_start(name, list(totals), len(totals), after, issue)


def _share_halves_wait(send, recv, totals, after, name):
    def await_all(refs, send_ref, recv_ref):
        for cp in _half_copies(refs, send_ref, recv_ref, True):
            cp.wait_send()
            cp.wait_recv()

    return _split_wait(name, totals, send, recv, after, await_all)


def _exchange_halves(grads):
    n = len(grads)
    hbm = pl.BlockSpec(memory_space=pl.ANY)

    def body(*refs):
        ins, outs = refs[:n], refs[n:2 * n]
        send, recv = refs[2 * n:]
        x, y, c = _position()
        cps = []
        for t in range(n):
            rh = ins[t].shape[1] // 2
            cp = pltpu.make_async_remote_copy(
                src_ref=ins[t].at[pl.ds(0, N_CHIPS), pl.ds((1 - c) * rh, rh)], dst_ref=outs[t], send_sem=send.at[t],
                recv_sem=recv.at[t], device_id=(x, y, 1 - c), device_id_type=MESH)
            cp.start()
            cps.append(cp)
        for cp in cps:
            cp.wait()

    return pl.pallas_call(
        body, name="reduce_exchange_halves", in_specs=[hbm] * n, out_specs=[hbm] * n,
        out_shape=[jax.ShapeDtypeStruct((g.shape[0], g.shape[1] // 2, g.shape[2]), BF16) for g in grads],
        scratch_shapes=[pltpu.SemaphoreType.DMA((n,)), pltpu.SemaphoreType.DMA((n,))],
    )(*grads)


def _add_halves(grad, got, c_idx, name):
    j, r, cols = grad.shape
    rh = r // 2
    tr = 128
    nb = rh // tr

    def body(c_ref, g_ref, o_ref_in, out_ref):
        out_ref[...] = (g_ref[...].astype(F32) + o_ref_in[...].astype(F32)).astype(BF16)

    return pl.pallas_call(
        body, name=name,
        grid_spec=pltpu.PrefetchScalarGridSpec(
            num_scalar_prefetch=1, grid=(j, nb),
            in_specs=[pl.BlockSpec((None, tr, cols), lambda jj, i, c_ref: (jj, c_ref[0] * nb + i, 0)),
                      pl.BlockSpec((None, tr, cols), lambda jj, i, c_ref: (jj, i, 0))],
            out_specs=pl.BlockSpec((None, tr, cols), lambda jj, i, c_ref: (jj, i, 0))),
        out_shape=jax.ShapeDtypeStruct((j, rh, cols), BF16),
        compiler_params=_params("parallel", "parallel"),
    )(c_idx, grad, got)


def _scatter_partials(partials):
    n = len(partials)
    hbm = pl.BlockSpec(memory_space=pl.ANY)

    def body(*refs):
        ins, outs = refs[:n], refs[n:2 * n]
        send, recv, local = refs[2 * n:]
        x, y, c = _position()
        chip = 2 * x + y
        cps, lcs = [], []
        for t in range(n):
            lc = pltpu.make_async_copy(ins[t].at[chip], outs[t].at[chip], local.at[t])
            lc.start()
            lcs.append(lc)
            for k in range(1, N_CHIPS):
                px, py = _chip_peer(x, y, k)
                s = 3 * t + k - 1
                cp = pltpu.make_async_remote_copy(
                    src_ref=ins[t].at[2 * px + py], dst_ref=outs[t].at[chip], send_sem=send.at[s],
                    recv_sem=recv.at[s], device_id=(px, py, c), device_id_type=MESH)
                cp.start()
                cps.append(cp)
        for cp in cps:
            cp.wait()
        for lc in lcs:
            lc.wait()

    return pl.pallas_call(
        body, name="reduce_scatter_partials", in_specs=[hbm] * n, out_specs=[hbm] * n,
        out_shape=[jax.ShapeDtypeStruct(p.shape, BF16) for p in partials],
        scratch_shapes=[pltpu.SemaphoreType.DMA((3 * n,)), pltpu.SemaphoreType.DMA((3 * n,)),
                        pltpu.SemaphoreType.DMA((n,))],
    )(*partials)


def _sum_chips(parts, name):
    j, rh, cols = parts.shape
    tr = 128

    def body(p_ref, o_ref):
        acc = p_ref[0].astype(F32)
        for s in range(1, j):
            acc = acc + p_ref[s].astype(F32)
        o_ref[...] = acc

    return pl.pallas_call(
        body, name=name, grid=(rh // tr,),
        in_specs=[pl.BlockSpec((j, tr, cols), lambda i: (0, i, 0))],
        out_specs=pl.BlockSpec((tr, cols), lambda i: (i, 0)),
        out_shape=jax.ShapeDtypeStruct((rh, cols), F32),
        compiler_params=_params("parallel"),
    )(parts)


def _share_totals(halves):
    n = len(halves)
    hbm = pl.BlockSpec(memory_space=pl.ANY)

    def body(*refs):
        ins, outs = refs[:n], refs[n:2 * n]
        send, recv, local = refs[2 * n:]
        x, y, c = _position()
        cps, lcs = [], []
        for t in range(n):
            rh = ins[t].shape[0]
            mine = outs[t].at[pl.ds(c * rh, rh)]
            lc = pltpu.make_async_copy(ins[t], mine, local.at[t])
            lc.start()
            lcs.append(lc)
            cp = pltpu.make_async_remote_copy(
                src_ref=ins[t], dst_ref=mine, send_sem=send.at[t], recv_sem=recv.at[t],
                device_id=(x, y, 1 - c), device_id_type=MESH)
            cp.start()
            cps.append(cp)
        for cp in cps:
            cp.wait()
        for lc in lcs:
            lc.wait()

    return pl.pallas_call(
        body, name="reduce_share_totals", in_specs=[hbm] * n, out_specs=[hbm] * n,
        out_shape=[jax.ShapeDtypeStruct((2 * h.shape[0], h.shape[1]), F32) for h in halves],
        scratch_shapes=[pltpu.SemaphoreType.DMA((n,)), pltpu.SemaphoreType.DMA((n,)),
                        pltpu.SemaphoreType.DMA((n,))],
    )(*halves)


SMALL_ROWS = 56


def _small_copies(refs, send, recv, receiving):
    x, y, c = _position()
    me = 4 * x + 2 * y + c
    cps = []
    for k in range(1, N_DEV):
        px, py, pc = _xor_peer(x, y, c, k)
        cps.append(pltpu.make_async_remote_copy(
            src_ref=refs[0], dst_ref=refs[1].at[4 * px + 2 * py + pc if receiving else me],
            send_sem=send.at[k - 1], recv_sem=recv.at[k - 1], device_id=(px, py, pc), device_id_type=MESH))
    return cps


def _small_gather_start(packed, after):
    land = lax.empty((N_DEV,) + packed.shape, F32)

    def issue(refs, send, recv):
        for cp in _small_copies(refs, send, recv, False):
            cp.start()

    return _split_start("small_gather_start", [packed, land], N_DEV - 1, after, issue)


def _small_gather_wait(send, recv, arrays, after):
    def await_all(refs, send_ref, recv_ref):
        for cp in _small_copies(refs, send_ref, recv_ref, True):
            cp.wait_send()
            cp.wait_recv()

    return _split_wait("small_gather_wait", arrays, send, recv, after, await_all)


def _reduce_small(packed, land, silu_c):
    ns = 3 * D_MODEL // N_CHIPS

    def body(p_ref, land_ref, sc_ref, tot_ref, gw_ref, loss_ref, qk_ref, allp):
        x, y, c = _position()
        me = 4 * x + 2 * y + c
        chip = 2 * x + y
        for i in range(N_DEV):
            allp[i] = jnp.where(me == i, p_ref[...], land_ref[i])
        tot = allp[0]
        for i in range(1, N_DEV):
            tot = tot + allp[i]
        tot_ref[...] = tot
        loss_ref[...] = jnp.sum(tot[11:12, :], axis=1, keepdims=True) * (0.5 / D_MODEL)
        fold = tot[5:11, 0:HEAD_DIM]
        for h in range(1, N_HEADS):
            fold = fold + tot[5:11, h * HEAD_DIM:(h + 1) * HEAD_DIM]
        qk_ref[...] = jnp.concatenate([fold, jnp.zeros((2, HEAD_DIM), F32)], axis=0)
        sct = sc_ref[...].T
        rc = 64
        for l in range(2):
            dms = [allp[i, pl.ds(12 + 4 * l + chip, 1), :][:, :ns] for i in range(N_DEV)]
            for r0 in range(0, D_MODEL, rc):
                acc = sct[r0:r0 + rc, 0:1] * dms[0]
                for i in range(1, N_DEV):
                    acc = acc + sct[r0:r0 + rc, i:i + 1] * dms[i]
                gw_ref[l, r0:r0 + rc, :] = acc

    vm = pl.BlockSpec(memory_space=pltpu.VMEM)
    return pl.pallas_call(
        body, name="reduce_small", in_specs=[vm, vm, vm], out_specs=[vm] * 4,
        out_shape=[jax.ShapeDtypeStruct((SMALL_ROWS, D_MODEL), F32), jax.ShapeDtypeStruct((2, D_MODEL, ns), F32),
                   jax.ShapeDtypeStruct((1, 1), F32), jax.ShapeDtypeStruct((8, HEAD_DIM), F32)],
        scratch_shapes=[pltpu.VMEM((N_DEV, SMALL_ROWS, D_MODEL), F32)],
        compiler_params=pltpu.CompilerParams(vmem_limit_bytes=VMEM_LIMIT_BYTES),
    )(packed, land, silu_c)


def _reduce_big(grads, c_idx):
    names = list(grads)
    got = _exchange_halves([grads[k] for k in names])
    partials = [_add_halves(grads[k], got[i], c_idx, f"reduce_add_{k}") for i, k in enumerate(names)]
    parts = _scatter_partials(partials)
    halves = [_sum_chips(parts[i], f"reduce_sum_{k}") for i, k in enumerate(names)]
    totals = _share_totals(halves)
    return dict(zip(names, totals))


def kernel(x, c, norm_g, ada_w, ada_b, a_w_in, a_conv_w, a_conv_b, a_ln_g, a_ln_b, a_w_out, b_w_in, b_q_norm, b_k_norm, b_w_out, loss_target, m_norm_g, m_ada_w, m_ada_b, m_a_w_in, m_a_conv_w, m_a_conv_b, m_a_ln_g, m_a_ln_b, m_a_w_out, m_b_w_in, m_b_q_norm, m_b_k_norm, m_b_w_out, v_norm_g, v_ada_w, v_ada_b, v_a_w_in, v_a_conv_w, v_a_conv_b, v_a_ln_g, v_a_ln_b, v_a_w_out, v_b_w_in, v_b_q_norm, v_b_k_norm, v_b_w_out):
    chip = 2 * lax.axis_index("x") + lax.axis_index("y")
    core = lax.axis_index("c")
    chip_idx = chip.astype(jnp.int32).reshape(1)
    dev_idx = jnp.stack([2 * chip + core, chip, core]).astype(jnp.int32)

    mods, silu_c, conv_w_full = _ada_forward(c, ada_w, ada_b, a_conv_w[0])
    lands_a = [_cast_into_slot(a_w_in[0], chip_idx, "cast_a_w_in"), _cast_into_slot(a_w_out[0], chip_idx, "cast_a_w_out")]
    send_a, recv_a, lands_a, token_a = _gather_start(lands_a, mods, "gather_start_a")
    lands_b = [_cast_into_slot(b_w_in[0], chip_idx, "cast_b_w_in"), _cast_into_slot(b_w_out[0], chip_idx, "cast_b_w_out")]
    send_b, recv_b, lands_b, token_b = _gather_start(lands_b, token_a, "gather_start_b")
    mods = mods + token_b[0:2, 0:1]

    def weights_a(after):
        send, recv, lands, _ = _gather_forward(send_a, recv_a, lands_a, after, "gather_forward_a")
        w_in, w_out = _gather_wait(send, recv, lands, after, "gather_wait_a")
        return w_in, w_out.reshape(D_MODEL, D_MODEL)

    forwarded_b = []

    def weights_b(after):
        send, recv, lands, _ = forwarded_b
        w_in, w_out = _gather_wait(send, recv, lands, after, "gather_wait_b")
        return w_in, w_out.reshape(D_MODEL, D_MODEL)

    def forward_weights_b(after):
        forwarded_b.extend(_gather_forward(send_b, recv_b, lands_b, after, "gather_forward_b"))
        return forwarded_b[3]

    stage1, stage2 = {}, {}

    def send_grads(tag, dw_in, dw_out):
        grads = [dw_in, dw_out.reshape(N_CHIPS, D_MODEL // N_CHIPS, D_MODEL)]
        send, recv, arrays, token = _reduce_sibling_start(grads, dw_out, f"reduce_d2d_start_{tag}")
        stage1[tag] = (send, recv, arrays)
        return token

    def forward_grads(tag, after):
        send, recv, arrays = stage1[tag]
        grads, got = _reduce_sibling_wait(send, recv, arrays, after, f"reduce_d2d_wait_{tag}")
        partials = [_add_sibling_half(grads[i], got[i], dev_idx, f"reduce_add_{tag}_{i}") for i in range(2)]
        send, recv, arrays, token = _reduce_chips_start(partials, partials[1], f"reduce_ici_start_{tag}")
        stage2[tag] = (send, recv, arrays)
        return token

    stage3 = {}

    def sum_grads(tag, after):
        send, recv, arrays = stage2[tag]
        partials, lands = _reduce_chips_wait(send, recv, arrays, after, f"reduce_ici_wait_{tag}")
        totals = [_sum_partials(lands[i], partials[i], dev_idx, f"reduce_sum_{tag}_{i}") for i in range(2)]
        send, recv, totals, token = _share_halves_start(totals, totals[1], f"reduce_share_start_{tag}")
        stage3[tag] = (send, recv, totals)
        return token

    def finish_grads(tag, after):
        send, recv, totals = stage3[tag]
        return _share_halves_wait(send, recv, totals, after, f"reduce_share_wait_{tag}")

    grad_x, small = _local_step(
        x[0], loss_target[0], mods.reshape(2, 3, D_MODEL), norm_g, conv_w_full, a_conv_b, a_ln_g[0:1],
        a_ln_b[0:1], b_q_norm[0], b_k_norm[0], weights_a, weights_b, forward_weights_b,
        functools.partial(send_grads, "b"), functools.partial(forward_grads, "b"), functools.partial(send_grads, "a"))

    ns = 3 * D_MODEL // N_CHIPS
    pad_mod = lambda dm: jnp.pad(dm.reshape(N_CHIPS, ns), ((0, 0), (0, D_MODEL - ns)))
    packed = jnp.concatenate([
        small["dnorm_g"], small["dconv_b"], small["dln_g"], small["dln_b"], small["dq_norm"], small["dk_norm"],
        small["loss_cols"], pad_mod(small["dmod0"]), pad_mod(small["dmod1"]), small["dconv_w"],
        jnp.zeros((SMALL_ROWS - 20 - CONV_WIDTH, D_MODEL), F32)], axis=0)
    send_s, recv_s, small_arrays, token_s = _small_gather_start(packed, packed)

    given = dict(norm_g=(norm_g, m_norm_g, v_norm_g), ada_w=(ada_w, m_ada_w, v_ada_w), ada_b=(ada_b, m_ada_b, v_ada_b),
                 a_w_in=(a_w_in, m_a_w_in, v_a_w_in), a_conv_w=(a_conv_w, m_a_conv_w, v_a_conv_w),
                 a_conv_b=(a_conv_b, m_a_conv_b, v_a_conv_b), a_ln_g=(a_ln_g, m_a_ln_g, v_a_ln_g),
                 a_ln_b=(a_ln_b, m_a_ln_b, v_a_ln_b), a_w_out=(a_w_out, m_a_w_out, v_a_w_out),
                 b_w_in=(b_w_in, m_b_w_in, v_b_w_in), b_q_norm=(b_q_norm, m_b_q_norm, v_b_q_norm),
                 b_k_norm=(b_k_norm, m_b_k_norm, v_b_k_norm), b_w_out=(b_w_out, m_b_w_out, v_b_w_out))
    order = ["norm_g", "ada_w", "ada_b", "a_w_in", "a_conv_w", "a_conv_b", "a_ln_g", "a_ln_b", "a_w_out", "b_w_in",
             "b_q_norm", "b_k_norm", "b_w_out"]
    outs = {}

    def update(k, g2, after=None, copy_grad=False):
        w, m, v = given[k]
        shape2 = g2.shape
        res = _adamw(w.reshape(shape2), g2, m.reshape(shape2), v.reshape(shape2), f"adamw_{k}", after, copy_grad)
        outs[k] = tuple(a.reshape(w.shape) for a in ((res[3] if copy_grad else g2), res[0], res[1], res[2]))

    token = forward_grads("a", token_s)
    token = sum_grads("b", token)
    packed, land = _small_gather_wait(send_s, recv_s, small_arrays, token)
    tot, g_ada_w, loss, qk = _reduce_small(packed, land, silu_c)
    g_b_in, g_b_out = finish_grads("b", tot)
    update("b_w_in", g_b_in, copy_grad=True)
    update("b_w_out", g_b_out, copy_grad=True)
    token = sum_grads("a", outs["b_w_in"][1])
    cw = D_MODEL // N_CHIPS
    g_small = dict(
        norm_g=tot[0:2], a_conv_b=tot[2:3], a_ln_g=tot[3:4], a_ln_b=tot[4:5],
        b_q_norm=qk[0:3], b_k_norm=qk[3:6],
        ada_b=jnp.stack([tot[12:16, :ns].reshape(3 * D_MODEL), tot[16:20, :ns].reshape(3 * D_MODEL)]),
        a_conv_w=lax.dynamic_slice(tot[20:20 + CONV_WIDTH], (0, chip * cw), (CONV_WIDTH, cw)),
    )
    update("ada_w", g_ada_w.reshape(2 * D_MODEL, ns), after=token)
    for k, g2 in g_small.items():
        update(k, g2, after=token)
    g_a_in, g_a_out = finish_grads("a", outs["ada_w"][1])
    update("a_w_in", g_a_in, copy_grad=True)
    update("a_w_out", g_a_out, copy_grad=True)
    return (loss.reshape(()), grad_x[None], *[outs[k][0] for k in order], *[outs[k][1] for k in order],
            *[outs[k][2] for k in order], *[outs[k][3] for k in order])
```

```python
import functools

import jax
import jax.numpy as jnp
from jax import lax
from jax.experimental import pallas as pl
from jax.experimental.pallas import tpu as pltpu

F32 = jnp.float32
BF16 = jnp.bfloat16

SEQ = 2048
D_MODEL = 1024
CONV_WIDTH = 31
HEAD_DIM = 64
N_HEADS = 16
DILATIONS = (1, 4, 16)
ATTN_BLOCK = 128
NORM_EPS = 1e-6
NEG_INF = -1e30
N_DEV = 8
N_CHIPS = 4

ADAM_LR = 0.001
ADAM_B1 = 0.9
ADAM_B2 = 0.999
ADAM_EPS = 1e-08
ADAM_WD = 0.01
ADAM_STEP = 10

VMEM_LIMIT_BYTES = 52 * 1024 * 1024
HALO = 32
LANES = 128
MESH = pl.DeviceIdType.MESH


def _params(*sem):
    return pltpu.CompilerParams(dimension_semantics=sem or None, vmem_limit_bytes=VMEM_LIMIT_BYTES)


def _sigmoid(v):
    return 1.0 / (1.0 + jnp.exp(-v))


def _row_spec(tm, cols, col_block=0):
    return pl.BlockSpec((tm, cols), lambda i: (i, col_block))


def _vec_spec(rows, cols):
    return pl.BlockSpec((rows, cols), lambda i: (0, 0))


def _normmod(xv, g, scale, shift):
    r = lax.rsqrt(jnp.mean(xv * xv, axis=-1, keepdims=True) + NORM_EPS)
    return xv * r * g * (1.0 + scale) + shift


def _normmod_fwd(x, g, scale, shift, name):
    tm = 256

    def body(x_ref, g_ref, sc_ref, sh_ref, h_ref, ht_ref):
        h = _normmod(x_ref[...], g_ref[...], sc_ref[...], sh_ref[...])
        h_ref[...] = h.astype(BF16)
        ht_ref[...] = h.T.astype(BF16)

    return pl.pallas_call(
        body, name=name, grid=(SEQ // tm,),
        in_specs=[_row_spec(tm, D_MODEL)] + [_vec_spec(1, D_MODEL)] * 3,
        out_specs=[_row_spec(tm, D_MODEL), pl.BlockSpec((D_MODEL, tm), lambda i: (0, i))],
        out_shape=[jax.ShapeDtypeStruct((SEQ, D_MODEL), BF16), jax.ShapeDtypeStruct((D_MODEL, SEQ), BF16)],
        compiler_params=_params("parallel"),
    )(x, g, scale, shift)


def _normmod_bwd(x, g, scale, dh_parts, dres, name, part_dilations=None, gated=None):
    tm = 256
    n_parts = len(dh_parts)
    dils = part_dilations or (1,) * n_parts
    dh_parts = [p if d == 1 else p.reshape(d, SEQ // d, D_MODEL) for p, d in zip(dh_parts, dils)]
    n_gated = 0 if gated is None else 2

    def body(x_ref, g_ref, sc_ref, dres_ref, *rest):
        part_refs = rest[:n_parts]
        gated_refs = rest[n_parts:n_parts + n_gated]
        out_refs = rest[n_parts + n_gated:]
        dx_ref, sums_ref, nat = out_refs[0], out_refs[1], out_refs[-1]
        xv = x_ref[...]
        r = lax.rsqrt(jnp.mean(xv * xv, axis=-1, keepdims=True) + NORM_EPS)
        xn = xv * r
        dh = _load_natural(part_refs[0], nat, dils[0])
        for p, d in zip(part_refs[1:], dils[1:]):
            dh = dh + _load_natural(p, nat, d)
        gv = g_ref[...]
        one_sc = 1.0 + sc_ref[...]
        dxn = dh * (gv * one_sc)
        dx = dres_ref[...] + r * (dxn - xn * jnp.mean(dxn * xn, axis=-1, keepdims=True))
        dx_ref[...] = dx
        dhx = dh * xn
        rows = [jnp.sum(dhx, axis=0, keepdims=True) * one_sc,
                jnp.sum(dhx, axis=0, keepdims=True) * gv,
                jnp.sum(dh, axis=0, keepdims=True)]
        if gated is not None:
            gate_ref, y_ref = gated_refs
            out_refs[2][...] = (dx * gate_ref[...]).astype(BF16)
            rows.append(jnp.sum(dx * y_ref[...], axis=0, keepdims=True))
        sums = jnp.concatenate(rows + [jnp.zeros((8 - len(rows), D_MODEL), F32)], axis=0)

        @pl.when(pl.program_id(0) == 0)
        def _():
            sums_ref[...] = jnp.zeros_like(sums_ref)

        sums_ref[...] += sums

    gated_specs = [] if gated is None else [_vec_spec(1, D_MODEL), _row_spec(tm, D_MODEL)]
    dy_spec = [] if gated is None else [_row_spec(tm, D_MODEL)]
    dy_shape = [] if gated is None else [jax.ShapeDtypeStruct((SEQ, D_MODEL), BF16)]
    return pl.pallas_call(
        body, name=name, grid=(SEQ // tm,),
        in_specs=[_row_spec(tm, D_MODEL), _vec_spec(1, D_MODEL), _vec_spec(1, D_MODEL), _row_spec(tm, D_MODEL)]
        + [_class_spec(tm, d) for d in dils] + gated_specs,
        out_specs=[_row_spec(tm, D_MODEL), _vec_spec(8, D_MODEL)] + dy_spec,
        out_shape=[jax.ShapeDtypeStruct((SEQ, D_MODEL), F32), jax.ShapeDtypeStruct((8, D_MODEL), F32)] + dy_shape,
        scratch_shapes=[_natural_scratch(tm)],
        compiler_params=_params("arbitrary"),
    )(x, g, scale, dres, *dh_parts, *(gated or ()))


def _mm(lhs, rhs, *, tn, tile0, n_tiles, out_dtype, name, out3d=None, prev=None, transpose_lhs=False):
    mo, kc = lhs.shape[::-1] if transpose_lhs else lhs.shape
    cm = min(mo, 1024)
    tc = 256

    def body(l_ref, r_ref, *rest):
        if transpose_lhs:
            o_ref, lt_ref = rest[-2], rest[-1]

            @pl.when(pl.program_id(0) == 0)
            def _():
                for c in range(kc // tc):
                    lt_ref[:, c * tc:(c + 1) * tc] = l_ref[c * tc:(c + 1) * tc, :].astype(F32).T.astype(l_ref.dtype)
        else:
            o_ref, lt_ref = rest[-1], l_ref
        for m in range(mo // cm):
            rows = pl.ds(m * cm, cm)
            o_ref[rows, :] = jnp.dot(lt_ref[rows, :], r_ref[...], preferred_element_type=F32).astype(out_dtype)

    if rhs.ndim == 3:
        tps_r = rhs.shape[2] // tn
        r_spec = pl.BlockSpec((None, kc, tn), lambda t: ((tile0 + t) // tps_r, 0, (tile0 + t) % tps_r))
    else:
        r_spec = pl.BlockSpec((kc, tn), lambda t: (0, t))
    in_specs = [pl.BlockSpec(lhs.shape, lambda t: (0, 0)), r_spec]
    args = [lhs, rhs]
    aliases = {}
    if out3d is None:
        o_spec = pl.BlockSpec((mo, tn), lambda t: (0, t))
        o_shape = jax.ShapeDtypeStruct((mo, n_tiles * tn), out_dtype)
    else:
        j_out, ns_out = out3d
        tps_o = ns_out // tn
        o_spec = pl.BlockSpec((None, mo, tn), lambda t: ((tile0 + t) // tps_o, 0, (tile0 + t) % tps_o))
        o_shape = jax.ShapeDtypeStruct((j_out, mo, ns_out), out_dtype)
        if prev is not None:
            in_specs.append(pl.BlockSpec(memory_space=pl.ANY))
            args.append(prev)
            aliases = {2: 0}
    return pl.pallas_call(
        body, name=name, grid=(n_tiles,), in_specs=in_specs, out_specs=o_spec, out_shape=o_shape,
        input_output_aliases=aliases,
        scratch_shapes=[pltpu.VMEM((mo, kc), lhs.dtype)] if transpose_lhs else [],
        compiler_params=_params("arbitrary" if transpose_lhs else "parallel"),
    )(*args)


def _mm_nt(dy, w3, *, tn, tile0, n_tiles, name, after=None):
    m_rows = dy.shape[0]
    _, kc, ns = w3.shape
    tps = ns // tn
    cm = 512
    extra = [] if after is None else [after]

    def body(dy_ref, w_ref, *rest):
        o_ref = rest[-1]

        @pl.when(pl.program_id(0) == 0)
        def _():
            o_ref[...] = jnp.zeros_like(o_ref)

        for m in range(m_rows // cm):
            rows = pl.ds(m * cm, cm)
            o_ref[rows, :] += lax.dot_general(dy_ref[rows, :], w_ref[...], (((1,), (1,)), ((), ())),
                                              preferred_element_type=F32)

    return pl.pallas_call(
        body, name=name, grid=(n_tiles,),
        in_specs=[pl.BlockSpec((m_rows, tn), lambda t: (0, t)),
                  pl.BlockSpec((None, kc, tn), lambda t: ((tile0 + t) // tps, 0, (tile0 + t) % tps))]
        + [pl.BlockSpec(memory_space=pl.ANY)] * len(extra),
        out_specs=pl.BlockSpec((m_rows, kc), lambda t: (0, 0)),
        out_shape=jax.ShapeDtypeStruct((m_rows, kc), F32),
        compiler_params=_params("arbitrary"),
    )(dy, w3, *extra)


CONV_CHUNK = 16


def _shift_copies(buf, shifted):
    rows = shifted.shape[1]
    for s in range(1, 8):
        shifted[s - 1] = buf[pl.ds(s, rows), :]


def _shifted_rows(buf, shifted, offset, r0):
    s = offset % 8
    if s == 0:
        return buf[pl.ds(r0 + offset, CONV_CHUNK), :]
    return shifted[s - 1, pl.ds(r0 + (offset - s), CONV_CHUNK), :]


def _spread_taps(w_ref, taps):
    for k in range(CONV_WIDTH):
        taps[k] = jnp.broadcast_to(w_ref[k:k + 1, :], (8, D_MODEL))


def _times_tap(taps, k, rows):
    return (rows.reshape(CONV_CHUNK // 8, 8, D_MODEL) * taps[k][None]).reshape(CONV_CHUNK, D_MODEL)


def _conv_fwd(proj, conv_w, conv_b, ln_g, ln_b, name):
    tm = 256
    hb = tm // HALO

    def body(vg_ref, halo_ref, z_ref, w_ref, b_ref, g_ref, be_ref, u5_ref, u5t_ref, u2_ref, buf, shifted, taps):
        i = pl.program_id(0)
        u1 = vg_ref[:, :D_MODEL] * _sigmoid(vg_ref[:, D_MODEL:])
        u1h = halo_ref[:, :D_MODEL] * _sigmoid(halo_ref[:, D_MODEL:])
        buf[pl.ds(0, HALO), :] = jnp.where(i > 0, u1h, 0.0)
        buf[pl.ds(HALO, tm), :] = u1
        _shift_copies(buf, shifted)
        _spread_taps(w_ref, taps)

        def chunk(ci, carry):
            r0 = pl.multiple_of(ci * CONV_CHUNK, CONV_CHUNK)
            acc = jnp.broadcast_to(b_ref[...], (CONV_CHUNK, D_MODEL))
            for k in range(CONV_WIDTH):
                acc = acc + _times_tap(taps, k, _shifted_rows(buf, shifted, HALO - (CONV_WIDTH - 1) + k, r0))
            u2_ref[pl.ds(r0, CONV_CHUNK), :] = acc
            return carry

        lax.fori_loop(0, tm // CONV_CHUNK, chunk, 0)
        acc = u2_ref[...]
        mu = jnp.mean(acc, axis=-1, keepdims=True)
        xc = acc - mu
        rstd = lax.rsqrt(jnp.mean(xc * xc, axis=-1, keepdims=True) + NORM_EPS)
        u3 = xc * rstd * g_ref[...] + be_ref[...]
        zv = z_ref[...]
        u5 = u3 * _sigmoid(u3) * (zv * _sigmoid(zv))
        u5_ref[...] = u5.astype(BF16)
        u5t_ref[...] = u5.T.astype(BF16)

    return pl.pallas_call(
        body, name=name, grid=(SEQ // tm,),
        in_specs=[pl.BlockSpec((tm, 2 * D_MODEL), lambda i: (i, 0)),
                  pl.BlockSpec((HALO, 2 * D_MODEL), lambda i: (jnp.maximum(i * hb - 1, 0), 0)),
                  _row_spec(tm, D_MODEL, 2),
                  _vec_spec(CONV_WIDTH, D_MODEL)] + [_vec_spec(1, D_MODEL)] * 3,
        out_specs=[_row_spec(tm, D_MODEL), pl.BlockSpec((D_MODEL, tm), lambda i: (0, i)), _row_spec(tm, D_MODEL)],
        out_shape=[jax.ShapeDtypeStruct((SEQ, D_MODEL), BF16), jax.ShapeDtypeStruct((D_MODEL, SEQ), BF16),
                   jax.ShapeDtypeStruct((SEQ, D_MODEL), F32)],
        scratch_shapes=[pltpu.VMEM((HALO + tm, D_MODEL), F32), pltpu.VMEM((7, HALO + tm - 8, D_MODEL), F32),
                        pltpu.VMEM((CONV_WIDTH, 8, D_MODEL), F32)],
        compiler_params=_params("parallel"),
    )(proj, proj, proj, conv_w, conv_b, ln_g, ln_b)


def _conv_bwd_pointwise(dy, w_out, proj, u2, ln_g, ln_b, name):
    tm = 256

    def body(dy_ref, w_ref, z_ref, u2_ref, g_ref, be_ref, du2_ref, dz_ref, sums_ref):
        u2v = u2_ref[...]
        mu = jnp.mean(u2v, axis=-1, keepdims=True)
        xc = u2v - mu
        rstd = lax.rsqrt(jnp.mean(xc * xc, axis=-1, keepdims=True) + NORM_EPS)
        xhat = xc * rstd
        u3 = xhat * g_ref[...] + be_ref[...]
        s3 = _sigmoid(u3)
        u4 = u3 * s3
        zv = z_ref[...]
        sz = _sigmoid(zv)
        du5v = lax.dot_general(dy_ref[...], w_ref[...], NT_DIMS, preferred_element_type=F32)
        dz_ref[...] = du5v * u4 * (sz * (1.0 + zv * (1.0 - sz)))
        du3 = du5v * (zv * sz) * (s3 * (1.0 + u3 * (1.0 - s3)))
        dxhat = du3 * g_ref[...]
        du2 = rstd * (dxhat - jnp.mean(dxhat, axis=-1, keepdims=True)
                      - xhat * jnp.mean(dxhat * xhat, axis=-1, keepdims=True))
        du2_ref[...] = du2
        sums = jnp.concatenate([
            jnp.sum(du3 * xhat, axis=0, keepdims=True),
            jnp.sum(du3, axis=0, keepdims=True),
            jnp.sum(du2, axis=0, keepdims=True),
            jnp.zeros((5, D_MODEL), F32)], axis=0)

        @pl.when(pl.program_id(0) == 0)
        def _():
            sums_ref[...] = jnp.zeros_like(sums_ref)

        sums_ref[...] += sums

    return pl.pallas_call(
        body, name=name, grid=(SEQ // tm,),
        in_specs=[_row_spec(tm, D_MODEL), _vec_spec(D_MODEL, D_MODEL), _row_spec(tm, D_MODEL, 2),
                  _row_spec(tm, D_MODEL), _vec_spec(1, D_MODEL), _vec_spec(1, D_MODEL)],
        out_specs=[_row_spec(tm, D_MODEL), _row_spec(tm, D_MODEL), _vec_spec(8, D_MODEL)],
        out_shape=[jax.ShapeDtypeStruct((SEQ, D_MODEL), F32), jax.ShapeDtypeStruct((SEQ, D_MODEL), F32),
                   jax.ShapeDtypeStruct((8, D_MODEL), F32)],
        compiler_params=_params("arbitrary"),
    )(dy, w_out, proj, u2, ln_g, ln_b)


def _conv_bwd_taps(du2, dz, proj, conv_w, name):
    tm = 256
    hb = tm // HALO
    n_blocks = SEQ // tm

    def body(du2_ref, dnext_ref, dz_ref, vg_ref, w_ref, dproj_ref, dw_ref, dbuf, dshift, sgbuf, ubuf, dwacc, taps):
        i = pl.program_id(0)
        _spread_taps(w_ref, taps)
        sg = _sigmoid(vg_ref[:, D_MODEL:])
        sgbuf[...] = sg
        ubuf[...] = vg_ref[:, :D_MODEL] * sg
        dbuf[pl.ds(0, tm), :] = du2_ref[...]
        dbuf[pl.ds(tm, HALO), :] = jnp.where(i < n_blocks - 1, dnext_ref[...], 0.0)
        _shift_copies(dbuf, dshift)

        @pl.when(i == 0)
        def _():
            dwacc[...] = jnp.zeros_like(dwacc)

        def chunk(ci, carry):
            r0 = pl.multiple_of(ci * CONV_CHUNK, CONV_CHUNK)
            rows = pl.ds(r0, CONV_CHUNK)
            u1c = ubuf[rows, :]
            du1 = jnp.zeros((CONV_CHUNK, D_MODEL), F32)
            for k in range(CONV_WIDTH):
                ahead = _shifted_rows(dbuf, dshift, CONV_WIDTH - 1 - k, r0)
                du1 = du1 + _times_tap(taps, k, ahead)
                prod = u1c * ahead
                dwacc[k] += prod[0:8] + prod[8:16]
            sgc = sgbuf[rows, :]
            dval = du1 * sgc
            dproj_ref[rows, 0:D_MODEL] = dval.astype(BF16)
            dproj_ref[rows, D_MODEL:2 * D_MODEL] = (dval * vg_ref[rows, 0:D_MODEL] * (1.0 - sgc)).astype(BF16)
            return carry

        lax.fori_loop(0, tm // CONV_CHUNK, chunk, 0)
        dproj_ref[:, 2 * D_MODEL:] = dz_ref[...].astype(BF16)

        @pl.when(i == n_blocks - 1)
        def _():
            for k in range(CONV_WIDTH):
                dw_ref[k:k + 1, :] = jnp.sum(dwacc[k], axis=0, keepdims=True)
            dw_ref[CONV_WIDTH:, :] = jnp.zeros((32 - CONV_WIDTH, D_MODEL), F32)

    return pl.pallas_call(
        body, name=name, grid=(n_blocks,),
        in_specs=[_row_spec(tm, D_MODEL),
                  pl.BlockSpec((HALO, D_MODEL), lambda i: (jnp.minimum((i + 1) * hb, SEQ // HALO - 1), 0)),
                  _row_spec(tm, D_MODEL),
                  pl.BlockSpec((tm, 2 * D_MODEL), lambda i: (i, 0)),
                  _vec_spec(CONV_WIDTH, D_MODEL)],
        out_specs=[_row_spec(tm, 3 * D_MODEL), _vec_spec(32, D_MODEL)],
        out_shape=[jax.ShapeDtypeStruct((SEQ, 3 * D_MODEL), BF16), jax.ShapeDtypeStruct((32, D_MODEL), F32)],
        scratch_shapes=[pltpu.VMEM((tm + HALO, D_MODEL), F32), pltpu.VMEM((7, HALO + tm - 8, D_MODEL), F32),
                        pltpu.VMEM((tm, D_MODEL), F32), pltpu.VMEM((tm, D_MODEL), F32),
                        pltpu.VMEM((CONV_WIDTH, 8, D_MODEL), F32), pltpu.VMEM((CONV_WIDTH, 8, D_MODEL), F32)],
        compiler_params=_params("arbitrary"),
    )(du2, du2, dz, proj, conv_w)


def _out_a(u5, w_out, x, gate, g1, scale1, shift1, name):
    tm = 256
    n_d = len(DILATIONS)

    def body(u_ref, w_ref, x_ref, gate_ref, g_ref, sc_ref, sh_ref, x1_ref, y_ref, ht_ref, *rest):
        h_refs, nat = rest[:n_d], rest[-1]
        y = jnp.dot(u_ref[...], w_ref[...], preferred_element_type=F32)
        x1 = x_ref[...] + gate_ref[...] * y
        y_ref[...] = y
        x1_ref[...] = x1
        h = _normmod(x1, g_ref[...], sc_ref[...], sh_ref[...])
        ht_ref[...] = h.T.astype(BF16)
        for h_ref, d in zip(h_refs, DILATIONS):
            _store_classes(h_ref, h, nat, d)

    res = pl.pallas_call(
        body, name=name, grid=(SEQ // tm,),
        in_specs=[_row_spec(tm, D_MODEL), _vec_spec(D_MODEL, D_MODEL), _row_spec(tm, D_MODEL)]
        + [_vec_spec(1, D_MODEL)] * 4,
        out_specs=[_row_spec(tm, D_MODEL), _row_spec(tm, D_MODEL), pl.BlockSpec((D_MODEL, tm), lambda i: (0, i))]
        + [_class_spec(tm, d) for d in DILATIONS],
        out_shape=[jax.ShapeDtypeStruct((SEQ, D_MODEL), F32), jax.ShapeDtypeStruct((SEQ, D_MODEL), F32),
                   jax.ShapeDtypeStruct((D_MODEL, SEQ), BF16)] + [_class_shape(d, BF16) for d in DILATIONS],
        scratch_shapes=[_natural_scratch(tm)],
        compiler_params=_params("parallel"),
    )(u5, w_out, x, gate, g1, scale1, shift1)
    return res[0], res[1], res[2], [a.reshape(SEQ, D_MODEL) for a in res[3:]]


def _out_b_loss(u, w_out, x1, gate, target, name):
    tm = 256

    def body(u_ref, w_ref, x_ref, gate_ref, t_ref, e_ref, dy_ref, sums_ref):
        y = jnp.dot(u_ref[...], w_ref[...], preferred_element_type=F32)
        diff = x_ref[...] + gate_ref[...] * y - t_ref[...]
        e = diff * (1.0 / D_MODEL)
        e_ref[...] = e
        dy_ref[...] = (e * gate_ref[...]).astype(BF16)
        sums = jnp.concatenate([
            jnp.sum(e * y, axis=0, keepdims=True),
            jnp.sum(diff * diff, axis=0, keepdims=True),
            jnp.zeros((6, D_MODEL), F32)], axis=0)

        @pl.when(pl.program_id(0) == 0)
        def _():
            sums_ref[...] = jnp.zeros_like(sums_ref)

        sums_ref[...] += sums

    return pl.pallas_call(
        body, name=name, grid=(SEQ // tm,),
        in_specs=[_row_spec(tm, D_MODEL), _vec_spec(D_MODEL, D_MODEL), _row_spec(tm, D_MODEL),
                  _vec_spec(1, D_MODEL), _row_spec(tm, D_MODEL)],
        out_specs=[_row_spec(tm, D_MODEL), _row_spec(tm, D_MODEL), _vec_spec(8, D_MODEL)],
        out_shape=[jax.ShapeDtypeStruct((SEQ, D_MODEL), F32), jax.ShapeDtypeStruct((SEQ, D_MODEL), BF16),
                   jax.ShapeDtypeStruct((8, D_MODEL), F32)],
        compiler_params=_params("arbitrary"),
    )(u, w_out, x1, gate, target)


def _dgate_dy(dx1, y, gate, name):
    tm = 256

    def body(d_ref, y_ref, gate_ref, dy_ref, sums_ref):
        dv = d_ref[...]
        dy_ref[...] = (dv * gate_ref[...]).astype(BF16)
        sums = jnp.concatenate([jnp.sum(dv * y_ref[...], axis=0, keepdims=True), jnp.zeros((7, D_MODEL), F32)], axis=0)

        @pl.when(pl.program_id(0) == 0)
        def _():
            sums_ref[...] = jnp.zeros_like(sums_ref)

        sums_ref[...] += sums

    return pl.pallas_call(
        body, name=name, grid=(SEQ // tm,),
        in_specs=[_row_spec(tm, D_MODEL), _row_spec(tm, D_MODEL), _vec_spec(1, D_MODEL)],
        out_specs=[_row_spec(tm, D_MODEL), _vec_spec(8, D_MODEL)],
        out_shape=[jax.ShapeDtypeStruct((SEQ, D_MODEL), BF16), jax.ShapeDtypeStruct((8, D_MODEL), F32)],
        compiler_params=_params("arbitrary"),
    )(dx1, y, gate)


def _mm_nt_res(dy, w, name):
    tm = 256
    kc, n = w.shape

    def body(dy_ref, w_ref, o_ref):
        o_ref[...] = lax.dot_general(dy_ref[...], w_ref[...], (((1,), (1,)), ((), ())), preferred_element_type=F32)

    return pl.pallas_call(
        body, name=name, grid=(SEQ // tm,),
        in_specs=[_row_spec(tm, n), _vec_spec(kc, n)],
        out_specs=_row_spec(tm, kc),
        out_shape=jax.ShapeDtypeStruct((SEQ, kc), F32),
        compiler_params=_params("parallel"),
    )(dy, w)


def _seg_matrix():
    r = lax.broadcasted_iota(jnp.int32, (256, 256), 0) // HEAD_DIM
    c = lax.broadcasted_iota(jnp.int32, (256, 256), 1) // HEAD_DIM
    return (r == c).astype(BF16)


def _segsum(v, seg):
    hi = v.astype(BF16)
    lo = (v - hi.astype(F32)).astype(BF16)
    outs = []
    for c0 in range(0, D_MODEL, 256):
        outs.append(jnp.dot(hi[:, c0:c0 + 256], seg, preferred_element_type=F32)
                    + jnp.dot(lo[:, c0:c0 + 256], seg, preferred_element_type=F32))
    return jnp.concatenate(outs, axis=1)


def _qk_rstd(v, seg):
    return lax.rsqrt(_segsum(v * v, seg) * (1.0 / HEAD_DIM) + NORM_EPS)


def _qknorm_fwd(proj, qw, kw, seg, name):
    tm = 256

    def body(p_ref, qw_ref, kw_ref, seg_ref, q_ref, k_ref):
        segv = seg_ref[...]
        q = p_ref[:, :D_MODEL].astype(F32)
        k = p_ref[:, D_MODEL:].astype(F32)
        q_ref[...] = (q * _qk_rstd(q, segv) * qw_ref[...]).astype(BF16)
        k_ref[...] = (k * _qk_rstd(k, segv) * kw_ref[...]).astype(BF16)

    return pl.pallas_call(
        body, name=name, grid=(SEQ // tm,),
        in_specs=[_row_spec(tm, 2 * D_MODEL), _vec_spec(1, D_MODEL), _vec_spec(1, D_MODEL), _vec_spec(256, 256)],
        out_specs=[_row_spec(tm, D_MODEL)] * 2,
        out_shape=[jax.ShapeDtypeStruct((SEQ, D_MODEL), BF16)] * 2,
        compiler_params=_params("parallel"),
    )(proj, qw, kw, seg)


def _attn_masks(b, bpc, dilation, slope):
    if bpc == 1:
        qi = lax.broadcasted_iota(jnp.int32, (ATTN_BLOCK, ATTN_BLOCK), 0)
        kj = lax.broadcasted_iota(jnp.int32, (ATTN_BLOCK, ATTN_BLOCK), 1)
        steps = qi - kj
        return (steps * dilation).astype(F32), steps >= 0
    qi = lax.broadcasted_iota(jnp.int32, (ATTN_BLOCK, 2 * ATTN_BLOCK), 0)
    kj = lax.broadcasted_iota(jnp.int32, (ATTN_BLOCK, 2 * ATTN_BLOCK), 1)
    steps = qi + ATTN_BLOCK - kj
    has_prev = (b % bpc) != 0
    valid = (steps >= 0) & (steps <= ATTN_BLOCK) & (has_prev | (kj >= ATTN_BLOCK))
    return (steps * dilation).astype(F32), valid


def _key_tile(prev_ref, cur_ref, cols, bpc):
    if bpc == 1:
        return cur_ref[:, cols]
    return jnp.concatenate([prev_ref[:, cols], cur_ref[:, cols]], axis=0)


ATTN_HEADS_FWD = 16
ATTN_HEADS_BWD = 16
NT_DIMS = (((1,), (1,)), ((), ()))
TN_DIMS = (((0,), (0,)), ((), ()))
BATCH_NT_DIMS = (((2,), (2,)), ((0,), (0,)))
BATCH_NN_DIMS = (((2,), (1,)), ((0,), (0,)))
BATCH_TN_DIMS = (((1,), (1,)), ((0,), (0,)))


def _head_stack(tile_of, heads):
    return jnp.stack([tile_of(slice(h * HEAD_DIM, (h + 1) * HEAD_DIM)) for h in range(heads)], axis=0)


def _attn_specs(heads, segment=0):
    width = heads * HEAD_DIM
    off = segment * (D_MODEL // width)
    last = SEQ // ATTN_BLOCK - 1
    cur = pl.BlockSpec((ATTN_BLOCK, width), lambda hg, b: (jnp.minimum(b, last), hg + off))
    prev = pl.BlockSpec((ATTN_BLOCK, width), lambda hg, b: (jnp.clip(b - 1, 0, last), hg + off))
    return cur, prev


def _attn_fwd(q, k, proj, slopes, dilation, name):
    bpc = SEQ // dilation // ATTN_BLOCK
    heads = ATTN_HEADS_FWD
    assert heads == N_HEADS
    cur, prev = _attn_specs(heads)
    v_cur, v_prev = _attn_specs(heads, segment=2)
    scale = HEAD_DIM ** -0.5

    def body(sl_ref, q_ref, kp_ref, kc_ref, vp_ref, vc_ref, o_ref, lse_ref):
        dist, valid = _attn_masks(pl.program_id(1), bpc, dilation, None)
        q3 = _head_stack(lambda cols: q_ref[:, cols], heads)
        k3 = _head_stack(lambda cols: _key_tile(kp_ref, kc_ref, cols, bpc), heads)
        v3 = _head_stack(lambda cols: _key_tile(vp_ref, vc_ref, cols, bpc), heads)
        s = lax.dot_general(q3, k3, BATCH_NT_DIMS, preferred_element_type=F32)
        s = jnp.where(valid[None], s * scale - dist[None] * sl_ref[...], NEG_INF)
        m = jnp.max(s, axis=-1, keepdims=True)
        p = jnp.exp(s - m)
        l = jnp.sum(p, axis=-1, keepdims=True)
        o3 = lax.dot_general(p.astype(BF16), v3, BATCH_NN_DIMS, preferred_element_type=F32) / l
        lse3 = m + jnp.log(l)
        for h in range(heads):
            o_ref[:, h * HEAD_DIM:(h + 1) * HEAD_DIM] = o3[h]
        lse_ref[...] = jnp.concatenate([lse3[h] for h in range(heads)]
                                       + [jnp.zeros((ATTN_BLOCK, LANES - heads), F32)], axis=1)

    return pl.pallas_call(
        body, name=name, grid=(N_HEADS // heads, SEQ // ATTN_BLOCK),
        in_specs=[pl.BlockSpec((heads, 1, 1), lambda hg, b: (hg, 0, 0)), cur, prev, cur, v_prev, v_cur],
        out_specs=[cur, pl.BlockSpec((ATTN_BLOCK, LANES), lambda hg, b: (b, 0))],
        out_shape=[jax.ShapeDtypeStruct((SEQ, D_MODEL), F32), jax.ShapeDtypeStruct((SEQ, LANES), F32)],
        compiler_params=_params("parallel", "parallel"),
    )(slopes.reshape(N_HEADS, 1, 1), q, k, k, proj, proj)


def _class_spec(tm, dilation, width=D_MODEL):
    if dilation == 1:
        return _row_spec(tm, width)
    return pl.BlockSpec((dilation, tm // dilation, width), lambda i: (0, i, 0))


def _class_shape(dilation, dtype, width=D_MODEL):
    if dilation == 1:
        return jax.ShapeDtypeStruct((SEQ, width), dtype)
    return jax.ShapeDtypeStruct((dilation, SEQ // dilation, width), dtype)


def _load_natural(in_ref, nat_ref, dilation):
    if dilation == 1:
        return in_ref[...].astype(F32)
    n = nat_ref.shape[1] // dilation
    tiles = in_ref.shape[-1] // LANES
    for r in range(dilation):
        for j in range(tiles):
            nat_ref.at[j][pl.ds(r, n, stride=dilation), :] = in_ref[r, :, j * LANES:(j + 1) * LANES].astype(F32)
    if tiles == 1:
        return nat_ref[0]
    return jnp.concatenate([nat_ref[j] for j in range(tiles)], axis=1)


def _store_classes(out_ref, value, nat_ref, dilation):
    if dilation == 1:
        out_ref[...] = value.astype(out_ref.dtype)
        return
    n = nat_ref.shape[1] // dilation
    tiles = value.shape[-1] // LANES
    for j in range(tiles):
        nat_ref[j] = value[:, j * LANES:(j + 1) * LANES]
    for r in range(dilation):
        for j in range(tiles):
            out_ref[r, :, j * LANES:(j + 1) * LANES] = (
                nat_ref.at[j][pl.ds(r, n, stride=dilation), :].astype(out_ref.dtype))


def _natural_scratch(tm):
    return pltpu.VMEM((D_MODEL // LANES, tm, LANES), F32)


def _head_selector():
    lane_head = lax.broadcasted_iota(jnp.int32, (D_MODEL, LANES), 0) // HEAD_DIM
    head = lax.broadcasted_iota(jnp.int32, (D_MODEL, LANES), 1)
    return (lane_head == head).astype(BF16)


def _dot_split(v, m01, dims):
    hi = v.astype(BF16)
    lo = (v - hi.astype(F32)).astype(BF16)
    return (lax.dot_general(hi, m01, dims, preferred_element_type=F32)
            + lax.dot_general(lo, m01, dims, preferred_element_type=F32))


def _merge_fwd(o_parts, lse_parts, z, sel, name):
    tm = 256
    h_spec = pl.BlockSpec((tm, LANES), lambda i: (i, 0))

    def body(o0, o1, o2, l0, l1, l2, z_ref, sel_ref, u_ref, ut_ref, o_ref, lse_ref, nat):
        ls = [_load_natural(l, nat, d) for l, d in zip((l0, l1, l2), DILATIONS)]
        m = jnp.maximum(jnp.maximum(ls[0], ls[1]), ls[2])
        tot = m + jnp.log(jnp.exp(ls[0] - m) + jnp.exp(ls[1] - m) + jnp.exp(ls[2] - m))
        o = jnp.zeros((tm, D_MODEL), F32)
        for o_in, l, d in zip((o0, o1, o2), ls, DILATIONS):
            weight = _dot_split(jnp.exp(l - tot), sel_ref[...], NT_DIMS)
            o = o + weight * _load_natural(o_in, nat, d)
        zv = z_ref[...]
        u = o * (zv * _sigmoid(zv))
        u_ref[...] = u.astype(BF16)
        ut_ref[...] = u.T.astype(BF16)
        o_ref[...] = o
        lse_ref[...] = tot

    return pl.pallas_call(
        body, name=name, grid=(SEQ // tm,),
        in_specs=[_class_spec(tm, d) for d in DILATIONS] + [_class_spec(tm, d, LANES) for d in DILATIONS]
        + [_row_spec(tm, D_MODEL), _vec_spec(D_MODEL, LANES)],
        out_specs=[_row_spec(tm, D_MODEL), pl.BlockSpec((D_MODEL, tm), lambda i: (0, i)),
                   _row_spec(tm, D_MODEL), h_spec],
        out_shape=[jax.ShapeDtypeStruct((SEQ, D_MODEL), BF16), jax.ShapeDtypeStruct((D_MODEL, SEQ), BF16),
                   jax.ShapeDtypeStruct((SEQ, D_MODEL), F32), jax.ShapeDtypeStruct((SEQ, LANES), F32)],
        scratch_shapes=[_natural_scratch(tm)],
        compiler_params=_params("parallel"),
    )(*o_parts, *lse_parts, z, sel)


def _merge_bwd(dy, w_out, o, lse, z, sel, name):
    tm = 256
    n_d = len(DILATIONS)

    def body(dy_ref, w_ref, o_ref, lse_ref, z_ref, sel_ref, dz_ref, *rest):
        do_refs, delta_refs, lse_refs, nat = rest[:n_d], rest[n_d:2 * n_d], rest[2 * n_d:3 * n_d], rest[-1]
        zv = z_ref[...]
        sz = _sigmoid(zv)
        duv = lax.dot_general(dy_ref[...], w_ref[...], NT_DIMS, preferred_element_type=F32)
        ov = o_ref[...]
        do = duv * (zv * sz)
        dz_ref[...] = (duv * ov * (sz * (1.0 + zv * (1.0 - sz)))).astype(BF16)
        delta = _dot_split(do * ov, sel_ref[...], (((1,), (0,)), ((), ())))
        lv = lse_ref[...]
        for i, d in enumerate(DILATIONS):
            _store_classes(do_refs[i], do, nat, d)
            _store_classes(delta_refs[i], delta, nat, d)
            _store_classes(lse_refs[i], lv, nat, d)

    res = pl.pallas_call(
        body, name=name, grid=(SEQ // tm,),
        in_specs=[_row_spec(tm, D_MODEL), _vec_spec(D_MODEL, D_MODEL), _row_spec(tm, D_MODEL), _row_spec(tm, LANES),
                  _row_spec(tm, D_MODEL), _vec_spec(D_MODEL, LANES)],
        out_specs=[_row_spec(tm, D_MODEL)] + [_class_spec(tm, d) for d in DILATIONS]
        + [_class_spec(tm, d, LANES) for d in DILATIONS] * 2,
        out_shape=[jax.ShapeDtypeStruct((SEQ, D_MODEL), BF16)] + [_class_shape(d, BF16) for d in DILATIONS]
        + [_class_shape(d, F32, LANES) for d in DILATIONS] * 2,
        scratch_shapes=[_natural_scratch(tm)],
        compiler_params=_params("parallel"),
    )(dy, w_out, o, lse, z, sel)
    flat = lambda a: a.reshape(SEQ, a.shape[-1])
    return (res[0], [flat(a) for a in res[1:1 + n_d]], [flat(a) for a in res[1 + n_d:1 + 2 * n_d]],
            [flat(a) for a in res[1 + 2 * n_d:]])


def _attn_bwd(q, k, proj, do, lse, delta, slopes, dilation, name):
    bpc = SEQ // dilation // ATTN_BLOCK
    heads = ATTN_HEADS_BWD
    n_blocks = SEQ // ATTN_BLOCK
    carry = bpc > 1
    width = heads * HEAD_DIM
    cur, prev = _attn_specs(heads)
    v_cur, v_prev = _attn_specs(heads, segment=2)
    assert heads == N_HEADS
    per_head = pl.BlockSpec((ATTN_BLOCK, LANES), lambda hg, b: (jnp.minimum(b, n_blocks - 1), 0))
    scale = HEAD_DIM ** -0.5

    def body(sl_ref, q_ref, kp_ref, kc_ref, vp_ref, vc_ref, do_ref, lse_ref, dl_ref,
             dq_ref, dk_ref, dv_ref, *scratch):
        b = pl.program_id(1)
        if carry:
            dk_carry, dv_carry = scratch

            @pl.when(b == n_blocks)
            def _():
                dk_ref[...] = dk_carry[...].astype(BF16)
                dv_ref[...] = dv_carry[...].astype(BF16)

            @pl.when(b < n_blocks)
            def _():
                step(sl_ref, q_ref, kp_ref, kc_ref, vp_ref, vc_ref, do_ref, lse_ref, dl_ref,
                     dq_ref, dk_ref, dv_ref, dk_carry, dv_carry, b)
        else:
            step(sl_ref, q_ref, kp_ref, kc_ref, vp_ref, vc_ref, do_ref, lse_ref, dl_ref,
                 dq_ref, dk_ref, dv_ref, None, None, b)

    def step(sl_ref, q_ref, kp_ref, kc_ref, vp_ref, vc_ref, do_ref, lse_ref, dl_ref,
             dq_ref, dk_ref, dv_ref, dk_carry, dv_carry, b):
        if carry:
            @pl.when(b == 0)
            def _():
                dk_carry[...] = jnp.zeros_like(dk_carry)
                dv_carry[...] = jnp.zeros_like(dv_carry)

        dist, valid = _attn_masks(b, bpc, dilation, None)
        q3 = _head_stack(lambda cols: q_ref[:, cols], heads)
        k3 = _head_stack(lambda cols: _key_tile(kp_ref, kc_ref, cols, bpc), heads)
        v3 = _head_stack(lambda cols: _key_tile(vp_ref, vc_ref, cols, bpc), heads)
        do3 = _head_stack(lambda cols: do_ref[:, cols], heads)
        lse3 = jnp.stack([lse_ref[:, h:h + 1] for h in range(heads)], axis=0)
        dl3 = jnp.stack([dl_ref[:, h:h + 1] for h in range(heads)], axis=0)
        s = lax.dot_general(q3, k3, BATCH_NT_DIMS, preferred_element_type=F32)
        p = jnp.exp(jnp.where(valid[None], s * scale - dist[None] * sl_ref[...], NEG_INF) - lse3)
        dp = lax.dot_general(do3, v3, BATCH_NT_DIMS, preferred_element_type=F32)
        ds = (p * (dp - dl3) * scale).astype(BF16)
        dq3 = lax.dot_general(ds, k3, BATCH_NN_DIMS, preferred_element_type=F32)
        dk3 = lax.dot_general(ds, q3, BATCH_TN_DIMS, preferred_element_type=F32)
        dv3 = lax.dot_general(p.astype(BF16), do3, BATCH_TN_DIMS, preferred_element_type=F32)
        for h in range(heads):
            cols = slice(h * HEAD_DIM, (h + 1) * HEAD_DIM)
            dq_ref[:, cols] = dq3[h].astype(BF16)
            if carry:
                dk_ref[:, cols] = (dk_carry[:, cols] + dk3[h, :ATTN_BLOCK]).astype(BF16)
                dv_ref[:, cols] = (dv_carry[:, cols] + dv3[h, :ATTN_BLOCK]).astype(BF16)
                dk_carry[:, cols] = dk3[h, ATTN_BLOCK:]
                dv_carry[:, cols] = dv3[h, ATTN_BLOCK:]
            else:
                dk_ref[:, cols] = dk3[h].astype(BF16)
                dv_ref[:, cols] = dv3[h].astype(BF16)

    kv_out = prev if carry else cur
    return pl.pallas_call(
        body, name=name, grid=(N_HEADS // heads, n_blocks + (1 if carry else 0)),
        in_specs=[pl.BlockSpec((heads, 1, 1), lambda hg, b: (hg, 0, 0)), cur, prev, cur, v_prev, v_cur,
                  cur, per_head, per_head],
        out_specs=[cur, kv_out, kv_out],
        out_shape=[jax.ShapeDtypeStruct((SEQ, D_MODEL), BF16)] * 3,
        scratch_shapes=[pltpu.VMEM((ATTN_BLOCK, width), F32)] * 2 if carry else [],
        compiler_params=_params("parallel", "arbitrary"),
    )(slopes.reshape(N_HEADS, 1, 1), q, k, k, proj, proj, do, lse, delta)


def _qknorm_bwd(proj, qw, kw, seg, dq, dk, dv, name):
    tm = 256

    def body(p_ref, qw_ref, kw_ref, seg_ref, dq_ref, dk_ref, dv_ref, dproj_ref, sums_ref):
        segv = seg_ref[...]
        sums = []
        for part, (w_ref, dn_ref) in enumerate(((qw_ref, dq_ref), (kw_ref, dk_ref))):
            raw = p_ref[:, part * D_MODEL:(part + 1) * D_MODEL].astype(F32)
            dn = dn_ref[...].astype(F32)
            r = _qk_rstd(raw, segv)
            gq = dn * w_ref[...]
            draw = r * gq - raw * (r * r * r) * (_segsum(raw * gq, segv) * (1.0 / HEAD_DIM))
            dproj_ref[:, part * D_MODEL:(part + 1) * D_MODEL] = draw.astype(BF16)
            sums.append(jnp.sum(dn * raw * r, axis=0, keepdims=True))
        dproj_ref[:, 2 * D_MODEL:] = dv_ref[...]

        @pl.when(pl.program_id(0) == 0)
        def _():
            sums_ref[...] = jnp.zeros_like(sums_ref)

        sums_ref[...] += jnp.concatenate(sums + [jnp.zeros((6, D_MODEL), F32)], axis=0)

    return pl.pallas_call(
        body, name=name, grid=(SEQ // tm,),
        in_specs=[_row_spec(tm, 3 * D_MODEL), _vec_spec(1, D_MODEL), _vec_spec(1, D_MODEL), _vec_spec(256, 256)]
        + [_row_spec(tm, D_MODEL)] * 3,
        out_specs=[_row_spec(tm, 3 * D_MODEL), _vec_spec(8, D_MODEL)],
        out_shape=[jax.ShapeDtypeStruct((SEQ, 3 * D_MODEL), BF16), jax.ShapeDtypeStruct((8, D_MODEL), F32)],
        compiler_params=_params("arbitrary"),
    )(proj, qw, kw, seg, dq, dk, dv)


def _to_classes(a, dilation):
    if dilation == 1:
        return a
    s, c = a.shape
    return a.reshape(s // dilation, dilation, c).transpose(1, 0, 2).reshape(s, c)


def _from_classes(a, dilation):
    if dilation == 1:
        return a
    s, c = a.shape
    return a.reshape(dilation, s // dilation, c).transpose(1, 0, 2).reshape(s, c)


def _cols_to_classes(a, dilation):
    if dilation == 1:
        return a
    r, s = a.shape
    return a.reshape(r, s // dilation, dilation).transpose(0, 2, 1).reshape(r, s)


B_TN = 512
B_GROUP_TILES = 3 * D_MODEL // B_TN
B_Z_TILE0 = 3 * B_GROUP_TILES
B_Z_TILES = D_MODEL // B_TN


def _local_step(x, target, mods, norm_g, conv_w, conv_b, ln_g, ln_b, q_norm, k_norm,
                weights_a, weights_b, forward_weights_b, send_grads_b, forward_grads_b, send_grads_a):
    row = lambda a, i: a[i:i + 1]
    shift0, scale0, gate0 = row(mods[0], 0), row(mods[0], 1), row(mods[0], 2)
    shift1, scale1, gate1 = row(mods[1], 0), row(mods[1], 1), row(mods[1], 2)
    g0, g1 = row(norm_g, 0), row(norm_g, 1)
    seg = _seg_matrix()
    slopes = jnp.exp2(-8.0 * jnp.arange(1, N_HEADS + 1, dtype=F32) / N_HEADS)
    qw = [jnp.tile(q_norm[g:g + 1], (1, N_HEADS)) for g in range(3)]
    kw = [jnp.tile(k_norm[g:g + 1], (1, N_HEADS)) for g in range(3)]

    h0, h0t = _normmod_fwd(x, g0, scale0, shift0, "prenorm0")
    wa_in, wa_out = weights_a(h0)
    ja, _, nsa = wa_in.shape
    proj_a = _mm(h0, wa_in, tn=nsa, tile0=0, n_tiles=ja, out_dtype=F32, name="a_in")
    u5, u5t, u2 = _conv_fwd(proj_a, conv_w, conv_b, ln_g, ln_b, "a_conv")
    x1, y_a, h1t, h1c = _out_a(u5, wa_out, x, gate0, g1, scale1, shift1, "a_out")

    forward_weights_b(x1)
    wb_in, wb_out = weights_b(x1)
    jb, _, nsb = wb_in.shape
    h1 = h1c[0]
    z_b = _mm(h1, wb_in, tn=B_TN, tile0=B_Z_TILE0, n_tiles=B_Z_TILES, out_dtype=F32, name="b_in_z")
    proj_g, qkv, o_parts, lse_parts = [], [], [], []
    for g, d in enumerate(DILATIONS):
        pg = _mm(h1c[g], wb_in, tn=B_TN, tile0=g * B_GROUP_TILES, n_tiles=B_GROUP_TILES, out_dtype=BF16,
                 name=f"b_in_g{g}")
        qn, kn = _qknorm_fwd(pg, qw[g], kw[g], seg, f"b_qknorm_g{g}")
        og, lg = _attn_fwd(qn, kn, pg, slopes, d, f"b_attn_g{g}")
        proj_g.append(pg)
        qkv.append((qn, kn))
        o_parts.append(og if d == 1 else og.reshape(d, SEQ // d, D_MODEL))
        lse_parts.append(lg if d == 1 else lg.reshape(d, SEQ // d, LANES))
    sel = _head_selector()
    u_b, u_bt, o_b, lse_b = _merge_fwd(o_parts, lse_parts, z_b, sel, "b_merge")
    e, dy_b, sums_loss = _out_b_loss(u_b, wb_out, x1, gate1, target, "b_out_loss")

    dwb_out = _mm(u_bt, dy_b, tn=D_MODEL, tile0=0, n_tiles=1, out_dtype=BF16, name="b_dwout")
    dz_b, do_c, delta_c, lse_c = _merge_bwd(dy_b, wb_out, o_b, lse_b, z_b, sel, "b_merge_bwd")
    dwb_in = _mm(h1t, dz_b, tn=B_TN, tile0=B_Z_TILE0, n_tiles=B_Z_TILES, out_dtype=BF16, name="b_dwin_z",
                 out3d=(jb, nsb))
    dh1_parts = [_mm_nt(dz_b, wb_in, tn=B_TN, tile0=B_Z_TILE0, n_tiles=B_Z_TILES, name="b_dh_z")]
    qk_sums = []
    for g, d in enumerate(DILATIONS):
        qn, kn = qkv[g]
        dq, dk, dv = _attn_bwd(qn, kn, proj_g[g], do_c[g], lse_c[g], delta_c[g], slopes, d, f"b_attn_bwd_g{g}")
        dproj, sums_qk = _qknorm_bwd(proj_g[g], qw[g], kw[g], seg, dq, dk, dv, f"b_qknorm_bwd_g{g}")
        qk_sums.append(sums_qk)
        dwb_in = _mm(h1t if d == 1 else h1c[g], dproj, tn=B_TN, tile0=g * B_GROUP_TILES, n_tiles=B_GROUP_TILES,
                     out_dtype=BF16, name=f"b_dwin_g{g}", out3d=(jb, nsb), prev=dwb_in, transpose_lhs=d != 1)
        dh = _mm_nt(dproj, wb_in, tn=B_TN, tile0=g * B_GROUP_TILES, n_tiles=B_GROUP_TILES, name=f"b_dh_g{g}")
        dh1_parts.append(dh)
    token = send_grads_b(dwb_in, dwb_out)
    dx1, sums_n1, dy_a = _normmod_bwd(x1, g1, scale1 + token[0:1, 0:1], dh1_parts, e, "prenorm1_bwd",
                                      part_dilations=(1,) + DILATIONS, gated=(gate0, y_a))
    token = forward_grads_b(dx1)

    dwa_out = _mm(u5t, dy_a, tn=D_MODEL, tile0=0, n_tiles=1, out_dtype=BF16, name="a_dwout")
    du2, dz_a, sums_ln = _conv_bwd_pointwise(dy_a, wa_out, proj_a, u2, ln_g + token[0:1, 0:1], ln_b,
                                             "a_conv_bwd_pw")
    dproj_a, dconv_w = _conv_bwd_taps(du2, dz_a, proj_a, conv_w, "a_conv_bwd_taps")
    dwa_in = _mm(h0t, dproj_a, tn=nsa, tile0=0, n_tiles=ja, out_dtype=BF16, name="a_dwin", out3d=(ja, nsa))
    token = send_grads_a(dwa_in, dwa_out)
    dh0 = _mm_nt(dproj_a, wa_in, tn=nsa, tile0=0, n_tiles=ja, name="a_dh", after=token)
    grad_x, sums_n0 = _normmod_bwd(x, g0, scale0, [dh0], dx1, "prenorm0_bwd")

    small = dict(
        dnorm_g=jnp.concatenate([sums_n0[0:1], sums_n1[0:1]], axis=0),
        dmod0=jnp.concatenate([sums_n0[2:3], sums_n0[1:2], sums_n1[3:4]], axis=0),
        dmod1=jnp.concatenate([sums_n1[2:3], sums_n1[1:2], sums_loss[0:1]], axis=0),
        dln_g=sums_ln[0:1], dln_b=sums_ln[1:2], dconv_b=sums_ln[2:3],
        dconv_w=dconv_w[:CONV_WIDTH],
        dq_norm=jnp.concatenate([s[0:1] for s in qk_sums], axis=0),
        dk_norm=jnp.concatenate([s[1:2] for s in qk_sums], axis=0),
        loss_cols=sums_loss[1:2],
    )
    return grad_x, small


def _adamw(w, g, m, v, name, after=None, copy_grad=False):
    rows, cols = w.shape
    tr = rows if rows <= 128 else 128
    c1 = 1.0 / (1.0 - ADAM_B1 ** ADAM_STEP)
    c2 = 1.0 / (1.0 - ADAM_B2 ** ADAM_STEP)
    extra = [] if after is None else [after]
    n_out = 4 if copy_grad else 3

    def body(w_ref, g_ref, m_ref, v_ref, *rest):
        d_ref, mo_ref, vo_ref = rest[len(extra):len(extra) + 3]
        gv = g_ref[...]
        if copy_grad:
            rest[-1][...] = gv
        mn = ADAM_B1 * m_ref[...] + (1.0 - ADAM_B1) * gv
        vn = ADAM_B2 * v_ref[...] + (1.0 - ADAM_B2) * (gv * gv)
        mo_ref[...] = mn
        vo_ref[...] = vn
        d_ref[...] = -ADAM_LR * ((mn * c1) / (jnp.sqrt(vn * c2) + ADAM_EPS) + ADAM_WD * w_ref[...])

    spec = pl.BlockSpec((tr, cols), lambda i: (i, 0))
    return pl.pallas_call(
        body, name=name, grid=(rows // tr,),
        in_specs=[spec] * 4 + [pl.BlockSpec(memory_space=pl.ANY)] * len(extra), out_specs=[spec] * n_out,
        out_shape=[jax.ShapeDtypeStruct((rows, cols), F32)] * n_out,
        compiler_params=_params("parallel"),
    )(w, g, m, v, *extra)


def _cast_into_slot(w, chip_idx, name):
    rows, cols = w.shape
    tr = 256

    def body(ch_ref, w_ref, o_ref):
        o_ref[...] = w_ref[...].astype(BF16)

    return pl.pallas_call(
        body, name=name,
        grid_spec=pltpu.PrefetchScalarGridSpec(
            num_scalar_prefetch=1, grid=(rows // tr,),
            in_specs=[pl.BlockSpec((tr, cols), lambda i, ch: (i, 0))],
            out_specs=pl.BlockSpec((None, tr, cols), lambda i, ch: (ch[0], i, 0))),
        out_shape=jax.ShapeDtypeStruct((N_CHIPS, rows, cols), BF16), compiler_params=_params("parallel"),
    )(chip_idx, w)


def _position():
    x, y, c = lax.axis_index("x"), lax.axis_index("y"), lax.axis_index("c")
    return x, y, c


def _xor_peer(x, y, c, k):
    return (x ^ ((k >> 2) & 1), y ^ ((k >> 1) & 1), c ^ (k & 1))


def _chip_peer(x, y, k):
    return (x ^ ((k >> 1) & 1), y ^ (k & 1))


def _ada_forward(c_row, ada_w, ada_b, conv_w):
    ns = ada_w.shape[2]
    cw = conv_w.shape[1]

    def body(c_ref, w_ref, b_ref, cv_ref, mod_ref, sc_ref, cvo_ref,
             c_all, mp, parts, cv_parts, send1, recv1, send2, recv2, send3, recv3):
        x, y, c = _position()
        me = 4 * x + 2 * y + c
        chip = 2 * x + y

        def c_copy(k):
            return pltpu.make_async_remote_copy(
                src_ref=c_all.at[me], dst_ref=c_all.at[me], send_sem=send1.at[k - 1], recv_sem=recv1.at[k - 1],
                device_id=_xor_peer(x, y, c, k), device_id_type=MESH)

        def cv_copy(k):
            px, py = _chip_peer(x, y, k)
            return pltpu.make_async_remote_copy(
                src_ref=cv_parts.at[chip], dst_ref=cv_parts.at[chip], send_sem=send3.at[k - 1],
                recv_sem=recv3.at[k - 1], device_id=(px, py, c), device_id_type=MESH)

        c_all[me] = c_ref[...]
        cv_parts[chip] = cv_ref[...]
        for k in range(1, N_DEV):
            c_copy(k).start()
        for k in range(1, N_CHIPS):
            cv_copy(k).start()
        for k in range(1, N_DEV):
            c_copy(k).wait_recv()
        cv = jnp.concatenate([c_all[i] for i in range(N_DEV)], axis=0)
        sc = cv * _sigmoid(cv)
        sc_ref[...] = sc
        for l in range(2):
            res = jnp.dot(sc, w_ref[l], preferred_element_type=F32, precision=lax.Precision.HIGHEST)
            for i in range(N_DEV):
                mp[i, l:l + 1, :] = res[i:i + 1, :]

        def mod_copy(k):
            px, py = _chip_peer(x, y, k)
            return pltpu.make_async_remote_copy(
                src_ref=mp.at[4 * px + 2 * py + c], dst_ref=parts.at[chip], send_sem=send2.at[k - 1],
                recv_sem=recv2.at[k - 1], device_id=(px, py, c), device_id_type=MESH)

        for k in range(1, N_CHIPS):
            mod_copy(k).start()
        parts[chip] = mp[me]
        for k in range(1, N_CHIPS):
            mod_copy(k).wait_recv()
            cv_copy(k).wait_recv()
        mod_ref[...] = jnp.concatenate([parts[j] for j in range(N_CHIPS)], axis=1) + b_ref[...]
        cvo_ref[...] = jnp.concatenate([cv_parts[j] for j in range(N_CHIPS)], axis=1)
        for k in range(1, N_DEV):
            c_copy(k).wait_send()
        for k in range(1, N_CHIPS):
            mod_copy(k).wait_send()
            cv_copy(k).wait_send()

    vm = pl.BlockSpec(memory_space=pltpu.VMEM)
    return pl.pallas_call(
        body, name="ada_forward",
        in_specs=[vm] * 4, out_specs=[vm] * 3,
        out_shape=[jax.ShapeDtypeStruct((2, 3 * D_MODEL), F32), jax.ShapeDtypeStruct((N_DEV, D_MODEL), F32),
                   jax.ShapeDtypeStruct((CONV_WIDTH, N_CHIPS * cw), F32)],
        scratch_shapes=[pltpu.VMEM((N_DEV, 1, D_MODEL), F32), pltpu.VMEM((N_DEV, 2, ns), F32),
                        pltpu.VMEM((N_CHIPS, 2, ns), F32), pltpu.VMEM((N_CHIPS, CONV_WIDTH, cw), F32),
                        pltpu.SemaphoreType.DMA((N_DEV - 1,)), pltpu.SemaphoreType.DMA((N_DEV - 1,)),
                        pltpu.SemaphoreType.DMA((N_CHIPS - 1,)), pltpu.SemaphoreType.DMA((N_CHIPS - 1,)),
                        pltpu.SemaphoreType.DMA((N_CHIPS - 1,)), pltpu.SemaphoreType.DMA((N_CHIPS - 1,))],
        compiler_params=pltpu.CompilerParams(vmem_limit_bytes=VMEM_LIMIT_BYTES),
    )(c_row, ada_w, ada_b, conv_w)


HBM_SPEC = pl.BlockSpec(memory_space=pltpu.HBM)
ANY_SPEC = pl.BlockSpec(memory_space=pl.ANY)
SEM_SPEC = pl.BlockSpec(memory_space=pltpu.SEMAPHORE)
SPLIT_PARAMS = dict(compiler_params=pltpu.CompilerParams(has_side_effects=pltpu.SideEffectType.DATAFLOW_SIDE_EFFECTING))
TOKEN = jax.ShapeDtypeStruct((8, 128), F32)


def _hbm(arrays):
    return [pltpu.with_memory_space_constraint(a, pltpu.HBM) for a in arrays]


def _hbm_like(arrays):
    return [pltpu.HBM(a.shape, a.dtype) for a in arrays]


def _gather_start(lands, after, name):
    n = len(lands)

    def body(*refs):
        ins = refs[:n]
        send, recv = refs[n + 1], refs[n + 2]
        x, y, c = _position()
        chip = 2 * x + y
        for t in range(n):
            rh = ins[t].shape[1] // 2
            for k in range(1, N_CHIPS):
                px, py = _chip_peer(x, y, k)
                block = ins[t].at[chip, pl.ds(c * rh, rh)]
                pltpu.make_async_remote_copy(
                    src_ref=block, dst_ref=block, send_sem=send.at[3 * t + k - 1], recv_sem=recv.at[3 * t + k - 1],
                    device_id=(px, py, c), device_id_type=MESH).start()
        refs[-1][...] = jnp.zeros(TOKEN.shape, F32)

    res = pl.pallas_call(
        body, name=name, in_specs=[HBM_SPEC] * n + [ANY_SPEC],
        out_specs=(SEM_SPEC, SEM_SPEC, *[HBM_SPEC] * n, pl.BlockSpec(memory_space=pltpu.VMEM)),
        out_shape=(pltpu.SemaphoreType.DMA((3 * n,)), pltpu.SemaphoreType.DMA((3 * n,)), *_hbm_like(lands), TOKEN),
        input_output_aliases={t: 2 + t for t in range(n)}, **SPLIT_PARAMS,
    )(*_hbm(lands), after)
    return res[0], res[1], list(res[2:2 + n]), res[-1]


def _gather_forward(send, recv, lands, after, name):
    n = len(lands)

    def body(*refs):
        ins = refs[:n]
        send1, recv1 = refs[n], refs[n + 1]
        send2, recv2 = refs[n + 3], refs[n + 4]
        x, y, c = _position()
        chip = 2 * x + y
        for t in range(n):
            rh = ins[t].shape[1] // 2
            half = pl.ds(c * rh, rh)
            for k in range(1, N_CHIPS):
                px, py = _chip_peer(x, y, k)
                s = 3 * t + k - 1
                got = ins[t].at[2 * px + py, half]
                cp = pltpu.make_async_remote_copy(
                    src_ref=ins[t].at[chip, half], dst_ref=got, send_sem=send1.at[s], recv_sem=recv1.at[s],
                    device_id=(px, py, c), device_id_type=MESH)
                cp.wait_send()
                cp.wait_recv()
                pltpu.make_async_remote_copy(
                    src_ref=got, dst_ref=got, send_sem=send2.at[s], recv_sem=recv2.at[s],
                    device_id=(x, y, 1 - c), device_id_type=MESH).start()
        refs[-1][...] = jnp.zeros(TOKEN.shape, F32)

    res = pl.pallas_call(
        body, name=name, in_specs=[HBM_SPEC] * n + [SEM_SPEC, SEM_SPEC, ANY_SPEC],
        out_specs=(SEM_SPEC, SEM_SPEC, *[HBM_SPEC] * n, pl.BlockSpec(memory_space=pltpu.VMEM)),
        out_shape=(pltpu.SemaphoreType.DMA((3 * n,)), pltpu.SemaphoreType.DMA((3 * n,)), *_hbm_like(lands), TOKEN),
        input_output_aliases={t: 2 + t for t in range(n)}, **SPLIT_PARAMS,
    )(*lands, send, recv, after)
    return res[0], res[1], list(res[2:2 + n]), res[-1]


def _gather_wait(send, recv, lands, after, name):
    n = len(lands)

    def body(*refs):
        ins = refs[:n]
        send_ref, recv_ref = refs[n], refs[n + 1]
        x, y, c = _position()
        for t in range(n):
            rh = ins[t].shape[1] // 2
            for k in range(1, N_CHIPS):
                px, py = _chip_peer(x, y, k)
                cp = pltpu.make_async_remote_copy(
                    src_ref=ins[t].at[2 * px + py, pl.ds(c * rh, rh)],
                    dst_ref=ins[t].at[2 * px + py, pl.ds((1 - c) * rh, rh)], send_sem=send_ref.at[3 * t + k - 1],
                    recv_sem=recv_ref.at[3 * t + k - 1], device_id=(x, y, 1 - c), device_id_type=MESH)
                cp.wait_send()
                cp.wait_recv()

    res = pl.pallas_call(
        body, name=name, in_specs=[HBM_SPEC] * n + [SEM_SPEC, SEM_SPEC, ANY_SPEC], out_specs=[HBM_SPEC] * n,
        out_shape=_hbm_like(lands), input_output_aliases={t: t for t in range(n)}, **SPLIT_PARAMS,
    )(*lands, send, recv, after)
    return list(res)


def _reduce_start(grads, after, name):
    n = len(grads)
    lands = [lax.empty((N_DEV, g.shape[1] // 2, g.shape[2]), BF16) for g in grads]

    def body(*refs):
        gs, ls = refs[:n], refs[n:2 * n]
        send, recv = refs[2 * n + 1], refs[2 * n + 2]
        x, y, c = _position()
        me = 4 * x + 2 * y + c
        for t in range(n):
            rh = gs[t].shape[1] // 2
            for k in range(1, N_DEV):
                px, py, pc = _xor_peer(x, y, c, k)
                pltpu.make_async_remote_copy(
                    src_ref=gs[t].at[2 * px + py, pl.ds(pc * rh, rh)], dst_ref=ls[t].at[me],
                    send_sem=send.at[7 * t + k - 1], recv_sem=recv.at[7 * t + k - 1],
                    device_id=(px, py, pc), device_id_type=MESH).start()
        refs[-1][...] = jnp.zeros(TOKEN.shape, F32)

    res = pl.pallas_call(
        body, name=name, in_specs=[HBM_SPEC] * (2 * n) + [ANY_SPEC],
        out_specs=(SEM_SPEC, SEM_SPEC, *[HBM_SPEC] * (2 * n), pl.BlockSpec(memory_space=pltpu.VMEM)),
        out_shape=(pltpu.SemaphoreType.DMA((7 * n,)), pltpu.SemaphoreType.DMA((7 * n,)),
                   *_hbm_like(grads), *_hbm_like(lands), TOKEN),
        input_output_aliases={t: 2 + t for t in range(2 * n)}, **SPLIT_PARAMS,
    )(*_hbm(grads), *_hbm(lands), after)
    return res[0], res[1], list(res[2:2 + n]), list(res[2 + n:2 + 2 * n]), res[-1]


def _reduce_wait(send, recv, grads, lands, after, name):
    n = len(grads)

    def body(*refs):
        gs, ls = refs[:n], refs[n:2 * n]
        send_ref, recv_ref = refs[2 * n], refs[2 * n + 1]
        x, y, c = _position()
        for t in range(n):
            rh = gs[t].shape[1] // 2
            for k in range(1, N_DEV):
                px, py, pc = _xor_peer(x, y, c, k)
                cp = pltpu.make_async_remote_copy(
                    src_ref=gs[t].at[2 * px + py, pl.ds(pc * rh, rh)], dst_ref=ls[t].at[4 * px + 2 * py + pc],
                    send_sem=send_ref.at[7 * t + k - 1], recv_sem=recv_ref.at[7 * t + k - 1],
                    device_id=(px, py, pc), device_id_type=MESH)
                cp.wait_send()
                cp.wait_recv()

    res = pl.pallas_call(
        body, name=name, in_specs=[HBM_SPEC] * (2 * n) + [SEM_SPEC, SEM_SPEC, ANY_SPEC], out_specs=[HBM_SPEC] * (2 * n),
        out_shape=_hbm_like(grads) + _hbm_like(lands), input_output_aliases={t: t for t in range(2 * n)}, **SPLIT_PARAMS,
    )(*grads, *lands, send, recv, after)
    return list(res[:n]), list(res[n:])


def _sum_devices(land, grad, dev_idx, name):
    _, rh, cols = land.shape
    tr = 128
    nb = rh // tr

    def body(idx_ref, l_ref, g_ref, o_ref):
        me = idx_ref[0]
        acc = jnp.where(me == 0, g_ref[...], l_ref[0]).astype(F32)
        for d in range(1, N_DEV):
            acc = acc + jnp.where(me == d, g_ref[...], l_ref[d]).astype(F32)
        o_ref[...] = acc

    return pl.pallas_call(
        body, name=name,
        grid_spec=pltpu.PrefetchScalarGridSpec(
            num_scalar_prefetch=1, grid=(nb,),
            in_specs=[pl.BlockSpec((N_DEV, tr, cols), lambda i, idx: (0, i, 0)),
                      pl.BlockSpec((None, tr, cols), lambda i, idx: (idx[1], idx[2] * nb + i, 0))],
            out_specs=pl.BlockSpec((tr, cols), lambda i, idx: (idx[2] * nb + i, 0))),
        out_shape=jax.ShapeDtypeStruct((2 * rh, cols), F32), compiler_params=_params("parallel"),
    )(dev_idx, land, grad)


def _split_start(name, arrays, n_sems, after, issue):
    m = len(arrays)

    def body(*refs):
        issue(refs[:m], refs[m + 1], refs[m + 2])
        refs[-1][...] = jnp.zeros(TOKEN.shape, F32)

    res = pl.pallas_call(
        body, name=name, in_specs=[HBM_SPEC] * m + [ANY_SPEC],
        out_specs=(SEM_SPEC, SEM_SPEC, *[HBM_SPEC] * m, pl.BlockSpec(memory_space=pltpu.VMEM)),
        out_shape=(pltpu.SemaphoreType.DMA((n_sems,)), pltpu.SemaphoreType.DMA((n_sems,)), *_hbm_like(arrays), TOKEN),
        input_output_aliases={t: 2 + t for t in range(m)}, **SPLIT_PARAMS,
    )(*_hbm(arrays), after)
    return res[0], res[1], list(res[2:2 + m]), res[-1]


def _split_wait(name, arrays, send, recv, after, await_all):
    m = len(arrays)

    def body(*refs):
        await_all(refs[:m], refs[m], refs[m + 1])

    res = pl.pallas_call(
        body, name=name, in_specs=[HBM_SPEC] * m + [SEM_SPEC, SEM_SPEC, ANY_SPEC], out_specs=[HBM_SPEC] * m,
        out_shape=_hbm_like(arrays), input_output_aliases={t: t for t in range(m)}, **SPLIT_PARAMS,
    )(*arrays, send, recv, after)
    return list(res)


def _sibling_copies(refs, send, recv, n):
    x, y, c = _position()
    cps = []
    for t in range(n):
        rh = refs[t].shape[1] // 2
        cps.append(pltpu.make_async_remote_copy(
            src_ref=refs[t].at[pl.ds(0, N_CHIPS), pl.ds((1 - c) * rh, rh)], dst_ref=refs[n + t],
            send_sem=send.at[t], recv_sem=recv.at[t], device_id=(x, y, 1 - c), device_id_type=MESH))
    return cps


def _reduce_sibling_start(grads, after, name):
    n = len(grads)
    lands = [lax.empty((N_CHIPS, g.shape[1] // 2, g.shape[2]), BF16) for g in grads]

    def issue(refs, send, recv):
        for cp in _sibling_copies(refs, send, recv, n):
            cp.start()

    return _split_start(name, list(grads) + lands, n, after, issue)


def _reduce_sibling_wait(send, recv, arrays, after, name):
    n = len(arrays) // 2

    def await_all(refs, send_ref, recv_ref):
        for cp in _sibling_copies(refs, send_ref, recv_ref, n):
            cp.wait_send()
            cp.wait_recv()

    res = _split_wait(name, arrays, send, recv, after, await_all)
    return res[:n], res[n:]


def _add_sibling_half(grad, got, dev_idx, name):
    j, r, cols = grad.shape
    rh = r // 2
    tr = rh
    nb = rh // tr

    def body(idx_ref, g_ref, got_ref, out_ref):
        out_ref[...] = (g_ref[...].astype(F32) + got_ref[...].astype(F32)).astype(BF16)

    return pl.pallas_call(
        body, name=name,
        grid_spec=pltpu.PrefetchScalarGridSpec(
            num_scalar_prefetch=1, grid=(j, nb),
            in_specs=[pl.BlockSpec((None, tr, cols), lambda jj, i, idx: (jj, idx[2] * nb + i, 0)),
                      pl.BlockSpec((None, tr, cols), lambda jj, i, idx: (jj, i, 0))],
            out_specs=pl.BlockSpec((None, tr, cols), lambda jj, i, idx: (jj, i, 0))),
        out_shape=jax.ShapeDtypeStruct((j, rh, cols), BF16),
        compiler_params=_params("parallel", "parallel"),
    )(dev_idx, grad, got)


def _chip_copies(refs, send, recv, n, receiving):
    x, y, c = _position()
    chip = 2 * x + y
    cps = []
    for t in range(n):
        for k in range(1, N_CHIPS):
            px, py = _chip_peer(x, y, k)
            cps.append(pltpu.make_async_remote_copy(
                src_ref=refs[t].at[2 * px + py], dst_ref=refs[n + t].at[2 * px + py if receiving else chip],
                send_sem=send.at[3 * t + k - 1], recv_sem=recv.at[3 * t + k - 1],
                device_id=(px, py, c), device_id_type=MESH))
    return cps


def _reduce_chips_start(partials, after, name):
    n = len(partials)
    lands = [lax.empty(p.shape, BF16) for p in partials]

    def issue(refs, send, recv):
        for cp in _chip_copies(refs, send, recv, n, False):
            cp.start()

    return _split_start(name, list(partials) + lands, 3 * n, after, issue)


def _reduce_chips_wait(send, recv, arrays, after, name):
    n = len(arrays) // 2

    def await_all(refs, send_ref, recv_ref):
        for cp in _chip_copies(refs, send_ref, recv_ref, n, True):
            cp.wait_send()
            cp.wait_recv()

    res = _split_wait(name, arrays, send, recv, after, await_all)
    return res[:n], res[n:]


def _sum_partials(land, partial, dev_idx, name):
    _, rh, cols = land.shape
    tr = 128
    nb = rh // tr

    def body(idx_ref, l_ref, p_ref, o_ref):
        chip = idx_ref[1]
        acc = jnp.where(chip == 0, p_ref[...], l_ref[0]).astype(F32)
        for s in range(1, N_CHIPS):
            acc = acc + jnp.where(chip == s, p_ref[...], l_ref[s]).astype(F32)
        o_ref[...] = acc

    return pl.pallas_call(
        body, name=name,
        grid_spec=pltpu.PrefetchScalarGridSpec(
            num_scalar_prefetch=1, grid=(nb,),
            in_specs=[pl.BlockSpec((N_CHIPS, tr, cols), lambda i, idx: (0, i, 0)),
                      pl.BlockSpec((None, tr, cols), lambda i, idx: (idx[1], i, 0))],
            out_specs=pl.BlockSpec((tr, cols), lambda i, idx: (idx[2] * nb + i, 0))),
        out_shape=jax.ShapeDtypeStruct((2 * rh, cols), F32), compiler_params=_params("parallel"),
    )(dev_idx, land, partial)


def _half_copies(refs, send, recv, receiving):
    x, y, c = _position()
    cps = []
    for t, ref in enumerate(refs):
        rh = ref.shape[0] // 2
        cps.append(pltpu.make_async_remote_copy(
            src_ref=ref.at[pl.ds(c * rh, rh)], dst_ref=ref.at[pl.ds(((1 - c) if receiving else c) * rh, rh)],
            send_sem=send.at[t], recv_sem=recv.at[t], device_id=(x, y, 1 - c), device_id_type=MESH))
    return cps


def _share_halves_start(totals, after, name):
    def issue(refs, send, recv):
        for cp in _half_copies(refs, send, recv, False):
            cp.start()

    return _split_start(name, list(totals), len(totals), after, issue)


def _share_halves_wait(send, recv, totals, after, name):
    def await_all(refs, send_ref, recv_ref):
        for cp in _half_copies(refs, send_ref, recv_ref, True):
            cp.wait_send()
            cp.wait_recv()

    return _split_wait(name, totals, send, recv, after, await_all)


def _exchange_halves(grads):
    n = len(grads)
    hbm = pl.BlockSpec(memory_space=pl.ANY)

    def body(*refs):
        ins, outs = refs[:n], refs[n:2 * n]
        send, recv = refs[2 * n:]
        x, y, c = _position()
        cps = []
        for t in range(n):
            rh = ins[t].shape[1] // 2
            cp = pltpu.make_async_remote_copy(
                src_ref=ins[t].at[pl.ds(0, N_CHIPS), pl.ds((1 - c) * rh, rh)], dst_ref=outs[t], send_sem=send.at[t],
                recv_sem=recv.at[t], device_id=(x, y, 1 - c), device_id_type=MESH)
            cp.start()
            cps.append(cp)
        for cp in cps:
            cp.wait()

    return pl.pallas_call(
        body, name="reduce_exchange_halves", in_specs=[hbm] * n, out_specs=[hbm] * n,
        out_shape=[jax.ShapeDtypeStruct((g.shape[0], g.shape[1] // 2, g.shape[2]), BF16) for g in grads],
        scratch_shapes=[pltpu.SemaphoreType.DMA((n,)), pltpu.SemaphoreType.DMA((n,))],
    )(*grads)


def _add_halves(grad, got, c_idx, name):
    j, r, cols = grad.shape
    rh = r // 2
    tr = 128
    nb = rh // tr

    def body(c_ref, g_ref, o_ref_in, out_ref):
        out_ref[...] = (g_ref[...].astype(F32) + o_ref_in[...].astype(F32)).astype(BF16)

    return pl.pallas_call(
        body, name=name,
        grid_spec=pltpu.PrefetchScalarGridSpec(
            num_scalar_prefetch=1, grid=(j, nb),
            in_specs=[pl.BlockSpec((None, tr, cols), lambda jj, i, c_ref: (jj, c_ref[0] * nb + i, 0)),
                      pl.BlockSpec((None, tr, cols), lambda jj, i, c_ref: (jj, i, 0))],
            out_specs=pl.BlockSpec((None, tr, cols), lambda jj, i, c_ref: (jj, i, 0))),
        out_shape=jax.ShapeDtypeStruct((j, rh, cols), BF16),
        compiler_params=_params("parallel", "parallel"),
    )(c_idx, grad, got)


def _scatter_partials(partials):
    n = len(partials)
    hbm = pl.BlockSpec(memory_space=pl.ANY)

    def body(*refs):
        ins, outs = refs[:n], refs[n:2 * n]
        send, recv, local = refs[2 * n:]
        x, y, c = _position()
        chip = 2 * x + y
        cps, lcs = [], []
        for t in range(n):
            lc = pltpu.make_async_copy(ins[t].at[chip], outs[t].at[chip], local.at[t])
            lc.start()
            lcs.append(lc)
            for k in range(1, N_CHIPS):
                px, py = _chip_peer(x, y, k)
                s = 3 * t + k - 1
                cp = pltpu.make_async_remote_copy(
                    src_ref=ins[t].at[2 * px + py], dst_ref=outs[t].at[chip], send_sem=send.at[s],
                    recv_sem=recv.at[s], device_id=(px, py, c), device_id_type=MESH)
                cp.start()
                cps.append(cp)
        for cp in cps:
            cp.wait()
        for lc in lcs:
            lc.wait()

    return pl.pallas_call(
        body, name="reduce_scatter_partials", in_specs=[hbm] * n, out_specs=[hbm] * n,
        out_shape=[jax.ShapeDtypeStruct(p.shape, BF16) for p in partials],
        scratch_shapes=[pltpu.SemaphoreType.DMA((3 * n,)), pltpu.SemaphoreType.DMA((3 * n,)),
                        pltpu.SemaphoreType.DMA((n,))],
    )(*partials)


def _sum_chips(parts, name):
    j, rh, cols = parts.shape
    tr = 128

    def body(p_ref, o_ref):
        acc = p_ref[0].astype(F32)
        for s in range(1, j):
            acc = acc + p_ref[s].astype(F32)
        o_ref[...] = acc

    return pl.pallas_call(
        body, name=name, grid=(rh // tr,),
        in_specs=[pl.BlockSpec((j, tr, cols), lambda i: (0, i, 0))],
        out_specs=pl.BlockSpec((tr, cols), lambda i: (i, 0)),
        out_shape=jax.ShapeDtypeStruct((rh, cols), F32),
        compiler_params=_params("parallel"),
    )(parts)


def _share_totals(halves):
    n = len(halves)
    hbm = pl.BlockSpec(memory_space=pl.ANY)

    def body(*refs):
        ins, outs = refs[:n], refs[n:2 * n]
        send, recv, local = refs[2 * n:]
        x, y, c = _position()
        cps, lcs = [], []
        for t in range(n):
            rh = ins[t].shape[0]
            mine = outs[t].at[pl.ds(c * rh, rh)]
            lc = pltpu.make_async_copy(ins[t], mine, local.at[t])
            lc.start()
            lcs.append(lc)
            cp = pltpu.make_async_remote_copy(
                src_ref=ins[t], dst_ref=mine, send_sem=send.at[t], recv_sem=recv.at[t],
                device_id=(x, y, 1 - c), device_id_type=MESH)
            cp.start()
            cps.append(cp)
        for cp in cps:
            cp.wait()
        for lc in lcs:
            lc.wait()

    return pl.pallas_call(
        body, name="reduce_share_totals", in_specs=[hbm] * n, out_specs=[hbm] * n,
        out_shape=[jax.ShapeDtypeStruct((2 * h.shape[0], h.shape[1]), F32) for h in halves],
        scratch_shapes=[pltpu.SemaphoreType.DMA((n,)), pltpu.SemaphoreType.DMA((n,)),
                        pltpu.SemaphoreType.DMA((n,))],
    )(*halves)


SMALL_ROWS = 56


def _small_copies(refs, send, recv, receiving):
    x, y, c = _position()
    me = 4 * x + 2 * y + c
    cps = []
    for k in range(1, N_DEV):
        px, py, pc = _xor_peer(x, y, c, k)
        cps.append(pltpu.make_async_remote_copy(
            src_ref=refs[0], dst_ref=refs[1].at[4 * px + 2 * py + pc if receiving else me],
            send_sem=send.at[k - 1], recv_sem=recv.at[k - 1], device_id=(px, py, pc), device_id_type=MESH))
    return cps


def _small_gather_start(packed, after):
    land = lax.empty((N_DEV,) + packed.shape, F32)

    def issue(refs, send, recv):
        for cp in _small_copies(refs, send, recv, False):
            cp.start()

    return _split_start("small_gather_start", [packed, land], N_DEV - 1, after, issue)


def _small_gather_wait(send, recv, arrays, after):
    def await_all(refs, send_ref, recv_ref):
        for cp in _small_copies(refs, send_ref, recv_ref, True):
            cp.wait_send()
            cp.wait_recv()

    return _split_wait("small_gather_wait", arrays, send, recv, after, await_all)


def _reduce_small(packed, land, silu_c):
    ns = 3 * D_MODEL // N_CHIPS

    def body(p_ref, land_ref, sc_ref, tot_ref, gw_ref, loss_ref, qk_ref, allp):
        x, y, c = _position()
        me = 4 * x + 2 * y + c
        chip = 2 * x + y
        for i in range(N_DEV):
            allp[i] = jnp.where(me == i, p_ref[...], land_ref[i])
        tot = allp[0]
        for i in range(1, N_DEV):
            tot = tot + allp[i]
        tot_ref[...] = tot
        loss_ref[...] = jnp.sum(tot[11:12, :], axis=1, keepdims=True) * (0.5 / D_MODEL)
        fold = tot[5:11, 0:HEAD_DIM]
        for h in range(1, N_HEADS):
            fold = fold + tot[5:11, h * HEAD_DIM:(h + 1) * HEAD_DIM]
        qk_ref[...] = jnp.concatenate([fold, jnp.zeros((2, HEAD_DIM), F32)], axis=0)
        sct = sc_ref[...].T
        rc = 64
        for l in range(2):
            dms = [allp[i, pl.ds(12 + 4 * l + chip, 1), :][:, :ns] for i in range(N_DEV)]
            for r0 in range(0, D_MODEL, rc):
                acc = sct[r0:r0 + rc, 0:1] * dms[0]
                for i in range(1, N_DEV):
                    acc = acc + sct[r0:r0 + rc, i:i + 1] * dms[i]
                gw_ref[l, r0:r0 + rc, :] = acc

    vm = pl.BlockSpec(memory_space=pltpu.VMEM)
    return pl.pallas_call(
        body, name="reduce_small", in_specs=[vm, vm, vm], out_specs=[vm] * 4,
        out_shape=[jax.ShapeDtypeStruct((SMALL_ROWS, D_MODEL), F32), jax.ShapeDtypeStruct((2, D_MODEL, ns), F32),
                   jax.ShapeDtypeStruct((1, 1), F32), jax.ShapeDtypeStruct((8, HEAD_DIM), F32)],
        scratch_shapes=[pltpu.VMEM((N_DEV, SMALL_ROWS, D_MODEL), F32)],
        compiler_params=pltpu.CompilerParams(vmem_limit_bytes=VMEM_LIMIT_BYTES),
    )(packed, land, silu_c)


def _reduce_big(grads, c_idx):
    names = list(grads)
    got = _exchange_halves([grads[k] for k in names])
    partials = [_add_halves(grads[k], got[i], c_idx, f"reduce_add_{k}") for i, k in enumerate(names)]
    parts = _scatter_partials(partials)
    halves = [_sum_chips(parts[i], f"reduce_sum_{k}") for i, k in enumerate(names)]
    totals = _share_totals(halves)
    return dict(zip(names, totals))


def kernel(x, c, norm_g, ada_w, ada_b, a_w_in, a_conv_w, a_conv_b, a_ln_g, a_ln_b, a_w_out, b_w_in, b_q_norm, b_k_norm, b_w_out, loss_target, m_norm_g, m_ada_w, m_ada_b, m_a_w_in, m_a_conv_w, m_a_conv_b, m_a_ln_g, m_a_ln_b, m_a_w_out, m_b_w_in, m_b_q_norm, m_b_k_norm, m_b_w_out, v_norm_g, v_ada_w, v_ada_b, v_a_w_in, v_a_conv_w, v_a_conv_b, v_a_ln_g, v_a_ln_b, v_a_w_out, v_b_w_in, v_b_q_norm, v_b_k_norm, v_b_w_out):
    chip = 2 * lax.axis_index("x") + lax.axis_index("y")
    core = lax.axis_index("c")
    chip_idx = chip.astype(jnp.int32).reshape(1)
    dev_idx = jnp.stack([2 * chip + core, chip, core]).astype(jnp.int32)

    mods, silu_c, conv_w_full = _ada_forward(c, ada_w, ada_b, a_conv_w[0])
    lands_a = [_cast_into_slot(a_w_in[0], chip_idx, "cast_a_w_in"), _cast_into_slot(a_w_out[0], chip_idx, "cast_a_w_out")]
    send_a, recv_a, lands_a, token_a = _gather_start(lands_a, mods, "gather_start_a")
    lands_b = [_cast_into_slot(b_w_in[0], chip_idx, "cast_b_w_in"), _cast_into_slot(b_w_out[0], chip_idx, "cast_b_w_out")]
    send_b, recv_b, lands_b, token_b = _gather_start(lands_b, token_a, "gather_start_b")
    mods = mods + token_b[0:2, 0:1]

    def weights_a(after):
        send, recv, lands, _ = _gather_forward(send_a, recv_a, lands_a, after, "gather_forward_a")
        w_in, w_out = _gather_wait(send, recv, lands, after, "gather_wait_a")
        return w_in, w_out.reshape(D_MODEL, D_MODEL)

    forwarded_b = []

    def weights_b(after):
        send, recv, lands, _ = forwarded_b
        w_in, w_out = _gather_wait(send, recv, lands, after, "gather_wait_b")
        return w_in, w_out.reshape(D_MODEL, D_MODEL)

    def forward_weights_b(after):
        forwarded_b.extend(_gather_forward(send_b, recv_b, lands_b, after, "gather_forward_b"))
        return forwarded_b[3]

    stage1, stage2 = {}, {}

    def send_grads(tag, dw_in, dw_out):
        grads = [dw_in, dw_out.reshape(N_CHIPS, D_MODEL // N_CHIPS, D_MODEL)]
        send, recv, arrays, token = _reduce_sibling_start(grads, dw_out, f"reduce_d2d_start_{tag}")
        stage1[tag] = (send, recv, arrays)
        return token

    def forward_grads(tag, after):
        send, recv, arrays = stage1[tag]
        grads, got = _reduce_sibling_wait(send, recv, arrays, after, f"reduce_d2d_wait_{tag}")
        partials = [_add_sibling_half(grads[i], got[i], dev_idx, f"reduce_add_{tag}_{i}") for i in range(2)]
        send, recv, arrays, token = _reduce_chips_start(partials, partials[1], f"reduce_ici_start_{tag}")
        stage2[tag] = (send, recv, arrays)
        return token

    stage3 = {}

    def sum_grads(tag, after):
        send, recv, arrays = stage2[tag]
        partials, lands = _reduce_chips_wait(send, recv, arrays, after, f"reduce_ici_wait_{tag}")
        totals = [_sum_partials(lands[i], partials[i], dev_idx, f"reduce_sum_{tag}_{i}") for i in range(2)]
        send, recv, totals, token = _share_halves_start(totals, totals[1], f"reduce_share_start_{tag}")
        stage3[tag] = (send, recv, totals)
        return token

    def finish_grads(tag, after):
        send, recv, totals = stage3[tag]
        return _share_halves_wait(send, recv, totals, after, f"reduce_share_wait_{tag}")

    grad_x, small = _local_step(
        x[0], loss_target[0], mods.reshape(2, 3, D_MODEL), norm_g, conv_w_full, a_conv_b, a_ln_g[0:1],
        a_ln_b[0:1], b_q_norm[0], b_k_norm[0], weights_a, weights_b, forward_weights_b,
        functools.partial(send_grads, "b"), functools.partial(forward_grads, "b"), functools.partial(send_grads, "a"))

    ns = 3 * D_MODEL // N_CHIPS
    pad_mod = lambda dm: jnp.pad(dm.reshape(N_CHIPS, ns), ((0, 0), (0, D_MODEL - ns)))
    packed = jnp.concatenate([
        small["dnorm_g"], small["dconv_b"], small["dln_g"], small["dln_b"], small["dq_norm"], small["dk_norm"],
        small["loss_cols"], pad_mod(small["dmod0"]), pad_mod(small["dmod1"]), small["dconv_w"],
        jnp.zeros((SMALL_ROWS - 20 - CONV_WIDTH, D_MODEL), F32)], axis=0)
    send_s, recv_s, small_arrays, token_s = _small_gather_start(packed, packed)

    given = dict(norm_g=(norm_g, m_norm_g, v_norm_g), ada_w=(ada_w, m_ada_w, v_ada_w), ada_b=(ada_b, m_ada_b, v_ada_b),
                 a_w_in=(a_w_in, m_a_w_in, v_a_w_in), a_conv_w=(a_conv_w, m_a_conv_w, v_a_conv_w),
                 a_conv_b=(a_conv_b, m_a_conv_b, v_a_conv_b), a_ln_g=(a_ln_g, m_a_ln_g, v_a_ln_g),
                 a_ln_b=(a_ln_b, m_a_ln_b, v_a_ln_b), a_w_out=(a_w_out, m_a_w_out, v_a_w_out),
                 b_w_in=(b_w_in, m_b_w_in, v_b_w_in), b_q_norm=(b_q_norm, m_b_q_norm, v_b_q_norm),
                 b_k_norm=(b_k_norm, m_b_k_norm, v_b_k_norm), b_w_out=(b_w_out, m_b_w_out, v_b_w_out))
    order = ["norm_g", "ada_w", "ada_b", "a_w_in", "a_conv_w", "a_conv_b", "a_ln_g", "a_ln_b", "a_w_out", "b_w_in",
             "b_q_norm", "b_k_norm", "b_w_out"]
    outs = {}

    def update(k, g2, after=None, copy_grad=False):
        w, m, v = given[k]
        shape2 = g2.shape
        res = _adamw(w.reshape(shape2), g2, m.reshape(shape2), v.reshape(shape2), f"adamw_{k}", after, copy_grad)
        outs[k] = tuple(a.reshape(w.shape) for a in ((res[3] if copy_grad else g2), res[0], res[1], res[2]))

    token = forward_grads("a", token_s)
    token = sum_grads("b", token)
    packed, land = _small_gather_wait(send_s, recv_s, small_arrays, token)
    tot, g_ada_w, loss, qk = _reduce_small(packed, land, silu_c)
    g_b_in, g_b_out = finish_grads("b", tot)
    update("b_w_in", g_b_in, copy_grad=True)
    update("b_w_out", g_b_out, copy_grad=True)
    token = sum_grads("a", outs["b_w_in"][1])
    cw = D_MODEL // N_CHIPS
    g_small = dict(
        norm_g=tot[0:2], a_conv_b=tot[2:3], a_ln_g=tot[3:4], a_ln_b=tot[4:5],
        b_q_norm=qk[0:3], b_k_norm=qk[3:6],
        ada_b=jnp.stack([tot[12:16, :ns].reshape(3 * D_MODEL), tot[16:20, :ns].reshape(3 * D_MODEL)]),
        a_conv_w=lax.dynamic_slice(tot[20:20 + CONV_WIDTH], (0, chip * cw), (CONV_WIDTH, cw)),
    )
    update("ada_w", g_ada_w.reshape(2 * D_MODEL, ns), after=token)
    for k, g2 in g_small.items():
        update(k, g2, after=token)
    g_a_in, g_a_out = finish_grads("a", outs["ada_w"][1])
    update("a_w_in", g_a_in, copy_grad=True)
    update("a_w_out", g_a_out, copy_grad=True)
    return (loss.reshape(()), grad_x[None], *[outs[k][0] for k in order], *[outs[k][1] for k in order],
            *[outs[k][2] for k in order], *[outs[k][3] for k in order])
```

```python
import functools

import jax
import jax.numpy as jnp
from jax import lax
from jax.experimental import pallas as pl
from jax.experimental.pallas import tpu as pltpu

F32 = jnp.float32
BF16 = jnp.bfloat16

SEQ = 2048
D_MODEL = 1024
CONV_WIDTH = 31
HEAD_DIM = 64
N_HEADS = 16
DILATIONS = (1, 4, 16)
ATTN_BLOCK = 128
NORM_EPS = 1e-6
NEG_INF = -1e30
N_DEV = 8
N_CHIPS = 4

ADAM_LR = 0.001
ADAM_B1 = 0.9
ADAM_B2 = 0.999
ADAM_EPS = 1e-08
ADAM_WD = 0.01
ADAM_STEP = 10

VMEM_LIMIT_BYTES = 52 * 1024 * 1024
HALO = 32
LANES = 128
MESH = pl.DeviceIdType.MESH


def _params(*sem):
    return pltpu.CompilerParams(dimension_semantics=sem or None, vmem_limit_bytes=VMEM_LIMIT_BYTES)


def _sigmoid(v):
    return 1.0 / (1.0 + jnp.exp(-v))


def _row_spec(tm, cols, col_block=0):
    return pl.BlockSpec((tm, cols), lambda i: (i, col_block))


def _vec_spec(rows, cols):
    return pl.BlockSpec((rows, cols), lambda i: (0, 0))


def _normmod(xv, g, scale, shift):
    r = lax.rsqrt(jnp.mean(xv * xv, axis=-1, keepdims=True) + NORM_EPS)
    return xv * r * g * (1.0 + scale) + shift


def _normmod_fwd(x, g, scale, shift, name):
    tm = 256

    def body(x_ref, g_ref, sc_ref, sh_ref, h_ref, ht_ref):
        h = _normmod(x_ref[...], g_ref[...], sc_ref[...], sh_ref[...])
        h_ref[...] = h.astype(BF16)
        ht_ref[...] = h.T.astype(BF16)

    return pl.pallas_call(
        body, name=name, grid=(SEQ // tm,),
        in_specs=[_row_spec(tm, D_MODEL)] + [_vec_spec(1, D_MODEL)] * 3,
        out_specs=[_row_spec(tm, D_MODEL), pl.BlockSpec((D_MODEL, tm), lambda i: (0, i))],
        out_shape=[jax.ShapeDtypeStruct((SEQ, D_MODEL), BF16), jax.ShapeDtypeStruct((D_MODEL, SEQ), BF16)],
        compiler_params=_params("parallel"),
    )(x, g, scale, shift)


def _normmod_bwd(x, g, scale, dh_parts, dres, name, part_dilations=None, gated=None):
    tm = 256
    n_parts = len(dh_parts)
    dils = part_dilations or (1,) * n_parts
    dh_parts = [p if d == 1 else p.reshape(d, SEQ // d, D_MODEL) for p, d in zip(dh_parts, dils)]
    n_gated = 0 if gated is None else 2

    def body(x_ref, g_ref, sc_ref, dres_ref, *rest):
        part_refs = rest[:n_parts]
        gated_refs = rest[n_parts:n_parts + n_gated]
        out_refs = rest[n_parts + n_gated:]
        dx_ref, sums_ref, nat = out_refs[0], out_refs[1], out_refs[-1]
        xv = x_ref[...]
        r = lax.rsqrt(jnp.mean(xv * xv, axis=-1, keepdims=True) + NORM_EPS)
        xn = xv * r
        dh = _load_natural(part_refs[0], nat, dils[0])
        for p, d in zip(part_refs[1:], dils[1:]):
            dh = dh + _load_natural(p, nat, d)
        gv = g_ref[...]
        one_sc = 1.0 + sc_ref[...]
        dxn = dh * (gv * one_sc)
        dx = dres_ref[...] + r * (dxn - xn * jnp.mean(dxn * xn, axis=-1, keepdims=True))
        dx_ref[...] = dx
        dhx = dh * xn
        rows = [jnp.sum(dhx, axis=0, keepdims=True) * one_sc,
                jnp.sum(dhx, axis=0, keepdims=True) * gv,
                jnp.sum(dh, axis=0, keepdims=True)]
        if gated is not None:
            gate_ref, y_ref = gated_refs
            out_refs[2][...] = (dx * gate_ref[...]).astype(BF16)
            rows.append(jnp.sum(dx * y_ref[...], axis=0, keepdims=True))
        sums = jnp.concatenate(rows + [jnp.zeros((8 - len(rows), D_MODEL), F32)], axis=0)

        @pl.when(pl.program_id(0) == 0)
        def _():
            sums_ref[...] = jnp.zeros_like(sums_ref)

        sums_ref[...] += sums

    gated_specs = [] if gated is None else [_vec_spec(1, D_MODEL), _row_spec(tm, D_MODEL)]
    dy_spec = [] if gated is None else [_row_spec(tm, D_MODEL)]
    dy_shape = [] if gated is None else [jax.ShapeDtypeStruct((SEQ, D_MODEL), BF16)]
    return pl.pallas_call(
        body, name=name, grid=(SEQ // tm,),
        in_specs=[_row_spec(tm, D_MODEL), _vec_spec(1, D_MODEL), _vec_spec(1, D_MODEL), _row_spec(tm, D_MODEL)]
        + [_class_spec(tm, d) for d in dils] + gated_specs,
        out_specs=[_row_spec(tm, D_MODEL), _vec_spec(8, D_MODEL)] + dy_spec,
        out_shape=[jax.ShapeDtypeStruct((SEQ, D_MODEL), F32), jax.ShapeDtypeStruct((8, D_MODEL), F32)] + dy_shape,
        scratch_shapes=[_natural_scratch(tm)],
        compiler_params=_params("arbitrary"),
    )(x, g, scale, dres, *dh_parts, *(gated or ()))


def _mm(lhs, rhs, *, tn, tile0, n_tiles, out_dtype, name, out3d=None, prev=None, transpose_lhs=False):
    mo, kc = lhs.shape[::-1] if transpose_lhs else lhs.shape
    cm = min(mo, 1024)
    tc = 256

    def body(l_ref, r_ref, *rest):
        if transpose_lhs:
            o_ref, lt_ref = rest[-2], rest[-1]

            @pl.when(pl.program_id(0) == 0)
            def _():
                for c in range(kc // tc):
                    lt_ref[:, c * tc:(c + 1) * tc] = l_ref[c * tc:(c + 1) * tc, :].astype(F32).T.astype(l_ref.dtype)
        else:
            o_ref, lt_ref = rest[-1], l_ref
        for m in range(mo // cm):
            rows = pl.ds(m * cm, cm)
            o_ref[rows, :] = jnp.dot(lt_ref[rows, :], r_ref[...], preferred_element_type=F32).astype(out_dtype)

    if rhs.ndim == 3:
        tps_r = rhs.shape[2] // tn
        r_spec = pl.BlockSpec((None, kc, tn), lambda t: ((tile0 + t) // tps_r, 0, (tile0 + t) % tps_r))
    else:
        r_spec = pl.BlockSpec((kc, tn), lambda t: (0, t))
    in_specs = [pl.BlockSpec(lhs.shape, lambda t: (0, 0)), r_spec]
    args = [lhs, rhs]
    aliases = {}
    if out3d is None:
        o_spec = pl.BlockSpec((mo, tn), lambda t: (0, t))
        o_shape = jax.ShapeDtypeStruct((mo, n_tiles * tn), out_dtype)
    else:
        j_out, ns_out = out3d
        tps_o = ns_out // tn
        o_spec = pl.BlockSpec((None, mo, tn), lambda t: ((tile0 + t) // tps_o, 0, (tile0 + t) % tps_o))
        o_shape = jax.ShapeDtypeStruct((j_out, mo, ns_out), out_dtype)
        if prev is not None:
            in_specs.append(pl.BlockSpec(memory_space=pl.ANY))
            args.append(prev)
            aliases = {2: 0}
    return pl.pallas_call(
        body, name=name, grid=(n_tiles,), in_specs=in_specs, out_specs=o_spec, out_shape=o_shape,
        input_output_aliases=aliases,
        scratch_shapes=[pltpu.VMEM((mo, kc), lhs.dtype)] if transpose_lhs else [],
        compiler_params=_params("arbitrary" if transpose_lhs else "parallel"),
    )(*args)


def _b_in_tiles(h_parts, w3, tile_ids, n_tiles, name, prev=None):
    _, kc, ns = w3.shape
    tps = ns // B_TN
    cm = 1024

    def body(ids_ref, h0_ref, h1_ref, h2_ref, w_ref, *rest):
        o_ref = rest[-1]
        out_tile = ids_ref[1, pl.program_id(0)]
        group = jnp.where(out_tile >= B_Z_TILE0, 0, out_tile // B_GROUP_TILES)
        for g, h_ref in enumerate((h0_ref, h1_ref, h2_ref)):
            @pl.when(group == g)
            def _():
                for m in range(SEQ // cm):
                    rows = pl.ds(m * cm, cm)
                    o_ref[rows, :] = jnp.dot(h_ref[rows, :], w_ref[...], preferred_element_type=F32).astype(BF16)

    resident = pl.BlockSpec((SEQ, kc), lambda t, ids: (0, 0))
    in_specs = [resident] * 3 + [pl.BlockSpec((None, kc, B_TN), lambda t, ids: (ids[0, t] // tps, 0, ids[0, t] % tps))]
    args = [*h_parts, w3]
    aliases = {}
    if prev is not None:
        in_specs.append(pl.BlockSpec(memory_space=pl.ANY))
        args.append(prev)
        aliases = {5: 0}
    return pl.pallas_call(
        body, name=name,
        grid_spec=pltpu.PrefetchScalarGridSpec(
            num_scalar_prefetch=1, grid=(n_tiles,), in_specs=in_specs,
            out_specs=pl.BlockSpec((SEQ, B_TN), lambda t, ids: (0, ids[1, t]))),
        out_shape=jax.ShapeDtypeStruct((SEQ, B_TILES * B_TN), BF16),
        input_output_aliases=aliases, compiler_params=_params("arbitrary"),
    )(tile_ids, *args)


def _mm_nt(dy, w3, *, tn, tile0, n_tiles, name, after=None):
    m_rows = dy.shape[0]
    _, kc, ns = w3.shape
    tps = ns // tn
    cm = 512
    extra = [] if after is None else [after]

    def body(dy_ref, w_ref, *rest):
        o_ref = rest[-1]

        @pl.when(pl.program_id(0) == 0)
        def _():
            o_ref[...] = jnp.zeros_like(o_ref)

        for m in range(m_rows // cm):
            rows = pl.ds(m * cm, cm)
            o_ref[rows, :] += lax.dot_general(dy_ref[rows, :], w_ref[...], (((1,), (1,)), ((), ())),
                                              preferred_element_type=F32)

    return pl.pallas_call(
        body, name=name, grid=(n_tiles,),
        in_specs=[pl.BlockSpec((m_rows, tn), lambda t: (0, t)),
                  pl.BlockSpec((None, kc, tn), lambda t: ((tile0 + t) // tps, 0, (tile0 + t) % tps))]
        + [pl.BlockSpec(memory_space=pl.ANY)] * len(extra),
        out_specs=pl.BlockSpec((m_rows, kc), lambda t: (0, 0)),
        out_shape=jax.ShapeDtypeStruct((m_rows, kc), F32),
        compiler_params=_params("arbitrary"),
    )(dy, w3, *extra)


CONV_CHUNK = 16


def _shift_copies(buf, shifted):
    rows = shifted.shape[1]
    for s in range(1, 8):
        shifted[s - 1] = buf[pl.ds(s, rows), :]


def _shifted_rows(buf, shifted, offset, r0):
    s = offset % 8
    if s == 0:
        return buf[pl.ds(r0 + offset, CONV_CHUNK), :]
    return shifted[s - 1, pl.ds(r0 + (offset - s), CONV_CHUNK), :]


def _spread_taps(w_ref, taps):
    for k in range(CONV_WIDTH):
        taps[k] = jnp.broadcast_to(w_ref[k:k + 1, :], (8, D_MODEL))


def _times_tap(taps, k, rows):
    return (rows.reshape(CONV_CHUNK // 8, 8, D_MODEL) * taps[k][None]).reshape(CONV_CHUNK, D_MODEL)


def _conv_fwd(proj, conv_w, conv_b, ln_g, ln_b, name):
    tm = 256
    hb = tm // HALO

    def body(vg_ref, halo_ref, z_ref, w_ref, b_ref, g_ref, be_ref, u5_ref, u5t_ref, u2_ref, buf, shifted, taps):
        i = pl.program_id(0)
        u1 = vg_ref[:, :D_MODEL] * _sigmoid(vg_ref[:, D_MODEL:])
        u1h = halo_ref[:, :D_MODEL] * _sigmoid(halo_ref[:, D_MODEL:])
        buf[pl.ds(0, HALO), :] = jnp.where(i > 0, u1h, 0.0)
        buf[pl.ds(HALO, tm), :] = u1
        _shift_copies(buf, shifted)
        _spread_taps(w_ref, taps)

        def chunk(ci, carry):
            r0 = pl.multiple_of(ci * CONV_CHUNK, CONV_CHUNK)
            acc = jnp.broadcast_to(b_ref[...], (CONV_CHUNK, D_MODEL))
            for k in range(CONV_WIDTH):
                acc = acc + _times_tap(taps, k, _shifted_rows(buf, shifted, HALO - (CONV_WIDTH - 1) + k, r0))
            u2_ref[pl.ds(r0, CONV_CHUNK), :] = acc
            return carry

        lax.fori_loop(0, tm // CONV_CHUNK, chunk, 0)
        acc = u2_ref[...]
        mu = jnp.mean(acc, axis=-1, keepdims=True)
        xc = acc - mu
        rstd = lax.rsqrt(jnp.mean(xc * xc, axis=-1, keepdims=True) + NORM_EPS)
        u3 = xc * rstd * g_ref[...] + be_ref[...]
        zv = z_ref[...]
        u5 = u3 * _sigmoid(u3) * (zv * _sigmoid(zv))
        u5_ref[...] = u5.astype(BF16)
        u5t_ref[...] = u5.T.astype(BF16)

    return pl.pallas_call(
        body, name=name, grid=(SEQ // tm,),
        in_specs=[pl.BlockSpec((tm, 2 * D_MODEL), lambda i: (i, 0)),
                  pl.BlockSpec((HALO, 2 * D_MODEL), lambda i: (jnp.maximum(i * hb - 1, 0), 0)),
                  _row_spec(tm, D_MODEL, 2),
                  _vec_spec(CONV_WIDTH, D_MODEL)] + [_vec_spec(1, D_MODEL)] * 3,
        out_specs=[_row_spec(tm, D_MODEL), pl.BlockSpec((D_MODEL, tm), lambda i: (0, i)), _row_spec(tm, D_MODEL)],
        out_shape=[jax.ShapeDtypeStruct((SEQ, D_MODEL), BF16), jax.ShapeDtypeStruct((D_MODEL, SEQ), BF16),
                   jax.ShapeDtypeStruct((SEQ, D_MODEL), F32)],
        scratch_shapes=[pltpu.VMEM((HALO + tm, D_MODEL), F32), pltpu.VMEM((7, HALO + tm - 8, D_MODEL), F32),
                        pltpu.VMEM((CONV_WIDTH, 8, D_MODEL), F32)],
        compiler_params=_params("parallel"),
    )(proj, proj, proj, conv_w, conv_b, ln_g, ln_b)


def _conv_bwd_pointwise(dy, w_out, proj, u2, ln_g, ln_b, name):
    tm = 256

    def body(dy_ref, w_ref, z_ref, u2_ref, g_ref, be_ref, du2_ref, dz_ref, sums_ref):
        u2v = u2_ref[...]
        mu = jnp.mean(u2v, axis=-1, keepdims=True)
        xc = u2v - mu
        rstd = lax.rsqrt(jnp.mean(xc * xc, axis=-1, keepdims=True) + NORM_EPS)
        xhat = xc * rstd
        u3 = xhat * g_ref[...] + be_ref[...]
        s3 = _sigmoid(u3)
        u4 = u3 * s3
        zv = z_ref[...]
        sz = _sigmoid(zv)
        du5v = lax.dot_general(dy_ref[...], w_ref[...], NT_DIMS, preferred_element_type=F32)
        dz_ref[...] = du5v * u4 * (sz * (1.0 + zv * (1.0 - sz)))
        du3 = du5v * (zv * sz) * (s3 * (1.0 + u3 * (1.0 - s3)))
        dxhat = du3 * g_ref[...]
        du2 = rstd * (dxhat - jnp.mean(dxhat, axis=-1, keepdims=True)
                      - xhat * jnp.mean(dxhat * xhat, axis=-1, keepdims=True))
        du2_ref[...] = du2
        sums = jnp.concatenate([
            jnp.sum(du3 * xhat, axis=0, keepdims=True),
            jnp.sum(du3, axis=0, keepdims=True),
            jnp.sum(du2, axis=0, keepdims=True),
            jnp.zeros((5, D_MODEL), F32)], axis=0)

        @pl.when(pl.program_id(0) == 0)
        def _():
            sums_ref[...] = jnp.zeros_like(sums_ref)

        sums_ref[...] += sums

    return pl.pallas_call(
        body, name=name, grid=(SEQ // tm,),
        in_specs=[_row_spec(tm, D_MODEL), _vec_spec(D_MODEL, D_MODEL), _row_spec(tm, D_MODEL, 2),
                  _row_spec(tm, D_MODEL), _vec_spec(1, D_MODEL), _vec_spec(1, D_MODEL)],
        out_specs=[_row_spec(tm, D_MODEL), _row_spec(tm, D_MODEL), _vec_spec(8, D_MODEL)],
        out_shape=[jax.ShapeDtypeStruct((SEQ, D_MODEL), F32), jax.ShapeDtypeStruct((SEQ, D_MODEL), F32),
                   jax.ShapeDtypeStruct((8, D_MODEL), F32)],
        compiler_params=_params("arbitrary"),
    )(dy, w_out, proj, u2, ln_g, ln_b)


def _conv_bwd_taps(du2, dz, proj, conv_w, name):
    tm = 256
    hb = tm // HALO
    n_blocks = SEQ // tm

    def body(du2_ref, dnext_ref, dz_ref, vg_ref, w_ref, dproj_ref, dw_ref, dbuf, dshift, sgbuf, ubuf, dwacc, taps):
        i = pl.program_id(0)
        _spread_taps(w_ref, taps)
        sg = _sigmoid(vg_ref[:, D_MODEL:])
        sgbuf[...] = sg
        ubuf[...] = vg_ref[:, :D_MODEL] * sg
        dbuf[pl.ds(0, tm), :] = du2_ref[...]
        dbuf[pl.ds(tm, HALO), :] = jnp.where(i < n_blocks - 1, dnext_ref[...], 0.0)
        _shift_copies(dbuf, dshift)

        @pl.when(i == 0)
        def _():
            dwacc[...] = jnp.zeros_like(dwacc)

        def chunk(ci, carry):
            r0 = pl.multiple_of(ci * CONV_CHUNK, CONV_CHUNK)
            rows = pl.ds(r0, CONV_CHUNK)
            u1c = ubuf[rows, :]
            du1 = jnp.zeros((CONV_CHUNK, D_MODEL), F32)
            for k in range(CONV_WIDTH):
                ahead = _shifted_rows(dbuf, dshift, CONV_WIDTH - 1 - k, r0)
                du1 = du1 + _times_tap(taps, k, ahead)
                prod = u1c * ahead
                dwacc[k] += prod[0:8] + prod[8:16]
            sgc = sgbuf[rows, :]
            dval = du1 * sgc
            dproj_ref[rows, 0:D_MODEL] = dval.astype(BF16)
            dproj_ref[rows, D_MODEL:2 * D_MODEL] = (dval * vg_ref[rows, 0:D_MODEL] * (1.0 - sgc)).astype(BF16)
            return carry

        lax.fori_loop(0, tm // CONV_CHUNK, chunk, 0)
        dproj_ref[:, 2 * D_MODEL:] = dz_ref[...].astype(BF16)

        @pl.when(i == n_blocks - 1)
        def _():
            for k in range(CONV_WIDTH):
                dw_ref[k:k + 1, :] = jnp.sum(dwacc[k], axis=0, keepdims=True)
            dw_ref[CONV_WIDTH:, :] = jnp.zeros((32 - CONV_WIDTH, D_MODEL), F32)

    return pl.pallas_call(
        body, name=name, grid=(n_blocks,),
        in_specs=[_row_spec(tm, D_MODEL),
                  pl.BlockSpec((HALO, D_MODEL), lambda i: (jnp.minimum((i + 1) * hb, SEQ // HALO - 1), 0)),
                  _row_spec(tm, D_MODEL),
                  pl.BlockSpec((tm, 2 * D_MODEL), lambda i: (i, 0)),
                  _vec_spec(CONV_WIDTH, D_MODEL)],
        out_specs=[_row_spec(tm, 3 * D_MODEL), _vec_spec(32, D_MODEL)],
        out_shape=[jax.ShapeDtypeStruct((SEQ, 3 * D_MODEL), BF16), jax.ShapeDtypeStruct((32, D_MODEL), F32)],
        scratch_shapes=[pltpu.VMEM((tm + HALO, D_MODEL), F32), pltpu.VMEM((7, HALO + tm - 8, D_MODEL), F32),
                        pltpu.VMEM((tm, D_MODEL), F32), pltpu.VMEM((tm, D_MODEL), F32),
                        pltpu.VMEM((CONV_WIDTH, 8, D_MODEL), F32), pltpu.VMEM((CONV_WIDTH, 8, D_MODEL), F32)],
        compiler_params=_params("arbitrary"),
    )(du2, du2, dz, proj, conv_w)


def _out_a(u5, w_out, x, gate, g1, scale1, shift1, name):
    tm = 256
    n_d = len(DILATIONS)

    def body(u_ref, w_ref, x_ref, gate_ref, g_ref, sc_ref, sh_ref, x1_ref, y_ref, ht_ref, *rest):
        h_refs, nat = rest[:n_d], rest[-1]
        y = jnp.dot(u_ref[...], w_ref[...], preferred_element_type=F32)
        x1 = x_ref[...] + gate_ref[...] * y
        y_ref[...] = y
        x1_ref[...] = x1
        h = _normmod(x1, g_ref[...], sc_ref[...], sh_ref[...])
        ht_ref[...] = h.T.astype(BF16)
        for h_ref, d in zip(h_refs, DILATIONS):
            _store_classes(h_ref, h, nat, d)

    res = pl.pallas_call(
        body, name=name, grid=(SEQ // tm,),
        in_specs=[_row_spec(tm, D_MODEL), _vec_spec(D_MODEL, D_MODEL), _row_spec(tm, D_MODEL)]
        + [_vec_spec(1, D_MODEL)] * 4,
        out_specs=[_row_spec(tm, D_MODEL), _row_spec(tm, D_MODEL), pl.BlockSpec((D_MODEL, tm), lambda i: (0, i))]
        + [_class_spec(tm, d) for d in DILATIONS],
        out_shape=[jax.ShapeDtypeStruct((SEQ, D_MODEL), F32), jax.ShapeDtypeStruct((SEQ, D_MODEL), F32),
                   jax.ShapeDtypeStruct((D_MODEL, SEQ), BF16)] + [_class_shape(d, BF16) for d in DILATIONS],
        scratch_shapes=[_natural_scratch(tm)],
        compiler_params=_params("parallel"),
    )(u5, w_out, x, gate, g1, scale1, shift1)
    return res[0], res[1], res[2], [a.reshape(SEQ, D_MODEL) for a in res[3:]]


def _out_b_loss(u, w_out, x1, gate, target, name):
    tm = 256

    def body(u_ref, w_ref, x_ref, gate_ref, t_ref, e_ref, dy_ref, sums_ref):
        y = jnp.dot(u_ref[...], w_ref[...], preferred_element_type=F32)
        diff = x_ref[...] + gate_ref[...] * y - t_ref[...]
        e = diff * (1.0 / D_MODEL)
        e_ref[...] = e
        dy_ref[...] = (e * gate_ref[...]).astype(BF16)
        sums = jnp.concatenate([
            jnp.sum(e * y, axis=0, keepdims=True),
            jnp.sum(diff * diff, axis=0, keepdims=True),
            jnp.zeros((6, D_MODEL), F32)], axis=0)

        @pl.when(pl.program_id(0) == 0)
        def _():
            sums_ref[...] = jnp.zeros_like(sums_ref)

        sums_ref[...] += sums

    return pl.pallas_call(
        body, name=name, grid=(SEQ // tm,),
        in_specs=[_row_spec(tm, D_MODEL), _vec_spec(D_MODEL, D_MODEL), _row_spec(tm, D_MODEL),
                  _vec_spec(1, D_MODEL), _row_spec(tm, D_MODEL)],
        out_specs=[_row_spec(tm, D_MODEL), _row_spec(tm, D_MODEL), _vec_spec(8, D_MODEL)],
        out_shape=[jax.ShapeDtypeStruct((SEQ, D_MODEL), F32), jax.ShapeDtypeStruct((SEQ, D_MODEL), BF16),
                   jax.ShapeDtypeStruct((8, D_MODEL), F32)],
        compiler_params=_params("arbitrary"),
    )(u, w_out, x1, gate, target)


def _seg_matrix():
    r = lax.broadcasted_iota(jnp.int32, (256, 256), 0) // HEAD_DIM
    c = lax.broadcasted_iota(jnp.int32, (256, 256), 1) // HEAD_DIM
    return (r == c).astype(BF16)


def _segsum(v, seg):
    hi = v.astype(BF16)
    lo = (v - hi.astype(F32)).astype(BF16)
    outs = []
    for c0 in range(0, D_MODEL, 256):
        outs.append(jnp.dot(hi[:, c0:c0 + 256], seg, preferred_element_type=F32)
                    + jnp.dot(lo[:, c0:c0 + 256], seg, preferred_element_type=F32))
    return jnp.concatenate(outs, axis=1)


def _qk_rstd(v, seg):
    return lax.rsqrt(_segsum(v * v, seg) * (1.0 / HEAD_DIM) + NORM_EPS)


def _qknorm_fwd(proj, group, qw, kw, seg, name):
    tm = 256

    def body(q_in, k_in, qw_ref, kw_ref, seg_ref, q_ref, k_ref):
        segv = seg_ref[...]
        q = q_in[...].astype(F32)
        k = k_in[...].astype(F32)
        q_ref[...] = (q * _qk_rstd(q, segv) * qw_ref[...]).astype(BF16)
        k_ref[...] = (k * _qk_rstd(k, segv) * kw_ref[...]).astype(BF16)

    return pl.pallas_call(
        body, name=name, grid=(SEQ // tm,),
        in_specs=[_row_spec(tm, D_MODEL, 3 * group), _row_spec(tm, D_MODEL, 3 * group + 1),
                  _vec_spec(1, D_MODEL), _vec_spec(1, D_MODEL), _vec_spec(256, 256)],
        out_specs=[_row_spec(tm, D_MODEL)] * 2,
        out_shape=[jax.ShapeDtypeStruct((SEQ, D_MODEL), BF16)] * 2,
        compiler_params=_params("parallel"),
    )(proj, proj, qw, kw, seg)


def _attn_masks(b, bpc, dilation, slope):
    if bpc == 1:
        qi = lax.broadcasted_iota(jnp.int32, (ATTN_BLOCK, ATTN_BLOCK), 0)
        kj = lax.broadcasted_iota(jnp.int32, (ATTN_BLOCK, ATTN_BLOCK), 1)
        steps = qi - kj
        return (steps * dilation).astype(F32), steps >= 0
    qi = lax.broadcasted_iota(jnp.int32, (ATTN_BLOCK, 2 * ATTN_BLOCK), 0)
    kj = lax.broadcasted_iota(jnp.int32, (ATTN_BLOCK, 2 * ATTN_BLOCK), 1)
    steps = qi + ATTN_BLOCK - kj
    has_prev = (b % bpc) != 0
    valid = (steps >= 0) & (steps <= ATTN_BLOCK) & (has_prev | (kj >= ATTN_BLOCK))
    return (steps * dilation).astype(F32), valid


def _key_tile(prev_ref, cur_ref, cols, bpc):
    if bpc == 1:
        return cur_ref[:, cols]
    return jnp.concatenate([prev_ref[:, cols], cur_ref[:, cols]], axis=0)


ATTN_HEADS_FWD = 16
ATTN_HEADS_BWD = 16
NT_DIMS = (((1,), (1,)), ((), ()))
BATCH_NT_DIMS = (((2,), (2,)), ((0,), (0,)))
BATCH_NN_DIMS = (((2,), (1,)), ((0,), (0,)))
BATCH_TN_DIMS = (((1,), (1,)), ((0,), (0,)))


def _head_stack(tile_of, heads):
    return jnp.stack([tile_of(slice(h * HEAD_DIM, (h + 1) * HEAD_DIM)) for h in range(heads)], axis=0)


def _attn_specs(heads, segment=0):
    width = heads * HEAD_DIM
    off = segment * (D_MODEL // width)
    last = SEQ // ATTN_BLOCK - 1
    cur = pl.BlockSpec((ATTN_BLOCK, width), lambda hg, b: (jnp.minimum(b, last), hg + off))
    prev = pl.BlockSpec((ATTN_BLOCK, width), lambda hg, b: (jnp.clip(b - 1, 0, last), hg + off))
    return cur, prev


def _attn_fwd(q, k, proj, group, slopes, dilation, name):
    bpc = SEQ // dilation // ATTN_BLOCK
    heads = ATTN_HEADS_FWD
    assert heads == N_HEADS
    cur, prev = _attn_specs(heads)
    v_cur, v_prev = _attn_specs(heads, segment=3 * group + 2)
    scale = HEAD_DIM ** -0.5

    def body(sl_ref, q_ref, kp_ref, kc_ref, vp_ref, vc_ref, o_ref, lse_ref):
        dist, valid = _attn_masks(pl.program_id(1), bpc, dilation, None)
        q3 = _head_stack(lambda cols: q_ref[:, cols], heads)
        k3 = _head_stack(lambda cols: _key_tile(kp_ref, kc_ref, cols, bpc), heads)
        v3 = _head_stack(lambda cols: _key_tile(vp_ref, vc_ref, cols, bpc), heads)
        s = lax.dot_general(q3, k3, BATCH_NT_DIMS, preferred_element_type=F32)
        s = jnp.where(valid[None], s * scale - dist[None] * sl_ref[...], NEG_INF)
        m = jnp.max(s, axis=-1, keepdims=True)
        p = jnp.exp(s - m)
        l = jnp.sum(p, axis=-1, keepdims=True)
        o3 = lax.dot_general(p.astype(BF16), v3, BATCH_NN_DIMS, preferred_element_type=F32) / l
        lse3 = m + jnp.log(l)
        for h in range(heads):
            o_ref[:, h * HEAD_DIM:(h + 1) * HEAD_DIM] = o3[h]
        lse_ref[...] = jnp.concatenate([lse3[h] for h in range(heads)]
                                       + [jnp.zeros((ATTN_BLOCK, LANES - heads), F32)], axis=1)

    return pl.pallas_call(
        body, name=name, grid=(N_HEADS // heads, SEQ // ATTN_BLOCK),
        in_specs=[pl.BlockSpec((heads, 1, 1), lambda hg, b: (hg, 0, 0)), cur, prev, cur, v_prev, v_cur],
        out_specs=[cur, pl.BlockSpec((ATTN_BLOCK, LANES), lambda hg, b: (b, 0))],
        out_shape=[jax.ShapeDtypeStruct((SEQ, D_MODEL), F32), jax.ShapeDtypeStruct((SEQ, LANES), F32)],
        compiler_params=_params("parallel", "parallel"),
    )(slopes.reshape(N_HEADS, 1, 1), q, k, k, proj, proj)


def _class_spec(tm, dilation, width=D_MODEL):
    if dilation == 1:
        return _row_spec(tm, width)
    return pl.BlockSpec((dilation, tm // dilation, width), lambda i: (0, i, 0))


def _class_shape(dilation, dtype, width=D_MODEL):
    if dilation == 1:
        return jax.ShapeDtypeStruct((SEQ, width), dtype)
    return jax.ShapeDtypeStruct((dilation, SEQ // dilation, width), dtype)


def _load_natural(in_ref, nat_ref, dilation):
    if dilation == 1:
        return in_ref[...].astype(F32)
    n = nat_ref.shape[1] // dilation
    tiles = in_ref.shape[-1] // LANES
    for r in range(dilation):
        for j in range(tiles):
            nat_ref.at[j][pl.ds(r, n, stride=dilation), :] = in_ref[r, :, j * LANES:(j + 1) * LANES].astype(F32)
    if tiles == 1:
        return nat_ref[0]
    return jnp.concatenate([nat_ref[j] for j in range(tiles)], axis=1)


def _store_classes(out_ref, value, nat_ref, dilation):
    if dilation == 1:
        out_ref[...] = value.astype(out_ref.dtype)
        return
    n = nat_ref.shape[1] // dilation
    tiles = value.shape[-1] // LANES
    for j in range(tiles):
        nat_ref[j] = value[:, j * LANES:(j + 1) * LANES]
    for r in range(dilation):
        for j in range(tiles):
            out_ref[r, :, j * LANES:(j + 1) * LANES] = (
                nat_ref.at[j][pl.ds(r, n, stride=dilation), :].astype(out_ref.dtype))


def _natural_scratch(tm):
    return pltpu.VMEM((D_MODEL // LANES, tm, LANES), F32)


def _head_selector():
    lane_head = lax.broadcasted_iota(jnp.int32, (D_MODEL, LANES), 0) // HEAD_DIM
    head = lax.broadcasted_iota(jnp.int32, (D_MODEL, LANES), 1)
    return (lane_head == head).astype(BF16)


def _dot_split(v, m01, dims):
    hi = v.astype(BF16)
    lo = (v - hi.astype(F32)).astype(BF16)
    return (lax.dot_general(hi, m01, dims, preferred_element_type=F32)
            + lax.dot_general(lo, m01, dims, preferred_element_type=F32))


def _merge_fwd(o_parts, lse_parts, z, sel, name):
    tm = 256
    h_spec = pl.BlockSpec((tm, LANES), lambda i: (i, 0))

    def body(o0, o1, o2, l0, l1, l2, z_ref, sel_ref, u_ref, ut_ref, o_ref, lse_ref, nat):
        ls = [_load_natural(l, nat, d) for l, d in zip((l0, l1, l2), DILATIONS)]
        m = jnp.maximum(jnp.maximum(ls[0], ls[1]), ls[2])
        tot = m + jnp.log(jnp.exp(ls[0] - m) + jnp.exp(ls[1] - m) + jnp.exp(ls[2] - m))
        o = jnp.zeros((tm, D_MODEL), F32)
        for o_in, l, d in zip((o0, o1, o2), ls, DILATIONS):
            weight = _dot_split(jnp.exp(l - tot), sel_ref[...], NT_DIMS)
            o = o + weight * _load_natural(o_in, nat, d)
        zv = z_ref[...].astype(F32)
        u = o * (zv * _sigmoid(zv))
        u_ref[...] = u.astype(BF16)
        ut_ref[...] = u.T.astype(BF16)
        o_ref[...] = o
        lse_ref[...] = tot

    return pl.pallas_call(
        body, name=name, grid=(SEQ // tm,),
        in_specs=[_class_spec(tm, d) for d in DILATIONS] + [_class_spec(tm, d, LANES) for d in DILATIONS]
        + [_row_spec(tm, D_MODEL, B_Z_SEGMENT), _vec_spec(D_MODEL, LANES)],
        out_specs=[_row_spec(tm, D_MODEL), pl.BlockSpec((D_MODEL, tm), lambda i: (0, i)),
                   _row_spec(tm, D_MODEL), h_spec],
        out_shape=[jax.ShapeDtypeStruct((SEQ, D_MODEL), BF16), jax.ShapeDtypeStruct((D_MODEL, SEQ), BF16),
                   jax.ShapeDtypeStruct((SEQ, D_MODEL), F32), jax.ShapeDtypeStruct((SEQ, LANES), F32)],
        scratch_shapes=[_natural_scratch(tm)],
        compiler_params=_params("parallel"),
    )(*o_parts, *lse_parts, z, sel)


def _merge_bwd(dy, w_out, o, lse, z, sel, name):
    tm = 256
    n_d = len(DILATIONS)

    def body(dy_ref, w_ref, o_ref, lse_ref, z_ref, sel_ref, dz_ref, *rest):
        do_refs, delta_refs, lse_refs, nat = rest[:n_d], rest[n_d:2 * n_d], rest[2 * n_d:3 * n_d], rest[-1]
        zv = z_ref[...].astype(F32)
        sz = _sigmoid(zv)
        duv = lax.dot_general(dy_ref[...], w_ref[...], NT_DIMS, preferred_element_type=F32)
        ov = o_ref[...]
        do = duv * (zv * sz)
        dz_ref[...] = (duv * ov * (sz * (1.0 + zv * (1.0 - sz)))).astype(BF16)
        delta = _dot_split(do * ov, sel_ref[...], (((1,), (0,)), ((), ())))
        lv = lse_ref[...]
        for i, d in enumerate(DILATIONS):
            _store_classes(do_refs[i], do, nat, d)
            _store_classes(delta_refs[i], delta, nat, d)
            _store_classes(lse_refs[i], lv, nat, d)

    res = pl.pallas_call(
        body, name=name, grid=(SEQ // tm,),
        in_specs=[_row_spec(tm, D_MODEL), _vec_spec(D_MODEL, D_MODEL), _row_spec(tm, D_MODEL), _row_spec(tm, LANES),
                  _row_spec(tm, D_MODEL, B_Z_SEGMENT), _vec_spec(D_MODEL, LANES)],
        out_specs=[_row_spec(tm, D_MODEL)] + [_class_spec(tm, d) for d in DILATIONS]
        + [_class_spec(tm, d, LANES) for d in DILATIONS] * 2,
        out_shape=[jax.ShapeDtypeStruct((SEQ, D_MODEL), BF16)] + [_class_shape(d, BF16) for d in DILATIONS]
        + [_class_shape(d, F32, LANES) for d in DILATIONS] * 2,
        scratch_shapes=[_natural_scratch(tm)],
        compiler_params=_params("parallel"),
    )(dy, w_out, o, lse, z, sel)
    flat = lambda a: a.reshape(SEQ, a.shape[-1])
    return (res[0], [flat(a) for a in res[1:1 + n_d]], [flat(a) for a in res[1 + n_d:1 + 2 * n_d]],
            [flat(a) for a in res[1 + 2 * n_d:]])


def _attn_bwd(q, k, proj, group, do, lse, delta, slopes, dilation, name):
    bpc = SEQ // dilation // ATTN_BLOCK
    heads = ATTN_HEADS_BWD
    n_blocks = SEQ // ATTN_BLOCK
    carry = bpc > 1
    width = heads * HEAD_DIM
    cur, prev = _attn_specs(heads)
    v_cur, v_prev = _attn_specs(heads, segment=3 * group + 2)
    assert heads == N_HEADS
    per_head = pl.BlockSpec((ATTN_BLOCK, LANES), lambda hg, b: (jnp.minimum(b, n_blocks - 1), 0))
    scale = HEAD_DIM ** -0.5

    def body(sl_ref, q_ref, kp_ref, kc_ref, vp_ref, vc_ref, do_ref, lse_ref, dl_ref,
             dq_ref, dk_ref, dv_ref, *scratch):
        b = pl.program_id(1)
        if carry:
            dk_carry, dv_carry = scratch

            @pl.when(b == n_blocks)
            def _():
                dk_ref[...] = dk_carry[...].astype(BF16)
                dv_ref[...] = dv_carry[...].astype(BF16)

            @pl.when(b < n_blocks)
            def _():
                step(sl_ref, q_ref, kp_ref, kc_ref, vp_ref, vc_ref, do_ref, lse_ref, dl_ref,
                     dq_ref, dk_ref, dv_ref, dk_carry, dv_carry, b)
        else:
            step(sl_ref, q_ref, kp_ref, kc_ref, vp_ref, vc_ref, do_ref, lse_ref, dl_ref,
                 dq_ref, dk_ref, dv_ref, None, None, b)

    def step(sl_ref, q_ref, kp_ref, kc_ref, vp_ref, vc_ref, do_ref, lse_ref, dl_ref,
             dq_ref, dk_ref, dv_ref, dk_carry, dv_carry, b):
        if carry:
            @pl.when(b == 0)
            def _():
                dk_carry[...] = jnp.zeros_like(dk_carry)
                dv_carry[...] = jnp.zeros_like(dv_carry)

        dist, valid = _attn_masks(b, bpc, dilation, None)
        q3 = _head_stack(lambda cols: q_ref[:, cols], heads)
        k3 = _head_stack(lambda cols: _key_tile(kp_ref, kc_ref, cols, bpc), heads)
        v3 = _head_stack(lambda cols: _key_tile(vp_ref, vc_ref, cols, bpc), heads)
        do3 = _head_stack(lambda cols: do_ref[:, cols], heads)
        lse3 = jnp.stack([lse_ref[:, h:h + 1] for h in range(heads)], axis=0)
        dl3 = jnp.stack([dl_ref[:, h:h + 1] for h in range(heads)], axis=0)
        s = lax.dot_general(q3, k3, BATCH_NT_DIMS, preferred_element_type=F32)
        p = jnp.exp(jnp.where(valid[None], s * scale - dist[None] * sl_ref[...], NEG_INF) - lse3)
        dp = lax.dot_general(do3, v3, BATCH_NT_DIMS, preferred_element_type=F32)
        ds = (p * (dp - dl3) * scale).astype(BF16)
        dq3 = lax.dot_general(ds, k3, BATCH_NN_DIMS, preferred_element_type=F32)
        dk3 = lax.dot_general(ds, q3, BATCH_TN_DIMS, preferred_element_type=F32)
        dv3 = lax.dot_general(p.astype(BF16), do3, BATCH_TN_DIMS, preferred_element_type=F32)
        for h in range(heads):
            cols = slice(h * HEAD_DIM, (h + 1) * HEAD_DIM)
            dq_ref[:, cols] = dq3[h].astype(BF16)
            if carry:
                dk_ref[:, cols] = (dk_carry[:, cols] + dk3[h, :ATTN_BLOCK]).astype(BF16)
                dv_ref[:, cols] = (dv_carry[:, cols] + dv3[h, :ATTN_BLOCK]).astype(BF16)
                dk_carry[:, cols] = dk3[h, ATTN_BLOCK:]
                dv_carry[:, cols] = dv3[h, ATTN_BLOCK:]
            else:
                dk_ref[:, cols] = dk3[h].astype(BF16)
                dv_ref[:, cols] = dv3[h].astype(BF16)

    kv_out = prev if carry else cur
    return pl.pallas_call(
        body, name=name, grid=(N_HEADS // heads, n_blocks + (1 if carry else 0)),
        in_specs=[pl.BlockSpec((heads, 1, 1), lambda hg, b: (hg, 0, 0)), cur, prev, cur, v_prev, v_cur,
                  cur, per_head, per_head],
        out_specs=[cur, kv_out, kv_out],
        out_shape=[jax.ShapeDtypeStruct((SEQ, D_MODEL), BF16)] * 3,
        scratch_shapes=[pltpu.VMEM((ATTN_BLOCK, width), F32)] * 2 if carry else [],
        compiler_params=_params("parallel", "arbitrary"),
    )(slopes.reshape(N_HEADS, 1, 1), q, k, k, proj, proj, do, lse, delta)


def _qknorm_bwd(proj, group, qw, kw, seg, dq, dk, dv, name):
    tm = 256

    def body(q_in, k_in, qw_ref, kw_ref, seg_ref, dq_ref, dk_ref, dv_ref, dproj_ref, sums_ref):
        segv = seg_ref[...]
        sums = []
        for part, (raw_ref, w_ref, dn_ref) in enumerate(((q_in, qw_ref, dq_ref), (k_in, kw_ref, dk_ref))):
            raw = raw_ref[...].astype(F32)
            dn = dn_ref[...].astype(F32)
            r = _qk_rstd(raw, segv)
            gq = dn * w_ref[...]
            draw = r * gq - raw * (r * r * r) * (_segsum(raw * gq, segv) * (1.0 / HEAD_DIM))
            dproj_ref[:, part * D_MODEL:(part + 1) * D_MODEL] = draw.astype(BF16)
            sums.append(jnp.sum(dn * raw * r, axis=0, keepdims=True))
        dproj_ref[:, 2 * D_MODEL:] = dv_ref[...]

        @pl.when(pl.program_id(0) == 0)
        def _():
            sums_ref[...] = jnp.zeros_like(sums_ref)

        sums_ref[...] += jnp.concatenate(sums + [jnp.zeros((6, D_MODEL), F32)], axis=0)

    return pl.pallas_call(
        body, name=name, grid=(SEQ // tm,),
        in_specs=[_row_spec(tm, D_MODEL, 3 * group), _row_spec(tm, D_MODEL, 3 * group + 1),
                  _vec_spec(1, D_MODEL), _vec_spec(1, D_MODEL), _vec_spec(256, 256)] + [_row_spec(tm, D_MODEL)] * 3,
        out_specs=[_row_spec(tm, 3 * D_MODEL), _vec_spec(8, D_MODEL)],
        out_shape=[jax.ShapeDtypeStruct((SEQ, 3 * D_MODEL), BF16), jax.ShapeDtypeStruct((8, D_MODEL), F32)],
        compiler_params=_params("arbitrary"),
    )(proj, proj, qw, kw, seg, dq, dk, dv)


B_TN = 512
B_GROUP_TILES = 3 * D_MODEL // B_TN
B_Z_TILE0 = 3 * B_GROUP_TILES
B_Z_TILES = D_MODEL // B_TN
B_TILES = B_Z_TILE0 + B_Z_TILES
B_Z_SEGMENT = 3 * len(DILATIONS)


def _local_step(x, target, mods, norm_g, conv_w, conv_b, ln_g, ln_b, q_norm, k_norm, chip, own_wb_in,
                weights_a, weights_b, forward_weights_b, send_grads_b, forward_grads_b, send_grads_a):
    row = lambda a, i: a[i:i + 1]
    shift0, scale0, gate0 = row(mods[0], 0), row(mods[0], 1), row(mods[0], 2)
    shift1, scale1, gate1 = row(mods[1], 0), row(mods[1], 1), row(mods[1], 2)
    g0, g1 = row(norm_g, 0), row(norm_g, 1)
    seg = _seg_matrix()
    slopes = jnp.exp2(-8.0 * jnp.arange(1, N_HEADS + 1, dtype=F32) / N_HEADS)
    qw = [jnp.tile(q_norm[g:g + 1], (1, N_HEADS)) for g in range(3)]
    kw = [jnp.tile(k_norm[g:g + 1], (1, N_HEADS)) for g in range(3)]

    h0, h0t = _normmod_fwd(x, g0, scale0, shift0, "prenorm0")
    wa_in, wa_out = weights_a(h0)
    ja, _, nsa = wa_in.shape
    proj_a = _mm(h0, wa_in, tn=nsa, tile0=0, n_tiles=ja, out_dtype=F32, name="a_in")
    u5, u5t, u2 = _conv_fwd(proj_a, conv_w, conv_b, ln_g, ln_b, "a_conv")
    x1, y_a, h1t, h1c = _out_a(u5, wa_out, x, gate0, g1, scale1, shift1, "a_out")

    own_tiles = B_TILES // N_CHIPS
    step = jnp.arange(B_TILES, dtype=jnp.int32)
    tiles = (own_tiles * chip + step) % B_TILES
    own_ids = jnp.stack([step[:own_tiles], tiles[:own_tiles]])
    rest_ids = jnp.stack([tiles[own_tiles:], tiles[own_tiles:]])
    proj_b = _b_in_tiles(h1c, own_wb_in, own_ids, own_tiles, "b_in_own")
    forward_weights_b(proj_b)
    wb_in, wb_out = weights_b(proj_b)
    jb, _, nsb = wb_in.shape
    proj_b = _b_in_tiles(h1c, wb_in, rest_ids, B_TILES - own_tiles, "b_in_rest", prev=proj_b)
    h1 = h1c[0]
    qkv, o_parts, lse_parts = [], [], []
    for g, d in enumerate(DILATIONS):
        qn, kn = _qknorm_fwd(proj_b, g, qw[g], kw[g], seg, f"b_qknorm_g{g}")
        og, lg = _attn_fwd(qn, kn, proj_b, g, slopes, d, f"b_attn_g{g}")
        qkv.append((qn, kn))
        o_parts.append(og if d == 1 else og.reshape(d, SEQ // d, D_MODEL))
        lse_parts.append(lg if d == 1 else lg.reshape(d, SEQ // d, LANES))
    sel = _head_selector()
    u_b, u_bt, o_b, lse_b = _merge_fwd(o_parts, lse_parts, proj_b, sel, "b_merge")
    e, dy_b, sums_loss = _out_b_loss(u_b, wb_out, x1, gate1, target, "b_out_loss")

    dwb_out = _mm(u_bt, dy_b, tn=D_MODEL, tile0=0, n_tiles=1, out_dtype=BF16, name="b_dwout")
    dz_b, do_c, delta_c, lse_c = _merge_bwd(dy_b, wb_out, o_b, lse_b, proj_b, sel, "b_merge_bwd")
    dwb_in = _mm(h1t, dz_b, tn=B_TN, tile0=B_Z_TILE0, n_tiles=B_Z_TILES, out_dtype=BF16, name="b_dwin_z",
                 out3d=(jb, nsb))
    dh1_parts = [_mm_nt(dz_b, wb_in, tn=B_TN, tile0=B_Z_TILE0, n_tiles=B_Z_TILES, name="b_dh_z")]
    qk_sums = []
    for g, d in enumerate(DILATIONS):
        qn, kn = qkv[g]
        dq, dk, dv = _attn_bwd(qn, kn, proj_b, g, do_c[g], lse_c[g], delta_c[g], slopes, d, f"b_attn_bwd_g{g}")
        dproj, sums_qk = _qknorm_bwd(proj_b, g, qw[g], kw[g], seg, dq, dk, dv, f"b_qknorm_bwd_g{g}")
        qk_sums.append(sums_qk)
        dwb_in = _mm(h1t if d == 1 else h1c[g], dproj, tn=B_TN, tile0=g * B_GROUP_TILES, n_tiles=B_GROUP_TILES,
                     out_dtype=BF16, name=f"b_dwin_g{g}", out3d=(jb, nsb), prev=dwb_in, transpose_lhs=d != 1)
        dh = _mm_nt(dproj, wb_in, tn=B_TN, tile0=g * B_GROUP_TILES, n_tiles=B_GROUP_TILES, name=f"b_dh_g{g}")
        dh1_parts.append(dh)
    token = send_grads_b(dwb_in, dwb_out)
    dx1, sums_n1, dy_a = _normmod_bwd(x1, g1, scale1 + token[0:1, 0:1], dh1_parts, e, "prenorm1_bwd",
                                      part_dilations=(1,) + DILATIONS, gated=(gate0, y_a))
    token = forward_grads_b(dx1)

    dwa_out = _mm(u5t, dy_a, tn=D_MODEL, tile0=0, n_tiles=1, out_dtype=BF16, name="a_dwout")
    du2, dz_a, sums_ln = _conv_bwd_pointwise(dy_a, wa_out, proj_a, u2, ln_g + token[0:1, 0:1], ln_b,
                                             "a_conv_bwd_pw")
    dproj_a, dconv_w = _conv_bwd_taps(du2, dz_a, proj_a, conv_w, "a_conv_bwd_taps")
    dwa_in = _mm(h0t, dproj_a, tn=nsa, tile0=0, n_tiles=ja, out_dtype=BF16, name="a_dwin", out3d=(ja, nsa))
    token = send_grads_a(dwa_in, dwa_out)
    dh0 = _mm_nt(dproj_a, wa_in, tn=nsa, tile0=0, n_tiles=ja, name="a_dh", after=token)
    grad_x, sums_n0 = _normmod_bwd(x, g0, scale0, [dh0], dx1, "prenorm0_bwd")

    small = dict(
        dnorm_g=jnp.concatenate([sums_n0[0:1], sums_n1[0:1]], axis=0),
        dmod0=jnp.concatenate([sums_n0[2:3], sums_n0[1:2], sums_n1[3:4]], axis=0),
        dmod1=jnp.concatenate([sums_n1[2:3], sums_n1[1:2], sums_loss[0:1]], axis=0),
        dln_g=sums_ln[0:1], dln_b=sums_ln[1:2], dconv_b=sums_ln[2:3],
        dconv_w=dconv_w[:CONV_WIDTH],
        dq_norm=jnp.concatenate([s[0:1] for s in qk_sums], axis=0),
        dk_norm=jnp.concatenate([s[1:2] for s in qk_sums], axis=0),
        loss_cols=sums_loss[1:2],
    )
    return grad_x, small


def _adamw(w, g, m, v, name, after=None, copy_grad=False):
    rows, cols = w.shape
    tr = rows if rows <= 128 else 128
    c1 = 1.0 / (1.0 - ADAM_B1 ** ADAM_STEP)
    c2 = 1.0 / (1.0 - ADAM_B2 ** ADAM_STEP)
    extra = [] if after is None else [after]
    n_out = 4 if copy_grad else 3

    def body(w_ref, g_ref, m_ref, v_ref, *rest):
        d_ref, mo_ref, vo_ref = rest[len(extra):len(extra) + 3]
        gv = g_ref[...]
        if copy_grad:
            rest[-1][...] = gv
        mn = ADAM_B1 * m_ref[...] + (1.0 - ADAM_B1) * gv
        vn = ADAM_B2 * v_ref[...] + (1.0 - ADAM_B2) * (gv * gv)
        mo_ref[...] = mn
        vo_ref[...] = vn
        d_ref[...] = -ADAM_LR * ((mn * c1) / (jnp.sqrt(vn * c2) + ADAM_EPS) + ADAM_WD * w_ref[...])

    spec = pl.BlockSpec((tr, cols), lambda i: (i, 0))
    return pl.pallas_call(
        body, name=name, grid=(rows // tr,),
        in_specs=[spec] * 4 + [pl.BlockSpec(memory_space=pl.ANY)] * len(extra), out_specs=[spec] * n_out,
        out_shape=[jax.ShapeDtypeStruct((rows, cols), F32)] * n_out,
        compiler_params=_params("parallel"),
    )(w, g, m, v, *extra)


def _cast_into_slot(w, chip_idx, name, keep_own=False):
    rows, cols = w.shape
    tr = 256

    def body(ch_ref, w_ref, *o_refs):
        wb = w_ref[...].astype(BF16)
        for o_ref in o_refs:
            o_ref[...] = wb

    slot_spec = pl.BlockSpec((None, tr, cols), lambda i, ch: (ch[0], i, 0))
    own_spec = pl.BlockSpec((None, tr, cols), lambda i, ch: (0, i, 0))
    res = pl.pallas_call(
        body, name=name,
        grid_spec=pltpu.PrefetchScalarGridSpec(
            num_scalar_prefetch=1, grid=(rows // tr,),
            in_specs=[pl.BlockSpec((tr, cols), lambda i, ch: (i, 0))],
            out_specs=[slot_spec, own_spec] if keep_own else [slot_spec]),
        out_shape=[jax.ShapeDtypeStruct((N_CHIPS, rows, cols), BF16)]
        + ([jax.ShapeDtypeStruct((1, rows, cols), BF16)] if keep_own else []),
        compiler_params=_params("parallel"),
    )(chip_idx, w)
    return tuple(res) if keep_own else res[0]


def _position():
    x, y, c = lax.axis_index("x"), lax.axis_index("y"), lax.axis_index("c")
    return x, y, c


def _xor_peer(x, y, c, k):
    return (x ^ ((k >> 2) & 1), y ^ ((k >> 1) & 1), c ^ (k & 1))


def _chip_peer(x, y, k):
    return (x ^ ((k >> 1) & 1), y ^ (k & 1))


def _ada_forward(c_row, ada_w, ada_b, conv_w):
    ns = ada_w.shape[2]
    cw = conv_w.shape[1]

    def body(c_ref, w_ref, b_ref, cv_ref, mod_ref, sc_ref, cvo_ref,
             c_all, mp, parts, cv_parts, send1, recv1, send2, recv2, send3, recv3):
        x, y, c = _position()
        me = 4 * x + 2 * y + c
        chip = 2 * x + y

        def c_copy(k):
            return pltpu.make_async_remote_copy(
                src_ref=c_all.at[me], dst_ref=c_all.at[me], send_sem=send1.at[k - 1], recv_sem=recv1.at[k - 1],
                device_id=_xor_peer(x, y, c, k), device_id_type=MESH)

        def cv_copy(k):
            px, py = _chip_peer(x, y, k)
            return pltpu.make_async_remote_copy(
                src_ref=cv_parts.at[chip], dst_ref=cv_parts.at[chip], send_sem=send3.at[k - 1],
                recv_sem=recv3.at[k - 1], device_id=(px, py, c), device_id_type=MESH)

        c_all[me] = c_ref[...]
        cv_parts[chip] = cv_ref[...]
        for k in range(1, N_DEV):
            c_copy(k).start()
        for k in range(1, N_CHIPS):
            cv_copy(k).start()
        for k in range(1, N_DEV):
            c_copy(k).wait_recv()
        cv = jnp.concatenate([c_all[i] for i in range(N_DEV)], axis=0)
        sc = cv * _sigmoid(cv)
        sc_ref[...] = sc
        for l in range(2):
            res = jnp.dot(sc, w_ref[l], preferred_element_type=F32, precision=lax.Precision.HIGHEST)
            for i in range(N_DEV):
                mp[i, l:l + 1, :] = res[i:i + 1, :]

        def mod_copy(k):
            px, py = _chip_peer(x, y, k)
            return pltpu.make_async_remote_copy(
                src_ref=mp.at[4 * px + 2 * py + c], dst_ref=parts.at[chip], send_sem=send2.at[k - 1],
                recv_sem=recv2.at[k - 1], device_id=(px, py, c), device_id_type=MESH)

        for k in range(1, N_CHIPS):
            mod_copy(k).start()
        parts[chip] = mp[me]
        for k in range(1, N_CHIPS):
            mod_copy(k).wait_recv()
            cv_copy(k).wait_recv()
        mod_ref[...] = jnp.concatenate([parts[j] for j in range(N_CHIPS)], axis=1) + b_ref[...]
        cvo_ref[...] = jnp.concatenate([cv_parts[j] for j in range(N_CHIPS)], axis=1)
        for k in range(1, N_DEV):
            c_copy(k).wait_send()
        for k in range(1, N_CHIPS):
            mod_copy(k).wait_send()
            cv_copy(k).wait_send()

    vm = pl.BlockSpec(memory_space=pltpu.VMEM)
    return pl.pallas_call(
        body, name="ada_forward",
        in_specs=[vm] * 4, out_specs=[vm] * 3,
        out_shape=[jax.ShapeDtypeStruct((2, 3 * D_MODEL), F32), jax.ShapeDtypeStruct((N_DEV, D_MODEL), F32),
                   jax.ShapeDtypeStruct((CONV_WIDTH, N_CHIPS * cw), F32)],
        scratch_shapes=[pltpu.VMEM((N_DEV, 1, D_MODEL), F32), pltpu.VMEM((N_DEV, 2, ns), F32),
                        pltpu.VMEM((N_CHIPS, 2, ns), F32), pltpu.VMEM((N_CHIPS, CONV_WIDTH, cw), F32),
                        pltpu.SemaphoreType.DMA((N_DEV - 1,)), pltpu.SemaphoreType.DMA((N_DEV - 1,)),
                        pltpu.SemaphoreType.DMA((N_CHIPS - 1,)), pltpu.SemaphoreType.DMA((N_CHIPS - 1,)),
                        pltpu.SemaphoreType.DMA((N_CHIPS - 1,)), pltpu.SemaphoreType.DMA((N_CHIPS - 1,))],
        compiler_params=pltpu.CompilerParams(vmem_limit_bytes=VMEM_LIMIT_BYTES),
    )(c_row, ada_w, ada_b, conv_w)


HBM_SPEC = pl.BlockSpec(memory_space=pltpu.HBM)
ANY_SPEC = pl.BlockSpec(memory_space=pl.ANY)
SEM_SPEC = pl.BlockSpec(memory_space=pltpu.SEMAPHORE)
SPLIT_PARAMS = dict(compiler_params=pltpu.CompilerParams(has_side_effects=pltpu.SideEffectType.DATAFLOW_SIDE_EFFECTING))
TOKEN = jax.ShapeDtypeStruct((8, 128), F32)


def _hbm(arrays):
    return [pltpu.with_memory_space_constraint(a, pltpu.HBM) for a in arrays]


def _hbm_like(arrays):
    return [pltpu.HBM(a.shape, a.dtype) for a in arrays]


def _gather_start(lands, after, name):
    n = len(lands)

    def body(*refs):
        ins = refs[:n]
        send, recv = refs[n + 1], refs[n + 2]
        x, y, c = _position()
        chip = 2 * x + y
        for t in range(n):
            rh = ins[t].shape[1] // 2
            for k in range(1, N_CHIPS):
                px, py = _chip_peer(x, y, k)
                block = ins[t].at[chip, pl.ds(c * rh, rh)]
                pltpu.make_async_remote_copy(
                    src_ref=block, dst_ref=block, send_sem=send.at[3 * t + k - 1], recv_sem=recv.at[3 * t + k - 1],
                    device_id=(px, py, c), device_id_type=MESH).start()
        refs[-1][...] = jnp.zeros(TOKEN.shape, F32)

    res = pl.pallas_call(
        body, name=name, in_specs=[HBM_SPEC] * n + [ANY_SPEC],
        out_specs=(SEM_SPEC, SEM_SPEC, *[HBM_SPEC] * n, pl.BlockSpec(memory_space=pltpu.VMEM)),
        out_shape=(pltpu.SemaphoreType.DMA((3 * n,)), pltpu.SemaphoreType.DMA((3 * n,)), *_hbm_like(lands), TOKEN),
        input_output_aliases={t: 2 + t for t in range(n)}, **SPLIT_PARAMS,
    )(*_hbm(lands), after)
    return res[0], res[1], list(res[2:2 + n]), res[-1]


def _gather_forward(send, recv, lands, after, name):
    n = len(lands)

    def body(*refs):
        ins = refs[:n]
        send1, recv1 = refs[n], refs[n + 1]
        send2, recv2 = refs[n + 3], refs[n + 4]
        x, y, c = _position()
        chip = 2 * x + y
        for t in range(n):
            rh = ins[t].shape[1] // 2
            half = pl.ds(c * rh, rh)
            for k in range(1, N_CHIPS):
                px, py = _chip_peer(x, y, k)
                s = 3 * t + k - 1
                got = ins[t].at[2 * px + py, half]
                cp = pltpu.make_async_remote_copy(
                    src_ref=ins[t].at[chip, half], dst_ref=got, send_sem=send1.at[s], recv_sem=recv1.at[s],
                    device_id=(px, py, c), device_id_type=MESH)
                cp.wait_send()
                cp.wait_recv()
                pltpu.make_async_remote_copy(
                    src_ref=got, dst_ref=got, send_sem=send2.at[s], recv_sem=recv2.at[s],
                    device_id=(x, y, 1 - c), device_id_type=MESH).start()
        refs[-1][...] = jnp.zeros(TOKEN.shape, F32)

    res = pl.pallas_call(
        body, name=name, in_specs=[HBM_SPEC] * n + [SEM_SPEC, SEM_SPEC, ANY_SPEC],
        out_specs=(SEM_SPEC, SEM_SPEC, *[HBM_SPEC] * n, pl.BlockSpec(memory_space=pltpu.VMEM)),
        out_shape=(pltpu.SemaphoreType.DMA((3 * n,)), pltpu.SemaphoreType.DMA((3 * n,)), *_hbm_like(lands), TOKEN),
        input_output_aliases={t: 2 + t for t in range(n)}, **SPLIT_PARAMS,
    )(*lands, send, recv, after)
    return res[0], res[1], list(res[2:2 + n]), res[-1]


def _gather_wait(send, recv, lands, after, name):
    n = len(lands)

    def body(*refs):
        ins = refs[:n]
        send_ref, recv_ref = refs[n], refs[n + 1]
        x, y, c = _position()
        for t in range(n):
            rh = ins[t].shape[1] // 2
            for k in range(1, N_CHIPS):
                px, py = _chip_peer(x, y, k)
                cp = pltpu.make_async_remote_copy(
                    src_ref=ins[t].at[2 * px + py, pl.ds(c * rh, rh)],
                    dst_ref=ins[t].at[2 * px + py, pl.ds((1 - c) * rh, rh)], send_sem=send_ref.at[3 * t + k - 1],
                    recv_sem=recv_ref.at[3 * t + k - 1], device_id=(x, y, 1 - c), device_id_type=MESH)
                cp.wait_send()
                cp.wait_recv()

    res = pl.pallas_call(
        body, name=name, in_specs=[HBM_SPEC] * n + [SEM_SPEC, SEM_SPEC, ANY_SPEC], out_specs=[HBM_SPEC] * n,
        out_shape=_hbm_like(lands), input_output_aliases={t: t for t in range(n)}, **SPLIT_PARAMS,
    )(*lands, send, recv, after)
    return list(res)


def _split_start(name, arrays, n_sems, after, issue):
    m = len(arrays)

    def body(*refs):
        issue(refs[:m], refs[m + 1], refs[m + 2])
        refs[-1][...] = jnp.zeros(TOKEN.shape, F32)

    res = pl.pallas_call(
        body, name=name, in_specs=[HBM_SPEC] * m + [ANY_SPEC],
        out_specs=(SEM_SPEC, SEM_SPEC, *[HBM_SPEC] * m, pl.BlockSpec(memory_space=pltpu.VMEM)),
        out_shape=(pltpu.SemaphoreType.DMA((n_sems,)), pltpu.SemaphoreType.DMA((n_sems,)), *_hbm_like(arrays), TOKEN),
        input_output_aliases={t: 2 + t for t in range(m)}, **SPLIT_PARAMS,
    )(*_hbm(arrays), after)
    return res[0], res[1], list(res[2:2 + m]), res[-1]


def _split_wait(name, arrays, send, recv, after, await_all):
    m = len(arrays)

    def body(*refs):
        await_all(refs[:m], refs[m], refs[m + 1])

    res = pl.pallas_call(
        body, name=name, in_specs=[HBM_SPEC] * m + [SEM_SPEC, SEM_SPEC, ANY_SPEC], out_specs=[HBM_SPEC] * m,
        out_shape=_hbm_like(arrays), input_output_aliases={t: t for t in range(m)}, **SPLIT_PARAMS,
    )(*arrays, send, recv, after)
    return list(res)


def _sibling_copies(refs, send, recv, n):
    x, y, c = _position()
    cps = []
    for t in range(n):
        rh = refs[t].shape[1] // 2
        cps.append(pltpu.make_async_remote_copy(
            src_ref=refs[t].at[pl.ds(0, N_CHIPS), pl.ds((1 - c) * rh, rh)], dst_ref=refs[n + t],
            send_sem=send.at[t], recv_sem=recv.at[t], device_id=(x, y, 1 - c), device_id_type=MESH))
    return cps


def _reduce_sibling_start(grads, after, name):
    n = len(grads)
    lands = [lax.empty((N_CHIPS, g.shape[1] // 2, g.shape[2]), BF16) for g in grads]

    def issue(refs, send, recv):
        for cp in _sibling_copies(refs, send, recv, n):
            cp.start()

    return _split_start(name, list(grads) + lands, n, after, issue)


def _reduce_sibling_wait(send, recv, arrays, after, name):
    n = len(arrays) // 2

    def await_all(refs, send_ref, recv_ref):
        for cp in _sibling_copies(refs, send_ref, recv_ref, n):
            cp.wait_send()
            cp.wait_recv()

    res = _split_wait(name, arrays, send, recv, after, await_all)
    return res[:n], res[n:]


def _add_sibling_half(grad, got, dev_idx, name):
    j, r, cols = grad.shape
    rh = r // 2
    tr = rh
    nb = rh // tr

    def body(idx_ref, g_ref, got_ref, out_ref):
        out_ref[...] = (g_ref[...].astype(F32) + got_ref[...].astype(F32)).astype(BF16)

    return pl.pallas_call(
        body, name=name,
        grid_spec=pltpu.PrefetchScalarGridSpec(
            num_scalar_prefetch=1, grid=(j, nb),
            in_specs=[pl.BlockSpec((None, tr, cols), lambda jj, i, idx: (jj, idx[2] * nb + i, 0)),
                      pl.BlockSpec((None, tr, cols), lambda jj, i, idx: (jj, i, 0))],
            out_specs=pl.BlockSpec((None, tr, cols), lambda jj, i, idx: (jj, i, 0))),
        out_shape=jax.ShapeDtypeStruct((j, rh, cols), BF16),
        compiler_params=_params("parallel", "parallel"),
    )(dev_idx, grad, got)


def _chip_copies(refs, send, recv, n, receiving):
    x, y, c = _position()
    chip = 2 * x + y
    cps = []
    for t in range(n):
        for k in range(1, N_CHIPS):
            px, py = _chip_peer(x, y, k)
            cps.append(pltpu.make_async_remote_copy(
                src_ref=refs[t].at[2 * px + py], dst_ref=refs[n + t].at[2 * px + py if receiving else chip],
                send_sem=send.at[3 * t + k - 1], recv_sem=recv.at[3 * t + k - 1],
                device_id=(px, py, c), device_id_type=MESH))
    return cps


def _reduce_chips_start(partials, after, name):
    n = len(partials)
    lands = [lax.empty(p.shape, BF16) for p in partials]

    def issue(refs, send, recv):
        for cp in _chip_copies(refs, send, recv, n, False):
            cp.start()

    return _split_start(name, list(partials) + lands, 3 * n, after, issue)


def _reduce_chips_wait(send, recv, arrays, after, name):
    n = len(arrays) // 2

    def await_all(refs, send_ref, recv_ref):
        for cp in _chip_copies(refs, send_ref, recv_ref, n, True):
            cp.wait_send()
            cp.wait_recv()

    res = _split_wait(name, arrays, send, recv, after, await_all)
    return res[:n], res[n:]


def _sum_partials(land, partial, dev_idx, name):
    _, rh, cols = land.shape
    tr = 128
    nb = rh // tr

    def body(idx_ref, l_ref, p_ref, o_ref):
        chip = idx_ref[1]
        acc = jnp.where(chip == 0, p_ref[...], l_ref[0]).astype(F32)
        for s in range(1, N_CHIPS):
            acc = acc + jnp.where(chip == s, p_ref[...], l_ref[s]).astype(F32)
        o_ref[...] = acc

    return pl.pallas_call(
        body, name=name,
        grid_spec=pltpu.PrefetchScalarGridSpec(
            num_scalar_prefetch=1, grid=(nb,),
            in_specs=[pl.BlockSpec((N_CHIPS, tr, cols), lambda i, idx: (0, i, 0)),
                      pl.BlockSpec((None, tr, cols), lambda i, idx: (idx[1], i, 0))],
            out_specs=pl.BlockSpec((tr, cols), lambda i, idx: (idx[2] * nb + i, 0))),
        out_shape=jax.ShapeDtypeStruct((2 * rh, cols), F32), compiler_params=_params("parallel"),
    )(dev_idx, land, partial)


def _half_copies(refs, send, recv, receiving):
    x, y, c = _position()
    cps = []
    for t, ref in enumerate(refs):
        rh = ref.shape[0] // 2
        cps.append(pltpu.make_async_remote_copy(
            src_ref=ref.at[pl.ds(c * rh, rh)], dst_ref=ref.at[pl.ds(((1 - c) if receiving else c) * rh, rh)],
            send_sem=send.at[t], recv_sem=recv.at[t], device_id=(x, y, 1 - c), device_id_type=MESH))
    return cps


def _share_halves_start(totals, after, name):
    def issue(refs, send, recv):
        for cp in _half_copies(refs, send, recv, False):
            cp.start()

    return _split_start(name, list(totals), len(totals), after, issue)


def _share_halves_wait(send, recv, totals, after, name):
    def await_all(refs, send_ref, recv_ref):
        for cp in _half_copies(refs, send_ref, recv_ref, True):
            cp.wait_send()
            cp.wait_recv()

    return _split_wait(name, totals, send, recv, after, await_all)


SMALL_ROWS = 56


def _small_copies(refs, send, recv, receiving):
    x, y, c = _position()
    me = 4 * x + 2 * y + c
    cps = []
    for k in range(1, N_DEV):
        px, py, pc = _xor_peer(x, y, c, k)
        cps.append(pltpu.make_async_remote_copy(
            src_ref=refs[0], dst_ref=refs[1].at[4 * px + 2 * py + pc if receiving else me],
            send_sem=send.at[k - 1], recv_sem=recv.at[k - 1], device_id=(px, py, pc), device_id_type=MESH))
    return cps


def _small_gather_start(packed, after):
    land = lax.empty((N_DEV,) + packed.shape, F32)

    def issue(refs, send, recv):
        for cp in _small_copies(refs, send, recv, False):
            cp.start()

    return _split_start("small_gather_start", [packed, land], N_DEV - 1, after, issue)


def _small_gather_wait(send, recv, arrays, after):
    def await_all(refs, send_ref, recv_ref):
        for cp in _small_copies(refs, send_ref, recv_ref, True):
            cp.wait_send()
            cp.wait_recv()

    return _split_wait("small_gather_wait", arrays, send, recv, after, await_all)


def _reduce_small(packed, land, silu_c):
    ns = 3 * D_MODEL // N_CHIPS

    def body(p_ref, land_ref, sc_ref, tot_ref, gw_ref, loss_ref, qk_ref, allp):
        x, y, c = _position()
        me = 4 * x + 2 * y + c
        chip = 2 * x + y
        for i in range(N_DEV):
            allp[i] = jnp.where(me == i, p_ref[...], land_ref[i])
        tot = allp[0]
        for i in range(1, N_DEV):
            tot = tot + allp[i]
        tot_ref[...] = tot
        loss_ref[...] = jnp.sum(tot[11:12, :], axis=1, keepdims=True) * (0.5 / D_MODEL)
        fold = tot[5:11, 0:HEAD_DIM]
        for h in range(1, N_HEADS):
            fold = fold + tot[5:11, h * HEAD_DIM:(h + 1) * HEAD_DIM]
        qk_ref[...] = jnp.concatenate([fold, jnp.zeros((2, HEAD_DIM), F32)], axis=0)
        sct = sc_ref[...].T
        rc = 64
        for l in range(2):
            dms = [allp[i, pl.ds(12 + 4 * l + chip, 1), :][:, :ns] for i in range(N_DEV)]
            for r0 in range(0, D_MODEL, rc):
                acc = sct[r0:r0 + rc, 0:1] * dms[0]
                for i in range(1, N_DEV):
                    acc = acc + sct[r0:r0 + rc, i:i + 1] * dms[i]
                gw_ref[l, r0:r0 + rc, :] = acc

    vm = pl.BlockSpec(memory_space=pltpu.VMEM)
    return pl.pallas_call(
        body, name="reduce_small", in_specs=[vm, vm, vm], out_specs=[vm] * 4,
        out_shape=[jax.ShapeDtypeStruct((SMALL_ROWS, D_MODEL), F32), jax.ShapeDtypeStruct((2, D_MODEL, ns), F32),
                   jax.ShapeDtypeStruct((1, 1), F32), jax.ShapeDtypeStruct((8, HEAD_DIM), F32)],
        scratch_shapes=[pltpu.VMEM((N_DEV, SMALL_ROWS, D_MODEL), F32)],
        compiler_params=pltpu.CompilerParams(vmem_limit_bytes=VMEM_LIMIT_BYTES),
    )(packed, land, silu_c)


def kernel(x, c, norm_g, ada_w, ada_b, a_w_in, a_conv_w, a_conv_b, a_ln_g, a_ln_b, a_w_out, b_w_in, b_q_norm, b_k_norm, b_w_out, loss_target, m_norm_g, m_ada_w, m_ada_b, m_a_w_in, m_a_conv_w, m_a_conv_b, m_a_ln_g, m_a_ln_b, m_a_w_out, m_b_w_in, m_b_q_norm, m_b_k_norm, m_b_w_out, v_norm_g, v_ada_w, v_ada_b, v_a_w_in, v_a_conv_w, v_a_conv_b, v_a_ln_g, v_a_ln_b, v_a_w_out, v_b_w_in, v_b_q_norm, v_b_k_norm, v_b_w_out):
    chip = 2 * lax.axis_index("x") + lax.axis_index("y")
    core = lax.axis_index("c")
    chip_idx = chip.astype(jnp.int32).reshape(1)
    dev_idx = jnp.stack([2 * chip + core, chip, core]).astype(jnp.int32)

    mods, silu_c, conv_w_full = _ada_forward(c, ada_w, ada_b, a_conv_w[0])
    lands_a = [_cast_into_slot(a_w_in[0], chip_idx, "cast_a_w_in"), _cast_into_slot(a_w_out[0], chip_idx, "cast_a_w_out")]
    send_a, recv_a, lands_a, token_a = _gather_start(lands_a, mods, "gather_start_a")
    land_b_in, own_wb_in = _cast_into_slot(b_w_in[0], chip_idx, "cast_b_w_in", keep_own=True)
    lands_b = [land_b_in, _cast_into_slot(b_w_out[0], chip_idx, "cast_b_w_out")]
    send_b, recv_b, lands_b, token_b = _gather_start(lands_b, token_a, "gather_start_b")
    mods = mods + token_b[0:2, 0:1]

    def weights_a(after):
        send, recv, lands, _ = _gather_forward(send_a, recv_a, lands_a, after, "gather_forward_a")
        w_in, w_out = _gather_wait(send, recv, lands, after, "gather_wait_a")
        return w_in, w_out.reshape(D_MODEL, D_MODEL)

    forwarded_b = []

    def weights_b(after):
        send, recv, lands, _ = forwarded_b
        w_in, w_out = _gather_wait(send, recv, lands, after, "gather_wait_b")
        return w_in, w_out.reshape(D_MODEL, D_MODEL)

    def forward_weights_b(after):
        forwarded_b.extend(_gather_forward(send_b, recv_b, lands_b, after, "gather_forward_b"))
        return forwarded_b[3]

    stage1, stage2 = {}, {}

    def send_grads(tag, dw_in, dw_out):
        grads = [dw_in, dw_out.reshape(N_CHIPS, D_MODEL // N_CHIPS, D_MODEL)]
        send, recv, arrays, token = _reduce_sibling_start(grads, dw_out, f"reduce_d2d_start_{tag}")
        stage1[tag] = (send, recv, arrays)
        return token

    def forward_grads(tag, after):
        send, recv, arrays = stage1[tag]
        grads, got = _reduce_sibling_wait(send, recv, arrays, after, f"reduce_d2d_wait_{tag}")
        partials = [_add_sibling_half(grads[i], got[i], dev_idx, f"reduce_add_{tag}_{i}") for i in range(2)]
        send, recv, arrays, token = _reduce_chips_start(partials, partials[1], f"reduce_ici_start_{tag}")
        stage2[tag] = (send, recv, arrays)
        return token

    stage3 = {}

    def sum_grads(tag, after):
        send, recv, arrays = stage2[tag]
        partials, lands = _reduce_chips_wait(send, recv, arrays, after, f"reduce_ici_wait_{tag}")
        totals = [_sum_partials(lands[i], partials[i], dev_idx, f"reduce_sum_{tag}_{i}") for i in range(2)]
        send, recv, totals, token = _share_halves_start(totals, totals[1], f"reduce_share_start_{tag}")
        stage3[tag] = (send, recv, totals)
        return token

    def finish_grads(tag, after):
        send, recv, totals = stage3[tag]
        return _share_halves_wait(send, recv, totals, after, f"reduce_share_wait_{tag}")

    grad_x, small = _local_step(
        x[0], loss_target[0], mods.reshape(2, 3, D_MODEL), norm_g, conv_w_full, a_conv_b, a_ln_g[0:1],
        a_ln_b[0:1], b_q_norm[0], b_k_norm[0], chip.astype(jnp.int32), own_wb_in,
        weights_a, weights_b, forward_weights_b,
        functools.partial(send_grads, "b"), functools.partial(forward_grads, "b"), functools.partial(send_grads, "a"))

    ns = 3 * D_MODEL // N_CHIPS
    pad_mod = lambda dm: jnp.pad(dm.reshape(N_CHIPS, ns), ((0, 0), (0, D_MODEL - ns)))
    packed = jnp.concatenate([
        small["dnorm_g"], small["dconv_b"], small["dln_g"], small["dln_b"], small["dq_norm"], small["dk_norm"],
        small["loss_cols"], pad_mod(small["dmod0"]), pad_mod(small["dmod1"]), small["dconv_w"],
        jnp.zeros((SMALL_ROWS - 20 - CONV_WIDTH, D_MODEL), F32)], axis=0)
    send_s, recv_s, small_arrays, token_s = _small_gather_start(packed, packed)

    given = dict(norm_g=(norm_g, m_norm_g, v_norm_g), ada_w=(ada_w, m_ada_w, v_ada_w), ada_b=(ada_b, m_ada_b, v_ada_b),
                 a_w_in=(a_w_in, m_a_w_in, v_a_w_in), a_conv_w=(a_conv_w, m_a_conv_w, v_a_conv_w),
                 a_conv_b=(a_conv_b, m_a_conv_b, v_a_conv_b), a_ln_g=(a_ln_g, m_a_ln_g, v_a_ln_g),
                 a_ln_b=(a_ln_b, m_a_ln_b, v_a_ln_b), a_w_out=(a_w_out, m_a_w_out, v_a_w_out),
                 b_w_in=(b_w_in, m_b_w_in, v_b_w_in), b_q_norm=(b_q_norm, m_b_q_norm, v_b_q_norm),
                 b_k_norm=(b_k_norm, m_b_k_norm, v_b_k_norm), b_w_out=(b_w_out, m_b_w_out, v_b_w_out))
    order = ["norm_g", "ada_w", "ada_b", "a_w_in", "a_conv_w", "a_conv_b", "a_ln_g", "a_ln_b", "a_w_out", "b_w_in",
             "b_q_norm", "b_k_norm", "b_w_out"]
    outs = {}

    def update(k, g2, after=None, copy_grad=False):
        w, m, v = given[k]
        shape2 = g2.shape
        res = _adamw(w.reshape(shape2), g2, m.reshape(shape2), v.reshape(shape2), f"adamw_{k}", after, copy_grad)
        outs[k] = tuple(a.reshape(w.shape) for a in ((res[3] if copy_grad else g2), res[0], res[1], res[2]))

    token = forward_grads("a", token_s)
    token = sum_grads("b", token)
    packed, land = _small_gather_wait(send_s, recv_s, small_arrays, token)
    tot, g_ada_w, loss, qk = _reduce_small(packed, land, silu_c)
    g_b_in, g_b_out = finish_grads("b", tot)
    update("b_w_in", g_b_in, copy_grad=True)
    update("b_w_out", g_b_out, copy_grad=True)
    token = sum_grads("a", outs["b_w_in"][1])
    cw = D_MODEL // N_CHIPS
    g_small = dict(
        norm_g=tot[0:2], a_conv_b=tot[2:3], a_ln_g=tot[3:4], a_ln_b=tot[4:5],
        b_q_norm=qk[0:3], b_k_norm=qk[3:6],
        ada_b=jnp.stack([tot[12:16, :ns].reshape(3 * D_MODEL), tot[16:20, :ns].reshape(3 * D_MODEL)]),
        a_conv_w=lax.dynamic_slice(tot[20:20 + CONV_WIDTH], (0, chip * cw), (CONV_WIDTH, cw)),
    )
    update("ada_w", g_ada_w.reshape(2 * D_MODEL, ns), after=token)
    for k, g2 in g_small.items():
        update(k, g2, after=token)
    g_a_in, g_a_out = finish_grads("a", outs["ada_w"][1])
    update("a_w_in", g_a_in, copy_grad=True)
    update("a_w_out", g_a_out, copy_grad=True)
    return (loss.reshape(()), grad_x[None], *[outs[k][0] for k in order], *[outs[k][1] for k in order],
            *[outs[k][2] for k in order], *[outs[k][3] for k in order])
```

```python
import functools

import jax
import jax.numpy as jnp
from jax import lax
from jax.experimental import pallas as pl
from jax.experimental.pallas import tpu as pltpu

F32 = jnp.float32
BF16 = jnp.bfloat16

SEQ = 2048
D_MODEL = 1024
CONV_WIDTH = 31
HEAD_DIM = 64
N_HEADS = 16
DILATIONS = (1, 4, 16)
ATTN_BLOCK = 128
NORM_EPS = 1e-6
NEG_INF = -1e30
N_DEV = 8
N_CHIPS = 4

ADAM_LR = 0.001
ADAM_B1 = 0.9
ADAM_B2 = 0.999
ADAM_EPS = 1e-08
ADAM_WD = 0.01
ADAM_STEP = 10

VMEM_LIMIT_BYTES = 52 * 1024 * 1024
HALO = 32
LANES = 128
MESH = pl.DeviceIdType.MESH


def _params(*sem):
    return pltpu.CompilerParams(dimension_semantics=sem or None, vmem_limit_bytes=VMEM_LIMIT_BYTES)


def _sigmoid(v):
    return 1.0 / (1.0 + jnp.exp(-v))


def _row_spec(tm, cols, col_block=0):
    return pl.BlockSpec((tm, cols), lambda i: (i, col_block))


def _vec_spec(rows, cols):
    return pl.BlockSpec((rows, cols), lambda i: (0, 0))


def _normmod(xv, g, scale, shift):
    r = lax.rsqrt(jnp.mean(xv * xv, axis=-1, keepdims=True) + NORM_EPS)
    return xv * r * g * (1.0 + scale) + shift


def _normmod_fwd(x, g, scale, shift, name):
    tm = 256

    def body(x_ref, g_ref, sc_ref, sh_ref, h_ref, ht_ref):
        h = _normmod(x_ref[...], g_ref[...], sc_ref[...], sh_ref[...])
        h_ref[...] = h.astype(BF16)
        ht_ref[...] = h.T.astype(BF16)

    return pl.pallas_call(
        body, name=name, grid=(SEQ // tm,),
        in_specs=[_row_spec(tm, D_MODEL)] + [_vec_spec(1, D_MODEL)] * 3,
        out_specs=[_row_spec(tm, D_MODEL), pl.BlockSpec((D_MODEL, tm), lambda i: (0, i))],
        out_shape=[jax.ShapeDtypeStruct((SEQ, D_MODEL), BF16), jax.ShapeDtypeStruct((D_MODEL, SEQ), BF16)],
        compiler_params=_params("parallel"),
    )(x, g, scale, shift)


def _normmod_bwd(x, g, scale, dh_parts, dres, name, part_dilations=None, gated=None):
    tm = 256
    n_parts = len(dh_parts)
    dils = part_dilations or (1,) * n_parts
    dh_parts = [p if d == 1 else p.reshape(d, SEQ // d, D_MODEL) for p, d in zip(dh_parts, dils)]
    n_gated = 0 if gated is None else 2

    def body(x_ref, g_ref, sc_ref, dres_ref, *rest):
        part_refs = rest[:n_parts]
        gated_refs = rest[n_parts:n_parts + n_gated]
        out_refs = rest[n_parts + n_gated:]
        dx_ref, sums_ref, nat = out_refs[0], out_refs[1], out_refs[-1]
        xv = x_ref[...]
        r = lax.rsqrt(jnp.mean(xv * xv, axis=-1, keepdims=True) + NORM_EPS)
        xn = xv * r
        dh = _load_natural(part_refs[0], nat, dils[0])
        for p, d in zip(part_refs[1:], dils[1:]):
            dh = dh + _load_natural(p, nat, d)
        gv = g_ref[...]
        one_sc = 1.0 + sc_ref[...]
        dxn = dh * (gv * one_sc)
        dx = dres_ref[...] + r * (dxn - xn * jnp.mean(dxn * xn, axis=-1, keepdims=True))
        dx_ref[...] = dx
        dhx = dh * xn
        rows = [jnp.sum(dhx, axis=0, keepdims=True) * one_sc,
                jnp.sum(dhx, axis=0, keepdims=True) * gv,
                jnp.sum(dh, axis=0, keepdims=True)]
        if gated is not None:
            gate_ref, y_ref = gated_refs
            out_refs[2][...] = (dx * gate_ref[...]).astype(BF16)
            rows.append(jnp.sum(dx * y_ref[...], axis=0, keepdims=True))
        sums = jnp.concatenate(rows + [jnp.zeros((8 - len(rows), D_MODEL), F32)], axis=0)

        @pl.when(pl.program_id(0) == 0)
        def _():
            sums_ref[...] = jnp.zeros_like(sums_ref)

        sums_ref[...] += sums

    gated_specs = [] if gated is None else [_vec_spec(1, D_MODEL), _row_spec(tm, D_MODEL)]
    dy_spec = [] if gated is None else [_row_spec(tm, D_MODEL)]
    dy_shape = [] if gated is None else [jax.ShapeDtypeStruct((SEQ, D_MODEL), BF16)]
    return pl.pallas_call(
        body, name=name, grid=(SEQ // tm,),
        in_specs=[_row_spec(tm, D_MODEL), _vec_spec(1, D_MODEL), _vec_spec(1, D_MODEL), _row_spec(tm, D_MODEL)]
        + [_class_spec(tm, d) for d in dils] + gated_specs,
        out_specs=[_row_spec(tm, D_MODEL), _vec_spec(8, D_MODEL)] + dy_spec,
        out_shape=[jax.ShapeDtypeStruct((SEQ, D_MODEL), F32), jax.ShapeDtypeStruct((8, D_MODEL), F32)] + dy_shape,
        scratch_shapes=[_natural_scratch(tm)],
        compiler_params=_params("arbitrary"),
    )(x, g, scale, dres, *dh_parts, *(gated or ()))


def _mm(lhs, rhs, *, tn, tile0, n_tiles, out_dtype, name, out3d=None, prev=None, transpose_lhs=False):
    mo, kc = lhs.shape[::-1] if transpose_lhs else lhs.shape
    cm = min(mo, 1024)
    tc = 256

    def body(l_ref, r_ref, *rest):
        if transpose_lhs:
            o_ref, lt_ref = rest[-2], rest[-1]

            @pl.when(pl.program_id(0) == 0)
            def _():
                for c in range(kc // tc):
                    lt_ref[:, c * tc:(c + 1) * tc] = l_ref[c * tc:(c + 1) * tc, :].astype(F32).T.astype(l_ref.dtype)
        else:
            o_ref, lt_ref = rest[-1], l_ref
        for m in range(mo // cm):
            rows = pl.ds(m * cm, cm)
            o_ref[rows, :] = jnp.dot(lt_ref[rows, :], r_ref[...], preferred_element_type=F32).astype(out_dtype)

    if rhs.ndim == 3:
        tps_r = rhs.shape[2] // tn
        r_spec = pl.BlockSpec((None, kc, tn), lambda t: ((tile0 + t) // tps_r, 0, (tile0 + t) % tps_r))
    else:
        r_spec = pl.BlockSpec((kc, tn), lambda t: (0, t))
    in_specs = [pl.BlockSpec(lhs.shape, lambda t: (0, 0)), r_spec]
    args = [lhs, rhs]
    aliases = {}
    if out3d is None:
        o_spec = pl.BlockSpec((mo, tn), lambda t: (0, t))
        o_shape = jax.ShapeDtypeStruct((mo, n_tiles * tn), out_dtype)
    else:
        j_out, ns_out = out3d
        tps_o = ns_out // tn
        o_spec = pl.BlockSpec((None, mo, tn), lambda t: ((tile0 + t) // tps_o, 0, (tile0 + t) % tps_o))
        o_shape = jax.ShapeDtypeStruct((j_out, mo, ns_out), out_dtype)
        if prev is not None:
            in_specs.append(pl.BlockSpec(memory_space=pl.ANY))
            args.append(prev)
            aliases = {2: 0}
    return pl.pallas_call(
        body, name=name, grid=(n_tiles,), in_specs=in_specs, out_specs=o_spec, out_shape=o_shape,
        input_output_aliases=aliases,
        scratch_shapes=[pltpu.VMEM((mo, kc), lhs.dtype)] if transpose_lhs else [],
        compiler_params=_params("arbitrary" if transpose_lhs else "parallel"),
    )(*args)


def _b_in_tiles(h_parts, w3, tile_ids, n_tiles, name, prev=None):
    _, kc, ns = w3.shape
    tps = ns // B_TN
    cm = 1024

    def body(ids_ref, h0_ref, h1_ref, h2_ref, w_ref, *rest):
        o_ref = rest[-1]
        out_tile = ids_ref[1, pl.program_id(0)]
        group = jnp.where(out_tile >= B_Z_TILE0, 0, out_tile // B_GROUP_TILES)
        for g, h_ref in enumerate((h0_ref, h1_ref, h2_ref)):
            @pl.when(group == g)
            def _():
                for m in range(SEQ // cm):
                    rows = pl.ds(m * cm, cm)
                    o_ref[rows, :] = jnp.dot(h_ref[rows, :], w_ref[...], preferred_element_type=F32).astype(BF16)

    resident = pl.BlockSpec((SEQ, kc), lambda t, ids: (0, 0))
    in_specs = [resident] * 3 + [pl.BlockSpec((None, kc, B_TN), lambda t, ids: (ids[0, t] // tps, 0, ids[0, t] % tps))]
    args = [*h_parts, w3]
    aliases = {}
    if prev is not None:
        in_specs.append(pl.BlockSpec(memory_space=pl.ANY))
        args.append(prev)
        aliases = {5: 0}
    return pl.pallas_call(
        body, name=name,
        grid_spec=pltpu.PrefetchScalarGridSpec(
            num_scalar_prefetch=1, grid=(n_tiles,), in_specs=in_specs,
            out_specs=pl.BlockSpec((SEQ, B_TN), lambda t, ids: (0, ids[1, t]))),
        out_shape=jax.ShapeDtypeStruct((SEQ, B_TILES * B_TN), BF16),
        input_output_aliases=aliases, compiler_params=_params("arbitrary"),
    )(tile_ids, *args)


def _mm_nt(dy, w3, *, tn, tile0, n_tiles, name, after=None):
    m_rows = dy.shape[0]
    _, kc, ns = w3.shape
    tps = ns // tn
    cm = 512
    extra = [] if after is None else [after]

    def body(dy_ref, w_ref, *rest):
        o_ref = rest[-1]

        @pl.when(pl.program_id(0) == 0)
        def _():
            o_ref[...] = jnp.zeros_like(o_ref)

        for m in range(m_rows // cm):
            rows = pl.ds(m * cm, cm)
            o_ref[rows, :] += lax.dot_general(dy_ref[rows, :], w_ref[...], (((1,), (1,)), ((), ())),
                                              preferred_element_type=F32)

    return pl.pallas_call(
        body, name=name, grid=(n_tiles,),
        in_specs=[pl.BlockSpec((m_rows, tn), lambda t: (0, t)),
                  pl.BlockSpec((None, kc, tn), lambda t: ((tile0 + t) // tps, 0, (tile0 + t) % tps))]
        + [pl.BlockSpec(memory_space=pl.ANY)] * len(extra),
        out_specs=pl.BlockSpec((m_rows, kc), lambda t: (0, 0)),
        out_shape=jax.ShapeDtypeStruct((m_rows, kc), F32),
        compiler_params=_params("arbitrary"),
    )(dy, w3, *extra)


CONV_CHUNK = 16


def _shift_copies(buf, shifted):
    rows = shifted.shape[1]
    for s in range(1, 8):
        shifted[s - 1] = buf[pl.ds(s, rows), :]


def _shifted_rows(buf, shifted, offset, r0):
    s = offset % 8
    if s == 0:
        return buf[pl.ds(r0 + offset, CONV_CHUNK), :]
    return shifted[s - 1, pl.ds(r0 + (offset - s), CONV_CHUNK), :]


def _spread_taps(w_ref, taps):
    for k in range(CONV_WIDTH):
        taps[k] = jnp.broadcast_to(w_ref[k:k + 1, :], (8, D_MODEL))


def _times_tap(taps, k, rows):
    return (rows.reshape(CONV_CHUNK // 8, 8, D_MODEL) * taps[k][None]).reshape(CONV_CHUNK, D_MODEL)


def _conv_fwd(proj, conv_w, conv_b, ln_g, ln_b, name):
    tm = 256
    hb = tm // HALO

    def body(vg_ref, halo_ref, z_ref, w_ref, b_ref, g_ref, be_ref, u5_ref, u5t_ref, u2_ref, buf, shifted, taps):
        i = pl.program_id(0)
        u1 = vg_ref[:, :D_MODEL] * _sigmoid(vg_ref[:, D_MODEL:])
        u1h = halo_ref[:, :D_MODEL] * _sigmoid(halo_ref[:, D_MODEL:])
        buf[pl.ds(0, HALO), :] = jnp.where(i > 0, u1h, 0.0)
        buf[pl.ds(HALO, tm), :] = u1
        _shift_copies(buf, shifted)
        _spread_taps(w_ref, taps)

        def chunk(ci, carry):
            r0 = pl.multiple_of(ci * CONV_CHUNK, CONV_CHUNK)
            acc = jnp.broadcast_to(b_ref[...], (CONV_CHUNK, D_MODEL))
            for k in range(CONV_WIDTH):
                acc = acc + _times_tap(taps, k, _shifted_rows(buf, shifted, HALO - (CONV_WIDTH - 1) + k, r0))
            u2_ref[pl.ds(r0, CONV_CHUNK), :] = acc
            return carry

        lax.fori_loop(0, tm // CONV_CHUNK, chunk, 0)
        acc = u2_ref[...]
        mu = jnp.mean(acc, axis=-1, keepdims=True)
        xc = acc - mu
        rstd = lax.rsqrt(jnp.mean(xc * xc, axis=-1, keepdims=True) + NORM_EPS)
        u3 = xc * rstd * g_ref[...] + be_ref[...]
        zv = z_ref[...]
        u5 = u3 * _sigmoid(u3) * (zv * _sigmoid(zv))
        u5_ref[...] = u5.astype(BF16)
        u5t_ref[...] = u5.T.astype(BF16)

    return pl.pallas_call(
        body, name=name, grid=(SEQ // tm,),
        in_specs=[pl.BlockSpec((tm, 2 * D_MODEL), lambda i: (i, 0)),
                  pl.BlockSpec((HALO, 2 * D_MODEL), lambda i: (jnp.maximum(i * hb - 1, 0), 0)),
                  _row_spec(tm, D_MODEL, 2),
                  _vec_spec(CONV_WIDTH, D_MODEL)] + [_vec_spec(1, D_MODEL)] * 3,
        out_specs=[_row_spec(tm, D_MODEL), pl.BlockSpec((D_MODEL, tm), lambda i: (0, i)), _row_spec(tm, D_MODEL)],
        out_shape=[jax.ShapeDtypeStruct((SEQ, D_MODEL), BF16), jax.ShapeDtypeStruct((D_MODEL, SEQ), BF16),
                   jax.ShapeDtypeStruct((SEQ, D_MODEL), F32)],
        scratch_shapes=[pltpu.VMEM((HALO + tm, D_MODEL), F32), pltpu.VMEM((7, HALO + tm - 8, D_MODEL), F32),
                        pltpu.VMEM((CONV_WIDTH, 8, D_MODEL), F32)],
        compiler_params=_params("parallel"),
    )(proj, proj, proj, conv_w, conv_b, ln_g, ln_b)


def _conv_bwd_pointwise(dy, w_out, proj, u2, ln_g, ln_b, name):
    tm = 256

    def body(dy_ref, w_ref, z_ref, u2_ref, g_ref, be_ref, du2_ref, dz_ref, sums_ref):
        u2v = u2_ref[...]
        mu = jnp.mean(u2v, axis=-1, keepdims=True)
        xc = u2v - mu
        rstd = lax.rsqrt(jnp.mean(xc * xc, axis=-1, keepdims=True) + NORM_EPS)
        xhat = xc * rstd
        u3 = xhat * g_ref[...] + be_ref[...]
        s3 = _sigmoid(u3)
        u4 = u3 * s3
        zv = z_ref[...]
        sz = _sigmoid(zv)
        du5v = lax.dot_general(dy_ref[...], w_ref[...], NT_DIMS, preferred_element_type=F32)
        dz_ref[...] = du5v * u4 * (sz * (1.0 + zv * (1.0 - sz)))
        du3 = du5v * (zv * sz) * (s3 * (1.0 + u3 * (1.0 - s3)))
        dxhat = du3 * g_ref[...]
        du2 = rstd * (dxhat - jnp.mean(dxhat, axis=-1, keepdims=True)
                      - xhat * jnp.mean(dxhat * xhat, axis=-1, keepdims=True))
        du2_ref[...] = du2
        sums = jnp.concatenate([
            jnp.sum(du3 * xhat, axis=0, keepdims=True),
            jnp.sum(du3, axis=0, keepdims=True),
            jnp.sum(du2, axis=0, keepdims=True),
            jnp.zeros((5, D_MODEL), F32)], axis=0)

        @pl.when(pl.program_id(0) == 0)
        def _():
            sums_ref[...] = jnp.zeros_like(sums_ref)

        sums_ref[...] += sums

    return pl.pallas_call(
        body, name=name, grid=(SEQ // tm,),
        in_specs=[_row_spec(tm, D_MODEL), _vec_spec(D_MODEL, D_MODEL), _row_spec(tm, D_MODEL, 2),
                  _row_spec(tm, D_MODEL), _vec_spec(1, D_MODEL), _vec_spec(1, D_MODEL)],
        out_specs=[_row_spec(tm, D_MODEL), _row_spec(tm, D_MODEL), _vec_spec(8, D_MODEL)],
        out_shape=[jax.ShapeDtypeStruct((SEQ, D_MODEL), F32), jax.ShapeDtypeStruct((SEQ, D_MODEL), F32),
                   jax.ShapeDtypeStruct((8, D_MODEL), F32)],
        compiler_params=_params("arbitrary"),
    )(dy, w_out, proj, u2, ln_g, ln_b)


def _conv_bwd_taps(du2, dz, proj, conv_w, name):
    tm = 256
    hb = tm // HALO
    n_blocks = SEQ // tm

    def body(du2_ref, dnext_ref, dz_ref, vg_ref, w_ref, dproj_ref, dw_ref, dbuf, dshift, sgbuf, ubuf, dwacc, taps):
        i = pl.program_id(0)
        _spread_taps(w_ref, taps)
        sg = _sigmoid(vg_ref[:, D_MODEL:])
        sgbuf[...] = sg
        ubuf[...] = vg_ref[:, :D_MODEL] * sg
        dbuf[pl.ds(0, tm), :] = du2_ref[...]
        dbuf[pl.ds(tm, HALO), :] = jnp.where(i < n_blocks - 1, dnext_ref[...], 0.0)
        _shift_copies(dbuf, dshift)

        @pl.when(i == 0)
        def _():
            dwacc[...] = jnp.zeros_like(dwacc)

        def chunk(ci, carry):
            r0 = pl.multiple_of(ci * CONV_CHUNK, CONV_CHUNK)
            rows = pl.ds(r0, CONV_CHUNK)
            u1c = ubuf[rows, :]
            du1 = jnp.zeros((CONV_CHUNK, D_MODEL), F32)
            for k in range(CONV_WIDTH):
                ahead = _shifted_rows(dbuf, dshift, CONV_WIDTH - 1 - k, r0)
                du1 = du1 + _times_tap(taps, k, ahead)
                prod = u1c * ahead
                dwacc[k] += prod[0:8] + prod[8:16]
            sgc = sgbuf[rows, :]
            dval = du1 * sgc
            dproj_ref[rows, 0:D_MODEL] = dval.astype(BF16)
            dproj_ref[rows, D_MODEL:2 * D_MODEL] = (dval * vg_ref[rows, 0:D_MODEL] * (1.0 - sgc)).astype(BF16)
            return carry

        lax.fori_loop(0, tm // CONV_CHUNK, chunk, 0)
        dproj_ref[:, 2 * D_MODEL:] = dz_ref[...].astype(BF16)

        @pl.when(i == n_blocks - 1)
        def _():
            for k in range(CONV_WIDTH):
                dw_ref[k:k + 1, :] = jnp.sum(dwacc[k], axis=0, keepdims=True)
            dw_ref[CONV_WIDTH:, :] = jnp.zeros((32 - CONV_WIDTH, D_MODEL), F32)

    return pl.pallas_call(
        body, name=name, grid=(n_blocks,),
        in_specs=[_row_spec(tm, D_MODEL),
                  pl.BlockSpec((HALO, D_MODEL), lambda i: (jnp.minimum((i + 1) * hb, SEQ // HALO - 1), 0)),
                  _row_spec(tm, D_MODEL),
                  pl.BlockSpec((tm, 2 * D_MODEL), lambda i: (i, 0)),
                  _vec_spec(CONV_WIDTH, D_MODEL)],
        out_specs=[_row_spec(tm, 3 * D_MODEL), _vec_spec(32, D_MODEL)],
        out_shape=[jax.ShapeDtypeStruct((SEQ, 3 * D_MODEL), BF16), jax.ShapeDtypeStruct((32, D_MODEL), F32)],
        scratch_shapes=[pltpu.VMEM((tm + HALO, D_MODEL), F32), pltpu.VMEM((7, HALO + tm - 8, D_MODEL), F32),
                        pltpu.VMEM((tm, D_MODEL), F32), pltpu.VMEM((tm, D_MODEL), F32),
                        pltpu.VMEM((CONV_WIDTH, 8, D_MODEL), F32), pltpu.VMEM((CONV_WIDTH, 8, D_MODEL), F32)],
        compiler_params=_params("arbitrary"),
    )(du2, du2, dz, proj, conv_w)


def _out_a(u5, w_out, x, gate, g1, scale1, shift1, name):
    tm = 256
    n_d = len(DILATIONS)

    def body(u_ref, w_ref, x_ref, gate_ref, g_ref, sc_ref, sh_ref, x1_ref, y_ref, ht_ref, *rest):
        h_refs, nat = rest[:n_d], rest[-1]
        y = jnp.dot(u_ref[...], w_ref[...], preferred_element_type=F32)
        x1 = x_ref[...] + gate_ref[...] * y
        y_ref[...] = y
        x1_ref[...] = x1
        h = _normmod(x1, g_ref[...], sc_ref[...], sh_ref[...])
        ht_ref[...] = h.T.astype(BF16)
        for h_ref, d in zip(h_refs, DILATIONS):
            _store_classes(h_ref, h, nat, d)

    res = pl.pallas_call(
        body, name=name, grid=(SEQ // tm,),
        in_specs=[_row_spec(tm, D_MODEL), _vec_spec(D_MODEL, D_MODEL), _row_spec(tm, D_MODEL)]
        + [_vec_spec(1, D_MODEL)] * 4,
        out_specs=[_row_spec(tm, D_MODEL), _row_spec(tm, D_MODEL), pl.BlockSpec((D_MODEL, tm), lambda i: (0, i))]
        + [_class_spec(tm, d) for d in DILATIONS],
        out_shape=[jax.ShapeDtypeStruct((SEQ, D_MODEL), F32), jax.ShapeDtypeStruct((SEQ, D_MODEL), F32),
                   jax.ShapeDtypeStruct((D_MODEL, SEQ), BF16)] + [_class_shape(d, BF16) for d in DILATIONS],
        scratch_shapes=[_natural_scratch(tm)],
        compiler_params=_params("parallel"),
    )(u5, w_out, x, gate, g1, scale1, shift1)
    return res[0], res[1], res[2], [a.reshape(SEQ, D_MODEL) for a in res[3:]]


def _out_b_loss(u, w_out, x1, gate, target, name):
    tm = 256

    def body(u_ref, w_ref, x_ref, gate_ref, t_ref, e_ref, dy_ref, sums_ref):
        y = jnp.dot(u_ref[...], w_ref[...], preferred_element_type=F32)
        diff = x_ref[...] + gate_ref[...] * y - t_ref[...]
        e = diff * (1.0 / D_MODEL)
        e_ref[...] = e
        dy_ref[...] = (e * gate_ref[...]).astype(BF16)
        sums = jnp.concatenate([
            jnp.sum(e * y, axis=0, keepdims=True),
            jnp.sum(diff * diff, axis=0, keepdims=True),
            jnp.zeros((6, D_MODEL), F32)], axis=0)

        @pl.when(pl.program_id(0) == 0)
        def _():
            sums_ref[...] = jnp.zeros_like(sums_ref)

        sums_ref[...] += sums

    return pl.pallas_call(
        body, name=name, grid=(SEQ // tm,),
        in_specs=[_row_spec(tm, D_MODEL), _vec_spec(D_MODEL, D_MODEL), _row_spec(tm, D_MODEL),
                  _vec_spec(1, D_MODEL), _row_spec(tm, D_MODEL)],
        out_specs=[_row_spec(tm, D_MODEL), _row_spec(tm, D_MODEL), _vec_spec(8, D_MODEL)],
        out_shape=[jax.ShapeDtypeStruct((SEQ, D_MODEL), F32), jax.ShapeDtypeStruct((SEQ, D_MODEL), BF16),
                   jax.ShapeDtypeStruct((8, D_MODEL), F32)],
        compiler_params=_params("arbitrary"),
    )(u, w_out, x1, gate, target)


def _seg_matrix():
    r = lax.broadcasted_iota(jnp.int32, (256, 256), 0) // HEAD_DIM
    c = lax.broadcasted_iota(jnp.int32, (256, 256), 1) // HEAD_DIM
    return (r == c).astype(BF16)


def _segsum(v, seg):
    hi = v.astype(BF16)
    lo = (v - hi.astype(F32)).astype(BF16)
    outs = []
    for c0 in range(0, D_MODEL, 256):
        outs.append(jnp.dot(hi[:, c0:c0 + 256], seg, preferred_element_type=F32)
                    + jnp.dot(lo[:, c0:c0 + 256], seg, preferred_element_type=F32))
    return jnp.concatenate(outs, axis=1)


def _qk_rstd(v, seg):
    return lax.rsqrt(_segsum(v * v, seg) * (1.0 / HEAD_DIM) + NORM_EPS)


def _qknorm_fwd(proj, group, qw, kw, seg, name):
    tm = 256

    def body(q_in, k_in, qw_ref, kw_ref, seg_ref, q_ref, k_ref):
        segv = seg_ref[...]
        q = q_in[...].astype(F32)
        k = k_in[...].astype(F32)
        q_ref[...] = (q * _qk_rstd(q, segv) * qw_ref[...]).astype(BF16)
        k_ref[...] = (k * _qk_rstd(k, segv) * kw_ref[...]).astype(BF16)

    return pl.pallas_call(
        body, name=name, grid=(SEQ // tm,),
        in_specs=[_row_spec(tm, D_MODEL, 3 * group), _row_spec(tm, D_MODEL, 3 * group + 1),
                  _vec_spec(1, D_MODEL), _vec_spec(1, D_MODEL), _vec_spec(256, 256)],
        out_specs=[_row_spec(tm, D_MODEL)] * 2,
        out_shape=[jax.ShapeDtypeStruct((SEQ, D_MODEL), BF16)] * 2,
        compiler_params=_params("parallel"),
    )(proj, proj, qw, kw, seg)


def _attn_masks(b, bpc, dilation, slope):
    if bpc == 1:
        qi = lax.broadcasted_iota(jnp.int32, (ATTN_BLOCK, ATTN_BLOCK), 0)
        kj = lax.broadcasted_iota(jnp.int32, (ATTN_BLOCK, ATTN_BLOCK), 1)
        steps = qi - kj
        return (steps * dilation).astype(F32), steps >= 0
    qi = lax.broadcasted_iota(jnp.int32, (ATTN_BLOCK, 2 * ATTN_BLOCK), 0)
    kj = lax.broadcasted_iota(jnp.int32, (ATTN_BLOCK, 2 * ATTN_BLOCK), 1)
    steps = qi + ATTN_BLOCK - kj
    has_prev = (b % bpc) != 0
    valid = (steps >= 0) & (steps <= ATTN_BLOCK) & (has_prev | (kj >= ATTN_BLOCK))
    return (steps * dilation).astype(F32), valid


def _key_tile(prev_ref, cur_ref, cols, bpc):
    if bpc == 1:
        return cur_ref[:, cols]
    return jnp.concatenate([prev_ref[:, cols], cur_ref[:, cols]], axis=0)


ATTN_HEADS_FWD = 16
ATTN_HEADS_BWD = 16
NT_DIMS = (((1,), (1,)), ((), ()))
BATCH_NT_DIMS = (((2,), (2,)), ((0,), (0,)))
BATCH_NN_DIMS = (((2,), (1,)), ((0,), (0,)))
BATCH_TN_DIMS = (((1,), (1,)), ((0,), (0,)))


def _head_stack(tile_of, heads):
    return jnp.stack([tile_of(slice(h * HEAD_DIM, (h + 1) * HEAD_DIM)) for h in range(heads)], axis=0)


def _attn_specs(heads, segment=0):
    width = heads * HEAD_DIM
    off = segment * (D_MODEL // width)
    last = SEQ // ATTN_BLOCK - 1
    cur = pl.BlockSpec((ATTN_BLOCK, width), lambda hg, b: (jnp.minimum(b, last), hg + off))
    prev = pl.BlockSpec((ATTN_BLOCK, width), lambda hg, b: (jnp.clip(b - 1, 0, last), hg + off))
    return cur, prev


def _attn_fwd(q, k, proj, group, slopes, dilation, name):
    bpc = SEQ // dilation // ATTN_BLOCK
    heads = ATTN_HEADS_FWD
    assert heads == N_HEADS
    cur, prev = _attn_specs(heads)
    v_cur, v_prev = _attn_specs(heads, segment=3 * group + 2)
    scale = HEAD_DIM ** -0.5

    def body(sl_ref, q_ref, kp_ref, kc_ref, vp_ref, vc_ref, o_ref, lse_ref):
        dist, valid = _attn_masks(pl.program_id(1), bpc, dilation, None)
        q3 = _head_stack(lambda cols: q_ref[:, cols], heads)
        k3 = _head_stack(lambda cols: _key_tile(kp_ref, kc_ref, cols, bpc), heads)
        v3 = _head_stack(lambda cols: _key_tile(vp_ref, vc_ref, cols, bpc), heads)
        s = lax.dot_general(q3, k3, BATCH_NT_DIMS, preferred_element_type=F32)
        s = jnp.where(valid[None], s * scale - dist[None] * sl_ref[...], NEG_INF)
        m = jnp.max(s, axis=-1, keepdims=True)
        p = jnp.exp(s - m)
        l = jnp.sum(p, axis=-1, keepdims=True)
        o3 = lax.dot_general(p.astype(BF16), v3, BATCH_NN_DIMS, preferred_element_type=F32) / l
        lse3 = m + jnp.log(l)
        for h in range(heads):
            o_ref[:, h * HEAD_DIM:(h + 1) * HEAD_DIM] = o3[h]
        lse_ref[...] = jnp.concatenate([lse3[h] for h in range(heads)]
                                       + [jnp.zeros((ATTN_BLOCK, LANES - heads), F32)], axis=1)

    return pl.pallas_call(
        body, name=name, grid=(N_HEADS // heads, SEQ // ATTN_BLOCK),
        in_specs=[pl.BlockSpec((heads, 1, 1), lambda hg, b: (hg, 0, 0)), cur, prev, cur, v_prev, v_cur],
        out_specs=[cur, pl.BlockSpec((ATTN_BLOCK, LANES), lambda hg, b: (b, 0))],
        out_shape=[jax.ShapeDtypeStruct((SEQ, D_MODEL), F32), jax.ShapeDtypeStruct((SEQ, LANES), F32)],
        compiler_params=_params("parallel", "parallel"),
    )(slopes.reshape(N_HEADS, 1, 1), q, k, k, proj, proj)


def _class_spec(tm, dilation, width=D_MODEL):
    if dilation == 1:
        return _row_spec(tm, width)
    return pl.BlockSpec((dilation, tm // dilation, width), lambda i: (0, i, 0))


def _class_shape(dilation, dtype, width=D_MODEL):
    if dilation == 1:
        return jax.ShapeDtypeStruct((SEQ, width), dtype)
    return jax.ShapeDtypeStruct((dilation, SEQ // dilation, width), dtype)


def _load_natural(in_ref, nat_ref, dilation):
    if dilation == 1:
        return in_ref[...].astype(F32)
    n = nat_ref.shape[1] // dilation
    tiles = in_ref.shape[-1] // LANES
    for r in range(dilation):
        for j in range(tiles):
            nat_ref.at[j][pl.ds(r, n, stride=dilation), :] = in_ref[r, :, j * LANES:(j + 1) * LANES].astype(F32)
    if tiles == 1:
        return nat_ref[0]
    return jnp.concatenate([nat_ref[j] for j in range(tiles)], axis=1)


def _store_classes(out_ref, value, nat_ref, dilation):
    if dilation == 1:
        out_ref[...] = value.astype(out_ref.dtype)
        return
    n = nat_ref.shape[1] // dilation
    tiles = value.shape[-1] // LANES
    for j in range(tiles):
        nat_ref[j] = value[:, j * LANES:(j + 1) * LANES]
    for r in range(dilation):
        for j in range(tiles):
            out_ref[r, :, j * LANES:(j + 1) * LANES] = (
                nat_ref.at[j][pl.ds(r, n, stride=dilation), :].astype(out_ref.dtype))


def _natural_scratch(tm):
    return pltpu.VMEM((D_MODEL // LANES, tm, LANES), F32)


def _head_selector():
    lane_head = lax.broadcasted_iota(jnp.int32, (D_MODEL, LANES), 0) // HEAD_DIM
    head = lax.broadcasted_iota(jnp.int32, (D_MODEL, LANES), 1)
    return (lane_head == head).astype(BF16)


def _dot_split(v, m01, dims):
    hi = v.astype(BF16)
    lo = (v - hi.astype(F32)).astype(BF16)
    return (lax.dot_general(hi, m01, dims, preferred_element_type=F32)
            + lax.dot_general(lo, m01, dims, preferred_element_type=F32))


def _merge_fwd(o_parts, lse_parts, z, sel, name):
    tm = 256
    h_spec = pl.BlockSpec((tm, LANES), lambda i: (i, 0))

    def body(o0, o1, o2, l0, l1, l2, z_ref, sel_ref, u_ref, ut_ref, o_ref, lse_ref, nat):
        ls = [_load_natural(l, nat, d) for l, d in zip((l0, l1, l2), DILATIONS)]
        m = jnp.maximum(jnp.maximum(ls[0], ls[1]), ls[2])
        tot = m + jnp.log(jnp.exp(ls[0] - m) + jnp.exp(ls[1] - m) + jnp.exp(ls[2] - m))
        o = jnp.zeros((tm, D_MODEL), F32)
        for o_in, l, d in zip((o0, o1, o2), ls, DILATIONS):
            weight = _dot_split(jnp.exp(l - tot), sel_ref[...], NT_DIMS)
            o = o + weight * _load_natural(o_in, nat, d)
        zv = z_ref[...].astype(F32)
        u = o * (zv * _sigmoid(zv))
        u_ref[...] = u.astype(BF16)
        ut_ref[...] = u.T.astype(BF16)
        o_ref[...] = o
        lse_ref[...] = tot

    return pl.pallas_call(
        body, name=name, grid=(SEQ // tm,),
        in_specs=[_class_spec(tm, d) for d in DILATIONS] + [_class_spec(tm, d, LANES) for d in DILATIONS]
        + [_row_spec(tm, D_MODEL, B_Z_SEGMENT), _vec_spec(D_MODEL, LANES)],
        out_specs=[_row_spec(tm, D_MODEL), pl.BlockSpec((D_MODEL, tm), lambda i: (0, i)),
                   _row_spec(tm, D_MODEL), h_spec],
        out_shape=[jax.ShapeDtypeStruct((SEQ, D_MODEL), BF16), jax.ShapeDtypeStruct((D_MODEL, SEQ), BF16),
                   jax.ShapeDtypeStruct((SEQ, D_MODEL), F32), jax.ShapeDtypeStruct((SEQ, LANES), F32)],
        scratch_shapes=[_natural_scratch(tm)],
        compiler_params=_params("parallel"),
    )(*o_parts, *lse_parts, z, sel)


def _merge_bwd(dy, w_out, o, lse, z, sel, name):
    tm = 256
    n_d = len(DILATIONS)

    def body(dy_ref, w_ref, o_ref, lse_ref, z_ref, sel_ref, dz_ref, *rest):
        do_refs, delta_refs, lse_refs, nat = rest[:n_d], rest[n_d:2 * n_d], rest[2 * n_d:3 * n_d], rest[-1]
        zv = z_ref[...].astype(F32)
        sz = _sigmoid(zv)
        duv = lax.dot_general(dy_ref[...], w_ref[...], NT_DIMS, preferred_element_type=F32)
        ov = o_ref[...]
        do = duv * (zv * sz)
        dz_ref[...] = (duv * ov * (sz * (1.0 + zv * (1.0 - sz)))).astype(BF16)
        delta = _dot_split(do * ov, sel_ref[...], (((1,), (0,)), ((), ())))
        lv = lse_ref[...]
        for i, d in enumerate(DILATIONS):
            _store_classes(do_refs[i], do, nat, d)
            _store_classes(delta_refs[i], delta, nat, d)
            _store_classes(lse_refs[i], lv, nat, d)

    res = pl.pallas_call(
        body, name=name, grid=(SEQ // tm,),
        in_specs=[_row_spec(tm, D_MODEL), _vec_spec(D_MODEL, D_MODEL), _row_spec(tm, D_MODEL), _row_spec(tm, LANES),
                  _row_spec(tm, D_MODEL, B_Z_SEGMENT), _vec_spec(D_MODEL, LANES)],
        out_specs=[_row_spec(tm, D_MODEL)] + [_class_spec(tm, d) for d in DILATIONS]
        + [_class_spec(tm, d, LANES) for d in DILATIONS] * 2,
        out_shape=[jax.ShapeDtypeStruct((SEQ, D_MODEL), BF16)] + [_class_shape(d, BF16) for d in DILATIONS]
        + [_class_shape(d, F32, LANES) for d in DILATIONS] * 2,
        scratch_shapes=[_natural_scratch(tm)],
        compiler_params=_params("parallel"),
    )(dy, w_out, o, lse, z, sel)
    flat = lambda a: a.reshape(SEQ, a.shape[-1])
    return (res[0], [flat(a) for a in res[1:1 + n_d]], [flat(a) for a in res[1 + n_d:1 + 2 * n_d]],
            [flat(a) for a in res[1 + 2 * n_d:]])


def _attn_bwd(q, k, proj, group, do, lse, delta, slopes, dilation, name):
    bpc = SEQ // dilation // ATTN_BLOCK
    heads = ATTN_HEADS_BWD
    n_blocks = SEQ // ATTN_BLOCK
    carry = bpc > 1
    width = heads * HEAD_DIM
    cur, prev = _attn_specs(heads)
    v_cur, v_prev = _attn_specs(heads, segment=3 * group + 2)
    assert heads == N_HEADS
    per_head = pl.BlockSpec((ATTN_BLOCK, LANES), lambda hg, b: (jnp.minimum(b, n_blocks - 1), 0))
    scale = HEAD_DIM ** -0.5

    def body(sl_ref, q_ref, kp_ref, kc_ref, vp_ref, vc_ref, do_ref, lse_ref, dl_ref,
             dq_ref, dk_ref, dv_ref, *scratch):
        b = pl.program_id(1)
        if carry:
            dk_carry, dv_carry = scratch

            @pl.when(b == n_blocks)
            def _():
                dk_ref[...] = dk_carry[...].astype(BF16)
                dv_ref[...] = dv_carry[...].astype(BF16)

            @pl.when(b < n_blocks)
            def _():
                step(sl_ref, q_ref, kp_ref, kc_ref, vp_ref, vc_ref, do_ref, lse_ref, dl_ref,
                     dq_ref, dk_ref, dv_ref, dk_carry, dv_carry, b)
        else:
            step(sl_ref, q_ref, kp_ref, kc_ref, vp_ref, vc_ref, do_ref, lse_ref, dl_ref,
                 dq_ref, dk_ref, dv_ref, None, None, b)

    def step(sl_ref, q_ref, kp_ref, kc_ref, vp_ref, vc_ref, do_ref, lse_ref, dl_ref,
             dq_ref, dk_ref, dv_ref, dk_carry, dv_carry, b):
        if carry:
            @pl.when(b == 0)
            def _():
                dk_carry[...] = jnp.zeros_like(dk_carry)
                dv_carry[...] = jnp.zeros_like(dv_carry)

        dist, valid = _attn_masks(b, bpc, dilation, None)
        q3 = _head_stack(lambda cols: q_ref[:, cols], heads)
        k3 = _head_stack(lambda cols: _key_tile(kp_ref, kc_ref, cols, bpc), heads)
        v3 = _head_stack(lambda cols: _key_tile(vp_ref, vc_ref, cols, bpc), heads)
        do3 = _head_stack(lambda cols: do_ref[:, cols], heads)
        lse3 = jnp.stack([lse_ref[:, h:h + 1] for h in range(heads)], axis=0)
        dl3 = jnp.stack([dl_ref[:, h:h + 1] for h in range(heads)], axis=0)
        s = lax.dot_general(q3, k3, BATCH_NT_DIMS, preferred_element_type=F32)
        p = jnp.exp(jnp.where(valid[None], s * scale - dist[None] * sl_ref[...], NEG_INF) - lse3)
        dp = lax.dot_general(do3, v3, BATCH_NT_DIMS, preferred_element_type=F32)
        ds = (p * (dp - dl3) * scale).astype(BF16)
        dq3 = lax.dot_general(ds, k3, BATCH_NN_DIMS, preferred_element_type=F32)
        dk3 = lax.dot_general(ds, q3, BATCH_TN_DIMS, preferred_element_type=F32)
        dv3 = lax.dot_general(p.astype(BF16), do3, BATCH_TN_DIMS, preferred_element_type=F32)
        for h in range(heads):
            cols = slice(h * HEAD_DIM, (h + 1) * HEAD_DIM)
            dq_ref[:, cols] = dq3[h].astype(BF16)
            if carry:
                dk_ref[:, cols] = (dk_carry[:, cols] + dk3[h, :ATTN_BLOCK]).astype(BF16)
                dv_ref[:, cols] = (dv_carry[:, cols] + dv3[h, :ATTN_BLOCK]).astype(BF16)
                dk_carry[:, cols] = dk3[h, ATTN_BLOCK:]
                dv_carry[:, cols] = dv3[h, ATTN_BLOCK:]
            else:
                dk_ref[:, cols] = dk3[h].astype(BF16)
                dv_ref[:, cols] = dv3[h].astype(BF16)

    kv_out = prev if carry else cur
    return pl.pallas_call(
        body, name=name, grid=(N_HEADS // heads, n_blocks + (1 if carry else 0)),
        in_specs=[pl.BlockSpec((heads, 1, 1), lambda hg, b: (hg, 0, 0)), cur, prev, cur, v_prev, v_cur,
                  cur, per_head, per_head],
        out_specs=[cur, kv_out, kv_out],
        out_shape=[jax.ShapeDtypeStruct((SEQ, D_MODEL), BF16)] * 3,
        scratch_shapes=[pltpu.VMEM((ATTN_BLOCK, width), F32)] * 2 if carry else [],
        compiler_params=_params("parallel", "arbitrary"),
    )(slopes.reshape(N_HEADS, 1, 1), q, k, k, proj, proj, do, lse, delta)


def _qknorm_bwd(proj, group, qw, kw, seg, dq, dk, dv, name):
    tm = 256

    def body(q_in, k_in, qw_ref, kw_ref, seg_ref, dq_ref, dk_ref, dv_ref, dproj_ref, sums_ref):
        segv = seg_ref[...]
        sums = []
        for part, (raw_ref, w_ref, dn_ref) in enumerate(((q_in, qw_ref, dq_ref), (k_in, kw_ref, dk_ref))):
            raw = raw_ref[...].astype(F32)
            dn = dn_ref[...].astype(F32)
            r = _qk_rstd(raw, segv)
            gq = dn * w_ref[...]
            draw = r * gq - raw * (r * r * r) * (_segsum(raw * gq, segv) * (1.0 / HEAD_DIM))
            dproj_ref[:, part * D_MODEL:(part + 1) * D_MODEL] = draw.astype(BF16)
            sums.append(jnp.sum(dn * raw * r, axis=0, keepdims=True))
        dproj_ref[:, 2 * D_MODEL:] = dv_ref[...]

        @pl.when(pl.program_id(0) == 0)
        def _():
            sums_ref[...] = jnp.zeros_like(sums_ref)

        sums_ref[...] += jnp.concatenate(sums + [jnp.zeros((6, D_MODEL), F32)], axis=0)

    return pl.pallas_call(
        body, name=name, grid=(SEQ // tm,),
        in_specs=[_row_spec(tm, D_MODEL, 3 * group), _row_spec(tm, D_MODEL, 3 * group + 1),
                  _vec_spec(1, D_MODEL), _vec_spec(1, D_MODEL), _vec_spec(256, 256)] + [_row_spec(tm, D_MODEL)] * 3,
        out_specs=[_row_spec(tm, 3 * D_MODEL), _vec_spec(8, D_MODEL)],
        out_shape=[jax.ShapeDtypeStruct((SEQ, 3 * D_MODEL), BF16), jax.ShapeDtypeStruct((8, D_MODEL), F32)],
        compiler_params=_params("arbitrary"),
    )(proj, proj, qw, kw, seg, dq, dk, dv)


B_TN = 512
B_GROUP_TILES = 3 * D_MODEL // B_TN
B_Z_TILE0 = 3 * B_GROUP_TILES
B_Z_TILES = D_MODEL // B_TN
B_TILES = B_Z_TILE0 + B_Z_TILES
B_Z_SEGMENT = 3 * len(DILATIONS)


def _local_step(x, target, mods, norm_g, conv_w, conv_b, ln_g, ln_b, q_norm, k_norm, chip, own_wb_in,
                weights_a, weights_b, forward_weights_b, send_grads_b, forward_grads_b, send_grads_a):
    row = lambda a, i: a[i:i + 1]
    shift0, scale0, gate0 = row(mods[0], 0), row(mods[0], 1), row(mods[0], 2)
    shift1, scale1, gate1 = row(mods[1], 0), row(mods[1], 1), row(mods[1], 2)
    g0, g1 = row(norm_g, 0), row(norm_g, 1)
    seg = _seg_matrix()
    slopes = jnp.exp2(-8.0 * jnp.arange(1, N_HEADS + 1, dtype=F32) / N_HEADS)
    qw = [jnp.tile(q_norm[g:g + 1], (1, N_HEADS)) for g in range(3)]
    kw = [jnp.tile(k_norm[g:g + 1], (1, N_HEADS)) for g in range(3)]

    h0, h0t = _normmod_fwd(x, g0, scale0, shift0, "prenorm0")
    wa_in, wa_out = weights_a(h0)
    ja, _, nsa = wa_in.shape
    proj_a = _mm(h0, wa_in, tn=nsa, tile0=0, n_tiles=ja, out_dtype=F32, name="a_in")
    u5, u5t, u2 = _conv_fwd(proj_a, conv_w, conv_b, ln_g, ln_b, "a_conv")
    x1, y_a, h1t, h1c = _out_a(u5, wa_out, x, gate0, g1, scale1, shift1, "a_out")

    own_tiles = B_TILES // N_CHIPS
    step = jnp.arange(B_TILES, dtype=jnp.int32)
    tiles = (own_tiles * chip + step) % B_TILES
    own_ids = jnp.stack([step[:own_tiles], tiles[:own_tiles]])
    rest_ids = jnp.stack([tiles[own_tiles:], tiles[own_tiles:]])
    proj_b = _b_in_tiles(h1c, own_wb_in, own_ids, own_tiles, "b_in_own")
    forward_weights_b(proj_b)
    wb_in, wb_out = weights_b(proj_b)
    jb, _, nsb = wb_in.shape
    proj_b = _b_in_tiles(h1c, wb_in, rest_ids, B_TILES - own_tiles, "b_in_rest", prev=proj_b)
    h1 = h1c[0]
    qkv, o_parts, lse_parts = [], [], []
    for g, d in enumerate(DILATIONS):
        qn, kn = _qknorm_fwd(proj_b, g, qw[g], kw[g], seg, f"b_qknorm_g{g}")
        og, lg = _attn_fwd(qn, kn, proj_b, g, slopes, d, f"b_attn_g{g}")
        qkv.append((qn, kn))
        o_parts.append(og if d == 1 else og.reshape(d, SEQ // d, D_MODEL))
        lse_parts.append(lg if d == 1 else lg.reshape(d, SEQ // d, LANES))
    sel = _head_selector()
    u_b, u_bt, o_b, lse_b = _merge_fwd(o_parts, lse_parts, proj_b, sel, "b_merge")
    e, dy_b, sums_loss = _out_b_loss(u_b, wb_out, x1, gate1, target, "b_out_loss")

    dwb_out = _mm(u_bt, dy_b, tn=D_MODEL, tile0=0, n_tiles=1, out_dtype=BF16, name="b_dwout")
    dz_b, do_c, delta_c, lse_c = _merge_bwd(dy_b, wb_out, o_b, lse_b, proj_b, sel, "b_merge_bwd")
    dwb_in = _mm(h1t, dz_b, tn=B_TN, tile0=B_Z_TILE0, n_tiles=B_Z_TILES, out_dtype=BF16, name="b_dwin_z",
                 out3d=(jb, nsb))
    dh1_parts = [_mm_nt(dz_b, wb_in, tn=B_TN, tile0=B_Z_TILE0, n_tiles=B_Z_TILES, name="b_dh_z")]
    qk_sums = []
    for g, d in enumerate(DILATIONS):
        qn, kn = qkv[g]
        dq, dk, dv = _attn_bwd(qn, kn, proj_b, g, do_c[g], lse_c[g], delta_c[g], slopes, d, f"b_attn_bwd_g{g}")
        dproj, sums_qk = _qknorm_bwd(proj_b, g, qw[g], kw[g], seg, dq, dk, dv, f"b_qknorm_bwd_g{g}")
        qk_sums.append(sums_qk)
        dwb_in = _mm(h1t if d == 1 else h1c[g], dproj, tn=B_TN, tile0=g * B_GROUP_TILES, n_tiles=B_GROUP_TILES,
                     out_dtype=BF16, name=f"b_dwin_g{g}", out3d=(jb, nsb), prev=dwb_in, transpose_lhs=d != 1)
        dh = _mm_nt(dproj, wb_in, tn=B_TN, tile0=g * B_GROUP_TILES, n_tiles=B_GROUP_TILES, name=f"b_dh_g{g}")
        dh1_parts.append(dh)
    token = send_grads_b(dwb_in, dwb_out)
    dx1, sums_n1, dy_a = _normmod_bwd(x1, g1, scale1 + token[0:1, 0:1], dh1_parts, e, "prenorm1_bwd",
                                      part_dilations=(1,) + DILATIONS, gated=(gate0, y_a))
    token = forward_grads_b(dx1)

    dwa_out = _mm(u5t, dy_a, tn=D_MODEL, tile0=0, n_tiles=1, out_dtype=BF16, name="a_dwout")
    du2, dz_a, sums_ln = _conv_bwd_pointwise(dy_a, wa_out, proj_a, u2, ln_g + token[0:1, 0:1], ln_b,
                                             "a_conv_bwd_pw")
    dproj_a, dconv_w = _conv_bwd_taps(du2, dz_a, proj_a, conv_w, "a_conv_bwd_taps")
    dwa_in = _mm(h0t, dproj_a, tn=nsa, tile0=0, n_tiles=ja, out_dtype=BF16, name="a_dwin", out3d=(ja, nsa))
    token = send_grads_a(dwa_in, dwa_out)
    dh0 = _mm_nt(dproj_a, wa_in, tn=nsa, tile0=0, n_tiles=ja, name="a_dh", after=token)
    grad_x, sums_n0 = _normmod_bwd(x, g0, scale0, [dh0], dx1, "prenorm0_bwd")

    small = dict(
        dnorm_g=jnp.concatenate([sums_n0[0:1], sums_n1[0:1]], axis=0),
        dmod0=jnp.concatenate([sums_n0[2:3], sums_n0[1:2], sums_n1[3:4]], axis=0),
        dmod1=jnp.concatenate([sums_n1[2:3], sums_n1[1:2], sums_loss[0:1]], axis=0),
        dln_g=sums_ln[0:1], dln_b=sums_ln[1:2], dconv_b=sums_ln[2:3],
        dconv_w=dconv_w[:CONV_WIDTH],
        dq_norm=jnp.concatenate([s[0:1] for s in qk_sums], axis=0),
        dk_norm=jnp.concatenate([s[1:2] for s in qk_sums], axis=0),
        loss_cols=sums_loss[1:2],
    )
    return grad_x, small


def _adamw(w, g, m, v, name, after=None, copy_grad=False):
    rows, cols = w.shape
    tr = rows if rows <= 128 else 128
    c1 = 1.0 / (1.0 - ADAM_B1 ** ADAM_STEP)
    c2 = 1.0 / (1.0 - ADAM_B2 ** ADAM_STEP)
    extra = [] if after is None else [after]
    n_out = 4 if copy_grad else 3

    def body(w_ref, g_ref, m_ref, v_ref, *rest):
        d_ref, mo_ref, vo_ref = rest[len(extra):len(extra) + 3]
        gv = g_ref[...]
        if copy_grad:
            rest[-1][...] = gv
        mn = ADAM_B1 * m_ref[...] + (1.0 - ADAM_B1) * gv
        vn = ADAM_B2 * v_ref[...] + (1.0 - ADAM_B2) * (gv * gv)
        mo_ref[...] = mn
        vo_ref[...] = vn
        d_ref[...] = -ADAM_LR * ((mn * c1) / (jnp.sqrt(vn * c2) + ADAM_EPS) + ADAM_WD * w_ref[...])

    spec = pl.BlockSpec((tr, cols), lambda i: (i, 0))
    return pl.pallas_call(
        body, name=name, grid=(rows // tr,),
        in_specs=[spec] * 4 + [pl.BlockSpec(memory_space=pl.ANY)] * len(extra), out_specs=[spec] * n_out,
        out_shape=[jax.ShapeDtypeStruct((rows, cols), F32)] * n_out,
        compiler_params=_params("parallel"),
    )(w, g, m, v, *extra)


def _cast_into_slot(w, chip_idx, name, keep_own=False, after=None):
    rows, cols = w.shape
    tr = 256
    extra = [] if after is None else [after]

    def body(ch_ref, w_ref, *rest):
        wb = w_ref[...].astype(BF16)
        for o_ref in rest[len(extra):]:
            o_ref[...] = wb

    slot_spec = pl.BlockSpec((None, tr, cols), lambda i, ch: (ch[0], i, 0))
    own_spec = pl.BlockSpec((None, tr, cols), lambda i, ch: (0, i, 0))
    res = pl.pallas_call(
        body, name=name,
        grid_spec=pltpu.PrefetchScalarGridSpec(
            num_scalar_prefetch=1, grid=(rows // tr,),
            in_specs=[pl.BlockSpec((tr, cols), lambda i, ch: (i, 0))] + [pl.BlockSpec(memory_space=pl.ANY)] * len(extra),
            out_specs=[slot_spec, own_spec] if keep_own else [slot_spec]),
        out_shape=[jax.ShapeDtypeStruct((N_CHIPS, rows, cols), BF16)]
        + ([jax.ShapeDtypeStruct((1, rows, cols), BF16)] if keep_own else []),
        compiler_params=_params("parallel"),
    )(chip_idx, w, *extra)
    return tuple(res) if keep_own else res[0]


def _position():
    x, y, c = lax.axis_index("x"), lax.axis_index("y"), lax.axis_index("c")
    return x, y, c


def _xor_peer(x, y, c, k):
    return (x ^ ((k >> 2) & 1), y ^ ((k >> 1) & 1), c ^ (k & 1))


def _chip_peer(x, y, k):
    return (x ^ ((k >> 1) & 1), y ^ (k & 1))


def _ada_forward(c_row, ada_w, ada_b, conv_w):
    ns = ada_w.shape[2]
    cw = conv_w.shape[1]

    def body(c_ref, w_ref, b_ref, cv_ref, mod_ref, sc_ref, cvo_ref,
             c_all, mp, parts, cv_parts, send1, recv1, send2, recv2, send3, recv3):
        x, y, c = _position()
        me = 4 * x + 2 * y + c
        chip = 2 * x + y

        def c_copy(k):
            return pltpu.make_async_remote_copy(
                src_ref=c_all.at[me], dst_ref=c_all.at[me], send_sem=send1.at[k - 1], recv_sem=recv1.at[k - 1],
                device_id=_xor_peer(x, y, c, k), device_id_type=MESH)

        def cv_copy(k):
            px, py = _chip_peer(x, y, k)
            return pltpu.make_async_remote_copy(
                src_ref=cv_parts.at[chip], dst_ref=cv_parts.at[chip], send_sem=send3.at[k - 1],
                recv_sem=recv3.at[k - 1], device_id=(px, py, c), device_id_type=MESH)

        c_all[me] = c_ref[...]
        cv_parts[chip] = cv_ref[...]
        for k in range(1, N_DEV):
            c_copy(k).start()
        for k in range(1, N_CHIPS):
            cv_copy(k).start()
        for k in range(1, N_DEV):
            c_copy(k).wait_recv()
        cv = jnp.concatenate([c_all[i] for i in range(N_DEV)], axis=0)
        sc = cv * _sigmoid(cv)
        sc_ref[...] = sc
        for l in range(2):
            res = jnp.dot(sc, w_ref[l], preferred_element_type=F32, precision=lax.Precision.HIGHEST)
            for i in range(N_DEV):
                mp[i, l:l + 1, :] = res[i:i + 1, :]

        def mod_copy(k):
            px, py = _chip_peer(x, y, k)
            return pltpu.make_async_remote_copy(
                src_ref=mp.at[4 * px + 2 * py + c], dst_ref=parts.at[chip], send_sem=send2.at[k - 1],
                recv_sem=recv2.at[k - 1], device_id=(px, py, c), device_id_type=MESH)

        for k in range(1, N_CHIPS):
            mod_copy(k).start()
        parts[chip] = mp[me]
        for k in range(1, N_CHIPS):
            mod_copy(k).wait_recv()
            cv_copy(k).wait_recv()
        mod_ref[...] = jnp.concatenate([parts[j] for j in range(N_CHIPS)], axis=1) + b_ref[...]
        cvo_ref[...] = jnp.concatenate([cv_parts[j] for j in range(N_CHIPS)], axis=1)
        for k in range(1, N_DEV):
            c_copy(k).wait_send()
        for k in range(1, N_CHIPS):
            mod_copy(k).wait_send()
            cv_copy(k).wait_send()

    vm = pl.BlockSpec(memory_space=pltpu.VMEM)
    return pl.pallas_call(
        body, name="ada_forward",
        in_specs=[vm] * 4, out_specs=[vm] * 3,
        out_shape=[jax.ShapeDtypeStruct((2, 3 * D_MODEL), F32), jax.ShapeDtypeStruct((N_DEV, D_MODEL), F32),
                   jax.ShapeDtypeStruct((CONV_WIDTH, N_CHIPS * cw), F32)],
        scratch_shapes=[pltpu.VMEM((N_DEV, 1, D_MODEL), F32), pltpu.VMEM((N_DEV, 2, ns), F32),
                        pltpu.VMEM((N_CHIPS, 2, ns), F32), pltpu.VMEM((N_CHIPS, CONV_WIDTH, cw), F32),
                        pltpu.SemaphoreType.DMA((N_DEV - 1,)), pltpu.SemaphoreType.DMA((N_DEV - 1,)),
                        pltpu.SemaphoreType.DMA((N_CHIPS - 1,)), pltpu.SemaphoreType.DMA((N_CHIPS - 1,)),
                        pltpu.SemaphoreType.DMA((N_CHIPS - 1,)), pltpu.SemaphoreType.DMA((N_CHIPS - 1,))],
        compiler_params=pltpu.CompilerParams(vmem_limit_bytes=VMEM_LIMIT_BYTES),
    )(c_row, ada_w, ada_b, conv_w)


HBM_SPEC = pl.BlockSpec(memory_space=pltpu.HBM)
ANY_SPEC = pl.BlockSpec(memory_space=pl.ANY)
SEM_SPEC = pl.BlockSpec(memory_space=pltpu.SEMAPHORE)
SPLIT_PARAMS = dict(compiler_params=pltpu.CompilerParams(has_side_effects=pltpu.SideEffectType.DATAFLOW_SIDE_EFFECTING))
TOKEN = jax.ShapeDtypeStruct((8, 128), F32)


def _hbm(arrays):
    return [pltpu.with_memory_space_constraint(a, pltpu.HBM) for a in arrays]


def _hbm_like(arrays):
    return [pltpu.HBM(a.shape, a.dtype) for a in arrays]


def _gather_start(lands, after, name):
    n = len(lands)

    def body(*refs):
        ins = refs[:n]
        send, recv = refs[n + 1], refs[n + 2]
        x, y, c = _position()
        chip = 2 * x + y
        for t in range(n):
            rh = ins[t].shape[1] // 2
            for k in range(1, N_CHIPS):
                px, py = _chip_peer(x, y, k)
                block = ins[t].at[chip, pl.ds(c * rh, rh)]
                pltpu.make_async_remote_copy(
                    src_ref=block, dst_ref=block, send_sem=send.at[3 * t + k - 1], recv_sem=recv.at[3 * t + k - 1],
                    device_id=(px, py, c), device_id_type=MESH).start()
        refs[-1][...] = jnp.zeros(TOKEN.shape, F32)

    res = pl.pallas_call(
        body, name=name, in_specs=[HBM_SPEC] * n + [ANY_SPEC],
        out_specs=(SEM_SPEC, SEM_SPEC, *[HBM_SPEC] * n, pl.BlockSpec(memory_space=pltpu.VMEM)),
        out_shape=(pltpu.SemaphoreType.DMA((3 * n,)), pltpu.SemaphoreType.DMA((3 * n,)), *_hbm_like(lands), TOKEN),
        input_output_aliases={t: 2 + t for t in range(n)}, **SPLIT_PARAMS,
    )(*_hbm(lands), after)
    return res[0], res[1], list(res[2:2 + n]), res[-1]


def _gather_forward(send, recv, lands, after, name):
    n = len(lands)

    def body(*refs):
        ins = refs[:n]
        send1, recv1 = refs[n], refs[n + 1]
        send2, recv2 = refs[n + 3], refs[n + 4]
        x, y, c = _position()
        chip = 2 * x + y
        for t in range(n):
            rh = ins[t].shape[1] // 2
            half = pl.ds(c * rh, rh)
            for k in range(1, N_CHIPS):
                px, py = _chip_peer(x, y, k)
                s = 3 * t + k - 1
                got = ins[t].at[2 * px + py, half]
                cp = pltpu.make_async_remote_copy(
                    src_ref=ins[t].at[chip, half], dst_ref=got, send_sem=send1.at[s], recv_sem=recv1.at[s],
                    device_id=(px, py, c), device_id_type=MESH)
                cp.wait_send()
                cp.wait_recv()
                pltpu.make_async_remote_copy(
                    src_ref=got, dst_ref=got, send_sem=send2.at[s], recv_sem=recv2.at[s],
                    device_id=(x, y, 1 - c), device_id_type=MESH).start()
        refs[-1][...] = jnp.zeros(TOKEN.shape, F32)

    res = pl.pallas_call(
        body, name=name, in_specs=[HBM_SPEC] * n + [SEM_SPEC, SEM_SPEC, ANY_SPEC],
        out_specs=(SEM_SPEC, SEM_SPEC, *[HBM_SPEC] * n, pl.BlockSpec(memory_space=pltpu.VMEM)),
        out_shape=(pltpu.SemaphoreType.DMA((3 * n,)), pltpu.SemaphoreType.DMA((3 * n,)), *_hbm_like(lands), TOKEN),
        input_output_aliases={t: 2 + t for t in range(n)}, **SPLIT_PARAMS,
    )(*lands, send, recv, after)
    return res[0], res[1], list(res[2:2 + n]), res[-1]


def _gather_wait(send, recv, lands, after, name):
    n = len(lands)

    def body(*refs):
        ins = refs[:n]
        send_ref, recv_ref = refs[n], refs[n + 1]
        x, y, c = _position()
        for t in range(n):
            rh = ins[t].shape[1] // 2
            for k in range(1, N_CHIPS):
                px, py = _chip_peer(x, y, k)
                cp = pltpu.make_async_remote_copy(
                    src_ref=ins[t].at[2 * px + py, pl.ds(c * rh, rh)],
                    dst_ref=ins[t].at[2 * px + py, pl.ds((1 - c) * rh, rh)], send_sem=send_ref.at[3 * t + k - 1],
                    recv_sem=recv_ref.at[3 * t + k - 1], device_id=(x, y, 1 - c), device_id_type=MESH)
                cp.wait_send()
                cp.wait_recv()

    res = pl.pallas_call(
        body, name=name, in_specs=[HBM_SPEC] * n + [SEM_SPEC, SEM_SPEC, ANY_SPEC], out_specs=[HBM_SPEC] * n,
        out_shape=_hbm_like(lands), input_output_aliases={t: t for t in range(n)}, **SPLIT_PARAMS,
    )(*lands, send, recv, after)
    return list(res)


def _split_start(name, arrays, n_sems, after, issue):
    m = len(arrays)

    def body(*refs):
        issue(refs[:m], refs[m + 1], refs[m + 2])
        refs[-1][...] = jnp.zeros(TOKEN.shape, F32)

    res = pl.pallas_call(
        body, name=name, in_specs=[HBM_SPEC] * m + [ANY_SPEC],
        out_specs=(SEM_SPEC, SEM_SPEC, *[HBM_SPEC] * m, pl.BlockSpec(memory_space=pltpu.VMEM)),
        out_shape=(pltpu.SemaphoreType.DMA((n_sems,)), pltpu.SemaphoreType.DMA((n_sems,)), *_hbm_like(arrays), TOKEN),
        input_output_aliases={t: 2 + t for t in range(m)}, **SPLIT_PARAMS,
    )(*_hbm(arrays), after)
    return res[0], res[1], list(res[2:2 + m]), res[-1]


def _split_wait(name, arrays, send, recv, after, await_all):
    m = len(arrays)

    def body(*refs):
        await_all(refs[:m], refs[m], refs[m + 1])

    res = pl.pallas_call(
        body, name=name, in_specs=[HBM_SPEC] * m + [SEM_SPEC, SEM_SPEC, ANY_SPEC], out_specs=[HBM_SPEC] * m,
        out_shape=_hbm_like(arrays), input_output_aliases={t: t for t in range(m)}, **SPLIT_PARAMS,
    )(*arrays, send, recv, after)
    return list(res)


def _sibling_copies(refs, send, recv, n):
    x, y, c = _position()
    cps = []
    for t in range(n):
        rh = refs[t].shape[1] // 2
        cps.append(pltpu.make_async_remote_copy(
            src_ref=refs[t].at[pl.ds(0, N_CHIPS), pl.ds((1 - c) * rh, rh)], dst_ref=refs[n + t],
            send_sem=send.at[t], recv_sem=recv.at[t], device_id=(x, y, 1 - c), device_id_type=MESH))
    return cps


def _reduce_sibling_start(grads, after, name):
    n = len(grads)
    lands = [lax.empty((N_CHIPS, g.shape[1] // 2, g.shape[2]), BF16) for g in grads]

    def issue(refs, send, recv):
        for cp in _sibling_copies(refs, send, recv, n):
            cp.start()

    return _split_start(name, list(grads) + lands, n, after, issue)


def _reduce_sibling_wait(send, recv, arrays, after, name):
    n = len(arrays) // 2

    def await_all(refs, send_ref, recv_ref):
        for cp in _sibling_copies(refs, send_ref, recv_ref, n):
            cp.wait_send()
            cp.wait_recv()

    res = _split_wait(name, arrays, send, recv, after, await_all)
    return res[:n], res[n:]


def _add_sibling_half(grad, got, dev_idx, name):
    j, r, cols = grad.shape
    rh = r // 2
    tr = rh
    nb = rh // tr

    def body(idx_ref, g_ref, got_ref, out_ref):
        out_ref[...] = (g_ref[...].astype(F32) + got_ref[...].astype(F32)).astype(BF16)

    return pl.pallas_call(
        body, name=name,
        grid_spec=pltpu.PrefetchScalarGridSpec(
            num_scalar_prefetch=1, grid=(j, nb),
            in_specs=[pl.BlockSpec((None, tr, cols), lambda jj, i, idx: (jj, idx[2] * nb + i, 0)),
                      pl.BlockSpec((None, tr, cols), lambda jj, i, idx: (jj, i, 0))],
            out_specs=pl.BlockSpec((None, tr, cols), lambda jj, i, idx: (jj, i, 0))),
        out_shape=jax.ShapeDtypeStruct((j, rh, cols), BF16),
        compiler_params=_params("parallel", "parallel"),
    )(dev_idx, grad, got)


def _chip_copies(refs, send, recv, n, receiving):
    x, y, c = _position()
    chip = 2 * x + y
    cps = []
    for t in range(n):
        for k in range(1, N_CHIPS):
            px, py = _chip_peer(x, y, k)
            cps.append(pltpu.make_async_remote_copy(
                src_ref=refs[t].at[2 * px + py], dst_ref=refs[n + t].at[2 * px + py if receiving else chip],
                send_sem=send.at[3 * t + k - 1], recv_sem=recv.at[3 * t + k - 1],
                device_id=(px, py, c), device_id_type=MESH))
    return cps


def _reduce_chips_start(partials, after, name):
    n = len(partials)
    lands = [lax.empty(p.shape, BF16) for p in partials]

    def issue(refs, send, recv):
        for cp in _chip_copies(refs, send, recv, n, False):
            cp.start()

    return _split_start(name, list(partials) + lands, 3 * n, after, issue)


def _reduce_chips_wait(send, recv, arrays, after, name):
    n = len(arrays) // 2

    def await_all(refs, send_ref, recv_ref):
        for cp in _chip_copies(refs, send_ref, recv_ref, n, True):
            cp.wait_send()
            cp.wait_recv()

    res = _split_wait(name, arrays, send, recv, after, await_all)
    return res[:n], res[n:]


def _sum_partials(land, partial, dev_idx, name):
    _, rh, cols = land.shape
    tr = min(rh, 256)
    nb = rh // tr

    def body(idx_ref, l_ref, p_ref, o_ref):
        chip = idx_ref[1]
        acc = jnp.where(chip == 0, p_ref[...], l_ref[0]).astype(F32)
        for s in range(1, N_CHIPS):
            acc = acc + jnp.where(chip == s, p_ref[...], l_ref[s]).astype(F32)
        o_ref[...] = acc

    return pl.pallas_call(
        body, name=name,
        grid_spec=pltpu.PrefetchScalarGridSpec(
            num_scalar_prefetch=1, grid=(nb,),
            in_specs=[pl.BlockSpec((N_CHIPS, tr, cols), lambda i, idx: (0, i, 0)),
                      pl.BlockSpec((None, tr, cols), lambda i, idx: (idx[1], i, 0))],
            out_specs=pl.BlockSpec((tr, cols), lambda i, idx: (idx[2] * nb + i, 0))),
        out_shape=jax.ShapeDtypeStruct((2 * rh, cols), F32), compiler_params=_params("parallel"),
    )(dev_idx, land, partial)


def _half_copies(refs, send, recv, receiving):
    x, y, c = _position()
    cps = []
    for t, ref in enumerate(refs):
        rh = ref.shape[0] // 2
        cps.append(pltpu.make_async_remote_copy(
            src_ref=ref.at[pl.ds(c * rh, rh)], dst_ref=ref.at[pl.ds(((1 - c) if receiving else c) * rh, rh)],
            send_sem=send.at[t], recv_sem=recv.at[t], device_id=(x, y, 1 - c), device_id_type=MESH))
    return cps


def _share_halves_start(totals, after, name):
    def issue(refs, send, recv):
        for cp in _half_copies(refs, send, recv, False):
            cp.start()

    return _split_start(name, list(totals), len(totals), after, issue)


def _share_halves_wait(send, recv, totals, after, name):
    def await_all(refs, send_ref, recv_ref):
        for cp in _half_copies(refs, send_ref, recv_ref, True):
            cp.wait_send()
            cp.wait_recv()

    return _split_wait(name, totals, send, recv, after, await_all)


SMALL_ROWS = 56


def _small_copies(refs, send, recv, receiving):
    x, y, c = _position()
    me = 4 * x + 2 * y + c
    cps = []
    for k in range(1, N_DEV):
        px, py, pc = _xor_peer(x, y, c, k)
        cps.append(pltpu.make_async_remote_copy(
            src_ref=refs[0], dst_ref=refs[1].at[4 * px + 2 * py + pc if receiving else me],
            send_sem=send.at[k - 1], recv_sem=recv.at[k - 1], device_id=(px, py, pc), device_id_type=MESH))
    return cps


def _small_gather_start(packed, after):
    land = lax.empty((N_DEV,) + packed.shape, F32)

    def issue(refs, send, recv):
        for cp in _small_copies(refs, send, recv, False):
            cp.start()

    return _split_start("small_gather_start", [packed, land], N_DEV - 1, after, issue)


def _small_gather_wait(send, recv, arrays, after):
    def await_all(refs, send_ref, recv_ref):
        for cp in _small_copies(refs, send_ref, recv_ref, True):
            cp.wait_send()
            cp.wait_recv()

    return _split_wait("small_gather_wait", arrays, send, recv, after, await_all)


def _reduce_small(packed, land, silu_c):
    ns = 3 * D_MODEL // N_CHIPS

    def body(p_ref, land_ref, sc_ref, tot_ref, gw_ref, loss_ref, qk_ref, allp):
        x, y, c = _position()
        me = 4 * x + 2 * y + c
        chip = 2 * x + y
        for i in range(N_DEV):
            allp[i] = jnp.where(me == i, p_ref[...], land_ref[i])
        tot = allp[0]
        for i in range(1, N_DEV):
            tot = tot + allp[i]
        tot_ref[...] = tot
        loss_ref[...] = jnp.sum(tot[11:12, :], axis=1, keepdims=True) * (0.5 / D_MODEL)
        fold = tot[5:11, 0:HEAD_DIM]
        for h in range(1, N_HEADS):
            fold = fold + tot[5:11, h * HEAD_DIM:(h + 1) * HEAD_DIM]
        qk_ref[...] = jnp.concatenate([fold, jnp.zeros((2, HEAD_DIM), F32)], axis=0)
        sct = sc_ref[...].T
        rc = 64
        for l in range(2):
            dms = [allp[i, pl.ds(12 + 4 * l + chip, 1), :][:, :ns] for i in range(N_DEV)]
            for r0 in range(0, D_MODEL, rc):
                acc = sct[r0:r0 + rc, 0:1] * dms[0]
                for i in range(1, N_DEV):
                    acc = acc + sct[r0:r0 + rc, i:i + 1] * dms[i]
                gw_ref[l, r0:r0 + rc, :] = acc

    vm = pl.BlockSpec(memory_space=pltpu.VMEM)
    return pl.pallas_call(
        body, name="reduce_small", in_specs=[vm, vm, vm], out_specs=[vm] * 4,
        out_shape=[jax.ShapeDtypeStruct((SMALL_ROWS, D_MODEL), F32), jax.ShapeDtypeStruct((2, D_MODEL, ns), F32),
                   jax.ShapeDtypeStruct((1, 1), F32), jax.ShapeDtypeStruct((8, HEAD_DIM), F32)],
        scratch_shapes=[pltpu.VMEM((N_DEV, SMALL_ROWS, D_MODEL), F32)],
        compiler_params=pltpu.CompilerParams(vmem_limit_bytes=VMEM_LIMIT_BYTES),
    )(packed, land, silu_c)


def kernel(x, c, norm_g, ada_w, ada_b, a_w_in, a_conv_w, a_conv_b, a_ln_g, a_ln_b, a_w_out, b_w_in, b_q_norm, b_k_norm, b_w_out, loss_target, m_norm_g, m_ada_w, m_ada_b, m_a_w_in, m_a_conv_w, m_a_conv_b, m_a_ln_g, m_a_ln_b, m_a_w_out, m_b_w_in, m_b_q_norm, m_b_k_norm, m_b_w_out, v_norm_g, v_ada_w, v_ada_b, v_a_w_in, v_a_conv_w, v_a_conv_b, v_a_ln_g, v_a_ln_b, v_a_w_out, v_b_w_in, v_b_q_norm, v_b_k_norm, v_b_w_out):
    chip = 2 * lax.axis_index("x") + lax.axis_index("y")
    core = lax.axis_index("c")
    chip_idx = chip.astype(jnp.int32).reshape(1)
    dev_idx = jnp.stack([2 * chip + core, chip, core]).astype(jnp.int32)

    mods, silu_c, conv_w_full = _ada_forward(c, ada_w, ada_b, a_conv_w[0])
    lands_a = [_cast_into_slot(a_w_in[0], chip_idx, "cast_a_w_in"), _cast_into_slot(a_w_out[0], chip_idx, "cast_a_w_out")]
    send_a, recv_a, lands_a, token_a = _gather_start(lands_a, mods, "gather_start_a")
    land_b_in, own_wb_in = _cast_into_slot(b_w_in[0], chip_idx, "cast_b_w_in", keep_own=True, after=token_a)
    lands_b = [land_b_in, _cast_into_slot(b_w_out[0], chip_idx, "cast_b_w_out", after=token_a)]
    send_b, recv_b, lands_b, token_b = _gather_start(lands_b, token_a, "gather_start_b")
    mods = mods + token_b[0:2, 0:1]

    def weights_a(after):
        send, recv, lands, _ = _gather_forward(send_a, recv_a, lands_a, after, "gather_forward_a")
        w_in, w_out = _gather_wait(send, recv, lands, after, "gather_wait_a")
        return w_in, w_out.reshape(D_MODEL, D_MODEL)

    forwarded_b = []

    def weights_b(after):
        send, recv, lands, _ = forwarded_b
        w_in, w_out = _gather_wait(send, recv, lands, after, "gather_wait_b")
        return w_in, w_out.reshape(D_MODEL, D_MODEL)

    def forward_weights_b(after):
        forwarded_b.extend(_gather_forward(send_b, recv_b, lands_b, after, "gather_forward_b"))
        return forwarded_b[3]

    stage1, stage2 = {}, {}

    def send_grads(tag, dw_in, dw_out):
        grads = [dw_in, dw_out.reshape(N_CHIPS, D_MODEL // N_CHIPS, D_MODEL)]
        send, recv, arrays, token = _reduce_sibling_start(grads, dw_out, f"reduce_d2d_start_{tag}")
        stage1[tag] = (send, recv, arrays)
        return token

    def forward_grads(tag, after):
        send, recv, arrays = stage1[tag]
        grads, got = _reduce_sibling_wait(send, recv, arrays, after, f"reduce_d2d_wait_{tag}")
        partials = [_add_sibling_half(grads[i], got[i], dev_idx, f"reduce_add_{tag}_{i}") for i in range(2)]
        send, recv, arrays, token = _reduce_chips_start(partials, partials[1], f"reduce_ici_start_{tag}")
        stage2[tag] = (send, recv, arrays)
        return token

    stage3 = {}

    def sum_grads(tag, after):
        send, recv, arrays = stage2[tag]
        partials, lands = _reduce_chips_wait(send, recv, arrays, after, f"reduce_ici_wait_{tag}")
        totals = [_sum_partials(lands[i], partials[i], dev_idx, f"reduce_sum_{tag}_{i}") for i in range(2)]
        send, recv, totals, token = _share_halves_start(totals, totals[1], f"reduce_share_start_{tag}")
        stage3[tag] = (send, recv, totals)
        return token

    def finish_grads(tag, after):
        send, recv, totals = stage3[tag]
        return _share_halves_wait(send, recv, totals, after, f"reduce_share_wait_{tag}")

    grad_x, small = _local_step(
        x[0], loss_target[0], mods.reshape(2, 3, D_MODEL), norm_g, conv_w_full, a_conv_b, a_ln_g[0:1],
        a_ln_b[0:1], b_q_norm[0], b_k_norm[0], chip.astype(jnp.int32), own_wb_in,
        weights_a, weights_b, forward_weights_b,
        functools.partial(send_grads, "b"), functools.partial(forward_grads, "b"), functools.partial(send_grads, "a"))

    ns = 3 * D_MODEL // N_CHIPS
    pad_mod = lambda dm: jnp.pad(dm.reshape(N_CHIPS, ns), ((0, 0), (0, D_MODEL - ns)))
    packed = jnp.concatenate([
        small["dnorm_g"], small["dconv_b"], small["dln_g"], small["dln_b"], small["dq_norm"], small["dk_norm"],
        small["loss_cols"], pad_mod(small["dmod0"]), pad_mod(small["dmod1"]), small["dconv_w"],
        jnp.zeros((SMALL_ROWS - 20 - CONV_WIDTH, D_MODEL), F32)], axis=0)
    send_s, recv_s, small_arrays, token_s = _small_gather_start(packed, packed)

    given = dict(norm_g=(norm_g, m_norm_g, v_norm_g), ada_w=(ada_w, m_ada_w, v_ada_w), ada_b=(ada_b, m_ada_b, v_ada_b),
                 a_w_in=(a_w_in, m_a_w_in, v_a_w_in), a_conv_w=(a_conv_w, m_a_conv_w, v_a_conv_w),
                 a_conv_b=(a_conv_b, m_a_conv_b, v_a_conv_b), a_ln_g=(a_ln_g, m_a_ln_g, v_a_ln_g),
                 a_ln_b=(a_ln_b, m_a_ln_b, v_a_ln_b), a_w_out=(a_w_out, m_a_w_out, v_a_w_out),
                 b_w_in=(b_w_in, m_b_w_in, v_b_w_in), b_q_norm=(b_q_norm, m_b_q_norm, v_b_q_norm),
                 b_k_norm=(b_k_norm, m_b_k_norm, v_b_k_norm), b_w_out=(b_w_out, m_b_w_out, v_b_w_out))
    order = ["norm_g", "ada_w", "ada_b", "a_w_in", "a_conv_w", "a_conv_b", "a_ln_g", "a_ln_b", "a_w_out", "b_w_in",
             "b_q_norm", "b_k_norm", "b_w_out"]
    outs = {}

    def update(k, g2, after=None, copy_grad=False):
        w, m, v = given[k]
        shape2 = g2.shape
        res = _adamw(w.reshape(shape2), g2, m.reshape(shape2), v.reshape(shape2), f"adamw_{k}", after, copy_grad)
        outs[k] = tuple(a.reshape(w.shape) for a in ((res[3] if copy_grad else g2), res[0], res[1], res[2]))

    token = forward_grads("a", token_s)
    token = sum_grads("b", token)
    packed, land = _small_gather_wait(send_s, recv_s, small_arrays, token)
    tot, g_ada_w, loss, qk = _reduce_small(packed, land, silu_c)
    g_b_in, g_b_out = finish_grads("b", tot)
    update("b_w_in", g_b_in, copy_grad=True)
    update("b_w_out", g_b_out, copy_grad=True)
    token = sum_grads("a", outs["b_w_in"][1])
    cw = D_MODEL // N_CHIPS
    g_small = dict(
        norm_g=tot[0:2], a_conv_b=tot[2:3], a_ln_g=tot[3:4], a_ln_b=tot[4:5],
        b_q_norm=qk[0:3], b_k_norm=qk[3:6],
        ada_b=jnp.stack([tot[12:16, :ns].reshape(3 * D_MODEL), tot[16:20, :ns].reshape(3 * D_MODEL)]),
        a_conv_w=lax.dynamic_slice(tot[20:20 + CONV_WIDTH], (0, chip * cw), (CONV_WIDTH, cw)),
    )
    update("ada_w", g_ada_w.reshape(2 * D_MODEL, ns), after=token)
    for k, g2 in g_small.items():
        update(k, g2, after=token)
    g_a_in, g_a_out = finish_grads("a", outs["ada_w"][1])
    update("a_w_in", g_a_in, copy_grad=True)
    update("a_w_out", g_a_out, copy_grad=True)
    return (loss.reshape(()), grad_x[None], *[outs[k][0] for k in order], *[outs[k][1] for k in order],
            *[outs[k][2] for k in order], *[outs[k][3] for k in order])
```

```python
import functools

import jax
import jax.numpy as jnp
from jax import lax
from jax.experimental import pallas as pl
from jax.experimental.pallas import tpu as pltpu

F32 = jnp.float32
BF16 = jnp.bfloat16

SEQ = 2048
D_MODEL = 1024
CONV_WIDTH = 31
HEAD_DIM = 64
N_HEADS = 16
DILATIONS = (1, 4, 16)
ATTN_BLOCK = 128
NORM_EPS = 1e-6
NEG_INF = -1e30
N_DEV = 8
N_CHIPS = 4

ADAM_LR = 0.001
ADAM_B1 = 0.9
ADAM_B2 = 0.999
ADAM_EPS = 1e-08
ADAM_WD = 0.01
ADAM_STEP = 10

VMEM_LIMIT_BYTES = 52 * 1024 * 1024
HALO = 32
LANES = 128
MESH = pl.DeviceIdType.MESH


def _params(*sem):
    return pltpu.CompilerParams(dimension_semantics=sem or None, vmem_limit_bytes=VMEM_LIMIT_BYTES)


def _sigmoid(v):
    return 1.0 / (1.0 + jnp.exp(-v))


def _row_spec(tm, cols, col_block=0):
    return pl.BlockSpec((tm, cols), lambda i: (i, col_block))


def _vec_spec(rows, cols):
    return pl.BlockSpec((rows, cols), lambda i: (0, 0))


def _normmod(xv, g, scale, shift):
    r = lax.rsqrt(jnp.mean(xv * xv, axis=-1, keepdims=True) + NORM_EPS)
    return xv * r * g * (1.0 + scale) + shift


def _normmod_fwd(x, g, scale, shift, name):
    tm = 256

    def body(x_ref, g_ref, sc_ref, sh_ref, h_ref, ht_ref):
        h = _normmod(x_ref[...], g_ref[...], sc_ref[...], sh_ref[...])
        h_ref[...] = h.astype(BF16)
        ht_ref[...] = h.T.astype(BF16)

    return pl.pallas_call(
        body, name=name, grid=(SEQ // tm,),
        in_specs=[_row_spec(tm, D_MODEL)] + [_vec_spec(1, D_MODEL)] * 3,
        out_specs=[_row_spec(tm, D_MODEL), pl.BlockSpec((D_MODEL, tm), lambda i: (0, i))],
        out_shape=[jax.ShapeDtypeStruct((SEQ, D_MODEL), BF16), jax.ShapeDtypeStruct((D_MODEL, SEQ), BF16)],
        compiler_params=_params("parallel"),
    )(x, g, scale, shift)


def _normmod_bwd(x, g, scale, dh_parts, dres, name, part_dilations=None, gated=None):
    tm = 256
    n_parts = len(dh_parts)
    dils = part_dilations or (1,) * n_parts
    dh_parts = [p if d == 1 else p.reshape(d, SEQ // d, D_MODEL) for p, d in zip(dh_parts, dils)]
    n_gated = 0 if gated is None else 2

    def body(x_ref, g_ref, sc_ref, dres_ref, *rest):
        part_refs = rest[:n_parts]
        gated_refs = rest[n_parts:n_parts + n_gated]
        out_refs = rest[n_parts + n_gated:]
        dx_ref, sums_ref, nat = out_refs[0], out_refs[1], out_refs[-1]
        xv = x_ref[...]
        r = lax.rsqrt(jnp.mean(xv * xv, axis=-1, keepdims=True) + NORM_EPS)
        xn = xv * r
        dh = _load_natural(part_refs[0], nat, dils[0])
        for p, d in zip(part_refs[1:], dils[1:]):
            dh = dh + _load_natural(p, nat, d)
        gv = g_ref[...]
        one_sc = 1.0 + sc_ref[...]
        dxn = dh * (gv * one_sc)
        dx = dres_ref[...] + r * (dxn - xn * jnp.mean(dxn * xn, axis=-1, keepdims=True))
        dx_ref[...] = dx
        dhx = dh * xn
        rows = [jnp.sum(dhx, axis=0, keepdims=True) * one_sc,
                jnp.sum(dhx, axis=0, keepdims=True) * gv,
                jnp.sum(dh, axis=0, keepdims=True)]
        if gated is not None:
            gate_ref, y_ref = gated_refs
            out_refs[2][...] = (dx * gate_ref[...]).astype(BF16)
            rows.append(jnp.sum(dx * y_ref[...], axis=0, keepdims=True))
        sums = jnp.concatenate(rows + [jnp.zeros((8 - len(rows), D_MODEL), F32)], axis=0)

        @pl.when(pl.program_id(0) == 0)
        def _():
            sums_ref[...] = jnp.zeros_like(sums_ref)

        sums_ref[...] += sums

    gated_specs = [] if gated is None else [_vec_spec(1, D_MODEL), _row_spec(tm, D_MODEL)]
    dy_spec = [] if gated is None else [_row_spec(tm, D_MODEL)]
    dy_shape = [] if gated is None else [jax.ShapeDtypeStruct((SEQ, D_MODEL), BF16)]
    return pl.pallas_call(
        body, name=name, grid=(SEQ // tm,),
        in_specs=[_row_spec(tm, D_MODEL), _vec_spec(1, D_MODEL), _vec_spec(1, D_MODEL), _row_spec(tm, D_MODEL)]
        + [_class_spec(tm, d) for d in dils] + gated_specs,
        out_specs=[_row_spec(tm, D_MODEL), _vec_spec(8, D_MODEL)] + dy_spec,
        out_shape=[jax.ShapeDtypeStruct((SEQ, D_MODEL), F32), jax.ShapeDtypeStruct((8, D_MODEL), F32)] + dy_shape,
        scratch_shapes=[_natural_scratch(tm)],
        compiler_params=_params("arbitrary"),
    )(x, g, scale, dres, *dh_parts, *(gated or ()))


def _mm(lhs, rhs, *, tn, tile0, n_tiles, out_dtype, name, out3d=None, prev=None, transpose_lhs=False):
    mo, kc = lhs.shape[::-1] if transpose_lhs else lhs.shape
    cm = min(mo, 1024)
    tc = 256

    def body(l_ref, r_ref, *rest):
        if transpose_lhs:
            o_ref, lt_ref = rest[-2], rest[-1]

            @pl.when(pl.program_id(0) == 0)
            def _():
                for c in range(kc // tc):
                    lt_ref[:, c * tc:(c + 1) * tc] = l_ref[c * tc:(c + 1) * tc, :].astype(F32).T.astype(l_ref.dtype)
        else:
            o_ref, lt_ref = rest[-1], l_ref
        for m in range(mo // cm):
            rows = pl.ds(m * cm, cm)
            o_ref[rows, :] = jnp.dot(lt_ref[rows, :], r_ref[...], preferred_element_type=F32).astype(out_dtype)

    if rhs.ndim == 3:
        tps_r = rhs.shape[2] // tn
        r_spec = pl.BlockSpec((None, kc, tn), lambda t: ((tile0 + t) // tps_r, 0, (tile0 + t) % tps_r))
    else:
        r_spec = pl.BlockSpec((kc, tn), lambda t: (0, t))
    in_specs = [pl.BlockSpec(lhs.shape, lambda t: (0, 0)), r_spec]
    args = [lhs, rhs]
    aliases = {}
    if out3d is None:
        o_spec = pl.BlockSpec((mo, tn), lambda t: (0, t))
        o_shape = jax.ShapeDtypeStruct((mo, n_tiles * tn), out_dtype)
    else:
        j_out, ns_out = out3d
        tps_o = ns_out // tn
        o_spec = pl.BlockSpec((None, mo, tn), lambda t: ((tile0 + t) // tps_o, 0, (tile0 + t) % tps_o))
        o_shape = jax.ShapeDtypeStruct((j_out, mo, ns_out), out_dtype)
        if prev is not None:
            in_specs.append(pl.BlockSpec(memory_space=pl.ANY))
            args.append(prev)
            aliases = {2: 0}
    return pl.pallas_call(
        body, name=name, grid=(n_tiles,), in_specs=in_specs, out_specs=o_spec, out_shape=o_shape,
        input_output_aliases=aliases,
        scratch_shapes=[pltpu.VMEM((mo, kc), lhs.dtype)] if transpose_lhs else [],
        compiler_params=_params("arbitrary" if transpose_lhs else "parallel"),
    )(*args)


def _b_in_tiles(h_parts, w3, tile_ids, n_tiles, name, prev=None):
    _, kc, ns = w3.shape
    tps = ns // B_TN
    cm = 1024

    def body(ids_ref, h0_ref, h1_ref, h2_ref, w_ref, *rest):
        o_ref = rest[-1]
        out_tile = ids_ref[1, pl.program_id(0)]
        group = jnp.where(out_tile >= B_Z_TILE0, 0, out_tile // B_GROUP_TILES)
        for g, h_ref in enumerate((h0_ref, h1_ref, h2_ref)):
            @pl.when(group == g)
            def _():
                for m in range(SEQ // cm):
                    rows = pl.ds(m * cm, cm)
                    o_ref[rows, :] = jnp.dot(h_ref[rows, :], w_ref[...], preferred_element_type=F32).astype(BF16)

    resident = pl.BlockSpec((SEQ, kc), lambda t, ids: (0, 0))
    in_specs = [resident] * 3 + [pl.BlockSpec((None, kc, B_TN), lambda t, ids: (ids[0, t] // tps, 0, ids[0, t] % tps))]
    args = [*h_parts, w3]
    aliases = {}
    if prev is not None:
        in_specs.append(pl.BlockSpec(memory_space=pl.ANY))
        args.append(prev)
        aliases = {5: 0}
    return pl.pallas_call(
        body, name=name,
        grid_spec=pltpu.PrefetchScalarGridSpec(
            num_scalar_prefetch=1, grid=(n_tiles,), in_specs=in_specs,
            out_specs=pl.BlockSpec((SEQ, B_TN), lambda t, ids: (0, ids[1, t]))),
        out_shape=jax.ShapeDtypeStruct((SEQ, B_TILES * B_TN), BF16),
        input_output_aliases=aliases, compiler_params=_params("arbitrary"),
    )(tile_ids, *args)


def _mm_nt(dy, w3, *, tn, tile0, n_tiles, name, after=None):
    m_rows = dy.shape[0]
    _, kc, ns = w3.shape
    tps = ns // tn
    cm = 512
    extra = [] if after is None else [after]

    def body(dy_ref, w_ref, *rest):
        o_ref = rest[-1]

        @pl.when(pl.program_id(0) == 0)
        def _():
            o_ref[...] = jnp.zeros_like(o_ref)

        for m in range(m_rows // cm):
            rows = pl.ds(m * cm, cm)
            o_ref[rows, :] += lax.dot_general(dy_ref[rows, :], w_ref[...], (((1,), (1,)), ((), ())),
                                              preferred_element_type=F32)

    return pl.pallas_call(
        body, name=name, grid=(n_tiles,),
        in_specs=[pl.BlockSpec((m_rows, tn), lambda t: (0, t)),
                  pl.BlockSpec((None, kc, tn), lambda t: ((tile0 + t) // tps, 0, (tile0 + t) % tps))]
        + [pl.BlockSpec(memory_space=pl.ANY)] * len(extra),
        out_specs=pl.BlockSpec((m_rows, kc), lambda t: (0, 0)),
        out_shape=jax.ShapeDtypeStruct((m_rows, kc), F32),
        compiler_params=_params("arbitrary"),
    )(dy, w3, *extra)


CONV_CHUNK = 16


def _shift_copies(buf, shifted):
    rows = shifted.shape[1]
    for s in range(1, 8):
        shifted[s - 1] = buf[pl.ds(s, rows), :]


def _shifted_rows(buf, shifted, offset, r0):
    s = offset % 8
    if s == 0:
        return buf[pl.ds(r0 + offset, CONV_CHUNK), :]
    return shifted[s - 1, pl.ds(r0 + (offset - s), CONV_CHUNK), :]


def _spread_taps(w_ref, taps):
    for k in range(CONV_WIDTH):
        taps[k] = jnp.broadcast_to(w_ref[k:k + 1, :], (8, D_MODEL))


def _times_tap(taps, k, rows):
    return (rows.reshape(CONV_CHUNK // 8, 8, D_MODEL) * taps[k][None]).reshape(CONV_CHUNK, D_MODEL)


def _conv_fwd(proj, conv_w, conv_b, ln_g, ln_b, name):
    tm = 256
    hb = tm // HALO

    def body(vg_ref, halo_ref, z_ref, w_ref, b_ref, g_ref, be_ref, u5_ref, u5t_ref, u2_ref, buf, shifted, taps):
        i = pl.program_id(0)
        u1 = vg_ref[:, :D_MODEL] * _sigmoid(vg_ref[:, D_MODEL:])
        u1h = halo_ref[:, :D_MODEL] * _sigmoid(halo_ref[:, D_MODEL:])
        buf[pl.ds(0, HALO), :] = jnp.where(i > 0, u1h, 0.0)
        buf[pl.ds(HALO, tm), :] = u1
        _shift_copies(buf, shifted)
        _spread_taps(w_ref, taps)

        def chunk(ci, carry):
            r0 = pl.multiple_of(ci * CONV_CHUNK, CONV_CHUNK)
            acc = jnp.broadcast_to(b_ref[...], (CONV_CHUNK, D_MODEL))
            for k in range(CONV_WIDTH):
                acc = acc + _times_tap(taps, k, _shifted_rows(buf, shifted, HALO - (CONV_WIDTH - 1) + k, r0))
            u2_ref[pl.ds(r0, CONV_CHUNK), :] = acc
            return carry

        lax.fori_loop(0, tm // CONV_CHUNK, chunk, 0)
        acc = u2_ref[...]
        mu = jnp.mean(acc, axis=-1, keepdims=True)
        xc = acc - mu
        rstd = lax.rsqrt(jnp.mean(xc * xc, axis=-1, keepdims=True) + NORM_EPS)
        u3 = xc * rstd * g_ref[...] + be_ref[...]
        zv = z_ref[...]
        u5 = u3 * _sigmoid(u3) * (zv * _sigmoid(zv))
        u5_ref[...] = u5.astype(BF16)
        u5t_ref[...] = u5.T.astype(BF16)

    return pl.pallas_call(
        body, name=name, grid=(SEQ // tm,),
        in_specs=[pl.BlockSpec((tm, 2 * D_MODEL), lambda i: (i, 0)),
                  pl.BlockSpec((HALO, 2 * D_MODEL), lambda i: (jnp.maximum(i * hb - 1, 0), 0)),
                  _row_spec(tm, D_MODEL, 2),
                  _vec_spec(CONV_WIDTH, D_MODEL)] + [_vec_spec(1, D_MODEL)] * 3,
        out_specs=[_row_spec(tm, D_MODEL), pl.BlockSpec((D_MODEL, tm), lambda i: (0, i)), _row_spec(tm, D_MODEL)],
        out_shape=[jax.ShapeDtypeStruct((SEQ, D_MODEL), BF16), jax.ShapeDtypeStruct((D_MODEL, SEQ), BF16),
                   jax.ShapeDtypeStruct((SEQ, D_MODEL), F32)],
        scratch_shapes=[pltpu.VMEM((HALO + tm, D_MODEL), F32), pltpu.VMEM((7, HALO + tm - 8, D_MODEL), F32),
                        pltpu.VMEM((CONV_WIDTH, 8, D_MODEL), F32)],
        compiler_params=_params("parallel"),
    )(proj, proj, proj, conv_w, conv_b, ln_g, ln_b)


def _conv_bwd_pointwise(dy, w_out, proj, u2, ln_g, ln_b, name):
    tm = 256

    def body(dy_ref, w_ref, z_ref, u2_ref, g_ref, be_ref, du2_ref, dz_ref, sums_ref):
        u2v = u2_ref[...]
        mu = jnp.mean(u2v, axis=-1, keepdims=True)
        xc = u2v - mu
        rstd = lax.rsqrt(jnp.mean(xc * xc, axis=-1, keepdims=True) + NORM_EPS)
        xhat = xc * rstd
        u3 = xhat * g_ref[...] + be_ref[...]
        s3 = _sigmoid(u3)
        u4 = u3 * s3
        zv = z_ref[...]
        sz = _sigmoid(zv)
        du5v = lax.dot_general(dy_ref[...], w_ref[...], NT_DIMS, preferred_element_type=F32)
        dz_ref[...] = du5v * u4 * (sz * (1.0 + zv * (1.0 - sz)))
        du3 = du5v * (zv * sz) * (s3 * (1.0 + u3 * (1.0 - s3)))
        dxhat = du3 * g_ref[...]
        du2 = rstd * (dxhat - jnp.mean(dxhat, axis=-1, keepdims=True)
                      - xhat * jnp.mean(dxhat * xhat, axis=-1, keepdims=True))
        du2_ref[...] = du2
        sums = jnp.concatenate([
            jnp.sum(du3 * xhat, axis=0, keepdims=True),
            jnp.sum(du3, axis=0, keepdims=True),
            jnp.sum(du2, axis=0, keepdims=True),
            jnp.zeros((5, D_MODEL), F32)], axis=0)

        @pl.when(pl.program_id(0) == 0)
        def _():
            sums_ref[...] = jnp.zeros_like(sums_ref)

        sums_ref[...] += sums

    return pl.pallas_call(
        body, name=name, grid=(SEQ // tm,),
        in_specs=[_row_spec(tm, D_MODEL), _vec_spec(D_MODEL, D_MODEL), _row_spec(tm, D_MODEL, 2),
                  _row_spec(tm, D_MODEL), _vec_spec(1, D_MODEL), _vec_spec(1, D_MODEL)],
        out_specs=[_row_spec(tm, D_MODEL), _row_spec(tm, D_MODEL), _vec_spec(8, D_MODEL)],
        out_shape=[jax.ShapeDtypeStruct((SEQ, D_MODEL), F32), jax.ShapeDtypeStruct((SEQ, D_MODEL), F32),
                   jax.ShapeDtypeStruct((8, D_MODEL), F32)],
        compiler_params=_params("arbitrary"),
    )(dy, w_out, proj, u2, ln_g, ln_b)


def _conv_bwd_taps(du2, dz, proj, conv_w, name):
    tm = 256
    hb = tm // HALO
    n_blocks = SEQ // tm

    def body(du2_ref, dnext_ref, dz_ref, vg_ref, w_ref, dproj_ref, dw_ref, dbuf, dshift, sgbuf, ubuf, dwacc, taps):
        i = pl.program_id(0)
        _spread_taps(w_ref, taps)
        sg = _sigmoid(vg_ref[:, D_MODEL:])
        sgbuf[...] = sg
        ubuf[...] = vg_ref[:, :D_MODEL] * sg
        dbuf[pl.ds(0, tm), :] = du2_ref[...]
        dbuf[pl.ds(tm, HALO), :] = jnp.where(i < n_blocks - 1, dnext_ref[...], 0.0)
        _shift_copies(dbuf, dshift)

        @pl.when(i == 0)
        def _():
            dwacc[...] = jnp.zeros_like(dwacc)

        def chunk(ci, carry):
            r0 = pl.multiple_of(ci * CONV_CHUNK, CONV_CHUNK)
            rows = pl.ds(r0, CONV_CHUNK)
            u1c = ubuf[rows, :]
            du1 = jnp.zeros((CONV_CHUNK, D_MODEL), F32)
            for k in range(CONV_WIDTH):
                ahead = _shifted_rows(dbuf, dshift, CONV_WIDTH - 1 - k, r0)
                du1 = du1 + _times_tap(taps, k, ahead)
                prod = u1c * ahead
                dwacc[k] += prod[0:8] + prod[8:16]
            sgc = sgbuf[rows, :]
            dval = du1 * sgc
            dproj_ref[rows, 0:D_MODEL] = dval.astype(BF16)
            dproj_ref[rows, D_MODEL:2 * D_MODEL] = (dval * vg_ref[rows, 0:D_MODEL] * (1.0 - sgc)).astype(BF16)
            return carry

        lax.fori_loop(0, tm // CONV_CHUNK, chunk, 0)
        dproj_ref[:, 2 * D_MODEL:] = dz_ref[...].astype(BF16)

        @pl.when(i == n_blocks - 1)
        def _():
            for k in range(CONV_WIDTH):
                dw_ref[k:k + 1, :] = jnp.sum(dwacc[k], axis=0, keepdims=True)
            dw_ref[CONV_WIDTH:, :] = jnp.zeros((32 - CONV_WIDTH, D_MODEL), F32)

    return pl.pallas_call(
        body, name=name, grid=(n_blocks,),
        in_specs=[_row_spec(tm, D_MODEL),
                  pl.BlockSpec((HALO, D_MODEL), lambda i: (jnp.minimum((i + 1) * hb, SEQ // HALO - 1), 0)),
                  _row_spec(tm, D_MODEL),
                  pl.BlockSpec((tm, 2 * D_MODEL), lambda i: (i, 0)),
                  _vec_spec(CONV_WIDTH, D_MODEL)],
        out_specs=[_row_spec(tm, 3 * D_MODEL), _vec_spec(32, D_MODEL)],
        out_shape=[jax.ShapeDtypeStruct((SEQ, 3 * D_MODEL), BF16), jax.ShapeDtypeStruct((32, D_MODEL), F32)],
        scratch_shapes=[pltpu.VMEM((tm + HALO, D_MODEL), F32), pltpu.VMEM((7, HALO + tm - 8, D_MODEL), F32),
                        pltpu.VMEM((tm, D_MODEL), F32), pltpu.VMEM((tm, D_MODEL), F32),
                        pltpu.VMEM((CONV_WIDTH, 8, D_MODEL), F32), pltpu.VMEM((CONV_WIDTH, 8, D_MODEL), F32)],
        compiler_params=_params("arbitrary"),
    )(du2, du2, dz, proj, conv_w)


def _out_a(u5, w_out, x, gate, g1, scale1, shift1, name):
    tm = 256
    n_d = len(DILATIONS)

    def body(u_ref, w_ref, x_ref, gate_ref, g_ref, sc_ref, sh_ref, x1_ref, y_ref, ht_ref, *rest):
        h_refs, nat = rest[:n_d], rest[-1]
        y = jnp.dot(u_ref[...], w_ref[...], preferred_element_type=F32)
        x1 = x_ref[...] + gate_ref[...] * y
        y_ref[...] = y
        x1_ref[...] = x1
        h = _normmod(x1, g_ref[...], sc_ref[...], sh_ref[...])
        ht_ref[...] = h.T.astype(BF16)
        for h_ref, d in zip(h_refs, DILATIONS):
            _store_classes(h_ref, h, nat, d)

    res = pl.pallas_call(
        body, name=name, grid=(SEQ // tm,),
        in_specs=[_row_spec(tm, D_MODEL), _vec_spec(D_MODEL, D_MODEL), _row_spec(tm, D_MODEL)]
        + [_vec_spec(1, D_MODEL)] * 4,
        out_specs=[_row_spec(tm, D_MODEL), _row_spec(tm, D_MODEL), pl.BlockSpec((D_MODEL, tm), lambda i: (0, i))]
        + [_class_spec(tm, d) for d in DILATIONS],
        out_shape=[jax.ShapeDtypeStruct((SEQ, D_MODEL), F32), jax.ShapeDtypeStruct((SEQ, D_MODEL), F32),
                   jax.ShapeDtypeStruct((D_MODEL, SEQ), BF16)] + [_class_shape(d, BF16) for d in DILATIONS],
        scratch_shapes=[_natural_scratch(tm)],
        compiler_params=_params("parallel"),
    )(u5, w_out, x, gate, g1, scale1, shift1)
    return res[0], res[1], res[2], [a.reshape(SEQ, D_MODEL) for a in res[3:]]


def _out_b_loss(u, w_out, x1, gate, target, name):
    tm = 256

    def body(u_ref, w_ref, x_ref, gate_ref, t_ref, e_ref, dy_ref, sums_ref):
        y = jnp.dot(u_ref[...], w_ref[...], preferred_element_type=F32)
        diff = x_ref[...] + gate_ref[...] * y - t_ref[...]
        e = diff * (1.0 / D_MODEL)
        e_ref[...] = e
        dy_ref[...] = (e * gate_ref[...]).astype(BF16)
        sums = jnp.concatenate([
            jnp.sum(e * y, axis=0, keepdims=True),
            jnp.sum(diff * diff, axis=0, keepdims=True),
            jnp.zeros((6, D_MODEL), F32)], axis=0)

        @pl.when(pl.program_id(0) == 0)
        def _():
            sums_ref[...] = jnp.zeros_like(sums_ref)

        sums_ref[...] += sums

    return pl.pallas_call(
        body, name=name, grid=(SEQ // tm,),
        in_specs=[_row_spec(tm, D_MODEL), _vec_spec(D_MODEL, D_MODEL), _row_spec(tm, D_MODEL),
                  _vec_spec(1, D_MODEL), _row_spec(tm, D_MODEL)],
        out_specs=[_row_spec(tm, D_MODEL), _row_spec(tm, D_MODEL), _vec_spec(8, D_MODEL)],
        out_shape=[jax.ShapeDtypeStruct((SEQ, D_MODEL), F32), jax.ShapeDtypeStruct((SEQ, D_MODEL), BF16),
                   jax.ShapeDtypeStruct((8, D_MODEL), F32)],
        compiler_params=_params("arbitrary"),
    )(u, w_out, x1, gate, target)


def _seg_matrix():
    r = lax.broadcasted_iota(jnp.int32, (256, 256), 0) // HEAD_DIM
    c = lax.broadcasted_iota(jnp.int32, (256, 256), 1) // HEAD_DIM
    return (r == c).astype(BF16)


def _segsum(v, seg):
    hi = v.astype(BF16)
    lo = (v - hi.astype(F32)).astype(BF16)
    outs = []
    for c0 in range(0, D_MODEL, 256):
        outs.append(jnp.dot(hi[:, c0:c0 + 256], seg, preferred_element_type=F32)
                    + jnp.dot(lo[:, c0:c0 + 256], seg, preferred_element_type=F32))
    return jnp.concatenate(outs, axis=1)


def _qk_rstd(v, seg):
    return lax.rsqrt(_segsum(v * v, seg) * (1.0 / HEAD_DIM) + NORM_EPS)


def _qknorm_fwd(proj, group, qw, kw, seg, name):
    tm = 256

    def body(q_in, k_in, qw_ref, kw_ref, seg_ref, q_ref, k_ref):
        segv = seg_ref[...]
        q = q_in[...].astype(F32)
        k = k_in[...].astype(F32)
        q_ref[...] = (q * _qk_rstd(q, segv) * qw_ref[...] * HEAD_DIM ** -0.5).astype(BF16)
        k_ref[...] = (k * _qk_rstd(k, segv) * kw_ref[...]).astype(BF16)

    return pl.pallas_call(
        body, name=name, grid=(SEQ // tm,),
        in_specs=[_row_spec(tm, D_MODEL, 3 * group), _row_spec(tm, D_MODEL, 3 * group + 1),
                  _vec_spec(1, D_MODEL), _vec_spec(1, D_MODEL), _vec_spec(256, 256)],
        out_specs=[_row_spec(tm, D_MODEL)] * 2,
        out_shape=[jax.ShapeDtypeStruct((SEQ, D_MODEL), BF16)] * 2,
        compiler_params=_params("parallel"),
    )(proj, proj, qw, kw, seg)


def _attn_masks(b, bpc, dilation, slope):
    if bpc == 1:
        qi = lax.broadcasted_iota(jnp.int32, (ATTN_BLOCK, ATTN_BLOCK), 0)
        kj = lax.broadcasted_iota(jnp.int32, (ATTN_BLOCK, ATTN_BLOCK), 1)
        steps = qi - kj
        return (steps * dilation).astype(F32), steps >= 0
    qi = lax.broadcasted_iota(jnp.int32, (ATTN_BLOCK, 2 * ATTN_BLOCK), 0)
    kj = lax.broadcasted_iota(jnp.int32, (ATTN_BLOCK, 2 * ATTN_BLOCK), 1)
    steps = qi + ATTN_BLOCK - kj
    has_prev = (b % bpc) != 0
    valid = (steps >= 0) & (steps <= ATTN_BLOCK) & (has_prev | (kj >= ATTN_BLOCK))
    return (steps * dilation).astype(F32), valid


MASKED = 1e30


def _bias_scratch(bpc):
    return pltpu.VMEM((1 if bpc == 1 else 2, N_HEADS, ATTN_BLOCK, (1 if bpc == 1 else 2) * ATTN_BLOCK), F32)


def _fill_bias(bias_ref, sl_ref, bpc, dilation):
    for variant in range(bias_ref.shape[0]):
        dist, valid = _attn_masks(variant, min(bpc, 2), dilation, None)
        bias_ref[variant] = jnp.where(valid[None], dist[None] * sl_ref[...], MASKED)


def _step_bias(bias_ref, b, bpc):
    if bpc == 1:
        return bias_ref[0]
    return bias_ref[jnp.where((b % bpc) != 0, 1, 0)]


def _key_tile(prev_ref, cur_ref, cols, bpc):
    if bpc == 1:
        return cur_ref[:, cols]
    return jnp.concatenate([prev_ref[:, cols], cur_ref[:, cols]], axis=0)


ATTN_HEADS_FWD = 16
ATTN_HEADS_BWD = 16
NT_DIMS = (((1,), (1,)), ((), ()))
BATCH_NT_DIMS = (((2,), (2,)), ((0,), (0,)))
BATCH_NN_DIMS = (((2,), (1,)), ((0,), (0,)))
BATCH_TN_DIMS = (((1,), (1,)), ((0,), (0,)))


def _head_stack(tile_of, heads):
    return jnp.stack([tile_of(slice(h * HEAD_DIM, (h + 1) * HEAD_DIM)) for h in range(heads)], axis=0)


def _attn_specs(heads, segment=0):
    width = heads * HEAD_DIM
    off = segment * (D_MODEL // width)
    last = SEQ // ATTN_BLOCK - 1
    cur = pl.BlockSpec((ATTN_BLOCK, width), lambda hg, b: (jnp.minimum(b, last), hg + off))
    prev = pl.BlockSpec((ATTN_BLOCK, width), lambda hg, b: (jnp.clip(b - 1, 0, last), hg + off))
    return cur, prev


def _attn_fwd(q, k, proj, group, slopes, dilation, name):
    bpc = SEQ // dilation // ATTN_BLOCK
    heads = ATTN_HEADS_FWD
    assert heads == N_HEADS
    cur, prev = _attn_specs(heads)
    v_cur, v_prev = _attn_specs(heads, segment=3 * group + 2)

    def body(sl_ref, q_ref, kp_ref, kc_ref, vp_ref, vc_ref, o_ref, lse_ref, bias_ref):
        b = pl.program_id(1)

        @pl.when(b == 0)
        def _():
            _fill_bias(bias_ref, sl_ref, bpc, dilation)

        q3 = _head_stack(lambda cols: q_ref[:, cols], heads)
        k3 = _head_stack(lambda cols: _key_tile(kp_ref, kc_ref, cols, bpc), heads)
        v3 = _head_stack(lambda cols: _key_tile(vp_ref, vc_ref, cols, bpc), heads)
        s = lax.dot_general(q3, k3, BATCH_NT_DIMS, preferred_element_type=F32)
        s = s - _step_bias(bias_ref, b, bpc)
        m = jnp.max(s, axis=-1, keepdims=True)
        p = jnp.exp(s - m)
        l = jnp.sum(p, axis=-1, keepdims=True)
        o3 = lax.dot_general(p.astype(BF16), v3, BATCH_NN_DIMS, preferred_element_type=F32) / l
        lse3 = m + jnp.log(l)
        for h in range(heads):
            o_ref[:, h * HEAD_DIM:(h + 1) * HEAD_DIM] = o3[h]
        lse_ref[...] = jnp.concatenate([lse3[h] for h in range(heads)]
                                       + [jnp.zeros((ATTN_BLOCK, LANES - heads), F32)], axis=1)

    return pl.pallas_call(
        body, name=name, grid=(N_HEADS // heads, SEQ // ATTN_BLOCK),
        in_specs=[pl.BlockSpec((heads, 1, 1), lambda hg, b: (hg, 0, 0)), cur, prev, cur, v_prev, v_cur],
        out_specs=[cur, pl.BlockSpec((ATTN_BLOCK, LANES), lambda hg, b: (b, 0))],
        out_shape=[jax.ShapeDtypeStruct((SEQ, D_MODEL), F32), jax.ShapeDtypeStruct((SEQ, LANES), F32)],
        scratch_shapes=[_bias_scratch(bpc)],
        compiler_params=_params("parallel", "arbitrary"),
    )(slopes.reshape(N_HEADS, 1, 1), q, k, k, proj, proj)


def _class_spec(tm, dilation, width=D_MODEL):
    if dilation == 1:
        return _row_spec(tm, width)
    return pl.BlockSpec((dilation, tm // dilation, width), lambda i: (0, i, 0))


def _class_shape(dilation, dtype, width=D_MODEL):
    if dilation == 1:
        return jax.ShapeDtypeStruct((SEQ, width), dtype)
    return jax.ShapeDtypeStruct((dilation, SEQ // dilation, width), dtype)


def _load_natural(in_ref, nat_ref, dilation):
    if dilation == 1:
        return in_ref[...].astype(F32)
    n = nat_ref.shape[1] // dilation
    tiles = in_ref.shape[-1] // LANES
    for r in range(dilation):
        for j in range(tiles):
            nat_ref.at[j][pl.ds(r, n, stride=dilation), :] = in_ref[r, :, j * LANES:(j + 1) * LANES].astype(F32)
    if tiles == 1:
        return nat_ref[0]
    return jnp.concatenate([nat_ref[j] for j in range(tiles)], axis=1)


def _store_classes(out_ref, value, nat_ref, dilation):
    if dilation == 1:
        out_ref[...] = value.astype(out_ref.dtype)
        return
    n = nat_ref.shape[1] // dilation
    tiles = value.shape[-1] // LANES
    for j in range(tiles):
        nat_ref[j] = value[:, j * LANES:(j + 1) * LANES]
    for r in range(dilation):
        for j in range(tiles):
            out_ref[r, :, j * LANES:(j + 1) * LANES] = (
                nat_ref.at[j][pl.ds(r, n, stride=dilation), :].astype(out_ref.dtype))


def _natural_scratch(tm):
    return pltpu.VMEM((D_MODEL // LANES, tm, LANES), F32)


def _head_selector():
    lane_head = lax.broadcasted_iota(jnp.int32, (D_MODEL, LANES), 0) // HEAD_DIM
    head = lax.broadcasted_iota(jnp.int32, (D_MODEL, LANES), 1)
    return (lane_head == head).astype(BF16)


def _dot_split(v, m01, dims):
    hi = v.astype(BF16)
    lo = (v - hi.astype(F32)).astype(BF16)
    return (lax.dot_general(hi, m01, dims, preferred_element_type=F32)
            + lax.dot_general(lo, m01, dims, preferred_element_type=F32))


def _merge_fwd(o_parts, lse_parts, z, sel, name):
    tm = 256
    h_spec = pl.BlockSpec((tm, LANES), lambda i: (i, 0))

    def body(o0, o1, o2, l0, l1, l2, z_ref, sel_ref, u_ref, ut_ref, o_ref, lse_ref, nat):
        ls = [_load_natural(l, nat, d) for l, d in zip((l0, l1, l2), DILATIONS)]
        m = jnp.maximum(jnp.maximum(ls[0], ls[1]), ls[2])
        tot = m + jnp.log(jnp.exp(ls[0] - m) + jnp.exp(ls[1] - m) + jnp.exp(ls[2] - m))
        o = jnp.zeros((tm, D_MODEL), F32)
        for o_in, l, d in zip((o0, o1, o2), ls, DILATIONS):
            weight = _dot_split(jnp.exp(l - tot), sel_ref[...], NT_DIMS)
            o = o + weight * _load_natural(o_in, nat, d)
        zv = z_ref[...].astype(F32)
        u = o * (zv * _sigmoid(zv))
        u_ref[...] = u.astype(BF16)
        ut_ref[...] = u.T.astype(BF16)
        o_ref[...] = o
        lse_ref[...] = tot

    return pl.pallas_call(
        body, name=name, grid=(SEQ // tm,),
        in_specs=[_class_spec(tm, d) for d in DILATIONS] + [_class_spec(tm, d, LANES) for d in DILATIONS]
        + [_row_spec(tm, D_MODEL, B_Z_SEGMENT), _vec_spec(D_MODEL, LANES)],
        out_specs=[_row_spec(tm, D_MODEL), pl.BlockSpec((D_MODEL, tm), lambda i: (0, i)),
                   _row_spec(tm, D_MODEL), h_spec],
        out_shape=[jax.ShapeDtypeStruct((SEQ, D_MODEL), BF16), jax.ShapeDtypeStruct((D_MODEL, SEQ), BF16),
                   jax.ShapeDtypeStruct((SEQ, D_MODEL), F32), jax.ShapeDtypeStruct((SEQ, LANES), F32)],
        scratch_shapes=[_natural_scratch(tm)],
        compiler_params=_params("parallel"),
    )(*o_parts, *lse_parts, z, sel)


def _merge_bwd(dy, w_out, o, lse, z, sel, name):
    tm = 256
    n_d = len(DILATIONS)

    def body(dy_ref, w_ref, o_ref, lse_ref, z_ref, sel_ref, dz_ref, *rest):
        do_refs, delta_refs, lse_refs, nat = rest[:n_d], rest[n_d:2 * n_d], rest[2 * n_d:3 * n_d], rest[-1]
        zv = z_ref[...].astype(F32)
        sz = _sigmoid(zv)
        duv = lax.dot_general(dy_ref[...], w_ref[...], NT_DIMS, preferred_element_type=F32)
        ov = o_ref[...]
        do = duv * (zv * sz)
        dz_ref[...] = (duv * ov * (sz * (1.0 + zv * (1.0 - sz)))).astype(BF16)
        delta = _dot_split(do * ov, sel_ref[...], (((1,), (0,)), ((), ())))
        lv = lse_ref[...]
        for i, d in enumerate(DILATIONS):
            _store_classes(do_refs[i], do, nat, d)
            _store_classes(delta_refs[i], delta, nat, d)
            _store_classes(lse_refs[i], lv, nat, d)

    res = pl.pallas_call(
        body, name=name, grid=(SEQ // tm,),
        in_specs=[_row_spec(tm, D_MODEL), _vec_spec(D_MODEL, D_MODEL), _row_spec(tm, D_MODEL), _row_spec(tm, LANES),
                  _row_spec(tm, D_MODEL, B_Z_SEGMENT), _vec_spec(D_MODEL, LANES)],
        out_specs=[_row_spec(tm, D_MODEL)] + [_class_spec(tm, d) for d in DILATIONS]
        + [_class_spec(tm, d, LANES) for d in DILATIONS] * 2,
        out_shape=[jax.ShapeDtypeStruct((SEQ, D_MODEL), BF16)] + [_class_shape(d, BF16) for d in DILATIONS]
        + [_class_shape(d, F32, LANES) for d in DILATIONS] * 2,
        scratch_shapes=[_natural_scratch(tm)],
        compiler_params=_params("parallel"),
    )(dy, w_out, o, lse, z, sel)
    flat = lambda a: a.reshape(SEQ, a.shape[-1])
    return (res[0], [flat(a) for a in res[1:1 + n_d]], [flat(a) for a in res[1 + n_d:1 + 2 * n_d]],
            [flat(a) for a in res[1 + 2 * n_d:]])


def _attn_bwd(q, k, proj, group, do, lse, delta, slopes, dilation, name):
    bpc = SEQ // dilation // ATTN_BLOCK
    heads = ATTN_HEADS_BWD
    n_blocks = SEQ // ATTN_BLOCK
    carry = bpc > 1
    width = heads * HEAD_DIM
    cur, prev = _attn_specs(heads)
    v_cur, v_prev = _attn_specs(heads, segment=3 * group + 2)
    assert heads == N_HEADS
    per_head = pl.BlockSpec((ATTN_BLOCK, LANES), lambda hg, b: (jnp.minimum(b, n_blocks - 1), 0))
    scale = HEAD_DIM ** -0.5

    def body(sl_ref, q_ref, kp_ref, kc_ref, vp_ref, vc_ref, do_ref, lse_ref, dl_ref,
             dq_ref, dk_ref, dv_ref, *scratch):
        b = pl.program_id(1)
        if carry:
            dk_carry, dv_carry = scratch

            @pl.when(b == n_blocks)
            def _():
                dk_ref[...] = dk_carry[...].astype(BF16)
                dv_ref[...] = dv_carry[...].astype(BF16)

            @pl.when(b < n_blocks)
            def _():
                step(sl_ref, q_ref, kp_ref, kc_ref, vp_ref, vc_ref, do_ref, lse_ref, dl_ref,
                     dq_ref, dk_ref, dv_ref, dk_carry, dv_carry, b)
        else:
            step(sl_ref, q_ref, kp_ref, kc_ref, vp_ref, vc_ref, do_ref, lse_ref, dl_ref,
                 dq_ref, dk_ref, dv_ref, None, None, b)

    def step(sl_ref, q_ref, kp_ref, kc_ref, vp_ref, vc_ref, do_ref, lse_ref, dl_ref,
             dq_ref, dk_ref, dv_ref, dk_carry, dv_carry, b):
        if carry:
            @pl.when(b == 0)
            def _():
                dk_carry[...] = jnp.zeros_like(dk_carry)
                dv_carry[...] = jnp.zeros_like(dv_carry)

        q3 = _head_stack(lambda cols: q_ref[:, cols], heads)
        k3 = _head_stack(lambda cols: _key_tile(kp_ref, kc_ref, cols, bpc), heads)
        v3 = _head_stack(lambda cols: _key_tile(vp_ref, vc_ref, cols, bpc), heads)
        do3 = _head_stack(lambda cols: do_ref[:, cols], heads)
        lse3 = jnp.stack([lse_ref[:, h:h + 1] for h in range(heads)], axis=0)
        dl3 = jnp.stack([dl_ref[:, h:h + 1] for h in range(heads)], axis=0)
        s = lax.dot_general(q3, k3, BATCH_NT_DIMS, preferred_element_type=F32)
        dist, valid = _attn_masks(b, bpc, dilation, None)
        p = jnp.exp(jnp.where(valid[None], s - dist[None] * sl_ref[...], NEG_INF) - lse3)
        dp = lax.dot_general(do3, v3, BATCH_NT_DIMS, preferred_element_type=F32)
        ds = (p * (dp - dl3)).astype(BF16)
        dq3 = lax.dot_general(ds, k3, BATCH_NN_DIMS, preferred_element_type=F32) * scale
        dk3 = lax.dot_general(ds, q3, BATCH_TN_DIMS, preferred_element_type=F32)
        dv3 = lax.dot_general(p.astype(BF16), do3, BATCH_TN_DIMS, preferred_element_type=F32)
        for h in range(heads):
            cols = slice(h * HEAD_DIM, (h + 1) * HEAD_DIM)
            dq_ref[:, cols] = dq3[h].astype(BF16)
            if carry:
                dk_ref[:, cols] = (dk_carry[:, cols] + dk3[h, :ATTN_BLOCK]).astype(BF16)
                dv_ref[:, cols] = (dv_carry[:, cols] + dv3[h, :ATTN_BLOCK]).astype(BF16)
                dk_carry[:, cols] = dk3[h, ATTN_BLOCK:]
                dv_carry[:, cols] = dv3[h, ATTN_BLOCK:]
            else:
                dk_ref[:, cols] = dk3[h].astype(BF16)
                dv_ref[:, cols] = dv3[h].astype(BF16)

    kv_out = prev if carry else cur
    return pl.pallas_call(
        body, name=name, grid=(N_HEADS // heads, n_blocks + (1 if carry else 0)),
        in_specs=[pl.BlockSpec((heads, 1, 1), lambda hg, b: (hg, 0, 0)), cur, prev, cur, v_prev, v_cur,
                  cur, per_head, per_head],
        out_specs=[cur, kv_out, kv_out],
        out_shape=[jax.ShapeDtypeStruct((SEQ, D_MODEL), BF16)] * 3,
        scratch_shapes=[pltpu.VMEM((ATTN_BLOCK, width), F32)] * 2 if carry else [],
        compiler_params=_params("parallel", "arbitrary"),
    )(slopes.reshape(N_HEADS, 1, 1), q, k, k, proj, proj, do, lse, delta)


def _qknorm_bwd(proj, group, qw, kw, seg, dq, dk, dv, name):
    tm = 256

    def body(q_in, k_in, qw_ref, kw_ref, seg_ref, dq_ref, dk_ref, dv_ref, dproj_ref, sums_ref):
        segv = seg_ref[...]
        sums = []
        for part, (raw_ref, w_ref, dn_ref) in enumerate(((q_in, qw_ref, dq_ref), (k_in, kw_ref, dk_ref))):
            raw = raw_ref[...].astype(F32)
            dn = dn_ref[...].astype(F32)
            r = _qk_rstd(raw, segv)
            gq = dn * w_ref[...]
            draw = r * gq - raw * (r * r * r) * (_segsum(raw * gq, segv) * (1.0 / HEAD_DIM))
            dproj_ref[:, part * D_MODEL:(part + 1) * D_MODEL] = draw.astype(BF16)
            sums.append(jnp.sum(dn * raw * r, axis=0, keepdims=True))
        dproj_ref[:, 2 * D_MODEL:] = dv_ref[...]

        @pl.when(pl.program_id(0) == 0)
        def _():
            sums_ref[...] = jnp.zeros_like(sums_ref)

        sums_ref[...] += jnp.concatenate(sums + [jnp.zeros((6, D_MODEL), F32)], axis=0)

    return pl.pallas_call(
        body, name=name, grid=(SEQ // tm,),
        in_specs=[_row_spec(tm, D_MODEL, 3 * group), _row_spec(tm, D_MODEL, 3 * group + 1),
                  _vec_spec(1, D_MODEL), _vec_spec(1, D_MODEL), _vec_spec(256, 256)] + [_row_spec(tm, D_MODEL)] * 3,
        out_specs=[_row_spec(tm, 3 * D_MODEL), _vec_spec(8, D_MODEL)],
        out_shape=[jax.ShapeDtypeStruct((SEQ, 3 * D_MODEL), BF16), jax.ShapeDtypeStruct((8, D_MODEL), F32)],
        compiler_params=_params("arbitrary"),
    )(proj, proj, qw, kw, seg, dq, dk, dv)


B_TN = 512
B_GROUP_TILES = 3 * D_MODEL // B_TN
B_Z_TILE0 = 3 * B_GROUP_TILES
B_Z_TILES = D_MODEL // B_TN
B_TILES = B_Z_TILE0 + B_Z_TILES
B_Z_SEGMENT = 3 * len(DILATIONS)


def _local_step(x, target, mods, norm_g, conv_w, conv_b, ln_g, ln_b, q_norm, k_norm, chip, own_wb_in,
                weights_a, weights_b, forward_weights_b, send_grads_b, forward_grads_b, send_grads_a):
    row = lambda a, i: a[i:i + 1]
    shift0, scale0, gate0 = row(mods[0], 0), row(mods[0], 1), row(mods[0], 2)
    shift1, scale1, gate1 = row(mods[1], 0), row(mods[1], 1), row(mods[1], 2)
    g0, g1 = row(norm_g, 0), row(norm_g, 1)
    seg = _seg_matrix()
    slopes = jnp.exp2(-8.0 * jnp.arange(1, N_HEADS + 1, dtype=F32) / N_HEADS)
    qw = [jnp.tile(q_norm[g:g + 1], (1, N_HEADS)) for g in range(3)]
    kw = [jnp.tile(k_norm[g:g + 1], (1, N_HEADS)) for g in range(3)]

    h0, h0t = _normmod_fwd(x, g0, scale0, shift0, "prenorm0")
    wa_in, wa_out = weights_a(h0)
    ja, _, nsa = wa_in.shape
    proj_a = _mm(h0, wa_in, tn=nsa, tile0=0, n_tiles=ja, out_dtype=F32, name="a_in")
    u5, u5t, u2 = _conv_fwd(proj_a, conv_w, conv_b, ln_g, ln_b, "a_conv")
    x1, y_a, h1t, h1c = _out_a(u5, wa_out, x, gate0, g1, scale1, shift1, "a_out")

    own_tiles = B_TILES // N_CHIPS
    step = jnp.arange(B_TILES, dtype=jnp.int32)
    tiles = (own_tiles * chip + step) % B_TILES
    own_ids = jnp.stack([step[:own_tiles], tiles[:own_tiles]])
    rest_ids = jnp.stack([tiles[own_tiles:], tiles[own_tiles:]])
    proj_b = _b_in_tiles(h1c, own_wb_in, own_ids, own_tiles, "b_in_own")
    forward_weights_b(proj_b)
    wb_in, wb_out = weights_b(proj_b)
    jb, _, nsb = wb_in.shape
    proj_b = _b_in_tiles(h1c, wb_in, rest_ids, B_TILES - own_tiles, "b_in_rest", prev=proj_b)
    h1 = h1c[0]
    qkv, o_parts, lse_parts = [], [], []
    for g, d in enumerate(DILATIONS):
        qn, kn = _qknorm_fwd(proj_b, g, qw[g], kw[g], seg, f"b_qknorm_g{g}")
        og, lg = _attn_fwd(qn, kn, proj_b, g, slopes, d, f"b_attn_g{g}")
        qkv.append((qn, kn))
        o_parts.append(og if d == 1 else og.reshape(d, SEQ // d, D_MODEL))
        lse_parts.append(lg if d == 1 else lg.reshape(d, SEQ // d, LANES))
    sel = _head_selector()
    u_b, u_bt, o_b, lse_b = _merge_fwd(o_parts, lse_parts, proj_b, sel, "b_merge")
    e, dy_b, sums_loss = _out_b_loss(u_b, wb_out, x1, gate1, target, "b_out_loss")

    dwb_out = _mm(u_bt, dy_b, tn=D_MODEL, tile0=0, n_tiles=1, out_dtype=BF16, name="b_dwout")
    dz_b, do_c, delta_c, lse_c = _merge_bwd(dy_b, wb_out, o_b, lse_b, proj_b, sel, "b_merge_bwd")
    dwb_in = _mm(h1t, dz_b, tn=B_TN, tile0=B_Z_TILE0, n_tiles=B_Z_TILES, out_dtype=BF16, name="b_dwin_z",
                 out3d=(jb, nsb))
    dh1_parts = [_mm_nt(dz_b, wb_in, tn=B_TN, tile0=B_Z_TILE0, n_tiles=B_Z_TILES, name="b_dh_z")]
    qk_sums = []
    for g, d in enumerate(DILATIONS):
        qn, kn = qkv[g]
        dq, dk, dv = _attn_bwd(qn, kn, proj_b, g, do_c[g], lse_c[g], delta_c[g], slopes, d, f"b_attn_bwd_g{g}")
        dproj, sums_qk = _qknorm_bwd(proj_b, g, qw[g], kw[g], seg, dq, dk, dv, f"b_qknorm_bwd_g{g}")
        qk_sums.append(sums_qk)
        dwb_in = _mm(h1t if d == 1 else h1c[g], dproj, tn=B_TN, tile0=g * B_GROUP_TILES, n_tiles=B_GROUP_TILES,
                     out_dtype=BF16, name=f"b_dwin_g{g}", out3d=(jb, nsb), prev=dwb_in, transpose_lhs=d != 1)
        dh = _mm_nt(dproj, wb_in, tn=B_TN, tile0=g * B_GROUP_TILES, n_tiles=B_GROUP_TILES, name=f"b_dh_g{g}")
        dh1_parts.append(dh)
    token = send_grads_b(dwb_in, dwb_out)
    dx1, sums_n1, dy_a = _normmod_bwd(x1, g1, scale1 + token[0:1, 0:1], dh1_parts, e, "prenorm1_bwd",
                                      part_dilations=(1,) + DILATIONS, gated=(gate0, y_a))
    token = forward_grads_b(dx1)

    dwa_out = _mm(u5t, dy_a, tn=D_MODEL, tile0=0, n_tiles=1, out_dtype=BF16, name="a_dwout")
    du2, dz_a, sums_ln = _conv_bwd_pointwise(dy_a, wa_out, proj_a, u2, ln_g + token[0:1, 0:1], ln_b,
                                             "a_conv_bwd_pw")
    dproj_a, dconv_w = _conv_bwd_taps(du2, dz_a, proj_a, conv_w, "a_conv_bwd_taps")
    dwa_in = _mm(h0t, dproj_a, tn=nsa, tile0=0, n_tiles=ja, out_dtype=BF16, name="a_dwin", out3d=(ja, nsa))
    token = send_grads_a(dwa_in, dwa_out)
    dh0 = _mm_nt(dproj_a, wa_in, tn=nsa, tile0=0, n_tiles=ja, name="a_dh", after=token)
    grad_x, sums_n0 = _normmod_bwd(x, g0, scale0, [dh0], dx1, "prenorm0_bwd")

    small = dict(
        dnorm_g=jnp.concatenate([sums_n0[0:1], sums_n1[0:1]], axis=0),
        dmod0=jnp.concatenate([sums_n0[2:3], sums_n0[1:2], sums_n1[3:4]], axis=0),
        dmod1=jnp.concatenate([sums_n1[2:3], sums_n1[1:2], sums_loss[0:1]], axis=0),
        dln_g=sums_ln[0:1], dln_b=sums_ln[1:2], dconv_b=sums_ln[2:3],
        dconv_w=dconv_w[:CONV_WIDTH],
        dq_norm=jnp.concatenate([s[0:1] for s in qk_sums], axis=0),
        dk_norm=jnp.concatenate([s[1:2] for s in qk_sums], axis=0),
        loss_cols=sums_loss[1:2],
    )
    return grad_x, small


def _adamw(w, g, m, v, name, after=None, copy_grad=False):
    rows, cols = w.shape
    tr = rows if rows <= 128 else 128
    c1 = 1.0 / (1.0 - ADAM_B1 ** ADAM_STEP)
    c2 = 1.0 / (1.0 - ADAM_B2 ** ADAM_STEP)
    extra = [] if after is None else [after]
    n_out = 4 if copy_grad else 3

    def body(w_ref, g_ref, m_ref, v_ref, *rest):
        d_ref, mo_ref, vo_ref = rest[len(extra):len(extra) + 3]
        gv = g_ref[...]
        if copy_grad:
            rest[-1][...] = gv
        mn = ADAM_B1 * m_ref[...] + (1.0 - ADAM_B1) * gv
        vn = ADAM_B2 * v_ref[...] + (1.0 - ADAM_B2) * (gv * gv)
        mo_ref[...] = mn
        vo_ref[...] = vn
        d_ref[...] = -ADAM_LR * ((mn * c1) / (jnp.sqrt(vn * c2) + ADAM_EPS) + ADAM_WD * w_ref[...])

    spec = pl.BlockSpec((tr, cols), lambda i: (i, 0))
    return pl.pallas_call(
        body, name=name, grid=(rows // tr,),
        in_specs=[spec] * 4 + [pl.BlockSpec(memory_space=pl.ANY)] * len(extra), out_specs=[spec] * n_out,
        out_shape=[jax.ShapeDtypeStruct((rows, cols), F32)] * n_out,
        compiler_params=_params("parallel"),
    )(w, g, m, v, *extra)


def _cast_into_slot(w, chip_idx, name, keep_own=False, after=None):
    rows, cols = w.shape
    tr = 256
    extra = [] if after is None else [after]

    def body(ch_ref, w_ref, *rest):
        wb = w_ref[...].astype(BF16)
        for o_ref in rest[len(extra):]:
            o_ref[...] = wb

    slot_spec = pl.BlockSpec((None, tr, cols), lambda i, ch: (ch[0], i, 0))
    own_spec = pl.BlockSpec((None, tr, cols), lambda i, ch: (0, i, 0))
    res = pl.pallas_call(
        body, name=name,
        grid_spec=pltpu.PrefetchScalarGridSpec(
            num_scalar_prefetch=1, grid=(rows // tr,),
            in_specs=[pl.BlockSpec((tr, cols), lambda i, ch: (i, 0))] + [pl.BlockSpec(memory_space=pl.ANY)] * len(extra),
            out_specs=[slot_spec, own_spec] if keep_own else [slot_spec]),
        out_shape=[jax.ShapeDtypeStruct((N_CHIPS, rows, cols), BF16)]
        + ([jax.ShapeDtypeStruct((1, rows, cols), BF16)] if keep_own else []),
        compiler_params=_params("parallel"),
    )(chip_idx, w, *extra)
    return tuple(res) if keep_own else res[0]


def _position():
    x, y, c = lax.axis_index("x"), lax.axis_index("y"), lax.axis_index("c")
    return x, y, c


def _xor_peer(x, y, c, k):
    return (x ^ ((k >> 2) & 1), y ^ ((k >> 1) & 1), c ^ (k & 1))


def _chip_peer(x, y, k):
    return (x ^ ((k >> 1) & 1), y ^ (k & 1))


def _ada_forward(c_row, ada_w, ada_b, conv_w):
    ns = ada_w.shape[2]
    cw = conv_w.shape[1]

    def body(c_ref, w_ref, b_ref, cv_ref, mod_ref, sc_ref, cvo_ref,
             c_all, mp, parts, cv_parts, send1, recv1, send2, recv2, send3, recv3):
        x, y, c = _position()
        me = 4 * x + 2 * y + c
        chip = 2 * x + y

        def c_copy(k):
            return pltpu.make_async_remote_copy(
                src_ref=c_all.at[me], dst_ref=c_all.at[me], send_sem=send1.at[k - 1], recv_sem=recv1.at[k - 1],
                device_id=_xor_peer(x, y, c, k), device_id_type=MESH)

        def cv_copy(k):
            px, py = _chip_peer(x, y, k)
            return pltpu.make_async_remote_copy(
                src_ref=cv_parts.at[chip], dst_ref=cv_parts.at[chip], send_sem=send3.at[k - 1],
                recv_sem=recv3.at[k - 1], device_id=(px, py, c), device_id_type=MESH)

        c_all[me] = c_ref[...]
        cv_parts[chip] = cv_ref[...]
        for k in range(1, N_DEV):
            c_copy(k).start()
        for k in range(1, N_CHIPS):
            cv_copy(k).start()
        for k in range(1, N_DEV):
            c_copy(k).wait_recv()
        cv = jnp.concatenate([c_all[i] for i in range(N_DEV)], axis=0)
        sc = cv * _sigmoid(cv)
        sc_ref[...] = sc
        for l in range(2):
            res = jnp.dot(sc, w_ref[l], preferred_element_type=F32, precision=lax.Precision.HIGHEST)
            for i in range(N_DEV):
                mp[i, l:l + 1, :] = res[i:i + 1, :]

        def mod_copy(k):
            px, py = _chip_peer(x, y, k)
            return pltpu.make_async_remote_copy(
                src_ref=mp.at[4 * px + 2 * py + c], dst_ref=parts.at[chip], send_sem=send2.at[k - 1],
                recv_sem=recv2.at[k - 1], device_id=(px, py, c), device_id_type=MESH)

        for k in range(1, N_CHIPS):
            mod_copy(k).start()
        parts[chip] = mp[me]
        for k in range(1, N_CHIPS):
            mod_copy(k).wait_recv()
            cv_copy(k).wait_recv()
        mod_ref[...] = jnp.concatenate([parts[j] for j in range(N_CHIPS)], axis=1) + b_ref[...]
        cvo_ref[...] = jnp.concatenate([cv_parts[j] for j in range(N_CHIPS)], axis=1)
        for k in range(1, N_DEV):
            c_copy(k).wait_send()
        for k in range(1, N_CHIPS):
            mod_copy(k).wait_send()
            cv_copy(k).wait_send()

    vm = pl.BlockSpec(memory_space=pltpu.VMEM)
    return pl.pallas_call(
        body, name="ada_forward",
        in_specs=[vm] * 4, out_specs=[vm] * 3,
        out_shape=[jax.ShapeDtypeStruct((2, 3 * D_MODEL), F32), jax.ShapeDtypeStruct((N_DEV, D_MODEL), F32),
                   jax.ShapeDtypeStruct((CONV_WIDTH, N_CHIPS * cw), F32)],
        scratch_shapes=[pltpu.VMEM((N_DEV, 1, D_MODEL), F32), pltpu.VMEM((N_DEV, 2, ns), F32),
                        pltpu.VMEM((N_CHIPS, 2, ns), F32), pltpu.VMEM((N_CHIPS, CONV_WIDTH, cw), F32),
                        pltpu.SemaphoreType.DMA((N_DEV - 1,)), pltpu.SemaphoreType.DMA((N_DEV - 1,)),
                        pltpu.SemaphoreType.DMA((N_CHIPS - 1,)), pltpu.SemaphoreType.DMA((N_CHIPS - 1,)),
                        pltpu.SemaphoreType.DMA((N_CHIPS - 1,)), pltpu.SemaphoreType.DMA((N_CHIPS - 1,))],
        compiler_params=pltpu.CompilerParams(vmem_limit_bytes=VMEM_LIMIT_BYTES),
    )(c_row, ada_w, ada_b, conv_w)


HBM_SPEC = pl.BlockSpec(memory_space=pltpu.HBM)
ANY_SPEC = pl.BlockSpec(memory_space=pl.ANY)
SEM_SPEC = pl.BlockSpec(memory_space=pltpu.SEMAPHORE)
SPLIT_PARAMS = dict(compiler_params=pltpu.CompilerParams(has_side_effects=pltpu.SideEffectType.DATAFLOW_SIDE_EFFECTING))
TOKEN = jax.ShapeDtypeStruct((8, 128), F32)


def _hbm(arrays):
    return [pltpu.with_memory_space_constraint(a, pltpu.HBM) for a in arrays]


def _hbm_like(arrays):
    return [pltpu.HBM(a.shape, a.dtype) for a in arrays]


def _gather_start(lands, after, name):
    n = len(lands)

    def body(*refs):
        ins = refs[:n]
        send, recv = refs[n + 1], refs[n + 2]
        x, y, c = _position()
        chip = 2 * x + y
        for t in range(n):
            rh = ins[t].shape[1] // 2
            for k in range(1, N_CHIPS):
                px, py = _chip_peer(x, y, k)
                block = ins[t].at[chip, pl.ds(c * rh, rh)]
                pltpu.make_async_remote_copy(
                    src_ref=block, dst_ref=block, send_sem=send.at[3 * t + k - 1], recv_sem=recv.at[3 * t + k - 1],
                    device_id=(px, py, c), device_id_type=MESH).start()
        refs[-1][...] = jnp.zeros(TOKEN.shape, F32)

    res = pl.pallas_call(
        body, name=name, in_specs=[HBM_SPEC] * n + [ANY_SPEC],
        out_specs=(SEM_SPEC, SEM_SPEC, *[HBM_SPEC] * n, pl.BlockSpec(memory_space=pltpu.VMEM)),
        out_shape=(pltpu.SemaphoreType.DMA((3 * n,)), pltpu.SemaphoreType.DMA((3 * n,)), *_hbm_like(lands), TOKEN),
        input_output_aliases={t: 2 + t for t in range(n)}, **SPLIT_PARAMS,
    )(*_hbm(lands), after)
    return res[0], res[1], list(res[2:2 + n]), res[-1]


def _gather_forward(send, recv, lands, after, name):
    n = len(lands)

    def body(*refs):
        ins = refs[:n]
        send1, recv1 = refs[n], refs[n + 1]
        send2, recv2 = refs[n + 3], refs[n + 4]
        x, y, c = _position()
        chip = 2 * x + y
        for t in range(n):
            rh = ins[t].shape[1] // 2
            half = pl.ds(c * rh, rh)
            for k in range(1, N_CHIPS):
                px, py = _chip_peer(x, y, k)
                s = 3 * t + k - 1
                got = ins[t].at[2 * px + py, half]
                cp = pltpu.make_async_remote_copy(
                    src_ref=ins[t].at[chip, half], dst_ref=got, send_sem=send1.at[s], recv_sem=recv1.at[s],
                    device_id=(px, py, c), device_id_type=MESH)
                cp.wait_send()
                cp.wait_recv()
                pltpu.make_async_remote_copy(
                    src_ref=got, dst_ref=got, send_sem=send2.at[s], recv_sem=recv2.at[s],
                    device_id=(x, y, 1 - c), device_id_type=MESH).start()
        refs[-1][...] = jnp.zeros(TOKEN.shape, F32)

    res = pl.pallas_call(
        body, name=name, in_specs=[HBM_SPEC] * n + [SEM_SPEC, SEM_SPEC, ANY_SPEC],
        out_specs=(SEM_SPEC, SEM_SPEC, *[HBM_SPEC] * n, pl.BlockSpec(memory_space=pltpu.VMEM)),
        out_shape=(pltpu.SemaphoreType.DMA((3 * n,)), pltpu.SemaphoreType.DMA((3 * n,)), *_hbm_like(lands), TOKEN),
        input_output_aliases={t: 2 + t for t in range(n)}, **SPLIT_PARAMS,
    )(*lands, send, recv, after)
    return res[0], res[1], list(res[2:2 + n]), res[-1]


def _gather_wait(send, recv, lands, after, name):
    n = len(lands)

    def body(*refs):
        ins = refs[:n]
        send_ref, recv_ref = refs[n], refs[n + 1]
        x, y, c = _position()
        for t in range(n):
            rh = ins[t].shape[1] // 2
            for k in range(1, N_CHIPS):
                px, py = _chip_peer(x, y, k)
                cp = pltpu.make_async_remote_copy(
                    src_ref=ins[t].at[2 * px + py, pl.ds(c * rh, rh)],
                    dst_ref=ins[t].at[2 * px + py, pl.ds((1 - c) * rh, rh)], send_sem=send_ref.at[3 * t + k - 1],
                    recv_sem=recv_ref.at[3 * t + k - 1], device_id=(x, y, 1 - c), device_id_type=MESH)
                cp.wait_send()
                cp.wait_recv()

    res = pl.pallas_call(
        body, name=name, in_specs=[HBM_SPEC] * n + [SEM_SPEC, SEM_SPEC, ANY_SPEC], out_specs=[HBM_SPEC] * n,
        out_shape=_hbm_like(lands), input_output_aliases={t: t for t in range(n)}, **SPLIT_PARAMS,
    )(*lands, send, recv, after)
    return list(res)


def _split_start(name, arrays, n_sems, after, issue):
    m = len(arrays)

    def body(*refs):
        issue(refs[:m], refs[m + 1], refs[m + 2])
        refs[-1][...] = jnp.zeros(TOKEN.shape, F32)

    res = pl.pallas_call(
        body, name=name, in_specs=[HBM_SPEC] * m + [ANY_SPEC],
        out_specs=(SEM_SPEC, SEM_SPEC, *[HBM_SPEC] * m, pl.BlockSpec(memory_space=pltpu.VMEM)),
        out_shape=(pltpu.SemaphoreType.DMA((n_sems,)), pltpu.SemaphoreType.DMA((n_sems,)), *_hbm_like(arrays), TOKEN),
        input_output_aliases={t: 2 + t for t in range(m)}, **SPLIT_PARAMS,
    )(*_hbm(arrays), after)
    return res[0], res[1], list(res[2:2 + m]), res[-1]


def _split_wait(name, arrays, send, recv, after, await_all):
    m = len(arrays)

    def body(*refs):
        await_all(refs[:m], refs[m], refs[m + 1])

    res = pl.pallas_call(
        body, name=name, in_specs=[HBM_SPEC] * m + [SEM_SPEC, SEM_SPEC, ANY_SPEC], out_specs=[HBM_SPEC] * m,
        out_shape=_hbm_like(arrays), input_output_aliases={t: t for t in range(m)}, **SPLIT_PARAMS,
    )(*arrays, send, recv, after)
    return list(res)


def _sibling_copies(refs, send, recv, n):
    x, y, c = _position()
    cps = []
    for t in range(n):
        rh = refs[t].shape[1] // 2
        cps.append(pltpu.make_async_remote_copy(
            src_ref=refs[t].at[pl.ds(0, N_CHIPS), pl.ds((1 - c) * rh, rh)], dst_ref=refs[n + t],
            send_sem=send.at[t], recv_sem=recv.at[t], device_id=(x, y, 1 - c), device_id_type=MESH))
    return cps


def _reduce_sibling_start(grads, after, name):
    n = len(grads)
    lands = [lax.empty((N_CHIPS, g.shape[1] // 2, g.shape[2]), BF16) for g in grads]

    def issue(refs, send, recv):
        for cp in _sibling_copies(refs, send, recv, n):
            cp.start()

    return _split_start(name, list(grads) + lands, n, after, issue)


def _reduce_sibling_wait(send, recv, arrays, after, name):
    n = len(arrays) // 2

    def await_all(refs, send_ref, recv_ref):
        for cp in _sibling_copies(refs, send_ref, recv_ref, n):
            cp.wait_send()
            cp.wait_recv()

    res = _split_wait(name, arrays, send, recv, after, await_all)
    return res[:n], res[n:]


def _add_sibling_half(grad, got, dev_idx, name):
    j, r, cols = grad.shape
    rh = r // 2
    tr = rh
    nb = rh // tr

    def body(idx_ref, g_ref, got_ref, out_ref):
        out_ref[...] = (g_ref[...].astype(F32) + got_ref[...].astype(F32)).astype(BF16)

    return pl.pallas_call(
        body, name=name,
        grid_spec=pltpu.PrefetchScalarGridSpec(
            num_scalar_prefetch=1, grid=(j, nb),
            in_specs=[pl.BlockSpec((None, tr, cols), lambda jj, i, idx: (jj, idx[2] * nb + i, 0)),
                      pl.BlockSpec((None, tr, cols), lambda jj, i, idx: (jj, i, 0))],
            out_specs=pl.BlockSpec((None, tr, cols), lambda jj, i, idx: (jj, i, 0))),
        out_shape=jax.ShapeDtypeStruct((j, rh, cols), BF16),
        compiler_params=_params("parallel", "parallel"),
    )(dev_idx, grad, got)


def _chip_copies(refs, send, recv, n, receiving):
    x, y, c = _position()
    chip = 2 * x + y
    cps = []
    for t in range(n):
        for k in range(1, N_CHIPS):
            px, py = _chip_peer(x, y, k)
            cps.append(pltpu.make_async_remote_copy(
                src_ref=refs[t].at[2 * px + py], dst_ref=refs[n + t].at[2 * px + py if receiving else chip],
                send_sem=send.at[3 * t + k - 1], recv_sem=recv.at[3 * t + k - 1],
                device_id=(px, py, c), device_id_type=MESH))
    return cps


def _reduce_chips_start(partials, after, name):
    n = len(partials)
    lands = [lax.empty(p.shape, BF16) for p in partials]

    def issue(refs, send, recv):
        for cp in _chip_copies(refs, send, recv, n, False):
            cp.start()

    return _split_start(name, list(partials) + lands, 3 * n, after, issue)


def _reduce_chips_wait(send, recv, arrays, after, name):
    n = len(arrays) // 2

    def await_all(refs, send_ref, recv_ref):
        for cp in _chip_copies(refs, send_ref, recv_ref, n, True):
            cp.wait_send()
            cp.wait_recv()

    res = _split_wait(name, arrays, send, recv, after, await_all)
    return res[:n], res[n:]


def _sum_partials(land, partial, dev_idx, name):
    _, rh, cols = land.shape
    tr = min(rh, 256)
    nb = rh // tr

    def body(idx_ref, l_ref, p_ref, o_ref):
        chip = idx_ref[1]
        acc = jnp.where(chip == 0, p_ref[...], l_ref[0]).astype(F32)
        for s in range(1, N_CHIPS):
            acc = acc + jnp.where(chip == s, p_ref[...], l_ref[s]).astype(F32)
        o_ref[...] = acc

    return pl.pallas_call(
        body, name=name,
        grid_spec=pltpu.PrefetchScalarGridSpec(
            num_scalar_prefetch=1, grid=(nb,),
            in_specs=[pl.BlockSpec((N_CHIPS, tr, cols), lambda i, idx: (0, i, 0)),
                      pl.BlockSpec((None, tr, cols), lambda i, idx: (idx[1], i, 0))],
            out_specs=pl.BlockSpec((tr, cols), lambda i, idx: (idx[2] * nb + i, 0))),
        out_shape=jax.ShapeDtypeStruct((2 * rh, cols), F32), compiler_params=_params("parallel"),
    )(dev_idx, land, partial)


def _half_copies(refs, send, recv, receiving):
    x, y, c = _position()
    cps = []
    for t, ref in enumerate(refs):
        rh = ref.shape[0] // 2
        cps.append(pltpu.make_async_remote_copy(
            src_ref=ref.at[pl.ds(c * rh, rh)], dst_ref=ref.at[pl.ds(((1 - c) if receiving else c) * rh, rh)],
            send_sem=send.at[t], recv_sem=recv.at[t], device_id=(x, y, 1 - c), device_id_type=MESH))
    return cps


def _share_halves_start(totals, after, name):
    def issue(refs, send, recv):
        for cp in _half_copies(refs, send, recv, False):
            cp.start()

    return _split_start(name, list(totals), len(totals), after, issue)


def _share_halves_wait(send, recv, totals, after, name):
    def await_all(refs, send_ref, recv_ref):
        for cp in _half_copies(refs, send_ref, recv_ref, True):
            cp.wait_send()
            cp.wait_recv()

    return _split_wait(name, totals, send, recv, after, await_all)


SMALL_ROWS = 56


def _small_copies(refs, send, recv, receiving):
    x, y, c = _position()
    me = 4 * x + 2 * y + c
    cps = []
    for k in range(1, N_DEV):
        px, py, pc = _xor_peer(x, y, c, k)
        cps.append(pltpu.make_async_remote_copy(
            src_ref=refs[0], dst_ref=refs[1].at[4 * px + 2 * py + pc if receiving else me],
            send_sem=send.at[k - 1], recv_sem=recv.at[k - 1], device_id=(px, py, pc), device_id_type=MESH))
    return cps


def _small_gather_start(packed, after):
    land = lax.empty((N_DEV,) + packed.shape, F32)

    def issue(refs, send, recv):
        for cp in _small_copies(refs, send, recv, False):
            cp.start()

    return _split_start("small_gather_start", [packed, land], N_DEV - 1, after, issue)


def _small_gather_wait(send, recv, arrays, after):
    def await_all(refs, send_ref, recv_ref):
        for cp in _small_copies(refs, send_ref, recv_ref, True):
            cp.wait_send()
            cp.wait_recv()

    return _split_wait("small_gather_wait", arrays, send, recv, after, await_all)


def _reduce_small(packed, land, silu_c):
    ns = 3 * D_MODEL // N_CHIPS

    def body(p_ref, land_ref, sc_ref, tot_ref, gw_ref, loss_ref, qk_ref, allp):
        x, y, c = _position()
        me = 4 * x + 2 * y + c
        chip = 2 * x + y
        for i in range(N_DEV):
            allp[i] = jnp.where(me == i, p_ref[...], land_ref[i])
        tot = allp[0]
        for i in range(1, N_DEV):
            tot = tot + allp[i]
        tot_ref[...] = tot
        loss_ref[...] = jnp.sum(tot[11:12, :], axis=1, keepdims=True) * (0.5 / D_MODEL)
        fold = tot[5:11, 0:HEAD_DIM]
        for h in range(1, N_HEADS):
            fold = fold + tot[5:11, h * HEAD_DIM:(h + 1) * HEAD_DIM]
        qk_ref[...] = jnp.concatenate([fold, jnp.zeros((2, HEAD_DIM), F32)], axis=0)
        sct = sc_ref[...].T
        rc = 64
        for l in range(2):
            dms = [allp[i, pl.ds(12 + 4 * l + chip, 1), :][:, :ns] for i in range(N_DEV)]
            for r0 in range(0, D_MODEL, rc):
                acc = sct[r0:r0 + rc, 0:1] * dms[0]
                for i in range(1, N_DEV):
                    acc = acc + sct[r0:r0 + rc, i:i + 1] * dms[i]
                gw_ref[l, r0:r0 + rc, :] = acc

    vm = pl.BlockSpec(memory_space=pltpu.VMEM)
    return pl.pallas_call(
        body, name="reduce_small", in_specs=[vm, vm, vm], out_specs=[vm] * 4,
        out_shape=[jax.ShapeDtypeStruct((SMALL_ROWS, D_MODEL), F32), jax.ShapeDtypeStruct((2, D_MODEL, ns), F32),
                   jax.ShapeDtypeStruct((1, 1), F32), jax.ShapeDtypeStruct((8, HEAD_DIM), F32)],
        scratch_shapes=[pltpu.VMEM((N_DEV, SMALL_ROWS, D_MODEL), F32)],
        compiler_params=pltpu.CompilerParams(vmem_limit_bytes=VMEM_LIMIT_BYTES),
    )(packed, land, silu_c)


def kernel(x, c, norm_g, ada_w, ada_b, a_w_in, a_conv_w, a_conv_b, a_ln_g, a_ln_b, a_w_out, b_w_in, b_q_norm, b_k_norm, b_w_out, loss_target, m_norm_g, m_ada_w, m_ada_b, m_a_w_in, m_a_conv_w, m_a_conv_b, m_a_ln_g, m_a_ln_b, m_a_w_out, m_b_w_in, m_b_q_norm, m_b_k_norm, m_b_w_out, v_norm_g, v_ada_w, v_ada_b, v_a_w_in, v_a_conv_w, v_a_conv_b, v_a_ln_g, v_a_ln_b, v_a_w_out, v_b_w_in, v_b_q_norm, v_b_k_norm, v_b_w_out):
    chip = 2 * lax.axis_index("x") + lax.axis_index("y")
    core = lax.axis_index("c")
    chip_idx = chip.astype(jnp.int32).reshape(1)
    dev_idx = jnp.stack([2 * chip + core, chip, core]).astype(jnp.int32)

    mods, silu_c, conv_w_full = _ada_forward(c, ada_w, ada_b, a_conv_w[0])
    lands_a = [_cast_into_slot(a_w_in[0], chip_idx, "cast_a_w_in"), _cast_into_slot(a_w_out[0], chip_idx, "cast_a_w_out")]
    send_a, recv_a, lands_a, token_a = _gather_start(lands_a, mods, "gather_start_a")
    land_b_in, own_wb_in = _cast_into_slot(b_w_in[0], chip_idx, "cast_b_w_in", keep_own=True, after=token_a)
    lands_b = [land_b_in, _cast_into_slot(b_w_out[0], chip_idx, "cast_b_w_out", after=token_a)]
    send_b, recv_b, lands_b, token_b = _gather_start(lands_b, token_a, "gather_start_b")
    mods = mods + token_b[0:2, 0:1]

    def weights_a(after):
        send, recv, lands, _ = _gather_forward(send_a, recv_a, lands_a, after, "gather_forward_a")
        w_in, w_out = _gather_wait(send, recv, lands, after, "gather_wait_a")
        return w_in, w_out.reshape(D_MODEL, D_MODEL)

    forwarded_b = []

    def weights_b(after):
        send, recv, lands, _ = forwarded_b
        w_in, w_out = _gather_wait(send, recv, lands, after, "gather_wait_b")
        return w_in, w_out.reshape(D_MODEL, D_MODEL)

    def forward_weights_b(after):
        forwarded_b.extend(_gather_forward(send_b, recv_b, lands_b, after, "gather_forward_b"))
        return forwarded_b[3]

    stage1, stage2 = {}, {}

    def send_grads(tag, dw_in, dw_out):
        grads = [dw_in, dw_out.reshape(N_CHIPS, D_MODEL // N_CHIPS, D_MODEL)]
        send, recv, arrays, token = _reduce_sibling_start(grads, dw_out, f"reduce_d2d_start_{tag}")
        stage1[tag] = (send, recv, arrays)
        return token

    def forward_grads(tag, after):
        send, recv, arrays = stage1[tag]
        grads, got = _reduce_sibling_wait(send, recv, arrays, after, f"reduce_d2d_wait_{tag}")
        partials = [_add_sibling_half(grads[i], got[i], dev_idx, f"reduce_add_{tag}_{i}") for i in range(2)]
        send, recv, arrays, token = _reduce_chips_start(partials, partials[1], f"reduce_ici_start_{tag}")
        stage2[tag] = (send, recv, arrays)
        return token

    stage3 = {}

    def sum_grads(tag, after):
        send, recv, arrays = stage2[tag]
        partials, lands = _reduce_chips_wait(send, recv, arrays, after, f"reduce_ici_wait_{tag}")
        totals = [_sum_partials(lands[i], partials[i], dev_idx, f"reduce_sum_{tag}_{i}") for i in range(2)]
        send, recv, totals, token = _share_halves_start(totals, totals[1], f"reduce_share_start_{tag}")
        stage3[tag] = (send, recv, totals)
        return token

    def finish_grads(tag, after):
        send, recv, totals = stage3[tag]
        return _share_halves_wait(send, recv, totals, after, f"reduce_share_wait_{tag}")

    grad_x, small = _local_step(
        x[0], loss_target[0], mods.reshape(2, 3, D_MODEL), norm_g, conv_w_full, a_conv_b, a_ln_g[0:1],
        a_ln_b[0:1], b_q_norm[0], b_k_norm[0], chip.astype(jnp.int32), own_wb_in,
        weights_a, weights_b, forward_weights_b,
        functools.partial(send_grads, "b"), functools.partial(forward_grads, "b"), functools.partial(send_grads, "a"))

    ns = 3 * D_MODEL // N_CHIPS
    pad_mod = lambda dm: jnp.pad(dm.reshape(N_CHIPS, ns), ((0, 0), (0, D_MODEL - ns)))
    packed = jnp.concatenate([
        small["dnorm_g"], small["dconv_b"], small["dln_g"], small["dln_b"], small["dq_norm"], small["dk_norm"],
        small["loss_cols"], pad_mod(small["dmod0"]), pad_mod(small["dmod1"]), small["dconv_w"],
        jnp.zeros((SMALL_ROWS - 20 - CONV_WIDTH, D_MODEL), F32)], axis=0)
    send_s, recv_s, small_arrays, token_s = _small_gather_start(packed, packed)

    given = dict(norm_g=(norm_g, m_norm_g, v_norm_g), ada_w=(ada_w, m_ada_w, v_ada_w), ada_b=(ada_b, m_ada_b, v_ada_b),
                 a_w_in=(a_w_in, m_a_w_in, v_a_w_in), a_conv_w=(a_conv_w, m_a_conv_w, v_a_conv_w),
                 a_conv_b=(a_conv_b, m_a_conv_b, v_a_conv_b), a_ln_g=(a_ln_g, m_a_ln_g, v_a_ln_g),
                 a_ln_b=(a_ln_b, m_a_ln_b, v_a_ln_b), a_w_out=(a_w_out, m_a_w_out, v_a_w_out),
                 b_w_in=(b_w_in, m_b_w_in, v_b_w_in), b_q_norm=(b_q_norm, m_b_q_norm, v_b_q_norm),
                 b_k_norm=(b_k_norm, m_b_k_norm, v_b_k_norm), b_w_out=(b_w_out, m_b_w_out, v_b_w_out))
    order = ["norm_g", "ada_w", "ada_b", "a_w_in", "a_conv_w", "a_conv_b", "a_ln_g", "a_ln_b", "a_w_out", "b_w_in",
             "b_q_norm", "b_k_norm", "b_w_out"]
    outs = {}

    def update(k, g2, after=None, copy_grad=False):
        w, m, v = given[k]
        shape2 = g2.shape
        res = _adamw(w.reshape(shape2), g2, m.reshape(shape2), v.reshape(shape2), f"adamw_{k}", after, copy_grad)
        outs[k] = tuple(a.reshape(w.shape) for a in ((res[3] if copy_grad else g2), res[0], res[1], res[2]))

    token = forward_grads("a", token_s)
    token = sum_grads("b", token)
    packed, land = _small_gather_wait(send_s, recv_s, small_arrays, token)
    tot, g_ada_w, loss, qk = _reduce_small(packed, land, silu_c)
    g_b_in, g_b_out = finish_grads("b", tot)
    update("b_w_in", g_b_in, copy_grad=True)
    update("b_w_out", g_b_out, copy_grad=True)
    token = sum_grads("a", outs["b_w_in"][1])
    cw = D_MODEL // N_CHIPS
    g_small = dict(
        norm_g=tot[0:2], a_conv_b=tot[2:3], a_ln_g=tot[3:4], a_ln_b=tot[4:5],
        b_q_norm=qk[0:3], b_k_norm=qk[3:6],
        ada_b=jnp.stack([tot[12:16, :ns].reshape(3 * D_MODEL), tot[16:20, :ns].reshape(3 * D_MODEL)]),
        a_conv_w=lax.dynamic_slice(tot[20:20 + CONV_WIDTH], (0, chip * cw), (CONV_WIDTH, cw)),
    )
    update("ada_w", g_ada_w.reshape(2 * D_MODEL, ns), after=token)
    for k, g2 in g_small.items():
        update(k, g2, after=token)
    g_a_in, g_a_out = finish_grads("a", outs["ada_w"][1])
    update("a_w_in", g_a_in, copy_grad=True)
    update("a_w_out", g_a_out, copy_grad=True)
    return (loss.reshape(()), grad_x[None], *[outs[k][0] for k in order], *[outs[k][1] for k in order],
            *[outs[k][2] for k in order], *[outs[k][3] for k in order])
```

```python
import functools

import jax
import jax.numpy as jnp
from jax import lax
from jax.experimental import pallas as pl
from jax.experimental.pallas import tpu as pltpu

F32 = jnp.float32
BF16 = jnp.bfloat16

SEQ = 2048
D_MODEL = 1024
CONV_WIDTH = 31
HEAD_DIM = 64
N_HEADS = 16
DILATIONS = (1, 4, 16)
ATTN_BLOCK = 128
NORM_EPS = 1e-6
NEG_INF = -1e30
N_DEV = 8
N_CHIPS = 4

ADAM_LR = 0.001
ADAM_B1 = 0.9
ADAM_B2 = 0.999
ADAM_EPS = 1e-08
ADAM_WD = 0.01
ADAM_STEP = 10

VMEM_LIMIT_BYTES = 52 * 1024 * 1024
HALO = 32
LANES = 128
MESH = pl.DeviceIdType.MESH


def _params(*sem):
    return pltpu.CompilerParams(dimension_semantics=sem or None, vmem_limit_bytes=VMEM_LIMIT_BYTES)


def _sigmoid(v):
    return 1.0 / (1.0 + jnp.exp(-v))


def _row_spec(tm, cols, col_block=0):
    return pl.BlockSpec((tm, cols), lambda i: (i, col_block))


def _vec_spec(rows, cols):
    return pl.BlockSpec((rows, cols), lambda i: (0, 0))


def _normmod(xv, g, scale, shift):
    r = lax.rsqrt(jnp.mean(xv * xv, axis=-1, keepdims=True) + NORM_EPS)
    return xv * r * g * (1.0 + scale) + shift


def _normmod_fwd(x, g, scale, shift, name):
    tm = 256

    def body(x_ref, g_ref, sc_ref, sh_ref, h_ref, ht_ref):
        h = _normmod(x_ref[...], g_ref[...], sc_ref[...], sh_ref[...])
        h_ref[...] = h.astype(BF16)
        ht_ref[...] = h.T.astype(BF16)

    return pl.pallas_call(
        body, name=name, grid=(SEQ // tm,),
        in_specs=[_row_spec(tm, D_MODEL)] + [_vec_spec(1, D_MODEL)] * 3,
        out_specs=[_row_spec(tm, D_MODEL), pl.BlockSpec((D_MODEL, tm), lambda i: (0, i))],
        out_shape=[jax.ShapeDtypeStruct((SEQ, D_MODEL), BF16), jax.ShapeDtypeStruct((D_MODEL, SEQ), BF16)],
        compiler_params=_params("parallel"),
    )(x, g, scale, shift)


def _normmod_bwd(x, g, scale, dh_parts, dres, name, part_dilations=None, gated=None):
    tm = 256
    n_parts = len(dh_parts)
    dils = part_dilations or (1,) * n_parts
    dh_parts = [p if d == 1 else p.reshape(d, SEQ // d, D_MODEL) for p, d in zip(dh_parts, dils)]
    n_gated = 0 if gated is None else 2

    def body(x_ref, g_ref, sc_ref, dres_ref, *rest):
        part_refs = rest[:n_parts]
        gated_refs = rest[n_parts:n_parts + n_gated]
        out_refs = rest[n_parts + n_gated:]
        dx_ref, sums_ref, nat = out_refs[0], out_refs[1], out_refs[-1]
        xv = x_ref[...]
        r = lax.rsqrt(jnp.mean(xv * xv, axis=-1, keepdims=True) + NORM_EPS)
        xn = xv * r
        dh = _load_natural(part_refs[0], nat, dils[0])
        for p, d in zip(part_refs[1:], dils[1:]):
            dh = dh + _load_natural(p, nat, d)
        gv = g_ref[...]
        one_sc = 1.0 + sc_ref[...]
        dxn = dh * (gv * one_sc)
        dx = dres_ref[...] + r * (dxn - xn * jnp.mean(dxn * xn, axis=-1, keepdims=True))
        dx_ref[...] = dx
        dhx = dh * xn
        rows = [jnp.sum(dhx, axis=0, keepdims=True) * one_sc,
                jnp.sum(dhx, axis=0, keepdims=True) * gv,
                jnp.sum(dh, axis=0, keepdims=True)]
        if gated is not None:
            gate_ref, y_ref = gated_refs
            out_refs[2][...] = (dx * gate_ref[...]).astype(BF16)
            rows.append(jnp.sum(dx * y_ref[...], axis=0, keepdims=True))
        sums = jnp.concatenate(rows + [jnp.zeros((8 - len(rows), D_MODEL), F32)], axis=0)

        @pl.when(pl.program_id(0) == 0)
        def _():
            sums_ref[...] = jnp.zeros_like(sums_ref)

        sums_ref[...] += sums

    gated_specs = [] if gated is None else [_vec_spec(1, D_MODEL), _row_spec(tm, D_MODEL)]
    dy_spec = [] if gated is None else [_row_spec(tm, D_MODEL)]
    dy_shape = [] if gated is None else [jax.ShapeDtypeStruct((SEQ, D_MODEL), BF16)]
    return pl.pallas_call(
        body, name=name, grid=(SEQ // tm,),
        in_specs=[_row_spec(tm, D_MODEL), _vec_spec(1, D_MODEL), _vec_spec(1, D_MODEL), _row_spec(tm, D_MODEL)]
        + [_class_spec(tm, d) for d in dils] + gated_specs,
        out_specs=[_row_spec(tm, D_MODEL), _vec_spec(8, D_MODEL)] + dy_spec,
        out_shape=[jax.ShapeDtypeStruct((SEQ, D_MODEL), F32), jax.ShapeDtypeStruct((8, D_MODEL), F32)] + dy_shape,
        scratch_shapes=[_natural_scratch(tm)],
        compiler_params=_params("arbitrary"),
    )(x, g, scale, dres, *dh_parts, *(gated or ()))


def _mm(lhs, rhs, *, tn, tile0, n_tiles, out_dtype, name, out3d=None, prev=None, transpose_lhs=False):
    mo, kc = lhs.shape[::-1] if transpose_lhs else lhs.shape
    cm = min(mo, 1024)
    tc = 256

    def body(l_ref, r_ref, *rest):
        if transpose_lhs:
            o_ref, lt_ref = rest[-2], rest[-1]

            @pl.when(pl.program_id(0) == 0)
            def _():
                for c in range(kc // tc):
                    lt_ref[:, c * tc:(c + 1) * tc] = l_ref[c * tc:(c + 1) * tc, :].astype(F32).T.astype(l_ref.dtype)
        else:
            o_ref, lt_ref = rest[-1], l_ref
        for m in range(mo // cm):
            rows = pl.ds(m * cm, cm)
            o_ref[rows, :] = jnp.dot(lt_ref[rows, :], r_ref[...], preferred_element_type=F32).astype(out_dtype)

    if rhs.ndim == 3:
        tps_r = rhs.shape[2] // tn
        r_spec = pl.BlockSpec((None, kc, tn), lambda t: ((tile0 + t) // tps_r, 0, (tile0 + t) % tps_r))
    else:
        r_spec = pl.BlockSpec((kc, tn), lambda t: (0, t))
    in_specs = [pl.BlockSpec(lhs.shape, lambda t: (0, 0)), r_spec]
    args = [lhs, rhs]
    aliases = {}
    if out3d is None:
        o_spec = pl.BlockSpec((mo, tn), lambda t: (0, t))
        o_shape = jax.ShapeDtypeStruct((mo, n_tiles * tn), out_dtype)
    else:
        j_out, ns_out = out3d
        tps_o = ns_out // tn
        o_spec = pl.BlockSpec((None, mo, tn), lambda t: ((tile0 + t) // tps_o, 0, (tile0 + t) % tps_o))
        o_shape = jax.ShapeDtypeStruct((j_out, mo, ns_out), out_dtype)
        if prev is not None:
            in_specs.append(pl.BlockSpec(memory_space=pl.ANY))
            args.append(prev)
            aliases = {2: 0}
    return pl.pallas_call(
        body, name=name, grid=(n_tiles,), in_specs=in_specs, out_specs=o_spec, out_shape=o_shape,
        input_output_aliases=aliases,
        scratch_shapes=[pltpu.VMEM((mo, kc), lhs.dtype)] if transpose_lhs else [],
        compiler_params=_params("arbitrary" if transpose_lhs else "parallel"),
    )(*args)


def _b_in_tiles(h_parts, w3, tile_ids, n_tiles, name, prev=None):
    _, kc, ns = w3.shape
    tps = ns // B_TN
    cm = 1024

    def body(ids_ref, h0_ref, h1_ref, h2_ref, w_ref, *rest):
        o_ref = rest[-1]
        out_tile = ids_ref[1, pl.program_id(0)]
        group = jnp.where(out_tile >= B_Z_TILE0, 0, out_tile // B_GROUP_TILES)
        for g, h_ref in enumerate((h0_ref, h1_ref, h2_ref)):
            @pl.when(group == g)
            def _():
                for m in range(SEQ // cm):
                    rows = pl.ds(m * cm, cm)
                    o_ref[rows, :] = jnp.dot(h_ref[rows, :], w_ref[...], preferred_element_type=F32).astype(BF16)

    resident = pl.BlockSpec((SEQ, kc), lambda t, ids: (0, 0))
    in_specs = [resident] * 3 + [pl.BlockSpec((None, kc, B_TN), lambda t, ids: (ids[0, t] // tps, 0, ids[0, t] % tps))]
    args = [*h_parts, w3]
    aliases = {}
    if prev is not None:
        in_specs.append(pl.BlockSpec(memory_space=pl.ANY))
        args.append(prev)
        aliases = {5: 0}
    return pl.pallas_call(
        body, name=name,
        grid_spec=pltpu.PrefetchScalarGridSpec(
            num_scalar_prefetch=1, grid=(n_tiles,), in_specs=in_specs,
            out_specs=pl.BlockSpec((SEQ, B_TN), lambda t, ids: (0, ids[1, t]))),
        out_shape=jax.ShapeDtypeStruct((SEQ, B_TILES * B_TN), BF16),
        input_output_aliases=aliases, compiler_params=_params("arbitrary"),
    )(tile_ids, *args)


def _mm_nt(dy, w3, *, tn, tile0, n_tiles, name, after=None):
    m_rows = dy.shape[0]
    _, kc, ns = w3.shape
    tps = ns // tn
    cm = 512
    extra = [] if after is None else [after]

    def body(dy_ref, w_ref, *rest):
        o_ref = rest[-1]

        @pl.when(pl.program_id(0) == 0)
        def _():
            o_ref[...] = jnp.zeros_like(o_ref)

        for m in range(m_rows // cm):
            rows = pl.ds(m * cm, cm)
            o_ref[rows, :] += lax.dot_general(dy_ref[rows, :], w_ref[...], (((1,), (1,)), ((), ())),
                                              preferred_element_type=F32)

    return pl.pallas_call(
        body, name=name, grid=(n_tiles,),
        in_specs=[pl.BlockSpec((m_rows, tn), lambda t: (0, t)),
                  pl.BlockSpec((None, kc, tn), lambda t: ((tile0 + t) // tps, 0, (tile0 + t) % tps))]
        + [pl.BlockSpec(memory_space=pl.ANY)] * len(extra),
        out_specs=pl.BlockSpec((m_rows, kc), lambda t: (0, 0)),
        out_shape=jax.ShapeDtypeStruct((m_rows, kc), F32),
        compiler_params=_params("arbitrary"),
    )(dy, w3, *extra)


CONV_CHUNK = 16


def _shift_copies(buf, shifted):
    rows = shifted.shape[1]
    for s in range(1, 8):
        shifted[s - 1] = buf[pl.ds(s, rows), :]


def _shifted_rows(buf, shifted, offset, r0):
    s = offset % 8
    if s == 0:
        return buf[pl.ds(r0 + offset, CONV_CHUNK), :]
    return shifted[s - 1, pl.ds(r0 + (offset - s), CONV_CHUNK), :]


def _spread_taps(w_ref, taps):
    for k in range(CONV_WIDTH):
        taps[k] = jnp.broadcast_to(w_ref[k:k + 1, :], (8, D_MODEL))


def _times_tap(taps, k, rows):
    return (rows.reshape(CONV_CHUNK // 8, 8, D_MODEL) * taps[k][None]).reshape(CONV_CHUNK, D_MODEL)


def _conv_fwd(proj, conv_w, conv_b, ln_g, ln_b, name):
    tm = 256
    hb = tm // HALO

    def body(vg_ref, halo_ref, z_ref, w_ref, b_ref, g_ref, be_ref, u5_ref, u5t_ref, u2_ref, buf, shifted, taps):
        i = pl.program_id(0)
        u1 = vg_ref[:, :D_MODEL] * _sigmoid(vg_ref[:, D_MODEL:])
        u1h = halo_ref[:, :D_MODEL] * _sigmoid(halo_ref[:, D_MODEL:])
        buf[pl.ds(0, HALO), :] = jnp.where(i > 0, u1h, 0.0)
        buf[pl.ds(HALO, tm), :] = u1
        _shift_copies(buf, shifted)
        _spread_taps(w_ref, taps)

        def chunk(ci, carry):
            r0 = pl.multiple_of(ci * CONV_CHUNK, CONV_CHUNK)
            acc = jnp.broadcast_to(b_ref[...], (CONV_CHUNK, D_MODEL))
            for k in range(CONV_WIDTH):
                acc = acc + _times_tap(taps, k, _shifted_rows(buf, shifted, HALO - (CONV_WIDTH - 1) + k, r0))
            u2_ref[pl.ds(r0, CONV_CHUNK), :] = acc
            return carry

        lax.fori_loop(0, tm // CONV_CHUNK, chunk, 0)
        acc = u2_ref[...]
        mu = jnp.mean(acc, axis=-1, keepdims=True)
        xc = acc - mu
        rstd = lax.rsqrt(jnp.mean(xc * xc, axis=-1, keepdims=True) + NORM_EPS)
        u3 = xc * rstd * g_ref[...] + be_ref[...]
        zv = z_ref[...]
        u5 = u3 * _sigmoid(u3) * (zv * _sigmoid(zv))
        u5_ref[...] = u5.astype(BF16)
        u5t_ref[...] = u5.T.astype(BF16)

    return pl.pallas_call(
        body, name=name, grid=(SEQ // tm,),
        in_specs=[pl.BlockSpec((tm, 2 * D_MODEL), lambda i: (i, 0)),
                  pl.BlockSpec((HALO, 2 * D_MODEL), lambda i: (jnp.maximum(i * hb - 1, 0), 0)),
                  _row_spec(tm, D_MODEL, 2),
                  _vec_spec(CONV_WIDTH, D_MODEL)] + [_vec_spec(1, D_MODEL)] * 3,
        out_specs=[_row_spec(tm, D_MODEL), pl.BlockSpec((D_MODEL, tm), lambda i: (0, i)), _row_spec(tm, D_MODEL)],
        out_shape=[jax.ShapeDtypeStruct((SEQ, D_MODEL), BF16), jax.ShapeDtypeStruct((D_MODEL, SEQ), BF16),
                   jax.ShapeDtypeStruct((SEQ, D_MODEL), F32)],
        scratch_shapes=[pltpu.VMEM((HALO + tm, D_MODEL), F32), pltpu.VMEM((7, HALO + tm - 8, D_MODEL), F32),
                        pltpu.VMEM((CONV_WIDTH, 8, D_MODEL), F32)],
        compiler_params=_params("parallel"),
    )(proj, proj, proj, conv_w, conv_b, ln_g, ln_b)


def _conv_bwd_pointwise(dy, w_out, proj, u2, ln_g, ln_b, name):
    tm = 256

    def body(dy_ref, w_ref, z_ref, u2_ref, g_ref, be_ref, du2_ref, dz_ref, sums_ref):
        u2v = u2_ref[...]
        mu = jnp.mean(u2v, axis=-1, keepdims=True)
        xc = u2v - mu
        rstd = lax.rsqrt(jnp.mean(xc * xc, axis=-1, keepdims=True) + NORM_EPS)
        xhat = xc * rstd
        u3 = xhat * g_ref[...] + be_ref[...]
        s3 = _sigmoid(u3)
        u4 = u3 * s3
        zv = z_ref[...]
        sz = _sigmoid(zv)
        du5v = lax.dot_general(dy_ref[...], w_ref[...], NT_DIMS, preferred_element_type=F32)
        dz_ref[...] = du5v * u4 * (sz * (1.0 + zv * (1.0 - sz)))
        du3 = du5v * (zv * sz) * (s3 * (1.0 + u3 * (1.0 - s3)))
        dxhat = du3 * g_ref[...]
        du2 = rstd * (dxhat - jnp.mean(dxhat, axis=-1, keepdims=True)
                      - xhat * jnp.mean(dxhat * xhat, axis=-1, keepdims=True))
        du2_ref[...] = du2
        sums = jnp.concatenate([
            jnp.sum(du3 * xhat, axis=0, keepdims=True),
            jnp.sum(du3, axis=0, keepdims=True),
            jnp.sum(du2, axis=0, keepdims=True),
            jnp.zeros((5, D_MODEL), F32)], axis=0)

        @pl.when(pl.program_id(0) == 0)
        def _():
            sums_ref[...] = jnp.zeros_like(sums_ref)

        sums_ref[...] += sums

    return pl.pallas_call(
        body, name=name, grid=(SEQ // tm,),
        in_specs=[_row_spec(tm, D_MODEL), _vec_spec(D_MODEL, D_MODEL), _row_spec(tm, D_MODEL, 2),
                  _row_spec(tm, D_MODEL), _vec_spec(1, D_MODEL), _vec_spec(1, D_MODEL)],
        out_specs=[_row_spec(tm, D_MODEL), _row_spec(tm, D_MODEL), _vec_spec(8, D_MODEL)],
        out_shape=[jax.ShapeDtypeStruct((SEQ, D_MODEL), F32), jax.ShapeDtypeStruct((SEQ, D_MODEL), F32),
                   jax.ShapeDtypeStruct((8, D_MODEL), F32)],
        compiler_params=_params("arbitrary"),
    )(dy, w_out, proj, u2, ln_g, ln_b)


def _conv_bwd_taps(du2, dz, proj, conv_w, name):
    tm = 256
    hb = tm // HALO
    n_blocks = SEQ // tm

    def body(du2_ref, dnext_ref, dz_ref, vg_ref, w_ref, dproj_ref, dw_ref, dbuf, dshift, sgbuf, ubuf, dwacc, taps):
        i = pl.program_id(0)
        _spread_taps(w_ref, taps)
        sg = _sigmoid(vg_ref[:, D_MODEL:])
        sgbuf[...] = sg
        ubuf[...] = vg_ref[:, :D_MODEL] * sg
        dbuf[pl.ds(0, tm), :] = du2_ref[...]
        dbuf[pl.ds(tm, HALO), :] = jnp.where(i < n_blocks - 1, dnext_ref[...], 0.0)
        _shift_copies(dbuf, dshift)

        @pl.when(i == 0)
        def _():
            dwacc[...] = jnp.zeros_like(dwacc)

        def chunk(ci, carry):
            r0 = pl.multiple_of(ci * CONV_CHUNK, CONV_CHUNK)
            rows = pl.ds(r0, CONV_CHUNK)
            u1c = ubuf[rows, :]
            du1 = jnp.zeros((CONV_CHUNK, D_MODEL), F32)
            for k in range(CONV_WIDTH):
                ahead = _shifted_rows(dbuf, dshift, CONV_WIDTH - 1 - k, r0)
                du1 = du1 + _times_tap(taps, k, ahead)
                prod = u1c * ahead
                dwacc[k] += prod[0:8] + prod[8:16]
            sgc = sgbuf[rows, :]
            dval = du1 * sgc
            dproj_ref[rows, 0:D_MODEL] = dval.astype(BF16)
            dproj_ref[rows, D_MODEL:2 * D_MODEL] = (dval * vg_ref[rows, 0:D_MODEL] * (1.0 - sgc)).astype(BF16)
            return carry

        lax.fori_loop(0, tm // CONV_CHUNK, chunk, 0)
        dproj_ref[:, 2 * D_MODEL:] = dz_ref[...].astype(BF16)

        @pl.when(i == n_blocks - 1)
        def _():
            for k in range(CONV_WIDTH):
                dw_ref[k:k + 1, :] = jnp.sum(dwacc[k], axis=0, keepdims=True)
            dw_ref[CONV_WIDTH:, :] = jnp.zeros((32 - CONV_WIDTH, D_MODEL), F32)

    return pl.pallas_call(
        body, name=name, grid=(n_blocks,),
        in_specs=[_row_spec(tm, D_MODEL),
                  pl.BlockSpec((HALO, D_MODEL), lambda i: (jnp.minimum((i + 1) * hb, SEQ // HALO - 1), 0)),
                  _row_spec(tm, D_MODEL),
                  pl.BlockSpec((tm, 2 * D_MODEL), lambda i: (i, 0)),
                  _vec_spec(CONV_WIDTH, D_MODEL)],
        out_specs=[_row_spec(tm, 3 * D_MODEL), _vec_spec(32, D_MODEL)],
        out_shape=[jax.ShapeDtypeStruct((SEQ, 3 * D_MODEL), BF16), jax.ShapeDtypeStruct((32, D_MODEL), F32)],
        scratch_shapes=[pltpu.VMEM((tm + HALO, D_MODEL), F32), pltpu.VMEM((7, HALO + tm - 8, D_MODEL), F32),
                        pltpu.VMEM((tm, D_MODEL), F32), pltpu.VMEM((tm, D_MODEL), F32),
                        pltpu.VMEM((CONV_WIDTH, 8, D_MODEL), F32), pltpu.VMEM((CONV_WIDTH, 8, D_MODEL), F32)],
        compiler_params=_params("arbitrary"),
    )(du2, du2, dz, proj, conv_w)


def _out_a(u5, w_out, x, gate, g1, scale1, shift1, name):
    tm = 256
    n_d = len(DILATIONS)

    def body(u_ref, w_ref, x_ref, gate_ref, g_ref, sc_ref, sh_ref, x1_ref, y_ref, ht_ref, *rest):
        h_refs, nat = rest[:n_d], rest[-1]
        y = jnp.dot(u_ref[...], w_ref[...], preferred_element_type=F32)
        x1 = x_ref[...] + gate_ref[...] * y
        y_ref[...] = y
        x1_ref[...] = x1
        h = _normmod(x1, g_ref[...], sc_ref[...], sh_ref[...])
        ht_ref[...] = h.T.astype(BF16)
        for h_ref, d in zip(h_refs, DILATIONS):
            _store_classes(h_ref, h, nat, d)

    res = pl.pallas_call(
        body, name=name, grid=(SEQ // tm,),
        in_specs=[_row_spec(tm, D_MODEL), _vec_spec(D_MODEL, D_MODEL), _row_spec(tm, D_MODEL)]
        + [_vec_spec(1, D_MODEL)] * 4,
        out_specs=[_row_spec(tm, D_MODEL), _row_spec(tm, D_MODEL), pl.BlockSpec((D_MODEL, tm), lambda i: (0, i))]
        + [_class_spec(tm, d) for d in DILATIONS],
        out_shape=[jax.ShapeDtypeStruct((SEQ, D_MODEL), F32), jax.ShapeDtypeStruct((SEQ, D_MODEL), F32),
                   jax.ShapeDtypeStruct((D_MODEL, SEQ), BF16)] + [_class_shape(d, BF16) for d in DILATIONS],
        scratch_shapes=[_natural_scratch(tm)],
        compiler_params=_params("parallel"),
    )(u5, w_out, x, gate, g1, scale1, shift1)
    return res[0], res[1], res[2], [a.reshape(SEQ, D_MODEL) for a in res[3:]]


def _out_b_loss(u, w_out, x1, gate, target, name):
    tm = 256

    def body(u_ref, w_ref, x_ref, gate_ref, t_ref, e_ref, dy_ref, sums_ref):
        y = jnp.dot(u_ref[...], w_ref[...], preferred_element_type=F32)
        diff = x_ref[...] + gate_ref[...] * y - t_ref[...]
        e = diff * (1.0 / D_MODEL)
        e_ref[...] = e
        dy_ref[...] = (e * gate_ref[...]).astype(BF16)
        sums = jnp.concatenate([
            jnp.sum(e * y, axis=0, keepdims=True),
            jnp.sum(diff * diff, axis=0, keepdims=True),
            jnp.zeros((6, D_MODEL), F32)], axis=0)

        @pl.when(pl.program_id(0) == 0)
        def _():
            sums_ref[...] = jnp.zeros_like(sums_ref)

        sums_ref[...] += sums

    return pl.pallas_call(
        body, name=name, grid=(SEQ // tm,),
        in_specs=[_row_spec(tm, D_MODEL), _vec_spec(D_MODEL, D_MODEL), _row_spec(tm, D_MODEL),
                  _vec_spec(1, D_MODEL), _row_spec(tm, D_MODEL)],
        out_specs=[_row_spec(tm, D_MODEL), _row_spec(tm, D_MODEL), _vec_spec(8, D_MODEL)],
        out_shape=[jax.ShapeDtypeStruct((SEQ, D_MODEL), F32), jax.ShapeDtypeStruct((SEQ, D_MODEL), BF16),
                   jax.ShapeDtypeStruct((8, D_MODEL), F32)],
        compiler_params=_params("arbitrary"),
    )(u, w_out, x1, gate, target)


def _seg_matrix():
    r = lax.broadcasted_iota(jnp.int32, (256, 256), 0) // HEAD_DIM
    c = lax.broadcasted_iota(jnp.int32, (256, 256), 1) // HEAD_DIM
    return (r == c).astype(BF16)


def _segsum(v, seg):
    hi = v.astype(BF16)
    lo = (v - hi.astype(F32)).astype(BF16)
    outs = []
    for c0 in range(0, D_MODEL, 256):
        outs.append(jnp.dot(hi[:, c0:c0 + 256], seg, preferred_element_type=F32)
                    + jnp.dot(lo[:, c0:c0 + 256], seg, preferred_element_type=F32))
    return jnp.concatenate(outs, axis=1)


def _qk_rstd(v, seg):
    return lax.rsqrt(_segsum(v * v, seg) * (1.0 / HEAD_DIM) + NORM_EPS)


def _qknorm_fwd(proj, group, qw, kw, seg, name):
    tm = 256

    def body(q_in, k_in, qw_ref, kw_ref, seg_ref, q_ref, k_ref):
        segv = seg_ref[...]
        q = q_in[...].astype(F32)
        k = k_in[...].astype(F32)
        q_ref[...] = (q * _qk_rstd(q, segv) * qw_ref[...] * HEAD_DIM ** -0.5).astype(BF16)
        k_ref[...] = (k * _qk_rstd(k, segv) * kw_ref[...]).astype(BF16)

    return pl.pallas_call(
        body, name=name, grid=(SEQ // tm,),
        in_specs=[_row_spec(tm, D_MODEL, 3 * group), _row_spec(tm, D_MODEL, 3 * group + 1),
                  _vec_spec(1, D_MODEL), _vec_spec(1, D_MODEL), _vec_spec(256, 256)],
        out_specs=[_row_spec(tm, D_MODEL)] * 2,
        out_shape=[jax.ShapeDtypeStruct((SEQ, D_MODEL), BF16)] * 2,
        compiler_params=_params("parallel"),
    )(proj, proj, qw, kw, seg)


def _attn_masks(b, bpc, dilation, transposed=False):
    keys = ATTN_BLOCK if bpc == 1 else 2 * ATTN_BLOCK
    shape, q_axis = ((keys, ATTN_BLOCK), 1) if transposed else ((ATTN_BLOCK, keys), 0)
    qi = lax.broadcasted_iota(jnp.int32, shape, q_axis)
    kj = lax.broadcasted_iota(jnp.int32, shape, 1 - q_axis)
    if bpc == 1:
        steps = qi - kj
        return (steps * dilation).astype(F32), steps >= 0
    steps = qi + ATTN_BLOCK - kj
    has_prev = (b % bpc) != 0
    valid = (steps >= 0) & (steps <= ATTN_BLOCK) & (has_prev | (kj >= ATTN_BLOCK))
    return (steps * dilation).astype(F32), valid


MASKED = 1e30


def _bias_scratch(bpc):
    return pltpu.VMEM((1 if bpc == 1 else 2, N_HEADS, ATTN_BLOCK, (1 if bpc == 1 else 2) * ATTN_BLOCK), F32)


def _fill_bias(bias_ref, sl_ref, bpc, dilation):
    for variant in range(bias_ref.shape[0]):
        dist, valid = _attn_masks(variant, min(bpc, 2), dilation)
        bias_ref[variant] = jnp.where(valid[None], dist[None] * sl_ref[...], MASKED)


def _step_bias(bias_ref, b, bpc):
    if bpc == 1:
        return bias_ref[0]
    return bias_ref[jnp.where((b % bpc) != 0, 1, 0)]


def _key_tile(prev_ref, cur_ref, cols, bpc):
    if bpc == 1:
        return cur_ref[:, cols]
    return jnp.concatenate([prev_ref[:, cols], cur_ref[:, cols]], axis=0)


ATTN_HEADS_FWD = 16
ATTN_HEADS_BWD = 16
NT_DIMS = (((1,), (1,)), ((), ()))
BATCH_NT_DIMS = (((2,), (2,)), ((0,), (0,)))
BATCH_NN_DIMS = (((2,), (1,)), ((0,), (0,)))
BATCH_TN_DIMS = (((1,), (1,)), ((0,), (0,)))


def _head_stack(tile_of, heads):
    return jnp.stack([tile_of(slice(h * HEAD_DIM, (h + 1) * HEAD_DIM)) for h in range(heads)], axis=0)


def _attn_specs(heads, segment=0):
    width = heads * HEAD_DIM
    off = segment * (D_MODEL // width)
    last = SEQ // ATTN_BLOCK - 1
    cur = pl.BlockSpec((ATTN_BLOCK, width), lambda hg, b: (jnp.minimum(b, last), hg + off))
    prev = pl.BlockSpec((ATTN_BLOCK, width), lambda hg, b: (jnp.clip(b - 1, 0, last), hg + off))
    return cur, prev


def _attn_fwd(q, k, proj, group, slopes, dilation, name):
    bpc = SEQ // dilation // ATTN_BLOCK
    heads = ATTN_HEADS_FWD
    assert heads == N_HEADS
    cur, prev = _attn_specs(heads)
    v_cur, v_prev = _attn_specs(heads, segment=3 * group + 2)

    def body(sl_ref, q_ref, kp_ref, kc_ref, vp_ref, vc_ref, o_ref, lse_ref, bias_ref):
        b = pl.program_id(1)

        @pl.when(b == 0)
        def _():
            _fill_bias(bias_ref, sl_ref, bpc, dilation)

        q3 = _head_stack(lambda cols: q_ref[:, cols], heads)
        k3 = _head_stack(lambda cols: _key_tile(kp_ref, kc_ref, cols, bpc), heads)
        v3 = _head_stack(lambda cols: _key_tile(vp_ref, vc_ref, cols, bpc), heads)
        s = lax.dot_general(q3, k3, BATCH_NT_DIMS, preferred_element_type=F32)
        s = s - _step_bias(bias_ref, b, bpc)
        m = jnp.max(s, axis=-1, keepdims=True)
        p = jnp.exp(s - m)
        l = jnp.sum(p, axis=-1, keepdims=True)
        o3 = lax.dot_general(p.astype(BF16), v3, BATCH_NN_DIMS, preferred_element_type=F32) / l
        lse3 = m + jnp.log(l)
        for h in range(heads):
            o_ref[:, h * HEAD_DIM:(h + 1) * HEAD_DIM] = o3[h]
        lse_ref[...] = jnp.concatenate([lse3[h] for h in range(heads)]
                                       + [jnp.zeros((ATTN_BLOCK, LANES - heads), F32)], axis=1)

    return pl.pallas_call(
        body, name=name, grid=(N_HEADS // heads, SEQ // ATTN_BLOCK),
        in_specs=[pl.BlockSpec((heads, 1, 1), lambda hg, b: (hg, 0, 0)), cur, prev, cur, v_prev, v_cur],
        out_specs=[cur, pl.BlockSpec((ATTN_BLOCK, LANES), lambda hg, b: (b, 0))],
        out_shape=[jax.ShapeDtypeStruct((SEQ, D_MODEL), F32), jax.ShapeDtypeStruct((SEQ, LANES), F32)],
        scratch_shapes=[_bias_scratch(bpc)],
        compiler_params=_params("parallel", "arbitrary"),
    )(slopes.reshape(N_HEADS, 1, 1), q, k, k, proj, proj)


def _class_spec(tm, dilation, width=D_MODEL):
    if dilation == 1:
        return _row_spec(tm, width)
    return pl.BlockSpec((dilation, tm // dilation, width), lambda i: (0, i, 0))


def _class_shape(dilation, dtype, width=D_MODEL):
    if dilation == 1:
        return jax.ShapeDtypeStruct((SEQ, width), dtype)
    return jax.ShapeDtypeStruct((dilation, SEQ // dilation, width), dtype)


def _load_natural(in_ref, nat_ref, dilation):
    if dilation == 1:
        return in_ref[...].astype(F32)
    n = nat_ref.shape[1] // dilation
    tiles = in_ref.shape[-1] // LANES
    for r in range(dilation):
        for j in range(tiles):
            nat_ref.at[j][pl.ds(r, n, stride=dilation), :] = in_ref[r, :, j * LANES:(j + 1) * LANES].astype(F32)
    if tiles == 1:
        return nat_ref[0]
    return jnp.concatenate([nat_ref[j] for j in range(tiles)], axis=1)


def _store_classes(out_ref, value, nat_ref, dilation):
    if dilation == 1:
        out_ref[...] = value.astype(out_ref.dtype)
        return
    n = nat_ref.shape[1] // dilation
    tiles = value.shape[-1] // LANES
    for j in range(tiles):
        nat_ref[j] = value[:, j * LANES:(j + 1) * LANES]
    for r in range(dilation):
        for j in range(tiles):
            out_ref[r, :, j * LANES:(j + 1) * LANES] = (
                nat_ref.at[j][pl.ds(r, n, stride=dilation), :].astype(out_ref.dtype))


def _natural_scratch(tm):
    return pltpu.VMEM((D_MODEL // LANES, tm, LANES), F32)


def _head_selector():
    lane_head = lax.broadcasted_iota(jnp.int32, (D_MODEL, LANES), 0) // HEAD_DIM
    head = lax.broadcasted_iota(jnp.int32, (D_MODEL, LANES), 1)
    return (lane_head == head).astype(BF16)


def _dot_split(v, m01, dims):
    hi = v.astype(BF16)
    lo = (v - hi.astype(F32)).astype(BF16)
    return (lax.dot_general(hi, m01, dims, preferred_element_type=F32)
            + lax.dot_general(lo, m01, dims, preferred_element_type=F32))


def _merge_fwd(o_parts, lse_parts, z, sel, name):
    tm = 256
    h_spec = pl.BlockSpec((tm, LANES), lambda i: (i, 0))

    def body(o0, o1, o2, l0, l1, l2, z_ref, sel_ref, u_ref, ut_ref, o_ref, lse_ref, nat):
        ls = [_load_natural(l, nat, d) for l, d in zip((l0, l1, l2), DILATIONS)]
        m = jnp.maximum(jnp.maximum(ls[0], ls[1]), ls[2])
        tot = m + jnp.log(jnp.exp(ls[0] - m) + jnp.exp(ls[1] - m) + jnp.exp(ls[2] - m))
        o = jnp.zeros((tm, D_MODEL), F32)
        for o_in, l, d in zip((o0, o1, o2), ls, DILATIONS):
            weight = _dot_split(jnp.exp(l - tot), sel_ref[...], NT_DIMS)
            o = o + weight * _load_natural(o_in, nat, d)
        zv = z_ref[...].astype(F32)
        u = o * (zv * _sigmoid(zv))
        u_ref[...] = u.astype(BF16)
        ut_ref[...] = u.T.astype(BF16)
        o_ref[...] = o
        lse_ref[...] = tot

    return pl.pallas_call(
        body, name=name, grid=(SEQ // tm,),
        in_specs=[_class_spec(tm, d) for d in DILATIONS] + [_class_spec(tm, d, LANES) for d in DILATIONS]
        + [_row_spec(tm, D_MODEL, B_Z_SEGMENT), _vec_spec(D_MODEL, LANES)],
        out_specs=[_row_spec(tm, D_MODEL), pl.BlockSpec((D_MODEL, tm), lambda i: (0, i)),
                   _row_spec(tm, D_MODEL), h_spec],
        out_shape=[jax.ShapeDtypeStruct((SEQ, D_MODEL), BF16), jax.ShapeDtypeStruct((D_MODEL, SEQ), BF16),
                   jax.ShapeDtypeStruct((SEQ, D_MODEL), F32), jax.ShapeDtypeStruct((SEQ, LANES), F32)],
        scratch_shapes=[_natural_scratch(tm)],
        compiler_params=_params("parallel"),
    )(*o_parts, *lse_parts, z, sel)


def _merge_bwd(dy, w_out, o, lse, z, sel, name):
    tm = 256
    n_d = len(DILATIONS)

    def body(dy_ref, w_ref, o_ref, lse_ref, z_ref, sel_ref, dz_ref, *rest):
        do_refs, delta_refs, lse_refs, nat = rest[:n_d], rest[n_d:2 * n_d], rest[2 * n_d:3 * n_d], rest[-1]
        zv = z_ref[...].astype(F32)
        sz = _sigmoid(zv)
        duv = lax.dot_general(dy_ref[...], w_ref[...], NT_DIMS, preferred_element_type=F32)
        ov = o_ref[...]
        do = duv * (zv * sz)
        dz_ref[...] = (duv * ov * (sz * (1.0 + zv * (1.0 - sz)))).astype(BF16)
        delta = _dot_split(do * ov, sel_ref[...], (((1,), (0,)), ((), ())))
        lv = lse_ref[...]
        for i, d in enumerate(DILATIONS):
            _store_classes(do_refs[i], do, nat, d)
            _store_classes(delta_refs[i], delta, nat, d)
            _store_classes(lse_refs[i], lv, nat, d)

    res = pl.pallas_call(
        body, name=name, grid=(SEQ // tm,),
        in_specs=[_row_spec(tm, D_MODEL), _vec_spec(D_MODEL, D_MODEL), _row_spec(tm, D_MODEL), _row_spec(tm, LANES),
                  _row_spec(tm, D_MODEL, B_Z_SEGMENT), _vec_spec(D_MODEL, LANES)],
        out_specs=[_row_spec(tm, D_MODEL)] + [_class_spec(tm, d) for d in DILATIONS]
        + [_class_spec(tm, d, LANES) for d in DILATIONS] * 2,
        out_shape=[jax.ShapeDtypeStruct((SEQ, D_MODEL), BF16)] + [_class_shape(d, BF16) for d in DILATIONS]
        + [_class_shape(d, F32, LANES) for d in DILATIONS] * 2,
        scratch_shapes=[_natural_scratch(tm)],
        compiler_params=_params("parallel"),
    )(dy, w_out, o, lse, z, sel)
    flat = lambda a: a.reshape(SEQ, a.shape[-1])
    return (res[0], [flat(a) for a in res[1:1 + n_d]], [flat(a) for a in res[1 + n_d:1 + 2 * n_d]],
            [flat(a) for a in res[1 + 2 * n_d:]])


def _attn_bwd(q, k, proj, group, do, lse, delta, slopes, dilation, name):
    bpc = SEQ // dilation // ATTN_BLOCK
    heads = ATTN_HEADS_BWD
    n_blocks = SEQ // ATTN_BLOCK
    carry = bpc > 1
    width = heads * HEAD_DIM
    cur, prev = _attn_specs(heads)
    v_cur, v_prev = _attn_specs(heads, segment=3 * group + 2)
    assert heads == N_HEADS
    per_head = pl.BlockSpec((ATTN_BLOCK, LANES), lambda hg, b: (jnp.minimum(b, n_blocks - 1), 0))
    scale = HEAD_DIM ** -0.5

    def body(sl_ref, q_ref, kp_ref, kc_ref, vp_ref, vc_ref, do_ref, lse_ref, dl_ref,
             dq_ref, dk_ref, dv_ref, *scratch):
        b = pl.program_id(1)
        if carry:
            dk_carry, dv_carry = scratch

            @pl.when(b == n_blocks)
            def _():
                dk_ref[...] = dk_carry[...].astype(BF16)
                dv_ref[...] = dv_carry[...].astype(BF16)

            @pl.when(b < n_blocks)
            def _():
                step(sl_ref, q_ref, kp_ref, kc_ref, vp_ref, vc_ref, do_ref, lse_ref, dl_ref,
                     dq_ref, dk_ref, dv_ref, dk_carry, dv_carry, b)
        else:
            step(sl_ref, q_ref, kp_ref, kc_ref, vp_ref, vc_ref, do_ref, lse_ref, dl_ref,
                 dq_ref, dk_ref, dv_ref, None, None, b)

    def step(sl_ref, q_ref, kp_ref, kc_ref, vp_ref, vc_ref, do_ref, lse_ref, dl_ref,
             dq_ref, dk_ref, dv_ref, dk_carry, dv_carry, b):
        if carry:
            @pl.when(b == 0)
            def _():
                dk_carry[...] = jnp.zeros_like(dk_carry)
                dv_carry[...] = jnp.zeros_like(dv_carry)

        q3 = _head_stack(lambda cols: q_ref[:, cols], heads)
        k3 = _head_stack(lambda cols: _key_tile(kp_ref, kc_ref, cols, bpc), heads)
        v3 = _head_stack(lambda cols: _key_tile(vp_ref, vc_ref, cols, bpc), heads)
        do3 = _head_stack(lambda cols: do_ref[:, cols], heads)
        lse_t = lse_ref[...].T
        dl_t = dl_ref[...].T
        lse3 = jnp.stack([lse_t[h:h + 1, :] for h in range(heads)], axis=0)
        dl3 = jnp.stack([dl_t[h:h + 1, :] for h in range(heads)], axis=0)
        s = lax.dot_general(k3, q3, BATCH_NT_DIMS, preferred_element_type=F32)
        dist, valid = _attn_masks(b, bpc, dilation, transposed=True)
        p = jnp.exp(jnp.where(valid[None], s - dist[None] * sl_ref[...], NEG_INF) - lse3)
        dp = lax.dot_general(v3, do3, BATCH_NT_DIMS, preferred_element_type=F32)
        ds = (p * (dp - dl3)).astype(BF16)
        dq3 = lax.dot_general(ds, k3, BATCH_TN_DIMS, preferred_element_type=F32) * scale
        dk3 = lax.dot_general(ds, q3, BATCH_NN_DIMS, preferred_element_type=F32)
        dv3 = lax.dot_general(p.astype(BF16), do3, BATCH_NN_DIMS, preferred_element_type=F32)
        for h in range(heads):
            cols = slice(h * HEAD_DIM, (h + 1) * HEAD_DIM)
            dq_ref[:, cols] = dq3[h].astype(BF16)
            if carry:
                dk_ref[:, cols] = (dk_carry[:, cols] + dk3[h, :ATTN_BLOCK]).astype(BF16)
                dv_ref[:, cols] = (dv_carry[:, cols] + dv3[h, :ATTN_BLOCK]).astype(BF16)
                dk_carry[:, cols] = dk3[h, ATTN_BLOCK:]
                dv_carry[:, cols] = dv3[h, ATTN_BLOCK:]
            else:
                dk_ref[:, cols] = dk3[h].astype(BF16)
                dv_ref[:, cols] = dv3[h].astype(BF16)

    kv_out = prev if carry else cur
    return pl.pallas_call(
        body, name=name, grid=(N_HEADS // heads, n_blocks + (1 if carry else 0)),
        in_specs=[pl.BlockSpec((heads, 1, 1), lambda hg, b: (hg, 0, 0)), cur, prev, cur, v_prev, v_cur,
                  cur, per_head, per_head],
        out_specs=[cur, kv_out, kv_out],
        out_shape=[jax.ShapeDtypeStruct((SEQ, D_MODEL), BF16)] * 3,
        scratch_shapes=[pltpu.VMEM((ATTN_BLOCK, width), F32)] * 2 if carry else [],
        compiler_params=_params("parallel", "arbitrary"),
    )(slopes.reshape(N_HEADS, 1, 1), q, k, k, proj, proj, do, lse, delta)


def _qknorm_bwd(proj, group, qw, kw, seg, dq, dk, dv, name):
    tm = 256

    def body(q_in, k_in, qw_ref, kw_ref, seg_ref, dq_ref, dk_ref, dv_ref, dproj_ref, sums_ref):
        segv = seg_ref[...]
        sums = []
        for part, (raw_ref, w_ref, dn_ref) in enumerate(((q_in, qw_ref, dq_ref), (k_in, kw_ref, dk_ref))):
            raw = raw_ref[...].astype(F32)
            dn = dn_ref[...].astype(F32)
            r = _qk_rstd(raw, segv)
            gq = dn * w_ref[...]
            draw = r * gq - raw * (r * r * r) * (_segsum(raw * gq, segv) * (1.0 / HEAD_DIM))
            dproj_ref[:, part * D_MODEL:(part + 1) * D_MODEL] = draw.astype(BF16)
            sums.append(jnp.sum(dn * raw * r, axis=0, keepdims=True))
        dproj_ref[:, 2 * D_MODEL:] = dv_ref[...]

        @pl.when(pl.program_id(0) == 0)
        def _():
            sums_ref[...] = jnp.zeros_like(sums_ref)

        sums_ref[...] += jnp.concatenate(sums + [jnp.zeros((6, D_MODEL), F32)], axis=0)

    return pl.pallas_call(
        body, name=name, grid=(SEQ // tm,),
        in_specs=[_row_spec(tm, D_MODEL, 3 * group), _row_spec(tm, D_MODEL, 3 * group + 1),
                  _vec_spec(1, D_MODEL), _vec_spec(1, D_MODEL), _vec_spec(256, 256)] + [_row_spec(tm, D_MODEL)] * 3,
        out_specs=[_row_spec(tm, 3 * D_MODEL), _vec_spec(8, D_MODEL)],
        out_shape=[jax.ShapeDtypeStruct((SEQ, 3 * D_MODEL), BF16), jax.ShapeDtypeStruct((8, D_MODEL), F32)],
        compiler_params=_params("arbitrary"),
    )(proj, proj, qw, kw, seg, dq, dk, dv)


B_TN = 512
B_GROUP_TILES = 3 * D_MODEL // B_TN
B_Z_TILE0 = 3 * B_GROUP_TILES
B_Z_TILES = D_MODEL // B_TN
B_TILES = B_Z_TILE0 + B_Z_TILES
B_Z_SEGMENT = 3 * len(DILATIONS)


def _local_step(x, target, mods, norm_g, conv_w, conv_b, ln_g, ln_b, q_norm, k_norm, chip, own_wb_in,
                weights_a, weights_b, forward_weights_b, send_grads_b, forward_grads_b, send_grads_a):
    row = lambda a, i: a[i:i + 1]
    shift0, scale0, gate0 = row(mods[0], 0), row(mods[0], 1), row(mods[0], 2)
    shift1, scale1, gate1 = row(mods[1], 0), row(mods[1], 1), row(mods[1], 2)
    g0, g1 = row(norm_g, 0), row(norm_g, 1)
    seg = _seg_matrix()
    slopes = jnp.exp2(-8.0 * jnp.arange(1, N_HEADS + 1, dtype=F32) / N_HEADS)
    qw = [jnp.tile(q_norm[g:g + 1], (1, N_HEADS)) for g in range(3)]
    kw = [jnp.tile(k_norm[g:g + 1], (1, N_HEADS)) for g in range(3)]

    h0, h0t = _normmod_fwd(x, g0, scale0, shift0, "prenorm0")
    wa_in, wa_out = weights_a(h0)
    ja, _, nsa = wa_in.shape
    proj_a = _mm(h0, wa_in, tn=nsa, tile0=0, n_tiles=ja, out_dtype=F32, name="a_in")
    u5, u5t, u2 = _conv_fwd(proj_a, conv_w, conv_b, ln_g, ln_b, "a_conv")
    x1, y_a, h1t, h1c = _out_a(u5, wa_out, x, gate0, g1, scale1, shift1, "a_out")

    own_tiles = B_TILES // N_CHIPS
    step = jnp.arange(B_TILES, dtype=jnp.int32)
    tiles = (own_tiles * chip + step) % B_TILES
    own_ids = jnp.stack([step[:own_tiles], tiles[:own_tiles]])
    rest_ids = jnp.stack([tiles[own_tiles:], tiles[own_tiles:]])
    proj_b = _b_in_tiles(h1c, own_wb_in, own_ids, own_tiles, "b_in_own")
    forward_weights_b(proj_b)
    wb_in, wb_out = weights_b(proj_b)
    jb, _, nsb = wb_in.shape
    proj_b = _b_in_tiles(h1c, wb_in, rest_ids, B_TILES - own_tiles, "b_in_rest", prev=proj_b)
    h1 = h1c[0]
    qkv, o_parts, lse_parts = [], [], []
    for g, d in enumerate(DILATIONS):
        qn, kn = _qknorm_fwd(proj_b, g, qw[g], kw[g], seg, f"b_qknorm_g{g}")
        og, lg = _attn_fwd(qn, kn, proj_b, g, slopes, d, f"b_attn_g{g}")
        qkv.append((qn, kn))
        o_parts.append(og if d == 1 else og.reshape(d, SEQ // d, D_MODEL))
        lse_parts.append(lg if d == 1 else lg.reshape(d, SEQ // d, LANES))
    sel = _head_selector()
    u_b, u_bt, o_b, lse_b = _merge_fwd(o_parts, lse_parts, proj_b, sel, "b_merge")
    e, dy_b, sums_loss = _out_b_loss(u_b, wb_out, x1, gate1, target, "b_out_loss")

    dwb_out = _mm(u_bt, dy_b, tn=D_MODEL, tile0=0, n_tiles=1, out_dtype=BF16, name="b_dwout")
    dz_b, do_c, delta_c, lse_c = _merge_bwd(dy_b, wb_out, o_b, lse_b, proj_b, sel, "b_merge_bwd")
    dwb_in = _mm(h1t, dz_b, tn=B_TN, tile0=B_Z_TILE0, n_tiles=B_Z_TILES, out_dtype=BF16, name="b_dwin_z",
                 out3d=(jb, nsb))
    dh1_parts = [_mm_nt(dz_b, wb_in, tn=B_TN, tile0=B_Z_TILE0, n_tiles=B_Z_TILES, name="b_dh_z")]
    qk_sums = []
    for g, d in enumerate(DILATIONS):
        qn, kn = qkv[g]
        dq, dk, dv = _attn_bwd(qn, kn, proj_b, g, do_c[g], lse_c[g], delta_c[g], slopes, d, f"b_attn_bwd_g{g}")
        dproj, sums_qk = _qknorm_bwd(proj_b, g, qw[g], kw[g], seg, dq, dk, dv, f"b_qknorm_bwd_g{g}")
        qk_sums.append(sums_qk)
        dwb_in = _mm(h1t if d == 1 else h1c[g], dproj, tn=B_TN, tile0=g * B_GROUP_TILES, n_tiles=B_GROUP_TILES,
                     out_dtype=BF16, name=f"b_dwin_g{g}", out3d=(jb, nsb), prev=dwb_in, transpose_lhs=d != 1)
        dh = _mm_nt(dproj, wb_in, tn=B_TN, tile0=g * B_GROUP_TILES, n_tiles=B_GROUP_TILES, name=f"b_dh_g{g}")
        dh1_parts.append(dh)
    token = send_grads_b(dwb_in, dwb_out)
    dx1, sums_n1, dy_a = _normmod_bwd(x1, g1, scale1 + token[0:1, 0:1], dh1_parts, e, "prenorm1_bwd",
                                      part_dilations=(1,) + DILATIONS, gated=(gate0, y_a))
    token = forward_grads_b(dx1)

    dwa_out = _mm(u5t, dy_a, tn=D_MODEL, tile0=0, n_tiles=1, out_dtype=BF16, name="a_dwout")
    du2, dz_a, sums_ln = _conv_bwd_pointwise(dy_a, wa_out, proj_a, u2, ln_g + token[0:1, 0:1], ln_b,
                                             "a_conv_bwd_pw")
    dproj_a, dconv_w = _conv_bwd_taps(du2, dz_a, proj_a, conv_w, "a_conv_bwd_taps")
    dwa_in = _mm(h0t, dproj_a, tn=nsa, tile0=0, n_tiles=ja, out_dtype=BF16, name="a_dwin", out3d=(ja, nsa))
    token = send_grads_a(dwa_in, dwa_out)
    dh0 = _mm_nt(dproj_a, wa_in, tn=nsa, tile0=0, n_tiles=ja, name="a_dh", after=token)
    grad_x, sums_n0 = _normmod_bwd(x, g0, scale0, [dh0], dx1, "prenorm0_bwd")

    small = dict(
        dnorm_g=jnp.concatenate([sums_n0[0:1], sums_n1[0:1]], axis=0),
        dmod0=jnp.concatenate([sums_n0[2:3], sums_n0[1:2], sums_n1[3:4]], axis=0),
        dmod1=jnp.concatenate([sums_n1[2:3], sums_n1[1:2], sums_loss[0:1]], axis=0),
        dln_g=sums_ln[0:1], dln_b=sums_ln[1:2], dconv_b=sums_ln[2:3],
        dconv_w=dconv_w[:CONV_WIDTH],
        dq_norm=jnp.concatenate([s[0:1] for s in qk_sums], axis=0),
        dk_norm=jnp.concatenate([s[1:2] for s in qk_sums], axis=0),
        loss_cols=sums_loss[1:2],
    )
    return grad_x, small


def _adamw(w, g, m, v, name, after=None, copy_grad=False):
    rows, cols = w.shape
    tr = rows if rows <= 128 else 128
    c1 = 1.0 / (1.0 - ADAM_B1 ** ADAM_STEP)
    c2 = 1.0 / (1.0 - ADAM_B2 ** ADAM_STEP)
    extra = [] if after is None else [after]
    n_out = 4 if copy_grad else 3

    def body(w_ref, g_ref, m_ref, v_ref, *rest):
        d_ref, mo_ref, vo_ref = rest[len(extra):len(extra) + 3]
        gv = g_ref[...]
        if copy_grad:
            rest[-1][...] = gv
        mn = ADAM_B1 * m_ref[...] + (1.0 - ADAM_B1) * gv
        vn = ADAM_B2 * v_ref[...] + (1.0 - ADAM_B2) * (gv * gv)
        mo_ref[...] = mn
        vo_ref[...] = vn
        d_ref[...] = -ADAM_LR * ((mn * c1) / (jnp.sqrt(vn * c2) + ADAM_EPS) + ADAM_WD * w_ref[...])

    spec = pl.BlockSpec((tr, cols), lambda i: (i, 0))
    return pl.pallas_call(
        body, name=name, grid=(rows // tr,),
        in_specs=[spec] * 4 + [pl.BlockSpec(memory_space=pl.ANY)] * len(extra), out_specs=[spec] * n_out,
        out_shape=[jax.ShapeDtypeStruct((rows, cols), F32)] * n_out,
        compiler_params=_params("parallel"),
    )(w, g, m, v, *extra)


def _cast_into_slot(w, chip_idx, name, keep_own=False, after=None):
    rows, cols = w.shape
    tr = 256
    extra = [] if after is None else [after]

    def body(ch_ref, w_ref, *rest):
        wb = w_ref[...].astype(BF16)
        for o_ref in rest[len(extra):]:
            o_ref[...] = wb

    slot_spec = pl.BlockSpec((None, tr, cols), lambda i, ch: (ch[0], i, 0))
    own_spec = pl.BlockSpec((None, tr, cols), lambda i, ch: (0, i, 0))
    res = pl.pallas_call(
        body, name=name,
        grid_spec=pltpu.PrefetchScalarGridSpec(
            num_scalar_prefetch=1, grid=(rows // tr,),
            in_specs=[pl.BlockSpec((tr, cols), lambda i, ch: (i, 0))] + [pl.BlockSpec(memory_space=pl.ANY)] * len(extra),
            out_specs=[slot_spec, own_spec] if keep_own else [slot_spec]),
        out_shape=[jax.ShapeDtypeStruct((N_CHIPS, rows, cols), BF16)]
        + ([jax.ShapeDtypeStruct((1, rows, cols), BF16)] if keep_own else []),
        compiler_params=_params("parallel"),
    )(chip_idx, w, *extra)
    return tuple(res) if keep_own else res[0]


def _position():
    x, y, c = lax.axis_index("x"), lax.axis_index("y"), lax.axis_index("c")
    return x, y, c


def _xor_peer(x, y, c, k):
    return (x ^ ((k >> 2) & 1), y ^ ((k >> 1) & 1), c ^ (k & 1))


def _chip_peer(x, y, k):
    return (x ^ ((k >> 1) & 1), y ^ (k & 1))


def _ada_forward(c_row, ada_w, ada_b, conv_w):
    ns = ada_w.shape[2]
    cw = conv_w.shape[1]

    def body(c_ref, w_ref, b_ref, cv_ref, mod_ref, sc_ref, cvo_ref,
             c_all, mp, parts, cv_parts, send1, recv1, send2, recv2, send3, recv3):
        x, y, c = _position()
        me = 4 * x + 2 * y + c
        chip = 2 * x + y

        def c_copy(k):
            return pltpu.make_async_remote_copy(
                src_ref=c_all.at[me], dst_ref=c_all.at[me], send_sem=send1.at[k - 1], recv_sem=recv1.at[k - 1],
                device_id=_xor_peer(x, y, c, k), device_id_type=MESH)

        def cv_copy(k):
            px, py = _chip_peer(x, y, k)
            return pltpu.make_async_remote_copy(
                src_ref=cv_parts.at[chip], dst_ref=cv_parts.at[chip], send_sem=send3.at[k - 1],
                recv_sem=recv3.at[k - 1], device_id=(px, py, c), device_id_type=MESH)

        c_all[me] = c_ref[...]
        cv_parts[chip] = cv_ref[...]
        for k in range(1, N_DEV):
            c_copy(k).start()
        for k in range(1, N_CHIPS):
            cv_copy(k).start()
        for k in range(1, N_DEV):
            c_copy(k).wait_recv()
        cv = jnp.concatenate([c_all[i] for i in range(N_DEV)], axis=0)
        sc = cv * _sigmoid(cv)
        sc_ref[...] = sc
        for l in range(2):
            res = jnp.dot(sc, w_ref[l], preferred_element_type=F32, precision=lax.Precision.HIGHEST)
            for i in range(N_DEV):
                mp[i, l:l + 1, :] = res[i:i + 1, :]

        def mod_copy(k):
            px, py = _chip_peer(x, y, k)
            return pltpu.make_async_remote_copy(
                src_ref=mp.at[4 * px + 2 * py + c], dst_ref=parts.at[chip], send_sem=send2.at[k - 1],
                recv_sem=recv2.at[k - 1], device_id=(px, py, c), device_id_type=MESH)

        for k in range(1, N_CHIPS):
            mod_copy(k).start()
        parts[chip] = mp[me]
        for k in range(1, N_CHIPS):
            mod_copy(k).wait_recv()
            cv_copy(k).wait_recv()
        mod_ref[...] = jnp.concatenate([parts[j] for j in range(N_CHIPS)], axis=1) + b_ref[...]
        cvo_ref[...] = jnp.concatenate([cv_parts[j] for j in range(N_CHIPS)], axis=1)
        for k in range(1, N_DEV):
            c_copy(k).wait_send()
        for k in range(1, N_CHIPS):
            mod_copy(k).wait_send()
            cv_copy(k).wait_send()

    vm = pl.BlockSpec(memory_space=pltpu.VMEM)
    return pl.pallas_call(
        body, name="ada_forward",
        in_specs=[vm] * 4, out_specs=[vm] * 3,
        out_shape=[jax.ShapeDtypeStruct((2, 3 * D_MODEL), F32), jax.ShapeDtypeStruct((N_DEV, D_MODEL), F32),
                   jax.ShapeDtypeStruct((CONV_WIDTH, N_CHIPS * cw), F32)],
        scratch_shapes=[pltpu.VMEM((N_DEV, 1, D_MODEL), F32), pltpu.VMEM((N_DEV, 2, ns), F32),
                        pltpu.VMEM((N_CHIPS, 2, ns), F32), pltpu.VMEM((N_CHIPS, CONV_WIDTH, cw), F32),
                        pltpu.SemaphoreType.DMA((N_DEV - 1,)), pltpu.SemaphoreType.DMA((N_DEV - 1,)),
                        pltpu.SemaphoreType.DMA((N_CHIPS - 1,)), pltpu.SemaphoreType.DMA((N_CHIPS - 1,)),
                        pltpu.SemaphoreType.DMA((N_CHIPS - 1,)), pltpu.SemaphoreType.DMA((N_CHIPS - 1,))],
        compiler_params=pltpu.CompilerParams(vmem_limit_bytes=VMEM_LIMIT_BYTES),
    )(c_row, ada_w, ada_b, conv_w)


HBM_SPEC = pl.BlockSpec(memory_space=pltpu.HBM)
ANY_SPEC = pl.BlockSpec(memory_space=pl.ANY)
SEM_SPEC = pl.BlockSpec(memory_space=pltpu.SEMAPHORE)
SPLIT_PARAMS = dict(compiler_params=pltpu.CompilerParams(has_side_effects=pltpu.SideEffectType.DATAFLOW_SIDE_EFFECTING))
TOKEN = jax.ShapeDtypeStruct((8, 128), F32)


def _hbm(arrays):
    return [pltpu.with_memory_space_constraint(a, pltpu.HBM) for a in arrays]


def _hbm_like(arrays):
    return [pltpu.HBM(a.shape, a.dtype) for a in arrays]


def _gather_start(lands, after, name):
    n = len(lands)

    def body(*refs):
        ins = refs[:n]
        send, recv = refs[n + 1], refs[n + 2]
        x, y, c = _position()
        chip = 2 * x + y
        for t in range(n):
            rh = ins[t].shape[1] // 2
            for k in range(1, N_CHIPS):
                px, py = _chip_peer(x, y, k)
                block = ins[t].at[chip, pl.ds(c * rh, rh)]
                pltpu.make_async_remote_copy(
                    src_ref=block, dst_ref=block, send_sem=send.at[3 * t + k - 1], recv_sem=recv.at[3 * t + k - 1],
                    device_id=(px, py, c), device_id_type=MESH).start()
        refs[-1][...] = jnp.zeros(TOKEN.shape, F32)

    res = pl.pallas_call(
        body, name=name, in_specs=[HBM_SPEC] * n + [ANY_SPEC],
        out_specs=(SEM_SPEC, SEM_SPEC, *[HBM_SPEC] * n, pl.BlockSpec(memory_space=pltpu.VMEM)),
        out_shape=(pltpu.SemaphoreType.DMA((3 * n,)), pltpu.SemaphoreType.DMA((3 * n,)), *_hbm_like(lands), TOKEN),
        input_output_aliases={t: 2 + t for t in range(n)}, **SPLIT_PARAMS,
    )(*_hbm(lands), after)
    return res[0], res[1], list(res[2:2 + n]), res[-1]


def _gather_forward(send, recv, lands, after, name):
    n = len(lands)

    def body(*refs):
        ins = refs[:n]
        send1, recv1 = refs[n], refs[n + 1]
        send2, recv2 = refs[n + 3], refs[n + 4]
        x, y, c = _position()
        chip = 2 * x + y
        for t in range(n):
            rh = ins[t].shape[1] // 2
            half = pl.ds(c * rh, rh)
            for k in range(1, N_CHIPS):
                px, py = _chip_peer(x, y, k)
                s = 3 * t + k - 1
                got = ins[t].at[2 * px + py, half]
                cp = pltpu.make_async_remote_copy(
                    src_ref=ins[t].at[chip, half], dst_ref=got, send_sem=send1.at[s], recv_sem=recv1.at[s],
                    device_id=(px, py, c), device_id_type=MESH)
                cp.wait_send()
                cp.wait_recv()
                pltpu.make_async_remote_copy(
                    src_ref=got, dst_ref=got, send_sem=send2.at[s], recv_sem=recv2.at[s],
                    device_id=(x, y, 1 - c), device_id_type=MESH).start()
        refs[-1][...] = jnp.zeros(TOKEN.shape, F32)

    res = pl.pallas_call(
        body, name=name, in_specs=[HBM_SPEC] * n + [SEM_SPEC, SEM_SPEC, ANY_SPEC],
        out_specs=(SEM_SPEC, SEM_SPEC, *[HBM_SPEC] * n, pl.BlockSpec(memory_space=pltpu.VMEM)),
        out_shape=(pltpu.SemaphoreType.DMA((3 * n,)), pltpu.SemaphoreType.DMA((3 * n,)), *_hbm_like(lands), TOKEN),
        input_output_aliases={t: 2 + t for t in range(n)}, **SPLIT_PARAMS,
    )(*lands, send, recv, after)
    return res[0], res[1], list(res[2:2 + n]), res[-1]


def _gather_wait(send, recv, lands, after, name):
    n = len(lands)

    def body(*refs):
        ins = refs[:n]
        send_ref, recv_ref = refs[n], refs[n + 1]
        x, y, c = _position()
        for t in range(n):
            rh = ins[t].shape[1] // 2
            for k in range(1, N_CHIPS):
                px, py = _chip_peer(x, y, k)
                cp = pltpu.make_async_remote_copy(
                    src_ref=ins[t].at[2 * px + py, pl.ds(c * rh, rh)],
                    dst_ref=ins[t].at[2 * px + py, pl.ds((1 - c) * rh, rh)], send_sem=send_ref.at[3 * t + k - 1],
                    recv_sem=recv_ref.at[3 * t + k - 1], device_id=(x, y, 1 - c), device_id_type=MESH)
                cp.wait_send()
                cp.wait_recv()

    res = pl.pallas_call(
        body, name=name, in_specs=[HBM_SPEC] * n + [SEM_SPEC, SEM_SPEC, ANY_SPEC], out_specs=[HBM_SPEC] * n,
        out_shape=_hbm_like(lands), input_output_aliases={t: t for t in range(n)}, **SPLIT_PARAMS,
    )(*lands, send, recv, after)
    return list(res)


def _split_start(name, arrays, n_sems, after, issue):
    m = len(arrays)

    def body(*refs):
        issue(refs[:m], refs[m + 1], refs[m + 2])
        refs[-1][...] = jnp.zeros(TOKEN.shape, F32)

    res = pl.pallas_call(
        body, name=name, in_specs=[HBM_SPEC] * m + [ANY_SPEC],
        out_specs=(SEM_SPEC, SEM_SPEC, *[HBM_SPEC] * m, pl.BlockSpec(memory_space=pltpu.VMEM)),
        out_shape=(pltpu.SemaphoreType.DMA((n_sems,)), pltpu.SemaphoreType.DMA((n_sems,)), *_hbm_like(arrays), TOKEN),
        input_output_aliases={t: 2 + t for t in range(m)}, **SPLIT_PARAMS,
    )(*_hbm(arrays), after)
    return res[0], res[1], list(res[2:2 + m]), res[-1]


def _split_wait(name, arrays, send, recv, after, await_all):
    m = len(arrays)

    def body(*refs):
        await_all(refs[:m], refs[m], refs[m + 1])

    res = pl.pallas_call(
        body, name=name, in_specs=[HBM_SPEC] * m + [SEM_SPEC, SEM_SPEC, ANY_SPEC], out_specs=[HBM_SPEC] * m,
        out_shape=_hbm_like(arrays), input_output_aliases={t: t for t in range(m)}, **SPLIT_PARAMS,
    )(*arrays, send, recv, after)
    return list(res)


def _sibling_copies(refs, send, recv, n):
    x, y, c = _position()
    cps = []
    for t in range(n):
        rh = refs[t].shape[1] // 2
        cps.append(pltpu.make_async_remote_copy(
            src_ref=refs[t].at[pl.ds(0, N_CHIPS), pl.ds((1 - c) * rh, rh)], dst_ref=refs[n + t],
            send_sem=send.at[t], recv_sem=recv.at[t], device_id=(x, y, 1 - c), device_id_type=MESH))
    return cps


def _reduce_sibling_start(grads, after, name):
    n = len(grads)
    lands = [lax.empty((N_CHIPS, g.shape[1] // 2, g.shape[2]), BF16) for g in grads]

    def issue(refs, send, recv):
        for cp in _sibling_copies(refs, send, recv, n):
            cp.start()

    return _split_start(name, list(grads) + lands, n, after, issue)


def _reduce_sibling_wait(send, recv, arrays, after, name):
    n = len(arrays) // 2

    def await_all(refs, send_ref, recv_ref):
        for cp in _sibling_copies(refs, send_ref, recv_ref, n):
            cp.wait_send()
            cp.wait_recv()

    res = _split_wait(name, arrays, send, recv, after, await_all)
    return res[:n], res[n:]


def _add_sibling_half(grad, got, dev_idx, name):
    j, r, cols = grad.shape
    rh = r // 2
    tr = rh
    nb = rh // tr

    def body(idx_ref, g_ref, got_ref, out_ref):
        out_ref[...] = (g_ref[...].astype(F32) + got_ref[...].astype(F32)).astype(BF16)

    return pl.pallas_call(
        body, name=name,
        grid_spec=pltpu.PrefetchScalarGridSpec(
            num_scalar_prefetch=1, grid=(j, nb),
            in_specs=[pl.BlockSpec((None, tr, cols), lambda jj, i, idx: (jj, idx[2] * nb + i, 0)),
                      pl.BlockSpec((None, tr, cols), lambda jj, i, idx: (jj, i, 0))],
            out_specs=pl.BlockSpec((None, tr, cols), lambda jj, i, idx: (jj, i, 0))),
        out_shape=jax.ShapeDtypeStruct((j, rh, cols), BF16),
        compiler_params=_params("parallel", "parallel"),
    )(dev_idx, grad, got)


def _chip_copies(refs, send, recv, n, receiving):
    x, y, c = _position()
    chip = 2 * x + y
    cps = []
    for t in range(n):
        for k in range(1, N_CHIPS):
            px, py = _chip_peer(x, y, k)
            cps.append(pltpu.make_async_remote_copy(
                src_ref=refs[t].at[2 * px + py], dst_ref=refs[n + t].at[2 * px + py if receiving else chip],
                send_sem=send.at[3 * t + k - 1], recv_sem=recv.at[3 * t + k - 1],
                device_id=(px, py, c), device_id_type=MESH))
    return cps


def _reduce_chips_start(partials, after, name):
    n = len(partials)
    lands = [lax.empty(p.shape, BF16) for p in partials]

    def issue(refs, send, recv):
        for cp in _chip_copies(refs, send, recv, n, False):
            cp.start()

    return _split_start(name, list(partials) + lands, 3 * n, after, issue)


def _reduce_chips_wait(send, recv, arrays, after, name):
    n = len(arrays) // 2

    def await_all(refs, send_ref, recv_ref):
        for cp in _chip_copies(refs, send_ref, recv_ref, n, True):
            cp.wait_send()
            cp.wait_recv()

    res = _split_wait(name, arrays, send, recv, after, await_all)
    return res[:n], res[n:]


def _sum_partials(land, partial, dev_idx, name):
    _, rh, cols = land.shape
    tr = min(rh, 256)
    nb = rh // tr

    def body(idx_ref, l_ref, p_ref, o_ref):
        chip = idx_ref[1]
        acc = jnp.where(chip == 0, p_ref[...], l_ref[0]).astype(F32)
        for s in range(1, N_CHIPS):
            acc = acc + jnp.where(chip == s, p_ref[...], l_ref[s]).astype(F32)
        o_ref[...] = acc

    return pl.pallas_call(
        body, name=name,
        grid_spec=pltpu.PrefetchScalarGridSpec(
            num_scalar_prefetch=1, grid=(nb,),
            in_specs=[pl.BlockSpec((N_CHIPS, tr, cols), lambda i, idx: (0, i, 0)),
                      pl.BlockSpec((None, tr, cols), lambda i, idx: (idx[1], i, 0))],
            out_specs=pl.BlockSpec((tr, cols), lambda i, idx: (idx[2] * nb + i, 0))),
        out_shape=jax.ShapeDtypeStruct((2 * rh, cols), F32), compiler_params=_params("parallel"),
    )(dev_idx, land, partial)


def _half_copies(refs, send, recv, receiving):
    x, y, c = _position()
    cps = []
    for t, ref in enumerate(refs):
        rh = ref.shape[0] // 2
        cps.append(pltpu.make_async_remote_copy(
            src_ref=ref.at[pl.ds(c * rh, rh)], dst_ref=ref.at[pl.ds(((1 - c) if receiving else c) * rh, rh)],
            send_sem=send.at[t], recv_sem=recv.at[t], device_id=(x, y, 1 - c), device_id_type=MESH))
    return cps


def _share_halves_start(totals, after, name):
    def issue(refs, send, recv):
        for cp in _half_copies(refs, send, recv, False):
            cp.start()

    return _split_start(name, list(totals), len(totals), after, issue)


def _share_halves_wait(send, recv, totals, after, name):
    def await_all(refs, send_ref, recv_ref):
        for cp in _half_copies(refs, send_ref, recv_ref, True):
            cp.wait_send()
            cp.wait_recv()

    return _split_wait(name, totals, send, recv, after, await_all)


SMALL_ROWS = 56


def _small_copies(refs, send, recv, receiving):
    x, y, c = _position()
    me = 4 * x + 2 * y + c
    cps = []
    for k in range(1, N_DEV):
        px, py, pc = _xor_peer(x, y, c, k)
        cps.append(pltpu.make_async_remote_copy(
            src_ref=refs[0], dst_ref=refs[1].at[4 * px + 2 * py + pc if receiving else me],
            send_sem=send.at[k - 1], recv_sem=recv.at[k - 1], device_id=(px, py, pc), device_id_type=MESH))
    return cps


def _small_gather_start(packed, after):
    land = lax.empty((N_DEV,) + packed.shape, F32)

    def issue(refs, send, recv):
        for cp in _small_copies(refs, send, recv, False):
            cp.start()

    return _split_start("small_gather_start", [packed, land], N_DEV - 1, after, issue)


def _small_gather_wait(send, recv, arrays, after):
    def await_all(refs, send_ref, recv_ref):
        for cp in _small_copies(refs, send_ref, recv_ref, True):
            cp.wait_send()
            cp.wait_recv()

    return _split_wait("small_gather_wait", arrays, send, recv, after, await_all)


def _reduce_small(packed, land, silu_c):
    ns = 3 * D_MODEL // N_CHIPS

    def body(p_ref, land_ref, sc_ref, tot_ref, gw_ref, loss_ref, qk_ref, allp):
        x, y, c = _position()
        me = 4 * x + 2 * y + c
        chip = 2 * x + y
        for i in range(N_DEV):
            allp[i] = jnp.where(me == i, p_ref[...], land_ref[i])
        tot = allp[0]
        for i in range(1, N_DEV):
            tot = tot + allp[i]
        tot_ref[...] = tot
        loss_ref[...] = jnp.sum(tot[11:12, :], axis=1, keepdims=True) * (0.5 / D_MODEL)
        fold = tot[5:11, 0:HEAD_DIM]
        for h in range(1, N_HEADS):
            fold = fold + tot[5:11, h * HEAD_DIM:(h + 1) * HEAD_DIM]
        qk_ref[...] = jnp.concatenate([fold, jnp.zeros((2, HEAD_DIM), F32)], axis=0)
        sct = sc_ref[...].T
        rc = 64
        for l in range(2):
            dms = [allp[i, pl.ds(12 + 4 * l + chip, 1), :][:, :ns] for i in range(N_DEV)]
            for r0 in range(0, D_MODEL, rc):
                acc = sct[r0:r0 + rc, 0:1] * dms[0]
                for i in range(1, N_DEV):
                    acc = acc + sct[r0:r0 + rc, i:i + 1] * dms[i]
                gw_ref[l, r0:r0 + rc, :] = acc

    vm = pl.BlockSpec(memory_space=pltpu.VMEM)
    return pl.pallas_call(
        body, name="reduce_small", in_specs=[vm, vm, vm], out_specs=[vm] * 4,
        out_shape=[jax.ShapeDtypeStruct((SMALL_ROWS, D_MODEL), F32), jax.ShapeDtypeStruct((2, D_MODEL, ns), F32),
                   jax.ShapeDtypeStruct((1, 1), F32), jax.ShapeDtypeStruct((8, HEAD_DIM), F32)],
        scratch_shapes=[pltpu.VMEM((N_DEV, SMALL_ROWS, D_MODEL), F32)],
        compiler_params=pltpu.CompilerParams(vmem_limit_bytes=VMEM_LIMIT_BYTES),
    )(packed, land, silu_c)


def kernel(x, c, norm_g, ada_w, ada_b, a_w_in, a_conv_w, a_conv_b, a_ln_g, a_ln_b, a_w_out, b_w_in, b_q_norm, b_k_norm, b_w_out, loss_target, m_norm_g, m_ada_w, m_ada_b, m_a_w_in, m_a_conv_w, m_a_conv_b, m_a_ln_g, m_a_ln_b, m_a_w_out, m_b_w_in, m_b_q_norm, m_b_k_norm, m_b_w_out, v_norm_g, v_ada_w, v_ada_b, v_a_w_in, v_a_conv_w, v_a_conv_b, v_a_ln_g, v_a_ln_b, v_a_w_out, v_b_w_in, v_b_q_norm, v_b_k_norm, v_b_w_out):
    chip = 2 * lax.axis_index("x") + lax.axis_index("y")
    core = lax.axis_index("c")
    chip_idx = chip.astype(jnp.int32).reshape(1)
    dev_idx = jnp.stack([2 * chip + core, chip, core]).astype(jnp.int32)

    mods, silu_c, conv_w_full = _ada_forward(c, ada_w, ada_b, a_conv_w[0])
    lands_a = [_cast_into_slot(a_w_in[0], chip_idx, "cast_a_w_in"), _cast_into_slot(a_w_out[0], chip_idx, "cast_a_w_out")]
    send_a, recv_a, lands_a, token_a = _gather_start(lands_a, mods, "gather_start_a")
    land_b_in, own_wb_in = _cast_into_slot(b_w_in[0], chip_idx, "cast_b_w_in", keep_own=True, after=token_a)
    lands_b = [land_b_in, _cast_into_slot(b_w_out[0], chip_idx, "cast_b_w_out", after=token_a)]
    send_b, recv_b, lands_b, token_b = _gather_start(lands_b, token_a, "gather_start_b")
    mods = mods + token_b[0:2, 0:1]

    def weights_a(after):
        send, recv, lands, _ = _gather_forward(send_a, recv_a, lands_a, after, "gather_forward_a")
        w_in, w_out = _gather_wait(send, recv, lands, after, "gather_wait_a")
        return w_in, w_out.reshape(D_MODEL, D_MODEL)

    forwarded_b = []

    def weights_b(after):
        send, recv, lands, _ = forwarded_b
        w_in, w_out = _gather_wait(send, recv, lands, after, "gather_wait_b")
        return w_in, w_out.reshape(D_MODEL, D_MODEL)

    def forward_weights_b(after):
        forwarded_b.extend(_gather_forward(send_b, recv_b, lands_b, after, "gather_forward_b"))
        return forwarded_b[3]

    stage1, stage2 = {}, {}

    def send_grads(tag, dw_in, dw_out):
        grads = [dw_in, dw_out.reshape(N_CHIPS, D_MODEL // N_CHIPS, D_MODEL)]
        send, recv, arrays, token = _reduce_sibling_start(grads, dw_out, f"reduce_d2d_start_{tag}")
        stage1[tag] = (send, recv, arrays)
        return token

    def forward_grads(tag, after):
        send, recv, arrays = stage1[tag]
        grads, got = _reduce_sibling_wait(send, recv, arrays, after, f"reduce_d2d_wait_{tag}")
        partials = [_add_sibling_half(grads[i], got[i], dev_idx, f"reduce_add_{tag}_{i}") for i in range(2)]
        send, recv, arrays, token = _reduce_chips_start(partials, partials[1], f"reduce_ici_start_{tag}")
        stage2[tag] = (send, recv, arrays)
        return token

    stage3 = {}

    def sum_grads(tag, after):
        send, recv, arrays = stage2[tag]
        partials, lands = _reduce_chips_wait(send, recv, arrays, after, f"reduce_ici_wait_{tag}")
        totals = [_sum_partials(lands[i], partials[i], dev_idx, f"reduce_sum_{tag}_{i}") for i in range(2)]
        send, recv, totals, token = _share_halves_start(totals, totals[1], f"reduce_share_start_{tag}")
        stage3[tag] = (send, recv, totals)
        return token

    def finish_grads(tag, after):
        send, recv, totals = stage3[tag]
        return _share_halves_wait(send, recv, totals, after, f"reduce_share_wait_{tag}")

    grad_x, small = _local_step(
        x[0], loss_target[0], mods.reshape(2, 3, D_MODEL), norm_g, conv_w_full, a_conv_b, a_ln_g[0:1],
        a_ln_b[0:1], b_q_norm[0], b_k_norm[0], chip.astype(jnp.int32), own_wb_in,
        weights_a, weights_b, forward_weights_b,
        functools.partial(send_grads, "b"), functools.partial(forward_grads, "b"), functools.partial(send_grads, "a"))

    ns = 3 * D_MODEL // N_CHIPS
    pad_mod = lambda dm: jnp.pad(dm.reshape(N_CHIPS, ns), ((0, 0), (0, D_MODEL - ns)))
    packed = jnp.concatenate([
        small["dnorm_g"], small["dconv_b"], small["dln_g"], small["dln_b"], small["dq_norm"], small["dk_norm"],
        small["loss_cols"], pad_mod(small["dmod0"]), pad_mod(small["dmod1"]), small["dconv_w"],
        jnp.zeros((SMALL_ROWS - 20 - CONV_WIDTH, D_MODEL), F32)], axis=0)
    send_s, recv_s, small_arrays, token_s = _small_gather_start(packed, packed)

    given = dict(norm_g=(norm_g, m_norm_g, v_norm_g), ada_w=(ada_w, m_ada_w, v_ada_w), ada_b=(ada_b, m_ada_b, v_ada_b),
                 a_w_in=(a_w_in, m_a_w_in, v_a_w_in), a_conv_w=(a_conv_w, m_a_conv_w, v_a_conv_w),
                 a_conv_b=(a_conv_b, m_a_conv_b, v_a_conv_b), a_ln_g=(a_ln_g, m_a_ln_g, v_a_ln_g),
                 a_ln_b=(a_ln_b, m_a_ln_b, v_a_ln_b), a_w_out=(a_w_out, m_a_w_out, v_a_w_out),
                 b_w_in=(b_w_in, m_b_w_in, v_b_w_in), b_q_norm=(b_q_norm, m_b_q_norm, v_b_q_norm),
                 b_k_norm=(b_k_norm, m_b_k_norm, v_b_k_norm), b_w_out=(b_w_out, m_b_w_out, v_b_w_out))
    order = ["norm_g", "ada_w", "ada_b", "a_w_in", "a_conv_w", "a_conv_b", "a_ln_g", "a_ln_b", "a_w_out", "b_w_in",
             "b_q_norm", "b_k_norm", "b_w_out"]
    outs = {}

    def update(k, g2, after=None, copy_grad=False):
        w, m, v = given[k]
        shape2 = g2.shape
        res = _adamw(w.reshape(shape2), g2, m.reshape(shape2), v.reshape(shape2), f"adamw_{k}", after, copy_grad)
        outs[k] = tuple(a.reshape(w.shape) for a in ((res[3] if copy_grad else g2), res[0], res[1], res[2]))

    token = forward_grads("a", token_s)
    token = sum_grads("b", token)
    packed, land = _small_gather_wait(send_s, recv_s, small_arrays, token)
    tot, g_ada_w, loss, qk = _reduce_small(packed, land, silu_c)
    g_b_in, g_b_out = finish_grads("b", tot)
    update("b_w_in", g_b_in, copy_grad=True)
    update("b_w_out", g_b_out, copy_grad=True)
    token = sum_grads("a", outs["b_w_in"][1])
    cw = D_MODEL // N_CHIPS
    g_small = dict(
        norm_g=tot[0:2], a_conv_b=tot[2:3], a_ln_g=tot[3:4], a_ln_b=tot[4:5],
        b_q_norm=qk[0:3], b_k_norm=qk[3:6],
        ada_b=jnp.stack([tot[12:16, :ns].reshape(3 * D_MODEL), tot[16:20, :ns].reshape(3 * D_MODEL)]),
        a_conv_w=lax.dynamic_slice(tot[20:20 + CONV_WIDTH], (0, chip * cw), (CONV_WIDTH, cw)),
    )
    update("ada_w", g_ada_w.reshape(2 * D_MODEL, ns), after=token)
    for k, g2 in g_small.items():
        update(k, g2, after=token)
    g_a_in, g_a_out = finish_grads("a", outs["ada_w"][1])
    update("a_w_in", g_a_in, copy_grad=True)
    update("a_w_out", g_a_out, copy_grad=True)
    return (loss.reshape(()), grad_x[None], *[outs[k][0] for k in order], *[outs[k][1] for k in order],
            *[outs[k][2] for k in order], *[outs[k][3] for k in order])
```

```python
import functools

import jax
import jax.numpy as jnp
from jax import lax
from jax.experimental import pallas as pl
from jax.experimental.pallas import tpu as pltpu

F32 = jnp.float32
BF16 = jnp.bfloat16

SEQ = 2048
D_MODEL = 1024
CONV_WIDTH = 31
HEAD_DIM = 64
N_HEADS = 16
DILATIONS = (1, 4, 16)
ATTN_BLOCK = 128
NORM_EPS = 1e-6
NEG_INF = -1e30
N_DEV = 8
N_CHIPS = 4

ADAM_LR = 0.001
ADAM_B1 = 0.9
ADAM_B2 = 0.999
ADAM_EPS = 1e-08
ADAM_WD = 0.01
ADAM_STEP = 10

VMEM_LIMIT_BYTES = 52 * 1024 * 1024
HALO = 32
LANES = 128
MESH = pl.DeviceIdType.MESH


def _params(*sem):
    return pltpu.CompilerParams(dimension_semantics=sem or None, vmem_limit_bytes=VMEM_LIMIT_BYTES)


def _sigmoid(v):
    return 1.0 / (1.0 + jnp.exp(-v))


def _row_spec(tm, cols, col_block=0):
    return pl.BlockSpec((tm, cols), lambda i: (i, col_block))


def _vec_spec(rows, cols):
    return pl.BlockSpec((rows, cols), lambda i: (0, 0))


def _normmod(xv, g, scale, shift):
    r = lax.rsqrt(jnp.mean(xv * xv, axis=-1, keepdims=True) + NORM_EPS)
    return xv * r * g * (1.0 + scale) + shift


def _normmod_fwd(x, g, scale, shift, name):
    tm = 256

    def body(x_ref, g_ref, sc_ref, sh_ref, h_ref, ht_ref):
        h = _normmod(x_ref[...], g_ref[...], sc_ref[...], sh_ref[...])
        h_ref[...] = h.astype(BF16)
        ht_ref[...] = h.T.astype(BF16)

    return pl.pallas_call(
        body, name=name, grid=(SEQ // tm,),
        in_specs=[_row_spec(tm, D_MODEL)] + [_vec_spec(1, D_MODEL)] * 3,
        out_specs=[_row_spec(tm, D_MODEL), pl.BlockSpec((D_MODEL, tm), lambda i: (0, i))],
        out_shape=[jax.ShapeDtypeStruct((SEQ, D_MODEL), BF16), jax.ShapeDtypeStruct((D_MODEL, SEQ), BF16)],
        compiler_params=_params("parallel"),
    )(x, g, scale, shift)


def _normmod_bwd(x, g, scale, dh_parts, dres, name, part_dilations=None, gated=None):
    tm = 256
    n_parts = len(dh_parts)
    dils = part_dilations or (1,) * n_parts
    dh_parts = [p if d == 1 else p.reshape(d, SEQ // d, D_MODEL) for p, d in zip(dh_parts, dils)]
    n_gated = 0 if gated is None else 2

    def body(x_ref, g_ref, sc_ref, dres_ref, *rest):
        part_refs = rest[:n_parts]
        gated_refs = rest[n_parts:n_parts + n_gated]
        out_refs = rest[n_parts + n_gated:]
        dx_ref, sums_ref, nat = out_refs[0], out_refs[1], out_refs[-1]
        xv = x_ref[...]
        r = lax.rsqrt(jnp.mean(xv * xv, axis=-1, keepdims=True) + NORM_EPS)
        xn = xv * r
        dh = _load_natural(part_refs[0], nat, dils[0])
        for p, d in zip(part_refs[1:], dils[1:]):
            dh = dh + _load_natural(p, nat, d)
        gv = g_ref[...]
        one_sc = 1.0 + sc_ref[...]
        dxn = dh * (gv * one_sc)
        dx = dres_ref[...] + r * (dxn - xn * jnp.mean(dxn * xn, axis=-1, keepdims=True))
        dx_ref[...] = dx
        dhx = dh * xn
        rows = [jnp.sum(dhx, axis=0, keepdims=True) * one_sc,
                jnp.sum(dhx, axis=0, keepdims=True) * gv,
                jnp.sum(dh, axis=0, keepdims=True)]
        if gated is not None:
            gate_ref, y_ref = gated_refs
            out_refs[2][...] = (dx * gate_ref[...]).astype(BF16)
            rows.append(jnp.sum(dx * y_ref[...], axis=0, keepdims=True))
        sums = jnp.concatenate(rows + [jnp.zeros((8 - len(rows), D_MODEL), F32)], axis=0)

        @pl.when(pl.program_id(0) == 0)
        def _():
            sums_ref[...] = jnp.zeros_like(sums_ref)

        sums_ref[...] += sums

    gated_specs = [] if gated is None else [_vec_spec(1, D_MODEL), _row_spec(tm, D_MODEL)]
    dy_spec = [] if gated is None else [_row_spec(tm, D_MODEL)]
    dy_shape = [] if gated is None else [jax.ShapeDtypeStruct((SEQ, D_MODEL), BF16)]
    return pl.pallas_call(
        body, name=name, grid=(SEQ // tm,),
        in_specs=[_row_spec(tm, D_MODEL), _vec_spec(1, D_MODEL), _vec_spec(1, D_MODEL), _row_spec(tm, D_MODEL)]
        + [_class_spec(tm, d) for d in dils] + gated_specs,
        out_specs=[_row_spec(tm, D_MODEL), _vec_spec(8, D_MODEL)] + dy_spec,
        out_shape=[jax.ShapeDtypeStruct((SEQ, D_MODEL), F32), jax.ShapeDtypeStruct((8, D_MODEL), F32)] + dy_shape,
        scratch_shapes=[_natural_scratch(tm)],
        compiler_params=_params("arbitrary"),
    )(x, g, scale, dres, *dh_parts, *(gated or ()))


def _mm(lhs, rhs, *, tn, tile0, n_tiles, out_dtype, name, out3d=None, prev=None, transpose_lhs=False):
    mo, kc = lhs.shape[::-1] if transpose_lhs else lhs.shape
    cm = min(mo, 1024)
    tc = 256

    def body(l_ref, r_ref, *rest):
        if transpose_lhs:
            o_ref, lt_ref = rest[-2], rest[-1]

            @pl.when(pl.program_id(0) == 0)
            def _():
                for c in range(kc // tc):
                    lt_ref[:, c * tc:(c + 1) * tc] = l_ref[c * tc:(c + 1) * tc, :].astype(F32).T.astype(l_ref.dtype)
        else:
            o_ref, lt_ref = rest[-1], l_ref
        for m in range(mo // cm):
            rows = pl.ds(m * cm, cm)
            o_ref[rows, :] = jnp.dot(lt_ref[rows, :], r_ref[...], preferred_element_type=F32).astype(out_dtype)

    if rhs.ndim == 3:
        tps_r = rhs.shape[2] // tn
        r_spec = pl.BlockSpec((None, kc, tn), lambda t: ((tile0 + t) // tps_r, 0, (tile0 + t) % tps_r))
    else:
        r_spec = pl.BlockSpec((kc, tn), lambda t: (0, t))
    in_specs = [pl.BlockSpec(lhs.shape, lambda t: (0, 0)), r_spec]
    args = [lhs, rhs]
    aliases = {}
    if out3d is None:
        o_spec = pl.BlockSpec((mo, tn), lambda t: (0, t))
        o_shape = jax.ShapeDtypeStruct((mo, n_tiles * tn), out_dtype)
    else:
        j_out, ns_out = out3d
        tps_o = ns_out // tn
        o_spec = pl.BlockSpec((None, mo, tn), lambda t: ((tile0 + t) // tps_o, 0, (tile0 + t) % tps_o))
        o_shape = jax.ShapeDtypeStruct((j_out, mo, ns_out), out_dtype)
        if prev is not None:
            in_specs.append(pl.BlockSpec(memory_space=pl.ANY))
            args.append(prev)
            aliases = {2: 0}
    return pl.pallas_call(
        body, name=name, grid=(n_tiles,), in_specs=in_specs, out_specs=o_spec, out_shape=o_shape,
        input_output_aliases=aliases,
        scratch_shapes=[pltpu.VMEM((mo, kc), lhs.dtype)] if transpose_lhs else [],
        compiler_params=_params("arbitrary" if transpose_lhs else "parallel"),
    )(*args)


def _b_in_tiles(h_parts, w3, tile_ids, n_tiles, name, prev=None):
    _, kc, ns = w3.shape
    tps = ns // B_TN
    cm = 1024

    def body(ids_ref, h0_ref, h1_ref, h2_ref, w_ref, *rest):
        o_ref = rest[-1]
        out_tile = ids_ref[1, pl.program_id(0)]
        group = jnp.where(out_tile >= B_Z_TILE0, 0, out_tile // B_GROUP_TILES)
        for g, h_ref in enumerate((h0_ref, h1_ref, h2_ref)):
            @pl.when(group == g)
            def _():
                for m in range(SEQ // cm):
                    rows = pl.ds(m * cm, cm)
                    o_ref[rows, :] = jnp.dot(h_ref[rows, :], w_ref[...], preferred_element_type=F32).astype(BF16)

    resident = pl.BlockSpec((SEQ, kc), lambda t, ids: (0, 0))
    in_specs = [resident] * 3 + [pl.BlockSpec((None, kc, B_TN), lambda t, ids: (ids[0, t] // tps, 0, ids[0, t] % tps))]
    args = [*h_parts, w3]
    aliases = {}
    if prev is not None:
        in_specs.append(pl.BlockSpec(memory_space=pl.ANY))
        args.append(prev)
        aliases = {5: 0}
    return pl.pallas_call(
        body, name=name,
        grid_spec=pltpu.PrefetchScalarGridSpec(
            num_scalar_prefetch=1, grid=(n_tiles,), in_specs=in_specs,
            out_specs=pl.BlockSpec((SEQ, B_TN), lambda t, ids: (0, ids[1, t]))),
        out_shape=jax.ShapeDtypeStruct((SEQ, B_TILES * B_TN), BF16),
        input_output_aliases=aliases, compiler_params=_params("arbitrary"),
    )(tile_ids, *args)


def _mm_nt(dy, w3, *, tn, tile0, n_tiles, name, after=None):
    m_rows = dy.shape[0]
    _, kc, ns = w3.shape
    tps = ns // tn
    cm = 512
    extra = [] if after is None else [after]

    def body(dy_ref, w_ref, *rest):
        o_ref, acc = rest[-2], rest[-1]
        t = pl.program_id(0)

        @pl.when(t == 0)
        def _():
            acc[...] = jnp.zeros_like(acc)

        for m in range(m_rows // cm):
            rows = pl.ds(m * cm, cm)
            acc[rows, :] += lax.dot_general(dy_ref[rows, :], w_ref[...], NT_DIMS, preferred_element_type=F32)

        @pl.when(t == n_tiles - 1)
        def _():
            o_ref[...] = acc[...].astype(BF16)

    return pl.pallas_call(
        body, name=name, grid=(n_tiles,),
        in_specs=[pl.BlockSpec((m_rows, tn), lambda t: (0, t)),
                  pl.BlockSpec((None, kc, tn), lambda t: ((tile0 + t) // tps, 0, (tile0 + t) % tps))]
        + [pl.BlockSpec(memory_space=pl.ANY)] * len(extra),
        out_specs=pl.BlockSpec((m_rows, kc), lambda t: (0, 0)),
        out_shape=jax.ShapeDtypeStruct((m_rows, kc), BF16),
        scratch_shapes=[pltpu.VMEM((m_rows, kc), F32)],
        compiler_params=_params("arbitrary"),
    )(dy, w3, *extra)


CONV_CHUNK = 16


def _shift_copies(buf, shifted):
    rows = shifted.shape[1]
    for s in range(1, 8):
        shifted[s - 1] = buf[pl.ds(s, rows), :]


def _shifted_rows(buf, shifted, offset, r0):
    s = offset % 8
    if s == 0:
        return buf[pl.ds(r0 + offset, CONV_CHUNK), :]
    return shifted[s - 1, pl.ds(r0 + (offset - s), CONV_CHUNK), :]


def _spread_taps(w_ref, taps):
    for k in range(CONV_WIDTH):
        taps[k] = jnp.broadcast_to(w_ref[k:k + 1, :], (8, D_MODEL))


def _times_tap(taps, k, rows):
    return (rows.reshape(CONV_CHUNK // 8, 8, D_MODEL) * taps[k][None]).reshape(CONV_CHUNK, D_MODEL)


def _conv_fwd(proj, conv_w, conv_b, ln_g, ln_b, name):
    tm = 256
    hb = tm // HALO

    def body(vg_ref, halo_ref, z_ref, w_ref, b_ref, g_ref, be_ref, u5_ref, u5t_ref, u2_ref, buf, shifted, taps):
        i = pl.program_id(0)
        u1 = vg_ref[:, :D_MODEL] * _sigmoid(vg_ref[:, D_MODEL:])
        u1h = halo_ref[:, :D_MODEL] * _sigmoid(halo_ref[:, D_MODEL:])
        buf[pl.ds(0, HALO), :] = jnp.where(i > 0, u1h, 0.0)
        buf[pl.ds(HALO, tm), :] = u1
        _shift_copies(buf, shifted)
        _spread_taps(w_ref, taps)

        def chunk(ci, carry):
            r0 = pl.multiple_of(ci * CONV_CHUNK, CONV_CHUNK)
            acc = jnp.broadcast_to(b_ref[...], (CONV_CHUNK, D_MODEL))
            for k in range(CONV_WIDTH):
                acc = acc + _times_tap(taps, k, _shifted_rows(buf, shifted, HALO - (CONV_WIDTH - 1) + k, r0))
            u2_ref[pl.ds(r0, CONV_CHUNK), :] = acc
            return carry

        lax.fori_loop(0, tm // CONV_CHUNK, chunk, 0)
        acc = u2_ref[...]
        mu = jnp.mean(acc, axis=-1, keepdims=True)
        xc = acc - mu
        rstd = lax.rsqrt(jnp.mean(xc * xc, axis=-1, keepdims=True) + NORM_EPS)
        u3 = xc * rstd * g_ref[...] + be_ref[...]
        zv = z_ref[...]
        u5 = u3 * _sigmoid(u3) * (zv * _sigmoid(zv))
        u5_ref[...] = u5.astype(BF16)
        u5t_ref[...] = u5.T.astype(BF16)

    return pl.pallas_call(
        body, name=name, grid=(SEQ // tm,),
        in_specs=[pl.BlockSpec((tm, 2 * D_MODEL), lambda i: (i, 0)),
                  pl.BlockSpec((HALO, 2 * D_MODEL), lambda i: (jnp.maximum(i * hb - 1, 0), 0)),
                  _row_spec(tm, D_MODEL, 2),
                  _vec_spec(CONV_WIDTH, D_MODEL)] + [_vec_spec(1, D_MODEL)] * 3,
        out_specs=[_row_spec(tm, D_MODEL), pl.BlockSpec((D_MODEL, tm), lambda i: (0, i)), _row_spec(tm, D_MODEL)],
        out_shape=[jax.ShapeDtypeStruct((SEQ, D_MODEL), BF16), jax.ShapeDtypeStruct((D_MODEL, SEQ), BF16),
                   jax.ShapeDtypeStruct((SEQ, D_MODEL), F32)],
        scratch_shapes=[pltpu.VMEM((HALO + tm, D_MODEL), F32), pltpu.VMEM((7, HALO + tm - 8, D_MODEL), F32),
                        pltpu.VMEM((CONV_WIDTH, 8, D_MODEL), F32)],
        compiler_params=_params("parallel"),
    )(proj, proj, proj, conv_w, conv_b, ln_g, ln_b)


def _conv_bwd_pointwise(dy, w_out, proj, u2, ln_g, ln_b, name):
    tm = 256

    def body(dy_ref, w_ref, z_ref, u2_ref, g_ref, be_ref, du2_ref, dz_ref, sums_ref):
        u2v = u2_ref[...]
        mu = jnp.mean(u2v, axis=-1, keepdims=True)
        xc = u2v - mu
        rstd = lax.rsqrt(jnp.mean(xc * xc, axis=-1, keepdims=True) + NORM_EPS)
        xhat = xc * rstd
        u3 = xhat * g_ref[...] + be_ref[...]
        s3 = _sigmoid(u3)
        u4 = u3 * s3
        zv = z_ref[...]
        sz = _sigmoid(zv)
        du5v = lax.dot_general(dy_ref[...], w_ref[...], NT_DIMS, preferred_element_type=F32)
        dz_ref[...] = du5v * u4 * (sz * (1.0 + zv * (1.0 - sz)))
        du3 = du5v * (zv * sz) * (s3 * (1.0 + u3 * (1.0 - s3)))
        dxhat = du3 * g_ref[...]
        du2 = rstd * (dxhat - jnp.mean(dxhat, axis=-1, keepdims=True)
                      - xhat * jnp.mean(dxhat * xhat, axis=-1, keepdims=True))
        du2_ref[...] = du2
        sums = jnp.concatenate([
            jnp.sum(du3 * xhat, axis=0, keepdims=True),
            jnp.sum(du3, axis=0, keepdims=True),
            jnp.sum(du2, axis=0, keepdims=True),
            jnp.zeros((5, D_MODEL), F32)], axis=0)

        @pl.when(pl.program_id(0) == 0)
        def _():
            sums_ref[...] = jnp.zeros_like(sums_ref)

        sums_ref[...] += sums

    return pl.pallas_call(
        body, name=name, grid=(SEQ // tm,),
        in_specs=[_row_spec(tm, D_MODEL), _vec_spec(D_MODEL, D_MODEL), _row_spec(tm, D_MODEL, 2),
                  _row_spec(tm, D_MODEL), _vec_spec(1, D_MODEL), _vec_spec(1, D_MODEL)],
        out_specs=[_row_spec(tm, D_MODEL), _row_spec(tm, D_MODEL), _vec_spec(8, D_MODEL)],
        out_shape=[jax.ShapeDtypeStruct((SEQ, D_MODEL), F32), jax.ShapeDtypeStruct((SEQ, D_MODEL), F32),
                   jax.ShapeDtypeStruct((8, D_MODEL), F32)],
        compiler_params=_params("arbitrary"),
    )(dy, w_out, proj, u2, ln_g, ln_b)


def _conv_bwd_taps(du2, dz, proj, conv_w, name):
    tm = 256
    hb = tm // HALO
    n_blocks = SEQ // tm

    def body(du2_ref, dnext_ref, dz_ref, vg_ref, w_ref, dproj_ref, dw_ref, dbuf, dshift, sgbuf, ubuf, dwacc, taps):
        i = pl.program_id(0)
        _spread_taps(w_ref, taps)
        sg = _sigmoid(vg_ref[:, D_MODEL:])
        sgbuf[...] = sg
        ubuf[...] = vg_ref[:, :D_MODEL] * sg
        dbuf[pl.ds(0, tm), :] = du2_ref[...]
        dbuf[pl.ds(tm, HALO), :] = jnp.where(i < n_blocks - 1, dnext_ref[...], 0.0)
        _shift_copies(dbuf, dshift)

        @pl.when(i == 0)
        def _():
            dwacc[...] = jnp.zeros_like(dwacc)

        def chunk(ci, carry):
            r0 = pl.multiple_of(ci * CONV_CHUNK, CONV_CHUNK)
            rows = pl.ds(r0, CONV_CHUNK)
            u1c = ubuf[rows, :]
            du1 = jnp.zeros((CONV_CHUNK, D_MODEL), F32)
            for k in range(CONV_WIDTH):
                ahead = _shifted_rows(dbuf, dshift, CONV_WIDTH - 1 - k, r0)
                du1 = du1 + _times_tap(taps, k, ahead)
                prod = u1c * ahead
                dwacc[k] += prod[0:8] + prod[8:16]
            sgc = sgbuf[rows, :]
            dval = du1 * sgc
            dproj_ref[rows, 0:D_MODEL] = dval.astype(BF16)
            dproj_ref[rows, D_MODEL:2 * D_MODEL] = (dval * vg_ref[rows, 0:D_MODEL] * (1.0 - sgc)).astype(BF16)
            return carry

        lax.fori_loop(0, tm // CONV_CHUNK, chunk, 0)
        dproj_ref[:, 2 * D_MODEL:] = dz_ref[...].astype(BF16)

        @pl.when(i == n_blocks - 1)
        def _():
            for k in range(CONV_WIDTH):
                dw_ref[k:k + 1, :] = jnp.sum(dwacc[k], axis=0, keepdims=True)
            dw_ref[CONV_WIDTH:, :] = jnp.zeros((32 - CONV_WIDTH, D_MODEL), F32)

    return pl.pallas_call(
        body, name=name, grid=(n_blocks,),
        in_specs=[_row_spec(tm, D_MODEL),
                  pl.BlockSpec((HALO, D_MODEL), lambda i: (jnp.minimum((i + 1) * hb, SEQ // HALO - 1), 0)),
                  _row_spec(tm, D_MODEL),
                  pl.BlockSpec((tm, 2 * D_MODEL), lambda i: (i, 0)),
                  _vec_spec(CONV_WIDTH, D_MODEL)],
        out_specs=[_row_spec(tm, 3 * D_MODEL), _vec_spec(32, D_MODEL)],
        out_shape=[jax.ShapeDtypeStruct((SEQ, 3 * D_MODEL), BF16), jax.ShapeDtypeStruct((32, D_MODEL), F32)],
        scratch_shapes=[pltpu.VMEM((tm + HALO, D_MODEL), F32), pltpu.VMEM((7, HALO + tm - 8, D_MODEL), F32),
                        pltpu.VMEM((tm, D_MODEL), F32), pltpu.VMEM((tm, D_MODEL), F32),
                        pltpu.VMEM((CONV_WIDTH, 8, D_MODEL), F32), pltpu.VMEM((CONV_WIDTH, 8, D_MODEL), F32)],
        compiler_params=_params("arbitrary"),
    )(du2, du2, dz, proj, conv_w)


def _out_a(u5, w_out, x, gate, g1, scale1, shift1, name):
    tm = 256
    n_d = len(DILATIONS)

    def body(u_ref, w_ref, x_ref, gate_ref, g_ref, sc_ref, sh_ref, x1_ref, y_ref, ht_ref, *rest):
        h_refs, nat = rest[:n_d], rest[-1]
        y = jnp.dot(u_ref[...], w_ref[...], preferred_element_type=F32)
        x1 = x_ref[...] + gate_ref[...] * y
        y_ref[...] = y
        x1_ref[...] = x1
        h = _normmod(x1, g_ref[...], sc_ref[...], sh_ref[...])
        ht_ref[...] = h.T.astype(BF16)
        for h_ref, d in zip(h_refs, DILATIONS):
            _store_classes(h_ref, h, nat, d)

    res = pl.pallas_call(
        body, name=name, grid=(SEQ // tm,),
        in_specs=[_row_spec(tm, D_MODEL), _vec_spec(D_MODEL, D_MODEL), _row_spec(tm, D_MODEL)]
        + [_vec_spec(1, D_MODEL)] * 4,
        out_specs=[_row_spec(tm, D_MODEL), _row_spec(tm, D_MODEL), pl.BlockSpec((D_MODEL, tm), lambda i: (0, i))]
        + [_class_spec(tm, d) for d in DILATIONS],
        out_shape=[jax.ShapeDtypeStruct((SEQ, D_MODEL), F32), jax.ShapeDtypeStruct((SEQ, D_MODEL), F32),
                   jax.ShapeDtypeStruct((D_MODEL, SEQ), BF16)] + [_class_shape(d, BF16) for d in DILATIONS],
        scratch_shapes=[_natural_scratch(tm)],
        compiler_params=_params("parallel"),
    )(u5, w_out, x, gate, g1, scale1, shift1)
    return res[0], res[1], res[2], [a.reshape(SEQ, D_MODEL) for a in res[3:]]


def _out_b_loss(u, w_out, x1, gate, target, name):
    tm = 256

    def body(u_ref, w_ref, x_ref, gate_ref, t_ref, e_ref, dy_ref, sums_ref):
        y = jnp.dot(u_ref[...], w_ref[...], preferred_element_type=F32)
        diff = x_ref[...] + gate_ref[...] * y - t_ref[...]
        e = diff * (1.0 / D_MODEL)
        e_ref[...] = e
        dy_ref[...] = (e * gate_ref[...]).astype(BF16)
        sums = jnp.concatenate([
            jnp.sum(e * y, axis=0, keepdims=True),
            jnp.sum(diff * diff, axis=0, keepdims=True),
            jnp.zeros((6, D_MODEL), F32)], axis=0)

        @pl.when(pl.program_id(0) == 0)
        def _():
            sums_ref[...] = jnp.zeros_like(sums_ref)

        sums_ref[...] += sums

    return pl.pallas_call(
        body, name=name, grid=(SEQ // tm,),
        in_specs=[_row_spec(tm, D_MODEL), _vec_spec(D_MODEL, D_MODEL), _row_spec(tm, D_MODEL),
                  _vec_spec(1, D_MODEL), _row_spec(tm, D_MODEL)],
        out_specs=[_row_spec(tm, D_MODEL), _row_spec(tm, D_MODEL), _vec_spec(8, D_MODEL)],
        out_shape=[jax.ShapeDtypeStruct((SEQ, D_MODEL), F32), jax.ShapeDtypeStruct((SEQ, D_MODEL), BF16),
                   jax.ShapeDtypeStruct((8, D_MODEL), F32)],
        compiler_params=_params("arbitrary"),
    )(u, w_out, x1, gate, target)


def _seg_matrix():
    r = lax.broadcasted_iota(jnp.int32, (256, 256), 0) // HEAD_DIM
    c = lax.broadcasted_iota(jnp.int32, (256, 256), 1) // HEAD_DIM
    return (r == c).astype(BF16)


def _segsum(v, seg):
    hi = v.astype(BF16)
    lo = (v - hi.astype(F32)).astype(BF16)
    outs = []
    for c0 in range(0, D_MODEL, 256):
        outs.append(jnp.dot(hi[:, c0:c0 + 256], seg, preferred_element_type=F32)
                    + jnp.dot(lo[:, c0:c0 + 256], seg, preferred_element_type=F32))
    return jnp.concatenate(outs, axis=1)


def _qk_rstd(v, seg):
    return lax.rsqrt(_segsum(v * v, seg) * (1.0 / HEAD_DIM) + NORM_EPS)


def _qknorm_fwd(proj, group, qw, kw, seg, name):
    tm = 256

    def body(q_in, k_in, qw_ref, kw_ref, seg_ref, q_ref, k_ref):
        segv = seg_ref[...]
        q = q_in[...].astype(F32)
        k = k_in[...].astype(F32)
        q_ref[...] = (q * _qk_rstd(q, segv) * qw_ref[...] * HEAD_DIM ** -0.5).astype(BF16)
        k_ref[...] = (k * _qk_rstd(k, segv) * kw_ref[...]).astype(BF16)

    return pl.pallas_call(
        body, name=name, grid=(SEQ // tm,),
        in_specs=[_row_spec(tm, D_MODEL, 3 * group), _row_spec(tm, D_MODEL, 3 * group + 1),
                  _vec_spec(1, D_MODEL), _vec_spec(1, D_MODEL), _vec_spec(256, 256)],
        out_specs=[_row_spec(tm, D_MODEL)] * 2,
        out_shape=[jax.ShapeDtypeStruct((SEQ, D_MODEL), BF16)] * 2,
        compiler_params=_params("parallel"),
    )(proj, proj, qw, kw, seg)


def _attn_masks(b, bpc, dilation, transposed=False):
    keys = ATTN_BLOCK if bpc == 1 else 2 * ATTN_BLOCK
    shape, q_axis = ((keys, ATTN_BLOCK), 1) if transposed else ((ATTN_BLOCK, keys), 0)
    qi = lax.broadcasted_iota(jnp.int32, shape, q_axis)
    kj = lax.broadcasted_iota(jnp.int32, shape, 1 - q_axis)
    if bpc == 1:
        steps = qi - kj
        return (steps * dilation).astype(F32), steps >= 0
    steps = qi + ATTN_BLOCK - kj
    has_prev = (b % bpc) != 0
    valid = (steps >= 0) & (steps <= ATTN_BLOCK) & (has_prev | (kj >= ATTN_BLOCK))
    return (steps * dilation).astype(F32), valid


MASKED = 1e30


def _bias_scratch(bpc):
    return pltpu.VMEM((1 if bpc == 1 else 2, N_HEADS, ATTN_BLOCK, (1 if bpc == 1 else 2) * ATTN_BLOCK), F32)


def _fill_bias(bias_ref, sl_ref, bpc, dilation):
    for variant in range(bias_ref.shape[0]):
        dist, valid = _attn_masks(variant, min(bpc, 2), dilation)
        bias_ref[variant] = jnp.where(valid[None], dist[None] * sl_ref[...], MASKED)


def _step_bias(bias_ref, b, bpc):
    if bpc == 1:
        return bias_ref[0]
    return bias_ref[jnp.where((b % bpc) != 0, 1, 0)]


def _key_tile(prev_ref, cur_ref, cols, bpc):
    if bpc == 1:
        return cur_ref[:, cols]
    return jnp.concatenate([prev_ref[:, cols], cur_ref[:, cols]], axis=0)


ATTN_HEADS_FWD = 16
ATTN_HEADS_BWD = 16
NT_DIMS = (((1,), (1,)), ((), ()))
BATCH_NT_DIMS = (((2,), (2,)), ((0,), (0,)))
BATCH_NN_DIMS = (((2,), (1,)), ((0,), (0,)))
BATCH_TN_DIMS = (((1,), (1,)), ((0,), (0,)))


def _head_stack(tile_of, heads):
    return jnp.stack([tile_of(slice(h * HEAD_DIM, (h + 1) * HEAD_DIM)) for h in range(heads)], axis=0)


def _attn_specs(heads, segment=0):
    width = heads * HEAD_DIM
    off = segment * (D_MODEL // width)
    last = SEQ // ATTN_BLOCK - 1
    cur = pl.BlockSpec((ATTN_BLOCK, width), lambda hg, b: (jnp.minimum(b, last), hg + off))
    prev = pl.BlockSpec((ATTN_BLOCK, width), lambda hg, b: (jnp.clip(b - 1, 0, last), hg + off))
    return cur, prev


def _attn_fwd(q, k, proj, group, slopes, dilation, name):
    bpc = SEQ // dilation // ATTN_BLOCK
    heads = ATTN_HEADS_FWD
    assert heads == N_HEADS
    cur, prev = _attn_specs(heads)
    v_cur, v_prev = _attn_specs(heads, segment=3 * group + 2)

    def body(sl_ref, q_ref, kp_ref, kc_ref, vp_ref, vc_ref, o_ref, lse_ref, bias_ref):
        b = pl.program_id(1)

        @pl.when(b == 0)
        def _():
            _fill_bias(bias_ref, sl_ref, bpc, dilation)

        q3 = _head_stack(lambda cols: q_ref[:, cols], heads)
        k3 = _head_stack(lambda cols: _key_tile(kp_ref, kc_ref, cols, bpc), heads)
        v3 = _head_stack(lambda cols: _key_tile(vp_ref, vc_ref, cols, bpc), heads)
        s = lax.dot_general(q3, k3, BATCH_NT_DIMS, preferred_element_type=F32)
        s = s - _step_bias(bias_ref, b, bpc)
        m = jnp.max(s, axis=-1, keepdims=True)
        p = jnp.exp(s - m)
        l = jnp.sum(p, axis=-1, keepdims=True)
        o3 = lax.dot_general(p.astype(BF16), v3, BATCH_NN_DIMS, preferred_element_type=F32) / l
        lse3 = m + jnp.log(l)
        for h in range(heads):
            o_ref[:, h * HEAD_DIM:(h + 1) * HEAD_DIM] = o3[h].astype(BF16)
        lse_ref[...] = jnp.concatenate([lse3[h] for h in range(heads)]
                                       + [jnp.zeros((ATTN_BLOCK, LANES - heads), F32)], axis=1)

    return pl.pallas_call(
        body, name=name, grid=(N_HEADS // heads, SEQ // ATTN_BLOCK),
        in_specs=[pl.BlockSpec((heads, 1, 1), lambda hg, b: (hg, 0, 0)), cur, prev, cur, v_prev, v_cur],
        out_specs=[cur, pl.BlockSpec((ATTN_BLOCK, LANES), lambda hg, b: (b, 0))],
        out_shape=[jax.ShapeDtypeStruct((SEQ, D_MODEL), BF16), jax.ShapeDtypeStruct((SEQ, LANES), F32)],
        scratch_shapes=[_bias_scratch(bpc)],
        compiler_params=_params("parallel", "arbitrary"),
    )(slopes.reshape(N_HEADS, 1, 1), q, k, k, proj, proj)


def _class_spec(tm, dilation, width=D_MODEL):
    if dilation == 1:
        return _row_spec(tm, width)
    return pl.BlockSpec((dilation, tm // dilation, width), lambda i: (0, i, 0))


def _class_shape(dilation, dtype, width=D_MODEL):
    if dilation == 1:
        return jax.ShapeDtypeStruct((SEQ, width), dtype)
    return jax.ShapeDtypeStruct((dilation, SEQ // dilation, width), dtype)


def _load_natural(in_ref, nat_ref, dilation):
    if dilation == 1:
        return in_ref[...].astype(F32)
    n = nat_ref.shape[1] // dilation
    tiles = in_ref.shape[-1] // LANES
    for r in range(dilation):
        for j in range(tiles):
            nat_ref.at[j][pl.ds(r, n, stride=dilation), :] = in_ref[r, :, j * LANES:(j + 1) * LANES].astype(F32)
    if tiles == 1:
        return nat_ref[0]
    return jnp.concatenate([nat_ref[j] for j in range(tiles)], axis=1)


def _store_classes(out_ref, value, nat_ref, dilation):
    if dilation == 1:
        out_ref[...] = value.astype(out_ref.dtype)
        return
    n = nat_ref.shape[1] // dilation
    tiles = value.shape[-1] // LANES
    for j in range(tiles):
        nat_ref[j] = value[:, j * LANES:(j + 1) * LANES]
    for r in range(dilation):
        for j in range(tiles):
            out_ref[r, :, j * LANES:(j + 1) * LANES] = (
                nat_ref.at[j][pl.ds(r, n, stride=dilation), :].astype(out_ref.dtype))


def _natural_scratch(tm):
    return pltpu.VMEM((D_MODEL // LANES, tm, LANES), F32)


def _head_selector():
    lane_head = lax.broadcasted_iota(jnp.int32, (D_MODEL, LANES), 0) // HEAD_DIM
    head = lax.broadcasted_iota(jnp.int32, (D_MODEL, LANES), 1)
    return (lane_head == head).astype(BF16)


def _dot_split(v, m01, dims):
    hi = v.astype(BF16)
    lo = (v - hi.astype(F32)).astype(BF16)
    return (lax.dot_general(hi, m01, dims, preferred_element_type=F32)
            + lax.dot_general(lo, m01, dims, preferred_element_type=F32))


def _merge_fwd(o_parts, lse_parts, z, sel, name):
    tm = 256
    h_spec = pl.BlockSpec((tm, LANES), lambda i: (i, 0))

    def body(o0, o1, o2, l0, l1, l2, z_ref, sel_ref, u_ref, ut_ref, o_ref, lse_ref, nat):
        ls = [_load_natural(l, nat, d) for l, d in zip((l0, l1, l2), DILATIONS)]
        m = jnp.maximum(jnp.maximum(ls[0], ls[1]), ls[2])
        tot = m + jnp.log(jnp.exp(ls[0] - m) + jnp.exp(ls[1] - m) + jnp.exp(ls[2] - m))
        o = jnp.zeros((tm, D_MODEL), F32)
        for o_in, l, d in zip((o0, o1, o2), ls, DILATIONS):
            weight = _dot_split(jnp.exp(l - tot), sel_ref[...], NT_DIMS)
            o = o + weight * _load_natural(o_in, nat, d)
        zv = z_ref[...].astype(F32)
        u = o * (zv * _sigmoid(zv))
        u_ref[...] = u.astype(BF16)
        ut_ref[...] = u.T.astype(BF16)
        o_ref[...] = o
        lse_ref[...] = tot

    return pl.pallas_call(
        body, name=name, grid=(SEQ // tm,),
        in_specs=[_class_spec(tm, d) for d in DILATIONS] + [_class_spec(tm, d, LANES) for d in DILATIONS]
        + [_row_spec(tm, D_MODEL, B_Z_SEGMENT), _vec_spec(D_MODEL, LANES)],
        out_specs=[_row_spec(tm, D_MODEL), pl.BlockSpec((D_MODEL, tm), lambda i: (0, i)),
                   _row_spec(tm, D_MODEL), h_spec],
        out_shape=[jax.ShapeDtypeStruct((SEQ, D_MODEL), BF16), jax.ShapeDtypeStruct((D_MODEL, SEQ), BF16),
                   jax.ShapeDtypeStruct((SEQ, D_MODEL), F32), jax.ShapeDtypeStruct((SEQ, LANES), F32)],
        scratch_shapes=[_natural_scratch(tm)],
        compiler_params=_params("parallel"),
    )(*o_parts, *lse_parts, z, sel)


def _merge_bwd(dy, w_out, o, lse, z, sel, name):
    tm = 256
    n_d = len(DILATIONS)

    def body(dy_ref, w_ref, o_ref, lse_ref, z_ref, sel_ref, dz_ref, *rest):
        do_refs, delta_refs, lse_refs, nat = rest[:n_d], rest[n_d:2 * n_d], rest[2 * n_d:3 * n_d], rest[-1]
        zv = z_ref[...].astype(F32)
        sz = _sigmoid(zv)
        duv = lax.dot_general(dy_ref[...], w_ref[...], NT_DIMS, preferred_element_type=F32)
        ov = o_ref[...]
        do = duv * (zv * sz)
        dz_ref[...] = (duv * ov * (sz * (1.0 + zv * (1.0 - sz)))).astype(BF16)
        delta = _dot_split(do * ov, sel_ref[...], (((1,), (0,)), ((), ())))
        lv = lse_ref[...]
        for i, d in enumerate(DILATIONS):
            _store_classes(do_refs[i], do, nat, d)
            _store_classes(delta_refs[i], delta, nat, d)
            _store_classes(lse_refs[i], lv, nat, d)

    res = pl.pallas_call(
        body, name=name, grid=(SEQ // tm,),
        in_specs=[_row_spec(tm, D_MODEL), _vec_spec(D_MODEL, D_MODEL), _row_spec(tm, D_MODEL), _row_spec(tm, LANES),
                  _row_spec(tm, D_MODEL, B_Z_SEGMENT), _vec_spec(D_MODEL, LANES)],
        out_specs=[_row_spec(tm, D_MODEL)] + [_class_spec(tm, d) for d in DILATIONS]
        + [_class_spec(tm, d, LANES) for d in DILATIONS] * 2,
        out_shape=[jax.ShapeDtypeStruct((SEQ, D_MODEL), BF16)] + [_class_shape(d, BF16) for d in DILATIONS]
        + [_class_shape(d, F32, LANES) for d in DILATIONS] * 2,
        scratch_shapes=[_natural_scratch(tm)],
        compiler_params=_params("parallel"),
    )(dy, w_out, o, lse, z, sel)
    flat = lambda a: a.reshape(SEQ, a.shape[-1])
    return (res[0], [flat(a) for a in res[1:1 + n_d]], [flat(a) for a in res[1 + n_d:1 + 2 * n_d]],
            [flat(a) for a in res[1 + 2 * n_d:]])


def _attn_bwd(q, k, proj, group, do, lse, delta, slopes, dilation, name):
    bpc = SEQ // dilation // ATTN_BLOCK
    heads = ATTN_HEADS_BWD
    n_blocks = SEQ // ATTN_BLOCK
    carry = bpc > 1
    width = heads * HEAD_DIM
    cur, prev = _attn_specs(heads)
    v_cur, v_prev = _attn_specs(heads, segment=3 * group + 2)
    assert heads == N_HEADS
    per_head = pl.BlockSpec((ATTN_BLOCK, LANES), lambda hg, b: (jnp.minimum(b, n_blocks - 1), 0))
    scale = HEAD_DIM ** -0.5

    def body(sl_ref, q_ref, kp_ref, kc_ref, vp_ref, vc_ref, do_ref, lse_ref, dl_ref,
             dq_ref, dk_ref, dv_ref, *scratch):
        b = pl.program_id(1)
        if carry:
            dk_carry, dv_carry = scratch

            @pl.when(b == n_blocks)
            def _():
                dk_ref[...] = dk_carry[...].astype(BF16)
                dv_ref[...] = dv_carry[...].astype(BF16)

            @pl.when(b < n_blocks)
            def _():
                step(sl_ref, q_ref, kp_ref, kc_ref, vp_ref, vc_ref, do_ref, lse_ref, dl_ref,
                     dq_ref, dk_ref, dv_ref, dk_carry, dv_carry, b)
        else:
            step(sl_ref, q_ref, kp_ref, kc_ref, vp_ref, vc_ref, do_ref, lse_ref, dl_ref,
                 dq_ref, dk_ref, dv_ref, None, None, b)

    def step(sl_ref, q_ref, kp_ref, kc_ref, vp_ref, vc_ref, do_ref, lse_ref, dl_ref,
             dq_ref, dk_ref, dv_ref, dk_carry, dv_carry, b):
        if carry:
            @pl.when(b == 0)
            def _():
                dk_carry[...] = jnp.zeros_like(dk_carry)
                dv_carry[...] = jnp.zeros_like(dv_carry)

        q3 = _head_stack(lambda cols: q_ref[:, cols], heads)
        k3 = _head_stack(lambda cols: _key_tile(kp_ref, kc_ref, cols, bpc), heads)
        v3 = _head_stack(lambda cols: _key_tile(vp_ref, vc_ref, cols, bpc), heads)
        do3 = _head_stack(lambda cols: do_ref[:, cols], heads)
        lse_t = lse_ref[...].T
        dl_t = dl_ref[...].T
        lse3 = jnp.stack([lse_t[h:h + 1, :] for h in range(heads)], axis=0)
        dl3 = jnp.stack([dl_t[h:h + 1, :] for h in range(heads)], axis=0)
        s = lax.dot_general(k3, q3, BATCH_NT_DIMS, preferred_element_type=F32)
        dist, valid = _attn_masks(b, bpc, dilation, transposed=True)
        p = jnp.exp(jnp.where(valid[None], s - dist[None] * sl_ref[...], NEG_INF) - lse3)
        dp = lax.dot_general(v3, do3, BATCH_NT_DIMS, preferred_element_type=F32)
        ds = (p * (dp - dl3)).astype(BF16)
        dq3 = lax.dot_general(ds, k3, BATCH_TN_DIMS, preferred_element_type=F32) * scale
        dk3 = lax.dot_general(ds, q3, BATCH_NN_DIMS, preferred_element_type=F32)
        dv3 = lax.dot_general(p.astype(BF16), do3, BATCH_NN_DIMS, preferred_element_type=F32)
        for h in range(heads):
            cols = slice(h * HEAD_DIM, (h + 1) * HEAD_DIM)
            dq_ref[:, cols] = dq3[h].astype(BF16)
            if carry:
                dk_ref[:, cols] = (dk_carry[:, cols] + dk3[h, :ATTN_BLOCK]).astype(BF16)
                dv_ref[:, cols] = (dv_carry[:, cols] + dv3[h, :ATTN_BLOCK]).astype(BF16)
                dk_carry[:, cols] = dk3[h, ATTN_BLOCK:]
                dv_carry[:, cols] = dv3[h, ATTN_BLOCK:]
            else:
                dk_ref[:, cols] = dk3[h].astype(BF16)
                dv_ref[:, cols] = dv3[h].astype(BF16)

    kv_out = prev if carry else cur
    return pl.pallas_call(
        body, name=name, grid=(N_HEADS // heads, n_blocks + (1 if carry else 0)),
        in_specs=[pl.BlockSpec((heads, 1, 1), lambda hg, b: (hg, 0, 0)), cur, prev, cur, v_prev, v_cur,
                  cur, per_head, per_head],
        out_specs=[cur, kv_out, kv_out],
        out_shape=[jax.ShapeDtypeStruct((SEQ, D_MODEL), BF16)] * 3,
        scratch_shapes=[pltpu.VMEM((ATTN_BLOCK, width), F32)] * 2 if carry else [],
        compiler_params=_params("parallel", "arbitrary"),
    )(slopes.reshape(N_HEADS, 1, 1), q, k, k, proj, proj, do, lse, delta)


def _qknorm_bwd(proj, group, qw, kw, seg, dq, dk, dv, name):
    tm = 256

    def body(q_in, k_in, qw_ref, kw_ref, seg_ref, dq_ref, dk_ref, dv_ref, dproj_ref, sums_ref):
        segv = seg_ref[...]
        sums = []
        for part, (raw_ref, w_ref, dn_ref) in enumerate(((q_in, qw_ref, dq_ref), (k_in, kw_ref, dk_ref))):
            raw = raw_ref[...].astype(F32)
            dn = dn_ref[...].astype(F32)
            r = _qk_rstd(raw, segv)
            gq = dn * w_ref[...]
            draw = r * gq - raw * (r * r * r) * (_segsum(raw * gq, segv) * (1.0 / HEAD_DIM))
            dproj_ref[:, part * D_MODEL:(part + 1) * D_MODEL] = draw.astype(BF16)
            sums.append(jnp.sum(dn * raw * r, axis=0, keepdims=True))
        dproj_ref[:, 2 * D_MODEL:] = dv_ref[...]

        @pl.when(pl.program_id(0) == 0)
        def _():
            sums_ref[...] = jnp.zeros_like(sums_ref)

        sums_ref[...] += jnp.concatenate(sums + [jnp.zeros((6, D_MODEL), F32)], axis=0)

    return pl.pallas_call(
        body, name=name, grid=(SEQ // tm,),
        in_specs=[_row_spec(tm, D_MODEL, 3 * group), _row_spec(tm, D_MODEL, 3 * group + 1),
                  _vec_spec(1, D_MODEL), _vec_spec(1, D_MODEL), _vec_spec(256, 256)] + [_row_spec(tm, D_MODEL)] * 3,
        out_specs=[_row_spec(tm, 3 * D_MODEL), _vec_spec(8, D_MODEL)],
        out_shape=[jax.ShapeDtypeStruct((SEQ, 3 * D_MODEL), BF16), jax.ShapeDtypeStruct((8, D_MODEL), F32)],
        compiler_params=_params("arbitrary"),
    )(proj, proj, qw, kw, seg, dq, dk, dv)


B_TN = 512
B_GROUP_TILES = 3 * D_MODEL // B_TN
B_Z_TILE0 = 3 * B_GROUP_TILES
B_Z_TILES = D_MODEL // B_TN
B_TILES = B_Z_TILE0 + B_Z_TILES
B_Z_SEGMENT = 3 * len(DILATIONS)


def _local_step(x, target, mods, norm_g, conv_w, conv_b, ln_g, ln_b, q_norm, k_norm, chip, own_wb_in,
                weights_a, weights_b, forward_weights_b, send_grads_b, forward_grads_b, send_grads_a):
    row = lambda a, i: a[i:i + 1]
    shift0, scale0, gate0 = row(mods[0], 0), row(mods[0], 1), row(mods[0], 2)
    shift1, scale1, gate1 = row(mods[1], 0), row(mods[1], 1), row(mods[1], 2)
    g0, g1 = row(norm_g, 0), row(norm_g, 1)
    seg = _seg_matrix()
    slopes = jnp.exp2(-8.0 * jnp.arange(1, N_HEADS + 1, dtype=F32) / N_HEADS)
    qw = [jnp.tile(q_norm[g:g + 1], (1, N_HEADS)) for g in range(3)]
    kw = [jnp.tile(k_norm[g:g + 1], (1, N_HEADS)) for g in range(3)]

    h0, h0t = _normmod_fwd(x, g0, scale0, shift0, "prenorm0")
    wa_in, wa_out = weights_a(h0)
    ja, _, nsa = wa_in.shape
    proj_a = _mm(h0, wa_in, tn=nsa, tile0=0, n_tiles=ja, out_dtype=F32, name="a_in")
    u5, u5t, u2 = _conv_fwd(proj_a, conv_w, conv_b, ln_g, ln_b, "a_conv")
    x1, y_a, h1t, h1c = _out_a(u5, wa_out, x, gate0, g1, scale1, shift1, "a_out")

    own_tiles = B_TILES // N_CHIPS
    step = jnp.arange(B_TILES, dtype=jnp.int32)
    tiles = (own_tiles * chip + step) % B_TILES
    own_ids = jnp.stack([step[:own_tiles], tiles[:own_tiles]])
    rest_ids = jnp.stack([tiles[own_tiles:], tiles[own_tiles:]])
    proj_b = _b_in_tiles(h1c, own_wb_in, own_ids, own_tiles, "b_in_own")
    forward_weights_b(proj_b)
    wb_in, wb_out = weights_b(proj_b)
    jb, _, nsb = wb_in.shape
    proj_b = _b_in_tiles(h1c, wb_in, rest_ids, B_TILES - own_tiles, "b_in_rest", prev=proj_b)
    h1 = h1c[0]
    qkv, o_parts, lse_parts = [], [], []
    for g, d in enumerate(DILATIONS):
        qn, kn = _qknorm_fwd(proj_b, g, qw[g], kw[g], seg, f"b_qknorm_g{g}")
        og, lg = _attn_fwd(qn, kn, proj_b, g, slopes, d, f"b_attn_g{g}")
        qkv.append((qn, kn))
        o_parts.append(og if d == 1 else og.reshape(d, SEQ // d, D_MODEL))
        lse_parts.append(lg if d == 1 else lg.reshape(d, SEQ // d, LANES))
    sel = _head_selector()
    u_b, u_bt, o_b, lse_b = _merge_fwd(o_parts, lse_parts, proj_b, sel, "b_merge")
    e, dy_b, sums_loss = _out_b_loss(u_b, wb_out, x1, gate1, target, "b_out_loss")

    dwb_out = _mm(u_bt, dy_b, tn=D_MODEL, tile0=0, n_tiles=1, out_dtype=BF16, name="b_dwout")
    dz_b, do_c, delta_c, lse_c = _merge_bwd(dy_b, wb_out, o_b, lse_b, proj_b, sel, "b_merge_bwd")
    dwb_in = _mm(h1t, dz_b, tn=B_TN, tile0=B_Z_TILE0, n_tiles=B_Z_TILES, out_dtype=BF16, name="b_dwin_z",
                 out3d=(jb, nsb))
    dh1_parts = [_mm_nt(dz_b, wb_in, tn=B_TN, tile0=B_Z_TILE0, n_tiles=B_Z_TILES, name="b_dh_z")]
    qk_sums = []
    for g, d in enumerate(DILATIONS):
        qn, kn = qkv[g]
        dq, dk, dv = _attn_bwd(qn, kn, proj_b, g, do_c[g], lse_c[g], delta_c[g], slopes, d, f"b_attn_bwd_g{g}")
        dproj, sums_qk = _qknorm_bwd(proj_b, g, qw[g], kw[g], seg, dq, dk, dv, f"b_qknorm_bwd_g{g}")
        qk_sums.append(sums_qk)
        dwb_in = _mm(h1t if d == 1 else h1c[g], dproj, tn=B_TN, tile0=g * B_GROUP_TILES, n_tiles=B_GROUP_TILES,
                     out_dtype=BF16, name=f"b_dwin_g{g}", out3d=(jb, nsb), prev=dwb_in, transpose_lhs=d != 1)
        dh = _mm_nt(dproj, wb_in, tn=B_TN, tile0=g * B_GROUP_TILES, n_tiles=B_GROUP_TILES, name=f"b_dh_g{g}")
        dh1_parts.append(dh)
    token = send_grads_b(dwb_in, dwb_out)
    dx1, sums_n1, dy_a = _normmod_bwd(x1, g1, scale1 + token[0:1, 0:1], dh1_parts, e, "prenorm1_bwd",
                                      part_dilations=(1,) + DILATIONS, gated=(gate0, y_a))
    token = forward_grads_b(dx1)

    dwa_out = _mm(u5t, dy_a, tn=D_MODEL, tile0=0, n_tiles=1, out_dtype=BF16, name="a_dwout")
    du2, dz_a, sums_ln = _conv_bwd_pointwise(dy_a, wa_out, proj_a, u2, ln_g + token[0:1, 0:1], ln_b,
                                             "a_conv_bwd_pw")
    dproj_a, dconv_w = _conv_bwd_taps(du2, dz_a, proj_a, conv_w, "a_conv_bwd_taps")
    dwa_in = _mm(h0t, dproj_a, tn=nsa, tile0=0, n_tiles=ja, out_dtype=BF16, name="a_dwin", out3d=(ja, nsa))
    token = send_grads_a(dwa_in, dwa_out)
    dh0 = _mm_nt(dproj_a, wa_in, tn=nsa, tile0=0, n_tiles=ja, name="a_dh", after=token)
    grad_x, sums_n0 = _normmod_bwd(x, g0, scale0, [dh0], dx1, "prenorm0_bwd")

    small = dict(
        dnorm_g=jnp.concatenate([sums_n0[0:1], sums_n1[0:1]], axis=0),
        dmod0=jnp.concatenate([sums_n0[2:3], sums_n0[1:2], sums_n1[3:4]], axis=0),
        dmod1=jnp.concatenate([sums_n1[2:3], sums_n1[1:2], sums_loss[0:1]], axis=0),
        dln_g=sums_ln[0:1], dln_b=sums_ln[1:2], dconv_b=sums_ln[2:3],
        dconv_w=dconv_w[:CONV_WIDTH],
        dq_norm=jnp.concatenate([s[0:1] for s in qk_sums], axis=0),
        dk_norm=jnp.concatenate([s[1:2] for s in qk_sums], axis=0),
        loss_cols=sums_loss[1:2],
    )
    return grad_x, small


def _adamw(w, g, m, v, name, after=None, copy_grad=False):
    rows, cols = w.shape
    tr = rows if rows <= 128 else 128
    c1 = 1.0 / (1.0 - ADAM_B1 ** ADAM_STEP)
    c2 = 1.0 / (1.0 - ADAM_B2 ** ADAM_STEP)
    extra = [] if after is None else [after]
    n_out = 4 if copy_grad else 3

    def body(w_ref, g_ref, m_ref, v_ref, *rest):
        d_ref, mo_ref, vo_ref = rest[len(extra):len(extra) + 3]
        gv = g_ref[...]
        if copy_grad:
            rest[-1][...] = gv
        mn = ADAM_B1 * m_ref[...] + (1.0 - ADAM_B1) * gv
        vn = ADAM_B2 * v_ref[...] + (1.0 - ADAM_B2) * (gv * gv)
        mo_ref[...] = mn
        vo_ref[...] = vn
        d_ref[...] = -ADAM_LR * ((mn * c1) / (jnp.sqrt(vn * c2) + ADAM_EPS) + ADAM_WD * w_ref[...])

    spec = pl.BlockSpec((tr, cols), lambda i: (i, 0))
    return pl.pallas_call(
        body, name=name, grid=(rows // tr,),
        in_specs=[spec] * 4 + [pl.BlockSpec(memory_space=pl.ANY)] * len(extra), out_specs=[spec] * n_out,
        out_shape=[jax.ShapeDtypeStruct((rows, cols), F32)] * n_out,
        compiler_params=_params("parallel"),
    )(w, g, m, v, *extra)


def _cast_into_slot(w, chip_idx, name, keep_own=False, after=None):
    rows, cols = w.shape
    tr = 256
    extra = [] if after is None else [after]

    def body(ch_ref, w_ref, *rest):
        wb = w_ref[...].astype(BF16)
        for o_ref in rest[len(extra):]:
            o_ref[...] = wb

    slot_spec = pl.BlockSpec((None, tr, cols), lambda i, ch: (ch[0], i, 0))
    own_spec = pl.BlockSpec((None, tr, cols), lambda i, ch: (0, i, 0))
    res = pl.pallas_call(
        body, name=name,
        grid_spec=pltpu.PrefetchScalarGridSpec(
            num_scalar_prefetch=1, grid=(rows // tr,),
            in_specs=[pl.BlockSpec((tr, cols), lambda i, ch: (i, 0))] + [pl.BlockSpec(memory_space=pl.ANY)] * len(extra),
            out_specs=[slot_spec, own_spec] if keep_own else [slot_spec]),
        out_shape=[jax.ShapeDtypeStruct((N_CHIPS, rows, cols), BF16)]
        + ([jax.ShapeDtypeStruct((1, rows, cols), BF16)] if keep_own else []),
        compiler_params=_params("parallel"),
    )(chip_idx, w, *extra)
    return tuple(res) if keep_own else res[0]


def _position():
    x, y, c = lax.axis_index("x"), lax.axis_index("y"), lax.axis_index("c")
    return x, y, c


def _xor_peer(x, y, c, k):
    return (x ^ ((k >> 2) & 1), y ^ ((k >> 1) & 1), c ^ (k & 1))


def _chip_peer(x, y, k):
    return (x ^ ((k >> 1) & 1), y ^ (k & 1))


def _ada_forward(c_row, ada_w, ada_b, conv_w):
    ns = ada_w.shape[2]
    cw = conv_w.shape[1]

    def body(c_ref, w_ref, b_ref, cv_ref, mod_ref, sc_ref, cvo_ref,
             c_all, mp, parts, cv_parts, send1, recv1, send2, recv2, send3, recv3):
        x, y, c = _position()
        me = 4 * x + 2 * y + c
        chip = 2 * x + y

        def c_copy(k):
            return pltpu.make_async_remote_copy(
                src_ref=c_all.at[me], dst_ref=c_all.at[me], send_sem=send1.at[k - 1], recv_sem=recv1.at[k - 1],
                device_id=_xor_peer(x, y, c, k), device_id_type=MESH)

        def cv_copy(k):
            px, py = _chip_peer(x, y, k)
            return pltpu.make_async_remote_copy(
                src_ref=cv_parts.at[chip], dst_ref=cv_parts.at[chip], send_sem=send3.at[k - 1],
                recv_sem=recv3.at[k - 1], device_id=(px, py, c), device_id_type=MESH)

        c_all[me] = c_ref[...]
        cv_parts[chip] = cv_ref[...]
        for k in range(1, N_DEV):
            c_copy(k).start()
        for k in range(1, N_CHIPS):
            cv_copy(k).start()
        for k in range(1, N_DEV):
            c_copy(k).wait_recv()
        cv = jnp.concatenate([c_all[i] for i in range(N_DEV)], axis=0)
        sc = cv * _sigmoid(cv)
        sc_ref[...] = sc
        for l in range(2):
            res = jnp.dot(sc, w_ref[l], preferred_element_type=F32, precision=lax.Precision.HIGHEST)
            for i in range(N_DEV):
                mp[i, l:l + 1, :] = res[i:i + 1, :]

        def mod_copy(k):
            px, py = _chip_peer(x, y, k)
            return pltpu.make_async_remote_copy(
                src_ref=mp.at[4 * px + 2 * py + c], dst_ref=parts.at[chip], send_sem=send2.at[k - 1],
                recv_sem=recv2.at[k - 1], device_id=(px, py, c), device_id_type=MESH)

        for k in range(1, N_CHIPS):
            mod_copy(k).start()
        parts[chip] = mp[me]
        for k in range(1, N_CHIPS):
            mod_copy(k).wait_recv()
            cv_copy(k).wait_recv()
        mod_ref[...] = jnp.concatenate([parts[j] for j in range(N_CHIPS)], axis=1) + b_ref[...]
        cvo_ref[...] = jnp.concatenate([cv_parts[j] for j in range(N_CHIPS)], axis=1)
        for k in range(1, N_DEV):
            c_copy(k).wait_send()
        for k in range(1, N_CHIPS):
            mod_copy(k).wait_send()
            cv_copy(k).wait_send()

    vm = pl.BlockSpec(memory_space=pltpu.VMEM)
    return pl.pallas_call(
        body, name="ada_forward",
        in_specs=[vm] * 4, out_specs=[vm] * 3,
        out_shape=[jax.ShapeDtypeStruct((2, 3 * D_MODEL), F32), jax.ShapeDtypeStruct((N_DEV, D_MODEL), F32),
                   jax.ShapeDtypeStruct((CONV_WIDTH, N_CHIPS * cw), F32)],
        scratch_shapes=[pltpu.VMEM((N_DEV, 1, D_MODEL), F32), pltpu.VMEM((N_DEV, 2, ns), F32),
                        pltpu.VMEM((N_CHIPS, 2, ns), F32), pltpu.VMEM((N_CHIPS, CONV_WIDTH, cw), F32),
                        pltpu.SemaphoreType.DMA((N_DEV - 1,)), pltpu.SemaphoreType.DMA((N_DEV - 1,)),
                        pltpu.SemaphoreType.DMA((N_CHIPS - 1,)), pltpu.SemaphoreType.DMA((N_CHIPS - 1,)),
                        pltpu.SemaphoreType.DMA((N_CHIPS - 1,)), pltpu.SemaphoreType.DMA((N_CHIPS - 1,))],
        compiler_params=pltpu.CompilerParams(vmem_limit_bytes=VMEM_LIMIT_BYTES),
    )(c_row, ada_w, ada_b, conv_w)


HBM_SPEC = pl.BlockSpec(memory_space=pltpu.HBM)
ANY_SPEC = pl.BlockSpec(memory_space=pl.ANY)
SEM_SPEC = pl.BlockSpec(memory_space=pltpu.SEMAPHORE)
SPLIT_PARAMS = dict(compiler_params=pltpu.CompilerParams(has_side_effects=pltpu.SideEffectType.DATAFLOW_SIDE_EFFECTING))
TOKEN = jax.ShapeDtypeStruct((8, 128), F32)


def _hbm(arrays):
    return [pltpu.with_memory_space_constraint(a, pltpu.HBM) for a in arrays]


def _hbm_like(arrays):
    return [pltpu.HBM(a.shape, a.dtype) for a in arrays]


def _gather_start(lands, after, name):
    n = len(lands)

    def body(*refs):
        ins = refs[:n]
        send, recv = refs[n + 1], refs[n + 2]
        x, y, c = _position()
        chip = 2 * x + y
        for t in range(n):
            rh = ins[t].shape[1] // 2
            for k in range(1, N_CHIPS):
                px, py = _chip_peer(x, y, k)
                block = ins[t].at[chip, pl.ds(c * rh, rh)]
                pltpu.make_async_remote_copy(
                    src_ref=block, dst_ref=block, send_sem=send.at[3 * t + k - 1], recv_sem=recv.at[3 * t + k - 1],
                    device_id=(px, py, c), device_id_type=MESH).start()
        refs[-1][...] = jnp.zeros(TOKEN.shape, F32)

    res = pl.pallas_call(
        body, name=name, in_specs=[HBM_SPEC] * n + [ANY_SPEC],
        out_specs=(SEM_SPEC, SEM_SPEC, *[HBM_SPEC] * n, pl.BlockSpec(memory_space=pltpu.VMEM)),
        out_shape=(pltpu.SemaphoreType.DMA((3 * n,)), pltpu.SemaphoreType.DMA((3 * n,)), *_hbm_like(lands), TOKEN),
        input_output_aliases={t: 2 + t for t in range(n)}, **SPLIT_PARAMS,
    )(*_hbm(lands), after)
    return res[0], res[1], list(res[2:2 + n]), res[-1]


def _gather_forward(send, recv, lands, after, name):
    n = len(lands)

    def body(*refs):
        ins = refs[:n]
        send1, recv1 = refs[n], refs[n + 1]
        send2, recv2 = refs[n + 3], refs[n + 4]
        x, y, c = _position()
        chip = 2 * x + y
        for t in range(n):
            rh = ins[t].shape[1] // 2
            half = pl.ds(c * rh, rh)
            for k in range(1, N_CHIPS):
                px, py = _chip_peer(x, y, k)
                s = 3 * t + k - 1
                got = ins[t].at[2 * px + py, half]
                cp = pltpu.make_async_remote_copy(
                    src_ref=ins[t].at[chip, half], dst_ref=got, send_sem=send1.at[s], recv_sem=recv1.at[s],
                    device_id=(px, py, c), device_id_type=MESH)
                cp.wait_send()
                cp.wait_recv()
                pltpu.make_async_remote_copy(
                    src_ref=got, dst_ref=got, send_sem=send2.at[s], recv_sem=recv2.at[s],
                    device_id=(x, y, 1 - c), device_id_type=MESH).start()
        refs[-1][...] = jnp.zeros(TOKEN.shape, F32)

    res = pl.pallas_call(
        body, name=name, in_specs=[HBM_SPEC] * n + [SEM_SPEC, SEM_SPEC, ANY_SPEC],
        out_specs=(SEM_SPEC, SEM_SPEC, *[HBM_SPEC] * n, pl.BlockSpec(memory_space=pltpu.VMEM)),
        out_shape=(pltpu.SemaphoreType.DMA((3 * n,)), pltpu.SemaphoreType.DMA((3 * n,)), *_hbm_like(lands), TOKEN),
        input_output_aliases={t: 2 + t for t in range(n)}, **SPLIT_PARAMS,
    )(*lands, send, recv, after)
    return res[0], res[1], list(res[2:2 + n]), res[-1]


def _gather_wait(send, recv, lands, after, name):
    n = len(lands)

    def body(*refs):
        ins = refs[:n]
        send_ref, recv_ref = refs[n], refs[n + 1]
        x, y, c = _position()
        for t in range(n):
            rh = ins[t].shape[1] // 2
            for k in range(1, N_CHIPS):
                px, py = _chip_peer(x, y, k)
                cp = pltpu.make_async_remote_copy(
                    src_ref=ins[t].at[2 * px + py, pl.ds(c * rh, rh)],
                    dst_ref=ins[t].at[2 * px + py, pl.ds((1 - c) * rh, rh)], send_sem=send_ref.at[3 * t + k - 1],
                    recv_sem=recv_ref.at[3 * t + k - 1], device_id=(x, y, 1 - c), device_id_type=MESH)
                cp.wait_send()
                cp.wait_recv()

    res = pl.pallas_call(
        body, name=name, in_specs=[HBM_SPEC] * n + [SEM_SPEC, SEM_SPEC, ANY_SPEC], out_specs=[HBM_SPEC] * n,
        out_shape=_hbm_like(lands), input_output_aliases={t: t for t in range(n)}, **SPLIT_PARAMS,
    )(*lands, send, recv, after)
    return list(res)


def _split_start(name, arrays, n_sems, after, issue):
    m = len(arrays)

    def body(*refs):
        issue(refs[:m], refs[m + 1], refs[m + 2])
        refs[-1][...] = jnp.zeros(TOKEN.shape, F32)

    res = pl.pallas_call(
        body, name=name, in_specs=[HBM_SPEC] * m + [ANY_SPEC],
        out_specs=(SEM_SPEC, SEM_SPEC, *[HBM_SPEC] * m, pl.BlockSpec(memory_space=pltpu.VMEM)),
        out_shape=(pltpu.SemaphoreType.DMA((n_sems,)), pltpu.SemaphoreType.DMA((n_sems,)), *_hbm_like(arrays), TOKEN),
        input_output_aliases={t: 2 + t for t in range(m)}, **SPLIT_PARAMS,
    )(*_hbm(arrays), after)
    return res[0], res[1], list(res[2:2 + m]), res[-1]


def _split_wait(name, arrays, send, recv, after, await_all):
    m = len(arrays)

    def body(*refs):
        await_all(refs[:m], refs[m], refs[m + 1])

    res = pl.pallas_call(
        body, name=name, in_specs=[HBM_SPEC] * m + [SEM_SPEC, SEM_SPEC, ANY_SPEC], out_specs=[HBM_SPEC] * m,
        out_shape=_hbm_like(arrays), input_output_aliases={t: t for t in range(m)}, **SPLIT_PARAMS,
    )(*arrays, send, recv, after)
    return list(res)


def _sibling_copies(refs, send, recv, n):
    x, y, c = _position()
    cps = []
    for t in range(n):
        rh = refs[t].shape[1] // 2
        cps.append(pltpu.make_async_remote_copy(
            src_ref=refs[t].at[pl.ds(0, N_CHIPS), pl.ds((1 - c) * rh, rh)], dst_ref=refs[n + t],
            send_sem=send.at[t], recv_sem=recv.at[t], device_id=(x, y, 1 - c), device_id_type=MESH))
    return cps


def _reduce_sibling_start(grads, after, name):
    n = len(grads)
    lands = [lax.empty((N_CHIPS, g.shape[1] // 2, g.shape[2]), BF16) for g in grads]

    def issue(refs, send, recv):
        for cp in _sibling_copies(refs, send, recv, n):
            cp.start()

    return _split_start(name, list(grads) + lands, n, after, issue)


def _reduce_sibling_wait(send, recv, arrays, after, name):
    n = len(arrays) // 2

    def await_all(refs, send_ref, recv_ref):
        for cp in _sibling_copies(refs, send_ref, recv_ref, n):
            cp.wait_send()
            cp.wait_recv()

    res = _split_wait(name, arrays, send, recv, after, await_all)
    return res[:n], res[n:]


def _add_sibling_half(grad, got, dev_idx, name):
    j, r, cols = grad.shape
    rh = r // 2
    tr = rh
    nb = rh // tr

    def body(idx_ref, g_ref, got_ref, out_ref):
        out_ref[...] = (g_ref[...].astype(F32) + got_ref[...].astype(F32)).astype(BF16)

    return pl.pallas_call(
        body, name=name,
        grid_spec=pltpu.PrefetchScalarGridSpec(
            num_scalar_prefetch=1, grid=(j, nb),
            in_specs=[pl.BlockSpec((None, tr, cols), lambda jj, i, idx: (jj, idx[2] * nb + i, 0)),
                      pl.BlockSpec((None, tr, cols), lambda jj, i, idx: (jj, i, 0))],
            out_specs=pl.BlockSpec((None, tr, cols), lambda jj, i, idx: (jj, i, 0))),
        out_shape=jax.ShapeDtypeStruct((j, rh, cols), BF16),
        compiler_params=_params("parallel", "parallel"),
    )(dev_idx, grad, got)


def _chip_copies(refs, send, recv, n, receiving):
    x, y, c = _position()
    chip = 2 * x + y
    cps = []
    for t in range(n):
        for k in range(1, N_CHIPS):
            px, py = _chip_peer(x, y, k)
            cps.append(pltpu.make_async_remote_copy(
                src_ref=refs[t].at[2 * px + py], dst_ref=refs[n + t].at[2 * px + py if receiving else chip],
                send_sem=send.at[3 * t + k - 1], recv_sem=recv.at[3 * t + k - 1],
                device_id=(px, py, c), device_id_type=MESH))
    return cps


def _reduce_chips_start(partials, after, name):
    n = len(partials)
    lands = [lax.empty(p.shape, BF16) for p in partials]

    def issue(refs, send, recv):
        for cp in _chip_copies(refs, send, recv, n, False):
            cp.start()

    return _split_start(name, list(partials) + lands, 3 * n, after, issue)


def _reduce_chips_wait(send, recv, arrays, after, name):
    n = len(arrays) // 2

    def await_all(refs, send_ref, recv_ref):
        for cp in _chip_copies(refs, send_ref, recv_ref, n, True):
            cp.wait_send()
            cp.wait_recv()

    res = _split_wait(name, arrays, send, recv, after, await_all)
    return res[:n], res[n:]


def _sum_partials(land, partial, dev_idx, name):
    _, rh, cols = land.shape
    tr = min(rh, 256)
    nb = rh // tr

    def body(idx_ref, l_ref, p_ref, o_ref):
        chip = idx_ref[1]
        acc = jnp.where(chip == 0, p_ref[...], l_ref[0]).astype(F32)
        for s in range(1, N_CHIPS):
            acc = acc + jnp.where(chip == s, p_ref[...], l_ref[s]).astype(F32)
        o_ref[...] = acc

    return pl.pallas_call(
        body, name=name,
        grid_spec=pltpu.PrefetchScalarGridSpec(
            num_scalar_prefetch=1, grid=(nb,),
            in_specs=[pl.BlockSpec((N_CHIPS, tr, cols), lambda i, idx: (0, i, 0)),
                      pl.BlockSpec((None, tr, cols), lambda i, idx: (idx[1], i, 0))],
            out_specs=pl.BlockSpec((tr, cols), lambda i, idx: (idx[2] * nb + i, 0))),
        out_shape=jax.ShapeDtypeStruct((2 * rh, cols), F32), compiler_params=_params("parallel"),
    )(dev_idx, land, partial)


def _half_copies(refs, send, recv, receiving):
    x, y, c = _position()
    cps = []
    for t, ref in enumerate(refs):
        rh = ref.shape[0] // 2
        cps.append(pltpu.make_async_remote_copy(
            src_ref=ref.at[pl.ds(c * rh, rh)], dst_ref=ref.at[pl.ds(((1 - c) if receiving else c) * rh, rh)],
            send_sem=send.at[t], recv_sem=recv.at[t], device_id=(x, y, 1 - c), device_id_type=MESH))
    return cps


def _share_halves_start(totals, after, name):
    def issue(refs, send, recv):
        for cp in _half_copies(refs, send, recv, False):
            cp.start()

    return _split_start(name, list(totals), len(totals), after, issue)


def _share_halves_wait(send, recv, totals, after, name):
    def await_all(refs, send_ref, recv_ref):
        for cp in _half_copies(refs, send_ref, recv_ref, True):
            cp.wait_send()
            cp.wait_recv()

    return _split_wait(name, totals, send, recv, after, await_all)


SMALL_ROWS = 56


def _small_copies(refs, send, recv, receiving):
    x, y, c = _position()
    me = 4 * x + 2 * y + c
    cps = []
    for k in range(1, N_DEV):
        px, py, pc = _xor_peer(x, y, c, k)
        cps.append(pltpu.make_async_remote_copy(
            src_ref=refs[0], dst_ref=refs[1].at[4 * px + 2 * py + pc if receiving else me],
            send_sem=send.at[k - 1], recv_sem=recv.at[k - 1], device_id=(px, py, pc), device_id_type=MESH))
    return cps


def _small_gather_start(packed, after):
    land = lax.empty((N_DEV,) + packed.shape, F32)

    def issue(refs, send, recv):
        for cp in _small_copies(refs, send, recv, False):
            cp.start()

    return _split_start("small_gather_start", [packed, land], N_DEV - 1, after, issue)


def _small_gather_wait(send, recv, arrays, after):
    def await_all(refs, send_ref, recv_ref):
        for cp in _small_copies(refs, send_ref, recv_ref, True):
            cp.wait_send()
            cp.wait_recv()

    return _split_wait("small_gather_wait", arrays, send, recv, after, await_all)


def _reduce_small(packed, land, silu_c):
    ns = 3 * D_MODEL // N_CHIPS

    def body(p_ref, land_ref, sc_ref, tot_ref, gw_ref, loss_ref, qk_ref, allp):
        x, y, c = _position()
        me = 4 * x + 2 * y + c
        chip = 2 * x + y
        for i in range(N_DEV):
            allp[i] = jnp.where(me == i, p_ref[...], land_ref[i])
        tot = allp[0]
        for i in range(1, N_DEV):
            tot = tot + allp[i]
        tot_ref[...] = tot
        loss_ref[...] = jnp.sum(tot[11:12, :], axis=1, keepdims=True) * (0.5 / D_MODEL)
        fold = tot[5:11, 0:HEAD_DIM]
        for h in range(1, N_HEADS):
            fold = fold + tot[5:11, h * HEAD_DIM:(h + 1) * HEAD_DIM]
        qk_ref[...] = jnp.concatenate([fold, jnp.zeros((2, HEAD_DIM), F32)], axis=0)
        sct = sc_ref[...].T
        rc = 64
        for l in range(2):
            dms = [allp[i, pl.ds(12 + 4 * l + chip, 1), :][:, :ns] for i in range(N_DEV)]
            for r0 in range(0, D_MODEL, rc):
                acc = sct[r0:r0 + rc, 0:1] * dms[0]
                for i in range(1, N_DEV):
                    acc = acc + sct[r0:r0 + rc, i:i + 1] * dms[i]
                gw_ref[l, r0:r0 + rc, :] = acc

    vm = pl.BlockSpec(memory_space=pltpu.VMEM)
    return pl.pallas_call(
        body, name="reduce_small", in_specs=[vm, vm, vm], out_specs=[vm] * 4,
        out_shape=[jax.ShapeDtypeStruct((SMALL_ROWS, D_MODEL), F32), jax.ShapeDtypeStruct((2, D_MODEL, ns), F32),
                   jax.ShapeDtypeStruct((1, 1), F32), jax.ShapeDtypeStruct((8, HEAD_DIM), F32)],
        scratch_shapes=[pltpu.VMEM((N_DEV, SMALL_ROWS, D_MODEL), F32)],
        compiler_params=pltpu.CompilerParams(vmem_limit_bytes=VMEM_LIMIT_BYTES),
    )(packed, land, silu_c)


def kernel(x, c, norm_g, ada_w, ada_b, a_w_in, a_conv_w, a_conv_b, a_ln_g, a_ln_b, a_w_out, b_w_in, b_q_norm, b_k_norm, b_w_out, loss_target, m_norm_g, m_ada_w, m_ada_b, m_a_w_in, m_a_conv_w, m_a_conv_b, m_a_ln_g, m_a_ln_b, m_a_w_out, m_b_w_in, m_b_q_norm, m_b_k_norm, m_b_w_out, v_norm_g, v_ada_w, v_ada_b, v_a_w_in, v_a_conv_w, v_a_conv_b, v_a_ln_g, v_a_ln_b, v_a_w_out, v_b_w_in, v_b_q_norm, v_b_k_norm, v_b_w_out):
    chip = 2 * lax.axis_index("x") + lax.axis_index("y")
    core = lax.axis_index("c")
    chip_idx = chip.astype(jnp.int32).reshape(1)
    dev_idx = jnp.stack([2 * chip + core, chip, core]).astype(jnp.int32)

    mods, silu_c, conv_w_full = _ada_forward(c, ada_w, ada_b, a_conv_w[0])
    lands_a = [_cast_into_slot(a_w_in[0], chip_idx, "cast_a_w_in"), _cast_into_slot(a_w_out[0], chip_idx, "cast_a_w_out")]
    send_a, recv_a, lands_a, token_a = _gather_start(lands_a, mods, "gather_start_a")
    land_b_in, own_wb_in = _cast_into_slot(b_w_in[0], chip_idx, "cast_b_w_in", keep_own=True, after=token_a)
    lands_b = [land_b_in, _cast_into_slot(b_w_out[0], chip_idx, "cast_b_w_out", after=token_a)]
    send_b, recv_b, lands_b, token_b = _gather_start(lands_b, token_a, "gather_start_b")
    mods = mods + token_b[0:2, 0:1]

    def weights_a(after):
        send, recv, lands, _ = _gather_forward(send_a, recv_a, lands_a, after, "gather_forward_a")
        w_in, w_out = _gather_wait(send, recv, lands, after, "gather_wait_a")
        return w_in, w_out.reshape(D_MODEL, D_MODEL)

    forwarded_b = []

    def weights_b(after):
        send, recv, lands, _ = forwarded_b
        w_in, w_out = _gather_wait(send, recv, lands, after, "gather_wait_b")
        return w_in, w_out.reshape(D_MODEL, D_MODEL)

    def forward_weights_b(after):
        forwarded_b.extend(_gather_forward(send_b, recv_b, lands_b, after, "gather_forward_b"))
        return forwarded_b[3]

    stage1, stage2 = {}, {}

    def send_grads(tag, dw_in, dw_out):
        grads = [dw_in, dw_out.reshape(N_CHIPS, D_MODEL // N_CHIPS, D_MODEL)]
        send, recv, arrays, token = _reduce_sibling_start(grads, dw_out, f"reduce_d2d_start_{tag}")
        stage1[tag] = (send, recv, arrays)
        return token

    def forward_grads(tag, after):
        send, recv, arrays = stage1[tag]
        grads, got = _reduce_sibling_wait(send, recv, arrays, after, f"reduce_d2d_wait_{tag}")
        partials = [_add_sibling_half(grads[i], got[i], dev_idx, f"reduce_add_{tag}_{i}") for i in range(2)]
        send, recv, arrays, token = _reduce_chips_start(partials, partials[1], f"reduce_ici_start_{tag}")
        stage2[tag] = (send, recv, arrays)
        return token

    stage3 = {}

    def sum_grads(tag, after):
        send, recv, arrays = stage2[tag]
        partials, lands = _reduce_chips_wait(send, recv, arrays, after, f"reduce_ici_wait_{tag}")
        totals = [_sum_partials(lands[i], partials[i], dev_idx, f"reduce_sum_{tag}_{i}") for i in range(2)]
        send, recv, totals, token = _share_halves_start(totals, totals[1], f"reduce_share_start_{tag}")
        stage3[tag] = (send, recv, totals)
        return token

    def finish_grads(tag, after):
        send, recv, totals = stage3[tag]
        return _share_halves_wait(send, recv, totals, after, f"reduce_share_wait_{tag}")

    grad_x, small = _local_step(
        x[0], loss_target[0], mods.reshape(2, 3, D_MODEL), norm_g, conv_w_full, a_conv_b, a_ln_g[0:1],
        a_ln_b[0:1], b_q_norm[0], b_k_norm[0], chip.astype(jnp.int32), own_wb_in,
        weights_a, weights_b, forward_weights_b,
        functools.partial(send_grads, "b"), functools.partial(forward_grads, "b"), functools.partial(send_grads, "a"))

    ns = 3 * D_MODEL // N_CHIPS
    pad_mod = lambda dm: jnp.pad(dm.reshape(N_CHIPS, ns), ((0, 0), (0, D_MODEL - ns)))
    packed = jnp.concatenate([
        small["dnorm_g"], small["dconv_b"], small["dln_g"], small["dln_b"], small["dq_norm"], small["dk_norm"],
        small["loss_cols"], pad_mod(small["dmod0"]), pad_mod(small["dmod1"]), small["dconv_w"],
        jnp.zeros((SMALL_ROWS - 20 - CONV_WIDTH, D_MODEL), F32)], axis=0)
    send_s, recv_s, small_arrays, token_s = _small_gather_start(packed, packed)

    given = dict(norm_g=(norm_g, m_norm_g, v_norm_g), ada_w=(ada_w, m_ada_w, v_ada_w), ada_b=(ada_b, m_ada_b, v_ada_b),
                 a_w_in=(a_w_in, m_a_w_in, v_a_w_in), a_conv_w=(a_conv_w, m_a_conv_w, v_a_conv_w),
                 a_conv_b=(a_conv_b, m_a_conv_b, v_a_conv_b), a_ln_g=(a_ln_g, m_a_ln_g, v_a_ln_g),
                 a_ln_b=(a_ln_b, m_a_ln_b, v_a_ln_b), a_w_out=(a_w_out, m_a_w_out, v_a_w_out),
                 b_w_in=(b_w_in, m_b_w_in, v_b_w_in), b_q_norm=(b_q_norm, m_b_q_norm, v_b_q_norm),
                 b_k_norm=(b_k_norm, m_b_k_norm, v_b_k_norm), b_w_out=(b_w_out, m_b_w_out, v_b_w_out))
    order = ["norm_g", "ada_w", "ada_b", "a_w_in", "a_conv_w", "a_conv_b", "a_ln_g", "a_ln_b", "a_w_out", "b_w_in",
             "b_q_norm", "b_k_norm", "b_w_out"]
    outs = {}

    def update(k, g2, after=None, copy_grad=False):
        w, m, v = given[k]
        shape2 = g2.shape
        res = _adamw(w.reshape(shape2), g2, m.reshape(shape2), v.reshape(shape2), f"adamw_{k}", after, copy_grad)
        outs[k] = tuple(a.reshape(w.shape) for a in ((res[3] if copy_grad else g2), res[0], res[1], res[2]))

    token = forward_grads("a", token_s)
    token = sum_grads("b", token)
    packed, land = _small_gather_wait(send_s, recv_s, small_arrays, token)
    tot, g_ada_w, loss, qk = _reduce_small(packed, land, silu_c)
    g_b_in, g_b_out = finish_grads("b", tot)
    update("b_w_in", g_b_in, copy_grad=True)
    update("b_w_out", g_b_out, copy_grad=True)
    token = sum_grads("a", outs["b_w_in"][1])
    cw = D_MODEL // N_CHIPS
    g_small = dict(
        norm_g=tot[0:2], a_conv_b=tot[2:3], a_ln_g=tot[3:4], a_ln_b=tot[4:5],
        b_q_norm=qk[0:3], b_k_norm=qk[3:6],
        ada_b=jnp.stack([tot[12:16, :ns].reshape(3 * D_MODEL), tot[16:20, :ns].reshape(3 * D_MODEL)]),
        a_conv_w=lax.dynamic_slice(tot[20:20 + CONV_WIDTH], (0, chip * cw), (CONV_WIDTH, cw)),
    )
    update("ada_w", g_ada_w.reshape(2 * D_MODEL, ns), after=token)
    for k, g2 in g_small.items():
        update(k, g2, after=token)
    g_a_in, g_a_out = finish_grads("a", outs["ada_w"][1])
    update("a_w_in", g_a_in, copy_grad=True)
    update("a_w_out", g_a_out, copy_grad=True)
    return (loss.reshape(()), grad_x[None], *[outs[k][0] for k in order], *[outs[k][1] for k in order],
            *[outs[k][2] for k in order], *[outs[k][3] for k in order])
```

```python
import functools

import jax
import jax.numpy as jnp
from jax import lax
from jax.experimental import pallas as pl
from jax.experimental.pallas import tpu as pltpu

F32 = jnp.float32
BF16 = jnp.bfloat16

SEQ = 2048
D_MODEL = 1024
CONV_WIDTH = 31
HEAD_DIM = 64
N_HEADS = 16
DILATIONS = (1, 4, 16)
ATTN_BLOCK = 128
NORM_EPS = 1e-6
NEG_INF = -1e30
N_DEV = 8
N_CHIPS = 4

ADAM_LR = 0.001
ADAM_B1 = 0.9
ADAM_B2 = 0.999
ADAM_EPS = 1e-08
ADAM_WD = 0.01
ADAM_STEP = 10

VMEM_LIMIT_BYTES = 52 * 1024 * 1024
HALO = 32
LANES = 128
MESH = pl.DeviceIdType.MESH


def _params(*sem):
    return pltpu.CompilerParams(dimension_semantics=sem or None, vmem_limit_bytes=VMEM_LIMIT_BYTES)


def _sigmoid(v):
    return 1.0 / (1.0 + jnp.exp(-v))


def _row_spec(tm, cols, col_block=0):
    return pl.BlockSpec((tm, cols), lambda i: (i, col_block))


def _vec_spec(rows, cols):
    return pl.BlockSpec((rows, cols), lambda i: (0, 0))


def _normmod(xv, g, scale, shift):
    r = lax.rsqrt(jnp.mean(xv * xv, axis=-1, keepdims=True) + NORM_EPS)
    return xv * r * g * (1.0 + scale) + shift


def _normmod_fwd(x, g, scale, shift, name):
    tm = 256

    def body(x_ref, g_ref, sc_ref, sh_ref, h_ref, ht_ref):
        h = _normmod(x_ref[...], g_ref[...], sc_ref[...], sh_ref[...])
        h_ref[...] = h.astype(BF16)
        ht_ref[...] = h.T.astype(BF16)

    return pl.pallas_call(
        body, name=name, grid=(SEQ // tm,),
        in_specs=[_row_spec(tm, D_MODEL)] + [_vec_spec(1, D_MODEL)] * 3,
        out_specs=[_row_spec(tm, D_MODEL), pl.BlockSpec((D_MODEL, tm), lambda i: (0, i))],
        out_shape=[jax.ShapeDtypeStruct((SEQ, D_MODEL), BF16), jax.ShapeDtypeStruct((D_MODEL, SEQ), BF16)],
        compiler_params=_params("parallel"),
    )(x, g, scale, shift)


def _normmod_bwd(x, g, scale, dh_parts, dres, name, part_dilations=None, gated=None):
    tm = 256
    n_parts = len(dh_parts)
    dils = part_dilations or (1,) * n_parts
    dh_parts = [p if d == 1 else p.reshape(d, SEQ // d, D_MODEL) for p, d in zip(dh_parts, dils)]
    n_gated = 0 if gated is None else 2

    def body(x_ref, g_ref, sc_ref, dres_ref, *rest):
        part_refs = rest[:n_parts]
        gated_refs = rest[n_parts:n_parts + n_gated]
        out_refs = rest[n_parts + n_gated:]
        dx_ref, sums_ref, nat = out_refs[0], out_refs[1], out_refs[-1]
        xv = x_ref[...]
        r = lax.rsqrt(jnp.mean(xv * xv, axis=-1, keepdims=True) + NORM_EPS)
        xn = xv * r
        dh = _load_natural(part_refs[0], nat, dils[0])
        for p, d in zip(part_refs[1:], dils[1:]):
            dh = dh + _load_natural(p, nat, d)
        gv = g_ref[...]
        one_sc = 1.0 + sc_ref[...]
        dxn = dh * (gv * one_sc)
        dx = dres_ref[...] + r * (dxn - xn * jnp.mean(dxn * xn, axis=-1, keepdims=True))
        dx_ref[...] = dx
        dhx = dh * xn
        rows = [jnp.sum(dhx, axis=0, keepdims=True) * one_sc,
                jnp.sum(dhx, axis=0, keepdims=True) * gv,
                jnp.sum(dh, axis=0, keepdims=True)]
        if gated is not None:
            gate_ref, y_ref = gated_refs
            out_refs[2][...] = (dx * gate_ref[...]).astype(BF16)
            rows.append(jnp.sum(dx * y_ref[...], axis=0, keepdims=True))
        sums = jnp.concatenate(rows + [jnp.zeros((8 - len(rows), D_MODEL), F32)], axis=0)

        @pl.when(pl.program_id(0) == 0)
        def _():
            sums_ref[...] = jnp.zeros_like(sums_ref)

        sums_ref[...] += sums

    gated_specs = [] if gated is None else [_vec_spec(1, D_MODEL), _row_spec(tm, D_MODEL)]
    dy_spec = [] if gated is None else [_row_spec(tm, D_MODEL)]
    dy_shape = [] if gated is None else [jax.ShapeDtypeStruct((SEQ, D_MODEL), BF16)]
    return pl.pallas_call(
        body, name=name, grid=(SEQ // tm,),
        in_specs=[_row_spec(tm, D_MODEL), _vec_spec(1, D_MODEL), _vec_spec(1, D_MODEL), _row_spec(tm, D_MODEL)]
        + [_class_spec(tm, d) for d in dils] + gated_specs,
        out_specs=[_row_spec(tm, D_MODEL), _vec_spec(8, D_MODEL)] + dy_spec,
        out_shape=[jax.ShapeDtypeStruct((SEQ, D_MODEL), F32), jax.ShapeDtypeStruct((8, D_MODEL), F32)] + dy_shape,
        scratch_shapes=[_natural_scratch(tm)],
        compiler_params=_params("arbitrary"),
    )(x, g, scale, dres, *dh_parts, *(gated or ()))


def _mm(lhs, rhs, *, tn, tile0, n_tiles, out_dtype, name, out3d=None, prev=None, transpose_lhs=False):
    mo, kc = lhs.shape[::-1] if transpose_lhs else lhs.shape
    cm = min(mo, 1024)
    tc = 256

    def body(l_ref, r_ref, *rest):
        if transpose_lhs:
            o_ref, lt_ref = rest[-2], rest[-1]

            @pl.when(pl.program_id(0) == 0)
            def _():
                for c in range(kc // tc):
                    lt_ref[:, c * tc:(c + 1) * tc] = l_ref[c * tc:(c + 1) * tc, :].astype(F32).T.astype(l_ref.dtype)
        else:
            o_ref, lt_ref = rest[-1], l_ref
        for m in range(mo // cm):
            rows = pl.ds(m * cm, cm)
            o_ref[rows, :] = jnp.dot(lt_ref[rows, :], r_ref[...], preferred_element_type=F32).astype(out_dtype)

    if rhs.ndim == 3:
        tps_r = rhs.shape[2] // tn
        r_spec = pl.BlockSpec((None, kc, tn), lambda t: ((tile0 + t) // tps_r, 0, (tile0 + t) % tps_r))
    else:
        r_spec = pl.BlockSpec((kc, tn), lambda t: (0, t))
    in_specs = [pl.BlockSpec(lhs.shape, lambda t: (0, 0)), r_spec]
    args = [lhs, rhs]
    aliases = {}
    if out3d is None:
        o_spec = pl.BlockSpec((mo, tn), lambda t: (0, t))
        o_shape = jax.ShapeDtypeStruct((mo, n_tiles * tn), out_dtype)
    else:
        j_out, ns_out = out3d
        tps_o = ns_out // tn
        o_spec = pl.BlockSpec((None, mo, tn), lambda t: ((tile0 + t) // tps_o, 0, (tile0 + t) % tps_o))
        o_shape = jax.ShapeDtypeStruct((j_out, mo, ns_out), out_dtype)
        if prev is not None:
            in_specs.append(pl.BlockSpec(memory_space=pl.ANY))
            args.append(prev)
            aliases = {2: 0}
    return pl.pallas_call(
        body, name=name, grid=(n_tiles,), in_specs=in_specs, out_specs=o_spec, out_shape=o_shape,
        input_output_aliases=aliases,
        scratch_shapes=[pltpu.VMEM((mo, kc), lhs.dtype)] if transpose_lhs else [],
        compiler_params=_params("arbitrary" if transpose_lhs else "parallel"),
    )(*args)


def _in_tiles(h_parts, w3, tile_ids, n_tiles, *, tn, total_tiles, part_of, name, prev=None):
    _, kc, ns = w3.shape
    tps = ns // tn
    cm = 1024
    n_parts = len(h_parts)

    def body(ids_ref, *rest):
        h_refs, w_ref, o_ref = rest[:n_parts], rest[n_parts], rest[-1]
        part = part_of(ids_ref[1, pl.program_id(0)])
        for g, h_ref in enumerate(h_refs):
            @pl.when(part == g)
            def _():
                for m in range(SEQ // cm):
                    rows = pl.ds(m * cm, cm)
                    o_ref[rows, :] = jnp.dot(h_ref[rows, :], w_ref[...], preferred_element_type=F32).astype(BF16)

    resident = pl.BlockSpec((SEQ, kc), lambda t, ids: (0, 0))
    in_specs = [resident] * n_parts + [
        pl.BlockSpec((None, kc, tn), lambda t, ids: (ids[0, t] // tps, 0, ids[0, t] % tps))]
    args = [*h_parts, w3]
    aliases = {}
    if prev is not None:
        in_specs.append(pl.BlockSpec(memory_space=pl.ANY))
        args.append(prev)
        aliases = {n_parts + 2: 0}
    return pl.pallas_call(
        body, name=name,
        grid_spec=pltpu.PrefetchScalarGridSpec(
            num_scalar_prefetch=1, grid=(n_tiles,), in_specs=in_specs,
            out_specs=pl.BlockSpec((SEQ, tn), lambda t, ids: (0, ids[1, t]))),
        out_shape=jax.ShapeDtypeStruct((SEQ, total_tiles * tn), BF16),
        input_output_aliases=aliases, compiler_params=_params("arbitrary"),
    )(tile_ids, *args)


def _own_first(chip, total_tiles):
    own = total_tiles // N_CHIPS
    step = jnp.arange(total_tiles, dtype=jnp.int32)
    tiles = (own * chip + step) % total_tiles
    return jnp.stack([step[:own], tiles[:own]]), jnp.stack([tiles[own:], tiles[own:]]), own


def _mm_nt(dy, w3, *, tn, tile0, n_tiles, name, after=None):
    m_rows = dy.shape[0]
    _, kc, ns = w3.shape
    tps = ns // tn
    cm = 512
    extra = [] if after is None else [after]

    def body(dy_ref, w_ref, *rest):
        o_ref, acc = rest[-2], rest[-1]
        t = pl.program_id(0)

        @pl.when(t == 0)
        def _():
            acc[...] = jnp.zeros_like(acc)

        for m in range(m_rows // cm):
            rows = pl.ds(m * cm, cm)
            acc[rows, :] += lax.dot_general(dy_ref[rows, :], w_ref[...], NT_DIMS, preferred_element_type=F32)

        @pl.when(t == n_tiles - 1)
        def _():
            o_ref[...] = acc[...].astype(BF16)

    return pl.pallas_call(
        body, name=name, grid=(n_tiles,),
        in_specs=[pl.BlockSpec((m_rows, tn), lambda t: (0, t)),
                  pl.BlockSpec((None, kc, tn), lambda t: ((tile0 + t) // tps, 0, (tile0 + t) % tps))]
        + [pl.BlockSpec(memory_space=pl.ANY)] * len(extra),
        out_specs=pl.BlockSpec((m_rows, kc), lambda t: (0, 0)),
        out_shape=jax.ShapeDtypeStruct((m_rows, kc), BF16),
        scratch_shapes=[pltpu.VMEM((m_rows, kc), F32)],
        compiler_params=_params("arbitrary"),
    )(dy, w3, *extra)


CONV_CHUNK = 16


def _shift_copies(buf, shifted):
    rows = shifted.shape[1]
    for s in range(1, 8):
        shifted[s - 1] = buf[pl.ds(s, rows), :]


def _shifted_rows(buf, shifted, offset, r0):
    s = offset % 8
    if s == 0:
        return buf[pl.ds(r0 + offset, CONV_CHUNK), :]
    return shifted[s - 1, pl.ds(r0 + (offset - s), CONV_CHUNK), :]


def _spread_taps(w_ref, taps):
    for k in range(CONV_WIDTH):
        taps[k] = jnp.broadcast_to(w_ref[k:k + 1, :], (8, D_MODEL))


def _times_tap(taps, k, rows):
    return (rows.reshape(CONV_CHUNK // 8, 8, D_MODEL) * taps[k][None]).reshape(CONV_CHUNK, D_MODEL)


def _conv_fwd(proj, conv_w, conv_b, ln_g, ln_b, name):
    tm = 256
    hb = tm // HALO

    def body(vg_ref, halo_ref, z_ref, w_ref, b_ref, g_ref, be_ref, u5_ref, u5t_ref, u2_ref, buf, shifted, taps):
        i = pl.program_id(0)
        u1 = vg_ref[:, :D_MODEL].astype(F32) * _sigmoid(vg_ref[:, D_MODEL:].astype(F32))
        u1h = halo_ref[:, :D_MODEL].astype(F32) * _sigmoid(halo_ref[:, D_MODEL:].astype(F32))
        buf[pl.ds(0, HALO), :] = jnp.where(i > 0, u1h, 0.0)
        buf[pl.ds(HALO, tm), :] = u1
        _shift_copies(buf, shifted)
        _spread_taps(w_ref, taps)

        def chunk(ci, carry):
            r0 = pl.multiple_of(ci * CONV_CHUNK, CONV_CHUNK)
            acc = jnp.broadcast_to(b_ref[...], (CONV_CHUNK, D_MODEL))
            for k in range(CONV_WIDTH):
                acc = acc + _times_tap(taps, k, _shifted_rows(buf, shifted, HALO - (CONV_WIDTH - 1) + k, r0))
            u2_ref[pl.ds(r0, CONV_CHUNK), :] = acc
            return carry

        lax.fori_loop(0, tm // CONV_CHUNK, chunk, 0)
        acc = u2_ref[...]
        mu = jnp.mean(acc, axis=-1, keepdims=True)
        xc = acc - mu
        rstd = lax.rsqrt(jnp.mean(xc * xc, axis=-1, keepdims=True) + NORM_EPS)
        u3 = xc * rstd * g_ref[...] + be_ref[...]
        zv = z_ref[...].astype(F32)
        u5 = u3 * _sigmoid(u3) * (zv * _sigmoid(zv))
        u5_ref[...] = u5.astype(BF16)
        u5t_ref[...] = u5.T.astype(BF16)

    return pl.pallas_call(
        body, name=name, grid=(SEQ // tm,),
        in_specs=[pl.BlockSpec((tm, 2 * D_MODEL), lambda i: (i, 0)),
                  pl.BlockSpec((HALO, 2 * D_MODEL), lambda i: (jnp.maximum(i * hb - 1, 0), 0)),
                  _row_spec(tm, D_MODEL, 2),
                  _vec_spec(CONV_WIDTH, D_MODEL)] + [_vec_spec(1, D_MODEL)] * 3,
        out_specs=[_row_spec(tm, D_MODEL), pl.BlockSpec((D_MODEL, tm), lambda i: (0, i)), _row_spec(tm, D_MODEL)],
        out_shape=[jax.ShapeDtypeStruct((SEQ, D_MODEL), BF16), jax.ShapeDtypeStruct((D_MODEL, SEQ), BF16),
                   jax.ShapeDtypeStruct((SEQ, D_MODEL), F32)],
        scratch_shapes=[pltpu.VMEM((HALO + tm, D_MODEL), F32), pltpu.VMEM((7, HALO + tm - 8, D_MODEL), F32),
                        pltpu.VMEM((CONV_WIDTH, 8, D_MODEL), F32)],
        compiler_params=_params("parallel"),
    )(proj, proj, proj, conv_w, conv_b, ln_g, ln_b)


def _conv_bwd_pointwise(dy, w_out, proj, u2, ln_g, ln_b, name):
    tm = 256

    def body(dy_ref, w_ref, z_ref, u2_ref, g_ref, be_ref, du2_ref, dz_ref, sums_ref):
        u2v = u2_ref[...]
        mu = jnp.mean(u2v, axis=-1, keepdims=True)
        xc = u2v - mu
        rstd = lax.rsqrt(jnp.mean(xc * xc, axis=-1, keepdims=True) + NORM_EPS)
        xhat = xc * rstd
        u3 = xhat * g_ref[...] + be_ref[...]
        s3 = _sigmoid(u3)
        u4 = u3 * s3
        zv = z_ref[...].astype(F32)
        sz = _sigmoid(zv)
        du5v = lax.dot_general(dy_ref[...], w_ref[...], NT_DIMS, preferred_element_type=F32)
        dz_ref[...] = du5v * u4 * (sz * (1.0 + zv * (1.0 - sz)))
        du3 = du5v * (zv * sz) * (s3 * (1.0 + u3 * (1.0 - s3)))
        dxhat = du3 * g_ref[...]
        du2 = rstd * (dxhat - jnp.mean(dxhat, axis=-1, keepdims=True)
                      - xhat * jnp.mean(dxhat * xhat, axis=-1, keepdims=True))
        du2_ref[...] = du2
        sums = jnp.concatenate([
            jnp.sum(du3 * xhat, axis=0, keepdims=True),
            jnp.sum(du3, axis=0, keepdims=True),
            jnp.sum(du2, axis=0, keepdims=True),
            jnp.zeros((5, D_MODEL), F32)], axis=0)

        @pl.when(pl.program_id(0) == 0)
        def _():
            sums_ref[...] = jnp.zeros_like(sums_ref)

        sums_ref[...] += sums

    return pl.pallas_call(
        body, name=name, grid=(SEQ // tm,),
        in_specs=[_row_spec(tm, D_MODEL), _vec_spec(D_MODEL, D_MODEL), _row_spec(tm, D_MODEL, 2),
                  _row_spec(tm, D_MODEL), _vec_spec(1, D_MODEL), _vec_spec(1, D_MODEL)],
        out_specs=[_row_spec(tm, D_MODEL), _row_spec(tm, D_MODEL), _vec_spec(8, D_MODEL)],
        out_shape=[jax.ShapeDtypeStruct((SEQ, D_MODEL), F32), jax.ShapeDtypeStruct((SEQ, D_MODEL), F32),
                   jax.ShapeDtypeStruct((8, D_MODEL), F32)],
        compiler_params=_params("arbitrary"),
    )(dy, w_out, proj, u2, ln_g, ln_b)


def _conv_bwd_taps(du2, dz, proj, conv_w, name):
    tm = 256
    hb = tm // HALO
    n_blocks = SEQ // tm

    def body(du2_ref, dnext_ref, dz_ref, vg_ref, w_ref, dproj_ref, dw_ref, dbuf, dshift, sgbuf, ubuf, dwacc, taps):
        i = pl.program_id(0)
        _spread_taps(w_ref, taps)
        sg = _sigmoid(vg_ref[:, D_MODEL:].astype(F32))
        sgbuf[...] = sg
        ubuf[...] = vg_ref[:, :D_MODEL].astype(F32) * sg
        dbuf[pl.ds(0, tm), :] = du2_ref[...]
        dbuf[pl.ds(tm, HALO), :] = jnp.where(i < n_blocks - 1, dnext_ref[...], 0.0)
        _shift_copies(dbuf, dshift)

        @pl.when(i == 0)
        def _():
            dwacc[...] = jnp.zeros_like(dwacc)

        def chunk(ci, carry):
            r0 = pl.multiple_of(ci * CONV_CHUNK, CONV_CHUNK)
            rows = pl.ds(r0, CONV_CHUNK)
            u1c = ubuf[rows, :]
            du1 = jnp.zeros((CONV_CHUNK, D_MODEL), F32)
            for k in range(CONV_WIDTH):
                ahead = _shifted_rows(dbuf, dshift, CONV_WIDTH - 1 - k, r0)
                du1 = du1 + _times_tap(taps, k, ahead)
                prod = u1c * ahead
                dwacc[k] += prod[0:8] + prod[8:16]
            sgc = sgbuf[rows, :]
            dval = du1 * sgc
            dproj_ref[rows, 0:D_MODEL] = dval.astype(BF16)
            dproj_ref[rows, D_MODEL:2 * D_MODEL] = (
                dval * vg_ref[rows, 0:D_MODEL].astype(F32) * (1.0 - sgc)).astype(BF16)
            return carry

        lax.fori_loop(0, tm // CONV_CHUNK, chunk, 0)
        dproj_ref[:, 2 * D_MODEL:] = dz_ref[...].astype(BF16)

        @pl.when(i == n_blocks - 1)
        def _():
            for k in range(CONV_WIDTH):
                dw_ref[k:k + 1, :] = jnp.sum(dwacc[k], axis=0, keepdims=True)
            dw_ref[CONV_WIDTH:, :] = jnp.zeros((32 - CONV_WIDTH, D_MODEL), F32)

    return pl.pallas_call(
        body, name=name, grid=(n_blocks,),
        in_specs=[_row_spec(tm, D_MODEL),
                  pl.BlockSpec((HALO, D_MODEL), lambda i: (jnp.minimum((i + 1) * hb, SEQ // HALO - 1), 0)),
                  _row_spec(tm, D_MODEL),
                  pl.BlockSpec((tm, 2 * D_MODEL), lambda i: (i, 0)),
                  _vec_spec(CONV_WIDTH, D_MODEL)],
        out_specs=[_row_spec(tm, 3 * D_MODEL), _vec_spec(32, D_MODEL)],
        out_shape=[jax.ShapeDtypeStruct((SEQ, 3 * D_MODEL), BF16), jax.ShapeDtypeStruct((32, D_MODEL), F32)],
        scratch_shapes=[pltpu.VMEM((tm + HALO, D_MODEL), F32), pltpu.VMEM((7, HALO + tm - 8, D_MODEL), F32),
                        pltpu.VMEM((tm, D_MODEL), F32), pltpu.VMEM((tm, D_MODEL), F32),
                        pltpu.VMEM((CONV_WIDTH, 8, D_MODEL), F32), pltpu.VMEM((CONV_WIDTH, 8, D_MODEL), F32)],
        compiler_params=_params("arbitrary"),
    )(du2, du2, dz, proj, conv_w)


def _out_a(u5, w_out, x, gate, g1, scale1, shift1, name):
    tm = 256
    n_d = len(DILATIONS)

    def body(u_ref, w_ref, x_ref, gate_ref, g_ref, sc_ref, sh_ref, x1_ref, y_ref, ht_ref, *rest):
        h_refs, nat = rest[:n_d], rest[-1]
        y = jnp.dot(u_ref[...], w_ref[...], preferred_element_type=F32)
        x1 = x_ref[...] + gate_ref[...] * y
        y_ref[...] = y
        x1_ref[...] = x1
        h = _normmod(x1, g_ref[...], sc_ref[...], sh_ref[...])
        ht_ref[...] = h.T.astype(BF16)
        for h_ref, d in zip(h_refs, DILATIONS):
            _store_classes(h_ref, h, nat, d)

    res = pl.pallas_call(
        body, name=name, grid=(SEQ // tm,),
        in_specs=[_row_spec(tm, D_MODEL), _vec_spec(D_MODEL, D_MODEL), _row_spec(tm, D_MODEL)]
        + [_vec_spec(1, D_MODEL)] * 4,
        out_specs=[_row_spec(tm, D_MODEL), _row_spec(tm, D_MODEL), pl.BlockSpec((D_MODEL, tm), lambda i: (0, i))]
        + [_class_spec(tm, d) for d in DILATIONS],
        out_shape=[jax.ShapeDtypeStruct((SEQ, D_MODEL), F32), jax.ShapeDtypeStruct((SEQ, D_MODEL), F32),
                   jax.ShapeDtypeStruct((D_MODEL, SEQ), BF16)] + [_class_shape(d, BF16) for d in DILATIONS],
        scratch_shapes=[_natural_scratch(tm)],
        compiler_params=_params("parallel"),
    )(u5, w_out, x, gate, g1, scale1, shift1)
    return res[0], res[1], res[2], [a.reshape(SEQ, D_MODEL) for a in res[3:]]


def _out_b_loss(u, w_out, x1, gate, target, name):
    tm = 256

    def body(u_ref, w_ref, x_ref, gate_ref, t_ref, e_ref, dy_ref, sums_ref):
        y = jnp.dot(u_ref[...], w_ref[...], preferred_element_type=F32)
        diff = x_ref[...] + gate_ref[...] * y - t_ref[...]
        e = diff * (1.0 / D_MODEL)
        e_ref[...] = e
        dy_ref[...] = (e * gate_ref[...]).astype(BF16)
        sums = jnp.concatenate([
            jnp.sum(e * y, axis=0, keepdims=True),
            jnp.sum(diff * diff, axis=0, keepdims=True),
            jnp.zeros((6, D_MODEL), F32)], axis=0)

        @pl.when(pl.program_id(0) == 0)
        def _():
            sums_ref[...] = jnp.zeros_like(sums_ref)

        sums_ref[...] += sums

    return pl.pallas_call(
        body, name=name, grid=(SEQ // tm,),
        in_specs=[_row_spec(tm, D_MODEL), _vec_spec(D_MODEL, D_MODEL), _row_spec(tm, D_MODEL),
                  _vec_spec(1, D_MODEL), _row_spec(tm, D_MODEL)],
        out_specs=[_row_spec(tm, D_MODEL), _row_spec(tm, D_MODEL), _vec_spec(8, D_MODEL)],
        out_shape=[jax.ShapeDtypeStruct((SEQ, D_MODEL), F32), jax.ShapeDtypeStruct((SEQ, D_MODEL), BF16),
                   jax.ShapeDtypeStruct((8, D_MODEL), F32)],
        compiler_params=_params("arbitrary"),
    )(u, w_out, x1, gate, target)


def _seg_matrix():
    r = lax.broadcasted_iota(jnp.int32, (256, 256), 0) // HEAD_DIM
    c = lax.broadcasted_iota(jnp.int32, (256, 256), 1) // HEAD_DIM
    return (r == c).astype(BF16)


def _segsum(v, seg):
    hi = v.astype(BF16)
    lo = (v - hi.astype(F32)).astype(BF16)
    outs = []
    for c0 in range(0, D_MODEL, 256):
        outs.append(jnp.dot(hi[:, c0:c0 + 256], seg, preferred_element_type=F32)
                    + jnp.dot(lo[:, c0:c0 + 256], seg, preferred_element_type=F32))
    return jnp.concatenate(outs, axis=1)


def _qk_rstd(v, seg):
    return lax.rsqrt(_segsum(v * v, seg) * (1.0 / HEAD_DIM) + NORM_EPS)


def _qknorm_fwd(proj, group, qw, kw, seg, name):
    tm = 256

    def body(q_in, k_in, qw_ref, kw_ref, seg_ref, q_ref, k_ref):
        segv = seg_ref[...]
        q = q_in[...].astype(F32)
        k = k_in[...].astype(F32)
        q_ref[...] = (q * _qk_rstd(q, segv) * qw_ref[...] * HEAD_DIM ** -0.5).astype(BF16)
        k_ref[...] = (k * _qk_rstd(k, segv) * kw_ref[...]).astype(BF16)

    return pl.pallas_call(
        body, name=name, grid=(SEQ // tm,),
        in_specs=[_row_spec(tm, D_MODEL, 3 * group), _row_spec(tm, D_MODEL, 3 * group + 1),
                  _vec_spec(1, D_MODEL), _vec_spec(1, D_MODEL), _vec_spec(256, 256)],
        out_specs=[_row_spec(tm, D_MODEL)] * 2,
        out_shape=[jax.ShapeDtypeStruct((SEQ, D_MODEL), BF16)] * 2,
        compiler_params=_params("parallel"),
    )(proj, proj, qw, kw, seg)


def _attn_masks(b, bpc, dilation, transposed=False):
    keys = ATTN_BLOCK if bpc == 1 else 2 * ATTN_BLOCK
    shape, q_axis = ((keys, ATTN_BLOCK), 1) if transposed else ((ATTN_BLOCK, keys), 0)
    qi = lax.broadcasted_iota(jnp.int32, shape, q_axis)
    kj = lax.broadcasted_iota(jnp.int32, shape, 1 - q_axis)
    if bpc == 1:
        steps = qi - kj
        return (steps * dilation).astype(F32), steps >= 0
    steps = qi + ATTN_BLOCK - kj
    has_prev = (b % bpc) != 0
    valid = (steps >= 0) & (steps <= ATTN_BLOCK) & (has_prev | (kj >= ATTN_BLOCK))
    return (steps * dilation).astype(F32), valid


MASKED = 1e30


def _bias_scratch(bpc):
    return pltpu.VMEM((1 if bpc == 1 else 2, N_HEADS, ATTN_BLOCK, (1 if bpc == 1 else 2) * ATTN_BLOCK), F32)


def _fill_bias(bias_ref, sl_ref, bpc, dilation):
    for variant in range(bias_ref.shape[0]):
        dist, valid = _attn_masks(variant, min(bpc, 2), dilation)
        bias_ref[variant] = jnp.where(valid[None], dist[None] * sl_ref[...], MASKED)


def _step_bias(bias_ref, b, bpc):
    if bpc == 1:
        return bias_ref[0]
    return bias_ref[jnp.where((b % bpc) != 0, 1, 0)]


def _key_tile(prev_ref, cur_ref, cols, bpc):
    if bpc == 1:
        return cur_ref[:, cols]
    return jnp.concatenate([prev_ref[:, cols], cur_ref[:, cols]], axis=0)


ATTN_HEADS_FWD = 16
ATTN_HEADS_BWD = 16
NT_DIMS = (((1,), (1,)), ((), ()))
BATCH_NT_DIMS = (((2,), (2,)), ((0,), (0,)))
BATCH_NN_DIMS = (((2,), (1,)), ((0,), (0,)))
BATCH_TN_DIMS = (((1,), (1,)), ((0,), (0,)))


def _head_stack(tile_of, heads):
    return jnp.stack([tile_of(slice(h * HEAD_DIM, (h + 1) * HEAD_DIM)) for h in range(heads)], axis=0)


def _attn_specs(heads, segment=0):
    width = heads * HEAD_DIM
    off = segment * (D_MODEL // width)
    last = SEQ // ATTN_BLOCK - 1
    cur = pl.BlockSpec((ATTN_BLOCK, width), lambda hg, b: (jnp.minimum(b, last), hg + off))
    prev = pl.BlockSpec((ATTN_BLOCK, width), lambda hg, b: (jnp.clip(b - 1, 0, last), hg + off))
    return cur, prev


def _attn_fwd(q, k, proj, group, slopes, dilation, name):
    bpc = SEQ // dilation // ATTN_BLOCK
    heads = ATTN_HEADS_FWD
    assert heads == N_HEADS
    cur, prev = _attn_specs(heads)
    v_cur, v_prev = _attn_specs(heads, segment=3 * group + 2)

    def body(sl_ref, q_ref, kp_ref, kc_ref, vp_ref, vc_ref, o_ref, lse_ref, bias_ref):
        b = pl.program_id(1)

        @pl.when(b == 0)
        def _():
            _fill_bias(bias_ref, sl_ref, bpc, dilation)

        q3 = _head_stack(lambda cols: q_ref[:, cols], heads)
        k3 = _head_stack(lambda cols: _key_tile(kp_ref, kc_ref, cols, bpc), heads)
        v3 = _head_stack(lambda cols: _key_tile(vp_ref, vc_ref, cols, bpc), heads)
        s = lax.dot_general(q3, k3, BATCH_NT_DIMS, preferred_element_type=F32)
        s = s - _step_bias(bias_ref, b, bpc)
        m = jnp.max(s, axis=-1, keepdims=True)
        p = jnp.exp(s - m)
        l = jnp.sum(p, axis=-1, keepdims=True)
        o3 = lax.dot_general(p.astype(BF16), v3, BATCH_NN_DIMS, preferred_element_type=F32) / l
        lse3 = m + jnp.log(l)
        for h in range(heads):
            o_ref[:, h * HEAD_DIM:(h + 1) * HEAD_DIM] = o3[h].astype(BF16)
        lse_ref[...] = jnp.concatenate([lse3[h] for h in range(heads)]
                                       + [jnp.zeros((ATTN_BLOCK, LANES - heads), F32)], axis=1)

    return pl.pallas_call(
        body, name=name, grid=(N_HEADS // heads, SEQ // ATTN_BLOCK),
        in_specs=[pl.BlockSpec((heads, 1, 1), lambda hg, b: (hg, 0, 0)), cur, prev, cur, v_prev, v_cur],
        out_specs=[cur, pl.BlockSpec((ATTN_BLOCK, LANES), lambda hg, b: (b, 0))],
        out_shape=[jax.ShapeDtypeStruct((SEQ, D_MODEL), BF16), jax.ShapeDtypeStruct((SEQ, LANES), F32)],
        scratch_shapes=[_bias_scratch(bpc)],
        compiler_params=_params("parallel", "arbitrary"),
    )(slopes.reshape(N_HEADS, 1, 1), q, k, k, proj, proj)


def _class_spec(tm, dilation, width=D_MODEL):
    if dilation == 1:
        return _row_spec(tm, width)
    return pl.BlockSpec((dilation, tm // dilation, width), lambda i: (0, i, 0))


def _class_shape(dilation, dtype, width=D_MODEL):
    if dilation == 1:
        return jax.ShapeDtypeStruct((SEQ, width), dtype)
    return jax.ShapeDtypeStruct((dilation, SEQ // dilation, width), dtype)


def _load_natural(in_ref, nat_ref, dilation):
    if dilation == 1:
        return in_ref[...].astype(F32)
    n = nat_ref.shape[1] // dilation
    tiles = in_ref.shape[-1] // LANES
    for r in range(dilation):
        for j in range(tiles):
            nat_ref.at[j][pl.ds(r, n, stride=dilation), :] = in_ref[r, :, j * LANES:(j + 1) * LANES].astype(F32)
    if tiles == 1:
        return nat_ref[0]
    return jnp.concatenate([nat_ref[j] for j in range(tiles)], axis=1)


def _store_classes(out_ref, value, nat_ref, dilation):
    if dilation == 1:
        out_ref[...] = value.astype(out_ref.dtype)
        return
    n = nat_ref.shape[1] // dilation
    tiles = value.shape[-1] // LANES
    for j in range(tiles):
        nat_ref[j] = value[:, j * LANES:(j + 1) * LANES]
    for r in range(dilation):
        for j in range(tiles):
            out_ref[r, :, j * LANES:(j + 1) * LANES] = (
                nat_ref.at[j][pl.ds(r, n, stride=dilation), :].astype(out_ref.dtype))


def _natural_scratch(tm):
    return pltpu.VMEM((D_MODEL // LANES, tm, LANES), F32)


def _head_selector():
    lane_head = lax.broadcasted_iota(jnp.int32, (D_MODEL, LANES), 0) // HEAD_DIM
    head = lax.broadcasted_iota(jnp.int32, (D_MODEL, LANES), 1)
    return (lane_head == head).astype(BF16)


def _dot_split(v, m01, dims):
    hi = v.astype(BF16)
    lo = (v - hi.astype(F32)).astype(BF16)
    return (lax.dot_general(hi, m01, dims, preferred_element_type=F32)
            + lax.dot_general(lo, m01, dims, preferred_element_type=F32))


def _merge_fwd(o_parts, lse_parts, z, sel, name):
    tm = 256
    h_spec = pl.BlockSpec((tm, LANES), lambda i: (i, 0))

    def body(o0, o1, o2, l0, l1, l2, z_ref, sel_ref, u_ref, ut_ref, o_ref, lse_ref, nat):
        ls = [_load_natural(l, nat, d) for l, d in zip((l0, l1, l2), DILATIONS)]
        m = jnp.maximum(jnp.maximum(ls[0], ls[1]), ls[2])
        tot = m + jnp.log(jnp.exp(ls[0] - m) + jnp.exp(ls[1] - m) + jnp.exp(ls[2] - m))
        o = jnp.zeros((tm, D_MODEL), F32)
        for o_in, l, d in zip((o0, o1, o2), ls, DILATIONS):
            weight = _dot_split(jnp.exp(l - tot), sel_ref[...], NT_DIMS)
            o = o + weight * _load_natural(o_in, nat, d)
        zv = z_ref[...].astype(F32)
        u = o * (zv * _sigmoid(zv))
        u_ref[...] = u.astype(BF16)
        ut_ref[...] = u.T.astype(BF16)
        o_ref[...] = o
        lse_ref[...] = tot

    return pl.pallas_call(
        body, name=name, grid=(SEQ // tm,),
        in_specs=[_class_spec(tm, d) for d in DILATIONS] + [_class_spec(tm, d, LANES) for d in DILATIONS]
        + [_row_spec(tm, D_MODEL, B_Z_SEGMENT), _vec_spec(D_MODEL, LANES)],
        out_specs=[_row_spec(tm, D_MODEL), pl.BlockSpec((D_MODEL, tm), lambda i: (0, i)),
                   _row_spec(tm, D_MODEL), h_spec],
        out_shape=[jax.ShapeDtypeStruct((SEQ, D_MODEL), BF16), jax.ShapeDtypeStruct((D_MODEL, SEQ), BF16),
                   jax.ShapeDtypeStruct((SEQ, D_MODEL), F32), jax.ShapeDtypeStruct((SEQ, LANES), F32)],
        scratch_shapes=[_natural_scratch(tm)],
        compiler_params=_params("parallel"),
    )(*o_parts, *lse_parts, z, sel)


def _merge_bwd(dy, w_out, o, lse, z, sel, name):
    tm = 256
    n_d = len(DILATIONS)

    def body(dy_ref, w_ref, o_ref, lse_ref, z_ref, sel_ref, dz_ref, *rest):
        do_refs, delta_refs, lse_refs, nat = rest[:n_d], rest[n_d:2 * n_d], rest[2 * n_d:3 * n_d], rest[-1]
        zv = z_ref[...].astype(F32)
        sz = _sigmoid(zv)
        duv = lax.dot_general(dy_ref[...], w_ref[...], NT_DIMS, preferred_element_type=F32)
        ov = o_ref[...]
        do = duv * (zv * sz)
        dz_ref[...] = (duv * ov * (sz * (1.0 + zv * (1.0 - sz)))).astype(BF16)
        delta = _dot_split(do * ov, sel_ref[...], (((1,), (0,)), ((), ())))
        lv = lse_ref[...]
        for i, d in enumerate(DILATIONS):
            _store_classes(do_refs[i], do, nat, d)
            _store_classes(delta_refs[i], delta, nat, d)
            _store_classes(lse_refs[i], lv, nat, d)

    res = pl.pallas_call(
        body, name=name, grid=(SEQ // tm,),
        in_specs=[_row_spec(tm, D_MODEL), _vec_spec(D_MODEL, D_MODEL), _row_spec(tm, D_MODEL), _row_spec(tm, LANES),
                  _row_spec(tm, D_MODEL, B_Z_SEGMENT), _vec_spec(D_MODEL, LANES)],
        out_specs=[_row_spec(tm, D_MODEL)] + [_class_spec(tm, d) for d in DILATIONS]
        + [_class_spec(tm, d, LANES) for d in DILATIONS] * 2,
        out_shape=[jax.ShapeDtypeStruct((SEQ, D_MODEL), BF16)] + [_class_shape(d, BF16) for d in DILATIONS]
        + [_class_shape(d, F32, LANES) for d in DILATIONS] * 2,
        scratch_shapes=[_natural_scratch(tm)],
        compiler_params=_params("parallel"),
    )(dy, w_out, o, lse, z, sel)
    flat = lambda a: a.reshape(SEQ, a.shape[-1])
    return (res[0], [flat(a) for a in res[1:1 + n_d]], [flat(a) for a in res[1 + n_d:1 + 2 * n_d]],
            [flat(a) for a in res[1 + 2 * n_d:]])


def _attn_bwd(q, k, proj, group, do, lse, delta, slopes, dilation, name):
    bpc = SEQ // dilation // ATTN_BLOCK
    heads = ATTN_HEADS_BWD
    n_blocks = SEQ // ATTN_BLOCK
    carry = bpc > 1
    width = heads * HEAD_DIM
    cur, prev = _attn_specs(heads)
    v_cur, v_prev = _attn_specs(heads, segment=3 * group + 2)
    assert heads == N_HEADS
    per_head = pl.BlockSpec((ATTN_BLOCK, LANES), lambda hg, b: (jnp.minimum(b, n_blocks - 1), 0))
    scale = HEAD_DIM ** -0.5

    def body(sl_ref, q_ref, kp_ref, kc_ref, vp_ref, vc_ref, do_ref, lse_ref, dl_ref,
             dq_ref, dk_ref, dv_ref, *scratch):
        b = pl.program_id(1)
        if carry:
            dk_carry, dv_carry = scratch

            @pl.when(b == n_blocks)
            def _():
                dk_ref[...] = dk_carry[...].astype(BF16)
                dv_ref[...] = dv_carry[...].astype(BF16)

            @pl.when(b < n_blocks)
            def _():
                step(sl_ref, q_ref, kp_ref, kc_ref, vp_ref, vc_ref, do_ref, lse_ref, dl_ref,
                     dq_ref, dk_ref, dv_ref, dk_carry, dv_carry, b)
        else:
            step(sl_ref, q_ref, kp_ref, kc_ref, vp_ref, vc_ref, do_ref, lse_ref, dl_ref,
                 dq_ref, dk_ref, dv_ref, None, None, b)

    def step(sl_ref, q_ref, kp_ref, kc_ref, vp_ref, vc_ref, do_ref, lse_ref, dl_ref,
             dq_ref, dk_ref, dv_ref, dk_carry, dv_carry, b):
        if carry:
            @pl.when(b == 0)
            def _():
                dk_carry[...] = jnp.zeros_like(dk_carry)
                dv_carry[...] = jnp.zeros_like(dv_carry)

        q3 = _head_stack(lambda cols: q_ref[:, cols], heads)
        k3 = _head_stack(lambda cols: _key_tile(kp_ref, kc_ref, cols, bpc), heads)
        v3 = _head_stack(lambda cols: _key_tile(vp_ref, vc_ref, cols, bpc), heads)
        do3 = _head_stack(lambda cols: do_ref[:, cols], heads)
        lse_t = lse_ref[...].T
        dl_t = dl_ref[...].T
        lse3 = jnp.stack([lse_t[h:h + 1, :] for h in range(heads)], axis=0)
        dl3 = jnp.stack([dl_t[h:h + 1, :] for h in range(heads)], axis=0)
        s = lax.dot_general(k3, q3, BATCH_NT_DIMS, preferred_element_type=F32)
        dist, valid = _attn_masks(b, bpc, dilation, transposed=True)
        p = jnp.exp(jnp.where(valid[None], s - dist[None] * sl_ref[...], NEG_INF) - lse3)
        dp = lax.dot_general(v3, do3, BATCH_NT_DIMS, preferred_element_type=F32)
        ds = (p * (dp - dl3)).astype(BF16)
        dq3 = lax.dot_general(ds, k3, BATCH_TN_DIMS, preferred_element_type=F32) * scale
        dk3 = lax.dot_general(ds, q3, BATCH_NN_DIMS, preferred_element_type=F32)
        dv3 = lax.dot_general(p.astype(BF16), do3, BATCH_NN_DIMS, preferred_element_type=F32)
        for h in range(heads):
            cols = slice(h * HEAD_DIM, (h + 1) * HEAD_DIM)
            dq_ref[:, cols] = dq3[h].astype(BF16)
            if carry:
                dk_ref[:, cols] = (dk_carry[:, cols] + dk3[h, :ATTN_BLOCK]).astype(BF16)
                dv_ref[:, cols] = (dv_carry[:, cols] + dv3[h, :ATTN_BLOCK]).astype(BF16)
                dk_carry[:, cols] = dk3[h, ATTN_BLOCK:]
                dv_carry[:, cols] = dv3[h, ATTN_BLOCK:]
            else:
                dk_ref[:, cols] = dk3[h].astype(BF16)
                dv_ref[:, cols] = dv3[h].astype(BF16)

    kv_out = prev if carry else cur
    return pl.pallas_call(
        body, name=name, grid=(N_HEADS // heads, n_blocks + (1 if carry else 0)),
        in_specs=[pl.BlockSpec((heads, 1, 1), lambda hg, b: (hg, 0, 0)), cur, prev, cur, v_prev, v_cur,
                  cur, per_head, per_head],
        out_specs=[cur, kv_out, kv_out],
        out_shape=[jax.ShapeDtypeStruct((SEQ, D_MODEL), BF16)] * 3,
        scratch_shapes=[pltpu.VMEM((ATTN_BLOCK, width), F32)] * 2 if carry else [],
        compiler_params=_params("parallel", "arbitrary"),
    )(slopes.reshape(N_HEADS, 1, 1), q, k, k, proj, proj, do, lse, delta)


def _qknorm_bwd(proj, group, qw, kw, seg, dq, dk, dv, name):
    tm = 256

    def body(q_in, k_in, qw_ref, kw_ref, seg_ref, dq_ref, dk_ref, dv_ref, dproj_ref, sums_ref):
        segv = seg_ref[...]
        sums = []
        for part, (raw_ref, w_ref, dn_ref) in enumerate(((q_in, qw_ref, dq_ref), (k_in, kw_ref, dk_ref))):
            raw = raw_ref[...].astype(F32)
            dn = dn_ref[...].astype(F32)
            r = _qk_rstd(raw, segv)
            gq = dn * w_ref[...]
            draw = r * gq - raw * (r * r * r) * (_segsum(raw * gq, segv) * (1.0 / HEAD_DIM))
            dproj_ref[:, part * D_MODEL:(part + 1) * D_MODEL] = draw.astype(BF16)
            sums.append(jnp.sum(dn * raw * r, axis=0, keepdims=True))
        dproj_ref[:, 2 * D_MODEL:] = dv_ref[...]

        @pl.when(pl.program_id(0) == 0)
        def _():
            sums_ref[...] = jnp.zeros_like(sums_ref)

        sums_ref[...] += jnp.concatenate(sums + [jnp.zeros((6, D_MODEL), F32)], axis=0)

    return pl.pallas_call(
        body, name=name, grid=(SEQ // tm,),
        in_specs=[_row_spec(tm, D_MODEL, 3 * group), _row_spec(tm, D_MODEL, 3 * group + 1),
                  _vec_spec(1, D_MODEL), _vec_spec(1, D_MODEL), _vec_spec(256, 256)] + [_row_spec(tm, D_MODEL)] * 3,
        out_specs=[_row_spec(tm, 3 * D_MODEL), _vec_spec(8, D_MODEL)],
        out_shape=[jax.ShapeDtypeStruct((SEQ, 3 * D_MODEL), BF16), jax.ShapeDtypeStruct((8, D_MODEL), F32)],
        compiler_params=_params("arbitrary"),
    )(proj, proj, qw, kw, seg, dq, dk, dv)


B_TN = 512
B_GROUP_TILES = 3 * D_MODEL // B_TN
B_Z_TILE0 = 3 * B_GROUP_TILES
B_Z_TILES = D_MODEL // B_TN
B_TILES = B_Z_TILE0 + B_Z_TILES
B_Z_SEGMENT = 3 * len(DILATIONS)


def _local_step(x, target, mods, norm_g, conv_w, conv_b, ln_g, ln_b, q_norm, k_norm, chip, own_wa_in, own_wb_in,
                weights_a, weights_b, forward_weights_b, send_grads_b, forward_grads_b, send_grads_a):
    row = lambda a, i: a[i:i + 1]
    shift0, scale0, gate0 = row(mods[0], 0), row(mods[0], 1), row(mods[0], 2)
    shift1, scale1, gate1 = row(mods[1], 0), row(mods[1], 1), row(mods[1], 2)
    g0, g1 = row(norm_g, 0), row(norm_g, 1)
    seg = _seg_matrix()
    slopes = jnp.exp2(-8.0 * jnp.arange(1, N_HEADS + 1, dtype=F32) / N_HEADS)
    qw = [jnp.tile(q_norm[g:g + 1], (1, N_HEADS)) for g in range(3)]
    kw = [jnp.tile(k_norm[g:g + 1], (1, N_HEADS)) for g in range(3)]

    h0, h0t = _normmod_fwd(x, g0, scale0, shift0, "prenorm0")
    nsa = own_wa_in.shape[2]
    tiles_a = dict(tn=nsa, total_tiles=N_CHIPS, part_of=lambda tile: 0)
    own_ids, rest_ids, own_tiles = _own_first(chip, N_CHIPS)
    proj_a = _in_tiles([h0], own_wa_in, own_ids, own_tiles, name="a_in_own", **tiles_a)
    wa_in, wa_out = weights_a(proj_a)
    ja = wa_in.shape[0]
    proj_a = _in_tiles([h0], wa_in, rest_ids, N_CHIPS - own_tiles, name="a_in_rest", prev=proj_a, **tiles_a)
    u5, u5t, u2 = _conv_fwd(proj_a, conv_w, conv_b, ln_g, ln_b, "a_conv")
    x1, y_a, h1t, h1c = _out_a(u5, wa_out, x, gate0, g1, scale1, shift1, "a_out")

    tiles_b = dict(tn=B_TN, total_tiles=B_TILES,
                   part_of=lambda tile: jnp.where(tile >= B_Z_TILE0, 0, tile // B_GROUP_TILES))
    own_ids, rest_ids, own_tiles = _own_first(chip, B_TILES)
    proj_b = _in_tiles(h1c, own_wb_in, own_ids, own_tiles, name="b_in_own", **tiles_b)
    forward_weights_b(proj_b)
    wb_in, wb_out = weights_b(proj_b)
    jb, _, nsb = wb_in.shape
    proj_b = _in_tiles(h1c, wb_in, rest_ids, B_TILES - own_tiles, name="b_in_rest", prev=proj_b, **tiles_b)
    h1 = h1c[0]
    qkv, o_parts, lse_parts = [], [], []
    for g, d in enumerate(DILATIONS):
        qn, kn = _qknorm_fwd(proj_b, g, qw[g], kw[g], seg, f"b_qknorm_g{g}")
        og, lg = _attn_fwd(qn, kn, proj_b, g, slopes, d, f"b_attn_g{g}")
        qkv.append((qn, kn))
        o_parts.append(og if d == 1 else og.reshape(d, SEQ // d, D_MODEL))
        lse_parts.append(lg if d == 1 else lg.reshape(d, SEQ // d, LANES))
    sel = _head_selector()
    u_b, u_bt, o_b, lse_b = _merge_fwd(o_parts, lse_parts, proj_b, sel, "b_merge")
    e, dy_b, sums_loss = _out_b_loss(u_b, wb_out, x1, gate1, target, "b_out_loss")

    dwb_out = _mm(u_bt, dy_b, tn=D_MODEL, tile0=0, n_tiles=1, out_dtype=BF16, name="b_dwout")
    dz_b, do_c, delta_c, lse_c = _merge_bwd(dy_b, wb_out, o_b, lse_b, proj_b, sel, "b_merge_bwd")
    dwb_in = _mm(h1t, dz_b, tn=B_TN, tile0=B_Z_TILE0, n_tiles=B_Z_TILES, out_dtype=BF16, name="b_dwin_z",
                 out3d=(jb, nsb))
    dh1_parts = [_mm_nt(dz_b, wb_in, tn=B_TN, tile0=B_Z_TILE0, n_tiles=B_Z_TILES, name="b_dh_z")]
    qk_sums = []
    for g, d in enumerate(DILATIONS):
        qn, kn = qkv[g]
        dq, dk, dv = _attn_bwd(qn, kn, proj_b, g, do_c[g], lse_c[g], delta_c[g], slopes, d, f"b_attn_bwd_g{g}")
        dproj, sums_qk = _qknorm_bwd(proj_b, g, qw[g], kw[g], seg, dq, dk, dv, f"b_qknorm_bwd_g{g}")
        qk_sums.append(sums_qk)
        dwb_in = _mm(h1t if d == 1 else h1c[g], dproj, tn=B_TN, tile0=g * B_GROUP_TILES, n_tiles=B_GROUP_TILES,
                     out_dtype=BF16, name=f"b_dwin_g{g}", out3d=(jb, nsb), prev=dwb_in, transpose_lhs=d != 1)
        dh = _mm_nt(dproj, wb_in, tn=B_TN, tile0=g * B_GROUP_TILES, n_tiles=B_GROUP_TILES, name=f"b_dh_g{g}")
        dh1_parts.append(dh)
    token = send_grads_b(dwb_in, dwb_out)
    dx1, sums_n1, dy_a = _normmod_bwd(x1, g1, scale1 + token[0:1, 0:1], dh1_parts, e, "prenorm1_bwd",
                                      part_dilations=(1,) + DILATIONS, gated=(gate0, y_a))
    token = forward_grads_b(dx1)

    dwa_out = _mm(u5t, dy_a, tn=D_MODEL, tile0=0, n_tiles=1, out_dtype=BF16, name="a_dwout")
    du2, dz_a, sums_ln = _conv_bwd_pointwise(dy_a, wa_out, proj_a, u2, ln_g + token[0:1, 0:1], ln_b,
                                             "a_conv_bwd_pw")
    dproj_a, dconv_w = _conv_bwd_taps(du2, dz_a, proj_a, conv_w, "a_conv_bwd_taps")
    dwa_in = _mm(h0t, dproj_a, tn=nsa, tile0=0, n_tiles=ja, out_dtype=BF16, name="a_dwin", out3d=(ja, nsa))
    token = send_grads_a(dwa_in, dwa_out)
    dh0 = _mm_nt(dproj_a, wa_in, tn=nsa, tile0=0, n_tiles=ja, name="a_dh", after=token)
    grad_x, sums_n0 = _normmod_bwd(x, g0, scale0, [dh0], dx1, "prenorm0_bwd")

    small = dict(
        dnorm_g=jnp.concatenate([sums_n0[0:1], sums_n1[0:1]], axis=0),
        dmod0=jnp.concatenate([sums_n0[2:3], sums_n0[1:2], sums_n1[3:4]], axis=0),
        dmod1=jnp.concatenate([sums_n1[2:3], sums_n1[1:2], sums_loss[0:1]], axis=0),
        dln_g=sums_ln[0:1], dln_b=sums_ln[1:2], dconv_b=sums_ln[2:3],
        dconv_w=dconv_w[:CONV_WIDTH],
        dq_norm=jnp.concatenate([s[0:1] for s in qk_sums], axis=0),
        dk_norm=jnp.concatenate([s[1:2] for s in qk_sums], axis=0),
        loss_cols=sums_loss[1:2],
    )
    return grad_x, small


def _adamw(w, g, m, v, name, after=None, copy_grad=False):
    rows, cols = w.shape
    tr = rows if rows <= 128 else 128
    c1 = 1.0 / (1.0 - ADAM_B1 ** ADAM_STEP)
    c2 = 1.0 / (1.0 - ADAM_B2 ** ADAM_STEP)
    extra = [] if after is None else [after]
    n_out = 4 if copy_grad else 3

    def body(w_ref, g_ref, m_ref, v_ref, *rest):
        d_ref, mo_ref, vo_ref = rest[len(extra):len(extra) + 3]
        gv = g_ref[...]
        if copy_grad:
            rest[-1][...] = gv
        mn = ADAM_B1 * m_ref[...] + (1.0 - ADAM_B1) * gv
        vn = ADAM_B2 * v_ref[...] + (1.0 - ADAM_B2) * (gv * gv)
        mo_ref[...] = mn
        vo_ref[...] = vn
        d_ref[...] = -ADAM_LR * ((mn * c1) / (jnp.sqrt(vn * c2) + ADAM_EPS) + ADAM_WD * w_ref[...])

    spec = pl.BlockSpec((tr, cols), lambda i: (i, 0))
    return pl.pallas_call(
        body, name=name, grid=(rows // tr,),
        in_specs=[spec] * 4 + [pl.BlockSpec(memory_space=pl.ANY)] * len(extra), out_specs=[spec] * n_out,
        out_shape=[jax.ShapeDtypeStruct((rows, cols), F32)] * n_out,
        compiler_params=_params("parallel"),
    )(w, g, m, v, *extra)


def _cast_into_slot(w, chip_idx, name, keep_own=False, after=None):
    rows, cols = w.shape
    tr = 256
    extra = [] if after is None else [after]

    def body(ch_ref, w_ref, *rest):
        wb = w_ref[...].astype(BF16)
        for o_ref in rest[len(extra):]:
            o_ref[...] = wb

    slot_spec = pl.BlockSpec((None, tr, cols), lambda i, ch: (ch[0], i, 0))
    own_spec = pl.BlockSpec((None, tr, cols), lambda i, ch: (0, i, 0))
    res = pl.pallas_call(
        body, name=name,
        grid_spec=pltpu.PrefetchScalarGridSpec(
            num_scalar_prefetch=1, grid=(rows // tr,),
            in_specs=[pl.BlockSpec((tr, cols), lambda i, ch: (i, 0))] + [pl.BlockSpec(memory_space=pl.ANY)] * len(extra),
            out_specs=[slot_spec, own_spec] if keep_own else [slot_spec]),
        out_shape=[jax.ShapeDtypeStruct((N_CHIPS, rows, cols), BF16)]
        + ([jax.ShapeDtypeStruct((1, rows, cols), BF16)] if keep_own else []),
        compiler_params=_params("parallel"),
    )(chip_idx, w, *extra)
    return tuple(res) if keep_own else res[0]


def _position():
    x, y, c = lax.axis_index("x"), lax.axis_index("y"), lax.axis_index("c")
    return x, y, c


def _xor_peer(x, y, c, k):
    return (x ^ ((k >> 2) & 1), y ^ ((k >> 1) & 1), c ^ (k & 1))


def _chip_peer(x, y, k):
    return (x ^ ((k >> 1) & 1), y ^ (k & 1))


def _ada_forward(c_row, ada_w, ada_b, conv_w):
    ns = ada_w.shape[2]
    cw = conv_w.shape[1]

    def body(c_ref, w_ref, b_ref, cv_ref, mod_ref, sc_ref, cvo_ref,
             c_all, mp, parts, cv_parts, send1, recv1, send2, recv2, send3, recv3):
        x, y, c = _position()
        me = 4 * x + 2 * y + c
        chip = 2 * x + y

        def c_copy(k):
            return pltpu.make_async_remote_copy(
                src_ref=c_all.at[me], dst_ref=c_all.at[me], send_sem=send1.at[k - 1], recv_sem=recv1.at[k - 1],
                device_id=_xor_peer(x, y, c, k), device_id_type=MESH)

        def cv_copy(k):
            px, py = _chip_peer(x, y, k)
            return pltpu.make_async_remote_copy(
                src_ref=cv_parts.at[chip], dst_ref=cv_parts.at[chip], send_sem=send3.at[k - 1],
                recv_sem=recv3.at[k - 1], device_id=(px, py, c), device_id_type=MESH)

        c_all[me] = c_ref[...]
        cv_parts[chip] = cv_ref[...]
        for k in range(1, N_DEV):
            c_copy(k).start()
        for k in range(1, N_CHIPS):
            cv_copy(k).start()
        for k in range(1, N_DEV):
            c_copy(k).wait_recv()
        cv = jnp.concatenate([c_all[i] for i in range(N_DEV)], axis=0)
        sc = cv * _sigmoid(cv)
        sc_ref[...] = sc
        for l in range(2):
            res = jnp.dot(sc, w_ref[l], preferred_element_type=F32, precision=lax.Precision.HIGHEST)
            for i in range(N_DEV):
                mp[i, l:l + 1, :] = res[i:i + 1, :]

        def mod_copy(k):
            px, py = _chip_peer(x, y, k)
            return pltpu.make_async_remote_copy(
                src_ref=mp.at[4 * px + 2 * py + c], dst_ref=parts.at[chip], send_sem=send2.at[k - 1],
                recv_sem=recv2.at[k - 1], device_id=(px, py, c), device_id_type=MESH)

        for k in range(1, N_CHIPS):
            mod_copy(k).start()
        parts[chip] = mp[me]
        for k in range(1, N_CHIPS):
            mod_copy(k).wait_recv()
            cv_copy(k).wait_recv()
        mod_ref[...] = jnp.concatenate([parts[j] for j in range(N_CHIPS)], axis=1) + b_ref[...]
        cvo_ref[...] = jnp.concatenate([cv_parts[j] for j in range(N_CHIPS)], axis=1)
        for k in range(1, N_DEV):
            c_copy(k).wait_send()
        for k in range(1, N_CHIPS):
            mod_copy(k).wait_send()
            cv_copy(k).wait_send()

    vm = pl.BlockSpec(memory_space=pltpu.VMEM)
    return pl.pallas_call(
        body, name="ada_forward",
        in_specs=[vm] * 4, out_specs=[vm] * 3,
        out_shape=[jax.ShapeDtypeStruct((2, 3 * D_MODEL), F32), jax.ShapeDtypeStruct((N_DEV, D_MODEL), F32),
                   jax.ShapeDtypeStruct((CONV_WIDTH, N_CHIPS * cw), F32)],
        scratch_shapes=[pltpu.VMEM((N_DEV, 1, D_MODEL), F32), pltpu.VMEM((N_DEV, 2, ns), F32),
                        pltpu.VMEM((N_CHIPS, 2, ns), F32), pltpu.VMEM((N_CHIPS, CONV_WIDTH, cw), F32),
                        pltpu.SemaphoreType.DMA((N_DEV - 1,)), pltpu.SemaphoreType.DMA((N_DEV - 1,)),
                        pltpu.SemaphoreType.DMA((N_CHIPS - 1,)), pltpu.SemaphoreType.DMA((N_CHIPS - 1,)),
                        pltpu.SemaphoreType.DMA((N_CHIPS - 1,)), pltpu.SemaphoreType.DMA((N_CHIPS - 1,))],
        compiler_params=pltpu.CompilerParams(vmem_limit_bytes=VMEM_LIMIT_BYTES),
    )(c_row, ada_w, ada_b, conv_w)


HBM_SPEC = pl.BlockSpec(memory_space=pltpu.HBM)
ANY_SPEC = pl.BlockSpec(memory_space=pl.ANY)
SEM_SPEC = pl.BlockSpec(memory_space=pltpu.SEMAPHORE)
SPLIT_PARAMS = dict(compiler_params=pltpu.CompilerParams(has_side_effects=pltpu.SideEffectType.DATAFLOW_SIDE_EFFECTING))
TOKEN = jax.ShapeDtypeStruct((8, 128), F32)


def _hbm(arrays):
    return [pltpu.with_memory_space_constraint(a, pltpu.HBM) for a in arrays]


def _hbm_like(arrays):
    return [pltpu.HBM(a.shape, a.dtype) for a in arrays]


def _gather_start(lands, after, name):
    n = len(lands)

    def body(*refs):
        ins = refs[:n]
        send, recv = refs[n + 1], refs[n + 2]
        x, y, c = _position()
        chip = 2 * x + y
        for t in range(n):
            rh = ins[t].shape[1] // 2
            for k in range(1, N_CHIPS):
                px, py = _chip_peer(x, y, k)
                block = ins[t].at[chip, pl.ds(c * rh, rh)]
                pltpu.make_async_remote_copy(
                    src_ref=block, dst_ref=block, send_sem=send.at[3 * t + k - 1], recv_sem=recv.at[3 * t + k - 1],
                    device_id=(px, py, c), device_id_type=MESH).start()
        refs[-1][...] = jnp.zeros(TOKEN.shape, F32)

    res = pl.pallas_call(
        body, name=name, in_specs=[HBM_SPEC] * n + [ANY_SPEC],
        out_specs=(SEM_SPEC, SEM_SPEC, *[HBM_SPEC] * n, pl.BlockSpec(memory_space=pltpu.VMEM)),
        out_shape=(pltpu.SemaphoreType.DMA((3 * n,)), pltpu.SemaphoreType.DMA((3 * n,)), *_hbm_like(lands), TOKEN),
        input_output_aliases={t: 2 + t for t in range(n)}, **SPLIT_PARAMS,
    )(*_hbm(lands), after)
    return res[0], res[1], list(res[2:2 + n]), res[-1]


def _gather_forward(send, recv, lands, after, name):
    n = len(lands)

    def body(*refs):
        ins = refs[:n]
        send1, recv1 = refs[n], refs[n + 1]
        send2, recv2 = refs[n + 3], refs[n + 4]
        x, y, c = _position()
        chip = 2 * x + y
        for t in range(n):
            rh = ins[t].shape[1] // 2
            half = pl.ds(c * rh, rh)
            for k in range(1, N_CHIPS):
                px, py = _chip_peer(x, y, k)
                s = 3 * t + k - 1
                got = ins[t].at[2 * px + py, half]
                cp = pltpu.make_async_remote_copy(
                    src_ref=ins[t].at[chip, half], dst_ref=got, send_sem=send1.at[s], recv_sem=recv1.at[s],
                    device_id=(px, py, c), device_id_type=MESH)
                cp.wait_send()
                cp.wait_recv()
                pltpu.make_async_remote_copy(
                    src_ref=got, dst_ref=got, send_sem=send2.at[s], recv_sem=recv2.at[s],
                    device_id=(x, y, 1 - c), device_id_type=MESH).start()
        refs[-1][...] = jnp.zeros(TOKEN.shape, F32)

    res = pl.pallas_call(
        body, name=name, in_specs=[HBM_SPEC] * n + [SEM_SPEC, SEM_SPEC, ANY_SPEC],
        out_specs=(SEM_SPEC, SEM_SPEC, *[HBM_SPEC] * n, pl.BlockSpec(memory_space=pltpu.VMEM)),
        out_shape=(pltpu.SemaphoreType.DMA((3 * n,)), pltpu.SemaphoreType.DMA((3 * n,)), *_hbm_like(lands), TOKEN),
        input_output_aliases={t: 2 + t for t in range(n)}, **SPLIT_PARAMS,
    )(*lands, send, recv, after)
    return res[0], res[1], list(res[2:2 + n]), res[-1]


def _gather_wait(send, recv, lands, after, name):
    n = len(lands)

    def body(*refs):
        ins = refs[:n]
        send_ref, recv_ref = refs[n], refs[n + 1]
        x, y, c = _position()
        for t in range(n):
            rh = ins[t].shape[1] // 2
            for k in range(1, N_CHIPS):
                px, py = _chip_peer(x, y, k)
                cp = pltpu.make_async_remote_copy(
                    src_ref=ins[t].at[2 * px + py, pl.ds(c * rh, rh)],
                    dst_ref=ins[t].at[2 * px + py, pl.ds((1 - c) * rh, rh)], send_sem=send_ref.at[3 * t + k - 1],
                    recv_sem=recv_ref.at[3 * t + k - 1], device_id=(x, y, 1 - c), device_id_type=MESH)
                cp.wait_send()
                cp.wait_recv()

    res = pl.pallas_call(
        body, name=name, in_specs=[HBM_SPEC] * n + [SEM_SPEC, SEM_SPEC, ANY_SPEC], out_specs=[HBM_SPEC] * n,
        out_shape=_hbm_like(lands), input_output_aliases={t: t for t in range(n)}, **SPLIT_PARAMS,
    )(*lands, send, recv, after)
    return list(res)


def _split_start(name, arrays, n_sems, after, issue):
    m = len(arrays)

    def body(*refs):
        issue(refs[:m], refs[m + 1], refs[m + 2])
        refs[-1][...] = jnp.zeros(TOKEN.shape, F32)

    res = pl.pallas_call(
        body, name=name, in_specs=[HBM_SPEC] * m + [ANY_SPEC],
        out_specs=(SEM_SPEC, SEM_SPEC, *[HBM_SPEC] * m, pl.BlockSpec(memory_space=pltpu.VMEM)),
        out_shape=(pltpu.SemaphoreType.DMA((n_sems,)), pltpu.SemaphoreType.DMA((n_sems,)), *_hbm_like(arrays), TOKEN),
        input_output_aliases={t: 2 + t for t in range(m)}, **SPLIT_PARAMS,
    )(*_hbm(arrays), after)
    return res[0], res[1], list(res[2:2 + m]), res[-1]


def _split_wait(name, arrays, send, recv, after, await_all):
    m = len(arrays)

    def body(*refs):
        await_all(refs[:m], refs[m], refs[m + 1])

    res = pl.pallas_call(
        body, name=name, in_specs=[HBM_SPEC] * m + [SEM_SPEC, SEM_SPEC, ANY_SPEC], out_specs=[HBM_SPEC] * m,
        out_shape=_hbm_like(arrays), input_output_aliases={t: t for t in range(m)}, **SPLIT_PARAMS,
    )(*arrays, send, recv, after)
    return list(res)


def _sibling_copies(refs, send, recv, n):
    x, y, c = _position()
    cps = []
    for t in range(n):
        rh = refs[t].shape[1] // 2
        cps.append(pltpu.make_async_remote_copy(
            src_ref=refs[t].at[pl.ds(0, N_CHIPS), pl.ds((1 - c) * rh, rh)], dst_ref=refs[n + t],
            send_sem=send.at[t], recv_sem=recv.at[t], device_id=(x, y, 1 - c), device_id_type=MESH))
    return cps


def _reduce_sibling_start(grads, after, name):
    n = len(grads)
    lands = [lax.empty((N_CHIPS, g.shape[1] // 2, g.shape[2]), BF16) for g in grads]

    def issue(refs, send, recv):
        for cp in _sibling_copies(refs, send, recv, n):
            cp.start()

    return _split_start(name, list(grads) + lands, n, after, issue)


def _reduce_sibling_wait(send, recv, arrays, after, name):
    n = len(arrays) // 2

    def await_all(refs, send_ref, recv_ref):
        for cp in _sibling_copies(refs, send_ref, recv_ref, n):
            cp.wait_send()
            cp.wait_recv()

    res = _split_wait(name, arrays, send, recv, after, await_all)
    return res[:n], res[n:]


def _add_sibling_half(grad, got, dev_idx, name):
    j, r, cols = grad.shape
    rh = r // 2
    tr = rh
    nb = rh // tr

    def body(idx_ref, g_ref, got_ref, out_ref):
        out_ref[...] = (g_ref[...].astype(F32) + got_ref[...].astype(F32)).astype(BF16)

    return pl.pallas_call(
        body, name=name,
        grid_spec=pltpu.PrefetchScalarGridSpec(
            num_scalar_prefetch=1, grid=(j, nb),
            in_specs=[pl.BlockSpec((None, tr, cols), lambda jj, i, idx: (jj, idx[2] * nb + i, 0)),
                      pl.BlockSpec((None, tr, cols), lambda jj, i, idx: (jj, i, 0))],
            out_specs=pl.BlockSpec((None, tr, cols), lambda jj, i, idx: (jj, i, 0))),
        out_shape=jax.ShapeDtypeStruct((j, rh, cols), BF16),
        compiler_params=_params("parallel", "parallel"),
    )(dev_idx, grad, got)


def _chip_copies(refs, send, recv, n, receiving):
    x, y, c = _position()
    chip = 2 * x + y
    cps = []
    for t in range(n):
        for k in range(1, N_CHIPS):
            px, py = _chip_peer(x, y, k)
            cps.append(pltpu.make_async_remote_copy(
                src_ref=refs[t].at[2 * px + py], dst_ref=refs[n + t].at[2 * px + py if receiving else chip],
                send_sem=send.at[3 * t + k - 1], recv_sem=recv.at[3 * t + k - 1],
                device_id=(px, py, c), device_id_type=MESH))
    return cps


def _reduce_chips_start(partials, after, name):
    n = len(partials)
    lands = [lax.empty(p.shape, BF16) for p in partials]

    def issue(refs, send, recv):
        for cp in _chip_copies(refs, send, recv, n, False):
            cp.start()

    return _split_start(name, list(partials) + lands, 3 * n, after, issue)


def _reduce_chips_wait(send, recv, arrays, after, name):
    n = len(arrays) // 2

    def await_all(refs, send_ref, recv_ref):
        for cp in _chip_copies(refs, send_ref, recv_ref, n, True):
            cp.wait_send()
            cp.wait_recv()

    res = _split_wait(name, arrays, send, recv, after, await_all)
    return res[:n], res[n:]


def _sum_partials(land, partial, dev_idx, name):
    _, rh, cols = land.shape
    tr = min(rh, 256)
    nb = rh // tr

    def body(idx_ref, l_ref, p_ref, o_ref):
        chip = idx_ref[1]
        acc = jnp.where(chip == 0, p_ref[...], l_ref[0]).astype(F32)
        for s in range(1, N_CHIPS):
            acc = acc + jnp.where(chip == s, p_ref[...], l_ref[s]).astype(F32)
        o_ref[...] = acc

    return pl.pallas_call(
        body, name=name,
        grid_spec=pltpu.PrefetchScalarGridSpec(
            num_scalar_prefetch=1, grid=(nb,),
            in_specs=[pl.BlockSpec((N_CHIPS, tr, cols), lambda i, idx: (0, i, 0)),
                      pl.BlockSpec((None, tr, cols), lambda i, idx: (idx[1], i, 0))],
            out_specs=pl.BlockSpec((tr, cols), lambda i, idx: (idx[2] * nb + i, 0))),
        out_shape=jax.ShapeDtypeStruct((2 * rh, cols), F32), compiler_params=_params("parallel"),
    )(dev_idx, land, partial)


def _half_copies(refs, send, recv, receiving):
    x, y, c = _position()
    cps = []
    for t, ref in enumerate(refs):
        rh = ref.shape[0] // 2
        cps.append(pltpu.make_async_remote_copy(
            src_ref=ref.at[pl.ds(c * rh, rh)], dst_ref=ref.at[pl.ds(((1 - c) if receiving else c) * rh, rh)],
            send_sem=send.at[t], recv_sem=recv.at[t], device_id=(x, y, 1 - c), device_id_type=MESH))
    return cps


def _share_halves_start(totals, after, name):
    def issue(refs, send, recv):
        for cp in _half_copies(refs, send, recv, False):
            cp.start()

    return _split_start(name, list(totals), len(totals), after, issue)


def _share_halves_wait(send, recv, totals, after, name):
    def await_all(refs, send_ref, recv_ref):
        for cp in _half_copies(refs, send_ref, recv_ref, True):
            cp.wait_send()
            cp.wait_recv()

    return _split_wait(name, totals, send, recv, after, await_all)


SMALL_ROWS = 56


def _small_copies(refs, send, recv, receiving):
    x, y, c = _position()
    me = 4 * x + 2 * y + c
    cps = []
    for k in range(1, N_DEV):
        px, py, pc = _xor_peer(x, y, c, k)
        cps.append(pltpu.make_async_remote_copy(
            src_ref=refs[0], dst_ref=refs[1].at[4 * px + 2 * py + pc if receiving else me],
            send_sem=send.at[k - 1], recv_sem=recv.at[k - 1], device_id=(px, py, pc), device_id_type=MESH))
    return cps


def _small_gather_start(packed, after):
    land = lax.empty((N_DEV,) + packed.shape, F32)

    def issue(refs, send, recv):
        for cp in _small_copies(refs, send, recv, False):
            cp.start()

    return _split_start("small_gather_start", [packed, land], N_DEV - 1, after, issue)


def _small_gather_wait(send, recv, arrays, after):
    def await_all(refs, send_ref, recv_ref):
        for cp in _small_copies(refs, send_ref, recv_ref, True):
            cp.wait_send()
            cp.wait_recv()

    return _split_wait("small_gather_wait", arrays, send, recv, after, await_all)


def _reduce_small(packed, land, silu_c):
    ns = 3 * D_MODEL // N_CHIPS

    def body(p_ref, land_ref, sc_ref, tot_ref, gw_ref, loss_ref, qk_ref, allp):
        x, y, c = _position()
        me = 4 * x + 2 * y + c
        chip = 2 * x + y
        for i in range(N_DEV):
            allp[i] = jnp.where(me == i, p_ref[...], land_ref[i])
        tot = allp[0]
        for i in range(1, N_DEV):
            tot = tot + allp[i]
        tot_ref[...] = tot
        loss_ref[...] = jnp.sum(tot[11:12, :], axis=1, keepdims=True) * (0.5 / D_MODEL)
        fold = tot[5:11, 0:HEAD_DIM]
        for h in range(1, N_HEADS):
            fold = fold + tot[5:11, h * HEAD_DIM:(h + 1) * HEAD_DIM]
        qk_ref[...] = jnp.concatenate([fold, jnp.zeros((2, HEAD_DIM), F32)], axis=0)
        sct = sc_ref[...].T
        rc = 64
        for l in range(2):
            dms = [allp[i, pl.ds(12 + 4 * l + chip, 1), :][:, :ns] for i in range(N_DEV)]
            for r0 in range(0, D_MODEL, rc):
                acc = sct[r0:r0 + rc, 0:1] * dms[0]
                for i in range(1, N_DEV):
                    acc = acc + sct[r0:r0 + rc, i:i + 1] * dms[i]
                gw_ref[l, r0:r0 + rc, :] = acc

    vm = pl.BlockSpec(memory_space=pltpu.VMEM)
    return pl.pallas_call(
        body, name="reduce_small", in_specs=[vm, vm, vm], out_specs=[vm] * 4,
        out_shape=[jax.ShapeDtypeStruct((SMALL_ROWS, D_MODEL), F32), jax.ShapeDtypeStruct((2, D_MODEL, ns), F32),
                   jax.ShapeDtypeStruct((1, 1), F32), jax.ShapeDtypeStruct((8, HEAD_DIM), F32)],
        scratch_shapes=[pltpu.VMEM((N_DEV, SMALL_ROWS, D_MODEL), F32)],
        compiler_params=pltpu.CompilerParams(vmem_limit_bytes=VMEM_LIMIT_BYTES),
    )(packed, land, silu_c)


def kernel(x, c, norm_g, ada_w, ada_b, a_w_in, a_conv_w, a_conv_b, a_ln_g, a_ln_b, a_w_out, b_w_in, b_q_norm, b_k_norm, b_w_out, loss_target, m_norm_g, m_ada_w, m_ada_b, m_a_w_in, m_a_conv_w, m_a_conv_b, m_a_ln_g, m_a_ln_b, m_a_w_out, m_b_w_in, m_b_q_norm, m_b_k_norm, m_b_w_out, v_norm_g, v_ada_w, v_ada_b, v_a_w_in, v_a_conv_w, v_a_conv_b, v_a_ln_g, v_a_ln_b, v_a_w_out, v_b_w_in, v_b_q_norm, v_b_k_norm, v_b_w_out):
    chip = 2 * lax.axis_index("x") + lax.axis_index("y")
    core = lax.axis_index("c")
    chip_idx = chip.astype(jnp.int32).reshape(1)
    dev_idx = jnp.stack([2 * chip + core, chip, core]).astype(jnp.int32)

    mods, silu_c, conv_w_full = _ada_forward(c, ada_w, ada_b, a_conv_w[0])
    land_a_in, own_wa_in = _cast_into_slot(a_w_in[0], chip_idx, "cast_a_w_in", keep_own=True)
    lands_a = [land_a_in, _cast_into_slot(a_w_out[0], chip_idx, "cast_a_w_out")]
    send_a, recv_a, lands_a, token_a = _gather_start(lands_a, mods, "gather_start_a")
    land_b_in, own_wb_in = _cast_into_slot(b_w_in[0], chip_idx, "cast_b_w_in", keep_own=True, after=token_a)
    lands_b = [land_b_in, _cast_into_slot(b_w_out[0], chip_idx, "cast_b_w_out", after=token_a)]
    send_b, recv_b, lands_b, token_b = _gather_start(lands_b, token_a, "gather_start_b")
    mods = mods + token_b[0:2, 0:1]

    def weights_a(after):
        send, recv, lands, _ = _gather_forward(send_a, recv_a, lands_a, after, "gather_forward_a")
        w_in, w_out = _gather_wait(send, recv, lands, after, "gather_wait_a")
        return w_in, w_out.reshape(D_MODEL, D_MODEL)

    forwarded_b = []

    def weights_b(after):
        send, recv, lands, _ = forwarded_b
        w_in, w_out = _gather_wait(send, recv, lands, after, "gather_wait_b")
        return w_in, w_out.reshape(D_MODEL, D_MODEL)

    def forward_weights_b(after):
        forwarded_b.extend(_gather_forward(send_b, recv_b, lands_b, after, "gather_forward_b"))
        return forwarded_b[3]

    stage1, stage2 = {}, {}

    def send_grads(tag, dw_in, dw_out):
        grads = [dw_in, dw_out.reshape(N_CHIPS, D_MODEL // N_CHIPS, D_MODEL)]
        send, recv, arrays, token = _reduce_sibling_start(grads, dw_out, f"reduce_d2d_start_{tag}")
        stage1[tag] = (send, recv, arrays)
        return token

    def forward_grads(tag, after):
        send, recv, arrays = stage1[tag]
        grads, got = _reduce_sibling_wait(send, recv, arrays, after, f"reduce_d2d_wait_{tag}")
        partials = [_add_sibling_half(grads[i], got[i], dev_idx, f"reduce_add_{tag}_{i}") for i in range(2)]
        send, recv, arrays, token = _reduce_chips_start(partials, partials[1], f"reduce_ici_start_{tag}")
        stage2[tag] = (send, recv, arrays)
        return token

    stage3 = {}

    def sum_grads(tag, after):
        send, recv, arrays = stage2[tag]
        partials, lands = _reduce_chips_wait(send, recv, arrays, after, f"reduce_ici_wait_{tag}")
        totals = [_sum_partials(lands[i], partials[i], dev_idx, f"reduce_sum_{tag}_{i}") for i in range(2)]
        send, recv, totals, token = _share_halves_start(totals, totals[1], f"reduce_share_start_{tag}")
        stage3[tag] = (send, recv, totals)
        return token

    def finish_grads(tag, after):
        send, recv, totals = stage3[tag]
        return _share_halves_wait(send, recv, totals, after, f"reduce_share_wait_{tag}")

    grad_x, small = _local_step(
        x[0], loss_target[0], mods.reshape(2, 3, D_MODEL), norm_g, conv_w_full, a_conv_b, a_ln_g[0:1],
        a_ln_b[0:1], b_q_norm[0], b_k_norm[0], chip.astype(jnp.int32), own_wa_in, own_wb_in,
        weights_a, weights_b, forward_weights_b,
        functools.partial(send_grads, "b"), functools.partial(forward_grads, "b"), functools.partial(send_grads, "a"))

    ns = 3 * D_MODEL // N_CHIPS
    pad_mod = lambda dm: jnp.pad(dm.reshape(N_CHIPS, ns), ((0, 0), (0, D_MODEL - ns)))
    packed = jnp.concatenate([
        small["dnorm_g"], small["dconv_b"], small["dln_g"], small["dln_b"], small["dq_norm"], small["dk_norm"],
        small["loss_cols"], pad_mod(small["dmod0"]), pad_mod(small["dmod1"]), small["dconv_w"],
        jnp.zeros((SMALL_ROWS - 20 - CONV_WIDTH, D_MODEL), F32)], axis=0)
    send_s, recv_s, small_arrays, token_s = _small_gather_start(packed, packed)

    given = dict(norm_g=(norm_g, m_norm_g, v_norm_g), ada_w=(ada_w, m_ada_w, v_ada_w), ada_b=(ada_b, m_ada_b, v_ada_b),
                 a_w_in=(a_w_in, m_a_w_in, v_a_w_in), a_conv_w=(a_conv_w, m_a_conv_w, v_a_conv_w),
                 a_conv_b=(a_conv_b, m_a_conv_b, v_a_conv_b), a_ln_g=(a_ln_g, m_a_ln_g, v_a_ln_g),
                 a_ln_b=(a_ln_b, m_a_ln_b, v_a_ln_b), a_w_out=(a_w_out, m_a_w_out, v_a_w_out),
                 b_w_in=(b_w_in, m_b_w_in, v_b_w_in), b_q_norm=(b_q_norm, m_b_q_norm, v_b_q_norm),
                 b_k_norm=(b_k_norm, m_b_k_norm, v_b_k_norm), b_w_out=(b_w_out, m_b_w_out, v_b_w_out))
    order = ["norm_g", "ada_w", "ada_b", "a_w_in", "a_conv_w", "a_conv_b", "a_ln_g", "a_ln_b", "a_w_out", "b_w_in",
             "b_q_norm", "b_k_norm", "b_w_out"]
    outs = {}

    def update(k, g2, after=None, copy_grad=False):
        w, m, v = given[k]
        shape2 = g2.shape
        res = _adamw(w.reshape(shape2), g2, m.reshape(shape2), v.reshape(shape2), f"adamw_{k}", after, copy_grad)
        outs[k] = tuple(a.reshape(w.shape) for a in ((res[3] if copy_grad else g2), res[0], res[1], res[2]))

    token = forward_grads("a", token_s)
    token = sum_grads("b", token)
    packed, land = _small_gather_wait(send_s, recv_s, small_arrays, token)
    tot, g_ada_w, loss, qk = _reduce_small(packed, land, silu_c)
    g_b_in, g_b_out = finish_grads("b", tot)
    update("b_w_in", g_b_in, copy_grad=True)
    update("b_w_out", g_b_out, copy_grad=True)
    token = sum_grads("a", outs["b_w_in"][1])
    cw = D_MODEL // N_CHIPS
    g_small = dict(
        norm_g=tot[0:2], a_conv_b=tot[2:3], a_ln_g=tot[3:4], a_ln_b=tot[4:5],
        b_q_norm=qk[0:3], b_k_norm=qk[3:6],
        ada_b=jnp.stack([tot[12:16, :ns].reshape(3 * D_MODEL), tot[16:20, :ns].reshape(3 * D_MODEL)]),
        a_conv_w=lax.dynamic_slice(tot[20:20 + CONV_WIDTH], (0, chip * cw), (CONV_WIDTH, cw)),
    )
    update("ada_w", g_ada_w.reshape(2 * D_MODEL, ns), after=token)
    for k, g2 in g_small.items():
        update(k, g2, after=token)
    g_a_in, g_a_out = finish_grads("a", outs["ada_w"][1])
    update("a_w_in", g_a_in, copy_grad=True)
    update("a_w_out", g_a_out, copy_grad=True)
    return (loss.reshape(()), grad_x[None], *[outs[k][0] for k in order], *[outs[k][1] for k in order],
            *[outs[k][2] for k in order], *[outs[k][3] for k in order])
```

```python
import functools

import jax
import jax.numpy as jnp
from jax import lax
from jax.experimental import pallas as pl
from jax.experimental.pallas import tpu as pltpu

F32 = jnp.float32
BF16 = jnp.bfloat16

SEQ = 2048
D_MODEL = 1024
CONV_WIDTH = 31
HEAD_DIM = 64
N_HEADS = 16
DILATIONS = (1, 4, 16)
ATTN_BLOCK = 128
NORM_EPS = 1e-6
NEG_INF = -1e30
N_DEV = 8
N_CHIPS = 4

ADAM_LR = 0.001
ADAM_B1 = 0.9
ADAM_B2 = 0.999
ADAM_EPS = 1e-08
ADAM_WD = 0.01
ADAM_STEP = 10

VMEM_LIMIT_BYTES = 52 * 1024 * 1024
HALO = 32
LANES = 128
MESH = pl.DeviceIdType.MESH


def _params(*sem):
    return pltpu.CompilerParams(dimension_semantics=sem or None, vmem_limit_bytes=VMEM_LIMIT_BYTES)


def _sigmoid(v):
    return 1.0 / (1.0 + jnp.exp(-v))


def _row_spec(tm, cols, col_block=0):
    return pl.BlockSpec((tm, cols), lambda i: (i, col_block))


def _vec_spec(rows, cols):
    return pl.BlockSpec((rows, cols), lambda i: (0, 0))


def _normmod(xv, g, scale, shift):
    r = lax.rsqrt(jnp.mean(xv * xv, axis=-1, keepdims=True) + NORM_EPS)
    return xv * r * g * (1.0 + scale) + shift


def _normmod_fwd(x, g, scale, shift, name):
    tm = 256

    def body(x_ref, g_ref, sc_ref, sh_ref, h_ref, ht_ref):
        h = _normmod(x_ref[...], g_ref[...], sc_ref[...], sh_ref[...])
        h_ref[...] = h.astype(BF16)
        ht_ref[...] = h.T.astype(BF16)

    return pl.pallas_call(
        body, name=name, grid=(SEQ // tm,),
        in_specs=[_row_spec(tm, D_MODEL)] + [_vec_spec(1, D_MODEL)] * 3,
        out_specs=[_row_spec(tm, D_MODEL), pl.BlockSpec((D_MODEL, tm), lambda i: (0, i))],
        out_shape=[jax.ShapeDtypeStruct((SEQ, D_MODEL), BF16), jax.ShapeDtypeStruct((D_MODEL, SEQ), BF16)],
        compiler_params=_params("parallel"),
    )(x, g, scale, shift)


def _normmod_bwd(x, g, scale, dh_parts, dres, name, part_dilations=None, gated=None):
    tm = 256
    n_parts = len(dh_parts)
    dils = part_dilations or (1,) * n_parts
    dh_parts = [p if d == 1 else p.reshape(d, SEQ // d, D_MODEL) for p, d in zip(dh_parts, dils)]
    n_gated = 0 if gated is None else 2

    def body(x_ref, g_ref, sc_ref, dres_ref, *rest):
        part_refs = rest[:n_parts]
        gated_refs = rest[n_parts:n_parts + n_gated]
        out_refs = rest[n_parts + n_gated:]
        dx_ref, sums_ref, nat = out_refs[0], out_refs[1], out_refs[-1]
        xv = x_ref[...]
        r = lax.rsqrt(jnp.mean(xv * xv, axis=-1, keepdims=True) + NORM_EPS)
        xn = xv * r
        dh = _load_natural(part_refs[0], nat, dils[0])
        for p, d in zip(part_refs[1:], dils[1:]):
            dh = dh + _load_natural(p, nat, d)
        gv = g_ref[...]
        one_sc = 1.0 + sc_ref[...]
        dxn = dh * (gv * one_sc)
        dx = dres_ref[...] + r * (dxn - xn * jnp.mean(dxn * xn, axis=-1, keepdims=True))
        dx_ref[...] = dx
        dhx = dh * xn
        rows = [jnp.sum(dhx, axis=0, keepdims=True) * one_sc,
                jnp.sum(dhx, axis=0, keepdims=True) * gv,
                jnp.sum(dh, axis=0, keepdims=True)]
        if gated is not None:
            gate_ref, y_ref = gated_refs
            out_refs[2][...] = (dx * gate_ref[...]).astype(BF16)
            rows.append(jnp.sum(dx * y_ref[...].astype(F32), axis=0, keepdims=True))
        sums = jnp.concatenate(rows + [jnp.zeros((8 - len(rows), D_MODEL), F32)], axis=0)

        @pl.when(pl.program_id(0) == 0)
        def _():
            sums_ref[...] = jnp.zeros_like(sums_ref)

        sums_ref[...] += sums

    gated_specs = [] if gated is None else [_vec_spec(1, D_MODEL), _row_spec(tm, D_MODEL)]
    dy_spec = [] if gated is None else [_row_spec(tm, D_MODEL)]
    dy_shape = [] if gated is None else [jax.ShapeDtypeStruct((SEQ, D_MODEL), BF16)]
    return pl.pallas_call(
        body, name=name, grid=(SEQ // tm,),
        in_specs=[_row_spec(tm, D_MODEL), _vec_spec(1, D_MODEL), _vec_spec(1, D_MODEL), _row_spec(tm, D_MODEL)]
        + [_class_spec(tm, d) for d in dils] + gated_specs,
        out_specs=[_row_spec(tm, D_MODEL), _vec_spec(8, D_MODEL)] + dy_spec,
        out_shape=[jax.ShapeDtypeStruct((SEQ, D_MODEL), F32), jax.ShapeDtypeStruct((8, D_MODEL), F32)] + dy_shape,
        scratch_shapes=[_natural_scratch(tm)],
        compiler_params=_params("arbitrary"),
    )(x, g, scale, dres, *dh_parts, *(gated or ()))


def _mm(lhs, rhs, *, tn, tile0, n_tiles, out_dtype, name, out3d=None, prev=None, transpose_lhs=False):
    mo, kc = lhs.shape[::-1] if transpose_lhs else lhs.shape
    cm = min(mo, 1024)
    tc = 256

    def body(l_ref, r_ref, *rest):
        if transpose_lhs:
            o_ref, lt_ref = rest[-2], rest[-1]

            @pl.when(pl.program_id(0) == 0)
            def _():
                for c in range(kc // tc):
                    lt_ref[:, c * tc:(c + 1) * tc] = l_ref[c * tc:(c + 1) * tc, :].astype(F32).T.astype(l_ref.dtype)
        else:
            o_ref, lt_ref = rest[-1], l_ref
        for m in range(mo // cm):
            rows = pl.ds(m * cm, cm)
            o_ref[rows, :] = jnp.dot(lt_ref[rows, :], r_ref[...], preferred_element_type=F32).astype(out_dtype)

    if rhs.ndim == 3:
        tps_r = rhs.shape[2] // tn
        r_spec = pl.BlockSpec((None, kc, tn), lambda t: ((tile0 + t) // tps_r, 0, (tile0 + t) % tps_r))
    else:
        r_spec = pl.BlockSpec((kc, tn), lambda t: (0, t))
    in_specs = [pl.BlockSpec(lhs.shape, lambda t: (0, 0)), r_spec]
    args = [lhs, rhs]
    aliases = {}
    if out3d is None:
        o_spec = pl.BlockSpec((mo, tn), lambda t: (0, t))
        o_shape = jax.ShapeDtypeStruct((mo, n_tiles * tn), out_dtype)
    else:
        j_out, ns_out = out3d
        tps_o = ns_out // tn
        o_spec = pl.BlockSpec((None, mo, tn), lambda t: ((tile0 + t) // tps_o, 0, (tile0 + t) % tps_o))
        o_shape = jax.ShapeDtypeStruct((j_out, mo, ns_out), out_dtype)
        if prev is not None:
            in_specs.append(pl.BlockSpec(memory_space=pl.ANY))
            args.append(prev)
            aliases = {2: 0}
    return pl.pallas_call(
        body, name=name, grid=(n_tiles,), in_specs=in_specs, out_specs=o_spec, out_shape=o_shape,
        input_output_aliases=aliases,
        scratch_shapes=[pltpu.VMEM((mo, kc), lhs.dtype)] if transpose_lhs else [],
        compiler_params=_params("arbitrary" if transpose_lhs else "parallel"),
    )(*args)


def _in_tiles(h_parts, w3, tile_ids, n_tiles, *, tn, total_tiles, part_of, name, prev=None):
    _, kc, ns = w3.shape
    tps = ns // tn
    cm = 1024
    n_parts = len(h_parts)

    def body(ids_ref, *rest):
        h_refs, w_ref, o_ref = rest[:n_parts], rest[n_parts], rest[-1]
        part = part_of(ids_ref[1, pl.program_id(0)])
        for g, h_ref in enumerate(h_refs):
            @pl.when(part == g)
            def _():
                for m in range(SEQ // cm):
                    rows = pl.ds(m * cm, cm)
                    o_ref[rows, :] = jnp.dot(h_ref[rows, :], w_ref[...], preferred_element_type=F32).astype(BF16)

    resident = pl.BlockSpec((SEQ, kc), lambda t, ids: (0, 0))
    in_specs = [resident] * n_parts + [
        pl.BlockSpec((None, kc, tn), lambda t, ids: (ids[0, t] // tps, 0, ids[0, t] % tps))]
    args = [*h_parts, w3]
    aliases = {}
    if prev is not None:
        in_specs.append(pl.BlockSpec(memory_space=pl.ANY))
        args.append(prev)
        aliases = {n_parts + 2: 0}
    return pl.pallas_call(
        body, name=name,
        grid_spec=pltpu.PrefetchScalarGridSpec(
            num_scalar_prefetch=1, grid=(n_tiles,), in_specs=in_specs,
            out_specs=pl.BlockSpec((SEQ, tn), lambda t, ids: (0, ids[1, t]))),
        out_shape=jax.ShapeDtypeStruct((SEQ, total_tiles * tn), BF16),
        input_output_aliases=aliases, compiler_params=_params("arbitrary"),
    )(tile_ids, *args)


def _own_first(chip, total_tiles):
    own = total_tiles // N_CHIPS
    step = jnp.arange(total_tiles, dtype=jnp.int32)
    tiles = (own * chip + step) % total_tiles
    return jnp.stack([step[:own], tiles[:own]]), jnp.stack([tiles[own:], tiles[own:]]), own


def _mm_nt(dy, w3, *, tn, tile0, n_tiles, name, after=None):
    m_rows = dy.shape[0]
    _, kc, ns = w3.shape
    tps = ns // tn
    cm = 512
    extra = [] if after is None else [after]

    def body(dy_ref, w_ref, *rest):
        o_ref, acc = rest[-2], rest[-1]
        t = pl.program_id(0)

        @pl.when(t == 0)
        def _():
            acc[...] = jnp.zeros_like(acc)

        for m in range(m_rows // cm):
            rows = pl.ds(m * cm, cm)
            acc[rows, :] += lax.dot_general(dy_ref[rows, :], w_ref[...], NT_DIMS, preferred_element_type=F32)

        @pl.when(t == n_tiles - 1)
        def _():
            o_ref[...] = acc[...].astype(BF16)

    return pl.pallas_call(
        body, name=name, grid=(n_tiles,),
        in_specs=[pl.BlockSpec((m_rows, tn), lambda t: (0, t)),
                  pl.BlockSpec((None, kc, tn), lambda t: ((tile0 + t) // tps, 0, (tile0 + t) % tps))]
        + [pl.BlockSpec(memory_space=pl.ANY)] * len(extra),
        out_specs=pl.BlockSpec((m_rows, kc), lambda t: (0, 0)),
        out_shape=jax.ShapeDtypeStruct((m_rows, kc), BF16),
        scratch_shapes=[pltpu.VMEM((m_rows, kc), F32)],
        compiler_params=_params("arbitrary"),
    )(dy, w3, *extra)


CONV_CHUNK = 16


def _shift_copies(buf, shifted):
    rows = shifted.shape[1]
    for s in range(1, 8):
        shifted[s - 1] = buf[pl.ds(s, rows), :]


def _shifted_rows(buf, shifted, offset, r0):
    s = offset % 8
    if s == 0:
        return buf[pl.ds(r0 + offset, CONV_CHUNK), :]
    return shifted[s - 1, pl.ds(r0 + (offset - s), CONV_CHUNK), :]


def _spread_taps(w_ref, taps):
    for k in range(CONV_WIDTH):
        taps[k] = jnp.broadcast_to(w_ref[k:k + 1, :], (8, D_MODEL))


def _times_tap(taps, k, rows):
    return (rows.reshape(CONV_CHUNK // 8, 8, D_MODEL) * taps[k][None]).reshape(CONV_CHUNK, D_MODEL)


def _conv_fwd(proj, conv_w, conv_b, ln_g, ln_b, name):
    tm = 256
    hb = tm // HALO

    def body(vg_ref, halo_ref, z_ref, w_ref, b_ref, g_ref, be_ref, u5_ref, u5t_ref, u2_ref, buf, shifted, taps):
        i = pl.program_id(0)
        u1 = vg_ref[:, :D_MODEL].astype(F32) * _sigmoid(vg_ref[:, D_MODEL:].astype(F32))
        u1h = halo_ref[:, :D_MODEL].astype(F32) * _sigmoid(halo_ref[:, D_MODEL:].astype(F32))
        buf[pl.ds(0, HALO), :] = jnp.where(i > 0, u1h, 0.0)
        buf[pl.ds(HALO, tm), :] = u1
        _shift_copies(buf, shifted)
        _spread_taps(w_ref, taps)

        def chunk(ci, carry):
            r0 = pl.multiple_of(ci * CONV_CHUNK, CONV_CHUNK)
            acc = jnp.broadcast_to(b_ref[...], (CONV_CHUNK, D_MODEL))
            for k in range(CONV_WIDTH):
                acc = acc + _times_tap(taps, k, _shifted_rows(buf, shifted, HALO - (CONV_WIDTH - 1) + k, r0))
            u2_ref[pl.ds(r0, CONV_CHUNK), :] = acc
            return carry

        lax.fori_loop(0, tm // CONV_CHUNK, chunk, 0)
        acc = u2_ref[...]
        mu = jnp.mean(acc, axis=-1, keepdims=True)
        xc = acc - mu
        rstd = lax.rsqrt(jnp.mean(xc * xc, axis=-1, keepdims=True) + NORM_EPS)
        u3 = xc * rstd * g_ref[...] + be_ref[...]
        zv = z_ref[...].astype(F32)
        u5 = u3 * _sigmoid(u3) * (zv * _sigmoid(zv))
        u5_ref[...] = u5.astype(BF16)
        u5t_ref[...] = u5.T.astype(BF16)

    return pl.pallas_call(
        body, name=name, grid=(SEQ // tm,),
        in_specs=[pl.BlockSpec((tm, 2 * D_MODEL), lambda i: (i, 0)),
                  pl.BlockSpec((HALO, 2 * D_MODEL), lambda i: (jnp.maximum(i * hb - 1, 0), 0)),
                  _row_spec(tm, D_MODEL, 2),
                  _vec_spec(CONV_WIDTH, D_MODEL)] + [_vec_spec(1, D_MODEL)] * 3,
        out_specs=[_row_spec(tm, D_MODEL), pl.BlockSpec((D_MODEL, tm), lambda i: (0, i)), _row_spec(tm, D_MODEL)],
        out_shape=[jax.ShapeDtypeStruct((SEQ, D_MODEL), BF16), jax.ShapeDtypeStruct((D_MODEL, SEQ), BF16),
                   jax.ShapeDtypeStruct((SEQ, D_MODEL), F32)],
        scratch_shapes=[pltpu.VMEM((HALO + tm, D_MODEL), F32), pltpu.VMEM((7, HALO + tm - 8, D_MODEL), F32),
                        pltpu.VMEM((CONV_WIDTH, 8, D_MODEL), F32)],
        compiler_params=_params("parallel"),
    )(proj, proj, proj, conv_w, conv_b, ln_g, ln_b)


def _conv_bwd_pointwise(dy, w_out, proj, u2, ln_g, ln_b, name):
    tm = 256

    def body(dy_ref, w_ref, z_ref, u2_ref, g_ref, be_ref, du2_ref, dz_ref, sums_ref):
        u2v = u2_ref[...]
        mu = jnp.mean(u2v, axis=-1, keepdims=True)
        xc = u2v - mu
        rstd = lax.rsqrt(jnp.mean(xc * xc, axis=-1, keepdims=True) + NORM_EPS)
        xhat = xc * rstd
        u3 = xhat * g_ref[...] + be_ref[...]
        s3 = _sigmoid(u3)
        u4 = u3 * s3
        zv = z_ref[...].astype(F32)
        sz = _sigmoid(zv)
        du5v = lax.dot_general(dy_ref[...], w_ref[...], NT_DIMS, preferred_element_type=F32)
        dz_ref[...] = du5v * u4 * (sz * (1.0 + zv * (1.0 - sz)))
        du3 = du5v * (zv * sz) * (s3 * (1.0 + u3 * (1.0 - s3)))
        dxhat = du3 * g_ref[...]
        du2 = rstd * (dxhat - jnp.mean(dxhat, axis=-1, keepdims=True)
                      - xhat * jnp.mean(dxhat * xhat, axis=-1, keepdims=True))
        du2_ref[...] = du2
        sums = jnp.concatenate([
            jnp.sum(du3 * xhat, axis=0, keepdims=True),
            jnp.sum(du3, axis=0, keepdims=True),
            jnp.sum(du2, axis=0, keepdims=True),
            jnp.zeros((5, D_MODEL), F32)], axis=0)

        @pl.when(pl.program_id(0) == 0)
        def _():
            sums_ref[...] = jnp.zeros_like(sums_ref)

        sums_ref[...] += sums

    return pl.pallas_call(
        body, name=name, grid=(SEQ // tm,),
        in_specs=[_row_spec(tm, D_MODEL), _vec_spec(D_MODEL, D_MODEL), _row_spec(tm, D_MODEL, 2),
                  _row_spec(tm, D_MODEL), _vec_spec(1, D_MODEL), _vec_spec(1, D_MODEL)],
        out_specs=[_row_spec(tm, D_MODEL), _row_spec(tm, D_MODEL), _vec_spec(8, D_MODEL)],
        out_shape=[jax.ShapeDtypeStruct((SEQ, D_MODEL), F32), jax.ShapeDtypeStruct((SEQ, D_MODEL), F32),
                   jax.ShapeDtypeStruct((8, D_MODEL), F32)],
        compiler_params=_params("arbitrary"),
    )(dy, w_out, proj, u2, ln_g, ln_b)


def _conv_bwd_taps(du2, dz, proj, conv_w, name):
    tm = 256
    hb = tm // HALO
    n_blocks = SEQ // tm

    def body(du2_ref, dnext_ref, dz_ref, vg_ref, w_ref, dproj_ref, dw_ref, dbuf, dshift, sgbuf, ubuf, dwacc, taps):
        i = pl.program_id(0)
        _spread_taps(w_ref, taps)
        sg = _sigmoid(vg_ref[:, D_MODEL:].astype(F32))
        sgbuf[...] = sg
        ubuf[...] = vg_ref[:, :D_MODEL].astype(F32) * sg
        dbuf[pl.ds(0, tm), :] = du2_ref[...]
        dbuf[pl.ds(tm, HALO), :] = jnp.where(i < n_blocks - 1, dnext_ref[...], 0.0)
        _shift_copies(dbuf, dshift)

        @pl.when(i == 0)
        def _():
            dwacc[...] = jnp.zeros_like(dwacc)

        def chunk(ci, carry):
            r0 = pl.multiple_of(ci * CONV_CHUNK, CONV_CHUNK)
            rows = pl.ds(r0, CONV_CHUNK)
            u1c = ubuf[rows, :]
            du1 = jnp.zeros((CONV_CHUNK, D_MODEL), F32)
            for k in range(CONV_WIDTH):
                ahead = _shifted_rows(dbuf, dshift, CONV_WIDTH - 1 - k, r0)
                du1 = du1 + _times_tap(taps, k, ahead)
                prod = u1c * ahead
                dwacc[k] += prod[0:8] + prod[8:16]
            sgc = sgbuf[rows, :]
            dval = du1 * sgc
            dproj_ref[rows, 0:D_MODEL] = dval.astype(BF16)
            dproj_ref[rows, D_MODEL:2 * D_MODEL] = (
                dval * vg_ref[rows, 0:D_MODEL].astype(F32) * (1.0 - sgc)).astype(BF16)
            return carry

        lax.fori_loop(0, tm // CONV_CHUNK, chunk, 0)
        dproj_ref[:, 2 * D_MODEL:] = dz_ref[...].astype(BF16)

        @pl.when(i == n_blocks - 1)
        def _():
            for k in range(CONV_WIDTH):
                dw_ref[k:k + 1, :] = jnp.sum(dwacc[k], axis=0, keepdims=True)
            dw_ref[CONV_WIDTH:, :] = jnp.zeros((32 - CONV_WIDTH, D_MODEL), F32)

    return pl.pallas_call(
        body, name=name, grid=(n_blocks,),
        in_specs=[_row_spec(tm, D_MODEL),
                  pl.BlockSpec((HALO, D_MODEL), lambda i: (jnp.minimum((i + 1) * hb, SEQ // HALO - 1), 0)),
                  _row_spec(tm, D_MODEL),
                  pl.BlockSpec((tm, 2 * D_MODEL), lambda i: (i, 0)),
                  _vec_spec(CONV_WIDTH, D_MODEL)],
        out_specs=[_row_spec(tm, 3 * D_MODEL), _vec_spec(32, D_MODEL)],
        out_shape=[jax.ShapeDtypeStruct((SEQ, 3 * D_MODEL), BF16), jax.ShapeDtypeStruct((32, D_MODEL), F32)],
        scratch_shapes=[pltpu.VMEM((tm + HALO, D_MODEL), F32), pltpu.VMEM((7, HALO + tm - 8, D_MODEL), F32),
                        pltpu.VMEM((tm, D_MODEL), F32), pltpu.VMEM((tm, D_MODEL), F32),
                        pltpu.VMEM((CONV_WIDTH, 8, D_MODEL), F32), pltpu.VMEM((CONV_WIDTH, 8, D_MODEL), F32)],
        compiler_params=_params("arbitrary"),
    )(du2, du2, dz, proj, conv_w)


def _out_a(u5, w_out, x, gate, g1, scale1, shift1, name):
    tm = 256
    n_d = len(DILATIONS)

    def body(u_ref, w_ref, x_ref, gate_ref, g_ref, sc_ref, sh_ref, x1_ref, y_ref, ht_ref, *rest):
        h_refs, nat = rest[:n_d], rest[-1]
        y = jnp.dot(u_ref[...], w_ref[...], preferred_element_type=F32)
        x1 = x_ref[...] + gate_ref[...] * y
        y_ref[...] = y.astype(BF16)
        x1_ref[...] = x1
        h = _normmod(x1, g_ref[...], sc_ref[...], sh_ref[...])
        ht_ref[...] = h.T.astype(BF16)
        for h_ref, d in zip(h_refs, DILATIONS):
            _store_classes(h_ref, h, nat, d)

    res = pl.pallas_call(
        body, name=name, grid=(SEQ // tm,),
        in_specs=[_row_spec(tm, D_MODEL), _vec_spec(D_MODEL, D_MODEL), _row_spec(tm, D_MODEL)]
        + [_vec_spec(1, D_MODEL)] * 4,
        out_specs=[_row_spec(tm, D_MODEL), _row_spec(tm, D_MODEL), pl.BlockSpec((D_MODEL, tm), lambda i: (0, i))]
        + [_class_spec(tm, d) for d in DILATIONS],
        out_shape=[jax.ShapeDtypeStruct((SEQ, D_MODEL), F32), jax.ShapeDtypeStruct((SEQ, D_MODEL), BF16),
                   jax.ShapeDtypeStruct((D_MODEL, SEQ), BF16)] + [_class_shape(d, BF16) for d in DILATIONS],
        scratch_shapes=[_natural_scratch(tm)],
        compiler_params=_params("parallel"),
    )(u5, w_out, x, gate, g1, scale1, shift1)
    return res[0], res[1], res[2], [a.reshape(SEQ, D_MODEL) for a in res[3:]]


def _out_b_loss(u, w_out, x1, gate, target, name):
    tm = 256

    def body(u_ref, w_ref, x_ref, gate_ref, t_ref, e_ref, dy_ref, sums_ref):
        y = jnp.dot(u_ref[...], w_ref[...], preferred_element_type=F32)
        diff = x_ref[...] + gate_ref[...] * y - t_ref[...]
        e = diff * (1.0 / D_MODEL)
        e_ref[...] = e
        dy_ref[...] = (e * gate_ref[...]).astype(BF16)
        sums = jnp.concatenate([
            jnp.sum(e * y, axis=0, keepdims=True),
            jnp.sum(diff * diff, axis=0, keepdims=True),
            jnp.zeros((6, D_MODEL), F32)], axis=0)

        @pl.when(pl.program_id(0) == 0)
        def _():
            sums_ref[...] = jnp.zeros_like(sums_ref)

        sums_ref[...] += sums

    return pl.pallas_call(
        body, name=name, grid=(SEQ // tm,),
        in_specs=[_row_spec(tm, D_MODEL), _vec_spec(D_MODEL, D_MODEL), _row_spec(tm, D_MODEL),
                  _vec_spec(1, D_MODEL), _row_spec(tm, D_MODEL)],
        out_specs=[_row_spec(tm, D_MODEL), _row_spec(tm, D_MODEL), _vec_spec(8, D_MODEL)],
        out_shape=[jax.ShapeDtypeStruct((SEQ, D_MODEL), F32), jax.ShapeDtypeStruct((SEQ, D_MODEL), BF16),
                   jax.ShapeDtypeStruct((8, D_MODEL), F32)],
        compiler_params=_params("arbitrary"),
    )(u, w_out, x1, gate, target)


def _seg_matrix():
    r = lax.broadcasted_iota(jnp.int32, (256, 256), 0) // HEAD_DIM
    c = lax.broadcasted_iota(jnp.int32, (256, 256), 1) // HEAD_DIM
    return (r == c).astype(BF16)


def _segsum(v, seg):
    hi = v.astype(BF16)
    lo = (v - hi.astype(F32)).astype(BF16)
    outs = []
    for c0 in range(0, D_MODEL, 256):
        outs.append(jnp.dot(hi[:, c0:c0 + 256], seg, preferred_element_type=F32)
                    + jnp.dot(lo[:, c0:c0 + 256], seg, preferred_element_type=F32))
    return jnp.concatenate(outs, axis=1)


def _qk_rstd(v, seg):
    return lax.rsqrt(_segsum(v * v, seg) * (1.0 / HEAD_DIM) + NORM_EPS)


def _qknorm_fwd(proj, group, qw, kw, seg, name):
    tm = 256

    def body(q_in, k_in, qw_ref, kw_ref, seg_ref, q_ref, k_ref):
        segv = seg_ref[...]
        q = q_in[...].astype(F32)
        k = k_in[...].astype(F32)
        q_ref[...] = (q * _qk_rstd(q, segv) * qw_ref[...] * HEAD_DIM ** -0.5).astype(BF16)
        k_ref[...] = (k * _qk_rstd(k, segv) * kw_ref[...]).astype(BF16)

    return pl.pallas_call(
        body, name=name, grid=(SEQ // tm,),
        in_specs=[_row_spec(tm, D_MODEL, 3 * group), _row_spec(tm, D_MODEL, 3 * group + 1),
                  _vec_spec(1, D_MODEL), _vec_spec(1, D_MODEL), _vec_spec(256, 256)],
        out_specs=[_row_spec(tm, D_MODEL)] * 2,
        out_shape=[jax.ShapeDtypeStruct((SEQ, D_MODEL), BF16)] * 2,
        compiler_params=_params("parallel"),
    )(proj, proj, qw, kw, seg)


def _attn_masks(b, bpc, dilation, transposed=False):
    keys = ATTN_BLOCK if bpc == 1 else 2 * ATTN_BLOCK
    shape, q_axis = ((keys, ATTN_BLOCK), 1) if transposed else ((ATTN_BLOCK, keys), 0)
    qi = lax.broadcasted_iota(jnp.int32, shape, q_axis)
    kj = lax.broadcasted_iota(jnp.int32, shape, 1 - q_axis)
    if bpc == 1:
        steps = qi - kj
        return (steps * dilation).astype(F32), steps >= 0
    steps = qi + ATTN_BLOCK - kj
    has_prev = (b % bpc) != 0
    valid = (steps >= 0) & (steps <= ATTN_BLOCK) & (has_prev | (kj >= ATTN_BLOCK))
    return (steps * dilation).astype(F32), valid


MASKED = 1e30


def _bias_scratch(bpc):
    return pltpu.VMEM((1 if bpc == 1 else 2, N_HEADS, ATTN_BLOCK, (1 if bpc == 1 else 2) * ATTN_BLOCK), F32)


def _fill_bias(bias_ref, sl_ref, bpc, dilation):
    for variant in range(bias_ref.shape[0]):
        dist, valid = _attn_masks(variant, min(bpc, 2), dilation)
        bias_ref[variant] = jnp.where(valid[None], dist[None] * sl_ref[...], MASKED)


def _step_bias(bias_ref, b, bpc):
    if bpc == 1:
        return bias_ref[0]
    return bias_ref[jnp.where((b % bpc) != 0, 1, 0)]


def _key_tile(prev_ref, cur_ref, cols, bpc):
    if bpc == 1:
        return cur_ref[:, cols]
    return jnp.concatenate([prev_ref[:, cols], cur_ref[:, cols]], axis=0)


ATTN_HEADS_FWD = 16
ATTN_HEADS_BWD = 16
NT_DIMS = (((1,), (1,)), ((), ()))
BATCH_NT_DIMS = (((2,), (2,)), ((0,), (0,)))
BATCH_NN_DIMS = (((2,), (1,)), ((0,), (0,)))
BATCH_TN_DIMS = (((1,), (1,)), ((0,), (0,)))


def _head_stack(tile_of, heads):
    return jnp.stack([tile_of(slice(h * HEAD_DIM, (h + 1) * HEAD_DIM)) for h in range(heads)], axis=0)


def _attn_specs(heads, segment=0):
    width = heads * HEAD_DIM
    off = segment * (D_MODEL // width)
    last = SEQ // ATTN_BLOCK - 1
    cur = pl.BlockSpec((ATTN_BLOCK, width), lambda hg, b: (jnp.minimum(b, last), hg + off))
    prev = pl.BlockSpec((ATTN_BLOCK, width), lambda hg, b: (jnp.clip(b - 1, 0, last), hg + off))
    return cur, prev


def _attn_fwd(q, k, proj, group, slopes, dilation, name):
    bpc = SEQ // dilation // ATTN_BLOCK
    heads = ATTN_HEADS_FWD
    assert heads == N_HEADS
    cur, prev = _attn_specs(heads)
    v_cur, v_prev = _attn_specs(heads, segment=3 * group + 2)

    def body(sl_ref, q_ref, kp_ref, kc_ref, vp_ref, vc_ref, o_ref, lse_ref, bias_ref):
        b = pl.program_id(1)

        @pl.when(b == 0)
        def _():
            _fill_bias(bias_ref, sl_ref, bpc, dilation)

        q3 = _head_stack(lambda cols: q_ref[:, cols], heads)
        k3 = _head_stack(lambda cols: _key_tile(kp_ref, kc_ref, cols, bpc), heads)
        v3 = _head_stack(lambda cols: _key_tile(vp_ref, vc_ref, cols, bpc), heads)
        s = lax.dot_general(q3, k3, BATCH_NT_DIMS, preferred_element_type=F32)
        s = s - _step_bias(bias_ref, b, bpc)
        m = jnp.max(s, axis=-1, keepdims=True)
        p = jnp.exp(s - m)
        l = jnp.sum(p, axis=-1, keepdims=True)
        o3 = lax.dot_general(p.astype(BF16), v3, BATCH_NN_DIMS, preferred_element_type=F32) / l
        lse3 = m + jnp.log(l)
        for h in range(heads):
            o_ref[:, h * HEAD_DIM:(h + 1) * HEAD_DIM] = o3[h].astype(BF16)
        lse_ref[...] = jnp.concatenate([lse3[h] for h in range(heads)]
                                       + [jnp.zeros((ATTN_BLOCK, LANES - heads), F32)], axis=1)

    return pl.pallas_call(
        body, name=name, grid=(N_HEADS // heads, SEQ // ATTN_BLOCK),
        in_specs=[pl.BlockSpec((heads, 1, 1), lambda hg, b: (hg, 0, 0)), cur, prev, cur, v_prev, v_cur],
        out_specs=[cur, pl.BlockSpec((ATTN_BLOCK, LANES), lambda hg, b: (b, 0))],
        out_shape=[jax.ShapeDtypeStruct((SEQ, D_MODEL), BF16), jax.ShapeDtypeStruct((SEQ, LANES), F32)],
        scratch_shapes=[_bias_scratch(bpc)],
        compiler_params=_params("parallel", "arbitrary"),
    )(slopes.reshape(N_HEADS, 1, 1), q, k, k, proj, proj)


def _class_spec(tm, dilation, width=D_MODEL):
    if dilation == 1:
        return _row_spec(tm, width)
    return pl.BlockSpec((dilation, tm // dilation, width), lambda i: (0, i, 0))


def _class_shape(dilation, dtype, width=D_MODEL):
    if dilation == 1:
        return jax.ShapeDtypeStruct((SEQ, width), dtype)
    return jax.ShapeDtypeStruct((dilation, SEQ // dilation, width), dtype)


def _load_natural(in_ref, nat_ref, dilation):
    if dilation == 1:
        return in_ref[...].astype(F32)
    n = nat_ref.shape[1] // dilation
    tiles = in_ref.shape[-1] // LANES
    for r in range(dilation):
        for j in range(tiles):
            nat_ref.at[j][pl.ds(r, n, stride=dilation), :] = in_ref[r, :, j * LANES:(j + 1) * LANES].astype(F32)
    if tiles == 1:
        return nat_ref[0]
    return jnp.concatenate([nat_ref[j] for j in range(tiles)], axis=1)


def _store_classes(out_ref, value, nat_ref, dilation):
    if dilation == 1:
        out_ref[...] = value.astype(out_ref.dtype)
        return
    n = nat_ref.shape[1] // dilation
    tiles = value.shape[-1] // LANES
    for j in range(tiles):
        nat_ref[j] = value[:, j * LANES:(j + 1) * LANES]
    for r in range(dilation):
        for j in range(tiles):
            out_ref[r, :, j * LANES:(j + 1) * LANES] = (
                nat_ref.at[j][pl.ds(r, n, stride=dilation), :].astype(out_ref.dtype))


def _natural_scratch(tm):
    return pltpu.VMEM((D_MODEL // LANES, tm, LANES), F32)


def _head_selector():
    lane_head = lax.broadcasted_iota(jnp.int32, (D_MODEL, LANES), 0) // HEAD_DIM
    head = lax.broadcasted_iota(jnp.int32, (D_MODEL, LANES), 1)
    return (lane_head == head).astype(BF16)


def _dot_split(v, m01, dims):
    hi = v.astype(BF16)
    lo = (v - hi.astype(F32)).astype(BF16)
    return (lax.dot_general(hi, m01, dims, preferred_element_type=F32)
            + lax.dot_general(lo, m01, dims, preferred_element_type=F32))


def _merge_fwd(o_parts, lse_parts, z, sel, name):
    tm = 256
    h_spec = pl.BlockSpec((tm, LANES), lambda i: (i, 0))

    def body(o0, o1, o2, l0, l1, l2, z_ref, sel_ref, u_ref, ut_ref, o_ref, lse_ref, nat):
        ls = [_load_natural(l, nat, d) for l, d in zip((l0, l1, l2), DILATIONS)]
        m = jnp.maximum(jnp.maximum(ls[0], ls[1]), ls[2])
        tot = m + jnp.log(jnp.exp(ls[0] - m) + jnp.exp(ls[1] - m) + jnp.exp(ls[2] - m))
        o = jnp.zeros((tm, D_MODEL), F32)
        for o_in, l, d in zip((o0, o1, o2), ls, DILATIONS):
            weight = _dot_split(jnp.exp(l - tot), sel_ref[...], NT_DIMS)
            o = o + weight * _load_natural(o_in, nat, d)
        zv = z_ref[...].astype(F32)
        u = o * (zv * _sigmoid(zv))
        u_ref[...] = u.astype(BF16)
        ut_ref[...] = u.T.astype(BF16)
        o_ref[...] = o.astype(BF16)
        lse_ref[...] = tot

    return pl.pallas_call(
        body, name=name, grid=(SEQ // tm,),
        in_specs=[_class_spec(tm, d) for d in DILATIONS] + [_class_spec(tm, d, LANES) for d in DILATIONS]
        + [_row_spec(tm, D_MODEL, B_Z_SEGMENT), _vec_spec(D_MODEL, LANES)],
        out_specs=[_row_spec(tm, D_MODEL), pl.BlockSpec((D_MODEL, tm), lambda i: (0, i)),
                   _row_spec(tm, D_MODEL), h_spec],
        out_shape=[jax.ShapeDtypeStruct((SEQ, D_MODEL), BF16), jax.ShapeDtypeStruct((D_MODEL, SEQ), BF16),
                   jax.ShapeDtypeStruct((SEQ, D_MODEL), BF16), jax.ShapeDtypeStruct((SEQ, LANES), F32)],
        scratch_shapes=[_natural_scratch(tm)],
        compiler_params=_params("parallel"),
    )(*o_parts, *lse_parts, z, sel)


def _merge_bwd(dy, w_out, o, lse, z, sel, name):
    tm = 256
    n_d = len(DILATIONS)

    def body(dy_ref, w_ref, o_ref, lse_ref, z_ref, sel_ref, dz_ref, *rest):
        do_refs, delta_refs, lse_refs, nat = rest[:n_d], rest[n_d:2 * n_d], rest[2 * n_d:3 * n_d], rest[-1]
        zv = z_ref[...].astype(F32)
        sz = _sigmoid(zv)
        duv = lax.dot_general(dy_ref[...], w_ref[...], NT_DIMS, preferred_element_type=F32)
        ov = o_ref[...].astype(F32)
        do = duv * (zv * sz)
        dz_ref[...] = (duv * ov * (sz * (1.0 + zv * (1.0 - sz)))).astype(BF16)
        delta = _dot_split(do * ov, sel_ref[...], (((1,), (0,)), ((), ())))
        lv = lse_ref[...]
        for i, d in enumerate(DILATIONS):
            _store_classes(do_refs[i], do, nat, d)
            _store_classes(delta_refs[i], delta, nat, d)
            _store_classes(lse_refs[i], lv, nat, d)

    res = pl.pallas_call(
        body, name=name, grid=(SEQ // tm,),
        in_specs=[_row_spec(tm, D_MODEL), _vec_spec(D_MODEL, D_MODEL), _row_spec(tm, D_MODEL), _row_spec(tm, LANES),
                  _row_spec(tm, D_MODEL, B_Z_SEGMENT), _vec_spec(D_MODEL, LANES)],
        out_specs=[_row_spec(tm, D_MODEL)] + [_class_spec(tm, d) for d in DILATIONS]
        + [_class_spec(tm, d, LANES) for d in DILATIONS] * 2,
        out_shape=[jax.ShapeDtypeStruct((SEQ, D_MODEL), BF16)] + [_class_shape(d, BF16) for d in DILATIONS]
        + [_class_shape(d, F32, LANES) for d in DILATIONS] * 2,
        scratch_shapes=[_natural_scratch(tm)],
        compiler_params=_params("parallel"),
    )(dy, w_out, o, lse, z, sel)
    flat = lambda a: a.reshape(SEQ, a.shape[-1])
    return (res[0], [flat(a) for a in res[1:1 + n_d]], [flat(a) for a in res[1 + n_d:1 + 2 * n_d]],
            [flat(a) for a in res[1 + 2 * n_d:]])


def _attn_bwd(q, k, proj, group, do, lse, delta, slopes, dilation, name):
    bpc = SEQ // dilation // ATTN_BLOCK
    heads = ATTN_HEADS_BWD
    n_blocks = SEQ // ATTN_BLOCK
    carry = bpc > 1
    width = heads * HEAD_DIM
    cur, prev = _attn_specs(heads)
    v_cur, v_prev = _attn_specs(heads, segment=3 * group + 2)
    assert heads == N_HEADS
    per_head = pl.BlockSpec((ATTN_BLOCK, LANES), lambda hg, b: (jnp.minimum(b, n_blocks - 1), 0))
    scale = HEAD_DIM ** -0.5

    def body(sl_ref, q_ref, kp_ref, kc_ref, vp_ref, vc_ref, do_ref, lse_ref, dl_ref,
             dq_ref, dk_ref, dv_ref, *scratch):
        b = pl.program_id(1)
        if carry:
            dk_carry, dv_carry = scratch

            @pl.when(b == n_blocks)
            def _():
                dk_ref[...] = dk_carry[...].astype(BF16)
                dv_ref[...] = dv_carry[...].astype(BF16)

            @pl.when(b < n_blocks)
            def _():
                step(sl_ref, q_ref, kp_ref, kc_ref, vp_ref, vc_ref, do_ref, lse_ref, dl_ref,
                     dq_ref, dk_ref, dv_ref, dk_carry, dv_carry, b)
        else:
            step(sl_ref, q_ref, kp_ref, kc_ref, vp_ref, vc_ref, do_ref, lse_ref, dl_ref,
                 dq_ref, dk_ref, dv_ref, None, None, b)

    def step(sl_ref, q_ref, kp_ref, kc_ref, vp_ref, vc_ref, do_ref, lse_ref, dl_ref,
             dq_ref, dk_ref, dv_ref, dk_carry, dv_carry, b):
        if carry:
            @pl.when(b == 0)
            def _():
                dk_carry[...] = jnp.zeros_like(dk_carry)
                dv_carry[...] = jnp.zeros_like(dv_carry)

        q3 = _head_stack(lambda cols: q_ref[:, cols], heads)
        k3 = _head_stack(lambda cols: _key_tile(kp_ref, kc_ref, cols, bpc), heads)
        v3 = _head_stack(lambda cols: _key_tile(vp_ref, vc_ref, cols, bpc), heads)
        do3 = _head_stack(lambda cols: do_ref[:, cols], heads)
        lse_t = lse_ref[...].T
        dl_t = dl_ref[...].T
        lse3 = jnp.stack([lse_t[h:h + 1, :] for h in range(heads)], axis=0)
        dl3 = jnp.stack([dl_t[h:h + 1, :] for h in range(heads)], axis=0)
        s = lax.dot_general(k3, q3, BATCH_NT_DIMS, preferred_element_type=F32)
        dist, valid = _attn_masks(b, bpc, dilation, transposed=True)
        p = jnp.exp(jnp.where(valid[None], s - dist[None] * sl_ref[...], NEG_INF) - lse3)
        dp = lax.dot_general(v3, do3, BATCH_NT_DIMS, preferred_element_type=F32)
        ds = (p * (dp - dl3)).astype(BF16)
        dq3 = lax.dot_general(ds, k3, BATCH_TN_DIMS, preferred_element_type=F32) * scale
        dk3 = lax.dot_general(ds, q3, BATCH_NN_DIMS, preferred_element_type=F32)
        dv3 = lax.dot_general(p.astype(BF16), do3, BATCH_NN_DIMS, preferred_element_type=F32)
        for h in range(heads):
            cols = slice(h * HEAD_DIM, (h + 1) * HEAD_DIM)
            dq_ref[:, cols] = dq3[h].astype(BF16)
            if carry:
                dk_ref[:, cols] = (dk_carry[:, cols] + dk3[h, :ATTN_BLOCK]).astype(BF16)
                dv_ref[:, cols] = (dv_carry[:, cols] + dv3[h, :ATTN_BLOCK]).astype(BF16)
                dk_carry[:, cols] = dk3[h, ATTN_BLOCK:]
                dv_carry[:, cols] = dv3[h, ATTN_BLOCK:]
            else:
                dk_ref[:, cols] = dk3[h].astype(BF16)
                dv_ref[:, cols] = dv3[h].astype(BF16)

    kv_out = prev if carry else cur
    return pl.pallas_call(
        body, name=name, grid=(N_HEADS // heads, n_blocks + (1 if carry else 0)),
        in_specs=[pl.BlockSpec((heads, 1, 1), lambda hg, b: (hg, 0, 0)), cur, prev, cur, v_prev, v_cur,
                  cur, per_head, per_head],
        out_specs=[cur, kv_out, kv_out],
        out_shape=[jax.ShapeDtypeStruct((SEQ, D_MODEL), BF16)] * 3,
        scratch_shapes=[pltpu.VMEM((ATTN_BLOCK, width), F32)] * 2 if carry else [],
        compiler_params=_params("parallel", "arbitrary"),
    )(slopes.reshape(N_HEADS, 1, 1), q, k, k, proj, proj, do, lse, delta)


def _qknorm_bwd(proj, group, qw, kw, seg, dq, dk, dv, name):
    tm = 256

    def body(q_in, k_in, qw_ref, kw_ref, seg_ref, dq_ref, dk_ref, dv_ref, dproj_ref, sums_ref):
        segv = seg_ref[...]
        sums = []
        for part, (raw_ref, w_ref, dn_ref) in enumerate(((q_in, qw_ref, dq_ref), (k_in, kw_ref, dk_ref))):
            raw = raw_ref[...].astype(F32)
            dn = dn_ref[...].astype(F32)
            r = _qk_rstd(raw, segv)
            gq = dn * w_ref[...]
            draw = r * gq - raw * (r * r * r) * (_segsum(raw * gq, segv) * (1.0 / HEAD_DIM))
            dproj_ref[:, part * D_MODEL:(part + 1) * D_MODEL] = draw.astype(BF16)
            sums.append(jnp.sum(dn * raw * r, axis=0, keepdims=True))
        dproj_ref[:, 2 * D_MODEL:] = dv_ref[...]

        @pl.when(pl.program_id(0) == 0)
        def _():
            sums_ref[...] = jnp.zeros_like(sums_ref)

        sums_ref[...] += jnp.concatenate(sums + [jnp.zeros((6, D_MODEL), F32)], axis=0)

    return pl.pallas_call(
        body, name=name, grid=(SEQ // tm,),
        in_specs=[_row_spec(tm, D_MODEL, 3 * group), _row_spec(tm, D_MODEL, 3 * group + 1),
                  _vec_spec(1, D_MODEL), _vec_spec(1, D_MODEL), _vec_spec(256, 256)] + [_row_spec(tm, D_MODEL)] * 3,
        out_specs=[_row_spec(tm, 3 * D_MODEL), _vec_spec(8, D_MODEL)],
        out_shape=[jax.ShapeDtypeStruct((SEQ, 3 * D_MODEL), BF16), jax.ShapeDtypeStruct((8, D_MODEL), F32)],
        compiler_params=_params("arbitrary"),
    )(proj, proj, qw, kw, seg, dq, dk, dv)


B_TN = 512
B_GROUP_TILES = 3 * D_MODEL // B_TN
B_Z_TILE0 = 3 * B_GROUP_TILES
B_Z_TILES = D_MODEL // B_TN
B_TILES = B_Z_TILE0 + B_Z_TILES
B_Z_SEGMENT = 3 * len(DILATIONS)


def _local_step(x, target, mods, norm_g, conv_w, conv_b, ln_g, ln_b, q_norm, k_norm, chip, own_wa_in, own_wb_in,
                weights_a, weights_b, forward_weights_b, send_grads_b, forward_grads_b, send_grads_a):
    row = lambda a, i: a[i:i + 1]
    shift0, scale0, gate0 = row(mods[0], 0), row(mods[0], 1), row(mods[0], 2)
    shift1, scale1, gate1 = row(mods[1], 0), row(mods[1], 1), row(mods[1], 2)
    g0, g1 = row(norm_g, 0), row(norm_g, 1)
    seg = _seg_matrix()
    slopes = jnp.exp2(-8.0 * jnp.arange(1, N_HEADS + 1, dtype=F32) / N_HEADS)
    qw = [jnp.tile(q_norm[g:g + 1], (1, N_HEADS)) for g in range(3)]
    kw = [jnp.tile(k_norm[g:g + 1], (1, N_HEADS)) for g in range(3)]

    h0, h0t = _normmod_fwd(x, g0, scale0, shift0, "prenorm0")
    nsa = own_wa_in.shape[2]
    tiles_a = dict(tn=nsa, total_tiles=N_CHIPS, part_of=lambda tile: 0)
    own_ids, rest_ids, own_tiles = _own_first(chip, N_CHIPS)
    proj_a = _in_tiles([h0], own_wa_in, own_ids, own_tiles, name="a_in_own", **tiles_a)
    wa_in, wa_out = weights_a(proj_a)
    ja = wa_in.shape[0]
    proj_a = _in_tiles([h0], wa_in, rest_ids, N_CHIPS - own_tiles, name="a_in_rest", prev=proj_a, **tiles_a)
    u5, u5t, u2 = _conv_fwd(proj_a, conv_w, conv_b, ln_g, ln_b, "a_conv")
    x1, y_a, h1t, h1c = _out_a(u5, wa_out, x, gate0, g1, scale1, shift1, "a_out")

    tiles_b = dict(tn=B_TN, total_tiles=B_TILES,
                   part_of=lambda tile: jnp.where(tile >= B_Z_TILE0, 0, tile // B_GROUP_TILES))
    own_ids, rest_ids, own_tiles = _own_first(chip, B_TILES)
    proj_b = _in_tiles(h1c, own_wb_in, own_ids, own_tiles, name="b_in_own", **tiles_b)
    forward_weights_b(proj_b)
    wb_in, wb_out = weights_b(proj_b)
    jb, _, nsb = wb_in.shape
    proj_b = _in_tiles(h1c, wb_in, rest_ids, B_TILES - own_tiles, name="b_in_rest", prev=proj_b, **tiles_b)
    h1 = h1c[0]
    qkv, o_parts, lse_parts = [], [], []
    for g, d in enumerate(DILATIONS):
        qn, kn = _qknorm_fwd(proj_b, g, qw[g], kw[g], seg, f"b_qknorm_g{g}")
        og, lg = _attn_fwd(qn, kn, proj_b, g, slopes, d, f"b_attn_g{g}")
        qkv.append((qn, kn))
        o_parts.append(og if d == 1 else og.reshape(d, SEQ // d, D_MODEL))
        lse_parts.append(lg if d == 1 else lg.reshape(d, SEQ // d, LANES))
    sel = _head_selector()
    u_b, u_bt, o_b, lse_b = _merge_fwd(o_parts, lse_parts, proj_b, sel, "b_merge")
    e, dy_b, sums_loss = _out_b_loss(u_b, wb_out, x1, gate1, target, "b_out_loss")

    dwb_out = _mm(u_bt, dy_b, tn=D_MODEL, tile0=0, n_tiles=1, out_dtype=BF16, name="b_dwout")
    dz_b, do_c, delta_c, lse_c = _merge_bwd(dy_b, wb_out, o_b, lse_b, proj_b, sel, "b_merge_bwd")
    dwb_in = _mm(h1t, dz_b, tn=B_TN, tile0=B_Z_TILE0, n_tiles=B_Z_TILES, out_dtype=BF16, name="b_dwin_z",
                 out3d=(jb, nsb))
    dh1_parts = [_mm_nt(dz_b, wb_in, tn=B_TN, tile0=B_Z_TILE0, n_tiles=B_Z_TILES, name="b_dh_z")]
    qk_sums = []
    for g, d in enumerate(DILATIONS):
        qn, kn = qkv[g]
        dq, dk, dv = _attn_bwd(qn, kn, proj_b, g, do_c[g], lse_c[g], delta_c[g], slopes, d, f"b_attn_bwd_g{g}")
        dproj, sums_qk = _qknorm_bwd(proj_b, g, qw[g], kw[g], seg, dq, dk, dv, f"b_qknorm_bwd_g{g}")
        qk_sums.append(sums_qk)
        dwb_in = _mm(h1t if d == 1 else h1c[g], dproj, tn=B_TN, tile0=g * B_GROUP_TILES, n_tiles=B_GROUP_TILES,
                     out_dtype=BF16, name=f"b_dwin_g{g}", out3d=(jb, nsb), prev=dwb_in, transpose_lhs=d != 1)
        dh = _mm_nt(dproj, wb_in, tn=B_TN, tile0=g * B_GROUP_TILES, n_tiles=B_GROUP_TILES, name=f"b_dh_g{g}")
        dh1_parts.append(dh)
    token = send_grads_b(dwb_in, dwb_out)
    dx1, sums_n1, dy_a = _normmod_bwd(x1, g1, scale1 + token[0:1, 0:1], dh1_parts, e, "prenorm1_bwd",
                                      part_dilations=(1,) + DILATIONS, gated=(gate0, y_a))
    token = forward_grads_b(dx1)

    dwa_out = _mm(u5t, dy_a, tn=D_MODEL, tile0=0, n_tiles=1, out_dtype=BF16, name="a_dwout")
    du2, dz_a, sums_ln = _conv_bwd_pointwise(dy_a, wa_out, proj_a, u2, ln_g + token[0:1, 0:1], ln_b,
                                             "a_conv_bwd_pw")
    dproj_a, dconv_w = _conv_bwd_taps(du2, dz_a, proj_a, conv_w, "a_conv_bwd_taps")
    dwa_in = _mm(h0t, dproj_a, tn=nsa, tile0=0, n_tiles=ja, out_dtype=BF16, name="a_dwin", out3d=(ja, nsa))
    token = send_grads_a(dwa_in, dwa_out)
    dh0 = _mm_nt(dproj_a, wa_in, tn=nsa, tile0=0, n_tiles=ja, name="a_dh", after=token)
    grad_x, sums_n0 = _normmod_bwd(x, g0, scale0, [dh0], dx1, "prenorm0_bwd")

    small = dict(
        dnorm_g=jnp.concatenate([sums_n0[0:1], sums_n1[0:1]], axis=0),
        dmod0=jnp.concatenate([sums_n0[2:3], sums_n0[1:2], sums_n1[3:4]], axis=0),
        dmod1=jnp.concatenate([sums_n1[2:3], sums_n1[1:2], sums_loss[0:1]], axis=0),
        dln_g=sums_ln[0:1], dln_b=sums_ln[1:2], dconv_b=sums_ln[2:3],
        dconv_w=dconv_w[:CONV_WIDTH],
        dq_norm=jnp.concatenate([s[0:1] for s in qk_sums], axis=0),
        dk_norm=jnp.concatenate([s[1:2] for s in qk_sums], axis=0),
        loss_cols=sums_loss[1:2],
    )
    return grad_x, small


def _adamw(w, g, m, v, name, after=None, copy_grad=False):
    rows, cols = w.shape
    tr = rows if rows <= 128 else 128
    c1 = 1.0 / (1.0 - ADAM_B1 ** ADAM_STEP)
    c2 = 1.0 / (1.0 - ADAM_B2 ** ADAM_STEP)
    extra = [] if after is None else [after]
    n_out = 4 if copy_grad else 3

    def body(w_ref, g_ref, m_ref, v_ref, *rest):
        d_ref, mo_ref, vo_ref = rest[len(extra):len(extra) + 3]
        gv = g_ref[...]
        if copy_grad:
            rest[-1][...] = gv
        mn = ADAM_B1 * m_ref[...] + (1.0 - ADAM_B1) * gv
        vn = ADAM_B2 * v_ref[...] + (1.0 - ADAM_B2) * (gv * gv)
        mo_ref[...] = mn
        vo_ref[...] = vn
        d_ref[...] = -ADAM_LR * ((mn * c1) / (jnp.sqrt(vn * c2) + ADAM_EPS) + ADAM_WD * w_ref[...])

    spec = pl.BlockSpec((tr, cols), lambda i: (i, 0))
    return pl.pallas_call(
        body, name=name, grid=(rows // tr,),
        in_specs=[spec] * 4 + [pl.BlockSpec(memory_space=pl.ANY)] * len(extra), out_specs=[spec] * n_out,
        out_shape=[jax.ShapeDtypeStruct((rows, cols), F32)] * n_out,
        compiler_params=_params("parallel"),
    )(w, g, m, v, *extra)


def _cast_into_slot(w, chip_idx, name, keep_own=False, after=None):
    rows, cols = w.shape
    tr = 256
    extra = [] if after is None else [after]

    def body(ch_ref, w_ref, *rest):
        wb = w_ref[...].astype(BF16)
        for o_ref in rest[len(extra):]:
            o_ref[...] = wb

    slot_spec = pl.BlockSpec((None, tr, cols), lambda i, ch: (ch[0], i, 0))
    own_spec = pl.BlockSpec((None, tr, cols), lambda i, ch: (0, i, 0))
    res = pl.pallas_call(
        body, name=name,
        grid_spec=pltpu.PrefetchScalarGridSpec(
            num_scalar_prefetch=1, grid=(rows // tr,),
            in_specs=[pl.BlockSpec((tr, cols), lambda i, ch: (i, 0))] + [pl.BlockSpec(memory_space=pl.ANY)] * len(extra),
            out_specs=[slot_spec, own_spec] if keep_own else [slot_spec]),
        out_shape=[jax.ShapeDtypeStruct((N_CHIPS, rows, cols), BF16)]
        + ([jax.ShapeDtypeStruct((1, rows, cols), BF16)] if keep_own else []),
        compiler_params=_params("parallel"),
    )(chip_idx, w, *extra)
    return tuple(res) if keep_own else res[0]


def _position():
    x, y, c = lax.axis_index("x"), lax.axis_index("y"), lax.axis_index("c")
    return x, y, c


def _xor_peer(x, y, c, k):
    return (x ^ ((k >> 2) & 1), y ^ ((k >> 1) & 1), c ^ (k & 1))


def _chip_peer(x, y, k):
    return (x ^ ((k >> 1) & 1), y ^ (k & 1))


def _ada_forward(c_row, ada_w, ada_b, conv_w, after=()):
    ns = ada_w.shape[2]
    cw = conv_w.shape[1]

    def body(c_ref, w_ref, b_ref, cv_ref, *rest):
        (mod_ref, sc_ref, cvo_ref, c_all, mp, parts, cv_parts,
         send1, recv1, send2, recv2, send3, recv3) = rest[len(after):]
        x, y, c = _position()
        me = 4 * x + 2 * y + c
        chip = 2 * x + y

        def c_copy(k):
            return pltpu.make_async_remote_copy(
                src_ref=c_all.at[me], dst_ref=c_all.at[me], send_sem=send1.at[k - 1], recv_sem=recv1.at[k - 1],
                device_id=_xor_peer(x, y, c, k), device_id_type=MESH)

        def cv_copy(k):
            px, py = _chip_peer(x, y, k)
            return pltpu.make_async_remote_copy(
                src_ref=cv_parts.at[chip], dst_ref=cv_parts.at[chip], send_sem=send3.at[k - 1],
                recv_sem=recv3.at[k - 1], device_id=(px, py, c), device_id_type=MESH)

        c_all[me] = c_ref[...]
        cv_parts[chip] = cv_ref[...]
        for k in range(1, N_DEV):
            c_copy(k).start()
        for k in range(1, N_CHIPS):
            cv_copy(k).start()
        for k in range(1, N_DEV):
            c_copy(k).wait_recv()
        cv = jnp.concatenate([c_all[i] for i in range(N_DEV)], axis=0)
        sc = cv * _sigmoid(cv)
        sc_ref[...] = sc
        for l in range(2):
            res = jnp.dot(sc, w_ref[l], preferred_element_type=F32, precision=lax.Precision.HIGHEST)
            for i in range(N_DEV):
                mp[i, l:l + 1, :] = res[i:i + 1, :]

        def mod_copy(k):
            px, py = _chip_peer(x, y, k)
            return pltpu.make_async_remote_copy(
                src_ref=mp.at[4 * px + 2 * py + c], dst_ref=parts.at[chip], send_sem=send2.at[k - 1],
                recv_sem=recv2.at[k - 1], device_id=(px, py, c), device_id_type=MESH)

        for k in range(1, N_CHIPS):
            mod_copy(k).start()
        parts[chip] = mp[me]
        for k in range(1, N_CHIPS):
            mod_copy(k).wait_recv()
            cv_copy(k).wait_recv()
        mod_ref[...] = jnp.concatenate([parts[j] for j in range(N_CHIPS)], axis=1) + b_ref[...]
        cvo_ref[...] = jnp.concatenate([cv_parts[j] for j in range(N_CHIPS)], axis=1)
        for k in range(1, N_DEV):
            c_copy(k).wait_send()
        for k in range(1, N_CHIPS):
            mod_copy(k).wait_send()
            cv_copy(k).wait_send()

    vm = pl.BlockSpec(memory_space=pltpu.VMEM)
    return pl.pallas_call(
        body, name="ada_forward",
        in_specs=[vm] * 4 + [pl.BlockSpec(memory_space=pl.ANY)] * len(after), out_specs=[vm] * 3,
        out_shape=[jax.ShapeDtypeStruct((2, 3 * D_MODEL), F32), jax.ShapeDtypeStruct((N_DEV, D_MODEL), F32),
                   jax.ShapeDtypeStruct((CONV_WIDTH, N_CHIPS * cw), F32)],
        scratch_shapes=[pltpu.VMEM((N_DEV, 1, D_MODEL), F32), pltpu.VMEM((N_DEV, 2, ns), F32),
                        pltpu.VMEM((N_CHIPS, 2, ns), F32), pltpu.VMEM((N_CHIPS, CONV_WIDTH, cw), F32),
                        pltpu.SemaphoreType.DMA((N_DEV - 1,)), pltpu.SemaphoreType.DMA((N_DEV - 1,)),
                        pltpu.SemaphoreType.DMA((N_CHIPS - 1,)), pltpu.SemaphoreType.DMA((N_CHIPS - 1,)),
                        pltpu.SemaphoreType.DMA((N_CHIPS - 1,)), pltpu.SemaphoreType.DMA((N_CHIPS - 1,))],
        compiler_params=pltpu.CompilerParams(vmem_limit_bytes=VMEM_LIMIT_BYTES),
    )(c_row, ada_w, ada_b, conv_w, *after)


HBM_SPEC = pl.BlockSpec(memory_space=pltpu.HBM)
ANY_SPEC = pl.BlockSpec(memory_space=pl.ANY)
SEM_SPEC = pl.BlockSpec(memory_space=pltpu.SEMAPHORE)
SPLIT_PARAMS = dict(compiler_params=pltpu.CompilerParams(has_side_effects=pltpu.SideEffectType.DATAFLOW_SIDE_EFFECTING))
TOKEN = jax.ShapeDtypeStruct((8, 128), F32)


def _hbm(arrays):
    return [pltpu.with_memory_space_constraint(a, pltpu.HBM) for a in arrays]


def _hbm_like(arrays):
    return [pltpu.HBM(a.shape, a.dtype) for a in arrays]


def _gather_start(lands, after, name):
    n = len(lands)

    def body(*refs):
        ins = refs[:n]
        send, recv = refs[n + 1], refs[n + 2]
        x, y, c = _position()
        chip = 2 * x + y
        for t in range(n):
            rh = ins[t].shape[1] // 2
            for k in range(1, N_CHIPS):
                px, py = _chip_peer(x, y, k)
                block = ins[t].at[chip, pl.ds(c * rh, rh)]
                pltpu.make_async_remote_copy(
                    src_ref=block, dst_ref=block, send_sem=send.at[3 * t + k - 1], recv_sem=recv.at[3 * t + k - 1],
                    device_id=(px, py, c), device_id_type=MESH).start()
        refs[-1][...] = jnp.zeros(TOKEN.shape, F32)

    res = pl.pallas_call(
        body, name=name, in_specs=[HBM_SPEC] * n + [ANY_SPEC],
        out_specs=(SEM_SPEC, SEM_SPEC, *[HBM_SPEC] * n, pl.BlockSpec(memory_space=pltpu.VMEM)),
        out_shape=(pltpu.SemaphoreType.DMA((3 * n,)), pltpu.SemaphoreType.DMA((3 * n,)), *_hbm_like(lands), TOKEN),
        input_output_aliases={t: 2 + t for t in range(n)}, **SPLIT_PARAMS,
    )(*_hbm(lands), after)
    return res[0], res[1], list(res[2:2 + n]), res[-1]


def _gather_forward(send, recv, lands, after, name):
    n = len(lands)

    def body(*refs):
        ins = refs[:n]
        send1, recv1 = refs[n], refs[n + 1]
        send2, recv2 = refs[n + 3], refs[n + 4]
        x, y, c = _position()
        chip = 2 * x + y
        for t in range(n):
            rh = ins[t].shape[1] // 2
            half = pl.ds(c * rh, rh)
            for k in range(1, N_CHIPS):
                px, py = _chip_peer(x, y, k)
                s = 3 * t + k - 1
                got = ins[t].at[2 * px + py, half]
                cp = pltpu.make_async_remote_copy(
                    src_ref=ins[t].at[chip, half], dst_ref=got, send_sem=send1.at[s], recv_sem=recv1.at[s],
                    device_id=(px, py, c), device_id_type=MESH)
                cp.wait_send()
                cp.wait_recv()
                pltpu.make_async_remote_copy(
                    src_ref=got, dst_ref=got, send_sem=send2.at[s], recv_sem=recv2.at[s],
                    device_id=(x, y, 1 - c), device_id_type=MESH).start()
        refs[-1][...] = jnp.zeros(TOKEN.shape, F32)

    res = pl.pallas_call(
        body, name=name, in_specs=[HBM_SPEC] * n + [SEM_SPEC, SEM_SPEC, ANY_SPEC],
        out_specs=(SEM_SPEC, SEM_SPEC, *[HBM_SPEC] * n, pl.BlockSpec(memory_space=pltpu.VMEM)),
        out_shape=(pltpu.SemaphoreType.DMA((3 * n,)), pltpu.SemaphoreType.DMA((3 * n,)), *_hbm_like(lands), TOKEN),
        input_output_aliases={t: 2 + t for t in range(n)}, **SPLIT_PARAMS,
    )(*lands, send, recv, after)
    return res[0], res[1], list(res[2:2 + n]), res[-1]


def _gather_wait(send, recv, lands, after, name):
    n = len(lands)

    def body(*refs):
        ins = refs[:n]
        send_ref, recv_ref = refs[n], refs[n + 1]
        x, y, c = _position()
        for t in range(n):
            rh = ins[t].shape[1] // 2
            for k in range(1, N_CHIPS):
                px, py = _chip_peer(x, y, k)
                cp = pltpu.make_async_remote_copy(
                    src_ref=ins[t].at[2 * px + py, pl.ds(c * rh, rh)],
                    dst_ref=ins[t].at[2 * px + py, pl.ds((1 - c) * rh, rh)], send_sem=send_ref.at[3 * t + k - 1],
                    recv_sem=recv_ref.at[3 * t + k - 1], device_id=(x, y, 1 - c), device_id_type=MESH)
                cp.wait_send()
                cp.wait_recv()

    res = pl.pallas_call(
        body, name=name, in_specs=[HBM_SPEC] * n + [SEM_SPEC, SEM_SPEC, ANY_SPEC], out_specs=[HBM_SPEC] * n,
        out_shape=_hbm_like(lands), input_output_aliases={t: t for t in range(n)}, **SPLIT_PARAMS,
    )(*lands, send, recv, after)
    return list(res)


def _split_start(name, arrays, n_sems, after, issue):
    m = len(arrays)

    def body(*refs):
        issue(refs[:m], refs[m + 1], refs[m + 2])
        refs[-1][...] = jnp.zeros(TOKEN.shape, F32)

    res = pl.pallas_call(
        body, name=name, in_specs=[HBM_SPEC] * m + [ANY_SPEC],
        out_specs=(SEM_SPEC, SEM_SPEC, *[HBM_SPEC] * m, pl.BlockSpec(memory_space=pltpu.VMEM)),
        out_shape=(pltpu.SemaphoreType.DMA((n_sems,)), pltpu.SemaphoreType.DMA((n_sems,)), *_hbm_like(arrays), TOKEN),
        input_output_aliases={t: 2 + t for t in range(m)}, **SPLIT_PARAMS,
    )(*_hbm(arrays), after)
    return res[0], res[1], list(res[2:2 + m]), res[-1]


def _split_wait(name, arrays, send, recv, after, await_all):
    m = len(arrays)

    def body(*refs):
        await_all(refs[:m], refs[m], refs[m + 1])

    res = pl.pallas_call(
        body, name=name, in_specs=[HBM_SPEC] * m + [SEM_SPEC, SEM_SPEC, ANY_SPEC], out_specs=[HBM_SPEC] * m,
        out_shape=_hbm_like(arrays), input_output_aliases={t: t for t in range(m)}, **SPLIT_PARAMS,
    )(*arrays, send, recv, after)
    return list(res)


def _sibling_copies(refs, send, recv, n):
    x, y, c = _position()
    cps = []
    for t in range(n):
        rh = refs[t].shape[1] // 2
        cps.append(pltpu.make_async_remote_copy(
            src_ref=refs[t].at[pl.ds(0, N_CHIPS), pl.ds((1 - c) * rh, rh)], dst_ref=refs[n + t],
            send_sem=send.at[t], recv_sem=recv.at[t], device_id=(x, y, 1 - c), device_id_type=MESH))
    return cps


def _reduce_sibling_start(grads, after, name):
    n = len(grads)
    lands = [lax.empty((N_CHIPS, g.shape[1] // 2, g.shape[2]), BF16) for g in grads]

    def issue(refs, send, recv):
        for cp in _sibling_copies(refs, send, recv, n):
            cp.start()

    return _split_start(name, list(grads) + lands, n, after, issue)


def _reduce_sibling_wait(send, recv, arrays, after, name):
    n = len(arrays) // 2

    def await_all(refs, send_ref, recv_ref):
        for cp in _sibling_copies(refs, send_ref, recv_ref, n):
            cp.wait_send()
            cp.wait_recv()

    res = _split_wait(name, arrays, send, recv, after, await_all)
    return res[:n], res[n:]


def _add_sibling_half(grad, got, dev_idx, name):
    j, r, cols = grad.shape
    rh = r // 2
    tr = rh
    nb = rh // tr

    def body(idx_ref, g_ref, got_ref, out_ref):
        out_ref[...] = (g_ref[...].astype(F32) + got_ref[...].astype(F32)).astype(BF16)

    return pl.pallas_call(
        body, name=name,
        grid_spec=pltpu.PrefetchScalarGridSpec(
            num_scalar_prefetch=1, grid=(j, nb),
            in_specs=[pl.BlockSpec((None, tr, cols), lambda jj, i, idx: (jj, idx[2] * nb + i, 0)),
                      pl.BlockSpec((None, tr, cols), lambda jj, i, idx: (jj, i, 0))],
            out_specs=pl.BlockSpec((None, tr, cols), lambda jj, i, idx: (jj, i, 0))),
        out_shape=jax.ShapeDtypeStruct((j, rh, cols), BF16),
        compiler_params=_params("parallel", "parallel"),
    )(dev_idx, grad, got)


def _chip_copies(refs, send, recv, n, receiving):
    x, y, c = _position()
    chip = 2 * x + y
    cps = []
    for t in range(n):
        for k in range(1, N_CHIPS):
            px, py = _chip_peer(x, y, k)
            cps.append(pltpu.make_async_remote_copy(
                src_ref=refs[t].at[2 * px + py], dst_ref=refs[n + t].at[2 * px + py if receiving else chip],
                send_sem=send.at[3 * t + k - 1], recv_sem=recv.at[3 * t + k - 1],
                device_id=(px, py, c), device_id_type=MESH))
    return cps


def _reduce_chips_start(partials, after, name):
    n = len(partials)
    lands = [lax.empty(p.shape, BF16) for p in partials]

    def issue(refs, send, recv):
        for cp in _chip_copies(refs, send, recv, n, False):
            cp.start()

    return _split_start(name, list(partials) + lands, 3 * n, after, issue)


def _reduce_chips_wait(send, recv, arrays, after, name):
    n = len(arrays) // 2

    def await_all(refs, send_ref, recv_ref):
        for cp in _chip_copies(refs, send_ref, recv_ref, n, True):
            cp.wait_send()
            cp.wait_recv()

    res = _split_wait(name, arrays, send, recv, after, await_all)
    return res[:n], res[n:]


def _sum_partials(land, partial, dev_idx, name):
    _, rh, cols = land.shape
    tr = min(rh, 256)
    nb = rh // tr

    def body(idx_ref, l_ref, p_ref, o_ref):
        chip = idx_ref[1]
        acc = jnp.where(chip == 0, p_ref[...], l_ref[0]).astype(F32)
        for s in range(1, N_CHIPS):
            acc = acc + jnp.where(chip == s, p_ref[...], l_ref[s]).astype(F32)
        o_ref[...] = acc

    return pl.pallas_call(
        body, name=name,
        grid_spec=pltpu.PrefetchScalarGridSpec(
            num_scalar_prefetch=1, grid=(nb,),
            in_specs=[pl.BlockSpec((N_CHIPS, tr, cols), lambda i, idx: (0, i, 0)),
                      pl.BlockSpec((None, tr, cols), lambda i, idx: (idx[1], i, 0))],
            out_specs=pl.BlockSpec((tr, cols), lambda i, idx: (idx[2] * nb + i, 0))),
        out_shape=jax.ShapeDtypeStruct((2 * rh, cols), F32), compiler_params=_params("parallel"),
    )(dev_idx, land, partial)


def _half_copies(refs, send, recv, receiving):
    x, y, c = _position()
    cps = []
    for t, ref in enumerate(refs):
        rh = ref.shape[0] // 2
        cps.append(pltpu.make_async_remote_copy(
            src_ref=ref.at[pl.ds(c * rh, rh)], dst_ref=ref.at[pl.ds(((1 - c) if receiving else c) * rh, rh)],
            send_sem=send.at[t], recv_sem=recv.at[t], device_id=(x, y, 1 - c), device_id_type=MESH))
    return cps


def _share_halves_start(totals, after, name):
    def issue(refs, send, recv):
        for cp in _half_copies(refs, send, recv, False):
            cp.start()

    return _split_start(name, list(totals), len(totals), after, issue)


def _share_halves_wait(send, recv, totals, after, name):
    def await_all(refs, send_ref, recv_ref):
        for cp in _half_copies(refs, send_ref, recv_ref, True):
            cp.wait_send()
            cp.wait_recv()

    return _split_wait(name, totals, send, recv, after, await_all)


SMALL_ROWS = 56


def _small_copies(refs, send, recv, receiving):
    x, y, c = _position()
    me = 4 * x + 2 * y + c
    cps = []
    for k in range(1, N_DEV):
        px, py, pc = _xor_peer(x, y, c, k)
        cps.append(pltpu.make_async_remote_copy(
            src_ref=refs[0], dst_ref=refs[1].at[4 * px + 2 * py + pc if receiving else me],
            send_sem=send.at[k - 1], recv_sem=recv.at[k - 1], device_id=(px, py, pc), device_id_type=MESH))
    return cps


def _small_gather_start(packed, after):
    land = lax.empty((N_DEV,) + packed.shape, F32)

    def issue(refs, send, recv):
        for cp in _small_copies(refs, send, recv, False):
            cp.start()

    return _split_start("small_gather_start", [packed, land], N_DEV - 1, after, issue)


def _small_gather_wait(send, recv, arrays, after):
    def await_all(refs, send_ref, recv_ref):
        for cp in _small_copies(refs, send_ref, recv_ref, True):
            cp.wait_send()
            cp.wait_recv()

    return _split_wait("small_gather_wait", arrays, send, recv, after, await_all)


def _reduce_small(packed, land, silu_c):
    ns = 3 * D_MODEL // N_CHIPS

    def body(p_ref, land_ref, sc_ref, tot_ref, gw_ref, loss_ref, qk_ref, allp):
        x, y, c = _position()
        me = 4 * x + 2 * y + c
        chip = 2 * x + y
        for i in range(N_DEV):
            allp[i] = jnp.where(me == i, p_ref[...], land_ref[i])
        tot = allp[0]
        for i in range(1, N_DEV):
            tot = tot + allp[i]
        tot_ref[...] = tot
        loss_ref[...] = jnp.sum(tot[11:12, :], axis=1, keepdims=True) * (0.5 / D_MODEL)
        fold = tot[5:11, 0:HEAD_DIM]
        for h in range(1, N_HEADS):
            fold = fold + tot[5:11, h * HEAD_DIM:(h + 1) * HEAD_DIM]
        qk_ref[...] = jnp.concatenate([fold, jnp.zeros((2, HEAD_DIM), F32)], axis=0)
        sct = sc_ref[...].T
        rc = 64
        for l in range(2):
            dms = [allp[i, pl.ds(12 + 4 * l + chip, 1), :][:, :ns] for i in range(N_DEV)]
            for r0 in range(0, D_MODEL, rc):
                acc = sct[r0:r0 + rc, 0:1] * dms[0]
                for i in range(1, N_DEV):
                    acc = acc + sct[r0:r0 + rc, i:i + 1] * dms[i]
                gw_ref[l, r0:r0 + rc, :] = acc

    vm = pl.BlockSpec(memory_space=pltpu.VMEM)
    return pl.pallas_call(
        body, name="reduce_small", in_specs=[vm, vm, vm], out_specs=[vm] * 4,
        out_shape=[jax.ShapeDtypeStruct((SMALL_ROWS, D_MODEL), F32), jax.ShapeDtypeStruct((2, D_MODEL, ns), F32),
                   jax.ShapeDtypeStruct((1, 1), F32), jax.ShapeDtypeStruct((8, HEAD_DIM), F32)],
        scratch_shapes=[pltpu.VMEM((N_DEV, SMALL_ROWS, D_MODEL), F32)],
        compiler_params=pltpu.CompilerParams(vmem_limit_bytes=VMEM_LIMIT_BYTES),
    )(packed, land, silu_c)


def kernel(x, c, norm_g, ada_w, ada_b, a_w_in, a_conv_w, a_conv_b, a_ln_g, a_ln_b, a_w_out, b_w_in, b_q_norm, b_k_norm, b_w_out, loss_target, m_norm_g, m_ada_w, m_ada_b, m_a_w_in, m_a_conv_w, m_a_conv_b, m_a_ln_g, m_a_ln_b, m_a_w_out, m_b_w_in, m_b_q_norm, m_b_k_norm, m_b_w_out, v_norm_g, v_ada_w, v_ada_b, v_a_w_in, v_a_conv_w, v_a_conv_b, v_a_ln_g, v_a_ln_b, v_a_w_out, v_b_w_in, v_b_q_norm, v_b_k_norm, v_b_w_out):
    chip = 2 * lax.axis_index("x") + lax.axis_index("y")
    core = lax.axis_index("c")
    chip_idx = chip.astype(jnp.int32).reshape(1)
    dev_idx = jnp.stack([2 * chip + core, chip, core]).astype(jnp.int32)

    land_a_in, own_wa_in = _cast_into_slot(a_w_in[0], chip_idx, "cast_a_w_in", keep_own=True)
    lands_a = [land_a_in, _cast_into_slot(a_w_out[0], chip_idx, "cast_a_w_out")]
    mods, silu_c, conv_w_full = _ada_forward(c, ada_w, ada_b, a_conv_w[0], after=tuple(lands_a))
    send_a, recv_a, lands_a, token_a = _gather_start(lands_a, mods, "gather_start_a")
    land_b_in, own_wb_in = _cast_into_slot(b_w_in[0], chip_idx, "cast_b_w_in", keep_own=True, after=token_a)
    lands_b = [land_b_in, _cast_into_slot(b_w_out[0], chip_idx, "cast_b_w_out", after=token_a)]
    send_b, recv_b, lands_b, token_b = _gather_start(lands_b, token_a, "gather_start_b")
    mods = mods + token_b[0:2, 0:1]

    def weights_a(after):
        send, recv, lands, _ = _gather_forward(send_a, recv_a, lands_a, after, "gather_forward_a")
        w_in, w_out = _gather_wait(send, recv, lands, after, "gather_wait_a")
        return w_in, w_out.reshape(D_MODEL, D_MODEL)

    forwarded_b = []

    def weights_b(after):
        send, recv, lands, _ = forwarded_b
        w_in, w_out = _gather_wait(send, recv, lands, after, "gather_wait_b")
        return w_in, w_out.reshape(D_MODEL, D_MODEL)

    def forward_weights_b(after):
        forwarded_b.extend(_gather_forward(send_b, recv_b, lands_b, after, "gather_forward_b"))
        return forwarded_b[3]

    stage1, stage2 = {}, {}

    def send_grads(tag, dw_in, dw_out):
        grads = [dw_in, dw_out.reshape(N_CHIPS, D_MODEL // N_CHIPS, D_MODEL)]
        send, recv, arrays, token = _reduce_sibling_start(grads, dw_out, f"reduce_d2d_start_{tag}")
        stage1[tag] = (send, recv, arrays)
        return token

    def forward_grads(tag, after):
        send, recv, arrays = stage1[tag]
        grads, got = _reduce_sibling_wait(send, recv, arrays, after, f"reduce_d2d_wait_{tag}")
        partials = [_add_sibling_half(grads[i], got[i], dev_idx, f"reduce_add_{tag}_{i}") for i in range(2)]
        send, recv, arrays, token = _reduce_chips_start(partials, partials[1], f"reduce_ici_start_{tag}")
        stage2[tag] = (send, recv, arrays)
        return token

    stage3 = {}

    def sum_grads(tag, after):
        send, recv, arrays = stage2[tag]
        partials, lands = _reduce_chips_wait(send, recv, arrays, after, f"reduce_ici_wait_{tag}")
        totals = [_sum_partials(lands[i], partials[i], dev_idx, f"reduce_sum_{tag}_{i}") for i in range(2)]
        send, recv, totals, token = _share_halves_start(totals, totals[1], f"reduce_share_start_{tag}")
        stage3[tag] = (send, recv, totals)
        return token

    def finish_grads(tag, after):
        send, recv, totals = stage3[tag]
        return _share_halves_wait(send, recv, totals, after, f"reduce_share_wait_{tag}")

    grad_x, small = _local_step(
        x[0], loss_target[0], mods.reshape(2, 3, D_MODEL), norm_g, conv_w_full, a_conv_b, a_ln_g[0:1],
        a_ln_b[0:1], b_q_norm[0], b_k_norm[0], chip.astype(jnp.int32), own_wa_in, own_wb_in,
        weights_a, weights_b, forward_weights_b,
        functools.partial(send_grads, "b"), functools.partial(forward_grads, "b"), functools.partial(send_grads, "a"))

    ns = 3 * D_MODEL // N_CHIPS
    pad_mod = lambda dm: jnp.pad(dm.reshape(N_CHIPS, ns), ((0, 0), (0, D_MODEL - ns)))
    packed = jnp.concatenate([
        small["dnorm_g"], small["dconv_b"], small["dln_g"], small["dln_b"], small["dq_norm"], small["dk_norm"],
        small["loss_cols"], pad_mod(small["dmod0"]), pad_mod(small["dmod1"]), small["dconv_w"],
        jnp.zeros((SMALL_ROWS - 20 - CONV_WIDTH, D_MODEL), F32)], axis=0)
    send_s, recv_s, small_arrays, token_s = _small_gather_start(packed, packed)

    given = dict(norm_g=(norm_g, m_norm_g, v_norm_g), ada_w=(ada_w, m_ada_w, v_ada_w), ada_b=(ada_b, m_ada_b, v_ada_b),
                 a_w_in=(a_w_in, m_a_w_in, v_a_w_in), a_conv_w=(a_conv_w, m_a_conv_w, v_a_conv_w),
                 a_conv_b=(a_conv_b, m_a_conv_b, v_a_conv_b), a_ln_g=(a_ln_g, m_a_ln_g, v_a_ln_g),
                 a_ln_b=(a_ln_b, m_a_ln_b, v_a_ln_b), a_w_out=(a_w_out, m_a_w_out, v_a_w_out),
                 b_w_in=(b_w_in, m_b_w_in, v_b_w_in), b_q_norm=(b_q_norm, m_b_q_norm, v_b_q_norm),
                 b_k_norm=(b_k_norm, m_b_k_norm, v_b_k_norm), b_w_out=(b_w_out, m_b_w_out, v_b_w_out))
    order = ["norm_g", "ada_w", "ada_b", "a_w_in", "a_conv_w", "a_conv_b", "a_ln_g", "a_ln_b", "a_w_out", "b_w_in",
             "b_q_norm", "b_k_norm", "b_w_out"]
    outs = {}

    def update(k, g2, after=None, copy_grad=False):
        w, m, v = given[k]
        shape2 = g2.shape
        res = _adamw(w.reshape(shape2), g2, m.reshape(shape2), v.reshape(shape2), f"adamw_{k}", after, copy_grad)
        outs[k] = tuple(a.reshape(w.shape) for a in ((res[3] if copy_grad else g2), res[0], res[1], res[2]))

    token = forward_grads("a", token_s)
    token = sum_grads("b", token)
    packed, land = _small_gather_wait(send_s, recv_s, small_arrays, token)
    tot, g_ada_w, loss, qk = _reduce_small(packed, land, silu_c)
    g_b_in, g_b_out = finish_grads("b", tot)
    update("b_w_in", g_b_in, copy_grad=True)
    update("b_w_out", g_b_out, copy_grad=True)
    token = sum_grads("a", outs["b_w_in"][1])
    cw = D_MODEL // N_CHIPS
    g_small = dict(
        norm_g=tot[0:2], a_conv_b=tot[2:3], a_ln_g=tot[3:4], a_ln_b=tot[4:5],
        b_q_norm=qk[0:3], b_k_norm=qk[3:6],
        ada_b=jnp.stack([tot[12:16, :ns].reshape(3 * D_MODEL), tot[16:20, :ns].reshape(3 * D_MODEL)]),
        a_conv_w=lax.dynamic_slice(tot[20:20 + CONV_WIDTH], (0, chip * cw), (CONV_WIDTH, cw)),
    )
    update("ada_w", g_ada_w.reshape(2 * D_MODEL, ns), after=token)
    for k, g2 in g_small.items():
        update(k, g2, after=token)
    g_a_in, g_a_out = finish_grads("a", outs["ada_w"][1])
    update("a_w_in", g_a_in, copy_grad=True)
    update("a_w_out", g_a_out, copy_grad=True)
    return (loss.reshape(()), grad_x[None], *[outs[k][0] for k in order], *[outs[k][1] for k in order],
            *[outs[k][2] for k in order], *[outs[k][3] for k in order])
```

```python
import functools

import jax
import jax.numpy as jnp
from jax import lax
from jax.experimental import pallas as pl
from jax.experimental.pallas import tpu as pltpu

F32 = jnp.float32
BF16 = jnp.bfloat16

SEQ = 2048
D_MODEL = 1024
CONV_WIDTH = 31
HEAD_DIM = 64
N_HEADS = 16
DILATIONS = (1, 4, 16)
ATTN_BLOCK = 128
NORM_EPS = 1e-6
NEG_INF = -1e30
N_DEV = 8
N_CHIPS = 4

ADAM_LR = 0.001
ADAM_B1 = 0.9
ADAM_B2 = 0.999
ADAM_EPS = 1e-08
ADAM_WD = 0.01
ADAM_STEP = 10

VMEM_LIMIT_BYTES = 52 * 1024 * 1024
HALO = 32
LANES = 128
MESH = pl.DeviceIdType.MESH


def _params(*sem):
    return pltpu.CompilerParams(dimension_semantics=sem or None, vmem_limit_bytes=VMEM_LIMIT_BYTES)


def _sigmoid(v):
    return 1.0 / (1.0 + jnp.exp(-v))


def _row_spec(tm, cols, col_block=0):
    return pl.BlockSpec((tm, cols), lambda i: (i, col_block))


def _vec_spec(rows, cols):
    return pl.BlockSpec((rows, cols), lambda i: (0, 0))


def _normmod(xv, g, scale, shift):
    r = lax.rsqrt(jnp.mean(xv * xv, axis=-1, keepdims=True) + NORM_EPS)
    return xv * r * g * (1.0 + scale) + shift


def _normmod_fwd(x, g, scale, shift, name):
    tm = 256

    def body(x_ref, g_ref, sc_ref, sh_ref, h_ref, ht_ref):
        h = _normmod(x_ref[...], g_ref[...], sc_ref[...], sh_ref[...])
        h_ref[...] = h.astype(BF16)
        ht_ref[...] = h.T.astype(BF16)

    return pl.pallas_call(
        body, name=name, grid=(SEQ // tm,),
        in_specs=[_row_spec(tm, D_MODEL)] + [_vec_spec(1, D_MODEL)] * 3,
        out_specs=[_row_spec(tm, D_MODEL), pl.BlockSpec((D_MODEL, tm), lambda i: (0, i))],
        out_shape=[jax.ShapeDtypeStruct((SEQ, D_MODEL), BF16), jax.ShapeDtypeStruct((D_MODEL, SEQ), BF16)],
        compiler_params=_params("parallel"),
    )(x, g, scale, shift)


def _normmod_bwd(x, g, scale, dh_parts, dres, name, part_dilations=None, gated=None):
    tm = 256
    n_parts = len(dh_parts)
    dils = part_dilations or (1,) * n_parts
    dh_parts = [p if d == 1 else p.reshape(d, SEQ // d, D_MODEL) for p, d in zip(dh_parts, dils)]
    n_gated = 0 if gated is None else 2

    def body(x_ref, g_ref, sc_ref, dres_ref, *rest):
        part_refs = rest[:n_parts]
        gated_refs = rest[n_parts:n_parts + n_gated]
        out_refs = rest[n_parts + n_gated:]
        dx_ref, sums_ref, nat = out_refs[0], out_refs[1], out_refs[-1]
        xv = x_ref[...]
        r = lax.rsqrt(jnp.mean(xv * xv, axis=-1, keepdims=True) + NORM_EPS)
        xn = xv * r
        dh = _load_natural(part_refs[0], nat, dils[0])
        for p, d in zip(part_refs[1:], dils[1:]):
            dh = dh + _load_natural(p, nat, d)
        gv = g_ref[...]
        one_sc = 1.0 + sc_ref[...]
        dxn = dh * (gv * one_sc)
        dx = dres_ref[...] + r * (dxn - xn * jnp.mean(dxn * xn, axis=-1, keepdims=True))
        dx_ref[...] = dx
        dhx = dh * xn
        rows = [jnp.sum(dhx, axis=0, keepdims=True) * one_sc,
                jnp.sum(dhx, axis=0, keepdims=True) * gv,
                jnp.sum(dh, axis=0, keepdims=True)]
        if gated is not None:
            gate_ref, y_ref = gated_refs
            out_refs[2][...] = (dx * gate_ref[...]).astype(BF16)
            rows.append(jnp.sum(dx * y_ref[...].astype(F32), axis=0, keepdims=True))
        sums = jnp.concatenate(rows + [jnp.zeros((8 - len(rows), D_MODEL), F32)], axis=0)

        @pl.when(pl.program_id(0) == 0)
        def _():
            sums_ref[...] = jnp.zeros_like(sums_ref)

        sums_ref[...] += sums

    gated_specs = [] if gated is None else [_vec_spec(1, D_MODEL), _row_spec(tm, D_MODEL)]
    dy_spec = [] if gated is None else [_row_spec(tm, D_MODEL)]
    dy_shape = [] if gated is None else [jax.ShapeDtypeStruct((SEQ, D_MODEL), BF16)]
    return pl.pallas_call(
        body, name=name, grid=(SEQ // tm,),
        in_specs=[_row_spec(tm, D_MODEL), _vec_spec(1, D_MODEL), _vec_spec(1, D_MODEL), _row_spec(tm, D_MODEL)]
        + [_class_spec(tm, d) for d in dils] + gated_specs,
        out_specs=[_row_spec(tm, D_MODEL), _vec_spec(8, D_MODEL)] + dy_spec,
        out_shape=[jax.ShapeDtypeStruct((SEQ, D_MODEL), F32), jax.ShapeDtypeStruct((8, D_MODEL), F32)] + dy_shape,
        scratch_shapes=[_natural_scratch(tm)],
        compiler_params=_params("arbitrary"),
    )(x, g, scale, dres, *dh_parts, *(gated or ()))


def _mm(lhs, rhs, *, tn, tile0, n_tiles, out_dtype, name, out3d=None, prev=None, transpose_lhs=False):
    mo, kc = lhs.shape[::-1] if transpose_lhs else lhs.shape
    cm = min(mo, 1024)
    tc = 256

    def body(l_ref, r_ref, *rest):
        if transpose_lhs:
            o_ref, lt_ref = rest[-2], rest[-1]

            @pl.when(pl.program_id(0) == 0)
            def _():
                for c in range(kc // tc):
                    lt_ref[:, c * tc:(c + 1) * tc] = l_ref[c * tc:(c + 1) * tc, :].astype(F32).T.astype(l_ref.dtype)
        else:
            o_ref, lt_ref = rest[-1], l_ref
        for m in range(mo // cm):
            rows = pl.ds(m * cm, cm)
            o_ref[rows, :] = jnp.dot(lt_ref[rows, :], r_ref[...], preferred_element_type=F32).astype(out_dtype)

    if rhs.ndim == 3:
        tps_r = rhs.shape[2] // tn
        r_spec = pl.BlockSpec((None, kc, tn), lambda t: ((tile0 + t) // tps_r, 0, (tile0 + t) % tps_r))
    else:
        r_spec = pl.BlockSpec((kc, tn), lambda t: (0, t))
    in_specs = [pl.BlockSpec(lhs.shape, lambda t: (0, 0)), r_spec]
    args = [lhs, rhs]
    aliases = {}
    if out3d is None:
        o_spec = pl.BlockSpec((mo, tn), lambda t: (0, t))
        o_shape = jax.ShapeDtypeStruct((mo, n_tiles * tn), out_dtype)
    else:
        j_out, ns_out = out3d
        tps_o = ns_out // tn
        o_spec = pl.BlockSpec((None, mo, tn), lambda t: ((tile0 + t) // tps_o, 0, (tile0 + t) % tps_o))
        o_shape = jax.ShapeDtypeStruct((j_out, mo, ns_out), out_dtype)
        if prev is not None:
            in_specs.append(pl.BlockSpec(memory_space=pl.ANY))
            args.append(prev)
            aliases = {2: 0}
    return pl.pallas_call(
        body, name=name, grid=(n_tiles,), in_specs=in_specs, out_specs=o_spec, out_shape=o_shape,
        input_output_aliases=aliases,
        scratch_shapes=[pltpu.VMEM((mo, kc), lhs.dtype)] if transpose_lhs else [],
        compiler_params=_params("arbitrary" if transpose_lhs else "parallel"),
    )(*args)


def _in_tiles(h_parts, w3, tile_ids, n_tiles, *, tn, total_tiles, part_of, name, prev=None):
    _, kc, ns = w3.shape
    tps = ns // tn
    cm = 1024
    n_parts = len(h_parts)

    def body(ids_ref, *rest):
        h_refs, w_ref, o_ref = rest[:n_parts], rest[n_parts], rest[-1]
        part = part_of(ids_ref[1, pl.program_id(0)])
        for g, h_ref in enumerate(h_refs):
            @pl.when(part == g)
            def _():
                for m in range(SEQ // cm):
                    rows = pl.ds(m * cm, cm)
                    o_ref[rows, :] = jnp.dot(h_ref[rows, :], w_ref[...], preferred_element_type=F32).astype(BF16)

    resident = pl.BlockSpec((SEQ, kc), lambda t, ids: (0, 0))
    in_specs = [resident] * n_parts + [
        pl.BlockSpec((None, kc, tn), lambda t, ids: (ids[0, t] // tps, 0, ids[0, t] % tps))]
    args = [*h_parts, w3]
    aliases = {}
    if prev is not None:
        in_specs.append(pl.BlockSpec(memory_space=pl.ANY))
        args.append(prev)
        aliases = {n_parts + 2: 0}
    return pl.pallas_call(
        body, name=name,
        grid_spec=pltpu.PrefetchScalarGridSpec(
            num_scalar_prefetch=1, grid=(n_tiles,), in_specs=in_specs,
            out_specs=pl.BlockSpec((SEQ, tn), lambda t, ids: (0, ids[1, t]))),
        out_shape=jax.ShapeDtypeStruct((SEQ, total_tiles * tn), BF16),
        input_output_aliases=aliases, compiler_params=_params("arbitrary"),
    )(tile_ids, *args)


def _own_first(chip, total_tiles):
    own = total_tiles // N_CHIPS
    step = jnp.arange(total_tiles, dtype=jnp.int32)
    tiles = (own * chip + step) % total_tiles
    return jnp.stack([step[:own], tiles[:own]]), jnp.stack([tiles[own:], tiles[own:]]), own


def _mm_nt(dy, w3, *, tn, tile0, n_tiles, name, after=None):
    m_rows = dy.shape[0]
    _, kc, ns = w3.shape
    tps = ns // tn
    cm = 512
    extra = [] if after is None else [after]

    def body(dy_ref, w_ref, *rest):
        o_ref, acc = rest[-2], rest[-1]
        t = pl.program_id(0)

        @pl.when(t == 0)
        def _():
            acc[...] = jnp.zeros_like(acc)

        for m in range(m_rows // cm):
            rows = pl.ds(m * cm, cm)
            acc[rows, :] += lax.dot_general(dy_ref[rows, :], w_ref[...], NT_DIMS, preferred_element_type=F32)

        @pl.when(t == n_tiles - 1)
        def _():
            o_ref[...] = acc[...].astype(BF16)

    return pl.pallas_call(
        body, name=name, grid=(n_tiles,),
        in_specs=[pl.BlockSpec((m_rows, tn), lambda t: (0, t)),
                  pl.BlockSpec((None, kc, tn), lambda t: ((tile0 + t) // tps, 0, (tile0 + t) % tps))]
        + [pl.BlockSpec(memory_space=pl.ANY)] * len(extra),
        out_specs=pl.BlockSpec((m_rows, kc), lambda t: (0, 0)),
        out_shape=jax.ShapeDtypeStruct((m_rows, kc), BF16),
        scratch_shapes=[pltpu.VMEM((m_rows, kc), F32)],
        compiler_params=_params("arbitrary"),
    )(dy, w3, *extra)


CONV_CHUNK = 16


def _shift_copies(buf, shifted):
    rows = shifted.shape[1]
    for s in range(1, 8):
        shifted[s - 1] = buf[pl.ds(s, rows), :]


def _shifted_rows(buf, shifted, offset, r0):
    s = offset % 8
    if s == 0:
        return buf[pl.ds(r0 + offset, CONV_CHUNK), :]
    return shifted[s - 1, pl.ds(r0 + (offset - s), CONV_CHUNK), :]


def _spread_taps(w_ref, taps):
    for k in range(CONV_WIDTH):
        taps[k] = jnp.broadcast_to(w_ref[k:k + 1, :], (8, D_MODEL))


def _times_tap(taps, k, rows):
    return (rows.reshape(CONV_CHUNK // 8, 8, D_MODEL) * taps[k][None]).reshape(CONV_CHUNK, D_MODEL)


def _conv_fwd(proj, conv_w, conv_b, ln_g, ln_b, name):
    tm = 256
    hb = tm // HALO

    def body(vg_ref, halo_ref, z_ref, w_ref, b_ref, g_ref, be_ref, u5_ref, u5t_ref, u2_ref, buf, shifted, taps):
        i = pl.program_id(0)
        u1 = vg_ref[:, :D_MODEL].astype(F32) * _sigmoid(vg_ref[:, D_MODEL:].astype(F32))
        u1h = halo_ref[:, :D_MODEL].astype(F32) * _sigmoid(halo_ref[:, D_MODEL:].astype(F32))
        buf[pl.ds(0, HALO), :] = jnp.where(i > 0, u1h, 0.0)
        buf[pl.ds(HALO, tm), :] = u1
        _shift_copies(buf, shifted)
        _spread_taps(w_ref, taps)

        def chunk(ci, carry):
            r0 = pl.multiple_of(ci * CONV_CHUNK, CONV_CHUNK)
            acc = jnp.broadcast_to(b_ref[...], (CONV_CHUNK, D_MODEL))
            for k in range(CONV_WIDTH):
                acc = acc + _times_tap(taps, k, _shifted_rows(buf, shifted, HALO - (CONV_WIDTH - 1) + k, r0))
            u2_ref[pl.ds(r0, CONV_CHUNK), :] = acc
            return carry

        lax.fori_loop(0, tm // CONV_CHUNK, chunk, 0)
        acc = u2_ref[...]
        mu = jnp.mean(acc, axis=-1, keepdims=True)
        xc = acc - mu
        rstd = lax.rsqrt(jnp.mean(xc * xc, axis=-1, keepdims=True) + NORM_EPS)
        u3 = xc * rstd * g_ref[...] + be_ref[...]
        zv = z_ref[...].astype(F32)
        u5 = u3 * _sigmoid(u3) * (zv * _sigmoid(zv))
        u5_ref[...] = u5.astype(BF16)
        u5t_ref[...] = u5.T.astype(BF16)

    return pl.pallas_call(
        body, name=name, grid=(SEQ // tm,),
        in_specs=[pl.BlockSpec((tm, 2 * D_MODEL), lambda i: (i, 0)),
                  pl.BlockSpec((HALO, 2 * D_MODEL), lambda i: (jnp.maximum(i * hb - 1, 0), 0)),
                  _row_spec(tm, D_MODEL, 2),
                  _vec_spec(CONV_WIDTH, D_MODEL)] + [_vec_spec(1, D_MODEL)] * 3,
        out_specs=[_row_spec(tm, D_MODEL), pl.BlockSpec((D_MODEL, tm), lambda i: (0, i)), _row_spec(tm, D_MODEL)],
        out_shape=[jax.ShapeDtypeStruct((SEQ, D_MODEL), BF16), jax.ShapeDtypeStruct((D_MODEL, SEQ), BF16),
                   jax.ShapeDtypeStruct((SEQ, D_MODEL), F32)],
        scratch_shapes=[pltpu.VMEM((HALO + tm, D_MODEL), F32), pltpu.VMEM((7, HALO + tm - 8, D_MODEL), F32),
                        pltpu.VMEM((CONV_WIDTH, 8, D_MODEL), F32)],
        compiler_params=_params("parallel"),
    )(proj, proj, proj, conv_w, conv_b, ln_g, ln_b)


def _conv_bwd_pointwise(dy, w_out, proj, u2, ln_g, ln_b, name):
    tm = 256

    def body(dy_ref, w_ref, z_ref, u2_ref, g_ref, be_ref, du2_ref, dz_ref, sums_ref):
        u2v = u2_ref[...]
        mu = jnp.mean(u2v, axis=-1, keepdims=True)
        xc = u2v - mu
        rstd = lax.rsqrt(jnp.mean(xc * xc, axis=-1, keepdims=True) + NORM_EPS)
        xhat = xc * rstd
        u3 = xhat * g_ref[...] + be_ref[...]
        s3 = _sigmoid(u3)
        u4 = u3 * s3
        zv = z_ref[...].astype(F32)
        sz = _sigmoid(zv)
        du5v = lax.dot_general(dy_ref[...], w_ref[...], NT_DIMS, preferred_element_type=F32)
        dz_ref[...] = du5v * u4 * (sz * (1.0 + zv * (1.0 - sz)))
        du3 = du5v * (zv * sz) * (s3 * (1.0 + u3 * (1.0 - s3)))
        dxhat = du3 * g_ref[...]
        du2 = rstd * (dxhat - jnp.mean(dxhat, axis=-1, keepdims=True)
                      - xhat * jnp.mean(dxhat * xhat, axis=-1, keepdims=True))
        du2_ref[...] = du2
        sums = jnp.concatenate([
            jnp.sum(du3 * xhat, axis=0, keepdims=True),
            jnp.sum(du3, axis=0, keepdims=True),
            jnp.sum(du2, axis=0, keepdims=True),
            jnp.zeros((5, D_MODEL), F32)], axis=0)

        @pl.when(pl.program_id(0) == 0)
        def _():
            sums_ref[...] = jnp.zeros_like(sums_ref)

        sums_ref[...] += sums

    return pl.pallas_call(
        body, name=name, grid=(SEQ // tm,),
        in_specs=[_row_spec(tm, D_MODEL), _vec_spec(D_MODEL, D_MODEL), _row_spec(tm, D_MODEL, 2),
                  _row_spec(tm, D_MODEL), _vec_spec(1, D_MODEL), _vec_spec(1, D_MODEL)],
        out_specs=[_row_spec(tm, D_MODEL), _row_spec(tm, D_MODEL), _vec_spec(8, D_MODEL)],
        out_shape=[jax.ShapeDtypeStruct((SEQ, D_MODEL), F32), jax.ShapeDtypeStruct((SEQ, D_MODEL), F32),
                   jax.ShapeDtypeStruct((8, D_MODEL), F32)],
        compiler_params=_params("arbitrary"),
    )(dy, w_out, proj, u2, ln_g, ln_b)


def _conv_bwd_taps(du2, dz, proj, conv_w, name):
    tm = 256
    hb = tm // HALO
    n_blocks = SEQ // tm

    def body(du2_ref, dnext_ref, dz_ref, vg_ref, w_ref, dproj_ref, dw_ref, dbuf, dshift, sgbuf, ubuf, dwacc, taps):
        i = pl.program_id(0)
        _spread_taps(w_ref, taps)
        sg = _sigmoid(vg_ref[:, D_MODEL:].astype(F32))
        sgbuf[...] = sg
        ubuf[...] = vg_ref[:, :D_MODEL].astype(F32) * sg
        dbuf[pl.ds(0, tm), :] = du2_ref[...]
        dbuf[pl.ds(tm, HALO), :] = jnp.where(i < n_blocks - 1, dnext_ref[...], 0.0)
        _shift_copies(dbuf, dshift)

        @pl.when(i == 0)
        def _():
            dwacc[...] = jnp.zeros_like(dwacc)

        def chunk(ci, carry):
            r0 = pl.multiple_of(ci * CONV_CHUNK, CONV_CHUNK)
            rows = pl.ds(r0, CONV_CHUNK)
            u1c = ubuf[rows, :]
            du1 = jnp.zeros((CONV_CHUNK, D_MODEL), F32)
            for k in range(CONV_WIDTH):
                ahead = _shifted_rows(dbuf, dshift, CONV_WIDTH - 1 - k, r0)
                du1 = du1 + _times_tap(taps, k, ahead)
                prod = u1c * ahead
                dwacc[k] += prod[0:8] + prod[8:16]
            sgc = sgbuf[rows, :]
            dval = du1 * sgc
            dproj_ref[rows, 0:D_MODEL] = dval.astype(BF16)
            dproj_ref[rows, D_MODEL:2 * D_MODEL] = (
                dval * vg_ref[rows, 0:D_MODEL].astype(F32) * (1.0 - sgc)).astype(BF16)
            return carry

        lax.fori_loop(0, tm // CONV_CHUNK, chunk, 0)
        dproj_ref[:, 2 * D_MODEL:] = dz_ref[...].astype(BF16)

        @pl.when(i == n_blocks - 1)
        def _():
            for k in range(CONV_WIDTH):
                dw_ref[k:k + 1, :] = jnp.sum(dwacc[k], axis=0, keepdims=True)
            dw_ref[CONV_WIDTH:, :] = jnp.zeros((32 - CONV_WIDTH, D_MODEL), F32)

    return pl.pallas_call(
        body, name=name, grid=(n_blocks,),
        in_specs=[_row_spec(tm, D_MODEL),
                  pl.BlockSpec((HALO, D_MODEL), lambda i: (jnp.minimum((i + 1) * hb, SEQ // HALO - 1), 0)),
                  _row_spec(tm, D_MODEL),
                  pl.BlockSpec((tm, 2 * D_MODEL), lambda i: (i, 0)),
                  _vec_spec(CONV_WIDTH, D_MODEL)],
        out_specs=[_row_spec(tm, 3 * D_MODEL), _vec_spec(32, D_MODEL)],
        out_shape=[jax.ShapeDtypeStruct((SEQ, 3 * D_MODEL), BF16), jax.ShapeDtypeStruct((32, D_MODEL), F32)],
        scratch_shapes=[pltpu.VMEM((tm + HALO, D_MODEL), F32), pltpu.VMEM((7, HALO + tm - 8, D_MODEL), F32),
                        pltpu.VMEM((tm, D_MODEL), F32), pltpu.VMEM((tm, D_MODEL), F32),
                        pltpu.VMEM((CONV_WIDTH, 8, D_MODEL), F32), pltpu.VMEM((CONV_WIDTH, 8, D_MODEL), F32)],
        compiler_params=_params("arbitrary"),
    )(du2, du2, dz, proj, conv_w)


def _out_a(u5, w_out, x, gate, g1, scale1, shift1, name):
    tm = 256
    n_d = len(DILATIONS)

    def body(u_ref, w_ref, x_ref, gate_ref, g_ref, sc_ref, sh_ref, x1_ref, y_ref, ht_ref, *rest):
        h_refs, nat = rest[:n_d], rest[-1]
        y = jnp.dot(u_ref[...], w_ref[...], preferred_element_type=F32)
        x1 = x_ref[...] + gate_ref[...] * y
        y_ref[...] = y.astype(BF16)
        x1_ref[...] = x1
        h = _normmod(x1, g_ref[...], sc_ref[...], sh_ref[...])
        ht_ref[...] = h.T.astype(BF16)
        for h_ref, d in zip(h_refs, DILATIONS):
            _store_classes(h_ref, h, nat, d)

    res = pl.pallas_call(
        body, name=name, grid=(SEQ // tm,),
        in_specs=[_row_spec(tm, D_MODEL), _vec_spec(D_MODEL, D_MODEL), _row_spec(tm, D_MODEL)]
        + [_vec_spec(1, D_MODEL)] * 4,
        out_specs=[_row_spec(tm, D_MODEL), _row_spec(tm, D_MODEL), pl.BlockSpec((D_MODEL, tm), lambda i: (0, i))]
        + [_class_spec(tm, d) for d in DILATIONS],
        out_shape=[jax.ShapeDtypeStruct((SEQ, D_MODEL), F32), jax.ShapeDtypeStruct((SEQ, D_MODEL), BF16),
                   jax.ShapeDtypeStruct((D_MODEL, SEQ), BF16)] + [_class_shape(d, BF16) for d in DILATIONS],
        scratch_shapes=[_natural_scratch(tm)],
        compiler_params=_params("parallel"),
    )(u5, w_out, x, gate, g1, scale1, shift1)
    return res[0], res[1], res[2], [a.reshape(SEQ, D_MODEL) for a in res[3:]]


def _out_b_loss(u, w_out, x1, gate, target, name):
    tm = 256

    def body(u_ref, w_ref, x_ref, gate_ref, t_ref, e_ref, dy_ref, sums_ref):
        y = jnp.dot(u_ref[...], w_ref[...], preferred_element_type=F32)
        diff = x_ref[...] + gate_ref[...] * y - t_ref[...]
        e = diff * (1.0 / D_MODEL)
        e_ref[...] = e
        dy_ref[...] = (e * gate_ref[...]).astype(BF16)
        sums = jnp.concatenate([
            jnp.sum(e * y, axis=0, keepdims=True),
            jnp.sum(diff * diff, axis=0, keepdims=True),
            jnp.zeros((6, D_MODEL), F32)], axis=0)

        @pl.when(pl.program_id(0) == 0)
        def _():
            sums_ref[...] = jnp.zeros_like(sums_ref)

        sums_ref[...] += sums

    return pl.pallas_call(
        body, name=name, grid=(SEQ // tm,),
        in_specs=[_row_spec(tm, D_MODEL), _vec_spec(D_MODEL, D_MODEL), _row_spec(tm, D_MODEL),
                  _vec_spec(1, D_MODEL), _row_spec(tm, D_MODEL)],
        out_specs=[_row_spec(tm, D_MODEL), _row_spec(tm, D_MODEL), _vec_spec(8, D_MODEL)],
        out_shape=[jax.ShapeDtypeStruct((SEQ, D_MODEL), F32), jax.ShapeDtypeStruct((SEQ, D_MODEL), BF16),
                   jax.ShapeDtypeStruct((8, D_MODEL), F32)],
        compiler_params=_params("arbitrary"),
    )(u, w_out, x1, gate, target)


def _seg_matrix():
    r = lax.broadcasted_iota(jnp.int32, (256, 256), 0) // HEAD_DIM
    c = lax.broadcasted_iota(jnp.int32, (256, 256), 1) // HEAD_DIM
    return jnp.where(r == c, 1.0 / HEAD_DIM, 0.0).astype(BF16)


def _segmean(v, seg):
    hi = v.astype(BF16)
    lo = (v - hi.astype(F32)).astype(BF16)
    outs = []
    for c0 in range(0, D_MODEL, 256):
        outs.append(jnp.dot(hi[:, c0:c0 + 256], seg, preferred_element_type=F32)
                    + jnp.dot(lo[:, c0:c0 + 256], seg, preferred_element_type=F32))
    return jnp.concatenate(outs, axis=1)


def _qk_rstd(v, seg):
    return lax.rsqrt(_segmean(v * v, seg) + NORM_EPS)


def _qknorm_fwd(proj, group, qw, kw, seg, name):
    tm = 256

    def body(q_in, k_in, qw_ref, kw_ref, seg_ref, q_ref, k_ref):
        segv = seg_ref[...]
        q = q_in[...].astype(F32)
        k = k_in[...].astype(F32)
        q_ref[...] = (q * _qk_rstd(q, segv) * qw_ref[...] * HEAD_DIM ** -0.5).astype(BF16)
        k_ref[...] = (k * _qk_rstd(k, segv) * kw_ref[...]).astype(BF16)

    return pl.pallas_call(
        body, name=name, grid=(SEQ // tm,),
        in_specs=[_row_spec(tm, D_MODEL, 3 * group), _row_spec(tm, D_MODEL, 3 * group + 1),
                  _vec_spec(1, D_MODEL), _vec_spec(1, D_MODEL), _vec_spec(256, 256)],
        out_specs=[_row_spec(tm, D_MODEL)] * 2,
        out_shape=[jax.ShapeDtypeStruct((SEQ, D_MODEL), BF16)] * 2,
        compiler_params=_params("parallel"),
    )(proj, proj, qw, kw, seg)


def _attn_masks(b, bpc, dilation, transposed=False):
    keys = ATTN_BLOCK if bpc == 1 else 2 * ATTN_BLOCK
    shape, q_axis = ((keys, ATTN_BLOCK), 1) if transposed else ((ATTN_BLOCK, keys), 0)
    qi = lax.broadcasted_iota(jnp.int32, shape, q_axis)
    kj = lax.broadcasted_iota(jnp.int32, shape, 1 - q_axis)
    if bpc == 1:
        steps = qi - kj
        return (steps * dilation).astype(F32), steps >= 0
    steps = qi + ATTN_BLOCK - kj
    has_prev = (b % bpc) != 0
    valid = (steps >= 0) & (steps <= ATTN_BLOCK) & (has_prev | (kj >= ATTN_BLOCK))
    return (steps * dilation).astype(F32), valid


MASKED = 1e30


def _bias_scratch(bpc):
    return pltpu.VMEM((1 if bpc == 1 else 2, N_HEADS, ATTN_BLOCK, (1 if bpc == 1 else 2) * ATTN_BLOCK), F32)


def _fill_bias(bias_ref, sl_ref, bpc, dilation):
    for variant in range(bias_ref.shape[0]):
        dist, valid = _attn_masks(variant, min(bpc, 2), dilation)
        bias_ref[variant] = jnp.where(valid[None], dist[None] * sl_ref[...], MASKED)


def _step_bias(bias_ref, b, bpc):
    if bpc == 1:
        return bias_ref[0]
    return bias_ref[jnp.where((b % bpc) != 0, 1, 0)]


def _key_tile(prev_ref, cur_ref, cols, bpc):
    if bpc == 1:
        return cur_ref[:, cols]
    return jnp.concatenate([prev_ref[:, cols], cur_ref[:, cols]], axis=0)


ATTN_HEADS_FWD = 16
ATTN_HEADS_BWD = 16
NT_DIMS = (((1,), (1,)), ((), ()))
BATCH_NT_DIMS = (((2,), (2,)), ((0,), (0,)))
BATCH_NN_DIMS = (((2,), (1,)), ((0,), (0,)))
BATCH_TN_DIMS = (((1,), (1,)), ((0,), (0,)))


def _head_stack(tile_of, heads):
    return jnp.stack([tile_of(slice(h * HEAD_DIM, (h + 1) * HEAD_DIM)) for h in range(heads)], axis=0)


def _attn_specs(heads, segment=0):
    width = heads * HEAD_DIM
    off = segment * (D_MODEL // width)
    last = SEQ // ATTN_BLOCK - 1
    cur = pl.BlockSpec((ATTN_BLOCK, width), lambda hg, b: (jnp.minimum(b, last), hg + off))
    prev = pl.BlockSpec((ATTN_BLOCK, width), lambda hg, b: (jnp.clip(b - 1, 0, last), hg + off))
    return cur, prev


def _attn_fwd(q, k, proj, group, slopes, dilation, name):
    bpc = SEQ // dilation // ATTN_BLOCK
    heads = ATTN_HEADS_FWD
    assert heads == N_HEADS
    cur, prev = _attn_specs(heads)
    v_cur, v_prev = _attn_specs(heads, segment=3 * group + 2)

    def body(sl_ref, q_ref, kp_ref, kc_ref, vp_ref, vc_ref, o_ref, lse_ref, bias_ref):
        b = pl.program_id(1)

        @pl.when(b == 0)
        def _():
            _fill_bias(bias_ref, sl_ref, bpc, dilation)

        q3 = _head_stack(lambda cols: q_ref[:, cols], heads)
        k3 = _head_stack(lambda cols: _key_tile(kp_ref, kc_ref, cols, bpc), heads)
        v3 = _head_stack(lambda cols: _key_tile(vp_ref, vc_ref, cols, bpc), heads)
        s = lax.dot_general(q3, k3, BATCH_NT_DIMS, preferred_element_type=F32)
        s = s - _step_bias(bias_ref, b, bpc)
        m = jnp.max(s, axis=-1, keepdims=True)
        p = jnp.exp(s - m)
        l = jnp.sum(p, axis=-1, keepdims=True)
        o3 = lax.dot_general(p.astype(BF16), v3, BATCH_NN_DIMS, preferred_element_type=F32) / l
        lse3 = m + jnp.log(l)
        for h in range(heads):
            o_ref[:, h * HEAD_DIM:(h + 1) * HEAD_DIM] = o3[h].astype(BF16)
        lse_ref[...] = jnp.concatenate([lse3[h] for h in range(heads)]
                                       + [jnp.zeros((ATTN_BLOCK, LANES - heads), F32)], axis=1)

    return pl.pallas_call(
        body, name=name, grid=(N_HEADS // heads, SEQ // ATTN_BLOCK),
        in_specs=[pl.BlockSpec((heads, 1, 1), lambda hg, b: (hg, 0, 0)), cur, prev, cur, v_prev, v_cur],
        out_specs=[cur, pl.BlockSpec((ATTN_BLOCK, LANES), lambda hg, b: (b, 0))],
        out_shape=[jax.ShapeDtypeStruct((SEQ, D_MODEL), BF16), jax.ShapeDtypeStruct((SEQ, LANES), F32)],
        scratch_shapes=[_bias_scratch(bpc)],
        compiler_params=_params("parallel", "arbitrary"),
    )(slopes.reshape(N_HEADS, 1, 1), q, k, k, proj, proj)


def _class_spec(tm, dilation, width=D_MODEL):
    if dilation == 1:
        return _row_spec(tm, width)
    return pl.BlockSpec((dilation, tm // dilation, width), lambda i: (0, i, 0))


def _class_shape(dilation, dtype, width=D_MODEL):
    if dilation == 1:
        return jax.ShapeDtypeStruct((SEQ, width), dtype)
    return jax.ShapeDtypeStruct((dilation, SEQ // dilation, width), dtype)


def _load_natural(in_ref, nat_ref, dilation):
    if dilation == 1:
        return in_ref[...].astype(F32)
    n = nat_ref.shape[1] // dilation
    tiles = in_ref.shape[-1] // LANES
    for r in range(dilation):
        for j in range(tiles):
            nat_ref.at[j][pl.ds(r, n, stride=dilation), :] = in_ref[r, :, j * LANES:(j + 1) * LANES].astype(F32)
    if tiles == 1:
        return nat_ref[0]
    return jnp.concatenate([nat_ref[j] for j in range(tiles)], axis=1)


def _store_classes(out_ref, value, nat_ref, dilation):
    if dilation == 1:
        out_ref[...] = value.astype(out_ref.dtype)
        return
    n = nat_ref.shape[1] // dilation
    tiles = value.shape[-1] // LANES
    for j in range(tiles):
        nat_ref[j] = value[:, j * LANES:(j + 1) * LANES]
    for r in range(dilation):
        for j in range(tiles):
            out_ref[r, :, j * LANES:(j + 1) * LANES] = (
                nat_ref.at[j][pl.ds(r, n, stride=dilation), :].astype(out_ref.dtype))


def _natural_scratch(tm):
    return pltpu.VMEM((D_MODEL // LANES, tm, LANES), F32)


def _head_selector():
    lane_head = lax.broadcasted_iota(jnp.int32, (D_MODEL, LANES), 0) // HEAD_DIM
    head = lax.broadcasted_iota(jnp.int32, (D_MODEL, LANES), 1)
    return (lane_head == head).astype(BF16)


def _dot_split(v, m01, dims):
    hi = v.astype(BF16)
    lo = (v - hi.astype(F32)).astype(BF16)
    return (lax.dot_general(hi, m01, dims, preferred_element_type=F32)
            + lax.dot_general(lo, m01, dims, preferred_element_type=F32))


def _merge_fwd(o_parts, lse_parts, z, sel, name):
    tm = 256
    h_spec = pl.BlockSpec((tm, LANES), lambda i: (i, 0))

    def body(o0, o1, o2, l0, l1, l2, z_ref, sel_ref, u_ref, ut_ref, o_ref, lse_ref, nat):
        ls = [_load_natural(l, nat, d) for l, d in zip((l0, l1, l2), DILATIONS)]
        m = jnp.maximum(jnp.maximum(ls[0], ls[1]), ls[2])
        tot = m + jnp.log(jnp.exp(ls[0] - m) + jnp.exp(ls[1] - m) + jnp.exp(ls[2] - m))
        o = jnp.zeros((tm, D_MODEL), F32)
        for o_in, l, d in zip((o0, o1, o2), ls, DILATIONS):
            weight = _dot_split(jnp.exp(l - tot), sel_ref[...], NT_DIMS)
            o = o + weight * _load_natural(o_in, nat, d)
        zv = z_ref[...].astype(F32)
        u = o * (zv * _sigmoid(zv))
        u_ref[...] = u.astype(BF16)
        ut_ref[...] = u.T.astype(BF16)
        o_ref[...] = o.astype(BF16)
        lse_ref[...] = tot

    return pl.pallas_call(
        body, name=name, grid=(SEQ // tm,),
        in_specs=[_class_spec(tm, d) for d in DILATIONS] + [_class_spec(tm, d, LANES) for d in DILATIONS]
        + [_row_spec(tm, D_MODEL, B_Z_SEGMENT), _vec_spec(D_MODEL, LANES)],
        out_specs=[_row_spec(tm, D_MODEL), pl.BlockSpec((D_MODEL, tm), lambda i: (0, i)),
                   _row_spec(tm, D_MODEL), h_spec],
        out_shape=[jax.ShapeDtypeStruct((SEQ, D_MODEL), BF16), jax.ShapeDtypeStruct((D_MODEL, SEQ), BF16),
                   jax.ShapeDtypeStruct((SEQ, D_MODEL), BF16), jax.ShapeDtypeStruct((SEQ, LANES), F32)],
        scratch_shapes=[_natural_scratch(tm)],
        compiler_params=_params("parallel"),
    )(*o_parts, *lse_parts, z, sel)


def _merge_bwd(dy, w_out, o, lse, z, sel, name):
    tm = 256
    n_d = len(DILATIONS)

    def body(dy_ref, w_ref, o_ref, lse_ref, z_ref, sel_ref, dz_ref, *rest):
        do_refs, delta_refs, lse_refs, nat = rest[:n_d], rest[n_d:2 * n_d], rest[2 * n_d:3 * n_d], rest[-1]
        zv = z_ref[...].astype(F32)
        sz = _sigmoid(zv)
        duv = lax.dot_general(dy_ref[...], w_ref[...], NT_DIMS, preferred_element_type=F32)
        ov = o_ref[...].astype(F32)
        do = duv * (zv * sz)
        dz_ref[...] = (duv * ov * (sz * (1.0 + zv * (1.0 - sz)))).astype(BF16)
        delta = _dot_split(do * ov, sel_ref[...], (((1,), (0,)), ((), ())))
        lv = lse_ref[...]
        for i, d in enumerate(DILATIONS):
            _store_classes(do_refs[i], do, nat, d)
            _store_classes(delta_refs[i], delta, nat, d)
            _store_classes(lse_refs[i], lv, nat, d)

    res = pl.pallas_call(
        body, name=name, grid=(SEQ // tm,),
        in_specs=[_row_spec(tm, D_MODEL), _vec_spec(D_MODEL, D_MODEL), _row_spec(tm, D_MODEL), _row_spec(tm, LANES),
                  _row_spec(tm, D_MODEL, B_Z_SEGMENT), _vec_spec(D_MODEL, LANES)],
        out_specs=[_row_spec(tm, D_MODEL)] + [_class_spec(tm, d) for d in DILATIONS]
        + [_class_spec(tm, d, LANES) for d in DILATIONS] * 2,
        out_shape=[jax.ShapeDtypeStruct((SEQ, D_MODEL), BF16)] + [_class_shape(d, BF16) for d in DILATIONS]
        + [_class_shape(d, F32, LANES) for d in DILATIONS] * 2,
        scratch_shapes=[_natural_scratch(tm)],
        compiler_params=_params("parallel"),
    )(dy, w_out, o, lse, z, sel)
    flat = lambda a: a.reshape(SEQ, a.shape[-1])
    return (res[0], [flat(a) for a in res[1:1 + n_d]], [flat(a) for a in res[1 + n_d:1 + 2 * n_d]],
            [flat(a) for a in res[1 + 2 * n_d:]])


def _attn_bwd(q, k, proj, group, do, lse, delta, slopes, dilation, name):
    bpc = SEQ // dilation // ATTN_BLOCK
    heads = ATTN_HEADS_BWD
    n_blocks = SEQ // ATTN_BLOCK
    carry = bpc > 1
    width = heads * HEAD_DIM
    cur, prev = _attn_specs(heads)
    v_cur, v_prev = _attn_specs(heads, segment=3 * group + 2)
    assert heads == N_HEADS
    per_head = pl.BlockSpec((ATTN_BLOCK, LANES), lambda hg, b: (jnp.minimum(b, n_blocks - 1), 0))
    scale = HEAD_DIM ** -0.5

    def body(sl_ref, q_ref, kp_ref, kc_ref, vp_ref, vc_ref, do_ref, lse_ref, dl_ref,
             dq_ref, dk_ref, dv_ref, *scratch):
        b = pl.program_id(1)
        if carry:
            dk_carry, dv_carry = scratch

            @pl.when(b == n_blocks)
            def _():
                dk_ref[...] = dk_carry[...].astype(BF16)
                dv_ref[...] = dv_carry[...].astype(BF16)

            @pl.when(b < n_blocks)
            def _():
                step(sl_ref, q_ref, kp_ref, kc_ref, vp_ref, vc_ref, do_ref, lse_ref, dl_ref,
                     dq_ref, dk_ref, dv_ref, dk_carry, dv_carry, b)
        else:
            step(sl_ref, q_ref, kp_ref, kc_ref, vp_ref, vc_ref, do_ref, lse_ref, dl_ref,
                 dq_ref, dk_ref, dv_ref, None, None, b)

    def step(sl_ref, q_ref, kp_ref, kc_ref, vp_ref, vc_ref, do_ref, lse_ref, dl_ref,
             dq_ref, dk_ref, dv_ref, dk_carry, dv_carry, b):
        if carry:
            @pl.when(b == 0)
            def _():
                dk_carry[...] = jnp.zeros_like(dk_carry)
                dv_carry[...] = jnp.zeros_like(dv_carry)

        q3 = _head_stack(lambda cols: q_ref[:, cols], heads)
        k3 = _head_stack(lambda cols: _key_tile(kp_ref, kc_ref, cols, bpc), heads)
        v3 = _head_stack(lambda cols: _key_tile(vp_ref, vc_ref, cols, bpc), heads)
        do3 = _head_stack(lambda cols: do_ref[:, cols], heads)
        lse_t = lse_ref[...].T
        dl_t = dl_ref[...].T
        lse3 = jnp.stack([lse_t[h:h + 1, :] for h in range(heads)], axis=0)
        dl3 = jnp.stack([dl_t[h:h + 1, :] for h in range(heads)], axis=0)
        s = lax.dot_general(k3, q3, BATCH_NT_DIMS, preferred_element_type=F32)
        dist, valid = _attn_masks(b, bpc, dilation, transposed=True)
        p = jnp.exp(jnp.where(valid[None], s - dist[None] * sl_ref[...], NEG_INF) - lse3)
        dp = lax.dot_general(v3, do3, BATCH_NT_DIMS, preferred_element_type=F32)
        ds = (p * (dp - dl3)).astype(BF16)
        dq3 = lax.dot_general(ds, k3, BATCH_TN_DIMS, preferred_element_type=F32) * scale
        dk3 = lax.dot_general(ds, q3, BATCH_NN_DIMS, preferred_element_type=F32)
        dv3 = lax.dot_general(p.astype(BF16), do3, BATCH_NN_DIMS, preferred_element_type=F32)
        for h in range(heads):
            cols = slice(h * HEAD_DIM, (h + 1) * HEAD_DIM)
            dq_ref[:, cols] = dq3[h].astype(BF16)
            if carry:
                dk_ref[:, cols] = (dk_carry[:, cols] + dk3[h, :ATTN_BLOCK]).astype(BF16)
                dv_ref[:, cols] = (dv_carry[:, cols] + dv3[h, :ATTN_BLOCK]).astype(BF16)
                dk_carry[:, cols] = dk3[h, ATTN_BLOCK:]
                dv_carry[:, cols] = dv3[h, ATTN_BLOCK:]
            else:
                dk_ref[:, cols] = dk3[h].astype(BF16)
                dv_ref[:, cols] = dv3[h].astype(BF16)

    kv_out = prev if carry else cur
    return pl.pallas_call(
        body, name=name, grid=(N_HEADS // heads, n_blocks + (1 if carry else 0)),
        in_specs=[pl.BlockSpec((heads, 1, 1), lambda hg, b: (hg, 0, 0)), cur, prev, cur, v_prev, v_cur,
                  cur, per_head, per_head],
        out_specs=[cur, kv_out, kv_out],
        out_shape=[jax.ShapeDtypeStruct((SEQ, D_MODEL), BF16)] * 3,
        scratch_shapes=[pltpu.VMEM((ATTN_BLOCK, width), F32)] * 2 if carry else [],
        compiler_params=_params("parallel", "arbitrary"),
    )(slopes.reshape(N_HEADS, 1, 1), q, k, k, proj, proj, do, lse, delta)


def _qknorm_bwd(proj, group, qw, kw, seg, dq, dk, dv, name):
    tm = 256

    def body(q_in, k_in, qw_ref, kw_ref, seg_ref, dq_ref, dk_ref, dv_ref, dproj_ref, sums_ref):
        segv = seg_ref[...]
        sums = []
        for part, (raw_ref, w_ref, dn_ref) in enumerate(((q_in, qw_ref, dq_ref), (k_in, kw_ref, dk_ref))):
            raw = raw_ref[...].astype(F32)
            dn = dn_ref[...].astype(F32)
            r = _qk_rstd(raw, segv)
            xhat = raw * r
            gq = dn * w_ref[...]
            draw = r * (gq - xhat * _segmean(xhat * gq, segv))
            dproj_ref[:, part * D_MODEL:(part + 1) * D_MODEL] = draw.astype(BF16)
            sums.append(jnp.sum(dn * xhat, axis=0, keepdims=True))
        dproj_ref[:, 2 * D_MODEL:] = dv_ref[...]

        @pl.when(pl.program_id(0) == 0)
        def _():
            sums_ref[...] = jnp.zeros_like(sums_ref)

        sums_ref[...] += jnp.concatenate(sums + [jnp.zeros((6, D_MODEL), F32)], axis=0)

    return pl.pallas_call(
        body, name=name, grid=(SEQ // tm,),
        in_specs=[_row_spec(tm, D_MODEL, 3 * group), _row_spec(tm, D_MODEL, 3 * group + 1),
                  _vec_spec(1, D_MODEL), _vec_spec(1, D_MODEL), _vec_spec(256, 256)] + [_row_spec(tm, D_MODEL)] * 3,
        out_specs=[_row_spec(tm, 3 * D_MODEL), _vec_spec(8, D_MODEL)],
        out_shape=[jax.ShapeDtypeStruct((SEQ, 3 * D_MODEL), BF16), jax.ShapeDtypeStruct((8, D_MODEL), F32)],
        compiler_params=_params("arbitrary"),
    )(proj, proj, qw, kw, seg, dq, dk, dv)


B_TN = 512
B_GROUP_TILES = 3 * D_MODEL // B_TN
B_Z_TILE0 = 3 * B_GROUP_TILES
B_Z_TILES = D_MODEL // B_TN
B_TILES = B_Z_TILE0 + B_Z_TILES
B_Z_SEGMENT = 3 * len(DILATIONS)


def _local_step(x, target, mods, norm_g, conv_w, conv_b, ln_g, ln_b, q_norm, k_norm, chip, own_wa_in, own_wb_in,
                weights_a, weights_b, forward_weights_b, send_grads_b, forward_grads_b, send_grads_a):
    row = lambda a, i: a[i:i + 1]
    shift0, scale0, gate0 = row(mods[0], 0), row(mods[0], 1), row(mods[0], 2)
    shift1, scale1, gate1 = row(mods[1], 0), row(mods[1], 1), row(mods[1], 2)
    g0, g1 = row(norm_g, 0), row(norm_g, 1)
    seg = _seg_matrix()
    slopes = jnp.exp2(-8.0 * jnp.arange(1, N_HEADS + 1, dtype=F32) / N_HEADS)
    qw = [jnp.tile(q_norm[g:g + 1], (1, N_HEADS)) for g in range(3)]
    kw = [jnp.tile(k_norm[g:g + 1], (1, N_HEADS)) for g in range(3)]

    h0, h0t = _normmod_fwd(x, g0, scale0, shift0, "prenorm0")
    nsa = own_wa_in.shape[2]
    tiles_a = dict(tn=nsa, total_tiles=N_CHIPS, part_of=lambda tile: 0)
    own_ids, rest_ids, own_tiles = _own_first(chip, N_CHIPS)
    proj_a = _in_tiles([h0], own_wa_in, own_ids, own_tiles, name="a_in_own", **tiles_a)
    wa_in, wa_out = weights_a(proj_a)
    ja = wa_in.shape[0]
    proj_a = _in_tiles([h0], wa_in, rest_ids, N_CHIPS - own_tiles, name="a_in_rest", prev=proj_a, **tiles_a)
    u5, u5t, u2 = _conv_fwd(proj_a, conv_w, conv_b, ln_g, ln_b, "a_conv")
    x1, y_a, h1t, h1c = _out_a(u5, wa_out, x, gate0, g1, scale1, shift1, "a_out")

    tiles_b = dict(tn=B_TN, total_tiles=B_TILES,
                   part_of=lambda tile: jnp.where(tile >= B_Z_TILE0, 0, tile // B_GROUP_TILES))
    own_ids, rest_ids, own_tiles = _own_first(chip, B_TILES)
    proj_b = _in_tiles(h1c, own_wb_in, own_ids, own_tiles, name="b_in_own", **tiles_b)
    forward_weights_b(proj_b)
    wb_in, wb_out = weights_b(proj_b)
    jb, _, nsb = wb_in.shape
    proj_b = _in_tiles(h1c, wb_in, rest_ids, B_TILES - own_tiles, name="b_in_rest", prev=proj_b, **tiles_b)
    h1 = h1c[0]
    qkv, o_parts, lse_parts = [], [], []
    for g, d in enumerate(DILATIONS):
        qn, kn = _qknorm_fwd(proj_b, g, qw[g], kw[g], seg, f"b_qknorm_g{g}")
        og, lg = _attn_fwd(qn, kn, proj_b, g, slopes, d, f"b_attn_g{g}")
        qkv.append((qn, kn))
        o_parts.append(og if d == 1 else og.reshape(d, SEQ // d, D_MODEL))
        lse_parts.append(lg if d == 1 else lg.reshape(d, SEQ // d, LANES))
    sel = _head_selector()
    u_b, u_bt, o_b, lse_b = _merge_fwd(o_parts, lse_parts, proj_b, sel, "b_merge")
    e, dy_b, sums_loss = _out_b_loss(u_b, wb_out, x1, gate1, target, "b_out_loss")

    dwb_out = _mm(u_bt, dy_b, tn=D_MODEL, tile0=0, n_tiles=1, out_dtype=BF16, name="b_dwout")
    dz_b, do_c, delta_c, lse_c = _merge_bwd(dy_b, wb_out, o_b, lse_b, proj_b, sel, "b_merge_bwd")
    dwb_in = _mm(h1t, dz_b, tn=B_TN, tile0=B_Z_TILE0, n_tiles=B_Z_TILES, out_dtype=BF16, name="b_dwin_z",
                 out3d=(jb, nsb))
    dh1_parts = [_mm_nt(dz_b, wb_in, tn=B_TN, tile0=B_Z_TILE0, n_tiles=B_Z_TILES, name="b_dh_z")]
    qk_sums = []
    for g, d in enumerate(DILATIONS):
        qn, kn = qkv[g]
        dq, dk, dv = _attn_bwd(qn, kn, proj_b, g, do_c[g], lse_c[g], delta_c[g], slopes, d, f"b_attn_bwd_g{g}")
        dproj, sums_qk = _qknorm_bwd(proj_b, g, qw[g], kw[g], seg, dq, dk, dv, f"b_qknorm_bwd_g{g}")
        qk_sums.append(sums_qk)
        dwb_in = _mm(h1t if d == 1 else h1c[g], dproj, tn=B_TN, tile0=g * B_GROUP_TILES, n_tiles=B_GROUP_TILES,
                     out_dtype=BF16, name=f"b_dwin_g{g}", out3d=(jb, nsb), prev=dwb_in, transpose_lhs=d != 1)
        dh = _mm_nt(dproj, wb_in, tn=B_TN, tile0=g * B_GROUP_TILES, n_tiles=B_GROUP_TILES, name=f"b_dh_g{g}")
        dh1_parts.append(dh)
    token = send_grads_b(dwb_in, dwb_out)
    dx1, sums_n1, dy_a = _normmod_bwd(x1, g1, scale1 + token[0:1, 0:1], dh1_parts, e, "prenorm1_bwd",
                                      part_dilations=(1,) + DILATIONS, gated=(gate0, y_a))
    token = forward_grads_b(dx1)

    dwa_out = _mm(u5t, dy_a, tn=D_MODEL, tile0=0, n_tiles=1, out_dtype=BF16, name="a_dwout")
    du2, dz_a, sums_ln = _conv_bwd_pointwise(dy_a, wa_out, proj_a, u2, ln_g + token[0:1, 0:1], ln_b,
                                             "a_conv_bwd_pw")
    dproj_a, dconv_w = _conv_bwd_taps(du2, dz_a, proj_a, conv_w, "a_conv_bwd_taps")
    dwa_in = _mm(h0t, dproj_a, tn=nsa, tile0=0, n_tiles=ja, out_dtype=BF16, name="a_dwin", out3d=(ja, nsa))
    token = send_grads_a(dwa_in, dwa_out)
    dh0 = _mm_nt(dproj_a, wa_in, tn=nsa, tile0=0, n_tiles=ja, name="a_dh", after=token)
    grad_x, sums_n0 = _normmod_bwd(x, g0, scale0, [dh0], dx1, "prenorm0_bwd")

    small = dict(
        dnorm_g=jnp.concatenate([sums_n0[0:1], sums_n1[0:1]], axis=0),
        dmod0=jnp.concatenate([sums_n0[2:3], sums_n0[1:2], sums_n1[3:4]], axis=0),
        dmod1=jnp.concatenate([sums_n1[2:3], sums_n1[1:2], sums_loss[0:1]], axis=0),
        dln_g=sums_ln[0:1], dln_b=sums_ln[1:2], dconv_b=sums_ln[2:3],
        dconv_w=dconv_w[:CONV_WIDTH],
        dq_norm=jnp.concatenate([s[0:1] for s in qk_sums], axis=0),
        dk_norm=jnp.concatenate([s[1:2] for s in qk_sums], axis=0),
        loss_cols=sums_loss[1:2],
    )
    return grad_x, small


def _adamw(w, g, m, v, name, after=None, copy_grad=False):
    rows, cols = w.shape
    tr = rows if rows <= 128 else 128
    c1 = 1.0 / (1.0 - ADAM_B1 ** ADAM_STEP)
    c2 = 1.0 / (1.0 - ADAM_B2 ** ADAM_STEP)
    extra = [] if after is None else [after]
    n_out = 4 if copy_grad else 3

    def body(w_ref, g_ref, m_ref, v_ref, *rest):
        d_ref, mo_ref, vo_ref = rest[len(extra):len(extra) + 3]
        gv = g_ref[...]
        if copy_grad:
            rest[-1][...] = gv
        mn = ADAM_B1 * m_ref[...] + (1.0 - ADAM_B1) * gv
        vn = ADAM_B2 * v_ref[...] + (1.0 - ADAM_B2) * (gv * gv)
        mo_ref[...] = mn
        vo_ref[...] = vn
        d_ref[...] = -ADAM_LR * ((mn * c1) / (jnp.sqrt(vn * c2) + ADAM_EPS) + ADAM_WD * w_ref[...])

    spec = pl.BlockSpec((tr, cols), lambda i: (i, 0))
    return pl.pallas_call(
        body, name=name, grid=(rows // tr,),
        in_specs=[spec] * 4 + [pl.BlockSpec(memory_space=pl.ANY)] * len(extra), out_specs=[spec] * n_out,
        out_shape=[jax.ShapeDtypeStruct((rows, cols), F32)] * n_out,
        compiler_params=_params("parallel"),
    )(w, g, m, v, *extra)


def _cast_into_slot(w, chip_idx, name, keep_own=False, after=None):
    rows, cols = w.shape
    tr = 256
    extra = [] if after is None else [after]

    def body(ch_ref, w_ref, *rest):
        wb = w_ref[...].astype(BF16)
        for o_ref in rest[len(extra):]:
            o_ref[...] = wb

    slot_spec = pl.BlockSpec((None, tr, cols), lambda i, ch: (ch[0], i, 0))
    own_spec = pl.BlockSpec((None, tr, cols), lambda i, ch: (0, i, 0))
    res = pl.pallas_call(
        body, name=name,
        grid_spec=pltpu.PrefetchScalarGridSpec(
            num_scalar_prefetch=1, grid=(rows // tr,),
            in_specs=[pl.BlockSpec((tr, cols), lambda i, ch: (i, 0))] + [pl.BlockSpec(memory_space=pl.ANY)] * len(extra),
            out_specs=[slot_spec, own_spec] if keep_own else [slot_spec]),
        out_shape=[jax.ShapeDtypeStruct((N_CHIPS, rows, cols), BF16)]
        + ([jax.ShapeDtypeStruct((1, rows, cols), BF16)] if keep_own else []),
        compiler_params=_params("parallel"),
    )(chip_idx, w, *extra)
    return tuple(res) if keep_own else res[0]


def _position():
    x, y, c = lax.axis_index("x"), lax.axis_index("y"), lax.axis_index("c")
    return x, y, c


def _xor_peer(x, y, c, k):
    return (x ^ ((k >> 2) & 1), y ^ ((k >> 1) & 1), c ^ (k & 1))


def _chip_peer(x, y, k):
    return (x ^ ((k >> 1) & 1), y ^ (k & 1))


def _ada_forward(c_row, ada_w, ada_b, conv_w, after=()):
    ns = ada_w.shape[2]
    cw = conv_w.shape[1]

    def body(c_ref, w_ref, b_ref, cv_ref, *rest):
        (mod_ref, sc_ref, cvo_ref, c_all, mp, parts, cv_parts,
         send1, recv1, send2, recv2, send3, recv3) = rest[len(after):]
        x, y, c = _position()
        me = 4 * x + 2 * y + c
        chip = 2 * x + y

        def c_copy(k):
            return pltpu.make_async_remote_copy(
                src_ref=c_all.at[me], dst_ref=c_all.at[me], send_sem=send1.at[k - 1], recv_sem=recv1.at[k - 1],
                device_id=_xor_peer(x, y, c, k), device_id_type=MESH)

        def cv_copy(k):
            px, py = _chip_peer(x, y, k)
            return pltpu.make_async_remote_copy(
                src_ref=cv_parts.at[chip], dst_ref=cv_parts.at[chip], send_sem=send3.at[k - 1],
                recv_sem=recv3.at[k - 1], device_id=(px, py, c), device_id_type=MESH)

        c_all[me] = c_ref[...]
        cv_parts[chip] = cv_ref[...]
        for k in range(1, N_DEV):
            c_copy(k).start()
        for k in range(1, N_CHIPS):
            cv_copy(k).start()
        for k in range(1, N_DEV):
            c_copy(k).wait_recv()
        cv = jnp.concatenate([c_all[i] for i in range(N_DEV)], axis=0)
        sc = cv * _sigmoid(cv)
        sc_ref[...] = sc
        for l in range(2):
            res = jnp.dot(sc, w_ref[l], preferred_element_type=F32, precision=lax.Precision.HIGHEST)
            for i in range(N_DEV):
                mp[i, l:l + 1, :] = res[i:i + 1, :]

        def mod_copy(k):
            px, py = _chip_peer(x, y, k)
            return pltpu.make_async_remote_copy(
                src_ref=mp.at[4 * px + 2 * py + c], dst_ref=parts.at[chip], send_sem=send2.at[k - 1],
                recv_sem=recv2.at[k - 1], device_id=(px, py, c), device_id_type=MESH)

        for k in range(1, N_CHIPS):
            mod_copy(k).start()
        parts[chip] = mp[me]
        for k in range(1, N_CHIPS):
            mod_copy(k).wait_recv()
            cv_copy(k).wait_recv()
        mod_ref[...] = jnp.concatenate([parts[j] for j in range(N_CHIPS)], axis=1) + b_ref[...]
        cvo_ref[...] = jnp.concatenate([cv_parts[j] for j in range(N_CHIPS)], axis=1)
        for k in range(1, N_DEV):
            c_copy(k).wait_send()
        for k in range(1, N_CHIPS):
            mod_copy(k).wait_send()
            cv_copy(k).wait_send()

    vm = pl.BlockSpec(memory_space=pltpu.VMEM)
    return pl.pallas_call(
        body, name="ada_forward",
        in_specs=[vm] * 4 + [pl.BlockSpec(memory_space=pl.ANY)] * len(after), out_specs=[vm] * 3,
        out_shape=[jax.ShapeDtypeStruct((2, 3 * D_MODEL), F32), jax.ShapeDtypeStruct((N_DEV, D_MODEL), F32),
                   jax.ShapeDtypeStruct((CONV_WIDTH, N_CHIPS * cw), F32)],
        scratch_shapes=[pltpu.VMEM((N_DEV, 1, D_MODEL), F32), pltpu.VMEM((N_DEV, 2, ns), F32),
                        pltpu.VMEM((N_CHIPS, 2, ns), F32), pltpu.VMEM((N_CHIPS, CONV_WIDTH, cw), F32),
                        pltpu.SemaphoreType.DMA((N_DEV - 1,)), pltpu.SemaphoreType.DMA((N_DEV - 1,)),
                        pltpu.SemaphoreType.DMA((N_CHIPS - 1,)), pltpu.SemaphoreType.DMA((N_CHIPS - 1,)),
                        pltpu.SemaphoreType.DMA((N_CHIPS - 1,)), pltpu.SemaphoreType.DMA((N_CHIPS - 1,))],
        compiler_params=pltpu.CompilerParams(vmem_limit_bytes=VMEM_LIMIT_BYTES),
    )(c_row, ada_w, ada_b, conv_w, *after)


HBM_SPEC = pl.BlockSpec(memory_space=pltpu.HBM)
ANY_SPEC = pl.BlockSpec(memory_space=pl.ANY)
SEM_SPEC = pl.BlockSpec(memory_space=pltpu.SEMAPHORE)
SPLIT_PARAMS = dict(compiler_params=pltpu.CompilerParams(has_side_effects=pltpu.SideEffectType.DATAFLOW_SIDE_EFFECTING))
TOKEN = jax.ShapeDtypeStruct((8, 128), F32)


def _hbm(arrays):
    return [pltpu.with_memory_space_constraint(a, pltpu.HBM) for a in arrays]


def _hbm_like(arrays):
    return [pltpu.HBM(a.shape, a.dtype) for a in arrays]


def _gather_start(lands, after, name):
    n = len(lands)

    def body(*refs):
        ins = refs[:n]
        send, recv = refs[n + 1], refs[n + 2]
        x, y, c = _position()
        chip = 2 * x + y
        for t in range(n):
            rh = ins[t].shape[1] // 2
            for k in range(1, N_CHIPS):
                px, py = _chip_peer(x, y, k)
                block = ins[t].at[chip, pl.ds(c * rh, rh)]
                pltpu.make_async_remote_copy(
                    src_ref=block, dst_ref=block, send_sem=send.at[3 * t + k - 1], recv_sem=recv.at[3 * t + k - 1],
                    device_id=(px, py, c), device_id_type=MESH).start()
        refs[-1][...] = jnp.zeros(TOKEN.shape, F32)

    res = pl.pallas_call(
        body, name=name, in_specs=[HBM_SPEC] * n + [ANY_SPEC],
        out_specs=(SEM_SPEC, SEM_SPEC, *[HBM_SPEC] * n, pl.BlockSpec(memory_space=pltpu.VMEM)),
        out_shape=(pltpu.SemaphoreType.DMA((3 * n,)), pltpu.SemaphoreType.DMA((3 * n,)), *_hbm_like(lands), TOKEN),
        input_output_aliases={t: 2 + t for t in range(n)}, **SPLIT_PARAMS,
    )(*_hbm(lands), after)
    return res[0], res[1], list(res[2:2 + n]), res[-1]


def _gather_forward(send, recv, lands, after, name):
    n = len(lands)

    def body(*refs):
        ins = refs[:n]
        send1, recv1 = refs[n], refs[n + 1]
        send2, recv2 = refs[n + 3], refs[n + 4]
        x, y, c = _position()
        chip = 2 * x + y
        for t in range(n):
            rh = ins[t].shape[1] // 2
            half = pl.ds(c * rh, rh)
            for k in range(1, N_CHIPS):
                px, py = _chip_peer(x, y, k)
                s = 3 * t + k - 1
                got = ins[t].at[2 * px + py, half]
                cp = pltpu.make_async_remote_copy(
                    src_ref=ins[t].at[chip, half], dst_ref=got, send_sem=send1.at[s], recv_sem=recv1.at[s],
                    device_id=(px, py, c), device_id_type=MESH)
                cp.wait_send()
                cp.wait_recv()
                pltpu.make_async_remote_copy(
                    src_ref=got, dst_ref=got, send_sem=send2.at[s], recv_sem=recv2.at[s],
                    device_id=(x, y, 1 - c), device_id_type=MESH).start()
        refs[-1][...] = jnp.zeros(TOKEN.shape, F32)

    res = pl.pallas_call(
        body, name=name, in_specs=[HBM_SPEC] * n + [SEM_SPEC, SEM_SPEC, ANY_SPEC],
        out_specs=(SEM_SPEC, SEM_SPEC, *[HBM_SPEC] * n, pl.BlockSpec(memory_space=pltpu.VMEM)),
        out_shape=(pltpu.SemaphoreType.DMA((3 * n,)), pltpu.SemaphoreType.DMA((3 * n,)), *_hbm_like(lands), TOKEN),
        input_output_aliases={t: 2 + t for t in range(n)}, **SPLIT_PARAMS,
    )(*lands, send, recv, after)
    return res[0], res[1], list(res[2:2 + n]), res[-1]


def _gather_wait(send, recv, lands, after, name):
    n = len(lands)

    def body(*refs):
        ins = refs[:n]
        send_ref, recv_ref = refs[n], refs[n + 1]
        x, y, c = _position()
        for t in range(n):
            rh = ins[t].shape[1] // 2
            for k in range(1, N_CHIPS):
                px, py = _chip_peer(x, y, k)
                cp = pltpu.make_async_remote_copy(
                    src_ref=ins[t].at[2 * px + py, pl.ds(c * rh, rh)],
                    dst_ref=ins[t].at[2 * px + py, pl.ds((1 - c) * rh, rh)], send_sem=send_ref.at[3 * t + k - 1],
                    recv_sem=recv_ref.at[3 * t + k - 1], device_id=(x, y, 1 - c), device_id_type=MESH)
                cp.wait_send()
                cp.wait_recv()

    res = pl.pallas_call(
        body, name=name, in_specs=[HBM_SPEC] * n + [SEM_SPEC, SEM_SPEC, ANY_SPEC], out_specs=[HBM_SPEC] * n,
        out_shape=_hbm_like(lands), input_output_aliases={t: t for t in range(n)}, **SPLIT_PARAMS,
    )(*lands, send, recv, after)
    return list(res)


def _split_start(name, arrays, n_sems, after, issue):
    m = len(arrays)

    def body(*refs):
        issue(refs[:m], refs[m + 1], refs[m + 2])
        refs[-1][...] = jnp.zeros(TOKEN.shape, F32)

    res = pl.pallas_call(
        body, name=name, in_specs=[HBM_SPEC] * m + [ANY_SPEC],
        out_specs=(SEM_SPEC, SEM_SPEC, *[HBM_SPEC] * m, pl.BlockSpec(memory_space=pltpu.VMEM)),
        out_shape=(pltpu.SemaphoreType.DMA((n_sems,)), pltpu.SemaphoreType.DMA((n_sems,)), *_hbm_like(arrays), TOKEN),
        input_output_aliases={t: 2 + t for t in range(m)}, **SPLIT_PARAMS,
    )(*_hbm(arrays), after)
    return res[0], res[1], list(res[2:2 + m]), res[-1]


def _split_wait(name, arrays, send, recv, after, await_all):
    m = len(arrays)

    def body(*refs):
        await_all(refs[:m], refs[m], refs[m + 1])

    res = pl.pallas_call(
        body, name=name, in_specs=[HBM_SPEC] * m + [SEM_SPEC, SEM_SPEC, ANY_SPEC], out_specs=[HBM_SPEC] * m,
        out_shape=_hbm_like(arrays), input_output_aliases={t: t for t in range(m)}, **SPLIT_PARAMS,
    )(*arrays, send, recv, after)
    return list(res)


def _sibling_copies(refs, send, recv, n):
    x, y, c = _position()
    cps = []
    for t in range(n):
        rh = refs[t].shape[1] // 2
        cps.append(pltpu.make_async_remote_copy(
            src_ref=refs[t].at[pl.ds(0, N_CHIPS), pl.ds((1 - c) * rh, rh)], dst_ref=refs[n + t],
            send_sem=send.at[t], recv_sem=recv.at[t], device_id=(x, y, 1 - c), device_id_type=MESH))
    return cps


def _reduce_sibling_start(grads, after, name):
    n = len(grads)
    lands = [lax.empty((N_CHIPS, g.shape[1] // 2, g.shape[2]), BF16) for g in grads]

    def issue(refs, send, recv):
        for cp in _sibling_copies(refs, send, recv, n):
            cp.start()

    return _split_start(name, list(grads) + lands, n, after, issue)


def _reduce_sibling_wait(send, recv, arrays, after, name):
    n = len(arrays) // 2

    def await_all(refs, send_ref, recv_ref):
        for cp in _sibling_copies(refs, send_ref, recv_ref, n):
            cp.wait_send()
            cp.wait_recv()

    res = _split_wait(name, arrays, send, recv, after, await_all)
    return res[:n], res[n:]


def _add_sibling_half(grad, got, dev_idx, name):
    j, r, cols = grad.shape
    rh = r // 2
    tr = rh
    nb = rh // tr

    def body(idx_ref, g_ref, got_ref, out_ref):
        out_ref[...] = (g_ref[...].astype(F32) + got_ref[...].astype(F32)).astype(BF16)

    return pl.pallas_call(
        body, name=name,
        grid_spec=pltpu.PrefetchScalarGridSpec(
            num_scalar_prefetch=1, grid=(j, nb),
            in_specs=[pl.BlockSpec((None, tr, cols), lambda jj, i, idx: (jj, idx[2] * nb + i, 0)),
                      pl.BlockSpec((None, tr, cols), lambda jj, i, idx: (jj, i, 0))],
            out_specs=pl.BlockSpec((None, tr, cols), lambda jj, i, idx: (jj, i, 0))),
        out_shape=jax.ShapeDtypeStruct((j, rh, cols), BF16),
        compiler_params=_params("parallel", "parallel"),
    )(dev_idx, grad, got)


def _chip_copies(refs, send, recv, n, receiving):
    x, y, c = _position()
    chip = 2 * x + y
    cps = []
    for t in range(n):
        for k in range(1, N_CHIPS):
            px, py = _chip_peer(x, y, k)
            cps.append(pltpu.make_async_remote_copy(
                src_ref=refs[t].at[2 * px + py], dst_ref=refs[n + t].at[2 * px + py if receiving else chip],
                send_sem=send.at[3 * t + k - 1], recv_sem=recv.at[3 * t + k - 1],
                device_id=(px, py, c), device_id_type=MESH))
    return cps


def _reduce_chips_start(partials, after, name):
    n = len(partials)
    lands = [lax.empty(p.shape, BF16) for p in partials]

    def issue(refs, send, recv):
        for cp in _chip_copies(refs, send, recv, n, False):
            cp.start()

    return _split_start(name, list(partials) + lands, 3 * n, after, issue)


def _reduce_chips_wait(send, recv, arrays, after, name):
    n = len(arrays) // 2

    def await_all(refs, send_ref, recv_ref):
        for cp in _chip_copies(refs, send_ref, recv_ref, n, True):
            cp.wait_send()
            cp.wait_recv()

    res = _split_wait(name, arrays, send, recv, after, await_all)
    return res[:n], res[n:]


def _sum_partials(land, partial, dev_idx, name):
    _, rh, cols = land.shape
    tr = min(rh, 256)
    nb = rh // tr

    def body(idx_ref, l_ref, p_ref, o_ref):
        chip = idx_ref[1]
        acc = jnp.where(chip == 0, p_ref[...], l_ref[0]).astype(F32)
        for s in range(1, N_CHIPS):
            acc = acc + jnp.where(chip == s, p_ref[...], l_ref[s]).astype(F32)
        o_ref[...] = acc

    return pl.pallas_call(
        body, name=name,
        grid_spec=pltpu.PrefetchScalarGridSpec(
            num_scalar_prefetch=1, grid=(nb,),
            in_specs=[pl.BlockSpec((N_CHIPS, tr, cols), lambda i, idx: (0, i, 0)),
                      pl.BlockSpec((None, tr, cols), lambda i, idx: (idx[1], i, 0))],
            out_specs=pl.BlockSpec((tr, cols), lambda i, idx: (idx[2] * nb + i, 0))),
        out_shape=jax.ShapeDtypeStruct((2 * rh, cols), F32), compiler_params=_params("parallel"),
    )(dev_idx, land, partial)


def _half_copies(refs, send, recv, receiving):
    x, y, c = _position()
    cps = []
    for t, ref in enumerate(refs):
        rh = ref.shape[0] // 2
        cps.append(pltpu.make_async_remote_copy(
            src_ref=ref.at[pl.ds(c * rh, rh)], dst_ref=ref.at[pl.ds(((1 - c) if receiving else c) * rh, rh)],
            send_sem=send.at[t], recv_sem=recv.at[t], device_id=(x, y, 1 - c), device_id_type=MESH))
    return cps


def _share_halves_start(totals, after, name):
    def issue(refs, send, recv):
        for cp in _half_copies(refs, send, recv, False):
            cp.start()

    return _split_start(name, list(totals), len(totals), after, issue)


def _share_halves_wait(send, recv, totals, after, name):
    def await_all(refs, send_ref, recv_ref):
        for cp in _half_copies(refs, send_ref, recv_ref, True):
            cp.wait_send()
            cp.wait_recv()

    return _split_wait(name, totals, send, recv, after, await_all)


SMALL_ROWS = 56


def _small_copies(refs, send, recv, receiving):
    x, y, c = _position()
    me = 4 * x + 2 * y + c
    cps = []
    for k in range(1, N_DEV):
        px, py, pc = _xor_peer(x, y, c, k)
        cps.append(pltpu.make_async_remote_copy(
            src_ref=refs[0], dst_ref=refs[1].at[4 * px + 2 * py + pc if receiving else me],
            send_sem=send.at[k - 1], recv_sem=recv.at[k - 1], device_id=(px, py, pc), device_id_type=MESH))
    return cps


def _small_gather_start(packed, after):
    land = lax.empty((N_DEV,) + packed.shape, F32)

    def issue(refs, send, recv):
        for cp in _small_copies(refs, send, recv, False):
            cp.start()

    return _split_start("small_gather_start", [packed, land], N_DEV - 1, after, issue)


def _small_gather_wait(send, recv, arrays, after):
    def await_all(refs, send_ref, recv_ref):
        for cp in _small_copies(refs, send_ref, recv_ref, True):
            cp.wait_send()
            cp.wait_recv()

    return _split_wait("small_gather_wait", arrays, send, recv, after, await_all)


def _reduce_small(packed, land, silu_c):
    ns = 3 * D_MODEL // N_CHIPS

    def body(p_ref, land_ref, sc_ref, tot_ref, gw_ref, loss_ref, qk_ref, allp):
        x, y, c = _position()
        me = 4 * x + 2 * y + c
        chip = 2 * x + y
        for i in range(N_DEV):
            allp[i] = jnp.where(me == i, p_ref[...], land_ref[i])
        tot = allp[0]
        for i in range(1, N_DEV):
            tot = tot + allp[i]
        tot_ref[...] = tot
        loss_ref[...] = jnp.sum(tot[11:12, :], axis=1, keepdims=True) * (0.5 / D_MODEL)
        fold = tot[5:11, 0:HEAD_DIM]
        for h in range(1, N_HEADS):
            fold = fold + tot[5:11, h * HEAD_DIM:(h + 1) * HEAD_DIM]
        qk_ref[...] = jnp.concatenate([fold, jnp.zeros((2, HEAD_DIM), F32)], axis=0)
        sct = sc_ref[...].T
        rc = 64
        for l in range(2):
            dms = [allp[i, pl.ds(12 + 4 * l + chip, 1), :][:, :ns] for i in range(N_DEV)]
            for r0 in range(0, D_MODEL, rc):
                acc = sct[r0:r0 + rc, 0:1] * dms[0]
                for i in range(1, N_DEV):
                    acc = acc + sct[r0:r0 + rc, i:i + 1] * dms[i]
                gw_ref[l, r0:r0 + rc, :] = acc

    vm = pl.BlockSpec(memory_space=pltpu.VMEM)
    return pl.pallas_call(
        body, name="reduce_small", in_specs=[vm, vm, vm], out_specs=[vm] * 4,
        out_shape=[jax.ShapeDtypeStruct((SMALL_ROWS, D_MODEL), F32), jax.ShapeDtypeStruct((2, D_MODEL, ns), F32),
                   jax.ShapeDtypeStruct((1, 1), F32), jax.ShapeDtypeStruct((8, HEAD_DIM), F32)],
        scratch_shapes=[pltpu.VMEM((N_DEV, SMALL_ROWS, D_MODEL), F32)],
        compiler_params=pltpu.CompilerParams(vmem_limit_bytes=VMEM_LIMIT_BYTES),
    )(packed, land, silu_c)


def kernel(x, c, norm_g, ada_w, ada_b, a_w_in, a_conv_w, a_conv_b, a_ln_g, a_ln_b, a_w_out, b_w_in, b_q_norm, b_k_norm, b_w_out, loss_target, m_norm_g, m_ada_w, m_ada_b, m_a_w_in, m_a_conv_w, m_a_conv_b, m_a_ln_g, m_a_ln_b, m_a_w_out, m_b_w_in, m_b_q_norm, m_b_k_norm, m_b_w_out, v_norm_g, v_ada_w, v_ada_b, v_a_w_in, v_a_conv_w, v_a_conv_b, v_a_ln_g, v_a_ln_b, v_a_w_out, v_b_w_in, v_b_q_norm, v_b_k_norm, v_b_w_out):
    chip = 2 * lax.axis_index("x") + lax.axis_index("y")
    core = lax.axis_index("c")
    chip_idx = chip.astype(jnp.int32).reshape(1)
    dev_idx = jnp.stack([2 * chip + core, chip, core]).astype(jnp.int32)

    land_a_in, own_wa_in = _cast_into_slot(a_w_in[0], chip_idx, "cast_a_w_in", keep_own=True)
    lands_a = [land_a_in, _cast_into_slot(a_w_out[0], chip_idx, "cast_a_w_out")]
    mods, silu_c, conv_w_full = _ada_forward(c, ada_w, ada_b, a_conv_w[0], after=tuple(lands_a))
    send_a, recv_a, lands_a, token_a = _gather_start(lands_a, mods, "gather_start_a")
    land_b_in, own_wb_in = _cast_into_slot(b_w_in[0], chip_idx, "cast_b_w_in", keep_own=True, after=token_a)
    lands_b = [land_b_in, _cast_into_slot(b_w_out[0], chip_idx, "cast_b_w_out", after=token_a)]
    send_b, recv_b, lands_b, token_b = _gather_start(lands_b, token_a, "gather_start_b")
    mods = mods + token_b[0:2, 0:1]

    def weights_a(after):
        send, recv, lands, _ = _gather_forward(send_a, recv_a, lands_a, after, "gather_forward_a")
        w_in, w_out = _gather_wait(send, recv, lands, after, "gather_wait_a")
        return w_in, w_out.reshape(D_MODEL, D_MODEL)

    forwarded_b = []

    def weights_b(after):
        send, recv, lands, _ = forwarded_b
        w_in, w_out = _gather_wait(send, recv, lands, after, "gather_wait_b")
        return w_in, w_out.reshape(D_MODEL, D_MODEL)

    def forward_weights_b(after):
        forwarded_b.extend(_gather_forward(send_b, recv_b, lands_b, after, "gather_forward_b"))
        return forwarded_b[3]

    stage1, stage2 = {}, {}

    def send_grads(tag, dw_in, dw_out):
        grads = [dw_in, dw_out.reshape(N_CHIPS, D_MODEL // N_CHIPS, D_MODEL)]
        send, recv, arrays, token = _reduce_sibling_start(grads, dw_out, f"reduce_d2d_start_{tag}")
        stage1[tag] = (send, recv, arrays)
        return token

    def forward_grads(tag, after):
        send, recv, arrays = stage1[tag]
        grads, got = _reduce_sibling_wait(send, recv, arrays, after, f"reduce_d2d_wait_{tag}")
        partials = [_add_sibling_half(grads[i], got[i], dev_idx, f"reduce_add_{tag}_{i}") for i in range(2)]
        send, recv, arrays, token = _reduce_chips_start(partials, partials[1], f"reduce_ici_start_{tag}")
        stage2[tag] = (send, recv, arrays)
        return token

    stage3 = {}

    def sum_grads(tag, after):
        send, recv, arrays = stage2[tag]
        partials, lands = _reduce_chips_wait(send, recv, arrays, after, f"reduce_ici_wait_{tag}")
        totals = [_sum_partials(lands[i], partials[i], dev_idx, f"reduce_sum_{tag}_{i}") for i in range(2)]
        send, recv, totals, token = _share_halves_start(totals, totals[1], f"reduce_share_start_{tag}")
        stage3[tag] = (send, recv, totals)
        return token

    def finish_grads(tag, after):
        send, recv, totals = stage3[tag]
        return _share_halves_wait(send, recv, totals, after, f"reduce_share_wait_{tag}")

    grad_x, small = _local_step(
        x[0], loss_target[0], mods.reshape(2, 3, D_MODEL), norm_g, conv_w_full, a_conv_b, a_ln_g[0:1],
        a_ln_b[0:1], b_q_norm[0], b_k_norm[0], chip.astype(jnp.int32), own_wa_in, own_wb_in,
        weights_a, weights_b, forward_weights_b,
        functools.partial(send_grads, "b"), functools.partial(forward_grads, "b"), functools.partial(send_grads, "a"))

    ns = 3 * D_MODEL // N_CHIPS
    pad_mod = lambda dm: jnp.pad(dm.reshape(N_CHIPS, ns), ((0, 0), (0, D_MODEL - ns)))
    packed = jnp.concatenate([
        small["dnorm_g"], small["dconv_b"], small["dln_g"], small["dln_b"], small["dq_norm"], small["dk_norm"],
        small["loss_cols"], pad_mod(small["dmod0"]), pad_mod(small["dmod1"]), small["dconv_w"],
        jnp.zeros((SMALL_ROWS - 20 - CONV_WIDTH, D_MODEL), F32)], axis=0)
    send_s, recv_s, small_arrays, token_s = _small_gather_start(packed, packed)

    given = dict(norm_g=(norm_g, m_norm_g, v_norm_g), ada_w=(ada_w, m_ada_w, v_ada_w), ada_b=(ada_b, m_ada_b, v_ada_b),
                 a_w_in=(a_w_in, m_a_w_in, v_a_w_in), a_conv_w=(a_conv_w, m_a_conv_w, v_a_conv_w),
                 a_conv_b=(a_conv_b, m_a_conv_b, v_a_conv_b), a_ln_g=(a_ln_g, m_a_ln_g, v_a_ln_g),
                 a_ln_b=(a_ln_b, m_a_ln_b, v_a_ln_b), a_w_out=(a_w_out, m_a_w_out, v_a_w_out),
                 b_w_in=(b_w_in, m_b_w_in, v_b_w_in), b_q_norm=(b_q_norm, m_b_q_norm, v_b_q_norm),
                 b_k_norm=(b_k_norm, m_b_k_norm, v_b_k_norm), b_w_out=(b_w_out, m_b_w_out, v_b_w_out))
    order = ["norm_g", "ada_w", "ada_b", "a_w_in", "a_conv_w", "a_conv_b", "a_ln_g", "a_ln_b", "a_w_out", "b_w_in",
             "b_q_norm", "b_k_norm", "b_w_out"]
    outs = {}

    def update(k, g2, after=None, copy_grad=False):
        w, m, v = given[k]
        shape2 = g2.shape
        res = _adamw(w.reshape(shape2), g2, m.reshape(shape2), v.reshape(shape2), f"adamw_{k}", after, copy_grad)
        outs[k] = tuple(a.reshape(w.shape) for a in ((res[3] if copy_grad else g2), res[0], res[1], res[2]))

    token = forward_grads("a", token_s)
    token = sum_grads("b", token)
    packed, land = _small_gather_wait(send_s, recv_s, small_arrays, token)
    tot, g_ada_w, loss, qk = _reduce_small(packed, land, silu_c)
    g_b_in, g_b_out = finish_grads("b", tot)
    update("b_w_in", g_b_in, copy_grad=True)
    update("b_w_out", g_b_out, copy_grad=True)
    token = sum_grads("a", outs["b_w_in"][1])
    cw = D_MODEL // N_CHIPS
    g_small = dict(
        norm_g=tot[0:2], a_conv_b=tot[2:3], a_ln_g=tot[3:4], a_ln_b=tot[4:5],
        b_q_norm=qk[0:3], b_k_norm=qk[3:6],
        ada_b=jnp.stack([tot[12:16, :ns].reshape(3 * D_MODEL), tot[16:20, :ns].reshape(3 * D_MODEL)]),
        a_conv_w=lax.dynamic_slice(tot[20:20 + CONV_WIDTH], (0, chip * cw), (CONV_WIDTH, cw)),
    )
    update("ada_w", g_ada_w.reshape(2 * D_MODEL, ns), after=token)
    for k, g2 in g_small.items():
        update(k, g2, after=token)
    g_a_in, g_a_out = finish_grads("a", outs["ada_w"][1])
    update("a_w_in", g_a_in, copy_grad=True)
    update("a_w_out", g_a_out, copy_grad=True)
    return (loss.reshape(()), grad_x[None], *[outs[k][0] for k in order], *[outs[k][1] for k in order],
            *[outs[k][2] for k in order], *[outs[k][3] for k in order])
```

```python
import functools

import jax
import jax.numpy as jnp
from jax import lax
from jax.experimental import pallas as pl
from jax.experimental.pallas import tpu as pltpu

F32 = jnp.float32
BF16 = jnp.bfloat16

SEQ = 2048
D_MODEL = 1024
CONV_WIDTH = 31
HEAD_DIM = 64
N_HEADS = 16
DILATIONS = (1, 4, 16)
ATTN_BLOCK = 128
NORM_EPS = 1e-6
NEG_INF = -1e30
N_DEV = 8
N_CHIPS = 4

ADAM_LR = 0.001
ADAM_B1 = 0.9
ADAM_B2 = 0.999
ADAM_EPS = 1e-08
ADAM_WD = 0.01
ADAM_STEP = 10

VMEM_LIMIT_BYTES = 52 * 1024 * 1024
HALO = 32
LANES = 128
ROW_TILE = 512
MESH = pl.DeviceIdType.MESH


def _params(*sem):
    return pltpu.CompilerParams(dimension_semantics=sem or None, vmem_limit_bytes=VMEM_LIMIT_BYTES)


def _sigmoid(v):
    return 1.0 / (1.0 + jnp.exp(-v))


def _row_spec(tm, cols, col_block=0):
    return pl.BlockSpec((tm, cols), lambda i: (i, col_block))


def _vec_spec(rows, cols):
    return pl.BlockSpec((rows, cols), lambda i: (0, 0))


def _normmod(xv, g, scale, shift):
    r = lax.rsqrt(jnp.mean(xv * xv, axis=-1, keepdims=True) + NORM_EPS)
    return xv * r * g * (1.0 + scale) + shift


def _normmod_fwd(x, g, scale, shift, name):
    tm = ROW_TILE

    def body(x_ref, g_ref, sc_ref, sh_ref, h_ref, ht_ref):
        h = _normmod(x_ref[...], g_ref[...], sc_ref[...], sh_ref[...])
        h_ref[...] = h.astype(BF16)
        ht_ref[...] = h.T.astype(BF16)

    return pl.pallas_call(
        body, name=name, grid=(SEQ // tm,),
        in_specs=[_row_spec(tm, D_MODEL)] + [_vec_spec(1, D_MODEL)] * 3,
        out_specs=[_row_spec(tm, D_MODEL), pl.BlockSpec((D_MODEL, tm), lambda i: (0, i))],
        out_shape=[jax.ShapeDtypeStruct((SEQ, D_MODEL), BF16), jax.ShapeDtypeStruct((D_MODEL, SEQ), BF16)],
        compiler_params=_params("parallel"),
    )(x, g, scale, shift)


def _normmod_bwd(x, g, scale, dh_parts, dres, name, part_dilations=None, gated=None):
    tm = ROW_TILE
    n_parts = len(dh_parts)
    dils = part_dilations or (1,) * n_parts
    dh_parts = [p if d == 1 else p.reshape(d, SEQ // d, D_MODEL) for p, d in zip(dh_parts, dils)]
    n_gated = 0 if gated is None else 2

    def body(x_ref, g_ref, sc_ref, dres_ref, *rest):
        part_refs = rest[:n_parts]
        gated_refs = rest[n_parts:n_parts + n_gated]
        out_refs = rest[n_parts + n_gated:]
        dx_ref, sums_ref, nat = out_refs[0], out_refs[1], out_refs[-1]
        xv = x_ref[...]
        r = lax.rsqrt(jnp.mean(xv * xv, axis=-1, keepdims=True) + NORM_EPS)
        xn = xv * r
        dh = _load_natural(part_refs[0], nat, dils[0])
        for p, d in zip(part_refs[1:], dils[1:]):
            dh = dh + _load_natural(p, nat, d)
        gv = g_ref[...]
        one_sc = 1.0 + sc_ref[...]
        dxn = dh * (gv * one_sc)
        dx = dres_ref[...] + r * (dxn - xn * jnp.mean(dxn * xn, axis=-1, keepdims=True))
        dx_ref[...] = dx
        dhx = dh * xn
        rows = [jnp.sum(dhx, axis=0, keepdims=True) * one_sc,
                jnp.sum(dhx, axis=0, keepdims=True) * gv,
                jnp.sum(dh, axis=0, keepdims=True)]
        if gated is not None:
            gate_ref, y_ref = gated_refs
            out_refs[2][...] = (dx * gate_ref[...]).astype(BF16)
            rows.append(jnp.sum(dx * y_ref[...].astype(F32), axis=0, keepdims=True))
        sums = jnp.concatenate(rows + [jnp.zeros((8 - len(rows), D_MODEL), F32)], axis=0)

        @pl.when(pl.program_id(0) == 0)
        def _():
            sums_ref[...] = jnp.zeros_like(sums_ref)

        sums_ref[...] += sums

    gated_specs = [] if gated is None else [_vec_spec(1, D_MODEL), _row_spec(tm, D_MODEL)]
    dy_spec = [] if gated is None else [_row_spec(tm, D_MODEL)]
    dy_shape = [] if gated is None else [jax.ShapeDtypeStruct((SEQ, D_MODEL), BF16)]
    return pl.pallas_call(
        body, name=name, grid=(SEQ // tm,),
        in_specs=[_row_spec(tm, D_MODEL), _vec_spec(1, D_MODEL), _vec_spec(1, D_MODEL), _row_spec(tm, D_MODEL)]
        + [_class_spec(tm, d) for d in dils] + gated_specs,
        out_specs=[_row_spec(tm, D_MODEL), _vec_spec(8, D_MODEL)] + dy_spec,
        out_shape=[jax.ShapeDtypeStruct((SEQ, D_MODEL), F32), jax.ShapeDtypeStruct((8, D_MODEL), F32)] + dy_shape,
        scratch_shapes=[_natural_scratch(tm)],
        compiler_params=_params("arbitrary"),
    )(x, g, scale, dres, *dh_parts, *(gated or ()))


def _mm(lhs, rhs, *, tn, tile0, n_tiles, out_dtype, name, out3d=None, prev=None, transpose_lhs=False):
    mo, kc = lhs.shape[::-1] if transpose_lhs else lhs.shape
    cm = min(mo, 1024)
    tc = 256

    def body(l_ref, r_ref, *rest):
        if transpose_lhs:
            o_ref, lt_ref = rest[-2], rest[-1]

            @pl.when(pl.program_id(0) == 0)
            def _():
                for c in range(kc // tc):
                    lt_ref[:, c * tc:(c + 1) * tc] = l_ref[c * tc:(c + 1) * tc, :].astype(F32).T.astype(l_ref.dtype)
        else:
            o_ref, lt_ref = rest[-1], l_ref
        for m in range(mo // cm):
            rows = pl.ds(m * cm, cm)
            o_ref[rows, :] = jnp.dot(lt_ref[rows, :], r_ref[...], preferred_element_type=F32).astype(out_dtype)

    if rhs.ndim == 3:
        tps_r = rhs.shape[2] // tn
        r_spec = pl.BlockSpec((None, kc, tn), lambda t: ((tile0 + t) // tps_r, 0, (tile0 + t) % tps_r))
    else:
        r_spec = pl.BlockSpec((kc, tn), lambda t: (0, t))
    in_specs = [pl.BlockSpec(lhs.shape, lambda t: (0, 0)), r_spec]
    args = [lhs, rhs]
    aliases = {}
    if out3d is None:
        o_spec = pl.BlockSpec((mo, tn), lambda t: (0, t))
        o_shape = jax.ShapeDtypeStruct((mo, n_tiles * tn), out_dtype)
    else:
        j_out, ns_out = out3d
        tps_o = ns_out // tn
        o_spec = pl.BlockSpec((None, mo, tn), lambda t: ((tile0 + t) // tps_o, 0, (tile0 + t) % tps_o))
        o_shape = jax.ShapeDtypeStruct((j_out, mo, ns_out), out_dtype)
        if prev is not None:
            in_specs.append(pl.BlockSpec(memory_space=pl.ANY))
            args.append(prev)
            aliases = {2: 0}
    return pl.pallas_call(
        body, name=name, grid=(n_tiles,), in_specs=in_specs, out_specs=o_spec, out_shape=o_shape,
        input_output_aliases=aliases,
        scratch_shapes=[pltpu.VMEM((mo, kc), lhs.dtype)] if transpose_lhs else [],
        compiler_params=_params("arbitrary" if transpose_lhs else "parallel"),
    )(*args)


def _in_tiles(h_parts, w3, tile_ids, n_tiles, *, tn, total_tiles, part_of, name, prev=None):
    _, kc, ns = w3.shape
    tps = ns // tn
    cm = 1024
    n_parts = len(h_parts)

    def body(ids_ref, *rest):
        h_refs, w_ref, o_ref = rest[:n_parts], rest[n_parts], rest[-1]
        part = part_of(ids_ref[1, pl.program_id(0)])
        for g, h_ref in enumerate(h_refs):
            @pl.when(part == g)
            def _():
                for m in range(SEQ // cm):
                    rows = pl.ds(m * cm, cm)
                    o_ref[rows, :] = jnp.dot(h_ref[rows, :], w_ref[...], preferred_element_type=F32).astype(BF16)

    resident = pl.BlockSpec((SEQ, kc), lambda t, ids: (0, 0))
    in_specs = [resident] * n_parts + [
        pl.BlockSpec((None, kc, tn), lambda t, ids: (ids[0, t] // tps, 0, ids[0, t] % tps))]
    args = [*h_parts, w3]
    aliases = {}
    if prev is not None:
        in_specs.append(pl.BlockSpec(memory_space=pl.ANY))
        args.append(prev)
        aliases = {n_parts + 2: 0}
    return pl.pallas_call(
        body, name=name,
        grid_spec=pltpu.PrefetchScalarGridSpec(
            num_scalar_prefetch=1, grid=(n_tiles,), in_specs=in_specs,
            out_specs=pl.BlockSpec((SEQ, tn), lambda t, ids: (0, ids[1, t]))),
        out_shape=jax.ShapeDtypeStruct((SEQ, total_tiles * tn), BF16),
        input_output_aliases=aliases, compiler_params=_params("arbitrary"),
    )(tile_ids, *args)


def _own_first(chip, total_tiles):
    own = total_tiles // N_CHIPS
    step = jnp.arange(total_tiles, dtype=jnp.int32)
    tiles = (own * chip + step) % total_tiles
    return jnp.stack([step[:own], tiles[:own]]), jnp.stack([tiles[own:], tiles[own:]]), own


def _mm_nt(dy, w3, *, tn, tile0, n_tiles, name, after=None):
    m_rows = dy.shape[0]
    _, kc, ns = w3.shape
    tps = ns // tn
    cm = 512
    extra = [] if after is None else [after]

    def body(dy_ref, w_ref, *rest):
        o_ref, acc = rest[-2], rest[-1]
        t = pl.program_id(0)

        @pl.when(t == 0)
        def _():
            acc[...] = jnp.zeros_like(acc)

        for m in range(m_rows // cm):
            rows = pl.ds(m * cm, cm)
            acc[rows, :] += lax.dot_general(dy_ref[rows, :], w_ref[...], NT_DIMS, preferred_element_type=F32)

        @pl.when(t == n_tiles - 1)
        def _():
            o_ref[...] = acc[...].astype(BF16)

    return pl.pallas_call(
        body, name=name, grid=(n_tiles,),
        in_specs=[pl.BlockSpec((m_rows, tn), lambda t: (0, t)),
                  pl.BlockSpec((None, kc, tn), lambda t: ((tile0 + t) // tps, 0, (tile0 + t) % tps))]
        + [pl.BlockSpec(memory_space=pl.ANY)] * len(extra),
        out_specs=pl.BlockSpec((m_rows, kc), lambda t: (0, 0)),
        out_shape=jax.ShapeDtypeStruct((m_rows, kc), BF16),
        scratch_shapes=[pltpu.VMEM((m_rows, kc), F32)],
        compiler_params=_params("arbitrary"),
    )(dy, w3, *extra)


CONV_CHUNK = 16


def _shift_copies(buf, shifted):
    rows = shifted.shape[1]
    for s in range(1, 8):
        shifted[s - 1] = buf[pl.ds(s, rows), :]


def _shifted_rows(buf, shifted, offset, r0):
    s = offset % 8
    if s == 0:
        return buf[pl.ds(r0 + offset, CONV_CHUNK), :]
    return shifted[s - 1, pl.ds(r0 + (offset - s), CONV_CHUNK), :]


def _spread_taps(w_ref, taps):
    for k in range(CONV_WIDTH):
        taps[k] = jnp.broadcast_to(w_ref[k:k + 1, :], (8, D_MODEL))


def _times_tap(taps, k, rows):
    return (rows.reshape(CONV_CHUNK // 8, 8, D_MODEL) * taps[k][None]).reshape(CONV_CHUNK, D_MODEL)


def _conv_fwd(proj, conv_w, conv_b, ln_g, ln_b, name):
    tm = 256
    hb = tm // HALO

    def body(vg_ref, halo_ref, z_ref, w_ref, b_ref, g_ref, be_ref, u5_ref, u5t_ref, u2_ref, buf, shifted, taps):
        i = pl.program_id(0)
        u1 = vg_ref[:, :D_MODEL].astype(F32) * _sigmoid(vg_ref[:, D_MODEL:].astype(F32))
        u1h = halo_ref[:, :D_MODEL].astype(F32) * _sigmoid(halo_ref[:, D_MODEL:].astype(F32))
        buf[pl.ds(0, HALO), :] = jnp.where(i > 0, u1h, 0.0)
        buf[pl.ds(HALO, tm), :] = u1
        _shift_copies(buf, shifted)
        _spread_taps(w_ref, taps)

        def chunk(ci, carry):
            r0 = pl.multiple_of(ci * CONV_CHUNK, CONV_CHUNK)
            acc = jnp.broadcast_to(b_ref[...], (CONV_CHUNK, D_MODEL))
            for k in range(CONV_WIDTH):
                acc = acc + _times_tap(taps, k, _shifted_rows(buf, shifted, HALO - (CONV_WIDTH - 1) + k, r0))
            u2_ref[pl.ds(r0, CONV_CHUNK), :] = acc
            return carry

        lax.fori_loop(0, tm // CONV_CHUNK, chunk, 0)
        acc = u2_ref[...]
        mu = jnp.mean(acc, axis=-1, keepdims=True)
        xc = acc - mu
        rstd = lax.rsqrt(jnp.mean(xc * xc, axis=-1, keepdims=True) + NORM_EPS)
        u3 = xc * rstd * g_ref[...] + be_ref[...]
        zv = z_ref[...].astype(F32)
        u5 = u3 * _sigmoid(u3) * (zv * _sigmoid(zv))
        u5_ref[...] = u5.astype(BF16)
        u5t_ref[...] = u5.T.astype(BF16)

    return pl.pallas_call(
        body, name=name, grid=(SEQ // tm,),
        in_specs=[pl.BlockSpec((tm, 2 * D_MODEL), lambda i: (i, 0)),
                  pl.BlockSpec((HALO, 2 * D_MODEL), lambda i: (jnp.maximum(i * hb - 1, 0), 0)),
                  _row_spec(tm, D_MODEL, 2),
                  _vec_spec(CONV_WIDTH, D_MODEL)] + [_vec_spec(1, D_MODEL)] * 3,
        out_specs=[_row_spec(tm, D_MODEL), pl.BlockSpec((D_MODEL, tm), lambda i: (0, i)), _row_spec(tm, D_MODEL)],
        out_shape=[jax.ShapeDtypeStruct((SEQ, D_MODEL), BF16), jax.ShapeDtypeStruct((D_MODEL, SEQ), BF16),
                   jax.ShapeDtypeStruct((SEQ, D_MODEL), F32)],
        scratch_shapes=[pltpu.VMEM((HALO + tm, D_MODEL), F32), pltpu.VMEM((7, HALO + tm - 8, D_MODEL), F32),
                        pltpu.VMEM((CONV_WIDTH, 8, D_MODEL), F32)],
        compiler_params=_params("parallel"),
    )(proj, proj, proj, conv_w, conv_b, ln_g, ln_b)


def _conv_bwd_pointwise(dy, w_out, proj, u2, ln_g, ln_b, name):
    tm = ROW_TILE

    def body(dy_ref, w_ref, z_ref, u2_ref, g_ref, be_ref, du2_ref, dz_ref, sums_ref):
        u2v = u2_ref[...]
        mu = jnp.mean(u2v, axis=-1, keepdims=True)
        xc = u2v - mu
        rstd = lax.rsqrt(jnp.mean(xc * xc, axis=-1, keepdims=True) + NORM_EPS)
        xhat = xc * rstd
        u3 = xhat * g_ref[...] + be_ref[...]
        s3 = _sigmoid(u3)
        u4 = u3 * s3
        zv = z_ref[...].astype(F32)
        sz = _sigmoid(zv)
        du5v = lax.dot_general(dy_ref[...], w_ref[...], NT_DIMS, preferred_element_type=F32)
        dz_ref[...] = du5v * u4 * (sz * (1.0 + zv * (1.0 - sz)))
        du3 = du5v * (zv * sz) * (s3 * (1.0 + u3 * (1.0 - s3)))
        dxhat = du3 * g_ref[...]
        du2 = rstd * (dxhat - jnp.mean(dxhat, axis=-1, keepdims=True)
                      - xhat * jnp.mean(dxhat * xhat, axis=-1, keepdims=True))
        du2_ref[...] = du2
        sums = jnp.concatenate([
            jnp.sum(du3 * xhat, axis=0, keepdims=True),
            jnp.sum(du3, axis=0, keepdims=True),
            jnp.sum(du2, axis=0, keepdims=True),
            jnp.zeros((5, D_MODEL), F32)], axis=0)

        @pl.when(pl.program_id(0) == 0)
        def _():
            sums_ref[...] = jnp.zeros_like(sums_ref)

        sums_ref[...] += sums

    return pl.pallas_call(
        body, name=name, grid=(SEQ // tm,),
        in_specs=[_row_spec(tm, D_MODEL), _vec_spec(D_MODEL, D_MODEL), _row_spec(tm, D_MODEL, 2),
                  _row_spec(tm, D_MODEL), _vec_spec(1, D_MODEL), _vec_spec(1, D_MODEL)],
        out_specs=[_row_spec(tm, D_MODEL), _row_spec(tm, D_MODEL), _vec_spec(8, D_MODEL)],
        out_shape=[jax.ShapeDtypeStruct((SEQ, D_MODEL), F32), jax.ShapeDtypeStruct((SEQ, D_MODEL), F32),
                   jax.ShapeDtypeStruct((8, D_MODEL), F32)],
        compiler_params=_params("arbitrary"),
    )(dy, w_out, proj, u2, ln_g, ln_b)


def _conv_bwd_taps(du2, dz, proj, conv_w, name):
    tm = 256
    hb = tm // HALO
    n_blocks = SEQ // tm

    def body(du2_ref, dnext_ref, dz_ref, vg_ref, w_ref, dproj_ref, dw_ref, dbuf, dshift, sgbuf, ubuf, dwacc, taps):
        i = pl.program_id(0)
        _spread_taps(w_ref, taps)
        sg = _sigmoid(vg_ref[:, D_MODEL:].astype(F32))
        sgbuf[...] = sg
        ubuf[...] = vg_ref[:, :D_MODEL].astype(F32) * sg
        dbuf[pl.ds(0, tm), :] = du2_ref[...]
        dbuf[pl.ds(tm, HALO), :] = jnp.where(i < n_blocks - 1, dnext_ref[...], 0.0)
        _shift_copies(dbuf, dshift)

        @pl.when(i == 0)
        def _():
            dwacc[...] = jnp.zeros_like(dwacc)

        def chunk(ci, carry):
            r0 = pl.multiple_of(ci * CONV_CHUNK, CONV_CHUNK)
            rows = pl.ds(r0, CONV_CHUNK)
            u1c = ubuf[rows, :]
            du1 = jnp.zeros((CONV_CHUNK, D_MODEL), F32)
            for k in range(CONV_WIDTH):
                ahead = _shifted_rows(dbuf, dshift, CONV_WIDTH - 1 - k, r0)
                du1 = du1 + _times_tap(taps, k, ahead)
                prod = u1c * ahead
                dwacc[k] += prod[0:8] + prod[8:16]
            sgc = sgbuf[rows, :]
            dval = du1 * sgc
            dproj_ref[rows, 0:D_MODEL] = dval.astype(BF16)
            dproj_ref[rows, D_MODEL:2 * D_MODEL] = (
                dval * vg_ref[rows, 0:D_MODEL].astype(F32) * (1.0 - sgc)).astype(BF16)
            return carry

        lax.fori_loop(0, tm // CONV_CHUNK, chunk, 0)
        dproj_ref[:, 2 * D_MODEL:] = dz_ref[...].astype(BF16)

        @pl.when(i == n_blocks - 1)
        def _():
            for k in range(CONV_WIDTH):
                dw_ref[k:k + 1, :] = jnp.sum(dwacc[k], axis=0, keepdims=True)
            dw_ref[CONV_WIDTH:, :] = jnp.zeros((32 - CONV_WIDTH, D_MODEL), F32)

    return pl.pallas_call(
        body, name=name, grid=(n_blocks,),
        in_specs=[_row_spec(tm, D_MODEL),
                  pl.BlockSpec((HALO, D_MODEL), lambda i: (jnp.minimum((i + 1) * hb, SEQ // HALO - 1), 0)),
                  _row_spec(tm, D_MODEL),
                  pl.BlockSpec((tm, 2 * D_MODEL), lambda i: (i, 0)),
                  _vec_spec(CONV_WIDTH, D_MODEL)],
        out_specs=[_row_spec(tm, 3 * D_MODEL), _vec_spec(32, D_MODEL)],
        out_shape=[jax.ShapeDtypeStruct((SEQ, 3 * D_MODEL), BF16), jax.ShapeDtypeStruct((32, D_MODEL), F32)],
        scratch_shapes=[pltpu.VMEM((tm + HALO, D_MODEL), F32), pltpu.VMEM((7, HALO + tm - 8, D_MODEL), F32),
                        pltpu.VMEM((tm, D_MODEL), F32), pltpu.VMEM((tm, D_MODEL), F32),
                        pltpu.VMEM((CONV_WIDTH, 8, D_MODEL), F32), pltpu.VMEM((CONV_WIDTH, 8, D_MODEL), F32)],
        compiler_params=_params("arbitrary"),
    )(du2, du2, dz, proj, conv_w)


def _out_a(u5, w_out, x, gate, g1, scale1, shift1, name):
    tm = ROW_TILE
    n_d = len(DILATIONS)

    def body(u_ref, w_ref, x_ref, gate_ref, g_ref, sc_ref, sh_ref, x1_ref, y_ref, ht_ref, *rest):
        h_refs, nat = rest[:n_d], rest[-1]
        y = jnp.dot(u_ref[...], w_ref[...], preferred_element_type=F32)
        x1 = x_ref[...] + gate_ref[...] * y
        y_ref[...] = y.astype(BF16)
        x1_ref[...] = x1
        h = _normmod(x1, g_ref[...], sc_ref[...], sh_ref[...])
        ht_ref[...] = h.T.astype(BF16)
        for h_ref, d in zip(h_refs, DILATIONS):
            _store_classes(h_ref, h, nat, d)

    res = pl.pallas_call(
        body, name=name, grid=(SEQ // tm,),
        in_specs=[_row_spec(tm, D_MODEL), _vec_spec(D_MODEL, D_MODEL), _row_spec(tm, D_MODEL)]
        + [_vec_spec(1, D_MODEL)] * 4,
        out_specs=[_row_spec(tm, D_MODEL), _row_spec(tm, D_MODEL), pl.BlockSpec((D_MODEL, tm), lambda i: (0, i))]
        + [_class_spec(tm, d) for d in DILATIONS],
        out_shape=[jax.ShapeDtypeStruct((SEQ, D_MODEL), F32), jax.ShapeDtypeStruct((SEQ, D_MODEL), BF16),
                   jax.ShapeDtypeStruct((D_MODEL, SEQ), BF16)] + [_class_shape(d, BF16) for d in DILATIONS],
        scratch_shapes=[_natural_scratch(tm)],
        compiler_params=_params("parallel"),
    )(u5, w_out, x, gate, g1, scale1, shift1)
    return res[0], res[1], res[2], [a.reshape(SEQ, D_MODEL) for a in res[3:]]


def _out_b_loss(u, w_out, x1, gate, target, name):
    tm = ROW_TILE

    def body(u_ref, w_ref, x_ref, gate_ref, t_ref, e_ref, dy_ref, sums_ref):
        y = jnp.dot(u_ref[...], w_ref[...], preferred_element_type=F32)
        diff = x_ref[...] + gate_ref[...] * y - t_ref[...]
        e = diff * (1.0 / D_MODEL)
        e_ref[...] = e
        dy_ref[...] = (e * gate_ref[...]).astype(BF16)
        sums = jnp.concatenate([
            jnp.sum(e * y, axis=0, keepdims=True),
            jnp.sum(diff * diff, axis=0, keepdims=True),
            jnp.zeros((6, D_MODEL), F32)], axis=0)

        @pl.when(pl.program_id(0) == 0)
        def _():
            sums_ref[...] = jnp.zeros_like(sums_ref)

        sums_ref[...] += sums

    return pl.pallas_call(
        body, name=name, grid=(SEQ // tm,),
        in_specs=[_row_spec(tm, D_MODEL), _vec_spec(D_MODEL, D_MODEL), _row_spec(tm, D_MODEL),
                  _vec_spec(1, D_MODEL), _row_spec(tm, D_MODEL)],
        out_specs=[_row_spec(tm, D_MODEL), _row_spec(tm, D_MODEL), _vec_spec(8, D_MODEL)],
        out_shape=[jax.ShapeDtypeStruct((SEQ, D_MODEL), F32), jax.ShapeDtypeStruct((SEQ, D_MODEL), BF16),
                   jax.ShapeDtypeStruct((8, D_MODEL), F32)],
        compiler_params=_params("arbitrary"),
    )(u, w_out, x1, gate, target)


def _seg_matrix():
    r = lax.broadcasted_iota(jnp.int32, (256, 256), 0) // HEAD_DIM
    c = lax.broadcasted_iota(jnp.int32, (256, 256), 1) // HEAD_DIM
    return jnp.where(r == c, 1.0 / HEAD_DIM, 0.0).astype(BF16)


def _segmean(v, seg):
    hi = v.astype(BF16)
    lo = (v - hi.astype(F32)).astype(BF16)
    outs = []
    for c0 in range(0, D_MODEL, 256):
        outs.append(jnp.dot(hi[:, c0:c0 + 256], seg, preferred_element_type=F32)
                    + jnp.dot(lo[:, c0:c0 + 256], seg, preferred_element_type=F32))
    return jnp.concatenate(outs, axis=1)


def _qk_rstd(v, seg):
    return lax.rsqrt(_segmean(v * v, seg) + NORM_EPS)


def _qknorm_fwd(proj, group, qw, kw, seg, name):
    tm = ROW_TILE

    def body(q_in, k_in, qw_ref, kw_ref, seg_ref, q_ref, k_ref):
        segv = seg_ref[...]
        q = q_in[...].astype(F32)
        k = k_in[...].astype(F32)
        q_ref[...] = (q * _qk_rstd(q, segv) * qw_ref[...] * HEAD_DIM ** -0.5).astype(BF16)
        k_ref[...] = (k * _qk_rstd(k, segv) * kw_ref[...]).astype(BF16)

    return pl.pallas_call(
        body, name=name, grid=(SEQ // tm,),
        in_specs=[_row_spec(tm, D_MODEL, 3 * group), _row_spec(tm, D_MODEL, 3 * group + 1),
                  _vec_spec(1, D_MODEL), _vec_spec(1, D_MODEL), _vec_spec(256, 256)],
        out_specs=[_row_spec(tm, D_MODEL)] * 2,
        out_shape=[jax.ShapeDtypeStruct((SEQ, D_MODEL), BF16)] * 2,
        compiler_params=_params("parallel"),
    )(proj, proj, qw, kw, seg)


def _attn_masks(b, bpc, dilation, transposed=False):
    keys = ATTN_BLOCK if bpc == 1 else 2 * ATTN_BLOCK
    shape, q_axis = ((keys, ATTN_BLOCK), 1) if transposed else ((ATTN_BLOCK, keys), 0)
    qi = lax.broadcasted_iota(jnp.int32, shape, q_axis)
    kj = lax.broadcasted_iota(jnp.int32, shape, 1 - q_axis)
    if bpc == 1:
        steps = qi - kj
        return (steps * dilation).astype(F32), steps >= 0
    steps = qi + ATTN_BLOCK - kj
    has_prev = (b % bpc) != 0
    valid = (steps >= 0) & (steps <= ATTN_BLOCK) & (has_prev | (kj >= ATTN_BLOCK))
    return (steps * dilation).astype(F32), valid


MASKED = 1e30


def _bias_scratch(bpc):
    return pltpu.VMEM((1 if bpc == 1 else 2, N_HEADS, ATTN_BLOCK, (1 if bpc == 1 else 2) * ATTN_BLOCK), F32)


def _fill_bias(bias_ref, sl_ref, bpc, dilation):
    for variant in range(bias_ref.shape[0]):
        dist, valid = _attn_masks(variant, min(bpc, 2), dilation)
        bias_ref[variant] = jnp.where(valid[None], dist[None] * sl_ref[...], MASKED)


def _step_bias(bias_ref, b, bpc):
    if bpc == 1:
        return bias_ref[0]
    return bias_ref[jnp.where((b % bpc) != 0, 1, 0)]


def _key_tile(prev_ref, cur_ref, cols, bpc):
    if bpc == 1:
        return cur_ref[:, cols]
    return jnp.concatenate([prev_ref[:, cols], cur_ref[:, cols]], axis=0)


ATTN_HEADS_FWD = 16
ATTN_HEADS_BWD = 16
NT_DIMS = (((1,), (1,)), ((), ()))
BATCH_NT_DIMS = (((2,), (2,)), ((0,), (0,)))
BATCH_NN_DIMS = (((2,), (1,)), ((0,), (0,)))
BATCH_TN_DIMS = (((1,), (1,)), ((0,), (0,)))


def _head_stack(tile_of, heads):
    return jnp.stack([tile_of(slice(h * HEAD_DIM, (h + 1) * HEAD_DIM)) for h in range(heads)], axis=0)


def _attn_specs(heads, segment=0):
    width = heads * HEAD_DIM
    off = segment * (D_MODEL // width)
    last = SEQ // ATTN_BLOCK - 1
    cur = pl.BlockSpec((ATTN_BLOCK, width), lambda hg, b: (jnp.minimum(b, last), hg + off))
    prev = pl.BlockSpec((ATTN_BLOCK, width), lambda hg, b: (jnp.clip(b - 1, 0, last), hg + off))
    return cur, prev


def _attn_fwd(q, k, proj, group, slopes, dilation, name):
    bpc = SEQ // dilation // ATTN_BLOCK
    heads = ATTN_HEADS_FWD
    assert heads == N_HEADS
    cur, prev = _attn_specs(heads)
    v_cur, v_prev = _attn_specs(heads, segment=3 * group + 2)

    def body(sl_ref, q_ref, kp_ref, kc_ref, vp_ref, vc_ref, o_ref, lse_ref, bias_ref):
        b = pl.program_id(1)

        @pl.when(b == 0)
        def _():
            _fill_bias(bias_ref, sl_ref, bpc, dilation)

        q3 = _head_stack(lambda cols: q_ref[:, cols], heads)
        k3 = _head_stack(lambda cols: _key_tile(kp_ref, kc_ref, cols, bpc), heads)
        v3 = _head_stack(lambda cols: _key_tile(vp_ref, vc_ref, cols, bpc), heads)
        s = lax.dot_general(q3, k3, BATCH_NT_DIMS, preferred_element_type=F32)
        s = s - _step_bias(bias_ref, b, bpc)
        m = jnp.max(s, axis=-1, keepdims=True)
        p = jnp.exp(s - m)
        l = jnp.sum(p, axis=-1, keepdims=True)
        o3 = lax.dot_general(p.astype(BF16), v3, BATCH_NN_DIMS, preferred_element_type=F32) / l
        lse3 = m + jnp.log(l)
        for h in range(heads):
            o_ref[:, h * HEAD_DIM:(h + 1) * HEAD_DIM] = o3[h].astype(BF16)
        lse_ref[...] = jnp.concatenate([lse3[h] for h in range(heads)]
                                       + [jnp.zeros((ATTN_BLOCK, LANES - heads), F32)], axis=1)

    return pl.pallas_call(
        body, name=name, grid=(N_HEADS // heads, SEQ // ATTN_BLOCK),
        in_specs=[pl.BlockSpec((heads, 1, 1), lambda hg, b: (hg, 0, 0)), cur, prev, cur, v_prev, v_cur],
        out_specs=[cur, pl.BlockSpec((ATTN_BLOCK, LANES), lambda hg, b: (b, 0))],
        out_shape=[jax.ShapeDtypeStruct((SEQ, D_MODEL), BF16), jax.ShapeDtypeStruct((SEQ, LANES), F32)],
        scratch_shapes=[_bias_scratch(bpc)],
        compiler_params=_params("parallel", "arbitrary"),
    )(slopes.reshape(N_HEADS, 1, 1), q, k, k, proj, proj)


def _class_spec(tm, dilation, width=D_MODEL):
    if dilation == 1:
        return _row_spec(tm, width)
    return pl.BlockSpec((dilation, tm // dilation, width), lambda i: (0, i, 0))


def _class_shape(dilation, dtype, width=D_MODEL):
    if dilation == 1:
        return jax.ShapeDtypeStruct((SEQ, width), dtype)
    return jax.ShapeDtypeStruct((dilation, SEQ // dilation, width), dtype)


def _load_natural(in_ref, nat_ref, dilation):
    if dilation == 1:
        return in_ref[...].astype(F32)
    n = nat_ref.shape[1] // dilation
    tiles = in_ref.shape[-1] // LANES
    for r in range(dilation):
        for j in range(tiles):
            nat_ref.at[j][pl.ds(r, n, stride=dilation), :] = in_ref[r, :, j * LANES:(j + 1) * LANES].astype(F32)
    if tiles == 1:
        return nat_ref[0]
    return jnp.concatenate([nat_ref[j] for j in range(tiles)], axis=1)


def _store_classes(out_ref, value, nat_ref, dilation):
    if dilation == 1:
        out_ref[...] = value.astype(out_ref.dtype)
        return
    n = nat_ref.shape[1] // dilation
    tiles = value.shape[-1] // LANES
    for j in range(tiles):
        nat_ref[j] = value[:, j * LANES:(j + 1) * LANES]
    for r in range(dilation):
        for j in range(tiles):
            out_ref[r, :, j * LANES:(j + 1) * LANES] = (
                nat_ref.at[j][pl.ds(r, n, stride=dilation), :].astype(out_ref.dtype))


def _natural_scratch(tm):
    return pltpu.VMEM((D_MODEL // LANES, tm, LANES), F32)


def _head_selector():
    lane_head = lax.broadcasted_iota(jnp.int32, (D_MODEL, LANES), 0) // HEAD_DIM
    head = lax.broadcasted_iota(jnp.int32, (D_MODEL, LANES), 1)
    return (lane_head == head).astype(BF16)


def _dot_split(v, m01, dims):
    hi = v.astype(BF16)
    lo = (v - hi.astype(F32)).astype(BF16)
    return (lax.dot_general(hi, m01, dims, preferred_element_type=F32)
            + lax.dot_general(lo, m01, dims, preferred_element_type=F32))


def _merge_fwd(o_parts, lse_parts, z, sel, name):
    tm = ROW_TILE
    h_spec = pl.BlockSpec((tm, LANES), lambda i: (i, 0))

    def body(o0, o1, o2, l0, l1, l2, z_ref, sel_ref, u_ref, ut_ref, o_ref, lse_ref, nat):
        ls = [_load_natural(l, nat, d) for l, d in zip((l0, l1, l2), DILATIONS)]
        m = jnp.maximum(jnp.maximum(ls[0], ls[1]), ls[2])
        tot = m + jnp.log(jnp.exp(ls[0] - m) + jnp.exp(ls[1] - m) + jnp.exp(ls[2] - m))
        o = jnp.zeros((tm, D_MODEL), F32)
        for o_in, l, d in zip((o0, o1, o2), ls, DILATIONS):
            weight = _dot_split(jnp.exp(l - tot), sel_ref[...], NT_DIMS)
            o = o + weight * _load_natural(o_in, nat, d)
        zv = z_ref[...].astype(F32)
        u = o * (zv * _sigmoid(zv))
        u_ref[...] = u.astype(BF16)
        ut_ref[...] = u.T.astype(BF16)
        o_ref[...] = o.astype(BF16)
        lse_ref[...] = tot

    return pl.pallas_call(
        body, name=name, grid=(SEQ // tm,),
        in_specs=[_class_spec(tm, d) for d in DILATIONS] + [_class_spec(tm, d, LANES) for d in DILATIONS]
        + [_row_spec(tm, D_MODEL, B_Z_SEGMENT), _vec_spec(D_MODEL, LANES)],
        out_specs=[_row_spec(tm, D_MODEL), pl.BlockSpec((D_MODEL, tm), lambda i: (0, i)),
                   _row_spec(tm, D_MODEL), h_spec],
        out_shape=[jax.ShapeDtypeStruct((SEQ, D_MODEL), BF16), jax.ShapeDtypeStruct((D_MODEL, SEQ), BF16),
                   jax.ShapeDtypeStruct((SEQ, D_MODEL), BF16), jax.ShapeDtypeStruct((SEQ, LANES), F32)],
        scratch_shapes=[_natural_scratch(tm)],
        compiler_params=_params("parallel"),
    )(*o_parts, *lse_parts, z, sel)


def _merge_bwd(dy, w_out, o, lse, z, sel, name):
    tm = ROW_TILE
    n_d = len(DILATIONS)

    def body(dy_ref, w_ref, o_ref, lse_ref, z_ref, sel_ref, dz_ref, *rest):
        do_refs, delta_refs, lse_refs, nat = rest[:n_d], rest[n_d:2 * n_d], rest[2 * n_d:3 * n_d], rest[-1]
        zv = z_ref[...].astype(F32)
        sz = _sigmoid(zv)
        duv = lax.dot_general(dy_ref[...], w_ref[...], NT_DIMS, preferred_element_type=F32)
        ov = o_ref[...].astype(F32)
        do = duv * (zv * sz)
        dz_ref[...] = (duv * ov * (sz * (1.0 + zv * (1.0 - sz)))).astype(BF16)
        delta = _dot_split(do * ov, sel_ref[...], (((1,), (0,)), ((), ())))
        lv = lse_ref[...]
        for i, d in enumerate(DILATIONS):
            _store_classes(do_refs[i], do, nat, d)
            _store_classes(delta_refs[i], delta, nat, d)
            _store_classes(lse_refs[i], lv, nat, d)

    res = pl.pallas_call(
        body, name=name, grid=(SEQ // tm,),
        in_specs=[_row_spec(tm, D_MODEL), _vec_spec(D_MODEL, D_MODEL), _row_spec(tm, D_MODEL), _row_spec(tm, LANES),
                  _row_spec(tm, D_MODEL, B_Z_SEGMENT), _vec_spec(D_MODEL, LANES)],
        out_specs=[_row_spec(tm, D_MODEL)] + [_class_spec(tm, d) for d in DILATIONS]
        + [_class_spec(tm, d, LANES) for d in DILATIONS] * 2,
        out_shape=[jax.ShapeDtypeStruct((SEQ, D_MODEL), BF16)] + [_class_shape(d, BF16) for d in DILATIONS]
        + [_class_shape(d, F32, LANES) for d in DILATIONS] * 2,
        scratch_shapes=[_natural_scratch(tm)],
        compiler_params=_params("parallel"),
    )(dy, w_out, o, lse, z, sel)
    flat = lambda a: a.reshape(SEQ, a.shape[-1])
    return (res[0], [flat(a) for a in res[1:1 + n_d]], [flat(a) for a in res[1 + n_d:1 + 2 * n_d]],
            [flat(a) for a in res[1 + 2 * n_d:]])


def _attn_bwd(q, k, proj, group, do, lse, delta, slopes, dilation, name):
    bpc = SEQ // dilation // ATTN_BLOCK
    heads = ATTN_HEADS_BWD
    n_blocks = SEQ // ATTN_BLOCK
    carry = bpc > 1
    width = heads * HEAD_DIM
    cur, prev = _attn_specs(heads)
    v_cur, v_prev = _attn_specs(heads, segment=3 * group + 2)
    assert heads == N_HEADS
    per_head = pl.BlockSpec((ATTN_BLOCK, LANES), lambda hg, b: (jnp.minimum(b, n_blocks - 1), 0))
    scale = HEAD_DIM ** -0.5

    def body(sl_ref, q_ref, kp_ref, kc_ref, vp_ref, vc_ref, do_ref, lse_ref, dl_ref,
             dq_ref, dk_ref, dv_ref, *scratch):
        b = pl.program_id(1)
        if carry:
            dk_carry, dv_carry = scratch

            @pl.when(b == n_blocks)
            def _():
                dk_ref[...] = dk_carry[...].astype(BF16)
                dv_ref[...] = dv_carry[...].astype(BF16)

            @pl.when(b < n_blocks)
            def _():
                step(sl_ref, q_ref, kp_ref, kc_ref, vp_ref, vc_ref, do_ref, lse_ref, dl_ref,
                     dq_ref, dk_ref, dv_ref, dk_carry, dv_carry, b)
        else:
            step(sl_ref, q_ref, kp_ref, kc_ref, vp_ref, vc_ref, do_ref, lse_ref, dl_ref,
                 dq_ref, dk_ref, dv_ref, None, None, b)

    def step(sl_ref, q_ref, kp_ref, kc_ref, vp_ref, vc_ref, do_ref, lse_ref, dl_ref,
             dq_ref, dk_ref, dv_ref, dk_carry, dv_carry, b):
        if carry:
            @pl.when(b == 0)
            def _():
                dk_carry[...] = jnp.zeros_like(dk_carry)
                dv_carry[...] = jnp.zeros_like(dv_carry)

        q3 = _head_stack(lambda cols: q_ref[:, cols], heads)
        k3 = _head_stack(lambda cols: _key_tile(kp_ref, kc_ref, cols, bpc), heads)
        v3 = _head_stack(lambda cols: _key_tile(vp_ref, vc_ref, cols, bpc), heads)
        do3 = _head_stack(lambda cols: do_ref[:, cols], heads)
        lse_t = lse_ref[...].T
        dl_t = dl_ref[...].T
        lse3 = jnp.stack([lse_t[h:h + 1, :] for h in range(heads)], axis=0)
        dl3 = jnp.stack([dl_t[h:h + 1, :] for h in range(heads)], axis=0)
        s = lax.dot_general(k3, q3, BATCH_NT_DIMS, preferred_element_type=F32)
        dist, valid = _attn_masks(b, bpc, dilation, transposed=True)
        p = jnp.exp(jnp.where(valid[None], s - dist[None] * sl_ref[...], NEG_INF) - lse3)
        dp = lax.dot_general(v3, do3, BATCH_NT_DIMS, preferred_element_type=F32)
        ds = (p * (dp - dl3)).astype(BF16)
        dq3 = lax.dot_general(ds, k3, BATCH_TN_DIMS, preferred_element_type=F32) * scale
        dk3 = lax.dot_general(ds, q3, BATCH_NN_DIMS, preferred_element_type=F32)
        dv3 = lax.dot_general(p.astype(BF16), do3, BATCH_NN_DIMS, preferred_element_type=F32)
        for h in range(heads):
            cols = slice(h * HEAD_DIM, (h + 1) * HEAD_DIM)
            dq_ref[:, cols] = dq3[h].astype(BF16)
            if carry:
                dk_ref[:, cols] = (dk_carry[:, cols] + dk3[h, :ATTN_BLOCK]).astype(BF16)
                dv_ref[:, cols] = (dv_carry[:, cols] + dv3[h, :ATTN_BLOCK]).astype(BF16)
                dk_carry[:, cols] = dk3[h, ATTN_BLOCK:]
                dv_carry[:, cols] = dv3[h, ATTN_BLOCK:]
            else:
                dk_ref[:, cols] = dk3[h].astype(BF16)
                dv_ref[:, cols] = dv3[h].astype(BF16)

    kv_out = prev if carry else cur
    return pl.pallas_call(
        body, name=name, grid=(N_HEADS // heads, n_blocks + (1 if carry else 0)),
        in_specs=[pl.BlockSpec((heads, 1, 1), lambda hg, b: (hg, 0, 0)), cur, prev, cur, v_prev, v_cur,
                  cur, per_head, per_head],
        out_specs=[cur, kv_out, kv_out],
        out_shape=[jax.ShapeDtypeStruct((SEQ, D_MODEL), BF16)] * 3,
        scratch_shapes=[pltpu.VMEM((ATTN_BLOCK, width), F32)] * 2 if carry else [],
        compiler_params=_params("parallel", "arbitrary"),
    )(slopes.reshape(N_HEADS, 1, 1), q, k, k, proj, proj, do, lse, delta)


def _qknorm_bwd(proj, group, qw, kw, seg, dq, dk, dv, name):
    tm = ROW_TILE

    def body(q_in, k_in, qw_ref, kw_ref, seg_ref, dq_ref, dk_ref, dv_ref, dproj_ref, sums_ref):
        segv = seg_ref[...]
        sums = []
        for part, (raw_ref, w_ref, dn_ref) in enumerate(((q_in, qw_ref, dq_ref), (k_in, kw_ref, dk_ref))):
            raw = raw_ref[...].astype(F32)
            dn = dn_ref[...].astype(F32)
            r = _qk_rstd(raw, segv)
            xhat = raw * r
            gq = dn * w_ref[...]
            draw = r * (gq - xhat * _segmean(xhat * gq, segv))
            dproj_ref[:, part * D_MODEL:(part + 1) * D_MODEL] = draw.astype(BF16)
            sums.append(jnp.sum(dn * xhat, axis=0, keepdims=True))
        dproj_ref[:, 2 * D_MODEL:] = dv_ref[...]

        @pl.when(pl.program_id(0) == 0)
        def _():
            sums_ref[...] = jnp.zeros_like(sums_ref)

        sums_ref[...] += jnp.concatenate(sums + [jnp.zeros((6, D_MODEL), F32)], axis=0)

    return pl.pallas_call(
        body, name=name, grid=(SEQ // tm,),
        in_specs=[_row_spec(tm, D_MODEL, 3 * group), _row_spec(tm, D_MODEL, 3 * group + 1),
                  _vec_spec(1, D_MODEL), _vec_spec(1, D_MODEL), _vec_spec(256, 256)] + [_row_spec(tm, D_MODEL)] * 3,
        out_specs=[_row_spec(tm, 3 * D_MODEL), _vec_spec(8, D_MODEL)],
        out_shape=[jax.ShapeDtypeStruct((SEQ, 3 * D_MODEL), BF16), jax.ShapeDtypeStruct((8, D_MODEL), F32)],
        compiler_params=_params("arbitrary"),
    )(proj, proj, qw, kw, seg, dq, dk, dv)


B_TN = 512
B_GROUP_TILES = 3 * D_MODEL // B_TN
B_Z_TILE0 = 3 * B_GROUP_TILES
B_Z_TILES = D_MODEL // B_TN
B_TILES = B_Z_TILE0 + B_Z_TILES
B_Z_SEGMENT = 3 * len(DILATIONS)


def _local_step(x, target, mods, norm_g, conv_w, conv_b, ln_g, ln_b, q_norm, k_norm, chip, own_wa_in, own_wb_in,
                weights_a, weights_b, forward_weights_b, send_grads_b, forward_grads_b, send_grads_a):
    row = lambda a, i: a[i:i + 1]
    shift0, scale0, gate0 = row(mods[0], 0), row(mods[0], 1), row(mods[0], 2)
    shift1, scale1, gate1 = row(mods[1], 0), row(mods[1], 1), row(mods[1], 2)
    g0, g1 = row(norm_g, 0), row(norm_g, 1)
    seg = _seg_matrix()
    slopes = jnp.exp2(-8.0 * jnp.arange(1, N_HEADS + 1, dtype=F32) / N_HEADS)
    qw = [jnp.tile(q_norm[g:g + 1], (1, N_HEADS)) for g in range(3)]
    kw = [jnp.tile(k_norm[g:g + 1], (1, N_HEADS)) for g in range(3)]

    h0, h0t = _normmod_fwd(x, g0, scale0, shift0, "prenorm0")
    nsa = own_wa_in.shape[2]
    tiles_a = dict(tn=nsa, total_tiles=N_CHIPS, part_of=lambda tile: 0)
    own_ids, rest_ids, own_tiles = _own_first(chip, N_CHIPS)
    proj_a = _in_tiles([h0], own_wa_in, own_ids, own_tiles, name="a_in_own", **tiles_a)
    wa_in, wa_out = weights_a(proj_a)
    ja = wa_in.shape[0]
    proj_a = _in_tiles([h0], wa_in, rest_ids, N_CHIPS - own_tiles, name="a_in_rest", prev=proj_a, **tiles_a)
    u5, u5t, u2 = _conv_fwd(proj_a, conv_w, conv_b, ln_g, ln_b, "a_conv")
    x1, y_a, h1t, h1c = _out_a(u5, wa_out, x, gate0, g1, scale1, shift1, "a_out")

    tiles_b = dict(tn=B_TN, total_tiles=B_TILES,
                   part_of=lambda tile: jnp.where(tile >= B_Z_TILE0, 0, tile // B_GROUP_TILES))
    own_ids, rest_ids, own_tiles = _own_first(chip, B_TILES)
    proj_b = _in_tiles(h1c, own_wb_in, own_ids, own_tiles, name="b_in_own", **tiles_b)
    forward_weights_b(proj_b)
    wb_in, wb_out = weights_b(proj_b)
    jb, _, nsb = wb_in.shape
    proj_b = _in_tiles(h1c, wb_in, rest_ids, B_TILES - own_tiles, name="b_in_rest", prev=proj_b, **tiles_b)
    h1 = h1c[0]
    qkv, o_parts, lse_parts = [], [], []
    for g, d in enumerate(DILATIONS):
        qn, kn = _qknorm_fwd(proj_b, g, qw[g], kw[g], seg, f"b_qknorm_g{g}")
        og, lg = _attn_fwd(qn, kn, proj_b, g, slopes, d, f"b_attn_g{g}")
        qkv.append((qn, kn))
        o_parts.append(og if d == 1 else og.reshape(d, SEQ // d, D_MODEL))
        lse_parts.append(lg if d == 1 else lg.reshape(d, SEQ // d, LANES))
    sel = _head_selector()
    u_b, u_bt, o_b, lse_b = _merge_fwd(o_parts, lse_parts, proj_b, sel, "b_merge")
    e, dy_b, sums_loss = _out_b_loss(u_b, wb_out, x1, gate1, target, "b_out_loss")

    dwb_out = _mm(u_bt, dy_b, tn=D_MODEL, tile0=0, n_tiles=1, out_dtype=BF16, name="b_dwout")
    dz_b, do_c, delta_c, lse_c = _merge_bwd(dy_b, wb_out, o_b, lse_b, proj_b, sel, "b_merge_bwd")
    dwb_in = _mm(h1t, dz_b, tn=B_TN, tile0=B_Z_TILE0, n_tiles=B_Z_TILES, out_dtype=BF16, name="b_dwin_z",
                 out3d=(jb, nsb))
    dh1_parts = [_mm_nt(dz_b, wb_in, tn=B_TN, tile0=B_Z_TILE0, n_tiles=B_Z_TILES, name="b_dh_z")]
    qk_sums = []
    for g, d in enumerate(DILATIONS):
        qn, kn = qkv[g]
        dq, dk, dv = _attn_bwd(qn, kn, proj_b, g, do_c[g], lse_c[g], delta_c[g], slopes, d, f"b_attn_bwd_g{g}")
        dproj, sums_qk = _qknorm_bwd(proj_b, g, qw[g], kw[g], seg, dq, dk, dv, f"b_qknorm_bwd_g{g}")
        qk_sums.append(sums_qk)
        dwb_in = _mm(h1t if d == 1 else h1c[g], dproj, tn=B_TN, tile0=g * B_GROUP_TILES, n_tiles=B_GROUP_TILES,
                     out_dtype=BF16, name=f"b_dwin_g{g}", out3d=(jb, nsb), prev=dwb_in, transpose_lhs=d != 1)
        dh = _mm_nt(dproj, wb_in, tn=B_TN, tile0=g * B_GROUP_TILES, n_tiles=B_GROUP_TILES, name=f"b_dh_g{g}")
        dh1_parts.append(dh)
    token = send_grads_b(dwb_in, dwb_out)
    dx1, sums_n1, dy_a = _normmod_bwd(x1, g1, scale1 + token[0:1, 0:1], dh1_parts, e, "prenorm1_bwd",
                                      part_dilations=(1,) + DILATIONS, gated=(gate0, y_a))
    token = forward_grads_b(dx1)

    dwa_out = _mm(u5t, dy_a, tn=D_MODEL, tile0=0, n_tiles=1, out_dtype=BF16, name="a_dwout")
    du2, dz_a, sums_ln = _conv_bwd_pointwise(dy_a, wa_out, proj_a, u2, ln_g + token[0:1, 0:1], ln_b,
                                             "a_conv_bwd_pw")
    dproj_a, dconv_w = _conv_bwd_taps(du2, dz_a, proj_a, conv_w, "a_conv_bwd_taps")
    dwa_in = _mm(h0t, dproj_a, tn=nsa, tile0=0, n_tiles=ja, out_dtype=BF16, name="a_dwin", out3d=(ja, nsa))
    token = send_grads_a(dwa_in, dwa_out)
    dh0 = _mm_nt(dproj_a, wa_in, tn=nsa, tile0=0, n_tiles=ja, name="a_dh", after=token)
    grad_x, sums_n0 = _normmod_bwd(x, g0, scale0, [dh0], dx1, "prenorm0_bwd")

    small = dict(
        dnorm_g=jnp.concatenate([sums_n0[0:1], sums_n1[0:1]], axis=0),
        dmod0=jnp.concatenate([sums_n0[2:3], sums_n0[1:2], sums_n1[3:4]], axis=0),
        dmod1=jnp.concatenate([sums_n1[2:3], sums_n1[1:2], sums_loss[0:1]], axis=0),
        dln_g=sums_ln[0:1], dln_b=sums_ln[1:2], dconv_b=sums_ln[2:3],
        dconv_w=dconv_w[:CONV_WIDTH],
        dq_norm=jnp.concatenate([s[0:1] for s in qk_sums], axis=0),
        dk_norm=jnp.concatenate([s[1:2] for s in qk_sums], axis=0),
        loss_cols=sums_loss[1:2],
    )
    return grad_x, small


def _adamw(w, g, m, v, name, after=None, copy_grad=False):
    rows, cols = w.shape
    tr = rows if rows <= 128 else 128
    c1 = 1.0 / (1.0 - ADAM_B1 ** ADAM_STEP)
    c2 = 1.0 / (1.0 - ADAM_B2 ** ADAM_STEP)
    extra = [] if after is None else [after]
    n_out = 4 if copy_grad else 3

    def body(w_ref, g_ref, m_ref, v_ref, *rest):
        d_ref, mo_ref, vo_ref = rest[len(extra):len(extra) + 3]
        gv = g_ref[...]
        if copy_grad:
            rest[-1][...] = gv
        mn = ADAM_B1 * m_ref[...] + (1.0 - ADAM_B1) * gv
        vn = ADAM_B2 * v_ref[...] + (1.0 - ADAM_B2) * (gv * gv)
        mo_ref[...] = mn
        vo_ref[...] = vn
        d_ref[...] = -ADAM_LR * ((mn * c1) / (jnp.sqrt(vn * c2) + ADAM_EPS) + ADAM_WD * w_ref[...])

    spec = pl.BlockSpec((tr, cols), lambda i: (i, 0))
    return pl.pallas_call(
        body, name=name, grid=(rows // tr,),
        in_specs=[spec] * 4 + [pl.BlockSpec(memory_space=pl.ANY)] * len(extra), out_specs=[spec] * n_out,
        out_shape=[jax.ShapeDtypeStruct((rows, cols), F32)] * n_out,
        compiler_params=_params("parallel"),
    )(w, g, m, v, *extra)


def _cast_into_slot(w, chip_idx, name, keep_own=False, after=None):
    rows, cols = w.shape
    tr = 256
    extra = [] if after is None else [after]

    def body(ch_ref, w_ref, *rest):
        wb = w_ref[...].astype(BF16)
        for o_ref in rest[len(extra):]:
            o_ref[...] = wb

    slot_spec = pl.BlockSpec((None, tr, cols), lambda i, ch: (ch[0], i, 0))
    own_spec = pl.BlockSpec((None, tr, cols), lambda i, ch: (0, i, 0))
    res = pl.pallas_call(
        body, name=name,
        grid_spec=pltpu.PrefetchScalarGridSpec(
            num_scalar_prefetch=1, grid=(rows // tr,),
            in_specs=[pl.BlockSpec((tr, cols), lambda i, ch: (i, 0))] + [pl.BlockSpec(memory_space=pl.ANY)] * len(extra),
            out_specs=[slot_spec, own_spec] if keep_own else [slot_spec]),
        out_shape=[jax.ShapeDtypeStruct((N_CHIPS, rows, cols), BF16)]
        + ([jax.ShapeDtypeStruct((1, rows, cols), BF16)] if keep_own else []),
        compiler_params=_params("parallel"),
    )(chip_idx, w, *extra)
    return tuple(res) if keep_own else res[0]


def _position():
    x, y, c = lax.axis_index("x"), lax.axis_index("y"), lax.axis_index("c")
    return x, y, c


def _xor_peer(x, y, c, k):
    return (x ^ ((k >> 2) & 1), y ^ ((k >> 1) & 1), c ^ (k & 1))


def _chip_peer(x, y, k):
    return (x ^ ((k >> 1) & 1), y ^ (k & 1))


def _ada_forward(c_row, ada_w, ada_b, conv_w, after=()):
    ns = ada_w.shape[2]
    cw = conv_w.shape[1]

    def body(c_ref, w_ref, b_ref, cv_ref, *rest):
        (mod_ref, sc_ref, cvo_ref, c_all, mp, parts, cv_parts,
         send1, recv1, send2, recv2, send3, recv3) = rest[len(after):]
        x, y, c = _position()
        me = 4 * x + 2 * y + c
        chip = 2 * x + y

        def c_copy(k):
            return pltpu.make_async_remote_copy(
                src_ref=c_all.at[me], dst_ref=c_all.at[me], send_sem=send1.at[k - 1], recv_sem=recv1.at[k - 1],
                device_id=_xor_peer(x, y, c, k), device_id_type=MESH)

        def cv_copy(k):
            px, py = _chip_peer(x, y, k)
            return pltpu.make_async_remote_copy(
                src_ref=cv_parts.at[chip], dst_ref=cv_parts.at[chip], send_sem=send3.at[k - 1],
                recv_sem=recv3.at[k - 1], device_id=(px, py, c), device_id_type=MESH)

        c_all[me] = c_ref[...]
        cv_parts[chip] = cv_ref[...]
        for k in range(1, N_DEV):
            c_copy(k).start()
        for k in range(1, N_CHIPS):
            cv_copy(k).start()
        for k in range(1, N_DEV):
            c_copy(k).wait_recv()
        cv = jnp.concatenate([c_all[i] for i in range(N_DEV)], axis=0)
        sc = cv * _sigmoid(cv)
        sc_ref[...] = sc
        for l in range(2):
            res = jnp.dot(sc, w_ref[l], preferred_element_type=F32, precision=lax.Precision.HIGHEST)
            for i in range(N_DEV):
                mp[i, l:l + 1, :] = res[i:i + 1, :]

        def mod_copy(k):
            px, py = _chip_peer(x, y, k)
            return pltpu.make_async_remote_copy(
                src_ref=mp.at[4 * px + 2 * py + c], dst_ref=parts.at[chip], send_sem=send2.at[k - 1],
                recv_sem=recv2.at[k - 1], device_id=(px, py, c), device_id_type=MESH)

        for k in range(1, N_CHIPS):
            mod_copy(k).start()
        parts[chip] = mp[me]
        for k in range(1, N_CHIPS):
            mod_copy(k).wait_recv()
            cv_copy(k).wait_recv()
        mod_ref[...] = jnp.concatenate([parts[j] for j in range(N_CHIPS)], axis=1) + b_ref[...]
        cvo_ref[...] = jnp.concatenate([cv_parts[j] for j in range(N_CHIPS)], axis=1)
        for k in range(1, N_DEV):
            c_copy(k).wait_send()
        for k in range(1, N_CHIPS):
            mod_copy(k).wait_send()
            cv_copy(k).wait_send()

    vm = pl.BlockSpec(memory_space=pltpu.VMEM)
    return pl.pallas_call(
        body, name="ada_forward",
        in_specs=[vm] * 4 + [pl.BlockSpec(memory_space=pl.ANY)] * len(after), out_specs=[vm] * 3,
        out_shape=[jax.ShapeDtypeStruct((2, 3 * D_MODEL), F32), jax.ShapeDtypeStruct((N_DEV, D_MODEL), F32),
                   jax.ShapeDtypeStruct((CONV_WIDTH, N_CHIPS * cw), F32)],
        scratch_shapes=[pltpu.VMEM((N_DEV, 1, D_MODEL), F32), pltpu.VMEM((N_DEV, 2, ns), F32),
                        pltpu.VMEM((N_CHIPS, 2, ns), F32), pltpu.VMEM((N_CHIPS, CONV_WIDTH, cw), F32),
                        pltpu.SemaphoreType.DMA((N_DEV - 1,)), pltpu.SemaphoreType.DMA((N_DEV - 1,)),
                        pltpu.SemaphoreType.DMA((N_CHIPS - 1,)), pltpu.SemaphoreType.DMA((N_CHIPS - 1,)),
                        pltpu.SemaphoreType.DMA((N_CHIPS - 1,)), pltpu.SemaphoreType.DMA((N_CHIPS - 1,))],
        compiler_params=pltpu.CompilerParams(vmem_limit_bytes=VMEM_LIMIT_BYTES),
    )(c_row, ada_w, ada_b, conv_w, *after)


HBM_SPEC = pl.BlockSpec(memory_space=pltpu.HBM)
ANY_SPEC = pl.BlockSpec(memory_space=pl.ANY)
SEM_SPEC = pl.BlockSpec(memory_space=pltpu.SEMAPHORE)
SPLIT_PARAMS = dict(compiler_params=pltpu.CompilerParams(has_side_effects=pltpu.SideEffectType.DATAFLOW_SIDE_EFFECTING))
TOKEN = jax.ShapeDtypeStruct((8, 128), F32)


def _hbm(arrays):
    return [pltpu.with_memory_space_constraint(a, pltpu.HBM) for a in arrays]


def _hbm_like(arrays):
    return [pltpu.HBM(a.shape, a.dtype) for a in arrays]


def _gather_start(lands, after, name):
    n = len(lands)

    def body(*refs):
        ins = refs[:n]
        send, recv = refs[n + 1], refs[n + 2]
        x, y, c = _position()
        chip = 2 * x + y
        for t in range(n):
            rh = ins[t].shape[1] // 2
            for k in range(1, N_CHIPS):
                px, py = _chip_peer(x, y, k)
                block = ins[t].at[chip, pl.ds(c * rh, rh)]
                pltpu.make_async_remote_copy(
                    src_ref=block, dst_ref=block, send_sem=send.at[3 * t + k - 1], recv_sem=recv.at[3 * t + k - 1],
                    device_id=(px, py, c), device_id_type=MESH).start()
        refs[-1][...] = jnp.zeros(TOKEN.shape, F32)

    res = pl.pallas_call(
        body, name=name, in_specs=[HBM_SPEC] * n + [ANY_SPEC],
        out_specs=(SEM_SPEC, SEM_SPEC, *[HBM_SPEC] * n, pl.BlockSpec(memory_space=pltpu.VMEM)),
        out_shape=(pltpu.SemaphoreType.DMA((3 * n,)), pltpu.SemaphoreType.DMA((3 * n,)), *_hbm_like(lands), TOKEN),
        input_output_aliases={t: 2 + t for t in range(n)}, **SPLIT_PARAMS,
    )(*_hbm(lands), after)
    return res[0], res[1], list(res[2:2 + n]), res[-1]


def _gather_forward(send, recv, lands, after, name):
    n = len(lands)

    def body(*refs):
        ins = refs[:n]
        send1, recv1 = refs[n], refs[n + 1]
        send2, recv2 = refs[n + 3], refs[n + 4]
        x, y, c = _position()
        chip = 2 * x + y
        for t in range(n):
            rh = ins[t].shape[1] // 2
            half = pl.ds(c * rh, rh)
            for k in range(1, N_CHIPS):
                px, py = _chip_peer(x, y, k)
                s = 3 * t + k - 1
                got = ins[t].at[2 * px + py, half]
                cp = pltpu.make_async_remote_copy(
                    src_ref=ins[t].at[chip, half], dst_ref=got, send_sem=send1.at[s], recv_sem=recv1.at[s],
                    device_id=(px, py, c), device_id_type=MESH)
                cp.wait_send()
                cp.wait_recv()
                pltpu.make_async_remote_copy(
                    src_ref=got, dst_ref=got, send_sem=send2.at[s], recv_sem=recv2.at[s],
                    device_id=(x, y, 1 - c), device_id_type=MESH).start()
        refs[-1][...] = jnp.zeros(TOKEN.shape, F32)

    res = pl.pallas_call(
        body, name=name, in_specs=[HBM_SPEC] * n + [SEM_SPEC, SEM_SPEC, ANY_SPEC],
        out_specs=(SEM_SPEC, SEM_SPEC, *[HBM_SPEC] * n, pl.BlockSpec(memory_space=pltpu.VMEM)),
        out_shape=(pltpu.SemaphoreType.DMA((3 * n,)), pltpu.SemaphoreType.DMA((3 * n,)), *_hbm_like(lands), TOKEN),
        input_output_aliases={t: 2 + t for t in range(n)}, **SPLIT_PARAMS,
    )(*lands, send, recv, after)
    return res[0], res[1], list(res[2:2 + n]), res[-1]


def _gather_wait(send, recv, lands, after, name):
    n = len(lands)

    def body(*refs):
        ins = refs[:n]
        send_ref, recv_ref = refs[n], refs[n + 1]
        x, y, c = _position()
        for t in range(n):
            rh = ins[t].shape[1] // 2
            for k in range(1, N_CHIPS):
                px, py = _chip_peer(x, y, k)
                cp = pltpu.make_async_remote_copy(
                    src_ref=ins[t].at[2 * px + py, pl.ds(c * rh, rh)],
                    dst_ref=ins[t].at[2 * px + py, pl.ds((1 - c) * rh, rh)], send_sem=send_ref.at[3 * t + k - 1],
                    recv_sem=recv_ref.at[3 * t + k - 1], device_id=(x, y, 1 - c), device_id_type=MESH)
                cp.wait_send()
                cp.wait_recv()

    res = pl.pallas_call(
        body, name=name, in_specs=[HBM_SPEC] * n + [SEM_SPEC, SEM_SPEC, ANY_SPEC], out_specs=[HBM_SPEC] * n,
        out_shape=_hbm_like(lands), input_output_aliases={t: t for t in range(n)}, **SPLIT_PARAMS,
    )(*lands, send, recv, after)
    return list(res)


def _split_start(name, arrays, n_sems, after, issue):
    m = len(arrays)

    def body(*refs):
        issue(refs[:m], refs[m + 1], refs[m + 2])
        refs[-1][...] = jnp.zeros(TOKEN.shape, F32)

    res = pl.pallas_call(
        body, name=name, in_specs=[HBM_SPEC] * m + [ANY_SPEC],
        out_specs=(SEM_SPEC, SEM_SPEC, *[HBM_SPEC] * m, pl.BlockSpec(memory_space=pltpu.VMEM)),
        out_shape=(pltpu.SemaphoreType.DMA((n_sems,)), pltpu.SemaphoreType.DMA((n_sems,)), *_hbm_like(arrays), TOKEN),
        input_output_aliases={t: 2 + t for t in range(m)}, **SPLIT_PARAMS,
    )(*_hbm(arrays), after)
    return res[0], res[1], list(res[2:2 + m]), res[-1]


def _split_wait(name, arrays, send, recv, after, await_all):
    m = len(arrays)

    def body(*refs):
        await_all(refs[:m], refs[m], refs[m + 1])

    res = pl.pallas_call(
        body, name=name, in_specs=[HBM_SPEC] * m + [SEM_SPEC, SEM_SPEC, ANY_SPEC], out_specs=[HBM_SPEC] * m,
        out_shape=_hbm_like(arrays), input_output_aliases={t: t for t in range(m)}, **SPLIT_PARAMS,
    )(*arrays, send, recv, after)
    return list(res)


def _sibling_copies(refs, send, recv, n):
    x, y, c = _position()
    cps = []
    for t in range(n):
        rh = refs[t].shape[1] // 2
        cps.append(pltpu.make_async_remote_copy(
            src_ref=refs[t].at[pl.ds(0, N_CHIPS), pl.ds((1 - c) * rh, rh)], dst_ref=refs[n + t],
            send_sem=send.at[t], recv_sem=recv.at[t], device_id=(x, y, 1 - c), device_id_type=MESH))
    return cps


def _reduce_sibling_start(grads, after, name):
    n = len(grads)
    lands = [lax.empty((N_CHIPS, g.shape[1] // 2, g.shape[2]), BF16) for g in grads]

    def issue(refs, send, recv):
        for cp in _sibling_copies(refs, send, recv, n):
            cp.start()

    return _split_start(name, list(grads) + lands, n, after, issue)


def _reduce_sibling_wait(send, recv, arrays, after, name):
    n = len(arrays) // 2

    def await_all(refs, send_ref, recv_ref):
        for cp in _sibling_copies(refs, send_ref, recv_ref, n):
            cp.wait_send()
            cp.wait_recv()

    res = _split_wait(name, arrays, send, recv, after, await_all)
    return res[:n], res[n:]


def _add_sibling_half(grad, got, dev_idx, name):
    j, r, cols = grad.shape
    rh = r // 2
    tr = rh
    nb = rh // tr

    def body(idx_ref, g_ref, got_ref, out_ref):
        out_ref[...] = (g_ref[...].astype(F32) + got_ref[...].astype(F32)).astype(BF16)

    return pl.pallas_call(
        body, name=name,
        grid_spec=pltpu.PrefetchScalarGridSpec(
            num_scalar_prefetch=1, grid=(j, nb),
            in_specs=[pl.BlockSpec((None, tr, cols), lambda jj, i, idx: (jj, idx[2] * nb + i, 0)),
                      pl.BlockSpec((None, tr, cols), lambda jj, i, idx: (jj, i, 0))],
            out_specs=pl.BlockSpec((None, tr, cols), lambda jj, i, idx: (jj, i, 0))),
        out_shape=jax.ShapeDtypeStruct((j, rh, cols), BF16),
        compiler_params=_params("parallel", "parallel"),
    )(dev_idx, grad, got)


def _chip_copies(refs, send, recv, n, receiving):
    x, y, c = _position()
    chip = 2 * x + y
    cps = []
    for t in range(n):
        for k in range(1, N_CHIPS):
            px, py = _chip_peer(x, y, k)
            cps.append(pltpu.make_async_remote_copy(
                src_ref=refs[t].at[2 * px + py], dst_ref=refs[n + t].at[2 * px + py if receiving else chip],
                send_sem=send.at[3 * t + k - 1], recv_sem=recv.at[3 * t + k - 1],
                device_id=(px, py, c), device_id_type=MESH))
    return cps


def _reduce_chips_start(partials, after, name):
    n = len(partials)
    lands = [lax.empty(p.shape, BF16) for p in partials]

    def issue(refs, send, recv):
        for cp in _chip_copies(refs, send, recv, n, False):
            cp.start()

    return _split_start(name, list(partials) + lands, 3 * n, after, issue)


def _reduce_chips_wait(send, recv, arrays, after, name):
    n = len(arrays) // 2

    def await_all(refs, send_ref, recv_ref):
        for cp in _chip_copies(refs, send_ref, recv_ref, n, True):
            cp.wait_send()
            cp.wait_recv()

    res = _split_wait(name, arrays, send, recv, after, await_all)
    return res[:n], res[n:]


def _sum_partials(land, partial, dev_idx, name):
    _, rh, cols = land.shape
    tr = min(rh, 256)
    nb = rh // tr

    def body(idx_ref, l_ref, p_ref, o_ref):
        chip = idx_ref[1]
        acc = jnp.where(chip == 0, p_ref[...], l_ref[0]).astype(F32)
        for s in range(1, N_CHIPS):
            acc = acc + jnp.where(chip == s, p_ref[...], l_ref[s]).astype(F32)
        o_ref[...] = acc

    return pl.pallas_call(
        body, name=name,
        grid_spec=pltpu.PrefetchScalarGridSpec(
            num_scalar_prefetch=1, grid=(nb,),
            in_specs=[pl.BlockSpec((N_CHIPS, tr, cols), lambda i, idx: (0, i, 0)),
                      pl.BlockSpec((None, tr, cols), lambda i, idx: (idx[1], i, 0))],
            out_specs=pl.BlockSpec((tr, cols), lambda i, idx: (idx[2] * nb + i, 0))),
        out_shape=jax.ShapeDtypeStruct((2 * rh, cols), F32), compiler_params=_params("parallel"),
    )(dev_idx, land, partial)


def _half_copies(refs, send, recv, receiving):
    x, y, c = _position()
    cps = []
    for t, ref in enumerate(refs):
        rh = ref.shape[0] // 2
        cps.append(pltpu.make_async_remote_copy(
            src_ref=ref.at[pl.ds(c * rh, rh)], dst_ref=ref.at[pl.ds(((1 - c) if receiving else c) * rh, rh)],
            send_sem=send.at[t], recv_sem=recv.at[t], device_id=(x, y, 1 - c), device_id_type=MESH))
    return cps


def _share_halves_start(totals, after, name):
    def issue(refs, send, recv):
        for cp in _half_copies(refs, send, recv, False):
            cp.start()

    return _split_start(name, list(totals), len(totals), after, issue)


def _share_halves_wait(send, recv, totals, after, name):
    def await_all(refs, send_ref, recv_ref):
        for cp in _half_copies(refs, send_ref, recv_ref, True):
            cp.wait_send()
            cp.wait_recv()

    return _split_wait(name, totals, send, recv, after, await_all)


SMALL_ROWS = 56


def _small_copies(refs, send, recv, receiving):
    x, y, c = _position()
    me = 4 * x + 2 * y + c
    cps = []
    for k in range(1, N_DEV):
        px, py, pc = _xor_peer(x, y, c, k)
        cps.append(pltpu.make_async_remote_copy(
            src_ref=refs[0], dst_ref=refs[1].at[4 * px + 2 * py + pc if receiving else me],
            send_sem=send.at[k - 1], recv_sem=recv.at[k - 1], device_id=(px, py, pc), device_id_type=MESH))
    return cps


def _small_gather_start(packed, after):
    land = lax.empty((N_DEV,) + packed.shape, F32)

    def issue(refs, send, recv):
        for cp in _small_copies(refs, send, recv, False):
            cp.start()

    return _split_start("small_gather_start", [packed, land], N_DEV - 1, after, issue)


def _small_gather_wait(send, recv, arrays, after):
    def await_all(refs, send_ref, recv_ref):
        for cp in _small_copies(refs, send_ref, recv_ref, True):
            cp.wait_send()
            cp.wait_recv()

    return _split_wait("small_gather_wait", arrays, send, recv, after, await_all)


def _reduce_small(packed, land, silu_c):
    ns = 3 * D_MODEL // N_CHIPS

    def body(p_ref, land_ref, sc_ref, tot_ref, gw_ref, loss_ref, qk_ref, allp):
        x, y, c = _position()
        me = 4 * x + 2 * y + c
        chip = 2 * x + y
        for i in range(N_DEV):
            allp[i] = jnp.where(me == i, p_ref[...], land_ref[i])
        tot = allp[0]
        for i in range(1, N_DEV):
            tot = tot + allp[i]
        tot_ref[...] = tot
        loss_ref[...] = jnp.sum(tot[11:12, :], axis=1, keepdims=True) * (0.5 / D_MODEL)
        fold = tot[5:11, 0:HEAD_DIM]
        for h in range(1, N_HEADS):
            fold = fold + tot[5:11, h * HEAD_DIM:(h + 1) * HEAD_DIM]
        qk_ref[...] = jnp.concatenate([fold, jnp.zeros((2, HEAD_DIM), F32)], axis=0)
        sct = sc_ref[...].T
        rc = 64
        for l in range(2):
            dms = [allp[i, pl.ds(12 + 4 * l + chip, 1), :][:, :ns] for i in range(N_DEV)]
            for r0 in range(0, D_MODEL, rc):
                acc = sct[r0:r0 + rc, 0:1] * dms[0]
                for i in range(1, N_DEV):
                    acc = acc + sct[r0:r0 + rc, i:i + 1] * dms[i]
                gw_ref[l, r0:r0 + rc, :] = acc

    vm = pl.BlockSpec(memory_space=pltpu.VMEM)
    return pl.pallas_call(
        body, name="reduce_small", in_specs=[vm, vm, vm], out_specs=[vm] * 4,
        out_shape=[jax.ShapeDtypeStruct((SMALL_ROWS, D_MODEL), F32), jax.ShapeDtypeStruct((2, D_MODEL, ns), F32),
                   jax.ShapeDtypeStruct((1, 1), F32), jax.ShapeDtypeStruct((8, HEAD_DIM), F32)],
        scratch_shapes=[pltpu.VMEM((N_DEV, SMALL_ROWS, D_MODEL), F32)],
        compiler_params=pltpu.CompilerParams(vmem_limit_bytes=VMEM_LIMIT_BYTES),
    )(packed, land, silu_c)


def kernel(x, c, norm_g, ada_w, ada_b, a_w_in, a_conv_w, a_conv_b, a_ln_g, a_ln_b, a_w_out, b_w_in, b_q_norm, b_k_norm, b_w_out, loss_target, m_norm_g, m_ada_w, m_ada_b, m_a_w_in, m_a_conv_w, m_a_conv_b, m_a_ln_g, m_a_ln_b, m_a_w_out, m_b_w_in, m_b_q_norm, m_b_k_norm, m_b_w_out, v_norm_g, v_ada_w, v_ada_b, v_a_w_in, v_a_conv_w, v_a_conv_b, v_a_ln_g, v_a_ln_b, v_a_w_out, v_b_w_in, v_b_q_norm, v_b_k_norm, v_b_w_out):
    chip = 2 * lax.axis_index("x") + lax.axis_index("y")
    core = lax.axis_index("c")
    chip_idx = chip.astype(jnp.int32).reshape(1)
    dev_idx = jnp.stack([2 * chip + core, chip, core]).astype(jnp.int32)

    land_a_in, own_wa_in = _cast_into_slot(a_w_in[0], chip_idx, "cast_a_w_in", keep_own=True)
    lands_a = [land_a_in, _cast_into_slot(a_w_out[0], chip_idx, "cast_a_w_out")]
    mods, silu_c, conv_w_full = _ada_forward(c, ada_w, ada_b, a_conv_w[0], after=tuple(lands_a))
    send_a, recv_a, lands_a, token_a = _gather_start(lands_a, mods, "gather_start_a")
    land_b_in, own_wb_in = _cast_into_slot(b_w_in[0], chip_idx, "cast_b_w_in", keep_own=True, after=token_a)
    lands_b = [land_b_in, _cast_into_slot(b_w_out[0], chip_idx, "cast_b_w_out", after=token_a)]
    send_b, recv_b, lands_b, token_b = _gather_start(lands_b, token_a, "gather_start_b")
    mods = mods + token_b[0:2, 0:1]

    def weights_a(after):
        send, recv, lands, _ = _gather_forward(send_a, recv_a, lands_a, after, "gather_forward_a")
        w_in, w_out = _gather_wait(send, recv, lands, after, "gather_wait_a")
        return w_in, w_out.reshape(D_MODEL, D_MODEL)

    forwarded_b = []

    def weights_b(after):
        send, recv, lands, _ = forwarded_b
        w_in, w_out = _gather_wait(send, recv, lands, after, "gather_wait_b")
        return w_in, w_out.reshape(D_MODEL, D_MODEL)

    def forward_weights_b(after):
        forwarded_b.extend(_gather_forward(send_b, recv_b, lands_b, after, "gather_forward_b"))

    stage1, stage2 = {}, {}

    def send_grads(tag, dw_in, dw_out):
        grads = [dw_in, dw_out.reshape(N_CHIPS, D_MODEL // N_CHIPS, D_MODEL)]
        send, recv, arrays, token = _reduce_sibling_start(grads, dw_out, f"reduce_d2d_start_{tag}")
        stage1[tag] = (send, recv, arrays)
        return token

    def forward_grads(tag, after):
        send, recv, arrays = stage1[tag]
        grads, got = _reduce_sibling_wait(send, recv, arrays, after, f"reduce_d2d_wait_{tag}")
        partials = [_add_sibling_half(grads[i], got[i], dev_idx, f"reduce_add_{tag}_{i}") for i in range(2)]
        send, recv, arrays, token = _reduce_chips_start(partials, partials[1], f"reduce_ici_start_{tag}")
        stage2[tag] = (send, recv, arrays)
        return token

    stage3 = {}

    def sum_grads(tag, after):
        send, recv, arrays = stage2[tag]
        partials, lands = _reduce_chips_wait(send, recv, arrays, after, f"reduce_ici_wait_{tag}")
        totals = [_sum_partials(lands[i], partials[i], dev_idx, f"reduce_sum_{tag}_{i}") for i in range(2)]
        send, recv, totals, token = _share_halves_start(totals, totals[1], f"reduce_share_start_{tag}")
        stage3[tag] = (send, recv, totals)
        return token

    def finish_grads(tag, after):
        send, recv, totals = stage3[tag]
        return _share_halves_wait(send, recv, totals, after, f"reduce_share_wait_{tag}")

    grad_x, small = _local_step(
        x[0], loss_target[0], mods.reshape(2, 3, D_MODEL), norm_g, conv_w_full, a_conv_b, a_ln_g[0:1],
        a_ln_b[0:1], b_q_norm[0], b_k_norm[0], chip.astype(jnp.int32), own_wa_in, own_wb_in,
        weights_a, weights_b, forward_weights_b,
        functools.partial(send_grads, "b"), functools.partial(forward_grads, "b"), functools.partial(send_grads, "a"))

    ns = 3 * D_MODEL // N_CHIPS
    pad_mod = lambda dm: jnp.pad(dm.reshape(N_CHIPS, ns), ((0, 0), (0, D_MODEL - ns)))
    packed = jnp.concatenate([
        small["dnorm_g"], small["dconv_b"], small["dln_g"], small["dln_b"], small["dq_norm"], small["dk_norm"],
        small["loss_cols"], pad_mod(small["dmod0"]), pad_mod(small["dmod1"]), small["dconv_w"],
        jnp.zeros((SMALL_ROWS - 20 - CONV_WIDTH, D_MODEL), F32)], axis=0)
    send_s, recv_s, small_arrays, token_s = _small_gather_start(packed, packed)

    given = dict(norm_g=(norm_g, m_norm_g, v_norm_g), ada_w=(ada_w, m_ada_w, v_ada_w), ada_b=(ada_b, m_ada_b, v_ada_b),
                 a_w_in=(a_w_in, m_a_w_in, v_a_w_in), a_conv_w=(a_conv_w, m_a_conv_w, v_a_conv_w),
                 a_conv_b=(a_conv_b, m_a_conv_b, v_a_conv_b), a_ln_g=(a_ln_g, m_a_ln_g, v_a_ln_g),
                 a_ln_b=(a_ln_b, m_a_ln_b, v_a_ln_b), a_w_out=(a_w_out, m_a_w_out, v_a_w_out),
                 b_w_in=(b_w_in, m_b_w_in, v_b_w_in), b_q_norm=(b_q_norm, m_b_q_norm, v_b_q_norm),
                 b_k_norm=(b_k_norm, m_b_k_norm, v_b_k_norm), b_w_out=(b_w_out, m_b_w_out, v_b_w_out))
    order = ["norm_g", "ada_w", "ada_b", "a_w_in", "a_conv_w", "a_conv_b", "a_ln_g", "a_ln_b", "a_w_out", "b_w_in",
             "b_q_norm", "b_k_norm", "b_w_out"]
    outs = {}

    def update(k, g2, after=None, copy_grad=False):
        w, m, v = given[k]
        shape2 = g2.shape
        res = _adamw(w.reshape(shape2), g2, m.reshape(shape2), v.reshape(shape2), f"adamw_{k}", after, copy_grad)
        outs[k] = tuple(a.reshape(w.shape) for a in ((res[3] if copy_grad else g2), res[0], res[1], res[2]))

    token = forward_grads("a", token_s)
    token = sum_grads("b", token)
    packed, land = _small_gather_wait(send_s, recv_s, small_arrays, token)
    tot, g_ada_w, loss, qk = _reduce_small(packed, land, silu_c)
    g_b_in, g_b_out = finish_grads("b", tot)
    update("b_w_in", g_b_in, copy_grad=True)
    update("b_w_out", g_b_out, copy_grad=True)
    token = sum_grads("a", outs["b_w_in"][1])
    cw = D_MODEL // N_CHIPS
    g_small = dict(
        norm_g=tot[0:2], a_conv_b=tot[2:3], a_ln_g=tot[3:4], a_ln_b=tot[4:5],
        b_q_norm=qk[0:3], b_k_norm=qk[3:6],
        ada_b=jnp.stack([tot[12:16, :ns].reshape(3 * D_MODEL), tot[16:20, :ns].reshape(3 * D_MODEL)]),
        a_conv_w=lax.dynamic_slice(tot[20:20 + CONV_WIDTH], (0, chip * cw), (CONV_WIDTH, cw)),
    )
    update("ada_w", g_ada_w.reshape(2 * D_MODEL, ns), after=token)
    for k, g2 in g_small.items():
        update(k, g2, after=token)
    g_a_in, g_a_out = finish_grads("a", outs["ada_w"][1])
    update("a_w_in", g_a_in, copy_grad=True)
    update("a_w_out", g_a_out, copy_grad=True)
    return (loss.reshape(()), grad_x[None], *[outs[k][0] for k in order], *[outs[k][1] for k in order],
            *[outs[k][2] for k in order], *[outs[k][3] for k in order])
```

```python
import functools

import jax
import jax.numpy as jnp
from jax import lax
from jax.experimental import pallas as pl
from jax.experimental.pallas import tpu as pltpu

F32 = jnp.float32
BF16 = jnp.bfloat16

SEQ = 2048
D_MODEL = 1024
CONV_WIDTH = 31
HEAD_DIM = 64
N_HEADS = 16
DILATIONS = (1, 4, 16)
ATTN_BLOCK = 128
NORM_EPS = 1e-6
NEG_INF = -1e30
N_DEV = 8
N_CHIPS = 4

ADAM_LR = 0.001
ADAM_B1 = 0.9
ADAM_B2 = 0.999
ADAM_EPS = 1e-08
ADAM_WD = 0.01
ADAM_STEP = 10

VMEM_LIMIT_BYTES = 52 * 1024 * 1024
HALO = 32
LANES = 128
ROW_TILE = 512
MESH = pl.DeviceIdType.MESH


def _params(*sem):
    return pltpu.CompilerParams(dimension_semantics=sem or None, vmem_limit_bytes=VMEM_LIMIT_BYTES)


def _sigmoid(v):
    return 1.0 / (1.0 + jnp.exp(-v))


def _row_spec(tm, cols, col_block=0):
    return pl.BlockSpec((tm, cols), lambda i: (i, col_block))


def _vec_spec(rows, cols):
    return pl.BlockSpec((rows, cols), lambda i: (0, 0))


def _normmod(xv, g, scale, shift):
    r = lax.rsqrt(jnp.mean(xv * xv, axis=-1, keepdims=True) + NORM_EPS)
    return xv * r * g * (1.0 + scale) + shift


def _normmod_fwd(x, g, scale, shift, name):
    tm = ROW_TILE

    def body(x_ref, g_ref, sc_ref, sh_ref, h_ref, ht_ref):
        h = _normmod(x_ref[...], g_ref[...], sc_ref[...], sh_ref[...])
        h_ref[...] = h.astype(BF16)
        ht_ref[...] = h.T.astype(BF16)

    return pl.pallas_call(
        body, name=name, grid=(SEQ // tm,),
        in_specs=[_row_spec(tm, D_MODEL)] + [_vec_spec(1, D_MODEL)] * 3,
        out_specs=[_row_spec(tm, D_MODEL), pl.BlockSpec((D_MODEL, tm), lambda i: (0, i))],
        out_shape=[jax.ShapeDtypeStruct((SEQ, D_MODEL), BF16), jax.ShapeDtypeStruct((D_MODEL, SEQ), BF16)],
        compiler_params=_params("parallel"),
    )(x, g, scale, shift)


def _normmod_bwd(x, g, scale, dh_parts, dres, name, part_dilations=None, gated=None):
    tm = ROW_TILE
    n_parts = len(dh_parts)
    dils = part_dilations or (1,) * n_parts
    dh_parts = [p if d == 1 else p.reshape(d, SEQ // d, D_MODEL) for p, d in zip(dh_parts, dils)]
    n_gated = 0 if gated is None else 2

    def body(x_ref, g_ref, sc_ref, dres_ref, *rest):
        part_refs = rest[:n_parts]
        gated_refs = rest[n_parts:n_parts + n_gated]
        out_refs = rest[n_parts + n_gated:]
        dx_ref, sums_ref, nat = out_refs[0], out_refs[1], out_refs[-1]
        xv = x_ref[...]
        r = lax.rsqrt(jnp.mean(xv * xv, axis=-1, keepdims=True) + NORM_EPS)
        xn = xv * r
        dh = _load_natural(part_refs[0], nat, dils[0])
        for p, d in zip(part_refs[1:], dils[1:]):
            dh = dh + _load_natural(p, nat, d)
        gv = g_ref[...]
        one_sc = 1.0 + sc_ref[...]
        dxn = dh * (gv * one_sc)
        dx = dres_ref[...] + r * (dxn - xn * jnp.mean(dxn * xn, axis=-1, keepdims=True))
        dx_ref[...] = dx
        dhx = dh * xn
        rows = [jnp.sum(dhx, axis=0, keepdims=True) * one_sc,
                jnp.sum(dhx, axis=0, keepdims=True) * gv,
                jnp.sum(dh, axis=0, keepdims=True)]
        if gated is not None:
            gate_ref, y_ref = gated_refs
            out_refs[2][...] = (dx * gate_ref[...]).astype(BF16)
            rows.append(jnp.sum(dx * y_ref[...].astype(F32), axis=0, keepdims=True))
        sums = jnp.concatenate(rows + [jnp.zeros((8 - len(rows), D_MODEL), F32)], axis=0)

        @pl.when(pl.program_id(0) == 0)
        def _():
            sums_ref[...] = jnp.zeros_like(sums_ref)

        sums_ref[...] += sums

    gated_specs = [] if gated is None else [_vec_spec(1, D_MODEL), _row_spec(tm, D_MODEL)]
    dy_spec = [] if gated is None else [_row_spec(tm, D_MODEL)]
    dy_shape = [] if gated is None else [jax.ShapeDtypeStruct((SEQ, D_MODEL), BF16)]
    return pl.pallas_call(
        body, name=name, grid=(SEQ // tm,),
        in_specs=[_row_spec(tm, D_MODEL), _vec_spec(1, D_MODEL), _vec_spec(1, D_MODEL), _row_spec(tm, D_MODEL)]
        + [_class_spec(tm, d) for d in dils] + gated_specs,
        out_specs=[_row_spec(tm, D_MODEL), _vec_spec(8, D_MODEL)] + dy_spec,
        out_shape=[jax.ShapeDtypeStruct((SEQ, D_MODEL), F32), jax.ShapeDtypeStruct((8, D_MODEL), F32)] + dy_shape,
        scratch_shapes=[_natural_scratch(tm)],
        compiler_params=_params("arbitrary"),
    )(x, g, scale, dres, *dh_parts, *(gated or ()))


def _mm(lhs, rhs, *, tn, tile0, n_tiles, out_dtype, name, out3d=None, prev=None, transpose_lhs=False):
    mo, kc = lhs.shape[::-1] if transpose_lhs else lhs.shape
    cm = min(mo, 1024)
    tc = 256

    def body(l_ref, r_ref, *rest):
        if transpose_lhs:
            o_ref, lt_ref = rest[-2], rest[-1]

            @pl.when(pl.program_id(0) == 0)
            def _():
                for c in range(kc // tc):
                    lt_ref[:, c * tc:(c + 1) * tc] = l_ref[c * tc:(c + 1) * tc, :].astype(F32).T.astype(l_ref.dtype)
        else:
            o_ref, lt_ref = rest[-1], l_ref
        for m in range(mo // cm):
            rows = pl.ds(m * cm, cm)
            o_ref[rows, :] = jnp.dot(lt_ref[rows, :], r_ref[...], preferred_element_type=F32).astype(out_dtype)

    if rhs.ndim == 3:
        tps_r = rhs.shape[2] // tn
        r_spec = pl.BlockSpec((None, kc, tn), lambda t: ((tile0 + t) // tps_r, 0, (tile0 + t) % tps_r))
    else:
        r_spec = pl.BlockSpec((kc, tn), lambda t: (0, t))
    in_specs = [pl.BlockSpec(lhs.shape, lambda t: (0, 0)), r_spec]
    args = [lhs, rhs]
    aliases = {}
    if out3d is None:
        o_spec = pl.BlockSpec((mo, tn), lambda t: (0, t))
        o_shape = jax.ShapeDtypeStruct((mo, n_tiles * tn), out_dtype)
    else:
        j_out, ns_out = out3d
        tps_o = ns_out // tn
        o_spec = pl.BlockSpec((None, mo, tn), lambda t: ((tile0 + t) // tps_o, 0, (tile0 + t) % tps_o))
        o_shape = jax.ShapeDtypeStruct((j_out, mo, ns_out), out_dtype)
        if prev is not None:
            in_specs.append(pl.BlockSpec(memory_space=pl.ANY))
            args.append(prev)
            aliases = {2: 0}
    return pl.pallas_call(
        body, name=name, grid=(n_tiles,), in_specs=in_specs, out_specs=o_spec, out_shape=o_shape,
        input_output_aliases=aliases,
        scratch_shapes=[pltpu.VMEM((mo, kc), lhs.dtype)] if transpose_lhs else [],
        compiler_params=_params("arbitrary" if transpose_lhs else "parallel"),
    )(*args)


def _in_tiles(h_parts, w3, tile_ids, n_tiles, *, tn, total_tiles, part_of, name, prev=None):
    _, kc, ns = w3.shape
    tps = ns // tn
    cm = 1024
    n_parts = len(h_parts)

    def body(ids_ref, *rest):
        h_refs, w_ref, o_ref = rest[:n_parts], rest[n_parts], rest[-1]
        part = part_of(ids_ref[1, pl.program_id(0)])
        for g, h_ref in enumerate(h_refs):
            @pl.when(part == g)
            def _():
                for m in range(SEQ // cm):
                    rows = pl.ds(m * cm, cm)
                    o_ref[rows, :] = jnp.dot(h_ref[rows, :], w_ref[...], preferred_element_type=F32).astype(BF16)

    resident = pl.BlockSpec((SEQ, kc), lambda t, ids: (0, 0))
    in_specs = [resident] * n_parts + [
        pl.BlockSpec((None, kc, tn), lambda t, ids: (ids[0, t] // tps, 0, ids[0, t] % tps))]
    args = [*h_parts, w3]
    aliases = {}
    if prev is not None:
        in_specs.append(pl.BlockSpec(memory_space=pl.ANY))
        args.append(prev)
        aliases = {n_parts + 2: 0}
    return pl.pallas_call(
        body, name=name,
        grid_spec=pltpu.PrefetchScalarGridSpec(
            num_scalar_prefetch=1, grid=(n_tiles,), in_specs=in_specs,
            out_specs=pl.BlockSpec((SEQ, tn), lambda t, ids: (0, ids[1, t]))),
        out_shape=jax.ShapeDtypeStruct((SEQ, total_tiles * tn), BF16),
        input_output_aliases=aliases, compiler_params=_params("arbitrary"),
    )(tile_ids, *args)


def _own_first(chip, total_tiles):
    own = total_tiles // N_CHIPS
    step = jnp.arange(total_tiles, dtype=jnp.int32)
    tiles = (own * chip + step) % total_tiles
    return jnp.stack([step[:own], tiles[:own]]), jnp.stack([tiles[own:], tiles[own:]]), own


def _mm_nt(dy, w3, *, tn, tile0, n_tiles, name, after=None):
    m_rows = dy.shape[0]
    _, kc, ns = w3.shape
    tps = ns // tn
    cm = 512
    extra = [] if after is None else [after]

    def body(dy_ref, w_ref, *rest):
        o_ref, acc = rest[-2], rest[-1]
        t = pl.program_id(0)

        @pl.when(t == 0)
        def _():
            acc[...] = jnp.zeros_like(acc)

        for m in range(m_rows // cm):
            rows = pl.ds(m * cm, cm)
            acc[rows, :] += lax.dot_general(dy_ref[rows, :], w_ref[...], NT_DIMS, preferred_element_type=F32)

        @pl.when(t == n_tiles - 1)
        def _():
            o_ref[...] = acc[...].astype(BF16)

    return pl.pallas_call(
        body, name=name, grid=(n_tiles,),
        in_specs=[pl.BlockSpec((m_rows, tn), lambda t: (0, t)),
                  pl.BlockSpec((None, kc, tn), lambda t: ((tile0 + t) // tps, 0, (tile0 + t) % tps))]
        + [pl.BlockSpec(memory_space=pl.ANY)] * len(extra),
        out_specs=pl.BlockSpec((m_rows, kc), lambda t: (0, 0)),
        out_shape=jax.ShapeDtypeStruct((m_rows, kc), BF16),
        scratch_shapes=[pltpu.VMEM((m_rows, kc), F32)],
        compiler_params=_params("arbitrary"),
    )(dy, w3, *extra)


CONV_CHUNK = 16


def _shift_copies(buf, shifted):
    rows = shifted.shape[1]
    for s in range(1, 8):
        shifted[s - 1] = buf[pl.ds(s, rows), :]


def _shifted_rows(buf, shifted, offset, r0):
    s = offset % 8
    if s == 0:
        return buf[pl.ds(r0 + offset, CONV_CHUNK), :]
    return shifted[s - 1, pl.ds(r0 + (offset - s), CONV_CHUNK), :]


def _spread_taps(w_ref, taps):
    for k in range(CONV_WIDTH):
        taps[k] = jnp.broadcast_to(w_ref[k:k + 1, :], (8, D_MODEL))


def _times_tap(taps, k, rows):
    return (rows.reshape(CONV_CHUNK // 8, 8, D_MODEL) * taps[k][None]).reshape(CONV_CHUNK, D_MODEL)


def _conv_fwd(proj, conv_w, conv_b, ln_g, ln_b, name):
    tm = ROW_TILE
    hb = tm // HALO

    def body(vg_ref, halo_ref, z_ref, w_ref, b_ref, g_ref, be_ref, u5_ref, u5t_ref, u2_ref, buf, shifted, taps):
        i = pl.program_id(0)
        u1 = vg_ref[:, :D_MODEL].astype(F32) * _sigmoid(vg_ref[:, D_MODEL:].astype(F32))
        u1h = halo_ref[:, :D_MODEL].astype(F32) * _sigmoid(halo_ref[:, D_MODEL:].astype(F32))
        buf[pl.ds(0, HALO), :] = jnp.where(i > 0, u1h, 0.0)
        buf[pl.ds(HALO, tm), :] = u1
        _shift_copies(buf, shifted)
        _spread_taps(w_ref, taps)

        def chunk(ci, carry):
            r0 = pl.multiple_of(ci * CONV_CHUNK, CONV_CHUNK)
            acc = jnp.broadcast_to(b_ref[...], (CONV_CHUNK, D_MODEL))
            for k in range(CONV_WIDTH):
                acc = acc + _times_tap(taps, k, _shifted_rows(buf, shifted, HALO - (CONV_WIDTH - 1) + k, r0))
            u2_ref[pl.ds(r0, CONV_CHUNK), :] = acc
            return carry

        lax.fori_loop(0, tm // CONV_CHUNK, chunk, 0)
        acc = u2_ref[...]
        mu = jnp.mean(acc, axis=-1, keepdims=True)
        xc = acc - mu
        rstd = lax.rsqrt(jnp.mean(xc * xc, axis=-1, keepdims=True) + NORM_EPS)
        u3 = xc * rstd * g_ref[...] + be_ref[...]
        zv = z_ref[...].astype(F32)
        u5 = u3 * _sigmoid(u3) * (zv * _sigmoid(zv))
        u5_ref[...] = u5.astype(BF16)
        u5t_ref[...] = u5.T.astype(BF16)

    return pl.pallas_call(
        body, name=name, grid=(SEQ // tm,),
        in_specs=[pl.BlockSpec((tm, 2 * D_MODEL), lambda i: (i, 0)),
                  pl.BlockSpec((HALO, 2 * D_MODEL), lambda i: (jnp.maximum(i * hb - 1, 0), 0)),
                  _row_spec(tm, D_MODEL, 2),
                  _vec_spec(CONV_WIDTH, D_MODEL)] + [_vec_spec(1, D_MODEL)] * 3,
        out_specs=[_row_spec(tm, D_MODEL), pl.BlockSpec((D_MODEL, tm), lambda i: (0, i)), _row_spec(tm, D_MODEL)],
        out_shape=[jax.ShapeDtypeStruct((SEQ, D_MODEL), BF16), jax.ShapeDtypeStruct((D_MODEL, SEQ), BF16),
                   jax.ShapeDtypeStruct((SEQ, D_MODEL), F32)],
        scratch_shapes=[pltpu.VMEM((HALO + tm, D_MODEL), F32), pltpu.VMEM((7, HALO + tm - 8, D_MODEL), F32),
                        pltpu.VMEM((CONV_WIDTH, 8, D_MODEL), F32)],
        compiler_params=_params("parallel"),
    )(proj, proj, proj, conv_w, conv_b, ln_g, ln_b)


def _conv_bwd_pointwise(dy, w_out, proj, u2, ln_g, ln_b, name):
    tm = ROW_TILE

    def body(dy_ref, w_ref, z_ref, u2_ref, g_ref, be_ref, du2_ref, dz_ref, sums_ref):
        u2v = u2_ref[...]
        mu = jnp.mean(u2v, axis=-1, keepdims=True)
        xc = u2v - mu
        rstd = lax.rsqrt(jnp.mean(xc * xc, axis=-1, keepdims=True) + NORM_EPS)
        xhat = xc * rstd
        u3 = xhat * g_ref[...] + be_ref[...]
        s3 = _sigmoid(u3)
        u4 = u3 * s3
        zv = z_ref[...].astype(F32)
        sz = _sigmoid(zv)
        du5v = lax.dot_general(dy_ref[...], w_ref[...], NT_DIMS, preferred_element_type=F32)
        dz_ref[...] = du5v * u4 * (sz * (1.0 + zv * (1.0 - sz)))
        du3 = du5v * (zv * sz) * (s3 * (1.0 + u3 * (1.0 - s3)))
        dxhat = du3 * g_ref[...]
        du2 = rstd * (dxhat - jnp.mean(dxhat, axis=-1, keepdims=True)
                      - xhat * jnp.mean(dxhat * xhat, axis=-1, keepdims=True))
        du2_ref[...] = du2
        sums = jnp.concatenate([
            jnp.sum(du3 * xhat, axis=0, keepdims=True),
            jnp.sum(du3, axis=0, keepdims=True),
            jnp.sum(du2, axis=0, keepdims=True),
            jnp.zeros((5, D_MODEL), F32)], axis=0)

        @pl.when(pl.program_id(0) == 0)
        def _():
            sums_ref[...] = jnp.zeros_like(sums_ref)

        sums_ref[...] += sums

    return pl.pallas_call(
        body, name=name, grid=(SEQ // tm,),
        in_specs=[_row_spec(tm, D_MODEL), _vec_spec(D_MODEL, D_MODEL), _row_spec(tm, D_MODEL, 2),
                  _row_spec(tm, D_MODEL), _vec_spec(1, D_MODEL), _vec_spec(1, D_MODEL)],
        out_specs=[_row_spec(tm, D_MODEL), _row_spec(tm, D_MODEL), _vec_spec(8, D_MODEL)],
        out_shape=[jax.ShapeDtypeStruct((SEQ, D_MODEL), F32), jax.ShapeDtypeStruct((SEQ, D_MODEL), F32),
                   jax.ShapeDtypeStruct((8, D_MODEL), F32)],
        compiler_params=_params("arbitrary"),
    )(dy, w_out, proj, u2, ln_g, ln_b)


def _conv_bwd_taps(du2, dz, proj, conv_w, name):
    tm = ROW_TILE
    hb = tm // HALO
    n_blocks = SEQ // tm

    def body(du2_ref, dnext_ref, dz_ref, vg_ref, w_ref, dproj_ref, dw_ref, dbuf, dshift, sgbuf, ubuf, dwacc, taps):
        i = pl.program_id(0)
        _spread_taps(w_ref, taps)
        sg = _sigmoid(vg_ref[:, D_MODEL:].astype(F32))
        sgbuf[...] = sg
        ubuf[...] = vg_ref[:, :D_MODEL].astype(F32) * sg
        dbuf[pl.ds(0, tm), :] = du2_ref[...]
        dbuf[pl.ds(tm, HALO), :] = jnp.where(i < n_blocks - 1, dnext_ref[...], 0.0)
        _shift_copies(dbuf, dshift)

        @pl.when(i == 0)
        def _():
            dwacc[...] = jnp.zeros_like(dwacc)

        def chunk(ci, carry):
            r0 = pl.multiple_of(ci * CONV_CHUNK, CONV_CHUNK)
            rows = pl.ds(r0, CONV_CHUNK)
            u1c = ubuf[rows, :]
            du1 = jnp.zeros((CONV_CHUNK, D_MODEL), F32)
            for k in range(CONV_WIDTH):
                ahead = _shifted_rows(dbuf, dshift, CONV_WIDTH - 1 - k, r0)
                du1 = du1 + _times_tap(taps, k, ahead)
                prod = u1c * ahead
                dwacc[k] += prod[0:8] + prod[8:16]
            sgc = sgbuf[rows, :]
            dval = du1 * sgc
            dproj_ref[rows, 0:D_MODEL] = dval.astype(BF16)
            dproj_ref[rows, D_MODEL:2 * D_MODEL] = (
                dval * vg_ref[rows, 0:D_MODEL].astype(F32) * (1.0 - sgc)).astype(BF16)
            return carry

        lax.fori_loop(0, tm // CONV_CHUNK, chunk, 0)
        dproj_ref[:, 2 * D_MODEL:] = dz_ref[...].astype(BF16)

        @pl.when(i == n_blocks - 1)
        def _():
            for k in range(CONV_WIDTH):
                dw_ref[k:k + 1, :] = jnp.sum(dwacc[k], axis=0, keepdims=True)
            dw_ref[CONV_WIDTH:, :] = jnp.zeros((32 - CONV_WIDTH, D_MODEL), F32)

    return pl.pallas_call(
        body, name=name, grid=(n_blocks,),
        in_specs=[_row_spec(tm, D_MODEL),
                  pl.BlockSpec((HALO, D_MODEL), lambda i: (jnp.minimum((i + 1) * hb, SEQ // HALO - 1), 0)),
                  _row_spec(tm, D_MODEL),
                  pl.BlockSpec((tm, 2 * D_MODEL), lambda i: (i, 0)),
                  _vec_spec(CONV_WIDTH, D_MODEL)],
        out_specs=[_row_spec(tm, 3 * D_MODEL), _vec_spec(32, D_MODEL)],
        out_shape=[jax.ShapeDtypeStruct((SEQ, 3 * D_MODEL), BF16), jax.ShapeDtypeStruct((32, D_MODEL), F32)],
        scratch_shapes=[pltpu.VMEM((tm + HALO, D_MODEL), F32), pltpu.VMEM((7, HALO + tm - 8, D_MODEL), F32),
                        pltpu.VMEM((tm, D_MODEL), F32), pltpu.VMEM((tm, D_MODEL), F32),
                        pltpu.VMEM((CONV_WIDTH, 8, D_MODEL), F32), pltpu.VMEM((CONV_WIDTH, 8, D_MODEL), F32)],
        compiler_params=_params("arbitrary"),
    )(du2, du2, dz, proj, conv_w)


def _out_a(u5, w_out, x, gate, g1, scale1, shift1, name):
    tm = ROW_TILE
    n_d = len(DILATIONS)

    def body(u_ref, w_ref, x_ref, gate_ref, g_ref, sc_ref, sh_ref, x1_ref, y_ref, ht_ref, *rest):
        h_refs, nat = rest[:n_d], rest[-1]
        y = jnp.dot(u_ref[...], w_ref[...], preferred_element_type=F32)
        x1 = x_ref[...] + gate_ref[...] * y
        y_ref[...] = y.astype(BF16)
        x1_ref[...] = x1
        h = _normmod(x1, g_ref[...], sc_ref[...], sh_ref[...])
        ht_ref[...] = h.T.astype(BF16)
        for h_ref, d in zip(h_refs, DILATIONS):
            _store_classes(h_ref, h, nat, d)

    res = pl.pallas_call(
        body, name=name, grid=(SEQ // tm,),
        in_specs=[_row_spec(tm, D_MODEL), _vec_spec(D_MODEL, D_MODEL), _row_spec(tm, D_MODEL)]
        + [_vec_spec(1, D_MODEL)] * 4,
        out_specs=[_row_spec(tm, D_MODEL), _row_spec(tm, D_MODEL), pl.BlockSpec((D_MODEL, tm), lambda i: (0, i))]
        + [_class_spec(tm, d) for d in DILATIONS],
        out_shape=[jax.ShapeDtypeStruct((SEQ, D_MODEL), F32), jax.ShapeDtypeStruct((SEQ, D_MODEL), BF16),
                   jax.ShapeDtypeStruct((D_MODEL, SEQ), BF16)] + [_class_shape(d, BF16) for d in DILATIONS],
        scratch_shapes=[_natural_scratch(tm)],
        compiler_params=_params("parallel"),
    )(u5, w_out, x, gate, g1, scale1, shift1)
    return res[0], res[1], res[2], [a.reshape(SEQ, D_MODEL) for a in res[3:]]


def _out_b_loss(u, w_out, x1, gate, target, name):
    tm = ROW_TILE

    def body(u_ref, w_ref, x_ref, gate_ref, t_ref, e_ref, dy_ref, sums_ref):
        y = jnp.dot(u_ref[...], w_ref[...], preferred_element_type=F32)
        diff = x_ref[...] + gate_ref[...] * y - t_ref[...]
        e = diff * (1.0 / D_MODEL)
        e_ref[...] = e
        dy_ref[...] = (e * gate_ref[...]).astype(BF16)
        sums = jnp.concatenate([
            jnp.sum(e * y, axis=0, keepdims=True),
            jnp.sum(diff * diff, axis=0, keepdims=True),
            jnp.zeros((6, D_MODEL), F32)], axis=0)

        @pl.when(pl.program_id(0) == 0)
        def _():
            sums_ref[...] = jnp.zeros_like(sums_ref)

        sums_ref[...] += sums

    return pl.pallas_call(
        body, name=name, grid=(SEQ // tm,),
        in_specs=[_row_spec(tm, D_MODEL), _vec_spec(D_MODEL, D_MODEL), _row_spec(tm, D_MODEL),
                  _vec_spec(1, D_MODEL), _row_spec(tm, D_MODEL)],
        out_specs=[_row_spec(tm, D_MODEL), _row_spec(tm, D_MODEL), _vec_spec(8, D_MODEL)],
        out_shape=[jax.ShapeDtypeStruct((SEQ, D_MODEL), F32), jax.ShapeDtypeStruct((SEQ, D_MODEL), BF16),
                   jax.ShapeDtypeStruct((8, D_MODEL), F32)],
        compiler_params=_params("arbitrary"),
    )(u, w_out, x1, gate, target)


def _seg_matrix():
    r = lax.broadcasted_iota(jnp.int32, (256, 256), 0) // HEAD_DIM
    c = lax.broadcasted_iota(jnp.int32, (256, 256), 1) // HEAD_DIM
    return jnp.where(r == c, 1.0 / HEAD_DIM, 0.0).astype(BF16)


def _segmean(v, seg):
    hi = v.astype(BF16)
    lo = (v - hi.astype(F32)).astype(BF16)
    outs = []
    for c0 in range(0, D_MODEL, 256):
        outs.append(jnp.dot(hi[:, c0:c0 + 256], seg, preferred_element_type=F32)
                    + jnp.dot(lo[:, c0:c0 + 256], seg, preferred_element_type=F32))
    return jnp.concatenate(outs, axis=1)


def _qk_rstd(v, seg):
    return lax.rsqrt(_segmean(v * v, seg) + NORM_EPS)


def _qknorm_fwd(proj, group, qw, kw, seg, name):
    tm = ROW_TILE

    def body(q_in, k_in, qw_ref, kw_ref, seg_ref, q_ref, k_ref):
        segv = seg_ref[...]
        q = q_in[...].astype(F32)
        k = k_in[...].astype(F32)
        q_ref[...] = (q * _qk_rstd(q, segv) * qw_ref[...] * HEAD_DIM ** -0.5).astype(BF16)
        k_ref[...] = (k * _qk_rstd(k, segv) * kw_ref[...]).astype(BF16)

    return pl.pallas_call(
        body, name=name, grid=(SEQ // tm,),
        in_specs=[_row_spec(tm, D_MODEL, 3 * group), _row_spec(tm, D_MODEL, 3 * group + 1),
                  _vec_spec(1, D_MODEL), _vec_spec(1, D_MODEL), _vec_spec(256, 256)],
        out_specs=[_row_spec(tm, D_MODEL)] * 2,
        out_shape=[jax.ShapeDtypeStruct((SEQ, D_MODEL), BF16)] * 2,
        compiler_params=_params("parallel"),
    )(proj, proj, qw, kw, seg)


def _attn_masks(b, bpc, dilation, transposed=False):
    keys = ATTN_BLOCK if bpc == 1 else 2 * ATTN_BLOCK
    shape, q_axis = ((keys, ATTN_BLOCK), 1) if transposed else ((ATTN_BLOCK, keys), 0)
    qi = lax.broadcasted_iota(jnp.int32, shape, q_axis)
    kj = lax.broadcasted_iota(jnp.int32, shape, 1 - q_axis)
    if bpc == 1:
        steps = qi - kj
        return (steps * dilation).astype(F32), steps >= 0
    steps = qi + ATTN_BLOCK - kj
    has_prev = (b % bpc) != 0
    valid = (steps >= 0) & (steps <= ATTN_BLOCK) & (has_prev | (kj >= ATTN_BLOCK))
    return (steps * dilation).astype(F32), valid


MASKED = 1e30


def _bias_scratch(bpc):
    return pltpu.VMEM((1 if bpc == 1 else 2, N_HEADS, ATTN_BLOCK, (1 if bpc == 1 else 2) * ATTN_BLOCK), F32)


def _fill_bias(bias_ref, sl_ref, bpc, dilation):
    for variant in range(bias_ref.shape[0]):
        dist, valid = _attn_masks(variant, min(bpc, 2), dilation)
        bias_ref[variant] = jnp.where(valid[None], dist[None] * sl_ref[...], MASKED)


def _step_bias(bias_ref, b, bpc):
    if bpc == 1:
        return bias_ref[0]
    return bias_ref[jnp.where((b % bpc) != 0, 1, 0)]


def _key_tile(prev_ref, cur_ref, cols, bpc):
    if bpc == 1:
        return cur_ref[:, cols]
    return jnp.concatenate([prev_ref[:, cols], cur_ref[:, cols]], axis=0)


ATTN_HEADS_FWD = 16
ATTN_HEADS_BWD = 16
NT_DIMS = (((1,), (1,)), ((), ()))
BATCH_NT_DIMS = (((2,), (2,)), ((0,), (0,)))
BATCH_NN_DIMS = (((2,), (1,)), ((0,), (0,)))
BATCH_TN_DIMS = (((1,), (1,)), ((0,), (0,)))


def _head_stack(tile_of, heads):
    return jnp.stack([tile_of(slice(h * HEAD_DIM, (h + 1) * HEAD_DIM)) for h in range(heads)], axis=0)


def _attn_specs(heads, segment=0):
    width = heads * HEAD_DIM
    off = segment * (D_MODEL // width)
    last = SEQ // ATTN_BLOCK - 1
    cur = pl.BlockSpec((ATTN_BLOCK, width), lambda hg, b: (jnp.minimum(b, last), hg + off))
    prev = pl.BlockSpec((ATTN_BLOCK, width), lambda hg, b: (jnp.clip(b - 1, 0, last), hg + off))
    return cur, prev


def _attn_fwd(q, k, proj, group, slopes, dilation, name):
    bpc = SEQ // dilation // ATTN_BLOCK
    heads = ATTN_HEADS_FWD
    assert heads == N_HEADS
    cur, prev = _attn_specs(heads)
    v_cur, v_prev = _attn_specs(heads, segment=3 * group + 2)

    def body(sl_ref, q_ref, kp_ref, kc_ref, vp_ref, vc_ref, o_ref, lse_ref, bias_ref):
        b = pl.program_id(1)

        @pl.when(b == 0)
        def _():
            _fill_bias(bias_ref, sl_ref, bpc, dilation)

        q3 = _head_stack(lambda cols: q_ref[:, cols], heads)
        k3 = _head_stack(lambda cols: _key_tile(kp_ref, kc_ref, cols, bpc), heads)
        v3 = _head_stack(lambda cols: _key_tile(vp_ref, vc_ref, cols, bpc), heads)
        s = lax.dot_general(q3, k3, BATCH_NT_DIMS, preferred_element_type=F32)
        s = s - _step_bias(bias_ref, b, bpc)
        m = jnp.max(s, axis=-1, keepdims=True)
        p = jnp.exp(s - m)
        l = jnp.sum(p, axis=-1, keepdims=True)
        o3 = lax.dot_general(p.astype(BF16), v3, BATCH_NN_DIMS, preferred_element_type=F32) / l
        lse3 = m + jnp.log(l)
        for h in range(heads):
            o_ref[:, h * HEAD_DIM:(h + 1) * HEAD_DIM] = o3[h].astype(BF16)
        lse_ref[...] = jnp.concatenate([lse3[h] for h in range(heads)]
                                       + [jnp.zeros((ATTN_BLOCK, LANES - heads), F32)], axis=1)

    return pl.pallas_call(
        body, name=name, grid=(N_HEADS // heads, SEQ // ATTN_BLOCK),
        in_specs=[pl.BlockSpec((heads, 1, 1), lambda hg, b: (hg, 0, 0)), cur, prev, cur, v_prev, v_cur],
        out_specs=[cur, pl.BlockSpec((ATTN_BLOCK, LANES), lambda hg, b: (b, 0))],
        out_shape=[jax.ShapeDtypeStruct((SEQ, D_MODEL), BF16), jax.ShapeDtypeStruct((SEQ, LANES), F32)],
        scratch_shapes=[_bias_scratch(bpc)],
        compiler_params=_params("parallel", "arbitrary"),
    )(slopes.reshape(N_HEADS, 1, 1), q, k, k, proj, proj)


def _class_spec(tm, dilation, width=D_MODEL):
    if dilation == 1:
        return _row_spec(tm, width)
    return pl.BlockSpec((dilation, tm // dilation, width), lambda i: (0, i, 0))


def _class_shape(dilation, dtype, width=D_MODEL):
    if dilation == 1:
        return jax.ShapeDtypeStruct((SEQ, width), dtype)
    return jax.ShapeDtypeStruct((dilation, SEQ // dilation, width), dtype)


def _load_natural(in_ref, nat_ref, dilation):
    if dilation == 1:
        return in_ref[...].astype(F32)
    n = nat_ref.shape[1] // dilation
    tiles = in_ref.shape[-1] // LANES
    for r in range(dilation):
        for j in range(tiles):
            nat_ref.at[j][pl.ds(r, n, stride=dilation), :] = in_ref[r, :, j * LANES:(j + 1) * LANES].astype(F32)
    if tiles == 1:
        return nat_ref[0]
    return jnp.concatenate([nat_ref[j] for j in range(tiles)], axis=1)


def _store_classes(out_ref, value, nat_ref, dilation):
    if dilation == 1:
        out_ref[...] = value.astype(out_ref.dtype)
        return
    n = nat_ref.shape[1] // dilation
    tiles = value.shape[-1] // LANES
    for j in range(tiles):
        nat_ref[j] = value[:, j * LANES:(j + 1) * LANES]
    for r in range(dilation):
        for j in range(tiles):
            out_ref[r, :, j * LANES:(j + 1) * LANES] = (
                nat_ref.at[j][pl.ds(r, n, stride=dilation), :].astype(out_ref.dtype))


def _natural_scratch(tm):
    return pltpu.VMEM((D_MODEL // LANES, tm, LANES), F32)


def _head_selector():
    lane_head = lax.broadcasted_iota(jnp.int32, (D_MODEL, LANES), 0) // HEAD_DIM
    head = lax.broadcasted_iota(jnp.int32, (D_MODEL, LANES), 1)
    return (lane_head == head).astype(BF16)


def _dot_split(v, m01, dims):
    hi = v.astype(BF16)
    lo = (v - hi.astype(F32)).astype(BF16)
    return (lax.dot_general(hi, m01, dims, preferred_element_type=F32)
            + lax.dot_general(lo, m01, dims, preferred_element_type=F32))


def _merge_fwd(o_parts, lse_parts, z, sel, name):
    tm = ROW_TILE
    h_spec = pl.BlockSpec((tm, LANES), lambda i: (i, 0))

    def body(o0, o1, o2, l0, l1, l2, z_ref, sel_ref, u_ref, ut_ref, o_ref, lse_ref, nat):
        ls = [_load_natural(l, nat, d) for l, d in zip((l0, l1, l2), DILATIONS)]
        m = jnp.maximum(jnp.maximum(ls[0], ls[1]), ls[2])
        tot = m + jnp.log(jnp.exp(ls[0] - m) + jnp.exp(ls[1] - m) + jnp.exp(ls[2] - m))
        o = jnp.zeros((tm, D_MODEL), F32)
        for o_in, l, d in zip((o0, o1, o2), ls, DILATIONS):
            weight = _dot_split(jnp.exp(l - tot), sel_ref[...], NT_DIMS)
            o = o + weight * _load_natural(o_in, nat, d)
        zv = z_ref[...].astype(F32)
        u = o * (zv * _sigmoid(zv))
        u_ref[...] = u.astype(BF16)
        ut_ref[...] = u.T.astype(BF16)
        o_ref[...] = o.astype(BF16)
        lse_ref[...] = tot

    return pl.pallas_call(
        body, name=name, grid=(SEQ // tm,),
        in_specs=[_class_spec(tm, d) for d in DILATIONS] + [_class_spec(tm, d, LANES) for d in DILATIONS]
        + [_row_spec(tm, D_MODEL, B_Z_SEGMENT), _vec_spec(D_MODEL, LANES)],
        out_specs=[_row_spec(tm, D_MODEL), pl.BlockSpec((D_MODEL, tm), lambda i: (0, i)),
                   _row_spec(tm, D_MODEL), h_spec],
        out_shape=[jax.ShapeDtypeStruct((SEQ, D_MODEL), BF16), jax.ShapeDtypeStruct((D_MODEL, SEQ), BF16),
                   jax.ShapeDtypeStruct((SEQ, D_MODEL), BF16), jax.ShapeDtypeStruct((SEQ, LANES), F32)],
        scratch_shapes=[_natural_scratch(tm)],
        compiler_params=_params("parallel"),
    )(*o_parts, *lse_parts, z, sel)


def _merge_bwd(dy, w_out, o, lse, z, sel, name):
    tm = ROW_TILE
    n_d = len(DILATIONS)

    def body(dy_ref, w_ref, o_ref, lse_ref, z_ref, sel_ref, dz_ref, *rest):
        do_refs, delta_refs, lse_refs, nat = rest[:n_d], rest[n_d:2 * n_d], rest[2 * n_d:3 * n_d], rest[-1]
        zv = z_ref[...].astype(F32)
        sz = _sigmoid(zv)
        duv = lax.dot_general(dy_ref[...], w_ref[...], NT_DIMS, preferred_element_type=F32)
        ov = o_ref[...].astype(F32)
        do = duv * (zv * sz)
        dz_ref[...] = (duv * ov * (sz * (1.0 + zv * (1.0 - sz)))).astype(BF16)
        delta = _dot_split(do * ov, sel_ref[...], (((1,), (0,)), ((), ())))
        lv = lse_ref[...]
        for i, d in enumerate(DILATIONS):
            _store_classes(do_refs[i], do, nat, d)
            _store_classes(delta_refs[i], delta, nat, d)
            _store_classes(lse_refs[i], lv, nat, d)

    res = pl.pallas_call(
        body, name=name, grid=(SEQ // tm,),
        in_specs=[_row_spec(tm, D_MODEL), _vec_spec(D_MODEL, D_MODEL), _row_spec(tm, D_MODEL), _row_spec(tm, LANES),
                  _row_spec(tm, D_MODEL, B_Z_SEGMENT), _vec_spec(D_MODEL, LANES)],
        out_specs=[_row_spec(tm, D_MODEL)] + [_class_spec(tm, d) for d in DILATIONS]
        + [_class_spec(tm, d, LANES) for d in DILATIONS] * 2,
        out_shape=[jax.ShapeDtypeStruct((SEQ, D_MODEL), BF16)] + [_class_shape(d, BF16) for d in DILATIONS]
        + [_class_shape(d, F32, LANES) for d in DILATIONS] * 2,
        scratch_shapes=[_natural_scratch(tm)],
        compiler_params=_params("parallel"),
    )(dy, w_out, o, lse, z, sel)
    flat = lambda a: a.reshape(SEQ, a.shape[-1])
    return (res[0], [flat(a) for a in res[1:1 + n_d]], [flat(a) for a in res[1 + n_d:1 + 2 * n_d]],
            [flat(a) for a in res[1 + 2 * n_d:]])


def _attn_bwd(q, k, proj, group, do, lse, delta, slopes, dilation, name):
    bpc = SEQ // dilation // ATTN_BLOCK
    heads = ATTN_HEADS_BWD
    n_blocks = SEQ // ATTN_BLOCK
    carry = bpc > 1
    width = heads * HEAD_DIM
    cur, prev = _attn_specs(heads)
    v_cur, v_prev = _attn_specs(heads, segment=3 * group + 2)
    assert heads == N_HEADS
    per_head = pl.BlockSpec((ATTN_BLOCK, LANES), lambda hg, b: (jnp.minimum(b, n_blocks - 1), 0))
    scale = HEAD_DIM ** -0.5

    def body(sl_ref, q_ref, kp_ref, kc_ref, vp_ref, vc_ref, do_ref, lse_ref, dl_ref,
             dq_ref, dk_ref, dv_ref, *scratch):
        b = pl.program_id(1)
        if carry:
            dk_carry, dv_carry = scratch

            @pl.when(b == n_blocks)
            def _():
                dk_ref[...] = dk_carry[...].astype(BF16)
                dv_ref[...] = dv_carry[...].astype(BF16)

            @pl.when(b < n_blocks)
            def _():
                step(sl_ref, q_ref, kp_ref, kc_ref, vp_ref, vc_ref, do_ref, lse_ref, dl_ref,
                     dq_ref, dk_ref, dv_ref, dk_carry, dv_carry, b)
        else:
            step(sl_ref, q_ref, kp_ref, kc_ref, vp_ref, vc_ref, do_ref, lse_ref, dl_ref,
                 dq_ref, dk_ref, dv_ref, None, None, b)

    def step(sl_ref, q_ref, kp_ref, kc_ref, vp_ref, vc_ref, do_ref, lse_ref, dl_ref,
             dq_ref, dk_ref, dv_ref, dk_carry, dv_carry, b):
        if carry:
            @pl.when(b == 0)
            def _():
                dk_carry[...] = jnp.zeros_like(dk_carry)
                dv_carry[...] = jnp.zeros_like(dv_carry)

        q3 = _head_stack(lambda cols: q_ref[:, cols], heads)
        k3 = _head_stack(lambda cols: _key_tile(kp_ref, kc_ref, cols, bpc), heads)
        v3 = _head_stack(lambda cols: _key_tile(vp_ref, vc_ref, cols, bpc), heads)
        do3 = _head_stack(lambda cols: do_ref[:, cols], heads)
        lse_t = lse_ref[...].T
        dl_t = dl_ref[...].T
        lse3 = jnp.stack([lse_t[h:h + 1, :] for h in range(heads)], axis=0)
        dl3 = jnp.stack([dl_t[h:h + 1, :] for h in range(heads)], axis=0)
        s = lax.dot_general(k3, q3, BATCH_NT_DIMS, preferred_element_type=F32)
        dist, valid = _attn_masks(b, bpc, dilation, transposed=True)
        p = jnp.exp(jnp.where(valid[None], s - dist[None] * sl_ref[...], NEG_INF) - lse3)
        dp = lax.dot_general(v3, do3, BATCH_NT_DIMS, preferred_element_type=F32)
        ds = (p * (dp - dl3)).astype(BF16)
        dq3 = lax.dot_general(ds, k3, BATCH_TN_DIMS, preferred_element_type=F32) * scale
        dk3 = lax.dot_general(ds, q3, BATCH_NN_DIMS, preferred_element_type=F32)
        dv3 = lax.dot_general(p.astype(BF16), do3, BATCH_NN_DIMS, preferred_element_type=F32)
        for h in range(heads):
            cols = slice(h * HEAD_DIM, (h + 1) * HEAD_DIM)
            dq_ref[:, cols] = dq3[h].astype(BF16)
            if carry:
                dk_ref[:, cols] = (dk_carry[:, cols] + dk3[h, :ATTN_BLOCK]).astype(BF16)
                dv_ref[:, cols] = (dv_carry[:, cols] + dv3[h, :ATTN_BLOCK]).astype(BF16)
                dk_carry[:, cols] = dk3[h, ATTN_BLOCK:]
                dv_carry[:, cols] = dv3[h, ATTN_BLOCK:]
            else:
                dk_ref[:, cols] = dk3[h].astype(BF16)
                dv_ref[:, cols] = dv3[h].astype(BF16)

    kv_out = prev if carry else cur
    return pl.pallas_call(
        body, name=name, grid=(N_HEADS // heads, n_blocks + (1 if carry else 0)),
        in_specs=[pl.BlockSpec((heads, 1, 1), lambda hg, b: (hg, 0, 0)), cur, prev, cur, v_prev, v_cur,
                  cur, per_head, per_head],
        out_specs=[cur, kv_out, kv_out],
        out_shape=[jax.ShapeDtypeStruct((SEQ, D_MODEL), BF16)] * 3,
        scratch_shapes=[pltpu.VMEM((ATTN_BLOCK, width), F32)] * 2 if carry else [],
        compiler_params=_params("parallel", "arbitrary"),
    )(slopes.reshape(N_HEADS, 1, 1), q, k, k, proj, proj, do, lse, delta)


def _qknorm_bwd(proj, group, qw, kw, seg, dq, dk, dv, name):
    tm = ROW_TILE

    def body(q_in, k_in, qw_ref, kw_ref, seg_ref, dq_ref, dk_ref, dv_ref, dproj_ref, sums_ref):
        segv = seg_ref[...]
        sums = []
        for part, (raw_ref, w_ref, dn_ref) in enumerate(((q_in, qw_ref, dq_ref), (k_in, kw_ref, dk_ref))):
            raw = raw_ref[...].astype(F32)
            dn = dn_ref[...].astype(F32)
            r = _qk_rstd(raw, segv)
            xhat = raw * r
            gq = dn * w_ref[...]
            draw = r * (gq - xhat * _segmean(xhat * gq, segv))
            dproj_ref[:, part * D_MODEL:(part + 1) * D_MODEL] = draw.astype(BF16)
            sums.append(jnp.sum(dn * xhat, axis=0, keepdims=True))
        dproj_ref[:, 2 * D_MODEL:] = dv_ref[...]

        @pl.when(pl.program_id(0) == 0)
        def _():
            sums_ref[...] = jnp.zeros_like(sums_ref)

        sums_ref[...] += jnp.concatenate(sums + [jnp.zeros((6, D_MODEL), F32)], axis=0)

    return pl.pallas_call(
        body, name=name, grid=(SEQ // tm,),
        in_specs=[_row_spec(tm, D_MODEL, 3 * group), _row_spec(tm, D_MODEL, 3 * group + 1),
                  _vec_spec(1, D_MODEL), _vec_spec(1, D_MODEL), _vec_spec(256, 256)] + [_row_spec(tm, D_MODEL)] * 3,
        out_specs=[_row_spec(tm, 3 * D_MODEL), _vec_spec(8, D_MODEL)],
        out_shape=[jax.ShapeDtypeStruct((SEQ, 3 * D_MODEL), BF16), jax.ShapeDtypeStruct((8, D_MODEL), F32)],
        compiler_params=_params("arbitrary"),
    )(proj, proj, qw, kw, seg, dq, dk, dv)


B_TN = 512
B_GROUP_TILES = 3 * D_MODEL // B_TN
B_Z_TILE0 = 3 * B_GROUP_TILES
B_Z_TILES = D_MODEL // B_TN
B_TILES = B_Z_TILE0 + B_Z_TILES
B_Z_SEGMENT = 3 * len(DILATIONS)


def _local_step(x, target, mods, norm_g, conv_w, conv_b, ln_g, ln_b, q_norm, k_norm, chip, own_wa_in, own_wb_in,
                weights_a, weights_b, forward_weights_b, send_grads_b, forward_grads_b, send_grads_a):
    row = lambda a, i: a[i:i + 1]
    shift0, scale0, gate0 = row(mods[0], 0), row(mods[0], 1), row(mods[0], 2)
    shift1, scale1, gate1 = row(mods[1], 0), row(mods[1], 1), row(mods[1], 2)
    g0, g1 = row(norm_g, 0), row(norm_g, 1)
    seg = _seg_matrix()
    slopes = jnp.exp2(-8.0 * jnp.arange(1, N_HEADS + 1, dtype=F32) / N_HEADS)
    qw = [jnp.tile(q_norm[g:g + 1], (1, N_HEADS)) for g in range(3)]
    kw = [jnp.tile(k_norm[g:g + 1], (1, N_HEADS)) for g in range(3)]

    h0, h0t = _normmod_fwd(x, g0, scale0, shift0, "prenorm0")
    nsa = own_wa_in.shape[2]
    tiles_a = dict(tn=nsa, total_tiles=N_CHIPS, part_of=lambda tile: 0)
    own_ids, rest_ids, own_tiles = _own_first(chip, N_CHIPS)
    proj_a = _in_tiles([h0], own_wa_in, own_ids, own_tiles, name="a_in_own", **tiles_a)
    wa_in, wa_out = weights_a(proj_a)
    ja = wa_in.shape[0]
    proj_a = _in_tiles([h0], wa_in, rest_ids, N_CHIPS - own_tiles, name="a_in_rest", prev=proj_a, **tiles_a)
    u5, u5t, u2 = _conv_fwd(proj_a, conv_w, conv_b, ln_g, ln_b, "a_conv")
    x1, y_a, h1t, h1c = _out_a(u5, wa_out, x, gate0, g1, scale1, shift1, "a_out")

    tiles_b = dict(tn=B_TN, total_tiles=B_TILES,
                   part_of=lambda tile: jnp.where(tile >= B_Z_TILE0, 0, tile // B_GROUP_TILES))
    own_ids, rest_ids, own_tiles = _own_first(chip, B_TILES)
    proj_b = _in_tiles(h1c, own_wb_in, own_ids, own_tiles, name="b_in_own", **tiles_b)
    forward_weights_b(proj_b)
    wb_in, wb_out = weights_b(proj_b)
    jb, _, nsb = wb_in.shape
    proj_b = _in_tiles(h1c, wb_in, rest_ids, B_TILES - own_tiles, name="b_in_rest", prev=proj_b, **tiles_b)
    h1 = h1c[0]
    qkv, o_parts, lse_parts = [], [], []
    for g, d in enumerate(DILATIONS):
        qn, kn = _qknorm_fwd(proj_b, g, qw[g], kw[g], seg, f"b_qknorm_g{g}")
        og, lg = _attn_fwd(qn, kn, proj_b, g, slopes, d, f"b_attn_g{g}")
        qkv.append((qn, kn))
        o_parts.append(og if d == 1 else og.reshape(d, SEQ // d, D_MODEL))
        lse_parts.append(lg if d == 1 else lg.reshape(d, SEQ // d, LANES))
    sel = _head_selector()
    u_b, u_bt, o_b, lse_b = _merge_fwd(o_parts, lse_parts, proj_b, sel, "b_merge")
    e, dy_b, sums_loss = _out_b_loss(u_b, wb_out, x1, gate1, target, "b_out_loss")

    dwb_out = _mm(u_bt, dy_b, tn=D_MODEL, tile0=0, n_tiles=1, out_dtype=BF16, name="b_dwout")
    dz_b, do_c, delta_c, lse_c = _merge_bwd(dy_b, wb_out, o_b, lse_b, proj_b, sel, "b_merge_bwd")
    dwb_in = _mm(h1t, dz_b, tn=B_TN, tile0=B_Z_TILE0, n_tiles=B_Z_TILES, out_dtype=BF16, name="b_dwin_z",
                 out3d=(jb, nsb))
    dh1_parts = [_mm_nt(dz_b, wb_in, tn=B_TN, tile0=B_Z_TILE0, n_tiles=B_Z_TILES, name="b_dh_z")]
    qk_sums = []
    for g, d in enumerate(DILATIONS):
        qn, kn = qkv[g]
        dq, dk, dv = _attn_bwd(qn, kn, proj_b, g, do_c[g], lse_c[g], delta_c[g], slopes, d, f"b_attn_bwd_g{g}")
        dproj, sums_qk = _qknorm_bwd(proj_b, g, qw[g], kw[g], seg, dq, dk, dv, f"b_qknorm_bwd_g{g}")
        qk_sums.append(sums_qk)
        dwb_in = _mm(h1t if d == 1 else h1c[g], dproj, tn=B_TN, tile0=g * B_GROUP_TILES, n_tiles=B_GROUP_TILES,
                     out_dtype=BF16, name=f"b_dwin_g{g}", out3d=(jb, nsb), prev=dwb_in, transpose_lhs=d != 1)
        dh = _mm_nt(dproj, wb_in, tn=B_TN, tile0=g * B_GROUP_TILES, n_tiles=B_GROUP_TILES, name=f"b_dh_g{g}")
        dh1_parts.append(dh)
    token = send_grads_b(dwb_in, dwb_out)
    dx1, sums_n1, dy_a = _normmod_bwd(x1, g1, scale1 + token[0:1, 0:1], dh1_parts, e, "prenorm1_bwd",
                                      part_dilations=(1,) + DILATIONS, gated=(gate0, y_a))
    token = forward_grads_b(dx1)

    dwa_out = _mm(u5t, dy_a, tn=D_MODEL, tile0=0, n_tiles=1, out_dtype=BF16, name="a_dwout")
    du2, dz_a, sums_ln = _conv_bwd_pointwise(dy_a, wa_out, proj_a, u2, ln_g + token[0:1, 0:1], ln_b,
                                             "a_conv_bwd_pw")
    dproj_a, dconv_w = _conv_bwd_taps(du2, dz_a, proj_a, conv_w, "a_conv_bwd_taps")
    dwa_in = _mm(h0t, dproj_a, tn=nsa, tile0=0, n_tiles=ja, out_dtype=BF16, name="a_dwin", out3d=(ja, nsa))
    token = send_grads_a(dwa_in, dwa_out)
    dh0 = _mm_nt(dproj_a, wa_in, tn=nsa, tile0=0, n_tiles=ja, name="a_dh", after=token)
    grad_x, sums_n0 = _normmod_bwd(x, g0, scale0, [dh0], dx1, "prenorm0_bwd")

    small = dict(
        dnorm_g=jnp.concatenate([sums_n0[0:1], sums_n1[0:1]], axis=0),
        dmod0=jnp.concatenate([sums_n0[2:3], sums_n0[1:2], sums_n1[3:4]], axis=0),
        dmod1=jnp.concatenate([sums_n1[2:3], sums_n1[1:2], sums_loss[0:1]], axis=0),
        dln_g=sums_ln[0:1], dln_b=sums_ln[1:2], dconv_b=sums_ln[2:3],
        dconv_w=dconv_w[:CONV_WIDTH],
        dq_norm=jnp.concatenate([s[0:1] for s in qk_sums], axis=0),
        dk_norm=jnp.concatenate([s[1:2] for s in qk_sums], axis=0),
        loss_cols=sums_loss[1:2],
    )
    return grad_x, small


def _adamw(w, g, m, v, name, after=None, copy_grad=False):
    rows, cols = w.shape
    tr = rows if rows <= 128 else (256 if cols <= D_MODEL else 128)
    c1 = 1.0 / (1.0 - ADAM_B1 ** ADAM_STEP)
    c2 = 1.0 / (1.0 - ADAM_B2 ** ADAM_STEP)
    extra = [] if after is None else [after]
    n_out = 4 if copy_grad else 3

    def body(w_ref, g_ref, m_ref, v_ref, *rest):
        d_ref, mo_ref, vo_ref = rest[len(extra):len(extra) + 3]
        gv = g_ref[...]
        if copy_grad:
            rest[-1][...] = gv
        mn = ADAM_B1 * m_ref[...] + (1.0 - ADAM_B1) * gv
        vn = ADAM_B2 * v_ref[...] + (1.0 - ADAM_B2) * (gv * gv)
        mo_ref[...] = mn
        vo_ref[...] = vn
        d_ref[...] = -ADAM_LR * ((mn * c1) / (jnp.sqrt(vn * c2) + ADAM_EPS) + ADAM_WD * w_ref[...])

    spec = pl.BlockSpec((tr, cols), lambda i: (i, 0))
    return pl.pallas_call(
        body, name=name, grid=(rows // tr,),
        in_specs=[spec] * 4 + [pl.BlockSpec(memory_space=pl.ANY)] * len(extra), out_specs=[spec] * n_out,
        out_shape=[jax.ShapeDtypeStruct((rows, cols), F32)] * n_out,
        compiler_params=_params("parallel"),
    )(w, g, m, v, *extra)


def _cast_into_slot(w, chip_idx, name, keep_own=False, after=None):
    rows, cols = w.shape
    tr = 256
    extra = [] if after is None else [after]

    def body(ch_ref, w_ref, *rest):
        wb = w_ref[...].astype(BF16)
        for o_ref in rest[len(extra):]:
            o_ref[...] = wb

    slot_spec = pl.BlockSpec((None, tr, cols), lambda i, ch: (ch[0], i, 0))
    own_spec = pl.BlockSpec((None, tr, cols), lambda i, ch: (0, i, 0))
    res = pl.pallas_call(
        body, name=name,
        grid_spec=pltpu.PrefetchScalarGridSpec(
            num_scalar_prefetch=1, grid=(rows // tr,),
            in_specs=[pl.BlockSpec((tr, cols), lambda i, ch: (i, 0))] + [pl.BlockSpec(memory_space=pl.ANY)] * len(extra),
            out_specs=[slot_spec, own_spec] if keep_own else [slot_spec]),
        out_shape=[jax.ShapeDtypeStruct((N_CHIPS, rows, cols), BF16)]
        + ([jax.ShapeDtypeStruct((1, rows, cols), BF16)] if keep_own else []),
        compiler_params=_params("parallel"),
    )(chip_idx, w, *extra)
    return tuple(res) if keep_own else res[0]


def _position():
    x, y, c = lax.axis_index("x"), lax.axis_index("y"), lax.axis_index("c")
    return x, y, c


def _xor_peer(x, y, c, k):
    return (x ^ ((k >> 2) & 1), y ^ ((k >> 1) & 1), c ^ (k & 1))


def _chip_peer(x, y, k):
    return (x ^ ((k >> 1) & 1), y ^ (k & 1))


def _ada_forward(c_row, ada_w, ada_b, conv_w, after=()):
    ns = ada_w.shape[2]
    cw = conv_w.shape[1]

    def body(c_ref, w_ref, b_ref, cv_ref, *rest):
        (mod_ref, sc_ref, cvo_ref, c_all, mp, parts, cv_parts,
         send1, recv1, send2, recv2, send3, recv3) = rest[len(after):]
        x, y, c = _position()
        me = 4 * x + 2 * y + c
        chip = 2 * x + y

        def c_copy(k):
            return pltpu.make_async_remote_copy(
                src_ref=c_all.at[me], dst_ref=c_all.at[me], send_sem=send1.at[k - 1], recv_sem=recv1.at[k - 1],
                device_id=_xor_peer(x, y, c, k), device_id_type=MESH)

        def cv_copy(k):
            px, py = _chip_peer(x, y, k)
            return pltpu.make_async_remote_copy(
                src_ref=cv_parts.at[chip], dst_ref=cv_parts.at[chip], send_sem=send3.at[k - 1],
                recv_sem=recv3.at[k - 1], device_id=(px, py, c), device_id_type=MESH)

        c_all[me] = c_ref[...]
        cv_parts[chip] = cv_ref[...]
        for k in range(1, N_DEV):
            c_copy(k).start()
        for k in range(1, N_CHIPS):
            cv_copy(k).start()
        for k in range(1, N_DEV):
            c_copy(k).wait_recv()
        cv = jnp.concatenate([c_all[i] for i in range(N_DEV)], axis=0)
        sc = cv * _sigmoid(cv)
        sc_ref[...] = sc
        for l in range(2):
            res = jnp.dot(sc, w_ref[l], preferred_element_type=F32, precision=lax.Precision.HIGHEST)
            for i in range(N_DEV):
                mp[i, l:l + 1, :] = res[i:i + 1, :]

        def mod_copy(k):
            px, py = _chip_peer(x, y, k)
            return pltpu.make_async_remote_copy(
                src_ref=mp.at[4 * px + 2 * py + c], dst_ref=parts.at[chip], send_sem=send2.at[k - 1],
                recv_sem=recv2.at[k - 1], device_id=(px, py, c), device_id_type=MESH)

        for k in range(1, N_CHIPS):
            mod_copy(k).start()
        parts[chip] = mp[me]
        for k in range(1, N_CHIPS):
            mod_copy(k).wait_recv()
            cv_copy(k).wait_recv()
        mod_ref[...] = jnp.concatenate([parts[j] for j in range(N_CHIPS)], axis=1) + b_ref[...]
        cvo_ref[...] = jnp.concatenate([cv_parts[j] for j in range(N_CHIPS)], axis=1)
        for k in range(1, N_DEV):
            c_copy(k).wait_send()
        for k in range(1, N_CHIPS):
            mod_copy(k).wait_send()
            cv_copy(k).wait_send()

    vm = pl.BlockSpec(memory_space=pltpu.VMEM)
    return pl.pallas_call(
        body, name="ada_forward",
        in_specs=[vm] * 4 + [pl.BlockSpec(memory_space=pl.ANY)] * len(after), out_specs=[vm] * 3,
        out_shape=[jax.ShapeDtypeStruct((2, 3 * D_MODEL), F32), jax.ShapeDtypeStruct((N_DEV, D_MODEL), F32),
                   jax.ShapeDtypeStruct((CONV_WIDTH, N_CHIPS * cw), F32)],
        scratch_shapes=[pltpu.VMEM((N_DEV, 1, D_MODEL), F32), pltpu.VMEM((N_DEV, 2, ns), F32),
                        pltpu.VMEM((N_CHIPS, 2, ns), F32), pltpu.VMEM((N_CHIPS, CONV_WIDTH, cw), F32),
                        pltpu.SemaphoreType.DMA((N_DEV - 1,)), pltpu.SemaphoreType.DMA((N_DEV - 1,)),
                        pltpu.SemaphoreType.DMA((N_CHIPS - 1,)), pltpu.SemaphoreType.DMA((N_CHIPS - 1,)),
                        pltpu.SemaphoreType.DMA((N_CHIPS - 1,)), pltpu.SemaphoreType.DMA((N_CHIPS - 1,))],
        compiler_params=pltpu.CompilerParams(vmem_limit_bytes=VMEM_LIMIT_BYTES),
    )(c_row, ada_w, ada_b, conv_w, *after)


HBM_SPEC = pl.BlockSpec(memory_space=pltpu.HBM)
ANY_SPEC = pl.BlockSpec(memory_space=pl.ANY)
SEM_SPEC = pl.BlockSpec(memory_space=pltpu.SEMAPHORE)
SPLIT_PARAMS = dict(compiler_params=pltpu.CompilerParams(has_side_effects=pltpu.SideEffectType.DATAFLOW_SIDE_EFFECTING))
TOKEN = jax.ShapeDtypeStruct((8, 128), F32)


def _hbm(arrays):
    return [pltpu.with_memory_space_constraint(a, pltpu.HBM) for a in arrays]


def _hbm_like(arrays):
    return [pltpu.HBM(a.shape, a.dtype) for a in arrays]


def _gather_start(lands, after, name):
    n = len(lands)

    def body(*refs):
        ins = refs[:n]
        send, recv = refs[n + 1], refs[n + 2]
        x, y, c = _position()
        chip = 2 * x + y
        for t in range(n):
            rh = ins[t].shape[1] // 2
            for k in range(1, N_CHIPS):
                px, py = _chip_peer(x, y, k)
                block = ins[t].at[chip, pl.ds(c * rh, rh)]
                pltpu.make_async_remote_copy(
                    src_ref=block, dst_ref=block, send_sem=send.at[3 * t + k - 1], recv_sem=recv.at[3 * t + k - 1],
                    device_id=(px, py, c), device_id_type=MESH).start()
        refs[-1][...] = jnp.zeros(TOKEN.shape, F32)

    res = pl.pallas_call(
        body, name=name, in_specs=[HBM_SPEC] * n + [ANY_SPEC],
        out_specs=(SEM_SPEC, SEM_SPEC, *[HBM_SPEC] * n, pl.BlockSpec(memory_space=pltpu.VMEM)),
        out_shape=(pltpu.SemaphoreType.DMA((3 * n,)), pltpu.SemaphoreType.DMA((3 * n,)), *_hbm_like(lands), TOKEN),
        input_output_aliases={t: 2 + t for t in range(n)}, **SPLIT_PARAMS,
    )(*_hbm(lands), after)
    return res[0], res[1], list(res[2:2 + n]), res[-1]


def _gather_forward(send, recv, lands, after, name):
    n = len(lands)

    def body(*refs):
        ins = refs[:n]
        send1, recv1 = refs[n], refs[n + 1]
        send2, recv2 = refs[n + 3], refs[n + 4]
        x, y, c = _position()
        chip = 2 * x + y
        for t in range(n):
            rh = ins[t].shape[1] // 2
            half = pl.ds(c * rh, rh)
            for k in range(1, N_CHIPS):
                px, py = _chip_peer(x, y, k)
                s = 3 * t + k - 1
                got = ins[t].at[2 * px + py, half]
                cp = pltpu.make_async_remote_copy(
                    src_ref=ins[t].at[chip, half], dst_ref=got, send_sem=send1.at[s], recv_sem=recv1.at[s],
                    device_id=(px, py, c), device_id_type=MESH)
                cp.wait_send()
                cp.wait_recv()
                pltpu.make_async_remote_copy(
                    src_ref=got, dst_ref=got, send_sem=send2.at[s], recv_sem=recv2.at[s],
                    device_id=(x, y, 1 - c), device_id_type=MESH).start()
        refs[-1][...] = jnp.zeros(TOKEN.shape, F32)

    res = pl.pallas_call(
        body, name=name, in_specs=[HBM_SPEC] * n + [SEM_SPEC, SEM_SPEC, ANY_SPEC],
        out_specs=(SEM_SPEC, SEM_SPEC, *[HBM_SPEC] * n, pl.BlockSpec(memory_space=pltpu.VMEM)),
        out_shape=(pltpu.SemaphoreType.DMA((3 * n,)), pltpu.SemaphoreType.DMA((3 * n,)), *_hbm_like(lands), TOKEN),
        input_output_aliases={t: 2 + t for t in range(n)}, **SPLIT_PARAMS,
    )(*lands, send, recv, after)
    return res[0], res[1], list(res[2:2 + n]), res[-1]


def _gather_wait(send, recv, lands, after, name):
    n = len(lands)

    def body(*refs):
        ins = refs[:n]
        send_ref, recv_ref = refs[n], refs[n + 1]
        x, y, c = _position()
        for t in range(n):
            rh = ins[t].shape[1] // 2
            for k in range(1, N_CHIPS):
                px, py = _chip_peer(x, y, k)
                cp = pltpu.make_async_remote_copy(
                    src_ref=ins[t].at[2 * px + py, pl.ds(c * rh, rh)],
                    dst_ref=ins[t].at[2 * px + py, pl.ds((1 - c) * rh, rh)], send_sem=send_ref.at[3 * t + k - 1],
                    recv_sem=recv_ref.at[3 * t + k - 1], device_id=(x, y, 1 - c), device_id_type=MESH)
                cp.wait_send()
                cp.wait_recv()

    res = pl.pallas_call(
        body, name=name, in_specs=[HBM_SPEC] * n + [SEM_SPEC, SEM_SPEC, ANY_SPEC], out_specs=[HBM_SPEC] * n,
        out_shape=_hbm_like(lands), input_output_aliases={t: t for t in range(n)}, **SPLIT_PARAMS,
    )(*lands, send, recv, after)
    return list(res)


def _split_start(name, arrays, n_sems, after, issue):
    m = len(arrays)

    def body(*refs):
        issue(refs[:m], refs[m + 1], refs[m + 2])
        refs[-1][...] = jnp.zeros(TOKEN.shape, F32)

    res = pl.pallas_call(
        body, name=name, in_specs=[HBM_SPEC] * m + [ANY_SPEC],
        out_specs=(SEM_SPEC, SEM_SPEC, *[HBM_SPEC] * m, pl.BlockSpec(memory_space=pltpu.VMEM)),
        out_shape=(pltpu.SemaphoreType.DMA((n_sems,)), pltpu.SemaphoreType.DMA((n_sems,)), *_hbm_like(arrays), TOKEN),
        input_output_aliases={t: 2 + t for t in range(m)}, **SPLIT_PARAMS,
    )(*_hbm(arrays), after)
    return res[0], res[1], list(res[2:2 + m]), res[-1]


def _split_wait(name, arrays, send, recv, after, await_all):
    m = len(arrays)

    def body(*refs):
        await_all(refs[:m], refs[m], refs[m + 1])

    res = pl.pallas_call(
        body, name=name, in_specs=[HBM_SPEC] * m + [SEM_SPEC, SEM_SPEC, ANY_SPEC], out_specs=[HBM_SPEC] * m,
        out_shape=_hbm_like(arrays), input_output_aliases={t: t for t in range(m)}, **SPLIT_PARAMS,
    )(*arrays, send, recv, after)
    return list(res)


def _sibling_copies(refs, send, recv, n):
    x, y, c = _position()
    cps = []
    for t in range(n):
        rh = refs[t].shape[1] // 2
        cps.append(pltpu.make_async_remote_copy(
            src_ref=refs[t].at[pl.ds(0, N_CHIPS), pl.ds((1 - c) * rh, rh)], dst_ref=refs[n + t],
            send_sem=send.at[t], recv_sem=recv.at[t], device_id=(x, y, 1 - c), device_id_type=MESH))
    return cps


def _reduce_sibling_start(grads, after, name):
    n = len(grads)
    lands = [lax.empty((N_CHIPS, g.shape[1] // 2, g.shape[2]), BF16) for g in grads]

    def issue(refs, send, recv):
        for cp in _sibling_copies(refs, send, recv, n):
            cp.start()

    return _split_start(name, list(grads) + lands, n, after, issue)


def _reduce_sibling_wait(send, recv, arrays, after, name):
    n = len(arrays) // 2

    def await_all(refs, send_ref, recv_ref):
        for cp in _sibling_copies(refs, send_ref, recv_ref, n):
            cp.wait_send()
            cp.wait_recv()

    res = _split_wait(name, arrays, send, recv, after, await_all)
    return res[:n], res[n:]


def _add_sibling_half(grad, got, dev_idx, name):
    j, r, cols = grad.shape
    rh = r // 2
    tr = rh
    nb = rh // tr

    def body(idx_ref, g_ref, got_ref, out_ref):
        out_ref[...] = (g_ref[...].astype(F32) + got_ref[...].astype(F32)).astype(BF16)

    return pl.pallas_call(
        body, name=name,
        grid_spec=pltpu.PrefetchScalarGridSpec(
            num_scalar_prefetch=1, grid=(j, nb),
            in_specs=[pl.BlockSpec((None, tr, cols), lambda jj, i, idx: (jj, idx[2] * nb + i, 0)),
                      pl.BlockSpec((None, tr, cols), lambda jj, i, idx: (jj, i, 0))],
            out_specs=pl.BlockSpec((None, tr, cols), lambda jj, i, idx: (jj, i, 0))),
        out_shape=jax.ShapeDtypeStruct((j, rh, cols), BF16),
        compiler_params=_params("parallel", "parallel"),
    )(dev_idx, grad, got)


def _chip_copies(refs, send, recv, n, receiving):
    x, y, c = _position()
    chip = 2 * x + y
    cps = []
    for t in range(n):
        for k in range(1, N_CHIPS):
            px, py = _chip_peer(x, y, k)
            cps.append(pltpu.make_async_remote_copy(
                src_ref=refs[t].at[2 * px + py], dst_ref=refs[n + t].at[2 * px + py if receiving else chip],
                send_sem=send.at[3 * t + k - 1], recv_sem=recv.at[3 * t + k - 1],
                device_id=(px, py, c), device_id_type=MESH))
    return cps


def _reduce_chips_start(partials, after, name):
    n = len(partials)
    lands = [lax.empty(p.shape, BF16) for p in partials]

    def issue(refs, send, recv):
        for cp in _chip_copies(refs, send, recv, n, False):
            cp.start()

    return _split_start(name, list(partials) + lands, 3 * n, after, issue)


def _reduce_chips_wait(send, recv, arrays, after, name):
    n = len(arrays) // 2

    def await_all(refs, send_ref, recv_ref):
        for cp in _chip_copies(refs, send_ref, recv_ref, n, True):
            cp.wait_send()
            cp.wait_recv()

    res = _split_wait(name, arrays, send, recv, after, await_all)
    return res[:n], res[n:]


def _sum_partials(land, partial, dev_idx, name):
    _, rh, cols = land.shape
    tr = min(rh, 256)
    nb = rh // tr

    def body(idx_ref, l_ref, p_ref, o_ref):
        chip = idx_ref[1]
        acc = jnp.where(chip == 0, p_ref[...], l_ref[0]).astype(F32)
        for s in range(1, N_CHIPS):
            acc = acc + jnp.where(chip == s, p_ref[...], l_ref[s]).astype(F32)
        o_ref[...] = acc

    return pl.pallas_call(
        body, name=name,
        grid_spec=pltpu.PrefetchScalarGridSpec(
            num_scalar_prefetch=1, grid=(nb,),
            in_specs=[pl.BlockSpec((N_CHIPS, tr, cols), lambda i, idx: (0, i, 0)),
                      pl.BlockSpec((None, tr, cols), lambda i, idx: (idx[1], i, 0))],
            out_specs=pl.BlockSpec((tr, cols), lambda i, idx: (idx[2] * nb + i, 0))),
        out_shape=jax.ShapeDtypeStruct((2 * rh, cols), F32), compiler_params=_params("parallel"),
    )(dev_idx, land, partial)


def _half_copies(refs, send, recv, receiving):
    x, y, c = _position()
    cps = []
    for t, ref in enumerate(refs):
        rh = ref.shape[0] // 2
        cps.append(pltpu.make_async_remote_copy(
            src_ref=ref.at[pl.ds(c * rh, rh)], dst_ref=ref.at[pl.ds(((1 - c) if receiving else c) * rh, rh)],
            send_sem=send.at[t], recv_sem=recv.at[t], device_id=(x, y, 1 - c), device_id_type=MESH))
    return cps


def _share_halves_start(totals, after, name):
    def issue(refs, send, recv):
        for cp in _half_copies(refs, send, recv, False):
            cp.start()

    return _split_start(name, list(totals), len(totals), after, issue)


def _share_halves_wait(send, recv, totals, after, name):
    def await_all(refs, send_ref, recv_ref):
        for cp in _half_copies(refs, send_ref, recv_ref, True):
            cp.wait_send()
            cp.wait_recv()

    return _split_wait(name, totals, send, recv, after, await_all)


SMALL_ROWS = 56


def _small_copies(refs, send, recv, receiving):
    x, y, c = _position()
    me = 4 * x + 2 * y + c
    cps = []
    for k in range(1, N_DEV):
        px, py, pc = _xor_peer(x, y, c, k)
        cps.append(pltpu.make_async_remote_copy(
            src_ref=refs[0], dst_ref=refs[1].at[4 * px + 2 * py + pc if receiving else me],
            send_sem=send.at[k - 1], recv_sem=recv.at[k - 1], device_id=(px, py, pc), device_id_type=MESH))
    return cps


def _small_gather_start(packed, after):
    land = lax.empty((N_DEV,) + packed.shape, F32)

    def issue(refs, send, recv):
        for cp in _small_copies(refs, send, recv, False):
            cp.start()

    return _split_start("small_gather_start", [packed, land], N_DEV - 1, after, issue)


def _small_gather_wait(send, recv, arrays, after):
    def await_all(refs, send_ref, recv_ref):
        for cp in _small_copies(refs, send_ref, recv_ref, True):
            cp.wait_send()
            cp.wait_recv()

    return _split_wait("small_gather_wait", arrays, send, recv, after, await_all)


def _reduce_small(packed, land, silu_c):
    ns = 3 * D_MODEL // N_CHIPS

    def body(p_ref, land_ref, sc_ref, tot_ref, gw_ref, loss_ref, qk_ref, allp):
        x, y, c = _position()
        me = 4 * x + 2 * y + c
        chip = 2 * x + y
        for i in range(N_DEV):
            allp[i] = jnp.where(me == i, p_ref[...], land_ref[i])
        tot = allp[0]
        for i in range(1, N_DEV):
            tot = tot + allp[i]
        tot_ref[...] = tot
        loss_ref[...] = jnp.sum(tot[11:12, :], axis=1, keepdims=True) * (0.5 / D_MODEL)
        fold = tot[5:11, 0:HEAD_DIM]
        for h in range(1, N_HEADS):
            fold = fold + tot[5:11, h * HEAD_DIM:(h + 1) * HEAD_DIM]
        qk_ref[...] = jnp.concatenate([fold, jnp.zeros((2, HEAD_DIM), F32)], axis=0)
        sct = sc_ref[...].T
        rc = 64
        for l in range(2):
            dms = [allp[i, pl.ds(12 + 4 * l + chip, 1), :][:, :ns] for i in range(N_DEV)]
            for r0 in range(0, D_MODEL, rc):
                acc = sct[r0:r0 + rc, 0:1] * dms[0]
                for i in range(1, N_DEV):
                    acc = acc + sct[r0:r0 + rc, i:i + 1] * dms[i]
                gw_ref[l, r0:r0 + rc, :] = acc

    vm = pl.BlockSpec(memory_space=pltpu.VMEM)
    return pl.pallas_call(
        body, name="reduce_small", in_specs=[vm, vm, vm], out_specs=[vm] * 4,
        out_shape=[jax.ShapeDtypeStruct((SMALL_ROWS, D_MODEL), F32), jax.ShapeDtypeStruct((2, D_MODEL, ns), F32),
                   jax.ShapeDtypeStruct((1, 1), F32), jax.ShapeDtypeStruct((8, HEAD_DIM), F32)],
        scratch_shapes=[pltpu.VMEM((N_DEV, SMALL_ROWS, D_MODEL), F32)],
        compiler_params=pltpu.CompilerParams(vmem_limit_bytes=VMEM_LIMIT_BYTES),
    )(packed, land, silu_c)


def kernel(x, c, norm_g, ada_w, ada_b, a_w_in, a_conv_w, a_conv_b, a_ln_g, a_ln_b, a_w_out, b_w_in, b_q_norm, b_k_norm, b_w_out, loss_target, m_norm_g, m_ada_w, m_ada_b, m_a_w_in, m_a_conv_w, m_a_conv_b, m_a_ln_g, m_a_ln_b, m_a_w_out, m_b_w_in, m_b_q_norm, m_b_k_norm, m_b_w_out, v_norm_g, v_ada_w, v_ada_b, v_a_w_in, v_a_conv_w, v_a_conv_b, v_a_ln_g, v_a_ln_b, v_a_w_out, v_b_w_in, v_b_q_norm, v_b_k_norm, v_b_w_out):
    chip = 2 * lax.axis_index("x") + lax.axis_index("y")
    core = lax.axis_index("c")
    chip_idx = chip.astype(jnp.int32).reshape(1)
    dev_idx = jnp.stack([2 * chip + core, chip, core]).astype(jnp.int32)

    land_a_in, own_wa_in = _cast_into_slot(a_w_in[0], chip_idx, "cast_a_w_in", keep_own=True)
    lands_a = [land_a_in, _cast_into_slot(a_w_out[0], chip_idx, "cast_a_w_out")]
    mods, silu_c, conv_w_full = _ada_forward(c, ada_w, ada_b, a_conv_w[0], after=tuple(lands_a))
    send_a, recv_a, lands_a, token_a = _gather_start(lands_a, mods, "gather_start_a")
    land_b_in, own_wb_in = _cast_into_slot(b_w_in[0], chip_idx, "cast_b_w_in", keep_own=True, after=token_a)
    lands_b = [land_b_in, _cast_into_slot(b_w_out[0], chip_idx, "cast_b_w_out", after=token_a)]
    send_b, recv_b, lands_b, token_b = _gather_start(lands_b, token_a, "gather_start_b")
    mods = mods + token_b[0:2, 0:1]

    def weights_a(after):
        send, recv, lands, _ = _gather_forward(send_a, recv_a, lands_a, after, "gather_forward_a")
        w_in, w_out = _gather_wait(send, recv, lands, after, "gather_wait_a")
        return w_in, w_out.reshape(D_MODEL, D_MODEL)

    forwarded_b = []

    def weights_b(after):
        send, recv, lands, _ = forwarded_b
        w_in, w_out = _gather_wait(send, recv, lands, after, "gather_wait_b")
        return w_in, w_out.reshape(D_MODEL, D_MODEL)

    def forward_weights_b(after):
        forwarded_b.extend(_gather_forward(send_b, recv_b, lands_b, after, "gather_forward_b"))

    stage1, stage2 = {}, {}

    def send_grads(tag, dw_in, dw_out):
        grads = [dw_in, dw_out.reshape(N_CHIPS, D_MODEL // N_CHIPS, D_MODEL)]
        send, recv, arrays, token = _reduce_sibling_start(grads, dw_out, f"reduce_d2d_start_{tag}")
        stage1[tag] = (send, recv, arrays)
        return token

    def forward_grads(tag, after):
        send, recv, arrays = stage1[tag]
        grads, got = _reduce_sibling_wait(send, recv, arrays, after, f"reduce_d2d_wait_{tag}")
        partials = [_add_sibling_half(grads[i], got[i], dev_idx, f"reduce_add_{tag}_{i}") for i in range(2)]
        send, recv, arrays, token = _reduce_chips_start(partials, partials[1], f"reduce_ici_start_{tag}")
        stage2[tag] = (send, recv, arrays)
        return token

    stage3 = {}

    def sum_grads(tag, after):
        send, recv, arrays = stage2[tag]
        partials, lands = _reduce_chips_wait(send, recv, arrays, after, f"reduce_ici_wait_{tag}")
        totals = [_sum_partials(lands[i], partials[i], dev_idx, f"reduce_sum_{tag}_{i}") for i in range(2)]
        send, recv, totals, token = _share_halves_start(totals, totals[1], f"reduce_share_start_{tag}")
        stage3[tag] = (send, recv, totals)
        return token

    def finish_grads(tag, after):
        send, recv, totals = stage3[tag]
        return _share_halves_wait(send, recv, totals, after, f"reduce_share_wait_{tag}")

    grad_x, small = _local_step(
        x[0], loss_target[0], mods.reshape(2, 3, D_MODEL), norm_g, conv_w_full, a_conv_b, a_ln_g[0:1],
        a_ln_b[0:1], b_q_norm[0], b_k_norm[0], chip.astype(jnp.int32), own_wa_in, own_wb_in,
        weights_a, weights_b, forward_weights_b,
        functools.partial(send_grads, "b"), functools.partial(forward_grads, "b"), functools.partial(send_grads, "a"))

    ns = 3 * D_MODEL // N_CHIPS
    pad_mod = lambda dm: jnp.pad(dm.reshape(N_CHIPS, ns), ((0, 0), (0, D_MODEL - ns)))
    packed = jnp.concatenate([
        small["dnorm_g"], small["dconv_b"], small["dln_g"], small["dln_b"], small["dq_norm"], small["dk_norm"],
        small["loss_cols"], pad_mod(small["dmod0"]), pad_mod(small["dmod1"]), small["dconv_w"],
        jnp.zeros((SMALL_ROWS - 20 - CONV_WIDTH, D_MODEL), F32)], axis=0)
    send_s, recv_s, small_arrays, token_s = _small_gather_start(packed, packed)

    given = dict(norm_g=(norm_g, m_norm_g, v_norm_g), ada_w=(ada_w, m_ada_w, v_ada_w), ada_b=(ada_b, m_ada_b, v_ada_b),
                 a_w_in=(a_w_in, m_a_w_in, v_a_w_in), a_conv_w=(a_conv_w, m_a_conv_w, v_a_conv_w),
                 a_conv_b=(a_conv_b, m_a_conv_b, v_a_conv_b), a_ln_g=(a_ln_g, m_a_ln_g, v_a_ln_g),
                 a_ln_b=(a_ln_b, m_a_ln_b, v_a_ln_b), a_w_out=(a_w_out, m_a_w_out, v_a_w_out),
                 b_w_in=(b_w_in, m_b_w_in, v_b_w_in), b_q_norm=(b_q_norm, m_b_q_norm, v_b_q_norm),
                 b_k_norm=(b_k_norm, m_b_k_norm, v_b_k_norm), b_w_out=(b_w_out, m_b_w_out, v_b_w_out))
    order = ["norm_g", "ada_w", "ada_b", "a_w_in", "a_conv_w", "a_conv_b", "a_ln_g", "a_ln_b", "a_w_out", "b_w_in",
             "b_q_norm", "b_k_norm", "b_w_out"]
    outs = {}

    def update(k, g2, after=None, copy_grad=False):
        w, m, v = given[k]
        shape2 = g2.shape
        res = _adamw(w.reshape(shape2), g2, m.reshape(shape2), v.reshape(shape2), f"adamw_{k}", after, copy_grad)
        outs[k] = tuple(a.reshape(w.shape) for a in ((res[3] if copy_grad else g2), res[0], res[1], res[2]))

    token = forward_grads("a", token_s)
    token = sum_grads("b", token)
    packed, land = _small_gather_wait(send_s, recv_s, small_arrays, token)
    tot, g_ada_w, loss, qk = _reduce_small(packed, land, silu_c)
    g_b_in, g_b_out = finish_grads("b", tot)
    update("b_w_in", g_b_in, copy_grad=True)
    update("b_w_out", g_b_out, copy_grad=True)
    token = sum_grads("a", outs["b_w_in"][1])
    cw = D_MODEL // N_CHIPS
    g_small = dict(
        norm_g=tot[0:2], a_conv_b=tot[2:3], a_ln_g=tot[3:4], a_ln_b=tot[4:5],
        b_q_norm=qk[0:3], b_k_norm=qk[3:6],
        ada_b=jnp.stack([tot[12:16, :ns].reshape(3 * D_MODEL), tot[16:20, :ns].reshape(3 * D_MODEL)]),
        a_conv_w=lax.dynamic_slice(tot[20:20 + CONV_WIDTH], (0, chip * cw), (CONV_WIDTH, cw)),
    )
    update("ada_w", g_ada_w.reshape(2 * D_MODEL, ns), after=token)
    for k, g2 in g_small.items():
        update(k, g2, after=token)
    g_a_in, g_a_out = finish_grads("a", outs["ada_w"][1])
    update("a_w_in", g_a_in, copy_grad=True)
    update("a_w_out", g_a_out, copy_grad=True)
    return (loss.reshape(()), grad_x[None], *[outs[k][0] for k in order], *[outs[k][1] for k in order],
            *[outs[k][2] for k in order], *[outs[k][3] for k in order])
```

```python
import functools

import jax
import jax.numpy as jnp
from jax import lax
from jax.experimental import pallas as pl
from jax.experimental.pallas import tpu as pltpu

F32 = jnp.float32
BF16 = jnp.bfloat16

SEQ = 2048
D_MODEL = 1024
CONV_WIDTH = 31
HEAD_DIM = 64
N_HEADS = 16
DILATIONS = (1, 4, 16)
ATTN_BLOCK = 128
NORM_EPS = 1e-6
NEG_INF = -1e30
N_DEV = 8
N_CHIPS = 4

ADAM_LR = 0.001
ADAM_B1 = 0.9
ADAM_B2 = 0.999
ADAM_EPS = 1e-08
ADAM_WD = 0.01
ADAM_STEP = 10

VMEM_LIMIT_BYTES = 52 * 1024 * 1024
HALO = 32
LANES = 128
ROW_TILE = 512
MESH = pl.DeviceIdType.MESH


def _params(*sem):
    return pltpu.CompilerParams(dimension_semantics=sem or None, vmem_limit_bytes=VMEM_LIMIT_BYTES)


def _sigmoid(v):
    return 1.0 / (1.0 + jnp.exp(-v))


def _row_spec(tm, cols, col_block=0):
    return pl.BlockSpec((tm, cols), lambda i: (i, col_block))


def _vec_spec(rows, cols):
    return pl.BlockSpec((rows, cols), lambda i: (0, 0))


def _normmod(xv, g, scale, shift):
    r = lax.rsqrt(jnp.mean(xv * xv, axis=-1, keepdims=True) + NORM_EPS)
    return xv * r * g * (1.0 + scale) + shift


def _normmod_fwd(x, g, scale, shift, name):
    tm = ROW_TILE

    def body(x_ref, g_ref, sc_ref, sh_ref, h_ref, ht_ref):
        h = _normmod(x_ref[...], g_ref[...], sc_ref[...], sh_ref[...])
        h_ref[...] = h.astype(BF16)
        ht_ref[...] = h.T.astype(BF16)

    return pl.pallas_call(
        body, name=name, grid=(SEQ // tm,),
        in_specs=[_row_spec(tm, D_MODEL)] + [_vec_spec(1, D_MODEL)] * 3,
        out_specs=[_row_spec(tm, D_MODEL), pl.BlockSpec((D_MODEL, tm), lambda i: (0, i))],
        out_shape=[jax.ShapeDtypeStruct((SEQ, D_MODEL), BF16), jax.ShapeDtypeStruct((D_MODEL, SEQ), BF16)],
        compiler_params=_params("parallel"),
    )(x, g, scale, shift)


def _normmod_bwd(x, g, scale, dh_parts, dres, name, part_dilations=None, gated=None):
    tm = ROW_TILE
    n_parts = len(dh_parts)
    dils = part_dilations or (1,) * n_parts
    dh_parts = [p if d == 1 else p.reshape(d, SEQ // d, D_MODEL) for p, d in zip(dh_parts, dils)]
    n_gated = 0 if gated is None else 2

    def body(x_ref, g_ref, sc_ref, dres_ref, *rest):
        part_refs = rest[:n_parts]
        gated_refs = rest[n_parts:n_parts + n_gated]
        out_refs = rest[n_parts + n_gated:]
        dx_ref, sums_ref, nat = out_refs[0], out_refs[1], out_refs[-1]
        xv = x_ref[...]
        r = lax.rsqrt(jnp.mean(xv * xv, axis=-1, keepdims=True) + NORM_EPS)
        xn = xv * r
        dh = _load_natural(part_refs[0], nat, dils[0])
        for p, d in zip(part_refs[1:], dils[1:]):
            dh = dh + _load_natural(p, nat, d)
        gv = g_ref[...]
        one_sc = 1.0 + sc_ref[...]
        dxn = dh * (gv * one_sc)
        dx = dres_ref[...] + r * (dxn - xn * jnp.mean(dxn * xn, axis=-1, keepdims=True))
        dx_ref[...] = dx
        dhx = dh * xn
        rows = [jnp.sum(dhx, axis=0, keepdims=True) * one_sc,
                jnp.sum(dhx, axis=0, keepdims=True) * gv,
                jnp.sum(dh, axis=0, keepdims=True)]
        if gated is not None:
            gate_ref, y_ref = gated_refs
            out_refs[2][...] = (dx * gate_ref[...]).astype(BF16)
            rows.append(jnp.sum(dx * y_ref[...].astype(F32), axis=0, keepdims=True))
        sums = jnp.concatenate(rows + [jnp.zeros((8 - len(rows), D_MODEL), F32)], axis=0)

        @pl.when(pl.program_id(0) == 0)
        def _():
            sums_ref[...] = jnp.zeros_like(sums_ref)

        sums_ref[...] += sums

    gated_specs = [] if gated is None else [_vec_spec(1, D_MODEL), _row_spec(tm, D_MODEL)]
    dy_spec = [] if gated is None else [_row_spec(tm, D_MODEL)]
    dy_shape = [] if gated is None else [jax.ShapeDtypeStruct((SEQ, D_MODEL), BF16)]
    return pl.pallas_call(
        body, name=name, grid=(SEQ // tm,),
        in_specs=[_row_spec(tm, D_MODEL), _vec_spec(1, D_MODEL), _vec_spec(1, D_MODEL), _row_spec(tm, D_MODEL)]
        + [_class_spec(tm, d) for d in dils] + gated_specs,
        out_specs=[_row_spec(tm, D_MODEL), _vec_spec(8, D_MODEL)] + dy_spec,
        out_shape=[jax.ShapeDtypeStruct((SEQ, D_MODEL), F32), jax.ShapeDtypeStruct((8, D_MODEL), F32)] + dy_shape,
        scratch_shapes=[_natural_scratch(tm)],
        compiler_params=_params("arbitrary"),
    )(x, g, scale, dres, *dh_parts, *(gated or ()))


def _mm(lhs, rhs, *, tn, tile0, n_tiles, out_dtype, name, out3d=None, prev=None, transpose_lhs=False):
    mo, kc = lhs.shape[::-1] if transpose_lhs else lhs.shape
    cm = min(mo, 1024)
    tc = 256

    def body(l_ref, r_ref, *rest):
        if transpose_lhs:
            o_ref, lt_ref = rest[-2], rest[-1]

            @pl.when(pl.program_id(0) == 0)
            def _():
                for c in range(kc // tc):
                    lt_ref[:, c * tc:(c + 1) * tc] = l_ref[c * tc:(c + 1) * tc, :].astype(F32).T.astype(l_ref.dtype)
        else:
            o_ref, lt_ref = rest[-1], l_ref
        for m in range(mo // cm):
            rows = pl.ds(m * cm, cm)
            o_ref[rows, :] = jnp.dot(lt_ref[rows, :], r_ref[...], preferred_element_type=F32).astype(out_dtype)

    if rhs.ndim == 3:
        tps_r = rhs.shape[2] // tn
        r_spec = pl.BlockSpec((None, kc, tn), lambda t: ((tile0 + t) // tps_r, 0, (tile0 + t) % tps_r))
    else:
        r_spec = pl.BlockSpec((kc, tn), lambda t: (0, t))
    in_specs = [pl.BlockSpec(lhs.shape, lambda t: (0, 0)), r_spec]
    args = [lhs, rhs]
    aliases = {}
    if out3d is None:
        o_spec = pl.BlockSpec((mo, tn), lambda t: (0, t))
        o_shape = jax.ShapeDtypeStruct((mo, n_tiles * tn), out_dtype)
    else:
        j_out, ns_out = out3d
        tps_o = ns_out // tn
        o_spec = pl.BlockSpec((None, mo, tn), lambda t: ((tile0 + t) // tps_o, 0, (tile0 + t) % tps_o))
        o_shape = jax.ShapeDtypeStruct((j_out, mo, ns_out), out_dtype)
        if prev is not None:
            in_specs.append(pl.BlockSpec(memory_space=pl.ANY))
            args.append(prev)
            aliases = {2: 0}
    return pl.pallas_call(
        body, name=name, grid=(n_tiles,), in_specs=in_specs, out_specs=o_spec, out_shape=o_shape,
        input_output_aliases=aliases,
        scratch_shapes=[pltpu.VMEM((mo, kc), lhs.dtype)] if transpose_lhs else [],
        compiler_params=_params("arbitrary" if transpose_lhs else "parallel"),
    )(*args)


def _in_tiles(h_parts, w3, tile_ids, n_tiles, *, tn, total_tiles, part_of, name, prev=None):
    _, kc, ns = w3.shape
    tps = ns // tn
    cm = 1024
    n_parts = len(h_parts)

    def body(ids_ref, *rest):
        h_refs, w_ref, o_ref = rest[:n_parts], rest[n_parts], rest[-1]
        part = part_of(ids_ref[1, pl.program_id(0)])
        for g, h_ref in enumerate(h_refs):
            @pl.when(part == g)
            def _():
                for m in range(SEQ // cm):
                    rows = pl.ds(m * cm, cm)
                    o_ref[rows, :] = jnp.dot(h_ref[rows, :], w_ref[...], preferred_element_type=F32).astype(BF16)

    resident = pl.BlockSpec((SEQ, kc), lambda t, ids: (0, 0))
    in_specs = [resident] * n_parts + [
        pl.BlockSpec((None, kc, tn), lambda t, ids: (ids[0, t] // tps, 0, ids[0, t] % tps))]
    args = [*h_parts, w3]
    aliases = {}
    if prev is not None:
        in_specs.append(pl.BlockSpec(memory_space=pl.ANY))
        args.append(prev)
        aliases = {n_parts + 2: 0}
    return pl.pallas_call(
        body, name=name,
        grid_spec=pltpu.PrefetchScalarGridSpec(
            num_scalar_prefetch=1, grid=(n_tiles,), in_specs=in_specs,
            out_specs=pl.BlockSpec((SEQ, tn), lambda t, ids: (0, ids[1, t]))),
        out_shape=jax.ShapeDtypeStruct((SEQ, total_tiles * tn), BF16),
        input_output_aliases=aliases, compiler_params=_params("arbitrary"),
    )(tile_ids, *args)


def _own_first(chip, total_tiles):
    own = total_tiles // N_CHIPS
    step = jnp.arange(total_tiles, dtype=jnp.int32)
    tiles = (own * chip + step) % total_tiles
    return jnp.stack([step[:own], tiles[:own]]), jnp.stack([tiles[own:], tiles[own:]]), own


def _mm_nt(dy, w3, *, tn, tile0, n_tiles, name, after=None):
    m_rows = dy.shape[0]
    _, kc, ns = w3.shape
    tps = ns // tn
    cm = 512
    extra = [] if after is None else [after]

    def body(dy_ref, w_ref, *rest):
        o_ref, acc = rest[-2], rest[-1]
        t = pl.program_id(0)

        @pl.when(t == 0)
        def _():
            acc[...] = jnp.zeros_like(acc)

        for m in range(m_rows // cm):
            rows = pl.ds(m * cm, cm)
            acc[rows, :] += lax.dot_general(dy_ref[rows, :], w_ref[...], NT_DIMS, preferred_element_type=F32)

        @pl.when(t == n_tiles - 1)
        def _():
            o_ref[...] = acc[...].astype(BF16)

    return pl.pallas_call(
        body, name=name, grid=(n_tiles,),
        in_specs=[pl.BlockSpec((m_rows, tn), lambda t: (0, t)),
                  pl.BlockSpec((None, kc, tn), lambda t: ((tile0 + t) // tps, 0, (tile0 + t) % tps))]
        + [pl.BlockSpec(memory_space=pl.ANY)] * len(extra),
        out_specs=pl.BlockSpec((m_rows, kc), lambda t: (0, 0)),
        out_shape=jax.ShapeDtypeStruct((m_rows, kc), BF16),
        scratch_shapes=[pltpu.VMEM((m_rows, kc), F32)],
        compiler_params=_params("arbitrary"),
    )(dy, w3, *extra)


CONV_CHUNK = 16


def _shift_copies(buf, shifted):
    rows = shifted.shape[1]
    for s in range(1, 8):
        shifted[s - 1] = buf[pl.ds(s, rows), :]


def _shifted_rows(buf, shifted, offset, r0):
    s = offset % 8
    if s == 0:
        return buf[pl.ds(r0 + offset, CONV_CHUNK), :]
    return shifted[s - 1, pl.ds(r0 + (offset - s), CONV_CHUNK), :]


def _spread_taps(w_ref, taps):
    for k in range(CONV_WIDTH):
        taps[k] = jnp.broadcast_to(w_ref[k:k + 1, :], (8, D_MODEL))


def _times_tap(taps, k, rows):
    return (rows.reshape(CONV_CHUNK // 8, 8, D_MODEL) * taps[k][None]).reshape(CONV_CHUNK, D_MODEL)


def _conv_fwd(proj, conv_w, conv_b, ln_g, ln_b, name):
    tm = ROW_TILE
    hb = tm // HALO

    def body(vg_ref, halo_ref, z_ref, w_ref, b_ref, g_ref, be_ref, u5_ref, u5t_ref, u2_ref, buf, shifted, taps):
        i = pl.program_id(0)
        u1 = vg_ref[:, :D_MODEL].astype(F32) * _sigmoid(vg_ref[:, D_MODEL:].astype(F32))
        u1h = halo_ref[:, :D_MODEL].astype(F32) * _sigmoid(halo_ref[:, D_MODEL:].astype(F32))
        buf[pl.ds(0, HALO), :] = jnp.where(i > 0, u1h, 0.0)
        buf[pl.ds(HALO, tm), :] = u1
        _shift_copies(buf, shifted)
        _spread_taps(w_ref, taps)

        def chunk(ci, carry):
            r0 = pl.multiple_of(ci * CONV_CHUNK, CONV_CHUNK)
            acc = jnp.broadcast_to(b_ref[...], (CONV_CHUNK, D_MODEL))
            for k in range(CONV_WIDTH):
                acc = acc + _times_tap(taps, k, _shifted_rows(buf, shifted, HALO - (CONV_WIDTH - 1) + k, r0))
            u2_ref[pl.ds(r0, CONV_CHUNK), :] = acc
            return carry

        lax.fori_loop(0, tm // CONV_CHUNK, chunk, 0)
        acc = u2_ref[...]
        mu = jnp.mean(acc, axis=-1, keepdims=True)
        xc = acc - mu
        rstd = lax.rsqrt(jnp.mean(xc * xc, axis=-1, keepdims=True) + NORM_EPS)
        u3 = xc * rstd * g_ref[...] + be_ref[...]
        zv = z_ref[...].astype(F32)
        u5 = u3 * _sigmoid(u3) * (zv * _sigmoid(zv))
        u5_ref[...] = u5.astype(BF16)
        u5t_ref[...] = u5.T.astype(BF16)

    return pl.pallas_call(
        body, name=name, grid=(SEQ // tm,),
        in_specs=[pl.BlockSpec((tm, 2 * D_MODEL), lambda i: (i, 0)),
                  pl.BlockSpec((HALO, 2 * D_MODEL), lambda i: (jnp.maximum(i * hb - 1, 0), 0)),
                  _row_spec(tm, D_MODEL, 2),
                  _vec_spec(CONV_WIDTH, D_MODEL)] + [_vec_spec(1, D_MODEL)] * 3,
        out_specs=[_row_spec(tm, D_MODEL), pl.BlockSpec((D_MODEL, tm), lambda i: (0, i)), _row_spec(tm, D_MODEL)],
        out_shape=[jax.ShapeDtypeStruct((SEQ, D_MODEL), BF16), jax.ShapeDtypeStruct((D_MODEL, SEQ), BF16),
                   jax.ShapeDtypeStruct((SEQ, D_MODEL), F32)],
        scratch_shapes=[pltpu.VMEM((HALO + tm, D_MODEL), F32), pltpu.VMEM((7, HALO + tm - 8, D_MODEL), F32),
                        pltpu.VMEM((CONV_WIDTH, 8, D_MODEL), F32)],
        compiler_params=_params("parallel"),
    )(proj, proj, proj, conv_w, conv_b, ln_g, ln_b)


def _conv_bwd_pointwise(dy, w_out, proj, u2, ln_g, ln_b, name):
    tm = ROW_TILE

    def body(dy_ref, w_ref, z_ref, u2_ref, g_ref, be_ref, du2_ref, dz_ref, sums_ref):
        u2v = u2_ref[...]
        mu = jnp.mean(u2v, axis=-1, keepdims=True)
        xc = u2v - mu
        rstd = lax.rsqrt(jnp.mean(xc * xc, axis=-1, keepdims=True) + NORM_EPS)
        xhat = xc * rstd
        u3 = xhat * g_ref[...] + be_ref[...]
        s3 = _sigmoid(u3)
        u4 = u3 * s3
        zv = z_ref[...].astype(F32)
        sz = _sigmoid(zv)
        du5v = lax.dot_general(dy_ref[...], w_ref[...], NT_DIMS, preferred_element_type=F32)
        dz_ref[...] = du5v * u4 * (sz * (1.0 + zv * (1.0 - sz)))
        du3 = du5v * (zv * sz) * (s3 * (1.0 + u3 * (1.0 - s3)))
        dxhat = du3 * g_ref[...]
        du2 = rstd * (dxhat - jnp.mean(dxhat, axis=-1, keepdims=True)
                      - xhat * jnp.mean(dxhat * xhat, axis=-1, keepdims=True))
        du2_ref[...] = du2
        sums = jnp.concatenate([
            jnp.sum(du3 * xhat, axis=0, keepdims=True),
            jnp.sum(du3, axis=0, keepdims=True),
            jnp.sum(du2, axis=0, keepdims=True),
            jnp.zeros((5, D_MODEL), F32)], axis=0)

        @pl.when(pl.program_id(0) == 0)
        def _():
            sums_ref[...] = jnp.zeros_like(sums_ref)

        sums_ref[...] += sums

    return pl.pallas_call(
        body, name=name, grid=(SEQ // tm,),
        in_specs=[_row_spec(tm, D_MODEL), _vec_spec(D_MODEL, D_MODEL), _row_spec(tm, D_MODEL, 2),
                  _row_spec(tm, D_MODEL), _vec_spec(1, D_MODEL), _vec_spec(1, D_MODEL)],
        out_specs=[_row_spec(tm, D_MODEL), _row_spec(tm, D_MODEL), _vec_spec(8, D_MODEL)],
        out_shape=[jax.ShapeDtypeStruct((SEQ, D_MODEL), F32), jax.ShapeDtypeStruct((SEQ, D_MODEL), F32),
                   jax.ShapeDtypeStruct((8, D_MODEL), F32)],
        compiler_params=_params("arbitrary"),
    )(dy, w_out, proj, u2, ln_g, ln_b)


def _conv_bwd_taps(du2, dz, proj, conv_w, name):
    tm = ROW_TILE
    hb = tm // HALO
    n_blocks = SEQ // tm

    def body(du2_ref, dnext_ref, dz_ref, vg_ref, w_ref, dproj_ref, dw_ref, dbuf, dshift, sgbuf, ubuf, dwacc, taps):
        i = pl.program_id(0)
        _spread_taps(w_ref, taps)
        sg = _sigmoid(vg_ref[:, D_MODEL:].astype(F32))
        sgbuf[...] = sg
        ubuf[...] = vg_ref[:, :D_MODEL].astype(F32) * sg
        dbuf[pl.ds(0, tm), :] = du2_ref[...]
        dbuf[pl.ds(tm, HALO), :] = jnp.where(i < n_blocks - 1, dnext_ref[...], 0.0)
        _shift_copies(dbuf, dshift)

        @pl.when(i == 0)
        def _():
            dwacc[...] = jnp.zeros_like(dwacc)

        def chunk(ci, carry):
            r0 = pl.multiple_of(ci * CONV_CHUNK, CONV_CHUNK)
            rows = pl.ds(r0, CONV_CHUNK)
            u1c = ubuf[rows, :]
            du1 = jnp.zeros((CONV_CHUNK, D_MODEL), F32)
            for k in range(CONV_WIDTH):
                ahead = _shifted_rows(dbuf, dshift, CONV_WIDTH - 1 - k, r0)
                du1 = du1 + _times_tap(taps, k, ahead)
                prod = u1c * ahead
                dwacc[k] += prod[0:8] + prod[8:16]
            sgc = sgbuf[rows, :]
            dval = du1 * sgc
            dproj_ref[rows, 0:D_MODEL] = dval.astype(BF16)
            dproj_ref[rows, D_MODEL:2 * D_MODEL] = (
                dval * vg_ref[rows, 0:D_MODEL].astype(F32) * (1.0 - sgc)).astype(BF16)
            return carry

        lax.fori_loop(0, tm // CONV_CHUNK, chunk, 0)
        dproj_ref[:, 2 * D_MODEL:] = dz_ref[...].astype(BF16)

        @pl.when(i == n_blocks - 1)
        def _():
            for k in range(CONV_WIDTH):
                dw_ref[k:k + 1, :] = jnp.sum(dwacc[k], axis=0, keepdims=True)
            dw_ref[CONV_WIDTH:, :] = jnp.zeros((32 - CONV_WIDTH, D_MODEL), F32)

    return pl.pallas_call(
        body, name=name, grid=(n_blocks,),
        in_specs=[_row_spec(tm, D_MODEL),
                  pl.BlockSpec((HALO, D_MODEL), lambda i: (jnp.minimum((i + 1) * hb, SEQ // HALO - 1), 0)),
                  _row_spec(tm, D_MODEL),
                  pl.BlockSpec((tm, 2 * D_MODEL), lambda i: (i, 0)),
                  _vec_spec(CONV_WIDTH, D_MODEL)],
        out_specs=[_row_spec(tm, 3 * D_MODEL), _vec_spec(32, D_MODEL)],
        out_shape=[jax.ShapeDtypeStruct((SEQ, 3 * D_MODEL), BF16), jax.ShapeDtypeStruct((32, D_MODEL), F32)],
        scratch_shapes=[pltpu.VMEM((tm + HALO, D_MODEL), F32), pltpu.VMEM((7, HALO + tm - 8, D_MODEL), F32),
                        pltpu.VMEM((tm, D_MODEL), F32), pltpu.VMEM((tm, D_MODEL), F32),
                        pltpu.VMEM((CONV_WIDTH, 8, D_MODEL), F32), pltpu.VMEM((CONV_WIDTH, 8, D_MODEL), F32)],
        compiler_params=_params("arbitrary"),
    )(du2, du2, dz, proj, conv_w)


def _out_a(u5, w_out, x, gate, g1, scale1, shift1, name):
    tm = ROW_TILE
    n_d = len(DILATIONS)

    def body(u_ref, w_ref, x_ref, gate_ref, g_ref, sc_ref, sh_ref, x1_ref, y_ref, ht_ref, *rest):
        h_refs, nat = rest[:n_d], rest[-1]
        y = jnp.dot(u_ref[...], w_ref[...], preferred_element_type=F32)
        x1 = x_ref[...] + gate_ref[...] * y
        y_ref[...] = y.astype(BF16)
        x1_ref[...] = x1
        h = _normmod(x1, g_ref[...], sc_ref[...], sh_ref[...])
        ht_ref[...] = h.T.astype(BF16)
        for h_ref, d in zip(h_refs, DILATIONS):
            _store_classes(h_ref, h, nat, d)

    res = pl.pallas_call(
        body, name=name, grid=(SEQ // tm,),
        in_specs=[_row_spec(tm, D_MODEL), _vec_spec(D_MODEL, D_MODEL), _row_spec(tm, D_MODEL)]
        + [_vec_spec(1, D_MODEL)] * 4,
        out_specs=[_row_spec(tm, D_MODEL), _row_spec(tm, D_MODEL), pl.BlockSpec((D_MODEL, tm), lambda i: (0, i))]
        + [_class_spec(tm, d) for d in DILATIONS],
        out_shape=[jax.ShapeDtypeStruct((SEQ, D_MODEL), F32), jax.ShapeDtypeStruct((SEQ, D_MODEL), BF16),
                   jax.ShapeDtypeStruct((D_MODEL, SEQ), BF16)] + [_class_shape(d, BF16) for d in DILATIONS],
        scratch_shapes=[_natural_scratch(tm)],
        compiler_params=_params("parallel"),
    )(u5, w_out, x, gate, g1, scale1, shift1)
    return res[0], res[1], res[2], [a.reshape(SEQ, D_MODEL) for a in res[3:]]


def _out_b_loss(u, w_out, x1, gate, target, name):
    tm = ROW_TILE

    def body(u_ref, w_ref, x_ref, gate_ref, t_ref, e_ref, dy_ref, sums_ref):
        y = jnp.dot(u_ref[...], w_ref[...], preferred_element_type=F32)
        diff = x_ref[...] + gate_ref[...] * y - t_ref[...]
        e = diff * (1.0 / D_MODEL)
        e_ref[...] = e
        dy_ref[...] = (e * gate_ref[...]).astype(BF16)
        sums = jnp.concatenate([
            jnp.sum(e * y, axis=0, keepdims=True),
            jnp.sum(diff * diff, axis=0, keepdims=True),
            jnp.zeros((6, D_MODEL), F32)], axis=0)

        @pl.when(pl.program_id(0) == 0)
        def _():
            sums_ref[...] = jnp.zeros_like(sums_ref)

        sums_ref[...] += sums

    return pl.pallas_call(
        body, name=name, grid=(SEQ // tm,),
        in_specs=[_row_spec(tm, D_MODEL), _vec_spec(D_MODEL, D_MODEL), _row_spec(tm, D_MODEL),
                  _vec_spec(1, D_MODEL), _row_spec(tm, D_MODEL)],
        out_specs=[_row_spec(tm, D_MODEL), _row_spec(tm, D_MODEL), _vec_spec(8, D_MODEL)],
        out_shape=[jax.ShapeDtypeStruct((SEQ, D_MODEL), F32), jax.ShapeDtypeStruct((SEQ, D_MODEL), BF16),
                   jax.ShapeDtypeStruct((8, D_MODEL), F32)],
        compiler_params=_params("arbitrary"),
    )(u, w_out, x1, gate, target)


def _seg_matrix():
    r = lax.broadcasted_iota(jnp.int32, (256, 256), 0) // HEAD_DIM
    c = lax.broadcasted_iota(jnp.int32, (256, 256), 1) // HEAD_DIM
    return jnp.where(r == c, 1.0 / HEAD_DIM, 0.0).astype(BF16)


def _segmean(v, seg):
    hi = v.astype(BF16)
    lo = (v - hi.astype(F32)).astype(BF16)
    outs = []
    for c0 in range(0, D_MODEL, 256):
        outs.append(jnp.dot(hi[:, c0:c0 + 256], seg, preferred_element_type=F32)
                    + jnp.dot(lo[:, c0:c0 + 256], seg, preferred_element_type=F32))
    return jnp.concatenate(outs, axis=1)


def _qk_rstd(v, seg):
    return lax.rsqrt(_segmean(v * v, seg) + NORM_EPS)


def _qknorm_fwd(proj, group, qw, kw, seg, name):
    tm = ROW_TILE

    def body(q_in, k_in, qw_ref, kw_ref, seg_ref, q_ref, k_ref):
        segv = seg_ref[...]
        q = q_in[...].astype(F32)
        k = k_in[...].astype(F32)
        q_ref[...] = (q * _qk_rstd(q, segv) * qw_ref[...] * HEAD_DIM ** -0.5).astype(BF16)
        k_ref[...] = (k * _qk_rstd(k, segv) * kw_ref[...]).astype(BF16)

    return pl.pallas_call(
        body, name=name, grid=(SEQ // tm,),
        in_specs=[_row_spec(tm, D_MODEL, 3 * group), _row_spec(tm, D_MODEL, 3 * group + 1),
                  _vec_spec(1, D_MODEL), _vec_spec(1, D_MODEL), _vec_spec(256, 256)],
        out_specs=[_row_spec(tm, D_MODEL)] * 2,
        out_shape=[jax.ShapeDtypeStruct((SEQ, D_MODEL), BF16)] * 2,
        compiler_params=_params("parallel"),
    )(proj, proj, qw, kw, seg)


def _attn_masks(b, bpc, dilation, transposed=False):
    keys = ATTN_BLOCK if bpc == 1 else 2 * ATTN_BLOCK
    shape, q_axis = ((keys, ATTN_BLOCK), 1) if transposed else ((ATTN_BLOCK, keys), 0)
    qi = lax.broadcasted_iota(jnp.int32, shape, q_axis)
    kj = lax.broadcasted_iota(jnp.int32, shape, 1 - q_axis)
    if bpc == 1:
        steps = qi - kj
        return (steps * dilation).astype(F32), steps >= 0
    steps = qi + ATTN_BLOCK - kj
    has_prev = (b % bpc) != 0
    valid = (steps >= 0) & (steps <= ATTN_BLOCK) & (has_prev | (kj >= ATTN_BLOCK))
    return (steps * dilation).astype(F32), valid


MASKED = 1e30


def _bias_scratch(bpc):
    return pltpu.VMEM((1 if bpc == 1 else 2, N_HEADS, ATTN_BLOCK, (1 if bpc == 1 else 2) * ATTN_BLOCK), F32)


def _fill_bias(bias_ref, sl_ref, bpc, dilation):
    for variant in range(bias_ref.shape[0]):
        dist, valid = _attn_masks(variant, min(bpc, 2), dilation)
        bias_ref[variant] = jnp.where(valid[None], dist[None] * sl_ref[...], MASKED)


def _step_bias(bias_ref, b, bpc):
    if bpc == 1:
        return bias_ref[0]
    return bias_ref[jnp.where((b % bpc) != 0, 1, 0)]


def _key_tile(prev_ref, cur_ref, cols, bpc):
    if bpc == 1:
        return cur_ref[:, cols]
    return jnp.concatenate([prev_ref[:, cols], cur_ref[:, cols]], axis=0)


ATTN_HEADS_FWD = 16
ATTN_HEADS_BWD = 16
NT_DIMS = (((1,), (1,)), ((), ()))
BATCH_NT_DIMS = (((2,), (2,)), ((0,), (0,)))
BATCH_NN_DIMS = (((2,), (1,)), ((0,), (0,)))
BATCH_TN_DIMS = (((1,), (1,)), ((0,), (0,)))


def _head_stack(tile_of, heads):
    return jnp.stack([tile_of(slice(h * HEAD_DIM, (h + 1) * HEAD_DIM)) for h in range(heads)], axis=0)


def _attn_specs(heads, segment=0):
    width = heads * HEAD_DIM
    off = segment * (D_MODEL // width)
    last = SEQ // ATTN_BLOCK - 1
    cur = pl.BlockSpec((ATTN_BLOCK, width), lambda hg, b: (jnp.minimum(b, last), hg + off))
    prev = pl.BlockSpec((ATTN_BLOCK, width), lambda hg, b: (jnp.clip(b - 1, 0, last), hg + off))
    return cur, prev


def _attn_fwd(q, k, proj, group, slopes, dilation, name):
    bpc = SEQ // dilation // ATTN_BLOCK
    heads = ATTN_HEADS_FWD
    assert heads == N_HEADS
    cur, prev = _attn_specs(heads)
    v_cur, v_prev = _attn_specs(heads, segment=3 * group + 2)

    def body(sl_ref, q_ref, kp_ref, kc_ref, vp_ref, vc_ref, o_ref, lse_ref, bias_ref):
        b = pl.program_id(1)

        @pl.when(b == 0)
        def _():
            _fill_bias(bias_ref, sl_ref, bpc, dilation)

        q3 = _head_stack(lambda cols: q_ref[:, cols], heads)
        k3 = _head_stack(lambda cols: _key_tile(kp_ref, kc_ref, cols, bpc), heads)
        v3 = _head_stack(lambda cols: _key_tile(vp_ref, vc_ref, cols, bpc), heads)
        s = lax.dot_general(q3, k3, BATCH_NT_DIMS, preferred_element_type=F32)
        s = s - _step_bias(bias_ref, b, bpc)
        m = jnp.max(s, axis=-1, keepdims=True)
        p = jnp.exp(s - m)
        l = jnp.sum(p, axis=-1, keepdims=True)
        o3 = lax.dot_general(p.astype(BF16), v3, BATCH_NN_DIMS, preferred_element_type=F32) / l
        lse3 = m + jnp.log(l)
        for h in range(heads):
            o_ref[:, h * HEAD_DIM:(h + 1) * HEAD_DIM] = o3[h].astype(BF16)
        lse_ref[...] = jnp.concatenate([lse3[h] for h in range(heads)]
                                       + [jnp.zeros((ATTN_BLOCK, LANES - heads), F32)], axis=1)

    return pl.pallas_call(
        body, name=name, grid=(N_HEADS // heads, SEQ // ATTN_BLOCK),
        in_specs=[pl.BlockSpec((heads, 1, 1), lambda hg, b: (hg, 0, 0)), cur, prev, cur, v_prev, v_cur],
        out_specs=[cur, pl.BlockSpec((ATTN_BLOCK, LANES), lambda hg, b: (b, 0))],
        out_shape=[jax.ShapeDtypeStruct((SEQ, D_MODEL), BF16), jax.ShapeDtypeStruct((SEQ, LANES), F32)],
        scratch_shapes=[_bias_scratch(bpc)],
        compiler_params=_params("parallel", "arbitrary"),
    )(slopes.reshape(N_HEADS, 1, 1), q, k, k, proj, proj)


def _class_spec(tm, dilation, width=D_MODEL):
    if dilation == 1:
        return _row_spec(tm, width)
    return pl.BlockSpec((dilation, tm // dilation, width), lambda i: (0, i, 0))


def _class_shape(dilation, dtype, width=D_MODEL):
    if dilation == 1:
        return jax.ShapeDtypeStruct((SEQ, width), dtype)
    return jax.ShapeDtypeStruct((dilation, SEQ // dilation, width), dtype)


def _load_natural(in_ref, nat_ref, dilation):
    if dilation == 1:
        return in_ref[...].astype(F32)
    n = nat_ref.shape[1] // dilation
    tiles = in_ref.shape[-1] // LANES
    for r in range(dilation):
        for j in range(tiles):
            nat_ref.at[j][pl.ds(r, n, stride=dilation), :] = in_ref[r, :, j * LANES:(j + 1) * LANES].astype(F32)
    if tiles == 1:
        return nat_ref[0]
    return jnp.concatenate([nat_ref[j] for j in range(tiles)], axis=1)


def _store_classes(out_ref, value, nat_ref, dilation):
    if dilation == 1:
        out_ref[...] = value.astype(out_ref.dtype)
        return
    n = nat_ref.shape[1] // dilation
    tiles = value.shape[-1] // LANES
    for j in range(tiles):
        nat_ref[j] = value[:, j * LANES:(j + 1) * LANES]
    for r in range(dilation):
        for j in range(tiles):
            out_ref[r, :, j * LANES:(j + 1) * LANES] = (
                nat_ref.at[j][pl.ds(r, n, stride=dilation), :].astype(out_ref.dtype))


def _natural_scratch(tm):
    return pltpu.VMEM((D_MODEL // LANES, tm, LANES), F32)


def _head_selector():
    lane_head = lax.broadcasted_iota(jnp.int32, (D_MODEL, LANES), 0) // HEAD_DIM
    head = lax.broadcasted_iota(jnp.int32, (D_MODEL, LANES), 1)
    return (lane_head == head).astype(BF16)


def _dot_split(v, m01, dims):
    hi = v.astype(BF16)
    lo = (v - hi.astype(F32)).astype(BF16)
    return (lax.dot_general(hi, m01, dims, preferred_element_type=F32)
            + lax.dot_general(lo, m01, dims, preferred_element_type=F32))


def _merge_fwd(o_parts, lse_parts, z, sel, name):
    tm = ROW_TILE
    h_spec = pl.BlockSpec((tm, LANES), lambda i: (i, 0))

    def body(o0, o1, o2, l0, l1, l2, z_ref, sel_ref, u_ref, ut_ref, o_ref, lse_ref, nat):
        ls = [_load_natural(l, nat, d) for l, d in zip((l0, l1, l2), DILATIONS)]
        m = jnp.maximum(jnp.maximum(ls[0], ls[1]), ls[2])
        tot = m + jnp.log(jnp.exp(ls[0] - m) + jnp.exp(ls[1] - m) + jnp.exp(ls[2] - m))
        o = jnp.zeros((tm, D_MODEL), F32)
        for o_in, l, d in zip((o0, o1, o2), ls, DILATIONS):
            weight = _dot_split(jnp.exp(l - tot), sel_ref[...], NT_DIMS)
            o = o + weight * _load_natural(o_in, nat, d)
        zv = z_ref[...].astype(F32)
        u = o * (zv * _sigmoid(zv))
        u_ref[...] = u.astype(BF16)
        ut_ref[...] = u.T.astype(BF16)
        o_ref[...] = o.astype(BF16)
        lse_ref[...] = tot

    return pl.pallas_call(
        body, name=name, grid=(SEQ // tm,),
        in_specs=[_class_spec(tm, d) for d in DILATIONS] + [_class_spec(tm, d, LANES) for d in DILATIONS]
        + [_row_spec(tm, D_MODEL, B_Z_SEGMENT), _vec_spec(D_MODEL, LANES)],
        out_specs=[_row_spec(tm, D_MODEL), pl.BlockSpec((D_MODEL, tm), lambda i: (0, i)),
                   _row_spec(tm, D_MODEL), h_spec],
        out_shape=[jax.ShapeDtypeStruct((SEQ, D_MODEL), BF16), jax.ShapeDtypeStruct((D_MODEL, SEQ), BF16),
                   jax.ShapeDtypeStruct((SEQ, D_MODEL), BF16), jax.ShapeDtypeStruct((SEQ, LANES), F32)],
        scratch_shapes=[_natural_scratch(tm)],
        compiler_params=_params("parallel"),
    )(*o_parts, *lse_parts, z, sel)


def _merge_bwd(dy, w_out, o, lse, z, sel, name):
    tm = ROW_TILE
    n_d = len(DILATIONS)

    def body(dy_ref, w_ref, o_ref, lse_ref, z_ref, sel_ref, dz_ref, *rest):
        do_refs, delta_refs, lse_refs, nat = rest[:n_d], rest[n_d:2 * n_d], rest[2 * n_d:3 * n_d], rest[-1]
        zv = z_ref[...].astype(F32)
        sz = _sigmoid(zv)
        duv = lax.dot_general(dy_ref[...], w_ref[...], NT_DIMS, preferred_element_type=F32)
        ov = o_ref[...].astype(F32)
        do = duv * (zv * sz)
        dz_ref[...] = (duv * ov * (sz * (1.0 + zv * (1.0 - sz)))).astype(BF16)
        delta = _dot_split(do * ov, sel_ref[...], (((1,), (0,)), ((), ())))
        lv = lse_ref[...]
        for i, d in enumerate(DILATIONS):
            _store_classes(do_refs[i], do, nat, d)
            _store_classes(delta_refs[i], delta, nat, d)
            _store_classes(lse_refs[i], lv, nat, d)

    res = pl.pallas_call(
        body, name=name, grid=(SEQ // tm,),
        in_specs=[_row_spec(tm, D_MODEL), _vec_spec(D_MODEL, D_MODEL), _row_spec(tm, D_MODEL), _row_spec(tm, LANES),
                  _row_spec(tm, D_MODEL, B_Z_SEGMENT), _vec_spec(D_MODEL, LANES)],
        out_specs=[_row_spec(tm, D_MODEL)] + [_class_spec(tm, d) for d in DILATIONS]
        + [_class_spec(tm, d, LANES) for d in DILATIONS] * 2,
        out_shape=[jax.ShapeDtypeStruct((SEQ, D_MODEL), BF16)] + [_class_shape(d, BF16) for d in DILATIONS]
        + [_class_shape(d, F32, LANES) for d in DILATIONS] * 2,
        scratch_shapes=[_natural_scratch(tm)],
        compiler_params=_params("parallel"),
    )(dy, w_out, o, lse, z, sel)
    flat = lambda a: a.reshape(SEQ, a.shape[-1])
    return (res[0], [flat(a) for a in res[1:1 + n_d]], [flat(a) for a in res[1 + n_d:1 + 2 * n_d]],
            [flat(a) for a in res[1 + 2 * n_d:]])


def _attn_bwd(q, k, proj, group, do, lse, delta, slopes, dilation, name):
    bpc = SEQ // dilation // ATTN_BLOCK
    heads = ATTN_HEADS_BWD
    n_blocks = SEQ // ATTN_BLOCK
    carry = bpc > 1
    width = heads * HEAD_DIM
    cur, prev = _attn_specs(heads)
    v_cur, v_prev = _attn_specs(heads, segment=3 * group + 2)
    assert heads == N_HEADS
    per_head = pl.BlockSpec((ATTN_BLOCK, LANES), lambda hg, b: (jnp.minimum(b, n_blocks - 1), 0))
    scale = HEAD_DIM ** -0.5

    def body(sl_ref, q_ref, kp_ref, kc_ref, vp_ref, vc_ref, do_ref, lse_ref, dl_ref,
             dq_ref, dk_ref, dv_ref, *scratch):
        b = pl.program_id(1)
        if carry:
            dk_carry, dv_carry = scratch

            @pl.when(b == n_blocks)
            def _():
                dk_ref[...] = dk_carry[...].astype(BF16)
                dv_ref[...] = dv_carry[...].astype(BF16)

            @pl.when(b < n_blocks)
            def _():
                step(sl_ref, q_ref, kp_ref, kc_ref, vp_ref, vc_ref, do_ref, lse_ref, dl_ref,
                     dq_ref, dk_ref, dv_ref, dk_carry, dv_carry, b)
        else:
            step(sl_ref, q_ref, kp_ref, kc_ref, vp_ref, vc_ref, do_ref, lse_ref, dl_ref,
                 dq_ref, dk_ref, dv_ref, None, None, b)

    def step(sl_ref, q_ref, kp_ref, kc_ref, vp_ref, vc_ref, do_ref, lse_ref, dl_ref,
             dq_ref, dk_ref, dv_ref, dk_carry, dv_carry, b):
        if carry:
            @pl.when(b == 0)
            def _():
                dk_carry[...] = jnp.zeros_like(dk_carry)
                dv_carry[...] = jnp.zeros_like(dv_carry)

        q3 = _head_stack(lambda cols: q_ref[:, cols], heads)
        k3 = _head_stack(lambda cols: _key_tile(kp_ref, kc_ref, cols, bpc), heads)
        v3 = _head_stack(lambda cols: _key_tile(vp_ref, vc_ref, cols, bpc), heads)
        do3 = _head_stack(lambda cols: do_ref[:, cols], heads)
        lse_t = lse_ref[...].T
        dl_t = dl_ref[...].T
        lse3 = jnp.stack([lse_t[h:h + 1, :] for h in range(heads)], axis=0)
        dl3 = jnp.stack([dl_t[h:h + 1, :] for h in range(heads)], axis=0)
        s = lax.dot_general(k3, q3, BATCH_NT_DIMS, preferred_element_type=F32)
        dist, valid = _attn_masks(b, bpc, dilation, transposed=True)
        p = jnp.exp(jnp.where(valid[None], s - dist[None] * sl_ref[...], NEG_INF) - lse3)
        dp = lax.dot_general(v3, do3, BATCH_NT_DIMS, preferred_element_type=F32)
        ds = (p * (dp - dl3)).astype(BF16)
        dq3 = lax.dot_general(ds, k3, BATCH_TN_DIMS, preferred_element_type=F32) * scale
        dk3 = lax.dot_general(ds, q3, BATCH_NN_DIMS, preferred_element_type=F32)
        dv3 = lax.dot_general(p.astype(BF16), do3, BATCH_NN_DIMS, preferred_element_type=F32)
        for h in range(heads):
            cols = slice(h * HEAD_DIM, (h + 1) * HEAD_DIM)
            dq_ref[:, cols] = dq3[h].astype(BF16)
            if carry:
                dk_ref[:, cols] = (dk_carry[:, cols] + dk3[h, :ATTN_BLOCK]).astype(BF16)
                dv_ref[:, cols] = (dv_carry[:, cols] + dv3[h, :ATTN_BLOCK]).astype(BF16)
                dk_carry[:, cols] = dk3[h, ATTN_BLOCK:]
                dv_carry[:, cols] = dv3[h, ATTN_BLOCK:]
            else:
                dk_ref[:, cols] = dk3[h].astype(BF16)
                dv_ref[:, cols] = dv3[h].astype(BF16)

    kv_out = prev if carry else cur
    return pl.pallas_call(
        body, name=name, grid=(N_HEADS // heads, n_blocks + (1 if carry else 0)),
        in_specs=[pl.BlockSpec((heads, 1, 1), lambda hg, b: (hg, 0, 0)), cur, prev, cur, v_prev, v_cur,
                  cur, per_head, per_head],
        out_specs=[cur, kv_out, kv_out],
        out_shape=[jax.ShapeDtypeStruct((SEQ, D_MODEL), BF16)] * 3,
        scratch_shapes=[pltpu.VMEM((ATTN_BLOCK, width), F32)] * 2 if carry else [],
        compiler_params=_params("parallel", "arbitrary"),
    )(slopes.reshape(N_HEADS, 1, 1), q, k, k, proj, proj, do, lse, delta)


def _qknorm_bwd(proj, group, qw, kw, seg, dq, dk, dv, name):
    tm = ROW_TILE

    def body(q_in, k_in, qw_ref, kw_ref, seg_ref, dq_ref, dk_ref, dv_ref, dproj_ref, sums_ref):
        segv = seg_ref[...]
        sums = []
        for part, (raw_ref, w_ref, dn_ref) in enumerate(((q_in, qw_ref, dq_ref), (k_in, kw_ref, dk_ref))):
            raw = raw_ref[...].astype(F32)
            dn = dn_ref[...].astype(F32)
            r = _qk_rstd(raw, segv)
            xhat = raw * r
            gq = dn * w_ref[...]
            draw = r * (gq - xhat * _segmean(xhat * gq, segv))
            dproj_ref[:, part * D_MODEL:(part + 1) * D_MODEL] = draw.astype(BF16)
            sums.append(jnp.sum(dn * xhat, axis=0, keepdims=True))
        dproj_ref[:, 2 * D_MODEL:] = dv_ref[...]

        @pl.when(pl.program_id(0) == 0)
        def _():
            sums_ref[...] = jnp.zeros_like(sums_ref)

        sums_ref[...] += jnp.concatenate(sums + [jnp.zeros((6, D_MODEL), F32)], axis=0)

    return pl.pallas_call(
        body, name=name, grid=(SEQ // tm,),
        in_specs=[_row_spec(tm, D_MODEL, 3 * group), _row_spec(tm, D_MODEL, 3 * group + 1),
                  _vec_spec(1, D_MODEL), _vec_spec(1, D_MODEL), _vec_spec(256, 256)] + [_row_spec(tm, D_MODEL)] * 3,
        out_specs=[_row_spec(tm, 3 * D_MODEL), _vec_spec(8, D_MODEL)],
        out_shape=[jax.ShapeDtypeStruct((SEQ, 3 * D_MODEL), BF16), jax.ShapeDtypeStruct((8, D_MODEL), F32)],
        compiler_params=_params("arbitrary"),
    )(proj, proj, qw, kw, seg, dq, dk, dv)


B_TN = 512
B_GROUP_TILES = 3 * D_MODEL // B_TN
B_Z_TILE0 = 3 * B_GROUP_TILES
B_Z_TILES = D_MODEL // B_TN
B_TILES = B_Z_TILE0 + B_Z_TILES
B_Z_SEGMENT = 3 * len(DILATIONS)


def _local_step(x, target, mods, norm_g, conv_w, conv_b, ln_g, ln_b, q_norm, k_norm, chip, own_wa_in, own_wb_in,
                weights_a, weights_b, forward_weights_b, send_grads_b, forward_grads_b, send_grads_a):
    row = lambda a, i: a[i:i + 1]
    shift0, scale0, gate0 = row(mods[0], 0), row(mods[0], 1), row(mods[0], 2)
    shift1, scale1, gate1 = row(mods[1], 0), row(mods[1], 1), row(mods[1], 2)
    g0, g1 = row(norm_g, 0), row(norm_g, 1)
    seg = _seg_matrix()
    slopes = jnp.exp2(-8.0 * jnp.arange(1, N_HEADS + 1, dtype=F32) / N_HEADS)
    qw = [jnp.tile(q_norm[g:g + 1], (1, N_HEADS)) for g in range(3)]
    kw = [jnp.tile(k_norm[g:g + 1], (1, N_HEADS)) for g in range(3)]

    h0, h0t = _normmod_fwd(x, g0, scale0, shift0, "prenorm0")
    nsa = own_wa_in.shape[2]
    tiles_a = dict(tn=nsa, total_tiles=N_CHIPS, part_of=lambda tile: 0)
    own_ids, rest_ids, own_tiles = _own_first(chip, N_CHIPS)
    proj_a = _in_tiles([h0], own_wa_in, own_ids, own_tiles, name="a_in_own", **tiles_a)
    wa_in, wa_out = weights_a(proj_a)
    ja = wa_in.shape[0]
    proj_a = _in_tiles([h0], wa_in, rest_ids, N_CHIPS - own_tiles, name="a_in_rest", prev=proj_a, **tiles_a)
    u5, u5t, u2 = _conv_fwd(proj_a, conv_w, conv_b, ln_g, ln_b, "a_conv")
    x1, y_a, h1t, h1c = _out_a(u5, wa_out, x, gate0, g1, scale1, shift1, "a_out")

    tiles_b = dict(tn=B_TN, total_tiles=B_TILES,
                   part_of=lambda tile: jnp.where(tile >= B_Z_TILE0, 0, tile // B_GROUP_TILES))
    own_ids, rest_ids, own_tiles = _own_first(chip, B_TILES)
    proj_b = _in_tiles(h1c, own_wb_in, own_ids, own_tiles, name="b_in_own", **tiles_b)
    forward_weights_b(proj_b)
    wb_in, wb_out = weights_b(proj_b)
    jb, _, nsb = wb_in.shape
    proj_b = _in_tiles(h1c, wb_in, rest_ids, B_TILES - own_tiles, name="b_in_rest", prev=proj_b, **tiles_b)
    h1 = h1c[0]
    qkv, o_parts, lse_parts = [], [], []
    for g, d in enumerate(DILATIONS):
        qn, kn = _qknorm_fwd(proj_b, g, qw[g], kw[g], seg, f"b_qknorm_g{g}")
        og, lg = _attn_fwd(qn, kn, proj_b, g, slopes, d, f"b_attn_g{g}")
        qkv.append((qn, kn))
        o_parts.append(og if d == 1 else og.reshape(d, SEQ // d, D_MODEL))
        lse_parts.append(lg if d == 1 else lg.reshape(d, SEQ // d, LANES))
    sel = _head_selector()
    u_b, u_bt, o_b, lse_b = _merge_fwd(o_parts, lse_parts, proj_b, sel, "b_merge")
    e, dy_b, sums_loss = _out_b_loss(u_b, wb_out, x1, gate1, target, "b_out_loss")

    dwb_out = _mm(u_bt, dy_b, tn=D_MODEL, tile0=0, n_tiles=1, out_dtype=BF16, name="b_dwout")
    dz_b, do_c, delta_c, lse_c = _merge_bwd(dy_b, wb_out, o_b, lse_b, proj_b, sel, "b_merge_bwd")
    dwb_in = _mm(h1t, dz_b, tn=B_TN, tile0=B_Z_TILE0, n_tiles=B_Z_TILES, out_dtype=BF16, name="b_dwin_z",
                 out3d=(jb, nsb))
    dh1_parts = [_mm_nt(dz_b, wb_in, tn=B_TN, tile0=B_Z_TILE0, n_tiles=B_Z_TILES, name="b_dh_z")]
    qk_sums = []
    for g, d in enumerate(DILATIONS):
        qn, kn = qkv[g]
        dq, dk, dv = _attn_bwd(qn, kn, proj_b, g, do_c[g], lse_c[g], delta_c[g], slopes, d, f"b_attn_bwd_g{g}")
        dproj, sums_qk = _qknorm_bwd(proj_b, g, qw[g], kw[g], seg, dq, dk, dv, f"b_qknorm_bwd_g{g}")
        qk_sums.append(sums_qk)
        dwb_in = _mm(h1t if d == 1 else h1c[g], dproj, tn=B_TN, tile0=g * B_GROUP_TILES, n_tiles=B_GROUP_TILES,
                     out_dtype=BF16, name=f"b_dwin_g{g}", out3d=(jb, nsb), prev=dwb_in, transpose_lhs=d != 1)
        dh = _mm_nt(dproj, wb_in, tn=B_TN, tile0=g * B_GROUP_TILES, n_tiles=B_GROUP_TILES, name=f"b_dh_g{g}")
        dh1_parts.append(dh)
    token = send_grads_b(dwb_in, dwb_out)
    dx1, sums_n1, dy_a = _normmod_bwd(x1, g1, scale1 + token[0:1, 0:1], dh1_parts, e, "prenorm1_bwd",
                                      part_dilations=(1,) + DILATIONS, gated=(gate0, y_a))
    token = forward_grads_b(dx1)

    dwa_out = _mm(u5t, dy_a, tn=D_MODEL, tile0=0, n_tiles=1, out_dtype=BF16, name="a_dwout")
    du2, dz_a, sums_ln = _conv_bwd_pointwise(dy_a, wa_out, proj_a, u2, ln_g + token[0:1, 0:1], ln_b,
                                             "a_conv_bwd_pw")
    dproj_a, dconv_w = _conv_bwd_taps(du2, dz_a, proj_a, conv_w, "a_conv_bwd_taps")
    dwa_in = _mm(h0t, dproj_a, tn=nsa, tile0=0, n_tiles=ja, out_dtype=BF16, name="a_dwin", out3d=(ja, nsa))
    token = send_grads_a(dwa_in, dwa_out)
    dh0 = _mm_nt(dproj_a, wa_in, tn=nsa, tile0=0, n_tiles=ja, name="a_dh", after=token)
    grad_x, sums_n0 = _normmod_bwd(x, g0, scale0, [dh0], dx1, "prenorm0_bwd")

    small = dict(
        dnorm_g=jnp.concatenate([sums_n0[0:1], sums_n1[0:1]], axis=0),
        dmod0=jnp.concatenate([sums_n0[2:3], sums_n0[1:2], sums_n1[3:4]], axis=0),
        dmod1=jnp.concatenate([sums_n1[2:3], sums_n1[1:2], sums_loss[0:1]], axis=0),
        dln_g=sums_ln[0:1], dln_b=sums_ln[1:2], dconv_b=sums_ln[2:3],
        dconv_w=dconv_w[:CONV_WIDTH],
        dq_norm=jnp.concatenate([s[0:1] for s in qk_sums], axis=0),
        dk_norm=jnp.concatenate([s[1:2] for s in qk_sums], axis=0),
        loss_cols=sums_loss[1:2],
    )
    return grad_x, small


def _adamw(w, g, m, v, name, after=None, copy_grad=False):
    rows, cols = w.shape
    tr = rows if rows <= 128 else (256 if cols <= D_MODEL else 128)
    c1 = 1.0 / (1.0 - ADAM_B1 ** ADAM_STEP)
    c2 = 1.0 / (1.0 - ADAM_B2 ** ADAM_STEP)
    extra = [] if after is None else [after]
    n_out = 4 if copy_grad else 3

    def body(w_ref, g_ref, m_ref, v_ref, *rest):
        d_ref, mo_ref, vo_ref = rest[len(extra):len(extra) + 3]
        gv = g_ref[...]
        if copy_grad:
            rest[-1][...] = gv
        mn = ADAM_B1 * m_ref[...] + (1.0 - ADAM_B1) * gv
        vn = ADAM_B2 * v_ref[...] + (1.0 - ADAM_B2) * (gv * gv)
        mo_ref[...] = mn
        vo_ref[...] = vn
        d_ref[...] = -ADAM_LR * ((mn * c1) / (jnp.sqrt(vn * c2) + ADAM_EPS) + ADAM_WD * w_ref[...])

    spec = pl.BlockSpec((tr, cols), lambda i: (i, 0))
    return pl.pallas_call(
        body, name=name, grid=(rows // tr,),
        in_specs=[spec] * 4 + [pl.BlockSpec(memory_space=pl.ANY)] * len(extra), out_specs=[spec] * n_out,
        out_shape=[jax.ShapeDtypeStruct((rows, cols), F32)] * n_out,
        compiler_params=_params("parallel"),
    )(w, g, m, v, *extra)


def _cast_into_slot(w, chip_idx, name, keep_own=False, after=None):
    rows, cols = w.shape
    tr = 256
    extra = [] if after is None else [after]

    def body(ch_ref, w_ref, *rest):
        wb = w_ref[...].astype(BF16)
        for o_ref in rest[len(extra):]:
            o_ref[...] = wb

    slot_spec = pl.BlockSpec((None, tr, cols), lambda i, ch: (ch[0], i, 0))
    own_spec = pl.BlockSpec((None, tr, cols), lambda i, ch: (0, i, 0))
    res = pl.pallas_call(
        body, name=name,
        grid_spec=pltpu.PrefetchScalarGridSpec(
            num_scalar_prefetch=1, grid=(rows // tr,),
            in_specs=[pl.BlockSpec((tr, cols), lambda i, ch: (i, 0))] + [pl.BlockSpec(memory_space=pl.ANY)] * len(extra),
            out_specs=[slot_spec, own_spec] if keep_own else [slot_spec]),
        out_shape=[jax.ShapeDtypeStruct((N_CHIPS, rows, cols), BF16)]
        + ([jax.ShapeDtypeStruct((1, rows, cols), BF16)] if keep_own else []),
        compiler_params=_params("parallel"),
    )(chip_idx, w, *extra)
    return tuple(res) if keep_own else res[0]


def _position():
    x, y, c = lax.axis_index("x"), lax.axis_index("y"), lax.axis_index("c")
    return x, y, c


def _xor_peer(x, y, c, k):
    return (x ^ ((k >> 2) & 1), y ^ ((k >> 1) & 1), c ^ (k & 1))


def _chip_peer(x, y, k):
    return (x ^ ((k >> 1) & 1), y ^ (k & 1))


def _ada_forward(c_row, ada_w, ada_b, conv_w, after=()):
    ns = ada_w.shape[2]
    cw = conv_w.shape[1]

    def body(c_ref, w_ref, b_ref, cv_ref, *rest):
        (mod_ref, sc_ref, cvo_ref, c_all, mp, parts, cv_parts,
         send1, recv1, send2, recv2, send3, recv3) = rest[len(after):]
        x, y, c = _position()
        me = 4 * x + 2 * y + c
        chip = 2 * x + y

        def c_copy(k):
            return pltpu.make_async_remote_copy(
                src_ref=c_all.at[me], dst_ref=c_all.at[me], send_sem=send1.at[k - 1], recv_sem=recv1.at[k - 1],
                device_id=_xor_peer(x, y, c, k), device_id_type=MESH)

        def cv_copy(k):
            px, py = _chip_peer(x, y, k)
            return pltpu.make_async_remote_copy(
                src_ref=cv_parts.at[chip], dst_ref=cv_parts.at[chip], send_sem=send3.at[k - 1],
                recv_sem=recv3.at[k - 1], device_id=(px, py, c), device_id_type=MESH)

        c_all[me] = c_ref[...]
        cv_parts[chip] = cv_ref[...]
        for k in range(1, N_DEV):
            c_copy(k).start()
        for k in range(1, N_CHIPS):
            cv_copy(k).start()
        for k in range(1, N_DEV):
            c_copy(k).wait_recv()
        cv = jnp.concatenate([c_all[i] for i in range(N_DEV)], axis=0)
        sc = cv * _sigmoid(cv)
        sc_ref[...] = sc
        for l in range(2):
            res = jnp.dot(sc, w_ref[l], preferred_element_type=F32, precision=lax.Precision.HIGHEST)
            for i in range(N_DEV):
                mp[i, l:l + 1, :] = res[i:i + 1, :]

        def mod_copy(k):
            px, py = _chip_peer(x, y, k)
            return pltpu.make_async_remote_copy(
                src_ref=mp.at[4 * px + 2 * py + c], dst_ref=parts.at[chip], send_sem=send2.at[k - 1],
                recv_sem=recv2.at[k - 1], device_id=(px, py, c), device_id_type=MESH)

        for k in range(1, N_CHIPS):
            mod_copy(k).start()
        parts[chip] = mp[me]
        for k in range(1, N_CHIPS):
            mod_copy(k).wait_recv()
            cv_copy(k).wait_recv()
        mod_ref[...] = jnp.concatenate([parts[j] for j in range(N_CHIPS)], axis=1) + b_ref[...]
        cvo_ref[...] = jnp.concatenate([cv_parts[j] for j in range(N_CHIPS)], axis=1)
        for k in range(1, N_DEV):
            c_copy(k).wait_send()
        for k in range(1, N_CHIPS):
            mod_copy(k).wait_send()
            cv_copy(k).wait_send()

    vm = pl.BlockSpec(memory_space=pltpu.VMEM)
    return pl.pallas_call(
        body, name="ada_forward",
        in_specs=[vm] * 4 + [pl.BlockSpec(memory_space=pl.ANY)] * len(after), out_specs=[vm] * 3,
        out_shape=[jax.ShapeDtypeStruct((2, 3 * D_MODEL), F32), jax.ShapeDtypeStruct((N_DEV, D_MODEL), F32),
                   jax.ShapeDtypeStruct((CONV_WIDTH, N_CHIPS * cw), F32)],
        scratch_shapes=[pltpu.VMEM((N_DEV, 1, D_MODEL), F32), pltpu.VMEM((N_DEV, 2, ns), F32),
                        pltpu.VMEM((N_CHIPS, 2, ns), F32), pltpu.VMEM((N_CHIPS, CONV_WIDTH, cw), F32),
                        pltpu.SemaphoreType.DMA((N_DEV - 1,)), pltpu.SemaphoreType.DMA((N_DEV - 1,)),
                        pltpu.SemaphoreType.DMA((N_CHIPS - 1,)), pltpu.SemaphoreType.DMA((N_CHIPS - 1,)),
                        pltpu.SemaphoreType.DMA((N_CHIPS - 1,)), pltpu.SemaphoreType.DMA((N_CHIPS - 1,))],
        compiler_params=pltpu.CompilerParams(vmem_limit_bytes=VMEM_LIMIT_BYTES),
    )(c_row, ada_w, ada_b, conv_w, *after)


HBM_SPEC = pl.BlockSpec(memory_space=pltpu.HBM)
ANY_SPEC = pl.BlockSpec(memory_space=pl.ANY)
SEM_SPEC = pl.BlockSpec(memory_space=pltpu.SEMAPHORE)
SPLIT_PARAMS = dict(compiler_params=pltpu.CompilerParams(has_side_effects=pltpu.SideEffectType.DATAFLOW_SIDE_EFFECTING))
TOKEN = jax.ShapeDtypeStruct((8, 128), F32)
SIBLING_BARRIERS = {name: i for i, name in enumerate((
    "gather_forward_a", "gather_forward_b", "reduce_d2d_start_b", "reduce_d2d_start_a",
    "reduce_share_start_b", "reduce_share_start_a"))}


def _hbm(arrays):
    return [pltpu.with_memory_space_constraint(a, pltpu.HBM) for a in arrays]


def _hbm_like(arrays):
    return [pltpu.HBM(a.shape, a.dtype) for a in arrays]


def _gather_start(lands, after, name):
    n = len(lands)

    def body(*refs):
        ins = refs[:n]
        send, recv = refs[n + 1], refs[n + 2]
        x, y, c = _position()
        chip = 2 * x + y
        for t in range(n):
            rh = ins[t].shape[1] // 2
            for k in range(1, N_CHIPS):
                px, py = _chip_peer(x, y, k)
                block = ins[t].at[chip, pl.ds(c * rh, rh)]
                pltpu.make_async_remote_copy(
                    src_ref=block, dst_ref=block, send_sem=send.at[3 * t + k - 1], recv_sem=recv.at[3 * t + k - 1],
                    device_id=(px, py, c), device_id_type=MESH).start()
        refs[-1][...] = jnp.zeros(TOKEN.shape, F32)

    res = pl.pallas_call(
        body, name=name, in_specs=[HBM_SPEC] * n + [ANY_SPEC],
        out_specs=(SEM_SPEC, SEM_SPEC, *[HBM_SPEC] * n, pl.BlockSpec(memory_space=pltpu.VMEM)),
        out_shape=(pltpu.SemaphoreType.DMA((3 * n,)), pltpu.SemaphoreType.DMA((3 * n,)), *_hbm_like(lands), TOKEN),
        input_output_aliases={t: 2 + t for t in range(n)}, **SPLIT_PARAMS,
    )(*_hbm(lands), after)
    return res[0], res[1], list(res[2:2 + n]), res[-1]


def _gather_forward(send, recv, lands, after, name, sibling_barrier):
    n = len(lands)

    def body(*refs):
        _sibling_handshake()
        ins = refs[:n]
        send1, recv1 = refs[n], refs[n + 1]
        send2, recv2 = refs[n + 3], refs[n + 4]
        x, y, c = _position()
        chip = 2 * x + y
        for t in range(n):
            rh = ins[t].shape[1] // 2
            half = pl.ds(c * rh, rh)
            for k in range(1, N_CHIPS):
                px, py = _chip_peer(x, y, k)
                s = 3 * t + k - 1
                got = ins[t].at[2 * px + py, half]
                cp = pltpu.make_async_remote_copy(
                    src_ref=ins[t].at[chip, half], dst_ref=got, send_sem=send1.at[s], recv_sem=recv1.at[s],
                    device_id=(px, py, c), device_id_type=MESH)
                cp.wait_send()
                cp.wait_recv()
                pltpu.make_async_remote_copy(
                    src_ref=got, dst_ref=got, send_sem=send2.at[s], recv_sem=recv2.at[s],
                    device_id=(x, y, 1 - c), device_id_type=MESH).start()
        refs[-1][...] = jnp.zeros(TOKEN.shape, F32)

    res = pl.pallas_call(
        body, name=name, in_specs=[HBM_SPEC] * n + [SEM_SPEC, SEM_SPEC, ANY_SPEC],
        out_specs=(SEM_SPEC, SEM_SPEC, *[HBM_SPEC] * n, pl.BlockSpec(memory_space=pltpu.VMEM)),
        out_shape=(pltpu.SemaphoreType.DMA((3 * n,)), pltpu.SemaphoreType.DMA((3 * n,)), *_hbm_like(lands), TOKEN),
        input_output_aliases={t: 2 + t for t in range(n)}, **_split_params(sibling_barrier),
    )(*lands, send, recv, after)
    return res[0], res[1], list(res[2:2 + n]), res[-1]


def _gather_wait(send, recv, lands, after, name):
    n = len(lands)

    def body(*refs):
        ins = refs[:n]
        send_ref, recv_ref = refs[n], refs[n + 1]
        x, y, c = _position()
        for t in range(n):
            rh = ins[t].shape[1] // 2
            for k in range(1, N_CHIPS):
                px, py = _chip_peer(x, y, k)
                cp = pltpu.make_async_remote_copy(
                    src_ref=ins[t].at[2 * px + py, pl.ds(c * rh, rh)],
                    dst_ref=ins[t].at[2 * px + py, pl.ds((1 - c) * rh, rh)], send_sem=send_ref.at[3 * t + k - 1],
                    recv_sem=recv_ref.at[3 * t + k - 1], device_id=(x, y, 1 - c), device_id_type=MESH)
                cp.wait_send()
                cp.wait_recv()

    res = pl.pallas_call(
        body, name=name, in_specs=[HBM_SPEC] * n + [SEM_SPEC, SEM_SPEC, ANY_SPEC], out_specs=[HBM_SPEC] * n,
        out_shape=_hbm_like(lands), input_output_aliases={t: t for t in range(n)}, **SPLIT_PARAMS,
    )(*lands, send, recv, after)
    return list(res)


def _sibling_handshake():
    x, y, c = _position()
    barrier = pltpu.get_barrier_semaphore()
    pl.semaphore_signal(barrier, inc=1, device_id=(x, y, 1 - c), device_id_type=MESH)
    pl.semaphore_wait(barrier, 1)


def _split_params(sibling_barrier):
    if sibling_barrier is None:
        return SPLIT_PARAMS
    return dict(compiler_params=pltpu.CompilerParams(
        has_side_effects=pltpu.SideEffectType.DATAFLOW_SIDE_EFFECTING, collective_id=sibling_barrier))


def _split_start(name, arrays, n_sems, after, issue, sibling_barrier=None):
    m = len(arrays)

    def body(*refs):
        if sibling_barrier is not None:
            _sibling_handshake()
        issue(refs[:m], refs[m + 1], refs[m + 2])
        refs[-1][...] = jnp.zeros(TOKEN.shape, F32)

    res = pl.pallas_call(
        body, name=name, in_specs=[HBM_SPEC] * m + [ANY_SPEC],
        out_specs=(SEM_SPEC, SEM_SPEC, *[HBM_SPEC] * m, pl.BlockSpec(memory_space=pltpu.VMEM)),
        out_shape=(pltpu.SemaphoreType.DMA((n_sems,)), pltpu.SemaphoreType.DMA((n_sems,)), *_hbm_like(arrays), TOKEN),
        input_output_aliases={t: 2 + t for t in range(m)}, **_split_params(sibling_barrier),
    )(*_hbm(arrays), after)
    return res[0], res[1], list(res[2:2 + m]), res[-1]


def _split_wait(name, arrays, send, recv, after, await_all):
    m = len(arrays)

    def body(*refs):
        await_all(refs[:m], refs[m], refs[m + 1])

    res = pl.pallas_call(
        body, name=name, in_specs=[HBM_SPEC] * m + [SEM_SPEC, SEM_SPEC, ANY_SPEC], out_specs=[HBM_SPEC] * m,
        out_shape=_hbm_like(arrays), input_output_aliases={t: t for t in range(m)}, **SPLIT_PARAMS,
    )(*arrays, send, recv, after)
    return list(res)


def _sibling_copies(refs, send, recv, n):
    x, y, c = _position()
    cps = []
    for t in range(n):
        rh = refs[t].shape[1] // 2
        cps.append(pltpu.make_async_remote_copy(
            src_ref=refs[t].at[pl.ds(0, N_CHIPS), pl.ds((1 - c) * rh, rh)], dst_ref=refs[n + t],
            send_sem=send.at[t], recv_sem=recv.at[t], device_id=(x, y, 1 - c), device_id_type=MESH))
    return cps


def _reduce_sibling_start(grads, after, name, sibling_barrier):
    n = len(grads)
    lands = [lax.empty((N_CHIPS, g.shape[1] // 2, g.shape[2]), BF16) for g in grads]

    def issue(refs, send, recv):
        for cp in _sibling_copies(refs, send, recv, n):
            cp.start()

    return _split_start(name, list(grads) + lands, n, after, issue, sibling_barrier)


def _reduce_sibling_wait(send, recv, arrays, after, name):
    n = len(arrays) // 2

    def await_all(refs, send_ref, recv_ref):
        for cp in _sibling_copies(refs, send_ref, recv_ref, n):
            cp.wait_send()
            cp.wait_recv()

    res = _split_wait(name, arrays, send, recv, after, await_all)
    return res[:n], res[n:]


def _add_sibling_half(grad, got, dev_idx, name):
    j, r, cols = grad.shape
    rh = r // 2
    tr = rh
    nb = rh // tr

    def body(idx_ref, g_ref, got_ref, out_ref):
        out_ref[...] = (g_ref[...].astype(F32) + got_ref[...].astype(F32)).astype(BF16)

    return pl.pallas_call(
        body, name=name,
        grid_spec=pltpu.PrefetchScalarGridSpec(
            num_scalar_prefetch=1, grid=(j, nb),
            in_specs=[pl.BlockSpec((None, tr, cols), lambda jj, i, idx: (jj, idx[2] * nb + i, 0)),
                      pl.BlockSpec((None, tr, cols), lambda jj, i, idx: (jj, i, 0))],
            out_specs=pl.BlockSpec((None, tr, cols), lambda jj, i, idx: (jj, i, 0))),
        out_shape=jax.ShapeDtypeStruct((j, rh, cols), BF16),
        compiler_params=_params("parallel", "parallel"),
    )(dev_idx, grad, got)


def _chip_copies(refs, send, recv, n, receiving):
    x, y, c = _position()
    chip = 2 * x + y
    cps = []
    for t in range(n):
        for k in range(1, N_CHIPS):
            px, py = _chip_peer(x, y, k)
            cps.append(pltpu.make_async_remote_copy(
                src_ref=refs[t].at[2 * px + py], dst_ref=refs[n + t].at[2 * px + py if receiving else chip],
                send_sem=send.at[3 * t + k - 1], recv_sem=recv.at[3 * t + k - 1],
                device_id=(px, py, c), device_id_type=MESH))
    return cps


def _reduce_chips_start(partials, after, name):
    n = len(partials)
    lands = [lax.empty(p.shape, BF16) for p in partials]

    def issue(refs, send, recv):
        for cp in _chip_copies(refs, send, recv, n, False):
            cp.start()

    return _split_start(name, list(partials) + lands, 3 * n, after, issue)


def _reduce_chips_wait(send, recv, arrays, after, name):
    n = len(arrays) // 2

    def await_all(refs, send_ref, recv_ref):
        for cp in _chip_copies(refs, send_ref, recv_ref, n, True):
            cp.wait_send()
            cp.wait_recv()

    res = _split_wait(name, arrays, send, recv, after, await_all)
    return res[:n], res[n:]


def _sum_partials(land, partial, dev_idx, name):
    _, rh, cols = land.shape
    tr = min(rh, 256)
    nb = rh // tr

    def body(idx_ref, l_ref, p_ref, o_ref):
        chip = idx_ref[1]
        acc = jnp.where(chip == 0, p_ref[...], l_ref[0]).astype(F32)
        for s in range(1, N_CHIPS):
            acc = acc + jnp.where(chip == s, p_ref[...], l_ref[s]).astype(F32)
        o_ref[...] = acc

    return pl.pallas_call(
        body, name=name,
        grid_spec=pltpu.PrefetchScalarGridSpec(
            num_scalar_prefetch=1, grid=(nb,),
            in_specs=[pl.BlockSpec((N_CHIPS, tr, cols), lambda i, idx: (0, i, 0)),
                      pl.BlockSpec((None, tr, cols), lambda i, idx: (idx[1], i, 0))],
            out_specs=pl.BlockSpec((tr, cols), lambda i, idx: (idx[2] * nb + i, 0))),
        out_shape=jax.ShapeDtypeStruct((2 * rh, cols), F32), compiler_params=_params("parallel"),
    )(dev_idx, land, partial)


def _half_copies(refs, send, recv, receiving):
    x, y, c = _position()
    cps = []
    for t, ref in enumerate(refs):
        rh = ref.shape[0] // 2
        cps.append(pltpu.make_async_remote_copy(
            src_ref=ref.at[pl.ds(c * rh, rh)], dst_ref=ref.at[pl.ds(((1 - c) if receiving else c) * rh, rh)],
            send_sem=send.at[t], recv_sem=recv.at[t], device_id=(x, y, 1 - c), device_id_type=MESH))
    return cps


def _share_halves_start(totals, after, name, sibling_barrier):
    def issue(refs, send, recv):
        for cp in _half_copies(refs, send, recv, False):
            cp.start()

    return _split_start(name, list(totals), len(totals), after, issue, sibling_barrier)


def _share_halves_wait(send, recv, totals, after, name):
    def await_all(refs, send_ref, recv_ref):
        for cp in _half_copies(refs, send_ref, recv_ref, True):
            cp.wait_send()
            cp.wait_recv()

    return _split_wait(name, totals, send, recv, after, await_all)


SMALL_ROWS = 56


def _small_copies(refs, send, recv, receiving):
    x, y, c = _position()
    me = 4 * x + 2 * y + c
    cps = []
    for k in range(1, N_DEV):
        px, py, pc = _xor_peer(x, y, c, k)
        cps.append(pltpu.make_async_remote_copy(
            src_ref=refs[0], dst_ref=refs[1].at[4 * px + 2 * py + pc if receiving else me],
            send_sem=send.at[k - 1], recv_sem=recv.at[k - 1], device_id=(px, py, pc), device_id_type=MESH))
    return cps


def _small_gather_start(packed, after):
    land = lax.empty((N_DEV,) + packed.shape, F32)

    def issue(refs, send, recv):
        for cp in _small_copies(refs, send, recv, False):
            cp.start()

    return _split_start("small_gather_start", [packed, land], N_DEV - 1, after, issue)


def _small_gather_wait(send, recv, arrays, after):
    def await_all(refs, send_ref, recv_ref):
        for cp in _small_copies(refs, send_ref, recv_ref, True):
            cp.wait_send()
            cp.wait_recv()

    return _split_wait("small_gather_wait", arrays, send, recv, after, await_all)


def _reduce_small(packed, land, silu_c):
    ns = 3 * D_MODEL // N_CHIPS

    def body(p_ref, land_ref, sc_ref, tot_ref, gw_ref, loss_ref, qk_ref, allp):
        x, y, c = _position()
        me = 4 * x + 2 * y + c
        chip = 2 * x + y
        for i in range(N_DEV):
            allp[i] = jnp.where(me == i, p_ref[...], land_ref[i])
        tot = allp[0]
        for i in range(1, N_DEV):
            tot = tot + allp[i]
        tot_ref[...] = tot
        loss_ref[...] = jnp.sum(tot[11:12, :], axis=1, keepdims=True) * (0.5 / D_MODEL)
        fold = tot[5:11, 0:HEAD_DIM]
        for h in range(1, N_HEADS):
            fold = fold + tot[5:11, h * HEAD_DIM:(h + 1) * HEAD_DIM]
        qk_ref[...] = jnp.concatenate([fold, jnp.zeros((2, HEAD_DIM), F32)], axis=0)
        sct = sc_ref[...].T
        rc = 64
        for l in range(2):
            dms = [allp[i, pl.ds(12 + 4 * l + chip, 1), :][:, :ns] for i in range(N_DEV)]
            for r0 in range(0, D_MODEL, rc):
                acc = sct[r0:r0 + rc, 0:1] * dms[0]
                for i in range(1, N_DEV):
                    acc = acc + sct[r0:r0 + rc, i:i + 1] * dms[i]
                gw_ref[l, r0:r0 + rc, :] = acc

    vm = pl.BlockSpec(memory_space=pltpu.VMEM)
    return pl.pallas_call(
        body, name="reduce_small", in_specs=[vm, vm, vm], out_specs=[vm] * 4,
        out_shape=[jax.ShapeDtypeStruct((SMALL_ROWS, D_MODEL), F32), jax.ShapeDtypeStruct((2, D_MODEL, ns), F32),
                   jax.ShapeDtypeStruct((1, 1), F32), jax.ShapeDtypeStruct((8, HEAD_DIM), F32)],
        scratch_shapes=[pltpu.VMEM((N_DEV, SMALL_ROWS, D_MODEL), F32)],
        compiler_params=pltpu.CompilerParams(vmem_limit_bytes=VMEM_LIMIT_BYTES),
    )(packed, land, silu_c)


def kernel(x, c, norm_g, ada_w, ada_b, a_w_in, a_conv_w, a_conv_b, a_ln_g, a_ln_b, a_w_out, b_w_in, b_q_norm, b_k_norm, b_w_out, loss_target, m_norm_g, m_ada_w, m_ada_b, m_a_w_in, m_a_conv_w, m_a_conv_b, m_a_ln_g, m_a_ln_b, m_a_w_out, m_b_w_in, m_b_q_norm, m_b_k_norm, m_b_w_out, v_norm_g, v_ada_w, v_ada_b, v_a_w_in, v_a_conv_w, v_a_conv_b, v_a_ln_g, v_a_ln_b, v_a_w_out, v_b_w_in, v_b_q_norm, v_b_k_norm, v_b_w_out):
    chip = 2 * lax.axis_index("x") + lax.axis_index("y")
    core = lax.axis_index("c")
    chip_idx = chip.astype(jnp.int32).reshape(1)
    dev_idx = jnp.stack([2 * chip + core, chip, core]).astype(jnp.int32)

    land_a_in, own_wa_in = _cast_into_slot(a_w_in[0], chip_idx, "cast_a_w_in", keep_own=True)
    lands_a = [land_a_in, _cast_into_slot(a_w_out[0], chip_idx, "cast_a_w_out")]
    mods, silu_c, conv_w_full = _ada_forward(c, ada_w, ada_b, a_conv_w[0], after=tuple(lands_a))
    send_a, recv_a, lands_a, token_a = _gather_start(lands_a, mods, "gather_start_a")
    land_b_in, own_wb_in = _cast_into_slot(b_w_in[0], chip_idx, "cast_b_w_in", keep_own=True, after=token_a)
    lands_b = [land_b_in, _cast_into_slot(b_w_out[0], chip_idx, "cast_b_w_out", after=token_a)]
    send_b, recv_b, lands_b, token_b = _gather_start(lands_b, token_a, "gather_start_b")
    mods = mods + token_b[0:2, 0:1]

    def weights_a(after):
        send, recv, lands, _ = _gather_forward(send_a, recv_a, lands_a, after, "gather_forward_a",
                                               SIBLING_BARRIERS["gather_forward_a"])
        w_in, w_out = _gather_wait(send, recv, lands, after, "gather_wait_a")
        return w_in, w_out.reshape(D_MODEL, D_MODEL)

    forwarded_b = []

    def weights_b(after):
        send, recv, lands, _ = forwarded_b
        w_in, w_out = _gather_wait(send, recv, lands, after, "gather_wait_b")
        return w_in, w_out.reshape(D_MODEL, D_MODEL)

    def forward_weights_b(after):
        forwarded_b.extend(_gather_forward(send_b, recv_b, lands_b, after, "gather_forward_b",
                                           SIBLING_BARRIERS["gather_forward_b"]))

    stage1, stage2 = {}, {}

    def send_grads(tag, dw_in, dw_out):
        grads = [dw_in, dw_out.reshape(N_CHIPS, D_MODEL // N_CHIPS, D_MODEL)]
        send, recv, arrays, token = _reduce_sibling_start(grads, dw_out, f"reduce_d2d_start_{tag}",
                                                          SIBLING_BARRIERS[f"reduce_d2d_start_{tag}"])
        stage1[tag] = (send, recv, arrays)
        return token

    def forward_grads(tag, after):
        send, recv, arrays = stage1[tag]
        grads, got = _reduce_sibling_wait(send, recv, arrays, after, f"reduce_d2d_wait_{tag}")
        partials = [_add_sibling_half(grads[i], got[i], dev_idx, f"reduce_add_{tag}_{i}") for i in range(2)]
        send, recv, arrays, token = _reduce_chips_start(partials, partials[1], f"reduce_ici_start_{tag}")
        stage2[tag] = (send, recv, arrays)
        return token

    stage3 = {}

    def sum_grads(tag, after):
        send, recv, arrays = stage2[tag]
        partials, lands = _reduce_chips_wait(send, recv, arrays, after, f"reduce_ici_wait_{tag}")
        totals = [_sum_partials(lands[i], partials[i], dev_idx, f"reduce_sum_{tag}_{i}") for i in range(2)]
        send, recv, totals, token = _share_halves_start(totals, totals[1], f"reduce_share_start_{tag}",
                                                        SIBLING_BARRIERS[f"reduce_share_start_{tag}"])
        stage3[tag] = (send, recv, totals)
        return token

    def finish_grads(tag, after):
        send, recv, totals = stage3[tag]
        return _share_halves_wait(send, recv, totals, after, f"reduce_share_wait_{tag}")

    grad_x, small = _local_step(
        x[0], loss_target[0], mods.reshape(2, 3, D_MODEL), norm_g, conv_w_full, a_conv_b, a_ln_g[0:1],
        a_ln_b[0:1], b_q_norm[0], b_k_norm[0], chip.astype(jnp.int32), own_wa_in, own_wb_in,
        weights_a, weights_b, forward_weights_b,
        functools.partial(send_grads, "b"), functools.partial(forward_grads, "b"), functools.partial(send_grads, "a"))

    ns = 3 * D_MODEL // N_CHIPS
    pad_mod = lambda dm: jnp.pad(dm.reshape(N_CHIPS, ns), ((0, 0), (0, D_MODEL - ns)))
    packed = jnp.concatenate([
        small["dnorm_g"], small["dconv_b"], small["dln_g"], small["dln_b"], small["dq_norm"], small["dk_norm"],
        small["loss_cols"], pad_mod(small["dmod0"]), pad_mod(small["dmod1"]), small["dconv_w"],
        jnp.zeros((SMALL_ROWS - 20 - CONV_WIDTH, D_MODEL), F32)], axis=0)
    send_s, recv_s, small_arrays, token_s = _small_gather_start(packed, packed)

    given = dict(norm_g=(norm_g, m_norm_g, v_norm_g), ada_w=(ada_w, m_ada_w, v_ada_w), ada_b=(ada_b, m_ada_b, v_ada_b),
                 a_w_in=(a_w_in, m_a_w_in, v_a_w_in), a_conv_w=(a_conv_w, m_a_conv_w, v_a_conv_w),
                 a_conv_b=(a_conv_b, m_a_conv_b, v_a_conv_b), a_ln_g=(a_ln_g, m_a_ln_g, v_a_ln_g),
                 a_ln_b=(a_ln_b, m_a_ln_b, v_a_ln_b), a_w_out=(a_w_out, m_a_w_out, v_a_w_out),
                 b_w_in=(b_w_in, m_b_w_in, v_b_w_in), b_q_norm=(b_q_norm, m_b_q_norm, v_b_q_norm),
                 b_k_norm=(b_k_norm, m_b_k_norm, v_b_k_norm), b_w_out=(b_w_out, m_b_w_out, v_b_w_out))
    order = ["norm_g", "ada_w", "ada_b", "a_w_in", "a_conv_w", "a_conv_b", "a_ln_g", "a_ln_b", "a_w_out", "b_w_in",
             "b_q_norm", "b_k_norm", "b_w_out"]
    outs = {}

    def update(k, g2, after=None, copy_grad=False):
        w, m, v = given[k]
        shape2 = g2.shape
        res = _adamw(w.reshape(shape2), g2, m.reshape(shape2), v.reshape(shape2), f"adamw_{k}", after, copy_grad)
        outs[k] = tuple(a.reshape(w.shape) for a in ((res[3] if copy_grad else g2), res[0], res[1], res[2]))

    token = forward_grads("a", token_s)
    token = sum_grads("b", token)
    packed, land = _small_gather_wait(send_s, recv_s, small_arrays, token)
    tot, g_ada_w, loss, qk = _reduce_small(packed, land, silu_c)
    g_b_in, g_b_out = finish_grads("b", tot)
    update("b_w_in", g_b_in, copy_grad=True)
    update("b_w_out", g_b_out, copy_grad=True)
    token = sum_grads("a", outs["b_w_in"][1])
    cw = D_MODEL // N_CHIPS
    g_small = dict(
        norm_g=tot[0:2], a_conv_b=tot[2:3], a_ln_g=tot[3:4], a_ln_b=tot[4:5],
        b_q_norm=qk[0:3], b_k_norm=qk[3:6],
        ada_b=jnp.stack([tot[12:16, :ns].reshape(3 * D_MODEL), tot[16:20, :ns].reshape(3 * D_MODEL)]),
        a_conv_w=lax.dynamic_slice(tot[20:20 + CONV_WIDTH], (0, chip * cw), (CONV_WIDTH, cw)),
    )
    update("ada_w", g_ada_w.reshape(2 * D_MODEL, ns), after=token)
    for k, g2 in g_small.items():
        update(k, g2, after=token)
    g_a_in, g_a_out = finish_grads("a", outs["ada_w"][1])
    update("a_w_in", g_a_in, copy_grad=True)
    update("a_w_out", g_a_out, copy_grad=True)
    return (loss.reshape(()), grad_x[None], *[outs[k][0] for k in order], *[outs[k][1] for k in order],
            *[outs[k][2] for k in order], *[outs[k][3] for k in order])
```

```python
import functools

import jax
import jax.numpy as jnp
from jax import lax
from jax.experimental import pallas as pl
from jax.experimental.pallas import tpu as pltpu

F32 = jnp.float32
BF16 = jnp.bfloat16

SEQ = 2048
D_MODEL = 1024
CONV_WIDTH = 31
HEAD_DIM = 64
N_HEADS = 16
DILATIONS = (1, 4, 16)
ATTN_BLOCK = 128
NORM_EPS = 1e-6
NEG_INF = -1e30
N_DEV = 8
N_CHIPS = 4

ADAM_LR = 0.001
ADAM_B1 = 0.9
ADAM_B2 = 0.999
ADAM_EPS = 1e-08
ADAM_WD = 0.01
ADAM_STEP = 10

VMEM_LIMIT_BYTES = 52 * 1024 * 1024
HALO = 32
LANES = 128
ROW_TILE = 512
MESH = pl.DeviceIdType.MESH


def _params(*sem):
    return pltpu.CompilerParams(dimension_semantics=sem or None, vmem_limit_bytes=VMEM_LIMIT_BYTES)


def _sigmoid(v):
    return 1.0 / (1.0 + jnp.exp(-v))


def _row_spec(tm, cols, col_block=0):
    return pl.BlockSpec((tm, cols), lambda i: (i, col_block))


def _vec_spec(rows, cols):
    return pl.BlockSpec((rows, cols), lambda i: (0, 0))


def _normmod(xv, g, scale, shift):
    r = lax.rsqrt(jnp.mean(xv * xv, axis=-1, keepdims=True) + NORM_EPS)
    return xv * r * g * (1.0 + scale) + shift


def _normmod_fwd(x, g, scale, shift, name):
    tm = ROW_TILE

    def body(x_ref, g_ref, sc_ref, sh_ref, h_ref, ht_ref):
        h = _normmod(x_ref[...], g_ref[...], sc_ref[...], sh_ref[...])
        h_ref[...] = h.astype(BF16)
        ht_ref[...] = h.T.astype(BF16)

    return pl.pallas_call(
        body, name=name, grid=(SEQ // tm,),
        in_specs=[_row_spec(tm, D_MODEL)] + [_vec_spec(1, D_MODEL)] * 3,
        out_specs=[_row_spec(tm, D_MODEL), pl.BlockSpec((D_MODEL, tm), lambda i: (0, i))],
        out_shape=[jax.ShapeDtypeStruct((SEQ, D_MODEL), BF16), jax.ShapeDtypeStruct((D_MODEL, SEQ), BF16)],
        compiler_params=_params("parallel"),
    )(x, g, scale, shift)


def _normmod_bwd(x, g, scale, dh_parts, dres, name, part_dilations=None, gated=None):
    tm = ROW_TILE
    n_parts = len(dh_parts)
    dils = part_dilations or (1,) * n_parts
    dh_parts = [p if d == 1 else p.reshape(d, SEQ // d, D_MODEL) for p, d in zip(dh_parts, dils)]
    n_gated = 0 if gated is None else 2

    def body(x_ref, g_ref, sc_ref, dres_ref, *rest):
        part_refs = rest[:n_parts]
        gated_refs = rest[n_parts:n_parts + n_gated]
        out_refs = rest[n_parts + n_gated:]
        dx_ref, sums_ref, nat = out_refs[0], out_refs[1], out_refs[-1]
        xv = x_ref[...]
        r = lax.rsqrt(jnp.mean(xv * xv, axis=-1, keepdims=True) + NORM_EPS)
        xn = xv * r
        dh = _load_natural(part_refs[0], nat, dils[0])
        for p, d in zip(part_refs[1:], dils[1:]):
            dh = dh + _load_natural(p, nat, d)
        gv = g_ref[...]
        one_sc = 1.0 + sc_ref[...]
        dxn = dh * (gv * one_sc)
        dx = dres_ref[...] + r * (dxn - xn * jnp.mean(dxn * xn, axis=-1, keepdims=True))
        dx_ref[...] = dx
        dhx = dh * xn
        rows = [jnp.sum(dhx, axis=0, keepdims=True) * one_sc,
                jnp.sum(dhx, axis=0, keepdims=True) * gv,
                jnp.sum(dh, axis=0, keepdims=True)]
        if gated is not None:
            gate_ref, y_ref = gated_refs
            out_refs[2][...] = (dx * gate_ref[...]).astype(BF16)
            rows.append(jnp.sum(dx * y_ref[...].astype(F32), axis=0, keepdims=True))
        sums = jnp.concatenate(rows + [jnp.zeros((8 - len(rows), D_MODEL), F32)], axis=0)

        @pl.when(pl.program_id(0) == 0)
        def _():
            sums_ref[...] = jnp.zeros_like(sums_ref)

        sums_ref[...] += sums

    gated_specs = [] if gated is None else [_vec_spec(1, D_MODEL), _row_spec(tm, D_MODEL)]
    dy_spec = [] if gated is None else [_row_spec(tm, D_MODEL)]
    dy_shape = [] if gated is None else [jax.ShapeDtypeStruct((SEQ, D_MODEL), BF16)]
    return pl.pallas_call(
        body, name=name, grid=(SEQ // tm,),
        in_specs=[_row_spec(tm, D_MODEL), _vec_spec(1, D_MODEL), _vec_spec(1, D_MODEL), _row_spec(tm, D_MODEL)]
        + [_class_spec(tm, d) for d in dils] + gated_specs,
        out_specs=[_row_spec(tm, D_MODEL), _vec_spec(8, D_MODEL)] + dy_spec,
        out_shape=[jax.ShapeDtypeStruct((SEQ, D_MODEL), F32), jax.ShapeDtypeStruct((8, D_MODEL), F32)] + dy_shape,
        scratch_shapes=[_natural_scratch(tm)],
        compiler_params=_params("arbitrary"),
    )(x, g, scale, dres, *dh_parts, *(gated or ()))


def _mm(lhs, rhs, *, tn, tile0, n_tiles, out_dtype, name, out3d=None, prev=None, transpose_lhs=False):
    mo, kc = lhs.shape[::-1] if transpose_lhs else lhs.shape
    cm = min(mo, 1024)
    tc = 256

    def body(l_ref, r_ref, *rest):
        if transpose_lhs:
            o_ref, lt_ref = rest[-2], rest[-1]

            @pl.when(pl.program_id(0) == 0)
            def _():
                for c in range(kc // tc):
                    lt_ref[:, c * tc:(c + 1) * tc] = l_ref[c * tc:(c + 1) * tc, :].astype(F32).T.astype(l_ref.dtype)
        else:
            o_ref, lt_ref = rest[-1], l_ref
        for m in range(mo // cm):
            rows = pl.ds(m * cm, cm)
            o_ref[rows, :] = jnp.dot(lt_ref[rows, :], r_ref[...], preferred_element_type=F32).astype(out_dtype)

    if rhs.ndim == 3:
        tps_r = rhs.shape[2] // tn
        r_spec = pl.BlockSpec((None, kc, tn), lambda t: ((tile0 + t) // tps_r, 0, (tile0 + t) % tps_r))
    else:
        r_spec = pl.BlockSpec((kc, tn), lambda t: (0, t))
    in_specs = [pl.BlockSpec(lhs.shape, lambda t: (0, 0)), r_spec]
    args = [lhs, rhs]
    aliases = {}
    if out3d is None:
        o_spec = pl.BlockSpec((mo, tn), lambda t: (0, t))
        o_shape = jax.ShapeDtypeStruct((mo, n_tiles * tn), out_dtype)
    else:
        j_out, ns_out = out3d
        tps_o = ns_out // tn
        o_spec = pl.BlockSpec((None, mo, tn), lambda t: ((tile0 + t) // tps_o, 0, (tile0 + t) % tps_o))
        o_shape = jax.ShapeDtypeStruct((j_out, mo, ns_out), out_dtype)
        if prev is not None:
            in_specs.append(pl.BlockSpec(memory_space=pl.ANY))
            args.append(prev)
            aliases = {2: 0}
    return pl.pallas_call(
        body, name=name, grid=(n_tiles,), in_specs=in_specs, out_specs=o_spec, out_shape=o_shape,
        input_output_aliases=aliases,
        scratch_shapes=[pltpu.VMEM((mo, kc), lhs.dtype)] if transpose_lhs else [],
        compiler_params=_params("arbitrary" if transpose_lhs else "parallel"),
    )(*args)


def _in_tiles(h_parts, w3, tile_ids, n_tiles, *, tn, total_tiles, part_of, name, prev=None):
    _, kc, ns = w3.shape
    tps = ns // tn
    cm = 1024
    n_parts = len(h_parts)

    def body(ids_ref, *rest):
        h_refs, w_ref, o_ref = rest[:n_parts], rest[n_parts], rest[-1]
        part = part_of(ids_ref[1, pl.program_id(0)])
        for g, h_ref in enumerate(h_refs):
            @pl.when(part == g)
            def _():
                for m in range(SEQ // cm):
                    rows = pl.ds(m * cm, cm)
                    o_ref[rows, :] = jnp.dot(h_ref[rows, :], w_ref[...], preferred_element_type=F32).astype(BF16)

    resident = pl.BlockSpec((SEQ, kc), lambda t, ids: (0, 0))
    in_specs = [resident] * n_parts + [
        pl.BlockSpec((None, kc, tn), lambda t, ids: (ids[0, t] // tps, 0, ids[0, t] % tps))]
    args = [*h_parts, w3]
    aliases = {}
    if prev is not None:
        in_specs.append(pl.BlockSpec(memory_space=pl.ANY))
        args.append(prev)
        aliases = {n_parts + 2: 0}
    return pl.pallas_call(
        body, name=name,
        grid_spec=pltpu.PrefetchScalarGridSpec(
            num_scalar_prefetch=1, grid=(n_tiles,), in_specs=in_specs,
            out_specs=pl.BlockSpec((SEQ, tn), lambda t, ids: (0, ids[1, t]))),
        out_shape=jax.ShapeDtypeStruct((SEQ, total_tiles * tn), BF16),
        input_output_aliases=aliases, compiler_params=_params("arbitrary"),
    )(tile_ids, *args)


def _own_first(chip, total_tiles):
    own = total_tiles // N_CHIPS
    step = jnp.arange(total_tiles, dtype=jnp.int32)
    tiles = (own * chip + step) % total_tiles
    return jnp.stack([step[:own], tiles[:own]]), jnp.stack([tiles[own:], tiles[own:]]), own


def _mm_nt(dy, w3, *, tn, tile0, n_tiles, name, after=None):
    m_rows = dy.shape[0]
    _, kc, ns = w3.shape
    tps = ns // tn
    cm = 512
    extra = [] if after is None else [after]

    def body(dy_ref, w_ref, *rest):
        o_ref, acc = rest[-2], rest[-1]
        t = pl.program_id(0)

        @pl.when(t == 0)
        def _():
            acc[...] = jnp.zeros_like(acc)

        for m in range(m_rows // cm):
            rows = pl.ds(m * cm, cm)
            acc[rows, :] += lax.dot_general(dy_ref[rows, :], w_ref[...], NT_DIMS, preferred_element_type=F32)

        @pl.when(t == n_tiles - 1)
        def _():
            o_ref[...] = acc[...].astype(BF16)

    return pl.pallas_call(
        body, name=name, grid=(n_tiles,),
        in_specs=[pl.BlockSpec((m_rows, tn), lambda t: (0, t)),
                  pl.BlockSpec((None, kc, tn), lambda t: ((tile0 + t) // tps, 0, (tile0 + t) % tps))]
        + [pl.BlockSpec(memory_space=pl.ANY)] * len(extra),
        out_specs=pl.BlockSpec((m_rows, kc), lambda t: (0, 0)),
        out_shape=jax.ShapeDtypeStruct((m_rows, kc), BF16),
        scratch_shapes=[pltpu.VMEM((m_rows, kc), F32)],
        compiler_params=_params("arbitrary"),
    )(dy, w3, *extra)


CONV_CHUNK = 16


def _shift_copies(buf, shifted):
    rows = shifted.shape[1]
    for s in range(1, 8):
        shifted[s - 1] = buf[pl.ds(s, rows), :]


def _shifted_rows(buf, shifted, offset, r0):
    s = offset % 8
    if s == 0:
        return buf[pl.ds(r0 + offset, CONV_CHUNK), :]
    return shifted[s - 1, pl.ds(r0 + (offset - s), CONV_CHUNK), :]


def _spread_taps(w_ref, taps):
    for k in range(CONV_WIDTH):
        taps[k] = jnp.broadcast_to(w_ref[k:k + 1, :], (8, D_MODEL))


def _times_tap(taps, k, rows):
    return (rows.reshape(CONV_CHUNK // 8, 8, D_MODEL) * taps[k][None]).reshape(CONV_CHUNK, D_MODEL)


def _conv_fwd(proj, conv_w, conv_b, ln_g, ln_b, name):
    tm = ROW_TILE
    hb = tm // HALO

    def body(vg_ref, halo_ref, z_ref, w_ref, b_ref, g_ref, be_ref, u5_ref, u5t_ref, u2_ref, buf, shifted, taps):
        i = pl.program_id(0)
        u1 = vg_ref[:, :D_MODEL].astype(F32) * _sigmoid(vg_ref[:, D_MODEL:].astype(F32))
        u1h = halo_ref[:, :D_MODEL].astype(F32) * _sigmoid(halo_ref[:, D_MODEL:].astype(F32))
        buf[pl.ds(0, HALO), :] = jnp.where(i > 0, u1h, 0.0)
        buf[pl.ds(HALO, tm), :] = u1
        _shift_copies(buf, shifted)
        _spread_taps(w_ref, taps)

        def chunk(ci, carry):
            r0 = pl.multiple_of(ci * CONV_CHUNK, CONV_CHUNK)
            acc = jnp.broadcast_to(b_ref[...], (CONV_CHUNK, D_MODEL))
            for k in range(CONV_WIDTH):
                acc = acc + _times_tap(taps, k, _shifted_rows(buf, shifted, HALO - (CONV_WIDTH - 1) + k, r0))
            u2_ref[pl.ds(r0, CONV_CHUNK), :] = acc
            return carry

        lax.fori_loop(0, tm // CONV_CHUNK, chunk, 0)
        acc = u2_ref[...]
        mu = jnp.mean(acc, axis=-1, keepdims=True)
        xc = acc - mu
        rstd = lax.rsqrt(jnp.mean(xc * xc, axis=-1, keepdims=True) + NORM_EPS)
        u3 = xc * rstd * g_ref[...] + be_ref[...]
        zv = z_ref[...].astype(F32)
        u5 = u3 * _sigmoid(u3) * (zv * _sigmoid(zv))
        u5_ref[...] = u5.astype(BF16)
        u5t_ref[...] = u5.T.astype(BF16)

    return pl.pallas_call(
        body, name=name, grid=(SEQ // tm,),
        in_specs=[pl.BlockSpec((tm, 2 * D_MODEL), lambda i: (i, 0)),
                  pl.BlockSpec((HALO, 2 * D_MODEL), lambda i: (jnp.maximum(i * hb - 1, 0), 0)),
                  _row_spec(tm, D_MODEL, 2),
                  _vec_spec(CONV_WIDTH, D_MODEL)] + [_vec_spec(1, D_MODEL)] * 3,
        out_specs=[_row_spec(tm, D_MODEL), pl.BlockSpec((D_MODEL, tm), lambda i: (0, i)), _row_spec(tm, D_MODEL)],
        out_shape=[jax.ShapeDtypeStruct((SEQ, D_MODEL), BF16), jax.ShapeDtypeStruct((D_MODEL, SEQ), BF16),
                   jax.ShapeDtypeStruct((SEQ, D_MODEL), F32)],
        scratch_shapes=[pltpu.VMEM((HALO + tm, D_MODEL), F32), pltpu.VMEM((7, HALO + tm - 8, D_MODEL), F32),
                        pltpu.VMEM((CONV_WIDTH, 8, D_MODEL), F32)],
        compiler_params=_params("parallel"),
    )(proj, proj, proj, conv_w, conv_b, ln_g, ln_b)


def _conv_bwd_pointwise(dy, w_out, proj, u2, ln_g, ln_b, name):
    tm = ROW_TILE

    def body(dy_ref, w_ref, z_ref, u2_ref, g_ref, be_ref, du2_ref, dz_ref, sums_ref):
        u2v = u2_ref[...]
        mu = jnp.mean(u2v, axis=-1, keepdims=True)
        xc = u2v - mu
        rstd = lax.rsqrt(jnp.mean(xc * xc, axis=-1, keepdims=True) + NORM_EPS)
        xhat = xc * rstd
        u3 = xhat * g_ref[...] + be_ref[...]
        s3 = _sigmoid(u3)
        u4 = u3 * s3
        zv = z_ref[...].astype(F32)
        sz = _sigmoid(zv)
        du5v = lax.dot_general(dy_ref[...], w_ref[...], NT_DIMS, preferred_element_type=F32)
        dz_ref[...] = du5v * u4 * (sz * (1.0 + zv * (1.0 - sz)))
        du3 = du5v * (zv * sz) * (s3 * (1.0 + u3 * (1.0 - s3)))
        dxhat = du3 * g_ref[...]
        du2 = rstd * (dxhat - jnp.mean(dxhat, axis=-1, keepdims=True)
                      - xhat * jnp.mean(dxhat * xhat, axis=-1, keepdims=True))
        du2_ref[...] = du2
        sums = jnp.concatenate([
            jnp.sum(du3 * xhat, axis=0, keepdims=True),
            jnp.sum(du3, axis=0, keepdims=True),
            jnp.sum(du2, axis=0, keepdims=True),
            jnp.zeros((5, D_MODEL), F32)], axis=0)

        @pl.when(pl.program_id(0) == 0)
        def _():
            sums_ref[...] = jnp.zeros_like(sums_ref)

        sums_ref[...] += sums

    return pl.pallas_call(
        body, name=name, grid=(SEQ // tm,),
        in_specs=[_row_spec(tm, D_MODEL), _vec_spec(D_MODEL, D_MODEL), _row_spec(tm, D_MODEL, 2),
                  _row_spec(tm, D_MODEL), _vec_spec(1, D_MODEL), _vec_spec(1, D_MODEL)],
        out_specs=[_row_spec(tm, D_MODEL), _row_spec(tm, D_MODEL), _vec_spec(8, D_MODEL)],
        out_shape=[jax.ShapeDtypeStruct((SEQ, D_MODEL), F32), jax.ShapeDtypeStruct((SEQ, D_MODEL), F32),
                   jax.ShapeDtypeStruct((8, D_MODEL), F32)],
        compiler_params=_params("arbitrary"),
    )(dy, w_out, proj, u2, ln_g, ln_b)


def _conv_bwd_taps(du2, dz, proj, conv_w, name):
    tm = ROW_TILE
    hb = tm // HALO
    n_blocks = SEQ // tm

    def body(du2_ref, dnext_ref, dz_ref, vg_ref, w_ref, dproj_ref, dw_ref, dbuf, dshift, sgbuf, ubuf, dwacc, taps):
        i = pl.program_id(0)
        _spread_taps(w_ref, taps)
        sg = _sigmoid(vg_ref[:, D_MODEL:].astype(F32))
        sgbuf[...] = sg
        ubuf[...] = vg_ref[:, :D_MODEL].astype(F32) * sg
        dbuf[pl.ds(0, tm), :] = du2_ref[...]
        dbuf[pl.ds(tm, HALO), :] = jnp.where(i < n_blocks - 1, dnext_ref[...], 0.0)
        _shift_copies(dbuf, dshift)

        @pl.when(i == 0)
        def _():
            dwacc[...] = jnp.zeros_like(dwacc)

        def chunk(ci, carry):
            r0 = pl.multiple_of(ci * CONV_CHUNK, CONV_CHUNK)
            rows = pl.ds(r0, CONV_CHUNK)
            u1c = ubuf[rows, :]
            du1 = jnp.zeros((CONV_CHUNK, D_MODEL), F32)
            for k in range(CONV_WIDTH):
                ahead = _shifted_rows(dbuf, dshift, CONV_WIDTH - 1 - k, r0)
                du1 = du1 + _times_tap(taps, k, ahead)
                prod = u1c * ahead
                dwacc[k] += prod[0:8] + prod[8:16]
            sgc = sgbuf[rows, :]
            dval = du1 * sgc
            dproj_ref[rows, 0:D_MODEL] = dval.astype(BF16)
            dproj_ref[rows, D_MODEL:2 * D_MODEL] = (
                dval * vg_ref[rows, 0:D_MODEL].astype(F32) * (1.0 - sgc)).astype(BF16)
            return carry

        lax.fori_loop(0, tm // CONV_CHUNK, chunk, 0)
        dproj_ref[:, 2 * D_MODEL:] = dz_ref[...].astype(BF16)

        @pl.when(i == n_blocks - 1)
        def _():
            for k in range(CONV_WIDTH):
                dw_ref[k:k + 1, :] = jnp.sum(dwacc[k], axis=0, keepdims=True)
            dw_ref[CONV_WIDTH:, :] = jnp.zeros((32 - CONV_WIDTH, D_MODEL), F32)

    return pl.pallas_call(
        body, name=name, grid=(n_blocks,),
        in_specs=[_row_spec(tm, D_MODEL),
                  pl.BlockSpec((HALO, D_MODEL), lambda i: (jnp.minimum((i + 1) * hb, SEQ // HALO - 1), 0)),
                  _row_spec(tm, D_MODEL),
                  pl.BlockSpec((tm, 2 * D_MODEL), lambda i: (i, 0)),
                  _vec_spec(CONV_WIDTH, D_MODEL)],
        out_specs=[_row_spec(tm, 3 * D_MODEL), _vec_spec(32, D_MODEL)],
        out_shape=[jax.ShapeDtypeStruct((SEQ, 3 * D_MODEL), BF16), jax.ShapeDtypeStruct((32, D_MODEL), F32)],
        scratch_shapes=[pltpu.VMEM((tm + HALO, D_MODEL), F32), pltpu.VMEM((7, HALO + tm - 8, D_MODEL), F32),
                        pltpu.VMEM((tm, D_MODEL), F32), pltpu.VMEM((tm, D_MODEL), F32),
                        pltpu.VMEM((CONV_WIDTH, 8, D_MODEL), F32), pltpu.VMEM((CONV_WIDTH, 8, D_MODEL), F32)],
        compiler_params=_params("arbitrary"),
    )(du2, du2, dz, proj, conv_w)


def _out_a(u5, w_out, x, gate, g1, scale1, shift1, name):
    tm = ROW_TILE
    n_d = len(DILATIONS)

    def body(u_ref, w_ref, x_ref, gate_ref, g_ref, sc_ref, sh_ref, x1_ref, y_ref, ht_ref, *rest):
        h_refs, nat = rest[:n_d], rest[-1]
        y = jnp.dot(u_ref[...], w_ref[...], preferred_element_type=F32)
        x1 = x_ref[...] + gate_ref[...] * y
        y_ref[...] = y.astype(BF16)
        x1_ref[...] = x1
        h = _normmod(x1, g_ref[...], sc_ref[...], sh_ref[...])
        ht_ref[...] = h.T.astype(BF16)
        for h_ref, d in zip(h_refs, DILATIONS):
            _store_classes(h_ref, h, nat, d)

    res = pl.pallas_call(
        body, name=name, grid=(SEQ // tm,),
        in_specs=[_row_spec(tm, D_MODEL), _vec_spec(D_MODEL, D_MODEL), _row_spec(tm, D_MODEL)]
        + [_vec_spec(1, D_MODEL)] * 4,
        out_specs=[_row_spec(tm, D_MODEL), _row_spec(tm, D_MODEL), pl.BlockSpec((D_MODEL, tm), lambda i: (0, i))]
        + [_class_spec(tm, d) for d in DILATIONS],
        out_shape=[jax.ShapeDtypeStruct((SEQ, D_MODEL), F32), jax.ShapeDtypeStruct((SEQ, D_MODEL), BF16),
                   jax.ShapeDtypeStruct((D_MODEL, SEQ), BF16)] + [_class_shape(d, BF16) for d in DILATIONS],
        scratch_shapes=[_natural_scratch(tm)],
        compiler_params=_params("parallel"),
    )(u5, w_out, x, gate, g1, scale1, shift1)
    return res[0], res[1], res[2], [a.reshape(SEQ, D_MODEL) for a in res[3:]]


def _out_b_loss(u, w_out, x1, gate, target, name):
    tm = ROW_TILE

    def body(u_ref, w_ref, x_ref, gate_ref, t_ref, e_ref, dy_ref, sums_ref):
        y = jnp.dot(u_ref[...], w_ref[...], preferred_element_type=F32)
        diff = x_ref[...] + gate_ref[...] * y - t_ref[...]
        e = diff * (1.0 / D_MODEL)
        e_ref[...] = e
        dy_ref[...] = (e * gate_ref[...]).astype(BF16)
        sums = jnp.concatenate([
            jnp.sum(e * y, axis=0, keepdims=True),
            jnp.sum(diff * diff, axis=0, keepdims=True),
            jnp.zeros((6, D_MODEL), F32)], axis=0)

        @pl.when(pl.program_id(0) == 0)
        def _():
            sums_ref[...] = jnp.zeros_like(sums_ref)

        sums_ref[...] += sums

    return pl.pallas_call(
        body, name=name, grid=(SEQ // tm,),
        in_specs=[_row_spec(tm, D_MODEL), _vec_spec(D_MODEL, D_MODEL), _row_spec(tm, D_MODEL),
                  _vec_spec(1, D_MODEL), _row_spec(tm, D_MODEL)],
        out_specs=[_row_spec(tm, D_MODEL), _row_spec(tm, D_MODEL), _vec_spec(8, D_MODEL)],
        out_shape=[jax.ShapeDtypeStruct((SEQ, D_MODEL), F32), jax.ShapeDtypeStruct((SEQ, D_MODEL), BF16),
                   jax.ShapeDtypeStruct((8, D_MODEL), F32)],
        compiler_params=_params("arbitrary"),
    )(u, w_out, x1, gate, target)


def _seg_matrix():
    r = lax.broadcasted_iota(jnp.int32, (256, 256), 0) // HEAD_DIM
    c = lax.broadcasted_iota(jnp.int32, (256, 256), 1) // HEAD_DIM
    return jnp.where(r == c, 1.0 / HEAD_DIM, 0.0).astype(BF16)


def _segmean(v, seg):
    hi = v.astype(BF16)
    lo = (v - hi.astype(F32)).astype(BF16)
    outs = []
    for c0 in range(0, D_MODEL, 256):
        outs.append(jnp.dot(hi[:, c0:c0 + 256], seg, preferred_element_type=F32)
                    + jnp.dot(lo[:, c0:c0 + 256], seg, preferred_element_type=F32))
    return jnp.concatenate(outs, axis=1)


def _qk_rstd(v, seg):
    return lax.rsqrt(_segmean(v * v, seg) + NORM_EPS)


def _qknorm_fwd(proj, group, qw, kw, seg, name):
    tm = ROW_TILE

    def body(q_in, k_in, qw_ref, kw_ref, seg_ref, q_ref, k_ref):
        segv = seg_ref[...]
        q = q_in[...].astype(F32)
        k = k_in[...].astype(F32)
        q_ref[...] = (q * _qk_rstd(q, segv) * qw_ref[...] * HEAD_DIM ** -0.5).astype(BF16)
        k_ref[...] = (k * _qk_rstd(k, segv) * kw_ref[...]).astype(BF16)

    return pl.pallas_call(
        body, name=name, grid=(SEQ // tm,),
        in_specs=[_row_spec(tm, D_MODEL, 3 * group), _row_spec(tm, D_MODEL, 3 * group + 1),
                  _vec_spec(1, D_MODEL), _vec_spec(1, D_MODEL), _vec_spec(256, 256)],
        out_specs=[_row_spec(tm, D_MODEL)] * 2,
        out_shape=[jax.ShapeDtypeStruct((SEQ, D_MODEL), BF16)] * 2,
        compiler_params=_params("parallel"),
    )(proj, proj, qw, kw, seg)


def _attn_masks(b, bpc, dilation, transposed=False):
    keys = ATTN_BLOCK if bpc == 1 else 2 * ATTN_BLOCK
    shape, q_axis = ((keys, ATTN_BLOCK), 1) if transposed else ((ATTN_BLOCK, keys), 0)
    qi = lax.broadcasted_iota(jnp.int32, shape, q_axis)
    kj = lax.broadcasted_iota(jnp.int32, shape, 1 - q_axis)
    if bpc == 1:
        steps = qi - kj
        return (steps * dilation).astype(F32), steps >= 0
    steps = qi + ATTN_BLOCK - kj
    has_prev = (b % bpc) != 0
    valid = (steps >= 0) & (steps <= ATTN_BLOCK) & (has_prev | (kj >= ATTN_BLOCK))
    return (steps * dilation).astype(F32), valid


MASKED = 1e30


def _bias_scratch(bpc):
    return pltpu.VMEM((1 if bpc == 1 else 2, N_HEADS, ATTN_BLOCK, (1 if bpc == 1 else 2) * ATTN_BLOCK), F32)


def _fill_bias(bias_ref, sl_ref, bpc, dilation):
    for variant in range(bias_ref.shape[0]):
        dist, valid = _attn_masks(variant, min(bpc, 2), dilation)
        bias_ref[variant] = jnp.where(valid[None], dist[None] * sl_ref[...], MASKED)


def _step_bias(bias_ref, b, bpc):
    if bpc == 1:
        return bias_ref[0]
    return bias_ref[jnp.where((b % bpc) != 0, 1, 0)]


def _key_tile(prev_ref, cur_ref, cols, bpc):
    if bpc == 1:
        return cur_ref[:, cols]
    return jnp.concatenate([prev_ref[:, cols], cur_ref[:, cols]], axis=0)


ATTN_HEADS_FWD = 16
ATTN_HEADS_BWD = 16
NT_DIMS = (((1,), (1,)), ((), ()))
BATCH_NT_DIMS = (((2,), (2,)), ((0,), (0,)))
BATCH_NN_DIMS = (((2,), (1,)), ((0,), (0,)))
BATCH_TN_DIMS = (((1,), (1,)), ((0,), (0,)))


def _head_stack(tile_of, heads):
    return jnp.stack([tile_of(slice(h * HEAD_DIM, (h + 1) * HEAD_DIM)) for h in range(heads)], axis=0)


def _attn_specs(heads, segment=0):
    width = heads * HEAD_DIM
    off = segment * (D_MODEL // width)
    last = SEQ // ATTN_BLOCK - 1
    cur = pl.BlockSpec((ATTN_BLOCK, width), lambda hg, b: (jnp.minimum(b, last), hg + off))
    prev = pl.BlockSpec((ATTN_BLOCK, width), lambda hg, b: (jnp.clip(b - 1, 0, last), hg + off))
    return cur, prev


def _attn_fwd(q, k, proj, group, slopes, dilation, name):
    bpc = SEQ // dilation // ATTN_BLOCK
    heads = ATTN_HEADS_FWD
    assert heads == N_HEADS
    cur, prev = _attn_specs(heads)
    v_cur, v_prev = _attn_specs(heads, segment=3 * group + 2)

    def body(sl_ref, q_ref, kp_ref, kc_ref, vp_ref, vc_ref, o_ref, lse_ref, bias_ref):
        b = pl.program_id(1)

        @pl.when(b == 0)
        def _():
            _fill_bias(bias_ref, sl_ref, bpc, dilation)

        q3 = _head_stack(lambda cols: q_ref[:, cols], heads)
        k3 = _head_stack(lambda cols: _key_tile(kp_ref, kc_ref, cols, bpc), heads)
        v3 = _head_stack(lambda cols: _key_tile(vp_ref, vc_ref, cols, bpc), heads)
        s = lax.dot_general(q3, k3, BATCH_NT_DIMS, preferred_element_type=F32)
        s = s - _step_bias(bias_ref, b, bpc)
        m = jnp.max(s, axis=-1, keepdims=True)
        p = jnp.exp(s - m)
        l = jnp.sum(p, axis=-1, keepdims=True)
        o3 = lax.dot_general(p.astype(BF16), v3, BATCH_NN_DIMS, preferred_element_type=F32) / l
        lse3 = m + jnp.log(l)
        for h in range(heads):
            o_ref[:, h * HEAD_DIM:(h + 1) * HEAD_DIM] = o3[h].astype(BF16)
        lse_ref[...] = jnp.concatenate([lse3[h] for h in range(heads)]
                                       + [jnp.zeros((ATTN_BLOCK, LANES - heads), F32)], axis=1)

    return pl.pallas_call(
        body, name=name, grid=(N_HEADS // heads, SEQ // ATTN_BLOCK),
        in_specs=[pl.BlockSpec((heads, 1, 1), lambda hg, b: (hg, 0, 0)), cur, prev, cur, v_prev, v_cur],
        out_specs=[cur, pl.BlockSpec((ATTN_BLOCK, LANES), lambda hg, b: (b, 0))],
        out_shape=[jax.ShapeDtypeStruct((SEQ, D_MODEL), BF16), jax.ShapeDtypeStruct((SEQ, LANES), F32)],
        scratch_shapes=[_bias_scratch(bpc)],
        compiler_params=_params("parallel", "arbitrary"),
    )(slopes.reshape(N_HEADS, 1, 1), q, k, k, proj, proj)


def _class_spec(tm, dilation, width=D_MODEL):
    if dilation == 1:
        return _row_spec(tm, width)
    return pl.BlockSpec((dilation, tm // dilation, width), lambda i: (0, i, 0))


def _class_shape(dilation, dtype, width=D_MODEL):
    if dilation == 1:
        return jax.ShapeDtypeStruct((SEQ, width), dtype)
    return jax.ShapeDtypeStruct((dilation, SEQ // dilation, width), dtype)


def _load_natural(in_ref, nat_ref, dilation):
    if dilation == 1:
        return in_ref[...].astype(F32)
    n = nat_ref.shape[1] // dilation
    tiles = in_ref.shape[-1] // LANES
    for r in range(dilation):
        for j in range(tiles):
            nat_ref.at[j][pl.ds(r, n, stride=dilation), :] = in_ref[r, :, j * LANES:(j + 1) * LANES].astype(F32)
    if tiles == 1:
        return nat_ref[0]
    return jnp.concatenate([nat_ref[j] for j in range(tiles)], axis=1)


def _store_classes(out_ref, value, nat_ref, dilation):
    if dilation == 1:
        out_ref[...] = value.astype(out_ref.dtype)
        return
    n = nat_ref.shape[1] // dilation
    tiles = value.shape[-1] // LANES
    for j in range(tiles):
        nat_ref[j] = value[:, j * LANES:(j + 1) * LANES]
    for r in range(dilation):
        for j in range(tiles):
            out_ref[r, :, j * LANES:(j + 1) * LANES] = (
                nat_ref.at[j][pl.ds(r, n, stride=dilation), :].astype(out_ref.dtype))


def _natural_scratch(tm):
    return pltpu.VMEM((D_MODEL // LANES, tm, LANES), F32)


def _head_selector():
    lane_head = lax.broadcasted_iota(jnp.int32, (D_MODEL, LANES), 0) // HEAD_DIM
    head = lax.broadcasted_iota(jnp.int32, (D_MODEL, LANES), 1)
    return (lane_head == head).astype(BF16)


def _dot_split(v, m01, dims):
    hi = v.astype(BF16)
    lo = (v - hi.astype(F32)).astype(BF16)
    return (lax.dot_general(hi, m01, dims, preferred_element_type=F32)
            + lax.dot_general(lo, m01, dims, preferred_element_type=F32))


def _merge_fwd(o_parts, lse_parts, z, sel, name):
    tm = ROW_TILE
    h_spec = pl.BlockSpec((tm, LANES), lambda i: (i, 0))

    def body(o0, o1, o2, l0, l1, l2, z_ref, sel_ref, u_ref, ut_ref, o_ref, lse_ref, nat):
        ls = [_load_natural(l, nat, d) for l, d in zip((l0, l1, l2), DILATIONS)]
        m = jnp.maximum(jnp.maximum(ls[0], ls[1]), ls[2])
        tot = m + jnp.log(jnp.exp(ls[0] - m) + jnp.exp(ls[1] - m) + jnp.exp(ls[2] - m))
        o = jnp.zeros((tm, D_MODEL), F32)
        for o_in, l, d in zip((o0, o1, o2), ls, DILATIONS):
            weight = _dot_split(jnp.exp(l - tot), sel_ref[...], NT_DIMS)
            o = o + weight * _load_natural(o_in, nat, d)
        zv = z_ref[...].astype(F32)
        u = o * (zv * _sigmoid(zv))
        u_ref[...] = u.astype(BF16)
        ut_ref[...] = u.T.astype(BF16)
        o_ref[...] = o.astype(BF16)
        lse_ref[...] = tot

    return pl.pallas_call(
        body, name=name, grid=(SEQ // tm,),
        in_specs=[_class_spec(tm, d) for d in DILATIONS] + [_class_spec(tm, d, LANES) for d in DILATIONS]
        + [_row_spec(tm, D_MODEL, B_Z_SEGMENT), _vec_spec(D_MODEL, LANES)],
        out_specs=[_row_spec(tm, D_MODEL), pl.BlockSpec((D_MODEL, tm), lambda i: (0, i)),
                   _row_spec(tm, D_MODEL), h_spec],
        out_shape=[jax.ShapeDtypeStruct((SEQ, D_MODEL), BF16), jax.ShapeDtypeStruct((D_MODEL, SEQ), BF16),
                   jax.ShapeDtypeStruct((SEQ, D_MODEL), BF16), jax.ShapeDtypeStruct((SEQ, LANES), F32)],
        scratch_shapes=[_natural_scratch(tm)],
        compiler_params=_params("parallel"),
    )(*o_parts, *lse_parts, z, sel)


def _merge_bwd(dy, w_out, o, lse, z, sel, name):
    tm = ROW_TILE
    n_d = len(DILATIONS)

    def body(dy_ref, w_ref, o_ref, lse_ref, z_ref, sel_ref, dz_ref, *rest):
        do_refs, delta_refs, lse_refs, nat = rest[:n_d], rest[n_d:2 * n_d], rest[2 * n_d:3 * n_d], rest[-1]
        zv = z_ref[...].astype(F32)
        sz = _sigmoid(zv)
        duv = lax.dot_general(dy_ref[...], w_ref[...], NT_DIMS, preferred_element_type=F32)
        ov = o_ref[...].astype(F32)
        do = duv * (zv * sz)
        dz_ref[...] = (duv * ov * (sz * (1.0 + zv * (1.0 - sz)))).astype(BF16)
        delta = _dot_split(do * ov, sel_ref[...], (((1,), (0,)), ((), ())))
        lv = lse_ref[...]
        for i, d in enumerate(DILATIONS):
            _store_classes(do_refs[i], do, nat, d)
            _store_classes(delta_refs[i], delta, nat, d)
            _store_classes(lse_refs[i], lv, nat, d)

    res = pl.pallas_call(
        body, name=name, grid=(SEQ // tm,),
        in_specs=[_row_spec(tm, D_MODEL), _vec_spec(D_MODEL, D_MODEL), _row_spec(tm, D_MODEL), _row_spec(tm, LANES),
                  _row_spec(tm, D_MODEL, B_Z_SEGMENT), _vec_spec(D_MODEL, LANES)],
        out_specs=[_row_spec(tm, D_MODEL)] + [_class_spec(tm, d) for d in DILATIONS]
        + [_class_spec(tm, d, LANES) for d in DILATIONS] * 2,
        out_shape=[jax.ShapeDtypeStruct((SEQ, D_MODEL), BF16)] + [_class_shape(d, BF16) for d in DILATIONS]
        + [_class_shape(d, F32, LANES) for d in DILATIONS] * 2,
        scratch_shapes=[_natural_scratch(tm)],
        compiler_params=_params("parallel"),
    )(dy, w_out, o, lse, z, sel)
    flat = lambda a: a.reshape(SEQ, a.shape[-1])
    return (res[0], [flat(a) for a in res[1:1 + n_d]], [flat(a) for a in res[1 + n_d:1 + 2 * n_d]],
            [flat(a) for a in res[1 + 2 * n_d:]])


def _attn_bwd(q, k, proj, group, do, lse, delta, slopes, dilation, name):
    bpc = SEQ // dilation // ATTN_BLOCK
    heads = ATTN_HEADS_BWD
    n_blocks = SEQ // ATTN_BLOCK
    carry = bpc > 1
    width = heads * HEAD_DIM
    cur, prev = _attn_specs(heads)
    v_cur, v_prev = _attn_specs(heads, segment=3 * group + 2)
    assert heads == N_HEADS
    per_head = pl.BlockSpec((ATTN_BLOCK, LANES), lambda hg, b: (jnp.minimum(b, n_blocks - 1), 0))
    scale = HEAD_DIM ** -0.5

    def body(sl_ref, q_ref, kp_ref, kc_ref, vp_ref, vc_ref, do_ref, lse_ref, dl_ref,
             dq_ref, dk_ref, dv_ref, *scratch):
        b = pl.program_id(1)
        if carry:
            dk_carry, dv_carry = scratch

            @pl.when(b == n_blocks)
            def _():
                dk_ref[...] = dk_carry[...].astype(BF16)
                dv_ref[...] = dv_carry[...].astype(BF16)

            @pl.when(b < n_blocks)
            def _():
                step(sl_ref, q_ref, kp_ref, kc_ref, vp_ref, vc_ref, do_ref, lse_ref, dl_ref,
                     dq_ref, dk_ref, dv_ref, dk_carry, dv_carry, b)
        else:
            step(sl_ref, q_ref, kp_ref, kc_ref, vp_ref, vc_ref, do_ref, lse_ref, dl_ref,
                 dq_ref, dk_ref, dv_ref, None, None, b)

    def step(sl_ref, q_ref, kp_ref, kc_ref, vp_ref, vc_ref, do_ref, lse_ref, dl_ref,
             dq_ref, dk_ref, dv_ref, dk_carry, dv_carry, b):
        if carry:
            @pl.when(b == 0)
            def _():
                dk_carry[...] = jnp.zeros_like(dk_carry)
                dv_carry[...] = jnp.zeros_like(dv_carry)

        q3 = _head_stack(lambda cols: q_ref[:, cols], heads)
        k3 = _head_stack(lambda cols: _key_tile(kp_ref, kc_ref, cols, bpc), heads)
        v3 = _head_stack(lambda cols: _key_tile(vp_ref, vc_ref, cols, bpc), heads)
        do3 = _head_stack(lambda cols: do_ref[:, cols], heads)
        lse_t = lse_ref[...].T
        dl_t = dl_ref[...].T
        lse3 = jnp.stack([lse_t[h:h + 1, :] for h in range(heads)], axis=0)
        dl3 = jnp.stack([dl_t[h:h + 1, :] for h in range(heads)], axis=0)
        s = lax.dot_general(k3, q3, BATCH_NT_DIMS, preferred_element_type=F32)
        dist, valid = _attn_masks(b, bpc, dilation, transposed=True)
        p = jnp.exp(jnp.where(valid[None], s - dist[None] * sl_ref[...], NEG_INF) - lse3)
        dp = lax.dot_general(v3, do3, BATCH_NT_DIMS, preferred_element_type=F32)
        ds = (p * (dp - dl3)).astype(BF16)
        dq3 = lax.dot_general(ds, k3, BATCH_TN_DIMS, preferred_element_type=F32) * scale
        dk3 = lax.dot_general(ds, q3, BATCH_NN_DIMS, preferred_element_type=F32)
        dv3 = lax.dot_general(p.astype(BF16), do3, BATCH_NN_DIMS, preferred_element_type=F32)
        for h in range(heads):
            cols = slice(h * HEAD_DIM, (h + 1) * HEAD_DIM)
            dq_ref[:, cols] = dq3[h].astype(BF16)
            if carry:
                dk_ref[:, cols] = (dk_carry[:, cols] + dk3[h, :ATTN_BLOCK]).astype(BF16)
                dv_ref[:, cols] = (dv_carry[:, cols] + dv3[h, :ATTN_BLOCK]).astype(BF16)
                dk_carry[:, cols] = dk3[h, ATTN_BLOCK:]
                dv_carry[:, cols] = dv3[h, ATTN_BLOCK:]
            else:
                dk_ref[:, cols] = dk3[h].astype(BF16)
                dv_ref[:, cols] = dv3[h].astype(BF16)

    kv_out = prev if carry else cur
    return pl.pallas_call(
        body, name=name, grid=(N_HEADS // heads, n_blocks + (1 if carry else 0)),
        in_specs=[pl.BlockSpec((heads, 1, 1), lambda hg, b: (hg, 0, 0)), cur, prev, cur, v_prev, v_cur,
                  cur, per_head, per_head],
        out_specs=[cur, kv_out, kv_out],
        out_shape=[jax.ShapeDtypeStruct((SEQ, D_MODEL), BF16)] * 3,
        scratch_shapes=[pltpu.VMEM((ATTN_BLOCK, width), F32)] * 2 if carry else [],
        compiler_params=_params("parallel", "arbitrary"),
    )(slopes.reshape(N_HEADS, 1, 1), q, k, k, proj, proj, do, lse, delta)


def _qknorm_bwd(proj, group, qw, kw, seg, dq, dk, dv, name):
    tm = ROW_TILE

    def body(q_in, k_in, qw_ref, kw_ref, seg_ref, dq_ref, dk_ref, dv_ref, dproj_ref, sums_ref):
        segv = seg_ref[...]
        sums = []
        for part, (raw_ref, w_ref, dn_ref) in enumerate(((q_in, qw_ref, dq_ref), (k_in, kw_ref, dk_ref))):
            raw = raw_ref[...].astype(F32)
            dn = dn_ref[...].astype(F32)
            r = _qk_rstd(raw, segv)
            xhat = raw * r
            gq = dn * w_ref[...]
            draw = r * (gq - xhat * _segmean(xhat * gq, segv))
            dproj_ref[:, part * D_MODEL:(part + 1) * D_MODEL] = draw.astype(BF16)
            sums.append(jnp.sum(dn * xhat, axis=0, keepdims=True))
        dproj_ref[:, 2 * D_MODEL:] = dv_ref[...]

        @pl.when(pl.program_id(0) == 0)
        def _():
            sums_ref[...] = jnp.zeros_like(sums_ref)

        sums_ref[...] += jnp.concatenate(sums + [jnp.zeros((6, D_MODEL), F32)], axis=0)

    return pl.pallas_call(
        body, name=name, grid=(SEQ // tm,),
        in_specs=[_row_spec(tm, D_MODEL, 3 * group), _row_spec(tm, D_MODEL, 3 * group + 1),
                  _vec_spec(1, D_MODEL), _vec_spec(1, D_MODEL), _vec_spec(256, 256)] + [_row_spec(tm, D_MODEL)] * 3,
        out_specs=[_row_spec(tm, 3 * D_MODEL), _vec_spec(8, D_MODEL)],
        out_shape=[jax.ShapeDtypeStruct((SEQ, 3 * D_MODEL), BF16), jax.ShapeDtypeStruct((8, D_MODEL), F32)],
        compiler_params=_params("arbitrary"),
    )(proj, proj, qw, kw, seg, dq, dk, dv)


B_TN = 512
B_GROUP_TILES = 3 * D_MODEL // B_TN
B_Z_TILE0 = 3 * B_GROUP_TILES
B_Z_TILES = D_MODEL // B_TN
B_TILES = B_Z_TILE0 + B_Z_TILES
B_Z_SEGMENT = 3 * len(DILATIONS)


def _local_step(x, target, mods, norm_g, conv_w, conv_b, ln_g, ln_b, q_norm, k_norm, chip, own_wa_in, own_wb_in,
                weights_a, weights_b, forward_weights_b, send_grads_b, forward_grads_b, send_grads_a):
    row = lambda a, i: a[i:i + 1]
    shift0, scale0, gate0 = row(mods[0], 0), row(mods[0], 1), row(mods[0], 2)
    shift1, scale1, gate1 = row(mods[1], 0), row(mods[1], 1), row(mods[1], 2)
    g0, g1 = row(norm_g, 0), row(norm_g, 1)
    seg = _seg_matrix()
    slopes = jnp.exp2(-8.0 * jnp.arange(1, N_HEADS + 1, dtype=F32) / N_HEADS)
    qw = [jnp.tile(q_norm[g:g + 1], (1, N_HEADS)) for g in range(3)]
    kw = [jnp.tile(k_norm[g:g + 1], (1, N_HEADS)) for g in range(3)]

    h0, h0t = _normmod_fwd(x, g0, scale0, shift0, "prenorm0")
    nsa = own_wa_in.shape[2]
    tiles_a = dict(tn=nsa, total_tiles=N_CHIPS, part_of=lambda tile: 0)
    own_ids, rest_ids, own_tiles = _own_first(chip, N_CHIPS)
    proj_a = _in_tiles([h0], own_wa_in, own_ids, own_tiles, name="a_in_own", **tiles_a)
    wa_in, wa_out = weights_a(proj_a)
    ja = wa_in.shape[0]
    proj_a = _in_tiles([h0], wa_in, rest_ids, N_CHIPS - own_tiles, name="a_in_rest", prev=proj_a, **tiles_a)
    u5, u5t, u2 = _conv_fwd(proj_a, conv_w, conv_b, ln_g, ln_b, "a_conv")
    x1, y_a, h1t, h1c = _out_a(u5, wa_out, x, gate0, g1, scale1, shift1, "a_out")

    tiles_b = dict(tn=B_TN, total_tiles=B_TILES,
                   part_of=lambda tile: jnp.where(tile >= B_Z_TILE0, 0, tile // B_GROUP_TILES))
    own_ids, rest_ids, own_tiles = _own_first(chip, B_TILES)
    proj_b = _in_tiles(h1c, own_wb_in, own_ids, own_tiles, name="b_in_own", **tiles_b)
    forward_weights_b(proj_b)
    wb_in, wb_out = weights_b(proj_b)
    jb, _, nsb = wb_in.shape
    proj_b = _in_tiles(h1c, wb_in, rest_ids, B_TILES - own_tiles, name="b_in_rest", prev=proj_b, **tiles_b)
    h1 = h1c[0]
    qkv, o_parts, lse_parts = [], [], []
    for g, d in enumerate(DILATIONS):
        qn, kn = _qknorm_fwd(proj_b, g, qw[g], kw[g], seg, f"b_qknorm_g{g}")
        og, lg = _attn_fwd(qn, kn, proj_b, g, slopes, d, f"b_attn_g{g}")
        qkv.append((qn, kn))
        o_parts.append(og if d == 1 else og.reshape(d, SEQ // d, D_MODEL))
        lse_parts.append(lg if d == 1 else lg.reshape(d, SEQ // d, LANES))
    sel = _head_selector()
    u_b, u_bt, o_b, lse_b = _merge_fwd(o_parts, lse_parts, proj_b, sel, "b_merge")
    e, dy_b, sums_loss = _out_b_loss(u_b, wb_out, x1, gate1, target, "b_out_loss")

    dwb_out = _mm(u_bt, dy_b, tn=D_MODEL, tile0=0, n_tiles=1, out_dtype=BF16, name="b_dwout")
    dz_b, do_c, delta_c, lse_c = _merge_bwd(dy_b, wb_out, o_b, lse_b, proj_b, sel, "b_merge_bwd")
    dwb_in = _mm(h1t, dz_b, tn=B_TN, tile0=B_Z_TILE0, n_tiles=B_Z_TILES, out_dtype=BF16, name="b_dwin_z",
                 out3d=(jb, nsb))
    dh1_parts = [_mm_nt(dz_b, wb_in, tn=B_TN, tile0=B_Z_TILE0, n_tiles=B_Z_TILES, name="b_dh_z")]
    qk_sums = []
    for g, d in enumerate(DILATIONS):
        qn, kn = qkv[g]
        dq, dk, dv = _attn_bwd(qn, kn, proj_b, g, do_c[g], lse_c[g], delta_c[g], slopes, d, f"b_attn_bwd_g{g}")
        dproj, sums_qk = _qknorm_bwd(proj_b, g, qw[g], kw[g], seg, dq, dk, dv, f"b_qknorm_bwd_g{g}")
        qk_sums.append(sums_qk)
        dwb_in = _mm(h1t if d == 1 else h1c[g], dproj, tn=B_TN, tile0=g * B_GROUP_TILES, n_tiles=B_GROUP_TILES,
                     out_dtype=BF16, name=f"b_dwin_g{g}", out3d=(jb, nsb), prev=dwb_in, transpose_lhs=d != 1)
        dh = _mm_nt(dproj, wb_in, tn=B_TN, tile0=g * B_GROUP_TILES, n_tiles=B_GROUP_TILES, name=f"b_dh_g{g}")
        dh1_parts.append(dh)
    token = send_grads_b(dwb_in, dwb_out)
    dx1, sums_n1, dy_a = _normmod_bwd(x1, g1, scale1 + token[0:1, 0:1], dh1_parts, e, "prenorm1_bwd",
                                      part_dilations=(1,) + DILATIONS, gated=(gate0, y_a))
    token = forward_grads_b(dx1)

    dwa_out = _mm(u5t, dy_a, tn=D_MODEL, tile0=0, n_tiles=1, out_dtype=BF16, name="a_dwout")
    du2, dz_a, sums_ln = _conv_bwd_pointwise(dy_a, wa_out, proj_a, u2, ln_g + token[0:1, 0:1], ln_b,
                                             "a_conv_bwd_pw")
    dproj_a, dconv_w = _conv_bwd_taps(du2, dz_a, proj_a, conv_w, "a_conv_bwd_taps")
    dwa_in = _mm(h0t, dproj_a, tn=nsa, tile0=0, n_tiles=ja, out_dtype=BF16, name="a_dwin", out3d=(ja, nsa))
    token = send_grads_a(dwa_in, dwa_out)
    dh0 = _mm_nt(dproj_a, wa_in, tn=nsa, tile0=0, n_tiles=ja, name="a_dh", after=token)
    grad_x, sums_n0 = _normmod_bwd(x, g0, scale0, [dh0], dx1, "prenorm0_bwd")

    small = dict(
        dnorm_g=jnp.concatenate([sums_n0[0:1], sums_n1[0:1]], axis=0),
        dmod0=jnp.concatenate([sums_n0[2:3], sums_n0[1:2], sums_n1[3:4]], axis=0),
        dmod1=jnp.concatenate([sums_n1[2:3], sums_n1[1:2], sums_loss[0:1]], axis=0),
        dln_g=sums_ln[0:1], dln_b=sums_ln[1:2], dconv_b=sums_ln[2:3],
        dconv_w=dconv_w[:CONV_WIDTH],
        dq_norm=jnp.concatenate([s[0:1] for s in qk_sums], axis=0),
        dk_norm=jnp.concatenate([s[1:2] for s in qk_sums], axis=0),
        loss_cols=sums_loss[1:2],
    )
    return grad_x, small


def _adamw(w, g, m, v, name, after=None, copy_grad=False):
    rows, cols = w.shape
    tr = rows if rows <= 128 else (256 if cols <= D_MODEL else 128)
    c1 = 1.0 / (1.0 - ADAM_B1 ** ADAM_STEP)
    c2 = 1.0 / (1.0 - ADAM_B2 ** ADAM_STEP)
    extra = [] if after is None else [after]
    n_out = 4 if copy_grad else 3

    def body(w_ref, g_ref, m_ref, v_ref, *rest):
        d_ref, mo_ref, vo_ref = rest[len(extra):len(extra) + 3]
        gv = g_ref[...]
        if copy_grad:
            rest[-1][...] = gv
        mn = ADAM_B1 * m_ref[...] + (1.0 - ADAM_B1) * gv
        vn = ADAM_B2 * v_ref[...] + (1.0 - ADAM_B2) * (gv * gv)
        mo_ref[...] = mn
        vo_ref[...] = vn
        d_ref[...] = -ADAM_LR * ((mn * c1) / (jnp.sqrt(vn * c2) + ADAM_EPS) + ADAM_WD * w_ref[...])

    spec = pl.BlockSpec((tr, cols), lambda i: (i, 0))
    return pl.pallas_call(
        body, name=name, grid=(rows // tr,),
        in_specs=[spec] * 4 + [pl.BlockSpec(memory_space=pl.ANY)] * len(extra), out_specs=[spec] * n_out,
        out_shape=[jax.ShapeDtypeStruct((rows, cols), F32)] * n_out,
        compiler_params=_params("parallel"),
    )(w, g, m, v, *extra)


def _cast_into_slot(w, chip_idx, name, keep_own=False, after=None):
    rows, cols = w.shape
    tr = 256
    extra = [] if after is None else [after]

    def body(ch_ref, w_ref, *rest):
        wb = w_ref[...].astype(BF16)
        for o_ref in rest[len(extra):]:
            o_ref[...] = wb

    slot_spec = pl.BlockSpec((None, tr, cols), lambda i, ch: (ch[0], i, 0))
    own_spec = pl.BlockSpec((None, tr, cols), lambda i, ch: (0, i, 0))
    res = pl.pallas_call(
        body, name=name,
        grid_spec=pltpu.PrefetchScalarGridSpec(
            num_scalar_prefetch=1, grid=(rows // tr,),
            in_specs=[pl.BlockSpec((tr, cols), lambda i, ch: (i, 0))] + [pl.BlockSpec(memory_space=pl.ANY)] * len(extra),
            out_specs=[slot_spec, own_spec] if keep_own else [slot_spec]),
        out_shape=[jax.ShapeDtypeStruct((N_CHIPS, rows, cols), BF16)]
        + ([jax.ShapeDtypeStruct((1, rows, cols), BF16)] if keep_own else []),
        compiler_params=_params("parallel"),
    )(chip_idx, w, *extra)
    return tuple(res) if keep_own else res[0]


def _position():
    x, y, c = lax.axis_index("x"), lax.axis_index("y"), lax.axis_index("c")
    return x, y, c


def _xor_peer(x, y, c, k):
    return (x ^ ((k >> 2) & 1), y ^ ((k >> 1) & 1), c ^ (k & 1))


def _chip_peer(x, y, k):
    return (x ^ ((k >> 1) & 1), y ^ (k & 1))


def _ada_forward(c_row, ada_w, ada_b, conv_w, after=()):
    ns = ada_w.shape[2]
    cw = conv_w.shape[1]

    def body(c_ref, w_ref, b_ref, cv_ref, *rest):
        (mod_ref, sc_ref, cvo_ref, c_all, mp, parts, cv_parts,
         send1, recv1, send2, recv2, send3, recv3) = rest[len(after):]
        x, y, c = _position()
        me = 4 * x + 2 * y + c
        chip = 2 * x + y

        def c_copy(k):
            return pltpu.make_async_remote_copy(
                src_ref=c_all.at[me], dst_ref=c_all.at[me], send_sem=send1.at[k - 1], recv_sem=recv1.at[k - 1],
                device_id=_xor_peer(x, y, c, k), device_id_type=MESH)

        def cv_copy(k):
            px, py = _chip_peer(x, y, k)
            return pltpu.make_async_remote_copy(
                src_ref=cv_parts.at[chip], dst_ref=cv_parts.at[chip], send_sem=send3.at[k - 1],
                recv_sem=recv3.at[k - 1], device_id=(px, py, c), device_id_type=MESH)

        c_all[me] = c_ref[...]
        cv_parts[chip] = cv_ref[...]
        for k in range(1, N_DEV):
            c_copy(k).start()
        for k in range(1, N_CHIPS):
            cv_copy(k).start()
        for k in range(1, N_DEV):
            c_copy(k).wait_recv()
        cv = jnp.concatenate([c_all[i] for i in range(N_DEV)], axis=0)
        sc = cv * _sigmoid(cv)
        sc_ref[...] = sc
        for l in range(2):
            res = jnp.dot(sc, w_ref[l], preferred_element_type=F32, precision=lax.Precision.HIGHEST)
            for i in range(N_DEV):
                mp[i, l:l + 1, :] = res[i:i + 1, :]

        def mod_copy(k):
            px, py = _chip_peer(x, y, k)
            return pltpu.make_async_remote_copy(
                src_ref=mp.at[4 * px + 2 * py + c], dst_ref=parts.at[chip], send_sem=send2.at[k - 1],
                recv_sem=recv2.at[k - 1], device_id=(px, py, c), device_id_type=MESH)

        for k in range(1, N_CHIPS):
            mod_copy(k).start()
        parts[chip] = mp[me]
        for k in range(1, N_CHIPS):
            mod_copy(k).wait_recv()
            cv_copy(k).wait_recv()
        mod_ref[...] = jnp.concatenate([parts[j] for j in range(N_CHIPS)], axis=1) + b_ref[...]
        cvo_ref[...] = jnp.concatenate([cv_parts[j] for j in range(N_CHIPS)], axis=1)
        for k in range(1, N_DEV):
            c_copy(k).wait_send()
        for k in range(1, N_CHIPS):
            mod_copy(k).wait_send()
            cv_copy(k).wait_send()

    vm = pl.BlockSpec(memory_space=pltpu.VMEM)
    return pl.pallas_call(
        body, name="ada_forward",
        in_specs=[vm] * 4 + [pl.BlockSpec(memory_space=pl.ANY)] * len(after), out_specs=[vm] * 3,
        out_shape=[jax.ShapeDtypeStruct((2, 3 * D_MODEL), F32), jax.ShapeDtypeStruct((N_DEV, D_MODEL), F32),
                   jax.ShapeDtypeStruct((CONV_WIDTH, N_CHIPS * cw), F32)],
        scratch_shapes=[pltpu.VMEM((N_DEV, 1, D_MODEL), F32), pltpu.VMEM((N_DEV, 2, ns), F32),
                        pltpu.VMEM((N_CHIPS, 2, ns), F32), pltpu.VMEM((N_CHIPS, CONV_WIDTH, cw), F32),
                        pltpu.SemaphoreType.DMA((N_DEV - 1,)), pltpu.SemaphoreType.DMA((N_DEV - 1,)),
                        pltpu.SemaphoreType.DMA((N_CHIPS - 1,)), pltpu.SemaphoreType.DMA((N_CHIPS - 1,)),
                        pltpu.SemaphoreType.DMA((N_CHIPS - 1,)), pltpu.SemaphoreType.DMA((N_CHIPS - 1,))],
        compiler_params=pltpu.CompilerParams(vmem_limit_bytes=VMEM_LIMIT_BYTES),
    )(c_row, ada_w, ada_b, conv_w, *after)


HBM_SPEC = pl.BlockSpec(memory_space=pltpu.HBM)
ANY_SPEC = pl.BlockSpec(memory_space=pl.ANY)
SEM_SPEC = pl.BlockSpec(memory_space=pltpu.SEMAPHORE)
SPLIT_PARAMS = dict(compiler_params=pltpu.CompilerParams(has_side_effects=pltpu.SideEffectType.DATAFLOW_SIDE_EFFECTING))
TOKEN = jax.ShapeDtypeStruct((8, 128), F32)
ENTRY_HANDSHAKES = {name: (i, peers) for i, (name, peers) in enumerate((
    ("gather_start_a", "chips"), ("gather_start_b", "chips"),
    ("gather_forward_a", "sibling"), ("gather_forward_b", "sibling"),
    ("reduce_d2d_start_b", "sibling"), ("reduce_d2d_start_a", "sibling"),
    ("reduce_ici_start_b", "chips"), ("reduce_ici_start_a", "chips"),
    ("reduce_share_start_b", "sibling"), ("reduce_share_start_a", "sibling"),
    ("small_gather_start", "devices")))}


def _hbm(arrays):
    return [pltpu.with_memory_space_constraint(a, pltpu.HBM) for a in arrays]


def _hbm_like(arrays):
    return [pltpu.HBM(a.shape, a.dtype) for a in arrays]


def _gather_start(lands, after, name):
    n = len(lands)

    def body(*refs):
        _handshake(ENTRY_HANDSHAKES[name][1])
        ins = refs[:n]
        send, recv = refs[n + 1], refs[n + 2]
        x, y, c = _position()
        chip = 2 * x + y
        for t in range(n):
            rh = ins[t].shape[1] // 2
            for k in range(1, N_CHIPS):
                px, py = _chip_peer(x, y, k)
                block = ins[t].at[chip, pl.ds(c * rh, rh)]
                pltpu.make_async_remote_copy(
                    src_ref=block, dst_ref=block, send_sem=send.at[3 * t + k - 1], recv_sem=recv.at[3 * t + k - 1],
                    device_id=(px, py, c), device_id_type=MESH).start()
        refs[-1][...] = jnp.zeros(TOKEN.shape, F32)

    res = pl.pallas_call(
        body, name=name, in_specs=[HBM_SPEC] * n + [ANY_SPEC],
        out_specs=(SEM_SPEC, SEM_SPEC, *[HBM_SPEC] * n, pl.BlockSpec(memory_space=pltpu.VMEM)),
        out_shape=(pltpu.SemaphoreType.DMA((3 * n,)), pltpu.SemaphoreType.DMA((3 * n,)), *_hbm_like(lands), TOKEN),
        input_output_aliases={t: 2 + t for t in range(n)}, **_split_params(name),
    )(*_hbm(lands), after)
    return res[0], res[1], list(res[2:2 + n]), res[-1]


def _gather_forward(send, recv, lands, after, name):
    n = len(lands)

    def body(*refs):
        _handshake(ENTRY_HANDSHAKES[name][1])
        ins = refs[:n]
        send1, recv1 = refs[n], refs[n + 1]
        send2, recv2 = refs[n + 3], refs[n + 4]
        x, y, c = _position()
        chip = 2 * x + y
        for t in range(n):
            rh = ins[t].shape[1] // 2
            half = pl.ds(c * rh, rh)
            for k in range(1, N_CHIPS):
                px, py = _chip_peer(x, y, k)
                s = 3 * t + k - 1
                got = ins[t].at[2 * px + py, half]
                cp = pltpu.make_async_remote_copy(
                    src_ref=ins[t].at[chip, half], dst_ref=got, send_sem=send1.at[s], recv_sem=recv1.at[s],
                    device_id=(px, py, c), device_id_type=MESH)
                cp.wait_send()
                cp.wait_recv()
                pltpu.make_async_remote_copy(
                    src_ref=got, dst_ref=got, send_sem=send2.at[s], recv_sem=recv2.at[s],
                    device_id=(x, y, 1 - c), device_id_type=MESH).start()
        refs[-1][...] = jnp.zeros(TOKEN.shape, F32)

    res = pl.pallas_call(
        body, name=name, in_specs=[HBM_SPEC] * n + [SEM_SPEC, SEM_SPEC, ANY_SPEC],
        out_specs=(SEM_SPEC, SEM_SPEC, *[HBM_SPEC] * n, pl.BlockSpec(memory_space=pltpu.VMEM)),
        out_shape=(pltpu.SemaphoreType.DMA((3 * n,)), pltpu.SemaphoreType.DMA((3 * n,)), *_hbm_like(lands), TOKEN),
        input_output_aliases={t: 2 + t for t in range(n)}, **_split_params(name),
    )(*lands, send, recv, after)
    return res[0], res[1], list(res[2:2 + n]), res[-1]


def _gather_wait(send, recv, lands, after, name):
    n = len(lands)

    def body(*refs):
        ins = refs[:n]
        send_ref, recv_ref = refs[n], refs[n + 1]
        x, y, c = _position()
        for t in range(n):
            rh = ins[t].shape[1] // 2
            for k in range(1, N_CHIPS):
                px, py = _chip_peer(x, y, k)
                cp = pltpu.make_async_remote_copy(
                    src_ref=ins[t].at[2 * px + py, pl.ds(c * rh, rh)],
                    dst_ref=ins[t].at[2 * px + py, pl.ds((1 - c) * rh, rh)], send_sem=send_ref.at[3 * t + k - 1],
                    recv_sem=recv_ref.at[3 * t + k - 1], device_id=(x, y, 1 - c), device_id_type=MESH)
                cp.wait_send()
                cp.wait_recv()

    res = pl.pallas_call(
        body, name=name, in_specs=[HBM_SPEC] * n + [SEM_SPEC, SEM_SPEC, ANY_SPEC], out_specs=[HBM_SPEC] * n,
        out_shape=_hbm_like(lands), input_output_aliases={t: t for t in range(n)}, **SPLIT_PARAMS,
    )(*lands, send, recv, after)
    return list(res)


def _handshake(peers):
    x, y, c = _position()
    if peers == "sibling":
        ids = [(x, y, 1 - c)]
    elif peers == "chips":
        ids = [(*_chip_peer(x, y, k), c) for k in range(1, N_CHIPS)]
    else:
        ids = [_xor_peer(x, y, c, k) for k in range(1, N_DEV)]
    barrier = pltpu.get_barrier_semaphore()
    for peer in ids:
        pl.semaphore_signal(barrier, inc=1, device_id=peer, device_id_type=MESH)
    pl.semaphore_wait(barrier, len(ids))


def _split_params(name):
    return dict(compiler_params=pltpu.CompilerParams(
        has_side_effects=pltpu.SideEffectType.DATAFLOW_SIDE_EFFECTING, collective_id=ENTRY_HANDSHAKES[name][0]))


def _split_start(name, arrays, n_sems, after, issue):
    m = len(arrays)

    def body(*refs):
        _handshake(ENTRY_HANDSHAKES[name][1])
        issue(refs[:m], refs[m + 1], refs[m + 2])
        refs[-1][...] = jnp.zeros(TOKEN.shape, F32)

    res = pl.pallas_call(
        body, name=name, in_specs=[HBM_SPEC] * m + [ANY_SPEC],
        out_specs=(SEM_SPEC, SEM_SPEC, *[HBM_SPEC] * m, pl.BlockSpec(memory_space=pltpu.VMEM)),
        out_shape=(pltpu.SemaphoreType.DMA((n_sems,)), pltpu.SemaphoreType.DMA((n_sems,)), *_hbm_like(arrays), TOKEN),
        input_output_aliases={t: 2 + t for t in range(m)}, **_split_params(name),
    )(*_hbm(arrays), after)
    return res[0], res[1], list(res[2:2 + m]), res[-1]


def _split_wait(name, arrays, send, recv, after, await_all):
    m = len(arrays)

    def body(*refs):
        await_all(refs[:m], refs[m], refs[m + 1])

    res = pl.pallas_call(
        body, name=name, in_specs=[HBM_SPEC] * m + [SEM_SPEC, SEM_SPEC, ANY_SPEC], out_specs=[HBM_SPEC] * m,
        out_shape=_hbm_like(arrays), input_output_aliases={t: t for t in range(m)}, **SPLIT_PARAMS,
    )(*arrays, send, recv, after)
    return list(res)


def _sibling_copies(refs, send, recv, n):
    x, y, c = _position()
    cps = []
    for t in range(n):
        rh = refs[t].shape[1] // 2
        cps.append(pltpu.make_async_remote_copy(
            src_ref=refs[t].at[pl.ds(0, N_CHIPS), pl.ds((1 - c) * rh, rh)], dst_ref=refs[n + t],
            send_sem=send.at[t], recv_sem=recv.at[t], device_id=(x, y, 1 - c), device_id_type=MESH))
    return cps


def _reduce_sibling_start(grads, after, name):
    n = len(grads)
    lands = [lax.empty((N_CHIPS, g.shape[1] // 2, g.shape[2]), BF16) for g in grads]

    def issue(refs, send, recv):
        for cp in _sibling_copies(refs, send, recv, n):
            cp.start()

    return _split_start(name, list(grads) + lands, n, after, issue)


def _reduce_sibling_wait(send, recv, arrays, after, name):
    n = len(arrays) // 2

    def await_all(refs, send_ref, recv_ref):
        for cp in _sibling_copies(refs, send_ref, recv_ref, n):
            cp.wait_send()
            cp.wait_recv()

    res = _split_wait(name, arrays, send, recv, after, await_all)
    return res[:n], res[n:]


def _add_sibling_half(grad, got, dev_idx, name):
    j, r, cols = grad.shape
    rh = r // 2
    tr = rh
    nb = rh // tr

    def body(idx_ref, g_ref, got_ref, out_ref):
        out_ref[...] = (g_ref[...].astype(F32) + got_ref[...].astype(F32)).astype(BF16)

    return pl.pallas_call(
        body, name=name,
        grid_spec=pltpu.PrefetchScalarGridSpec(
            num_scalar_prefetch=1, grid=(j, nb),
            in_specs=[pl.BlockSpec((None, tr, cols), lambda jj, i, idx: (jj, idx[2] * nb + i, 0)),
                      pl.BlockSpec((None, tr, cols), lambda jj, i, idx: (jj, i, 0))],
            out_specs=pl.BlockSpec((None, tr, cols), lambda jj, i, idx: (jj, i, 0))),
        out_shape=jax.ShapeDtypeStruct((j, rh, cols), BF16),
        compiler_params=_params("parallel", "parallel"),
    )(dev_idx, grad, got)


def _chip_copies(refs, send, recv, n, receiving):
    x, y, c = _position()
    chip = 2 * x + y
    cps = []
    for t in range(n):
        for k in range(1, N_CHIPS):
            px, py = _chip_peer(x, y, k)
            cps.append(pltpu.make_async_remote_copy(
                src_ref=refs[t].at[2 * px + py], dst_ref=refs[n + t].at[2 * px + py if receiving else chip],
                send_sem=send.at[3 * t + k - 1], recv_sem=recv.at[3 * t + k - 1],
                device_id=(px, py, c), device_id_type=MESH))
    return cps


def _reduce_chips_start(partials, after, name):
    n = len(partials)
    lands = [lax.empty(p.shape, BF16) for p in partials]

    def issue(refs, send, recv):
        for cp in _chip_copies(refs, send, recv, n, False):
            cp.start()

    return _split_start(name, list(partials) + lands, 3 * n, after, issue)


def _reduce_chips_wait(send, recv, arrays, after, name):
    n = len(arrays) // 2

    def await_all(refs, send_ref, recv_ref):
        for cp in _chip_copies(refs, send_ref, recv_ref, n, True):
            cp.wait_send()
            cp.wait_recv()

    res = _split_wait(name, arrays, send, recv, after, await_all)
    return res[:n], res[n:]


def _sum_partials(land, partial, dev_idx, name):
    _, rh, cols = land.shape
    tr = min(rh, 256)
    nb = rh // tr

    def body(idx_ref, l_ref, p_ref, o_ref):
        chip = idx_ref[1]
        acc = jnp.where(chip == 0, p_ref[...], l_ref[0]).astype(F32)
        for s in range(1, N_CHIPS):
            acc = acc + jnp.where(chip == s, p_ref[...], l_ref[s]).astype(F32)
        o_ref[...] = acc

    return pl.pallas_call(
        body, name=name,
        grid_spec=pltpu.PrefetchScalarGridSpec(
            num_scalar_prefetch=1, grid=(nb,),
            in_specs=[pl.BlockSpec((N_CHIPS, tr, cols), lambda i, idx: (0, i, 0)),
                      pl.BlockSpec((None, tr, cols), lambda i, idx: (idx[1], i, 0))],
            out_specs=pl.BlockSpec((tr, cols), lambda i, idx: (idx[2] * nb + i, 0))),
        out_shape=jax.ShapeDtypeStruct((2 * rh, cols), F32), compiler_params=_params("parallel"),
    )(dev_idx, land, partial)


def _half_copies(refs, send, recv, receiving):
    x, y, c = _position()
    cps = []
    for t, ref in enumerate(refs):
        rh = ref.shape[0] // 2
        cps.append(pltpu.make_async_remote_copy(
            src_ref=ref.at[pl.ds(c * rh, rh)], dst_ref=ref.at[pl.ds(((1 - c) if receiving else c) * rh, rh)],
            send_sem=send.at[t], recv_sem=recv.at[t], device_id=(x, y, 1 - c), device_id_type=MESH))
    return cps


def _share_halves_start(totals, after, name):
    def issue(refs, send, recv):
        for cp in _half_copies(refs, send, recv, False):
            cp.start()

    return _split_start(name, list(totals), len(totals), after, issue)


def _share_halves_wait(send, recv, totals, after, name):
    def await_all(refs, send_ref, recv_ref):
        for cp in _half_copies(refs, send_ref, recv_ref, True):
            cp.wait_send()
            cp.wait_recv()

    return _split_wait(name, totals, send, recv, after, await_all)


SMALL_ROWS = 56


def _small_copies(refs, send, recv, receiving):
    x, y, c = _position()
    me = 4 * x + 2 * y + c
    cps = []
    for k in range(1, N_DEV):
        px, py, pc = _xor_peer(x, y, c, k)
        cps.append(pltpu.make_async_remote_copy(
            src_ref=refs[0], dst_ref=refs[1].at[4 * px + 2 * py + pc if receiving else me],
            send_sem=send.at[k - 1], recv_sem=recv.at[k - 1], device_id=(px, py, pc), device_id_type=MESH))
    return cps


def _small_gather_start(packed, after):
    land = lax.empty((N_DEV,) + packed.shape, F32)

    def issue(refs, send, recv):
        for cp in _small_copies(refs, send, recv, False):
            cp.start()

    return _split_start("small_gather_start", [packed, land], N_DEV - 1, after, issue)


def _small_gather_wait(send, recv, arrays, after):
    def await_all(refs, send_ref, recv_ref):
        for cp in _small_copies(refs, send_ref, recv_ref, True):
            cp.wait_send()
            cp.wait_recv()

    return _split_wait("small_gather_wait", arrays, send, recv, after, await_all)


def _reduce_small(packed, land, silu_c):
    ns = 3 * D_MODEL // N_CHIPS

    def body(p_ref, land_ref, sc_ref, tot_ref, gw_ref, loss_ref, qk_ref, allp):
        x, y, c = _position()
        me = 4 * x + 2 * y + c
        chip = 2 * x + y
        for i in range(N_DEV):
            allp[i] = jnp.where(me == i, p_ref[...], land_ref[i])
        tot = allp[0]
        for i in range(1, N_DEV):
            tot = tot + allp[i]
        tot_ref[...] = tot
        loss_ref[...] = jnp.sum(tot[11:12, :], axis=1, keepdims=True) * (0.5 / D_MODEL)
        fold = tot[5:11, 0:HEAD_DIM]
        for h in range(1, N_HEADS):
            fold = fold + tot[5:11, h * HEAD_DIM:(h + 1) * HEAD_DIM]
        qk_ref[...] = jnp.concatenate([fold, jnp.zeros((2, HEAD_DIM), F32)], axis=0)
        sct = sc_ref[...].T
        rc = 64
        for l in range(2):
            dms = [allp[i, pl.ds(12 + 4 * l + chip, 1), :][:, :ns] for i in range(N_DEV)]
            for r0 in range(0, D_MODEL, rc):
                acc = sct[r0:r0 + rc, 0:1] * dms[0]
                for i in range(1, N_DEV):
                    acc = acc + sct[r0:r0 + rc, i:i + 1] * dms[i]
                gw_ref[l, r0:r0 + rc, :] = acc

    vm = pl.BlockSpec(memory_space=pltpu.VMEM)
    return pl.pallas_call(
        body, name="reduce_small", in_specs=[vm, vm, vm], out_specs=[vm] * 4,
        out_shape=[jax.ShapeDtypeStruct((SMALL_ROWS, D_MODEL), F32), jax.ShapeDtypeStruct((2, D_MODEL, ns), F32),
                   jax.ShapeDtypeStruct((1, 1), F32), jax.ShapeDtypeStruct((8, HEAD_DIM), F32)],
        scratch_shapes=[pltpu.VMEM((N_DEV, SMALL_ROWS, D_MODEL), F32)],
        compiler_params=pltpu.CompilerParams(vmem_limit_bytes=VMEM_LIMIT_BYTES),
    )(packed, land, silu_c)


def kernel(x, c, norm_g, ada_w, ada_b, a_w_in, a_conv_w, a_conv_b, a_ln_g, a_ln_b, a_w_out, b_w_in, b_q_norm, b_k_norm, b_w_out, loss_target, m_norm_g, m_ada_w, m_ada_b, m_a_w_in, m_a_conv_w, m_a_conv_b, m_a_ln_g, m_a_ln_b, m_a_w_out, m_b_w_in, m_b_q_norm, m_b_k_norm, m_b_w_out, v_norm_g, v_ada_w, v_ada_b, v_a_w_in, v_a_conv_w, v_a_conv_b, v_a_ln_g, v_a_ln_b, v_a_w_out, v_b_w_in, v_b_q_norm, v_b_k_norm, v_b_w_out):
    chip = 2 * lax.axis_index("x") + lax.axis_index("y")
    core = lax.axis_index("c")
    chip_idx = chip.astype(jnp.int32).reshape(1)
    dev_idx = jnp.stack([2 * chip + core, chip, core]).astype(jnp.int32)

    land_a_in, own_wa_in = _cast_into_slot(a_w_in[0], chip_idx, "cast_a_w_in", keep_own=True)
    lands_a = [land_a_in, _cast_into_slot(a_w_out[0], chip_idx, "cast_a_w_out")]
    mods, silu_c, conv_w_full = _ada_forward(c, ada_w, ada_b, a_conv_w[0], after=tuple(lands_a))
    send_a, recv_a, lands_a, token_a = _gather_start(lands_a, mods, "gather_start_a")
    land_b_in, own_wb_in = _cast_into_slot(b_w_in[0], chip_idx, "cast_b_w_in", keep_own=True, after=token_a)
    lands_b = [land_b_in, _cast_into_slot(b_w_out[0], chip_idx, "cast_b_w_out", after=token_a)]
    send_b, recv_b, lands_b, token_b = _gather_start(lands_b, token_a, "gather_start_b")
    mods = mods + token_b[0:2, 0:1]

    def weights_a(after):
        send, recv, lands, _ = _gather_forward(send_a, recv_a, lands_a, after, "gather_forward_a")
        w_in, w_out = _gather_wait(send, recv, lands, after, "gather_wait_a")
        return w_in, w_out.reshape(D_MODEL, D_MODEL)

    forwarded_b = []

    def weights_b(after):
        send, recv, lands, _ = forwarded_b
        w_in, w_out = _gather_wait(send, recv, lands, after, "gather_wait_b")
        return w_in, w_out.reshape(D_MODEL, D_MODEL)

    def forward_weights_b(after):
        forwarded_b.extend(_gather_forward(send_b, recv_b, lands_b, after, "gather_forward_b"))

    stage1, stage2 = {}, {}

    def send_grads(tag, dw_in, dw_out):
        grads = [dw_in, dw_out.reshape(N_CHIPS, D_MODEL // N_CHIPS, D_MODEL)]
        send, recv, arrays, token = _reduce_sibling_start(grads, dw_out, f"reduce_d2d_start_{tag}")
        stage1[tag] = (send, recv, arrays)
        return token

    def forward_grads(tag, after):
        send, recv, arrays = stage1[tag]
        grads, got = _reduce_sibling_wait(send, recv, arrays, after, f"reduce_d2d_wait_{tag}")
        partials = [_add_sibling_half(grads[i], got[i], dev_idx, f"reduce_add_{tag}_{i}") for i in range(2)]
        send, recv, arrays, token = _reduce_chips_start(partials, partials[1], f"reduce_ici_start_{tag}")
        stage2[tag] = (send, recv, arrays)
        return token

    stage3 = {}

    def sum_grads(tag, after):
        send, recv, arrays = stage2[tag]
        partials, lands = _reduce_chips_wait(send, recv, arrays, after, f"reduce_ici_wait_{tag}")
        totals = [_sum_partials(lands[i], partials[i], dev_idx, f"reduce_sum_{tag}_{i}") for i in range(2)]
        send, recv, totals, token = _share_halves_start(totals, totals[1], f"reduce_share_start_{tag}")
        stage3[tag] = (send, recv, totals)
        return token

    def finish_grads(tag, after):
        send, recv, totals = stage3[tag]
        return _share_halves_wait(send, recv, totals, after, f"reduce_share_wait_{tag}")

    grad_x, small = _local_step(
        x[0], loss_target[0], mods.reshape(2, 3, D_MODEL), norm_g, conv_w_full, a_conv_b, a_ln_g[0:1],
        a_ln_b[0:1], b_q_norm[0], b_k_norm[0], chip.astype(jnp.int32), own_wa_in, own_wb_in,
        weights_a, weights_b, forward_weights_b,
        functools.partial(send_grads, "b"), functools.partial(forward_grads, "b"), functools.partial(send_grads, "a"))

    ns = 3 * D_MODEL // N_CHIPS
    pad_mod = lambda dm: jnp.pad(dm.reshape(N_CHIPS, ns), ((0, 0), (0, D_MODEL - ns)))
    packed = jnp.concatenate([
        small["dnorm_g"], small["dconv_b"], small["dln_g"], small["dln_b"], small["dq_norm"], small["dk_norm"],
        small["loss_cols"], pad_mod(small["dmod0"]), pad_mod(small["dmod1"]), small["dconv_w"],
        jnp.zeros((SMALL_ROWS - 20 - CONV_WIDTH, D_MODEL), F32)], axis=0)
    send_s, recv_s, small_arrays, token_s = _small_gather_start(packed, packed)

    given = dict(norm_g=(norm_g, m_norm_g, v_norm_g), ada_w=(ada_w, m_ada_w, v_ada_w), ada_b=(ada_b, m_ada_b, v_ada_b),
                 a_w_in=(a_w_in, m_a_w_in, v_a_w_in), a_conv_w=(a_conv_w, m_a_conv_w, v_a_conv_w),
                 a_conv_b=(a_conv_b, m_a_conv_b, v_a_conv_b), a_ln_g=(a_ln_g, m_a_ln_g, v_a_ln_g),
                 a_ln_b=(a_ln_b, m_a_ln_b, v_a_ln_b), a_w_out=(a_w_out, m_a_w_out, v_a_w_out),
                 b_w_in=(b_w_in, m_b_w_in, v_b_w_in), b_q_norm=(b_q_norm, m_b_q_norm, v_b_q_norm),
                 b_k_norm=(b_k_norm, m_b_k_norm, v_b_k_norm), b_w_out=(b_w_out, m_b_w_out, v_b_w_out))
    order = ["norm_g", "ada_w", "ada_b", "a_w_in", "a_conv_w", "a_conv_b", "a_ln_g", "a_ln_b", "a_w_out", "b_w_in",
             "b_q_norm", "b_k_norm", "b_w_out"]
    outs = {}

    def update(k, g2, after=None, copy_grad=False):
        w, m, v = given[k]
        shape2 = g2.shape
        res = _adamw(w.reshape(shape2), g2, m.reshape(shape2), v.reshape(shape2), f"adamw_{k}", after, copy_grad)
        outs[k] = tuple(a.reshape(w.shape) for a in ((res[3] if copy_grad else g2), res[0], res[1], res[2]))

    token = forward_grads("a", token_s)
    token = sum_grads("b", token)
    packed, land = _small_gather_wait(send_s, recv_s, small_arrays, token)
    tot, g_ada_w, loss, qk = _reduce_small(packed, land, silu_c)
    g_b_in, g_b_out = finish_grads("b", tot)
    update("b_w_in", g_b_in, copy_grad=True)
    update("b_w_out", g_b_out, copy_grad=True)
    token = sum_grads("a", outs["b_w_in"][1])
    cw = D_MODEL // N_CHIPS
    g_small = dict(
        norm_g=tot[0:2], a_conv_b=tot[2:3], a_ln_g=tot[3:4], a_ln_b=tot[4:5],
        b_q_norm=qk[0:3], b_k_norm=qk[3:6],
        ada_b=jnp.stack([tot[12:16, :ns].reshape(3 * D_MODEL), tot[16:20, :ns].reshape(3 * D_MODEL)]),
        a_conv_w=lax.dynamic_slice(tot[20:20 + CONV_WIDTH], (0, chip * cw), (CONV_WIDTH, cw)),
    )
    update("ada_w", g_ada_w.reshape(2 * D_MODEL, ns), after=token)
    for k, g2 in g_small.items():
        update(k, g2, after=token)
    g_a_in, g_a_out = finish_grads("a", outs["ada_w"][1])
    update("a_w_in", g_a_in, copy_grad=True)
    update("a_w_out", g_a_out, copy_grad=True)
    return (loss.reshape(()), grad_x[None], *[outs[k][0] for k in order], *[outs[k][1] for k in order],
            *[outs[k][2] for k in order], *[outs[k][3] for k in order])
```

```python
import functools

import jax
import jax.numpy as jnp
from jax import lax
from jax.experimental import pallas as pl
from jax.experimental.pallas import tpu as pltpu

F32 = jnp.float32
BF16 = jnp.bfloat16

SEQ = 2048
D_MODEL = 1024
CONV_WIDTH = 31
HEAD_DIM = 64
N_HEADS = 16
DILATIONS = (1, 4, 16)
ATTN_BLOCK = 128
NORM_EPS = 1e-6
NEG_INF = -1e30
N_DEV = 8
N_CHIPS = 4

ADAM_LR = 0.001
ADAM_B1 = 0.9
ADAM_B2 = 0.999
ADAM_EPS = 1e-08
ADAM_WD = 0.01
ADAM_STEP = 10

VMEM_LIMIT_BYTES = 52 * 1024 * 1024
HALO = 32
LANES = 128
ROW_TILE = 512
MESH = pl.DeviceIdType.MESH


def _params(*sem):
    return pltpu.CompilerParams(dimension_semantics=sem or None, vmem_limit_bytes=VMEM_LIMIT_BYTES)


def _sigmoid(v):
    return 1.0 / (1.0 + jnp.exp(-v))


def _row_spec(tm, cols, col_block=0):
    return pl.BlockSpec((tm, cols), lambda i: (i, col_block))


def _vec_spec(rows, cols):
    return pl.BlockSpec((rows, cols), lambda i: (0, 0))


def _normmod(xv, g, scale, shift):
    r = lax.rsqrt(jnp.mean(xv * xv, axis=-1, keepdims=True) + NORM_EPS)
    return xv * r * g * (1.0 + scale) + shift


def _normmod_fwd(x, g, scale, shift, name):
    tm = ROW_TILE

    def body(x_ref, g_ref, sc_ref, sh_ref, h_ref, ht_ref):
        h = _normmod(x_ref[...], g_ref[...], sc_ref[...], sh_ref[...])
        h_ref[...] = h.astype(BF16)
        ht_ref[...] = h.T.astype(BF16)

    return pl.pallas_call(
        body, name=name, grid=(SEQ // tm,),
        in_specs=[_row_spec(tm, D_MODEL)] + [_vec_spec(1, D_MODEL)] * 3,
        out_specs=[_row_spec(tm, D_MODEL), pl.BlockSpec((D_MODEL, tm), lambda i: (0, i))],
        out_shape=[jax.ShapeDtypeStruct((SEQ, D_MODEL), BF16), jax.ShapeDtypeStruct((D_MODEL, SEQ), BF16)],
        compiler_params=_params("parallel"),
    )(x, g, scale, shift)


def _normmod_bwd(x, g, scale, dh_parts, dres, name, part_dilations=None, gated=None):
    tm = ROW_TILE
    n_parts = len(dh_parts)
    dils = part_dilations or (1,) * n_parts
    dh_parts = [p if d == 1 else p.reshape(d, SEQ // d, D_MODEL) for p, d in zip(dh_parts, dils)]
    n_gated = 0 if gated is None else 2

    def body(x_ref, g_ref, sc_ref, dres_ref, *rest):
        part_refs = rest[:n_parts]
        gated_refs = rest[n_parts:n_parts + n_gated]
        out_refs = rest[n_parts + n_gated:]
        dx_ref, sums_ref, nat = out_refs[0], out_refs[1], out_refs[-1]
        xv = x_ref[...]
        r = lax.rsqrt(jnp.mean(xv * xv, axis=-1, keepdims=True) + NORM_EPS)
        xn = xv * r
        dh = _load_natural(part_refs[0], nat, dils[0])
        for p, d in zip(part_refs[1:], dils[1:]):
            dh = dh + _load_natural(p, nat, d)
        gv = g_ref[...]
        one_sc = 1.0 + sc_ref[...]
        dxn = dh * (gv * one_sc)
        dx = dres_ref[...] + r * (dxn - xn * jnp.mean(dxn * xn, axis=-1, keepdims=True))
        dx_ref[...] = dx
        dhx = dh * xn
        rows = [jnp.sum(dhx, axis=0, keepdims=True) * one_sc,
                jnp.sum(dhx, axis=0, keepdims=True) * gv,
                jnp.sum(dh, axis=0, keepdims=True)]
        if gated is not None:
            gate_ref, y_ref = gated_refs
            out_refs[2][...] = (dx * gate_ref[...]).astype(BF16)
            rows.append(jnp.sum(dx * y_ref[...].astype(F32), axis=0, keepdims=True))
        sums = jnp.concatenate(rows + [jnp.zeros((8 - len(rows), D_MODEL), F32)], axis=0)

        @pl.when(pl.program_id(0) == 0)
        def _():
            sums_ref[...] = jnp.zeros_like(sums_ref)

        sums_ref[...] += sums

    gated_specs = [] if gated is None else [_vec_spec(1, D_MODEL), _row_spec(tm, D_MODEL)]
    dy_spec = [] if gated is None else [_row_spec(tm, D_MODEL)]
    dy_shape = [] if gated is None else [jax.ShapeDtypeStruct((SEQ, D_MODEL), BF16)]
    return pl.pallas_call(
        body, name=name, grid=(SEQ // tm,),
        in_specs=[_row_spec(tm, D_MODEL), _vec_spec(1, D_MODEL), _vec_spec(1, D_MODEL), _row_spec(tm, D_MODEL)]
        + [_class_spec(tm, d) for d in dils] + gated_specs,
        out_specs=[_row_spec(tm, D_MODEL), _vec_spec(8, D_MODEL)] + dy_spec,
        out_shape=[jax.ShapeDtypeStruct((SEQ, D_MODEL), F32), jax.ShapeDtypeStruct((8, D_MODEL), F32)] + dy_shape,
        scratch_shapes=[_natural_scratch(tm)],
        compiler_params=_params("arbitrary"),
    )(x, g, scale, dres, *dh_parts, *(gated or ()))


def _mm(lhs, rhs, *, tn, tile0, n_tiles, out_dtype, name, out3d=None, prev=None, transpose_lhs=False):
    mo, kc = lhs.shape[::-1] if transpose_lhs else lhs.shape
    cm = min(mo, 1024)
    tc = 256

    def body(l_ref, r_ref, *rest):
        if transpose_lhs:
            o_ref, lt_ref = rest[-2], rest[-1]

            @pl.when(pl.program_id(0) == 0)
            def _():
                for c in range(kc // tc):
                    lt_ref[:, c * tc:(c + 1) * tc] = l_ref[c * tc:(c + 1) * tc, :].astype(F32).T.astype(l_ref.dtype)
        else:
            o_ref, lt_ref = rest[-1], l_ref
        for m in range(mo // cm):
            rows = pl.ds(m * cm, cm)
            o_ref[rows, :] = jnp.dot(lt_ref[rows, :], r_ref[...], preferred_element_type=F32).astype(out_dtype)

    if rhs.ndim == 3:
        tps_r = rhs.shape[2] // tn
        r_spec = pl.BlockSpec((None, kc, tn), lambda t: ((tile0 + t) // tps_r, 0, (tile0 + t) % tps_r))
    else:
        r_spec = pl.BlockSpec((kc, tn), lambda t: (0, t))
    in_specs = [pl.BlockSpec(lhs.shape, lambda t: (0, 0)), r_spec]
    args = [lhs, rhs]
    aliases = {}
    if out3d is None:
        o_spec = pl.BlockSpec((mo, tn), lambda t: (0, t))
        o_shape = jax.ShapeDtypeStruct((mo, n_tiles * tn), out_dtype)
    else:
        j_out, ns_out = out3d
        tps_o = ns_out // tn
        o_spec = pl.BlockSpec((None, mo, tn), lambda t: ((tile0 + t) // tps_o, 0, (tile0 + t) % tps_o))
        o_shape = jax.ShapeDtypeStruct((j_out, mo, ns_out), out_dtype)
        if prev is not None:
            in_specs.append(pl.BlockSpec(memory_space=pl.ANY))
            args.append(prev)
            aliases = {2: 0}
    return pl.pallas_call(
        body, name=name, grid=(n_tiles,), in_specs=in_specs, out_specs=o_spec, out_shape=o_shape,
        input_output_aliases=aliases,
        scratch_shapes=[pltpu.VMEM((mo, kc), lhs.dtype)] if transpose_lhs else [],
        compiler_params=_params("arbitrary" if transpose_lhs else "parallel"),
    )(*args)


def _in_tiles(h_parts, w3, tile_ids, n_tiles, *, tn, total_tiles, part_of, name, prev=None):
    _, kc, ns = w3.shape
    tps = ns // tn
    cm = 1024
    n_parts = len(h_parts)

    def body(ids_ref, *rest):
        h_refs, w_ref, o_ref = rest[:n_parts], rest[n_parts], rest[-1]
        part = part_of(ids_ref[1, pl.program_id(0)])
        for g, h_ref in enumerate(h_refs):
            @pl.when(part == g)
            def _():
                for m in range(SEQ // cm):
                    rows = pl.ds(m * cm, cm)
                    o_ref[rows, :] = jnp.dot(h_ref[rows, :], w_ref[...], preferred_element_type=F32).astype(BF16)

    resident = pl.BlockSpec((SEQ, kc), lambda t, ids: (0, 0))
    in_specs = [resident] * n_parts + [
        pl.BlockSpec((None, kc, tn), lambda t, ids: (ids[0, t] // tps, 0, ids[0, t] % tps))]
    args = [*h_parts, w3]
    aliases = {}
    if prev is not None:
        in_specs.append(pl.BlockSpec(memory_space=pl.ANY))
        args.append(prev)
        aliases = {n_parts + 2: 0}
    return pl.pallas_call(
        body, name=name,
        grid_spec=pltpu.PrefetchScalarGridSpec(
            num_scalar_prefetch=1, grid=(n_tiles,), in_specs=in_specs,
            out_specs=pl.BlockSpec((SEQ, tn), lambda t, ids: (0, ids[1, t]))),
        out_shape=jax.ShapeDtypeStruct((SEQ, total_tiles * tn), BF16),
        input_output_aliases=aliases, compiler_params=_params("arbitrary"),
    )(tile_ids, *args)


def _own_first(chip, total_tiles):
    own = total_tiles // N_CHIPS
    step = jnp.arange(total_tiles, dtype=jnp.int32)
    tiles = (own * chip + step) % total_tiles
    return jnp.stack([step[:own], tiles[:own]]), jnp.stack([tiles[own:], tiles[own:]]), own


def _mm_nt(dy, w3, *, tn, tile0, n_tiles, name, after=None):
    m_rows = dy.shape[0]
    _, kc, ns = w3.shape
    tps = ns // tn
    cm = 512
    extra = [] if after is None else [after]

    def body(dy_ref, w_ref, *rest):
        o_ref, acc = rest[-2], rest[-1]
        t = pl.program_id(0)

        @pl.when(t == 0)
        def _():
            acc[...] = jnp.zeros_like(acc)

        for m in range(m_rows // cm):
            rows = pl.ds(m * cm, cm)
            acc[rows, :] += lax.dot_general(dy_ref[rows, :], w_ref[...], NT_DIMS, preferred_element_type=F32)

        @pl.when(t == n_tiles - 1)
        def _():
            o_ref[...] = acc[...].astype(BF16)

    return pl.pallas_call(
        body, name=name, grid=(n_tiles,),
        in_specs=[pl.BlockSpec((m_rows, tn), lambda t: (0, t)),
                  pl.BlockSpec((None, kc, tn), lambda t: ((tile0 + t) // tps, 0, (tile0 + t) % tps))]
        + [pl.BlockSpec(memory_space=pl.ANY)] * len(extra),
        out_specs=pl.BlockSpec((m_rows, kc), lambda t: (0, 0)),
        out_shape=jax.ShapeDtypeStruct((m_rows, kc), BF16),
        scratch_shapes=[pltpu.VMEM((m_rows, kc), F32)],
        compiler_params=_params("arbitrary"),
    )(dy, w3, *extra)


CONV_CHUNK = 16


def _shift_copies(buf, shifted):
    rows = shifted.shape[1]
    for s in range(1, 8):
        shifted[s - 1] = buf[pl.ds(s, rows), :]


def _shifted_rows(buf, shifted, offset, r0):
    s = offset % 8
    if s == 0:
        return buf[pl.ds(r0 + offset, CONV_CHUNK), :]
    return shifted[s - 1, pl.ds(r0 + (offset - s), CONV_CHUNK), :]


def _spread_taps(w_ref, taps):
    for k in range(CONV_WIDTH):
        taps[k] = jnp.broadcast_to(w_ref[k:k + 1, :], (8, D_MODEL))


def _times_tap(taps, k, rows):
    return (rows.reshape(CONV_CHUNK // 8, 8, D_MODEL) * taps[k][None]).reshape(CONV_CHUNK, D_MODEL)


def _conv_fwd(proj, conv_w, conv_b, ln_g, ln_b, name):
    tm = ROW_TILE
    hb = tm // HALO

    def body(vg_ref, halo_ref, z_ref, w_ref, b_ref, g_ref, be_ref, u5_ref, u5t_ref, u2_ref, buf, shifted, taps):
        i = pl.program_id(0)
        u1 = vg_ref[:, :D_MODEL].astype(F32) * _sigmoid(vg_ref[:, D_MODEL:].astype(F32))
        u1h = halo_ref[:, :D_MODEL].astype(F32) * _sigmoid(halo_ref[:, D_MODEL:].astype(F32))
        buf[pl.ds(0, HALO), :] = jnp.where(i > 0, u1h, 0.0)
        buf[pl.ds(HALO, tm), :] = u1
        _shift_copies(buf, shifted)
        _spread_taps(w_ref, taps)

        def chunk(ci, carry):
            r0 = pl.multiple_of(ci * CONV_CHUNK, CONV_CHUNK)
            acc = jnp.broadcast_to(b_ref[...], (CONV_CHUNK, D_MODEL))
            for k in range(CONV_WIDTH):
                acc = acc + _times_tap(taps, k, _shifted_rows(buf, shifted, HALO - (CONV_WIDTH - 1) + k, r0))
            u2_ref[pl.ds(r0, CONV_CHUNK), :] = acc
            return carry

        lax.fori_loop(0, tm // CONV_CHUNK, chunk, 0)
        acc = u2_ref[...]
        mu = jnp.mean(acc, axis=-1, keepdims=True)
        xc = acc - mu
        rstd = lax.rsqrt(jnp.mean(xc * xc, axis=-1, keepdims=True) + NORM_EPS)
        u3 = xc * rstd * g_ref[...] + be_ref[...]
        zv = z_ref[...].astype(F32)
        u5 = u3 * _sigmoid(u3) * (zv * _sigmoid(zv))
        u5_ref[...] = u5.astype(BF16)
        u5t_ref[...] = u5.T.astype(BF16)

    return pl.pallas_call(
        body, name=name, grid=(SEQ // tm,),
        in_specs=[pl.BlockSpec((tm, 2 * D_MODEL), lambda i: (i, 0)),
                  pl.BlockSpec((HALO, 2 * D_MODEL), lambda i: (jnp.maximum(i * hb - 1, 0), 0)),
                  _row_spec(tm, D_MODEL, 2),
                  _vec_spec(CONV_WIDTH, D_MODEL)] + [_vec_spec(1, D_MODEL)] * 3,
        out_specs=[_row_spec(tm, D_MODEL), pl.BlockSpec((D_MODEL, tm), lambda i: (0, i)), _row_spec(tm, D_MODEL)],
        out_shape=[jax.ShapeDtypeStruct((SEQ, D_MODEL), BF16), jax.ShapeDtypeStruct((D_MODEL, SEQ), BF16),
                   jax.ShapeDtypeStruct((SEQ, D_MODEL), F32)],
        scratch_shapes=[pltpu.VMEM((HALO + tm, D_MODEL), F32), pltpu.VMEM((7, HALO + tm - 8, D_MODEL), F32),
                        pltpu.VMEM((CONV_WIDTH, 8, D_MODEL), F32)],
        compiler_params=_params("parallel"),
    )(proj, proj, proj, conv_w, conv_b, ln_g, ln_b)


def _conv_bwd_pointwise(dy, w_out, proj, u2, ln_g, ln_b, name):
    tm = ROW_TILE

    def body(dy_ref, w_ref, z_ref, u2_ref, g_ref, be_ref, du2_ref, dz_ref, sums_ref):
        u2v = u2_ref[...]
        mu = jnp.mean(u2v, axis=-1, keepdims=True)
        xc = u2v - mu
        rstd = lax.rsqrt(jnp.mean(xc * xc, axis=-1, keepdims=True) + NORM_EPS)
        xhat = xc * rstd
        u3 = xhat * g_ref[...] + be_ref[...]
        s3 = _sigmoid(u3)
        u4 = u3 * s3
        zv = z_ref[...].astype(F32)
        sz = _sigmoid(zv)
        du5v = lax.dot_general(dy_ref[...], w_ref[...], NT_DIMS, preferred_element_type=F32)
        dz_ref[...] = du5v * u4 * (sz * (1.0 + zv * (1.0 - sz)))
        du3 = du5v * (zv * sz) * (s3 * (1.0 + u3 * (1.0 - s3)))
        dxhat = du3 * g_ref[...]
        du2 = rstd * (dxhat - jnp.mean(dxhat, axis=-1, keepdims=True)
                      - xhat * jnp.mean(dxhat * xhat, axis=-1, keepdims=True))
        du2_ref[...] = du2
        sums = jnp.concatenate([
            jnp.sum(du3 * xhat, axis=0, keepdims=True),
            jnp.sum(du3, axis=0, keepdims=True),
            jnp.sum(du2, axis=0, keepdims=True),
            jnp.zeros((5, D_MODEL), F32)], axis=0)

        @pl.when(pl.program_id(0) == 0)
        def _():
            sums_ref[...] = jnp.zeros_like(sums_ref)

        sums_ref[...] += sums

    return pl.pallas_call(
        body, name=name, grid=(SEQ // tm,),
        in_specs=[_row_spec(tm, D_MODEL), _vec_spec(D_MODEL, D_MODEL), _row_spec(tm, D_MODEL, 2),
                  _row_spec(tm, D_MODEL), _vec_spec(1, D_MODEL), _vec_spec(1, D_MODEL)],
        out_specs=[_row_spec(tm, D_MODEL), _row_spec(tm, D_MODEL), _vec_spec(8, D_MODEL)],
        out_shape=[jax.ShapeDtypeStruct((SEQ, D_MODEL), F32), jax.ShapeDtypeStruct((SEQ, D_MODEL), F32),
                   jax.ShapeDtypeStruct((8, D_MODEL), F32)],
        compiler_params=_params("arbitrary"),
    )(dy, w_out, proj, u2, ln_g, ln_b)


def _conv_bwd_taps(du2, dz, proj, conv_w, name):
    tm = ROW_TILE
    hb = tm // HALO
    n_blocks = SEQ // tm

    def body(du2_ref, dnext_ref, dz_ref, vg_ref, w_ref, dproj_ref, dw_ref, dbuf, dshift, sgbuf, ubuf, dwacc, taps):
        i = pl.program_id(0)
        _spread_taps(w_ref, taps)
        sg = _sigmoid(vg_ref[:, D_MODEL:].astype(F32))
        sgbuf[...] = sg
        ubuf[...] = vg_ref[:, :D_MODEL].astype(F32) * sg
        dbuf[pl.ds(0, tm), :] = du2_ref[...]
        dbuf[pl.ds(tm, HALO), :] = jnp.where(i < n_blocks - 1, dnext_ref[...], 0.0)
        _shift_copies(dbuf, dshift)

        @pl.when(i == 0)
        def _():
            dwacc[...] = jnp.zeros_like(dwacc)

        def chunk(ci, carry):
            r0 = pl.multiple_of(ci * CONV_CHUNK, CONV_CHUNK)
            rows = pl.ds(r0, CONV_CHUNK)
            u1c = ubuf[rows, :]
            du1 = jnp.zeros((CONV_CHUNK, D_MODEL), F32)
            for k in range(CONV_WIDTH):
                ahead = _shifted_rows(dbuf, dshift, CONV_WIDTH - 1 - k, r0)
                du1 = du1 + _times_tap(taps, k, ahead)
                prod = u1c * ahead
                dwacc[k] += prod[0:8] + prod[8:16]
            sgc = sgbuf[rows, :]
            dval = du1 * sgc
            dproj_ref[rows, 0:D_MODEL] = dval.astype(BF16)
            dproj_ref[rows, D_MODEL:2 * D_MODEL] = (
                dval * vg_ref[rows, 0:D_MODEL].astype(F32) * (1.0 - sgc)).astype(BF16)
            return carry

        lax.fori_loop(0, tm // CONV_CHUNK, chunk, 0)
        dproj_ref[:, 2 * D_MODEL:] = dz_ref[...].astype(BF16)

        @pl.when(i == n_blocks - 1)
        def _():
            for k in range(CONV_WIDTH):
                dw_ref[k:k + 1, :] = jnp.sum(dwacc[k], axis=0, keepdims=True)
            dw_ref[CONV_WIDTH:, :] = jnp.zeros((32 - CONV_WIDTH, D_MODEL), F32)

    return pl.pallas_call(
        body, name=name, grid=(n_blocks,),
        in_specs=[_row_spec(tm, D_MODEL),
                  pl.BlockSpec((HALO, D_MODEL), lambda i: (jnp.minimum((i + 1) * hb, SEQ // HALO - 1), 0)),
                  _row_spec(tm, D_MODEL),
                  pl.BlockSpec((tm, 2 * D_MODEL), lambda i: (i, 0)),
                  _vec_spec(CONV_WIDTH, D_MODEL)],
        out_specs=[_row_spec(tm, 3 * D_MODEL), _vec_spec(32, D_MODEL)],
        out_shape=[jax.ShapeDtypeStruct((SEQ, 3 * D_MODEL), BF16), jax.ShapeDtypeStruct((32, D_MODEL), F32)],
        scratch_shapes=[pltpu.VMEM((tm + HALO, D_MODEL), F32), pltpu.VMEM((7, HALO + tm - 8, D_MODEL), F32),
                        pltpu.VMEM((tm, D_MODEL), F32), pltpu.VMEM((tm, D_MODEL), F32),
                        pltpu.VMEM((CONV_WIDTH, 8, D_MODEL), F32), pltpu.VMEM((CONV_WIDTH, 8, D_MODEL), F32)],
        compiler_params=_params("arbitrary"),
    )(du2, du2, dz, proj, conv_w)


def _out_a(u5, w_out, x, gate, g1, scale1, shift1, name):
    tm = ROW_TILE
    n_d = len(DILATIONS)

    def body(u_ref, w_ref, x_ref, gate_ref, g_ref, sc_ref, sh_ref, x1_ref, y_ref, ht_ref, *rest):
        h_refs, nat = rest[:n_d], rest[-1]
        y = jnp.dot(u_ref[...], w_ref[...], preferred_element_type=F32)
        x1 = x_ref[...] + gate_ref[...] * y
        y_ref[...] = y.astype(BF16)
        x1_ref[...] = x1
        h = _normmod(x1, g_ref[...], sc_ref[...], sh_ref[...])
        ht_ref[...] = h.T.astype(BF16)
        for h_ref, d in zip(h_refs, DILATIONS):
            _store_classes(h_ref, h, nat, d)

    res = pl.pallas_call(
        body, name=name, grid=(SEQ // tm,),
        in_specs=[_row_spec(tm, D_MODEL), _vec_spec(D_MODEL, D_MODEL), _row_spec(tm, D_MODEL)]
        + [_vec_spec(1, D_MODEL)] * 4,
        out_specs=[_row_spec(tm, D_MODEL), _row_spec(tm, D_MODEL), pl.BlockSpec((D_MODEL, tm), lambda i: (0, i))]
        + [_class_spec(tm, d) for d in DILATIONS],
        out_shape=[jax.ShapeDtypeStruct((SEQ, D_MODEL), F32), jax.ShapeDtypeStruct((SEQ, D_MODEL), BF16),
                   jax.ShapeDtypeStruct((D_MODEL, SEQ), BF16)] + [_class_shape(d, BF16) for d in DILATIONS],
        scratch_shapes=[_natural_scratch(tm)],
        compiler_params=_params("parallel"),
    )(u5, w_out, x, gate, g1, scale1, shift1)
    return res[0], res[1], res[2], [a.reshape(SEQ, D_MODEL) for a in res[3:]]


def _out_b_loss(u, w_out, x1, gate, target, name):
    tm = ROW_TILE

    def body(u_ref, w_ref, x_ref, gate_ref, t_ref, e_ref, dy_ref, sums_ref):
        y = jnp.dot(u_ref[...], w_ref[...], preferred_element_type=F32)
        diff = x_ref[...] + gate_ref[...] * y - t_ref[...]
        e = diff * (1.0 / D_MODEL)
        e_ref[...] = e
        dy_ref[...] = (e * gate_ref[...]).astype(BF16)
        sums = jnp.concatenate([
            jnp.sum(e * y, axis=0, keepdims=True),
            jnp.sum(diff * diff, axis=0, keepdims=True),
            jnp.zeros((6, D_MODEL), F32)], axis=0)

        @pl.when(pl.program_id(0) == 0)
        def _():
            sums_ref[...] = jnp.zeros_like(sums_ref)

        sums_ref[...] += sums

    return pl.pallas_call(
        body, name=name, grid=(SEQ // tm,),
        in_specs=[_row_spec(tm, D_MODEL), _vec_spec(D_MODEL, D_MODEL), _row_spec(tm, D_MODEL),
                  _vec_spec(1, D_MODEL), _row_spec(tm, D_MODEL)],
        out_specs=[_row_spec(tm, D_MODEL), _row_spec(tm, D_MODEL), _vec_spec(8, D_MODEL)],
        out_shape=[jax.ShapeDtypeStruct((SEQ, D_MODEL), F32), jax.ShapeDtypeStruct((SEQ, D_MODEL), BF16),
                   jax.ShapeDtypeStruct((8, D_MODEL), F32)],
        compiler_params=_params("arbitrary"),
    )(u, w_out, x1, gate, target)


def _seg_matrix():
    r = lax.broadcasted_iota(jnp.int32, (256, 256), 0) // HEAD_DIM
    c = lax.broadcasted_iota(jnp.int32, (256, 256), 1) // HEAD_DIM
    return jnp.where(r == c, 1.0 / HEAD_DIM, 0.0).astype(BF16)


def _segmean(v, seg):
    hi = v.astype(BF16)
    lo = (v - hi.astype(F32)).astype(BF16)
    outs = []
    for c0 in range(0, D_MODEL, 256):
        outs.append(jnp.dot(hi[:, c0:c0 + 256], seg, preferred_element_type=F32)
                    + jnp.dot(lo[:, c0:c0 + 256], seg, preferred_element_type=F32))
    return jnp.concatenate(outs, axis=1)


def _qk_rstd(v, seg):
    return lax.rsqrt(_segmean(v * v, seg) + NORM_EPS)


def _qknorm_fwd(proj, group, qw, kw, seg, name):
    tm = ROW_TILE

    def body(q_in, k_in, qw_ref, kw_ref, seg_ref, q_ref, k_ref):
        segv = seg_ref[...]
        q = q_in[...].astype(F32)
        k = k_in[...].astype(F32)
        q_ref[...] = (q * _qk_rstd(q, segv) * qw_ref[...] * HEAD_DIM ** -0.5).astype(BF16)
        k_ref[...] = (k * _qk_rstd(k, segv) * kw_ref[...]).astype(BF16)

    return pl.pallas_call(
        body, name=name, grid=(SEQ // tm,),
        in_specs=[_row_spec(tm, D_MODEL, 3 * group), _row_spec(tm, D_MODEL, 3 * group + 1),
                  _vec_spec(1, D_MODEL), _vec_spec(1, D_MODEL), _vec_spec(256, 256)],
        out_specs=[_row_spec(tm, D_MODEL)] * 2,
        out_shape=[jax.ShapeDtypeStruct((SEQ, D_MODEL), BF16)] * 2,
        compiler_params=_params("parallel"),
    )(proj, proj, qw, kw, seg)


def _attn_masks(b, bpc, dilation, transposed=False):
    keys = ATTN_BLOCK if bpc == 1 else 2 * ATTN_BLOCK
    shape, q_axis = ((keys, ATTN_BLOCK), 1) if transposed else ((ATTN_BLOCK, keys), 0)
    qi = lax.broadcasted_iota(jnp.int32, shape, q_axis)
    kj = lax.broadcasted_iota(jnp.int32, shape, 1 - q_axis)
    if bpc == 1:
        steps = qi - kj
        return (steps * dilation).astype(F32), steps >= 0
    steps = qi + ATTN_BLOCK - kj
    has_prev = (b % bpc) != 0
    valid = (steps >= 0) & (steps <= ATTN_BLOCK) & (has_prev | (kj >= ATTN_BLOCK))
    return (steps * dilation).astype(F32), valid


MASKED = 1e30


def _bias_scratch(bpc):
    return pltpu.VMEM((1 if bpc == 1 else 2, N_HEADS, ATTN_BLOCK, (1 if bpc == 1 else 2) * ATTN_BLOCK), F32)


def _fill_bias(bias_ref, sl_ref, bpc, dilation):
    for variant in range(bias_ref.shape[0]):
        dist, valid = _attn_masks(variant, min(bpc, 2), dilation)
        bias_ref[variant] = jnp.where(valid[None], dist[None] * sl_ref[...], MASKED)


def _step_bias(bias_ref, b, bpc):
    if bpc == 1:
        return bias_ref[0]
    return bias_ref[jnp.where((b % bpc) != 0, 1, 0)]


def _key_tile(prev_ref, cur_ref, cols, bpc):
    if bpc == 1:
        return cur_ref[:, cols]
    return jnp.concatenate([prev_ref[:, cols], cur_ref[:, cols]], axis=0)


ATTN_HEADS_FWD = 16
ATTN_HEADS_BWD = 16
NT_DIMS = (((1,), (1,)), ((), ()))
BATCH_NT_DIMS = (((2,), (2,)), ((0,), (0,)))
BATCH_NN_DIMS = (((2,), (1,)), ((0,), (0,)))
BATCH_TN_DIMS = (((1,), (1,)), ((0,), (0,)))


def _head_stack(tile_of, heads):
    return jnp.stack([tile_of(slice(h * HEAD_DIM, (h + 1) * HEAD_DIM)) for h in range(heads)], axis=0)


def _attn_specs(heads, segment=0):
    width = heads * HEAD_DIM
    off = segment * (D_MODEL // width)
    last = SEQ // ATTN_BLOCK - 1
    cur = pl.BlockSpec((ATTN_BLOCK, width), lambda hg, b: (jnp.minimum(b, last), hg + off))
    prev = pl.BlockSpec((ATTN_BLOCK, width), lambda hg, b: (jnp.clip(b - 1, 0, last), hg + off))
    return cur, prev


def _attn_fwd(q, k, proj, group, slopes, dilation, name):
    bpc = SEQ // dilation // ATTN_BLOCK
    heads = ATTN_HEADS_FWD
    assert heads == N_HEADS
    cur, prev = _attn_specs(heads)
    v_cur, v_prev = _attn_specs(heads, segment=3 * group + 2)

    def body(sl_ref, q_ref, kp_ref, kc_ref, vp_ref, vc_ref, o_ref, lse_ref, bias_ref):
        b = pl.program_id(1)

        @pl.when(b == 0)
        def _():
            _fill_bias(bias_ref, sl_ref, bpc, dilation)

        q3 = _head_stack(lambda cols: q_ref[:, cols], heads)
        k3 = _head_stack(lambda cols: _key_tile(kp_ref, kc_ref, cols, bpc), heads)
        v3 = _head_stack(lambda cols: _key_tile(vp_ref, vc_ref, cols, bpc), heads)
        s = lax.dot_general(q3, k3, BATCH_NT_DIMS, preferred_element_type=F32)
        s = s - _step_bias(bias_ref, b, bpc)
        m = jnp.max(s, axis=-1, keepdims=True)
        p = jnp.exp(s - m)
        l = jnp.sum(p, axis=-1, keepdims=True)
        o3 = lax.dot_general(p.astype(BF16), v3, BATCH_NN_DIMS, preferred_element_type=F32) / l
        lse3 = m + jnp.log(l)
        for h in range(heads):
            o_ref[:, h * HEAD_DIM:(h + 1) * HEAD_DIM] = o3[h].astype(BF16)
        lse_ref[...] = jnp.concatenate([lse3[h] for h in range(heads)]
                                       + [jnp.zeros((ATTN_BLOCK, LANES - heads), F32)], axis=1)

    return pl.pallas_call(
        body, name=name, grid=(N_HEADS // heads, SEQ // ATTN_BLOCK),
        in_specs=[pl.BlockSpec((heads, 1, 1), lambda hg, b: (hg, 0, 0)), cur, prev, cur, v_prev, v_cur],
        out_specs=[cur, pl.BlockSpec((ATTN_BLOCK, LANES), lambda hg, b: (b, 0))],
        out_shape=[jax.ShapeDtypeStruct((SEQ, D_MODEL), BF16), jax.ShapeDtypeStruct((SEQ, LANES), F32)],
        scratch_shapes=[_bias_scratch(bpc)],
        compiler_params=_params("parallel", "arbitrary"),
    )(slopes.reshape(N_HEADS, 1, 1), q, k, k, proj, proj)


def _class_spec(tm, dilation, width=D_MODEL):
    if dilation == 1:
        return _row_spec(tm, width)
    return pl.BlockSpec((dilation, tm // dilation, width), lambda i: (0, i, 0))


def _class_shape(dilation, dtype, width=D_MODEL):
    if dilation == 1:
        return jax.ShapeDtypeStruct((SEQ, width), dtype)
    return jax.ShapeDtypeStruct((dilation, SEQ // dilation, width), dtype)


def _load_natural(in_ref, nat_ref, dilation):
    if dilation == 1:
        return in_ref[...].astype(F32)
    n = nat_ref.shape[1] // dilation
    tiles = in_ref.shape[-1] // LANES
    for r in range(dilation):
        for j in range(tiles):
            nat_ref.at[j][pl.ds(r, n, stride=dilation), :] = in_ref[r, :, j * LANES:(j + 1) * LANES].astype(F32)
    if tiles == 1:
        return nat_ref[0]
    return jnp.concatenate([nat_ref[j] for j in range(tiles)], axis=1)


def _store_classes(out_ref, value, nat_ref, dilation):
    if dilation == 1:
        out_ref[...] = value.astype(out_ref.dtype)
        return
    n = nat_ref.shape[1] // dilation
    tiles = value.shape[-1] // LANES
    for j in range(tiles):
        nat_ref[j] = value[:, j * LANES:(j + 1) * LANES]
    for r in range(dilation):
        for j in range(tiles):
            out_ref[r, :, j * LANES:(j + 1) * LANES] = (
                nat_ref.at[j][pl.ds(r, n, stride=dilation), :].astype(out_ref.dtype))


def _natural_scratch(tm):
    return pltpu.VMEM((D_MODEL // LANES, tm, LANES), F32)


def _head_selector():
    lane_head = lax.broadcasted_iota(jnp.int32, (D_MODEL, LANES), 0) // HEAD_DIM
    head = lax.broadcasted_iota(jnp.int32, (D_MODEL, LANES), 1)
    return (lane_head == head).astype(BF16)


def _dot_split(v, m01, dims):
    hi = v.astype(BF16)
    lo = (v - hi.astype(F32)).astype(BF16)
    return (lax.dot_general(hi, m01, dims, preferred_element_type=F32)
            + lax.dot_general(lo, m01, dims, preferred_element_type=F32))


def _merge_fwd(o_parts, lse_parts, z, sel, name):
    tm = ROW_TILE
    h_spec = pl.BlockSpec((tm, LANES), lambda i: (i, 0))

    def body(o0, o1, o2, l0, l1, l2, z_ref, sel_ref, u_ref, ut_ref, o_ref, lse_ref, nat):
        ls = [_load_natural(l, nat, d) for l, d in zip((l0, l1, l2), DILATIONS)]
        m = jnp.maximum(jnp.maximum(ls[0], ls[1]), ls[2])
        tot = m + jnp.log(jnp.exp(ls[0] - m) + jnp.exp(ls[1] - m) + jnp.exp(ls[2] - m))
        o = jnp.zeros((tm, D_MODEL), F32)
        for o_in, l, d in zip((o0, o1, o2), ls, DILATIONS):
            weight = _dot_split(jnp.exp(l - tot), sel_ref[...], NT_DIMS)
            o = o + weight * _load_natural(o_in, nat, d)
        zv = z_ref[...].astype(F32)
        u = o * (zv * _sigmoid(zv))
        u_ref[...] = u.astype(BF16)
        ut_ref[...] = u.T.astype(BF16)
        o_ref[...] = o.astype(BF16)
        lse_ref[...] = tot

    return pl.pallas_call(
        body, name=name, grid=(SEQ // tm,),
        in_specs=[_class_spec(tm, d) for d in DILATIONS] + [_class_spec(tm, d, LANES) for d in DILATIONS]
        + [_row_spec(tm, D_MODEL, B_Z_SEGMENT), _vec_spec(D_MODEL, LANES)],
        out_specs=[_row_spec(tm, D_MODEL), pl.BlockSpec((D_MODEL, tm), lambda i: (0, i)),
                   _row_spec(tm, D_MODEL), h_spec],
        out_shape=[jax.ShapeDtypeStruct((SEQ, D_MODEL), BF16), jax.ShapeDtypeStruct((D_MODEL, SEQ), BF16),
                   jax.ShapeDtypeStruct((SEQ, D_MODEL), BF16), jax.ShapeDtypeStruct((SEQ, LANES), F32)],
        scratch_shapes=[_natural_scratch(tm)],
        compiler_params=_params("parallel"),
    )(*o_parts, *lse_parts, z, sel)


def _merge_bwd(dy, w_out, o, lse, z, sel, name):
    tm = ROW_TILE
    n_d = len(DILATIONS)

    def body(dy_ref, w_ref, o_ref, lse_ref, z_ref, sel_ref, dz_ref, *rest):
        do_refs, delta_refs, lse_refs, nat = rest[:n_d], rest[n_d:2 * n_d], rest[2 * n_d:3 * n_d], rest[-1]
        zv = z_ref[...].astype(F32)
        sz = _sigmoid(zv)
        duv = lax.dot_general(dy_ref[...], w_ref[...], NT_DIMS, preferred_element_type=F32)
        ov = o_ref[...].astype(F32)
        do = duv * (zv * sz)
        dz_ref[...] = (duv * ov * (sz * (1.0 + zv * (1.0 - sz)))).astype(BF16)
        delta = _dot_split(do * ov, sel_ref[...], (((1,), (0,)), ((), ())))
        lv = lse_ref[...]
        for i, d in enumerate(DILATIONS):
            _store_classes(do_refs[i], do, nat, d)
            _store_classes(delta_refs[i], delta, nat, d)
            _store_classes(lse_refs[i], lv, nat, d)

    res = pl.pallas_call(
        body, name=name, grid=(SEQ // tm,),
        in_specs=[_row_spec(tm, D_MODEL), _vec_spec(D_MODEL, D_MODEL), _row_spec(tm, D_MODEL), _row_spec(tm, LANES),
                  _row_spec(tm, D_MODEL, B_Z_SEGMENT), _vec_spec(D_MODEL, LANES)],
        out_specs=[_row_spec(tm, D_MODEL)] + [_class_spec(tm, d) for d in DILATIONS]
        + [_class_spec(tm, d, LANES) for d in DILATIONS] * 2,
        out_shape=[jax.ShapeDtypeStruct((SEQ, D_MODEL), BF16)] + [_class_shape(d, BF16) for d in DILATIONS]
        + [_class_shape(d, F32, LANES) for d in DILATIONS] * 2,
        scratch_shapes=[_natural_scratch(tm)],
        compiler_params=_params("parallel"),
    )(dy, w_out, o, lse, z, sel)
    flat = lambda a: a.reshape(SEQ, a.shape[-1])
    return (res[0], [flat(a) for a in res[1:1 + n_d]], [flat(a) for a in res[1 + n_d:1 + 2 * n_d]],
            [flat(a) for a in res[1 + 2 * n_d:]])


def _attn_bwd(q, k, proj, group, do, lse, delta, slopes, dilation, name):
    bpc = SEQ // dilation // ATTN_BLOCK
    heads = ATTN_HEADS_BWD
    n_blocks = SEQ // ATTN_BLOCK
    carry = bpc > 1
    width = heads * HEAD_DIM
    cur, prev = _attn_specs(heads)
    v_cur, v_prev = _attn_specs(heads, segment=3 * group + 2)
    assert heads == N_HEADS
    per_head = pl.BlockSpec((ATTN_BLOCK, LANES), lambda hg, b: (jnp.minimum(b, n_blocks - 1), 0))
    scale = HEAD_DIM ** -0.5

    def body(sl_ref, q_ref, kp_ref, kc_ref, vp_ref, vc_ref, do_ref, lse_ref, dl_ref,
             dq_ref, dk_ref, dv_ref, *scratch):
        b = pl.program_id(1)
        if carry:
            dk_carry, dv_carry = scratch

            @pl.when(b == n_blocks)
            def _():
                dk_ref[...] = dk_carry[...].astype(BF16)
                dv_ref[...] = dv_carry[...].astype(BF16)

            @pl.when(b < n_blocks)
            def _():
                step(sl_ref, q_ref, kp_ref, kc_ref, vp_ref, vc_ref, do_ref, lse_ref, dl_ref,
                     dq_ref, dk_ref, dv_ref, dk_carry, dv_carry, b)
        else:
            step(sl_ref, q_ref, kp_ref, kc_ref, vp_ref, vc_ref, do_ref, lse_ref, dl_ref,
                 dq_ref, dk_ref, dv_ref, None, None, b)

    def step(sl_ref, q_ref, kp_ref, kc_ref, vp_ref, vc_ref, do_ref, lse_ref, dl_ref,
             dq_ref, dk_ref, dv_ref, dk_carry, dv_carry, b):
        if carry:
            @pl.when(b == 0)
            def _():
                dk_carry[...] = jnp.zeros_like(dk_carry)
                dv_carry[...] = jnp.zeros_like(dv_carry)

        q3 = _head_stack(lambda cols: q_ref[:, cols], heads)
        k3 = _head_stack(lambda cols: _key_tile(kp_ref, kc_ref, cols, bpc), heads)
        v3 = _head_stack(lambda cols: _key_tile(vp_ref, vc_ref, cols, bpc), heads)
        do3 = _head_stack(lambda cols: do_ref[:, cols], heads)
        lse_t = lse_ref[...].T
        dl_t = dl_ref[...].T
        lse3 = jnp.stack([lse_t[h:h + 1, :] for h in range(heads)], axis=0)
        dl3 = jnp.stack([dl_t[h:h + 1, :] for h in range(heads)], axis=0)
        s = lax.dot_general(k3, q3, BATCH_NT_DIMS, preferred_element_type=F32)
        dist, valid = _attn_masks(b, bpc, dilation, transposed=True)
        p = jnp.exp(jnp.where(valid[None], s - dist[None] * sl_ref[...], NEG_INF) - lse3)
        dp = lax.dot_general(v3, do3, BATCH_NT_DIMS, preferred_element_type=F32)
        ds = (p * (dp - dl3)).astype(BF16)
        dq3 = lax.dot_general(ds, k3, BATCH_TN_DIMS, preferred_element_type=F32) * scale
        dk3 = lax.dot_general(ds, q3, BATCH_NN_DIMS, preferred_element_type=F32)
        dv3 = lax.dot_general(p.astype(BF16), do3, BATCH_NN_DIMS, preferred_element_type=F32)
        for h in range(heads):
            cols = slice(h * HEAD_DIM, (h + 1) * HEAD_DIM)
            dq_ref[:, cols] = dq3[h].astype(BF16)
            if carry:
                dk_ref[:, cols] = (dk_carry[:, cols] + dk3[h, :ATTN_BLOCK]).astype(BF16)
                dv_ref[:, cols] = (dv_carry[:, cols] + dv3[h, :ATTN_BLOCK]).astype(BF16)
                dk_carry[:, cols] = dk3[h, ATTN_BLOCK:]
                dv_carry[:, cols] = dv3[h, ATTN_BLOCK:]
            else:
                dk_ref[:, cols] = dk3[h].astype(BF16)
                dv_ref[:, cols] = dv3[h].astype(BF16)

    kv_out = prev if carry else cur
    return pl.pallas_call(
        body, name=name, grid=(N_HEADS // heads, n_blocks + (1 if carry else 0)),
        in_specs=[pl.BlockSpec((heads, 1, 1), lambda hg, b: (hg, 0, 0)), cur, prev, cur, v_prev, v_cur,
                  cur, per_head, per_head],
        out_specs=[cur, kv_out, kv_out],
        out_shape=[jax.ShapeDtypeStruct((SEQ, D_MODEL), BF16)] * 3,
        scratch_shapes=[pltpu.VMEM((ATTN_BLOCK, width), F32)] * 2 if carry else [],
        compiler_params=_params("parallel", "arbitrary"),
    )(slopes.reshape(N_HEADS, 1, 1), q, k, k, proj, proj, do, lse, delta)


def _qknorm_bwd(proj, group, qw, kw, seg, dq, dk, dv, name):
    tm = ROW_TILE

    def body(q_in, k_in, qw_ref, kw_ref, seg_ref, dq_ref, dk_ref, dv_ref, dproj_ref, sums_ref):
        segv = seg_ref[...]
        sums = []
        for part, (raw_ref, w_ref, dn_ref) in enumerate(((q_in, qw_ref, dq_ref), (k_in, kw_ref, dk_ref))):
            raw = raw_ref[...].astype(F32)
            dn = dn_ref[...].astype(F32)
            r = _qk_rstd(raw, segv)
            xhat = raw * r
            gq = dn * w_ref[...]
            draw = r * (gq - xhat * _segmean(xhat * gq, segv))
            dproj_ref[:, part * D_MODEL:(part + 1) * D_MODEL] = draw.astype(BF16)
            sums.append(jnp.sum(dn * xhat, axis=0, keepdims=True))
        dproj_ref[:, 2 * D_MODEL:] = dv_ref[...]

        @pl.when(pl.program_id(0) == 0)
        def _():
            sums_ref[...] = jnp.zeros_like(sums_ref)

        sums_ref[...] += jnp.concatenate(sums + [jnp.zeros((6, D_MODEL), F32)], axis=0)

    return pl.pallas_call(
        body, name=name, grid=(SEQ // tm,),
        in_specs=[_row_spec(tm, D_MODEL, 3 * group), _row_spec(tm, D_MODEL, 3 * group + 1),
                  _vec_spec(1, D_MODEL), _vec_spec(1, D_MODEL), _vec_spec(256, 256)] + [_row_spec(tm, D_MODEL)] * 3,
        out_specs=[_row_spec(tm, 3 * D_MODEL), _vec_spec(8, D_MODEL)],
        out_shape=[jax.ShapeDtypeStruct((SEQ, 3 * D_MODEL), BF16), jax.ShapeDtypeStruct((8, D_MODEL), F32)],
        compiler_params=_params("arbitrary"),
    )(proj, proj, qw, kw, seg, dq, dk, dv)


B_TN = 512
B_GROUP_TILES = 3 * D_MODEL // B_TN
B_Z_TILE0 = 3 * B_GROUP_TILES
B_Z_TILES = D_MODEL // B_TN
B_TILES = B_Z_TILE0 + B_Z_TILES
B_Z_SEGMENT = 3 * len(DILATIONS)


def _local_step(x, target, mods, norm_g, conv_w, conv_b, ln_g, ln_b, q_norm, k_norm, chip, own_wa_in, own_wb_in,
                forward_weights, weights, send_grads_b, forward_grads_b, send_grads_a):
    row = lambda a, i: a[i:i + 1]
    shift0, scale0, gate0 = row(mods[0], 0), row(mods[0], 1), row(mods[0], 2)
    shift1, scale1, gate1 = row(mods[1], 0), row(mods[1], 1), row(mods[1], 2)
    g0, g1 = row(norm_g, 0), row(norm_g, 1)
    seg = _seg_matrix()
    slopes = jnp.exp2(-8.0 * jnp.arange(1, N_HEADS + 1, dtype=F32) / N_HEADS)
    qw = [jnp.tile(q_norm[g:g + 1], (1, N_HEADS)) for g in range(3)]
    kw = [jnp.tile(k_norm[g:g + 1], (1, N_HEADS)) for g in range(3)]

    h0, h0t = _normmod_fwd(x, g0, scale0, shift0, "prenorm0")
    nsa = own_wa_in.shape[2]
    tiles_a = dict(tn=nsa, total_tiles=N_CHIPS, part_of=lambda tile: 0)
    own_ids, rest_ids, own_tiles = _own_first(chip, N_CHIPS)
    proj_a = _in_tiles([h0], own_wa_in, own_ids, own_tiles, name="a_in_own", **tiles_a)
    forward_weights("a_in", proj_a)
    wa_in = weights("a_in", proj_a)
    ja = wa_in.shape[0]
    proj_a = _in_tiles([h0], wa_in, rest_ids, N_CHIPS - own_tiles, name="a_in_rest", prev=proj_a, **tiles_a)
    forward_weights("a_out", proj_a)
    u5, u5t, u2 = _conv_fwd(proj_a, conv_w, conv_b, ln_g, ln_b, "a_conv")
    wa_out = weights("a_out", u5)
    x1, y_a, h1t, h1c = _out_a(u5, wa_out, x, gate0, g1, scale1, shift1, "a_out")

    tiles_b = dict(tn=B_TN, total_tiles=B_TILES,
                   part_of=lambda tile: jnp.where(tile >= B_Z_TILE0, 0, tile // B_GROUP_TILES))
    own_ids, rest_ids, own_tiles = _own_first(chip, B_TILES)
    proj_b = _in_tiles(h1c, own_wb_in, own_ids, own_tiles, name="b_in_own", **tiles_b)
    forward_weights("b_in", proj_b)
    wb_in = weights("b_in", proj_b)
    jb, _, nsb = wb_in.shape
    proj_b = _in_tiles(h1c, wb_in, rest_ids, B_TILES - own_tiles, name="b_in_rest", prev=proj_b, **tiles_b)
    forward_weights("b_out", proj_b)
    h1 = h1c[0]
    qkv, o_parts, lse_parts = [], [], []
    for g, d in enumerate(DILATIONS):
        qn, kn = _qknorm_fwd(proj_b, g, qw[g], kw[g], seg, f"b_qknorm_g{g}")
        og, lg = _attn_fwd(qn, kn, proj_b, g, slopes, d, f"b_attn_g{g}")
        qkv.append((qn, kn))
        o_parts.append(og if d == 1 else og.reshape(d, SEQ // d, D_MODEL))
        lse_parts.append(lg if d == 1 else lg.reshape(d, SEQ // d, LANES))
    sel = _head_selector()
    u_b, u_bt, o_b, lse_b = _merge_fwd(o_parts, lse_parts, proj_b, sel, "b_merge")
    wb_out = weights("b_out", u_b)
    e, dy_b, sums_loss = _out_b_loss(u_b, wb_out, x1, gate1, target, "b_out_loss")

    dwb_out = _mm(u_bt, dy_b, tn=D_MODEL, tile0=0, n_tiles=1, out_dtype=BF16, name="b_dwout")
    dz_b, do_c, delta_c, lse_c = _merge_bwd(dy_b, wb_out, o_b, lse_b, proj_b, sel, "b_merge_bwd")
    dwb_in = _mm(h1t, dz_b, tn=B_TN, tile0=B_Z_TILE0, n_tiles=B_Z_TILES, out_dtype=BF16, name="b_dwin_z",
                 out3d=(jb, nsb))
    dh1_parts = [_mm_nt(dz_b, wb_in, tn=B_TN, tile0=B_Z_TILE0, n_tiles=B_Z_TILES, name="b_dh_z")]
    qk_sums = []
    for g, d in enumerate(DILATIONS):
        qn, kn = qkv[g]
        dq, dk, dv = _attn_bwd(qn, kn, proj_b, g, do_c[g], lse_c[g], delta_c[g], slopes, d, f"b_attn_bwd_g{g}")
        dproj, sums_qk = _qknorm_bwd(proj_b, g, qw[g], kw[g], seg, dq, dk, dv, f"b_qknorm_bwd_g{g}")
        qk_sums.append(sums_qk)
        dwb_in = _mm(h1t if d == 1 else h1c[g], dproj, tn=B_TN, tile0=g * B_GROUP_TILES, n_tiles=B_GROUP_TILES,
                     out_dtype=BF16, name=f"b_dwin_g{g}", out3d=(jb, nsb), prev=dwb_in, transpose_lhs=d != 1)
        dh = _mm_nt(dproj, wb_in, tn=B_TN, tile0=g * B_GROUP_TILES, n_tiles=B_GROUP_TILES, name=f"b_dh_g{g}")
        dh1_parts.append(dh)
    token = send_grads_b(dwb_in, dwb_out)
    dx1, sums_n1, dy_a = _normmod_bwd(x1, g1, scale1 + token[0:1, 0:1], dh1_parts, e, "prenorm1_bwd",
                                      part_dilations=(1,) + DILATIONS, gated=(gate0, y_a))
    token = forward_grads_b(dx1)

    dwa_out = _mm(u5t, dy_a, tn=D_MODEL, tile0=0, n_tiles=1, out_dtype=BF16, name="a_dwout")
    du2, dz_a, sums_ln = _conv_bwd_pointwise(dy_a, wa_out, proj_a, u2, ln_g + token[0:1, 0:1], ln_b,
                                             "a_conv_bwd_pw")
    dproj_a, dconv_w = _conv_bwd_taps(du2, dz_a, proj_a, conv_w, "a_conv_bwd_taps")
    dwa_in = _mm(h0t, dproj_a, tn=nsa, tile0=0, n_tiles=ja, out_dtype=BF16, name="a_dwin", out3d=(ja, nsa))
    token = send_grads_a(dwa_in, dwa_out)
    dh0 = _mm_nt(dproj_a, wa_in, tn=nsa, tile0=0, n_tiles=ja, name="a_dh", after=token)
    grad_x, sums_n0 = _normmod_bwd(x, g0, scale0, [dh0], dx1, "prenorm0_bwd")

    small = dict(
        dnorm_g=jnp.concatenate([sums_n0[0:1], sums_n1[0:1]], axis=0),
        dmod0=jnp.concatenate([sums_n0[2:3], sums_n0[1:2], sums_n1[3:4]], axis=0),
        dmod1=jnp.concatenate([sums_n1[2:3], sums_n1[1:2], sums_loss[0:1]], axis=0),
        dln_g=sums_ln[0:1], dln_b=sums_ln[1:2], dconv_b=sums_ln[2:3],
        dconv_w=dconv_w[:CONV_WIDTH],
        dq_norm=jnp.concatenate([s[0:1] for s in qk_sums], axis=0),
        dk_norm=jnp.concatenate([s[1:2] for s in qk_sums], axis=0),
        loss_cols=sums_loss[1:2],
    )
    return grad_x, small


def _adamw(w, g, m, v, name, after=None, copy_grad=False):
    rows, cols = w.shape
    tr = rows if rows <= 128 else (256 if cols <= D_MODEL else 128)
    c1 = 1.0 / (1.0 - ADAM_B1 ** ADAM_STEP)
    c2 = 1.0 / (1.0 - ADAM_B2 ** ADAM_STEP)
    extra = [] if after is None else [after]
    n_out = 4 if copy_grad else 3

    def body(w_ref, g_ref, m_ref, v_ref, *rest):
        d_ref, mo_ref, vo_ref = rest[len(extra):len(extra) + 3]
        gv = g_ref[...]
        if copy_grad:
            rest[-1][...] = gv
        mn = ADAM_B1 * m_ref[...] + (1.0 - ADAM_B1) * gv
        vn = ADAM_B2 * v_ref[...] + (1.0 - ADAM_B2) * (gv * gv)
        mo_ref[...] = mn
        vo_ref[...] = vn
        d_ref[...] = -ADAM_LR * ((mn * c1) / (jnp.sqrt(vn * c2) + ADAM_EPS) + ADAM_WD * w_ref[...])

    spec = pl.BlockSpec((tr, cols), lambda i: (i, 0))
    return pl.pallas_call(
        body, name=name, grid=(rows // tr,),
        in_specs=[spec] * 4 + [pl.BlockSpec(memory_space=pl.ANY)] * len(extra), out_specs=[spec] * n_out,
        out_shape=[jax.ShapeDtypeStruct((rows, cols), F32)] * n_out,
        compiler_params=_params("parallel"),
    )(w, g, m, v, *extra)


def _cast_into_slot(w, chip_idx, name, keep_own=False, after=None):
    rows, cols = w.shape
    tr = 256
    extra = [] if after is None else [after]

    def body(ch_ref, w_ref, *rest):
        wb = w_ref[...].astype(BF16)
        for o_ref in rest[len(extra):]:
            o_ref[...] = wb

    slot_spec = pl.BlockSpec((None, tr, cols), lambda i, ch: (ch[0], i, 0))
    own_spec = pl.BlockSpec((None, tr, cols), lambda i, ch: (0, i, 0))
    res = pl.pallas_call(
        body, name=name,
        grid_spec=pltpu.PrefetchScalarGridSpec(
            num_scalar_prefetch=1, grid=(rows // tr,),
            in_specs=[pl.BlockSpec((tr, cols), lambda i, ch: (i, 0))] + [pl.BlockSpec(memory_space=pl.ANY)] * len(extra),
            out_specs=[slot_spec, own_spec] if keep_own else [slot_spec]),
        out_shape=[jax.ShapeDtypeStruct((N_CHIPS, rows, cols), BF16)]
        + ([jax.ShapeDtypeStruct((1, rows, cols), BF16)] if keep_own else []),
        compiler_params=_params("parallel"),
    )(chip_idx, w, *extra)
    return tuple(res) if keep_own else res[0]


def _position():
    x, y, c = lax.axis_index("x"), lax.axis_index("y"), lax.axis_index("c")
    return x, y, c


def _xor_peer(x, y, c, k):
    return (x ^ ((k >> 2) & 1), y ^ ((k >> 1) & 1), c ^ (k & 1))


def _chip_peer(x, y, k):
    return (x ^ ((k >> 1) & 1), y ^ (k & 1))


def _ada_forward(c_row, ada_w, ada_b, conv_w, after=()):
    ns = ada_w.shape[2]
    cw = conv_w.shape[1]

    def body(c_ref, w_ref, b_ref, cv_ref, *rest):
        (mod_ref, sc_ref, cvo_ref, c_all, mp, parts, cv_parts,
         send1, recv1, send2, recv2, send3, recv3) = rest[len(after):]
        x, y, c = _position()
        me = 4 * x + 2 * y + c
        chip = 2 * x + y

        def c_copy(k):
            return pltpu.make_async_remote_copy(
                src_ref=c_all.at[me], dst_ref=c_all.at[me], send_sem=send1.at[k - 1], recv_sem=recv1.at[k - 1],
                device_id=_xor_peer(x, y, c, k), device_id_type=MESH)

        def cv_copy(k):
            px, py = _chip_peer(x, y, k)
            return pltpu.make_async_remote_copy(
                src_ref=cv_parts.at[chip], dst_ref=cv_parts.at[chip], send_sem=send3.at[k - 1],
                recv_sem=recv3.at[k - 1], device_id=(px, py, c), device_id_type=MESH)

        c_all[me] = c_ref[...]
        cv_parts[chip] = cv_ref[...]
        for k in range(1, N_DEV):
            c_copy(k).start()
        for k in range(1, N_CHIPS):
            cv_copy(k).start()
        for k in range(1, N_DEV):
            c_copy(k).wait_recv()
        cv = jnp.concatenate([c_all[i] for i in range(N_DEV)], axis=0)
        sc = cv * _sigmoid(cv)
        sc_ref[...] = sc
        for l in range(2):
            res = jnp.dot(sc, w_ref[l], preferred_element_type=F32, precision=lax.Precision.HIGHEST)
            for i in range(N_DEV):
                mp[i, l:l + 1, :] = res[i:i + 1, :]

        def mod_copy(k):
            px, py = _chip_peer(x, y, k)
            return pltpu.make_async_remote_copy(
                src_ref=mp.at[4 * px + 2 * py + c], dst_ref=parts.at[chip], send_sem=send2.at[k - 1],
                recv_sem=recv2.at[k - 1], device_id=(px, py, c), device_id_type=MESH)

        for k in range(1, N_CHIPS):
            mod_copy(k).start()
        parts[chip] = mp[me]
        for k in range(1, N_CHIPS):
            mod_copy(k).wait_recv()
            cv_copy(k).wait_recv()
        mod_ref[...] = jnp.concatenate([parts[j] for j in range(N_CHIPS)], axis=1) + b_ref[...]
        cvo_ref[...] = jnp.concatenate([cv_parts[j] for j in range(N_CHIPS)], axis=1)
        for k in range(1, N_DEV):
            c_copy(k).wait_send()
        for k in range(1, N_CHIPS):
            mod_copy(k).wait_send()
            cv_copy(k).wait_send()

    vm = pl.BlockSpec(memory_space=pltpu.VMEM)
    return pl.pallas_call(
        body, name="ada_forward",
        in_specs=[vm] * 4 + [pl.BlockSpec(memory_space=pl.ANY)] * len(after), out_specs=[vm] * 3,
        out_shape=[jax.ShapeDtypeStruct((2, 3 * D_MODEL), F32), jax.ShapeDtypeStruct((N_DEV, D_MODEL), F32),
                   jax.ShapeDtypeStruct((CONV_WIDTH, N_CHIPS * cw), F32)],
        scratch_shapes=[pltpu.VMEM((N_DEV, 1, D_MODEL), F32), pltpu.VMEM((N_DEV, 2, ns), F32),
                        pltpu.VMEM((N_CHIPS, 2, ns), F32), pltpu.VMEM((N_CHIPS, CONV_WIDTH, cw), F32),
                        pltpu.SemaphoreType.DMA((N_DEV - 1,)), pltpu.SemaphoreType.DMA((N_DEV - 1,)),
                        pltpu.SemaphoreType.DMA((N_CHIPS - 1,)), pltpu.SemaphoreType.DMA((N_CHIPS - 1,)),
                        pltpu.SemaphoreType.DMA((N_CHIPS - 1,)), pltpu.SemaphoreType.DMA((N_CHIPS - 1,))],
        compiler_params=pltpu.CompilerParams(vmem_limit_bytes=VMEM_LIMIT_BYTES),
    )(c_row, ada_w, ada_b, conv_w, *after)


HBM_SPEC = pl.BlockSpec(memory_space=pltpu.HBM)
ANY_SPEC = pl.BlockSpec(memory_space=pl.ANY)
SEM_SPEC = pl.BlockSpec(memory_space=pltpu.SEMAPHORE)
SPLIT_PARAMS = dict(compiler_params=pltpu.CompilerParams(has_side_effects=pltpu.SideEffectType.DATAFLOW_SIDE_EFFECTING))
TOKEN = jax.ShapeDtypeStruct((8, 128), F32)
ENTRY_HANDSHAKES = {name: (i, peers) for i, (name, peers) in enumerate((
    ("gather_start_a", "chips"), ("gather_start_b", "chips"),
    ("gather_forward_a_in", "sibling"), ("gather_forward_a_out", "sibling"),
    ("gather_forward_b_in", "sibling"), ("gather_forward_b_out", "sibling"),
    ("reduce_d2d_start_b", "sibling"), ("reduce_d2d_start_a", "sibling"),
    ("reduce_ici_start_b", "chips"), ("reduce_ici_start_a", "chips"),
    ("reduce_share_start_b", "sibling"), ("reduce_share_start_a", "sibling"),
    ("small_gather_start", "devices")))}


def _hbm(arrays):
    return [pltpu.with_memory_space_constraint(a, pltpu.HBM) for a in arrays]


def _hbm_like(arrays):
    return [pltpu.HBM(a.shape, a.dtype) for a in arrays]


def _gather_start(lands, after, name):
    n = len(lands)

    def body(*refs):
        _handshake(ENTRY_HANDSHAKES[name][1])
        ins = refs[:n]
        send, recv = refs[n + 1], refs[n + 2]
        x, y, c = _position()
        chip = 2 * x + y
        for t in range(n):
            rh = ins[t].shape[1] // 2
            for k in range(1, N_CHIPS):
                px, py = _chip_peer(x, y, k)
                block = ins[t].at[chip, pl.ds(c * rh, rh)]
                pltpu.make_async_remote_copy(
                    src_ref=block, dst_ref=block, send_sem=send.at[3 * t + k - 1], recv_sem=recv.at[3 * t + k - 1],
                    device_id=(px, py, c), device_id_type=MESH).start()
        refs[-1][...] = jnp.zeros(TOKEN.shape, F32)

    res = pl.pallas_call(
        body, name=name, in_specs=[HBM_SPEC] * n + [ANY_SPEC],
        out_specs=(SEM_SPEC, SEM_SPEC, *[HBM_SPEC] * n, pl.BlockSpec(memory_space=pltpu.VMEM)),
        out_shape=(pltpu.SemaphoreType.DMA((3 * n,)), pltpu.SemaphoreType.DMA((3 * n,)), *_hbm_like(lands), TOKEN),
        input_output_aliases={t: 2 + t for t in range(n)}, **_split_params(name),
    )(*_hbm(lands), after)
    return res[0], res[1], list(res[2:2 + n]), res[-1]


def _gather_forward(send, recv, lands, first, after, name):
    n = len(lands)

    def body(*refs):
        _handshake(ENTRY_HANDSHAKES[name][1])
        ins = refs[:n]
        send1, recv1 = refs[n], refs[n + 1]
        send2, recv2 = refs[n + 3], refs[n + 4]
        x, y, c = _position()
        chip = 2 * x + y
        for t in range(n):
            rh = ins[t].shape[1] // 2
            half = pl.ds(c * rh, rh)
            for k in range(1, N_CHIPS):
                px, py = _chip_peer(x, y, k)
                s = 3 * t + k - 1
                got = ins[t].at[2 * px + py, half]
                cp = pltpu.make_async_remote_copy(
                    src_ref=ins[t].at[chip, half], dst_ref=got, send_sem=send1.at[3 * first + s],
                    recv_sem=recv1.at[3 * first + s], device_id=(px, py, c), device_id_type=MESH)
                cp.wait_send()
                cp.wait_recv()
                pltpu.make_async_remote_copy(
                    src_ref=got, dst_ref=got, send_sem=send2.at[s], recv_sem=recv2.at[s],
                    device_id=(x, y, 1 - c), device_id_type=MESH).start()
        refs[-1][...] = jnp.zeros(TOKEN.shape, F32)

    res = pl.pallas_call(
        body, name=name, in_specs=[HBM_SPEC] * n + [SEM_SPEC, SEM_SPEC, ANY_SPEC],
        out_specs=(SEM_SPEC, SEM_SPEC, *[HBM_SPEC] * n, pl.BlockSpec(memory_space=pltpu.VMEM)),
        out_shape=(pltpu.SemaphoreType.DMA((3 * n,)), pltpu.SemaphoreType.DMA((3 * n,)), *_hbm_like(lands), TOKEN),
        input_output_aliases={t: 2 + t for t in range(n)}, **_split_params(name),
    )(*lands, send, recv, after)
    return res[0], res[1], list(res[2:2 + n]), res[-1]


def _gather_wait(send, recv, lands, after, name):
    n = len(lands)

    def body(*refs):
        ins = refs[:n]
        send_ref, recv_ref = refs[n], refs[n + 1]
        x, y, c = _position()
        for t in range(n):
            rh = ins[t].shape[1] // 2
            for k in range(1, N_CHIPS):
                px, py = _chip_peer(x, y, k)
                cp = pltpu.make_async_remote_copy(
                    src_ref=ins[t].at[2 * px + py, pl.ds(c * rh, rh)],
                    dst_ref=ins[t].at[2 * px + py, pl.ds((1 - c) * rh, rh)], send_sem=send_ref.at[3 * t + k - 1],
                    recv_sem=recv_ref.at[3 * t + k - 1], device_id=(x, y, 1 - c), device_id_type=MESH)
                cp.wait_send()
                cp.wait_recv()

    res = pl.pallas_call(
        body, name=name, in_specs=[HBM_SPEC] * n + [SEM_SPEC, SEM_SPEC, ANY_SPEC], out_specs=[HBM_SPEC] * n,
        out_shape=_hbm_like(lands), input_output_aliases={t: t for t in range(n)}, **SPLIT_PARAMS,
    )(*lands, send, recv, after)
    return list(res)


def _handshake(peers):
    x, y, c = _position()
    if peers == "sibling":
        ids = [(x, y, 1 - c)]
    elif peers == "chips":
        ids = [(*_chip_peer(x, y, k), c) for k in range(1, N_CHIPS)]
    else:
        ids = [_xor_peer(x, y, c, k) for k in range(1, N_DEV)]
    barrier = pltpu.get_barrier_semaphore()
    for peer in ids:
        pl.semaphore_signal(barrier, inc=1, device_id=peer, device_id_type=MESH)
    pl.semaphore_wait(barrier, len(ids))


def _split_params(name):
    return dict(compiler_params=pltpu.CompilerParams(
        has_side_effects=pltpu.SideEffectType.DATAFLOW_SIDE_EFFECTING, collective_id=ENTRY_HANDSHAKES[name][0]))


def _split_start(name, arrays, n_sems, after, issue):
    m = len(arrays)

    def body(*refs):
        _handshake(ENTRY_HANDSHAKES[name][1])
        issue(refs[:m], refs[m + 1], refs[m + 2])
        refs[-1][...] = jnp.zeros(TOKEN.shape, F32)

    res = pl.pallas_call(
        body, name=name, in_specs=[HBM_SPEC] * m + [ANY_SPEC],
        out_specs=(SEM_SPEC, SEM_SPEC, *[HBM_SPEC] * m, pl.BlockSpec(memory_space=pltpu.VMEM)),
        out_shape=(pltpu.SemaphoreType.DMA((n_sems,)), pltpu.SemaphoreType.DMA((n_sems,)), *_hbm_like(arrays), TOKEN),
        input_output_aliases={t: 2 + t for t in range(m)}, **_split_params(name),
    )(*_hbm(arrays), after)
    return res[0], res[1], list(res[2:2 + m]), res[-1]


def _split_wait(name, arrays, send, recv, after, await_all):
    m = len(arrays)

    def body(*refs):
        await_all(refs[:m], refs[m], refs[m + 1])

    res = pl.pallas_call(
        body, name=name, in_specs=[HBM_SPEC] * m + [SEM_SPEC, SEM_SPEC, ANY_SPEC], out_specs=[HBM_SPEC] * m,
        out_shape=_hbm_like(arrays), input_output_aliases={t: t for t in range(m)}, **SPLIT_PARAMS,
    )(*arrays, send, recv, after)
    return list(res)


def _sibling_copies(refs, send, recv, n):
    x, y, c = _position()
    cps = []
    for t in range(n):
        rh = refs[t].shape[1] // 2
        cps.append(pltpu.make_async_remote_copy(
            src_ref=refs[t].at[pl.ds(0, N_CHIPS), pl.ds((1 - c) * rh, rh)], dst_ref=refs[n + t],
            send_sem=send.at[t], recv_sem=recv.at[t], device_id=(x, y, 1 - c), device_id_type=MESH))
    return cps


def _reduce_sibling_start(grads, after, name):
    n = len(grads)
    lands = [lax.empty((N_CHIPS, g.shape[1] // 2, g.shape[2]), BF16) for g in grads]

    def issue(refs, send, recv):
        for cp in _sibling_copies(refs, send, recv, n):
            cp.start()

    return _split_start(name, list(grads) + lands, n, after, issue)


def _reduce_sibling_wait(send, recv, arrays, after, name):
    n = len(arrays) // 2

    def await_all(refs, send_ref, recv_ref):
        for cp in _sibling_copies(refs, send_ref, recv_ref, n):
            cp.wait_send()
            cp.wait_recv()

    res = _split_wait(name, arrays, send, recv, after, await_all)
    return res[:n], res[n:]


def _add_sibling_half(grad, got, dev_idx, name):
    j, r, cols = grad.shape
    rh = r // 2
    tr = rh
    nb = rh // tr

    def body(idx_ref, g_ref, got_ref, out_ref):
        out_ref[...] = (g_ref[...].astype(F32) + got_ref[...].astype(F32)).astype(BF16)

    return pl.pallas_call(
        body, name=name,
        grid_spec=pltpu.PrefetchScalarGridSpec(
            num_scalar_prefetch=1, grid=(j, nb),
            in_specs=[pl.BlockSpec((None, tr, cols), lambda jj, i, idx: (jj, idx[2] * nb + i, 0)),
                      pl.BlockSpec((None, tr, cols), lambda jj, i, idx: (jj, i, 0))],
            out_specs=pl.BlockSpec((None, tr, cols), lambda jj, i, idx: (jj, i, 0))),
        out_shape=jax.ShapeDtypeStruct((j, rh, cols), BF16),
        compiler_params=_params("parallel", "parallel"),
    )(dev_idx, grad, got)


def _chip_copies(refs, send, recv, n, receiving):
    x, y, c = _position()
    chip = 2 * x + y
    cps = []
    for t in range(n):
        for k in range(1, N_CHIPS):
            px, py = _chip_peer(x, y, k)
            cps.append(pltpu.make_async_remote_copy(
                src_ref=refs[t].at[2 * px + py], dst_ref=refs[n + t].at[2 * px + py if receiving else chip],
                send_sem=send.at[3 * t + k - 1], recv_sem=recv.at[3 * t + k - 1],
                device_id=(px, py, c), device_id_type=MESH))
    return cps


def _reduce_chips_start(partials, after, name):
    n = len(partials)
    lands = [lax.empty(p.shape, BF16) for p in partials]

    def issue(refs, send, recv):
        for cp in _chip_copies(refs, send, recv, n, False):
            cp.start()

    return _split_start(name, list(partials) + lands, 3 * n, after, issue)


def _reduce_chips_wait(send, recv, arrays, after, name):
    n = len(arrays) // 2

    def await_all(refs, send_ref, recv_ref):
        for cp in _chip_copies(refs, send_ref, recv_ref, n, True):
            cp.wait_send()
            cp.wait_recv()

    res = _split_wait(name, arrays, send, recv, after, await_all)
    return res[:n], res[n:]


def _sum_partials(land, partial, dev_idx, name):
    _, rh, cols = land.shape
    tr = min(rh, 256)
    nb = rh // tr

    def body(idx_ref, l_ref, p_ref, o_ref):
        chip = idx_ref[1]
        acc = jnp.where(chip == 0, p_ref[...], l_ref[0]).astype(F32)
        for s in range(1, N_CHIPS):
            acc = acc + jnp.where(chip == s, p_ref[...], l_ref[s]).astype(F32)
        o_ref[...] = acc

    return pl.pallas_call(
        body, name=name,
        grid_spec=pltpu.PrefetchScalarGridSpec(
            num_scalar_prefetch=1, grid=(nb,),
            in_specs=[pl.BlockSpec((N_CHIPS, tr, cols), lambda i, idx: (0, i, 0)),
                      pl.BlockSpec((None, tr, cols), lambda i, idx: (idx[1], i, 0))],
            out_specs=pl.BlockSpec((tr, cols), lambda i, idx: (idx[2] * nb + i, 0))),
        out_shape=jax.ShapeDtypeStruct((2 * rh, cols), F32), compiler_params=_params("parallel"),
    )(dev_idx, land, partial)


def _half_copies(refs, send, recv, receiving):
    x, y, c = _position()
    cps = []
    for t, ref in enumerate(refs):
        rh = ref.shape[0] // 2
        cps.append(pltpu.make_async_remote_copy(
            src_ref=ref.at[pl.ds(c * rh, rh)], dst_ref=ref.at[pl.ds(((1 - c) if receiving else c) * rh, rh)],
            send_sem=send.at[t], recv_sem=recv.at[t], device_id=(x, y, 1 - c), device_id_type=MESH))
    return cps


def _share_halves_start(totals, after, name):
    def issue(refs, send, recv):
        for cp in _half_copies(refs, send, recv, False):
            cp.start()

    return _split_start(name, list(totals), len(totals), after, issue)


def _share_halves_wait(send, recv, totals, after, name):
    def await_all(refs, send_ref, recv_ref):
        for cp in _half_copies(refs, send_ref, recv_ref, True):
            cp.wait_send()
            cp.wait_recv()

    return _split_wait(name, totals, send, recv, after, await_all)


SMALL_ROWS = 56


def _small_copies(refs, send, recv, receiving):
    x, y, c = _position()
    me = 4 * x + 2 * y + c
    cps = []
    for k in range(1, N_DEV):
        px, py, pc = _xor_peer(x, y, c, k)
        cps.append(pltpu.make_async_remote_copy(
            src_ref=refs[0], dst_ref=refs[1].at[4 * px + 2 * py + pc if receiving else me],
            send_sem=send.at[k - 1], recv_sem=recv.at[k - 1], device_id=(px, py, pc), device_id_type=MESH))
    return cps


def _small_gather_start(packed, after):
    land = lax.empty((N_DEV,) + packed.shape, F32)

    def issue(refs, send, recv):
        for cp in _small_copies(refs, send, recv, False):
            cp.start()

    return _split_start("small_gather_start", [packed, land], N_DEV - 1, after, issue)


def _small_gather_wait(send, recv, arrays, after):
    def await_all(refs, send_ref, recv_ref):
        for cp in _small_copies(refs, send_ref, recv_ref, True):
            cp.wait_send()
            cp.wait_recv()

    return _split_wait("small_gather_wait", arrays, send, recv, after, await_all)


def _reduce_small(packed, land, silu_c):
    ns = 3 * D_MODEL // N_CHIPS

    def body(p_ref, land_ref, sc_ref, tot_ref, gw_ref, loss_ref, qk_ref, allp):
        x, y, c = _position()
        me = 4 * x + 2 * y + c
        chip = 2 * x + y
        for i in range(N_DEV):
            allp[i] = jnp.where(me == i, p_ref[...], land_ref[i])
        tot = allp[0]
        for i in range(1, N_DEV):
            tot = tot + allp[i]
        tot_ref[...] = tot
        loss_ref[...] = jnp.sum(tot[11:12, :], axis=1, keepdims=True) * (0.5 / D_MODEL)
        fold = tot[5:11, 0:HEAD_DIM]
        for h in range(1, N_HEADS):
            fold = fold + tot[5:11, h * HEAD_DIM:(h + 1) * HEAD_DIM]
        qk_ref[...] = jnp.concatenate([fold, jnp.zeros((2, HEAD_DIM), F32)], axis=0)
        sct = sc_ref[...].T
        rc = 64
        for l in range(2):
            dms = [allp[i, pl.ds(12 + 4 * l + chip, 1), :][:, :ns] for i in range(N_DEV)]
            for r0 in range(0, D_MODEL, rc):
                acc = sct[r0:r0 + rc, 0:1] * dms[0]
                for i in range(1, N_DEV):
                    acc = acc + sct[r0:r0 + rc, i:i + 1] * dms[i]
                gw_ref[l, r0:r0 + rc, :] = acc

    vm = pl.BlockSpec(memory_space=pltpu.VMEM)
    return pl.pallas_call(
        body, name="reduce_small", in_specs=[vm, vm, vm], out_specs=[vm] * 4,
        out_shape=[jax.ShapeDtypeStruct((SMALL_ROWS, D_MODEL), F32), jax.ShapeDtypeStruct((2, D_MODEL, ns), F32),
                   jax.ShapeDtypeStruct((1, 1), F32), jax.ShapeDtypeStruct((8, HEAD_DIM), F32)],
        scratch_shapes=[pltpu.VMEM((N_DEV, SMALL_ROWS, D_MODEL), F32)],
        compiler_params=pltpu.CompilerParams(vmem_limit_bytes=VMEM_LIMIT_BYTES),
    )(packed, land, silu_c)


def kernel(x, c, norm_g, ada_w, ada_b, a_w_in, a_conv_w, a_conv_b, a_ln_g, a_ln_b, a_w_out, b_w_in, b_q_norm, b_k_norm, b_w_out, loss_target, m_norm_g, m_ada_w, m_ada_b, m_a_w_in, m_a_conv_w, m_a_conv_b, m_a_ln_g, m_a_ln_b, m_a_w_out, m_b_w_in, m_b_q_norm, m_b_k_norm, m_b_w_out, v_norm_g, v_ada_w, v_ada_b, v_a_w_in, v_a_conv_w, v_a_conv_b, v_a_ln_g, v_a_ln_b, v_a_w_out, v_b_w_in, v_b_q_norm, v_b_k_norm, v_b_w_out):
    chip = 2 * lax.axis_index("x") + lax.axis_index("y")
    core = lax.axis_index("c")
    chip_idx = chip.astype(jnp.int32).reshape(1)
    dev_idx = jnp.stack([2 * chip + core, chip, core]).astype(jnp.int32)

    land_a_in, own_wa_in = _cast_into_slot(a_w_in[0], chip_idx, "cast_a_w_in", keep_own=True)
    lands_a = [land_a_in, _cast_into_slot(a_w_out[0], chip_idx, "cast_a_w_out")]
    mods, silu_c, conv_w_full = _ada_forward(c, ada_w, ada_b, a_conv_w[0], after=tuple(lands_a))
    send_a, recv_a, lands_a, token_a = _gather_start(lands_a, mods, "gather_start_a")
    land_b_in, own_wb_in = _cast_into_slot(b_w_in[0], chip_idx, "cast_b_w_in", keep_own=True, after=token_a)
    lands_b = [land_b_in, _cast_into_slot(b_w_out[0], chip_idx, "cast_b_w_out", after=token_a)]
    send_b, recv_b, lands_b, token_b = _gather_start(lands_b, token_a, "gather_start_b")
    mods = mods + token_b[0:2, 0:1]

    started = {"a_in": (send_a, recv_a, lands_a, 0), "a_out": (send_a, recv_a, lands_a, 1),
               "b_in": (send_b, recv_b, lands_b, 0), "b_out": (send_b, recv_b, lands_b, 1)}
    forwarded = {}

    def forward_weights(which, after):
        send, recv, lands, t = started[which]
        forwarded[which] = _gather_forward(send, recv, [lands[t]], t, after, f"gather_forward_{which}")

    def weights(which, after):
        send, recv, lands, _ = forwarded[which]
        w, = _gather_wait(send, recv, lands, after, f"gather_wait_{which}")
        return w if which.endswith("_in") else w.reshape(D_MODEL, D_MODEL)

    stage1, stage2 = {}, {}

    def send_grads(tag, dw_in, dw_out):
        grads = [dw_in, dw_out.reshape(N_CHIPS, D_MODEL // N_CHIPS, D_MODEL)]
        send, recv, arrays, token = _reduce_sibling_start(grads, dw_out, f"reduce_d2d_start_{tag}")
        stage1[tag] = (send, recv, arrays)
        return token

    def forward_grads(tag, after):
        send, recv, arrays = stage1[tag]
        grads, got = _reduce_sibling_wait(send, recv, arrays, after, f"reduce_d2d_wait_{tag}")
        partials = [_add_sibling_half(grads[i], got[i], dev_idx, f"reduce_add_{tag}_{i}") for i in range(2)]
        send, recv, arrays, token = _reduce_chips_start(partials, partials[1], f"reduce_ici_start_{tag}")
        stage2[tag] = (send, recv, arrays)
        return token

    stage3 = {}

    def sum_grads(tag, after):
        send, recv, arrays = stage2[tag]
        partials, lands = _reduce_chips_wait(send, recv, arrays, after, f"reduce_ici_wait_{tag}")
        totals = [_sum_partials(lands[i], partials[i], dev_idx, f"reduce_sum_{tag}_{i}") for i in range(2)]
        send, recv, totals, token = _share_halves_start(totals, totals[1], f"reduce_share_start_{tag}")
        stage3[tag] = (send, recv, totals)
        return token

    def finish_grads(tag, after):
        send, recv, totals = stage3[tag]
        return _share_halves_wait(send, recv, totals, after, f"reduce_share_wait_{tag}")

    grad_x, small = _local_step(
        x[0], loss_target[0], mods.reshape(2, 3, D_MODEL), norm_g, conv_w_full, a_conv_b, a_ln_g[0:1],
        a_ln_b[0:1], b_q_norm[0], b_k_norm[0], chip.astype(jnp.int32), own_wa_in, own_wb_in,
        forward_weights, weights,
        functools.partial(send_grads, "b"), functools.partial(forward_grads, "b"), functools.partial(send_grads, "a"))

    ns = 3 * D_MODEL // N_CHIPS
    pad_mod = lambda dm: jnp.pad(dm.reshape(N_CHIPS, ns), ((0, 0), (0, D_MODEL - ns)))
    packed = jnp.concatenate([
        small["dnorm_g"], small["dconv_b"], small["dln_g"], small["dln_b"], small["dq_norm"], small["dk_norm"],
        small["loss_cols"], pad_mod(small["dmod0"]), pad_mod(small["dmod1"]), small["dconv_w"],
        jnp.zeros((SMALL_ROWS - 20 - CONV_WIDTH, D_MODEL), F32)], axis=0)
    send_s, recv_s, small_arrays, token_s = _small_gather_start(packed, packed)

    given = dict(norm_g=(norm_g, m_norm_g, v_norm_g), ada_w=(ada_w, m_ada_w, v_ada_w), ada_b=(ada_b, m_ada_b, v_ada_b),
                 a_w_in=(a_w_in, m_a_w_in, v_a_w_in), a_conv_w=(a_conv_w, m_a_conv_w, v_a_conv_w),
                 a_conv_b=(a_conv_b, m_a_conv_b, v_a_conv_b), a_ln_g=(a_ln_g, m_a_ln_g, v_a_ln_g),
                 a_ln_b=(a_ln_b, m_a_ln_b, v_a_ln_b), a_w_out=(a_w_out, m_a_w_out, v_a_w_out),
                 b_w_in=(b_w_in, m_b_w_in, v_b_w_in), b_q_norm=(b_q_norm, m_b_q_norm, v_b_q_norm),
                 b_k_norm=(b_k_norm, m_b_k_norm, v_b_k_norm), b_w_out=(b_w_out, m_b_w_out, v_b_w_out))
    order = ["norm_g", "ada_w", "ada_b", "a_w_in", "a_conv_w", "a_conv_b", "a_ln_g", "a_ln_b", "a_w_out", "b_w_in",
             "b_q_norm", "b_k_norm", "b_w_out"]
    outs = {}

    def update(k, g2, after=None, copy_grad=False):
        w, m, v = given[k]
        shape2 = g2.shape
        res = _adamw(w.reshape(shape2), g2, m.reshape(shape2), v.reshape(shape2), f"adamw_{k}", after, copy_grad)
        outs[k] = tuple(a.reshape(w.shape) for a in ((res[3] if copy_grad else g2), res[0], res[1], res[2]))

    token = forward_grads("a", token_s)
    token = sum_grads("b", token)
    packed, land = _small_gather_wait(send_s, recv_s, small_arrays, token)
    tot, g_ada_w, loss, qk = _reduce_small(packed, land, silu_c)
    g_b_in, g_b_out = finish_grads("b", tot)
    update("b_w_in", g_b_in, copy_grad=True)
    update("b_w_out", g_b_out, copy_grad=True)
    token = sum_grads("a", outs["b_w_in"][1])
    cw = D_MODEL // N_CHIPS
    g_small = dict(
        norm_g=tot[0:2], a_conv_b=tot[2:3], a_ln_g=tot[3:4], a_ln_b=tot[4:5],
        b_q_norm=qk[0:3], b_k_norm=qk[3:6],
        ada_b=jnp.stack([tot[12:16, :ns].reshape(3 * D_MODEL), tot[16:20, :ns].reshape(3 * D_MODEL)]),
        a_conv_w=lax.dynamic_slice(tot[20:20 + CONV_WIDTH], (0, chip * cw), (CONV_WIDTH, cw)),
    )
    update("ada_w", g_ada_w.reshape(2 * D_MODEL, ns), after=token)
    for k, g2 in g_small.items():
        update(k, g2, after=token)
    g_a_in, g_a_out = finish_grads("a", outs["ada_w"][1])
    update("a_w_in", g_a_in, copy_grad=True)
    update("a_w_out", g_a_out, copy_grad=True)
    return (loss.reshape(()), grad_x[None], *[outs[k][0] for k in order], *[outs[k][1] for k in order],
            *[outs[k][2] for k in order], *[outs[k][3] for k in order])
```

```python
import functools

import jax
import jax.numpy as jnp
from jax import lax
from jax.experimental import pallas as pl
from jax.experimental.pallas import tpu as pltpu

F32 = jnp.float32
BF16 = jnp.bfloat16

SEQ = 2048
D_MODEL = 1024
CONV_WIDTH = 31
HEAD_DIM = 64
N_HEADS = 16
DILATIONS = (1, 4, 16)
ATTN_BLOCK = 128
NORM_EPS = 1e-6
NEG_INF = -1e30
N_DEV = 8
N_CHIPS = 4

ADAM_LR = 0.001
ADAM_B1 = 0.9
ADAM_B2 = 0.999
ADAM_EPS = 1e-08
ADAM_WD = 0.01
ADAM_STEP = 10

VMEM_LIMIT_BYTES = 52 * 1024 * 1024
HALO = 32
LANES = 128
ROW_TILE = 512
MESH = pl.DeviceIdType.MESH


def _params(*sem):
    return pltpu.CompilerParams(dimension_semantics=sem or None, vmem_limit_bytes=VMEM_LIMIT_BYTES)


def _sigmoid(v):
    return 1.0 / (1.0 + jnp.exp(-v))


def _row_spec(tm, cols, col_block=0):
    return pl.BlockSpec((tm, cols), lambda i: (i, col_block))


def _vec_spec(rows, cols):
    return pl.BlockSpec((rows, cols), lambda i: (0, 0))


def _normmod(xv, g, scale, shift):
    r = lax.rsqrt(jnp.mean(xv * xv, axis=-1, keepdims=True) + NORM_EPS)
    return xv * r * g * (1.0 + scale) + shift


def _normmod_fwd(x, g, scale, shift, name):
    tm = ROW_TILE

    def body(x_ref, g_ref, sc_ref, sh_ref, h_ref, ht_ref):
        h = _normmod(x_ref[...], g_ref[...], sc_ref[...], sh_ref[...])
        h_ref[...] = h.astype(BF16)
        ht_ref[...] = h.T.astype(BF16)

    return pl.pallas_call(
        body, name=name, grid=(SEQ // tm,),
        in_specs=[_row_spec(tm, D_MODEL)] + [_vec_spec(1, D_MODEL)] * 3,
        out_specs=[_row_spec(tm, D_MODEL), pl.BlockSpec((D_MODEL, tm), lambda i: (0, i))],
        out_shape=[jax.ShapeDtypeStruct((SEQ, D_MODEL), BF16), jax.ShapeDtypeStruct((D_MODEL, SEQ), BF16)],
        compiler_params=_params("parallel"),
    )(x, g, scale, shift)


def _normmod_bwd(x, g, scale, dh_parts, dres, name, part_dilations=None, gated=None):
    tm = ROW_TILE
    n_parts = len(dh_parts)
    dils = part_dilations or (1,) * n_parts
    dh_parts = [p if d == 1 else p.reshape(d, SEQ // d, D_MODEL) for p, d in zip(dh_parts, dils)]
    n_gated = 0 if gated is None else 2

    def body(x_ref, g_ref, sc_ref, dres_ref, *rest):
        part_refs = rest[:n_parts]
        gated_refs = rest[n_parts:n_parts + n_gated]
        out_refs = rest[n_parts + n_gated:]
        dx_ref, sums_ref, nat = out_refs[0], out_refs[1], out_refs[-1]
        xv = x_ref[...]
        r = lax.rsqrt(jnp.mean(xv * xv, axis=-1, keepdims=True) + NORM_EPS)
        xn = xv * r
        dh = _load_natural(part_refs[0], nat, dils[0])
        for p, d in zip(part_refs[1:], dils[1:]):
            dh = dh + _load_natural(p, nat, d)
        gv = g_ref[...]
        one_sc = 1.0 + sc_ref[...]
        dxn = dh * (gv * one_sc)
        dx = dres_ref[...] + r * (dxn - xn * jnp.mean(dxn * xn, axis=-1, keepdims=True))
        dx_ref[...] = dx
        dhx = dh * xn
        rows = [jnp.sum(dhx, axis=0, keepdims=True) * one_sc,
                jnp.sum(dhx, axis=0, keepdims=True) * gv,
                jnp.sum(dh, axis=0, keepdims=True)]
        if gated is not None:
            gate_ref, y_ref = gated_refs
            out_refs[2][...] = (dx * gate_ref[...]).astype(BF16)
            rows.append(jnp.sum(dx * y_ref[...].astype(F32), axis=0, keepdims=True))
        sums = jnp.concatenate(rows + [jnp.zeros((8 - len(rows), D_MODEL), F32)], axis=0)

        @pl.when(pl.program_id(0) == 0)
        def _():
            sums_ref[...] = jnp.zeros_like(sums_ref)

        sums_ref[...] += sums

    gated_specs = [] if gated is None else [_vec_spec(1, D_MODEL), _row_spec(tm, D_MODEL)]
    dy_spec = [] if gated is None else [_row_spec(tm, D_MODEL)]
    dy_shape = [] if gated is None else [jax.ShapeDtypeStruct((SEQ, D_MODEL), BF16)]
    return pl.pallas_call(
        body, name=name, grid=(SEQ // tm,),
        in_specs=[_row_spec(tm, D_MODEL), _vec_spec(1, D_MODEL), _vec_spec(1, D_MODEL), _row_spec(tm, D_MODEL)]
        + [_class_spec(tm, d) for d in dils] + gated_specs,
        out_specs=[_row_spec(tm, D_MODEL), _vec_spec(8, D_MODEL)] + dy_spec,
        out_shape=[jax.ShapeDtypeStruct((SEQ, D_MODEL), F32), jax.ShapeDtypeStruct((8, D_MODEL), F32)] + dy_shape,
        scratch_shapes=[_natural_scratch(tm)],
        compiler_params=_params("arbitrary"),
    )(x, g, scale, dres, *dh_parts, *(gated or ()))


def _mm(lhs, rhs, *, tn, tile0, n_tiles, out_dtype, name, out3d=None, prev=None, transpose_lhs=False):
    mo, kc = lhs.shape[::-1] if transpose_lhs else lhs.shape
    cm = min(mo, 1024)
    tc = 256

    def body(l_ref, r_ref, *rest):
        if transpose_lhs:
            o_ref, lt_ref = rest[-2], rest[-1]

            @pl.when(pl.program_id(0) == 0)
            def _():
                for c in range(kc // tc):
                    lt_ref[:, c * tc:(c + 1) * tc] = l_ref[c * tc:(c + 1) * tc, :].astype(F32).T.astype(l_ref.dtype)
        else:
            o_ref, lt_ref = rest[-1], l_ref
        for m in range(mo // cm):
            rows = pl.ds(m * cm, cm)
            o_ref[rows, :] = jnp.dot(lt_ref[rows, :], r_ref[...], preferred_element_type=F32).astype(out_dtype)

    if rhs.ndim == 3:
        tps_r = rhs.shape[2] // tn
        r_spec = pl.BlockSpec((None, kc, tn), lambda t: ((tile0 + t) // tps_r, 0, (tile0 + t) % tps_r))
    else:
        r_spec = pl.BlockSpec((kc, tn), lambda t: (0, t))
    in_specs = [pl.BlockSpec(lhs.shape, lambda t: (0, 0)), r_spec]
    args = [lhs, rhs]
    aliases = {}
    if out3d is None:
        o_spec = pl.BlockSpec((mo, tn), lambda t: (0, t))
        o_shape = jax.ShapeDtypeStruct((mo, n_tiles * tn), out_dtype)
    else:
        j_out, ns_out = out3d
        tps_o = ns_out // tn
        o_spec = pl.BlockSpec((None, mo, tn), lambda t: ((tile0 + t) // tps_o, 0, (tile0 + t) % tps_o))
        o_shape = jax.ShapeDtypeStruct((j_out, mo, ns_out), out_dtype)
        if prev is not None:
            in_specs.append(pl.BlockSpec(memory_space=pl.ANY))
            args.append(prev)
            aliases = {2: 0}
    return pl.pallas_call(
        body, name=name, grid=(n_tiles,), in_specs=in_specs, out_specs=o_spec, out_shape=o_shape,
        input_output_aliases=aliases,
        scratch_shapes=[pltpu.VMEM((mo, kc), lhs.dtype)] if transpose_lhs else [],
        compiler_params=_params("arbitrary" if transpose_lhs else "parallel"),
    )(*args)


def _in_tiles(h_parts, w3, tile_ids, n_tiles, *, tn, total_tiles, part_of, name, prev=None):
    _, kc, ns = w3.shape
    tps = ns // tn
    cm = 1024
    n_parts = len(h_parts)

    def body(ids_ref, *rest):
        h_refs, w_ref, o_ref = rest[:n_parts], rest[n_parts], rest[-1]
        part = part_of(ids_ref[1, pl.program_id(0)])
        for g, h_ref in enumerate(h_refs):
            @pl.when(part == g)
            def _():
                for m in range(SEQ // cm):
                    rows = pl.ds(m * cm, cm)
                    o_ref[rows, :] = jnp.dot(h_ref[rows, :], w_ref[...], preferred_element_type=F32).astype(BF16)

    resident = pl.BlockSpec((SEQ, kc), lambda t, ids: (0, 0))
    in_specs = [resident] * n_parts + [
        pl.BlockSpec((None, kc, tn), lambda t, ids: (ids[0, t] // tps, 0, ids[0, t] % tps))]
    args = [*h_parts, w3]
    aliases = {}
    if prev is not None:
        in_specs.append(pl.BlockSpec(memory_space=pl.ANY))
        args.append(prev)
        aliases = {n_parts + 2: 0}
    return pl.pallas_call(
        body, name=name,
        grid_spec=pltpu.PrefetchScalarGridSpec(
            num_scalar_prefetch=1, grid=(n_tiles,), in_specs=in_specs,
            out_specs=pl.BlockSpec((SEQ, tn), lambda t, ids: (0, ids[1, t]))),
        out_shape=jax.ShapeDtypeStruct((SEQ, total_tiles * tn), BF16),
        input_output_aliases=aliases, compiler_params=_params("arbitrary"),
    )(tile_ids, *args)


def _own_first(chip, total_tiles):
    own = total_tiles // N_CHIPS
    step = jnp.arange(total_tiles, dtype=jnp.int32)
    tiles = (own * chip + step) % total_tiles
    return jnp.stack([step[:own], tiles[:own]]), jnp.stack([tiles[own:], tiles[own:]]), own


def _mm_nt(dy, w3, *, tn, tile0, n_tiles, name, after=None):
    m_rows = dy.shape[0]
    _, kc, ns = w3.shape
    tps = ns // tn
    cm = 512
    extra = [] if after is None else [after]

    def body(dy_ref, w_ref, *rest):
        o_ref, acc = rest[-2], rest[-1]
        t = pl.program_id(0)

        @pl.when(t == 0)
        def _():
            acc[...] = jnp.zeros_like(acc)

        for m in range(m_rows // cm):
            rows = pl.ds(m * cm, cm)
            acc[rows, :] += lax.dot_general(dy_ref[rows, :], w_ref[...], NT_DIMS, preferred_element_type=F32)

        @pl.when(t == n_tiles - 1)
        def _():
            o_ref[...] = acc[...].astype(BF16)

    return pl.pallas_call(
        body, name=name, grid=(n_tiles,),
        in_specs=[pl.BlockSpec((m_rows, tn), lambda t: (0, t)),
                  pl.BlockSpec((None, kc, tn), lambda t: ((tile0 + t) // tps, 0, (tile0 + t) % tps))]
        + [pl.BlockSpec(memory_space=pl.ANY)] * len(extra),
        out_specs=pl.BlockSpec((m_rows, kc), lambda t: (0, 0)),
        out_shape=jax.ShapeDtypeStruct((m_rows, kc), BF16),
        scratch_shapes=[pltpu.VMEM((m_rows, kc), F32)],
        compiler_params=_params("arbitrary"),
    )(dy, w3, *extra)


CONV_CHUNK = 16


def _shift_copies(buf, shifted):
    rows = shifted.shape[1]
    for s in range(1, 8):
        shifted[s - 1] = buf[pl.ds(s, rows), :]


def _shifted_rows(buf, shifted, offset, r0):
    s = offset % 8
    if s == 0:
        return buf[pl.ds(r0 + offset, CONV_CHUNK), :]
    return shifted[s - 1, pl.ds(r0 + (offset - s), CONV_CHUNK), :]


def _spread_taps(w_ref, taps):
    for k in range(CONV_WIDTH):
        taps[k] = jnp.broadcast_to(w_ref[k:k + 1, :], (8, D_MODEL))


def _times_tap(taps, k, rows):
    return (rows.reshape(CONV_CHUNK // 8, 8, D_MODEL) * taps[k][None]).reshape(CONV_CHUNK, D_MODEL)


def _conv_fwd(proj, conv_w, conv_b, ln_g, ln_b, name):
    tm = ROW_TILE
    hb = tm // HALO

    def body(vg_ref, halo_ref, z_ref, w_ref, b_ref, g_ref, be_ref, u5_ref, u5t_ref, u2_ref, buf, shifted, taps):
        i = pl.program_id(0)
        u1 = vg_ref[:, :D_MODEL].astype(F32) * _sigmoid(vg_ref[:, D_MODEL:].astype(F32))
        u1h = halo_ref[:, :D_MODEL].astype(F32) * _sigmoid(halo_ref[:, D_MODEL:].astype(F32))
        buf[pl.ds(0, HALO), :] = jnp.where(i > 0, u1h, 0.0)
        buf[pl.ds(HALO, tm), :] = u1
        _shift_copies(buf, shifted)
        _spread_taps(w_ref, taps)

        def chunk(ci, carry):
            r0 = pl.multiple_of(ci * CONV_CHUNK, CONV_CHUNK)
            acc = jnp.broadcast_to(b_ref[...], (CONV_CHUNK, D_MODEL))
            for k in range(CONV_WIDTH):
                acc = acc + _times_tap(taps, k, _shifted_rows(buf, shifted, HALO - (CONV_WIDTH - 1) + k, r0))
            u2_ref[pl.ds(r0, CONV_CHUNK), :] = acc
            return carry

        lax.fori_loop(0, tm // CONV_CHUNK, chunk, 0)
        acc = u2_ref[...]
        mu = jnp.mean(acc, axis=-1, keepdims=True)
        xc = acc - mu
        rstd = lax.rsqrt(jnp.mean(xc * xc, axis=-1, keepdims=True) + NORM_EPS)
        u3 = xc * rstd * g_ref[...] + be_ref[...]
        zv = z_ref[...].astype(F32)
        u5 = u3 * _sigmoid(u3) * (zv * _sigmoid(zv))
        u5_ref[...] = u5.astype(BF16)
        u5t_ref[...] = u5.T.astype(BF16)

    return pl.pallas_call(
        body, name=name, grid=(SEQ // tm,),
        in_specs=[pl.BlockSpec((tm, 2 * D_MODEL), lambda i: (i, 0)),
                  pl.BlockSpec((HALO, 2 * D_MODEL), lambda i: (jnp.maximum(i * hb - 1, 0), 0)),
                  _row_spec(tm, D_MODEL, 2),
                  _vec_spec(CONV_WIDTH, D_MODEL)] + [_vec_spec(1, D_MODEL)] * 3,
        out_specs=[_row_spec(tm, D_MODEL), pl.BlockSpec((D_MODEL, tm), lambda i: (0, i)), _row_spec(tm, D_MODEL)],
        out_shape=[jax.ShapeDtypeStruct((SEQ, D_MODEL), BF16), jax.ShapeDtypeStruct((D_MODEL, SEQ), BF16),
                   jax.ShapeDtypeStruct((SEQ, D_MODEL), F32)],
        scratch_shapes=[pltpu.VMEM((HALO + tm, D_MODEL), F32), pltpu.VMEM((7, HALO + tm - 8, D_MODEL), F32),
                        pltpu.VMEM((CONV_WIDTH, 8, D_MODEL), F32)],
        compiler_params=_params("parallel"),
    )(proj, proj, proj, conv_w, conv_b, ln_g, ln_b)


def _conv_bwd_pointwise(dy, w_out, proj, u2, ln_g, ln_b, name):
    tm = ROW_TILE

    def body(dy_ref, w_ref, z_ref, u2_ref, g_ref, be_ref, du2_ref, dz_ref, sums_ref):
        u2v = u2_ref[...]
        mu = jnp.mean(u2v, axis=-1, keepdims=True)
        xc = u2v - mu
        rstd = lax.rsqrt(jnp.mean(xc * xc, axis=-1, keepdims=True) + NORM_EPS)
        xhat = xc * rstd
        u3 = xhat * g_ref[...] + be_ref[...]
        s3 = _sigmoid(u3)
        u4 = u3 * s3
        zv = z_ref[...].astype(F32)
        sz = _sigmoid(zv)
        du5v = lax.dot_general(dy_ref[...], w_ref[...], NT_DIMS, preferred_element_type=F32)
        dz_ref[...] = du5v * u4 * (sz * (1.0 + zv * (1.0 - sz)))
        du3 = du5v * (zv * sz) * (s3 * (1.0 + u3 * (1.0 - s3)))
        dxhat = du3 * g_ref[...]
        du2 = rstd * (dxhat - jnp.mean(dxhat, axis=-1, keepdims=True)
                      - xhat * jnp.mean(dxhat * xhat, axis=-1, keepdims=True))
        du2_ref[...] = du2
        sums = jnp.concatenate([
            jnp.sum(du3 * xhat, axis=0, keepdims=True),
            jnp.sum(du3, axis=0, keepdims=True),
            jnp.sum(du2, axis=0, keepdims=True),
            jnp.zeros((5, D_MODEL), F32)], axis=0)

        @pl.when(pl.program_id(0) == 0)
        def _():
            sums_ref[...] = jnp.zeros_like(sums_ref)

        sums_ref[...] += sums

    return pl.pallas_call(
        body, name=name, grid=(SEQ // tm,),
        in_specs=[_row_spec(tm, D_MODEL), _vec_spec(D_MODEL, D_MODEL), _row_spec(tm, D_MODEL, 2),
                  _row_spec(tm, D_MODEL), _vec_spec(1, D_MODEL), _vec_spec(1, D_MODEL)],
        out_specs=[_row_spec(tm, D_MODEL), _row_spec(tm, D_MODEL), _vec_spec(8, D_MODEL)],
        out_shape=[jax.ShapeDtypeStruct((SEQ, D_MODEL), F32), jax.ShapeDtypeStruct((SEQ, D_MODEL), F32),
                   jax.ShapeDtypeStruct((8, D_MODEL), F32)],
        compiler_params=_params("arbitrary"),
    )(dy, w_out, proj, u2, ln_g, ln_b)


def _conv_bwd_taps(du2, dz, proj, conv_w, name):
    tm = ROW_TILE
    hb = tm // HALO
    n_blocks = SEQ // tm

    def body(du2_ref, dnext_ref, dz_ref, vg_ref, w_ref, dproj_ref, dw_ref, dbuf, dshift, sgbuf, ubuf, dwacc, taps):
        i = pl.program_id(0)
        _spread_taps(w_ref, taps)
        sg = _sigmoid(vg_ref[:, D_MODEL:].astype(F32))
        sgbuf[...] = sg
        ubuf[...] = vg_ref[:, :D_MODEL].astype(F32) * sg
        dbuf[pl.ds(0, tm), :] = du2_ref[...]
        dbuf[pl.ds(tm, HALO), :] = jnp.where(i < n_blocks - 1, dnext_ref[...], 0.0)
        _shift_copies(dbuf, dshift)

        @pl.when(i == 0)
        def _():
            dwacc[...] = jnp.zeros_like(dwacc)

        def chunk(ci, carry):
            r0 = pl.multiple_of(ci * CONV_CHUNK, CONV_CHUNK)
            rows = pl.ds(r0, CONV_CHUNK)
            u1c = ubuf[rows, :]
            du1 = jnp.zeros((CONV_CHUNK, D_MODEL), F32)
            for k in range(CONV_WIDTH):
                ahead = _shifted_rows(dbuf, dshift, CONV_WIDTH - 1 - k, r0)
                du1 = du1 + _times_tap(taps, k, ahead)
                prod = u1c * ahead
                dwacc[k] += prod[0:8] + prod[8:16]
            sgc = sgbuf[rows, :]
            dval = du1 * sgc
            dproj_ref[rows, 0:D_MODEL] = dval.astype(BF16)
            dproj_ref[rows, D_MODEL:2 * D_MODEL] = (
                dval * vg_ref[rows, 0:D_MODEL].astype(F32) * (1.0 - sgc)).astype(BF16)
            return carry

        lax.fori_loop(0, tm // CONV_CHUNK, chunk, 0)
        dproj_ref[:, 2 * D_MODEL:] = dz_ref[...].astype(BF16)

        @pl.when(i == n_blocks - 1)
        def _():
            for k in range(CONV_WIDTH):
                dw_ref[k:k + 1, :] = jnp.sum(dwacc[k], axis=0, keepdims=True)
            dw_ref[CONV_WIDTH:, :] = jnp.zeros((32 - CONV_WIDTH, D_MODEL), F32)

    return pl.pallas_call(
        body, name=name, grid=(n_blocks,),
        in_specs=[_row_spec(tm, D_MODEL),
                  pl.BlockSpec((HALO, D_MODEL), lambda i: (jnp.minimum((i + 1) * hb, SEQ // HALO - 1), 0)),
                  _row_spec(tm, D_MODEL),
                  pl.BlockSpec((tm, 2 * D_MODEL), lambda i: (i, 0)),
                  _vec_spec(CONV_WIDTH, D_MODEL)],
        out_specs=[_row_spec(tm, 3 * D_MODEL), _vec_spec(32, D_MODEL)],
        out_shape=[jax.ShapeDtypeStruct((SEQ, 3 * D_MODEL), BF16), jax.ShapeDtypeStruct((32, D_MODEL), F32)],
        scratch_shapes=[pltpu.VMEM((tm + HALO, D_MODEL), F32), pltpu.VMEM((7, HALO + tm - 8, D_MODEL), F32),
                        pltpu.VMEM((tm, D_MODEL), F32), pltpu.VMEM((tm, D_MODEL), F32),
                        pltpu.VMEM((CONV_WIDTH, 8, D_MODEL), F32), pltpu.VMEM((CONV_WIDTH, 8, D_MODEL), F32)],
        compiler_params=_params("arbitrary"),
    )(du2, du2, dz, proj, conv_w)


def _out_a(u5, w_out, x, gate, g1, scale1, shift1, name):
    tm = ROW_TILE
    n_d = len(DILATIONS)

    def body(u_ref, w_ref, x_ref, gate_ref, g_ref, sc_ref, sh_ref, x1_ref, y_ref, ht_ref, *rest):
        h_refs, nat = rest[:n_d], rest[-1]
        y = jnp.dot(u_ref[...], w_ref[...], preferred_element_type=F32)
        x1 = x_ref[...] + gate_ref[...] * y
        y_ref[...] = y.astype(BF16)
        x1_ref[...] = x1
        h = _normmod(x1, g_ref[...], sc_ref[...], sh_ref[...])
        ht_ref[...] = h.T.astype(BF16)
        for h_ref, d in zip(h_refs, DILATIONS):
            _store_classes(h_ref, h, nat, d)

    res = pl.pallas_call(
        body, name=name, grid=(SEQ // tm,),
        in_specs=[_row_spec(tm, D_MODEL), _vec_spec(D_MODEL, D_MODEL), _row_spec(tm, D_MODEL)]
        + [_vec_spec(1, D_MODEL)] * 4,
        out_specs=[_row_spec(tm, D_MODEL), _row_spec(tm, D_MODEL), pl.BlockSpec((D_MODEL, tm), lambda i: (0, i))]
        + [_class_spec(tm, d) for d in DILATIONS],
        out_shape=[jax.ShapeDtypeStruct((SEQ, D_MODEL), F32), jax.ShapeDtypeStruct((SEQ, D_MODEL), BF16),
                   jax.ShapeDtypeStruct((D_MODEL, SEQ), BF16)] + [_class_shape(d, BF16) for d in DILATIONS],
        scratch_shapes=[_natural_scratch(tm)],
        compiler_params=_params("parallel"),
    )(u5, w_out, x, gate, g1, scale1, shift1)
    return res[0], res[1], res[2], [a.reshape(SEQ, D_MODEL) for a in res[3:]]


def _out_b_loss(u, w_out, x1, gate, target, name):
    tm = ROW_TILE

    def body(u_ref, w_ref, x_ref, gate_ref, t_ref, e_ref, dy_ref, sums_ref):
        y = jnp.dot(u_ref[...], w_ref[...], preferred_element_type=F32)
        diff = x_ref[...] + gate_ref[...] * y - t_ref[...]
        e = diff * (1.0 / D_MODEL)
        e_ref[...] = e
        dy_ref[...] = (e * gate_ref[...]).astype(BF16)
        sums = jnp.concatenate([
            jnp.sum(e * y, axis=0, keepdims=True),
            jnp.sum(diff * diff, axis=0, keepdims=True),
            jnp.zeros((6, D_MODEL), F32)], axis=0)

        @pl.when(pl.program_id(0) == 0)
        def _():
            sums_ref[...] = jnp.zeros_like(sums_ref)

        sums_ref[...] += sums

    return pl.pallas_call(
        body, name=name, grid=(SEQ // tm,),
        in_specs=[_row_spec(tm, D_MODEL), _vec_spec(D_MODEL, D_MODEL), _row_spec(tm, D_MODEL),
                  _vec_spec(1, D_MODEL), _row_spec(tm, D_MODEL)],
        out_specs=[_row_spec(tm, D_MODEL), _row_spec(tm, D_MODEL), _vec_spec(8, D_MODEL)],
        out_shape=[jax.ShapeDtypeStruct((SEQ, D_MODEL), F32), jax.ShapeDtypeStruct((SEQ, D_MODEL), BF16),
                   jax.ShapeDtypeStruct((8, D_MODEL), F32)],
        compiler_params=_params("arbitrary"),
    )(u, w_out, x1, gate, target)


def _seg_matrix():
    r = lax.broadcasted_iota(jnp.int32, (256, 256), 0) // HEAD_DIM
    c = lax.broadcasted_iota(jnp.int32, (256, 256), 1) // HEAD_DIM
    return jnp.where(r == c, 1.0 / HEAD_DIM, 0.0).astype(BF16)


def _segmean(v, seg):
    hi = v.astype(BF16)
    lo = (v - hi.astype(F32)).astype(BF16)
    outs = []
    for c0 in range(0, D_MODEL, 256):
        outs.append(jnp.dot(hi[:, c0:c0 + 256], seg, preferred_element_type=F32)
                    + jnp.dot(lo[:, c0:c0 + 256], seg, preferred_element_type=F32))
    return jnp.concatenate(outs, axis=1)


def _qk_rstd(v, seg):
    return lax.rsqrt(_segmean(v * v, seg) + NORM_EPS)


def _qknorm_fwd(proj, group, qw, kw, seg, name):
    tm = ROW_TILE

    def body(q_in, k_in, qw_ref, kw_ref, seg_ref, q_ref, k_ref):
        segv = seg_ref[...]
        q = q_in[...].astype(F32)
        k = k_in[...].astype(F32)
        q_ref[...] = (q * _qk_rstd(q, segv) * qw_ref[...] * HEAD_DIM ** -0.5).astype(BF16)
        k_ref[...] = (k * _qk_rstd(k, segv) * kw_ref[...]).astype(BF16)

    return pl.pallas_call(
        body, name=name, grid=(SEQ // tm,),
        in_specs=[_row_spec(tm, D_MODEL, 3 * group), _row_spec(tm, D_MODEL, 3 * group + 1),
                  _vec_spec(1, D_MODEL), _vec_spec(1, D_MODEL), _vec_spec(256, 256)],
        out_specs=[_row_spec(tm, D_MODEL)] * 2,
        out_shape=[jax.ShapeDtypeStruct((SEQ, D_MODEL), BF16)] * 2,
        compiler_params=_params("parallel"),
    )(proj, proj, qw, kw, seg)


def _attn_masks(b, bpc, dilation, transposed=False):
    keys = ATTN_BLOCK if bpc == 1 else 2 * ATTN_BLOCK
    shape, q_axis = ((keys, ATTN_BLOCK), 1) if transposed else ((ATTN_BLOCK, keys), 0)
    qi = lax.broadcasted_iota(jnp.int32, shape, q_axis)
    kj = lax.broadcasted_iota(jnp.int32, shape, 1 - q_axis)
    if bpc == 1:
        steps = qi - kj
        return (steps * dilation).astype(F32), steps >= 0
    steps = qi + ATTN_BLOCK - kj
    has_prev = (b % bpc) != 0
    valid = (steps >= 0) & (steps <= ATTN_BLOCK) & (has_prev | (kj >= ATTN_BLOCK))
    return (steps * dilation).astype(F32), valid


MASKED = 1e30


def _bias_scratch(bpc):
    return pltpu.VMEM((1 if bpc == 1 else 2, N_HEADS, ATTN_BLOCK, (1 if bpc == 1 else 2) * ATTN_BLOCK), F32)


def _fill_bias(bias_ref, sl_ref, bpc, dilation):
    for variant in range(bias_ref.shape[0]):
        dist, valid = _attn_masks(variant, min(bpc, 2), dilation)
        bias_ref[variant] = jnp.where(valid[None], dist[None] * sl_ref[...], MASKED)


def _step_bias(bias_ref, b, bpc):
    if bpc == 1:
        return bias_ref[0]
    return bias_ref[jnp.where((b % bpc) != 0, 1, 0)]


def _key_tile(prev_ref, cur_ref, cols, bpc):
    if bpc == 1:
        return cur_ref[:, cols]
    return jnp.concatenate([prev_ref[:, cols], cur_ref[:, cols]], axis=0)


ATTN_HEADS_FWD = 16
ATTN_HEADS_BWD = 16
NT_DIMS = (((1,), (1,)), ((), ()))
BATCH_NT_DIMS = (((2,), (2,)), ((0,), (0,)))
BATCH_NN_DIMS = (((2,), (1,)), ((0,), (0,)))
BATCH_TN_DIMS = (((1,), (1,)), ((0,), (0,)))


def _head_stack(tile_of, heads):
    return jnp.stack([tile_of(slice(h * HEAD_DIM, (h + 1) * HEAD_DIM)) for h in range(heads)], axis=0)


def _attn_specs(heads, segment=0):
    width = heads * HEAD_DIM
    off = segment * (D_MODEL // width)
    last = SEQ // ATTN_BLOCK - 1
    cur = pl.BlockSpec((ATTN_BLOCK, width), lambda hg, b: (jnp.minimum(b, last), hg + off))
    prev = pl.BlockSpec((ATTN_BLOCK, width), lambda hg, b: (jnp.clip(b - 1, 0, last), hg + off))
    return cur, prev


def _attn_fwd(q, k, proj, group, slopes, dilation, name):
    bpc = SEQ // dilation // ATTN_BLOCK
    heads = ATTN_HEADS_FWD
    assert heads == N_HEADS
    cur, prev = _attn_specs(heads)
    v_cur, v_prev = _attn_specs(heads, segment=3 * group + 2)

    def body(sl_ref, q_ref, kp_ref, kc_ref, vp_ref, vc_ref, o_ref, lse_ref, bias_ref):
        b = pl.program_id(1)

        @pl.when(b == 0)
        def _():
            _fill_bias(bias_ref, sl_ref, bpc, dilation)

        q3 = _head_stack(lambda cols: q_ref[:, cols], heads)
        k3 = _head_stack(lambda cols: _key_tile(kp_ref, kc_ref, cols, bpc), heads)
        v3 = _head_stack(lambda cols: _key_tile(vp_ref, vc_ref, cols, bpc), heads)
        s = lax.dot_general(q3, k3, BATCH_NT_DIMS, preferred_element_type=F32)
        s = s - _step_bias(bias_ref, b, bpc)
        m = jnp.max(s, axis=-1, keepdims=True)
        p = jnp.exp(s - m)
        l = jnp.sum(p, axis=-1, keepdims=True)
        o3 = lax.dot_general(p.astype(BF16), v3, BATCH_NN_DIMS, preferred_element_type=F32) / l
        lse3 = m + jnp.log(l)
        for h in range(heads):
            o_ref[:, h * HEAD_DIM:(h + 1) * HEAD_DIM] = o3[h].astype(BF16)
        lse_ref[...] = jnp.concatenate([lse3[h] for h in range(heads)]
                                       + [jnp.zeros((ATTN_BLOCK, LANES - heads), F32)], axis=1)

    return pl.pallas_call(
        body, name=name, grid=(N_HEADS // heads, SEQ // ATTN_BLOCK),
        in_specs=[pl.BlockSpec((heads, 1, 1), lambda hg, b: (hg, 0, 0)), cur, prev, cur, v_prev, v_cur],
        out_specs=[cur, pl.BlockSpec((ATTN_BLOCK, LANES), lambda hg, b: (b, 0))],
        out_shape=[jax.ShapeDtypeStruct((SEQ, D_MODEL), BF16), jax.ShapeDtypeStruct((SEQ, LANES), F32)],
        scratch_shapes=[_bias_scratch(bpc)],
        compiler_params=_params("parallel", "arbitrary"),
    )(slopes.reshape(N_HEADS, 1, 1), q, k, k, proj, proj)


def _class_spec(tm, dilation, width=D_MODEL):
    if dilation == 1:
        return _row_spec(tm, width)
    return pl.BlockSpec((dilation, tm // dilation, width), lambda i: (0, i, 0))


def _class_shape(dilation, dtype, width=D_MODEL):
    if dilation == 1:
        return jax.ShapeDtypeStruct((SEQ, width), dtype)
    return jax.ShapeDtypeStruct((dilation, SEQ // dilation, width), dtype)


def _load_natural(in_ref, nat_ref, dilation):
    if dilation == 1:
        return in_ref[...].astype(F32)
    n = nat_ref.shape[1] // dilation
    tiles = in_ref.shape[-1] // LANES
    for r in range(dilation):
        for j in range(tiles):
            nat_ref.at[j][pl.ds(r, n, stride=dilation), :] = in_ref[r, :, j * LANES:(j + 1) * LANES].astype(F32)
    if tiles == 1:
        return nat_ref[0]
    return jnp.concatenate([nat_ref[j] for j in range(tiles)], axis=1)


def _store_classes(out_ref, value, nat_ref, dilation):
    if dilation == 1:
        out_ref[...] = value.astype(out_ref.dtype)
        return
    n = nat_ref.shape[1] // dilation
    tiles = value.shape[-1] // LANES
    for j in range(tiles):
        nat_ref[j] = value[:, j * LANES:(j + 1) * LANES]
    for r in range(dilation):
        for j in range(tiles):
            out_ref[r, :, j * LANES:(j + 1) * LANES] = (
                nat_ref.at[j][pl.ds(r, n, stride=dilation), :].astype(out_ref.dtype))


def _natural_scratch(tm):
    return pltpu.VMEM((D_MODEL // LANES, tm, LANES), F32)


def _head_selector():
    lane_head = lax.broadcasted_iota(jnp.int32, (D_MODEL, LANES), 0) // HEAD_DIM
    head = lax.broadcasted_iota(jnp.int32, (D_MODEL, LANES), 1)
    return (lane_head == head).astype(BF16)


def _dot_split(v, m01, dims):
    hi = v.astype(BF16)
    lo = (v - hi.astype(F32)).astype(BF16)
    return (lax.dot_general(hi, m01, dims, preferred_element_type=F32)
            + lax.dot_general(lo, m01, dims, preferred_element_type=F32))


def _merge_fwd(o_parts, lse_parts, z, sel, name):
    tm = ROW_TILE
    h_spec = pl.BlockSpec((tm, LANES), lambda i: (i, 0))

    def body(o0, o1, o2, l0, l1, l2, z_ref, sel_ref, u_ref, ut_ref, o_ref, lse_ref, nat):
        ls = [_load_natural(l, nat, d) for l, d in zip((l0, l1, l2), DILATIONS)]
        m = jnp.maximum(jnp.maximum(ls[0], ls[1]), ls[2])
        tot = m + jnp.log(jnp.exp(ls[0] - m) + jnp.exp(ls[1] - m) + jnp.exp(ls[2] - m))
        o = jnp.zeros((tm, D_MODEL), F32)
        for o_in, l, d in zip((o0, o1, o2), ls, DILATIONS):
            weight = _dot_split(jnp.exp(l - tot), sel_ref[...], NT_DIMS)
            o = o + weight * _load_natural(o_in, nat, d)
        zv = z_ref[...].astype(F32)
        u = o * (zv * _sigmoid(zv))
        u_ref[...] = u.astype(BF16)
        ut_ref[...] = u.T.astype(BF16)
        o_ref[...] = o.astype(BF16)
        lse_ref[...] = tot

    return pl.pallas_call(
        body, name=name, grid=(SEQ // tm,),
        in_specs=[_class_spec(tm, d) for d in DILATIONS] + [_class_spec(tm, d, LANES) for d in DILATIONS]
        + [_row_spec(tm, D_MODEL, B_Z_SEGMENT), _vec_spec(D_MODEL, LANES)],
        out_specs=[_row_spec(tm, D_MODEL), pl.BlockSpec((D_MODEL, tm), lambda i: (0, i)),
                   _row_spec(tm, D_MODEL), h_spec],
        out_shape=[jax.ShapeDtypeStruct((SEQ, D_MODEL), BF16), jax.ShapeDtypeStruct((D_MODEL, SEQ), BF16),
                   jax.ShapeDtypeStruct((SEQ, D_MODEL), BF16), jax.ShapeDtypeStruct((SEQ, LANES), F32)],
        scratch_shapes=[_natural_scratch(tm)],
        compiler_params=_params("parallel"),
    )(*o_parts, *lse_parts, z, sel)


def _merge_bwd(dy, w_out, o, lse, z, sel, name):
    tm = ROW_TILE
    n_d = len(DILATIONS)

    def body(dy_ref, w_ref, o_ref, lse_ref, z_ref, sel_ref, dz_ref, *rest):
        do_refs, delta_refs, lse_refs, nat = rest[:n_d], rest[n_d:2 * n_d], rest[2 * n_d:3 * n_d], rest[-1]
        zv = z_ref[...].astype(F32)
        sz = _sigmoid(zv)
        duv = lax.dot_general(dy_ref[...], w_ref[...], NT_DIMS, preferred_element_type=F32)
        ov = o_ref[...].astype(F32)
        do = duv * (zv * sz)
        dz_ref[...] = (duv * ov * (sz * (1.0 + zv * (1.0 - sz)))).astype(BF16)
        delta = _dot_split(do * ov, sel_ref[...], (((1,), (0,)), ((), ())))
        lv = lse_ref[...]
        for i, d in enumerate(DILATIONS):
            _store_classes(do_refs[i], do, nat, d)
            _store_classes(delta_refs[i], delta, nat, d)
            _store_classes(lse_refs[i], lv, nat, d)

    res = pl.pallas_call(
        body, name=name, grid=(SEQ // tm,),
        in_specs=[_row_spec(tm, D_MODEL), _vec_spec(D_MODEL, D_MODEL), _row_spec(tm, D_MODEL), _row_spec(tm, LANES),
                  _row_spec(tm, D_MODEL, B_Z_SEGMENT), _vec_spec(D_MODEL, LANES)],
        out_specs=[_row_spec(tm, D_MODEL)] + [_class_spec(tm, d) for d in DILATIONS]
        + [_class_spec(tm, d, LANES) for d in DILATIONS] * 2,
        out_shape=[jax.ShapeDtypeStruct((SEQ, D_MODEL), BF16)] + [_class_shape(d, BF16) for d in DILATIONS]
        + [_class_shape(d, F32, LANES) for d in DILATIONS] * 2,
        scratch_shapes=[_natural_scratch(tm)],
        compiler_params=_params("parallel"),
    )(dy, w_out, o, lse, z, sel)
    flat = lambda a: a.reshape(SEQ, a.shape[-1])
    return (res[0], [flat(a) for a in res[1:1 + n_d]], [flat(a) for a in res[1 + n_d:1 + 2 * n_d]],
            [flat(a) for a in res[1 + 2 * n_d:]])


def _attn_bwd(q, k, proj, group, do, lse, delta, slopes, dilation, name):
    bpc = SEQ // dilation // ATTN_BLOCK
    heads = ATTN_HEADS_BWD
    n_blocks = SEQ // ATTN_BLOCK
    carry = bpc > 1
    width = heads * HEAD_DIM
    cur, prev = _attn_specs(heads)
    v_cur, v_prev = _attn_specs(heads, segment=3 * group + 2)
    assert heads == N_HEADS
    per_head = pl.BlockSpec((ATTN_BLOCK, LANES), lambda hg, b: (jnp.minimum(b, n_blocks - 1), 0))
    scale = HEAD_DIM ** -0.5

    def body(sl_ref, q_ref, kp_ref, kc_ref, vp_ref, vc_ref, do_ref, lse_ref, dl_ref,
             dq_ref, dk_ref, dv_ref, *scratch):
        b = pl.program_id(1)
        if carry:
            dk_carry, dv_carry = scratch

            @pl.when(b == n_blocks)
            def _():
                dk_ref[...] = dk_carry[...].astype(BF16)
                dv_ref[...] = dv_carry[...].astype(BF16)

            @pl.when(b < n_blocks)
            def _():
                step(sl_ref, q_ref, kp_ref, kc_ref, vp_ref, vc_ref, do_ref, lse_ref, dl_ref,
                     dq_ref, dk_ref, dv_ref, dk_carry, dv_carry, b)
        else:
            step(sl_ref, q_ref, kp_ref, kc_ref, vp_ref, vc_ref, do_ref, lse_ref, dl_ref,
                 dq_ref, dk_ref, dv_ref, None, None, b)

    def step(sl_ref, q_ref, kp_ref, kc_ref, vp_ref, vc_ref, do_ref, lse_ref, dl_ref,
             dq_ref, dk_ref, dv_ref, dk_carry, dv_carry, b):
        if carry:
            @pl.when(b == 0)
            def _():
                dk_carry[...] = jnp.zeros_like(dk_carry)
                dv_carry[...] = jnp.zeros_like(dv_carry)

        q3 = _head_stack(lambda cols: q_ref[:, cols], heads)
        k3 = _head_stack(lambda cols: _key_tile(kp_ref, kc_ref, cols, bpc), heads)
        v3 = _head_stack(lambda cols: _key_tile(vp_ref, vc_ref, cols, bpc), heads)
        do3 = _head_stack(lambda cols: do_ref[:, cols], heads)
        lse_t = lse_ref[...].T
        dl_t = dl_ref[...].T
        lse3 = jnp.stack([lse_t[h:h + 1, :] for h in range(heads)], axis=0)
        dl3 = jnp.stack([dl_t[h:h + 1, :] for h in range(heads)], axis=0)
        s = lax.dot_general(k3, q3, BATCH_NT_DIMS, preferred_element_type=F32)
        dist, valid = _attn_masks(b, bpc, dilation, transposed=True)
        p = jnp.exp(jnp.where(valid[None], s - dist[None] * sl_ref[...], NEG_INF) - lse3)
        dp = lax.dot_general(v3, do3, BATCH_NT_DIMS, preferred_element_type=F32)
        ds = (p * (dp - dl3)).astype(BF16)
        dq3 = lax.dot_general(ds, k3, BATCH_TN_DIMS, preferred_element_type=F32) * scale
        dk3 = lax.dot_general(ds, q3, BATCH_NN_DIMS, preferred_element_type=F32)
        dv3 = lax.dot_general(p.astype(BF16), do3, BATCH_NN_DIMS, preferred_element_type=F32)
        for h in range(heads):
            cols = slice(h * HEAD_DIM, (h + 1) * HEAD_DIM)
            dq_ref[:, cols] = dq3[h].astype(BF16)
            if carry:
                dk_ref[:, cols] = (dk_carry[:, cols] + dk3[h, :ATTN_BLOCK]).astype(BF16)
                dv_ref[:, cols] = (dv_carry[:, cols] + dv3[h, :ATTN_BLOCK]).astype(BF16)
                dk_carry[:, cols] = dk3[h, ATTN_BLOCK:]
                dv_carry[:, cols] = dv3[h, ATTN_BLOCK:]
            else:
                dk_ref[:, cols] = dk3[h].astype(BF16)
                dv_ref[:, cols] = dv3[h].astype(BF16)

    kv_out = prev if carry else cur
    return pl.pallas_call(
        body, name=name, grid=(N_HEADS // heads, n_blocks + (1 if carry else 0)),
        in_specs=[pl.BlockSpec((heads, 1, 1), lambda hg, b: (hg, 0, 0)), cur, prev, cur, v_prev, v_cur,
                  cur, per_head, per_head],
        out_specs=[cur, kv_out, kv_out],
        out_shape=[jax.ShapeDtypeStruct((SEQ, D_MODEL), BF16)] * 3,
        scratch_shapes=[pltpu.VMEM((ATTN_BLOCK, width), F32)] * 2 if carry else [],
        compiler_params=_params("parallel", "arbitrary"),
    )(slopes.reshape(N_HEADS, 1, 1), q, k, k, proj, proj, do, lse, delta)


def _qknorm_bwd(proj, group, qw, kw, seg, dq, dk, dv, name):
    tm = ROW_TILE

    def body(q_in, k_in, qw_ref, kw_ref, seg_ref, dq_ref, dk_ref, dv_ref, dproj_ref, sums_ref):
        segv = seg_ref[...]
        sums = []
        for part, (raw_ref, w_ref, dn_ref) in enumerate(((q_in, qw_ref, dq_ref), (k_in, kw_ref, dk_ref))):
            raw = raw_ref[...].astype(F32)
            dn = dn_ref[...].astype(F32)
            r = _qk_rstd(raw, segv)
            xhat = raw * r
            gq = dn * w_ref[...]
            draw = r * (gq - xhat * _segmean(xhat * gq, segv))
            dproj_ref[:, part * D_MODEL:(part + 1) * D_MODEL] = draw.astype(BF16)
            sums.append(jnp.sum(dn * xhat, axis=0, keepdims=True))
        dproj_ref[:, 2 * D_MODEL:] = dv_ref[...]

        @pl.when(pl.program_id(0) == 0)
        def _():
            sums_ref[...] = jnp.zeros_like(sums_ref)

        sums_ref[...] += jnp.concatenate(sums + [jnp.zeros((6, D_MODEL), F32)], axis=0)

    return pl.pallas_call(
        body, name=name, grid=(SEQ // tm,),
        in_specs=[_row_spec(tm, D_MODEL, 3 * group), _row_spec(tm, D_MODEL, 3 * group + 1),
                  _vec_spec(1, D_MODEL), _vec_spec(1, D_MODEL), _vec_spec(256, 256)] + [_row_spec(tm, D_MODEL)] * 3,
        out_specs=[_row_spec(tm, 3 * D_MODEL), _vec_spec(8, D_MODEL)],
        out_shape=[jax.ShapeDtypeStruct((SEQ, 3 * D_MODEL), BF16), jax.ShapeDtypeStruct((8, D_MODEL), F32)],
        compiler_params=_params("arbitrary"),
    )(proj, proj, qw, kw, seg, dq, dk, dv)


B_TN = 512
B_GROUP_TILES = 3 * D_MODEL // B_TN
B_Z_TILE0 = 3 * B_GROUP_TILES
B_Z_TILES = D_MODEL // B_TN
B_TILES = B_Z_TILE0 + B_Z_TILES
B_Z_SEGMENT = 3 * len(DILATIONS)


def _local_step(x, target, mods, norm_g, conv_w, conv_b, ln_g, ln_b, q_norm, k_norm, chip, own_wa_in, own_wb_in,
                weights_a, weights_b, forward_weights_b, send_grads_b, forward_grads_b, send_grads_a):
    row = lambda a, i: a[i:i + 1]
    shift0, scale0, gate0 = row(mods[0], 0), row(mods[0], 1), row(mods[0], 2)
    shift1, scale1, gate1 = row(mods[1], 0), row(mods[1], 1), row(mods[1], 2)
    g0, g1 = row(norm_g, 0), row(norm_g, 1)
    seg = _seg_matrix()
    slopes = jnp.exp2(-8.0 * jnp.arange(1, N_HEADS + 1, dtype=F32) / N_HEADS)
    qw = [jnp.tile(q_norm[g:g + 1], (1, N_HEADS)) for g in range(3)]
    kw = [jnp.tile(k_norm[g:g + 1], (1, N_HEADS)) for g in range(3)]

    h0, h0t = _normmod_fwd(x, g0, scale0, shift0, "prenorm0")
    nsa = own_wa_in.shape[2]
    tiles_a = dict(tn=nsa, total_tiles=N_CHIPS, part_of=lambda tile: 0)
    own_ids, rest_ids, own_tiles = _own_first(chip, N_CHIPS)
    proj_a = _in_tiles([h0], own_wa_in, own_ids, own_tiles, name="a_in_own", **tiles_a)
    wa_in, wa_out = weights_a(proj_a)
    ja = wa_in.shape[0]
    proj_a = _in_tiles([h0], wa_in, rest_ids, N_CHIPS - own_tiles, name="a_in_rest", prev=proj_a, **tiles_a)
    u5, u5t, u2 = _conv_fwd(proj_a, conv_w, conv_b, ln_g, ln_b, "a_conv")
    x1, y_a, h1t, h1c = _out_a(u5, wa_out, x, gate0, g1, scale1, shift1, "a_out")

    tiles_b = dict(tn=B_TN, total_tiles=B_TILES,
                   part_of=lambda tile: jnp.where(tile >= B_Z_TILE0, 0, tile // B_GROUP_TILES))
    own_ids, rest_ids, own_tiles = _own_first(chip, B_TILES)
    proj_b = _in_tiles(h1c, own_wb_in, own_ids, own_tiles, name="b_in_own", **tiles_b)
    forward_weights_b(proj_b)
    wb_in, wb_out = weights_b(proj_b)
    jb, _, nsb = wb_in.shape
    proj_b = _in_tiles(h1c, wb_in, rest_ids, B_TILES - own_tiles, name="b_in_rest", prev=proj_b, **tiles_b)
    h1 = h1c[0]
    qkv, o_parts, lse_parts = [], [], []
    for g, d in enumerate(DILATIONS):
        qn, kn = _qknorm_fwd(proj_b, g, qw[g], kw[g], seg, f"b_qknorm_g{g}")
        og, lg = _attn_fwd(qn, kn, proj_b, g, slopes, d, f"b_attn_g{g}")
        qkv.append((qn, kn))
        o_parts.append(og if d == 1 else og.reshape(d, SEQ // d, D_MODEL))
        lse_parts.append(lg if d == 1 else lg.reshape(d, SEQ // d, LANES))
    sel = _head_selector()
    u_b, u_bt, o_b, lse_b = _merge_fwd(o_parts, lse_parts, proj_b, sel, "b_merge")
    e, dy_b, sums_loss = _out_b_loss(u_b, wb_out, x1, gate1, target, "b_out_loss")

    dwb_out = _mm(u_bt, dy_b, tn=D_MODEL, tile0=0, n_tiles=1, out_dtype=BF16, name="b_dwout")
    dz_b, do_c, delta_c, lse_c = _merge_bwd(dy_b, wb_out, o_b, lse_b, proj_b, sel, "b_merge_bwd")
    dwb_in = _mm(h1t, dz_b, tn=B_TN, tile0=B_Z_TILE0, n_tiles=B_Z_TILES, out_dtype=BF16, name="b_dwin_z",
                 out3d=(jb, nsb))
    dh1_parts = [_mm_nt(dz_b, wb_in, tn=B_TN, tile0=B_Z_TILE0, n_tiles=B_Z_TILES, name="b_dh_z")]
    qk_sums = []
    for g, d in enumerate(DILATIONS):
        qn, kn = qkv[g]
        dq, dk, dv = _attn_bwd(qn, kn, proj_b, g, do_c[g], lse_c[g], delta_c[g], slopes, d, f"b_attn_bwd_g{g}")
        dproj, sums_qk = _qknorm_bwd(proj_b, g, qw[g], kw[g], seg, dq, dk, dv, f"b_qknorm_bwd_g{g}")
        qk_sums.append(sums_qk)
        dwb_in = _mm(h1t if d == 1 else h1c[g], dproj, tn=B_TN, tile0=g * B_GROUP_TILES, n_tiles=B_GROUP_TILES,
                     out_dtype=BF16, name=f"b_dwin_g{g}", out3d=(jb, nsb), prev=dwb_in, transpose_lhs=d != 1)
        dh = _mm_nt(dproj, wb_in, tn=B_TN, tile0=g * B_GROUP_TILES, n_tiles=B_GROUP_TILES, name=f"b_dh_g{g}")
        dh1_parts.append(dh)
    token = send_grads_b(dwb_in, dwb_out)
    dx1, sums_n1, dy_a = _normmod_bwd(x1, g1, scale1 + token[0:1, 0:1], dh1_parts, e, "prenorm1_bwd",
                                      part_dilations=(1,) + DILATIONS, gated=(gate0, y_a))
    token = forward_grads_b(dx1)

    dwa_out = _mm(u5t, dy_a, tn=D_MODEL, tile0=0, n_tiles=1, out_dtype=BF16, name="a_dwout")
    du2, dz_a, sums_ln = _conv_bwd_pointwise(dy_a, wa_out, proj_a, u2, ln_g + token[0:1, 0:1], ln_b,
                                             "a_conv_bwd_pw")
    dproj_a, dconv_w = _conv_bwd_taps(du2, dz_a, proj_a, conv_w, "a_conv_bwd_taps")
    dwa_in = _mm(h0t, dproj_a, tn=nsa, tile0=0, n_tiles=ja, out_dtype=BF16, name="a_dwin", out3d=(ja, nsa))
    token = send_grads_a(dwa_in, dwa_out)
    dh0 = _mm_nt(dproj_a, wa_in, tn=nsa, tile0=0, n_tiles=ja, name="a_dh", after=token)
    grad_x, sums_n0 = _normmod_bwd(x, g0, scale0, [dh0], dx1, "prenorm0_bwd")

    small = dict(
        dnorm_g=jnp.concatenate([sums_n0[0:1], sums_n1[0:1]], axis=0),
        dmod0=jnp.concatenate([sums_n0[2:3], sums_n0[1:2], sums_n1[3:4]], axis=0),
        dmod1=jnp.concatenate([sums_n1[2:3], sums_n1[1:2], sums_loss[0:1]], axis=0),
        dln_g=sums_ln[0:1], dln_b=sums_ln[1:2], dconv_b=sums_ln[2:3],
        dconv_w=dconv_w[:CONV_WIDTH],
        dq_norm=jnp.concatenate([s[0:1] for s in qk_sums], axis=0),
        dk_norm=jnp.concatenate([s[1:2] for s in qk_sums], axis=0),
        loss_cols=sums_loss[1:2],
    )
    return grad_x, small


def _adamw(w, g, m, v, name, after=None, copy_grad=False):
    rows, cols = w.shape
    tr = rows if rows <= 128 else (256 if cols <= D_MODEL else 128)
    c1 = 1.0 / (1.0 - ADAM_B1 ** ADAM_STEP)
    c2 = 1.0 / (1.0 - ADAM_B2 ** ADAM_STEP)
    extra = [] if after is None else [after]
    n_out = 4 if copy_grad else 3

    def body(w_ref, g_ref, m_ref, v_ref, *rest):
        d_ref, mo_ref, vo_ref = rest[len(extra):len(extra) + 3]
        gv = g_ref[...]
        if copy_grad:
            rest[-1][...] = gv
        mn = ADAM_B1 * m_ref[...] + (1.0 - ADAM_B1) * gv
        vn = ADAM_B2 * v_ref[...] + (1.0 - ADAM_B2) * (gv * gv)
        mo_ref[...] = mn
        vo_ref[...] = vn
        d_ref[...] = -ADAM_LR * ((mn * c1) / (jnp.sqrt(vn * c2) + ADAM_EPS) + ADAM_WD * w_ref[...])

    spec = pl.BlockSpec((tr, cols), lambda i: (i, 0))
    return pl.pallas_call(
        body, name=name, grid=(rows // tr,),
        in_specs=[spec] * 4 + [pl.BlockSpec(memory_space=pl.ANY)] * len(extra), out_specs=[spec] * n_out,
        out_shape=[jax.ShapeDtypeStruct((rows, cols), F32)] * n_out,
        compiler_params=_params("parallel"),
    )(w, g, m, v, *extra)


def _cast_into_slot(w, chip_idx, name, keep_own=False, after=None):
    rows, cols = w.shape
    tr = 256
    extra = [] if after is None else [after]

    def body(ch_ref, w_ref, *rest):
        wb = w_ref[...].astype(BF16)
        for o_ref in rest[len(extra):]:
            o_ref[...] = wb

    slot_spec = pl.BlockSpec((None, tr, cols), lambda i, ch: (ch[0], i, 0))
    own_spec = pl.BlockSpec((None, tr, cols), lambda i, ch: (0, i, 0))
    res = pl.pallas_call(
        body, name=name,
        grid_spec=pltpu.PrefetchScalarGridSpec(
            num_scalar_prefetch=1, grid=(rows // tr,),
            in_specs=[pl.BlockSpec((tr, cols), lambda i, ch: (i, 0))] + [pl.BlockSpec(memory_space=pl.ANY)] * len(extra),
            out_specs=[slot_spec, own_spec] if keep_own else [slot_spec]),
        out_shape=[jax.ShapeDtypeStruct((N_CHIPS, rows, cols), BF16)]
        + ([jax.ShapeDtypeStruct((1, rows, cols), BF16)] if keep_own else []),
        compiler_params=_params("parallel"),
    )(chip_idx, w, *extra)
    return tuple(res) if keep_own else res[0]


def _position():
    x, y, c = lax.axis_index("x"), lax.axis_index("y"), lax.axis_index("c")
    return x, y, c


def _xor_peer(x, y, c, k):
    return (x ^ ((k >> 2) & 1), y ^ ((k >> 1) & 1), c ^ (k & 1))


def _chip_peer(x, y, k):
    return (x ^ ((k >> 1) & 1), y ^ (k & 1))


def _ada_forward(c_row, ada_w, ada_b, conv_w, after=()):
    ns = ada_w.shape[2]
    cw = conv_w.shape[1]

    def body(c_ref, w_ref, b_ref, cv_ref, *rest):
        (mod_ref, sc_ref, cvo_ref, c_all, mp, parts, cv_parts,
         send1, recv1, send2, recv2, send3, recv3) = rest[len(after):]
        x, y, c = _position()
        me = 4 * x + 2 * y + c
        chip = 2 * x + y

        def c_copy(k):
            return pltpu.make_async_remote_copy(
                src_ref=c_all.at[me], dst_ref=c_all.at[me], send_sem=send1.at[k - 1], recv_sem=recv1.at[k - 1],
                device_id=_xor_peer(x, y, c, k), device_id_type=MESH)

        def cv_copy(k):
            px, py = _chip_peer(x, y, k)
            return pltpu.make_async_remote_copy(
                src_ref=cv_parts.at[chip], dst_ref=cv_parts.at[chip], send_sem=send3.at[k - 1],
                recv_sem=recv3.at[k - 1], device_id=(px, py, c), device_id_type=MESH)

        c_all[me] = c_ref[...]
        cv_parts[chip] = cv_ref[...]
        for k in range(1, N_DEV):
            c_copy(k).start()
        for k in range(1, N_CHIPS):
            cv_copy(k).start()
        for k in range(1, N_DEV):
            c_copy(k).wait_recv()
        cv = jnp.concatenate([c_all[i] for i in range(N_DEV)], axis=0)
        sc = cv * _sigmoid(cv)
        sc_ref[...] = sc
        for l in range(2):
            res = jnp.dot(sc.astype(BF16), w_ref[l].astype(BF16), preferred_element_type=F32)
            for i in range(N_DEV):
                mp[i, l:l + 1, :] = res[i:i + 1, :]

        def mod_copy(k):
            px, py = _chip_peer(x, y, k)
            return pltpu.make_async_remote_copy(
                src_ref=mp.at[4 * px + 2 * py + c], dst_ref=parts.at[chip], send_sem=send2.at[k - 1],
                recv_sem=recv2.at[k - 1], device_id=(px, py, c), device_id_type=MESH)

        for k in range(1, N_CHIPS):
            mod_copy(k).start()
        parts[chip] = mp[me]
        for k in range(1, N_CHIPS):
            mod_copy(k).wait_recv()
            cv_copy(k).wait_recv()
        mod_ref[...] = jnp.concatenate([parts[j] for j in range(N_CHIPS)], axis=1) + b_ref[...]
        cvo_ref[...] = jnp.concatenate([cv_parts[j] for j in range(N_CHIPS)], axis=1)
        for k in range(1, N_DEV):
            c_copy(k).wait_send()
        for k in range(1, N_CHIPS):
            mod_copy(k).wait_send()
            cv_copy(k).wait_send()

    vm = pl.BlockSpec(memory_space=pltpu.VMEM)
    return pl.pallas_call(
        body, name="ada_forward",
        in_specs=[vm] * 4 + [pl.BlockSpec(memory_space=pl.ANY)] * len(after), out_specs=[vm] * 3,
        out_shape=[jax.ShapeDtypeStruct((2, 3 * D_MODEL), F32), jax.ShapeDtypeStruct((N_DEV, D_MODEL), F32),
                   jax.ShapeDtypeStruct((CONV_WIDTH, N_CHIPS * cw), F32)],
        scratch_shapes=[pltpu.VMEM((N_DEV, 1, D_MODEL), F32), pltpu.VMEM((N_DEV, 2, ns), F32),
                        pltpu.VMEM((N_CHIPS, 2, ns), F32), pltpu.VMEM((N_CHIPS, CONV_WIDTH, cw), F32),
                        pltpu.SemaphoreType.DMA((N_DEV - 1,)), pltpu.SemaphoreType.DMA((N_DEV - 1,)),
                        pltpu.SemaphoreType.DMA((N_CHIPS - 1,)), pltpu.SemaphoreType.DMA((N_CHIPS - 1,)),
                        pltpu.SemaphoreType.DMA((N_CHIPS - 1,)), pltpu.SemaphoreType.DMA((N_CHIPS - 1,))],
        compiler_params=pltpu.CompilerParams(vmem_limit_bytes=VMEM_LIMIT_BYTES),
    )(c_row, ada_w, ada_b, conv_w, *after)


HBM_SPEC = pl.BlockSpec(memory_space=pltpu.HBM)
ANY_SPEC = pl.BlockSpec(memory_space=pl.ANY)
SEM_SPEC = pl.BlockSpec(memory_space=pltpu.SEMAPHORE)
SPLIT_PARAMS = dict(compiler_params=pltpu.CompilerParams(has_side_effects=pltpu.SideEffectType.DATAFLOW_SIDE_EFFECTING))
TOKEN = jax.ShapeDtypeStruct((8, 128), F32)
ENTRY_HANDSHAKES = {name: (i, peers) for i, (name, peers) in enumerate((
    ("gather_start_a", "chips"), ("gather_start_b", "chips"),
    ("gather_forward_a", "sibling"), ("gather_forward_b", "sibling"),
    ("reduce_d2d_start_b", "sibling"), ("reduce_d2d_start_a", "sibling"),
    ("reduce_ici_start_b", "chips"), ("reduce_ici_start_a", "chips"),
    ("reduce_share_start_b", "sibling"), ("reduce_share_start_a", "sibling"),
    ("small_gather_start", "devices")))}


def _hbm(arrays):
    return [pltpu.with_memory_space_constraint(a, pltpu.HBM) for a in arrays]


def _hbm_like(arrays):
    return [pltpu.HBM(a.shape, a.dtype) for a in arrays]


def _gather_start(lands, after, name):
    n = len(lands)

    def body(*refs):
        _handshake(ENTRY_HANDSHAKES[name][1])
        ins = refs[:n]
        send, recv = refs[n + 1], refs[n + 2]
        x, y, c = _position()
        chip = 2 * x + y
        for t in range(n):
            rh = ins[t].shape[1] // 2
            for k in range(1, N_CHIPS):
                px, py = _chip_peer(x, y, k)
                block = ins[t].at[chip, pl.ds(c * rh, rh)]
                pltpu.make_async_remote_copy(
                    src_ref=block, dst_ref=block, send_sem=send.at[3 * t + k - 1], recv_sem=recv.at[3 * t + k - 1],
                    device_id=(px, py, c), device_id_type=MESH).start()
        refs[-1][...] = jnp.zeros(TOKEN.shape, F32)

    res = pl.pallas_call(
        body, name=name, in_specs=[HBM_SPEC] * n + [ANY_SPEC],
        out_specs=(SEM_SPEC, SEM_SPEC, *[HBM_SPEC] * n, pl.BlockSpec(memory_space=pltpu.VMEM)),
        out_shape=(pltpu.SemaphoreType.DMA((3 * n,)), pltpu.SemaphoreType.DMA((3 * n,)), *_hbm_like(lands), TOKEN),
        input_output_aliases={t: 2 + t for t in range(n)}, **_split_params(name),
    )(*_hbm(lands), after)
    return res[0], res[1], list(res[2:2 + n]), res[-1]


def _gather_forward(send, recv, lands, after, name):
    n = len(lands)

    def body(*refs):
        _handshake(ENTRY_HANDSHAKES[name][1])
        ins = refs[:n]
        send1, recv1 = refs[n], refs[n + 1]
        send2, recv2 = refs[n + 3], refs[n + 4]
        x, y, c = _position()
        chip = 2 * x + y
        for t in range(n):
            rh = ins[t].shape[1] // 2
            half = pl.ds(c * rh, rh)
            for k in range(1, N_CHIPS):
                px, py = _chip_peer(x, y, k)
                s = 3 * t + k - 1
                got = ins[t].at[2 * px + py, half]
                cp = pltpu.make_async_remote_copy(
                    src_ref=ins[t].at[chip, half], dst_ref=got, send_sem=send1.at[s], recv_sem=recv1.at[s],
                    device_id=(px, py, c), device_id_type=MESH)
                cp.wait_send()
                cp.wait_recv()
                pltpu.make_async_remote_copy(
                    src_ref=got, dst_ref=got, send_sem=send2.at[s], recv_sem=recv2.at[s],
                    device_id=(x, y, 1 - c), device_id_type=MESH).start()
        refs[-1][...] = jnp.zeros(TOKEN.shape, F32)

    res = pl.pallas_call(
        body, name=name, in_specs=[HBM_SPEC] * n + [SEM_SPEC, SEM_SPEC, ANY_SPEC],
        out_specs=(SEM_SPEC, SEM_SPEC, *[HBM_SPEC] * n, pl.BlockSpec(memory_space=pltpu.VMEM)),
        out_shape=(pltpu.SemaphoreType.DMA((3 * n,)), pltpu.SemaphoreType.DMA((3 * n,)), *_hbm_like(lands), TOKEN),
        input_output_aliases={t: 2 + t for t in range(n)}, **_split_params(name),
    )(*lands, send, recv, after)
    return res[0], res[1], list(res[2:2 + n]), res[-1]


def _gather_wait(send, recv, lands, after, name):
    n = len(lands)

    def body(*refs):
        ins = refs[:n]
        send_ref, recv_ref = refs[n], refs[n + 1]
        x, y, c = _position()
        for t in range(n):
            rh = ins[t].shape[1] // 2
            for k in range(1, N_CHIPS):
                px, py = _chip_peer(x, y, k)
                cp = pltpu.make_async_remote_copy(
                    src_ref=ins[t].at[2 * px + py, pl.ds(c * rh, rh)],
                    dst_ref=ins[t].at[2 * px + py, pl.ds((1 - c) * rh, rh)], send_sem=send_ref.at[3 * t + k - 1],
                    recv_sem=recv_ref.at[3 * t + k - 1], device_id=(x, y, 1 - c), device_id_type=MESH)
                cp.wait_send()
                cp.wait_recv()

    res = pl.pallas_call(
        body, name=name, in_specs=[HBM_SPEC] * n + [SEM_SPEC, SEM_SPEC, ANY_SPEC], out_specs=[HBM_SPEC] * n,
        out_shape=_hbm_like(lands), input_output_aliases={t: t for t in range(n)}, **SPLIT_PARAMS,
    )(*lands, send, recv, after)
    return list(res)


def _handshake(peers):
    x, y, c = _position()
    if peers == "sibling":
        ids = [(x, y, 1 - c)]
    elif peers == "chips":
        ids = [(*_chip_peer(x, y, k), c) for k in range(1, N_CHIPS)]
    else:
        ids = [_xor_peer(x, y, c, k) for k in range(1, N_DEV)]
    barrier = pltpu.get_barrier_semaphore()
    for peer in ids:
        pl.semaphore_signal(barrier, inc=1, device_id=peer, device_id_type=MESH)
    pl.semaphore_wait(barrier, len(ids))


def _split_params(name):
    return dict(compiler_params=pltpu.CompilerParams(
        has_side_effects=pltpu.SideEffectType.DATAFLOW_SIDE_EFFECTING, collective_id=ENTRY_HANDSHAKES[name][0]))


def _split_start(name, arrays, n_sems, after, issue):
    m = len(arrays)

    def body(*refs):
        _handshake(ENTRY_HANDSHAKES[name][1])
        issue(refs[:m], refs[m + 1], refs[m + 2])
        refs[-1][...] = jnp.zeros(TOKEN.shape, F32)

    res = pl.pallas_call(
        body, name=name, in_specs=[HBM_SPEC] * m + [ANY_SPEC],
        out_specs=(SEM_SPEC, SEM_SPEC, *[HBM_SPEC] * m, pl.BlockSpec(memory_space=pltpu.VMEM)),
        out_shape=(pltpu.SemaphoreType.DMA((n_sems,)), pltpu.SemaphoreType.DMA((n_sems,)), *_hbm_like(arrays), TOKEN),
        input_output_aliases={t: 2 + t for t in range(m)}, **_split_params(name),
    )(*_hbm(arrays), after)
    return res[0], res[1], list(res[2:2 + m]), res[-1]


def _split_wait(name, arrays, send, recv, after, await_all):
    m = len(arrays)

    def body(*refs):
        await_all(refs[:m], refs[m], refs[m + 1])

    res = pl.pallas_call(
        body, name=name, in_specs=[HBM_SPEC] * m + [SEM_SPEC, SEM_SPEC, ANY_SPEC], out_specs=[HBM_SPEC] * m,
        out_shape=_hbm_like(arrays), input_output_aliases={t: t for t in range(m)}, **SPLIT_PARAMS,
    )(*arrays, send, recv, after)
    return list(res)


def _sibling_copies(refs, send, recv, n):
    x, y, c = _position()
    cps = []
    for t in range(n):
        rh = refs[t].shape[1] // 2
        cps.append(pltpu.make_async_remote_copy(
            src_ref=refs[t].at[pl.ds(0, N_CHIPS), pl.ds((1 - c) * rh, rh)], dst_ref=refs[n + t],
            send_sem=send.at[t], recv_sem=recv.at[t], device_id=(x, y, 1 - c), device_id_type=MESH))
    return cps


def _reduce_sibling_start(grads, after, name):
    n = len(grads)
    lands = [lax.empty((N_CHIPS, g.shape[1] // 2, g.shape[2]), BF16) for g in grads]

    def issue(refs, send, recv):
        for cp in _sibling_copies(refs, send, recv, n):
            cp.start()

    return _split_start(name, list(grads) + lands, n, after, issue)


def _reduce_sibling_wait(send, recv, arrays, after, name):
    n = len(arrays) // 2

    def await_all(refs, send_ref, recv_ref):
        for cp in _sibling_copies(refs, send_ref, recv_ref, n):
            cp.wait_send()
            cp.wait_recv()

    res = _split_wait(name, arrays, send, recv, after, await_all)
    return res[:n], res[n:]


def _add_sibling_half(grad, got, dev_idx, name):
    j, r, cols = grad.shape
    rh = r // 2
    tr = rh
    nb = rh // tr

    def body(idx_ref, g_ref, got_ref, out_ref):
        out_ref[...] = (g_ref[...].astype(F32) + got_ref[...].astype(F32)).astype(BF16)

    return pl.pallas_call(
        body, name=name,
        grid_spec=pltpu.PrefetchScalarGridSpec(
            num_scalar_prefetch=1, grid=(j, nb),
            in_specs=[pl.BlockSpec((None, tr, cols), lambda jj, i, idx: (jj, idx[2] * nb + i, 0)),
                      pl.BlockSpec((None, tr, cols), lambda jj, i, idx: (jj, i, 0))],
            out_specs=pl.BlockSpec((None, tr, cols), lambda jj, i, idx: (jj, i, 0))),
        out_shape=jax.ShapeDtypeStruct((j, rh, cols), BF16),
        compiler_params=_params("parallel", "parallel"),
    )(dev_idx, grad, got)


def _chip_copies(refs, send, recv, n, receiving):
    x, y, c = _position()
    chip = 2 * x + y
    cps = []
    for t in range(n):
        for k in range(1, N_CHIPS):
            px, py = _chip_peer(x, y, k)
            cps.append(pltpu.make_async_remote_copy(
                src_ref=refs[t].at[2 * px + py], dst_ref=refs[n + t].at[2 * px + py if receiving else chip],
                send_sem=send.at[3 * t + k - 1], recv_sem=recv.at[3 * t + k - 1],
                device_id=(px, py, c), device_id_type=MESH))
    return cps


def _reduce_chips_start(partials, after, name):
    n = len(partials)
    lands = [lax.empty(p.shape, BF16) for p in partials]

    def issue(refs, send, recv):
        for cp in _chip_copies(refs, send, recv, n, False):
            cp.start()

    return _split_start(name, list(partials) + lands, 3 * n, after, issue)


def _reduce_chips_wait(send, recv, arrays, after, name):
    n = len(arrays) // 2

    def await_all(refs, send_ref, recv_ref):
        for cp in _chip_copies(refs, send_ref, recv_ref, n, True):
            cp.wait_send()
            cp.wait_recv()

    res = _split_wait(name, arrays, send, recv, after, await_all)
    return res[:n], res[n:]


def _sum_partials(land, partial, dev_idx, name):
    _, rh, cols = land.shape
    tr = min(rh, 256)
    nb = rh // tr

    def body(idx_ref, l_ref, p_ref, o_ref):
        chip = idx_ref[1]
        acc = jnp.where(chip == 0, p_ref[...], l_ref[0]).astype(F32)
        for s in range(1, N_CHIPS):
            acc = acc + jnp.where(chip == s, p_ref[...], l_ref[s]).astype(F32)
        o_ref[...] = acc

    return pl.pallas_call(
        body, name=name,
        grid_spec=pltpu.PrefetchScalarGridSpec(
            num_scalar_prefetch=1, grid=(nb,),
            in_specs=[pl.BlockSpec((N_CHIPS, tr, cols), lambda i, idx: (0, i, 0)),
                      pl.BlockSpec((None, tr, cols), lambda i, idx: (idx[1], i, 0))],
            out_specs=pl.BlockSpec((tr, cols), lambda i, idx: (idx[2] * nb + i, 0))),
        out_shape=jax.ShapeDtypeStruct((2 * rh, cols), F32), compiler_params=_params("parallel"),
    )(dev_idx, land, partial)


def _half_copies(refs, send, recv, receiving):
    x, y, c = _position()
    cps = []
    for t, ref in enumerate(refs):
        rh = ref.shape[0] // 2
        cps.append(pltpu.make_async_remote_copy(
            src_ref=ref.at[pl.ds(c * rh, rh)], dst_ref=ref.at[pl.ds(((1 - c) if receiving else c) * rh, rh)],
            send_sem=send.at[t], recv_sem=recv.at[t], device_id=(x, y, 1 - c), device_id_type=MESH))
    return cps


def _share_halves_start(totals, after, name):
    def issue(refs, send, recv):
        for cp in _half_copies(refs, send, recv, False):
            cp.start()

    return _split_start(name, list(totals), len(totals), after, issue)


def _share_halves_wait(send, recv, totals, after, name):
    def await_all(refs, send_ref, recv_ref):
        for cp in _half_copies(refs, send_ref, recv_ref, True):
            cp.wait_send()
            cp.wait_recv()

    return _split_wait(name, totals, send, recv, after, await_all)


SMALL_ROWS = 56


def _small_copies(refs, send, recv, receiving):
    x, y, c = _position()
    me = 4 * x + 2 * y + c
    cps = []
    for k in range(1, N_DEV):
        px, py, pc = _xor_peer(x, y, c, k)
        cps.append(pltpu.make_async_remote_copy(
            src_ref=refs[0], dst_ref=refs[1].at[4 * px + 2 * py + pc if receiving else me],
            send_sem=send.at[k - 1], recv_sem=recv.at[k - 1], device_id=(px, py, pc), device_id_type=MESH))
    return cps


def _small_gather_start(packed, after):
    land = lax.empty((N_DEV,) + packed.shape, F32)

    def issue(refs, send, recv):
        for cp in _small_copies(refs, send, recv, False):
            cp.start()

    return _split_start("small_gather_start", [packed, land], N_DEV - 1, after, issue)


def _small_gather_wait(send, recv, arrays, after):
    def await_all(refs, send_ref, recv_ref):
        for cp in _small_copies(refs, send_ref, recv_ref, True):
            cp.wait_send()
            cp.wait_recv()

    return _split_wait("small_gather_wait", arrays, send, recv, after, await_all)


def _reduce_small(packed, land, silu_c):
    ns = 3 * D_MODEL // N_CHIPS

    def body(p_ref, land_ref, sc_ref, tot_ref, gw_ref, loss_ref, qk_ref, allp):
        x, y, c = _position()
        me = 4 * x + 2 * y + c
        chip = 2 * x + y
        for i in range(N_DEV):
            allp[i] = jnp.where(me == i, p_ref[...], land_ref[i])
        tot = allp[0]
        for i in range(1, N_DEV):
            tot = tot + allp[i]
        tot_ref[...] = tot
        loss_ref[...] = jnp.sum(tot[11:12, :], axis=1, keepdims=True) * (0.5 / D_MODEL)
        fold = tot[5:11, 0:HEAD_DIM]
        for h in range(1, N_HEADS):
            fold = fold + tot[5:11, h * HEAD_DIM:(h + 1) * HEAD_DIM]
        qk_ref[...] = jnp.concatenate([fold, jnp.zeros((2, HEAD_DIM), F32)], axis=0)
        sct = sc_ref[...].T
        rc = 64
        for l in range(2):
            dms = [allp[i, pl.ds(12 + 4 * l + chip, 1), :][:, :ns] for i in range(N_DEV)]
            for r0 in range(0, D_MODEL, rc):
                acc = sct[r0:r0 + rc, 0:1] * dms[0]
                for i in range(1, N_DEV):
                    acc = acc + sct[r0:r0 + rc, i:i + 1] * dms[i]
                gw_ref[l, r0:r0 + rc, :] = acc

    vm = pl.BlockSpec(memory_space=pltpu.VMEM)
    return pl.pallas_call(
        body, name="reduce_small", in_specs=[vm, vm, vm], out_specs=[vm] * 4,
        out_shape=[jax.ShapeDtypeStruct((SMALL_ROWS, D_MODEL), F32), jax.ShapeDtypeStruct((2, D_MODEL, ns), F32),
                   jax.ShapeDtypeStruct((1, 1), F32), jax.ShapeDtypeStruct((8, HEAD_DIM), F32)],
        scratch_shapes=[pltpu.VMEM((N_DEV, SMALL_ROWS, D_MODEL), F32)],
        compiler_params=pltpu.CompilerParams(vmem_limit_bytes=VMEM_LIMIT_BYTES),
    )(packed, land, silu_c)


def kernel(x, c, norm_g, ada_w, ada_b, a_w_in, a_conv_w, a_conv_b, a_ln_g, a_ln_b, a_w_out, b_w_in, b_q_norm, b_k_norm, b_w_out, loss_target, m_norm_g, m_ada_w, m_ada_b, m_a_w_in, m_a_conv_w, m_a_conv_b, m_a_ln_g, m_a_ln_b, m_a_w_out, m_b_w_in, m_b_q_norm, m_b_k_norm, m_b_w_out, v_norm_g, v_ada_w, v_ada_b, v_a_w_in, v_a_conv_w, v_a_conv_b, v_a_ln_g, v_a_ln_b, v_a_w_out, v_b_w_in, v_b_q_norm, v_b_k_norm, v_b_w_out):
    chip = 2 * lax.axis_index("x") + lax.axis_index("y")
    core = lax.axis_index("c")
    chip_idx = chip.astype(jnp.int32).reshape(1)
    dev_idx = jnp.stack([2 * chip + core, chip, core]).astype(jnp.int32)

    land_a_in, own_wa_in = _cast_into_slot(a_w_in[0], chip_idx, "cast_a_w_in", keep_own=True)
    lands_a = [land_a_in, _cast_into_slot(a_w_out[0], chip_idx, "cast_a_w_out")]
    mods, silu_c, conv_w_full = _ada_forward(c, ada_w, ada_b, a_conv_w[0], after=tuple(lands_a))
    send_a, recv_a, lands_a, token_a = _gather_start(lands_a, mods, "gather_start_a")
    land_b_in, own_wb_in = _cast_into_slot(b_w_in[0], chip_idx, "cast_b_w_in", keep_own=True, after=token_a)
    lands_b = [land_b_in, _cast_into_slot(b_w_out[0], chip_idx, "cast_b_w_out", after=token_a)]
    send_b, recv_b, lands_b, token_b = _gather_start(lands_b, token_a, "gather_start_b")
    mods = mods + token_b[0:2, 0:1]

    def weights_a(after):
        send, recv, lands, _ = _gather_forward(send_a, recv_a, lands_a, after, "gather_forward_a")
        w_in, w_out = _gather_wait(send, recv, lands, after, "gather_wait_a")
        return w_in, w_out.reshape(D_MODEL, D_MODEL)

    forwarded_b = []

    def weights_b(after):
        send, recv, lands, _ = forwarded_b
        w_in, w_out = _gather_wait(send, recv, lands, after, "gather_wait_b")
        return w_in, w_out.reshape(D_MODEL, D_MODEL)

    def forward_weights_b(after):
        forwarded_b.extend(_gather_forward(send_b, recv_b, lands_b, after, "gather_forward_b"))

    stage1, stage2 = {}, {}

    def send_grads(tag, dw_in, dw_out):
        grads = [dw_in, dw_out.reshape(N_CHIPS, D_MODEL // N_CHIPS, D_MODEL)]
        send, recv, arrays, token = _reduce_sibling_start(grads, dw_out, f"reduce_d2d_start_{tag}")
        stage1[tag] = (send, recv, arrays)
        return token

    def forward_grads(tag, after):
        send, recv, arrays = stage1[tag]
        grads, got = _reduce_sibling_wait(send, recv, arrays, after, f"reduce_d2d_wait_{tag}")
        partials = [_add_sibling_half(grads[i], got[i], dev_idx, f"reduce_add_{tag}_{i}") for i in range(2)]
        send, recv, arrays, token = _reduce_chips_start(partials, partials[1], f"reduce_ici_start_{tag}")
        stage2[tag] = (send, recv, arrays)
        return token

    stage3 = {}

    def sum_grads(tag, after):
        send, recv, arrays = stage2[tag]
        partials, lands = _reduce_chips_wait(send, recv, arrays, after, f"reduce_ici_wait_{tag}")
        totals = [_sum_partials(lands[i], partials[i], dev_idx, f"reduce_sum_{tag}_{i}") for i in range(2)]
        send, recv, totals, token = _share_halves_start(totals, totals[1], f"reduce_share_start_{tag}")
        stage3[tag] = (send, recv, totals)
        return token

    def finish_grads(tag, after):
        send, recv, totals = stage3[tag]
        return _share_halves_wait(send, recv, totals, after, f"reduce_share_wait_{tag}")

    grad_x, small = _local_step(
        x[0], loss_target[0], mods.reshape(2, 3, D_MODEL), norm_g, conv_w_full, a_conv_b, a_ln_g[0:1],
        a_ln_b[0:1], b_q_norm[0], b_k_norm[0], chip.astype(jnp.int32), own_wa_in, own_wb_in,
        weights_a, weights_b, forward_weights_b,
        functools.partial(send_grads, "b"), functools.partial(forward_grads, "b"), functools.partial(send_grads, "a"))

    ns = 3 * D_MODEL // N_CHIPS
    pad_mod = lambda dm: jnp.pad(dm.reshape(N_CHIPS, ns), ((0, 0), (0, D_MODEL - ns)))
    packed = jnp.concatenate([
        small["dnorm_g"], small["dconv_b"], small["dln_g"], small["dln_b"], small["dq_norm"], small["dk_norm"],
        small["loss_cols"], pad_mod(small["dmod0"]), pad_mod(small["dmod1"]), small["dconv_w"],
        jnp.zeros((SMALL_ROWS - 20 - CONV_WIDTH, D_MODEL), F32)], axis=0)
    send_s, recv_s, small_arrays, token_s = _small_gather_start(packed, packed)

    given = dict(norm_g=(norm_g, m_norm_g, v_norm_g), ada_w=(ada_w, m_ada_w, v_ada_w), ada_b=(ada_b, m_ada_b, v_ada_b),
                 a_w_in=(a_w_in, m_a_w_in, v_a_w_in), a_conv_w=(a_conv_w, m_a_conv_w, v_a_conv_w),
                 a_conv_b=(a_conv_b, m_a_conv_b, v_a_conv_b), a_ln_g=(a_ln_g, m_a_ln_g, v_a_ln_g),
                 a_ln_b=(a_ln_b, m_a_ln_b, v_a_ln_b), a_w_out=(a_w_out, m_a_w_out, v_a_w_out),
                 b_w_in=(b_w_in, m_b_w_in, v_b_w_in), b_q_norm=(b_q_norm, m_b_q_norm, v_b_q_norm),
                 b_k_norm=(b_k_norm, m_b_k_norm, v_b_k_norm), b_w_out=(b_w_out, m_b_w_out, v_b_w_out))
    order = ["norm_g", "ada_w", "ada_b", "a_w_in", "a_conv_w", "a_conv_b", "a_ln_g", "a_ln_b", "a_w_out", "b_w_in",
             "b_q_norm", "b_k_norm", "b_w_out"]
    outs = {}

    def update(k, g2, after=None, copy_grad=False):
        w, m, v = given[k]
        shape2 = g2.shape
        res = _adamw(w.reshape(shape2), g2, m.reshape(shape2), v.reshape(shape2), f"adamw_{k}", after, copy_grad)
        outs[k] = tuple(a.reshape(w.shape) for a in ((res[3] if copy_grad else g2), res[0], res[1], res[2]))

    token = forward_grads("a", token_s)
    token = sum_grads("b", token)
    packed, land = _small_gather_wait(send_s, recv_s, small_arrays, token)
    tot, g_ada_w, loss, qk = _reduce_small(packed, land, silu_c)
    g_b_in, g_b_out = finish_grads("b", tot)
    update("b_w_in", g_b_in, copy_grad=True)
    update("b_w_out", g_b_out, copy_grad=True)
    token = sum_grads("a", outs["b_w_in"][1])
    cw = D_MODEL // N_CHIPS
    g_small = dict(
        norm_g=tot[0:2], a_conv_b=tot[2:3], a_ln_g=tot[3:4], a_ln_b=tot[4:5],
        b_q_norm=qk[0:3], b_k_norm=qk[3:6],
        ada_b=jnp.stack([tot[12:16, :ns].reshape(3 * D_MODEL), tot[16:20, :ns].reshape(3 * D_MODEL)]),
        a_conv_w=lax.dynamic_slice(tot[20:20 + CONV_WIDTH], (0, chip * cw), (CONV_WIDTH, cw)),
    )
    update("ada_w", g_ada_w.reshape(2 * D_MODEL, ns), after=token)
    for k, g2 in g_small.items():
        update(k, g2, after=token)
    g_a_in, g_a_out = finish_grads("a", outs["ada_w"][1])
    update("a_w_in", g_a_in, copy_grad=True)
    update("a_w_out", g_a_out, copy_grad=True)
    return (loss.reshape(()), grad_x[None], *[outs[k][0] for k in order], *[outs[k][1] for k in order],
            *[outs[k][2] for k in order], *[outs[k][3] for k in order])
```

```python
import functools

import jax
import jax.numpy as jnp
from jax import lax
from jax.experimental import pallas as pl
from jax.experimental.pallas import tpu as pltpu

F32 = jnp.float32
BF16 = jnp.bfloat16

SEQ = 2048
D_MODEL = 1024
CONV_WIDTH = 31
HEAD_DIM = 64
N_HEADS = 16
DILATIONS = (1, 4, 16)
ATTN_BLOCK = 128
NORM_EPS = 1e-6
NEG_INF = -1e30
N_DEV = 8
N_CHIPS = 4

ADAM_LR = 0.001
ADAM_B1 = 0.9
ADAM_B2 = 0.999
ADAM_EPS = 1e-08
ADAM_WD = 0.01
ADAM_STEP = 10

VMEM_LIMIT_BYTES = 52 * 1024 * 1024
HALO = 32
LANES = 128
ROW_TILE = 512
MESH = pl.DeviceIdType.MESH


def _params(*sem):
    return pltpu.CompilerParams(dimension_semantics=sem or None, vmem_limit_bytes=VMEM_LIMIT_BYTES)


def _sigmoid(v):
    return 1.0 / (1.0 + jnp.exp(-v))


def _row_spec(tm, cols, col_block=0):
    return pl.BlockSpec((tm, cols), lambda i: (i, col_block))


def _vec_spec(rows, cols):
    return pl.BlockSpec((rows, cols), lambda i: (0, 0))


def _normmod(xv, g, scale, shift):
    r = lax.rsqrt(jnp.mean(xv * xv, axis=-1, keepdims=True) + NORM_EPS)
    return xv * r * g * (1.0 + scale) + shift


def _normmod_fwd(x, g, scale, shift, name):
    tm = ROW_TILE

    def body(x_ref, g_ref, sc_ref, sh_ref, h_ref, ht_ref):
        h = _normmod(x_ref[...], g_ref[...], sc_ref[...], sh_ref[...])
        h_ref[...] = h.astype(BF16)
        ht_ref[...] = h.T.astype(BF16)

    return pl.pallas_call(
        body, name=name, grid=(SEQ // tm,),
        in_specs=[_row_spec(tm, D_MODEL)] + [_vec_spec(1, D_MODEL)] * 3,
        out_specs=[_row_spec(tm, D_MODEL), pl.BlockSpec((D_MODEL, tm), lambda i: (0, i))],
        out_shape=[jax.ShapeDtypeStruct((SEQ, D_MODEL), BF16), jax.ShapeDtypeStruct((D_MODEL, SEQ), BF16)],
        compiler_params=_params("parallel"),
    )(x, g, scale, shift)


def _normmod_bwd(x, g, scale, dh_parts, dres, name, part_dilations=None, gated=None):
    tm = ROW_TILE
    n_parts = len(dh_parts)
    dils = part_dilations or (1,) * n_parts
    dh_parts = [p if d == 1 else p.reshape(d, SEQ // d, D_MODEL) for p, d in zip(dh_parts, dils)]
    n_gated = 0 if gated is None else 2

    def body(x_ref, g_ref, sc_ref, dres_ref, *rest):
        part_refs = rest[:n_parts]
        gated_refs = rest[n_parts:n_parts + n_gated]
        out_refs = rest[n_parts + n_gated:]
        dx_ref, sums_ref, nat = out_refs[0], out_refs[1], out_refs[-1]
        xv = x_ref[...]
        r = lax.rsqrt(jnp.mean(xv * xv, axis=-1, keepdims=True) + NORM_EPS)
        xn = xv * r
        dh = _load_natural(part_refs[0], nat, dils[0])
        for p, d in zip(part_refs[1:], dils[1:]):
            dh = dh + _load_natural(p, nat, d)
        gv = g_ref[...]
        one_sc = 1.0 + sc_ref[...]
        dxn = dh * (gv * one_sc)
        dx = dres_ref[...] + r * (dxn - xn * jnp.mean(dxn * xn, axis=-1, keepdims=True))
        dx_ref[...] = dx
        dhx = dh * xn
        rows = [jnp.sum(dhx, axis=0, keepdims=True) * one_sc,
                jnp.sum(dhx, axis=0, keepdims=True) * gv,
                jnp.sum(dh, axis=0, keepdims=True)]
        if gated is not None:
            gate_ref, y_ref = gated_refs
            out_refs[2][...] = (dx * gate_ref[...]).astype(BF16)
            rows.append(jnp.sum(dx * y_ref[...].astype(F32), axis=0, keepdims=True))
        sums = jnp.concatenate(rows + [jnp.zeros((8 - len(rows), D_MODEL), F32)], axis=0)

        @pl.when(pl.program_id(0) == 0)
        def _():
            sums_ref[...] = jnp.zeros_like(sums_ref)

        sums_ref[...] += sums

    gated_specs = [] if gated is None else [_vec_spec(1, D_MODEL), _row_spec(tm, D_MODEL)]
    dy_spec = [] if gated is None else [_row_spec(tm, D_MODEL)]
    dy_shape = [] if gated is None else [jax.ShapeDtypeStruct((SEQ, D_MODEL), BF16)]
    return pl.pallas_call(
        body, name=name, grid=(SEQ // tm,),
        in_specs=[_row_spec(tm, D_MODEL), _vec_spec(1, D_MODEL), _vec_spec(1, D_MODEL), _row_spec(tm, D_MODEL)]
        + [_class_spec(tm, d) for d in dils] + gated_specs,
        out_specs=[_row_spec(tm, D_MODEL), _vec_spec(8, D_MODEL)] + dy_spec,
        out_shape=[jax.ShapeDtypeStruct((SEQ, D_MODEL), F32), jax.ShapeDtypeStruct((8, D_MODEL), F32)] + dy_shape,
        scratch_shapes=[_natural_scratch(tm)],
        compiler_params=_params("arbitrary"),
    )(x, g, scale, dres, *dh_parts, *(gated or ()))


def _mm(lhs, rhs, *, tn, tile0, n_tiles, out_dtype, name, out3d=None, prev=None, transpose_lhs=False):
    mo, kc = lhs.shape[::-1] if transpose_lhs else lhs.shape
    cm = min(mo, 1024)
    tc = 256

    def body(l_ref, r_ref, *rest):
        if transpose_lhs:
            o_ref, lt_ref = rest[-2], rest[-1]

            @pl.when(pl.program_id(0) == 0)
            def _():
                for c in range(kc // tc):
                    lt_ref[:, c * tc:(c + 1) * tc] = l_ref[c * tc:(c + 1) * tc, :].astype(F32).T.astype(l_ref.dtype)
        else:
            o_ref, lt_ref = rest[-1], l_ref
        for m in range(mo // cm):
            rows = pl.ds(m * cm, cm)
            o_ref[rows, :] = jnp.dot(lt_ref[rows, :], r_ref[...], preferred_element_type=F32).astype(out_dtype)

    if rhs.ndim == 3:
        tps_r = rhs.shape[2] // tn
        r_spec = pl.BlockSpec((None, kc, tn), lambda t: ((tile0 + t) // tps_r, 0, (tile0 + t) % tps_r))
    else:
        r_spec = pl.BlockSpec((kc, tn), lambda t: (0, t))
    in_specs = [pl.BlockSpec(lhs.shape, lambda t: (0, 0)), r_spec]
    args = [lhs, rhs]
    aliases = {}
    if out3d is None:
        o_spec = pl.BlockSpec((mo, tn), lambda t: (0, t))
        o_shape = jax.ShapeDtypeStruct((mo, n_tiles * tn), out_dtype)
    else:
        j_out, ns_out = out3d
        tps_o = ns_out // tn
        o_spec = pl.BlockSpec((None, mo, tn), lambda t: ((tile0 + t) // tps_o, 0, (tile0 + t) % tps_o))
        o_shape = jax.ShapeDtypeStruct((j_out, mo, ns_out), out_dtype)
        if prev is not None:
            in_specs.append(pl.BlockSpec(memory_space=pl.ANY))
            args.append(prev)
            aliases = {2: 0}
    return pl.pallas_call(
        body, name=name, grid=(n_tiles,), in_specs=in_specs, out_specs=o_spec, out_shape=o_shape,
        input_output_aliases=aliases,
        scratch_shapes=[pltpu.VMEM((mo, kc), lhs.dtype)] if transpose_lhs else [],
        compiler_params=_params("arbitrary" if transpose_lhs else "parallel"),
    )(*args)


def _in_tiles(h_parts, w3, tile_ids, n_tiles, *, tn, total_tiles, part_of, name, prev=None):
    _, kc, ns = w3.shape
    tps = ns // tn
    cm = 1024
    n_parts = len(h_parts)

    def body(ids_ref, *rest):
        h_refs, w_ref, o_ref = rest[:n_parts], rest[n_parts], rest[-1]
        part = part_of(ids_ref[1, pl.program_id(0)])
        for g, h_ref in enumerate(h_refs):
            @pl.when(part == g)
            def _():
                for m in range(SEQ // cm):
                    rows = pl.ds(m * cm, cm)
                    o_ref[rows, :] = jnp.dot(h_ref[rows, :], w_ref[...], preferred_element_type=F32).astype(BF16)

    resident = pl.BlockSpec((SEQ, kc), lambda t, ids: (0, 0))
    in_specs = [resident] * n_parts + [
        pl.BlockSpec((None, kc, tn), lambda t, ids: (ids[0, t] // tps, 0, ids[0, t] % tps))]
    args = [*h_parts, w3]
    aliases = {}
    if prev is not None:
        in_specs.append(pl.BlockSpec(memory_space=pl.ANY))
        args.append(prev)
        aliases = {n_parts + 2: 0}
    return pl.pallas_call(
        body, name=name,
        grid_spec=pltpu.PrefetchScalarGridSpec(
            num_scalar_prefetch=1, grid=(n_tiles,), in_specs=in_specs,
            out_specs=pl.BlockSpec((SEQ, tn), lambda t, ids: (0, ids[1, t]))),
        out_shape=jax.ShapeDtypeStruct((SEQ, total_tiles * tn), BF16),
        input_output_aliases=aliases, compiler_params=_params("arbitrary"),
    )(tile_ids, *args)


def _own_first(chip, total_tiles):
    own = total_tiles // N_CHIPS
    step = jnp.arange(total_tiles, dtype=jnp.int32)
    tiles = (own * chip + step) % total_tiles
    return jnp.stack([step[:own], tiles[:own]]), jnp.stack([tiles[own:], tiles[own:]]), own


def _mm_nt(dy, w3, *, tn, tile0, n_tiles, name, after=None):
    m_rows = dy.shape[0]
    _, kc, ns = w3.shape
    tps = ns // tn
    cm = 512
    extra = [] if after is None else [after]

    def body(dy_ref, w_ref, *rest):
        o_ref, acc = rest[-2], rest[-1]
        t = pl.program_id(0)

        @pl.when(t == 0)
        def _():
            acc[...] = jnp.zeros_like(acc)

        for m in range(m_rows // cm):
            rows = pl.ds(m * cm, cm)
            acc[rows, :] += lax.dot_general(dy_ref[rows, :], w_ref[...], NT_DIMS, preferred_element_type=F32)

        @pl.when(t == n_tiles - 1)
        def _():
            o_ref[...] = acc[...].astype(BF16)

    return pl.pallas_call(
        body, name=name, grid=(n_tiles,),
        in_specs=[pl.BlockSpec((m_rows, tn), lambda t: (0, t)),
                  pl.BlockSpec((None, kc, tn), lambda t: ((tile0 + t) // tps, 0, (tile0 + t) % tps))]
        + [pl.BlockSpec(memory_space=pl.ANY)] * len(extra),
        out_specs=pl.BlockSpec((m_rows, kc), lambda t: (0, 0)),
        out_shape=jax.ShapeDtypeStruct((m_rows, kc), BF16),
        scratch_shapes=[pltpu.VMEM((m_rows, kc), F32)],
        compiler_params=_params("arbitrary"),
    )(dy, w3, *extra)


CONV_CHUNK = 16


def _shift_copies(buf, shifted):
    rows = shifted.shape[1]
    for s in range(1, 8):
        shifted[s - 1] = buf[pl.ds(s, rows), :]


def _shifted_rows(buf, shifted, offset, r0):
    s = offset % 8
    if s == 0:
        return buf[pl.ds(r0 + offset, CONV_CHUNK), :]
    return shifted[s - 1, pl.ds(r0 + (offset - s), CONV_CHUNK), :]


def _spread_taps(w_ref, taps):
    for k in range(CONV_WIDTH):
        taps[k] = jnp.broadcast_to(w_ref[k:k + 1, :], (8, D_MODEL))


def _times_tap(taps, k, rows):
    return (rows.reshape(CONV_CHUNK // 8, 8, D_MODEL) * taps[k][None]).reshape(CONV_CHUNK, D_MODEL)


def _conv_fwd(proj, conv_w, conv_b, ln_g, ln_b, name):
    tm = ROW_TILE
    hb = tm // HALO

    def body(vg_ref, halo_ref, z_ref, w_ref, b_ref, g_ref, be_ref, u5_ref, u5t_ref, u2_ref, buf, shifted, taps):
        i = pl.program_id(0)
        u1 = vg_ref[:, :D_MODEL].astype(F32) * _sigmoid(vg_ref[:, D_MODEL:].astype(F32))
        u1h = halo_ref[:, :D_MODEL].astype(F32) * _sigmoid(halo_ref[:, D_MODEL:].astype(F32))
        buf[pl.ds(0, HALO), :] = jnp.where(i > 0, u1h, 0.0)
        buf[pl.ds(HALO, tm), :] = u1
        _shift_copies(buf, shifted)
        _spread_taps(w_ref, taps)

        def chunk(ci, carry):
            r0 = pl.multiple_of(ci * CONV_CHUNK, CONV_CHUNK)
            acc = jnp.broadcast_to(b_ref[...], (CONV_CHUNK, D_MODEL))
            for k in range(CONV_WIDTH):
                acc = acc + _times_tap(taps, k, _shifted_rows(buf, shifted, HALO - (CONV_WIDTH - 1) + k, r0))
            u2_ref[pl.ds(r0, CONV_CHUNK), :] = acc
            return carry

        lax.fori_loop(0, tm // CONV_CHUNK, chunk, 0)
        acc = u2_ref[...]
        mu = jnp.mean(acc, axis=-1, keepdims=True)
        xc = acc - mu
        rstd = lax.rsqrt(jnp.mean(xc * xc, axis=-1, keepdims=True) + NORM_EPS)
        u3 = xc * rstd * g_ref[...] + be_ref[...]
        zv = z_ref[...].astype(F32)
        u5 = u3 * _sigmoid(u3) * (zv * _sigmoid(zv))
        u5_ref[...] = u5.astype(BF16)
        u5t_ref[...] = u5.T.astype(BF16)

    return pl.pallas_call(
        body, name=name, grid=(SEQ // tm,),
        in_specs=[pl.BlockSpec((tm, 2 * D_MODEL), lambda i: (i, 0)),
                  pl.BlockSpec((HALO, 2 * D_MODEL), lambda i: (jnp.maximum(i * hb - 1, 0), 0)),
                  _row_spec(tm, D_MODEL, 2),
                  _vec_spec(CONV_WIDTH, D_MODEL)] + [_vec_spec(1, D_MODEL)] * 3,
        out_specs=[_row_spec(tm, D_MODEL), pl.BlockSpec((D_MODEL, tm), lambda i: (0, i)), _row_spec(tm, D_MODEL)],
        out_shape=[jax.ShapeDtypeStruct((SEQ, D_MODEL), BF16), jax.ShapeDtypeStruct((D_MODEL, SEQ), BF16),
                   jax.ShapeDtypeStruct((SEQ, D_MODEL), F32)],
        scratch_shapes=[pltpu.VMEM((HALO + tm, D_MODEL), F32), pltpu.VMEM((7, HALO + tm - 8, D_MODEL), F32),
                        pltpu.VMEM((CONV_WIDTH, 8, D_MODEL), F32)],
        compiler_params=_params("parallel"),
    )(proj, proj, proj, conv_w, conv_b, ln_g, ln_b)


def _conv_bwd_pointwise(dy, w_out, proj, u2, ln_g, ln_b, name):
    tm = ROW_TILE

    def body(dy_ref, w_ref, z_ref, u2_ref, g_ref, be_ref, du2_ref, dz_ref, sums_ref):
        u2v = u2_ref[...]
        mu = jnp.mean(u2v, axis=-1, keepdims=True)
        xc = u2v - mu
        rstd = lax.rsqrt(jnp.mean(xc * xc, axis=-1, keepdims=True) + NORM_EPS)
        xhat = xc * rstd
        u3 = xhat * g_ref[...] + be_ref[...]
        s3 = _sigmoid(u3)
        u4 = u3 * s3
        zv = z_ref[...].astype(F32)
        sz = _sigmoid(zv)
        du5v = lax.dot_general(dy_ref[...], w_ref[...], NT_DIMS, preferred_element_type=F32)
        dz_ref[...] = du5v * u4 * (sz * (1.0 + zv * (1.0 - sz)))
        du3 = du5v * (zv * sz) * (s3 * (1.0 + u3 * (1.0 - s3)))
        dxhat = du3 * g_ref[...]
        du2 = rstd * (dxhat - jnp.mean(dxhat, axis=-1, keepdims=True)
                      - xhat * jnp.mean(dxhat * xhat, axis=-1, keepdims=True))
        du2_ref[...] = du2
        sums = jnp.concatenate([
            jnp.sum(du3 * xhat, axis=0, keepdims=True),
            jnp.sum(du3, axis=0, keepdims=True),
            jnp.sum(du2, axis=0, keepdims=True),
            jnp.zeros((5, D_MODEL), F32)], axis=0)

        @pl.when(pl.program_id(0) == 0)
        def _():
            sums_ref[...] = jnp.zeros_like(sums_ref)

        sums_ref[...] += sums

    return pl.pallas_call(
        body, name=name, grid=(SEQ // tm,),
        in_specs=[_row_spec(tm, D_MODEL), _vec_spec(D_MODEL, D_MODEL), _row_spec(tm, D_MODEL, 2),
                  _row_spec(tm, D_MODEL), _vec_spec(1, D_MODEL), _vec_spec(1, D_MODEL)],
        out_specs=[_row_spec(tm, D_MODEL), _row_spec(tm, D_MODEL), _vec_spec(8, D_MODEL)],
        out_shape=[jax.ShapeDtypeStruct((SEQ, D_MODEL), F32), jax.ShapeDtypeStruct((SEQ, D_MODEL), F32),
                   jax.ShapeDtypeStruct((8, D_MODEL), F32)],
        compiler_params=_params("arbitrary"),
    )(dy, w_out, proj, u2, ln_g, ln_b)


def _conv_bwd_taps(du2, dz, proj, conv_w, name):
    tm = ROW_TILE
    hb = tm // HALO
    n_blocks = SEQ // tm

    def body(du2_ref, dnext_ref, dz_ref, vg_ref, w_ref, dproj_ref, dw_ref, dbuf, dshift, sgbuf, ubuf, dwacc, taps):
        i = pl.program_id(0)
        _spread_taps(w_ref, taps)
        sg = _sigmoid(vg_ref[:, D_MODEL:].astype(F32))
        sgbuf[...] = sg
        ubuf[...] = vg_ref[:, :D_MODEL].astype(F32) * sg
        dbuf[pl.ds(0, tm), :] = du2_ref[...]
        dbuf[pl.ds(tm, HALO), :] = jnp.where(i < n_blocks - 1, dnext_ref[...], 0.0)
        _shift_copies(dbuf, dshift)

        @pl.when(i == 0)
        def _():
            dwacc[...] = jnp.zeros_like(dwacc)

        def chunk(ci, carry):
            r0 = pl.multiple_of(ci * CONV_CHUNK, CONV_CHUNK)
            rows = pl.ds(r0, CONV_CHUNK)
            u1c = ubuf[rows, :]
            du1 = jnp.zeros((CONV_CHUNK, D_MODEL), F32)
            for k in range(CONV_WIDTH):
                ahead = _shifted_rows(dbuf, dshift, CONV_WIDTH - 1 - k, r0)
                du1 = du1 + _times_tap(taps, k, ahead)
                prod = u1c * ahead
                dwacc[k] += prod[0:8] + prod[8:16]
            sgc = sgbuf[rows, :]
            dval = du1 * sgc
            dproj_ref[rows, 0:D_MODEL] = dval.astype(BF16)
            dproj_ref[rows, D_MODEL:2 * D_MODEL] = (
                dval * vg_ref[rows, 0:D_MODEL].astype(F32) * (1.0 - sgc)).astype(BF16)
            return carry

        lax.fori_loop(0, tm // CONV_CHUNK, chunk, 0)
        dproj_ref[:, 2 * D_MODEL:] = dz_ref[...].astype(BF16)

        @pl.when(i == n_blocks - 1)
        def _():
            for k in range(CONV_WIDTH):
                dw_ref[k:k + 1, :] = jnp.sum(dwacc[k], axis=0, keepdims=True)
            dw_ref[CONV_WIDTH:, :] = jnp.zeros((32 - CONV_WIDTH, D_MODEL), F32)

    return pl.pallas_call(
        body, name=name, grid=(n_blocks,),
        in_specs=[_row_spec(tm, D_MODEL),
                  pl.BlockSpec((HALO, D_MODEL), lambda i: (jnp.minimum((i + 1) * hb, SEQ // HALO - 1), 0)),
                  _row_spec(tm, D_MODEL),
                  pl.BlockSpec((tm, 2 * D_MODEL), lambda i: (i, 0)),
                  _vec_spec(CONV_WIDTH, D_MODEL)],
        out_specs=[_row_spec(tm, 3 * D_MODEL), _vec_spec(32, D_MODEL)],
        out_shape=[jax.ShapeDtypeStruct((SEQ, 3 * D_MODEL), BF16), jax.ShapeDtypeStruct((32, D_MODEL), F32)],
        scratch_shapes=[pltpu.VMEM((tm + HALO, D_MODEL), F32), pltpu.VMEM((7, HALO + tm - 8, D_MODEL), F32),
                        pltpu.VMEM((tm, D_MODEL), F32), pltpu.VMEM((tm, D_MODEL), F32),
                        pltpu.VMEM((CONV_WIDTH, 8, D_MODEL), F32), pltpu.VMEM((CONV_WIDTH, 8, D_MODEL), F32)],
        compiler_params=_params("arbitrary"),
    )(du2, du2, dz, proj, conv_w)


def _out_a(u5, w_out, x, gate, g1, scale1, shift1, name):
    tm = ROW_TILE
    n_d = len(DILATIONS)

    def body(u_ref, w_ref, x_ref, gate_ref, g_ref, sc_ref, sh_ref, x1_ref, y_ref, ht_ref, *rest):
        h_refs, nat = rest[:n_d], rest[-1]
        y = jnp.dot(u_ref[...], w_ref[...], preferred_element_type=F32)
        x1 = x_ref[...] + gate_ref[...] * y
        y_ref[...] = y.astype(BF16)
        x1_ref[...] = x1
        h = _normmod(x1, g_ref[...], sc_ref[...], sh_ref[...])
        ht_ref[...] = h.T.astype(BF16)
        for h_ref, d in zip(h_refs, DILATIONS):
            _store_classes(h_ref, h, nat, d)

    res = pl.pallas_call(
        body, name=name, grid=(SEQ // tm,),
        in_specs=[_row_spec(tm, D_MODEL), _vec_spec(D_MODEL, D_MODEL), _row_spec(tm, D_MODEL)]
        + [_vec_spec(1, D_MODEL)] * 4,
        out_specs=[_row_spec(tm, D_MODEL), _row_spec(tm, D_MODEL), pl.BlockSpec((D_MODEL, tm), lambda i: (0, i))]
        + [_class_spec(tm, d) for d in DILATIONS],
        out_shape=[jax.ShapeDtypeStruct((SEQ, D_MODEL), F32), jax.ShapeDtypeStruct((SEQ, D_MODEL), BF16),
                   jax.ShapeDtypeStruct((D_MODEL, SEQ), BF16)] + [_class_shape(d, BF16) for d in DILATIONS],
        scratch_shapes=[_natural_scratch(tm)],
        compiler_params=_params("parallel"),
    )(u5, w_out, x, gate, g1, scale1, shift1)
    return res[0], res[1], res[2], [a.reshape(SEQ, D_MODEL) for a in res[3:]]


def _out_b_loss(u, w_out, x1, gate, target, name):
    tm = ROW_TILE

    def body(u_ref, w_ref, x_ref, gate_ref, t_ref, e_ref, dy_ref, sums_ref):
        y = jnp.dot(u_ref[...], w_ref[...], preferred_element_type=F32)
        diff = x_ref[...] + gate_ref[...] * y - t_ref[...]
        e = diff * (1.0 / D_MODEL)
        e_ref[...] = e
        dy_ref[...] = (e * gate_ref[...]).astype(BF16)
        sums = jnp.concatenate([
            jnp.sum(e * y, axis=0, keepdims=True),
            jnp.sum(diff * diff, axis=0, keepdims=True),
            jnp.zeros((6, D_MODEL), F32)], axis=0)

        @pl.when(pl.program_id(0) == 0)
        def _():
            sums_ref[...] = jnp.zeros_like(sums_ref)

        sums_ref[...] += sums

    return pl.pallas_call(
        body, name=name, grid=(SEQ // tm,),
        in_specs=[_row_spec(tm, D_MODEL), _vec_spec(D_MODEL, D_MODEL), _row_spec(tm, D_MODEL),
                  _vec_spec(1, D_MODEL), _row_spec(tm, D_MODEL)],
        out_specs=[_row_spec(tm, D_MODEL), _row_spec(tm, D_MODEL), _vec_spec(8, D_MODEL)],
        out_shape=[jax.ShapeDtypeStruct((SEQ, D_MODEL), F32), jax.ShapeDtypeStruct((SEQ, D_MODEL), BF16),
                   jax.ShapeDtypeStruct((8, D_MODEL), F32)],
        compiler_params=_params("arbitrary"),
    )(u, w_out, x1, gate, target)


def _seg_matrix():
    r = lax.broadcasted_iota(jnp.int32, (256, 256), 0) // HEAD_DIM
    c = lax.broadcasted_iota(jnp.int32, (256, 256), 1) // HEAD_DIM
    return jnp.where(r == c, 1.0 / HEAD_DIM, 0.0).astype(BF16)


def _segmean(v, seg):
    hi = v.astype(BF16)
    lo = (v - hi.astype(F32)).astype(BF16)
    outs = []
    for c0 in range(0, D_MODEL, 256):
        outs.append(jnp.dot(hi[:, c0:c0 + 256], seg, preferred_element_type=F32)
                    + jnp.dot(lo[:, c0:c0 + 256], seg, preferred_element_type=F32))
    return jnp.concatenate(outs, axis=1)


def _qk_rstd(v, seg):
    return lax.rsqrt(_segmean(v * v, seg) + NORM_EPS)


def _qknorm_fwd(proj, group, qw, kw, seg, name):
    tm = ROW_TILE

    def body(q_in, k_in, qw_ref, kw_ref, seg_ref, q_ref, k_ref):
        segv = seg_ref[...]
        q = q_in[...].astype(F32)
        k = k_in[...].astype(F32)
        q_ref[...] = (q * _qk_rstd(q, segv) * qw_ref[...] * HEAD_DIM ** -0.5).astype(BF16)
        k_ref[...] = (k * _qk_rstd(k, segv) * kw_ref[...]).astype(BF16)

    return pl.pallas_call(
        body, name=name, grid=(SEQ // tm,),
        in_specs=[_row_spec(tm, D_MODEL, 3 * group), _row_spec(tm, D_MODEL, 3 * group + 1),
                  _vec_spec(1, D_MODEL), _vec_spec(1, D_MODEL), _vec_spec(256, 256)],
        out_specs=[_row_spec(tm, D_MODEL)] * 2,
        out_shape=[jax.ShapeDtypeStruct((SEQ, D_MODEL), BF16)] * 2,
        compiler_params=_params("parallel"),
    )(proj, proj, qw, kw, seg)


def _attn_masks(b, bpc, dilation, transposed=False):
    keys = ATTN_BLOCK if bpc == 1 else 2 * ATTN_BLOCK
    shape, q_axis = ((keys, ATTN_BLOCK), 1) if transposed else ((ATTN_BLOCK, keys), 0)
    qi = lax.broadcasted_iota(jnp.int32, shape, q_axis)
    kj = lax.broadcasted_iota(jnp.int32, shape, 1 - q_axis)
    if bpc == 1:
        steps = qi - kj
        return (steps * dilation).astype(F32), steps >= 0
    steps = qi + ATTN_BLOCK - kj
    has_prev = (b % bpc) != 0
    valid = (steps >= 0) & (steps <= ATTN_BLOCK) & (has_prev | (kj >= ATTN_BLOCK))
    return (steps * dilation).astype(F32), valid


MASKED = 1e30


def _bias_scratch(bpc):
    return pltpu.VMEM((1 if bpc == 1 else 2, N_HEADS, ATTN_BLOCK, (1 if bpc == 1 else 2) * ATTN_BLOCK), F32)


def _fill_bias(bias_ref, sl_ref, bpc, dilation):
    for variant in range(bias_ref.shape[0]):
        dist, valid = _attn_masks(variant, min(bpc, 2), dilation)
        bias_ref[variant] = jnp.where(valid[None], dist[None] * sl_ref[...], MASKED)


def _step_bias(bias_ref, b, bpc):
    if bpc == 1:
        return bias_ref[0]
    return bias_ref[jnp.where((b % bpc) != 0, 1, 0)]


def _key_tile(prev_ref, cur_ref, cols, bpc):
    if bpc == 1:
        return cur_ref[:, cols]
    return jnp.concatenate([prev_ref[:, cols], cur_ref[:, cols]], axis=0)


ATTN_HEADS_FWD = 16
ATTN_HEADS_BWD = 16
NT_DIMS = (((1,), (1,)), ((), ()))
BATCH_NT_DIMS = (((2,), (2,)), ((0,), (0,)))
BATCH_NN_DIMS = (((2,), (1,)), ((0,), (0,)))
BATCH_TN_DIMS = (((1,), (1,)), ((0,), (0,)))


def _head_stack(tile_of, heads):
    return jnp.stack([tile_of(slice(h * HEAD_DIM, (h + 1) * HEAD_DIM)) for h in range(heads)], axis=0)


def _attn_specs(heads, segment=0):
    width = heads * HEAD_DIM
    off = segment * (D_MODEL // width)
    last = SEQ // ATTN_BLOCK - 1
    cur = pl.BlockSpec((ATTN_BLOCK, width), lambda hg, b: (jnp.minimum(b, last), hg + off))
    prev = pl.BlockSpec((ATTN_BLOCK, width), lambda hg, b: (jnp.clip(b - 1, 0, last), hg + off))
    return cur, prev


def _attn_fwd(q, k, proj, group, slopes, dilation, name):
    bpc = SEQ // dilation // ATTN_BLOCK
    heads = ATTN_HEADS_FWD
    assert heads == N_HEADS
    cur, prev = _attn_specs(heads)
    v_cur, v_prev = _attn_specs(heads, segment=3 * group + 2)

    def body(sl_ref, q_ref, kp_ref, kc_ref, vp_ref, vc_ref, o_ref, lse_ref, bias_ref):
        b = pl.program_id(1)

        @pl.when(b == 0)
        def _():
            _fill_bias(bias_ref, sl_ref, bpc, dilation)

        q3 = _head_stack(lambda cols: q_ref[:, cols], heads)
        k3 = _head_stack(lambda cols: _key_tile(kp_ref, kc_ref, cols, bpc), heads)
        v3 = _head_stack(lambda cols: _key_tile(vp_ref, vc_ref, cols, bpc), heads)
        s = lax.dot_general(q3, k3, BATCH_NT_DIMS, preferred_element_type=F32)
        s = s - _step_bias(bias_ref, b, bpc)
        m = jnp.max(s, axis=-1, keepdims=True)
        p = jnp.exp(s - m)
        l = jnp.sum(p, axis=-1, keepdims=True)
        o3 = lax.dot_general(p.astype(BF16), v3, BATCH_NN_DIMS, preferred_element_type=F32) / l
        lse3 = m + jnp.log(l)
        for h in range(heads):
            o_ref[:, h * HEAD_DIM:(h + 1) * HEAD_DIM] = o3[h].astype(BF16)
        lse_ref[...] = jnp.concatenate([lse3[h] for h in range(heads)]
                                       + [jnp.zeros((ATTN_BLOCK, LANES - heads), F32)], axis=1)

    return pl.pallas_call(
        body, name=name, grid=(N_HEADS // heads, SEQ // ATTN_BLOCK),
        in_specs=[pl.BlockSpec((heads, 1, 1), lambda hg, b: (hg, 0, 0)), cur, prev, cur, v_prev, v_cur],
        out_specs=[cur, pl.BlockSpec((ATTN_BLOCK, LANES), lambda hg, b: (b, 0))],
        out_shape=[jax.ShapeDtypeStruct((SEQ, D_MODEL), BF16), jax.ShapeDtypeStruct((SEQ, LANES), F32)],
        scratch_shapes=[_bias_scratch(bpc)],
        compiler_params=_params("parallel", "arbitrary"),
    )(slopes.reshape(N_HEADS, 1, 1), q, k, k, proj, proj)


def _class_spec(tm, dilation, width=D_MODEL):
    if dilation == 1:
        return _row_spec(tm, width)
    return pl.BlockSpec((dilation, tm // dilation, width), lambda i: (0, i, 0))


def _class_shape(dilation, dtype, width=D_MODEL):
    if dilation == 1:
        return jax.ShapeDtypeStruct((SEQ, width), dtype)
    return jax.ShapeDtypeStruct((dilation, SEQ // dilation, width), dtype)


def _load_natural(in_ref, nat_ref, dilation):
    if dilation == 1:
        return in_ref[...].astype(F32)
    n = nat_ref.shape[1] // dilation
    tiles = in_ref.shape[-1] // LANES
    for r in range(dilation):
        for j in range(tiles):
            nat_ref.at[j][pl.ds(r, n, stride=dilation), :] = in_ref[r, :, j * LANES:(j + 1) * LANES].astype(F32)
    if tiles == 1:
        return nat_ref[0]
    return jnp.concatenate([nat_ref[j] for j in range(tiles)], axis=1)


def _store_classes(out_ref, value, nat_ref, dilation):
    if dilation == 1:
        out_ref[...] = value.astype(out_ref.dtype)
        return
    n = nat_ref.shape[1] // dilation
    tiles = value.shape[-1] // LANES
    for j in range(tiles):
        nat_ref[j] = value[:, j * LANES:(j + 1) * LANES]
    for r in range(dilation):
        for j in range(tiles):
            out_ref[r, :, j * LANES:(j + 1) * LANES] = (
                nat_ref.at[j][pl.ds(r, n, stride=dilation), :].astype(out_ref.dtype))


def _natural_scratch(tm):
    return pltpu.VMEM((D_MODEL // LANES, tm, LANES), F32)


def _head_selector():
    lane_head = lax.broadcasted_iota(jnp.int32, (D_MODEL, LANES), 0) // HEAD_DIM
    head = lax.broadcasted_iota(jnp.int32, (D_MODEL, LANES), 1)
    return (lane_head == head).astype(BF16)


def _dot_split(v, m01, dims):
    hi = v.astype(BF16)
    lo = (v - hi.astype(F32)).astype(BF16)
    return (lax.dot_general(hi, m01, dims, preferred_element_type=F32)
            + lax.dot_general(lo, m01, dims, preferred_element_type=F32))


def _merge_fwd(o_parts, lse_parts, z, sel, name):
    tm = ROW_TILE
    h_spec = pl.BlockSpec((tm, LANES), lambda i: (i, 0))

    def body(o0, o1, o2, l0, l1, l2, z_ref, sel_ref, u_ref, ut_ref, o_ref, lse_ref, nat):
        ls = [_load_natural(l, nat, d) for l, d in zip((l0, l1, l2), DILATIONS)]
        m = jnp.maximum(jnp.maximum(ls[0], ls[1]), ls[2])
        tot = m + jnp.log(jnp.exp(ls[0] - m) + jnp.exp(ls[1] - m) + jnp.exp(ls[2] - m))
        o = jnp.zeros((tm, D_MODEL), F32)
        for o_in, l, d in zip((o0, o1, o2), ls, DILATIONS):
            weight = _dot_split(jnp.exp(l - tot), sel_ref[...], NT_DIMS)
            o = o + weight * _load_natural(o_in, nat, d)
        zv = z_ref[...].astype(F32)
        u = o * (zv * _sigmoid(zv))
        u_ref[...] = u.astype(BF16)
        ut_ref[...] = u.T.astype(BF16)
        o_ref[...] = o.astype(BF16)
        lse_ref[...] = tot

    return pl.pallas_call(
        body, name=name, grid=(SEQ // tm,),
        in_specs=[_class_spec(tm, d) for d in DILATIONS] + [_class_spec(tm, d, LANES) for d in DILATIONS]
        + [_row_spec(tm, D_MODEL, B_Z_SEGMENT), _vec_spec(D_MODEL, LANES)],
        out_specs=[_row_spec(tm, D_MODEL), pl.BlockSpec((D_MODEL, tm), lambda i: (0, i)),
                   _row_spec(tm, D_MODEL), h_spec],
        out_shape=[jax.ShapeDtypeStruct((SEQ, D_MODEL), BF16), jax.ShapeDtypeStruct((D_MODEL, SEQ), BF16),
                   jax.ShapeDtypeStruct((SEQ, D_MODEL), BF16), jax.ShapeDtypeStruct((SEQ, LANES), F32)],
        scratch_shapes=[_natural_scratch(tm)],
        compiler_params=_params("parallel"),
    )(*o_parts, *lse_parts, z, sel)


def _merge_bwd(dy, w_out, o, lse, z, sel, name):
    tm = ROW_TILE
    n_d = len(DILATIONS)

    def body(dy_ref, w_ref, o_ref, lse_ref, z_ref, sel_ref, dz_ref, *rest):
        do_refs, delta_refs, lse_refs, nat = rest[:n_d], rest[n_d:2 * n_d], rest[2 * n_d:3 * n_d], rest[-1]
        zv = z_ref[...].astype(F32)
        sz = _sigmoid(zv)
        duv = lax.dot_general(dy_ref[...], w_ref[...], NT_DIMS, preferred_element_type=F32)
        ov = o_ref[...].astype(F32)
        do = duv * (zv * sz)
        dz_ref[...] = (duv * ov * (sz * (1.0 + zv * (1.0 - sz)))).astype(BF16)
        delta = _dot_split(do * ov, sel_ref[...], (((1,), (0,)), ((), ())))
        lv = lse_ref[...]
        for i, d in enumerate(DILATIONS):
            _store_classes(do_refs[i], do, nat, d)
            _store_classes(delta_refs[i], delta, nat, d)
            _store_classes(lse_refs[i], lv, nat, d)

    res = pl.pallas_call(
        body, name=name, grid=(SEQ // tm,),
        in_specs=[_row_spec(tm, D_MODEL), _vec_spec(D_MODEL, D_MODEL), _row_spec(tm, D_MODEL), _row_spec(tm, LANES),
                  _row_spec(tm, D_MODEL, B_Z_SEGMENT), _vec_spec(D_MODEL, LANES)],
        out_specs=[_row_spec(tm, D_MODEL)] + [_class_spec(tm, d) for d in DILATIONS]
        + [_class_spec(tm, d, LANES) for d in DILATIONS] * 2,
        out_shape=[jax.ShapeDtypeStruct((SEQ, D_MODEL), BF16)] + [_class_shape(d, BF16) for d in DILATIONS]
        + [_class_shape(d, F32, LANES) for d in DILATIONS] * 2,
        scratch_shapes=[_natural_scratch(tm)],
        compiler_params=_params("parallel"),
    )(dy, w_out, o, lse, z, sel)
    flat = lambda a: a.reshape(SEQ, a.shape[-1])
    return (res[0], [flat(a) for a in res[1:1 + n_d]], [flat(a) for a in res[1 + n_d:1 + 2 * n_d]],
            [flat(a) for a in res[1 + 2 * n_d:]])


def _attn_bwd(q, k, proj, group, do, lse, delta, slopes, dilation, name):
    bpc = SEQ // dilation // ATTN_BLOCK
    heads = ATTN_HEADS_BWD
    n_blocks = SEQ // ATTN_BLOCK
    carry = bpc > 1
    width = heads * HEAD_DIM
    cur, prev = _attn_specs(heads)
    v_cur, v_prev = _attn_specs(heads, segment=3 * group + 2)
    assert heads == N_HEADS
    per_head = pl.BlockSpec((ATTN_BLOCK, LANES), lambda hg, b: (jnp.minimum(b, n_blocks - 1), 0))
    scale = HEAD_DIM ** -0.5

    def body(sl_ref, q_ref, kp_ref, kc_ref, vp_ref, vc_ref, do_ref, lse_ref, dl_ref,
             dq_ref, dk_ref, dv_ref, *scratch):
        b = pl.program_id(1)
        if carry:
            dk_carry, dv_carry = scratch

            @pl.when(b == n_blocks)
            def _():
                dk_ref[...] = dk_carry[...].astype(BF16)
                dv_ref[...] = dv_carry[...].astype(BF16)

            @pl.when(b < n_blocks)
            def _():
                step(sl_ref, q_ref, kp_ref, kc_ref, vp_ref, vc_ref, do_ref, lse_ref, dl_ref,
                     dq_ref, dk_ref, dv_ref, dk_carry, dv_carry, b)
        else:
            step(sl_ref, q_ref, kp_ref, kc_ref, vp_ref, vc_ref, do_ref, lse_ref, dl_ref,
                 dq_ref, dk_ref, dv_ref, None, None, b)

    def step(sl_ref, q_ref, kp_ref, kc_ref, vp_ref, vc_ref, do_ref, lse_ref, dl_ref,
             dq_ref, dk_ref, dv_ref, dk_carry, dv_carry, b):
        if carry:
            @pl.when(b == 0)
            def _():
                dk_carry[...] = jnp.zeros_like(dk_carry)
                dv_carry[...] = jnp.zeros_like(dv_carry)

        q3 = _head_stack(lambda cols: q_ref[:, cols], heads)
        k3 = _head_stack(lambda cols: _key_tile(kp_ref, kc_ref, cols, bpc), heads)
        v3 = _head_stack(lambda cols: _key_tile(vp_ref, vc_ref, cols, bpc), heads)
        do3 = _head_stack(lambda cols: do_ref[:, cols], heads)
        lse_t = lse_ref[...].T
        dl_t = dl_ref[...].T
        lse3 = jnp.stack([lse_t[h:h + 1, :] for h in range(heads)], axis=0)
        dl3 = jnp.stack([dl_t[h:h + 1, :] for h in range(heads)], axis=0)
        s = lax.dot_general(k3, q3, BATCH_NT_DIMS, preferred_element_type=F32)
        dist, valid = _attn_masks(b, bpc, dilation, transposed=True)
        p = jnp.exp(jnp.where(valid[None], s - dist[None] * sl_ref[...], NEG_INF) - lse3)
        dp = lax.dot_general(v3, do3, BATCH_NT_DIMS, preferred_element_type=F32)
        ds = (p * (dp - dl3)).astype(BF16)
        dq3 = lax.dot_general(ds, k3, BATCH_TN_DIMS, preferred_element_type=F32) * scale
        dk3 = lax.dot_general(ds, q3, BATCH_NN_DIMS, preferred_element_type=F32)
        dv3 = lax.dot_general(p.astype(BF16), do3, BATCH_NN_DIMS, preferred_element_type=F32)
        for h in range(heads):
            cols = slice(h * HEAD_DIM, (h + 1) * HEAD_DIM)
            dq_ref[:, cols] = dq3[h].astype(BF16)
            if carry:
                dk_ref[:, cols] = (dk_carry[:, cols] + dk3[h, :ATTN_BLOCK]).astype(BF16)
                dv_ref[:, cols] = (dv_carry[:, cols] + dv3[h, :ATTN_BLOCK]).astype(BF16)
                dk_carry[:, cols] = dk3[h, ATTN_BLOCK:]
                dv_carry[:, cols] = dv3[h, ATTN_BLOCK:]
            else:
                dk_ref[:, cols] = dk3[h].astype(BF16)
                dv_ref[:, cols] = dv3[h].astype(BF16)

    kv_out = prev if carry else cur
    return pl.pallas_call(
        body, name=name, grid=(N_HEADS // heads, n_blocks + (1 if carry else 0)),
        in_specs=[pl.BlockSpec((heads, 1, 1), lambda hg, b: (hg, 0, 0)), cur, prev, cur, v_prev, v_cur,
                  cur, per_head, per_head],
        out_specs=[cur, kv_out, kv_out],
        out_shape=[jax.ShapeDtypeStruct((SEQ, D_MODEL), BF16)] * 3,
        scratch_shapes=[pltpu.VMEM((ATTN_BLOCK, width), F32)] * 2 if carry else [],
        compiler_params=_params("parallel", "arbitrary"),
    )(slopes.reshape(N_HEADS, 1, 1), q, k, k, proj, proj, do, lse, delta)


def _qknorm_bwd(proj, group, qw, kw, seg, dq, dk, dv, name):
    tm = ROW_TILE

    def body(q_in, k_in, qw_ref, kw_ref, seg_ref, dq_ref, dk_ref, dv_ref, dproj_ref, sums_ref):
        segv = seg_ref[...]
        sums = []
        for part, (raw_ref, w_ref, dn_ref) in enumerate(((q_in, qw_ref, dq_ref), (k_in, kw_ref, dk_ref))):
            raw = raw_ref[...].astype(F32)
            dn = dn_ref[...].astype(F32)
            r = _qk_rstd(raw, segv)
            xhat = raw * r
            gq = dn * w_ref[...]
            draw = r * (gq - xhat * _segmean(xhat * gq, segv))
            dproj_ref[:, part * D_MODEL:(part + 1) * D_MODEL] = draw.astype(BF16)
            sums.append(jnp.sum(dn * xhat, axis=0, keepdims=True))
        dproj_ref[:, 2 * D_MODEL:] = dv_ref[...]

        @pl.when(pl.program_id(0) == 0)
        def _():
            sums_ref[...] = jnp.zeros_like(sums_ref)

        sums_ref[...] += jnp.concatenate(sums + [jnp.zeros((6, D_MODEL), F32)], axis=0)

    return pl.pallas_call(
        body, name=name, grid=(SEQ // tm,),
        in_specs=[_row_spec(tm, D_MODEL, 3 * group), _row_spec(tm, D_MODEL, 3 * group + 1),
                  _vec_spec(1, D_MODEL), _vec_spec(1, D_MODEL), _vec_spec(256, 256)] + [_row_spec(tm, D_MODEL)] * 3,
        out_specs=[_row_spec(tm, 3 * D_MODEL), _vec_spec(8, D_MODEL)],
        out_shape=[jax.ShapeDtypeStruct((SEQ, 3 * D_MODEL), BF16), jax.ShapeDtypeStruct((8, D_MODEL), F32)],
        compiler_params=_params("arbitrary"),
    )(proj, proj, qw, kw, seg, dq, dk, dv)


B_TN = 512
B_GROUP_TILES = 3 * D_MODEL // B_TN
B_Z_TILE0 = 3 * B_GROUP_TILES
B_Z_TILES = D_MODEL // B_TN
B_TILES = B_Z_TILE0 + B_Z_TILES
B_Z_SEGMENT = 3 * len(DILATIONS)


def _local_step(x, target, mods, norm_g, conv_w, conv_b, ln_g, ln_b, q_norm, k_norm, chip, own_wa_in, own_wb_in,
                weights_a, weights_b, forward_weights_b, send_grads_b, forward_grads_b, send_grads_a):
    row = lambda a, i: a[i:i + 1]
    shift0, scale0, gate0 = row(mods[0], 0), row(mods[0], 1), row(mods[0], 2)
    shift1, scale1, gate1 = row(mods[1], 0), row(mods[1], 1), row(mods[1], 2)
    g0, g1 = row(norm_g, 0), row(norm_g, 1)
    seg = _seg_matrix()
    slopes = jnp.exp2(-8.0 * jnp.arange(1, N_HEADS + 1, dtype=F32) / N_HEADS)
    qw = [jnp.tile(q_norm[g:g + 1], (1, N_HEADS)) for g in range(3)]
    kw = [jnp.tile(k_norm[g:g + 1], (1, N_HEADS)) for g in range(3)]

    h0, h0t = _normmod_fwd(x, g0, scale0, shift0, "prenorm0")
    nsa = own_wa_in.shape[2]
    tiles_a = dict(tn=nsa, total_tiles=N_CHIPS, part_of=lambda tile: 0)
    own_ids, rest_ids, own_tiles = _own_first(chip, N_CHIPS)
    proj_a = _in_tiles([h0], own_wa_in, own_ids, own_tiles, name="a_in_own", **tiles_a)
    wa_in, wa_out = weights_a(proj_a)
    ja = wa_in.shape[0]
    proj_a = _in_tiles([h0], wa_in, rest_ids, N_CHIPS - own_tiles, name="a_in_rest", prev=proj_a, **tiles_a)
    u5, u5t, u2 = _conv_fwd(proj_a, conv_w, conv_b, ln_g, ln_b, "a_conv")
    x1, y_a, h1t, h1c = _out_a(u5, wa_out, x, gate0, g1, scale1, shift1, "a_out")

    tiles_b = dict(tn=B_TN, total_tiles=B_TILES,
                   part_of=lambda tile: jnp.where(tile >= B_Z_TILE0, 0, tile // B_GROUP_TILES))
    own_ids, rest_ids, own_tiles = _own_first(chip, B_TILES)
    proj_b = _in_tiles(h1c, own_wb_in, own_ids, own_tiles, name="b_in_own", **tiles_b)
    forward_weights_b(proj_b)
    wb_in, wb_out = weights_b(proj_b)
    jb, _, nsb = wb_in.shape
    proj_b = _in_tiles(h1c, wb_in, rest_ids, B_TILES - own_tiles, name="b_in_rest", prev=proj_b, **tiles_b)
    h1 = h1c[0]
    qkv, o_parts, lse_parts = [], [], []
    for g, d in enumerate(DILATIONS):
        qn, kn = _qknorm_fwd(proj_b, g, qw[g], kw[g], seg, f"b_qknorm_g{g}")
        og, lg = _attn_fwd(qn, kn, proj_b, g, slopes, d, f"b_attn_g{g}")
        qkv.append((qn, kn))
        o_parts.append(og if d == 1 else og.reshape(d, SEQ // d, D_MODEL))
        lse_parts.append(lg if d == 1 else lg.reshape(d, SEQ // d, LANES))
    sel = _head_selector()
    u_b, u_bt, o_b, lse_b = _merge_fwd(o_parts, lse_parts, proj_b, sel, "b_merge")
    e, dy_b, sums_loss = _out_b_loss(u_b, wb_out, x1, gate1, target, "b_out_loss")

    dwb_out = _mm(u_bt, dy_b, tn=D_MODEL, tile0=0, n_tiles=1, out_dtype=BF16, name="b_dwout")
    dz_b, do_c, delta_c, lse_c = _merge_bwd(dy_b, wb_out, o_b, lse_b, proj_b, sel, "b_merge_bwd")
    dwb_in = _mm(h1t, dz_b, tn=B_TN, tile0=B_Z_TILE0, n_tiles=B_Z_TILES, out_dtype=BF16, name="b_dwin_z",
                 out3d=(jb, nsb))
    dh1_parts = [_mm_nt(dz_b, wb_in, tn=B_TN, tile0=B_Z_TILE0, n_tiles=B_Z_TILES, name="b_dh_z")]
    qk_sums = []
    for g, d in enumerate(DILATIONS):
        qn, kn = qkv[g]
        dq, dk, dv = _attn_bwd(qn, kn, proj_b, g, do_c[g], lse_c[g], delta_c[g], slopes, d, f"b_attn_bwd_g{g}")
        dproj, sums_qk = _qknorm_bwd(proj_b, g, qw[g], kw[g], seg, dq, dk, dv, f"b_qknorm_bwd_g{g}")
        qk_sums.append(sums_qk)
        dwb_in = _mm(h1t if d == 1 else h1c[g], dproj, tn=B_TN, tile0=g * B_GROUP_TILES, n_tiles=B_GROUP_TILES,
                     out_dtype=BF16, name=f"b_dwin_g{g}", out3d=(jb, nsb), prev=dwb_in, transpose_lhs=d != 1)
        dh = _mm_nt(dproj, wb_in, tn=B_TN, tile0=g * B_GROUP_TILES, n_tiles=B_GROUP_TILES, name=f"b_dh_g{g}")
        dh1_parts.append(dh)
    token = send_grads_b(dwb_in, dwb_out)
    dx1, sums_n1, dy_a = _normmod_bwd(x1, g1, scale1 + token[0:1, 0:1], dh1_parts, e, "prenorm1_bwd",
                                      part_dilations=(1,) + DILATIONS, gated=(gate0, y_a))
    token = forward_grads_b(dx1)

    dwa_out = _mm(u5t, dy_a, tn=D_MODEL, tile0=0, n_tiles=1, out_dtype=BF16, name="a_dwout")
    du2, dz_a, sums_ln = _conv_bwd_pointwise(dy_a, wa_out, proj_a, u2, ln_g + token[0:1, 0:1], ln_b,
                                             "a_conv_bwd_pw")
    dproj_a, dconv_w = _conv_bwd_taps(du2, dz_a, proj_a, conv_w, "a_conv_bwd_taps")
    dwa_in = _mm(h0t, dproj_a, tn=nsa, tile0=0, n_tiles=ja, out_dtype=BF16, name="a_dwin", out3d=(ja, nsa))
    token = send_grads_a(dwa_in, dwa_out)
    dh0 = _mm_nt(dproj_a, wa_in, tn=nsa, tile0=0, n_tiles=ja, name="a_dh", after=token)
    grad_x, sums_n0 = _normmod_bwd(x, g0, scale0, [dh0], dx1, "prenorm0_bwd")

    small = dict(
        dnorm_g=jnp.concatenate([sums_n0[0:1], sums_n1[0:1]], axis=0),
        dmod0=jnp.concatenate([sums_n0[2:3], sums_n0[1:2], sums_n1[3:4]], axis=0),
        dmod1=jnp.concatenate([sums_n1[2:3], sums_n1[1:2], sums_loss[0:1]], axis=0),
        dln_g=sums_ln[0:1], dln_b=sums_ln[1:2], dconv_b=sums_ln[2:3],
        dconv_w=dconv_w[:CONV_WIDTH],
        dq_norm=jnp.concatenate([s[0:1] for s in qk_sums], axis=0),
        dk_norm=jnp.concatenate([s[1:2] for s in qk_sums], axis=0),
        loss_cols=sums_loss[1:2],
    )
    return grad_x, small


def _adamw(w, g, m, v, name, after=None, copy_grad=False):
    rows, cols = w.shape
    tr = rows if rows <= 128 else (256 if cols <= D_MODEL else 128)
    c1 = 1.0 / (1.0 - ADAM_B1 ** ADAM_STEP)
    c2 = 1.0 / (1.0 - ADAM_B2 ** ADAM_STEP)
    extra = [] if after is None else [after]
    n_out = 4 if copy_grad else 3

    def body(w_ref, g_ref, m_ref, v_ref, *rest):
        d_ref, mo_ref, vo_ref = rest[len(extra):len(extra) + 3]
        gv = g_ref[...]
        if copy_grad:
            rest[-1][...] = gv
        mn = ADAM_B1 * m_ref[...] + (1.0 - ADAM_B1) * gv
        vn = ADAM_B2 * v_ref[...] + (1.0 - ADAM_B2) * (gv * gv)
        mo_ref[...] = mn
        vo_ref[...] = vn
        d_ref[...] = -ADAM_LR * ((mn * c1) / (jnp.sqrt(vn * c2) + ADAM_EPS) + ADAM_WD * w_ref[...])

    spec = pl.BlockSpec((tr, cols), lambda i: (i, 0))
    return pl.pallas_call(
        body, name=name, grid=(rows // tr,),
        in_specs=[spec] * 4 + [pl.BlockSpec(memory_space=pl.ANY)] * len(extra), out_specs=[spec] * n_out,
        out_shape=[jax.ShapeDtypeStruct((rows, cols), F32)] * n_out,
        compiler_params=_params("parallel"),
    )(w, g, m, v, *extra)


def _cast_into_slot(w, chip_idx, name, keep_own=False, after=None):
    rows, cols = w.shape
    tr = 256
    extra = [] if after is None else [after]

    def body(ch_ref, w_ref, *rest):
        wb = w_ref[...].astype(BF16)
        for o_ref in rest[len(extra):]:
            o_ref[...] = wb

    slot_spec = pl.BlockSpec((None, tr, cols), lambda i, ch: (ch[0], i, 0))
    own_spec = pl.BlockSpec((None, tr, cols), lambda i, ch: (0, i, 0))
    res = pl.pallas_call(
        body, name=name,
        grid_spec=pltpu.PrefetchScalarGridSpec(
            num_scalar_prefetch=1, grid=(rows // tr,),
            in_specs=[pl.BlockSpec((tr, cols), lambda i, ch: (i, 0))] + [pl.BlockSpec(memory_space=pl.ANY)] * len(extra),
            out_specs=[slot_spec, own_spec] if keep_own else [slot_spec]),
        out_shape=[jax.ShapeDtypeStruct((N_CHIPS, rows, cols), BF16)]
        + ([jax.ShapeDtypeStruct((1, rows, cols), BF16)] if keep_own else []),
        compiler_params=_params("parallel"),
    )(chip_idx, w, *extra)
    return tuple(res) if keep_own else res[0]


def _position():
    x, y, c = lax.axis_index("x"), lax.axis_index("y"), lax.axis_index("c")
    return x, y, c


def _xor_peer(x, y, c, k):
    return (x ^ ((k >> 2) & 1), y ^ ((k >> 1) & 1), c ^ (k & 1))


def _chip_peer(x, y, k):
    return (x ^ ((k >> 1) & 1), y ^ (k & 1))


def _ada_forward(c_row, ada_w, ada_b, conv_w, after=()):
    ns = ada_w.shape[2]
    cw = conv_w.shape[1]

    def body(c_ref, w_ref, b_ref, cv_ref, *rest):
        (mod_ref, sc_ref, cvo_ref, c_all, mp, parts, cv_parts,
         send1, recv1, send2, recv2, send3, recv3) = rest[len(after):]
        x, y, c = _position()
        me = 4 * x + 2 * y + c
        chip = 2 * x + y

        def c_copy(k):
            return pltpu.make_async_remote_copy(
                src_ref=c_all.at[me], dst_ref=c_all.at[me], send_sem=send1.at[k - 1], recv_sem=recv1.at[k - 1],
                device_id=_xor_peer(x, y, c, k), device_id_type=MESH)

        def cv_copy(k):
            px, py = _chip_peer(x, y, k)
            return pltpu.make_async_remote_copy(
                src_ref=cv_parts.at[chip], dst_ref=cv_parts.at[chip], send_sem=send3.at[k - 1],
                recv_sem=recv3.at[k - 1], device_id=(px, py, c), device_id_type=MESH)

        c_all[me] = c_ref[...]
        cv_parts[chip] = cv_ref[...]
        for k in range(1, N_DEV):
            c_copy(k).start()
        for k in range(1, N_CHIPS):
            cv_copy(k).start()
        for k in range(1, N_DEV):
            c_copy(k).wait_recv()
        cv = jnp.concatenate([c_all[i] for i in range(N_DEV)], axis=0)
        sc = cv * _sigmoid(cv)
        sc_ref[...] = sc
        for l in range(2):
            res = jnp.dot(sc.astype(BF16), w_ref[l].astype(BF16), preferred_element_type=F32)
            for i in range(N_DEV):
                mp[i, l:l + 1, :] = res[i:i + 1, :]

        def mod_copy(k):
            px, py = _chip_peer(x, y, k)
            return pltpu.make_async_remote_copy(
                src_ref=mp.at[4 * px + 2 * py + c], dst_ref=parts.at[chip], send_sem=send2.at[k - 1],
                recv_sem=recv2.at[k - 1], device_id=(px, py, c), device_id_type=MESH)

        for k in range(1, N_CHIPS):
            mod_copy(k).start()
        parts[chip] = mp[me]
        for k in range(1, N_CHIPS):
            mod_copy(k).wait_recv()
            cv_copy(k).wait_recv()
        mod_ref[...] = jnp.concatenate([parts[j] for j in range(N_CHIPS)], axis=1) + b_ref[...]
        cvo_ref[...] = jnp.concatenate([cv_parts[j] for j in range(N_CHIPS)], axis=1)
        for k in range(1, N_DEV):
            c_copy(k).wait_send()
        for k in range(1, N_CHIPS):
            mod_copy(k).wait_send()
            cv_copy(k).wait_send()

    vm = pl.BlockSpec(memory_space=pltpu.VMEM)
    return pl.pallas_call(
        body, name="ada_forward",
        in_specs=[vm] * 4 + [pl.BlockSpec(memory_space=pl.ANY)] * len(after), out_specs=[vm] * 3,
        out_shape=[jax.ShapeDtypeStruct((2, 3 * D_MODEL), F32), jax.ShapeDtypeStruct((N_DEV, D_MODEL), F32),
                   jax.ShapeDtypeStruct((CONV_WIDTH, N_CHIPS * cw), F32)],
        scratch_shapes=[pltpu.VMEM((N_DEV, 1, D_MODEL), F32), pltpu.VMEM((N_DEV, 2, ns), F32),
                        pltpu.VMEM((N_CHIPS, 2, ns), F32), pltpu.VMEM((N_CHIPS, CONV_WIDTH, cw), F32),
                        pltpu.SemaphoreType.DMA((N_DEV - 1,)), pltpu.SemaphoreType.DMA((N_DEV - 1,)),
                        pltpu.SemaphoreType.DMA((N_CHIPS - 1,)), pltpu.SemaphoreType.DMA((N_CHIPS - 1,)),
                        pltpu.SemaphoreType.DMA((N_CHIPS - 1,)), pltpu.SemaphoreType.DMA((N_CHIPS - 1,))],
        compiler_params=pltpu.CompilerParams(vmem_limit_bytes=VMEM_LIMIT_BYTES),
    )(c_row, ada_w, ada_b, conv_w, *after)


HBM_SPEC = pl.BlockSpec(memory_space=pltpu.HBM)
ANY_SPEC = pl.BlockSpec(memory_space=pl.ANY)
SEM_SPEC = pl.BlockSpec(memory_space=pltpu.SEMAPHORE)
SPLIT_PARAMS = dict(compiler_params=pltpu.CompilerParams(has_side_effects=pltpu.SideEffectType.DATAFLOW_SIDE_EFFECTING))
TOKEN = jax.ShapeDtypeStruct((8, 128), F32)
ENTRY_HANDSHAKES = {name: (i, peers) for i, (name, peers) in enumerate((
    ("gather_start_a", "chips"), ("gather_start_b", "chips"),
    ("gather_forward_a", "sibling"), ("gather_forward_b", "sibling"),
    ("reduce_d2d_start_b", "sibling"), ("reduce_d2d_start_a", "sibling"),
    ("reduce_ici_start_b", "chips"), ("reduce_ici_start_a", "chips"),
    ("reduce_share_start_b", "sibling"), ("reduce_share_start_a", "sibling"),
    ("small_gather_start", "devices")))}


def _hbm(arrays):
    return [pltpu.with_memory_space_constraint(a, pltpu.HBM) for a in arrays]


def _hbm_like(arrays):
    return [pltpu.HBM(a.shape, a.dtype) for a in arrays]


def _gather_start(lands, after, name):
    n = len(lands)

    def body(*refs):
        _handshake(ENTRY_HANDSHAKES[name][1])
        ins = refs[:n]
        send, recv = refs[n + 1], refs[n + 2]
        x, y, c = _position()
        chip = 2 * x + y
        for t in range(n):
            rh = ins[t].shape[1] // 2
            for k in range(1, N_CHIPS):
                px, py = _chip_peer(x, y, k)
                block = ins[t].at[chip, pl.ds(c * rh, rh)]
                pltpu.make_async_remote_copy(
                    src_ref=block, dst_ref=block, send_sem=send.at[3 * t + k - 1], recv_sem=recv.at[3 * t + k - 1],
                    device_id=(px, py, c), device_id_type=MESH).start()
        refs[-1][...] = jnp.zeros(TOKEN.shape, F32)

    res = pl.pallas_call(
        body, name=name, in_specs=[HBM_SPEC] * n + [ANY_SPEC],
        out_specs=(SEM_SPEC, SEM_SPEC, *[HBM_SPEC] * n, pl.BlockSpec(memory_space=pltpu.VMEM)),
        out_shape=(pltpu.SemaphoreType.DMA((3 * n,)), pltpu.SemaphoreType.DMA((3 * n,)), *_hbm_like(lands), TOKEN),
        input_output_aliases={t: 2 + t for t in range(n)}, **_split_params(name),
    )(*_hbm(lands), after)
    return res[0], res[1], list(res[2:2 + n]), res[-1]


def _gather_forward(send, recv, lands, after, name):
    n = len(lands)

    def body(*refs):
        _handshake(ENTRY_HANDSHAKES[name][1])
        ins = refs[:n]
        send1, recv1 = refs[n], refs[n + 1]
        send2, recv2 = refs[n + 3], refs[n + 4]
        x, y, c = _position()
        chip = 2 * x + y
        for t in range(n):
            rh = ins[t].shape[1] // 2
            half = pl.ds(c * rh, rh)
            for k in range(1, N_CHIPS):
                px, py = _chip_peer(x, y, k)
                s = 3 * t + k - 1
                got = ins[t].at[2 * px + py, half]
                cp = pltpu.make_async_remote_copy(
                    src_ref=ins[t].at[chip, half], dst_ref=got, send_sem=send1.at[s], recv_sem=recv1.at[s],
                    device_id=(px, py, c), device_id_type=MESH)
                cp.wait_send()
                cp.wait_recv()
                pltpu.make_async_remote_copy(
                    src_ref=got, dst_ref=got, send_sem=send2.at[s], recv_sem=recv2.at[s],
                    device_id=(x, y, 1 - c), device_id_type=MESH).start()
        refs[-1][...] = jnp.zeros(TOKEN.shape, F32)

    res = pl.pallas_call(
        body, name=name, in_specs=[HBM_SPEC] * n + [SEM_SPEC, SEM_SPEC, ANY_SPEC],
        out_specs=(SEM_SPEC, SEM_SPEC, *[HBM_SPEC] * n, pl.BlockSpec(memory_space=pltpu.VMEM)),
        out_shape=(pltpu.SemaphoreType.DMA((3 * n,)), pltpu.SemaphoreType.DMA((3 * n,)), *_hbm_like(lands), TOKEN),
        input_output_aliases={t: 2 + t for t in range(n)}, **_split_params(name),
    )(*lands, send, recv, after)
    return res[0], res[1], list(res[2:2 + n]), res[-1]


def _gather_wait(send, recv, lands, after, name):
    n = len(lands)

    def body(*refs):
        ins = refs[:n]
        send_ref, recv_ref = refs[n], refs[n + 1]
        x, y, c = _position()
        for t in range(n):
            rh = ins[t].shape[1] // 2
            for k in range(1, N_CHIPS):
                px, py = _chip_peer(x, y, k)
                cp = pltpu.make_async_remote_copy(
                    src_ref=ins[t].at[2 * px + py, pl.ds(c * rh, rh)],
                    dst_ref=ins[t].at[2 * px + py, pl.ds((1 - c) * rh, rh)], send_sem=send_ref.at[3 * t + k - 1],
                    recv_sem=recv_ref.at[3 * t + k - 1], device_id=(x, y, 1 - c), device_id_type=MESH)
                cp.wait_send()
                cp.wait_recv()

    res = pl.pallas_call(
        body, name=name, in_specs=[HBM_SPEC] * n + [SEM_SPEC, SEM_SPEC, ANY_SPEC], out_specs=[HBM_SPEC] * n,
        out_shape=_hbm_like(lands), input_output_aliases={t: t for t in range(n)}, **SPLIT_PARAMS,
    )(*lands, send, recv, after)
    return list(res)


def _handshake(peers):
    x, y, c = _position()
    if peers == "sibling":
        ids = [(x, y, 1 - c)]
    elif peers == "chips":
        ids = [(*_chip_peer(x, y, k), c) for k in range(1, N_CHIPS)]
    else:
        ids = [_xor_peer(x, y, c, k) for k in range(1, N_DEV)]
    barrier = pltpu.get_barrier_semaphore()
    for peer in ids:
        pl.semaphore_signal(barrier, inc=1, device_id=peer, device_id_type=MESH)
    pl.semaphore_wait(barrier, len(ids))


def _split_params(name):
    return dict(compiler_params=pltpu.CompilerParams(
        has_side_effects=pltpu.SideEffectType.DATAFLOW_SIDE_EFFECTING, collective_id=ENTRY_HANDSHAKES[name][0]))


def _split_start(name, arrays, n_sems, after, issue):
    m = len(arrays)

    def body(*refs):
        _handshake(ENTRY_HANDSHAKES[name][1])
        issue(refs[:m], refs[m + 1], refs[m + 2])
        refs[-1][...] = jnp.zeros(TOKEN.shape, F32)

    res = pl.pallas_call(
        body, name=name, in_specs=[HBM_SPEC] * m + [ANY_SPEC],
        out_specs=(SEM_SPEC, SEM_SPEC, *[HBM_SPEC] * m, pl.BlockSpec(memory_space=pltpu.VMEM)),
        out_shape=(pltpu.SemaphoreType.DMA((n_sems,)), pltpu.SemaphoreType.DMA((n_sems,)), *_hbm_like(arrays), TOKEN),
        input_output_aliases={t: 2 + t for t in range(m)}, **_split_params(name),
    )(*_hbm(arrays), after)
    return res[0], res[1], list(res[2:2 + m]), res[-1]


def _split_wait(name, arrays, send, recv, after, await_all):
    m = len(arrays)

    def body(*refs):
        await_all(refs[:m], refs[m], refs[m + 1])

    res = pl.pallas_call(
        body, name=name, in_specs=[HBM_SPEC] * m + [SEM_SPEC, SEM_SPEC, ANY_SPEC], out_specs=[HBM_SPEC] * m,
        out_shape=_hbm_like(arrays), input_output_aliases={t: t for t in range(m)}, **SPLIT_PARAMS,
    )(*arrays, send, recv, after)
    return list(res)


def _sibling_copies(refs, send, recv, n):
    x, y, c = _position()
    cps = []
    for t in range(n):
        rh = refs[t].shape[1] // 2
        cps.append(pltpu.make_async_remote_copy(
            src_ref=refs[t].at[pl.ds(0, N_CHIPS), pl.ds((1 - c) * rh, rh)], dst_ref=refs[n + t],
            send_sem=send.at[t], recv_sem=recv.at[t], device_id=(x, y, 1 - c), device_id_type=MESH))
    return cps


def _reduce_sibling_start(grads, after, name):
    n = len(grads)
    lands = [lax.empty((N_CHIPS, g.shape[1] // 2, g.shape[2]), BF16) for g in grads]

    def issue(refs, send, recv):
        for cp in _sibling_copies(refs, send, recv, n):
            cp.start()

    return _split_start(name, list(grads) + lands, n, after, issue)


def _reduce_sibling_wait(send, recv, arrays, after, name):
    n = len(arrays) // 2

    def await_all(refs, send_ref, recv_ref):
        for cp in _sibling_copies(refs, send_ref, recv_ref, n):
            cp.wait_send()
            cp.wait_recv()

    res = _split_wait(name, arrays, send, recv, after, await_all)
    return res[:n], res[n:]


def _add_sibling_half(grad, got, dev_idx, name):
    j, r, cols = grad.shape
    rh = r // 2
    tr = rh
    nb = rh // tr

    def body(idx_ref, g_ref, got_ref, out_ref):
        out_ref[...] = (g_ref[...].astype(F32) + got_ref[...].astype(F32)).astype(BF16)

    return pl.pallas_call(
        body, name=name,
        grid_spec=pltpu.PrefetchScalarGridSpec(
            num_scalar_prefetch=1, grid=(j, nb),
            in_specs=[pl.BlockSpec((None, tr, cols), lambda jj, i, idx: (jj, idx[2] * nb + i, 0)),
                      pl.BlockSpec((None, tr, cols), lambda jj, i, idx: (jj, i, 0))],
            out_specs=pl.BlockSpec((None, tr, cols), lambda jj, i, idx: (jj, i, 0))),
        out_shape=jax.ShapeDtypeStruct((j, rh, cols), BF16),
        compiler_params=_params("parallel", "parallel"),
    )(dev_idx, grad, got)


def _chip_copies(refs, send, recv, n, receiving):
    x, y, c = _position()
    chip = 2 * x + y
    cps = []
    for t in range(n):
        for k in range(1, N_CHIPS):
            px, py = _chip_peer(x, y, k)
            cps.append(pltpu.make_async_remote_copy(
                src_ref=refs[t].at[2 * px + py], dst_ref=refs[n + t].at[2 * px + py if receiving else chip],
                send_sem=send.at[3 * t + k - 1], recv_sem=recv.at[3 * t + k - 1],
                device_id=(px, py, c), device_id_type=MESH))
    return cps


def _reduce_chips_start(partials, after, name):
    n = len(partials)
    lands = [lax.empty(p.shape, BF16) for p in partials]

    def issue(refs, send, recv):
        for cp in _chip_copies(refs, send, recv, n, False):
            cp.start()

    return _split_start(name, list(partials) + lands, 3 * n, after, issue)


def _reduce_chips_wait(send, recv, arrays, after, name):
    n = len(arrays) // 2

    def await_all(refs, send_ref, recv_ref):
        for cp in _chip_copies(refs, send_ref, recv_ref, n, True):
            cp.wait_send()
            cp.wait_recv()

    res = _split_wait(name, arrays, send, recv, after, await_all)
    return res[:n], res[n:]


def _sum_partials(land, partial, dev_idx, name):
    _, rh, cols = land.shape
    tr = min(rh, 256)
    nb = rh // tr

    def body(idx_ref, l_ref, p_ref, o_ref):
        chip = idx_ref[1]
        acc = jnp.where(chip == 0, p_ref[...], l_ref[0]).astype(F32)
        for s in range(1, N_CHIPS):
            acc = acc + jnp.where(chip == s, p_ref[...], l_ref[s]).astype(F32)
        o_ref[...] = acc

    return pl.pallas_call(
        body, name=name,
        grid_spec=pltpu.PrefetchScalarGridSpec(
            num_scalar_prefetch=1, grid=(nb,),
            in_specs=[pl.BlockSpec((N_CHIPS, tr, cols), lambda i, idx: (0, i, 0)),
                      pl.BlockSpec((None, tr, cols), lambda i, idx: (idx[1], i, 0))],
            out_specs=pl.BlockSpec((tr, cols), lambda i, idx: (idx[2] * nb + i, 0))),
        out_shape=jax.ShapeDtypeStruct((2 * rh, cols), F32), compiler_params=_params("parallel"),
    )(dev_idx, land, partial)


def _half_copies(refs, send, recv, receiving):
    x, y, c = _position()
    cps = []
    for t, ref in enumerate(refs):
        rh = ref.shape[0] // 2
        cps.append(pltpu.make_async_remote_copy(
            src_ref=ref.at[pl.ds(c * rh, rh)], dst_ref=ref.at[pl.ds(((1 - c) if receiving else c) * rh, rh)],
            send_sem=send.at[t], recv_sem=recv.at[t], device_id=(x, y, 1 - c), device_id_type=MESH))
    return cps


def _share_halves_start(totals, after, name):
    def issue(refs, send, recv):
        for cp in _half_copies(refs, send, recv, False):
            cp.start()

    return _split_start(name, list(totals), len(totals), after, issue)


def _share_halves_wait(send, recv, totals, after, name):
    def await_all(refs, send_ref, recv_ref):
        for cp in _half_copies(refs, send_ref, recv_ref, True):
            cp.wait_send()
            cp.wait_recv()

    return _split_wait(name, totals, send, recv, after, await_all)


SMALL_ROWS = 56


def _small_copies(refs, send, recv, receiving):
    x, y, c = _position()
    me = 4 * x + 2 * y + c
    cps = []
    for k in range(1, N_DEV):
        px, py, pc = _xor_peer(x, y, c, k)
        cps.append(pltpu.make_async_remote_copy(
            src_ref=refs[0], dst_ref=refs[1].at[4 * px + 2 * py + pc if receiving else me],
            send_sem=send.at[k - 1], recv_sem=recv.at[k - 1], device_id=(px, py, pc), device_id_type=MESH))
    return cps


def _small_gather_start(packed, after):
    land = lax.empty((N_DEV,) + packed.shape, F32)

    def issue(refs, send, recv):
        for cp in _small_copies(refs, send, recv, False):
            cp.start()

    return _split_start("small_gather_start", [packed, land], N_DEV - 1, after, issue)


def _small_gather_wait(send, recv, arrays, after):
    def await_all(refs, send_ref, recv_ref):
        for cp in _small_copies(refs, send_ref, recv_ref, True):
            cp.wait_send()
            cp.wait_recv()

    return _split_wait("small_gather_wait", arrays, send, recv, after, await_all)


def _reduce_small(packed, land, silu_c):
    ns = 3 * D_MODEL // N_CHIPS

    def body(p_ref, land_ref, sc_ref, tot_ref, gw_ref, loss_ref, qk_ref, allp):
        x, y, c = _position()
        me = 4 * x + 2 * y + c
        chip = 2 * x + y
        for i in range(N_DEV):
            allp[i] = jnp.where(me == i, p_ref[...], land_ref[i])
        tot = allp[0]
        for i in range(1, N_DEV):
            tot = tot + allp[i]
        tot_ref[...] = tot
        loss_ref[...] = jnp.sum(tot[11:12, :], axis=1, keepdims=True) * (0.5 / D_MODEL)
        fold = tot[5:11, 0:HEAD_DIM]
        for h in range(1, N_HEADS):
            fold = fold + tot[5:11, h * HEAD_DIM:(h + 1) * HEAD_DIM]
        qk_ref[...] = jnp.concatenate([fold, jnp.zeros((2, HEAD_DIM), F32)], axis=0)
        pad = jnp.zeros((LANES - N_DEV, D_MODEL), F32)
        sct = jnp.concatenate([sc_ref[...], pad], axis=0).T.astype(BF16)
        for l in range(2):
            dms = [allp[i, pl.ds(12 + 4 * l + chip, 1), :] for i in range(N_DEV)]
            dm = jnp.concatenate(dms + [pad], axis=0)[:, :ns].astype(BF16)
            gw_ref[l] = jnp.dot(sct, dm, preferred_element_type=F32)

    vm = pl.BlockSpec(memory_space=pltpu.VMEM)
    return pl.pallas_call(
        body, name="reduce_small", in_specs=[vm, vm, vm], out_specs=[vm] * 4,
        out_shape=[jax.ShapeDtypeStruct((SMALL_ROWS, D_MODEL), F32), jax.ShapeDtypeStruct((2, D_MODEL, ns), F32),
                   jax.ShapeDtypeStruct((1, 1), F32), jax.ShapeDtypeStruct((8, HEAD_DIM), F32)],
        scratch_shapes=[pltpu.VMEM((N_DEV, SMALL_ROWS, D_MODEL), F32)],
        compiler_params=pltpu.CompilerParams(vmem_limit_bytes=VMEM_LIMIT_BYTES),
    )(packed, land, silu_c)


def kernel(x, c, norm_g, ada_w, ada_b, a_w_in, a_conv_w, a_conv_b, a_ln_g, a_ln_b, a_w_out, b_w_in, b_q_norm, b_k_norm, b_w_out, loss_target, m_norm_g, m_ada_w, m_ada_b, m_a_w_in, m_a_conv_w, m_a_conv_b, m_a_ln_g, m_a_ln_b, m_a_w_out, m_b_w_in, m_b_q_norm, m_b_k_norm, m_b_w_out, v_norm_g, v_ada_w, v_ada_b, v_a_w_in, v_a_conv_w, v_a_conv_b, v_a_ln_g, v_a_ln_b, v_a_w_out, v_b_w_in, v_b_q_norm, v_b_k_norm, v_b_w_out):
    chip = 2 * lax.axis_index("x") + lax.axis_index("y")
    core = lax.axis_index("c")
    chip_idx = chip.astype(jnp.int32).reshape(1)
    dev_idx = jnp.stack([2 * chip + core, chip, core]).astype(jnp.int32)

    land_a_in, own_wa_in = _cast_into_slot(a_w_in[0], chip_idx, "cast_a_w_in", keep_own=True)
    lands_a = [land_a_in, _cast_into_slot(a_w_out[0], chip_idx, "cast_a_w_out")]
    mods, silu_c, conv_w_full = _ada_forward(c, ada_w, ada_b, a_conv_w[0], after=tuple(lands_a))
    send_a, recv_a, lands_a, token_a = _gather_start(lands_a, mods, "gather_start_a")
    land_b_in, own_wb_in = _cast_into_slot(b_w_in[0], chip_idx, "cast_b_w_in", keep_own=True, after=token_a)
    lands_b = [land_b_in, _cast_into_slot(b_w_out[0], chip_idx, "cast_b_w_out", after=token_a)]
    send_b, recv_b, lands_b, token_b = _gather_start(lands_b, token_a, "gather_start_b")
    mods = mods + token_b[0:2, 0:1]

    def weights_a(after):
        send, recv, lands, _ = _gather_forward(send_a, recv_a, lands_a, after, "gather_forward_a")
        w_in, w_out = _gather_wait(send, recv, lands, after, "gather_wait_a")
        return w_in, w_out.reshape(D_MODEL, D_MODEL)

    forwarded_b = []

    def weights_b(after):
        send, recv, lands, _ = forwarded_b
        w_in, w_out = _gather_wait(send, recv, lands, after, "gather_wait_b")
        return w_in, w_out.reshape(D_MODEL, D_MODEL)

    def forward_weights_b(after):
        forwarded_b.extend(_gather_forward(send_b, recv_b, lands_b, after, "gather_forward_b"))

    stage1, stage2 = {}, {}

    def send_grads(tag, dw_in, dw_out):
        grads = [dw_in, dw_out.reshape(N_CHIPS, D_MODEL // N_CHIPS, D_MODEL)]
        send, recv, arrays, token = _reduce_sibling_start(grads, dw_out, f"reduce_d2d_start_{tag}")
        stage1[tag] = (send, recv, arrays)
        return token

    def forward_grads(tag, after):
        send, recv, arrays = stage1[tag]
        grads, got = _reduce_sibling_wait(send, recv, arrays, after, f"reduce_d2d_wait_{tag}")
        partials = [_add_sibling_half(grads[i], got[i], dev_idx, f"reduce_add_{tag}_{i}") for i in range(2)]
        send, recv, arrays, token = _reduce_chips_start(partials, partials[1], f"reduce_ici_start_{tag}")
        stage2[tag] = (send, recv, arrays)
        return token

    stage3 = {}

    def sum_grads(tag, after):
        send, recv, arrays = stage2[tag]
        partials, lands = _reduce_chips_wait(send, recv, arrays, after, f"reduce_ici_wait_{tag}")
        totals = [_sum_partials(lands[i], partials[i], dev_idx, f"reduce_sum_{tag}_{i}") for i in range(2)]
        send, recv, totals, token = _share_halves_start(totals, totals[1], f"reduce_share_start_{tag}")
        stage3[tag] = (send, recv, totals)
        return token

    def finish_grads(tag, after):
        send, recv, totals = stage3[tag]
        return _share_halves_wait(send, recv, totals, after, f"reduce_share_wait_{tag}")

    grad_x, small = _local_step(
        x[0], loss_target[0], mods.reshape(2, 3, D_MODEL), norm_g, conv_w_full, a_conv_b, a_ln_g[0:1],
        a_ln_b[0:1], b_q_norm[0], b_k_norm[0], chip.astype(jnp.int32), own_wa_in, own_wb_in,
        weights_a, weights_b, forward_weights_b,
        functools.partial(send_grads, "b"), functools.partial(forward_grads, "b"), functools.partial(send_grads, "a"))

    ns = 3 * D_MODEL // N_CHIPS
    pad_mod = lambda dm: jnp.pad(dm.reshape(N_CHIPS, ns), ((0, 0), (0, D_MODEL - ns)))
    packed = jnp.concatenate([
        small["dnorm_g"], small["dconv_b"], small["dln_g"], small["dln_b"], small["dq_norm"], small["dk_norm"],
        small["loss_cols"], pad_mod(small["dmod0"]), pad_mod(small["dmod1"]), small["dconv_w"],
        jnp.zeros((SMALL_ROWS - 20 - CONV_WIDTH, D_MODEL), F32)], axis=0)
    send_s, recv_s, small_arrays, token_s = _small_gather_start(packed, packed)

    given = dict(norm_g=(norm_g, m_norm_g, v_norm_g), ada_w=(ada_w, m_ada_w, v_ada_w), ada_b=(ada_b, m_ada_b, v_ada_b),
                 a_w_in=(a_w_in, m_a_w_in, v_a_w_in), a_conv_w=(a_conv_w, m_a_conv_w, v_a_conv_w),
                 a_conv_b=(a_conv_b, m_a_conv_b, v_a_conv_b), a_ln_g=(a_ln_g, m_a_ln_g, v_a_ln_g),
                 a_ln_b=(a_ln_b, m_a_ln_b, v_a_ln_b), a_w_out=(a_w_out, m_a_w_out, v_a_w_out),
                 b_w_in=(b_w_in, m_b_w_in, v_b_w_in), b_q_norm=(b_q_norm, m_b_q_norm, v_b_q_norm),
                 b_k_norm=(b_k_norm, m_b_k_norm, v_b_k_norm), b_w_out=(b_w_out, m_b_w_out, v_b_w_out))
    order = ["norm_g", "ada_w", "ada_b", "a_w_in", "a_conv_w", "a_conv_b", "a_ln_g", "a_ln_b", "a_w_out", "b_w_in",
             "b_q_norm", "b_k_norm", "b_w_out"]
    outs = {}

    def update(k, g2, after=None, copy_grad=False):
        w, m, v = given[k]
        shape2 = g2.shape
        res = _adamw(w.reshape(shape2), g2, m.reshape(shape2), v.reshape(shape2), f"adamw_{k}", after, copy_grad)
        outs[k] = tuple(a.reshape(w.shape) for a in ((res[3] if copy_grad else g2), res[0], res[1], res[2]))

    token = forward_grads("a", token_s)
    token = sum_grads("b", token)
    packed, land = _small_gather_wait(send_s, recv_s, small_arrays, token)
    tot, g_ada_w, loss, qk = _reduce_small(packed, land, silu_c)
    g_b_in, g_b_out = finish_grads("b", tot)
    update("b_w_in", g_b_in, copy_grad=True)
    update("b_w_out", g_b_out, copy_grad=True)
    token = sum_grads("a", outs["b_w_in"][1])
    cw = D_MODEL // N_CHIPS
    g_small = dict(
        norm_g=tot[0:2], a_conv_b=tot[2:3], a_ln_g=tot[3:4], a_ln_b=tot[4:5],
        b_q_norm=qk[0:3], b_k_norm=qk[3:6],
        ada_b=jnp.stack([tot[12:16, :ns].reshape(3 * D_MODEL), tot[16:20, :ns].reshape(3 * D_MODEL)]),
        a_conv_w=lax.dynamic_slice(tot[20:20 + CONV_WIDTH], (0, chip * cw), (CONV_WIDTH, cw)),
    )
    update("ada_w", g_ada_w.reshape(2 * D_MODEL, ns), after=token)
    for k, g2 in g_small.items():
        update(k, g2, after=token)
    g_a_in, g_a_out = finish_grads("a", outs["ada_w"][1])
    update("a_w_in", g_a_in, copy_grad=True)
    update("a_w_out", g_a_out, copy_grad=True)
    return (loss.reshape(()), grad_x[None], *[outs[k][0] for k in order], *[outs[k][1] for k in order],
            *[outs[k][2] for k in order], *[outs[k][3] for k in order])
```

```python
import functools

import jax
import jax.numpy as jnp
from jax import lax
from jax.experimental import pallas as pl
from jax.experimental.pallas import tpu as pltpu

F32 = jnp.float32
BF16 = jnp.bfloat16

SEQ = 2048
D_MODEL = 1024
CONV_WIDTH = 31
HEAD_DIM = 64
N_HEADS = 16
DILATIONS = (1, 4, 16)
ATTN_BLOCK = 128
NORM_EPS = 1e-6
NEG_INF = -1e30
N_DEV = 8
N_CHIPS = 4

ADAM_LR = 0.001
ADAM_B1 = 0.9
ADAM_B2 = 0.999
ADAM_EPS = 1e-08
ADAM_WD = 0.01
ADAM_STEP = 10

VMEM_LIMIT_BYTES = 52 * 1024 * 1024
HALO = 32
LANES = 128
ROW_TILE = 512
MESH = pl.DeviceIdType.MESH


def _params(*sem):
    return pltpu.CompilerParams(dimension_semantics=sem or None, vmem_limit_bytes=VMEM_LIMIT_BYTES)


def _sigmoid(v):
    return 1.0 / (1.0 + jnp.exp(-v))


def _row_spec(tm, cols, col_block=0):
    return pl.BlockSpec((tm, cols), lambda i: (i, col_block))


def _vec_spec(rows, cols):
    return pl.BlockSpec((rows, cols), lambda i: (0, 0))


def _normmod(xv, g, scale, shift):
    r = lax.rsqrt(jnp.mean(xv * xv, axis=-1, keepdims=True) + NORM_EPS)
    return xv * r * g * (1.0 + scale) + shift


def _normmod_fwd(x, g, scale, shift, name):
    tm = ROW_TILE

    def body(x_ref, g_ref, sc_ref, sh_ref, h_ref, ht_ref):
        h = _normmod(x_ref[...], g_ref[...], sc_ref[...], sh_ref[...])
        h_ref[...] = h.astype(BF16)
        ht_ref[...] = h.T.astype(BF16)

    return pl.pallas_call(
        body, name=name, grid=(SEQ // tm,),
        in_specs=[_row_spec(tm, D_MODEL)] + [_vec_spec(1, D_MODEL)] * 3,
        out_specs=[_row_spec(tm, D_MODEL), pl.BlockSpec((D_MODEL, tm), lambda i: (0, i))],
        out_shape=[jax.ShapeDtypeStruct((SEQ, D_MODEL), BF16), jax.ShapeDtypeStruct((D_MODEL, SEQ), BF16)],
        compiler_params=_params("parallel"),
    )(x, g, scale, shift)


def _normmod_bwd(x, g, scale, dh_parts, dres, name, part_dilations=None, gated=None):
    tm = ROW_TILE
    n_parts = len(dh_parts)
    dils = part_dilations or (1,) * n_parts
    dh_parts = [p if d == 1 else p.reshape(d, SEQ // d, D_MODEL) for p, d in zip(dh_parts, dils)]
    n_gated = 0 if gated is None else 2

    def body(x_ref, g_ref, sc_ref, dres_ref, *rest):
        part_refs = rest[:n_parts]
        gated_refs = rest[n_parts:n_parts + n_gated]
        out_refs = rest[n_parts + n_gated:]
        dx_ref, sums_ref, nat = out_refs[0], out_refs[1], out_refs[-1]
        xv = x_ref[...]
        r = lax.rsqrt(jnp.mean(xv * xv, axis=-1, keepdims=True) + NORM_EPS)
        xn = xv * r
        dh = _load_natural(part_refs[0], nat, dils[0])
        for p, d in zip(part_refs[1:], dils[1:]):
            dh = dh + _load_natural(p, nat, d)
        gv = g_ref[...]
        one_sc = 1.0 + sc_ref[...]
        dxn = dh * (gv * one_sc)
        dx = dres_ref[...] + r * (dxn - xn * jnp.mean(dxn * xn, axis=-1, keepdims=True))
        dx_ref[...] = dx
        dhx = dh * xn
        rows = [jnp.sum(dhx, axis=0, keepdims=True) * one_sc,
                jnp.sum(dhx, axis=0, keepdims=True) * gv,
                jnp.sum(dh, axis=0, keepdims=True)]
        if gated is not None:
            gate_ref, y_ref = gated_refs
            out_refs[2][...] = (dx * gate_ref[...]).astype(BF16)
            rows.append(jnp.sum(dx * y_ref[...].astype(F32), axis=0, keepdims=True))
        sums = jnp.concatenate(rows + [jnp.zeros((8 - len(rows), D_MODEL), F32)], axis=0)

        @pl.when(pl.program_id(0) == 0)
        def _():
            sums_ref[...] = jnp.zeros_like(sums_ref)

        sums_ref[...] += sums

    gated_specs = [] if gated is None else [_vec_spec(1, D_MODEL), _row_spec(tm, D_MODEL)]
    dy_spec = [] if gated is None else [_row_spec(tm, D_MODEL)]
    dy_shape = [] if gated is None else [jax.ShapeDtypeStruct((SEQ, D_MODEL), BF16)]
    return pl.pallas_call(
        body, name=name, grid=(SEQ // tm,),
        in_specs=[_row_spec(tm, D_MODEL), _vec_spec(1, D_MODEL), _vec_spec(1, D_MODEL), _row_spec(tm, D_MODEL)]
        + [_class_spec(tm, d) for d in dils] + gated_specs,
        out_specs=[_row_spec(tm, D_MODEL), _vec_spec(8, D_MODEL)] + dy_spec,
        out_shape=[jax.ShapeDtypeStruct((SEQ, D_MODEL), F32), jax.ShapeDtypeStruct((8, D_MODEL), F32)] + dy_shape,
        scratch_shapes=[_natural_scratch(tm)],
        compiler_params=_params("arbitrary"),
    )(x, g, scale, dres, *dh_parts, *(gated or ()))


def _mm(lhs, rhs, *, tn, tile0, n_tiles, out_dtype, name, out3d=None, prev=None, transpose_lhs=False):
    mo, kc = lhs.shape[::-1] if transpose_lhs else lhs.shape
    cm = min(mo, 1024)
    tc = 256

    def body(l_ref, r_ref, *rest):
        if transpose_lhs:
            o_ref, lt_ref = rest[-2], rest[-1]

            @pl.when(pl.program_id(0) == 0)
            def _():
                for c in range(kc // tc):
                    lt_ref[:, c * tc:(c + 1) * tc] = l_ref[c * tc:(c + 1) * tc, :].astype(F32).T.astype(l_ref.dtype)
        else:
            o_ref, lt_ref = rest[-1], l_ref
        for m in range(mo // cm):
            rows = pl.ds(m * cm, cm)
            o_ref[rows, :] = jnp.dot(lt_ref[rows, :], r_ref[...], preferred_element_type=F32).astype(out_dtype)

    if rhs.ndim == 3:
        tps_r = rhs.shape[2] // tn
        r_spec = pl.BlockSpec((None, kc, tn), lambda t: ((tile0 + t) // tps_r, 0, (tile0 + t) % tps_r))
    else:
        r_spec = pl.BlockSpec((kc, tn), lambda t: (0, t))
    in_specs = [pl.BlockSpec(lhs.shape, lambda t: (0, 0)), r_spec]
    args = [lhs, rhs]
    aliases = {}
    if out3d is None:
        o_spec = pl.BlockSpec((mo, tn), lambda t: (0, t))
        o_shape = jax.ShapeDtypeStruct((mo, n_tiles * tn), out_dtype)
    else:
        j_out, ns_out = out3d
        tps_o = ns_out // tn
        o_spec = pl.BlockSpec((None, mo, tn), lambda t: ((tile0 + t) // tps_o, 0, (tile0 + t) % tps_o))
        o_shape = jax.ShapeDtypeStruct((j_out, mo, ns_out), out_dtype)
        if prev is not None:
            in_specs.append(pl.BlockSpec(memory_space=pl.ANY))
            args.append(prev)
            aliases = {2: 0}
    return pl.pallas_call(
        body, name=name, grid=(n_tiles,), in_specs=in_specs, out_specs=o_spec, out_shape=o_shape,
        input_output_aliases=aliases,
        scratch_shapes=[pltpu.VMEM((mo, kc), lhs.dtype)] if transpose_lhs else [],
        compiler_params=_params("arbitrary" if transpose_lhs else "parallel"),
    )(*args)


def _in_tiles(h_parts, w3, tile_ids, n_tiles, *, tn, total_tiles, part_of, name, prev=None):
    _, kc, ns = w3.shape
    tps = ns // tn
    cm = 1024
    n_parts = len(h_parts)

    def body(ids_ref, *rest):
        h_refs, w_ref, o_ref = rest[:n_parts], rest[n_parts], rest[-1]
        part = part_of(ids_ref[1, pl.program_id(0)])
        for g, h_ref in enumerate(h_refs):
            @pl.when(part == g)
            def _():
                for m in range(SEQ // cm):
                    rows = pl.ds(m * cm, cm)
                    o_ref[rows, :] = jnp.dot(h_ref[rows, :], w_ref[...], preferred_element_type=F32).astype(BF16)

    resident = pl.BlockSpec((SEQ, kc), lambda t, ids: (0, 0))
    in_specs = [resident] * n_parts + [
        pl.BlockSpec((None, kc, tn), lambda t, ids: (ids[0, t] // tps, 0, ids[0, t] % tps))]
    args = [*h_parts, w3]
    aliases = {}
    if prev is not None:
        in_specs.append(pl.BlockSpec(memory_space=pl.ANY))
        args.append(prev)
        aliases = {n_parts + 2: 0}
    return pl.pallas_call(
        body, name=name,
        grid_spec=pltpu.PrefetchScalarGridSpec(
            num_scalar_prefetch=1, grid=(n_tiles,), in_specs=in_specs,
            out_specs=pl.BlockSpec((SEQ, tn), lambda t, ids: (0, ids[1, t]))),
        out_shape=jax.ShapeDtypeStruct((SEQ, total_tiles * tn), BF16),
        input_output_aliases=aliases, compiler_params=_params("arbitrary"),
    )(tile_ids, *args)


def _own_first(chip, total_tiles):
    own = total_tiles // N_CHIPS
    step = jnp.arange(total_tiles, dtype=jnp.int32)
    tiles = (own * chip + step) % total_tiles
    return jnp.stack([step[:own], tiles[:own]]), jnp.stack([tiles[own:], tiles[own:]]), own


def _mm_nt(dy, w3, *, tn, tile0, n_tiles, name, after=None):
    m_rows = dy.shape[0]
    _, kc, ns = w3.shape
    tps = ns // tn
    cm = 512
    extra = [] if after is None else [after]

    def body(dy_ref, w_ref, *rest):
        o_ref, acc = rest[-2], rest[-1]
        t = pl.program_id(0)

        @pl.when(t == 0)
        def _():
            acc[...] = jnp.zeros_like(acc)

        for m in range(m_rows // cm):
            rows = pl.ds(m * cm, cm)
            acc[rows, :] += lax.dot_general(dy_ref[rows, :], w_ref[...], NT_DIMS, preferred_element_type=F32)

        @pl.when(t == n_tiles - 1)
        def _():
            o_ref[...] = acc[...].astype(BF16)

    return pl.pallas_call(
        body, name=name, grid=(n_tiles,),
        in_specs=[pl.BlockSpec((m_rows, tn), lambda t: (0, t)),
                  pl.BlockSpec((None, kc, tn), lambda t: ((tile0 + t) // tps, 0, (tile0 + t) % tps))]
        + [pl.BlockSpec(memory_space=pl.ANY)] * len(extra),
        out_specs=pl.BlockSpec((m_rows, kc), lambda t: (0, 0)),
        out_shape=jax.ShapeDtypeStruct((m_rows, kc), BF16),
        scratch_shapes=[pltpu.VMEM((m_rows, kc), F32)],
        compiler_params=_params("arbitrary"),
    )(dy, w3, *extra)


CONV_CHUNK = 16


def _shift_copies(buf, shifted):
    rows = shifted.shape[1]
    for s in range(1, 8):
        shifted[s - 1] = buf[pl.ds(s, rows), :]


def _shifted_rows(buf, shifted, offset, r0):
    s = offset % 8
    if s == 0:
        return buf[pl.ds(r0 + offset, CONV_CHUNK), :]
    return shifted[s - 1, pl.ds(r0 + (offset - s), CONV_CHUNK), :]


def _spread_taps(w_ref, taps):
    for k in range(CONV_WIDTH):
        taps[k] = jnp.broadcast_to(w_ref[k:k + 1, :], (8, D_MODEL))


def _times_tap(taps, k, rows):
    return (rows.reshape(CONV_CHUNK // 8, 8, D_MODEL) * taps[k][None]).reshape(CONV_CHUNK, D_MODEL)


def _conv_fwd(proj, conv_w, conv_b, ln_g, ln_b, name):
    tm = ROW_TILE
    hb = tm // HALO

    def body(vg_ref, halo_ref, z_ref, w_ref, b_ref, g_ref, be_ref, u5_ref, u5t_ref, u2_ref, buf, shifted, taps):
        i = pl.program_id(0)
        u1 = vg_ref[:, :D_MODEL].astype(F32) * _sigmoid(vg_ref[:, D_MODEL:].astype(F32))
        u1h = halo_ref[:, :D_MODEL].astype(F32) * _sigmoid(halo_ref[:, D_MODEL:].astype(F32))
        buf[pl.ds(0, HALO), :] = jnp.where(i > 0, u1h, 0.0)
        buf[pl.ds(HALO, tm), :] = u1
        _shift_copies(buf, shifted)
        _spread_taps(w_ref, taps)

        def chunk(ci, carry):
            r0 = pl.multiple_of(ci * CONV_CHUNK, CONV_CHUNK)
            acc = jnp.broadcast_to(b_ref[...], (CONV_CHUNK, D_MODEL))
            for k in range(CONV_WIDTH):
                acc = acc + _times_tap(taps, k, _shifted_rows(buf, shifted, HALO - (CONV_WIDTH - 1) + k, r0))
            u2_ref[pl.ds(r0, CONV_CHUNK), :] = acc
            return carry

        lax.fori_loop(0, tm // CONV_CHUNK, chunk, 0)
        acc = u2_ref[...]
        mu = jnp.mean(acc, axis=-1, keepdims=True)
        xc = acc - mu
        rstd = lax.rsqrt(jnp.mean(xc * xc, axis=-1, keepdims=True) + NORM_EPS)
        u3 = xc * rstd * g_ref[...] + be_ref[...]
        zv = z_ref[...].astype(F32)
        u5 = u3 * _sigmoid(u3) * (zv * _sigmoid(zv))
        u5_ref[...] = u5.astype(BF16)
        u5t_ref[...] = u5.T.astype(BF16)

    return pl.pallas_call(
        body, name=name, grid=(SEQ // tm,),
        in_specs=[pl.BlockSpec((tm, 2 * D_MODEL), lambda i: (i, 0)),
                  pl.BlockSpec((HALO, 2 * D_MODEL), lambda i: (jnp.maximum(i * hb - 1, 0), 0)),
                  _row_spec(tm, D_MODEL, 2),
                  _vec_spec(CONV_WIDTH, D_MODEL)] + [_vec_spec(1, D_MODEL)] * 3,
        out_specs=[_row_spec(tm, D_MODEL), pl.BlockSpec((D_MODEL, tm), lambda i: (0, i)), _row_spec(tm, D_MODEL)],
        out_shape=[jax.ShapeDtypeStruct((SEQ, D_MODEL), BF16), jax.ShapeDtypeStruct((D_MODEL, SEQ), BF16),
                   jax.ShapeDtypeStruct((SEQ, D_MODEL), F32)],
        scratch_shapes=[pltpu.VMEM((HALO + tm, D_MODEL), F32), pltpu.VMEM((7, HALO + tm - 8, D_MODEL), F32),
                        pltpu.VMEM((CONV_WIDTH, 8, D_MODEL), F32)],
        compiler_params=_params("parallel"),
    )(proj, proj, proj, conv_w, conv_b, ln_g, ln_b)


def _conv_bwd_pointwise(dy, w_out, proj, u2, ln_g, ln_b, name):
    tm = ROW_TILE

    def body(dy_ref, w_ref, z_ref, u2_ref, g_ref, be_ref, du2_ref, dz_ref, sums_ref):
        u2v = u2_ref[...]
        mu = jnp.mean(u2v, axis=-1, keepdims=True)
        xc = u2v - mu
        rstd = lax.rsqrt(jnp.mean(xc * xc, axis=-1, keepdims=True) + NORM_EPS)
        xhat = xc * rstd
        u3 = xhat * g_ref[...] + be_ref[...]
        s3 = _sigmoid(u3)
        u4 = u3 * s3
        zv = z_ref[...].astype(F32)
        sz = _sigmoid(zv)
        du5v = lax.dot_general(dy_ref[...], w_ref[...], NT_DIMS, preferred_element_type=F32)
        dz_ref[...] = du5v * u4 * (sz * (1.0 + zv * (1.0 - sz)))
        du3 = du5v * (zv * sz) * (s3 * (1.0 + u3 * (1.0 - s3)))
        dxhat = du3 * g_ref[...]
        du2 = rstd * (dxhat - jnp.mean(dxhat, axis=-1, keepdims=True)
                      - xhat * jnp.mean(dxhat * xhat, axis=-1, keepdims=True))
        du2_ref[...] = du2
        sums = jnp.concatenate([
            jnp.sum(du3 * xhat, axis=0, keepdims=True),
            jnp.sum(du3, axis=0, keepdims=True),
            jnp.sum(du2, axis=0, keepdims=True),
            jnp.zeros((5, D_MODEL), F32)], axis=0)

        @pl.when(pl.program_id(0) == 0)
        def _():
            sums_ref[...] = jnp.zeros_like(sums_ref)

        sums_ref[...] += sums

    return pl.pallas_call(
        body, name=name, grid=(SEQ // tm,),
        in_specs=[_row_spec(tm, D_MODEL), _vec_spec(D_MODEL, D_MODEL), _row_spec(tm, D_MODEL, 2),
                  _row_spec(tm, D_MODEL), _vec_spec(1, D_MODEL), _vec_spec(1, D_MODEL)],
        out_specs=[_row_spec(tm, D_MODEL), _row_spec(tm, D_MODEL), _vec_spec(8, D_MODEL)],
        out_shape=[jax.ShapeDtypeStruct((SEQ, D_MODEL), F32), jax.ShapeDtypeStruct((SEQ, D_MODEL), F32),
                   jax.ShapeDtypeStruct((8, D_MODEL), F32)],
        compiler_params=_params("arbitrary"),
    )(dy, w_out, proj, u2, ln_g, ln_b)


def _conv_bwd_taps(du2, dz, proj, conv_w, name):
    tm = ROW_TILE
    hb = tm // HALO
    n_blocks = SEQ // tm

    def body(du2_ref, dnext_ref, dz_ref, vg_ref, w_ref, dproj_ref, dw_ref, dbuf, dshift, sgbuf, ubuf, dwacc, taps):
        i = pl.program_id(0)
        _spread_taps(w_ref, taps)
        sg = _sigmoid(vg_ref[:, D_MODEL:].astype(F32))
        sgbuf[...] = sg
        ubuf[...] = vg_ref[:, :D_MODEL].astype(F32) * sg
        dbuf[pl.ds(0, tm), :] = du2_ref[...]
        dbuf[pl.ds(tm, HALO), :] = jnp.where(i < n_blocks - 1, dnext_ref[...], 0.0)
        _shift_copies(dbuf, dshift)

        @pl.when(i == 0)
        def _():
            dwacc[...] = jnp.zeros_like(dwacc)

        def chunk(ci, carry):
            r0 = pl.multiple_of(ci * CONV_CHUNK, CONV_CHUNK)
            rows = pl.ds(r0, CONV_CHUNK)
            u1c = ubuf[rows, :]
            du1 = jnp.zeros((CONV_CHUNK, D_MODEL), F32)
            for k in range(CONV_WIDTH):
                ahead = _shifted_rows(dbuf, dshift, CONV_WIDTH - 1 - k, r0)
                du1 = du1 + _times_tap(taps, k, ahead)
                prod = u1c * ahead
                dwacc[k] += prod[0:8] + prod[8:16]
            sgc = sgbuf[rows, :]
            dval = du1 * sgc
            dproj_ref[rows, 0:D_MODEL] = dval.astype(BF16)
            dproj_ref[rows, D_MODEL:2 * D_MODEL] = (
                dval * vg_ref[rows, 0:D_MODEL].astype(F32) * (1.0 - sgc)).astype(BF16)
            return carry

        lax.fori_loop(0, tm // CONV_CHUNK, chunk, 0)
        dproj_ref[:, 2 * D_MODEL:] = dz_ref[...].astype(BF16)

        @pl.when(i == n_blocks - 1)
        def _():
            for k in range(CONV_WIDTH):
                dw_ref[k:k + 1, :] = jnp.sum(dwacc[k], axis=0, keepdims=True)
            dw_ref[CONV_WIDTH:, :] = jnp.zeros((32 - CONV_WIDTH, D_MODEL), F32)

    return pl.pallas_call(
        body, name=name, grid=(n_blocks,),
        in_specs=[_row_spec(tm, D_MODEL),
                  pl.BlockSpec((HALO, D_MODEL), lambda i: (jnp.minimum((i + 1) * hb, SEQ // HALO - 1), 0)),
                  _row_spec(tm, D_MODEL),
                  pl.BlockSpec((tm, 2 * D_MODEL), lambda i: (i, 0)),
                  _vec_spec(CONV_WIDTH, D_MODEL)],
        out_specs=[_row_spec(tm, 3 * D_MODEL), _vec_spec(32, D_MODEL)],
        out_shape=[jax.ShapeDtypeStruct((SEQ, 3 * D_MODEL), BF16), jax.ShapeDtypeStruct((32, D_MODEL), F32)],
        scratch_shapes=[pltpu.VMEM((tm + HALO, D_MODEL), F32), pltpu.VMEM((7, HALO + tm - 8, D_MODEL), F32),
                        pltpu.VMEM((tm, D_MODEL), F32), pltpu.VMEM((tm, D_MODEL), F32),
                        pltpu.VMEM((CONV_WIDTH, 8, D_MODEL), F32), pltpu.VMEM((CONV_WIDTH, 8, D_MODEL), F32)],
        compiler_params=_params("arbitrary"),
    )(du2, du2, dz, proj, conv_w)


def _out_a(u5, w_out, x, gate, g1, scale1, shift1, name):
    tm = ROW_TILE
    n_d = len(DILATIONS)

    def body(u_ref, w_ref, x_ref, gate_ref, g_ref, sc_ref, sh_ref, x1_ref, y_ref, ht_ref, *rest):
        h_refs, nat = rest[:n_d], rest[-1]
        y = jnp.dot(u_ref[...], w_ref[...], preferred_element_type=F32)
        x1 = x_ref[...] + gate_ref[...] * y
        y_ref[...] = y.astype(BF16)
        x1_ref[...] = x1
        h = _normmod(x1, g_ref[...], sc_ref[...], sh_ref[...])
        ht_ref[...] = h.T.astype(BF16)
        for h_ref, d in zip(h_refs, DILATIONS):
            _store_classes(h_ref, h, nat, d)

    res = pl.pallas_call(
        body, name=name, grid=(SEQ // tm,),
        in_specs=[_row_spec(tm, D_MODEL), _vec_spec(D_MODEL, D_MODEL), _row_spec(tm, D_MODEL)]
        + [_vec_spec(1, D_MODEL)] * 4,
        out_specs=[_row_spec(tm, D_MODEL), _row_spec(tm, D_MODEL), pl.BlockSpec((D_MODEL, tm), lambda i: (0, i))]
        + [_class_spec(tm, d) for d in DILATIONS],
        out_shape=[jax.ShapeDtypeStruct((SEQ, D_MODEL), F32), jax.ShapeDtypeStruct((SEQ, D_MODEL), BF16),
                   jax.ShapeDtypeStruct((D_MODEL, SEQ), BF16)] + [_class_shape(d, BF16) for d in DILATIONS],
        scratch_shapes=[_natural_scratch(tm)],
        compiler_params=_params("parallel"),
    )(u5, w_out, x, gate, g1, scale1, shift1)
    return res[0], res[1], res[2], [a.reshape(SEQ, D_MODEL) for a in res[3:]]


def _out_b_loss(u, w_out, x1, gate, target, name):
    tm = ROW_TILE

    def body(u_ref, w_ref, x_ref, gate_ref, t_ref, e_ref, dy_ref, sums_ref):
        y = jnp.dot(u_ref[...], w_ref[...], preferred_element_type=F32)
        diff = x_ref[...] + gate_ref[...] * y - t_ref[...]
        e = diff * (1.0 / D_MODEL)
        e_ref[...] = e
        dy_ref[...] = (e * gate_ref[...]).astype(BF16)
        sums = jnp.concatenate([
            jnp.sum(e * y, axis=0, keepdims=True),
            jnp.sum(diff * diff, axis=0, keepdims=True),
            jnp.zeros((6, D_MODEL), F32)], axis=0)

        @pl.when(pl.program_id(0) == 0)
        def _():
            sums_ref[...] = jnp.zeros_like(sums_ref)

        sums_ref[...] += sums

    return pl.pallas_call(
        body, name=name, grid=(SEQ // tm,),
        in_specs=[_row_spec(tm, D_MODEL), _vec_spec(D_MODEL, D_MODEL), _row_spec(tm, D_MODEL),
                  _vec_spec(1, D_MODEL), _row_spec(tm, D_MODEL)],
        out_specs=[_row_spec(tm, D_MODEL), _row_spec(tm, D_MODEL), _vec_spec(8, D_MODEL)],
        out_shape=[jax.ShapeDtypeStruct((SEQ, D_MODEL), F32), jax.ShapeDtypeStruct((SEQ, D_MODEL), BF16),
                   jax.ShapeDtypeStruct((8, D_MODEL), F32)],
        compiler_params=_params("arbitrary"),
    )(u, w_out, x1, gate, target)


def _seg_matrix():
    r = lax.broadcasted_iota(jnp.int32, (256, 256), 0) // HEAD_DIM
    c = lax.broadcasted_iota(jnp.int32, (256, 256), 1) // HEAD_DIM
    return jnp.where(r == c, 1.0 / HEAD_DIM, 0.0).astype(BF16)


def _segmean(v, seg):
    hi = v.astype(BF16)
    lo = (v - hi.astype(F32)).astype(BF16)
    outs = []
    for c0 in range(0, D_MODEL, 256):
        outs.append(jnp.dot(hi[:, c0:c0 + 256], seg, preferred_element_type=F32)
                    + jnp.dot(lo[:, c0:c0 + 256], seg, preferred_element_type=F32))
    return jnp.concatenate(outs, axis=1)


def _qk_rstd(v, seg):
    return lax.rsqrt(_segmean(v * v, seg) + NORM_EPS)


def _qknorm_fwd(proj, group, qw, kw, seg, name):
    tm = ROW_TILE

    def body(q_in, k_in, qw_ref, kw_ref, seg_ref, q_ref, k_ref):
        segv = seg_ref[...]
        q = q_in[...].astype(F32)
        k = k_in[...].astype(F32)
        q_ref[...] = (q * _qk_rstd(q, segv) * qw_ref[...] * HEAD_DIM ** -0.5).astype(BF16)
        k_ref[...] = (k * _qk_rstd(k, segv) * kw_ref[...]).astype(BF16)

    return pl.pallas_call(
        body, name=name, grid=(SEQ // tm,),
        in_specs=[_row_spec(tm, D_MODEL, 3 * group), _row_spec(tm, D_MODEL, 3 * group + 1),
                  _vec_spec(1, D_MODEL), _vec_spec(1, D_MODEL), _vec_spec(256, 256)],
        out_specs=[_row_spec(tm, D_MODEL)] * 2,
        out_shape=[jax.ShapeDtypeStruct((SEQ, D_MODEL), BF16)] * 2,
        compiler_params=_params("parallel"),
    )(proj, proj, qw, kw, seg)


def _attn_masks(b, bpc, dilation, transposed=False):
    keys = ATTN_BLOCK if bpc == 1 else 2 * ATTN_BLOCK
    shape, q_axis = ((keys, ATTN_BLOCK), 1) if transposed else ((ATTN_BLOCK, keys), 0)
    qi = lax.broadcasted_iota(jnp.int32, shape, q_axis)
    kj = lax.broadcasted_iota(jnp.int32, shape, 1 - q_axis)
    if bpc == 1:
        steps = qi - kj
        return (steps * dilation).astype(F32), steps >= 0
    steps = qi + ATTN_BLOCK - kj
    has_prev = (b % bpc) != 0
    valid = (steps >= 0) & (steps <= ATTN_BLOCK) & (has_prev | (kj >= ATTN_BLOCK))
    return (steps * dilation).astype(F32), valid


MASKED = 1e30


def _bias_scratch(bpc):
    return pltpu.VMEM((1 if bpc == 1 else 2, N_HEADS, ATTN_BLOCK, (1 if bpc == 1 else 2) * ATTN_BLOCK), F32)


def _fill_bias(bias_ref, sl_ref, bpc, dilation):
    for variant in range(bias_ref.shape[0]):
        dist, valid = _attn_masks(variant, min(bpc, 2), dilation)
        bias_ref[variant] = jnp.where(valid[None], dist[None] * sl_ref[...], MASKED)


def _step_bias(bias_ref, b, bpc):
    if bpc == 1:
        return bias_ref[0]
    return bias_ref[jnp.where((b % bpc) != 0, 1, 0)]


def _key_tile(prev_ref, cur_ref, cols, bpc):
    if bpc == 1:
        return cur_ref[:, cols]
    return jnp.concatenate([prev_ref[:, cols], cur_ref[:, cols]], axis=0)


ATTN_HEADS_FWD = 16
ATTN_HEADS_BWD = 16
NT_DIMS = (((1,), (1,)), ((), ()))
BATCH_NT_DIMS = (((2,), (2,)), ((0,), (0,)))
BATCH_NN_DIMS = (((2,), (1,)), ((0,), (0,)))
BATCH_TN_DIMS = (((1,), (1,)), ((0,), (0,)))


def _head_stack(tile_of, heads):
    return jnp.stack([tile_of(slice(h * HEAD_DIM, (h + 1) * HEAD_DIM)) for h in range(heads)], axis=0)


def _attn_specs(heads, segment=0):
    width = heads * HEAD_DIM
    off = segment * (D_MODEL // width)
    last = SEQ // ATTN_BLOCK - 1
    cur = pl.BlockSpec((ATTN_BLOCK, width), lambda hg, b: (jnp.minimum(b, last), hg + off))
    prev = pl.BlockSpec((ATTN_BLOCK, width), lambda hg, b: (jnp.clip(b - 1, 0, last), hg + off))
    return cur, prev


def _attn_fwd(q, k, proj, group, slopes, dilation, name):
    bpc = SEQ // dilation // ATTN_BLOCK
    heads = ATTN_HEADS_FWD
    assert heads == N_HEADS
    cur, prev = _attn_specs(heads)
    v_cur, v_prev = _attn_specs(heads, segment=3 * group + 2)

    def body(sl_ref, q_ref, kp_ref, kc_ref, vp_ref, vc_ref, o_ref, lse_ref, bias_ref):
        b = pl.program_id(1)

        @pl.when(b == 0)
        def _():
            _fill_bias(bias_ref, sl_ref, bpc, dilation)

        q3 = _head_stack(lambda cols: q_ref[:, cols], heads)
        k3 = _head_stack(lambda cols: _key_tile(kp_ref, kc_ref, cols, bpc), heads)
        v3 = _head_stack(lambda cols: _key_tile(vp_ref, vc_ref, cols, bpc), heads)
        s = lax.dot_general(q3, k3, BATCH_NT_DIMS, preferred_element_type=F32)
        s = s - _step_bias(bias_ref, b, bpc)
        m = jnp.max(s, axis=-1, keepdims=True)
        p = jnp.exp(s - m)
        l = jnp.sum(p, axis=-1, keepdims=True)
        o3 = lax.dot_general(p.astype(BF16), v3, BATCH_NN_DIMS, preferred_element_type=F32) / l
        lse3 = m + jnp.log(l)
        for h in range(heads):
            o_ref[:, h * HEAD_DIM:(h + 1) * HEAD_DIM] = o3[h].astype(BF16)
        lse_ref[...] = jnp.concatenate([lse3[h] for h in range(heads)]
                                       + [jnp.zeros((ATTN_BLOCK, LANES - heads), F32)], axis=1)

    return pl.pallas_call(
        body, name=name, grid=(N_HEADS // heads, SEQ // ATTN_BLOCK),
        in_specs=[pl.BlockSpec((heads, 1, 1), lambda hg, b: (hg, 0, 0)), cur, prev, cur, v_prev, v_cur],
        out_specs=[cur, pl.BlockSpec((ATTN_BLOCK, LANES), lambda hg, b: (b, 0))],
        out_shape=[jax.ShapeDtypeStruct((SEQ, D_MODEL), BF16), jax.ShapeDtypeStruct((SEQ, LANES), F32)],
        scratch_shapes=[_bias_scratch(bpc)],
        compiler_params=_params("parallel", "arbitrary"),
    )(slopes.reshape(N_HEADS, 1, 1), q, k, k, proj, proj)


def _class_spec(tm, dilation, width=D_MODEL):
    if dilation == 1:
        return _row_spec(tm, width)
    return pl.BlockSpec((dilation, tm // dilation, width), lambda i: (0, i, 0))


def _class_shape(dilation, dtype, width=D_MODEL):
    if dilation == 1:
        return jax.ShapeDtypeStruct((SEQ, width), dtype)
    return jax.ShapeDtypeStruct((dilation, SEQ // dilation, width), dtype)


def _load_natural(in_ref, nat_ref, dilation):
    if dilation == 1:
        return in_ref[...].astype(F32)
    n = nat_ref.shape[1] // dilation
    tiles = in_ref.shape[-1] // LANES
    for r in range(dilation):
        for j in range(tiles):
            nat_ref.at[j][pl.ds(r, n, stride=dilation), :] = in_ref[r, :, j * LANES:(j + 1) * LANES].astype(F32)
    if tiles == 1:
        return nat_ref[0]
    return jnp.concatenate([nat_ref[j] for j in range(tiles)], axis=1)


def _store_classes(out_ref, value, nat_ref, dilation):
    if dilation == 1:
        out_ref[...] = value.astype(out_ref.dtype)
        return
    n = nat_ref.shape[1] // dilation
    tiles = value.shape[-1] // LANES
    for j in range(tiles):
        nat_ref[j] = value[:, j * LANES:(j + 1) * LANES]
    for r in range(dilation):
        for j in range(tiles):
            out_ref[r, :, j * LANES:(j + 1) * LANES] = (
                nat_ref.at[j][pl.ds(r, n, stride=dilation), :].astype(out_ref.dtype))


def _natural_scratch(tm):
    return pltpu.VMEM((D_MODEL // LANES, tm, LANES), F32)


def _head_selector():
    lane_head = lax.broadcasted_iota(jnp.int32, (D_MODEL, LANES), 0) // HEAD_DIM
    head = lax.broadcasted_iota(jnp.int32, (D_MODEL, LANES), 1)
    return (lane_head == head).astype(BF16)


def _dot_split(v, m01, dims):
    hi = v.astype(BF16)
    lo = (v - hi.astype(F32)).astype(BF16)
    return (lax.dot_general(hi, m01, dims, preferred_element_type=F32)
            + lax.dot_general(lo, m01, dims, preferred_element_type=F32))


def _merge_fwd(o_parts, lse_parts, z, sel, name):
    tm = ROW_TILE
    h_spec = pl.BlockSpec((tm, LANES), lambda i: (i, 0))

    def body(o0, o1, o2, l0, l1, l2, z_ref, sel_ref, u_ref, ut_ref, o_ref, lse_ref, nat):
        ls = [_load_natural(l, nat, d) for l, d in zip((l0, l1, l2), DILATIONS)]
        m = jnp.maximum(jnp.maximum(ls[0], ls[1]), ls[2])
        tot = m + jnp.log(jnp.exp(ls[0] - m) + jnp.exp(ls[1] - m) + jnp.exp(ls[2] - m))
        o = jnp.zeros((tm, D_MODEL), F32)
        for o_in, l, d in zip((o0, o1, o2), ls, DILATIONS):
            weight = _dot_split(jnp.exp(l - tot), sel_ref[...], NT_DIMS)
            o = o + weight * _load_natural(o_in, nat, d)
        zv = z_ref[...].astype(F32)
        u = o * (zv * _sigmoid(zv))
        u_ref[...] = u.astype(BF16)
        ut_ref[...] = u.T.astype(BF16)
        o_ref[...] = o.astype(BF16)
        lse_ref[...] = tot

    return pl.pallas_call(
        body, name=name, grid=(SEQ // tm,),
        in_specs=[_class_spec(tm, d) for d in DILATIONS] + [_class_spec(tm, d, LANES) for d in DILATIONS]
        + [_row_spec(tm, D_MODEL, B_Z_SEGMENT), _vec_spec(D_MODEL, LANES)],
        out_specs=[_row_spec(tm, D_MODEL), pl.BlockSpec((D_MODEL, tm), lambda i: (0, i)),
                   _row_spec(tm, D_MODEL), h_spec],
        out_shape=[jax.ShapeDtypeStruct((SEQ, D_MODEL), BF16), jax.ShapeDtypeStruct((D_MODEL, SEQ), BF16),
                   jax.ShapeDtypeStruct((SEQ, D_MODEL), BF16), jax.ShapeDtypeStruct((SEQ, LANES), F32)],
        scratch_shapes=[_natural_scratch(tm)],
        compiler_params=_params("parallel"),
    )(*o_parts, *lse_parts, z, sel)


def _merge_bwd(dy, w_out, o, lse, z, sel, name):
    tm = ROW_TILE
    n_d = len(DILATIONS)

    def body(dy_ref, w_ref, o_ref, lse_ref, z_ref, sel_ref, dz_ref, *rest):
        do_refs, delta_refs, lse_refs, nat = rest[:n_d], rest[n_d:2 * n_d], rest[2 * n_d:3 * n_d], rest[-1]
        zv = z_ref[...].astype(F32)
        sz = _sigmoid(zv)
        duv = lax.dot_general(dy_ref[...], w_ref[...], NT_DIMS, preferred_element_type=F32)
        ov = o_ref[...].astype(F32)
        do = duv * (zv * sz)
        dz_ref[...] = (duv * ov * (sz * (1.0 + zv * (1.0 - sz)))).astype(BF16)
        delta = _dot_split(do * ov, sel_ref[...], (((1,), (0,)), ((), ())))
        lv = lse_ref[...]
        for i, d in enumerate(DILATIONS):
            _store_classes(do_refs[i], do, nat, d)
            _store_classes(delta_refs[i], delta, nat, d)
            _store_classes(lse_refs[i], lv, nat, d)

    res = pl.pallas_call(
        body, name=name, grid=(SEQ // tm,),
        in_specs=[_row_spec(tm, D_MODEL), _vec_spec(D_MODEL, D_MODEL), _row_spec(tm, D_MODEL), _row_spec(tm, LANES),
                  _row_spec(tm, D_MODEL, B_Z_SEGMENT), _vec_spec(D_MODEL, LANES)],
        out_specs=[_row_spec(tm, D_MODEL)] + [_class_spec(tm, d) for d in DILATIONS]
        + [_class_spec(tm, d, LANES) for d in DILATIONS] * 2,
        out_shape=[jax.ShapeDtypeStruct((SEQ, D_MODEL), BF16)] + [_class_shape(d, BF16) for d in DILATIONS]
        + [_class_shape(d, F32, LANES) for d in DILATIONS] * 2,
        scratch_shapes=[_natural_scratch(tm)],
        compiler_params=_params("parallel"),
    )(dy, w_out, o, lse, z, sel)
    flat = lambda a: a.reshape(SEQ, a.shape[-1])
    return (res[0], [flat(a) for a in res[1:1 + n_d]], [flat(a) for a in res[1 + n_d:1 + 2 * n_d]],
            [flat(a) for a in res[1 + 2 * n_d:]])


def _attn_bwd(q, k, proj, group, do, lse, delta, slopes, dilation, name):
    bpc = SEQ // dilation // ATTN_BLOCK
    heads = ATTN_HEADS_BWD
    n_blocks = SEQ // ATTN_BLOCK
    carry = bpc > 1
    width = heads * HEAD_DIM
    cur, prev = _attn_specs(heads)
    v_cur, v_prev = _attn_specs(heads, segment=3 * group + 2)
    assert heads == N_HEADS
    per_head = pl.BlockSpec((ATTN_BLOCK, LANES), lambda hg, b: (jnp.minimum(b, n_blocks - 1), 0))
    scale = HEAD_DIM ** -0.5

    def body(sl_ref, q_ref, kp_ref, kc_ref, vp_ref, vc_ref, do_ref, lse_ref, dl_ref,
             dq_ref, dk_ref, dv_ref, *scratch):
        b = pl.program_id(1)
        if carry:
            dk_carry, dv_carry = scratch

            @pl.when(b == n_blocks)
            def _():
                dk_ref[...] = dk_carry[...].astype(BF16)
                dv_ref[...] = dv_carry[...].astype(BF16)

            @pl.when(b < n_blocks)
            def _():
                step(sl_ref, q_ref, kp_ref, kc_ref, vp_ref, vc_ref, do_ref, lse_ref, dl_ref,
                     dq_ref, dk_ref, dv_ref, dk_carry, dv_carry, b)
        else:
            step(sl_ref, q_ref, kp_ref, kc_ref, vp_ref, vc_ref, do_ref, lse_ref, dl_ref,
                 dq_ref, dk_ref, dv_ref, None, None, b)

    def step(sl_ref, q_ref, kp_ref, kc_ref, vp_ref, vc_ref, do_ref, lse_ref, dl_ref,
             dq_ref, dk_ref, dv_ref, dk_carry, dv_carry, b):
        if carry:
            @pl.when(b == 0)
            def _():
                dk_carry[...] = jnp.zeros_like(dk_carry)
                dv_carry[...] = jnp.zeros_like(dv_carry)

        q3 = _head_stack(lambda cols: q_ref[:, cols], heads)
        k3 = _head_stack(lambda cols: _key_tile(kp_ref, kc_ref, cols, bpc), heads)
        v3 = _head_stack(lambda cols: _key_tile(vp_ref, vc_ref, cols, bpc), heads)
        do3 = _head_stack(lambda cols: do_ref[:, cols], heads)
        lse_t = lse_ref[...].T
        dl_t = dl_ref[...].T
        lse3 = jnp.stack([lse_t[h:h + 1, :] for h in range(heads)], axis=0)
        dl3 = jnp.stack([dl_t[h:h + 1, :] for h in range(heads)], axis=0)
        s = lax.dot_general(k3, q3, BATCH_NT_DIMS, preferred_element_type=F32)
        dist, valid = _attn_masks(b, bpc, dilation, transposed=True)
        p = jnp.exp(jnp.where(valid[None], s - dist[None] * sl_ref[...], NEG_INF) - lse3)
        dp = lax.dot_general(v3, do3, BATCH_NT_DIMS, preferred_element_type=F32)
        ds = (p * (dp - dl3)).astype(BF16)
        dq3 = lax.dot_general(ds, k3, BATCH_TN_DIMS, preferred_element_type=F32) * scale
        dk3 = lax.dot_general(ds, q3, BATCH_NN_DIMS, preferred_element_type=F32)
        dv3 = lax.dot_general(p.astype(BF16), do3, BATCH_NN_DIMS, preferred_element_type=F32)
        for h in range(heads):
            cols = slice(h * HEAD_DIM, (h + 1) * HEAD_DIM)
            dq_ref[:, cols] = dq3[h].astype(BF16)
            if carry:
                dk_ref[:, cols] = (dk_carry[:, cols] + dk3[h, :ATTN_BLOCK]).astype(BF16)
                dv_ref[:, cols] = (dv_carry[:, cols] + dv3[h, :ATTN_BLOCK]).astype(BF16)
                dk_carry[:, cols] = dk3[h, ATTN_BLOCK:]
                dv_carry[:, cols] = dv3[h, ATTN_BLOCK:]
            else:
                dk_ref[:, cols] = dk3[h].astype(BF16)
                dv_ref[:, cols] = dv3[h].astype(BF16)

    kv_out = prev if carry else cur
    return pl.pallas_call(
        body, name=name, grid=(N_HEADS // heads, n_blocks + (1 if carry else 0)),
        in_specs=[pl.BlockSpec((heads, 1, 1), lambda hg, b: (hg, 0, 0)), cur, prev, cur, v_prev, v_cur,
                  cur, per_head, per_head],
        out_specs=[cur, kv_out, kv_out],
        out_shape=[jax.ShapeDtypeStruct((SEQ, D_MODEL), BF16)] * 3,
        scratch_shapes=[pltpu.VMEM((ATTN_BLOCK, width), F32)] * 2 if carry else [],
        compiler_params=_params("parallel", "arbitrary"),
    )(slopes.reshape(N_HEADS, 1, 1), q, k, k, proj, proj, do, lse, delta)


def _qknorm_bwd(proj, group, qw, kw, seg, dq, dk, dv, name):
    tm = ROW_TILE

    def body(q_in, k_in, qw_ref, kw_ref, seg_ref, dq_ref, dk_ref, dv_ref, dproj_ref, sums_ref):
        segv = seg_ref[...]
        sums = []
        for part, (raw_ref, w_ref, dn_ref) in enumerate(((q_in, qw_ref, dq_ref), (k_in, kw_ref, dk_ref))):
            raw = raw_ref[...].astype(F32)
            dn = dn_ref[...].astype(F32)
            r = _qk_rstd(raw, segv)
            xhat = raw * r
            gq = dn * w_ref[...]
            draw = r * (gq - xhat * _segmean(xhat * gq, segv))
            dproj_ref[:, part * D_MODEL:(part + 1) * D_MODEL] = draw.astype(BF16)
            sums.append(jnp.sum(dn * xhat, axis=0, keepdims=True))
        dproj_ref[:, 2 * D_MODEL:] = dv_ref[...]

        @pl.when(pl.program_id(0) == 0)
        def _():
            sums_ref[...] = jnp.zeros_like(sums_ref)

        sums_ref[...] += jnp.concatenate(sums + [jnp.zeros((6, D_MODEL), F32)], axis=0)

    return pl.pallas_call(
        body, name=name, grid=(SEQ // tm,),
        in_specs=[_row_spec(tm, D_MODEL, 3 * group), _row_spec(tm, D_MODEL, 3 * group + 1),
                  _vec_spec(1, D_MODEL), _vec_spec(1, D_MODEL), _vec_spec(256, 256)] + [_row_spec(tm, D_MODEL)] * 3,
        out_specs=[_row_spec(tm, 3 * D_MODEL), _vec_spec(8, D_MODEL)],
        out_shape=[jax.ShapeDtypeStruct((SEQ, 3 * D_MODEL), BF16), jax.ShapeDtypeStruct((8, D_MODEL), F32)],
        compiler_params=_params("arbitrary"),
    )(proj, proj, qw, kw, seg, dq, dk, dv)


B_TN = 512
B_GROUP_TILES = 3 * D_MODEL // B_TN
B_Z_TILE0 = 3 * B_GROUP_TILES
B_Z_TILES = D_MODEL // B_TN
B_TILES = B_Z_TILE0 + B_Z_TILES
B_Z_SEGMENT = 3 * len(DILATIONS)


def _local_step(x, target, mods, norm_g, conv_w, conv_b, ln_g, ln_b, q_norm, k_norm, chip, own_wa_in, own_wb_in,
                weights_a, weights_b, forward_weights_b, send_grads_b, forward_grads_b, send_grads_a):
    row = lambda a, i: a[i:i + 1]
    shift0, scale0, gate0 = row(mods[0], 0), row(mods[0], 1), row(mods[0], 2)
    shift1, scale1, gate1 = row(mods[1], 0), row(mods[1], 1), row(mods[1], 2)
    g0, g1 = row(norm_g, 0), row(norm_g, 1)
    seg = _seg_matrix()
    slopes = jnp.exp2(-8.0 * jnp.arange(1, N_HEADS + 1, dtype=F32) / N_HEADS)
    qw = [jnp.tile(q_norm[g:g + 1], (1, N_HEADS)) for g in range(3)]
    kw = [jnp.tile(k_norm[g:g + 1], (1, N_HEADS)) for g in range(3)]

    h0, h0t = _normmod_fwd(x, g0, scale0, shift0, "prenorm0")
    nsa = own_wa_in.shape[2]
    tiles_a = dict(tn=nsa, total_tiles=N_CHIPS, part_of=lambda tile: 0)
    own_ids, rest_ids, own_tiles = _own_first(chip, N_CHIPS)
    proj_a = _in_tiles([h0], own_wa_in, own_ids, own_tiles, name="a_in_own", **tiles_a)
    wa_in, wa_out = weights_a(proj_a)
    ja = wa_in.shape[0]
    proj_a = _in_tiles([h0], wa_in, rest_ids, N_CHIPS - own_tiles, name="a_in_rest", prev=proj_a, **tiles_a)
    u5, u5t, u2 = _conv_fwd(proj_a, conv_w, conv_b, ln_g, ln_b, "a_conv")
    x1, y_a, h1t, h1c = _out_a(u5, wa_out, x, gate0, g1, scale1, shift1, "a_out")

    tiles_b = dict(tn=B_TN, total_tiles=B_TILES,
                   part_of=lambda tile: jnp.where(tile >= B_Z_TILE0, 0, tile // B_GROUP_TILES))
    own_ids, rest_ids, own_tiles = _own_first(chip, B_TILES)
    proj_b = _in_tiles(h1c, own_wb_in, own_ids, own_tiles, name="b_in_own", **tiles_b)
    forward_weights_b(proj_b)
    wb_in, wb_out = weights_b(proj_b)
    jb, _, nsb = wb_in.shape
    proj_b = _in_tiles(h1c, wb_in, rest_ids, B_TILES - own_tiles, name="b_in_rest", prev=proj_b, **tiles_b)
    h1 = h1c[0]
    qkv, o_parts, lse_parts = [], [], []
    for g, d in enumerate(DILATIONS):
        qn, kn = _qknorm_fwd(proj_b, g, qw[g], kw[g], seg, f"b_qknorm_g{g}")
        og, lg = _attn_fwd(qn, kn, proj_b, g, slopes, d, f"b_attn_g{g}")
        qkv.append((qn, kn))
        o_parts.append(og if d == 1 else og.reshape(d, SEQ // d, D_MODEL))
        lse_parts.append(lg if d == 1 else lg.reshape(d, SEQ // d, LANES))
    sel = _head_selector()
    u_b, u_bt, o_b, lse_b = _merge_fwd(o_parts, lse_parts, proj_b, sel, "b_merge")
    e, dy_b, sums_loss = _out_b_loss(u_b, wb_out, x1, gate1, target, "b_out_loss")

    dwb_out = _mm(u_bt, dy_b, tn=D_MODEL, tile0=0, n_tiles=1, out_dtype=BF16, name="b_dwout")
    dz_b, do_c, delta_c, lse_c = _merge_bwd(dy_b, wb_out, o_b, lse_b, proj_b, sel, "b_merge_bwd")
    dwb_in = _mm(h1t, dz_b, tn=B_TN, tile0=B_Z_TILE0, n_tiles=B_Z_TILES, out_dtype=BF16, name="b_dwin_z",
                 out3d=(jb, nsb))
    dh1_parts = [_mm_nt(dz_b, wb_in, tn=B_TN, tile0=B_Z_TILE0, n_tiles=B_Z_TILES, name="b_dh_z")]
    qk_sums = []
    for g, d in enumerate(DILATIONS):
        qn, kn = qkv[g]
        dq, dk, dv = _attn_bwd(qn, kn, proj_b, g, do_c[g], lse_c[g], delta_c[g], slopes, d, f"b_attn_bwd_g{g}")
        dproj, sums_qk = _qknorm_bwd(proj_b, g, qw[g], kw[g], seg, dq, dk, dv, f"b_qknorm_bwd_g{g}")
        qk_sums.append(sums_qk)
        dwb_in = _mm(h1t if d == 1 else h1c[g], dproj, tn=B_TN, tile0=g * B_GROUP_TILES, n_tiles=B_GROUP_TILES,
                     out_dtype=BF16, name=f"b_dwin_g{g}", out3d=(jb, nsb), prev=dwb_in, transpose_lhs=d != 1)
        dh = _mm_nt(dproj, wb_in, tn=B_TN, tile0=g * B_GROUP_TILES, n_tiles=B_GROUP_TILES, name=f"b_dh_g{g}")
        dh1_parts.append(dh)
    token = send_grads_b(dwb_in, dwb_out)
    dx1, sums_n1, dy_a = _normmod_bwd(x1, g1, scale1 + token[0:1, 0:1], dh1_parts, e, "prenorm1_bwd",
                                      part_dilations=(1,) + DILATIONS, gated=(gate0, y_a))
    token = forward_grads_b(dx1)

    dwa_out = _mm(u5t, dy_a, tn=D_MODEL, tile0=0, n_tiles=1, out_dtype=BF16, name="a_dwout")
    du2, dz_a, sums_ln = _conv_bwd_pointwise(dy_a, wa_out, proj_a, u2, ln_g + token[0:1, 0:1], ln_b,
                                             "a_conv_bwd_pw")
    dproj_a, dconv_w = _conv_bwd_taps(du2, dz_a, proj_a, conv_w, "a_conv_bwd_taps")
    dwa_in = _mm(h0t, dproj_a, tn=nsa, tile0=0, n_tiles=ja, out_dtype=BF16, name="a_dwin", out3d=(ja, nsa))
    token = send_grads_a(dwa_in, dwa_out)
    dh0 = _mm_nt(dproj_a, wa_in, tn=nsa, tile0=0, n_tiles=ja, name="a_dh", after=token)
    grad_x, sums_n0 = _normmod_bwd(x, g0, scale0, [dh0], dx1, "prenorm0_bwd")

    small = dict(
        dnorm_g=jnp.concatenate([sums_n0[0:1], sums_n1[0:1]], axis=0),
        dmod0=jnp.concatenate([sums_n0[2:3], sums_n0[1:2], sums_n1[3:4]], axis=0),
        dmod1=jnp.concatenate([sums_n1[2:3], sums_n1[1:2], sums_loss[0:1]], axis=0),
        dln_g=sums_ln[0:1], dln_b=sums_ln[1:2], dconv_b=sums_ln[2:3],
        dconv_w=dconv_w[:CONV_WIDTH],
        dq_norm=jnp.concatenate([s[0:1] for s in qk_sums], axis=0),
        dk_norm=jnp.concatenate([s[1:2] for s in qk_sums], axis=0),
        loss_cols=sums_loss[1:2],
    )
    return grad_x, small


def _adamw(w, g, m, v, name, after=None, copy_grad=False):
    rows, cols = w.shape
    tr = rows if rows <= 128 else (256 if cols <= D_MODEL else 128)
    c1 = 1.0 / (1.0 - ADAM_B1 ** ADAM_STEP)
    c2 = 1.0 / (1.0 - ADAM_B2 ** ADAM_STEP)
    extra = [] if after is None else [after]
    n_out = 4 if copy_grad else 3

    def body(w_ref, g_ref, m_ref, v_ref, *rest):
        d_ref, mo_ref, vo_ref = rest[len(extra):len(extra) + 3]
        gv = g_ref[...]
        if copy_grad:
            rest[-1][...] = gv
        mn = ADAM_B1 * m_ref[...] + (1.0 - ADAM_B1) * gv
        vn = ADAM_B2 * v_ref[...] + (1.0 - ADAM_B2) * (gv * gv)
        mo_ref[...] = mn
        vo_ref[...] = vn
        d_ref[...] = -ADAM_LR * ((mn * c1) / (jnp.sqrt(vn * c2) + ADAM_EPS) + ADAM_WD * w_ref[...])

    spec = pl.BlockSpec((tr, cols), lambda i: (i, 0))
    return pl.pallas_call(
        body, name=name, grid=(rows // tr,),
        in_specs=[spec] * 4 + [pl.BlockSpec(memory_space=pl.ANY)] * len(extra), out_specs=[spec] * n_out,
        out_shape=[jax.ShapeDtypeStruct((rows, cols), F32)] * n_out,
        compiler_params=_params("parallel"),
    )(w, g, m, v, *extra)


def _cast_into_slot(w, chip_idx, name, keep_own=False, after=None):
    rows, cols = w.shape
    tr = 256
    extra = [] if after is None else [after]

    def body(ch_ref, w_ref, *rest):
        wb = w_ref[...].astype(BF16)
        for o_ref in rest[len(extra):]:
            o_ref[...] = wb

    slot_spec = pl.BlockSpec((None, tr, cols), lambda i, ch: (ch[0], i, 0))
    own_spec = pl.BlockSpec((None, tr, cols), lambda i, ch: (0, i, 0))
    res = pl.pallas_call(
        body, name=name,
        grid_spec=pltpu.PrefetchScalarGridSpec(
            num_scalar_prefetch=1, grid=(rows // tr,),
            in_specs=[pl.BlockSpec((tr, cols), lambda i, ch: (i, 0))] + [pl.BlockSpec(memory_space=pl.ANY)] * len(extra),
            out_specs=[slot_spec, own_spec] if keep_own else [slot_spec]),
        out_shape=[jax.ShapeDtypeStruct((N_CHIPS, rows, cols), BF16)]
        + ([jax.ShapeDtypeStruct((1, rows, cols), BF16)] if keep_own else []),
        compiler_params=_params("parallel"),
    )(chip_idx, w, *extra)
    return tuple(res) if keep_own else res[0]


def _position():
    x, y, c = lax.axis_index("x"), lax.axis_index("y"), lax.axis_index("c")
    return x, y, c


def _xor_peer(x, y, c, k):
    return (x ^ ((k >> 2) & 1), y ^ ((k >> 1) & 1), c ^ (k & 1))


def _chip_peer(x, y, k):
    return (x ^ ((k >> 1) & 1), y ^ (k & 1))


def _ada_forward(c_row, ada_w, ada_b, conv_w, after=()):
    ns = ada_w.shape[2]
    cw = conv_w.shape[1]

    def body(c_ref, w_ref, b_ref, cv_ref, *rest):
        (mod_ref, sc_ref, cvo_ref, c_all, mp, parts, cv_parts,
         send1, recv1, send2, recv2, send3, recv3, w_vmem, w_sem) = rest[len(after):]
        w_load = pltpu.make_async_copy(w_ref, w_vmem, w_sem)
        w_load.start()
        x, y, c = _position()
        me = 4 * x + 2 * y + c
        chip = 2 * x + y

        def c_copy(k):
            return pltpu.make_async_remote_copy(
                src_ref=c_all.at[me], dst_ref=c_all.at[me], send_sem=send1.at[k - 1], recv_sem=recv1.at[k - 1],
                device_id=_xor_peer(x, y, c, k), device_id_type=MESH)

        def cv_copy(k):
            px, py = _chip_peer(x, y, k)
            return pltpu.make_async_remote_copy(
                src_ref=cv_parts.at[chip], dst_ref=cv_parts.at[chip], send_sem=send3.at[k - 1],
                recv_sem=recv3.at[k - 1], device_id=(px, py, c), device_id_type=MESH)

        c_all[me] = c_ref[...]
        cv_parts[chip] = cv_ref[...]
        for k in range(1, N_DEV):
            c_copy(k).start()
        for k in range(1, N_CHIPS):
            cv_copy(k).start()
        for k in range(1, N_DEV):
            c_copy(k).wait_recv()
        cv = jnp.concatenate([c_all[i] for i in range(N_DEV)], axis=0)
        sc = cv * _sigmoid(cv)
        sc_ref[...] = sc
        w_load.wait()
        for l in range(2):
            res = jnp.dot(sc.astype(BF16), w_vmem[l].astype(BF16), preferred_element_type=F32)
            for i in range(N_DEV):
                mp[i, l:l + 1, :] = res[i:i + 1, :]

        def mod_copy(k):
            px, py = _chip_peer(x, y, k)
            return pltpu.make_async_remote_copy(
                src_ref=mp.at[4 * px + 2 * py + c], dst_ref=parts.at[chip], send_sem=send2.at[k - 1],
                recv_sem=recv2.at[k - 1], device_id=(px, py, c), device_id_type=MESH)

        for k in range(1, N_CHIPS):
            mod_copy(k).start()
        parts[chip] = mp[me]
        for k in range(1, N_CHIPS):
            mod_copy(k).wait_recv()
            cv_copy(k).wait_recv()
        mod_ref[...] = jnp.concatenate([parts[j] for j in range(N_CHIPS)], axis=1) + b_ref[...]
        cvo_ref[...] = jnp.concatenate([cv_parts[j] for j in range(N_CHIPS)], axis=1)
        for k in range(1, N_DEV):
            c_copy(k).wait_send()
        for k in range(1, N_CHIPS):
            mod_copy(k).wait_send()
            cv_copy(k).wait_send()

    vm = pl.BlockSpec(memory_space=pltpu.VMEM)
    return pl.pallas_call(
        body, name="ada_forward",
        in_specs=[vm, ANY_SPEC, vm, vm] + [ANY_SPEC] * len(after), out_specs=[vm] * 3,
        out_shape=[jax.ShapeDtypeStruct((2, 3 * D_MODEL), F32), jax.ShapeDtypeStruct((N_DEV, D_MODEL), F32),
                   jax.ShapeDtypeStruct((CONV_WIDTH, N_CHIPS * cw), F32)],
        scratch_shapes=[pltpu.VMEM((N_DEV, 1, D_MODEL), F32), pltpu.VMEM((N_DEV, 2, ns), F32),
                        pltpu.VMEM((N_CHIPS, 2, ns), F32), pltpu.VMEM((N_CHIPS, CONV_WIDTH, cw), F32),
                        pltpu.SemaphoreType.DMA((N_DEV - 1,)), pltpu.SemaphoreType.DMA((N_DEV - 1,)),
                        pltpu.SemaphoreType.DMA((N_CHIPS - 1,)), pltpu.SemaphoreType.DMA((N_CHIPS - 1,)),
                        pltpu.SemaphoreType.DMA((N_CHIPS - 1,)), pltpu.SemaphoreType.DMA((N_CHIPS - 1,)),
                        pltpu.VMEM(ada_w.shape, F32), pltpu.SemaphoreType.DMA(())],
        compiler_params=pltpu.CompilerParams(vmem_limit_bytes=VMEM_LIMIT_BYTES),
    )(c_row, ada_w, ada_b, conv_w, *after)


HBM_SPEC = pl.BlockSpec(memory_space=pltpu.HBM)
ANY_SPEC = pl.BlockSpec(memory_space=pl.ANY)
SEM_SPEC = pl.BlockSpec(memory_space=pltpu.SEMAPHORE)
SPLIT_PARAMS = dict(compiler_params=pltpu.CompilerParams(has_side_effects=pltpu.SideEffectType.DATAFLOW_SIDE_EFFECTING))
TOKEN = jax.ShapeDtypeStruct((8, 128), F32)
ENTRY_HANDSHAKES = {name: (i, peers) for i, (name, peers) in enumerate((
    ("gather_start_a", "chips"), ("gather_start_b", "chips"),
    ("gather_forward_a", "sibling"), ("gather_forward_b", "sibling"),
    ("reduce_d2d_start_b", "sibling"), ("reduce_d2d_start_a", "sibling"),
    ("reduce_ici_start_b", "chips"), ("reduce_ici_start_a", "chips"),
    ("reduce_share_start_b", "sibling"), ("reduce_share_start_a", "sibling"),
    ("small_gather_start", "devices")))}


def _hbm(arrays):
    return [pltpu.with_memory_space_constraint(a, pltpu.HBM) for a in arrays]


def _hbm_like(arrays):
    return [pltpu.HBM(a.shape, a.dtype) for a in arrays]


def _gather_start(lands, after, name):
    n = len(lands)

    def body(*refs):
        _handshake(ENTRY_HANDSHAKES[name][1])
        ins = refs[:n]
        send, recv = refs[n + 1], refs[n + 2]
        x, y, c = _position()
        chip = 2 * x + y
        for t in range(n):
            rh = ins[t].shape[1] // 2
            for k in range(1, N_CHIPS):
                px, py = _chip_peer(x, y, k)
                block = ins[t].at[chip, pl.ds(c * rh, rh)]
                pltpu.make_async_remote_copy(
                    src_ref=block, dst_ref=block, send_sem=send.at[3 * t + k - 1], recv_sem=recv.at[3 * t + k - 1],
                    device_id=(px, py, c), device_id_type=MESH).start()
        refs[-1][...] = jnp.zeros(TOKEN.shape, F32)

    res = pl.pallas_call(
        body, name=name, in_specs=[HBM_SPEC] * n + [ANY_SPEC],
        out_specs=(SEM_SPEC, SEM_SPEC, *[HBM_SPEC] * n, pl.BlockSpec(memory_space=pltpu.VMEM)),
        out_shape=(pltpu.SemaphoreType.DMA((3 * n,)), pltpu.SemaphoreType.DMA((3 * n,)), *_hbm_like(lands), TOKEN),
        input_output_aliases={t: 2 + t for t in range(n)}, **_split_params(name),
    )(*_hbm(lands), after)
    return res[0], res[1], list(res[2:2 + n]), res[-1]


def _gather_forward(send, recv, lands, after, name):
    n = len(lands)

    def body(*refs):
        _handshake(ENTRY_HANDSHAKES[name][1])
        ins = refs[:n]
        send1, recv1 = refs[n], refs[n + 1]
        send2, recv2 = refs[n + 3], refs[n + 4]
        x, y, c = _position()
        chip = 2 * x + y
        for t in range(n):
            rh = ins[t].shape[1] // 2
            half = pl.ds(c * rh, rh)
            for k in range(1, N_CHIPS):
                px, py = _chip_peer(x, y, k)
                s = 3 * t + k - 1
                got = ins[t].at[2 * px + py, half]
                cp = pltpu.make_async_remote_copy(
                    src_ref=ins[t].at[chip, half], dst_ref=got, send_sem=send1.at[s], recv_sem=recv1.at[s],
                    device_id=(px, py, c), device_id_type=MESH)
                cp.wait_send()
                cp.wait_recv()
                pltpu.make_async_remote_copy(
                    src_ref=got, dst_ref=got, send_sem=send2.at[s], recv_sem=recv2.at[s],
                    device_id=(x, y, 1 - c), device_id_type=MESH).start()
        refs[-1][...] = jnp.zeros(TOKEN.shape, F32)

    res = pl.pallas_call(
        body, name=name, in_specs=[HBM_SPEC] * n + [SEM_SPEC, SEM_SPEC, ANY_SPEC],
        out_specs=(SEM_SPEC, SEM_SPEC, *[HBM_SPEC] * n, pl.BlockSpec(memory_space=pltpu.VMEM)),
        out_shape=(pltpu.SemaphoreType.DMA((3 * n,)), pltpu.SemaphoreType.DMA((3 * n,)), *_hbm_like(lands), TOKEN),
        input_output_aliases={t: 2 + t for t in range(n)}, **_split_params(name),
    )(*lands, send, recv, after)
    return res[0], res[1], list(res[2:2 + n]), res[-1]


def _gather_wait(send, recv, lands, after, name):
    n = len(lands)

    def body(*refs):
        ins = refs[:n]
        send_ref, recv_ref = refs[n], refs[n + 1]
        x, y, c = _position()
        for t in range(n):
            rh = ins[t].shape[1] // 2
            for k in range(1, N_CHIPS):
                px, py = _chip_peer(x, y, k)
                cp = pltpu.make_async_remote_copy(
                    src_ref=ins[t].at[2 * px + py, pl.ds(c * rh, rh)],
                    dst_ref=ins[t].at[2 * px + py, pl.ds((1 - c) * rh, rh)], send_sem=send_ref.at[3 * t + k - 1],
                    recv_sem=recv_ref.at[3 * t + k - 1], device_id=(x, y, 1 - c), device_id_type=MESH)
                cp.wait_send()
                cp.wait_recv()

    res = pl.pallas_call(
        body, name=name, in_specs=[HBM_SPEC] * n + [SEM_SPEC, SEM_SPEC, ANY_SPEC], out_specs=[HBM_SPEC] * n,
        out_shape=_hbm_like(lands), input_output_aliases={t: t for t in range(n)}, **SPLIT_PARAMS,
    )(*lands, send, recv, after)
    return list(res)


def _handshake(peers):
    x, y, c = _position()
    if peers == "sibling":
        ids = [(x, y, 1 - c)]
    elif peers == "chips":
        ids = [(*_chip_peer(x, y, k), c) for k in range(1, N_CHIPS)]
    else:
        ids = [_xor_peer(x, y, c, k) for k in range(1, N_DEV)]
    barrier = pltpu.get_barrier_semaphore()
    for peer in ids:
        pl.semaphore_signal(barrier, inc=1, device_id=peer, device_id_type=MESH)
    pl.semaphore_wait(barrier, len(ids))


def _split_params(name):
    return dict(compiler_params=pltpu.CompilerParams(
        has_side_effects=pltpu.SideEffectType.DATAFLOW_SIDE_EFFECTING, collective_id=ENTRY_HANDSHAKES[name][0]))


def _split_start(name, arrays, n_sems, after, issue):
    m = len(arrays)

    def body(*refs):
        _handshake(ENTRY_HANDSHAKES[name][1])
        issue(refs[:m], refs[m + 1], refs[m + 2])
        refs[-1][...] = jnp.zeros(TOKEN.shape, F32)

    res = pl.pallas_call(
        body, name=name, in_specs=[HBM_SPEC] * m + [ANY_SPEC],
        out_specs=(SEM_SPEC, SEM_SPEC, *[HBM_SPEC] * m, pl.BlockSpec(memory_space=pltpu.VMEM)),
        out_shape=(pltpu.SemaphoreType.DMA((n_sems,)), pltpu.SemaphoreType.DMA((n_sems,)), *_hbm_like(arrays), TOKEN),
        input_output_aliases={t: 2 + t for t in range(m)}, **_split_params(name),
    )(*_hbm(arrays), after)
    return res[0], res[1], list(res[2:2 + m]), res[-1]


def _split_wait(name, arrays, send, recv, after, await_all):
    m = len(arrays)

    def body(*refs):
        await_all(refs[:m], refs[m], refs[m + 1])

    res = pl.pallas_call(
        body, name=name, in_specs=[HBM_SPEC] * m + [SEM_SPEC, SEM_SPEC, ANY_SPEC], out_specs=[HBM_SPEC] * m,
        out_shape=_hbm_like(arrays), input_output_aliases={t: t for t in range(m)}, **SPLIT_PARAMS,
    )(*arrays, send, recv, after)
    return list(res)


def _sibling_copies(refs, send, recv, n):
    x, y, c = _position()
    cps = []
    for t in range(n):
        rh = refs[t].shape[1] // 2
        cps.append(pltpu.make_async_remote_copy(
            src_ref=refs[t].at[pl.ds(0, N_CHIPS), pl.ds((1 - c) * rh, rh)], dst_ref=refs[n + t],
            send_sem=send.at[t], recv_sem=recv.at[t], device_id=(x, y, 1 - c), device_id_type=MESH))
    return cps


def _reduce_sibling_start(grads, after, name):
    n = len(grads)
    lands = [lax.empty((N_CHIPS, g.shape[1] // 2, g.shape[2]), BF16) for g in grads]

    def issue(refs, send, recv):
        for cp in _sibling_copies(refs, send, recv, n):
            cp.start()

    return _split_start(name, list(grads) + lands, n, after, issue)


def _reduce_sibling_wait(send, recv, arrays, after, name):
    n = len(arrays) // 2

    def await_all(refs, send_ref, recv_ref):
        for cp in _sibling_copies(refs, send_ref, recv_ref, n):
            cp.wait_send()
            cp.wait_recv()

    res = _split_wait(name, arrays, send, recv, after, await_all)
    return res[:n], res[n:]


def _add_sibling_half(grad, got, dev_idx, name):
    j, r, cols = grad.shape
    rh = r // 2
    tr = rh
    nb = rh // tr

    def body(idx_ref, g_ref, got_ref, out_ref):
        out_ref[...] = (g_ref[...].astype(F32) + got_ref[...].astype(F32)).astype(BF16)

    return pl.pallas_call(
        body, name=name,
        grid_spec=pltpu.PrefetchScalarGridSpec(
            num_scalar_prefetch=1, grid=(j, nb),
            in_specs=[pl.BlockSpec((None, tr, cols), lambda jj, i, idx: (jj, idx[2] * nb + i, 0)),
                      pl.BlockSpec((None, tr, cols), lambda jj, i, idx: (jj, i, 0))],
            out_specs=pl.BlockSpec((None, tr, cols), lambda jj, i, idx: (jj, i, 0))),
        out_shape=jax.ShapeDtypeStruct((j, rh, cols), BF16),
        compiler_params=_params("parallel", "parallel"),
    )(dev_idx, grad, got)


def _chip_copies(refs, send, recv, n, receiving):
    x, y, c = _position()
    chip = 2 * x + y
    cps = []
    for t in range(n):
        for k in range(1, N_CHIPS):
            px, py = _chip_peer(x, y, k)
            cps.append(pltpu.make_async_remote_copy(
                src_ref=refs[t].at[2 * px + py], dst_ref=refs[n + t].at[2 * px + py if receiving else chip],
                send_sem=send.at[3 * t + k - 1], recv_sem=recv.at[3 * t + k - 1],
                device_id=(px, py, c), device_id_type=MESH))
    return cps


def _reduce_chips_start(partials, after, name):
    n = len(partials)
    lands = [lax.empty(p.shape, BF16) for p in partials]

    def issue(refs, send, recv):
        for cp in _chip_copies(refs, send, recv, n, False):
            cp.start()

    return _split_start(name, list(partials) + lands, 3 * n, after, issue)


def _reduce_chips_wait(send, recv, arrays, after, name):
    n = len(arrays) // 2

    def await_all(refs, send_ref, recv_ref):
        for cp in _chip_copies(refs, send_ref, recv_ref, n, True):
            cp.wait_send()
            cp.wait_recv()

    res = _split_wait(name, arrays, send, recv, after, await_all)
    return res[:n], res[n:]


def _sum_partials(land, partial, dev_idx, name):
    _, rh, cols = land.shape
    tr = min(rh, 256)
    nb = rh // tr

    def body(idx_ref, l_ref, p_ref, o_ref):
        chip = idx_ref[1]
        acc = jnp.where(chip == 0, p_ref[...], l_ref[0]).astype(F32)
        for s in range(1, N_CHIPS):
            acc = acc + jnp.where(chip == s, p_ref[...], l_ref[s]).astype(F32)
        o_ref[...] = acc

    return pl.pallas_call(
        body, name=name,
        grid_spec=pltpu.PrefetchScalarGridSpec(
            num_scalar_prefetch=1, grid=(nb,),
            in_specs=[pl.BlockSpec((N_CHIPS, tr, cols), lambda i, idx: (0, i, 0)),
                      pl.BlockSpec((None, tr, cols), lambda i, idx: (idx[1], i, 0))],
            out_specs=pl.BlockSpec((tr, cols), lambda i, idx: (idx[2] * nb + i, 0))),
        out_shape=jax.ShapeDtypeStruct((2 * rh, cols), F32), compiler_params=_params("parallel"),
    )(dev_idx, land, partial)


def _half_copies(refs, send, recv, receiving):
    x, y, c = _position()
    cps = []
    for t, ref in enumerate(refs):
        rh = ref.shape[0] // 2
        cps.append(pltpu.make_async_remote_copy(
            src_ref=ref.at[pl.ds(c * rh, rh)], dst_ref=ref.at[pl.ds(((1 - c) if receiving else c) * rh, rh)],
            send_sem=send.at[t], recv_sem=recv.at[t], device_id=(x, y, 1 - c), device_id_type=MESH))
    return cps


def _share_halves_start(totals, after, name):
    def issue(refs, send, recv):
        for cp in _half_copies(refs, send, recv, False):
            cp.start()

    return _split_start(name, list(totals), len(totals), after, issue)


def _share_halves_wait(send, recv, totals, after, name):
    def await_all(refs, send_ref, recv_ref):
        for cp in _half_copies(refs, send_ref, recv_ref, True):
            cp.wait_send()
            cp.wait_recv()

    return _split_wait(name, totals, send, recv, after, await_all)


SMALL_ROWS = 56


def _small_copies(refs, send, recv, receiving):
    x, y, c = _position()
    me = 4 * x + 2 * y + c
    cps = []
    for k in range(1, N_DEV):
        px, py, pc = _xor_peer(x, y, c, k)
        cps.append(pltpu.make_async_remote_copy(
            src_ref=refs[0], dst_ref=refs[1].at[4 * px + 2 * py + pc if receiving else me],
            send_sem=send.at[k - 1], recv_sem=recv.at[k - 1], device_id=(px, py, pc), device_id_type=MESH))
    return cps


def _small_gather_start(packed, after):
    land = lax.empty((N_DEV,) + packed.shape, F32)

    def issue(refs, send, recv):
        for cp in _small_copies(refs, send, recv, False):
            cp.start()

    return _split_start("small_gather_start", [packed, land], N_DEV - 1, after, issue)


def _small_gather_wait(send, recv, arrays, after):
    def await_all(refs, send_ref, recv_ref):
        for cp in _small_copies(refs, send_ref, recv_ref, True):
            cp.wait_send()
            cp.wait_recv()

    return _split_wait("small_gather_wait", arrays, send, recv, after, await_all)


def _reduce_small(packed, land, silu_c):
    ns = 3 * D_MODEL // N_CHIPS

    def body(p_ref, land_ref, sc_ref, tot_ref, gw_ref, loss_ref, qk_ref, allp):
        x, y, c = _position()
        me = 4 * x + 2 * y + c
        chip = 2 * x + y
        for i in range(N_DEV):
            allp[i] = jnp.where(me == i, p_ref[...], land_ref[i])
        tot = allp[0]
        for i in range(1, N_DEV):
            tot = tot + allp[i]
        tot_ref[...] = tot
        loss_ref[...] = jnp.sum(tot[11:12, :], axis=1, keepdims=True) * (0.5 / D_MODEL)
        fold = tot[5:11, 0:HEAD_DIM]
        for h in range(1, N_HEADS):
            fold = fold + tot[5:11, h * HEAD_DIM:(h + 1) * HEAD_DIM]
        qk_ref[...] = jnp.concatenate([fold, jnp.zeros((2, HEAD_DIM), F32)], axis=0)
        pad = jnp.zeros((LANES - N_DEV, D_MODEL), F32)
        sct = jnp.concatenate([sc_ref[...], pad], axis=0).T.astype(BF16)
        for l in range(2):
            dms = [allp[i, pl.ds(12 + 4 * l + chip, 1), :] for i in range(N_DEV)]
            dm = jnp.concatenate(dms + [pad], axis=0)[:, :ns].astype(BF16)
            gw_ref[l] = jnp.dot(sct, dm, preferred_element_type=F32)

    vm = pl.BlockSpec(memory_space=pltpu.VMEM)
    return pl.pallas_call(
        body, name="reduce_small", in_specs=[vm, vm, vm], out_specs=[vm] * 4,
        out_shape=[jax.ShapeDtypeStruct((SMALL_ROWS, D_MODEL), F32), jax.ShapeDtypeStruct((2, D_MODEL, ns), F32),
                   jax.ShapeDtypeStruct((1, 1), F32), jax.ShapeDtypeStruct((8, HEAD_DIM), F32)],
        scratch_shapes=[pltpu.VMEM((N_DEV, SMALL_ROWS, D_MODEL), F32)],
        compiler_params=pltpu.CompilerParams(vmem_limit_bytes=VMEM_LIMIT_BYTES),
    )(packed, land, silu_c)


def kernel(x, c, norm_g, ada_w, ada_b, a_w_in, a_conv_w, a_conv_b, a_ln_g, a_ln_b, a_w_out, b_w_in, b_q_norm, b_k_norm, b_w_out, loss_target, m_norm_g, m_ada_w, m_ada_b, m_a_w_in, m_a_conv_w, m_a_conv_b, m_a_ln_g, m_a_ln_b, m_a_w_out, m_b_w_in, m_b_q_norm, m_b_k_norm, m_b_w_out, v_norm_g, v_ada_w, v_ada_b, v_a_w_in, v_a_conv_w, v_a_conv_b, v_a_ln_g, v_a_ln_b, v_a_w_out, v_b_w_in, v_b_q_norm, v_b_k_norm, v_b_w_out):
    chip = 2 * lax.axis_index("x") + lax.axis_index("y")
    core = lax.axis_index("c")
    chip_idx = chip.astype(jnp.int32).reshape(1)
    dev_idx = jnp.stack([2 * chip + core, chip, core]).astype(jnp.int32)

    land_a_in, own_wa_in = _cast_into_slot(a_w_in[0], chip_idx, "cast_a_w_in", keep_own=True)
    lands_a = [land_a_in, _cast_into_slot(a_w_out[0], chip_idx, "cast_a_w_out")]
    mods, silu_c, conv_w_full = _ada_forward(c, ada_w, ada_b, a_conv_w[0], after=tuple(lands_a))
    send_a, recv_a, lands_a, token_a = _gather_start(lands_a, mods, "gather_start_a")
    land_b_in, own_wb_in = _cast_into_slot(b_w_in[0], chip_idx, "cast_b_w_in", keep_own=True, after=token_a)
    lands_b = [land_b_in, _cast_into_slot(b_w_out[0], chip_idx, "cast_b_w_out", after=token_a)]
    send_b, recv_b, lands_b, token_b = _gather_start(lands_b, token_a, "gather_start_b")
    mods = mods + token_b[0:2, 0:1]

    def weights_a(after):
        send, recv, lands, _ = _gather_forward(send_a, recv_a, lands_a, after, "gather_forward_a")
        w_in, w_out = _gather_wait(send, recv, lands, after, "gather_wait_a")
        return w_in, w_out.reshape(D_MODEL, D_MODEL)

    forwarded_b = []

    def weights_b(after):
        send, recv, lands, _ = forwarded_b
        w_in, w_out = _gather_wait(send, recv, lands, after, "gather_wait_b")
        return w_in, w_out.reshape(D_MODEL, D_MODEL)

    def forward_weights_b(after):
        forwarded_b.extend(_gather_forward(send_b, recv_b, lands_b, after, "gather_forward_b"))

    stage1, stage2 = {}, {}

    def send_grads(tag, dw_in, dw_out):
        grads = [dw_in, dw_out.reshape(N_CHIPS, D_MODEL // N_CHIPS, D_MODEL)]
        send, recv, arrays, token = _reduce_sibling_start(grads, dw_out, f"reduce_d2d_start_{tag}")
        stage1[tag] = (send, recv, arrays)
        return token

    def forward_grads(tag, after):
        send, recv, arrays = stage1[tag]
        grads, got = _reduce_sibling_wait(send, recv, arrays, after, f"reduce_d2d_wait_{tag}")
        partials = [_add_sibling_half(grads[i], got[i], dev_idx, f"reduce_add_{tag}_{i}") for i in range(2)]
        send, recv, arrays, token = _reduce_chips_start(partials, partials[1], f"reduce_ici_start_{tag}")
        stage2[tag] = (send, recv, arrays)
        return token

    stage3 = {}

    def sum_grads(tag, after):
        send, recv, arrays = stage2[tag]
        partials, lands = _reduce_chips_wait(send, recv, arrays, after, f"reduce_ici_wait_{tag}")
        totals = [_sum_partials(lands[i], partials[i], dev_idx, f"reduce_sum_{tag}_{i}") for i in range(2)]
        send, recv, totals, token = _share_halves_start(totals, totals[1], f"reduce_share_start_{tag}")
        stage3[tag] = (send, recv, totals)
        return token

    def finish_grads(tag, after):
        send, recv, totals = stage3[tag]
        return _share_halves_wait(send, recv, totals, after, f"reduce_share_wait_{tag}")

    grad_x, small = _local_step(
        x[0], loss_target[0], mods.reshape(2, 3, D_MODEL), norm_g, conv_w_full, a_conv_b, a_ln_g[0:1],
        a_ln_b[0:1], b_q_norm[0], b_k_norm[0], chip.astype(jnp.int32), own_wa_in, own_wb_in,
        weights_a, weights_b, forward_weights_b,
        functools.partial(send_grads, "b"), functools.partial(forward_grads, "b"), functools.partial(send_grads, "a"))

    ns = 3 * D_MODEL // N_CHIPS
    pad_mod = lambda dm: jnp.pad(dm.reshape(N_CHIPS, ns), ((0, 0), (0, D_MODEL - ns)))
    packed = jnp.concatenate([
        small["dnorm_g"], small["dconv_b"], small["dln_g"], small["dln_b"], small["dq_norm"], small["dk_norm"],
        small["loss_cols"], pad_mod(small["dmod0"]), pad_mod(small["dmod1"]), small["dconv_w"],
        jnp.zeros((SMALL_ROWS - 20 - CONV_WIDTH, D_MODEL), F32)], axis=0)
    send_s, recv_s, small_arrays, token_s = _small_gather_start(packed, packed)

    given = dict(norm_g=(norm_g, m_norm_g, v_norm_g), ada_w=(ada_w, m_ada_w, v_ada_w), ada_b=(ada_b, m_ada_b, v_ada_b),
                 a_w_in=(a_w_in, m_a_w_in, v_a_w_in), a_conv_w=(a_conv_w, m_a_conv_w, v_a_conv_w),
                 a_conv_b=(a_conv_b, m_a_conv_b, v_a_conv_b), a_ln_g=(a_ln_g, m_a_ln_g, v_a_ln_g),
                 a_ln_b=(a_ln_b, m_a_ln_b, v_a_ln_b), a_w_out=(a_w_out, m_a_w_out, v_a_w_out),
                 b_w_in=(b_w_in, m_b_w_in, v_b_w_in), b_q_norm=(b_q_norm, m_b_q_norm, v_b_q_norm),
                 b_k_norm=(b_k_norm, m_b_k_norm, v_b_k_norm), b_w_out=(b_w_out, m_b_w_out, v_b_w_out))
    order = ["norm_g", "ada_w", "ada_b", "a_w_in", "a_conv_w", "a_conv_b", "a_ln_g", "a_ln_b", "a_w_out", "b_w_in",
             "b_q_norm", "b_k_norm", "b_w_out"]
    outs = {}

    def update(k, g2, after=None, copy_grad=False):
        w, m, v = given[k]
        shape2 = g2.shape
        res = _adamw(w.reshape(shape2), g2, m.reshape(shape2), v.reshape(shape2), f"adamw_{k}", after, copy_grad)
        outs[k] = tuple(a.reshape(w.shape) for a in ((res[3] if copy_grad else g2), res[0], res[1], res[2]))

    token = forward_grads("a", token_s)
    token = sum_grads("b", token)
    packed, land = _small_gather_wait(send_s, recv_s, small_arrays, token)
    tot, g_ada_w, loss, qk = _reduce_small(packed, land, silu_c)
    g_b_in, g_b_out = finish_grads("b", tot)
    update("b_w_in", g_b_in, copy_grad=True)
    update("b_w_out", g_b_out, copy_grad=True)
    token = sum_grads("a", outs["b_w_in"][1])
    cw = D_MODEL // N_CHIPS
    g_small = dict(
        norm_g=tot[0:2], a_conv_b=tot[2:3], a_ln_g=tot[3:4], a_ln_b=tot[4:5],
        b_q_norm=qk[0:3], b_k_norm=qk[3:6],
        ada_b=jnp.stack([tot[12:16, :ns].reshape(3 * D_MODEL), tot[16:20, :ns].reshape(3 * D_MODEL)]),
        a_conv_w=lax.dynamic_slice(tot[20:20 + CONV_WIDTH], (0, chip * cw), (CONV_WIDTH, cw)),
    )
    update("ada_w", g_ada_w.reshape(2 * D_MODEL, ns), after=token)
    for k, g2 in g_small.items():
        update(k, g2, after=token)
    g_a_in, g_a_out = finish_grads("a", outs["ada_w"][1])
    update("a_w_in", g_a_in, copy_grad=True)
    update("a_w_out", g_a_out, copy_grad=True)
    return (loss.reshape(()), grad_x[None], *[outs[k][0] for k in order], *[outs[k][1] for k in order],
            *[outs[k][2] for k in order], *[outs[k][3] for k in order])
```

```python
import functools

import jax
import jax.numpy as jnp
from jax import lax
from jax.experimental import pallas as pl
from jax.experimental.pallas import tpu as pltpu

F32 = jnp.float32
BF16 = jnp.bfloat16

SEQ = 2048
D_MODEL = 1024
CONV_WIDTH = 31
HEAD_DIM = 64
N_HEADS = 16
DILATIONS = (1, 4, 16)
ATTN_BLOCK = 128
NORM_EPS = 1e-6
NEG_INF = -1e30
N_DEV = 8
N_CHIPS = 4

ADAM_LR = 0.001
ADAM_B1 = 0.9
ADAM_B2 = 0.999
ADAM_EPS = 1e-08
ADAM_WD = 0.01
ADAM_STEP = 10

VMEM_LIMIT_BYTES = 52 * 1024 * 1024
HALO = 32
LANES = 128
ROW_TILE = 512
MESH = pl.DeviceIdType.MESH


def _params(*sem):
    return pltpu.CompilerParams(dimension_semantics=sem or None, vmem_limit_bytes=VMEM_LIMIT_BYTES)


def _sigmoid(v):
    return 1.0 / (1.0 + jnp.exp(-v))


def _row_spec(tm, cols, col_block=0):
    return pl.BlockSpec((tm, cols), lambda i: (i, col_block))


def _vec_spec(rows, cols):
    return pl.BlockSpec((rows, cols), lambda i: (0, 0))


def _normmod(xv, g, scale, shift):
    r = lax.rsqrt(jnp.mean(xv * xv, axis=-1, keepdims=True) + NORM_EPS)
    return xv * r * g * (1.0 + scale) + shift


def _normmod_fwd(x, g, scale, shift, name):
    tm = ROW_TILE

    def body(x_ref, g_ref, sc_ref, sh_ref, h_ref, ht_ref):
        h = _normmod(x_ref[...], g_ref[...], sc_ref[...], sh_ref[...])
        h_ref[...] = h.astype(BF16)
        ht_ref[...] = h.T.astype(BF16)

    return pl.pallas_call(
        body, name=name, grid=(SEQ // tm,),
        in_specs=[_row_spec(tm, D_MODEL)] + [_vec_spec(1, D_MODEL)] * 3,
        out_specs=[_row_spec(tm, D_MODEL), pl.BlockSpec((D_MODEL, tm), lambda i: (0, i))],
        out_shape=[jax.ShapeDtypeStruct((SEQ, D_MODEL), BF16), jax.ShapeDtypeStruct((D_MODEL, SEQ), BF16)],
        compiler_params=_params("parallel"),
    )(x, g, scale, shift)


def _normmod_bwd(x, g, scale, dh_parts, dres, name, part_dilations=None, gated=None):
    tm = ROW_TILE
    n_parts = len(dh_parts)
    dils = part_dilations or (1,) * n_parts
    dh_parts = [p if d == 1 else p.reshape(d, SEQ // d, D_MODEL) for p, d in zip(dh_parts, dils)]
    n_gated = 0 if gated is None else 2

    def body(x_ref, g_ref, sc_ref, dres_ref, *rest):
        part_refs = rest[:n_parts]
        gated_refs = rest[n_parts:n_parts + n_gated]
        out_refs = rest[n_parts + n_gated:]
        dx_ref, sums_ref, nat = out_refs[0], out_refs[1], out_refs[-1]
        xv = x_ref[...]
        r = lax.rsqrt(jnp.mean(xv * xv, axis=-1, keepdims=True) + NORM_EPS)
        xn = xv * r
        dh = _load_natural(part_refs[0], nat, dils[0])
        for p, d in zip(part_refs[1:], dils[1:]):
            dh = dh + _load_natural(p, nat, d)
        gv = g_ref[...]
        one_sc = 1.0 + sc_ref[...]
        dxn = dh * (gv * one_sc)
        dx = dres_ref[...] + r * (dxn - xn * jnp.mean(dxn * xn, axis=-1, keepdims=True))
        dx_ref[...] = dx
        dhx = dh * xn
        rows = [jnp.sum(dhx, axis=0, keepdims=True) * one_sc,
                jnp.sum(dhx, axis=0, keepdims=True) * gv,
                jnp.sum(dh, axis=0, keepdims=True)]
        if gated is not None:
            gate_ref, y_ref = gated_refs
            out_refs[2][...] = (dx * gate_ref[...]).astype(BF16)
            rows.append(jnp.sum(dx * y_ref[...].astype(F32), axis=0, keepdims=True))
        sums = jnp.concatenate(rows + [jnp.zeros((8 - len(rows), D_MODEL), F32)], axis=0)

        @pl.when(pl.program_id(0) == 0)
        def _():
            sums_ref[...] = jnp.zeros_like(sums_ref)

        sums_ref[...] += sums

    gated_specs = [] if gated is None else [_vec_spec(1, D_MODEL), _row_spec(tm, D_MODEL)]
    dy_spec = [] if gated is None else [_row_spec(tm, D_MODEL)]
    dy_shape = [] if gated is None else [jax.ShapeDtypeStruct((SEQ, D_MODEL), BF16)]
    return pl.pallas_call(
        body, name=name, grid=(SEQ // tm,),
        in_specs=[_row_spec(tm, D_MODEL), _vec_spec(1, D_MODEL), _vec_spec(1, D_MODEL), _row_spec(tm, D_MODEL)]
        + [_class_spec(tm, d) for d in dils] + gated_specs,
        out_specs=[_row_spec(tm, D_MODEL), _vec_spec(8, D_MODEL)] + dy_spec,
        out_shape=[jax.ShapeDtypeStruct((SEQ, D_MODEL), F32), jax.ShapeDtypeStruct((8, D_MODEL), F32)] + dy_shape,
        scratch_shapes=[_natural_scratch(tm)],
        compiler_params=_params("arbitrary"),
    )(x, g, scale, dres, *dh_parts, *(gated or ()))


def _mm(lhs, rhs, *, tn, tile0, n_tiles, out_dtype, name, out3d=None, prev=None, transpose_lhs=False):
    mo, kc = lhs.shape[::-1] if transpose_lhs else lhs.shape
    cm = min(mo, 1024)
    tc = 256

    def body(l_ref, r_ref, *rest):
        if transpose_lhs:
            o_ref, lt_ref = rest[-2], rest[-1]

            @pl.when(pl.program_id(0) == 0)
            def _():
                for c in range(kc // tc):
                    lt_ref[:, c * tc:(c + 1) * tc] = l_ref[c * tc:(c + 1) * tc, :].astype(F32).T.astype(l_ref.dtype)
        else:
            o_ref, lt_ref = rest[-1], l_ref
        for m in range(mo // cm):
            rows = pl.ds(m * cm, cm)
            o_ref[rows, :] = jnp.dot(lt_ref[rows, :], r_ref[...], preferred_element_type=F32).astype(out_dtype)

    if rhs.ndim == 3:
        tps_r = rhs.shape[2] // tn
        r_spec = pl.BlockSpec((None, kc, tn), lambda t: ((tile0 + t) // tps_r, 0, (tile0 + t) % tps_r))
    else:
        r_spec = pl.BlockSpec((kc, tn), lambda t: (0, t))
    in_specs = [pl.BlockSpec(lhs.shape, lambda t: (0, 0)), r_spec]
    args = [lhs, rhs]
    aliases = {}
    if out3d is None:
        o_spec = pl.BlockSpec((mo, tn), lambda t: (0, t))
        o_shape = jax.ShapeDtypeStruct((mo, n_tiles * tn), out_dtype)
    else:
        j_out, ns_out = out3d
        tps_o = ns_out // tn
        o_spec = pl.BlockSpec((None, mo, tn), lambda t: ((tile0 + t) // tps_o, 0, (tile0 + t) % tps_o))
        o_shape = jax.ShapeDtypeStruct((j_out, mo, ns_out), out_dtype)
        if prev is not None:
            in_specs.append(pl.BlockSpec(memory_space=pl.ANY))
            args.append(prev)
            aliases = {2: 0}
    return pl.pallas_call(
        body, name=name, grid=(n_tiles,), in_specs=in_specs, out_specs=o_spec, out_shape=o_shape,
        input_output_aliases=aliases,
        scratch_shapes=[pltpu.VMEM((mo, kc), lhs.dtype)] if transpose_lhs else [],
        compiler_params=_params("arbitrary" if transpose_lhs else "parallel"),
    )(*args)


def _in_tiles(h_parts, w3, tile_ids, n_tiles, *, tn, total_tiles, part_of, name, prev=None):
    _, kc, ns = w3.shape
    tps = ns // tn
    cm = 1024
    n_parts = len(h_parts)

    def body(ids_ref, *rest):
        h_refs, w_ref, o_ref = rest[:n_parts], rest[n_parts], rest[-1]
        part = part_of(ids_ref[1, pl.program_id(0)])
        for g, h_ref in enumerate(h_refs):
            @pl.when(part == g)
            def _():
                for m in range(SEQ // cm):
                    rows = pl.ds(m * cm, cm)
                    o_ref[rows, :] = jnp.dot(h_ref[rows, :], w_ref[...], preferred_element_type=F32).astype(BF16)

    resident = pl.BlockSpec((SEQ, kc), lambda t, ids: (0, 0))
    in_specs = [resident] * n_parts + [
        pl.BlockSpec((None, kc, tn), lambda t, ids: (ids[0, t] // tps, 0, ids[0, t] % tps))]
    args = [*h_parts, w3]
    aliases = {}
    if prev is not None:
        in_specs.append(pl.BlockSpec(memory_space=pl.ANY))
        args.append(prev)
        aliases = {n_parts + 2: 0}
    return pl.pallas_call(
        body, name=name,
        grid_spec=pltpu.PrefetchScalarGridSpec(
            num_scalar_prefetch=1, grid=(n_tiles,), in_specs=in_specs,
            out_specs=pl.BlockSpec((SEQ, tn), lambda t, ids: (0, ids[1, t]))),
        out_shape=jax.ShapeDtypeStruct((SEQ, total_tiles * tn), BF16),
        input_output_aliases=aliases, compiler_params=_params("arbitrary"),
    )(tile_ids, *args)


def _own_first(chip, total_tiles):
    own = total_tiles // N_CHIPS
    step = jnp.arange(total_tiles, dtype=jnp.int32)
    tiles = (own * chip + step) % total_tiles
    return jnp.stack([step[:own], tiles[:own]]), jnp.stack([tiles[own:], tiles[own:]]), own


def _mm_nt(dy, w3, *, tn, tile0, n_tiles, name, after=None):
    m_rows = dy.shape[0]
    _, kc, ns = w3.shape
    tps = ns // tn
    cm = 512
    extra = [] if after is None else [after]

    def body(dy_ref, w_ref, *rest):
        o_ref, acc = rest[-2], rest[-1]
        t = pl.program_id(0)

        @pl.when(t == 0)
        def _():
            acc[...] = jnp.zeros_like(acc)

        for m in range(m_rows // cm):
            rows = pl.ds(m * cm, cm)
            acc[rows, :] += lax.dot_general(dy_ref[rows, :], w_ref[...], NT_DIMS, preferred_element_type=F32)

        @pl.when(t == n_tiles - 1)
        def _():
            o_ref[...] = acc[...].astype(BF16)

    return pl.pallas_call(
        body, name=name, grid=(n_tiles,),
        in_specs=[pl.BlockSpec((m_rows, tn), lambda t: (0, t)),
                  pl.BlockSpec((None, kc, tn), lambda t: ((tile0 + t) // tps, 0, (tile0 + t) % tps))]
        + [pl.BlockSpec(memory_space=pl.ANY)] * len(extra),
        out_specs=pl.BlockSpec((m_rows, kc), lambda t: (0, 0)),
        out_shape=jax.ShapeDtypeStruct((m_rows, kc), BF16),
        scratch_shapes=[pltpu.VMEM((m_rows, kc), F32)],
        compiler_params=_params("arbitrary"),
    )(dy, w3, *extra)


CONV_CHUNK = 16


def _shift_copies(buf, shifted):
    rows = shifted.shape[1]
    for s in range(1, 8):
        shifted[s - 1] = buf[pl.ds(s, rows), :]


def _shifted_rows(buf, shifted, offset, r0):
    s = offset % 8
    if s == 0:
        return buf[pl.ds(r0 + offset, CONV_CHUNK), :]
    return shifted[s - 1, pl.ds(r0 + (offset - s), CONV_CHUNK), :]


def _spread_taps(w_ref, taps):
    for k in range(CONV_WIDTH):
        taps[k] = jnp.broadcast_to(w_ref[k:k + 1, :], (8, D_MODEL))


def _times_tap(taps, k, rows):
    return (rows.reshape(CONV_CHUNK // 8, 8, D_MODEL) * taps[k][None]).reshape(CONV_CHUNK, D_MODEL)


def _conv_fwd(proj, conv_w, conv_b, ln_g, ln_b, name):
    tm = ROW_TILE
    hb = tm // HALO

    def body(vg_ref, halo_ref, z_ref, w_ref, b_ref, g_ref, be_ref, u5_ref, u5t_ref, u2_ref, buf, shifted, taps):
        i = pl.program_id(0)
        u1 = vg_ref[:, :D_MODEL].astype(F32) * _sigmoid(vg_ref[:, D_MODEL:].astype(F32))
        u1h = halo_ref[:, :D_MODEL].astype(F32) * _sigmoid(halo_ref[:, D_MODEL:].astype(F32))
        buf[pl.ds(0, HALO), :] = jnp.where(i > 0, u1h, 0.0)
        buf[pl.ds(HALO, tm), :] = u1
        _shift_copies(buf, shifted)
        _spread_taps(w_ref, taps)

        def chunk(ci, carry):
            r0 = pl.multiple_of(ci * CONV_CHUNK, CONV_CHUNK)
            acc = jnp.broadcast_to(b_ref[...], (CONV_CHUNK, D_MODEL))
            for k in range(CONV_WIDTH):
                acc = acc + _times_tap(taps, k, _shifted_rows(buf, shifted, HALO - (CONV_WIDTH - 1) + k, r0))
            u2_ref[pl.ds(r0, CONV_CHUNK), :] = acc
            return carry

        lax.fori_loop(0, tm // CONV_CHUNK, chunk, 0)
        acc = u2_ref[...]
        mu = jnp.mean(acc, axis=-1, keepdims=True)
        xc = acc - mu
        rstd = lax.rsqrt(jnp.mean(xc * xc, axis=-1, keepdims=True) + NORM_EPS)
        u3 = xc * rstd * g_ref[...] + be_ref[...]
        zv = z_ref[...].astype(F32)
        u5 = u3 * _sigmoid(u3) * (zv * _sigmoid(zv))
        u5_ref[...] = u5.astype(BF16)
        u5t_ref[...] = u5.T.astype(BF16)

    return pl.pallas_call(
        body, name=name, grid=(SEQ // tm,),
        in_specs=[pl.BlockSpec((tm, 2 * D_MODEL), lambda i: (i, 0)),
                  pl.BlockSpec((HALO, 2 * D_MODEL), lambda i: (jnp.maximum(i * hb - 1, 0), 0)),
                  _row_spec(tm, D_MODEL, 2),
                  _vec_spec(CONV_WIDTH, D_MODEL)] + [_vec_spec(1, D_MODEL)] * 3,
        out_specs=[_row_spec(tm, D_MODEL), pl.BlockSpec((D_MODEL, tm), lambda i: (0, i)), _row_spec(tm, D_MODEL)],
        out_shape=[jax.ShapeDtypeStruct((SEQ, D_MODEL), BF16), jax.ShapeDtypeStruct((D_MODEL, SEQ), BF16),
                   jax.ShapeDtypeStruct((SEQ, D_MODEL), F32)],
        scratch_shapes=[pltpu.VMEM((HALO + tm, D_MODEL), F32), pltpu.VMEM((7, HALO + tm - 8, D_MODEL), F32),
                        pltpu.VMEM((CONV_WIDTH, 8, D_MODEL), F32)],
        compiler_params=_params("parallel"),
    )(proj, proj, proj, conv_w, conv_b, ln_g, ln_b)


def _conv_bwd_pointwise(dy, w_out, proj, u2, ln_g, ln_b, name):
    tm = ROW_TILE

    def body(dy_ref, w_ref, z_ref, u2_ref, g_ref, be_ref, du2_ref, dz_ref, sums_ref):
        u2v = u2_ref[...]
        mu = jnp.mean(u2v, axis=-1, keepdims=True)
        xc = u2v - mu
        rstd = lax.rsqrt(jnp.mean(xc * xc, axis=-1, keepdims=True) + NORM_EPS)
        xhat = xc * rstd
        u3 = xhat * g_ref[...] + be_ref[...]
        s3 = _sigmoid(u3)
        u4 = u3 * s3
        zv = z_ref[...].astype(F32)
        sz = _sigmoid(zv)
        du5v = lax.dot_general(dy_ref[...], w_ref[...], NT_DIMS, preferred_element_type=F32)
        dz_ref[...] = du5v * u4 * (sz * (1.0 + zv * (1.0 - sz)))
        du3 = du5v * (zv * sz) * (s3 * (1.0 + u3 * (1.0 - s3)))
        dxhat = du3 * g_ref[...]
        du2 = rstd * (dxhat - jnp.mean(dxhat, axis=-1, keepdims=True)
                      - xhat * jnp.mean(dxhat * xhat, axis=-1, keepdims=True))
        du2_ref[...] = du2
        sums = jnp.concatenate([
            jnp.sum(du3 * xhat, axis=0, keepdims=True),
            jnp.sum(du3, axis=0, keepdims=True),
            jnp.sum(du2, axis=0, keepdims=True),
            jnp.zeros((5, D_MODEL), F32)], axis=0)

        @pl.when(pl.program_id(0) == 0)
        def _():
            sums_ref[...] = jnp.zeros_like(sums_ref)

        sums_ref[...] += sums

    return pl.pallas_call(
        body, name=name, grid=(SEQ // tm,),
        in_specs=[_row_spec(tm, D_MODEL), _vec_spec(D_MODEL, D_MODEL), _row_spec(tm, D_MODEL, 2),
                  _row_spec(tm, D_MODEL), _vec_spec(1, D_MODEL), _vec_spec(1, D_MODEL)],
        out_specs=[_row_spec(tm, D_MODEL), _row_spec(tm, D_MODEL), _vec_spec(8, D_MODEL)],
        out_shape=[jax.ShapeDtypeStruct((SEQ, D_MODEL), F32), jax.ShapeDtypeStruct((SEQ, D_MODEL), F32),
                   jax.ShapeDtypeStruct((8, D_MODEL), F32)],
        compiler_params=_params("arbitrary"),
    )(dy, w_out, proj, u2, ln_g, ln_b)


def _conv_bwd_taps(du2, dz, proj, conv_w, name):
    tm = ROW_TILE
    hb = tm // HALO
    n_blocks = SEQ // tm

    def body(du2_ref, dnext_ref, dz_ref, vg_ref, w_ref, dproj_ref, dw_ref, dbuf, dshift, sgbuf, ubuf, dwacc, taps):
        i = pl.program_id(0)
        _spread_taps(w_ref, taps)
        sg = _sigmoid(vg_ref[:, D_MODEL:].astype(F32))
        sgbuf[...] = sg
        ubuf[...] = vg_ref[:, :D_MODEL].astype(F32) * sg
        dbuf[pl.ds(0, tm), :] = du2_ref[...]
        dbuf[pl.ds(tm, HALO), :] = jnp.where(i < n_blocks - 1, dnext_ref[...], 0.0)
        _shift_copies(dbuf, dshift)

        @pl.when(i == 0)
        def _():
            dwacc[...] = jnp.zeros_like(dwacc)

        def chunk(ci, carry):
            r0 = pl.multiple_of(ci * CONV_CHUNK, CONV_CHUNK)
            rows = pl.ds(r0, CONV_CHUNK)
            u1c = ubuf[rows, :]
            du1 = jnp.zeros((CONV_CHUNK, D_MODEL), F32)
            for k in range(CONV_WIDTH):
                ahead = _shifted_rows(dbuf, dshift, CONV_WIDTH - 1 - k, r0)
                du1 = du1 + _times_tap(taps, k, ahead)
                prod = u1c * ahead
                dwacc[k] += prod[0:8] + prod[8:16]
            sgc = sgbuf[rows, :]
            dval = du1 * sgc
            dproj_ref[rows, 0:D_MODEL] = dval.astype(BF16)
            dproj_ref[rows, D_MODEL:2 * D_MODEL] = (
                dval * vg_ref[rows, 0:D_MODEL].astype(F32) * (1.0 - sgc)).astype(BF16)
            return carry

        lax.fori_loop(0, tm // CONV_CHUNK, chunk, 0)
        dproj_ref[:, 2 * D_MODEL:] = dz_ref[...].astype(BF16)

        @pl.when(i == n_blocks - 1)
        def _():
            for k in range(CONV_WIDTH):
                dw_ref[k:k + 1, :] = jnp.sum(dwacc[k], axis=0, keepdims=True)
            dw_ref[CONV_WIDTH:, :] = jnp.zeros((32 - CONV_WIDTH, D_MODEL), F32)

    return pl.pallas_call(
        body, name=name, grid=(n_blocks,),
        in_specs=[_row_spec(tm, D_MODEL),
                  pl.BlockSpec((HALO, D_MODEL), lambda i: (jnp.minimum((i + 1) * hb, SEQ // HALO - 1), 0)),
                  _row_spec(tm, D_MODEL),
                  pl.BlockSpec((tm, 2 * D_MODEL), lambda i: (i, 0)),
                  _vec_spec(CONV_WIDTH, D_MODEL)],
        out_specs=[_row_spec(tm, 3 * D_MODEL), _vec_spec(32, D_MODEL)],
        out_shape=[jax.ShapeDtypeStruct((SEQ, 3 * D_MODEL), BF16), jax.ShapeDtypeStruct((32, D_MODEL), F32)],
        scratch_shapes=[pltpu.VMEM((tm + HALO, D_MODEL), F32), pltpu.VMEM((7, HALO + tm - 8, D_MODEL), F32),
                        pltpu.VMEM((tm, D_MODEL), F32), pltpu.VMEM((tm, D_MODEL), F32),
                        pltpu.VMEM((CONV_WIDTH, 8, D_MODEL), F32), pltpu.VMEM((CONV_WIDTH, 8, D_MODEL), F32)],
        compiler_params=_params("arbitrary"),
    )(du2, du2, dz, proj, conv_w)


def _out_a(u5, w_out, x, gate, g1, scale1, shift1, name):
    tm = ROW_TILE
    n_d = len(DILATIONS)

    def body(u_ref, w_ref, x_ref, gate_ref, g_ref, sc_ref, sh_ref, x1_ref, y_ref, ht_ref, *rest):
        h_refs, nat = rest[:n_d], rest[-1]
        y = jnp.dot(u_ref[...], w_ref[...], preferred_element_type=F32)
        x1 = x_ref[...] + gate_ref[...] * y
        y_ref[...] = y.astype(BF16)
        x1_ref[...] = x1
        h = _normmod(x1, g_ref[...], sc_ref[...], sh_ref[...])
        ht_ref[...] = h.T.astype(BF16)
        for h_ref, d in zip(h_refs, DILATIONS):
            _store_classes(h_ref, h, nat, d)

    res = pl.pallas_call(
        body, name=name, grid=(SEQ // tm,),
        in_specs=[_row_spec(tm, D_MODEL), _vec_spec(D_MODEL, D_MODEL), _row_spec(tm, D_MODEL)]
        + [_vec_spec(1, D_MODEL)] * 4,
        out_specs=[_row_spec(tm, D_MODEL), _row_spec(tm, D_MODEL), pl.BlockSpec((D_MODEL, tm), lambda i: (0, i))]
        + [_class_spec(tm, d) for d in DILATIONS],
        out_shape=[jax.ShapeDtypeStruct((SEQ, D_MODEL), F32), jax.ShapeDtypeStruct((SEQ, D_MODEL), BF16),
                   jax.ShapeDtypeStruct((D_MODEL, SEQ), BF16)] + [_class_shape(d, BF16) for d in DILATIONS],
        scratch_shapes=[_natural_scratch(tm)],
        compiler_params=_params("parallel"),
    )(u5, w_out, x, gate, g1, scale1, shift1)
    return res[0], res[1], res[2], [a.reshape(SEQ, D_MODEL) for a in res[3:]]


def _out_b_loss(u, w_out, x1, gate, target, name):
    tm = ROW_TILE

    def body(u_ref, w_ref, x_ref, gate_ref, t_ref, e_ref, dy_ref, sums_ref):
        y = jnp.dot(u_ref[...], w_ref[...], preferred_element_type=F32)
        diff = x_ref[...] + gate_ref[...] * y - t_ref[...]
        e = diff * (1.0 / D_MODEL)
        e_ref[...] = e
        dy_ref[...] = (e * gate_ref[...]).astype(BF16)
        sums = jnp.concatenate([
            jnp.sum(e * y, axis=0, keepdims=True),
            jnp.sum(diff * diff, axis=0, keepdims=True),
            jnp.zeros((6, D_MODEL), F32)], axis=0)

        @pl.when(pl.program_id(0) == 0)
        def _():
            sums_ref[...] = jnp.zeros_like(sums_ref)

        sums_ref[...] += sums

    return pl.pallas_call(
        body, name=name, grid=(SEQ // tm,),
        in_specs=[_row_spec(tm, D_MODEL), _vec_spec(D_MODEL, D_MODEL), _row_spec(tm, D_MODEL),
                  _vec_spec(1, D_MODEL), _row_spec(tm, D_MODEL)],
        out_specs=[_row_spec(tm, D_MODEL), _row_spec(tm, D_MODEL), _vec_spec(8, D_MODEL)],
        out_shape=[jax.ShapeDtypeStruct((SEQ, D_MODEL), F32), jax.ShapeDtypeStruct((SEQ, D_MODEL), BF16),
                   jax.ShapeDtypeStruct((8, D_MODEL), F32)],
        compiler_params=_params("arbitrary"),
    )(u, w_out, x1, gate, target)


def _seg_matrix():
    r = lax.broadcasted_iota(jnp.int32, (256, 256), 0) // HEAD_DIM
    c = lax.broadcasted_iota(jnp.int32, (256, 256), 1) // HEAD_DIM
    return jnp.where(r == c, 1.0 / HEAD_DIM, 0.0).astype(BF16)


def _segmean(v, seg):
    hi = v.astype(BF16)
    lo = (v - hi.astype(F32)).astype(BF16)
    outs = []
    for c0 in range(0, D_MODEL, 256):
        outs.append(jnp.dot(hi[:, c0:c0 + 256], seg, preferred_element_type=F32)
                    + jnp.dot(lo[:, c0:c0 + 256], seg, preferred_element_type=F32))
    return jnp.concatenate(outs, axis=1)


def _qk_rstd(v, seg):
    return lax.rsqrt(_segmean(v * v, seg) + NORM_EPS)


def _qknorm_fwd(proj, group, qw, kw, seg, name):
    tm = ROW_TILE

    def body(q_in, k_in, qw_ref, kw_ref, seg_ref, q_ref, k_ref):
        segv = seg_ref[...]
        q = q_in[...].astype(F32)
        k = k_in[...].astype(F32)
        q_ref[...] = (q * _qk_rstd(q, segv) * qw_ref[...] * HEAD_DIM ** -0.5).astype(BF16)
        k_ref[...] = (k * _qk_rstd(k, segv) * kw_ref[...]).astype(BF16)

    return pl.pallas_call(
        body, name=name, grid=(SEQ // tm,),
        in_specs=[_row_spec(tm, D_MODEL, 3 * group), _row_spec(tm, D_MODEL, 3 * group + 1),
                  _vec_spec(1, D_MODEL), _vec_spec(1, D_MODEL), _vec_spec(256, 256)],
        out_specs=[_row_spec(tm, D_MODEL)] * 2,
        out_shape=[jax.ShapeDtypeStruct((SEQ, D_MODEL), BF16)] * 2,
        compiler_params=_params("parallel"),
    )(proj, proj, qw, kw, seg)


def _attn_masks(b, bpc, dilation, transposed=False):
    keys = ATTN_BLOCK if bpc == 1 else 2 * ATTN_BLOCK
    shape, q_axis = ((keys, ATTN_BLOCK), 1) if transposed else ((ATTN_BLOCK, keys), 0)
    qi = lax.broadcasted_iota(jnp.int32, shape, q_axis)
    kj = lax.broadcasted_iota(jnp.int32, shape, 1 - q_axis)
    if bpc == 1:
        steps = qi - kj
        return (steps * dilation).astype(F32), steps >= 0
    steps = qi + ATTN_BLOCK - kj
    has_prev = (b % bpc) != 0
    valid = (steps >= 0) & (steps <= ATTN_BLOCK) & (has_prev | (kj >= ATTN_BLOCK))
    return (steps * dilation).astype(F32), valid


MASKED = 1e30


def _bias_scratch(bpc):
    return pltpu.VMEM((1 if bpc == 1 else 2, N_HEADS, ATTN_BLOCK, (1 if bpc == 1 else 2) * ATTN_BLOCK), F32)


def _fill_bias(bias_ref, sl_ref, bpc, dilation):
    for variant in range(bias_ref.shape[0]):
        dist, valid = _attn_masks(variant, min(bpc, 2), dilation)
        bias_ref[variant] = jnp.where(valid[None], dist[None] * sl_ref[...], MASKED)


def _step_bias(bias_ref, b, bpc):
    if bpc == 1:
        return bias_ref[0]
    return bias_ref[jnp.where((b % bpc) != 0, 1, 0)]


def _key_tile(prev_ref, cur_ref, cols, bpc):
    if bpc == 1:
        return cur_ref[:, cols]
    return jnp.concatenate([prev_ref[:, cols], cur_ref[:, cols]], axis=0)


ATTN_HEADS_FWD = 16
ATTN_HEADS_BWD = 16
NT_DIMS = (((1,), (1,)), ((), ()))
BATCH_NT_DIMS = (((2,), (2,)), ((0,), (0,)))
BATCH_NN_DIMS = (((2,), (1,)), ((0,), (0,)))
BATCH_TN_DIMS = (((1,), (1,)), ((0,), (0,)))


def _head_stack(tile_of, heads):
    return jnp.stack([tile_of(slice(h * HEAD_DIM, (h + 1) * HEAD_DIM)) for h in range(heads)], axis=0)


def _attn_specs(heads, segment=0):
    width = heads * HEAD_DIM
    off = segment * (D_MODEL // width)
    last = SEQ // ATTN_BLOCK - 1
    cur = pl.BlockSpec((ATTN_BLOCK, width), lambda hg, b: (jnp.minimum(b, last), hg + off))
    prev = pl.BlockSpec((ATTN_BLOCK, width), lambda hg, b: (jnp.clip(b - 1, 0, last), hg + off))
    return cur, prev


def _attn_fwd(q, k, proj, group, slopes, dilation, name):
    bpc = SEQ // dilation // ATTN_BLOCK
    heads = ATTN_HEADS_FWD
    assert heads == N_HEADS
    cur, prev = _attn_specs(heads)
    v_cur, v_prev = _attn_specs(heads, segment=3 * group + 2)

    def body(sl_ref, q_ref, kp_ref, kc_ref, vp_ref, vc_ref, o_ref, lse_ref, bias_ref):
        b = pl.program_id(1)

        @pl.when(b == 0)
        def _():
            _fill_bias(bias_ref, sl_ref, bpc, dilation)

        q3 = _head_stack(lambda cols: q_ref[:, cols], heads)
        k3 = _head_stack(lambda cols: _key_tile(kp_ref, kc_ref, cols, bpc), heads)
        v3 = _head_stack(lambda cols: _key_tile(vp_ref, vc_ref, cols, bpc), heads)
        s = lax.dot_general(q3, k3, BATCH_NT_DIMS, preferred_element_type=F32)
        s = s - _step_bias(bias_ref, b, bpc)
        m = jnp.max(s, axis=-1, keepdims=True)
        p = jnp.exp(s - m)
        l = jnp.sum(p, axis=-1, keepdims=True)
        o3 = lax.dot_general(p.astype(BF16), v3, BATCH_NN_DIMS, preferred_element_type=F32) / l
        lse3 = m + jnp.log(l)
        for h in range(heads):
            o_ref[:, h * HEAD_DIM:(h + 1) * HEAD_DIM] = o3[h].astype(BF16)
        lse_ref[...] = jnp.concatenate([lse3[h] for h in range(heads)]
                                       + [jnp.zeros((ATTN_BLOCK, LANES - heads), F32)], axis=1)

    return pl.pallas_call(
        body, name=name, grid=(N_HEADS // heads, SEQ // ATTN_BLOCK),
        in_specs=[pl.BlockSpec((heads, 1, 1), lambda hg, b: (hg, 0, 0)), cur, prev, cur, v_prev, v_cur],
        out_specs=[cur, pl.BlockSpec((ATTN_BLOCK, LANES), lambda hg, b: (b, 0))],
        out_shape=[jax.ShapeDtypeStruct((SEQ, D_MODEL), BF16), jax.ShapeDtypeStruct((SEQ, LANES), F32)],
        scratch_shapes=[_bias_scratch(bpc)],
        compiler_params=_params("parallel", "arbitrary"),
    )(slopes.reshape(N_HEADS, 1, 1), q, k, k, proj, proj)


def _class_spec(tm, dilation, width=D_MODEL):
    if dilation == 1:
        return _row_spec(tm, width)
    return pl.BlockSpec((dilation, tm // dilation, width), lambda i: (0, i, 0))


def _class_shape(dilation, dtype, width=D_MODEL):
    if dilation == 1:
        return jax.ShapeDtypeStruct((SEQ, width), dtype)
    return jax.ShapeDtypeStruct((dilation, SEQ // dilation, width), dtype)


def _load_natural(in_ref, nat_ref, dilation):
    if dilation == 1:
        return in_ref[...].astype(F32)
    n = nat_ref.shape[1] // dilation
    tiles = in_ref.shape[-1] // LANES
    for r in range(dilation):
        for j in range(tiles):
            nat_ref.at[j][pl.ds(r, n, stride=dilation), :] = in_ref[r, :, j * LANES:(j + 1) * LANES].astype(F32)
    if tiles == 1:
        return nat_ref[0]
    return jnp.concatenate([nat_ref[j] for j in range(tiles)], axis=1)


def _store_classes(out_ref, value, nat_ref, dilation):
    if dilation == 1:
        out_ref[...] = value.astype(out_ref.dtype)
        return
    n = nat_ref.shape[1] // dilation
    tiles = value.shape[-1] // LANES
    for j in range(tiles):
        nat_ref[j] = value[:, j * LANES:(j + 1) * LANES]
    for r in range(dilation):
        for j in range(tiles):
            out_ref[r, :, j * LANES:(j + 1) * LANES] = (
                nat_ref.at[j][pl.ds(r, n, stride=dilation), :].astype(out_ref.dtype))


def _natural_scratch(tm):
    return pltpu.VMEM((D_MODEL // LANES, tm, LANES), F32)


def _head_selector():
    lane_head = lax.broadcasted_iota(jnp.int32, (D_MODEL, LANES), 0) // HEAD_DIM
    head = lax.broadcasted_iota(jnp.int32, (D_MODEL, LANES), 1)
    return (lane_head == head).astype(BF16)


def _dot_split(v, m01, dims):
    hi = v.astype(BF16)
    lo = (v - hi.astype(F32)).astype(BF16)
    return (lax.dot_general(hi, m01, dims, preferred_element_type=F32)
            + lax.dot_general(lo, m01, dims, preferred_element_type=F32))


def _merge_fwd(o_parts, lse_parts, z, sel, name):
    tm = ROW_TILE
    h_spec = pl.BlockSpec((tm, LANES), lambda i: (i, 0))

    def body(o0, o1, o2, l0, l1, l2, z_ref, sel_ref, u_ref, ut_ref, o_ref, lse_ref, nat):
        ls = [_load_natural(l, nat, d) for l, d in zip((l0, l1, l2), DILATIONS)]
        m = jnp.maximum(jnp.maximum(ls[0], ls[1]), ls[2])
        tot = m + jnp.log(jnp.exp(ls[0] - m) + jnp.exp(ls[1] - m) + jnp.exp(ls[2] - m))
        o = jnp.zeros((tm, D_MODEL), F32)
        for o_in, l, d in zip((o0, o1, o2), ls, DILATIONS):
            weight = _dot_split(jnp.exp(l - tot), sel_ref[...], NT_DIMS)
            o = o + weight * _load_natural(o_in, nat, d)
        zv = z_ref[...].astype(F32)
        u = o * (zv * _sigmoid(zv))
        u_ref[...] = u.astype(BF16)
        ut_ref[...] = u.T.astype(BF16)
        o_ref[...] = o.astype(BF16)
        lse_ref[...] = tot

    return pl.pallas_call(
        body, name=name, grid=(SEQ // tm,),
        in_specs=[_class_spec(tm, d) for d in DILATIONS] + [_class_spec(tm, d, LANES) for d in DILATIONS]
        + [_row_spec(tm, D_MODEL, B_Z_SEGMENT), _vec_spec(D_MODEL, LANES)],
        out_specs=[_row_spec(tm, D_MODEL), pl.BlockSpec((D_MODEL, tm), lambda i: (0, i)),
                   _row_spec(tm, D_MODEL), h_spec],
        out_shape=[jax.ShapeDtypeStruct((SEQ, D_MODEL), BF16), jax.ShapeDtypeStruct((D_MODEL, SEQ), BF16),
                   jax.ShapeDtypeStruct((SEQ, D_MODEL), BF16), jax.ShapeDtypeStruct((SEQ, LANES), F32)],
        scratch_shapes=[_natural_scratch(tm)],
        compiler_params=_params("parallel"),
    )(*o_parts, *lse_parts, z, sel)


def _merge_bwd(dy, w_out, o, lse, z, sel, name):
    tm = ROW_TILE
    n_d = len(DILATIONS)

    def body(dy_ref, w_ref, o_ref, lse_ref, z_ref, sel_ref, dz_ref, *rest):
        do_refs, delta_refs, lse_refs, nat = rest[:n_d], rest[n_d:2 * n_d], rest[2 * n_d:3 * n_d], rest[-1]
        zv = z_ref[...].astype(F32)
        sz = _sigmoid(zv)
        duv = lax.dot_general(dy_ref[...], w_ref[...], NT_DIMS, preferred_element_type=F32)
        ov = o_ref[...].astype(F32)
        do = duv * (zv * sz)
        dz_ref[...] = (duv * ov * (sz * (1.0 + zv * (1.0 - sz)))).astype(BF16)
        delta = _dot_split(do * ov, sel_ref[...], (((1,), (0,)), ((), ())))
        lv = lse_ref[...]
        for i, d in enumerate(DILATIONS):
            _store_classes(do_refs[i], do, nat, d)
            _store_classes(delta_refs[i], delta, nat, d)
            _store_classes(lse_refs[i], lv, nat, d)

    res = pl.pallas_call(
        body, name=name, grid=(SEQ // tm,),
        in_specs=[_row_spec(tm, D_MODEL), _vec_spec(D_MODEL, D_MODEL), _row_spec(tm, D_MODEL), _row_spec(tm, LANES),
                  _row_spec(tm, D_MODEL, B_Z_SEGMENT), _vec_spec(D_MODEL, LANES)],
        out_specs=[_row_spec(tm, D_MODEL)] + [_class_spec(tm, d) for d in DILATIONS]
        + [_class_spec(tm, d, LANES) for d in DILATIONS] * 2,
        out_shape=[jax.ShapeDtypeStruct((SEQ, D_MODEL), BF16)] + [_class_shape(d, BF16) for d in DILATIONS]
        + [_class_shape(d, F32, LANES) for d in DILATIONS] * 2,
        scratch_shapes=[_natural_scratch(tm)],
        compiler_params=_params("parallel"),
    )(dy, w_out, o, lse, z, sel)
    flat = lambda a: a.reshape(SEQ, a.shape[-1])
    return (res[0], [flat(a) for a in res[1:1 + n_d]], [flat(a) for a in res[1 + n_d:1 + 2 * n_d]],
            [flat(a) for a in res[1 + 2 * n_d:]])


def _attn_bwd(q, k, proj, group, do, lse, delta, slopes, dilation, name):
    bpc = SEQ // dilation // ATTN_BLOCK
    heads = ATTN_HEADS_BWD
    n_blocks = SEQ // ATTN_BLOCK
    carry = bpc > 1
    width = heads * HEAD_DIM
    cur, prev = _attn_specs(heads)
    v_cur, v_prev = _attn_specs(heads, segment=3 * group + 2)
    assert heads == N_HEADS
    per_head = pl.BlockSpec((ATTN_BLOCK, LANES), lambda hg, b: (jnp.minimum(b, n_blocks - 1), 0))
    scale = HEAD_DIM ** -0.5

    def body(sl_ref, q_ref, kp_ref, kc_ref, vp_ref, vc_ref, do_ref, lse_ref, dl_ref,
             dq_ref, dk_ref, dv_ref, *scratch):
        b = pl.program_id(1)
        if carry:
            dk_carry, dv_carry = scratch

            @pl.when(b == n_blocks)
            def _():
                dk_ref[...] = dk_carry[...].astype(BF16)
                dv_ref[...] = dv_carry[...].astype(BF16)

            @pl.when(b < n_blocks)
            def _():
                step(sl_ref, q_ref, kp_ref, kc_ref, vp_ref, vc_ref, do_ref, lse_ref, dl_ref,
                     dq_ref, dk_ref, dv_ref, dk_carry, dv_carry, b)
        else:
            step(sl_ref, q_ref, kp_ref, kc_ref, vp_ref, vc_ref, do_ref, lse_ref, dl_ref,
                 dq_ref, dk_ref, dv_ref, None, None, b)

    def step(sl_ref, q_ref, kp_ref, kc_ref, vp_ref, vc_ref, do_ref, lse_ref, dl_ref,
             dq_ref, dk_ref, dv_ref, dk_carry, dv_carry, b):
        if carry:
            @pl.when(b == 0)
            def _():
                dk_carry[...] = jnp.zeros_like(dk_carry)
                dv_carry[...] = jnp.zeros_like(dv_carry)

        q3 = _head_stack(lambda cols: q_ref[:, cols], heads)
        k3 = _head_stack(lambda cols: _key_tile(kp_ref, kc_ref, cols, bpc), heads)
        v3 = _head_stack(lambda cols: _key_tile(vp_ref, vc_ref, cols, bpc), heads)
        do3 = _head_stack(lambda cols: do_ref[:, cols], heads)
        lse_t = lse_ref[...].T
        dl_t = dl_ref[...].T
        lse3 = jnp.stack([lse_t[h:h + 1, :] for h in range(heads)], axis=0)
        dl3 = jnp.stack([dl_t[h:h + 1, :] for h in range(heads)], axis=0)
        s = lax.dot_general(k3, q3, BATCH_NT_DIMS, preferred_element_type=F32)
        dist, valid = _attn_masks(b, bpc, dilation, transposed=True)
        p = jnp.exp(jnp.where(valid[None], s - dist[None] * sl_ref[...], NEG_INF) - lse3)
        dp = lax.dot_general(v3, do3, BATCH_NT_DIMS, preferred_element_type=F32)
        ds = (p * (dp - dl3)).astype(BF16)
        dq3 = lax.dot_general(ds, k3, BATCH_TN_DIMS, preferred_element_type=F32) * scale
        dk3 = lax.dot_general(ds, q3, BATCH_NN_DIMS, preferred_element_type=F32)
        dv3 = lax.dot_general(p.astype(BF16), do3, BATCH_NN_DIMS, preferred_element_type=F32)
        for h in range(heads):
            cols = slice(h * HEAD_DIM, (h + 1) * HEAD_DIM)
            dq_ref[:, cols] = dq3[h].astype(BF16)
            if carry:
                dk_ref[:, cols] = (dk_carry[:, cols] + dk3[h, :ATTN_BLOCK]).astype(BF16)
                dv_ref[:, cols] = (dv_carry[:, cols] + dv3[h, :ATTN_BLOCK]).astype(BF16)
                dk_carry[:, cols] = dk3[h, ATTN_BLOCK:]
                dv_carry[:, cols] = dv3[h, ATTN_BLOCK:]
            else:
                dk_ref[:, cols] = dk3[h].astype(BF16)
                dv_ref[:, cols] = dv3[h].astype(BF16)

    kv_out = prev if carry else cur
    return pl.pallas_call(
        body, name=name, grid=(N_HEADS // heads, n_blocks + (1 if carry else 0)),
        in_specs=[pl.BlockSpec((heads, 1, 1), lambda hg, b: (hg, 0, 0)), cur, prev, cur, v_prev, v_cur,
                  cur, per_head, per_head],
        out_specs=[cur, kv_out, kv_out],
        out_shape=[jax.ShapeDtypeStruct((SEQ, D_MODEL), BF16)] * 3,
        scratch_shapes=[pltpu.VMEM((ATTN_BLOCK, width), F32)] * 2 if carry else [],
        compiler_params=_params("parallel", "arbitrary"),
    )(slopes.reshape(N_HEADS, 1, 1), q, k, k, proj, proj, do, lse, delta)


def _qknorm_bwd(proj, group, qw, kw, seg, dq, dk, dv, name):
    tm = ROW_TILE

    def body(q_in, k_in, qw_ref, kw_ref, seg_ref, dq_ref, dk_ref, dv_ref, dproj_ref, sums_ref):
        segv = seg_ref[...]
        sums = []
        for part, (raw_ref, w_ref, dn_ref) in enumerate(((q_in, qw_ref, dq_ref), (k_in, kw_ref, dk_ref))):
            raw = raw_ref[...].astype(F32)
            dn = dn_ref[...].astype(F32)
            r = _qk_rstd(raw, segv)
            xhat = raw * r
            gq = dn * w_ref[...]
            draw = r * (gq - xhat * _segmean(xhat * gq, segv))
            dproj_ref[:, part * D_MODEL:(part + 1) * D_MODEL] = draw.astype(BF16)
            sums.append(jnp.sum(dn * xhat, axis=0, keepdims=True))
        dproj_ref[:, 2 * D_MODEL:] = dv_ref[...]

        @pl.when(pl.program_id(0) == 0)
        def _():
            sums_ref[...] = jnp.zeros_like(sums_ref)

        sums_ref[...] += jnp.concatenate(sums + [jnp.zeros((6, D_MODEL), F32)], axis=0)

    return pl.pallas_call(
        body, name=name, grid=(SEQ // tm,),
        in_specs=[_row_spec(tm, D_MODEL, 3 * group), _row_spec(tm, D_MODEL, 3 * group + 1),
                  _vec_spec(1, D_MODEL), _vec_spec(1, D_MODEL), _vec_spec(256, 256)] + [_row_spec(tm, D_MODEL)] * 3,
        out_specs=[_row_spec(tm, 3 * D_MODEL), _vec_spec(8, D_MODEL)],
        out_shape=[jax.ShapeDtypeStruct((SEQ, 3 * D_MODEL), BF16), jax.ShapeDtypeStruct((8, D_MODEL), F32)],
        compiler_params=_params("arbitrary"),
    )(proj, proj, qw, kw, seg, dq, dk, dv)


B_TN = 512
B_GROUP_TILES = 3 * D_MODEL // B_TN
B_Z_TILE0 = 3 * B_GROUP_TILES
B_Z_TILES = D_MODEL // B_TN
B_TILES = B_Z_TILE0 + B_Z_TILES
B_Z_SEGMENT = 3 * len(DILATIONS)


def _local_step(x, target, mods, norm_g, conv_w, conv_b, ln_g, ln_b, q_norm, k_norm, chip, own_wa_in, own_wb_in,
                weights_a, weights_b, forward_weights_b, send_grads_b, forward_grads_b, send_grads_a):
    row = lambda a, i: a[i:i + 1]
    shift0, scale0, gate0 = row(mods[0], 0), row(mods[0], 1), row(mods[0], 2)
    shift1, scale1, gate1 = row(mods[1], 0), row(mods[1], 1), row(mods[1], 2)
    g0, g1 = row(norm_g, 0), row(norm_g, 1)
    seg = _seg_matrix()
    slopes = jnp.exp2(-8.0 * jnp.arange(1, N_HEADS + 1, dtype=F32) / N_HEADS)
    qw = [jnp.tile(q_norm[g:g + 1], (1, N_HEADS)) for g in range(3)]
    kw = [jnp.tile(k_norm[g:g + 1], (1, N_HEADS)) for g in range(3)]

    h0, h0t = _normmod_fwd(x, g0, scale0, shift0, "prenorm0")
    nsa = own_wa_in.shape[2]
    tiles_a = dict(tn=nsa, total_tiles=N_CHIPS, part_of=lambda tile: 0)
    own_ids, rest_ids, own_tiles = _own_first(chip, N_CHIPS)
    proj_a = _in_tiles([h0], own_wa_in, own_ids, own_tiles, name="a_in_own", **tiles_a)
    wa_in, wa_out = weights_a(proj_a)
    ja = wa_in.shape[0]
    proj_a = _in_tiles([h0], wa_in, rest_ids, N_CHIPS - own_tiles, name="a_in_rest", prev=proj_a, **tiles_a)
    u5, u5t, u2 = _conv_fwd(proj_a, conv_w, conv_b, ln_g, ln_b, "a_conv")
    x1, y_a, h1t, h1c = _out_a(u5, wa_out, x, gate0, g1, scale1, shift1, "a_out")

    tiles_b = dict(tn=B_TN, total_tiles=B_TILES,
                   part_of=lambda tile: jnp.where(tile >= B_Z_TILE0, 0, tile // B_GROUP_TILES))
    own_ids, rest_ids, own_tiles = _own_first(chip, B_TILES)
    proj_b = _in_tiles(h1c, own_wb_in, own_ids, own_tiles, name="b_in_own", **tiles_b)
    forward_weights_b(proj_b)
    wb_in, wb_out = weights_b(proj_b)
    jb, _, nsb = wb_in.shape
    proj_b = _in_tiles(h1c, wb_in, rest_ids, B_TILES - own_tiles, name="b_in_rest", prev=proj_b, **tiles_b)
    h1 = h1c[0]
    qkv, o_parts, lse_parts = [], [], []
    for g, d in enumerate(DILATIONS):
        qn, kn = _qknorm_fwd(proj_b, g, qw[g], kw[g], seg, f"b_qknorm_g{g}")
        og, lg = _attn_fwd(qn, kn, proj_b, g, slopes, d, f"b_attn_g{g}")
        qkv.append((qn, kn))
        o_parts.append(og if d == 1 else og.reshape(d, SEQ // d, D_MODEL))
        lse_parts.append(lg if d == 1 else lg.reshape(d, SEQ // d, LANES))
    sel = _head_selector()
    u_b, u_bt, o_b, lse_b = _merge_fwd(o_parts, lse_parts, proj_b, sel, "b_merge")
    e, dy_b, sums_loss = _out_b_loss(u_b, wb_out, x1, gate1, target, "b_out_loss")

    dwb_out = _mm(u_bt, dy_b, tn=D_MODEL, tile0=0, n_tiles=1, out_dtype=BF16, name="b_dwout")
    dz_b, do_c, delta_c, lse_c = _merge_bwd(dy_b, wb_out, o_b, lse_b, proj_b, sel, "b_merge_bwd")
    dwb_in = _mm(h1t, dz_b, tn=B_TN, tile0=B_Z_TILE0, n_tiles=B_Z_TILES, out_dtype=BF16, name="b_dwin_z",
                 out3d=(jb, nsb))
    dh1_parts = [_mm_nt(dz_b, wb_in, tn=B_TN, tile0=B_Z_TILE0, n_tiles=B_Z_TILES, name="b_dh_z")]
    qk_sums = []
    for g, d in enumerate(DILATIONS):
        qn, kn = qkv[g]
        dq, dk, dv = _attn_bwd(qn, kn, proj_b, g, do_c[g], lse_c[g], delta_c[g], slopes, d, f"b_attn_bwd_g{g}")
        dproj, sums_qk = _qknorm_bwd(proj_b, g, qw[g], kw[g], seg, dq, dk, dv, f"b_qknorm_bwd_g{g}")
        qk_sums.append(sums_qk)
        dwb_in = _mm(h1t if d == 1 else h1c[g], dproj, tn=B_TN, tile0=g * B_GROUP_TILES, n_tiles=B_GROUP_TILES,
                     out_dtype=BF16, name=f"b_dwin_g{g}", out3d=(jb, nsb), prev=dwb_in, transpose_lhs=d != 1)
        dh = _mm_nt(dproj, wb_in, tn=B_TN, tile0=g * B_GROUP_TILES, n_tiles=B_GROUP_TILES, name=f"b_dh_g{g}")
        dh1_parts.append(dh)
    token = send_grads_b(dwb_in, dwb_out)
    dx1, sums_n1, dy_a = _normmod_bwd(x1, g1, scale1 + token[0:1, 0:1], dh1_parts, e, "prenorm1_bwd",
                                      part_dilations=(1,) + DILATIONS, gated=(gate0, y_a))
    token = forward_grads_b(dx1)

    dwa_out = _mm(u5t, dy_a, tn=D_MODEL, tile0=0, n_tiles=1, out_dtype=BF16, name="a_dwout")
    du2, dz_a, sums_ln = _conv_bwd_pointwise(dy_a, wa_out, proj_a, u2, ln_g + token[0:1, 0:1], ln_b,
                                             "a_conv_bwd_pw")
    dproj_a, dconv_w = _conv_bwd_taps(du2, dz_a, proj_a, conv_w, "a_conv_bwd_taps")
    dwa_in = _mm(h0t, dproj_a, tn=nsa, tile0=0, n_tiles=ja, out_dtype=BF16, name="a_dwin", out3d=(ja, nsa))
    token = send_grads_a(dwa_in, dwa_out)
    dh0 = _mm_nt(dproj_a, wa_in, tn=nsa, tile0=0, n_tiles=ja, name="a_dh", after=token)
    grad_x, sums_n0 = _normmod_bwd(x, g0, scale0, [dh0], dx1, "prenorm0_bwd")

    small = dict(
        dnorm_g=jnp.concatenate([sums_n0[0:1], sums_n1[0:1]], axis=0),
        dmod0=jnp.concatenate([sums_n0[2:3], sums_n0[1:2], sums_n1[3:4]], axis=0),
        dmod1=jnp.concatenate([sums_n1[2:3], sums_n1[1:2], sums_loss[0:1]], axis=0),
        dln_g=sums_ln[0:1], dln_b=sums_ln[1:2], dconv_b=sums_ln[2:3],
        dconv_w=dconv_w[:CONV_WIDTH],
        dq_norm=jnp.concatenate([s[0:1] for s in qk_sums], axis=0),
        dk_norm=jnp.concatenate([s[1:2] for s in qk_sums], axis=0),
        loss_cols=sums_loss[1:2],
    )
    return grad_x, small


def _adamw(w, g, m, v, name, after=None, copy_grad=False):
    rows, cols = w.shape
    tr = rows if rows <= 128 else (256 if cols <= D_MODEL else 128)
    c1 = 1.0 / (1.0 - ADAM_B1 ** ADAM_STEP)
    c2 = 1.0 / (1.0 - ADAM_B2 ** ADAM_STEP)
    extra = [] if after is None else [after]
    n_out = 4 if copy_grad else 3

    def body(w_ref, g_ref, m_ref, v_ref, *rest):
        d_ref, mo_ref, vo_ref = rest[len(extra):len(extra) + 3]
        gv = g_ref[...]
        if copy_grad:
            rest[-1][...] = gv
        mn = ADAM_B1 * m_ref[...] + (1.0 - ADAM_B1) * gv
        vn = ADAM_B2 * v_ref[...] + (1.0 - ADAM_B2) * (gv * gv)
        mo_ref[...] = mn
        vo_ref[...] = vn
        d_ref[...] = -ADAM_LR * ((mn * c1) / (jnp.sqrt(vn * c2) + ADAM_EPS) + ADAM_WD * w_ref[...])

    spec = pl.BlockSpec((tr, cols), lambda i: (i, 0))
    return pl.pallas_call(
        body, name=name, grid=(rows // tr,),
        in_specs=[spec] * 4 + [pl.BlockSpec(memory_space=pl.ANY)] * len(extra), out_specs=[spec] * n_out,
        out_shape=[jax.ShapeDtypeStruct((rows, cols), F32)] * n_out,
        compiler_params=_params("parallel"),
    )(w, g, m, v, *extra)


def _cast_into_slot(w, chip_idx, name, keep_own=False, after=None):
    rows, cols = w.shape
    tr = 256
    extra = [] if after is None else [after]

    def body(ch_ref, w_ref, *rest):
        wb = w_ref[...].astype(BF16)
        for o_ref in rest[len(extra):]:
            o_ref[...] = wb

    slot_spec = pl.BlockSpec((None, tr, cols), lambda i, ch: (ch[0], i, 0))
    own_spec = pl.BlockSpec((None, tr, cols), lambda i, ch: (0, i, 0))
    res = pl.pallas_call(
        body, name=name,
        grid_spec=pltpu.PrefetchScalarGridSpec(
            num_scalar_prefetch=1, grid=(rows // tr,),
            in_specs=[pl.BlockSpec((tr, cols), lambda i, ch: (i, 0))] + [pl.BlockSpec(memory_space=pl.ANY)] * len(extra),
            out_specs=[slot_spec, own_spec] if keep_own else [slot_spec]),
        out_shape=[jax.ShapeDtypeStruct((N_CHIPS, rows, cols), BF16)]
        + ([jax.ShapeDtypeStruct((1, rows, cols), BF16)] if keep_own else []),
        compiler_params=_params("parallel"),
    )(chip_idx, w, *extra)
    return tuple(res) if keep_own else res[0]


def _position():
    x, y, c = lax.axis_index("x"), lax.axis_index("y"), lax.axis_index("c")
    return x, y, c


def _xor_peer(x, y, c, k):
    return (x ^ ((k >> 2) & 1), y ^ ((k >> 1) & 1), c ^ (k & 1))


def _chip_peer(x, y, k):
    return (x ^ ((k >> 1) & 1), y ^ (k & 1))


def _ada_forward(c_row, ada_w, ada_b, conv_w, after=()):
    ns = ada_w.shape[2]
    cw = conv_w.shape[1]

    def body(c_ref, w_ref, b_ref, cv_ref, *rest):
        (mod_ref, sc_ref, cvo_ref, c_all, mp, parts, cv_parts,
         send1, recv1, send2, recv2, send3, recv3, w_vmem, w_sem, w_bf16) = rest[len(after):]
        w_load = pltpu.make_async_copy(w_ref, w_vmem, w_sem)
        w_load.start()
        x, y, c = _position()
        me = 4 * x + 2 * y + c
        chip = 2 * x + y

        def c_copy(k):
            return pltpu.make_async_remote_copy(
                src_ref=c_all.at[me], dst_ref=c_all.at[me], send_sem=send1.at[k - 1], recv_sem=recv1.at[k - 1],
                device_id=_xor_peer(x, y, c, k), device_id_type=MESH)

        def cv_copy(k):
            px, py = _chip_peer(x, y, k)
            return pltpu.make_async_remote_copy(
                src_ref=cv_parts.at[chip], dst_ref=cv_parts.at[chip], send_sem=send3.at[k - 1],
                recv_sem=recv3.at[k - 1], device_id=(px, py, c), device_id_type=MESH)

        c_all[me] = c_ref[...]
        cv_parts[chip] = cv_ref[...]
        for k in range(1, N_DEV):
            c_copy(k).start()
        for k in range(1, N_CHIPS):
            cv_copy(k).start()
        w_load.wait()
        for l in range(2):
            w_bf16[l] = w_vmem[l].astype(BF16)
        for k in range(1, N_DEV):
            c_copy(k).wait_recv()
        cv = jnp.concatenate([c_all[i] for i in range(N_DEV)], axis=0)
        sc = cv * _sigmoid(cv)
        sc_ref[...] = sc
        for l in range(2):
            res = jnp.dot(sc.astype(BF16), w_bf16[l], preferred_element_type=F32)
            for i in range(N_DEV):
                mp[i, l:l + 1, :] = res[i:i + 1, :]

        def mod_copy(k):
            px, py = _chip_peer(x, y, k)
            return pltpu.make_async_remote_copy(
                src_ref=mp.at[4 * px + 2 * py + c], dst_ref=parts.at[chip], send_sem=send2.at[k - 1],
                recv_sem=recv2.at[k - 1], device_id=(px, py, c), device_id_type=MESH)

        for k in range(1, N_CHIPS):
            mod_copy(k).start()
        parts[chip] = mp[me]
        for k in range(1, N_CHIPS):
            mod_copy(k).wait_recv()
            cv_copy(k).wait_recv()
        mod_ref[...] = jnp.concatenate([parts[j] for j in range(N_CHIPS)], axis=1) + b_ref[...]
        cvo_ref[...] = jnp.concatenate([cv_parts[j] for j in range(N_CHIPS)], axis=1)
        for k in range(1, N_DEV):
            c_copy(k).wait_send()
        for k in range(1, N_CHIPS):
            mod_copy(k).wait_send()
            cv_copy(k).wait_send()

    vm = pl.BlockSpec(memory_space=pltpu.VMEM)
    return pl.pallas_call(
        body, name="ada_forward",
        in_specs=[vm, ANY_SPEC, vm, vm] + [ANY_SPEC] * len(after), out_specs=[vm] * 3,
        out_shape=[jax.ShapeDtypeStruct((2, 3 * D_MODEL), F32), jax.ShapeDtypeStruct((N_DEV, D_MODEL), F32),
                   jax.ShapeDtypeStruct((CONV_WIDTH, N_CHIPS * cw), F32)],
        scratch_shapes=[pltpu.VMEM((N_DEV, 1, D_MODEL), F32), pltpu.VMEM((N_DEV, 2, ns), F32),
                        pltpu.VMEM((N_CHIPS, 2, ns), F32), pltpu.VMEM((N_CHIPS, CONV_WIDTH, cw), F32),
                        pltpu.SemaphoreType.DMA((N_DEV - 1,)), pltpu.SemaphoreType.DMA((N_DEV - 1,)),
                        pltpu.SemaphoreType.DMA((N_CHIPS - 1,)), pltpu.SemaphoreType.DMA((N_CHIPS - 1,)),
                        pltpu.SemaphoreType.DMA((N_CHIPS - 1,)), pltpu.SemaphoreType.DMA((N_CHIPS - 1,)),
                        pltpu.VMEM(ada_w.shape, F32), pltpu.SemaphoreType.DMA(()), pltpu.VMEM(ada_w.shape, BF16)],
        compiler_params=pltpu.CompilerParams(vmem_limit_bytes=VMEM_LIMIT_BYTES),
    )(c_row, ada_w, ada_b, conv_w, *after)


HBM_SPEC = pl.BlockSpec(memory_space=pltpu.HBM)
ANY_SPEC = pl.BlockSpec(memory_space=pl.ANY)
SEM_SPEC = pl.BlockSpec(memory_space=pltpu.SEMAPHORE)
SPLIT_PARAMS = dict(compiler_params=pltpu.CompilerParams(has_side_effects=pltpu.SideEffectType.DATAFLOW_SIDE_EFFECTING))
TOKEN = jax.ShapeDtypeStruct((8, 128), F32)
ENTRY_HANDSHAKES = {name: (i, peers) for i, (name, peers) in enumerate((
    ("gather_start_a", "chips"), ("gather_start_b", "chips"),
    ("gather_forward_a", "sibling"), ("gather_forward_b", "sibling"),
    ("reduce_d2d_start_b", "sibling"), ("reduce_d2d_start_a", "sibling"),
    ("reduce_ici_start_b", "chips"), ("reduce_ici_start_a", "chips"),
    ("reduce_share_start_b", "sibling"), ("reduce_share_start_a", "sibling"),
    ("small_gather_start", "devices")))}


def _hbm(arrays):
    return [pltpu.with_memory_space_constraint(a, pltpu.HBM) for a in arrays]


def _hbm_like(arrays):
    return [pltpu.HBM(a.shape, a.dtype) for a in arrays]


def _gather_start(lands, after, name):
    n = len(lands)

    def body(*refs):
        _handshake(ENTRY_HANDSHAKES[name][1])
        ins = refs[:n]
        send, recv = refs[n + 1], refs[n + 2]
        x, y, c = _position()
        chip = 2 * x + y
        for t in range(n):
            rh = ins[t].shape[1] // 2
            for k in range(1, N_CHIPS):
                px, py = _chip_peer(x, y, k)
                block = ins[t].at[chip, pl.ds(c * rh, rh)]
                pltpu.make_async_remote_copy(
                    src_ref=block, dst_ref=block, send_sem=send.at[3 * t + k - 1], recv_sem=recv.at[3 * t + k - 1],
                    device_id=(px, py, c), device_id_type=MESH).start()
        refs[-1][...] = jnp.zeros(TOKEN.shape, F32)

    res = pl.pallas_call(
        body, name=name, in_specs=[HBM_SPEC] * n + [ANY_SPEC],
        out_specs=(SEM_SPEC, SEM_SPEC, *[HBM_SPEC] * n, pl.BlockSpec(memory_space=pltpu.VMEM)),
        out_shape=(pltpu.SemaphoreType.DMA((3 * n,)), pltpu.SemaphoreType.DMA((3 * n,)), *_hbm_like(lands), TOKEN),
        input_output_aliases={t: 2 + t for t in range(n)}, **_split_params(name),
    )(*_hbm(lands), after)
    return res[0], res[1], list(res[2:2 + n]), res[-1]


def _gather_forward(send, recv, lands, after, name):
    n = len(lands)

    def body(*refs):
        _handshake(ENTRY_HANDSHAKES[name][1])
        ins = refs[:n]
        send1, recv1 = refs[n], refs[n + 1]
        send2, recv2 = refs[n + 3], refs[n + 4]
        x, y, c = _position()
        chip = 2 * x + y
        for t in range(n):
            rh = ins[t].shape[1] // 2
            half = pl.ds(c * rh, rh)
            for k in range(1, N_CHIPS):
                px, py = _chip_peer(x, y, k)
                s = 3 * t + k - 1
                got = ins[t].at[2 * px + py, half]
                cp = pltpu.make_async_remote_copy(
                    src_ref=ins[t].at[chip, half], dst_ref=got, send_sem=send1.at[s], recv_sem=recv1.at[s],
                    device_id=(px, py, c), device_id_type=MESH)
                cp.wait_send()
                cp.wait_recv()
                pltpu.make_async_remote_copy(
                    src_ref=got, dst_ref=got, send_sem=send2.at[s], recv_sem=recv2.at[s],
                    device_id=(x, y, 1 - c), device_id_type=MESH).start()
        refs[-1][...] = jnp.zeros(TOKEN.shape, F32)

    res = pl.pallas_call(
        body, name=name, in_specs=[HBM_SPEC] * n + [SEM_SPEC, SEM_SPEC, ANY_SPEC],
        out_specs=(SEM_SPEC, SEM_SPEC, *[HBM_SPEC] * n, pl.BlockSpec(memory_space=pltpu.VMEM)),
        out_shape=(pltpu.SemaphoreType.DMA((3 * n,)), pltpu.SemaphoreType.DMA((3 * n,)), *_hbm_like(lands), TOKEN),
        input_output_aliases={t: 2 + t for t in range(n)}, **_split_params(name),
    )(*lands, send, recv, after)
    return res[0], res[1], list(res[2:2 + n]), res[-1]


def _gather_wait(send, recv, lands, after, name):
    n = len(lands)

    def body(*refs):
        ins = refs[:n]
        send_ref, recv_ref = refs[n], refs[n + 1]
        x, y, c = _position()
        for t in range(n):
            rh = ins[t].shape[1] // 2
            for k in range(1, N_CHIPS):
                px, py = _chip_peer(x, y, k)
                cp = pltpu.make_async_remote_copy(
                    src_ref=ins[t].at[2 * px + py, pl.ds(c * rh, rh)],
                    dst_ref=ins[t].at[2 * px + py, pl.ds((1 - c) * rh, rh)], send_sem=send_ref.at[3 * t + k - 1],
                    recv_sem=recv_ref.at[3 * t + k - 1], device_id=(x, y, 1 - c), device_id_type=MESH)
                cp.wait_send()
                cp.wait_recv()

    res = pl.pallas_call(
        body, name=name, in_specs=[HBM_SPEC] * n + [SEM_SPEC, SEM_SPEC, ANY_SPEC], out_specs=[HBM_SPEC] * n,
        out_shape=_hbm_like(lands), input_output_aliases={t: t for t in range(n)}, **SPLIT_PARAMS,
    )(*lands, send, recv, after)
    return list(res)


def _handshake(peers):
    x, y, c = _position()
    if peers == "sibling":
        ids = [(x, y, 1 - c)]
    elif peers == "chips":
        ids = [(*_chip_peer(x, y, k), c) for k in range(1, N_CHIPS)]
    else:
        ids = [_xor_peer(x, y, c, k) for k in range(1, N_DEV)]
    barrier = pltpu.get_barrier_semaphore()
    for peer in ids:
        pl.semaphore_signal(barrier, inc=1, device_id=peer, device_id_type=MESH)
    pl.semaphore_wait(barrier, len(ids))


def _split_params(name):
    return dict(compiler_params=pltpu.CompilerParams(
        has_side_effects=pltpu.SideEffectType.DATAFLOW_SIDE_EFFECTING, collective_id=ENTRY_HANDSHAKES[name][0]))


def _split_start(name, arrays, n_sems, after, issue):
    m = len(arrays)

    def body(*refs):
        _handshake(ENTRY_HANDSHAKES[name][1])
        issue(refs[:m], refs[m + 1], refs[m + 2])
        refs[-1][...] = jnp.zeros(TOKEN.shape, F32)

    res = pl.pallas_call(
        body, name=name, in_specs=[HBM_SPEC] * m + [ANY_SPEC],
        out_specs=(SEM_SPEC, SEM_SPEC, *[HBM_SPEC] * m, pl.BlockSpec(memory_space=pltpu.VMEM)),
        out_shape=(pltpu.SemaphoreType.DMA((n_sems,)), pltpu.SemaphoreType.DMA((n_sems,)), *_hbm_like(arrays), TOKEN),
        input_output_aliases={t: 2 + t for t in range(m)}, **_split_params(name),
    )(*_hbm(arrays), after)
    return res[0], res[1], list(res[2:2 + m]), res[-1]


def _split_wait(name, arrays, send, recv, after, await_all):
    m = len(arrays)

    def body(*refs):
        await_all(refs[:m], refs[m], refs[m + 1])

    res = pl.pallas_call(
        body, name=name, in_specs=[HBM_SPEC] * m + [SEM_SPEC, SEM_SPEC, ANY_SPEC], out_specs=[HBM_SPEC] * m,
        out_shape=_hbm_like(arrays), input_output_aliases={t: t for t in range(m)}, **SPLIT_PARAMS,
    )(*arrays, send, recv, after)
    return list(res)


def _sibling_copies(refs, send, recv, n):
    x, y, c = _position()
    cps = []
    for t in range(n):
        rh = refs[t].shape[1] // 2
        cps.append(pltpu.make_async_remote_copy(
            src_ref=refs[t].at[pl.ds(0, N_CHIPS), pl.ds((1 - c) * rh, rh)], dst_ref=refs[n + t],
            send_sem=send.at[t], recv_sem=recv.at[t], device_id=(x, y, 1 - c), device_id_type=MESH))
    return cps


def _reduce_sibling_start(grads, after, name):
    n = len(grads)
    lands = [lax.empty((N_CHIPS, g.shape[1] // 2, g.shape[2]), BF16) for g in grads]

    def issue(refs, send, recv):
        for cp in _sibling_copies(refs, send, recv, n):
            cp.start()

    return _split_start(name, list(grads) + lands, n, after, issue)


def _reduce_sibling_wait(send, recv, arrays, after, name):
    n = len(arrays) // 2

    def await_all(refs, send_ref, recv_ref):
        for cp in _sibling_copies(refs, send_ref, recv_ref, n):
            cp.wait_send()
            cp.wait_recv()

    res = _split_wait(name, arrays, send, recv, after, await_all)
    return res[:n], res[n:]


def _add_sibling_half(grad, got, dev_idx, name):
    j, r, cols = grad.shape
    rh = r // 2
    tr = rh
    nb = rh // tr

    def body(idx_ref, g_ref, got_ref, out_ref):
        out_ref[...] = (g_ref[...].astype(F32) + got_ref[...].astype(F32)).astype(BF16)

    return pl.pallas_call(
        body, name=name,
        grid_spec=pltpu.PrefetchScalarGridSpec(
            num_scalar_prefetch=1, grid=(j, nb),
            in_specs=[pl.BlockSpec((None, tr, cols), lambda jj, i, idx: (jj, idx[2] * nb + i, 0)),
                      pl.BlockSpec((None, tr, cols), lambda jj, i, idx: (jj, i, 0))],
            out_specs=pl.BlockSpec((None, tr, cols), lambda jj, i, idx: (jj, i, 0))),
        out_shape=jax.ShapeDtypeStruct((j, rh, cols), BF16),
        compiler_params=_params("parallel", "parallel"),
    )(dev_idx, grad, got)


def _chip_copies(refs, send, recv, n, receiving):
    x, y, c = _position()
    chip = 2 * x + y
    cps = []
    for t in range(n):
        for k in range(1, N_CHIPS):
            px, py = _chip_peer(x, y, k)
            cps.append(pltpu.make_async_remote_copy(
                src_ref=refs[t].at[2 * px + py], dst_ref=refs[n + t].at[2 * px + py if receiving else chip],
                send_sem=send.at[3 * t + k - 1], recv_sem=recv.at[3 * t + k - 1],
                device_id=(px, py, c), device_id_type=MESH))
    return cps


def _reduce_chips_start(partials, after, name):
    n = len(partials)
    lands = [lax.empty(p.shape, BF16) for p in partials]

    def issue(refs, send, recv):
        for cp in _chip_copies(refs, send, recv, n, False):
            cp.start()

    return _split_start(name, list(partials) + lands, 3 * n, after, issue)


def _reduce_chips_wait(send, recv, arrays, after, name):
    n = len(arrays) // 2

    def await_all(refs, send_ref, recv_ref):
        for cp in _chip_copies(refs, send_ref, recv_ref, n, True):
            cp.wait_send()
            cp.wait_recv()

    res = _split_wait(name, arrays, send, recv, after, await_all)
    return res[:n], res[n:]


def _sum_partials(land, partial, dev_idx, name):
    _, rh, cols = land.shape
    tr = min(rh, 256)
    nb = rh // tr

    def body(idx_ref, l_ref, p_ref, o_ref):
        chip = idx_ref[1]
        acc = jnp.where(chip == 0, p_ref[...], l_ref[0]).astype(F32)
        for s in range(1, N_CHIPS):
            acc = acc + jnp.where(chip == s, p_ref[...], l_ref[s]).astype(F32)
        o_ref[...] = acc

    return pl.pallas_call(
        body, name=name,
        grid_spec=pltpu.PrefetchScalarGridSpec(
            num_scalar_prefetch=1, grid=(nb,),
            in_specs=[pl.BlockSpec((N_CHIPS, tr, cols), lambda i, idx: (0, i, 0)),
                      pl.BlockSpec((None, tr, cols), lambda i, idx: (idx[1], i, 0))],
            out_specs=pl.BlockSpec((tr, cols), lambda i, idx: (idx[2] * nb + i, 0))),
        out_shape=jax.ShapeDtypeStruct((2 * rh, cols), F32), compiler_params=_params("parallel"),
    )(dev_idx, land, partial)


def _half_copies(refs, send, recv, receiving):
    x, y, c = _position()
    cps = []
    for t, ref in enumerate(refs):
        rh = ref.shape[0] // 2
        cps.append(pltpu.make_async_remote_copy(
            src_ref=ref.at[pl.ds(c * rh, rh)], dst_ref=ref.at[pl.ds(((1 - c) if receiving else c) * rh, rh)],
            send_sem=send.at[t], recv_sem=recv.at[t], device_id=(x, y, 1 - c), device_id_type=MESH))
    return cps


def _share_halves_start(totals, after, name):
    def issue(refs, send, recv):
        for cp in _half_copies(refs, send, recv, False):
            cp.start()

    return _split_start(name, list(totals), len(totals), after, issue)


def _share_halves_wait(send, recv, totals, after, name):
    def await_all(refs, send_ref, recv_ref):
        for cp in _half_copies(refs, send_ref, recv_ref, True):
            cp.wait_send()
            cp.wait_recv()

    return _split_wait(name, totals, send, recv, after, await_all)


SMALL_ROWS = 56


def _small_copies(refs, send, recv, receiving):
    x, y, c = _position()
    me = 4 * x + 2 * y + c
    cps = []
    for k in range(1, N_DEV):
        px, py, pc = _xor_peer(x, y, c, k)
        cps.append(pltpu.make_async_remote_copy(
            src_ref=refs[0], dst_ref=refs[1].at[4 * px + 2 * py + pc if receiving else me],
            send_sem=send.at[k - 1], recv_sem=recv.at[k - 1], device_id=(px, py, pc), device_id_type=MESH))
    return cps


def _small_gather_start(packed, after):
    land = lax.empty((N_DEV,) + packed.shape, F32)

    def issue(refs, send, recv):
        for cp in _small_copies(refs, send, recv, False):
            cp.start()

    return _split_start("small_gather_start", [packed, land], N_DEV - 1, after, issue)


def _small_gather_wait(send, recv, arrays, after):
    def await_all(refs, send_ref, recv_ref):
        for cp in _small_copies(refs, send_ref, recv_ref, True):
            cp.wait_send()
            cp.wait_recv()

    return _split_wait("small_gather_wait", arrays, send, recv, after, await_all)


def _reduce_small(packed, land, silu_c):
    ns = 3 * D_MODEL // N_CHIPS

    def body(p_ref, land_ref, sc_ref, tot_ref, gw_ref, loss_ref, qk_ref, allp):
        x, y, c = _position()
        me = 4 * x + 2 * y + c
        chip = 2 * x + y
        for i in range(N_DEV):
            allp[i] = jnp.where(me == i, p_ref[...], land_ref[i])
        tot = allp[0]
        for i in range(1, N_DEV):
            tot = tot + allp[i]
        tot_ref[...] = tot
        loss_ref[...] = jnp.sum(tot[11:12, :], axis=1, keepdims=True) * (0.5 / D_MODEL)
        fold = tot[5:11, 0:HEAD_DIM]
        for h in range(1, N_HEADS):
            fold = fold + tot[5:11, h * HEAD_DIM:(h + 1) * HEAD_DIM]
        qk_ref[...] = jnp.concatenate([fold, jnp.zeros((2, HEAD_DIM), F32)], axis=0)
        pad = jnp.zeros((LANES - N_DEV, D_MODEL), F32)
        sct = jnp.concatenate([sc_ref[...], pad], axis=0).T.astype(BF16)
        for l in range(2):
            dms = [allp[i, pl.ds(12 + 4 * l + chip, 1), :] for i in range(N_DEV)]
            dm = jnp.concatenate(dms + [pad], axis=0)[:, :ns].astype(BF16)
            gw_ref[l] = jnp.dot(sct, dm, preferred_element_type=F32)

    vm = pl.BlockSpec(memory_space=pltpu.VMEM)
    return pl.pallas_call(
        body, name="reduce_small", in_specs=[vm, vm, vm], out_specs=[vm] * 4,
        out_shape=[jax.ShapeDtypeStruct((SMALL_ROWS, D_MODEL), F32), jax.ShapeDtypeStruct((2, D_MODEL, ns), F32),
                   jax.ShapeDtypeStruct((1, 1), F32), jax.ShapeDtypeStruct((8, HEAD_DIM), F32)],
        scratch_shapes=[pltpu.VMEM((N_DEV, SMALL_ROWS, D_MODEL), F32)],
        compiler_params=pltpu.CompilerParams(vmem_limit_bytes=VMEM_LIMIT_BYTES),
    )(packed, land, silu_c)


def kernel(x, c, norm_g, ada_w, ada_b, a_w_in, a_conv_w, a_conv_b, a_ln_g, a_ln_b, a_w_out, b_w_in, b_q_norm, b_k_norm, b_w_out, loss_target, m_norm_g, m_ada_w, m_ada_b, m_a_w_in, m_a_conv_w, m_a_conv_b, m_a_ln_g, m_a_ln_b, m_a_w_out, m_b_w_in, m_b_q_norm, m_b_k_norm, m_b_w_out, v_norm_g, v_ada_w, v_ada_b, v_a_w_in, v_a_conv_w, v_a_conv_b, v_a_ln_g, v_a_ln_b, v_a_w_out, v_b_w_in, v_b_q_norm, v_b_k_norm, v_b_w_out):
    chip = 2 * lax.axis_index("x") + lax.axis_index("y")
    core = lax.axis_index("c")
    chip_idx = chip.astype(jnp.int32).reshape(1)
    dev_idx = jnp.stack([2 * chip + core, chip, core]).astype(jnp.int32)

    land_a_in, own_wa_in = _cast_into_slot(a_w_in[0], chip_idx, "cast_a_w_in", keep_own=True)
    lands_a = [land_a_in, _cast_into_slot(a_w_out[0], chip_idx, "cast_a_w_out")]
    mods, silu_c, conv_w_full = _ada_forward(c, ada_w, ada_b, a_conv_w[0], after=tuple(lands_a))
    send_a, recv_a, lands_a, token_a = _gather_start(lands_a, mods, "gather_start_a")
    land_b_in, own_wb_in = _cast_into_slot(b_w_in[0], chip_idx, "cast_b_w_in", keep_own=True, after=token_a)
    lands_b = [land_b_in, _cast_into_slot(b_w_out[0], chip_idx, "cast_b_w_out", after=token_a)]
    send_b, recv_b, lands_b, token_b = _gather_start(lands_b, token_a, "gather_start_b")
    mods = mods + token_b[0:2, 0:1]

    def weights_a(after):
        send, recv, lands, _ = _gather_forward(send_a, recv_a, lands_a, after, "gather_forward_a")
        w_in, w_out = _gather_wait(send, recv, lands, after, "gather_wait_a")
        return w_in, w_out.reshape(D_MODEL, D_MODEL)

    forwarded_b = []

    def weights_b(after):
        send, recv, lands, _ = forwarded_b
        w_in, w_out = _gather_wait(send, recv, lands, after, "gather_wait_b")
        return w_in, w_out.reshape(D_MODEL, D_MODEL)

    def forward_weights_b(after):
        forwarded_b.extend(_gather_forward(send_b, recv_b, lands_b, after, "gather_forward_b"))

    stage1, stage2 = {}, {}

    def send_grads(tag, dw_in, dw_out):
        grads = [dw_in, dw_out.reshape(N_CHIPS, D_MODEL // N_CHIPS, D_MODEL)]
        send, recv, arrays, token = _reduce_sibling_start(grads, dw_out, f"reduce_d2d_start_{tag}")
        stage1[tag] = (send, recv, arrays)
        return token

    def forward_grads(tag, after):
        send, recv, arrays = stage1[tag]
        grads, got = _reduce_sibling_wait(send, recv, arrays, after, f"reduce_d2d_wait_{tag}")
        partials = [_add_sibling_half(grads[i], got[i], dev_idx, f"reduce_add_{tag}_{i}") for i in range(2)]
        send, recv, arrays, token = _reduce_chips_start(partials, partials[1], f"reduce_ici_start_{tag}")
        stage2[tag] = (send, recv, arrays)
        return token

    stage3 = {}

    def sum_grads(tag, after):
        send, recv, arrays = stage2[tag]
        partials, lands = _reduce_chips_wait(send, recv, arrays, after, f"reduce_ici_wait_{tag}")
        totals = [_sum_partials(lands[i], partials[i], dev_idx, f"reduce_sum_{tag}_{i}") for i in range(2)]
        send, recv, totals, token = _share_halves_start(totals, totals[1], f"reduce_share_start_{tag}")
        stage3[tag] = (send, recv, totals)
        return token

    def finish_grads(tag, after):
        send, recv, totals = stage3[tag]
        return _share_halves_wait(send, recv, totals, after, f"reduce_share_wait_{tag}")

    grad_x, small = _local_step(
        x[0], loss_target[0], mods.reshape(2, 3, D_MODEL), norm_g, conv_w_full, a_conv_b, a_ln_g[0:1],
        a_ln_b[0:1], b_q_norm[0], b_k_norm[0], chip.astype(jnp.int32), own_wa_in, own_wb_in,
        weights_a, weights_b, forward_weights_b,
        functools.partial(send_grads, "b"), functools.partial(forward_grads, "b"), functools.partial(send_grads, "a"))

    ns = 3 * D_MODEL // N_CHIPS
    pad_mod = lambda dm: jnp.pad(dm.reshape(N_CHIPS, ns), ((0, 0), (0, D_MODEL - ns)))
    packed = jnp.concatenate([
        small["dnorm_g"], small["dconv_b"], small["dln_g"], small["dln_b"], small["dq_norm"], small["dk_norm"],
        small["loss_cols"], pad_mod(small["dmod0"]), pad_mod(small["dmod1"]), small["dconv_w"],
        jnp.zeros((SMALL_ROWS - 20 - CONV_WIDTH, D_MODEL), F32)], axis=0)
    send_s, recv_s, small_arrays, token_s = _small_gather_start(packed, packed)

    given = dict(norm_g=(norm_g, m_norm_g, v_norm_g), ada_w=(ada_w, m_ada_w, v_ada_w), ada_b=(ada_b, m_ada_b, v_ada_b),
                 a_w_in=(a_w_in, m_a_w_in, v_a_w_in), a_conv_w=(a_conv_w, m_a_conv_w, v_a_conv_w),
                 a_conv_b=(a_conv_b, m_a_conv_b, v_a_conv_b), a_ln_g=(a_ln_g, m_a_ln_g, v_a_ln_g),
                 a_ln_b=(a_ln_b, m_a_ln_b, v_a_ln_b), a_w_out=(a_w_out, m_a_w_out, v_a_w_out),
                 b_w_in=(b_w_in, m_b_w_in, v_b_w_in), b_q_norm=(b_q_norm, m_b_q_norm, v_b_q_norm),
                 b_k_norm=(b_k_norm, m_b_k_norm, v_b_k_norm), b_w_out=(b_w_out, m_b_w_out, v_b_w_out))
    order = ["norm_g", "ada_w", "ada_b", "a_w_in", "a_conv_w", "a_conv_b", "a_ln_g", "a_ln_b", "a_w_out", "b_w_in",
             "b_q_norm", "b_k_norm", "b_w_out"]
    outs = {}

    def update(k, g2, after=None, copy_grad=False):
        w, m, v = given[k]
        shape2 = g2.shape
        res = _adamw(w.reshape(shape2), g2, m.reshape(shape2), v.reshape(shape2), f"adamw_{k}", after, copy_grad)
        outs[k] = tuple(a.reshape(w.shape) for a in ((res[3] if copy_grad else g2), res[0], res[1], res[2]))

    token = forward_grads("a", token_s)
    token = sum_grads("b", token)
    packed, land = _small_gather_wait(send_s, recv_s, small_arrays, token)
    tot, g_ada_w, loss, qk = _reduce_small(packed, land, silu_c)
    g_b_in, g_b_out = finish_grads("b", tot)
    update("b_w_in", g_b_in, copy_grad=True)
    update("b_w_out", g_b_out, copy_grad=True)
    token = sum_grads("a", outs["b_w_in"][1])
    cw = D_MODEL // N_CHIPS
    g_small = dict(
        norm_g=tot[0:2], a_conv_b=tot[2:3], a_ln_g=tot[3:4], a_ln_b=tot[4:5],
        b_q_norm=qk[0:3], b_k_norm=qk[3:6],
        ada_b=jnp.stack([tot[12:16, :ns].reshape(3 * D_MODEL), tot[16:20, :ns].reshape(3 * D_MODEL)]),
        a_conv_w=lax.dynamic_slice(tot[20:20 + CONV_WIDTH], (0, chip * cw), (CONV_WIDTH, cw)),
    )
    update("ada_w", g_ada_w.reshape(2 * D_MODEL, ns), after=token)
    for k, g2 in g_small.items():
        update(k, g2, after=token)
    g_a_in, g_a_out = finish_grads("a", outs["ada_w"][1])
    update("a_w_in", g_a_in, copy_grad=True)
    update("a_w_out", g_a_out, copy_grad=True)
    return (loss.reshape(()), grad_x[None], *[outs[k][0] for k in order], *[outs[k][1] for k in order],
            *[outs[k][2] for k in order], *[outs[k][3] for k in order])
```

```python
import functools

import jax
import jax.numpy as jnp
from jax import lax
from jax.experimental import pallas as pl
from jax.experimental.pallas import tpu as pltpu

F32 = jnp.float32
BF16 = jnp.bfloat16

SEQ = 2048
D_MODEL = 1024
CONV_WIDTH = 31
HEAD_DIM = 64
N_HEADS = 16
DILATIONS = (1, 4, 16)
ATTN_BLOCK = 128
NORM_EPS = 1e-6
NEG_INF = -1e30
N_DEV = 8
N_CHIPS = 4

ADAM_LR = 0.001
ADAM_B1 = 0.9
ADAM_B2 = 0.999
ADAM_EPS = 1e-08
ADAM_WD = 0.01
ADAM_STEP = 10

VMEM_LIMIT_BYTES = 52 * 1024 * 1024
HALO = 32
LANES = 128
ROW_TILE = 512
MESH = pl.DeviceIdType.MESH


def _params(*sem):
    return pltpu.CompilerParams(dimension_semantics=sem or None, vmem_limit_bytes=VMEM_LIMIT_BYTES)


def _sigmoid(v):
    return 1.0 / (1.0 + jnp.exp(-v))


def _row_spec(tm, cols, col_block=0):
    return pl.BlockSpec((tm, cols), lambda i: (i, col_block))


def _vec_spec(rows, cols):
    return pl.BlockSpec((rows, cols), lambda i: (0, 0))


def _normmod(xv, g, scale, shift):
    r = lax.rsqrt(jnp.mean(xv * xv, axis=-1, keepdims=True) + NORM_EPS)
    return xv * r * g * (1.0 + scale) + shift


def _normmod_fwd(x, g, scale, shift, name):
    tm = ROW_TILE

    def body(x_ref, g_ref, sc_ref, sh_ref, h_ref, ht_ref):
        h = _normmod(x_ref[...], g_ref[...], sc_ref[...], sh_ref[...])
        h_ref[...] = h.astype(BF16)
        ht_ref[...] = h.T.astype(BF16)

    return pl.pallas_call(
        body, name=name, grid=(SEQ // tm,),
        in_specs=[_row_spec(tm, D_MODEL)] + [_vec_spec(1, D_MODEL)] * 3,
        out_specs=[_row_spec(tm, D_MODEL), pl.BlockSpec((D_MODEL, tm), lambda i: (0, i))],
        out_shape=[jax.ShapeDtypeStruct((SEQ, D_MODEL), BF16), jax.ShapeDtypeStruct((D_MODEL, SEQ), BF16)],
        compiler_params=_params("parallel"),
    )(x, g, scale, shift)


def _normmod_bwd(x, g, scale, dh_parts, dres, name, part_dilations=None, gated=None):
    tm = ROW_TILE
    n_parts = len(dh_parts)
    dils = part_dilations or (1,) * n_parts
    dh_parts = [p if d == 1 else p.reshape(d, SEQ // d, D_MODEL) for p, d in zip(dh_parts, dils)]
    n_gated = 0 if gated is None else 2

    def body(x_ref, g_ref, sc_ref, dres_ref, *rest):
        part_refs = rest[:n_parts]
        gated_refs = rest[n_parts:n_parts + n_gated]
        out_refs = rest[n_parts + n_gated:]
        dx_ref, sums_ref, nat = out_refs[0], out_refs[1], out_refs[-1]
        xv = x_ref[...]
        r = lax.rsqrt(jnp.mean(xv * xv, axis=-1, keepdims=True) + NORM_EPS)
        xn = xv * r
        dh = _load_natural(part_refs[0], nat, dils[0])
        for p, d in zip(part_refs[1:], dils[1:]):
            dh = dh + _load_natural(p, nat, d)
        gv = g_ref[...]
        one_sc = 1.0 + sc_ref[...]
        dxn = dh * (gv * one_sc)
        dx = dres_ref[...] + r * (dxn - xn * jnp.mean(dxn * xn, axis=-1, keepdims=True))
        dx_ref[...] = dx
        dhx = dh * xn
        rows = [jnp.sum(dhx, axis=0, keepdims=True) * one_sc,
                jnp.sum(dhx, axis=0, keepdims=True) * gv,
                jnp.sum(dh, axis=0, keepdims=True)]
        if gated is not None:
            gate_ref, y_ref = gated_refs
            out_refs[2][...] = (dx * gate_ref[...]).astype(BF16)
            rows.append(jnp.sum(dx * y_ref[...].astype(F32), axis=0, keepdims=True))
        sums = jnp.concatenate(rows + [jnp.zeros((8 - len(rows), D_MODEL), F32)], axis=0)

        @pl.when(pl.program_id(0) == 0)
        def _():
            sums_ref[...] = jnp.zeros_like(sums_ref)

        sums_ref[...] += sums

    gated_specs = [] if gated is None else [_vec_spec(1, D_MODEL), _row_spec(tm, D_MODEL)]
    dy_spec = [] if gated is None else [_row_spec(tm, D_MODEL)]
    dy_shape = [] if gated is None else [jax.ShapeDtypeStruct((SEQ, D_MODEL), BF16)]
    return pl.pallas_call(
        body, name=name, grid=(SEQ // tm,),
        in_specs=[_row_spec(tm, D_MODEL), _vec_spec(1, D_MODEL), _vec_spec(1, D_MODEL), _row_spec(tm, D_MODEL)]
        + [_class_spec(tm, d) for d in dils] + gated_specs,
        out_specs=[_row_spec(tm, D_MODEL), _vec_spec(8, D_MODEL)] + dy_spec,
        out_shape=[jax.ShapeDtypeStruct((SEQ, D_MODEL), F32), jax.ShapeDtypeStruct((8, D_MODEL), F32)] + dy_shape,
        scratch_shapes=[_natural_scratch(tm)],
        compiler_params=_params("arbitrary"),
    )(x, g, scale, dres, *dh_parts, *(gated or ()))


def _mm(lhs, rhs, *, tn, tile0, n_tiles, out_dtype, name, out3d=None, prev=None, transpose_lhs=False):
    mo, kc = lhs.shape[::-1] if transpose_lhs else lhs.shape
    cm = min(mo, 1024)
    tc = 256

    def body(l_ref, r_ref, *rest):
        if transpose_lhs:
            o_ref, lt_ref = rest[-2], rest[-1]

            @pl.when(pl.program_id(0) == 0)
            def _():
                for c in range(kc // tc):
                    lt_ref[:, c * tc:(c + 1) * tc] = l_ref[c * tc:(c + 1) * tc, :].astype(F32).T.astype(l_ref.dtype)
        else:
            o_ref, lt_ref = rest[-1], l_ref
        for m in range(mo // cm):
            rows = pl.ds(m * cm, cm)
            o_ref[rows, :] = jnp.dot(lt_ref[rows, :], r_ref[...], preferred_element_type=F32).astype(out_dtype)

    if rhs.ndim == 3:
        tps_r = rhs.shape[2] // tn
        r_spec = pl.BlockSpec((None, kc, tn), lambda t: ((tile0 + t) // tps_r, 0, (tile0 + t) % tps_r))
    else:
        r_spec = pl.BlockSpec((kc, tn), lambda t: (0, t))
    in_specs = [pl.BlockSpec(lhs.shape, lambda t: (0, 0)), r_spec]
    args = [lhs, rhs]
    aliases = {}
    if out3d is None:
        o_spec = pl.BlockSpec((mo, tn), lambda t: (0, t))
        o_shape = jax.ShapeDtypeStruct((mo, n_tiles * tn), out_dtype)
    else:
        j_out, ns_out = out3d
        tps_o = ns_out // tn
        o_spec = pl.BlockSpec((None, mo, tn), lambda t: ((tile0 + t) // tps_o, 0, (tile0 + t) % tps_o))
        o_shape = jax.ShapeDtypeStruct((j_out, mo, ns_out), out_dtype)
        if prev is not None:
            in_specs.append(pl.BlockSpec(memory_space=pl.ANY))
            args.append(prev)
            aliases = {2: 0}
    return pl.pallas_call(
        body, name=name, grid=(n_tiles,), in_specs=in_specs, out_specs=o_spec, out_shape=o_shape,
        input_output_aliases=aliases,
        scratch_shapes=[pltpu.VMEM((mo, kc), lhs.dtype)] if transpose_lhs else [],
        compiler_params=_params("arbitrary" if transpose_lhs else "parallel"),
    )(*args)


def _in_tiles(h_parts, w3, tile_ids, n_tiles, *, tn, total_tiles, part_of, name, prev=None):
    _, kc, ns = w3.shape
    tps = ns // tn
    cm = 1024
    n_parts = len(h_parts)

    def body(ids_ref, *rest):
        h_refs, w_ref, o_ref = rest[:n_parts], rest[n_parts], rest[-1]
        part = part_of(ids_ref[1, pl.program_id(0)])
        for g, h_ref in enumerate(h_refs):
            @pl.when(part == g)
            def _():
                for m in range(SEQ // cm):
                    rows = pl.ds(m * cm, cm)
                    o_ref[rows, :] = jnp.dot(h_ref[rows, :], w_ref[...], preferred_element_type=F32).astype(BF16)

    resident = pl.BlockSpec((SEQ, kc), lambda t, ids: (0, 0))
    in_specs = [resident] * n_parts + [
        pl.BlockSpec((None, kc, tn), lambda t, ids: (ids[0, t] // tps, 0, ids[0, t] % tps))]
    args = [*h_parts, w3]
    aliases = {}
    if prev is not None:
        in_specs.append(pl.BlockSpec(memory_space=pl.ANY))
        args.append(prev)
        aliases = {n_parts + 2: 0}
    return pl.pallas_call(
        body, name=name,
        grid_spec=pltpu.PrefetchScalarGridSpec(
            num_scalar_prefetch=1, grid=(n_tiles,), in_specs=in_specs,
            out_specs=pl.BlockSpec((SEQ, tn), lambda t, ids: (0, ids[1, t]))),
        out_shape=jax.ShapeDtypeStruct((SEQ, total_tiles * tn), BF16),
        input_output_aliases=aliases, compiler_params=_params("arbitrary"),
    )(tile_ids, *args)


def _own_first(chip, total_tiles):
    own = total_tiles // N_CHIPS
    step = jnp.arange(total_tiles, dtype=jnp.int32)
    tiles = (own * chip + step) % total_tiles
    return jnp.stack([step[:own], tiles[:own]]), jnp.stack([tiles[own:], tiles[own:]]), own


def _mm_nt(dy, w3, *, tn, tile0, n_tiles, name, after=None):
    m_rows = dy.shape[0]
    _, kc, ns = w3.shape
    tps = ns // tn
    cm = 512
    extra = [] if after is None else [after]

    def body(dy_ref, w_ref, *rest):
        o_ref, acc = rest[-2], rest[-1]
        t = pl.program_id(0)

        @pl.when(t == 0)
        def _():
            acc[...] = jnp.zeros_like(acc)

        for m in range(m_rows // cm):
            rows = pl.ds(m * cm, cm)
            acc[rows, :] += lax.dot_general(dy_ref[rows, :], w_ref[...], NT_DIMS, preferred_element_type=F32)

        @pl.when(t == n_tiles - 1)
        def _():
            o_ref[...] = acc[...].astype(BF16)

    return pl.pallas_call(
        body, name=name, grid=(n_tiles,),
        in_specs=[pl.BlockSpec((m_rows, tn), lambda t: (0, t)),
                  pl.BlockSpec((None, kc, tn), lambda t: ((tile0 + t) // tps, 0, (tile0 + t) % tps))]
        + [pl.BlockSpec(memory_space=pl.ANY)] * len(extra),
        out_specs=pl.BlockSpec((m_rows, kc), lambda t: (0, 0)),
        out_shape=jax.ShapeDtypeStruct((m_rows, kc), BF16),
        scratch_shapes=[pltpu.VMEM((m_rows, kc), F32)],
        compiler_params=_params("arbitrary"),
    )(dy, w3, *extra)


CONV_CHUNK = 16


def _shift_copies(buf, shifted):
    rows = shifted.shape[1]
    for s in range(1, 8):
        shifted[s - 1] = buf[pl.ds(s, rows), :]


def _shifted_rows(buf, shifted, offset, r0):
    s = offset % 8
    if s == 0:
        return buf[pl.ds(r0 + offset, CONV_CHUNK), :]
    return shifted[s - 1, pl.ds(r0 + (offset - s), CONV_CHUNK), :]


def _spread_taps(w_ref, taps):
    for k in range(CONV_WIDTH):
        taps[k] = jnp.broadcast_to(w_ref[k:k + 1, :], (8, D_MODEL))


def _times_tap(taps, k, rows):
    return (rows.reshape(CONV_CHUNK // 8, 8, D_MODEL) * taps[k][None]).reshape(CONV_CHUNK, D_MODEL)


def _conv_fwd(proj, conv_w, conv_b, ln_g, ln_b, name):
    tm = ROW_TILE
    hb = tm // HALO

    def body(vg_ref, halo_ref, z_ref, w_ref, b_ref, g_ref, be_ref, u5_ref, u5t_ref, u2_ref, buf, shifted, taps):
        i = pl.program_id(0)
        u1 = vg_ref[:, :D_MODEL].astype(F32) * _sigmoid(vg_ref[:, D_MODEL:].astype(F32))
        u1h = halo_ref[:, :D_MODEL].astype(F32) * _sigmoid(halo_ref[:, D_MODEL:].astype(F32))
        buf[pl.ds(0, HALO), :] = jnp.where(i > 0, u1h, 0.0)
        buf[pl.ds(HALO, tm), :] = u1
        _shift_copies(buf, shifted)
        _spread_taps(w_ref, taps)

        def chunk(ci, carry):
            r0 = pl.multiple_of(ci * CONV_CHUNK, CONV_CHUNK)
            acc = jnp.broadcast_to(b_ref[...], (CONV_CHUNK, D_MODEL))
            for k in range(CONV_WIDTH):
                acc = acc + _times_tap(taps, k, _shifted_rows(buf, shifted, HALO - (CONV_WIDTH - 1) + k, r0))
            u2_ref[pl.ds(r0, CONV_CHUNK), :] = acc
            return carry

        lax.fori_loop(0, tm // CONV_CHUNK, chunk, 0)
        acc = u2_ref[...]
        mu = jnp.mean(acc, axis=-1, keepdims=True)
        xc = acc - mu
        rstd = lax.rsqrt(jnp.mean(xc * xc, axis=-1, keepdims=True) + NORM_EPS)
        u3 = xc * rstd * g_ref[...] + be_ref[...]
        zv = z_ref[...].astype(F32)
        u5 = u3 * _sigmoid(u3) * (zv * _sigmoid(zv))
        u5_ref[...] = u5.astype(BF16)
        u5t_ref[...] = u5.T.astype(BF16)

    return pl.pallas_call(
        body, name=name, grid=(SEQ // tm,),
        in_specs=[pl.BlockSpec((tm, 2 * D_MODEL), lambda i: (i, 0)),
                  pl.BlockSpec((HALO, 2 * D_MODEL), lambda i: (jnp.maximum(i * hb - 1, 0), 0)),
                  _row_spec(tm, D_MODEL, 2),
                  _vec_spec(CONV_WIDTH, D_MODEL)] + [_vec_spec(1, D_MODEL)] * 3,
        out_specs=[_row_spec(tm, D_MODEL), pl.BlockSpec((D_MODEL, tm), lambda i: (0, i)), _row_spec(tm, D_MODEL)],
        out_shape=[jax.ShapeDtypeStruct((SEQ, D_MODEL), BF16), jax.ShapeDtypeStruct((D_MODEL, SEQ), BF16),
                   jax.ShapeDtypeStruct((SEQ, D_MODEL), F32)],
        scratch_shapes=[pltpu.VMEM((HALO + tm, D_MODEL), F32), pltpu.VMEM((7, HALO + tm - 8, D_MODEL), F32),
                        pltpu.VMEM((CONV_WIDTH, 8, D_MODEL), F32)],
        compiler_params=_params("parallel"),
    )(proj, proj, proj, conv_w, conv_b, ln_g, ln_b)


def _conv_bwd_pointwise(dy, w_out, proj, u2, ln_g, ln_b, name):
    tm = ROW_TILE

    def body(dy_ref, w_ref, z_ref, u2_ref, g_ref, be_ref, du2_ref, dz_ref, sums_ref):
        u2v = u2_ref[...]
        mu = jnp.mean(u2v, axis=-1, keepdims=True)
        xc = u2v - mu
        rstd = lax.rsqrt(jnp.mean(xc * xc, axis=-1, keepdims=True) + NORM_EPS)
        xhat = xc * rstd
        u3 = xhat * g_ref[...] + be_ref[...]
        s3 = _sigmoid(u3)
        u4 = u3 * s3
        zv = z_ref[...].astype(F32)
        sz = _sigmoid(zv)
        du5v = lax.dot_general(dy_ref[...], w_ref[...], NT_DIMS, preferred_element_type=F32)
        dz_ref[...] = du5v * u4 * (sz * (1.0 + zv * (1.0 - sz)))
        du3 = du5v * (zv * sz) * (s3 * (1.0 + u3 * (1.0 - s3)))
        dxhat = du3 * g_ref[...]
        du2 = rstd * (dxhat - jnp.mean(dxhat, axis=-1, keepdims=True)
                      - xhat * jnp.mean(dxhat * xhat, axis=-1, keepdims=True))
        du2_ref[...] = du2
        sums = jnp.concatenate([
            jnp.sum(du3 * xhat, axis=0, keepdims=True),
            jnp.sum(du3, axis=0, keepdims=True),
            jnp.sum(du2, axis=0, keepdims=True),
            jnp.zeros((5, D_MODEL), F32)], axis=0)

        @pl.when(pl.program_id(0) == 0)
        def _():
            sums_ref[...] = jnp.zeros_like(sums_ref)

        sums_ref[...] += sums

    return pl.pallas_call(
        body, name=name, grid=(SEQ // tm,),
        in_specs=[_row_spec(tm, D_MODEL), _vec_spec(D_MODEL, D_MODEL), _row_spec(tm, D_MODEL, 2),
                  _row_spec(tm, D_MODEL), _vec_spec(1, D_MODEL), _vec_spec(1, D_MODEL)],
        out_specs=[_row_spec(tm, D_MODEL), _row_spec(tm, D_MODEL), _vec_spec(8, D_MODEL)],
        out_shape=[jax.ShapeDtypeStruct((SEQ, D_MODEL), F32), jax.ShapeDtypeStruct((SEQ, D_MODEL), F32),
                   jax.ShapeDtypeStruct((8, D_MODEL), F32)],
        compiler_params=_params("arbitrary"),
    )(dy, w_out, proj, u2, ln_g, ln_b)


def _conv_bwd_taps(du2, dz, proj, conv_w, name):
    tm = ROW_TILE
    hb = tm // HALO
    n_blocks = SEQ // tm

    def body(du2_ref, dnext_ref, dz_ref, vg_ref, w_ref, dproj_ref, dw_ref, dbuf, dshift, sgbuf, ubuf, dwacc, taps):
        i = pl.program_id(0)
        _spread_taps(w_ref, taps)
        sg = _sigmoid(vg_ref[:, D_MODEL:].astype(F32))
        sgbuf[...] = sg
        ubuf[...] = vg_ref[:, :D_MODEL].astype(F32) * sg
        dbuf[pl.ds(0, tm), :] = du2_ref[...]
        dbuf[pl.ds(tm, HALO), :] = jnp.where(i < n_blocks - 1, dnext_ref[...], 0.0)
        _shift_copies(dbuf, dshift)

        @pl.when(i == 0)
        def _():
            dwacc[...] = jnp.zeros_like(dwacc)

        def chunk(ci, carry):
            r0 = pl.multiple_of(ci * CONV_CHUNK, CONV_CHUNK)
            rows = pl.ds(r0, CONV_CHUNK)
            u1c = ubuf[rows, :]
            du1 = jnp.zeros((CONV_CHUNK, D_MODEL), F32)
            for k in range(CONV_WIDTH):
                ahead = _shifted_rows(dbuf, dshift, CONV_WIDTH - 1 - k, r0)
                du1 = du1 + _times_tap(taps, k, ahead)
                prod = u1c * ahead
                dwacc[k] += prod[0:8] + prod[8:16]
            sgc = sgbuf[rows, :]
            dval = du1 * sgc
            dproj_ref[rows, 0:D_MODEL] = dval.astype(BF16)
            dproj_ref[rows, D_MODEL:2 * D_MODEL] = (
                dval * vg_ref[rows, 0:D_MODEL].astype(F32) * (1.0 - sgc)).astype(BF16)
            return carry

        lax.fori_loop(0, tm // CONV_CHUNK, chunk, 0)
        dproj_ref[:, 2 * D_MODEL:] = dz_ref[...].astype(BF16)

        @pl.when(i == n_blocks - 1)
        def _():
            for k in range(CONV_WIDTH):
                dw_ref[k:k + 1, :] = jnp.sum(dwacc[k], axis=0, keepdims=True)
            dw_ref[CONV_WIDTH:, :] = jnp.zeros((32 - CONV_WIDTH, D_MODEL), F32)

    return pl.pallas_call(
        body, name=name, grid=(n_blocks,),
        in_specs=[_row_spec(tm, D_MODEL),
                  pl.BlockSpec((HALO, D_MODEL), lambda i: (jnp.minimum((i + 1) * hb, SEQ // HALO - 1), 0)),
                  _row_spec(tm, D_MODEL),
                  pl.BlockSpec((tm, 2 * D_MODEL), lambda i: (i, 0)),
                  _vec_spec(CONV_WIDTH, D_MODEL)],
        out_specs=[_row_spec(tm, 3 * D_MODEL), _vec_spec(32, D_MODEL)],
        out_shape=[jax.ShapeDtypeStruct((SEQ, 3 * D_MODEL), BF16), jax.ShapeDtypeStruct((32, D_MODEL), F32)],
        scratch_shapes=[pltpu.VMEM((tm + HALO, D_MODEL), F32), pltpu.VMEM((7, HALO + tm - 8, D_MODEL), F32),
                        pltpu.VMEM((tm, D_MODEL), F32), pltpu.VMEM((tm, D_MODEL), F32),
                        pltpu.VMEM((CONV_WIDTH, 8, D_MODEL), F32), pltpu.VMEM((CONV_WIDTH, 8, D_MODEL), F32)],
        compiler_params=_params("arbitrary"),
    )(du2, du2, dz, proj, conv_w)


def _out_a(u5, w_out, x, gate, g1, scale1, shift1, name):
    tm = ROW_TILE
    n_d = len(DILATIONS)

    def body(u_ref, w_ref, x_ref, gate_ref, g_ref, sc_ref, sh_ref, x1_ref, y_ref, ht_ref, *rest):
        h_refs, nat = rest[:n_d], rest[-1]
        y = jnp.dot(u_ref[...], w_ref[...], preferred_element_type=F32)
        x1 = x_ref[...] + gate_ref[...] * y
        y_ref[...] = y.astype(BF16)
        x1_ref[...] = x1
        h = _normmod(x1, g_ref[...], sc_ref[...], sh_ref[...])
        ht_ref[...] = h.T.astype(BF16)
        for h_ref, d in zip(h_refs, DILATIONS):
            _store_classes(h_ref, h, nat, d)

    res = pl.pallas_call(
        body, name=name, grid=(SEQ // tm,),
        in_specs=[_row_spec(tm, D_MODEL), _vec_spec(D_MODEL, D_MODEL), _row_spec(tm, D_MODEL)]
        + [_vec_spec(1, D_MODEL)] * 4,
        out_specs=[_row_spec(tm, D_MODEL), _row_spec(tm, D_MODEL), pl.BlockSpec((D_MODEL, tm), lambda i: (0, i))]
        + [_class_spec(tm, d) for d in DILATIONS],
        out_shape=[jax.ShapeDtypeStruct((SEQ, D_MODEL), F32), jax.ShapeDtypeStruct((SEQ, D_MODEL), BF16),
                   jax.ShapeDtypeStruct((D_MODEL, SEQ), BF16)] + [_class_shape(d, BF16) for d in DILATIONS],
        scratch_shapes=[_natural_scratch(tm)],
        compiler_params=_params("parallel"),
    )(u5, w_out, x, gate, g1, scale1, shift1)
    return res[0], res[1], res[2], [a.reshape(SEQ, D_MODEL) for a in res[3:]]


def _out_b_loss(u, w_out, x1, gate, target, name):
    tm = ROW_TILE

    def body(u_ref, w_ref, x_ref, gate_ref, t_ref, e_ref, dy_ref, sums_ref):
        y = jnp.dot(u_ref[...], w_ref[...], preferred_element_type=F32)
        diff = x_ref[...] + gate_ref[...] * y - t_ref[...]
        e = diff * (1.0 / D_MODEL)
        e_ref[...] = e
        dy_ref[...] = (e * gate_ref[...]).astype(BF16)
        sums = jnp.concatenate([
            jnp.sum(e * y, axis=0, keepdims=True),
            jnp.sum(diff * diff, axis=0, keepdims=True),
            jnp.zeros((6, D_MODEL), F32)], axis=0)

        @pl.when(pl.program_id(0) == 0)
        def _():
            sums_ref[...] = jnp.zeros_like(sums_ref)

        sums_ref[...] += sums

    return pl.pallas_call(
        body, name=name, grid=(SEQ // tm,),
        in_specs=[_row_spec(tm, D_MODEL), _vec_spec(D_MODEL, D_MODEL), _row_spec(tm, D_MODEL),
                  _vec_spec(1, D_MODEL), _row_spec(tm, D_MODEL)],
        out_specs=[_row_spec(tm, D_MODEL), _row_spec(tm, D_MODEL), _vec_spec(8, D_MODEL)],
        out_shape=[jax.ShapeDtypeStruct((SEQ, D_MODEL), F32), jax.ShapeDtypeStruct((SEQ, D_MODEL), BF16),
                   jax.ShapeDtypeStruct((8, D_MODEL), F32)],
        compiler_params=_params("arbitrary"),
    )(u, w_out, x1, gate, target)


def _seg_matrix():
    r = lax.broadcasted_iota(jnp.int32, (256, 256), 0) // HEAD_DIM
    c = lax.broadcasted_iota(jnp.int32, (256, 256), 1) // HEAD_DIM
    return jnp.where(r == c, 1.0 / HEAD_DIM, 0.0).astype(BF16)


def _segmean(v, seg):
    hi = v.astype(BF16)
    lo = (v - hi.astype(F32)).astype(BF16)
    outs = []
    for c0 in range(0, D_MODEL, 256):
        outs.append(jnp.dot(hi[:, c0:c0 + 256], seg, preferred_element_type=F32)
                    + jnp.dot(lo[:, c0:c0 + 256], seg, preferred_element_type=F32))
    return jnp.concatenate(outs, axis=1)


def _qk_rstd(v, seg):
    return lax.rsqrt(_segmean(v * v, seg) + NORM_EPS)


def _qknorm_fwd(proj, group, qw, kw, seg, name):
    tm = ROW_TILE

    def body(q_in, k_in, qw_ref, kw_ref, seg_ref, q_ref, k_ref):
        segv = seg_ref[...]
        q = q_in[...].astype(F32)
        k = k_in[...].astype(F32)
        q_ref[...] = (q * _qk_rstd(q, segv) * qw_ref[...] * HEAD_DIM ** -0.5).astype(BF16)
        k_ref[...] = (k * _qk_rstd(k, segv) * kw_ref[...]).astype(BF16)

    return pl.pallas_call(
        body, name=name, grid=(SEQ // tm,),
        in_specs=[_row_spec(tm, D_MODEL, 3 * group), _row_spec(tm, D_MODEL, 3 * group + 1),
                  _vec_spec(1, D_MODEL), _vec_spec(1, D_MODEL), _vec_spec(256, 256)],
        out_specs=[_row_spec(tm, D_MODEL)] * 2,
        out_shape=[jax.ShapeDtypeStruct((SEQ, D_MODEL), BF16)] * 2,
        compiler_params=_params("parallel"),
    )(proj, proj, qw, kw, seg)


def _attn_masks(b, bpc, dilation, transposed=False):
    keys = ATTN_BLOCK if bpc == 1 else 2 * ATTN_BLOCK
    shape, q_axis = ((keys, ATTN_BLOCK), 1) if transposed else ((ATTN_BLOCK, keys), 0)
    qi = lax.broadcasted_iota(jnp.int32, shape, q_axis)
    kj = lax.broadcasted_iota(jnp.int32, shape, 1 - q_axis)
    if bpc == 1:
        steps = qi - kj
        return (steps * dilation).astype(F32), steps >= 0
    steps = qi + ATTN_BLOCK - kj
    has_prev = (b % bpc) != 0
    valid = (steps >= 0) & (steps <= ATTN_BLOCK) & (has_prev | (kj >= ATTN_BLOCK))
    return (steps * dilation).astype(F32), valid


MASKED = 1e30


def _bias_scratch(bpc):
    return pltpu.VMEM((1 if bpc == 1 else 2, N_HEADS, ATTN_BLOCK, (1 if bpc == 1 else 2) * ATTN_BLOCK), F32)


def _fill_bias(bias_ref, sl_ref, bpc, dilation):
    for variant in range(bias_ref.shape[0]):
        dist, valid = _attn_masks(variant, min(bpc, 2), dilation)
        bias_ref[variant] = jnp.where(valid[None], dist[None] * sl_ref[...], MASKED)


def _step_bias(bias_ref, b, bpc):
    if bpc == 1:
        return bias_ref[0]
    return bias_ref[jnp.where((b % bpc) != 0, 1, 0)]


def _key_tile(prev_ref, cur_ref, cols, bpc):
    if bpc == 1:
        return cur_ref[:, cols]
    return jnp.concatenate([prev_ref[:, cols], cur_ref[:, cols]], axis=0)


ATTN_HEADS_FWD = 16
ATTN_HEADS_BWD = 16
NT_DIMS = (((1,), (1,)), ((), ()))
BATCH_NT_DIMS = (((2,), (2,)), ((0,), (0,)))
BATCH_NN_DIMS = (((2,), (1,)), ((0,), (0,)))
BATCH_TN_DIMS = (((1,), (1,)), ((0,), (0,)))


def _head_stack(tile_of, heads):
    return jnp.stack([tile_of(slice(h * HEAD_DIM, (h + 1) * HEAD_DIM)) for h in range(heads)], axis=0)


def _attn_specs(heads, segment=0):
    width = heads * HEAD_DIM
    off = segment * (D_MODEL // width)
    last = SEQ // ATTN_BLOCK - 1
    cur = pl.BlockSpec((ATTN_BLOCK, width), lambda hg, b: (jnp.minimum(b, last), hg + off))
    prev = pl.BlockSpec((ATTN_BLOCK, width), lambda hg, b: (jnp.clip(b - 1, 0, last), hg + off))
    return cur, prev


def _attn_fwd(q, k, proj, group, slopes, dilation, name):
    bpc = SEQ // dilation // ATTN_BLOCK
    heads = ATTN_HEADS_FWD
    assert heads == N_HEADS
    cur, prev = _attn_specs(heads)
    v_cur, v_prev = _attn_specs(heads, segment=3 * group + 2)

    def body(sl_ref, q_ref, kp_ref, kc_ref, vp_ref, vc_ref, o_ref, lse_ref, bias_ref):
        b = pl.program_id(1)

        @pl.when(b == 0)
        def _():
            _fill_bias(bias_ref, sl_ref, bpc, dilation)

        q3 = _head_stack(lambda cols: q_ref[:, cols], heads)
        k3 = _head_stack(lambda cols: _key_tile(kp_ref, kc_ref, cols, bpc), heads)
        v3 = _head_stack(lambda cols: _key_tile(vp_ref, vc_ref, cols, bpc), heads)
        s = lax.dot_general(q3, k3, BATCH_NT_DIMS, preferred_element_type=F32)
        s = s - _step_bias(bias_ref, b, bpc)
        m = jnp.max(s, axis=-1, keepdims=True)
        p = jnp.exp(s - m)
        l = jnp.sum(p, axis=-1, keepdims=True)
        o3 = lax.dot_general(p.astype(BF16), v3, BATCH_NN_DIMS, preferred_element_type=F32) / l
        lse3 = m + jnp.log(l)
        for h in range(heads):
            o_ref[:, h * HEAD_DIM:(h + 1) * HEAD_DIM] = o3[h].astype(BF16)
        lse_ref[...] = jnp.concatenate([lse3[h] for h in range(heads)]
                                       + [jnp.zeros((ATTN_BLOCK, LANES - heads), F32)], axis=1)

    return pl.pallas_call(
        body, name=name, grid=(N_HEADS // heads, SEQ // ATTN_BLOCK),
        in_specs=[pl.BlockSpec((heads, 1, 1), lambda hg, b: (hg, 0, 0)), cur, prev, cur, v_prev, v_cur],
        out_specs=[cur, pl.BlockSpec((ATTN_BLOCK, LANES), lambda hg, b: (b, 0))],
        out_shape=[jax.ShapeDtypeStruct((SEQ, D_MODEL), BF16), jax.ShapeDtypeStruct((SEQ, LANES), F32)],
        scratch_shapes=[_bias_scratch(bpc)],
        compiler_params=_params("parallel", "arbitrary"),
    )(slopes.reshape(N_HEADS, 1, 1), q, k, k, proj, proj)


def _class_spec(tm, dilation, width=D_MODEL):
    if dilation == 1:
        return _row_spec(tm, width)
    return pl.BlockSpec((dilation, tm // dilation, width), lambda i: (0, i, 0))


def _class_shape(dilation, dtype, width=D_MODEL):
    if dilation == 1:
        return jax.ShapeDtypeStruct((SEQ, width), dtype)
    return jax.ShapeDtypeStruct((dilation, SEQ // dilation, width), dtype)


def _load_natural(in_ref, nat_ref, dilation):
    if dilation == 1:
        return in_ref[...].astype(F32)
    n = nat_ref.shape[1] // dilation
    tiles = in_ref.shape[-1] // LANES
    for r in range(dilation):
        for j in range(tiles):
            nat_ref.at[j][pl.ds(r, n, stride=dilation), :] = in_ref[r, :, j * LANES:(j + 1) * LANES].astype(F32)
    if tiles == 1:
        return nat_ref[0]
    return jnp.concatenate([nat_ref[j] for j in range(tiles)], axis=1)


def _store_classes(out_ref, value, nat_ref, dilation):
    if dilation == 1:
        out_ref[...] = value.astype(out_ref.dtype)
        return
    n = nat_ref.shape[1] // dilation
    tiles = value.shape[-1] // LANES
    for j in range(tiles):
        nat_ref[j] = value[:, j * LANES:(j + 1) * LANES]
    for r in range(dilation):
        for j in range(tiles):
            out_ref[r, :, j * LANES:(j + 1) * LANES] = (
                nat_ref.at[j][pl.ds(r, n, stride=dilation), :].astype(out_ref.dtype))


def _natural_scratch(tm):
    return pltpu.VMEM((D_MODEL // LANES, tm, LANES), F32)


def _head_selector():
    lane_head = lax.broadcasted_iota(jnp.int32, (D_MODEL, LANES), 0) // HEAD_DIM
    head = lax.broadcasted_iota(jnp.int32, (D_MODEL, LANES), 1)
    return (lane_head == head).astype(BF16)


def _dot_split(v, m01, dims):
    hi = v.astype(BF16)
    lo = (v - hi.astype(F32)).astype(BF16)
    return (lax.dot_general(hi, m01, dims, preferred_element_type=F32)
            + lax.dot_general(lo, m01, dims, preferred_element_type=F32))


def _merge_fwd(o_parts, lse_parts, z, sel, name):
    tm = ROW_TILE
    h_spec = pl.BlockSpec((tm, LANES), lambda i: (i, 0))

    def body(o0, o1, o2, l0, l1, l2, z_ref, sel_ref, u_ref, ut_ref, o_ref, lse_ref, nat):
        ls = [_load_natural(l, nat, d) for l, d in zip((l0, l1, l2), DILATIONS)]
        m = jnp.maximum(jnp.maximum(ls[0], ls[1]), ls[2])
        tot = m + jnp.log(jnp.exp(ls[0] - m) + jnp.exp(ls[1] - m) + jnp.exp(ls[2] - m))
        o = jnp.zeros((tm, D_MODEL), F32)
        for o_in, l, d in zip((o0, o1, o2), ls, DILATIONS):
            weight = _dot_split(jnp.exp(l - tot), sel_ref[...], NT_DIMS)
            o = o + weight * _load_natural(o_in, nat, d)
        zv = z_ref[...].astype(F32)
        u = o * (zv * _sigmoid(zv))
        u_ref[...] = u.astype(BF16)
        ut_ref[...] = u.T.astype(BF16)
        o_ref[...] = o.astype(BF16)
        lse_ref[...] = tot

    return pl.pallas_call(
        body, name=name, grid=(SEQ // tm,),
        in_specs=[_class_spec(tm, d) for d in DILATIONS] + [_class_spec(tm, d, LANES) for d in DILATIONS]
        + [_row_spec(tm, D_MODEL, B_Z_SEGMENT), _vec_spec(D_MODEL, LANES)],
        out_specs=[_row_spec(tm, D_MODEL), pl.BlockSpec((D_MODEL, tm), lambda i: (0, i)),
                   _row_spec(tm, D_MODEL), h_spec],
        out_shape=[jax.ShapeDtypeStruct((SEQ, D_MODEL), BF16), jax.ShapeDtypeStruct((D_MODEL, SEQ), BF16),
                   jax.ShapeDtypeStruct((SEQ, D_MODEL), BF16), jax.ShapeDtypeStruct((SEQ, LANES), F32)],
        scratch_shapes=[_natural_scratch(tm)],
        compiler_params=_params("parallel"),
    )(*o_parts, *lse_parts, z, sel)


def _merge_bwd(dy, w_out, o, lse, z, sel, name):
    tm = ROW_TILE
    n_d = len(DILATIONS)

    def body(dy_ref, w_ref, o_ref, lse_ref, z_ref, sel_ref, dz_ref, *rest):
        do_refs, delta_refs, lse_refs, nat = rest[:n_d], rest[n_d:2 * n_d], rest[2 * n_d:3 * n_d], rest[-1]
        zv = z_ref[...].astype(F32)
        sz = _sigmoid(zv)
        duv = lax.dot_general(dy_ref[...], w_ref[...], NT_DIMS, preferred_element_type=F32)
        ov = o_ref[...].astype(F32)
        do = duv * (zv * sz)
        dz_ref[...] = (duv * ov * (sz * (1.0 + zv * (1.0 - sz)))).astype(BF16)
        delta = _dot_split(do * ov, sel_ref[...], (((1,), (0,)), ((), ())))
        lv = lse_ref[...]
        for i, d in enumerate(DILATIONS):
            _store_classes(do_refs[i], do, nat, d)
            _store_classes(delta_refs[i], delta, nat, d)
            _store_classes(lse_refs[i], lv, nat, d)

    res = pl.pallas_call(
        body, name=name, grid=(SEQ // tm,),
        in_specs=[_row_spec(tm, D_MODEL), _vec_spec(D_MODEL, D_MODEL), _row_spec(tm, D_MODEL), _row_spec(tm, LANES),
                  _row_spec(tm, D_MODEL, B_Z_SEGMENT), _vec_spec(D_MODEL, LANES)],
        out_specs=[_row_spec(tm, D_MODEL)] + [_class_spec(tm, d) for d in DILATIONS]
        + [_class_spec(tm, d, LANES) for d in DILATIONS] * 2,
        out_shape=[jax.ShapeDtypeStruct((SEQ, D_MODEL), BF16)] + [_class_shape(d, BF16) for d in DILATIONS]
        + [_class_shape(d, F32, LANES) for d in DILATIONS] * 2,
        scratch_shapes=[_natural_scratch(tm)],
        compiler_params=_params("parallel"),
    )(dy, w_out, o, lse, z, sel)
    flat = lambda a: a.reshape(SEQ, a.shape[-1])
    return (res[0], [flat(a) for a in res[1:1 + n_d]], [flat(a) for a in res[1 + n_d:1 + 2 * n_d]],
            [flat(a) for a in res[1 + 2 * n_d:]])


def _attn_bwd(q, k, proj, group, do, lse, delta, slopes, dilation, name):
    bpc = SEQ // dilation // ATTN_BLOCK
    heads = ATTN_HEADS_BWD
    n_blocks = SEQ // ATTN_BLOCK
    carry = bpc > 1
    width = heads * HEAD_DIM
    cur, prev = _attn_specs(heads)
    v_cur, v_prev = _attn_specs(heads, segment=3 * group + 2)
    assert heads == N_HEADS
    per_head = pl.BlockSpec((ATTN_BLOCK, LANES), lambda hg, b: (jnp.minimum(b, n_blocks - 1), 0))
    scale = HEAD_DIM ** -0.5

    def body(sl_ref, q_ref, kp_ref, kc_ref, vp_ref, vc_ref, do_ref, lse_ref, dl_ref,
             dq_ref, dk_ref, dv_ref, *scratch):
        b = pl.program_id(1)
        if carry:
            dk_carry, dv_carry = scratch

            @pl.when(b == n_blocks)
            def _():
                dk_ref[...] = dk_carry[...].astype(BF16)
                dv_ref[...] = dv_carry[...].astype(BF16)

            @pl.when(b < n_blocks)
            def _():
                step(sl_ref, q_ref, kp_ref, kc_ref, vp_ref, vc_ref, do_ref, lse_ref, dl_ref,
                     dq_ref, dk_ref, dv_ref, dk_carry, dv_carry, b)
        else:
            step(sl_ref, q_ref, kp_ref, kc_ref, vp_ref, vc_ref, do_ref, lse_ref, dl_ref,
                 dq_ref, dk_ref, dv_ref, None, None, b)

    def step(sl_ref, q_ref, kp_ref, kc_ref, vp_ref, vc_ref, do_ref, lse_ref, dl_ref,
             dq_ref, dk_ref, dv_ref, dk_carry, dv_carry, b):
        if carry:
            @pl.when(b == 0)
            def _():
                dk_carry[...] = jnp.zeros_like(dk_carry)
                dv_carry[...] = jnp.zeros_like(dv_carry)

        q3 = _head_stack(lambda cols: q_ref[:, cols], heads)
        k3 = _head_stack(lambda cols: _key_tile(kp_ref, kc_ref, cols, bpc), heads)
        v3 = _head_stack(lambda cols: _key_tile(vp_ref, vc_ref, cols, bpc), heads)
        do3 = _head_stack(lambda cols: do_ref[:, cols], heads)
        lse_t = lse_ref[...].T
        dl_t = dl_ref[...].T
        lse3 = jnp.stack([lse_t[h:h + 1, :] for h in range(heads)], axis=0)
        dl3 = jnp.stack([dl_t[h:h + 1, :] for h in range(heads)], axis=0)
        s = lax.dot_general(k3, q3, BATCH_NT_DIMS, preferred_element_type=F32)
        dist, valid = _attn_masks(b, bpc, dilation, transposed=True)
        p = jnp.exp(jnp.where(valid[None], s - dist[None] * sl_ref[...], NEG_INF) - lse3)
        dp = lax.dot_general(v3, do3, BATCH_NT_DIMS, preferred_element_type=F32)
        ds = (p * (dp - dl3)).astype(BF16)
        dq3 = lax.dot_general(ds, k3, BATCH_TN_DIMS, preferred_element_type=F32) * scale
        dk3 = lax.dot_general(ds, q3, BATCH_NN_DIMS, preferred_element_type=F32)
        dv3 = lax.dot_general(p.astype(BF16), do3, BATCH_NN_DIMS, preferred_element_type=F32)
        for h in range(heads):
            cols = slice(h * HEAD_DIM, (h + 1) * HEAD_DIM)
            dq_ref[:, cols] = dq3[h].astype(BF16)
            if carry:
                dk_ref[:, cols] = (dk_carry[:, cols] + dk3[h, :ATTN_BLOCK]).astype(BF16)
                dv_ref[:, cols] = (dv_carry[:, cols] + dv3[h, :ATTN_BLOCK]).astype(BF16)
                dk_carry[:, cols] = dk3[h, ATTN_BLOCK:]
                dv_carry[:, cols] = dv3[h, ATTN_BLOCK:]
            else:
                dk_ref[:, cols] = dk3[h].astype(BF16)
                dv_ref[:, cols] = dv3[h].astype(BF16)

    kv_out = prev if carry else cur
    return pl.pallas_call(
        body, name=name, grid=(N_HEADS // heads, n_blocks + (1 if carry else 0)),
        in_specs=[pl.BlockSpec((heads, 1, 1), lambda hg, b: (hg, 0, 0)), cur, prev, cur, v_prev, v_cur,
                  cur, per_head, per_head],
        out_specs=[cur, kv_out, kv_out],
        out_shape=[jax.ShapeDtypeStruct((SEQ, D_MODEL), BF16)] * 3,
        scratch_shapes=[pltpu.VMEM((ATTN_BLOCK, width), F32)] * 2 if carry else [],
        compiler_params=_params("parallel", "arbitrary"),
    )(slopes.reshape(N_HEADS, 1, 1), q, k, k, proj, proj, do, lse, delta)


def _qknorm_bwd(proj, group, qw, kw, seg, dq, dk, dv, name):
    tm = ROW_TILE

    def body(q_in, k_in, qw_ref, kw_ref, seg_ref, dq_ref, dk_ref, dv_ref, dproj_ref, sums_ref):
        segv = seg_ref[...]
        sums = []
        for part, (raw_ref, w_ref, dn_ref) in enumerate(((q_in, qw_ref, dq_ref), (k_in, kw_ref, dk_ref))):
            raw = raw_ref[...].astype(F32)
            dn = dn_ref[...].astype(F32)
            r = _qk_rstd(raw, segv)
            xhat = raw * r
            gq = dn * w_ref[...]
            draw = r * (gq - xhat * _segmean(xhat * gq, segv))
            dproj_ref[:, part * D_MODEL:(part + 1) * D_MODEL] = draw.astype(BF16)
            sums.append(jnp.sum(dn * xhat, axis=0, keepdims=True))
        dproj_ref[:, 2 * D_MODEL:] = dv_ref[...]

        @pl.when(pl.program_id(0) == 0)
        def _():
            sums_ref[...] = jnp.zeros_like(sums_ref)

        sums_ref[...] += jnp.concatenate(sums + [jnp.zeros((6, D_MODEL), F32)], axis=0)

    return pl.pallas_call(
        body, name=name, grid=(SEQ // tm,),
        in_specs=[_row_spec(tm, D_MODEL, 3 * group), _row_spec(tm, D_MODEL, 3 * group + 1),
                  _vec_spec(1, D_MODEL), _vec_spec(1, D_MODEL), _vec_spec(256, 256)] + [_row_spec(tm, D_MODEL)] * 3,
        out_specs=[_row_spec(tm, 3 * D_MODEL), _vec_spec(8, D_MODEL)],
        out_shape=[jax.ShapeDtypeStruct((SEQ, 3 * D_MODEL), BF16), jax.ShapeDtypeStruct((8, D_MODEL), F32)],
        compiler_params=_params("arbitrary"),
    )(proj, proj, qw, kw, seg, dq, dk, dv)


B_TN = 512
B_GROUP_TILES = 3 * D_MODEL // B_TN
B_Z_TILE0 = 3 * B_GROUP_TILES
B_Z_TILES = D_MODEL // B_TN
B_TILES = B_Z_TILE0 + B_Z_TILES
B_Z_SEGMENT = 3 * len(DILATIONS)


def _local_step(x, target, mods, norm_g, conv_w, conv_b, ln_g, ln_b, q_norm, k_norm, chip, own_wa_in, own_wb_in,
                weights_a, weights_b, forward_weights_b, send_grads_b, forward_grads_b, send_grads_a):
    row = lambda a, i: a[i:i + 1]
    shift0, scale0, gate0 = row(mods[0], 0), row(mods[0], 1), row(mods[0], 2)
    shift1, scale1, gate1 = row(mods[1], 0), row(mods[1], 1), row(mods[1], 2)
    g0, g1 = row(norm_g, 0), row(norm_g, 1)
    seg = _seg_matrix()
    slopes = jnp.exp2(-8.0 * jnp.arange(1, N_HEADS + 1, dtype=F32) / N_HEADS)
    qw = [jnp.tile(q_norm[g:g + 1], (1, N_HEADS)) for g in range(3)]
    kw = [jnp.tile(k_norm[g:g + 1], (1, N_HEADS)) for g in range(3)]

    h0, h0t = _normmod_fwd(x, g0, scale0, shift0, "prenorm0")
    nsa = own_wa_in.shape[2]
    tiles_a = dict(tn=nsa, total_tiles=N_CHIPS, part_of=lambda tile: 0)
    own_ids, rest_ids, own_tiles = _own_first(chip, N_CHIPS)
    proj_a = _in_tiles([h0], own_wa_in, own_ids, own_tiles, name="a_in_own", **tiles_a)
    wa_in, wa_out = weights_a(proj_a)
    ja = wa_in.shape[0]
    proj_a = _in_tiles([h0], wa_in, rest_ids, N_CHIPS - own_tiles, name="a_in_rest", prev=proj_a, **tiles_a)
    u5, u5t, u2 = _conv_fwd(proj_a, conv_w, conv_b, ln_g, ln_b, "a_conv")
    x1, y_a, h1t, h1c = _out_a(u5, wa_out, x, gate0, g1, scale1, shift1, "a_out")

    tiles_b = dict(tn=B_TN, total_tiles=B_TILES,
                   part_of=lambda tile: jnp.where(tile >= B_Z_TILE0, 0, tile // B_GROUP_TILES))
    own_ids, rest_ids, own_tiles = _own_first(chip, B_TILES)
    proj_b = _in_tiles(h1c, own_wb_in, own_ids, own_tiles, name="b_in_own", **tiles_b)
    forward_weights_b(proj_b)
    wb_in, wb_out = weights_b(proj_b)
    jb, _, nsb = wb_in.shape
    proj_b = _in_tiles(h1c, wb_in, rest_ids, B_TILES - own_tiles, name="b_in_rest", prev=proj_b, **tiles_b)
    h1 = h1c[0]
    qkv, o_parts, lse_parts = [], [], []
    for g, d in enumerate(DILATIONS):
        qn, kn = _qknorm_fwd(proj_b, g, qw[g], kw[g], seg, f"b_qknorm_g{g}")
        og, lg = _attn_fwd(qn, kn, proj_b, g, slopes, d, f"b_attn_g{g}")
        qkv.append((qn, kn))
        o_parts.append(og if d == 1 else og.reshape(d, SEQ // d, D_MODEL))
        lse_parts.append(lg if d == 1 else lg.reshape(d, SEQ // d, LANES))
    sel = _head_selector()
    u_b, u_bt, o_b, lse_b = _merge_fwd(o_parts, lse_parts, proj_b, sel, "b_merge")
    e, dy_b, sums_loss = _out_b_loss(u_b, wb_out, x1, gate1, target, "b_out_loss")

    dwb_out = _mm(u_bt, dy_b, tn=D_MODEL, tile0=0, n_tiles=1, out_dtype=BF16, name="b_dwout")
    dz_b, do_c, delta_c, lse_c = _merge_bwd(dy_b, wb_out, o_b, lse_b, proj_b, sel, "b_merge_bwd")
    dwb_in = _mm(h1t, dz_b, tn=B_TN, tile0=B_Z_TILE0, n_tiles=B_Z_TILES, out_dtype=BF16, name="b_dwin_z",
                 out3d=(jb, nsb))
    dh1_parts = [_mm_nt(dz_b, wb_in, tn=B_TN, tile0=B_Z_TILE0, n_tiles=B_Z_TILES, name="b_dh_z")]
    qk_sums = []
    for g, d in enumerate(DILATIONS):
        qn, kn = qkv[g]
        dq, dk, dv = _attn_bwd(qn, kn, proj_b, g, do_c[g], lse_c[g], delta_c[g], slopes, d, f"b_attn_bwd_g{g}")
        dproj, sums_qk = _qknorm_bwd(proj_b, g, qw[g], kw[g], seg, dq, dk, dv, f"b_qknorm_bwd_g{g}")
        qk_sums.append(sums_qk)
        dwb_in = _mm(h1t if d == 1 else h1c[g], dproj, tn=B_TN, tile0=g * B_GROUP_TILES, n_tiles=B_GROUP_TILES,
                     out_dtype=BF16, name=f"b_dwin_g{g}", out3d=(jb, nsb), prev=dwb_in, transpose_lhs=d != 1)
        dh = _mm_nt(dproj, wb_in, tn=B_TN, tile0=g * B_GROUP_TILES, n_tiles=B_GROUP_TILES, name=f"b_dh_g{g}")
        dh1_parts.append(dh)
    token = send_grads_b(dwb_in, dwb_out)
    dx1, sums_n1, dy_a = _normmod_bwd(x1, g1, scale1 + token[0:1, 0:1], dh1_parts, e, "prenorm1_bwd",
                                      part_dilations=(1,) + DILATIONS, gated=(gate0, y_a))
    token = forward_grads_b(dx1)

    dwa_out = _mm(u5t, dy_a, tn=D_MODEL, tile0=0, n_tiles=1, out_dtype=BF16, name="a_dwout")
    du2, dz_a, sums_ln = _conv_bwd_pointwise(dy_a, wa_out, proj_a, u2, ln_g + token[0:1, 0:1], ln_b,
                                             "a_conv_bwd_pw")
    dproj_a, dconv_w = _conv_bwd_taps(du2, dz_a, proj_a, conv_w, "a_conv_bwd_taps")
    dwa_in = _mm(h0t, dproj_a, tn=nsa, tile0=0, n_tiles=ja, out_dtype=BF16, name="a_dwin", out3d=(ja, nsa))
    token = send_grads_a(dwa_in, dwa_out)
    dh0 = _mm_nt(dproj_a, wa_in, tn=nsa, tile0=0, n_tiles=ja, name="a_dh", after=token)
    grad_x, sums_n0 = _normmod_bwd(x, g0, scale0, [dh0], dx1, "prenorm0_bwd")

    small = dict(
        dnorm_g=jnp.concatenate([sums_n0[0:1], sums_n1[0:1]], axis=0),
        dmod0=jnp.concatenate([sums_n0[2:3], sums_n0[1:2], sums_n1[3:4]], axis=0),
        dmod1=jnp.concatenate([sums_n1[2:3], sums_n1[1:2], sums_loss[0:1]], axis=0),
        dln_g=sums_ln[0:1], dln_b=sums_ln[1:2], dconv_b=sums_ln[2:3],
        dconv_w=dconv_w[:CONV_WIDTH],
        dq_norm=jnp.concatenate([s[0:1] for s in qk_sums], axis=0),
        dk_norm=jnp.concatenate([s[1:2] for s in qk_sums], axis=0),
        loss_cols=sums_loss[1:2],
    )
    return grad_x, small


def _adamw(w, g, m, v, name, after=None, copy_grad=False):
    rows, cols = w.shape
    tr = rows if rows <= 128 else (256 if cols <= D_MODEL else 128)
    c1 = 1.0 / (1.0 - ADAM_B1 ** ADAM_STEP)
    c2 = 1.0 / (1.0 - ADAM_B2 ** ADAM_STEP)
    extra = [] if after is None else [after]
    n_out = 4 if copy_grad else 3

    def body(w_ref, g_ref, m_ref, v_ref, *rest):
        d_ref, mo_ref, vo_ref = rest[len(extra):len(extra) + 3]
        gv = g_ref[...]
        if copy_grad:
            rest[-1][...] = gv
        mn = ADAM_B1 * m_ref[...] + (1.0 - ADAM_B1) * gv
        vn = ADAM_B2 * v_ref[...] + (1.0 - ADAM_B2) * (gv * gv)
        mo_ref[...] = mn
        vo_ref[...] = vn
        d_ref[...] = -ADAM_LR * ((mn * c1) / (jnp.sqrt(vn * c2) + ADAM_EPS) + ADAM_WD * w_ref[...])

    spec = pl.BlockSpec((tr, cols), lambda i: (i, 0))
    return pl.pallas_call(
        body, name=name, grid=(rows // tr,),
        in_specs=[spec] * 4 + [pl.BlockSpec(memory_space=pl.ANY)] * len(extra), out_specs=[spec] * n_out,
        out_shape=[jax.ShapeDtypeStruct((rows, cols), F32)] * n_out,
        compiler_params=_params("parallel"),
    )(w, g, m, v, *extra)


def _cast_into_slot(w, chip_idx, name, keep_own=False, after=None):
    rows, cols = w.shape
    tr = 256
    extra = [] if after is None else [after]

    def body(ch_ref, w_ref, *rest):
        wb = w_ref[...].astype(BF16)
        for o_ref in rest[len(extra):]:
            o_ref[...] = wb

    slot_spec = pl.BlockSpec((None, tr, cols), lambda i, ch: (ch[0], i, 0))
    own_spec = pl.BlockSpec((None, tr, cols), lambda i, ch: (0, i, 0))
    res = pl.pallas_call(
        body, name=name,
        grid_spec=pltpu.PrefetchScalarGridSpec(
            num_scalar_prefetch=1, grid=(rows // tr,),
            in_specs=[pl.BlockSpec((tr, cols), lambda i, ch: (i, 0))] + [pl.BlockSpec(memory_space=pl.ANY)] * len(extra),
            out_specs=[slot_spec, own_spec] if keep_own else [slot_spec]),
        out_shape=[jax.ShapeDtypeStruct((N_CHIPS, rows, cols), BF16)]
        + ([jax.ShapeDtypeStruct((1, rows, cols), BF16)] if keep_own else []),
        compiler_params=_params("parallel"),
    )(chip_idx, w, *extra)
    return tuple(res) if keep_own else res[0]


def _position():
    x, y, c = lax.axis_index("x"), lax.axis_index("y"), lax.axis_index("c")
    return x, y, c


def _xor_peer(x, y, c, k):
    return (x ^ ((k >> 2) & 1), y ^ ((k >> 1) & 1), c ^ (k & 1))


def _chip_peer(x, y, k):
    return (x ^ ((k >> 1) & 1), y ^ (k & 1))


def _ada_forward(c_row, ada_w, ada_b, conv_w, after=()):
    ns = ada_w.shape[2]
    cw = conv_w.shape[1]

    def body(c_ref, w_ref, b_ref, cv_ref, *rest):
        (mod_ref, sc_ref, cvo_ref, c_all, mp, parts, cv_parts,
         send1, recv1, send2, recv2, send3, recv3, w_vmem, w_sem) = rest[len(after):]
        w_load = pltpu.make_async_copy(w_ref, w_vmem, w_sem)
        w_load.start()
        _handshake(ENTRY_HANDSHAKES["ada_forward"][1])
        x, y, c = _position()
        me = 4 * x + 2 * y + c
        chip = 2 * x + y

        def c_copy(k):
            return pltpu.make_async_remote_copy(
                src_ref=c_all.at[me], dst_ref=c_all.at[me], send_sem=send1.at[k - 1], recv_sem=recv1.at[k - 1],
                device_id=_xor_peer(x, y, c, k), device_id_type=MESH)

        def cv_copy(k):
            px, py = _chip_peer(x, y, k)
            return pltpu.make_async_remote_copy(
                src_ref=cv_parts.at[chip], dst_ref=cv_parts.at[chip], send_sem=send3.at[k - 1],
                recv_sem=recv3.at[k - 1], device_id=(px, py, c), device_id_type=MESH)

        c_all[me] = c_ref[...]
        cv_parts[chip] = cv_ref[...]
        for k in range(1, N_DEV):
            c_copy(k).start()
        for k in range(1, N_CHIPS):
            cv_copy(k).start()
        for k in range(1, N_DEV):
            c_copy(k).wait_recv()
        cv = jnp.concatenate([c_all[i] for i in range(N_DEV)], axis=0)
        sc = cv * _sigmoid(cv)
        sc_ref[...] = sc
        w_load.wait()
        for l in range(2):
            res = jnp.dot(sc.astype(BF16), w_vmem[l].astype(BF16), preferred_element_type=F32)
            for i in range(N_DEV):
                mp[i, l:l + 1, :] = res[i:i + 1, :]

        def mod_copy(k):
            px, py = _chip_peer(x, y, k)
            return pltpu.make_async_remote_copy(
                src_ref=mp.at[4 * px + 2 * py + c], dst_ref=parts.at[chip], send_sem=send2.at[k - 1],
                recv_sem=recv2.at[k - 1], device_id=(px, py, c), device_id_type=MESH)

        for k in range(1, N_CHIPS):
            mod_copy(k).start()
        parts[chip] = mp[me]
        for k in range(1, N_CHIPS):
            mod_copy(k).wait_recv()
            cv_copy(k).wait_recv()
        mod_ref[...] = jnp.concatenate([parts[j] for j in range(N_CHIPS)], axis=1) + b_ref[...]
        cvo_ref[...] = jnp.concatenate([cv_parts[j] for j in range(N_CHIPS)], axis=1)
        for k in range(1, N_DEV):
            c_copy(k).wait_send()
        for k in range(1, N_CHIPS):
            mod_copy(k).wait_send()
            cv_copy(k).wait_send()

    vm = pl.BlockSpec(memory_space=pltpu.VMEM)
    return pl.pallas_call(
        body, name="ada_forward",
        in_specs=[vm, ANY_SPEC, vm, vm] + [ANY_SPEC] * len(after), out_specs=[vm] * 3,
        out_shape=[jax.ShapeDtypeStruct((2, 3 * D_MODEL), F32), jax.ShapeDtypeStruct((N_DEV, D_MODEL), F32),
                   jax.ShapeDtypeStruct((CONV_WIDTH, N_CHIPS * cw), F32)],
        scratch_shapes=[pltpu.VMEM((N_DEV, 1, D_MODEL), F32), pltpu.VMEM((N_DEV, 2, ns), F32),
                        pltpu.VMEM((N_CHIPS, 2, ns), F32), pltpu.VMEM((N_CHIPS, CONV_WIDTH, cw), F32),
                        pltpu.SemaphoreType.DMA((N_DEV - 1,)), pltpu.SemaphoreType.DMA((N_DEV - 1,)),
                        pltpu.SemaphoreType.DMA((N_CHIPS - 1,)), pltpu.SemaphoreType.DMA((N_CHIPS - 1,)),
                        pltpu.SemaphoreType.DMA((N_CHIPS - 1,)), pltpu.SemaphoreType.DMA((N_CHIPS - 1,)),
                        pltpu.VMEM(ada_w.shape, F32), pltpu.SemaphoreType.DMA(())],
        compiler_params=pltpu.CompilerParams(vmem_limit_bytes=VMEM_LIMIT_BYTES,
                                             collective_id=ENTRY_HANDSHAKES["ada_forward"][0]),
    )(c_row, ada_w, ada_b, conv_w, *after)


HBM_SPEC = pl.BlockSpec(memory_space=pltpu.HBM)
ANY_SPEC = pl.BlockSpec(memory_space=pl.ANY)
SEM_SPEC = pl.BlockSpec(memory_space=pltpu.SEMAPHORE)
SPLIT_PARAMS = dict(compiler_params=pltpu.CompilerParams(has_side_effects=pltpu.SideEffectType.DATAFLOW_SIDE_EFFECTING))
TOKEN = jax.ShapeDtypeStruct((8, 128), F32)
ENTRY_HANDSHAKES = {name: (i, peers) for i, (name, peers) in enumerate((
    ("gather_start_a", "chips"), ("gather_start_b", "chips"),
    ("gather_forward_a", "sibling"), ("gather_forward_b", "sibling"),
    ("reduce_d2d_start_b", "sibling"), ("reduce_d2d_start_a", "sibling"),
    ("reduce_ici_start_b", "chips"), ("reduce_ici_start_a", "chips"),
    ("reduce_share_start_b", "sibling"), ("reduce_share_start_a", "sibling"),
    ("small_gather_start", "devices"), ("ada_forward", "devices")))}


def _hbm(arrays):
    return [pltpu.with_memory_space_constraint(a, pltpu.HBM) for a in arrays]


def _hbm_like(arrays):
    return [pltpu.HBM(a.shape, a.dtype) for a in arrays]


def _gather_start(lands, after, name):
    n = len(lands)

    def body(*refs):
        _handshake(ENTRY_HANDSHAKES[name][1])
        ins = refs[:n]
        send, recv = refs[n + 1], refs[n + 2]
        x, y, c = _position()
        chip = 2 * x + y
        for t in range(n):
            rh = ins[t].shape[1] // 2
            for k in range(1, N_CHIPS):
                px, py = _chip_peer(x, y, k)
                block = ins[t].at[chip, pl.ds(c * rh, rh)]
                pltpu.make_async_remote_copy(
                    src_ref=block, dst_ref=block, send_sem=send.at[3 * t + k - 1], recv_sem=recv.at[3 * t + k - 1],
                    device_id=(px, py, c), device_id_type=MESH).start()
        refs[-1][...] = jnp.zeros(TOKEN.shape, F32)

    res = pl.pallas_call(
        body, name=name, in_specs=[HBM_SPEC] * n + [ANY_SPEC],
        out_specs=(SEM_SPEC, SEM_SPEC, *[HBM_SPEC] * n, pl.BlockSpec(memory_space=pltpu.VMEM)),
        out_shape=(pltpu.SemaphoreType.DMA((3 * n,)), pltpu.SemaphoreType.DMA((3 * n,)), *_hbm_like(lands), TOKEN),
        input_output_aliases={t: 2 + t for t in range(n)}, **_split_params(name),
    )(*_hbm(lands), after)
    return res[0], res[1], list(res[2:2 + n]), res[-1]


def _gather_forward(send, recv, lands, after, name):
    n = len(lands)

    def body(*refs):
        _handshake(ENTRY_HANDSHAKES[name][1])
        ins = refs[:n]
        send1, recv1 = refs[n], refs[n + 1]
        send2, recv2 = refs[n + 3], refs[n + 4]
        x, y, c = _position()
        chip = 2 * x + y
        for t in range(n):
            rh = ins[t].shape[1] // 2
            half = pl.ds(c * rh, rh)
            for k in range(1, N_CHIPS):
                px, py = _chip_peer(x, y, k)
                s = 3 * t + k - 1
                got = ins[t].at[2 * px + py, half]
                cp = pltpu.make_async_remote_copy(
                    src_ref=ins[t].at[chip, half], dst_ref=got, send_sem=send1.at[s], recv_sem=recv1.at[s],
                    device_id=(px, py, c), device_id_type=MESH)
                cp.wait_send()
                cp.wait_recv()
                pltpu.make_async_remote_copy(
                    src_ref=got, dst_ref=got, send_sem=send2.at[s], recv_sem=recv2.at[s],
                    device_id=(x, y, 1 - c), device_id_type=MESH).start()
        refs[-1][...] = jnp.zeros(TOKEN.shape, F32)

    res = pl.pallas_call(
        body, name=name, in_specs=[HBM_SPEC] * n + [SEM_SPEC, SEM_SPEC, ANY_SPEC],
        out_specs=(SEM_SPEC, SEM_SPEC, *[HBM_SPEC] * n, pl.BlockSpec(memory_space=pltpu.VMEM)),
        out_shape=(pltpu.SemaphoreType.DMA((3 * n,)), pltpu.SemaphoreType.DMA((3 * n,)), *_hbm_like(lands), TOKEN),
        input_output_aliases={t: 2 + t for t in range(n)}, **_split_params(name),
    )(*lands, send, recv, after)
    return res[0], res[1], list(res[2:2 + n]), res[-1]


def _gather_wait(send, recv, lands, after, name):
    n = len(lands)

    def body(*refs):
        ins = refs[:n]
        send_ref, recv_ref = refs[n], refs[n + 1]
        x, y, c = _position()
        for t in range(n):
            rh = ins[t].shape[1] // 2
            for k in range(1, N_CHIPS):
                px, py = _chip_peer(x, y, k)
                cp = pltpu.make_async_remote_copy(
                    src_ref=ins[t].at[2 * px + py, pl.ds(c * rh, rh)],
                    dst_ref=ins[t].at[2 * px + py, pl.ds((1 - c) * rh, rh)], send_sem=send_ref.at[3 * t + k - 1],
                    recv_sem=recv_ref.at[3 * t + k - 1], device_id=(x, y, 1 - c), device_id_type=MESH)
                cp.wait_send()
                cp.wait_recv()

    res = pl.pallas_call(
        body, name=name, in_specs=[HBM_SPEC] * n + [SEM_SPEC, SEM_SPEC, ANY_SPEC], out_specs=[HBM_SPEC] * n,
        out_shape=_hbm_like(lands), input_output_aliases={t: t for t in range(n)}, **SPLIT_PARAMS,
    )(*lands, send, recv, after)
    return list(res)


def _handshake(peers):
    x, y, c = _position()
    if peers == "sibling":
        ids = [(x, y, 1 - c)]
    elif peers == "chips":
        ids = [(*_chip_peer(x, y, k), c) for k in range(1, N_CHIPS)]
    else:
        ids = [_xor_peer(x, y, c, k) for k in range(1, N_DEV)]
    barrier = pltpu.get_barrier_semaphore()
    for peer in ids:
        pl.semaphore_signal(barrier, inc=1, device_id=peer, device_id_type=MESH)
    pl.semaphore_wait(barrier, len(ids))


def _split_params(name):
    return dict(compiler_params=pltpu.CompilerParams(
        has_side_effects=pltpu.SideEffectType.DATAFLOW_SIDE_EFFECTING, collective_id=ENTRY_HANDSHAKES[name][0]))


def _split_start(name, arrays, n_sems, after, issue):
    m = len(arrays)

    def body(*refs):
        _handshake(ENTRY_HANDSHAKES[name][1])
        issue(refs[:m], refs[m + 1], refs[m + 2])
        refs[-1][...] = jnp.zeros(TOKEN.shape, F32)

    res = pl.pallas_call(
        body, name=name, in_specs=[HBM_SPEC] * m + [ANY_SPEC],
        out_specs=(SEM_SPEC, SEM_SPEC, *[HBM_SPEC] * m, pl.BlockSpec(memory_space=pltpu.VMEM)),
        out_shape=(pltpu.SemaphoreType.DMA((n_sems,)), pltpu.SemaphoreType.DMA((n_sems,)), *_hbm_like(arrays), TOKEN),
        input_output_aliases={t: 2 + t for t in range(m)}, **_split_params(name),
    )(*_hbm(arrays), after)
    return res[0], res[1], list(res[2:2 + m]), res[-1]


def _split_wait(name, arrays, send, recv, after, await_all):
    m = len(arrays)

    def body(*refs):
        await_all(refs[:m], refs[m], refs[m + 1])

    res = pl.pallas_call(
        body, name=name, in_specs=[HBM_SPEC] * m + [SEM_SPEC, SEM_SPEC, ANY_SPEC], out_specs=[HBM_SPEC] * m,
        out_shape=_hbm_like(arrays), input_output_aliases={t: t for t in range(m)}, **SPLIT_PARAMS,
    )(*arrays, send, recv, after)
    return list(res)


def _sibling_copies(refs, send, recv, n):
    x, y, c = _position()
    cps = []
    for t in range(n):
        rh = refs[t].shape[1] // 2
        cps.append(pltpu.make_async_remote_copy(
            src_ref=refs[t].at[pl.ds(0, N_CHIPS), pl.ds((1 - c) * rh, rh)], dst_ref=refs[n + t],
            send_sem=send.at[t], recv_sem=recv.at[t], device_id=(x, y, 1 - c), device_id_type=MESH))
    return cps


def _reduce_sibling_start(grads, after, name):
    n = len(grads)
    lands = [lax.empty((N_CHIPS, g.shape[1] // 2, g.shape[2]), BF16) for g in grads]

    def issue(refs, send, recv):
        for cp in _sibling_copies(refs, send, recv, n):
            cp.start()

    return _split_start(name, list(grads) + lands, n, after, issue)


def _reduce_sibling_wait(send, recv, arrays, after, name):
    n = len(arrays) // 2

    def await_all(refs, send_ref, recv_ref):
        for cp in _sibling_copies(refs, send_ref, recv_ref, n):
            cp.wait_send()
            cp.wait_recv()

    res = _split_wait(name, arrays, send, recv, after, await_all)
    return res[:n], res[n:]


def _add_sibling_half(grad, got, dev_idx, name):
    j, r, cols = grad.shape
    rh = r // 2
    tr = rh
    nb = rh // tr

    def body(idx_ref, g_ref, got_ref, out_ref):
        out_ref[...] = (g_ref[...].astype(F32) + got_ref[...].astype(F32)).astype(BF16)

    return pl.pallas_call(
        body, name=name,
        grid_spec=pltpu.PrefetchScalarGridSpec(
            num_scalar_prefetch=1, grid=(j, nb),
            in_specs=[pl.BlockSpec((None, tr, cols), lambda jj, i, idx: (jj, idx[2] * nb + i, 0)),
                      pl.BlockSpec((None, tr, cols), lambda jj, i, idx: (jj, i, 0))],
            out_specs=pl.BlockSpec((None, tr, cols), lambda jj, i, idx: (jj, i, 0))),
        out_shape=jax.ShapeDtypeStruct((j, rh, cols), BF16),
        compiler_params=_params("parallel", "parallel"),
    )(dev_idx, grad, got)


def _chip_copies(refs, send, recv, n, receiving):
    x, y, c = _position()
    chip = 2 * x + y
    cps = []
    for t in range(n):
        for k in range(1, N_CHIPS):
            px, py = _chip_peer(x, y, k)
            cps.append(pltpu.make_async_remote_copy(
                src_ref=refs[t].at[2 * px + py], dst_ref=refs[n + t].at[2 * px + py if receiving else chip],
                send_sem=send.at[3 * t + k - 1], recv_sem=recv.at[3 * t + k - 1],
                device_id=(px, py, c), device_id_type=MESH))
    return cps


def _reduce_chips_start(partials, after, name):
    n = len(partials)
    lands = [lax.empty(p.shape, BF16) for p in partials]

    def issue(refs, send, recv):
        for cp in _chip_copies(refs, send, recv, n, False):
            cp.start()

    return _split_start(name, list(partials) + lands, 3 * n, after, issue)


def _reduce_chips_wait(send, recv, arrays, after, name):
    n = len(arrays) // 2

    def await_all(refs, send_ref, recv_ref):
        for cp in _chip_copies(refs, send_ref, recv_ref, n, True):
            cp.wait_send()
            cp.wait_recv()

    res = _split_wait(name, arrays, send, recv, after, await_all)
    return res[:n], res[n:]


def _sum_partials(land, partial, dev_idx, name):
    _, rh, cols = land.shape
    tr = min(rh, 256)
    nb = rh // tr

    def body(idx_ref, l_ref, p_ref, o_ref):
        chip = idx_ref[1]
        acc = jnp.where(chip == 0, p_ref[...], l_ref[0]).astype(F32)
        for s in range(1, N_CHIPS):
            acc = acc + jnp.where(chip == s, p_ref[...], l_ref[s]).astype(F32)
        o_ref[...] = acc

    return pl.pallas_call(
        body, name=name,
        grid_spec=pltpu.PrefetchScalarGridSpec(
            num_scalar_prefetch=1, grid=(nb,),
            in_specs=[pl.BlockSpec((N_CHIPS, tr, cols), lambda i, idx: (0, i, 0)),
                      pl.BlockSpec((None, tr, cols), lambda i, idx: (idx[1], i, 0))],
            out_specs=pl.BlockSpec((tr, cols), lambda i, idx: (idx[2] * nb + i, 0))),
        out_shape=jax.ShapeDtypeStruct((2 * rh, cols), F32), compiler_params=_params("parallel"),
    )(dev_idx, land, partial)


def _half_copies(refs, send, recv, receiving):
    x, y, c = _position()
    cps = []
    for t, ref in enumerate(refs):
        rh = ref.shape[0] // 2
        cps.append(pltpu.make_async_remote_copy(
            src_ref=ref.at[pl.ds(c * rh, rh)], dst_ref=ref.at[pl.ds(((1 - c) if receiving else c) * rh, rh)],
            send_sem=send.at[t], recv_sem=recv.at[t], device_id=(x, y, 1 - c), device_id_type=MESH))
    return cps


def _share_halves_start(totals, after, name):
    def issue(refs, send, recv):
        for cp in _half_copies(refs, send, recv, False):
            cp.start()

    return _split_start(name, list(totals), len(totals), after, issue)


def _share_halves_wait(send, recv, totals, after, name):
    def await_all(refs, send_ref, recv_ref):
        for cp in _half_copies(refs, send_ref, recv_ref, True):
            cp.wait_send()
            cp.wait_recv()

    return _split_wait(name, totals, send, recv, after, await_all)


SMALL_ROWS = 56


def _small_copies(refs, send, recv, receiving):
    x, y, c = _position()
    me = 4 * x + 2 * y + c
    cps = []
    for k in range(1, N_DEV):
        px, py, pc = _xor_peer(x, y, c, k)
        cps.append(pltpu.make_async_remote_copy(
            src_ref=refs[0], dst_ref=refs[1].at[4 * px + 2 * py + pc if receiving else me],
            send_sem=send.at[k - 1], recv_sem=recv.at[k - 1], device_id=(px, py, pc), device_id_type=MESH))
    return cps


def _small_gather_start(packed, after):
    land = lax.empty((N_DEV,) + packed.shape, F32)

    def issue(refs, send, recv):
        for cp in _small_copies(refs, send, recv, False):
            cp.start()

    return _split_start("small_gather_start", [packed, land], N_DEV - 1, after, issue)


def _small_gather_wait(send, recv, arrays, after):
    def await_all(refs, send_ref, recv_ref):
        for cp in _small_copies(refs, send_ref, recv_ref, True):
            cp.wait_send()
            cp.wait_recv()

    return _split_wait("small_gather_wait", arrays, send, recv, after, await_all)


def _reduce_small(packed, land, silu_c):
    ns = 3 * D_MODEL // N_CHIPS

    def body(p_ref, land_ref, sc_ref, tot_ref, gw_ref, loss_ref, qk_ref, allp):
        x, y, c = _position()
        me = 4 * x + 2 * y + c
        chip = 2 * x + y
        for i in range(N_DEV):
            allp[i] = jnp.where(me == i, p_ref[...], land_ref[i])
        tot = allp[0]
        for i in range(1, N_DEV):
            tot = tot + allp[i]
        tot_ref[...] = tot
        loss_ref[...] = jnp.sum(tot[11:12, :], axis=1, keepdims=True) * (0.5 / D_MODEL)
        fold = tot[5:11, 0:HEAD_DIM]
        for h in range(1, N_HEADS):
            fold = fold + tot[5:11, h * HEAD_DIM:(h + 1) * HEAD_DIM]
        qk_ref[...] = jnp.concatenate([fold, jnp.zeros((2, HEAD_DIM), F32)], axis=0)
        pad = jnp.zeros((LANES - N_DEV, D_MODEL), F32)
        sct = jnp.concatenate([sc_ref[...], pad], axis=0).T.astype(BF16)
        for l in range(2):
            dms = [allp[i, pl.ds(12 + 4 * l + chip, 1), :] for i in range(N_DEV)]
            dm = jnp.concatenate(dms + [pad], axis=0)[:, :ns].astype(BF16)
            gw_ref[l] = jnp.dot(sct, dm, preferred_element_type=F32)

    vm = pl.BlockSpec(memory_space=pltpu.VMEM)
    return pl.pallas_call(
        body, name="reduce_small", in_specs=[vm, vm, vm], out_specs=[vm] * 4,
        out_shape=[jax.ShapeDtypeStruct((SMALL_ROWS, D_MODEL), F32), jax.ShapeDtypeStruct((2, D_MODEL, ns), F32),
                   jax.ShapeDtypeStruct((1, 1), F32), jax.ShapeDtypeStruct((8, HEAD_DIM), F32)],
        scratch_shapes=[pltpu.VMEM((N_DEV, SMALL_ROWS, D_MODEL), F32)],
        compiler_params=pltpu.CompilerParams(vmem_limit_bytes=VMEM_LIMIT_BYTES),
    )(packed, land, silu_c)


def kernel(x, c, norm_g, ada_w, ada_b, a_w_in, a_conv_w, a_conv_b, a_ln_g, a_ln_b, a_w_out, b_w_in, b_q_norm, b_k_norm, b_w_out, loss_target, m_norm_g, m_ada_w, m_ada_b, m_a_w_in, m_a_conv_w, m_a_conv_b, m_a_ln_g, m_a_ln_b, m_a_w_out, m_b_w_in, m_b_q_norm, m_b_k_norm, m_b_w_out, v_norm_g, v_ada_w, v_ada_b, v_a_w_in, v_a_conv_w, v_a_conv_b, v_a_ln_g, v_a_ln_b, v_a_w_out, v_b_w_in, v_b_q_norm, v_b_k_norm, v_b_w_out):
    chip = 2 * lax.axis_index("x") + lax.axis_index("y")
    core = lax.axis_index("c")
    chip_idx = chip.astype(jnp.int32).reshape(1)
    dev_idx = jnp.stack([2 * chip + core, chip, core]).astype(jnp.int32)

    land_a_in, own_wa_in = _cast_into_slot(a_w_in[0], chip_idx, "cast_a_w_in", keep_own=True)
    lands_a = [land_a_in, _cast_into_slot(a_w_out[0], chip_idx, "cast_a_w_out")]
    mods, silu_c, conv_w_full = _ada_forward(c, ada_w, ada_b, a_conv_w[0], after=tuple(lands_a))
    send_a, recv_a, lands_a, token_a = _gather_start(lands_a, mods, "gather_start_a")
    land_b_in, own_wb_in = _cast_into_slot(b_w_in[0], chip_idx, "cast_b_w_in", keep_own=True, after=token_a)
    lands_b = [land_b_in, _cast_into_slot(b_w_out[0], chip_idx, "cast_b_w_out", after=token_a)]
    send_b, recv_b, lands_b, token_b = _gather_start(lands_b, token_a, "gather_start_b")
    mods = mods + token_b[0:2, 0:1]

    def weights_a(after):
        send, recv, lands, _ = _gather_forward(send_a, recv_a, lands_a, after, "gather_forward_a")
        w_in, w_out = _gather_wait(send, recv, lands, after, "gather_wait_a")
        return w_in, w_out.reshape(D_MODEL, D_MODEL)

    forwarded_b = []

    def weights_b(after):
        send, recv, lands, _ = forwarded_b
        w_in, w_out = _gather_wait(send, recv, lands, after, "gather_wait_b")
        return w_in, w_out.reshape(D_MODEL, D_MODEL)

    def forward_weights_b(after):
        forwarded_b.extend(_gather_forward(send_b, recv_b, lands_b, after, "gather_forward_b"))

    stage1, stage2 = {}, {}

    def send_grads(tag, dw_in, dw_out):
        grads = [dw_in, dw_out.reshape(N_CHIPS, D_MODEL // N_CHIPS, D_MODEL)]
        send, recv, arrays, token = _reduce_sibling_start(grads, dw_out, f"reduce_d2d_start_{tag}")
        stage1[tag] = (send, recv, arrays)
        return token

    def forward_grads(tag, after):
        send, recv, arrays = stage1[tag]
        grads, got = _reduce_sibling_wait(send, recv, arrays, after, f"reduce_d2d_wait_{tag}")
        partials = [_add_sibling_half(grads[i], got[i], dev_idx, f"reduce_add_{tag}_{i}") for i in range(2)]
        send, recv, arrays, token = _reduce_chips_start(partials, partials[1], f"reduce_ici_start_{tag}")
        stage2[tag] = (send, recv, arrays)
        return token

    stage3 = {}

    def sum_grads(tag, after):
        send, recv, arrays = stage2[tag]
        partials, lands = _reduce_chips_wait(send, recv, arrays, after, f"reduce_ici_wait_{tag}")
        totals = [_sum_partials(lands[i], partials[i], dev_idx, f"reduce_sum_{tag}_{i}") for i in range(2)]
        send, recv, totals, token = _share_halves_start(totals, totals[1], f"reduce_share_start_{tag}")
        stage3[tag] = (send, recv, totals)
        return token

    def finish_grads(tag, after):
        send, recv, totals = stage3[tag]
        return _share_halves_wait(send, recv, totals, after, f"reduce_share_wait_{tag}")

    grad_x, small = _local_step(
        x[0], loss_target[0], mods.reshape(2, 3, D_MODEL), norm_g, conv_w_full, a_conv_b, a_ln_g[0:1],
        a_ln_b[0:1], b_q_norm[0], b_k_norm[0], chip.astype(jnp.int32), own_wa_in, own_wb_in,
        weights_a, weights_b, forward_weights_b,
        functools.partial(send_grads, "b"), functools.partial(forward_grads, "b"), functools.partial(send_grads, "a"))

    ns = 3 * D_MODEL // N_CHIPS
    pad_mod = lambda dm: jnp.pad(dm.reshape(N_CHIPS, ns), ((0, 0), (0, D_MODEL - ns)))
    packed = jnp.concatenate([
        small["dnorm_g"], small["dconv_b"], small["dln_g"], small["dln_b"], small["dq_norm"], small["dk_norm"],
        small["loss_cols"], pad_mod(small["dmod0"]), pad_mod(small["dmod1"]), small["dconv_w"],
        jnp.zeros((SMALL_ROWS - 20 - CONV_WIDTH, D_MODEL), F32)], axis=0)
    send_s, recv_s, small_arrays, token_s = _small_gather_start(packed, packed)

    given = dict(norm_g=(norm_g, m_norm_g, v_norm_g), ada_w=(ada_w, m_ada_w, v_ada_w), ada_b=(ada_b, m_ada_b, v_ada_b),
                 a_w_in=(a_w_in, m_a_w_in, v_a_w_in), a_conv_w=(a_conv_w, m_a_conv_w, v_a_conv_w),
                 a_conv_b=(a_conv_b, m_a_conv_b, v_a_conv_b), a_ln_g=(a_ln_g, m_a_ln_g, v_a_ln_g),
                 a_ln_b=(a_ln_b, m_a_ln_b, v_a_ln_b), a_w_out=(a_w_out, m_a_w_out, v_a_w_out),
                 b_w_in=(b_w_in, m_b_w_in, v_b_w_in), b_q_norm=(b_q_norm, m_b_q_norm, v_b_q_norm),
                 b_k_norm=(b_k_norm, m_b_k_norm, v_b_k_norm), b_w_out=(b_w_out, m_b_w_out, v_b_w_out))
    order = ["norm_g", "ada_w", "ada_b", "a_w_in", "a_conv_w", "a_conv_b", "a_ln_g", "a_ln_b", "a_w_out", "b_w_in",
             "b_q_norm", "b_k_norm", "b_w_out"]
    outs = {}

    def update(k, g2, after=None, copy_grad=False):
        w, m, v = given[k]
        shape2 = g2.shape
        res = _adamw(w.reshape(shape2), g2, m.reshape(shape2), v.reshape(shape2), f"adamw_{k}", after, copy_grad)
        outs[k] = tuple(a.reshape(w.shape) for a in ((res[3] if copy_grad else g2), res[0], res[1], res[2]))

    token = forward_grads("a", token_s)
    token = sum_grads("b", token)
    packed, land = _small_gather_wait(send_s, recv_s, small_arrays, token)
    tot, g_ada_w, loss, qk = _reduce_small(packed, land, silu_c)
    g_b_in, g_b_out = finish_grads("b", tot)
    update("b_w_in", g_b_in, copy_grad=True)
    update("b_w_out", g_b_out, copy_grad=True)
    token = sum_grads("a", outs["b_w_in"][1])
    cw = D_MODEL // N_CHIPS
    g_small = dict(
        norm_g=tot[0:2], a_conv_b=tot[2:3], a_ln_g=tot[3:4], a_ln_b=tot[4:5],
        b_q_norm=qk[0:3], b_k_norm=qk[3:6],
        ada_b=jnp.stack([tot[12:16, :ns].reshape(3 * D_MODEL), tot[16:20, :ns].reshape(3 * D_MODEL)]),
        a_conv_w=lax.dynamic_slice(tot[20:20 + CONV_WIDTH], (0, chip * cw), (CONV_WIDTH, cw)),
    )
    update("ada_w", g_ada_w.reshape(2 * D_MODEL, ns), after=token)
    for k, g2 in g_small.items():
        update(k, g2, after=token)
    g_a_in, g_a_out = finish_grads("a", outs["ada_w"][1])
    update("a_w_in", g_a_in, copy_grad=True)
    update("a_w_out", g_a_out, copy_grad=True)
    return (loss.reshape(()), grad_x[None], *[outs[k][0] for k in order], *[outs[k][1] for k in order],
            *[outs[k][2] for k in order], *[outs[k][3] for k in order])
```
